```python
import jax, jax.numpy as jnp
from jax import lax
import numpy as np

D_MODEL = 1024
BATCH = 8
SEQ = 4096
DEPTH = 1

GRID_W = 64
CTX_LEN = 256
D_MIX = D_MODEL
D_A = D_MIX // 2
D_B = D_MIX - D_A
CHUNK = 128
SGU_GROUPS = 4
SGU_CH = D_A // SGU_GROUPS
NA_HEAD_DIM = 64
NA_HEADS = D_B // NA_HEAD_DIM
WIN_R = 8
WIN_C = 16
D_IN = 3 * D_A + 4 * D_B
BRANCH_WIDTHS = (D_A, D_A, D_A, D_B, D_B, D_B, D_B)
SPLIT_POINTS = tuple(int(s) for s in np.cumsum(BRANCH_WIDTHS)[:-1])
KV_START = 3 * D_A + D_B
KV_END = 3 * D_A + 3 * D_B
EPS = 1e-6
NEG_INF = -1e30

kernel_name = "hybrid_sgu_natten_prefix_block"


def rms_norm(x, g):
    xf = x.astype(jnp.float32)
    y = xf * lax.rsqrt(jnp.mean(xf * xf, axis=-1, keepdims=True) + EPS)
    return (y * g.astype(jnp.float32)).astype(x.dtype)


def ada_params(cond, w_ada, b_ada):
    mod = jax.nn.silu(cond) @ w_ada + b_ada
    shift, scale, gate = jnp.split(mod, 3, axis=-1)
    return shift[..., None, :], scale[..., None, :], gate[..., None, :]


def chunk_sgu(u, v, g, sgu_g, w_s, b_s):
    B, L, _ = u.shape
    u = jax.nn.gelu(u, approximate=False)
    v = jax.nn.gelu(v, approximate=False).reshape(B, L // CHUNK, CHUNK, SGU_GROUPS, SGU_CH)
    v = rms_norm(v, sgu_g.reshape(SGU_GROUPS, SGU_CH))
    mixed = jnp.einsum('gpq,bnqgc->bnpgc', w_s, v) + b_s.T[:, :, None]
    return u * mixed.reshape(B, L, D_A) * jax.nn.silu(g)


def neighborhood_attention(q, k, v, k_ctx, v_ctx, rpb):
    B, L, H, Dh = q.shape
    rows = L // GRID_W
    wr = min(WIN_R, rows)
    scale = Dh ** -0.5
    qg = q.reshape(B, rows, GRID_W, H, Dh)
    kg = k.reshape(B, rows, GRID_W, H, Dh)
    vg = v.reshape(B, rows, GRID_W, H, Dh)
    r = jnp.arange(rows)
    r0 = jnp.clip(r - WIN_R // 2, 0, rows - wr)
    key_rows = r0[:, None] + jnp.arange(wr)[None, :]
    kw = jnp.take(kg, key_rows, axis=1)
    vw = jnp.take(vg, key_rows, axis=1)
    dr = key_rows - r[:, None] + (WIN_R - 1)
    cols = jnp.arange(GRID_W)
    c0 = jnp.clip(cols - WIN_C // 2, 0, GRID_W - WIN_C)
    in_win = (cols[None, :] >= c0[:, None]) & (cols[None, :] < c0[:, None] + WIN_C)
    dc = jnp.clip(cols[None, :] - cols[:, None] + (WIN_C - 1), 0, 2 * WIN_C - 2)
    bias = rpb.astype(jnp.float32)[:, dr[:, None, :, None], dc[None, :, None, :]]
    bias = bias.transpose(1, 0, 2, 3, 4)
    s_lat = jnp.einsum('brqhd,brjkhd->brhqjk', qg, kw).astype(jnp.float32) * scale + bias[None]
    s_lat = jnp.where(in_win[:, None, :], s_lat, NEG_INF)
    s_ctx = jnp.einsum('brqhd,bchd->brhqc', qg, k_ctx).astype(jnp.float32) * scale
    m = jnp.maximum(jnp.max(s_lat, axis=(-2, -1)), jnp.max(s_ctx, axis=-1))
    p_lat = jnp.exp(s_lat - m[..., None, None])
    p_ctx = jnp.exp(s_ctx - m[..., None])
    denom = jnp.sum(p_lat, axis=(-2, -1)) + jnp.sum(p_ctx, axis=-1)
    out = (jnp.einsum('brhqjk,brjkhd->brqhd', p_lat.astype(v.dtype), vw)
           + jnp.einsum('brhqc,bchd->brqhd', p_ctx.astype(v.dtype), v_ctx))
    out = out / denom.transpose(0, 1, 3, 2)[..., None].astype(out.dtype)
    return out.reshape(B, L, H, Dh)


def context_attention(q, k, v):
    s = jnp.einsum('bqhd,bkhd->bhqk', q, k).astype(jnp.float32) * (q.shape[-1] ** -0.5)
    p = jax.nn.softmax(s, axis=-1).astype(v.dtype)
    return jnp.einsum('bhqk,bkhd->bqhd', p, v)


def _fwd_setup_inputs(seed: int = 0) -> dict:
    key = jax.random.key(seed)
    ks = jax.random.split(key, 16)
    f32 = jnp.float32
    nrm = lambda k, shape: jax.random.normal(k, shape, dtype=f32)
    return {
        "x": nrm(ks[0], (BATCH, SEQ, D_MODEL)),
        "c": nrm(ks[1], (BATCH, D_MODEL)),
        "ctx": nrm(ks[2], (BATCH, CTX_LEN, D_MODEL)),
        "c_ctx": nrm(ks[3], (D_MODEL,)),
        "w_ada": nrm(ks[4], (DEPTH, D_MODEL, 3 * D_MODEL)) * (0.5 * D_MODEL ** -0.5),
        "b_ada": nrm(ks[5], (DEPTH, 3 * D_MODEL)) * 0.02,
        "norm_g": 1.0 + 0.02 * nrm(ks[6], (DEPTH, D_MODEL)),
        "w_in": nrm(ks[7], (DEPTH, D_MODEL, D_IN)) * D_MODEL ** -0.5,
        "sgu_norm_g": 1.0 + 0.02 * nrm(ks[8], (DEPTH, D_A)),
        "w_spatial": nrm(ks[9], (DEPTH, SGU_GROUPS, CHUNK, CHUNK)) * CHUNK ** -0.5,
        "b_spatial": nrm(ks[10], (DEPTH, SGU_GROUPS, CHUNK)) * 0.02,
        "q_norm_g": 1.0 + 0.02 * nrm(ks[11], (DEPTH, NA_HEAD_DIM)),
        "k_norm_g": 1.0 + 0.02 * nrm(ks[12], (DEPTH, NA_HEAD_DIM)),
        "rpb": nrm(ks[13], (DEPTH, NA_HEADS, 2 * WIN_R - 1, 2 * WIN_C - 1)) * 0.02,
        "w_out": nrm(ks[14], (DEPTH, D_MIX, D_MODEL)) * D_MIX ** -0.5,
    }


def _fwd_reference(x, c, ctx, c_ctx, w_ada, b_ada, norm_g, w_in, sgu_norm_g, w_spatial,
              b_spatial, q_norm_g, k_norm_g, rpb, w_out):
    B, L, _ = x.shape
    Bc, C, _ = ctx.shape
    for layer in range(DEPTH):
        last = layer == DEPTH - 1
        shift, scale, gate = ada_params(c, w_ada[layer], b_ada[layer])
        cshift, cscale, cgate = ada_params(c_ctx, w_ada[layer], b_ada[layer])
        h = rms_norm(x, norm_g[layer]) * (1.0 + scale) + shift
        hc = rms_norm(ctx, norm_g[layer]) * (1.0 + cscale) + cshift

        z = h @ w_in[layer]
        a_u, a_v, a_g, b_q, b_k, b_v, b_g = jnp.split(z, SPLIT_POINTS, axis=-1)
        q = rms_norm(b_q.reshape(B, L, NA_HEADS, NA_HEAD_DIM), q_norm_g[layer])
        k = rms_norm(b_k.reshape(B, L, NA_HEADS, NA_HEAD_DIM), k_norm_g[layer])
        v = b_v.reshape(B, L, NA_HEADS, NA_HEAD_DIM)

        if last:
            ck, cv = jnp.split(hc @ w_in[layer][:, KV_START:KV_END], 2, axis=-1)
        else:
            zc = hc @ w_in[layer]
            cu, cvv, cga, cq, ck, cv, cgb = jnp.split(zc, SPLIT_POINTS, axis=-1)
        ck = rms_norm(ck.reshape(Bc, C, NA_HEADS, NA_HEAD_DIM), k_norm_g[layer])
        cv = cv.reshape(Bc, C, NA_HEADS, NA_HEAD_DIM)

        out_a = chunk_sgu(a_u, a_v, a_g, sgu_norm_g[layer], w_spatial[layer], b_spatial[layer])
        out_b = neighborhood_attention(q, k, v, ck, cv, rpb[layer]).reshape(B, L, D_B) * jax.nn.silu(b_g)
        mix = jnp.concatenate([out_a, out_b], axis=-1) @ w_out[layer]
        new_x = x + gate * mix

        if not last:
            cq = rms_norm(cq.reshape(Bc, C, NA_HEADS, NA_HEAD_DIM), q_norm_g[layer])
            cout_a = chunk_sgu(cu, cvv, cga, sgu_norm_g[layer], w_spatial[layer], b_spatial[layer])
            cout_b = context_attention(cq, ck, cv).reshape(Bc, C, D_B) * jax.nn.silu(cgb)
            cmix = jnp.concatenate([cout_a, cout_b], axis=-1) @ w_out[layer]
            ctx = ctx + cgate * cmix
        x = new_x
    return x


import jax as _jax
import jax.numpy as _jnp

TWIN_FORMAT = 'train_step'
FWD_PARAMS = ['x', 'c', 'ctx', 'c_ctx', 'w_ada', 'b_ada', 'norm_g', 'w_in', 'sgu_norm_g', 'w_spatial', 'b_spatial', 'q_norm_g', 'k_norm_g', 'rpb', 'w_out']
TWIN_WEIGHTS = ['c_ctx', 'w_ada', 'b_ada', 'norm_g', 'w_in', 'sgu_norm_g', 'w_spatial', 'b_spatial', 'q_norm_g', 'k_norm_g', 'rpb', 'w_out']
TWIN_DIFF_INPUT = 'x'
TWIN_INPUTS = ['x', 'c', 'ctx', 'c_ctx', 'w_ada', 'b_ada', 'norm_g', 'w_in', 'sgu_norm_g', 'w_spatial', 'b_spatial', 'q_norm_g', 'k_norm_g', 'rpb', 'w_out', 'loss_target', 'm_c_ctx', 'm_w_ada', 'm_b_ada', 'm_norm_g', 'm_w_in', 'm_sgu_norm_g', 'm_w_spatial', 'm_b_spatial', 'm_q_norm_g', 'm_k_norm_g', 'm_rpb', 'm_w_out', 'v_c_ctx', 'v_w_ada', 'v_b_ada', 'v_norm_g', 'v_w_in', 'v_sgu_norm_g', 'v_w_spatial', 'v_b_spatial', 'v_q_norm_g', 'v_k_norm_g', 'v_rpb', 'v_w_out']
TWIN_OUTPUTS = ['loss', 'grad_x', 'grad_c_ctx', 'grad_w_ada', 'grad_b_ada', 'grad_norm_g', 'grad_w_in', 'grad_sgu_norm_g', 'grad_w_spatial', 'grad_b_spatial', 'grad_q_norm_g', 'grad_k_norm_g', 'grad_rpb', 'grad_w_out', 'delta_c_ctx', 'delta_w_ada', 'delta_b_ada', 'delta_norm_g', 'delta_w_in', 'delta_sgu_norm_g', 'delta_w_spatial', 'delta_b_spatial', 'delta_q_norm_g', 'delta_k_norm_g', 'delta_rpb', 'delta_w_out', 'new_m_c_ctx', 'new_m_w_ada', 'new_m_b_ada', 'new_m_norm_g', 'new_m_w_in', 'new_m_sgu_norm_g', 'new_m_w_spatial', 'new_m_b_spatial', 'new_m_q_norm_g', 'new_m_k_norm_g', 'new_m_rpb', 'new_m_w_out', 'new_v_c_ctx', 'new_v_w_ada', 'new_v_b_ada', 'new_v_norm_g', 'new_v_w_in', 'new_v_sgu_norm_g', 'new_v_w_spatial', 'new_v_b_spatial', 'new_v_q_norm_g', 'new_v_k_norm_g', 'new_v_rpb', 'new_v_w_out']
TWIN_LEAF_KINDS = {'loss': 'loss', 'grad_x': 'grad_x', 'grad_c_ctx': 'grad_w', 'grad_w_ada': 'grad_w', 'grad_b_ada': 'grad_w', 'grad_norm_g': 'grad_w', 'grad_w_in': 'grad_w', 'grad_sgu_norm_g': 'grad_w', 'grad_w_spatial': 'grad_w', 'grad_b_spatial': 'grad_w', 'grad_q_norm_g': 'grad_w', 'grad_k_norm_g': 'grad_w', 'grad_rpb': 'grad_w', 'grad_w_out': 'grad_w', 'delta_c_ctx': 'delta_w', 'delta_w_ada': 'delta_w', 'delta_b_ada': 'delta_w', 'delta_norm_g': 'delta_w', 'delta_w_in': 'delta_w', 'delta_sgu_norm_g': 'delta_w', 'delta_w_spatial': 'delta_w', 'delta_b_spatial': 'delta_w', 'delta_q_norm_g': 'delta_w', 'delta_k_norm_g': 'delta_w', 'delta_rpb': 'delta_w', 'delta_w_out': 'delta_w', 'new_m_c_ctx': 'new_m', 'new_m_w_ada': 'new_m', 'new_m_b_ada': 'new_m', 'new_m_norm_g': 'new_m', 'new_m_w_in': 'new_m', 'new_m_sgu_norm_g': 'new_m', 'new_m_w_spatial': 'new_m', 'new_m_b_spatial': 'new_m', 'new_m_q_norm_g': 'new_m', 'new_m_k_norm_g': 'new_m', 'new_m_rpb': 'new_m', 'new_m_w_out': 'new_m', 'new_v_c_ctx': 'new_v', 'new_v_w_ada': 'new_v', 'new_v_b_ada': 'new_v', 'new_v_norm_g': 'new_v', 'new_v_w_in': 'new_v', 'new_v_sgu_norm_g': 'new_v', 'new_v_w_spatial': 'new_v', 'new_v_b_spatial': 'new_v', 'new_v_q_norm_g': 'new_v', 'new_v_k_norm_g': 'new_v', 'new_v_rpb': 'new_v', 'new_v_w_out': 'new_v'}


def _forward(args):
    return _fwd_reference(*[args[k] for k in FWD_PARAMS])


def _output_shape():
    out = _jax.eval_shape(lambda: _forward(_fwd_setup_inputs(0)))
    return out.shape, out.dtype

N_MICROBATCH = 1
ADAM_LR = 0.001
ADAM_B1 = 0.9
ADAM_B2 = 0.999
ADAM_EPS = 1e-08
ADAM_WD = 0.01
ADAM_STEP = 10
PER_EXAMPLE_BATCH_AXIS = {'x': 0, 'c': 0, 'ctx': 0, 'loss_target': 0}
SHARED_INPUTS = []
_WEIGHT_DTYPES = {'c_ctx': _jnp.float32, 'w_ada': _jnp.float32, 'b_ada': _jnp.float32, 'norm_g': _jnp.float32, 'w_in': _jnp.float32, 'sgu_norm_g': _jnp.float32, 'w_spatial': _jnp.float32, 'b_spatial': _jnp.float32, 'q_norm_g': _jnp.float32, 'k_norm_g': _jnp.float32, 'rpb': _jnp.float32, 'w_out': _jnp.float32}
MOMENT_SCALE = {'c_ctx': 4.440645e-02, 'w_ada': 2.450432e-01, 'b_ada': 5.767617e-01, 'norm_g': 7.171368e-01, 'w_in': 5.155959e-02, 'sgu_norm_g': 6.388473e-01, 'w_spatial': 1.217882e-01, 'b_spatial': 2.463179e-01, 'q_norm_g': 6.868375e-02, 'k_norm_g': 6.857675e-02, 'rpb': 1.453672e-03, 'w_out': 3.032230e-02}


def _to_microbatches(a, axis):
    t = _jnp.moveaxis(a, axis, 0)
    t = t.reshape((N_MICROBATCH, t.shape[0] // N_MICROBATCH) + t.shape[1:])
    return _jnp.moveaxis(t, 1, axis + 1)


def setup_inputs(seed: int = 0) -> dict:
    inp = _fwd_setup_inputs(seed)
    key = _jax.random.fold_in(_jax.random.key(seed), 7919)
    shape, _ = _output_shape()
    out = dict(inp)
    out["loss_target"] = _jax.random.normal(_jax.random.fold_in(key, 0), shape, _jnp.float32)
    for i, name in enumerate(TWIN_WEIGHTS):
        w = inp[name].astype(_jnp.float32)
        if MOMENT_SCALE is None:
            s = _jnp.sqrt(_jnp.mean(_jnp.square(w)) + 1e-30)
        else:
            s = MOMENT_SCALE[name]
        km, kv = _jax.random.split(_jax.random.fold_in(key, i + 1))
        out[name] = w
        out["m_" + name] = s * _jax.random.normal(km, w.shape, _jnp.float32)
        out["v_" + name] = (s * s) * _jax.random.uniform(kv, w.shape, _jnp.float32, 0.5, 1.5)
    if N_MICROBATCH > 1:
        for name, axis in PER_EXAMPLE_BATCH_AXIS.items():
            out[name] = _to_microbatches(out[name], axis)
    return {'x': out['x'], 'c': out['c'], 'ctx': out['ctx'], 'c_ctx': out['c_ctx'], 'w_ada': out['w_ada'], 'b_ada': out['b_ada'], 'norm_g': out['norm_g'], 'w_in': out['w_in'], 'sgu_norm_g': out['sgu_norm_g'], 'w_spatial': out['w_spatial'], 'b_spatial': out['b_spatial'], 'q_norm_g': out['q_norm_g'], 'k_norm_g': out['k_norm_g'], 'rpb': out['rpb'], 'w_out': out['w_out'], 'loss_target': out['loss_target'], 'm_c_ctx': out['m_c_ctx'], 'm_w_ada': out['m_w_ada'], 'm_b_ada': out['m_b_ada'], 'm_norm_g': out['m_norm_g'], 'm_w_in': out['m_w_in'], 'm_sgu_norm_g': out['m_sgu_norm_g'], 'm_w_spatial': out['m_w_spatial'], 'm_b_spatial': out['m_b_spatial'], 'm_q_norm_g': out['m_q_norm_g'], 'm_k_norm_g': out['m_k_norm_g'], 'm_rpb': out['m_rpb'], 'm_w_out': out['m_w_out'], 'v_c_ctx': out['v_c_ctx'], 'v_w_ada': out['v_w_ada'], 'v_b_ada': out['v_b_ada'], 'v_norm_g': out['v_norm_g'], 'v_w_in': out['v_w_in'], 'v_sgu_norm_g': out['v_sgu_norm_g'], 'v_w_spatial': out['v_w_spatial'], 'v_b_spatial': out['v_b_spatial'], 'v_q_norm_g': out['v_q_norm_g'], 'v_k_norm_g': out['v_k_norm_g'], 'v_rpb': out['v_rpb'], 'v_w_out': out['v_w_out']}


def _loss(weights, diff, rest, loss_target):
    with _jax.named_scope("forward"):
        args = {**rest, TWIN_DIFF_INPUT: diff, **{k: w.astype(_WEIGHT_DTYPES[k]) for k, w in weights.items()}}
        y = _forward(args)
    with _jax.named_scope("loss_head"):
        err = _jnp.square(y.astype(_jnp.float32) - loss_target)
        return 0.5 * _jnp.sum(_jnp.mean(err, axis=-1)) if err.ndim else 0.5 * err


def _adamw(w, g, m, v):
    m = ADAM_B1 * m + (1.0 - ADAM_B1) * g
    v = ADAM_B2 * v + (1.0 - ADAM_B2) * _jnp.square(g)
    m_hat = m / (1.0 - ADAM_B1 ** ADAM_STEP)
    v_hat = v / (1.0 - ADAM_B2 ** ADAM_STEP)
    delta = -ADAM_LR * (m_hat / (_jnp.sqrt(v_hat) + ADAM_EPS) + ADAM_WD * w)
    return delta, m, v


def reference(x, c, ctx, c_ctx, w_ada, b_ada, norm_g, w_in, sgu_norm_g, w_spatial, b_spatial, q_norm_g, k_norm_g, rpb, w_out, loss_target, m_c_ctx, m_w_ada, m_b_ada, m_norm_g, m_w_in, m_sgu_norm_g, m_w_spatial, m_b_spatial, m_q_norm_g, m_k_norm_g, m_rpb, m_w_out, v_c_ctx, v_w_ada, v_b_ada, v_norm_g, v_w_in, v_sgu_norm_g, v_w_spatial, v_b_spatial, v_q_norm_g, v_k_norm_g, v_rpb, v_w_out):
    given = dict(x=x, c=c, ctx=ctx, c_ctx=c_ctx, w_ada=w_ada, b_ada=b_ada, norm_g=norm_g, w_in=w_in, sgu_norm_g=sgu_norm_g, w_spatial=w_spatial, b_spatial=b_spatial, q_norm_g=q_norm_g, k_norm_g=k_norm_g, rpb=rpb, w_out=w_out, loss_target=loss_target, m_c_ctx=m_c_ctx, m_w_ada=m_w_ada, m_b_ada=m_b_ada, m_norm_g=m_norm_g, m_w_in=m_w_in, m_sgu_norm_g=m_sgu_norm_g, m_w_spatial=m_w_spatial, m_b_spatial=m_b_spatial, m_q_norm_g=m_q_norm_g, m_k_norm_g=m_k_norm_g, m_rpb=m_rpb, m_w_out=m_w_out, v_c_ctx=v_c_ctx, v_w_ada=v_w_ada, v_b_ada=v_b_ada, v_norm_g=v_norm_g, v_w_in=v_w_in, v_sgu_norm_g=v_sgu_norm_g, v_w_spatial=v_w_spatial, v_b_spatial=v_b_spatial, v_q_norm_g=v_q_norm_g, v_k_norm_g=v_k_norm_g, v_rpb=v_rpb, v_w_out=v_w_out)
    weights = {n: given[n] for n in TWIN_WEIGHTS}
    shared = {n: given[n] for n in SHARED_INPUTS}
    per_example = {n: given[n] for n in ['x', 'c', 'ctx']}
    grad_fn = _jax.value_and_grad(_loss, argnums=(0, 1))

    def one_microbatch(ex, loss_target):
        ex = dict(ex)
        diff = ex.pop(TWIN_DIFF_INPUT)
        return grad_fn(weights, diff, {**shared, **ex}, loss_target)

    if N_MICROBATCH == 1:
        loss, (grad_w, grad_x) = one_microbatch(per_example, given["loss_target"])
    else:
        def body(carry, xs):
            loss_sum, grad_sum = carry
            l_k, (gw_k, gx_k) = one_microbatch(xs[0], xs[1])
            with _jax.named_scope("update"):
                return (loss_sum + l_k, _jax.tree.map(_jnp.add, grad_sum, gw_k)), gx_k

        init = (_jnp.zeros((), _jnp.float32), _jax.tree.map(_jnp.zeros_like, weights))
        (loss, grad_w), grad_x = _jax.lax.scan(body, init, (per_example, given["loss_target"]))
    with _jax.named_scope("update"):
        delta_w, new_m, new_v = {}, {}, {}
        for n in TWIN_WEIGHTS:
            delta_w[n], new_m[n], new_v[n] = _adamw(weights[n], grad_w[n], given["m_" + n], given["v_" + n])
    return (loss, grad_x, *[grad_w[n] for n in TWIN_WEIGHTS], *[delta_w[n] for n in TWIN_WEIGHTS],
            *[new_m[n] for n in TWIN_WEIGHTS], *[new_v[n] for n in TWIN_WEIGHTS])
```

```python
import functools

import jax
import jax.numpy as jnp
from jax import lax
from jax.experimental import pallas as pl
from jax.experimental.pallas import tpu as pltpu

F32, BF16 = jnp.float32, jnp.bfloat16
SEQ, DM, CTX, DIN = 4096, 1024, 256, 3584
NCHIP, NDEV = 4, 8
SHARD_IN = DIN // NCHIP
SHARD_ADA = 3 * DM // NCHIP
SHARD_OUT = DM // NCHIP
GRID_W = 64
QROWS = 4
KROWS = 12
QBLK, KBLK = QROWS * GRID_W, KROWS * GRID_W
NQBLK = SEQ // QBLK
HEADS, HDIM, NPAIR = 8, 64, 4
EPS = 1e-6
NEG_INF = -1e30
ZQ, ZK, ZV, ZG = 12, 16, 20, 24
LR, B1, B2, ADAM_EPS, WD, STEP = 0.001, 0.9, 0.999, 1e-08, 0.01, 10
VMEM_BIG = 56 * 1024 * 1024
MESH_ID = pl.DeviceIdType.MESH


def _dot(a, b, lhs_c, rhs_c):
    return lax.dot_general(a.astype(BF16), b.astype(BF16), (((lhs_c,), (rhs_c,)), ((), ())),
                           preferred_element_type=F32)


@jax.custom_vjp
def mm(a, b):
    return _dot(a, b, 1, 0)


@jax.custom_vjp
def mm_nt(a, b):
    return _dot(a, b, 1, 1)


@jax.custom_vjp
def mm_tn(a, b):
    return _dot(a, b, 0, 0)


mm.defvjp(lambda a, b: (mm(a, b), (a, b)), lambda r, ct: (mm_nt(ct, r[1]), mm_tn(r[0], ct)))
mm_nt.defvjp(lambda a, b: (mm_nt(a, b), (a, b)), lambda r, ct: (mm(ct, r[1]), mm_tn(ct, r[0])))
mm_tn.defvjp(lambda a, b: (mm_tn(a, b), (a, b)), lambda r, ct: (mm_nt(r[1], ct), mm(r[0], ct)))


def _rms(x, g):
    return x * lax.rsqrt(jnp.mean(x * x, axis=-1, keepdims=True) + EPS) * g


def _modulated(x, g, scale, shift):
    return _rms(x, g) * (1.0 + scale) + shift


def _pair_rms(x, g2):
    lo = lax.broadcasted_iota(jnp.int32, (1, 2 * HDIM), 1) < HDIM
    sq = x * x
    s_lo = jnp.sum(jnp.where(lo, sq, 0.0), axis=-1, keepdims=True)
    s_hi = jnp.sum(jnp.where(lo, 0.0, sq), axis=-1, keepdims=True)
    rs = jnp.where(lo, lax.rsqrt(s_lo / HDIM + EPS), lax.rsqrt(s_hi / HDIM + EPS))
    return x * rs * g2


def _cparams(sem, vmem=None):
    return pltpu.CompilerParams(dimension_semantics=sem, vmem_limit_bytes=vmem)


def _row(n):
    return pl.BlockSpec((1, n), lambda *_: (0, 0))


def inproj_fwd(x, shift, scale, norm_g, w_full):
    tl = 512

    def kern(x_ref, sh_ref, sc_ref, g_ref, w_ref, z_ref, h_ref):
        @pl.when(pl.program_id(1) == 0)
        def _():
            h_ref[...] = _modulated(x_ref[...], g_ref[...], sc_ref[...], sh_ref[...]).astype(BF16)

        z_ref[...] = jnp.dot(h_ref[...], w_ref[0], preferred_element_type=F32)

    return pl.pallas_call(
        kern, name="inproj_fwd", grid=(SEQ // tl, NCHIP),
        in_specs=[pl.BlockSpec((tl, DM), lambda t, j: (t, 0)), _row(DM), _row(DM), _row(DM),
                  pl.BlockSpec((1, DM, SHARD_IN), lambda t, j: (j, 0, 0))],
        out_specs=[pl.BlockSpec((tl, SHARD_IN), lambda t, j: (t, j)),
                   pl.BlockSpec((tl, DM), lambda t, j: (t, 0))],
        out_shape=[jax.ShapeDtypeStruct((SEQ, DIN), F32), jax.ShapeDtypeStruct((SEQ, DM), BF16)],
        compiler_params=_cparams(("arbitrary", "arbitrary"), 40 * 1024 * 1024),
    )(x, shift, scale, norm_g, w_full)


def ctx_fwd(ctx, cshift, cscale, norm_g, w_full):
    def kern(c_ref, sh_ref, sc_ref, g_ref, w2_ref, w3_ref, zc_ref, hc_ref):
        hc = _modulated(c_ref[...], g_ref[...], sc_ref[...], sh_ref[...]).astype(BF16)
        hc_ref[...] = hc
        zc_ref[:, :SHARD_IN] = jnp.dot(hc, w2_ref[0], preferred_element_type=F32)
        zc_ref[:, SHARD_IN:] = jnp.dot(hc, w3_ref[0], preferred_element_type=F32)

    return pl.pallas_call(
        kern, name="ctx_fwd", grid=(1,),
        in_specs=[pl.BlockSpec((CTX, DM), lambda i: (0, 0)), _row(DM), _row(DM), _row(DM),
                  pl.BlockSpec((1, DM, SHARD_IN), lambda i: (2, 0, 0)),
                  pl.BlockSpec((1, DM, SHARD_IN), lambda i: (3, 0, 0))],
        out_specs=[pl.BlockSpec((CTX, 2 * SHARD_IN), lambda i: (0, 0)),
                   pl.BlockSpec((CTX, DM), lambda i: (0, 0))],
        out_shape=[jax.ShapeDtypeStruct((CTX, 2 * SHARD_IN), F32), jax.ShapeDtypeStruct((CTX, DM), BF16)],
        compiler_params=_cparams(("arbitrary",)),
    )(ctx, cshift, cscale, norm_g, w_full, w_full)


SGU_CHUNK, SGU_PER_STEP = 128, 4


def _gelu(x):
    return 0.5 * x * (1.0 + lax.erf(x * 0.7071067811865476))


def _sgu_chunk(au, av, ag, sg, ws, bsb):
    u, v = _gelu(au), _gelu(av)
    outs = []
    for g in range(4):
        sl = slice(128 * g, 128 * (g + 1))
        mixed = mm(ws[g], _rms(v[:, sl], sg[:, sl])) + bsb[g]
        outs.append(u[:, sl] * mixed * jax.nn.silu(ag[:, sl]))
    return jnp.concatenate(outs, axis=-1)


def _sgu_specs():
    rows = SGU_CHUNK * SGU_PER_STEP
    zspec = lambda c: pl.BlockSpec((rows, 512), lambda n: (n, c))
    wspec = pl.BlockSpec((4, 128, 128), lambda n: (0, 0, 0))
    return rows, [zspec(0), zspec(1), zspec(2), _row(512), wspec, wspec]


def sgu_fwd(z, sg, ws, bsb):
    rows, in_specs = _sgu_specs()

    def kern(au_ref, av_ref, ag_ref, sg_ref, ws_ref, bs_ref, o_ref):
        for c in range(SGU_PER_STEP):
            sl = slice(c * SGU_CHUNK, (c + 1) * SGU_CHUNK)
            o_ref[sl, :] = _sgu_chunk(au_ref[sl, :], av_ref[sl, :], ag_ref[sl, :], sg_ref[...], ws_ref[...],
                                      bs_ref[...])

    return pl.pallas_call(
        kern, name="sgu_fwd", grid=(SEQ // rows,), in_specs=in_specs,
        out_specs=pl.BlockSpec((rows, 512), lambda n: (n, 0)),
        out_shape=jax.ShapeDtypeStruct((SEQ, 512), F32),
        compiler_params=_cparams(("arbitrary",)),
    )(z, z, z, sg, ws, bsb)


def sgu_bwd(z, sg, ws, bsb, dcat):
    rows, in_specs = _sgu_specs()

    def kern(au_ref, av_ref, ag_ref, sg_ref, ws_ref, bs_ref, do_ref, dz_ref, dsg_ref, dws_ref, dbs_ref):
        @pl.when(pl.program_id(0) == 0)
        def _():
            dsg_ref[...] = jnp.zeros_like(dsg_ref)
            dws_ref[...] = jnp.zeros_like(dws_ref)
            dbs_ref[...] = jnp.zeros_like(dbs_ref)

        for c in range(SGU_PER_STEP):
            sl = slice(c * SGU_CHUNK, (c + 1) * SGU_CHUNK)
            _, vjp = jax.vjp(_sgu_chunk, au_ref[sl, :], av_ref[sl, :], ag_ref[sl, :], sg_ref[...], ws_ref[...],
                             bs_ref[...])
            dau, dav, dag, dsg, dws, dbs = vjp(do_ref[sl, :])
            dz_ref[sl, 0:512] = dau
            dz_ref[sl, 512:1024] = dav
            dz_ref[sl, 1024:1536] = dag
            dsg_ref[...] += dsg
            dws_ref[...] += dws
            dbs_ref[...] += dbs

        @pl.when(pl.program_id(0) == pl.num_programs(0) - 1)
        def _():
            dbs_ref[...] = jnp.broadcast_to(jnp.sum(dbs_ref[...], axis=-1, keepdims=True), dbs_ref.shape)

    wspec = pl.BlockSpec((4, 128, 128), lambda n: (0, 0, 0))
    return pl.pallas_call(
        kern, name="sgu_bwd", grid=(SEQ // rows,),
        in_specs=in_specs + [pl.BlockSpec((rows, 512), lambda n: (n, 0))],
        out_specs=[pl.BlockSpec((rows, 1536), lambda n: (n, 0)), _row(512), wspec, wspec],
        out_shape=[jax.ShapeDtypeStruct((SEQ, 1536), F32), jax.ShapeDtypeStruct((1, 512), F32),
                   jax.ShapeDtypeStruct((4, 128, 128), F32), jax.ShapeDtypeStruct((4, 128, 128), F32)],
        compiler_params=_cparams(("arbitrary",)),
    )(z, z, z, sg, ws, bsb, dcat)


_DR_OFF = (7, 3, -1)


def _row_valid(v, rr, j):
    return (j < 8, rr <= j < rr + 8, 4 <= j < 12)[v]


def _col_window():
    q = lax.broadcasted_iota(jnp.int32, (GRID_W, 128), 0)
    kc = lax.broadcasted_iota(jnp.int32, (GRID_W, 128), 1) % GRID_W
    c0 = jnp.clip(q - 8, 0, GRID_W - 16)
    return (kc >= c0) & (kc < c0 + 16)


def rpb_tables(rpb2):
    def kern(r_ref, b_ref):
        base = r_ref[0]
        lo = lax.broadcasted_iota(jnp.int32, (1, 128), 1) < GRID_W
        win = _col_window()
        neg = jnp.full((GRID_W, 128), NEG_INF, F32)
        for v in range(3):
            for rr in range(QROWS):
                for jp in range(KROWS // 2):
                    j0, j1 = 2 * jp, 2 * jp + 1
                    ok0, ok1 = _row_valid(v, rr, j0), _row_valid(v, rr, j1)
                    if not (ok0 or ok1):
                        tile = neg
                    else:
                        d0 = j0 - rr + _DR_OFF[v]
                        r0 = base[d0:d0 + 1, :] if ok0 else jnp.zeros((1, 128), F32)
                        r1 = base[d0 + 1:d0 + 2, :] if ok1 else jnp.zeros((1, 128), F32)
                        y = jnp.broadcast_to(jnp.where(lo, r0, r1), (GRID_W, 128))
                        y = pltpu.roll(pltpu.roll(y, 128 - 15, 1), 0, 1, stride=1, stride_axis=0)
                        ok = win & jnp.where(lo, ok0, ok1)
                        tile = jnp.where(ok, y, NEG_INF)
                    b_ref[v, 0, rr * GRID_W:(rr + 1) * GRID_W, jp * 128:(jp + 1) * 128] = tile

    return pl.pallas_call(
        kern, name="rpb_tables", grid=(HEADS,),
        in_specs=[pl.BlockSpec((1, 15, 128), lambda h: (h, 0, 0))],
        out_specs=pl.BlockSpec((3, 1, QBLK, KBLK), lambda h: (0, h, 0, 0)),
        out_shape=jax.ShapeDtypeStruct((3, HEADS, QBLK, KBLK), F32),
        compiler_params=_cparams(("arbitrary",)),
    )(rpb2)


def rpb_bwd(dbias):
    def kern(g_ref, o_ref):
        lo = lax.broadcasted_iota(jnp.int32, (1, 128), 1) < GRID_W
        ri = lax.broadcasted_iota(jnp.int32, (GRID_W, GRID_W), 0)
        ci = lax.broadcasted_iota(jnp.int32, (GRID_W, GRID_W), 1)
        flip = (ri + ci == GRID_W - 1).astype(F32)
        acc = [jnp.zeros((1, 128), F32) for _ in range(15)]
        for v in range(3):
            for rr in range(QROWS):
                for jp in range(KROWS // 2):
                    j0, j1 = 2 * jp, 2 * jp + 1
                    ok0, ok1 = _row_valid(v, rr, j0), _row_valid(v, rr, j1)
                    if not (ok0 or ok1):
                        continue
                    g = g_ref[v, 0, rr * GRID_W:(rr + 1) * GRID_W, jp * 128:(jp + 1) * 128]
                    g = lax.dot_general(flip, g, (((1,), (0,)), ((), ())), precision=lax.Precision.HIGHEST,
                                        preferred_element_type=F32)
                    g = pltpu.roll(pltpu.roll(g, 128 - 48, 1), 0, 1, stride=1, stride_axis=0)
                    s = jnp.sum(g, axis=0, keepdims=True)
                    d0 = j0 - rr + _DR_OFF[v]
                    if ok0:
                        acc[d0] = acc[d0] + jnp.where(lo, s, 0.0)
                    if ok1:
                        acc[d0 + 1] = acc[d0 + 1] + jnp.where(lo, 0.0, s)
        for d in range(15):
            o_ref[0, d:d + 1, :] = acc[d] + pltpu.roll(acc[d], GRID_W, 1)

    return pl.pallas_call(
        kern, name="rpb_bwd", grid=(HEADS,),
        in_specs=[pl.BlockSpec((3, 1, QBLK, KBLK), lambda h: (0, h, 0, 0))],
        out_specs=pl.BlockSpec((1, 15, 128), lambda h: (h, 0, 0)),
        out_shape=jax.ShapeDtypeStruct((HEADS, 15, 128), F32),
        compiler_params=_cparams(("arbitrary",)),
    )(dbias)


def _attn_step(q_raw, kn, v, ckn, cv, bias2, qg):
    qn = _pair_rms(q_raw, qg) * (HDIM ** -0.5)
    lo = lax.broadcasted_iota(jnp.int32, (1, 2 * HDIM), 1) < HDIM
    out = None
    for a in range(2):
        mine = lo if a == 0 else jnp.logical_not(lo)
        qa = jnp.where(mine, qn, 0.0)
        s_lat = mm_nt(qa, kn) + bias2[a]
        s_ctx = mm_nt(qa, ckn)
        m = lax.stop_gradient(jnp.maximum(jnp.max(s_lat, axis=-1, keepdims=True),
                                          jnp.max(s_ctx, axis=-1, keepdims=True)))
        p_lat = jnp.exp(s_lat - m)
        p_ctx = jnp.exp(s_ctx - m)
        den = jnp.sum(p_lat, axis=-1, keepdims=True) + jnp.sum(p_ctx, axis=-1, keepdims=True)
        o = jnp.where(mine, (mm(p_lat, v) + mm(p_ctx, cv)) / den, 0.0)
        out = o if out is None else out + o
    return out


def _attn_gated(q_raw, kn, v, ckn, cv, bias2, qg, bg):
    return _attn_step(q_raw, kn, v, ckn, cv, bias2, qg) * jax.nn.silu(bg)


def _kstart(i):
    return pl.multiple_of(jnp.clip((i - 1) * QBLK, 0, SEQ - KBLK), QBLK)


def _bias_variant(i):
    return jnp.where(i == 0, 0, jnp.where(i == NQBLK - 1, 2, 1))


def _attn_in_specs():
    return [
        pl.BlockSpec((QBLK, 128), lambda p, i: (i, ZQ + p)),
        pl.BlockSpec((SEQ, 128), lambda p, i: (0, ZK + p)),
        pl.BlockSpec((SEQ, 128), lambda p, i: (0, ZV + p)),
        pl.BlockSpec((QBLK, 128), lambda p, i: (i, ZG + p)),
        pl.BlockSpec((CTX, 128), lambda p, i: (0, 2 + p)),
        pl.BlockSpec((CTX, 128), lambda p, i: (0, 6 + p)),
        pl.BlockSpec((1, 2, QBLK, KBLK), lambda p, i: (_bias_variant(i), p, 0, 0)),
        _row(128), _row(128),
    ]


NORM_ROWS = 512


def _norm_keys(k_ref, ck_ref, kg_ref, kn_scr, ckn_scr):
    def body(c, carry):
        sl = pl.ds(pl.multiple_of(c * NORM_ROWS, NORM_ROWS), NORM_ROWS)
        kn_scr[sl, :] = _pair_rms(k_ref[sl, :], kg_ref[...])
        return carry

    lax.fori_loop(0, SEQ // NORM_ROWS, body, 0)
    ckn_scr[...] = _pair_rms(ck_ref[...], kg_ref[...])


def attn_fwd(z, zc, bias, qg2, kg2):
    def kern(q_ref, k_ref, v_ref, bg_ref, ck_ref, cv_ref, b_ref, qg_ref, kg_ref, o_ref, kn_scr, ckn_scr):
        i = pl.program_id(1)

        @pl.when(i == 0)
        def _():
            _norm_keys(k_ref, ck_ref, kg_ref, kn_scr, ckn_scr)

        ks = pl.ds(_kstart(i), KBLK)
        o_ref[...] = _attn_gated(q_ref[...], kn_scr[ks, :], v_ref[ks, :], ckn_scr[...], cv_ref[...], b_ref[0],
                                 qg_ref[...], bg_ref[...])

    return pl.pallas_call(
        kern, name="attn_fwd", grid=(NPAIR, NQBLK), in_specs=_attn_in_specs(),
        out_specs=pl.BlockSpec((QBLK, 128), lambda p, i: (i, p)),
        out_shape=jax.ShapeDtypeStruct((SEQ, 512), F32),
        scratch_shapes=[pltpu.VMEM((SEQ, 128), F32), pltpu.VMEM((CTX, 128), F32)],
        compiler_params=_cparams(("arbitrary", "arbitrary"), 40 * 1024 * 1024),
    )(z, z, z, z, zc, zc, bias, qg2, kg2)


def attn_bwd(z, zc, bias, qg2, kg2, dcat):
    def kern(q_ref, k_ref, v_ref, bg_ref, ck_ref, cv_ref, b_ref, qg_ref, kg_ref, do_ref,
             dq_ref, dk_ref, dv_ref, dbg_ref, dck_ref, dcv_ref, db_ref, dqg_ref, dkg_ref,
             kn_scr, ckn_scr, dkn_scr, dckn_scr):
        p, i = pl.program_id(0), pl.program_id(1)
        last = i == NQBLK - 1

        @pl.when(i == 0)
        def _():
            _norm_keys(k_ref, ck_ref, kg_ref, kn_scr, ckn_scr)
            dkn_scr[...] = jnp.zeros_like(dkn_scr)
            dv_ref[...] = jnp.zeros_like(dv_ref)
            dckn_scr[...] = jnp.zeros_like(dckn_scr)
            dcv_ref[...] = jnp.zeros_like(dcv_ref)

        @pl.when((i == 0) & (p == 0))
        def _():
            dqg_ref[...] = jnp.zeros_like(dqg_ref)
            dkg_ref[...] = jnp.zeros_like(dkg_ref)

        ks = pl.ds(_kstart(i), KBLK)
        _, vjp = jax.vjp(_attn_gated, q_ref[...], kn_scr[ks, :], v_ref[ks, :], ckn_scr[...], cv_ref[...], b_ref[0],
                         qg_ref[...], bg_ref[...])
        dq, dkn, dv, dckn, dcv, db, dqg, dbg = vjp(do_ref[...])
        dq_ref[...] = dq
        dbg_ref[...] = dbg
        dkn_scr[ks, :] += dkn
        dv_ref[ks, :] += dv
        dckn_scr[...] += dckn
        dcv_ref[...] += dcv
        dqg_ref[...] += dqg
        fresh = (i == 0) | (i == 1) | last

        @pl.when(fresh)
        def _():
            db_ref[0] = db

        @pl.when(jnp.logical_not(fresh))
        def _():
            db_ref[0] += db

        @pl.when(last)
        def _():
            def body(c, dkg):
                sl = pl.ds(pl.multiple_of(c * NORM_ROWS, NORM_ROWS), NORM_ROWS)
                _, nvjp = jax.vjp(_pair_rms, k_ref[sl, :], kg_ref[...])
                dk, dg = nvjp(dkn_scr[sl, :])
                dk_ref[sl, :] = dk
                return dkg + dg

            dkg = lax.fori_loop(0, SEQ // NORM_ROWS, body, jnp.zeros((1, 128), F32))
            _, nvjp = jax.vjp(_pair_rms, ck_ref[...], kg_ref[...])
            dck, dg = nvjp(dckn_scr[...])
            dck_ref[...] = dck
            dkg_ref[...] += dkg + dg

        @pl.when(last & (p == NPAIR - 1))
        def _():
            dqg_ref[...] = dqg_ref[...] + pltpu.roll(dqg_ref[...], HDIM, 1)
            dkg_ref[...] = dkg_ref[...] + pltpu.roll(dkg_ref[...], HDIM, 1)

    blk = lambda rows: pl.BlockSpec((rows, 128), lambda p, i: (0, p))
    qblk = pl.BlockSpec((QBLK, 128), lambda p, i: (i, p))
    return pl.pallas_call(
        kern, name="attn_bwd", grid=(NPAIR, NQBLK),
        in_specs=_attn_in_specs() + [pl.BlockSpec((QBLK, 128), lambda p, i: (i, 4 + p))],
        out_specs=[qblk, blk(SEQ), blk(SEQ), qblk, blk(CTX), blk(CTX),
                   pl.BlockSpec((1, 2, QBLK, KBLK), lambda p, i: (_bias_variant(i), p, 0, 0)),
                   _row(128), _row(128)],
        out_shape=[jax.ShapeDtypeStruct((SEQ, 512), F32)] * 4 + [jax.ShapeDtypeStruct((CTX, 512), F32)] * 2
        + [jax.ShapeDtypeStruct((3, HEADS, QBLK, KBLK), F32), jax.ShapeDtypeStruct((1, 128), F32),
           jax.ShapeDtypeStruct((1, 128), F32)],
        scratch_shapes=[pltpu.VMEM((SEQ, 128), F32), pltpu.VMEM((CTX, 128), F32),
                        pltpu.VMEM((SEQ, 128), F32), pltpu.VMEM((CTX, 128), F32)],
        compiler_params=_cparams(("arbitrary", "arbitrary"), VMEM_BIG),
    )(z, z, z, z, zc, zc, bias, qg2, kg2, dcat)


def outproj(out_a, out_b, x, target, gate, wo):
    tl = 512

    def kern(a_ref, b_ref, x_ref, t_ref, g_ref, w_ref, loss_ref, dy_ref, dcat_ref, dg_ref, dw_ref):
        @pl.when(pl.program_id(0) == 0)
        def _():
            loss_ref[...] = jnp.zeros_like(loss_ref)
            dg_ref[...] = jnp.zeros_like(dg_ref)
            dw_ref[...] = jnp.zeros_like(dw_ref)

        a, b = a_ref[...].astype(BF16), b_ref[...].astype(BF16)
        mix = (jnp.dot(a, w_ref[0:512, :], preferred_element_type=F32)
               + jnp.dot(b, w_ref[512:1024, :], preferred_element_type=F32))
        err = x_ref[...] + g_ref[...] * mix - t_ref[...]
        loss_ref[...] += 0.5 * jnp.sum(jnp.mean(err * err, axis=-1))
        dy = err * (1.0 / DM)
        dy_ref[...] = dy
        dg_ref[...] += jnp.sum(dy * mix, axis=0, keepdims=True)
        dmix = (g_ref[...] * dy).astype(BF16)
        dcat_ref[...] = lax.dot_general(dmix, w_ref[...], (((1,), (1,)), ((), ())), preferred_element_type=F32)
        dw_ref[0:512, :] += lax.dot_general(a, dmix, (((0,), (0,)), ((), ())), preferred_element_type=F32)
        dw_ref[512:1024, :] += lax.dot_general(b, dmix, (((0,), (0,)), ((), ())), preferred_element_type=F32)

    tile = lambda w: pl.BlockSpec((tl, w), lambda t: (t, 0))
    whole = pl.BlockSpec((DM, DM), lambda t: (0, 0))
    return pl.pallas_call(
        kern, name="outproj", grid=(SEQ // tl,),
        in_specs=[tile(512), tile(512), tile(DM), tile(DM), _row(DM), whole],
        out_specs=[pl.BlockSpec((8, 128), lambda t: (0, 0)), tile(DM), tile(DM), _row(DM), whole],
        out_shape=[jax.ShapeDtypeStruct((8, 128), F32), jax.ShapeDtypeStruct((SEQ, DM), F32),
                   jax.ShapeDtypeStruct((SEQ, DM), F32), jax.ShapeDtypeStruct((1, DM), F32),
                   jax.ShapeDtypeStruct((DM, DM), F32)],
        compiler_params=_cparams(("arbitrary",), 48 * 1024 * 1024),
    )(out_a, out_b, x, target, gate, wo)


def dh_bwd(dz, w_full, x, dy, shift, scale, norm_g, dg_ctx):
    tl = 512

    def kern(dz_ref, w_ref, x_ref, dy_ref, sh_ref, sc_ref, g_ref, dgc_ref, gx_ref, dsh_ref, dsc_ref, dg_ref, acc):
        t, j = pl.program_id(0), pl.program_id(1)
        part = lax.dot_general(dz_ref[...].astype(BF16), w_ref[0], (((1,), (1,)), ((), ())),
                               preferred_element_type=F32)

        @pl.when(j == 0)
        def _():
            acc[...] = part

        @pl.when(j > 0)
        def _():
            acc[...] += part

        @pl.when((t == 0) & (j == 0))
        def _():
            dsh_ref[...] = jnp.zeros_like(dsh_ref)
            dsc_ref[...] = jnp.zeros_like(dsc_ref)
            dg_ref[...] = dgc_ref[...]

        @pl.when(j == NCHIP - 1)
        def _():
            _, vjp = jax.vjp(_modulated, x_ref[...], g_ref[...], sc_ref[...], sh_ref[...])
            dx, dg, dsc, dsh = vjp(acc[...])
            gx_ref[...] = dy_ref[...] + dx
            dg_ref[...] += dg
            dsc_ref[...] += dsc
            dsh_ref[...] += dsh

    tile = pl.BlockSpec((tl, DM), lambda t, j: (t, 0))
    return pl.pallas_call(
        kern, name="dh_bwd", grid=(SEQ // tl, NCHIP),
        in_specs=[pl.BlockSpec((tl, SHARD_IN), lambda t, j: (t, j)),
                  pl.BlockSpec((1, DM, SHARD_IN), lambda t, j: (j, 0, 0)), tile, tile, _row(DM), _row(DM), _row(DM),
                  _row(DM)],
        out_specs=[tile, _row(DM), _row(DM), _row(DM)],
        out_shape=[jax.ShapeDtypeStruct((SEQ, DM), F32)] + [jax.ShapeDtypeStruct((1, DM), F32)] * 3,
        scratch_shapes=[pltpu.VMEM((tl, DM), F32)],
        compiler_params=_cparams(("arbitrary", "arbitrary"), 40 * 1024 * 1024),
    )(dz, w_full, x, dy, shift, scale, norm_g, dg_ctx)


def dw_bwd(h, dz, hc, dzc):
    tl = 512

    def kern(h_ref, dz_ref, hc_ref, dzc_ref, dw_ref):
        part = lax.dot_general(h_ref[...], dz_ref[...].astype(BF16), (((0,), (0,)), ((), ())),
                               preferred_element_type=F32)

        @pl.when(pl.program_id(1) == 0)
        def _():
            dw_ref[0] = part + lax.dot_general(hc_ref[...], dzc_ref[...].astype(BF16), (((0,), (0,)), ((), ())),
                                               preferred_element_type=F32)

        @pl.when(pl.program_id(1) > 0)
        def _():
            dw_ref[0] += part

    return pl.pallas_call(
        kern, name="dw_bwd", grid=(NCHIP, SEQ // tl),
        in_specs=[pl.BlockSpec((tl, DM), lambda j, t: (t, 0)), pl.BlockSpec((tl, SHARD_IN), lambda j, t: (t, j)),
                  pl.BlockSpec((CTX, DM), lambda j, t: (0, 0)), pl.BlockSpec((CTX, SHARD_IN), lambda j, t: (0, j))],
        out_specs=pl.BlockSpec((1, DM, SHARD_IN), lambda j, t: (j, 0, 0)),
        out_shape=jax.ShapeDtypeStruct((NCHIP, DM, SHARD_IN), F32),
        compiler_params=_cparams(("arbitrary", "arbitrary"), 40 * 1024 * 1024),
    )(h, dz, hc, dzc)


def ctx_bwd(dzc, w_full, ctx, cshift, cscale, norm_g):
    def kern(dz_ref, w2_ref, w3_ref, c_ref, sh_ref, sc_ref, g_ref, dsh_ref, dsc_ref, dg_ref):
        nt = (((1,), (1,)), ((), ()))
        dz = dz_ref[...].astype(BF16)
        dhc = (lax.dot_general(dz[:, 2 * SHARD_IN:3 * SHARD_IN], w2_ref[0], nt, preferred_element_type=F32)
               + lax.dot_general(dz[:, 3 * SHARD_IN:], w3_ref[0], nt, preferred_element_type=F32))
        _, vjp = jax.vjp(lambda g, sc, sh: _modulated(c_ref[...], g, sc, sh), g_ref[...], sc_ref[...], sh_ref[...])
        dg_ref[...], dsc_ref[...], dsh_ref[...] = vjp(dhc)

    return pl.pallas_call(
        kern, name="ctx_bwd", grid=(1,),
        in_specs=[pl.BlockSpec((CTX, DIN), lambda i: (0, 0)),
                  pl.BlockSpec((1, DM, SHARD_IN), lambda i: (2, 0, 0)),
                  pl.BlockSpec((1, DM, SHARD_IN), lambda i: (3, 0, 0)),
                  pl.BlockSpec((CTX, DM), lambda i: (0, 0)), _row(DM), _row(DM), _row(DM)],
        out_specs=[_row(DM), _row(DM), _row(DM)],
        out_shape=[jax.ShapeDtypeStruct((1, DM), F32)] * 3,
        compiler_params=_cparams(("arbitrary",)),
    )(dzc, w_full, w_full, ctx, cshift, cscale, norm_g)


def _lane_pad_rpb(rpb):
    r = jnp.pad(rpb, ((0, 0), (0, 0), (0, GRID_W - rpb.shape[-1])))
    return jnp.concatenate([r, r], axis=-1)


def local_step(x, ctx, target, mod, cmod, norm_g, sgu_g, w_s, b_s, q_g, k_g, rpb, w_in_full, w_out_full):
    shift, scale, gate = mod[:, :DM], mod[:, DM:2 * DM], mod[:, 2 * DM:]
    cshift, cscale = cmod[:, :DM], cmod[:, DM:2 * DM]
    bsb = jnp.broadcast_to(b_s[:, :, None], (4, 128, 128))
    qg2, kg2 = jnp.tile(q_g, (1, 2)), jnp.tile(k_g, (1, 2))

    z, h = inproj_fwd(x, shift, scale, norm_g, w_in_full)
    zc, hc = ctx_fwd(ctx, cshift, cscale, norm_g, w_in_full)
    bias = rpb_tables(_lane_pad_rpb(rpb))
    out_a = sgu_fwd(z, sgu_g, w_s, bsb)
    out_b = attn_fwd(z, zc, bias, qg2, kg2)
    loss8, dy, dcat, dgate, dwo = outproj(out_a, out_b, x, target, gate, w_out_full)
    dz_a, dsg, dws, dbsb = sgu_bwd(z, sgu_g, w_s, bsb, dcat)
    dq, dk, dv, dbg, dck, dcv, dbias, dqg2, dkg2 = attn_bwd(z, zc, bias, qg2, kg2, dcat)
    drpb = rpb_bwd(dbias)[:, :, :rpb.shape[-1]]
    dz = jnp.concatenate([dz_a, dq, dk, dv, dbg], axis=-1)
    dzc = jnp.concatenate([jnp.zeros((CTX, 2048), F32), dck, dcv, jnp.zeros((CTX, 512), F32)], axis=-1)
    dcshift, dcscale, dng_c = ctx_bwd(dzc, w_in_full, ctx, cshift, cscale, norm_g)
    grad_x, dshift, dscale, dng = dh_bwd(dz, w_in_full, x, dy, shift, scale, norm_g, dng_c)
    dw_in = dw_bwd(h, dz, hc, dzc)
    return dict(
        loss=loss8[0:1, 0:1], grad_x=grad_x, dw_in=dw_in, dw_out=dwo,
        dmod=jnp.concatenate([dshift, dscale, dgate], axis=-1),
        dcmod=jnp.concatenate([dcshift, dcscale, jnp.zeros((1, DM), F32)], axis=-1),
        d_norm_g=dng, d_sgu_g=dsg, d_w_s=dws, d_b_s=dbsb[:, :, 0],
        d_q_g=dqg2[:, :HDIM], d_k_g=dkg2[:, :HDIM], d_rpb=drpb)


def _me():
    return lax.axis_index("x"), lax.axis_index("y"), lax.axis_index("c")


def _flip(q):
    x, y, c = _me()
    return ((1 - x) if q & 4 else x, (1 - y) if q & 2 else y, (1 - c) if q & 1 else c)


def _chip_of(dev):
    return 2 * dev[0] + dev[1]


def _rcopy(src, dst, send_sems, recv_sems, k, dev):
    return pltpu.make_async_remote_copy(src_ref=src, dst_ref=dst, send_sem=send_sems.at[k], recv_sem=recv_sems.at[k],
                                        device_id=dev, device_id_type=MESH_ID)


_VMEM_SPEC = pl.BlockSpec(memory_space=pltpu.VMEM)
CS_ROWS = 8 * NDEV + 8


def ada_fwd(c, c_ctx, w_ada, b_shard):
    def kern(c_ref, cc_ref, w_ref, b_ref, mod_ref, cs_ref, mine, send_sems, recv_sems):
        x, y, cc = _me()
        slot = lambda d: pl.ds(pl.multiple_of(8 * d, 8), 8)
        me = 4 * x + 2 * y + cc
        first = lax.broadcasted_iota(jnp.int32, (8, DM), 0) == 0
        mine[...] = jnp.where(first, jnp.broadcast_to(c_ref[...], (8, DM)), 0.0)
        cs_ref[slot(me), :] = mine[...]
        cs_ref[slot(NDEV), :] = jnp.where(first, jnp.broadcast_to(cc_ref[...], (8, DM)), 0.0)
        sends = [_rcopy(mine, cs_ref.at[slot(me), :], send_sems, recv_sems, q - 1, _flip(q)) for q in range(1, NDEV)]
        for cp in sends:
            cp.start()
        for q in range(1, NDEV):
            px, py, pc = _flip(q)
            _rcopy(mine, cs_ref.at[slot(4 * px + 2 * py + pc), :], send_sems, recv_sems, q - 1, _flip(q)).wait_recv()
        k = 2 * x + y
        act = jax.nn.silu(cs_ref[...]).astype(BF16)
        mod_ref[k] = jnp.dot(act, w_ref[...].astype(BF16), preferred_element_type=F32) + b_ref[...]
        sends2 = [_rcopy(mod_ref.at[k], mod_ref.at[k], send_sems, recv_sems, NDEV - 1 + q // 2 - 1, _flip(q))
                  for q in (2, 4, 6)]
        for cp in sends2:
            cp.start()
        for q in (2, 4, 6):
            kq = _chip_of(_flip(q))
            _rcopy(mod_ref.at[kq], mod_ref.at[kq], send_sems, recv_sems, NDEV - 1 + q // 2 - 1, _flip(q)).wait_recv()
        for cp in sends + sends2:
            cp.wait_send()

    return pl.pallas_call(
        kern, name="ada_fwd", in_specs=[_VMEM_SPEC] * 4, out_specs=[_VMEM_SPEC] * 2,
        out_shape=[jax.ShapeDtypeStruct((NCHIP, CS_ROWS, SHARD_ADA), F32), jax.ShapeDtypeStruct((CS_ROWS, DM), F32)],
        scratch_shapes=[pltpu.VMEM((8, DM), F32), pltpu.SemaphoreType.DMA((NDEV + 2,)),
                        pltpu.SemaphoreType.DMA((NDEV + 2,))],
    )(c, c_ctx, w_ada, b_shard)


def w_gather(w_in, w_out):
    hin, hout = DM // 2, SHARD_OUT // 2

    def kern(wi_ref, wo_ref, oi_ref, oo_ref, send_sems, recv_sems):
        x, y, c = _me()
        k = 2 * x + y
        sib = _flip(1)
        oi_ref[k] = wi_ref[...].astype(BF16)
        oo_ref[k] = wo_ref[...].astype(BF16)

        def blocks(chip, half):
            return (oi_ref.at[chip, pl.ds(pl.multiple_of(half * hin, hin), hin), :],
                    oo_ref.at[chip, pl.ds(pl.multiple_of(half * hout, hout), hout), :])

        sends = []
        for q in (2, 4, 6):
            for n, blk in enumerate(blocks(k, c)):
                sends.append(_rcopy(blk, blk, send_sems, recv_sems, 2 * (q // 2 - 1) + n, _flip(q)))
        for cp in sends:
            cp.start()
        passed = []
        for q in (2, 4, 6):
            for n, blk in enumerate(blocks(_chip_of(_flip(q)), c)):
                _rcopy(blk, blk, send_sems, recv_sems, 2 * (q // 2 - 1) + n, _flip(q)).wait_recv()
                cp = _rcopy(blk, blk, send_sems, recv_sems, 6 + 2 * (q // 2 - 1) + n, sib)
                cp.start()
                passed.append(cp)
        for q in (2, 4, 6):
            for n, blk in enumerate(blocks(_chip_of(_flip(q)), 1 - c)):
                _rcopy(blk, blk, send_sems, recv_sems, 6 + 2 * (q // 2 - 1) + n, sib).wait_recv()
        for cp in sends + passed:
            cp.wait_send()

    return pl.pallas_call(
        kern, name="w_gather", in_specs=[_VMEM_SPEC] * 2, out_specs=[_VMEM_SPEC] * 2,
        out_shape=[jax.ShapeDtypeStruct((NCHIP, DM, SHARD_IN), BF16), jax.ShapeDtypeStruct((NCHIP, SHARD_OUT, DM), BF16)],
        scratch_shapes=[pltpu.SemaphoreType.DMA((12,)), pltpu.SemaphoreType.DMA((12,))],
        compiler_params=pltpu.CompilerParams(vmem_limit_bytes=40 * 1024 * 1024),
    )(w_in, w_out)


SLAB_ROWS = 80


def small_gather(slab):
    def kern(s_ref, all_ref, tot_ref, send_sems, recv_sems):
        x, y, c = _me()
        me = 4 * x + 2 * y + c
        all_ref[me] = s_ref[...]
        sends = [_rcopy(s_ref, all_ref.at[me], send_sems, recv_sems, q - 1, _flip(q)) for q in range(1, NDEV)]
        for cp in sends:
            cp.start()
        for q in range(1, NDEV):
            px, py, pc = _flip(q)
            d = 4 * px + 2 * py + pc
            _rcopy(s_ref, all_ref.at[d], send_sems, recv_sems, q - 1, _flip(q)).wait_recv()
        tot = all_ref[0]
        for d in range(1, NDEV):
            tot = tot + all_ref[d]
        tot_ref[...] = tot
        for cp in sends:
            cp.wait_send()

    return pl.pallas_call(
        kern, name="small_gather", in_specs=[_VMEM_SPEC], out_specs=[_VMEM_SPEC] * 2,
        out_shape=[jax.ShapeDtypeStruct((NDEV, SLAB_ROWS, DM), F32), jax.ShapeDtypeStruct((SLAB_ROWS, DM), F32)],
        scratch_shapes=[pltpu.SemaphoreType.DMA((NDEV - 1,)), pltpu.SemaphoreType.DMA((NDEV - 1,))],
    )(slab)


def ada_bwd(a_in, dm, dm_shard, w_ada, c_ctx):
    def kern(a_ref, dm_ref, dms_ref, w_ref, cc_ref, dw_ref, db_ref, dcc_ref, parts, send_sems, recv_sems):
        x, y, c = _me()
        k = 2 * x + y
        act = jax.nn.silu(a_ref[...]).astype(BF16)
        dms = dms_ref[...].astype(BF16)
        dw_ref[...] = lax.dot_general(act, dms, (((0,), (0,)), ((), ())), preferred_element_type=F32)
        db_ref[...] = jnp.sum(dm_ref[...], axis=0, keepdims=True)
        parts[k] = lax.dot_general(dms, w_ref[...].astype(BF16), (((1,), (1,)), ((), ())), preferred_element_type=F32)
        sends = [_rcopy(parts.at[k], parts.at[k], send_sems, recv_sems, q // 2 - 1, _flip(q)) for q in (2, 4, 6)]
        for cp in sends:
            cp.start()
        for q in (2, 4, 6):
            kq = _chip_of(_flip(q))
            _rcopy(parts.at[kq], parts.at[kq], send_sems, recv_sems, q // 2 - 1, _flip(q)).wait_recv()
        dact = ((parts[0] + parts[1]) + parts[2]) + parts[3]
        _, vjp = jax.vjp(jax.nn.silu, cc_ref[...])
        dcc_ref[...] = vjp(dact[8:9, :])[0]
        for cp in sends:
            cp.wait_send()

    return pl.pallas_call(
        kern, name="ada_bwd", in_specs=[_VMEM_SPEC] * 5, out_specs=[_VMEM_SPEC] * 3,
        out_shape=[jax.ShapeDtypeStruct((DM, SHARD_ADA), F32), jax.ShapeDtypeStruct((1, 3 * DM), F32),
                   jax.ShapeDtypeStruct((1, DM), F32)],
        scratch_shapes=[pltpu.VMEM((NCHIP, 16, DM), F32), pltpu.SemaphoreType.DMA((3,)), pltpu.SemaphoreType.DMA((3,))],
    )(a_in, dm, dm_shard, w_ada, c_ctx)


def reduce_scatter(g, name):
    _, rows, width = g.shape
    rh = rows // 2

    def kern(g_hbm, out_ref, mine, rcv1, rcv2, load_sem, send_sems, recv_sems):
        x, y, c = _me()
        k = 2 * x + y
        sib = _flip(1)
        half = lambda h: pl.ds(pl.multiple_of(h * rh, rh), rh)
        load = pltpu.make_async_copy(g_hbm.at[:, half(c), :], mine, load_sem)
        load.start()
        pair = _rcopy(g_hbm.at[:, half(1 - c), :], rcv1, send_sems, recv_sems, 0, sib)
        pair.start()
        load.wait()
        pair.wait_recv()
        for j in range(NCHIP):
            mine[j] = mine[j] + rcv1[j]
        sends = [_rcopy(mine.at[_chip_of(_flip(q))], rcv2.at[q // 2 - 1], send_sems, recv_sems, q // 2, _flip(q))
                 for q in (2, 4, 6)]
        for cp in sends:
            cp.start()
        for q in (2, 4, 6):
            _rcopy(mine.at[0], rcv2.at[q // 2 - 1], send_sems, recv_sems, q // 2, _flip(q)).wait_recv()
        out_ref[half(c), :] = ((mine[k] + rcv2[0]) + rcv2[1]) + rcv2[2]
        share = _rcopy(out_ref.at[half(c), :], out_ref.at[half(c), :], send_sems, recv_sems, 4, sib)
        share.start()
        _rcopy(out_ref.at[half(1 - c), :], out_ref.at[half(1 - c), :], send_sems, recv_sems, 4, sib).wait_recv()
        for cp in [pair, share] + sends:
            cp.wait_send()

    return pl.pallas_call(
        kern, name=name, in_specs=[pl.BlockSpec(memory_space=pl.ANY)], out_specs=_VMEM_SPEC,
        out_shape=jax.ShapeDtypeStruct((rows, width), F32),
        scratch_shapes=[pltpu.VMEM((NCHIP, rh, width), F32), pltpu.VMEM((NCHIP, rh, width), F32),
                        pltpu.VMEM((NCHIP - 1, rh, width), F32), pltpu.SemaphoreType.DMA(()),
                        pltpu.SemaphoreType.DMA((5,)), pltpu.SemaphoreType.DMA((5,))],
        compiler_params=pltpu.CompilerParams(vmem_limit_bytes=40 * 1024 * 1024),
    )(g)


def _adamw_math(w, g, m, v):
    m = B1 * m + (1.0 - B1) * g
    v = B2 * v + (1.0 - B2) * (g * g)
    m_hat = m / (1.0 - B1 ** STEP)
    v_hat = v / (1.0 - B2 ** STEP)
    return -LR * (m_hat / (jnp.sqrt(v_hat) + ADAM_EPS) + WD * w), m, v


def adamw_big(w, g, m, v, name, block_rows=256):
    rows, width = w.shape

    def kern(w_ref, g_ref, m_ref, v_ref, d_ref, nm_ref, nv_ref):
        d_ref[...], nm_ref[...], nv_ref[...] = _adamw_math(w_ref[...], g_ref[...], m_ref[...], v_ref[...])

    spec = pl.BlockSpec((block_rows, width), lambda i: (i, 0))
    return pl.pallas_call(
        kern, name=name, grid=(rows // block_rows,), in_specs=[spec] * 4, out_specs=[spec] * 3,
        out_shape=[jax.ShapeDtypeStruct((rows, width), F32)] * 3,
        compiler_params=_cparams(("arbitrary",)),
    )(w, g, m, v)


def adamw_small(quads):
    n = len(quads)

    def kern(*refs):
        ins, outs = refs[:4 * n], refs[4 * n:]
        for i in range(n):
            w, g, m, v = (r[...] for r in ins[4 * i:4 * i + 4])
            outs[3 * i][...], outs[3 * i + 1][...], outs[3 * i + 2][...] = _adamw_math(w, g, m, v)

    flat = [a for quad in quads for a in quad]
    res = pl.pallas_call(
        kern, name="adamw_small", in_specs=[_VMEM_SPEC] * (4 * n), out_specs=[_VMEM_SPEC] * (3 * n),
        out_shape=[jax.ShapeDtypeStruct(q[0].shape, F32) for q in quads for _ in range(3)],
    )(*flat)
    return [tuple(res[3 * i:3 * i + 3]) for i in range(n)]


def _rows_of(a, rows):
    flat = a.reshape(-1)
    return jnp.pad(flat, (0, rows * DM - flat.shape[0])).reshape(rows, DM)


def kernel(x, c, ctx, c_ctx, w_ada, b_ada, norm_g, w_in, sgu_norm_g, w_spatial, b_spatial, q_norm_g, k_norm_g, rpb, w_out, loss_target, m_c_ctx, m_w_ada, m_b_ada, m_norm_g, m_w_in, m_sgu_norm_g, m_w_spatial, m_b_spatial, m_q_norm_g, m_k_norm_g, m_rpb, m_w_out, v_c_ctx, v_w_ada, v_b_ada, v_norm_g, v_w_in, v_sgu_norm_g, v_w_spatial, v_b_spatial, v_q_norm_g, v_k_norm_g, v_rpb, v_w_out):
    xi, yi, ci = lax.axis_index("x"), lax.axis_index("y"), lax.axis_index("c")
    chip, dev = 2 * xi + yi, 4 * xi + 2 * yi + ci
    c_ctx2 = c_ctx.reshape(1, DM)

    b_shard = lax.dynamic_slice(b_ada, (0, chip * SHARD_ADA), (1, SHARD_ADA))
    mod_all, cs = ada_fwd(c, c_ctx2, w_ada[0], b_shard)
    mods = mod_all.transpose(1, 0, 2).reshape(CS_ROWS, 3 * DM)
    mod = lax.dynamic_slice(mods, (8 * dev, 0), (1, 3 * DM))
    cmod = mods[8 * NDEV:8 * NDEV + 1]

    w_in_full, w_out_full = w_gather(w_in[0], w_out[0])
    part = local_step(x[0], ctx[0], loss_target[0], mod, cmod, norm_g, sgu_norm_g, w_spatial[0], b_spatial[0],
                      q_norm_g, k_norm_g, rpb[0], w_in_full, w_out_full.reshape(DM, DM))

    slab = jnp.concatenate([
        part["d_norm_g"], _rows_of(part["d_sgu_g"], 1), _rows_of(part["d_b_s"], 1),
        _rows_of(jnp.concatenate([part["d_q_g"], part["d_k_g"]], axis=-1), 1), _rows_of(part["d_rpb"], 4),
        _rows_of(part["loss"], 1), _rows_of(part["dcmod"], 3), _rows_of(part["dmod"], 3), jnp.zeros((1, DM), F32),
        _rows_of(part["d_w_s"], 64)], axis=0)
    gathered, tot = small_gather(slab)
    dm = jnp.concatenate([gathered[:, 12:15, :].reshape(NDEV, 3 * DM), tot[9:12].reshape(1, 3 * DM),
                          jnp.zeros((7, 3 * DM), F32)], axis=0)
    a_in = jnp.concatenate([cs[0:8 * NDEV:8], cs[8 * NDEV:8 * NDEV + 1], jnp.zeros((7, DM), F32)], axis=0)
    dm_shard = lax.dynamic_slice(dm, (0, chip * SHARD_ADA), (16, SHARD_ADA))
    g_w_ada, g_b_ada, g_c_ctx = ada_bwd(a_in, dm, dm_shard, w_ada[0], c_ctx2)
    g_w_in = reduce_scatter(part["dw_in"], "rs_w_in")
    g_w_out = reduce_scatter(part["dw_out"].reshape(NCHIP, SHARD_OUT, DM), "rs_w_out")

    loss = tot[8, 0]
    g_small = dict(
        c_ctx=g_c_ctx, b_ada=g_b_ada, norm_g=tot[0:1], sgu_norm_g=tot[1:2, :512], w_spatial=tot[16:80].reshape(512, 128),
        b_spatial=tot[2:3, :512].reshape(4, 128), q_norm_g=tot[3:4, :HDIM], k_norm_g=tot[3:4, HDIM:2 * HDIM],
        rpb=tot[4:8].reshape(-1)[:HEADS * 15 * 31].reshape(HEADS * 15, 31))
    shapes = dict(c_ctx=(DM,), w_ada=(1, DM, SHARD_ADA), b_ada=(1, 3 * DM), norm_g=(1, DM), w_in=(1, DM, SHARD_IN),
                  sgu_norm_g=(1, 512), w_spatial=(1, 4, 128, 128), b_spatial=(1, 4, 128), q_norm_g=(1, HDIM),
                  k_norm_g=(1, HDIM), rpb=(1, HEADS, 15, 31), w_out=(1, SHARD_OUT, DM))
    names = list(shapes)
    weights = dict(c_ctx=c_ctx, w_ada=w_ada, b_ada=b_ada, norm_g=norm_g, w_in=w_in, sgu_norm_g=sgu_norm_g,
                   w_spatial=w_spatial, b_spatial=b_spatial, q_norm_g=q_norm_g, k_norm_g=k_norm_g, rpb=rpb, w_out=w_out)
    m_in = dict(zip(names, (m_c_ctx, m_w_ada, m_b_ada, m_norm_g, m_w_in, m_sgu_norm_g, m_w_spatial, m_b_spatial,
                            m_q_norm_g, m_k_norm_g, m_rpb, m_w_out)))
    v_in = dict(zip(names, (v_c_ctx, v_w_ada, v_b_ada, v_norm_g, v_w_in, v_sgu_norm_g, v_w_spatial, v_b_spatial,
                            v_q_norm_g, v_k_norm_g, v_rpb, v_w_out)))
    grads = dict(g_small, w_ada=g_w_ada, w_in=g_w_in, w_out=g_w_out)
    upd = {}
    for n in ("w_ada", "w_in", "w_out"):
        g = grads[n]
        upd[n] = adamw_big(weights[n].reshape(g.shape), g, m_in[n].reshape(g.shape), v_in[n].reshape(g.shape),
                           "adamw_" + n)
    small = [n for n in names if n not in upd]
    res = adamw_small([(weights[n].reshape(grads[n].shape), grads[n], m_in[n].reshape(grads[n].shape),
                        v_in[n].reshape(grads[n].shape)) for n in small])
    upd.update(zip(small, res))
    out = [loss, part["grad_x"].reshape(1, SEQ, DM)]
    out += [grads[n].reshape(shapes[n]) for n in names]
    for slot in range(3):
        out += [upd[n][slot].reshape(shapes[n]) for n in names]
    return tuple(out)
```

```python
import functools

import jax
import jax.numpy as jnp
from jax import lax
from jax.experimental import pallas as pl
from jax.experimental.pallas import tpu as pltpu

F32, BF16 = jnp.float32, jnp.bfloat16
SEQ, DM, CTX, DIN = 4096, 1024, 256, 3584
NCHIP, NDEV = 4, 8
SHARD_IN = DIN // NCHIP
SHARD_ADA = 3 * DM // NCHIP
SHARD_OUT = DM // NCHIP
GRID_W = 64
QROWS = 4
KROWS = 12
QBLK, KBLK = QROWS * GRID_W, KROWS * GRID_W
NQBLK = SEQ // QBLK
HEADS, HDIM, NPAIR = 8, 64, 4
EPS = 1e-6
NEG_INF = -1e30
ZQ, ZK, ZV, ZG = 12, 16, 20, 24
LR, B1, B2, ADAM_EPS, WD, STEP = 0.001, 0.9, 0.999, 1e-08, 0.01, 10
VMEM_BIG = 56 * 1024 * 1024
MESH_ID = pl.DeviceIdType.MESH


def _dot(a, b, lhs_c, rhs_c):
    return lax.dot_general(a.astype(BF16), b.astype(BF16), (((lhs_c,), (rhs_c,)), ((), ())),
                           preferred_element_type=F32)


@jax.custom_vjp
def mm(a, b):
    return _dot(a, b, 1, 0)


@jax.custom_vjp
def mm_nt(a, b):
    return _dot(a, b, 1, 1)


@jax.custom_vjp
def mm_tn(a, b):
    return _dot(a, b, 0, 0)


mm.defvjp(lambda a, b: (mm(a, b), (a, b)), lambda r, ct: (mm_nt(ct, r[1]), mm_tn(r[0], ct)))
mm_nt.defvjp(lambda a, b: (mm_nt(a, b), (a, b)), lambda r, ct: (mm(ct, r[1]), mm_tn(ct, r[0])))
mm_tn.defvjp(lambda a, b: (mm_tn(a, b), (a, b)), lambda r, ct: (mm_nt(r[1], ct), mm(r[0], ct)))


def _rms(x, g):
    return x * lax.rsqrt(jnp.mean(x * x, axis=-1, keepdims=True) + EPS) * g


def _modulated(x, g, scale, shift):
    return _rms(x, g) * (1.0 + scale) + shift


def _pair_rms(x, g2):
    lo = lax.broadcasted_iota(jnp.int32, (1, 2 * HDIM), 1) < HDIM
    sq = x * x
    s_lo = jnp.sum(jnp.where(lo, sq, 0.0), axis=-1, keepdims=True)
    s_hi = jnp.sum(jnp.where(lo, 0.0, sq), axis=-1, keepdims=True)
    rs = jnp.where(lo, lax.rsqrt(s_lo / HDIM + EPS), lax.rsqrt(s_hi / HDIM + EPS))
    return x * rs * g2


def _cparams(sem, vmem=None):
    return pltpu.CompilerParams(dimension_semantics=sem, vmem_limit_bytes=vmem)


def _row(n):
    return pl.BlockSpec((1, n), lambda *_: (0, 0))


def inproj_fwd(x, shift, scale, norm_g, w_full):
    tl = 512

    def kern(x_ref, sh_ref, sc_ref, g_ref, w_ref, z_ref, h_ref):
        @pl.when(pl.program_id(1) == 0)
        def _():
            h_ref[...] = _modulated(x_ref[...], g_ref[...], sc_ref[...], sh_ref[...]).astype(BF16)

        z_ref[...] = jnp.dot(h_ref[...], w_ref[0], preferred_element_type=F32)

    return pl.pallas_call(
        kern, name="inproj_fwd", grid=(SEQ // tl, NCHIP),
        in_specs=[pl.BlockSpec((tl, DM), lambda t, j: (t, 0)), _row(DM), _row(DM), _row(DM),
                  pl.BlockSpec((1, DM, SHARD_IN), lambda t, j: (j, 0, 0))],
        out_specs=[pl.BlockSpec((tl, SHARD_IN), lambda t, j: (t, j)),
                   pl.BlockSpec((tl, DM), lambda t, j: (t, 0))],
        out_shape=[jax.ShapeDtypeStruct((SEQ, DIN), F32), jax.ShapeDtypeStruct((SEQ, DM), BF16)],
        compiler_params=_cparams(("arbitrary", "arbitrary"), 40 * 1024 * 1024),
    )(x, shift, scale, norm_g, w_full)


def ctx_fwd(ctx, cshift, cscale, norm_g, w_full):
    def kern(c_ref, sh_ref, sc_ref, g_ref, w2_ref, w3_ref, zc_ref, hc_ref):
        hc = _modulated(c_ref[...], g_ref[...], sc_ref[...], sh_ref[...]).astype(BF16)
        hc_ref[...] = hc
        zc_ref[:, :SHARD_IN] = jnp.dot(hc, w2_ref[0], preferred_element_type=F32)
        zc_ref[:, SHARD_IN:] = jnp.dot(hc, w3_ref[0], preferred_element_type=F32)

    return pl.pallas_call(
        kern, name="ctx_fwd", grid=(1,),
        in_specs=[pl.BlockSpec((CTX, DM), lambda i: (0, 0)), _row(DM), _row(DM), _row(DM),
                  pl.BlockSpec((1, DM, SHARD_IN), lambda i: (2, 0, 0)),
                  pl.BlockSpec((1, DM, SHARD_IN), lambda i: (3, 0, 0))],
        out_specs=[pl.BlockSpec((CTX, 2 * SHARD_IN), lambda i: (0, 0)),
                   pl.BlockSpec((CTX, DM), lambda i: (0, 0))],
        out_shape=[jax.ShapeDtypeStruct((CTX, 2 * SHARD_IN), F32), jax.ShapeDtypeStruct((CTX, DM), BF16)],
        compiler_params=_cparams(("arbitrary",)),
    )(ctx, cshift, cscale, norm_g, w_full, w_full)


SGU_CHUNK, SGU_PER_STEP = 128, 4


def _gelu(x):
    return 0.5 * x * (1.0 + lax.erf(x * 0.7071067811865476))


def _sgu_chunk(au, av, ag, sg, ws, bsb):
    u, v = _gelu(au), _gelu(av)
    outs = []
    for g in range(4):
        sl = slice(128 * g, 128 * (g + 1))
        mixed = mm(ws[g], _rms(v[:, sl], sg[:, sl])) + bsb[g]
        outs.append(u[:, sl] * mixed * jax.nn.silu(ag[:, sl]))
    return jnp.concatenate(outs, axis=-1)


def _sgu_specs():
    rows = SGU_CHUNK * SGU_PER_STEP
    zspec = lambda c: pl.BlockSpec((rows, 512), lambda n: (n, c))
    wspec = pl.BlockSpec((4, 128, 128), lambda n: (0, 0, 0))
    return rows, [zspec(0), zspec(1), zspec(2), _row(512), wspec, wspec]


def sgu_fwd(z, sg, ws, bsb):
    rows, in_specs = _sgu_specs()

    def kern(au_ref, av_ref, ag_ref, sg_ref, ws_ref, bs_ref, o_ref):
        for c in range(SGU_PER_STEP):
            sl = slice(c * SGU_CHUNK, (c + 1) * SGU_CHUNK)
            o_ref[sl, :] = _sgu_chunk(au_ref[sl, :], av_ref[sl, :], ag_ref[sl, :], sg_ref[...], ws_ref[...],
                                      bs_ref[...])

    return pl.pallas_call(
        kern, name="sgu_fwd", grid=(SEQ // rows,), in_specs=in_specs,
        out_specs=pl.BlockSpec((rows, 512), lambda n: (n, 0)),
        out_shape=jax.ShapeDtypeStruct((SEQ, 512), F32),
        compiler_params=_cparams(("arbitrary",)),
    )(z, z, z, sg, ws, bsb)


def sgu_bwd(z, sg, ws, bsb, dcat):
    rows, in_specs = _sgu_specs()

    def kern(au_ref, av_ref, ag_ref, sg_ref, ws_ref, bs_ref, do_ref, dz_ref, dsg_ref, dws_ref, dbs_ref):
        @pl.when(pl.program_id(0) == 0)
        def _():
            dsg_ref[...] = jnp.zeros_like(dsg_ref)
            dws_ref[...] = jnp.zeros_like(dws_ref)
            dbs_ref[...] = jnp.zeros_like(dbs_ref)

        for c in range(SGU_PER_STEP):
            sl = slice(c * SGU_CHUNK, (c + 1) * SGU_CHUNK)
            _, vjp = jax.vjp(_sgu_chunk, au_ref[sl, :], av_ref[sl, :], ag_ref[sl, :], sg_ref[...], ws_ref[...],
                             bs_ref[...])
            dau, dav, dag, dsg, dws, dbs = vjp(do_ref[sl, :])
            dz_ref[sl, 0:512] = dau
            dz_ref[sl, 512:1024] = dav
            dz_ref[sl, 1024:1536] = dag
            dsg_ref[...] += dsg
            dws_ref[...] += dws
            dbs_ref[...] += dbs

        @pl.when(pl.program_id(0) == pl.num_programs(0) - 1)
        def _():
            dbs_ref[...] = jnp.broadcast_to(jnp.sum(dbs_ref[...], axis=-1, keepdims=True), dbs_ref.shape)

    wspec = pl.BlockSpec((4, 128, 128), lambda n: (0, 0, 0))
    return pl.pallas_call(
        kern, name="sgu_bwd", grid=(SEQ // rows,),
        in_specs=in_specs + [pl.BlockSpec((rows, 512), lambda n: (n, 0))],
        out_specs=[pl.BlockSpec((rows, 1536), lambda n: (n, 0)), _row(512), wspec, wspec],
        out_shape=[jax.ShapeDtypeStruct((SEQ, 1536), F32), jax.ShapeDtypeStruct((1, 512), F32),
                   jax.ShapeDtypeStruct((4, 128, 128), F32), jax.ShapeDtypeStruct((4, 128, 128), F32)],
        compiler_params=_cparams(("arbitrary",)),
    )(z, z, z, sg, ws, bsb, dcat)


_DR_OFF = (7, 3, -1)


def _row_valid(v, rr, j):
    return (j < 8, rr <= j < rr + 8, 4 <= j < 12)[v]


def _col_window():
    q = lax.broadcasted_iota(jnp.int32, (GRID_W, 128), 0)
    kc = lax.broadcasted_iota(jnp.int32, (GRID_W, 128), 1) % GRID_W
    c0 = jnp.clip(q - 8, 0, GRID_W - 16)
    return (kc >= c0) & (kc < c0 + 16)


def rpb_tables(rpb2):
    def kern(r_ref, b_ref):
        base = r_ref[0]
        lo = lax.broadcasted_iota(jnp.int32, (1, 128), 1) < GRID_W
        win = _col_window()
        neg = jnp.full((GRID_W, 128), NEG_INF, F32)
        for v in range(3):
            for rr in range(QROWS):
                for jp in range(KROWS // 2):
                    j0, j1 = 2 * jp, 2 * jp + 1
                    ok0, ok1 = _row_valid(v, rr, j0), _row_valid(v, rr, j1)
                    if not (ok0 or ok1):
                        tile = neg
                    else:
                        d0 = j0 - rr + _DR_OFF[v]
                        r0 = base[d0:d0 + 1, :] if ok0 else jnp.zeros((1, 128), F32)
                        r1 = base[d0 + 1:d0 + 2, :] if ok1 else jnp.zeros((1, 128), F32)
                        y = jnp.broadcast_to(jnp.where(lo, r0, r1), (GRID_W, 128))
                        y = pltpu.roll(pltpu.roll(y, 128 - 15, 1), 0, 1, stride=1, stride_axis=0)
                        ok = win & jnp.where(lo, ok0, ok1)
                        tile = jnp.where(ok, y, NEG_INF)
                    b_ref[v, 0, rr * GRID_W:(rr + 1) * GRID_W, jp * 128:(jp + 1) * 128] = tile

    return pl.pallas_call(
        kern, name="rpb_tables", grid=(HEADS,),
        in_specs=[pl.BlockSpec((1, 15, 128), lambda h: (h, 0, 0))],
        out_specs=pl.BlockSpec((3, 1, QBLK, KBLK), lambda h: (0, h, 0, 0)),
        out_shape=jax.ShapeDtypeStruct((3, HEADS, QBLK, KBLK), F32),
        compiler_params=_cparams(("arbitrary",)),
    )(rpb2)


def rpb_bwd(dbias):
    def kern(g_ref, o_ref):
        lo = lax.broadcasted_iota(jnp.int32, (1, 128), 1) < GRID_W
        ri = lax.broadcasted_iota(jnp.int32, (GRID_W, GRID_W), 0)
        ci = lax.broadcasted_iota(jnp.int32, (GRID_W, GRID_W), 1)
        flip = (ri + ci == GRID_W - 1).astype(F32)
        acc = [jnp.zeros((1, 128), F32) for _ in range(15)]
        for v in range(3):
            for rr in range(QROWS):
                for jp in range(KROWS // 2):
                    j0, j1 = 2 * jp, 2 * jp + 1
                    ok0, ok1 = _row_valid(v, rr, j0), _row_valid(v, rr, j1)
                    if not (ok0 or ok1):
                        continue
                    g = g_ref[v, 0, rr * GRID_W:(rr + 1) * GRID_W, jp * 128:(jp + 1) * 128]
                    g = lax.dot_general(flip, g, (((1,), (0,)), ((), ())), precision=lax.Precision.HIGHEST,
                                        preferred_element_type=F32)
                    g = pltpu.roll(pltpu.roll(g, 128 - 48, 1), 0, 1, stride=1, stride_axis=0)
                    s = jnp.sum(g, axis=0, keepdims=True)
                    d0 = j0 - rr + _DR_OFF[v]
                    if ok0:
                        acc[d0] = acc[d0] + jnp.where(lo, s, 0.0)
                    if ok1:
                        acc[d0 + 1] = acc[d0 + 1] + jnp.where(lo, 0.0, s)
        for d in range(15):
            o_ref[0, d:d + 1, :] = acc[d] + pltpu.roll(acc[d], GRID_W, 1)

    return pl.pallas_call(
        kern, name="rpb_bwd", grid=(HEADS,),
        in_specs=[pl.BlockSpec((3, 1, QBLK, KBLK), lambda h: (0, h, 0, 0))],
        out_specs=pl.BlockSpec((1, 15, 128), lambda h: (h, 0, 0)),
        out_shape=jax.ShapeDtypeStruct((HEADS, 15, 128), F32),
        compiler_params=_cparams(("arbitrary",)),
    )(dbias)


def _attn_step(q_raw, kn, v, ckn, cv, bias2, qg):
    qn = _pair_rms(q_raw, qg) * (HDIM ** -0.5)
    lo = lax.broadcasted_iota(jnp.int32, (1, 2 * HDIM), 1) < HDIM
    out = None
    for a in range(2):
        mine = lo if a == 0 else jnp.logical_not(lo)
        qa = jnp.where(mine, qn, 0.0)
        s_lat = mm_nt(qa, kn) + bias2[a]
        s_ctx = mm_nt(qa, ckn)
        m = lax.stop_gradient(jnp.maximum(jnp.max(s_lat, axis=-1, keepdims=True),
                                          jnp.max(s_ctx, axis=-1, keepdims=True)))
        p_lat = jnp.exp(s_lat - m)
        p_ctx = jnp.exp(s_ctx - m)
        den = jnp.sum(p_lat, axis=-1, keepdims=True) + jnp.sum(p_ctx, axis=-1, keepdims=True)
        o = jnp.where(mine, (mm(p_lat, v) + mm(p_ctx, cv)) / den, 0.0)
        out = o if out is None else out + o
    return out


def _attn_gated(q_raw, kn, v, ckn, cv, bias2, qg, bg):
    return _attn_step(q_raw, kn, v, ckn, cv, bias2, qg) * jax.nn.silu(bg)


def _kstart(i):
    return pl.multiple_of(jnp.clip((i - 1) * QBLK, 0, SEQ - KBLK), QBLK)


def _bias_variant(i):
    return jnp.where(i == 0, 0, jnp.where(i == NQBLK - 1, 2, 1))


def _attn_in_specs():
    return [
        pl.BlockSpec((QBLK, 128), lambda p, i: (i, ZQ + p)),
        pl.BlockSpec((SEQ, 128), lambda p, i: (0, ZK + p)),
        pl.BlockSpec((SEQ, 128), lambda p, i: (0, ZV + p)),
        pl.BlockSpec((QBLK, 128), lambda p, i: (i, ZG + p)),
        pl.BlockSpec((CTX, 128), lambda p, i: (0, 2 + p)),
        pl.BlockSpec((CTX, 128), lambda p, i: (0, 6 + p)),
        pl.BlockSpec((1, 2, QBLK, KBLK), lambda p, i: (_bias_variant(i), p, 0, 0)),
        _row(128), _row(128),
    ]


NORM_ROWS = 512


def _norm_keys(k_ref, ck_ref, kg_ref, kn_scr, ckn_scr):
    def body(c, carry):
        sl = pl.ds(pl.multiple_of(c * NORM_ROWS, NORM_ROWS), NORM_ROWS)
        kn_scr[sl, :] = _pair_rms(k_ref[sl, :], kg_ref[...])
        return carry

    lax.fori_loop(0, SEQ // NORM_ROWS, body, 0)
    ckn_scr[...] = _pair_rms(ck_ref[...], kg_ref[...])


def attn_fwd(z, zc, bias, qg2, kg2):
    def kern(q_ref, k_ref, v_ref, bg_ref, ck_ref, cv_ref, b_ref, qg_ref, kg_ref, o_ref, kn_scr, ckn_scr):
        i = pl.program_id(1)

        @pl.when(i == 0)
        def _():
            _norm_keys(k_ref, ck_ref, kg_ref, kn_scr, ckn_scr)

        ks = pl.ds(_kstart(i), KBLK)
        o_ref[...] = _attn_gated(q_ref[...], kn_scr[ks, :], v_ref[ks, :], ckn_scr[...], cv_ref[...], b_ref[0],
                                 qg_ref[...], bg_ref[...])

    return pl.pallas_call(
        kern, name="attn_fwd", grid=(NPAIR, NQBLK), in_specs=_attn_in_specs(),
        out_specs=pl.BlockSpec((QBLK, 128), lambda p, i: (i, p)),
        out_shape=jax.ShapeDtypeStruct((SEQ, 512), F32),
        scratch_shapes=[pltpu.VMEM((SEQ, 128), F32), pltpu.VMEM((CTX, 128), F32)],
        compiler_params=_cparams(("arbitrary", "arbitrary"), 40 * 1024 * 1024),
    )(z, z, z, z, zc, zc, bias, qg2, kg2)


def attn_bwd(z, zc, bias, qg2, kg2, dcat):
    def kern(q_ref, k_ref, v_ref, bg_ref, ck_ref, cv_ref, b_ref, qg_ref, kg_ref, do_ref,
             dq_ref, dk_ref, dv_ref, dbg_ref, dck_ref, dcv_ref, db_ref, dqg_ref, dkg_ref,
             kn_scr, ckn_scr, dkn_scr, dckn_scr):
        p, i = pl.program_id(0), pl.program_id(1)
        last = i == NQBLK - 1

        @pl.when(i == 0)
        def _():
            _norm_keys(k_ref, ck_ref, kg_ref, kn_scr, ckn_scr)
            dkn_scr[...] = jnp.zeros_like(dkn_scr)
            dv_ref[...] = jnp.zeros_like(dv_ref)
            dckn_scr[...] = jnp.zeros_like(dckn_scr)
            dcv_ref[...] = jnp.zeros_like(dcv_ref)

        @pl.when((i == 0) & (p == 0))
        def _():
            dqg_ref[...] = jnp.zeros_like(dqg_ref)
            dkg_ref[...] = jnp.zeros_like(dkg_ref)

        ks = pl.ds(_kstart(i), KBLK)
        _, vjp = jax.vjp(_attn_gated, q_ref[...], kn_scr[ks, :], v_ref[ks, :], ckn_scr[...], cv_ref[...], b_ref[0],
                         qg_ref[...], bg_ref[...])
        dq, dkn, dv, dckn, dcv, db, dqg, dbg = vjp(do_ref[...])
        dq_ref[...] = dq
        dbg_ref[...] = dbg
        dkn_scr[ks, :] += dkn
        dv_ref[ks, :] += dv
        dckn_scr[...] += dckn
        dcv_ref[...] += dcv
        dqg_ref[...] += dqg
        fresh = (i == 0) | (i == 1) | last

        @pl.when(fresh)
        def _():
            db_ref[0] = db

        @pl.when(jnp.logical_not(fresh))
        def _():
            db_ref[0] += db

        @pl.when(last)
        def _():
            def body(c, dkg):
                sl = pl.ds(pl.multiple_of(c * NORM_ROWS, NORM_ROWS), NORM_ROWS)
                _, nvjp = jax.vjp(_pair_rms, k_ref[sl, :], kg_ref[...])
                dk, dg = nvjp(dkn_scr[sl, :])
                dk_ref[sl, :] = dk
                return dkg + dg

            dkg = lax.fori_loop(0, SEQ // NORM_ROWS, body, jnp.zeros((1, 128), F32))
            _, nvjp = jax.vjp(_pair_rms, ck_ref[...], kg_ref[...])
            dck, dg = nvjp(dckn_scr[...])
            dck_ref[...] = dck
            dkg_ref[...] += dkg + dg

        @pl.when(last & (p == NPAIR - 1))
        def _():
            dqg_ref[...] = dqg_ref[...] + pltpu.roll(dqg_ref[...], HDIM, 1)
            dkg_ref[...] = dkg_ref[...] + pltpu.roll(dkg_ref[...], HDIM, 1)

    blk = lambda rows: pl.BlockSpec((rows, 128), lambda p, i: (0, p))
    qblk = pl.BlockSpec((QBLK, 128), lambda p, i: (i, p))
    return pl.pallas_call(
        kern, name="attn_bwd", grid=(NPAIR, NQBLK),
        in_specs=_attn_in_specs() + [pl.BlockSpec((QBLK, 128), lambda p, i: (i, 4 + p))],
        out_specs=[qblk, blk(SEQ), blk(SEQ), qblk, blk(CTX), blk(CTX),
                   pl.BlockSpec((1, 2, QBLK, KBLK), lambda p, i: (_bias_variant(i), p, 0, 0)),
                   _row(128), _row(128)],
        out_shape=[jax.ShapeDtypeStruct((SEQ, 512), F32)] * 4 + [jax.ShapeDtypeStruct((CTX, 512), F32)] * 2
        + [jax.ShapeDtypeStruct((3, HEADS, QBLK, KBLK), F32), jax.ShapeDtypeStruct((1, 128), F32),
           jax.ShapeDtypeStruct((1, 128), F32)],
        scratch_shapes=[pltpu.VMEM((SEQ, 128), F32), pltpu.VMEM((CTX, 128), F32),
                        pltpu.VMEM((SEQ, 128), F32), pltpu.VMEM((CTX, 128), F32)],
        compiler_params=_cparams(("arbitrary", "arbitrary"), VMEM_BIG),
    )(z, z, z, z, zc, zc, bias, qg2, kg2, dcat)


def outproj(out_a, out_b, x, target, gate, wo):
    tl = 512

    def kern(a_ref, b_ref, x_ref, t_ref, g_ref, w_ref, loss_ref, dy_ref, dcat_ref, dg_ref, dw_ref):
        @pl.when(pl.program_id(0) == 0)
        def _():
            loss_ref[...] = jnp.zeros_like(loss_ref)
            dg_ref[...] = jnp.zeros_like(dg_ref)
            dw_ref[...] = jnp.zeros_like(dw_ref)

        a, b = a_ref[...].astype(BF16), b_ref[...].astype(BF16)
        mix = (jnp.dot(a, w_ref[0:512, :], preferred_element_type=F32)
               + jnp.dot(b, w_ref[512:1024, :], preferred_element_type=F32))
        err = x_ref[...] + g_ref[...] * mix - t_ref[...]
        loss_ref[...] += 0.5 * jnp.sum(jnp.mean(err * err, axis=-1))
        dy = err * (1.0 / DM)
        dy_ref[...] = dy
        dg_ref[...] += jnp.sum(dy * mix, axis=0, keepdims=True)
        dmix = (g_ref[...] * dy).astype(BF16)
        dcat_ref[...] = lax.dot_general(dmix, w_ref[...], (((1,), (1,)), ((), ())), preferred_element_type=F32)
        dw_ref[0:512, :] += lax.dot_general(a, dmix, (((0,), (0,)), ((), ())), preferred_element_type=F32)
        dw_ref[512:1024, :] += lax.dot_general(b, dmix, (((0,), (0,)), ((), ())), preferred_element_type=F32)

    tile = lambda w: pl.BlockSpec((tl, w), lambda t: (t, 0))
    whole = pl.BlockSpec((DM, DM), lambda t: (0, 0))
    return pl.pallas_call(
        kern, name="outproj", grid=(SEQ // tl,),
        in_specs=[tile(512), tile(512), tile(DM), tile(DM), _row(DM), whole],
        out_specs=[pl.BlockSpec((8, 128), lambda t: (0, 0)), tile(DM), tile(DM), _row(DM), whole],
        out_shape=[jax.ShapeDtypeStruct((8, 128), F32), jax.ShapeDtypeStruct((SEQ, DM), F32),
                   jax.ShapeDtypeStruct((SEQ, DM), F32), jax.ShapeDtypeStruct((1, DM), F32),
                   jax.ShapeDtypeStruct((DM, DM), F32)],
        compiler_params=_cparams(("arbitrary",), 48 * 1024 * 1024),
    )(out_a, out_b, x, target, gate, wo)


def dh_bwd(dz, w_full, x, dy, shift, scale, norm_g, dg_ctx):
    tl = 512

    def kern(dz_ref, w_ref, x_ref, dy_ref, sh_ref, sc_ref, g_ref, dgc_ref, gx_ref, dsh_ref, dsc_ref, dg_ref, acc):
        t, j = pl.program_id(0), pl.program_id(1)
        part = lax.dot_general(dz_ref[...].astype(BF16), w_ref[0], (((1,), (1,)), ((), ())),
                               preferred_element_type=F32)

        @pl.when(j == 0)
        def _():
            acc[...] = part

        @pl.when(j > 0)
        def _():
            acc[...] += part

        @pl.when((t == 0) & (j == 0))
        def _():
            dsh_ref[...] = jnp.zeros_like(dsh_ref)
            dsc_ref[...] = jnp.zeros_like(dsc_ref)
            dg_ref[...] = dgc_ref[...]

        @pl.when(j == NCHIP - 1)
        def _():
            _, vjp = jax.vjp(_modulated, x_ref[...], g_ref[...], sc_ref[...], sh_ref[...])
            dx, dg, dsc, dsh = vjp(acc[...])
            gx_ref[...] = dy_ref[...] + dx
            dg_ref[...] += dg
            dsc_ref[...] += dsc
            dsh_ref[...] += dsh

    tile = pl.BlockSpec((tl, DM), lambda t, j: (t, 0))
    return pl.pallas_call(
        kern, name="dh_bwd", grid=(SEQ // tl, NCHIP),
        in_specs=[pl.BlockSpec((tl, SHARD_IN), lambda t, j: (t, j)),
                  pl.BlockSpec((1, DM, SHARD_IN), lambda t, j: (j, 0, 0)), tile, tile, _row(DM), _row(DM), _row(DM),
                  _row(DM)],
        out_specs=[tile, _row(DM), _row(DM), _row(DM)],
        out_shape=[jax.ShapeDtypeStruct((SEQ, DM), F32)] + [jax.ShapeDtypeStruct((1, DM), F32)] * 3,
        scratch_shapes=[pltpu.VMEM((tl, DM), F32)],
        compiler_params=_cparams(("arbitrary", "arbitrary"), 40 * 1024 * 1024),
    )(dz, w_full, x, dy, shift, scale, norm_g, dg_ctx)


def dw_bwd(h, dz, hc, dzc):
    tl = 512

    def kern(h_ref, dz_ref, hc_ref, dzc_ref, dw_ref):
        part = lax.dot_general(h_ref[...], dz_ref[...].astype(BF16), (((0,), (0,)), ((), ())),
                               preferred_element_type=F32)

        @pl.when(pl.program_id(1) == 0)
        def _():
            dw_ref[0] = part + lax.dot_general(hc_ref[...], dzc_ref[...].astype(BF16), (((0,), (0,)), ((), ())),
                                               preferred_element_type=F32)

        @pl.when(pl.program_id(1) > 0)
        def _():
            dw_ref[0] += part

    return pl.pallas_call(
        kern, name="dw_bwd", grid=(NCHIP, SEQ // tl),
        in_specs=[pl.BlockSpec((tl, DM), lambda j, t: (t, 0)), pl.BlockSpec((tl, SHARD_IN), lambda j, t: (t, j)),
                  pl.BlockSpec((CTX, DM), lambda j, t: (0, 0)), pl.BlockSpec((CTX, SHARD_IN), lambda j, t: (0, j))],
        out_specs=pl.BlockSpec((1, DM, SHARD_IN), lambda j, t: (j, 0, 0)),
        out_shape=jax.ShapeDtypeStruct((NCHIP, DM, SHARD_IN), F32),
        compiler_params=_cparams(("arbitrary", "arbitrary"), 40 * 1024 * 1024),
    )(h, dz, hc, dzc)


def ctx_bwd(dzc, w_full, ctx, cshift, cscale, norm_g):
    def kern(dz_ref, w2_ref, w3_ref, c_ref, sh_ref, sc_ref, g_ref, dsh_ref, dsc_ref, dg_ref):
        nt = (((1,), (1,)), ((), ()))
        dz = dz_ref[...].astype(BF16)
        dhc = (lax.dot_general(dz[:, 2 * SHARD_IN:3 * SHARD_IN], w2_ref[0], nt, preferred_element_type=F32)
               + lax.dot_general(dz[:, 3 * SHARD_IN:], w3_ref[0], nt, preferred_element_type=F32))
        _, vjp = jax.vjp(lambda g, sc, sh: _modulated(c_ref[...], g, sc, sh), g_ref[...], sc_ref[...], sh_ref[...])
        dg_ref[...], dsc_ref[...], dsh_ref[...] = vjp(dhc)

    return pl.pallas_call(
        kern, name="ctx_bwd", grid=(1,),
        in_specs=[pl.BlockSpec((CTX, DIN), lambda i: (0, 0)),
                  pl.BlockSpec((1, DM, SHARD_IN), lambda i: (2, 0, 0)),
                  pl.BlockSpec((1, DM, SHARD_IN), lambda i: (3, 0, 0)),
                  pl.BlockSpec((CTX, DM), lambda i: (0, 0)), _row(DM), _row(DM), _row(DM)],
        out_specs=[_row(DM), _row(DM), _row(DM)],
        out_shape=[jax.ShapeDtypeStruct((1, DM), F32)] * 3,
        compiler_params=_cparams(("arbitrary",)),
    )(dzc, w_full, w_full, ctx, cshift, cscale, norm_g)


def _lane_pad_rpb(rpb):
    r = jnp.pad(rpb, ((0, 0), (0, 0), (0, GRID_W - rpb.shape[-1])))
    return jnp.concatenate([r, r], axis=-1)


def local_step(x, ctx, target, mod, cmod, norm_g, sgu_g, w_s, b_s, q_g, k_g, rpb, w_in_full, w_out_full):
    shift, scale, gate = mod[:, :DM], mod[:, DM:2 * DM], mod[:, 2 * DM:]
    cshift, cscale = cmod[:, :DM], cmod[:, DM:2 * DM]
    bsb = jnp.broadcast_to(b_s[:, :, None], (4, 128, 128))
    qg2, kg2 = jnp.tile(q_g, (1, 2)), jnp.tile(k_g, (1, 2))

    z, h = inproj_fwd(x, shift, scale, norm_g, w_in_full)
    zc, hc = ctx_fwd(ctx, cshift, cscale, norm_g, w_in_full)
    bias = rpb_tables(_lane_pad_rpb(rpb))
    out_a = sgu_fwd(z, sgu_g, w_s, bsb)
    out_b = attn_fwd(z, zc, bias, qg2, kg2)
    loss8, dy, dcat, dgate, dwo = outproj(out_a, out_b, x, target, gate, w_out_full)
    dz_a, dsg, dws, dbsb = sgu_bwd(z, sgu_g, w_s, bsb, dcat)
    dq, dk, dv, dbg, dck, dcv, dbias, dqg2, dkg2 = attn_bwd(z, zc, bias, qg2, kg2, dcat)
    drpb = rpb_bwd(dbias)[:, :, :rpb.shape[-1]]
    dz = jnp.concatenate([dz_a, dq, dk, dv, dbg], axis=-1)
    dzc = jnp.concatenate([jnp.zeros((CTX, 2048), F32), dck, dcv, jnp.zeros((CTX, 512), F32)], axis=-1)
    dcshift, dcscale, dng_c = ctx_bwd(dzc, w_in_full, ctx, cshift, cscale, norm_g)
    grad_x, dshift, dscale, dng = dh_bwd(dz, w_in_full, x, dy, shift, scale, norm_g, dng_c)
    dw_in = dw_bwd(h, dz, hc, dzc)
    return dict(
        loss=loss8[0:1, 0:1], grad_x=grad_x, dw_in=dw_in, dw_out=dwo,
        dmod=jnp.concatenate([dshift, dscale, dgate], axis=-1),
        dcmod=jnp.concatenate([dcshift, dcscale, jnp.zeros((1, DM), F32)], axis=-1),
        d_norm_g=dng, d_sgu_g=dsg, d_w_s=dws, d_b_s=dbsb[:, :, 0],
        d_q_g=dqg2[:, :HDIM], d_k_g=dkg2[:, :HDIM], d_rpb=drpb)


def _me():
    return lax.axis_index("x"), lax.axis_index("y"), lax.axis_index("c")


def _flip(q):
    x, y, c = _me()
    return ((1 - x) if q & 4 else x, (1 - y) if q & 2 else y, (1 - c) if q & 1 else c)


def _chip_of(dev):
    return 2 * dev[0] + dev[1]


def _rcopy(src, dst, send_sems, recv_sems, k, dev):
    return pltpu.make_async_remote_copy(src_ref=src, dst_ref=dst, send_sem=send_sems.at[k], recv_sem=recv_sems.at[k],
                                        device_id=dev, device_id_type=MESH_ID)


_VMEM_SPEC = pl.BlockSpec(memory_space=pltpu.VMEM)
CS_ROWS = 8 * NDEV + 8


def gather_fwd(c, c_ctx, w_ada, b_shard, w_in, w_out):
    hin, hout = DM // 2, SHARD_OUT // 2
    n_w, n_c, n_m = 12, NDEV - 1, 3

    def kern(c_ref, cc_ref, wa_ref, b_ref, wi_ref, wo_ref, oi_ref, oo_ref, mod_ref, cs_ref, mine, send_sems, recv_sems):
        x, y, cc = _me()
        k, me = 2 * x + y, 4 * x + 2 * y + cc
        sib = _flip(1)
        oi_ref[k] = wi_ref[...].astype(BF16)
        oo_ref[k] = wo_ref[...].astype(BF16)

        def blocks(chip, half):
            return (oi_ref.at[chip, pl.ds(pl.multiple_of(half * hin, hin), hin), :],
                    oo_ref.at[chip, pl.ds(pl.multiple_of(half * hout, hout), hout), :])

        sends = []
        for q in (2, 4, 6):
            for n, blk in enumerate(blocks(k, cc)):
                sends.append(_rcopy(blk, blk, send_sems, recv_sems, 2 * (q // 2 - 1) + n, _flip(q)))
        for cp in sends:
            cp.start()

        slot = lambda d: pl.ds(pl.multiple_of(8 * d, 8), 8)
        first = lax.broadcasted_iota(jnp.int32, (8, DM), 0) == 0
        mine[...] = jnp.where(first, jnp.broadcast_to(c_ref[...], (8, DM)), 0.0)
        cs_ref[slot(me), :] = mine[...]
        cs_ref[slot(NDEV), :] = jnp.where(first, jnp.broadcast_to(cc_ref[...], (8, DM)), 0.0)
        csends = [_rcopy(mine, cs_ref.at[slot(me), :], send_sems, recv_sems, n_w + q - 1, _flip(q))
                  for q in range(1, NDEV)]
        for cp in csends:
            cp.start()
        for q in range(1, NDEV):
            px, py, pc = _flip(q)
            _rcopy(mine, cs_ref.at[slot(4 * px + 2 * py + pc), :], send_sems, recv_sems, n_w + q - 1,
                   _flip(q)).wait_recv()
        act = jax.nn.silu(cs_ref[...]).astype(BF16)
        mod_ref[k] = jnp.dot(act, wa_ref[...].astype(BF16), preferred_element_type=F32) + b_ref[...]
        msends = [_rcopy(mod_ref.at[k], mod_ref.at[k], send_sems, recv_sems, n_w + n_c + q // 2 - 1, _flip(q))
                  for q in (2, 4, 6)]
        for cp in msends:
            cp.start()
        for q in (2, 4, 6):
            kq = _chip_of(_flip(q))
            _rcopy(mod_ref.at[kq], mod_ref.at[kq], send_sems, recv_sems, n_w + n_c + q // 2 - 1, _flip(q)).wait_recv()

        passed = []
        for q in (2, 4, 6):
            for n, blk in enumerate(blocks(_chip_of(_flip(q)), cc)):
                _rcopy(blk, blk, send_sems, recv_sems, 2 * (q // 2 - 1) + n, _flip(q)).wait_recv()
                cp = _rcopy(blk, blk, send_sems, recv_sems, 6 + 2 * (q // 2 - 1) + n, sib)
                cp.start()
                passed.append(cp)
        for q in (2, 4, 6):
            for n, blk in enumerate(blocks(_chip_of(_flip(q)), 1 - cc)):
                _rcopy(blk, blk, send_sems, recv_sems, 6 + 2 * (q // 2 - 1) + n, sib).wait_recv()
        for cp in sends + csends + msends + passed:
            cp.wait_send()

    n_sem = n_w + n_c + n_m
    return pl.pallas_call(
        kern, name="gather_fwd", in_specs=[_VMEM_SPEC] * 6, out_specs=[_VMEM_SPEC] * 4,
        out_shape=[jax.ShapeDtypeStruct((NCHIP, DM, SHARD_IN), BF16), jax.ShapeDtypeStruct((NCHIP, SHARD_OUT, DM), BF16),
                   jax.ShapeDtypeStruct((NCHIP, CS_ROWS, SHARD_ADA), F32), jax.ShapeDtypeStruct((CS_ROWS, DM), F32)],
        scratch_shapes=[pltpu.VMEM((8, DM), F32), pltpu.SemaphoreType.DMA((n_sem,)), pltpu.SemaphoreType.DMA((n_sem,))],
        compiler_params=pltpu.CompilerParams(vmem_limit_bytes=48 * 1024 * 1024),
    )(c, c_ctx, w_ada, b_shard, w_in, w_out)


SLAB_ROWS = 80


def small_gather(slab):
    def kern(s_ref, all_ref, tot_ref, send_sems, recv_sems):
        x, y, c = _me()
        me = 4 * x + 2 * y + c
        all_ref[me] = s_ref[...]
        sends = [_rcopy(s_ref, all_ref.at[me], send_sems, recv_sems, q - 1, _flip(q)) for q in range(1, NDEV)]
        for cp in sends:
            cp.start()
        for q in range(1, NDEV):
            px, py, pc = _flip(q)
            d = 4 * px + 2 * py + pc
            _rcopy(s_ref, all_ref.at[d], send_sems, recv_sems, q - 1, _flip(q)).wait_recv()
        tot = all_ref[0]
        for d in range(1, NDEV):
            tot = tot + all_ref[d]
        tot_ref[...] = tot
        for cp in sends:
            cp.wait_send()

    return pl.pallas_call(
        kern, name="small_gather", in_specs=[_VMEM_SPEC], out_specs=[_VMEM_SPEC] * 2,
        out_shape=[jax.ShapeDtypeStruct((NDEV, SLAB_ROWS, DM), F32), jax.ShapeDtypeStruct((SLAB_ROWS, DM), F32)],
        scratch_shapes=[pltpu.SemaphoreType.DMA((NDEV - 1,)), pltpu.SemaphoreType.DMA((NDEV - 1,))],
    )(slab)


def ada_bwd(a_in, dm, dm_shard, w_ada, c_ctx):
    def kern(a_ref, dm_ref, dms_ref, w_ref, cc_ref, dw_ref, db_ref, dcc_ref, parts, send_sems, recv_sems):
        x, y, c = _me()
        k = 2 * x + y
        act = jax.nn.silu(a_ref[...]).astype(BF16)
        dms = dms_ref[...].astype(BF16)
        dw_ref[...] = lax.dot_general(act, dms, (((0,), (0,)), ((), ())), preferred_element_type=F32)
        db_ref[...] = jnp.sum(dm_ref[...], axis=0, keepdims=True)
        parts[k] = lax.dot_general(dms, w_ref[...].astype(BF16), (((1,), (1,)), ((), ())), preferred_element_type=F32)
        sends = [_rcopy(parts.at[k], parts.at[k], send_sems, recv_sems, q // 2 - 1, _flip(q)) for q in (2, 4, 6)]
        for cp in sends:
            cp.start()
        for q in (2, 4, 6):
            kq = _chip_of(_flip(q))
            _rcopy(parts.at[kq], parts.at[kq], send_sems, recv_sems, q // 2 - 1, _flip(q)).wait_recv()
        dact = ((parts[0] + parts[1]) + parts[2]) + parts[3]
        _, vjp = jax.vjp(jax.nn.silu, cc_ref[...])
        dcc_ref[...] = vjp(dact[8:9, :])[0]
        for cp in sends:
            cp.wait_send()

    return pl.pallas_call(
        kern, name="ada_bwd", in_specs=[_VMEM_SPEC] * 5, out_specs=[_VMEM_SPEC] * 3,
        out_shape=[jax.ShapeDtypeStruct((DM, SHARD_ADA), F32), jax.ShapeDtypeStruct((1, 3 * DM), F32),
                   jax.ShapeDtypeStruct((1, DM), F32)],
        scratch_shapes=[pltpu.VMEM((NCHIP, 16, DM), F32), pltpu.SemaphoreType.DMA((3,)), pltpu.SemaphoreType.DMA((3,))],
    )(a_in, dm, dm_shard, w_ada, c_ctx)


def reduce_scatter(g, name):
    _, rows, width = g.shape
    rh = rows // 2

    def kern(g_hbm, out_ref, mine, rcv1, wire, rcv2, load_sem, send_sems, recv_sems):
        x, y, c = _me()
        k = 2 * x + y
        sib = _flip(1)
        half = lambda h: pl.ds(pl.multiple_of(h * rh, rh), rh)
        load = pltpu.make_async_copy(g_hbm.at[:, half(c), :], mine, load_sem)
        load.start()
        pair = _rcopy(g_hbm.at[:, half(1 - c), :], rcv1, send_sems, recv_sems, 0, sib)
        pair.start()
        load.wait()
        pair.wait_recv()
        for j in range(NCHIP):
            pair_sum = mine[j] + rcv1[j]
            mine[j] = pair_sum
            wire[j] = pair_sum.astype(BF16)
        sends = [_rcopy(wire.at[_chip_of(_flip(q))], rcv2.at[q // 2 - 1], send_sems, recv_sems, q // 2, _flip(q))
                 for q in (2, 4, 6)]
        for cp in sends:
            cp.start()
        for q in (2, 4, 6):
            _rcopy(wire.at[0], rcv2.at[q // 2 - 1], send_sems, recv_sems, q // 2, _flip(q)).wait_recv()
        out_ref[half(c), :] = ((mine[k] + rcv2[0].astype(F32)) + rcv2[1].astype(F32)) + rcv2[2].astype(F32)
        share = _rcopy(out_ref.at[half(c), :], out_ref.at[half(c), :], send_sems, recv_sems, 4, sib)
        share.start()
        _rcopy(out_ref.at[half(1 - c), :], out_ref.at[half(1 - c), :], send_sems, recv_sems, 4, sib).wait_recv()
        for cp in [pair, share] + sends:
            cp.wait_send()

    return pl.pallas_call(
        kern, name=name, in_specs=[pl.BlockSpec(memory_space=pl.ANY)], out_specs=_VMEM_SPEC,
        out_shape=jax.ShapeDtypeStruct((rows, width), F32),
        scratch_shapes=[pltpu.VMEM((NCHIP, rh, width), F32), pltpu.VMEM((NCHIP, rh, width), F32),
                        pltpu.VMEM((NCHIP, rh, width), BF16), pltpu.VMEM((NCHIP - 1, rh, width), BF16),
                        pltpu.SemaphoreType.DMA(()),
                        pltpu.SemaphoreType.DMA((5,)), pltpu.SemaphoreType.DMA((5,))],
        compiler_params=pltpu.CompilerParams(vmem_limit_bytes=40 * 1024 * 1024),
    )(g)


def _adamw_math(w, g, m, v):
    m = B1 * m + (1.0 - B1) * g
    v = B2 * v + (1.0 - B2) * (g * g)
    m_hat = m / (1.0 - B1 ** STEP)
    v_hat = v / (1.0 - B2 ** STEP)
    return -LR * (m_hat / (jnp.sqrt(v_hat) + ADAM_EPS) + WD * w), m, v


def adamw_big(w, g, m, v, name, block_rows=256):
    rows, width = w.shape

    def kern(w_ref, g_ref, m_ref, v_ref, d_ref, nm_ref, nv_ref):
        d_ref[...], nm_ref[...], nv_ref[...] = _adamw_math(w_ref[...], g_ref[...], m_ref[...], v_ref[...])

    spec = pl.BlockSpec((block_rows, width), lambda i: (i, 0))
    return pl.pallas_call(
        kern, name=name, grid=(rows // block_rows,), in_specs=[spec] * 4, out_specs=[spec] * 3,
        out_shape=[jax.ShapeDtypeStruct((rows, width), F32)] * 3,
        compiler_params=_cparams(("arbitrary",)),
    )(w, g, m, v)


def adamw_small(quads):
    n = len(quads)

    def kern(*refs):
        ins, outs = refs[:4 * n], refs[4 * n:]
        for i in range(n):
            w, g, m, v = (r[...] for r in ins[4 * i:4 * i + 4])
            outs[3 * i][...], outs[3 * i + 1][...], outs[3 * i + 2][...] = _adamw_math(w, g, m, v)

    flat = [a for quad in quads for a in quad]
    res = pl.pallas_call(
        kern, name="adamw_small", in_specs=[_VMEM_SPEC] * (4 * n), out_specs=[_VMEM_SPEC] * (3 * n),
        out_shape=[jax.ShapeDtypeStruct(q[0].shape, F32) for q in quads for _ in range(3)],
    )(*flat)
    return [tuple(res[3 * i:3 * i + 3]) for i in range(n)]


def _rows_of(a, rows):
    flat = a.reshape(-1)
    return jnp.pad(flat, (0, rows * DM - flat.shape[0])).reshape(rows, DM)


def kernel(x, c, ctx, c_ctx, w_ada, b_ada, norm_g, w_in, sgu_norm_g, w_spatial, b_spatial, q_norm_g, k_norm_g, rpb, w_out, loss_target, m_c_ctx, m_w_ada, m_b_ada, m_norm_g, m_w_in, m_sgu_norm_g, m_w_spatial, m_b_spatial, m_q_norm_g, m_k_norm_g, m_rpb, m_w_out, v_c_ctx, v_w_ada, v_b_ada, v_norm_g, v_w_in, v_sgu_norm_g, v_w_spatial, v_b_spatial, v_q_norm_g, v_k_norm_g, v_rpb, v_w_out):
    xi, yi, ci = lax.axis_index("x"), lax.axis_index("y"), lax.axis_index("c")
    chip, dev = 2 * xi + yi, 4 * xi + 2 * yi + ci
    c_ctx2 = c_ctx.reshape(1, DM)

    b_shard = lax.dynamic_slice(b_ada, (0, chip * SHARD_ADA), (1, SHARD_ADA))
    w_in_full, w_out_full, mod_all, cs = gather_fwd(c, c_ctx2, w_ada[0], b_shard, w_in[0], w_out[0])
    mods = mod_all.transpose(1, 0, 2).reshape(CS_ROWS, 3 * DM)
    mod = lax.dynamic_slice(mods, (8 * dev, 0), (1, 3 * DM))
    cmod = mods[8 * NDEV:8 * NDEV + 1]

    part = local_step(x[0], ctx[0], loss_target[0], mod, cmod, norm_g, sgu_norm_g, w_spatial[0], b_spatial[0],
                      q_norm_g, k_norm_g, rpb[0], w_in_full, w_out_full.reshape(DM, DM))

    slab = jnp.concatenate([
        part["d_norm_g"], _rows_of(part["d_sgu_g"], 1), _rows_of(part["d_b_s"], 1),
        _rows_of(jnp.concatenate([part["d_q_g"], part["d_k_g"]], axis=-1), 1), _rows_of(part["d_rpb"], 4),
        _rows_of(part["loss"], 1), _rows_of(part["dcmod"], 3), _rows_of(part["dmod"], 3), jnp.zeros((1, DM), F32),
        _rows_of(part["d_w_s"], 64)], axis=0)
    gathered, tot = small_gather(slab)
    dm = jnp.concatenate([gathered[:, 12:15, :].reshape(NDEV, 3 * DM), tot[9:12].reshape(1, 3 * DM),
                          jnp.zeros((7, 3 * DM), F32)], axis=0)
    a_in = jnp.concatenate([cs[0:8 * NDEV:8], cs[8 * NDEV:8 * NDEV + 1], jnp.zeros((7, DM), F32)], axis=0)
    dm_shard = lax.dynamic_slice(dm, (0, chip * SHARD_ADA), (16, SHARD_ADA))
    g_w_ada, g_b_ada, g_c_ctx = ada_bwd(a_in, dm, dm_shard, w_ada[0], c_ctx2)
    g_w_in = reduce_scatter(part["dw_in"], "rs_w_in")
    g_w_out = reduce_scatter(part["dw_out"].reshape(NCHIP, SHARD_OUT, DM), "rs_w_out")

    loss = tot[8, 0]
    g_small = dict(
        c_ctx=g_c_ctx, b_ada=g_b_ada, norm_g=tot[0:1], sgu_norm_g=tot[1:2, :512], w_spatial=tot[16:80].reshape(512, 128),
        b_spatial=tot[2:3, :512].reshape(4, 128), q_norm_g=tot[3:4, :HDIM], k_norm_g=tot[3:4, HDIM:2 * HDIM],
        rpb=tot[4:8].reshape(-1)[:HEADS * 15 * 31].reshape(HEADS * 15, 31))
    shapes = dict(c_ctx=(DM,), w_ada=(1, DM, SHARD_ADA), b_ada=(1, 3 * DM), norm_g=(1, DM), w_in=(1, DM, SHARD_IN),
                  sgu_norm_g=(1, 512), w_spatial=(1, 4, 128, 128), b_spatial=(1, 4, 128), q_norm_g=(1, HDIM),
                  k_norm_g=(1, HDIM), rpb=(1, HEADS, 15, 31), w_out=(1, SHARD_OUT, DM))
    names = list(shapes)
    weights = dict(c_ctx=c_ctx, w_ada=w_ada, b_ada=b_ada, norm_g=norm_g, w_in=w_in, sgu_norm_g=sgu_norm_g,
                   w_spatial=w_spatial, b_spatial=b_spatial, q_norm_g=q_norm_g, k_norm_g=k_norm_g, rpb=rpb, w_out=w_out)
    m_in = dict(zip(names, (m_c_ctx, m_w_ada, m_b_ada, m_norm_g, m_w_in, m_sgu_norm_g, m_w_spatial, m_b_spatial,
                            m_q_norm_g, m_k_norm_g, m_rpb, m_w_out)))
    v_in = dict(zip(names, (v_c_ctx, v_w_ada, v_b_ada, v_norm_g, v_w_in, v_sgu_norm_g, v_w_spatial, v_b_spatial,
                            v_q_norm_g, v_k_norm_g, v_rpb, v_w_out)))
    grads = dict(g_small, w_ada=g_w_ada, w_in=g_w_in, w_out=g_w_out)
    upd = {}
    for n in ("w_ada", "w_in", "w_out"):
        g = grads[n]
        upd[n] = adamw_big(weights[n].reshape(g.shape), g, m_in[n].reshape(g.shape), v_in[n].reshape(g.shape),
                           "adamw_" + n)
    small = [n for n in names if n not in upd]
    res = adamw_small([(weights[n].reshape(grads[n].shape), grads[n], m_in[n].reshape(grads[n].shape),
                        v_in[n].reshape(grads[n].shape)) for n in small])
    upd.update(zip(small, res))
    out = [loss, part["grad_x"].reshape(1, SEQ, DM)]
    out += [grads[n].reshape(shapes[n]) for n in names]
    for slot in range(3):
        out += [upd[n][slot].reshape(shapes[n]) for n in names]
    return tuple(out)
```

```python
import functools

import jax
import jax.numpy as jnp
from jax import lax
from jax.experimental import pallas as pl
from jax.experimental.pallas import tpu as pltpu

F32, BF16 = jnp.float32, jnp.bfloat16
SEQ, DM, CTX, DIN = 4096, 1024, 256, 3584
NCHIP, NDEV = 4, 8
SHARD_IN = DIN // NCHIP
SHARD_ADA = 3 * DM // NCHIP
SHARD_OUT = DM // NCHIP
GRID_W = 64
QROWS = 4
KROWS = 12
QBLK, KBLK = QROWS * GRID_W, KROWS * GRID_W
NQBLK = SEQ // QBLK
HEADS, HDIM, NPAIR = 8, 64, 4
EPS = 1e-6
NEG_INF = -1e30
ZQ, ZK, ZV, ZG = 12, 16, 20, 24
LR, B1, B2, ADAM_EPS, WD, STEP = 0.001, 0.9, 0.999, 1e-08, 0.01, 10
VMEM_BIG = 56 * 1024 * 1024
MESH_ID = pl.DeviceIdType.MESH


def _dot(a, b, lhs_c, rhs_c):
    return lax.dot_general(a.astype(BF16), b.astype(BF16), (((lhs_c,), (rhs_c,)), ((), ())),
                           preferred_element_type=F32)


@jax.custom_vjp
def mm(a, b):
    return _dot(a, b, 1, 0)


@jax.custom_vjp
def mm_nt(a, b):
    return _dot(a, b, 1, 1)


@jax.custom_vjp
def mm_tn(a, b):
    return _dot(a, b, 0, 0)


mm.defvjp(lambda a, b: (mm(a, b), (a, b)), lambda r, ct: (mm_nt(ct, r[1]), mm_tn(r[0], ct)))
mm_nt.defvjp(lambda a, b: (mm_nt(a, b), (a, b)), lambda r, ct: (mm(ct, r[1]), mm_tn(ct, r[0])))
mm_tn.defvjp(lambda a, b: (mm_tn(a, b), (a, b)), lambda r, ct: (mm_nt(r[1], ct), mm(r[0], ct)))


def _rms(x, g):
    return x * lax.rsqrt(jnp.mean(x * x, axis=-1, keepdims=True) + EPS) * g


def _modulated(x, g, scale, shift):
    return _rms(x, g) * (1.0 + scale) + shift


def _pair_rms(x, g2):
    lo = lax.broadcasted_iota(jnp.int32, (1, 2 * HDIM), 1) < HDIM
    sq = x * x
    s_lo = jnp.sum(jnp.where(lo, sq, 0.0), axis=-1, keepdims=True)
    s_hi = jnp.sum(jnp.where(lo, 0.0, sq), axis=-1, keepdims=True)
    rs = jnp.where(lo, lax.rsqrt(s_lo / HDIM + EPS), lax.rsqrt(s_hi / HDIM + EPS))
    return x * rs * g2


def _cparams(sem, vmem=None):
    return pltpu.CompilerParams(dimension_semantics=sem, vmem_limit_bytes=vmem)


def _row(n):
    return pl.BlockSpec((1, n), lambda *_: (0, 0))


def inproj_fwd(x, shift, scale, norm_g, w_full):
    tl = 512

    def kern(x_ref, sh_ref, sc_ref, g_ref, w_ref, z_ref, h_ref):
        @pl.when(pl.program_id(1) == 0)
        def _():
            h_ref[...] = _modulated(x_ref[...], g_ref[...], sc_ref[...], sh_ref[...]).astype(BF16)

        z_ref[...] = jnp.dot(h_ref[...], w_ref[0], preferred_element_type=F32)

    return pl.pallas_call(
        kern, name="inproj_fwd", grid=(SEQ // tl, NCHIP),
        in_specs=[pl.BlockSpec((tl, DM), lambda t, j: (t, 0)), _row(DM), _row(DM), _row(DM),
                  pl.BlockSpec((1, DM, SHARD_IN), lambda t, j: (j, 0, 0))],
        out_specs=[pl.BlockSpec((tl, SHARD_IN), lambda t, j: (t, j)),
                   pl.BlockSpec((tl, DM), lambda t, j: (t, 0))],
        out_shape=[jax.ShapeDtypeStruct((SEQ, DIN), F32), jax.ShapeDtypeStruct((SEQ, DM), BF16)],
        compiler_params=_cparams(("arbitrary", "arbitrary"), 40 * 1024 * 1024),
    )(x, shift, scale, norm_g, w_full)


def ctx_fwd(ctx, cshift, cscale, norm_g, w_full):
    def kern(c_ref, sh_ref, sc_ref, g_ref, w2_ref, w3_ref, zc_ref, hc_ref):
        hc = _modulated(c_ref[...], g_ref[...], sc_ref[...], sh_ref[...]).astype(BF16)
        hc_ref[...] = hc
        zc_ref[:, :SHARD_IN] = jnp.dot(hc, w2_ref[0], preferred_element_type=F32)
        zc_ref[:, SHARD_IN:] = jnp.dot(hc, w3_ref[0], preferred_element_type=F32)

    return pl.pallas_call(
        kern, name="ctx_fwd", grid=(1,),
        in_specs=[pl.BlockSpec((CTX, DM), lambda i: (0, 0)), _row(DM), _row(DM), _row(DM),
                  pl.BlockSpec((1, DM, SHARD_IN), lambda i: (2, 0, 0)),
                  pl.BlockSpec((1, DM, SHARD_IN), lambda i: (3, 0, 0))],
        out_specs=[pl.BlockSpec((CTX, 2 * SHARD_IN), lambda i: (0, 0)),
                   pl.BlockSpec((CTX, DM), lambda i: (0, 0))],
        out_shape=[jax.ShapeDtypeStruct((CTX, 2 * SHARD_IN), F32), jax.ShapeDtypeStruct((CTX, DM), BF16)],
        compiler_params=_cparams(("arbitrary",)),
    )(ctx, cshift, cscale, norm_g, w_full, w_full)


SGU_CHUNK, SGU_PER_STEP = 128, 4


def _gelu(x):
    return 0.5 * x * (1.0 + lax.erf(x * 0.7071067811865476))


def _sgu_chunk(au, av, ag, sg, ws, bsb):
    u, v = _gelu(au), _gelu(av)
    outs = []
    for g in range(4):
        sl = slice(128 * g, 128 * (g + 1))
        mixed = mm(ws[g], _rms(v[:, sl], sg[:, sl])) + bsb[g]
        outs.append(u[:, sl] * mixed * jax.nn.silu(ag[:, sl]))
    return jnp.concatenate(outs, axis=-1)


def _sgu_specs():
    rows = SGU_CHUNK * SGU_PER_STEP
    zspec = lambda c: pl.BlockSpec((rows, 512), lambda n: (n, c))
    wspec = pl.BlockSpec((4, 128, 128), lambda n: (0, 0, 0))
    return rows, [zspec(0), zspec(1), zspec(2), _row(512), wspec, wspec]


def sgu_fwd(z, sg, ws, bsb):
    rows, in_specs = _sgu_specs()

    def kern(au_ref, av_ref, ag_ref, sg_ref, ws_ref, bs_ref, o_ref):
        for c in range(SGU_PER_STEP):
            sl = slice(c * SGU_CHUNK, (c + 1) * SGU_CHUNK)
            o_ref[sl, :] = _sgu_chunk(au_ref[sl, :], av_ref[sl, :], ag_ref[sl, :], sg_ref[...], ws_ref[...],
                                      bs_ref[...])

    return pl.pallas_call(
        kern, name="sgu_fwd", grid=(SEQ // rows,), in_specs=in_specs,
        out_specs=pl.BlockSpec((rows, 512), lambda n: (n, 0)),
        out_shape=jax.ShapeDtypeStruct((SEQ, 512), F32),
        compiler_params=_cparams(("arbitrary",)),
    )(z, z, z, sg, ws, bsb)


def sgu_bwd(z, sg, ws, bsb, dcat):
    rows, in_specs = _sgu_specs()

    def kern(au_ref, av_ref, ag_ref, sg_ref, ws_ref, bs_ref, do_ref, dz_ref, dsg_ref, dws_ref, dbs_ref):
        @pl.when(pl.program_id(0) == 0)
        def _():
            dsg_ref[...] = jnp.zeros_like(dsg_ref)
            dws_ref[...] = jnp.zeros_like(dws_ref)
            dbs_ref[...] = jnp.zeros_like(dbs_ref)

        for c in range(SGU_PER_STEP):
            sl = slice(c * SGU_CHUNK, (c + 1) * SGU_CHUNK)
            _, vjp = jax.vjp(_sgu_chunk, au_ref[sl, :], av_ref[sl, :], ag_ref[sl, :], sg_ref[...], ws_ref[...],
                             bs_ref[...])
            dau, dav, dag, dsg, dws, dbs = vjp(do_ref[sl, :])
            dz_ref[sl, 0:512] = dau.astype(BF16)
            dz_ref[sl, 512:1024] = dav.astype(BF16)
            dz_ref[sl, 1024:1536] = dag.astype(BF16)
            dsg_ref[...] += dsg
            dws_ref[...] += dws
            dbs_ref[...] += dbs

        @pl.when(pl.program_id(0) == pl.num_programs(0) - 1)
        def _():
            dbs_ref[...] = jnp.broadcast_to(jnp.sum(dbs_ref[...], axis=-1, keepdims=True), dbs_ref.shape)

    wspec = pl.BlockSpec((4, 128, 128), lambda n: (0, 0, 0))
    return pl.pallas_call(
        kern, name="sgu_bwd", grid=(SEQ // rows,),
        in_specs=in_specs + [pl.BlockSpec((rows, 512), lambda n: (n, 0))],
        out_specs=[pl.BlockSpec((rows, 1536), lambda n: (n, 0)), _row(512), wspec, wspec],
        out_shape=[jax.ShapeDtypeStruct((SEQ, 1536), BF16), jax.ShapeDtypeStruct((1, 512), F32),
                   jax.ShapeDtypeStruct((4, 128, 128), F32), jax.ShapeDtypeStruct((4, 128, 128), F32)],
        compiler_params=_cparams(("arbitrary",)),
    )(z, z, z, sg, ws, bsb, dcat)


_DR_OFF = (7, 3, -1)


def _row_valid(v, rr, j):
    return (j < 8, rr <= j < rr + 8, 4 <= j < 12)[v]


def _col_window():
    q = lax.broadcasted_iota(jnp.int32, (GRID_W, 128), 0)
    kc = lax.broadcasted_iota(jnp.int32, (GRID_W, 128), 1) % GRID_W
    c0 = jnp.clip(q - 8, 0, GRID_W - 16)
    return (kc >= c0) & (kc < c0 + 16)


def rpb_tables(rpb2):
    def kern(r_ref, b_ref):
        base = r_ref[0]
        lo = lax.broadcasted_iota(jnp.int32, (1, 128), 1) < GRID_W
        win = _col_window()
        neg = jnp.full((GRID_W, 128), NEG_INF, F32)
        for v in range(3):
            for rr in range(QROWS):
                for jp in range(KROWS // 2):
                    j0, j1 = 2 * jp, 2 * jp + 1
                    ok0, ok1 = _row_valid(v, rr, j0), _row_valid(v, rr, j1)
                    if not (ok0 or ok1):
                        tile = neg
                    else:
                        d0 = j0 - rr + _DR_OFF[v]
                        r0 = base[d0:d0 + 1, :] if ok0 else jnp.zeros((1, 128), F32)
                        r1 = base[d0 + 1:d0 + 2, :] if ok1 else jnp.zeros((1, 128), F32)
                        y = jnp.broadcast_to(jnp.where(lo, r0, r1), (GRID_W, 128))
                        y = pltpu.roll(pltpu.roll(y, 128 - 15, 1), 0, 1, stride=1, stride_axis=0)
                        ok = win & jnp.where(lo, ok0, ok1)
                        tile = jnp.where(ok, y, NEG_INF)
                    b_ref[v, 0, rr * GRID_W:(rr + 1) * GRID_W, jp * 128:(jp + 1) * 128] = tile

    return pl.pallas_call(
        kern, name="rpb_tables", grid=(HEADS,),
        in_specs=[pl.BlockSpec((1, 15, 128), lambda h: (h, 0, 0))],
        out_specs=pl.BlockSpec((3, 1, QBLK, KBLK), lambda h: (0, h, 0, 0)),
        out_shape=jax.ShapeDtypeStruct((3, HEADS, QBLK, KBLK), F32),
        compiler_params=_cparams(("arbitrary",)),
    )(rpb2)


def rpb_bwd(dbias):
    def kern(g_ref, o_ref):
        lo = lax.broadcasted_iota(jnp.int32, (1, 128), 1) < GRID_W
        ri = lax.broadcasted_iota(jnp.int32, (GRID_W, GRID_W), 0)
        ci = lax.broadcasted_iota(jnp.int32, (GRID_W, GRID_W), 1)
        flip = (ri + ci == GRID_W - 1).astype(F32)
        acc = [jnp.zeros((1, 128), F32) for _ in range(15)]
        for v in range(3):
            for rr in range(QROWS):
                for jp in range(KROWS // 2):
                    j0, j1 = 2 * jp, 2 * jp + 1
                    ok0, ok1 = _row_valid(v, rr, j0), _row_valid(v, rr, j1)
                    if not (ok0 or ok1):
                        continue
                    g = g_ref[v, 0, rr * GRID_W:(rr + 1) * GRID_W, jp * 128:(jp + 1) * 128]
                    g = lax.dot_general(flip, g, (((1,), (0,)), ((), ())), precision=lax.Precision.HIGHEST,
                                        preferred_element_type=F32)
                    g = pltpu.roll(pltpu.roll(g, 128 - 48, 1), 0, 1, stride=1, stride_axis=0)
                    s = jnp.sum(g, axis=0, keepdims=True)
                    d0 = j0 - rr + _DR_OFF[v]
                    if ok0:
                        acc[d0] = acc[d0] + jnp.where(lo, s, 0.0)
                    if ok1:
                        acc[d0 + 1] = acc[d0 + 1] + jnp.where(lo, 0.0, s)
        for d in range(15):
            o_ref[0, d:d + 1, :] = acc[d] + pltpu.roll(acc[d], GRID_W, 1)

    return pl.pallas_call(
        kern, name="rpb_bwd", grid=(HEADS,),
        in_specs=[pl.BlockSpec((3, 1, QBLK, KBLK), lambda h: (0, h, 0, 0))],
        out_specs=pl.BlockSpec((1, 15, 128), lambda h: (h, 0, 0)),
        out_shape=jax.ShapeDtypeStruct((HEADS, 15, 128), F32),
        compiler_params=_cparams(("arbitrary",)),
    )(dbias)


def _attn_step(q_raw, kn, v, ckn, cv, bias2, qg):
    qn = _pair_rms(q_raw, qg) * (HDIM ** -0.5)
    lo = lax.broadcasted_iota(jnp.int32, (1, 2 * HDIM), 1) < HDIM
    out = None
    for a in range(2):
        mine = lo if a == 0 else jnp.logical_not(lo)
        qa = jnp.where(mine, qn, 0.0)
        s_lat = mm_nt(qa, kn) + bias2[a]
        s_ctx = mm_nt(qa, ckn)
        m = lax.stop_gradient(jnp.maximum(jnp.max(s_lat, axis=-1, keepdims=True),
                                          jnp.max(s_ctx, axis=-1, keepdims=True)))
        p_lat = jnp.exp(s_lat - m)
        p_ctx = jnp.exp(s_ctx - m)
        den = jnp.sum(p_lat, axis=-1, keepdims=True) + jnp.sum(p_ctx, axis=-1, keepdims=True)
        o = jnp.where(mine, (mm(p_lat, v) + mm(p_ctx, cv)) / den, 0.0)
        out = o if out is None else out + o
    return out


def _attn_gated(q_raw, kn, v, ckn, cv, bias2, qg, bg):
    return _attn_step(q_raw, kn, v, ckn, cv, bias2, qg) * jax.nn.silu(bg)


def _kstart(i):
    return pl.multiple_of(jnp.clip((i - 1) * QBLK, 0, SEQ - KBLK), QBLK)


def _bias_variant(i):
    return jnp.where(i == 0, 0, jnp.where(i == NQBLK - 1, 2, 1))


def _attn_in_specs():
    return [
        pl.BlockSpec((QBLK, 128), lambda p, i: (i, ZQ + p)),
        pl.BlockSpec((SEQ, 128), lambda p, i: (0, ZK + p)),
        pl.BlockSpec((SEQ, 128), lambda p, i: (0, ZV + p)),
        pl.BlockSpec((QBLK, 128), lambda p, i: (i, ZG + p)),
        pl.BlockSpec((CTX, 128), lambda p, i: (0, 2 + p)),
        pl.BlockSpec((CTX, 128), lambda p, i: (0, 6 + p)),
        pl.BlockSpec((1, 2, QBLK, KBLK), lambda p, i: (_bias_variant(i), p, 0, 0)),
        _row(128), _row(128),
    ]


NORM_ROWS = 512


def _norm_keys(k_ref, ck_ref, kg_ref, kn_scr, ckn_scr):
    def body(c, carry):
        sl = pl.ds(pl.multiple_of(c * NORM_ROWS, NORM_ROWS), NORM_ROWS)
        kn_scr[sl, :] = _pair_rms(k_ref[sl, :], kg_ref[...])
        return carry

    lax.fori_loop(0, SEQ // NORM_ROWS, body, 0)
    ckn_scr[...] = _pair_rms(ck_ref[...], kg_ref[...])


def attn_fwd(z, zc, bias, qg2, kg2):
    def kern(q_ref, k_ref, v_ref, bg_ref, ck_ref, cv_ref, b_ref, qg_ref, kg_ref, o_ref, kn_scr, ckn_scr):
        i = pl.program_id(1)

        @pl.when(i == 0)
        def _():
            _norm_keys(k_ref, ck_ref, kg_ref, kn_scr, ckn_scr)

        ks = pl.ds(_kstart(i), KBLK)
        o_ref[...] = _attn_gated(q_ref[...], kn_scr[ks, :], v_ref[ks, :], ckn_scr[...], cv_ref[...], b_ref[0],
                                 qg_ref[...], bg_ref[...])

    return pl.pallas_call(
        kern, name="attn_fwd", grid=(NPAIR, NQBLK), in_specs=_attn_in_specs(),
        out_specs=pl.BlockSpec((QBLK, 128), lambda p, i: (i, p)),
        out_shape=jax.ShapeDtypeStruct((SEQ, 512), F32),
        scratch_shapes=[pltpu.VMEM((SEQ, 128), F32), pltpu.VMEM((CTX, 128), F32)],
        compiler_params=_cparams(("arbitrary", "arbitrary"), 40 * 1024 * 1024),
    )(z, z, z, z, zc, zc, bias, qg2, kg2)


def attn_bwd(z, zc, bias, qg2, kg2, dcat):
    def kern(q_ref, k_ref, v_ref, bg_ref, ck_ref, cv_ref, b_ref, qg_ref, kg_ref, do_ref,
             dq_ref, dk_ref, dv_ref, dbg_ref, dck_ref, dcv_ref, db_ref, dqg_ref, dkg_ref,
             kn_scr, ckn_scr, dkn_scr, dckn_scr, dv_scr):
        p, i = pl.program_id(0), pl.program_id(1)
        last = i == NQBLK - 1

        @pl.when(i == 0)
        def _():
            _norm_keys(k_ref, ck_ref, kg_ref, kn_scr, ckn_scr)
            dkn_scr[...] = jnp.zeros_like(dkn_scr)
            dv_scr[...] = jnp.zeros_like(dv_scr)
            dckn_scr[...] = jnp.zeros_like(dckn_scr)
            dcv_ref[...] = jnp.zeros_like(dcv_ref)

        @pl.when((i == 0) & (p == 0))
        def _():
            dqg_ref[...] = jnp.zeros_like(dqg_ref)
            dkg_ref[...] = jnp.zeros_like(dkg_ref)

        ks = pl.ds(_kstart(i), KBLK)
        _, vjp = jax.vjp(_attn_gated, q_ref[...], kn_scr[ks, :], v_ref[ks, :], ckn_scr[...], cv_ref[...], b_ref[0],
                         qg_ref[...], bg_ref[...])
        dq, dkn, dv, dckn, dcv, db, dqg, dbg = vjp(do_ref[...])
        dq_ref[...] = dq.astype(BF16)
        dbg_ref[...] = dbg.astype(BF16)
        dkn_scr[ks, :] += dkn
        dv_scr[ks, :] += dv
        dckn_scr[...] += dckn
        dcv_ref[...] += dcv
        dqg_ref[...] += dqg
        fresh = (i == 0) | (i == 1) | last

        @pl.when(fresh)
        def _():
            db_ref[0] = db

        @pl.when(jnp.logical_not(fresh))
        def _():
            db_ref[0] += db

        @pl.when(last)
        def _():
            def body(c, dkg):
                sl = pl.ds(pl.multiple_of(c * NORM_ROWS, NORM_ROWS), NORM_ROWS)
                _, nvjp = jax.vjp(_pair_rms, k_ref[sl, :], kg_ref[...])
                dk, dg = nvjp(dkn_scr[sl, :])
                dk_ref[sl, :] = dk.astype(BF16)
                dv_ref[sl, :] = dv_scr[sl, :].astype(BF16)
                return dkg + dg

            dkg = lax.fori_loop(0, SEQ // NORM_ROWS, body, jnp.zeros((1, 128), F32))
            _, nvjp = jax.vjp(_pair_rms, ck_ref[...], kg_ref[...])
            dck, dg = nvjp(dckn_scr[...])
            dck_ref[...] = dck
            dkg_ref[...] += dkg + dg

        @pl.when(last & (p == NPAIR - 1))
        def _():
            dqg_ref[...] = dqg_ref[...] + pltpu.roll(dqg_ref[...], HDIM, 1)
            dkg_ref[...] = dkg_ref[...] + pltpu.roll(dkg_ref[...], HDIM, 1)

    blk = lambda rows: pl.BlockSpec((rows, 128), lambda p, i: (0, p))
    qblk = pl.BlockSpec((QBLK, 128), lambda p, i: (i, p))
    return pl.pallas_call(
        kern, name="attn_bwd", grid=(NPAIR, NQBLK),
        in_specs=_attn_in_specs() + [pl.BlockSpec((QBLK, 128), lambda p, i: (i, 4 + p))],
        out_specs=[qblk, blk(SEQ), blk(SEQ), qblk, blk(CTX), blk(CTX),
                   pl.BlockSpec((1, 2, QBLK, KBLK), lambda p, i: (_bias_variant(i), p, 0, 0)),
                   _row(128), _row(128)],
        out_shape=[jax.ShapeDtypeStruct((SEQ, 512), BF16)] * 4 + [jax.ShapeDtypeStruct((CTX, 512), F32)] * 2
        + [jax.ShapeDtypeStruct((3, HEADS, QBLK, KBLK), F32), jax.ShapeDtypeStruct((1, 128), F32),
           jax.ShapeDtypeStruct((1, 128), F32)],
        scratch_shapes=[pltpu.VMEM((SEQ, 128), F32), pltpu.VMEM((CTX, 128), F32),
                        pltpu.VMEM((SEQ, 128), F32), pltpu.VMEM((CTX, 128), F32), pltpu.VMEM((SEQ, 128), F32)],
        compiler_params=_cparams(("arbitrary", "arbitrary"), VMEM_BIG),
    )(z, z, z, z, zc, zc, bias, qg2, kg2, dcat)


def outproj(out_a, out_b, x, target, gate, wo):
    tl = 512

    def kern(a_ref, b_ref, x_ref, t_ref, g_ref, w_ref, loss_ref, dy_ref, dcat_ref, dg_ref, dw_ref):
        @pl.when(pl.program_id(0) == 0)
        def _():
            loss_ref[...] = jnp.zeros_like(loss_ref)
            dg_ref[...] = jnp.zeros_like(dg_ref)
            dw_ref[...] = jnp.zeros_like(dw_ref)

        a, b = a_ref[...].astype(BF16), b_ref[...].astype(BF16)
        mix = (jnp.dot(a, w_ref[0:512, :], preferred_element_type=F32)
               + jnp.dot(b, w_ref[512:1024, :], preferred_element_type=F32))
        err = x_ref[...] + g_ref[...] * mix - t_ref[...]
        loss_ref[...] += 0.5 * jnp.sum(jnp.mean(err * err, axis=-1))
        dy = err * (1.0 / DM)
        dy_ref[...] = dy
        dg_ref[...] += jnp.sum(dy * mix, axis=0, keepdims=True)
        dmix = (g_ref[...] * dy).astype(BF16)
        dcat_ref[...] = lax.dot_general(dmix, w_ref[...], (((1,), (1,)), ((), ())), preferred_element_type=F32)
        dw_ref[0:512, :] += lax.dot_general(a, dmix, (((0,), (0,)), ((), ())), preferred_element_type=F32)
        dw_ref[512:1024, :] += lax.dot_general(b, dmix, (((0,), (0,)), ((), ())), preferred_element_type=F32)

    tile = lambda w: pl.BlockSpec((tl, w), lambda t: (t, 0))
    whole = pl.BlockSpec((DM, DM), lambda t: (0, 0))
    return pl.pallas_call(
        kern, name="outproj", grid=(SEQ // tl,),
        in_specs=[tile(512), tile(512), tile(DM), tile(DM), _row(DM), whole],
        out_specs=[pl.BlockSpec((8, 128), lambda t: (0, 0)), tile(DM), tile(DM), _row(DM), whole],
        out_shape=[jax.ShapeDtypeStruct((8, 128), F32), jax.ShapeDtypeStruct((SEQ, DM), F32),
                   jax.ShapeDtypeStruct((SEQ, DM), F32), jax.ShapeDtypeStruct((1, DM), F32),
                   jax.ShapeDtypeStruct((DM, DM), F32)],
        compiler_params=_cparams(("arbitrary",), 48 * 1024 * 1024),
    )(out_a, out_b, x, target, gate, wo)


def _pieces(sources):
    out = []
    for name, c0, c1 in sources:
        for j in range(NCHIP):
            lo, hi = max(c0, j * SHARD_IN), min(c1, (j + 1) * SHARD_IN)
            if lo < hi:
                out.append((j, lo - j * SHARD_IN, hi - j * SHARD_IN, name, lo - c0, hi - c0))
    return out


DZ_PIECES = _pieces((("a", 0, 1536), ("q", 1536, 2048), ("k", 2048, 2560), ("v", 2560, 3072), ("g", 3072, DIN)))
DZC_PIECES = _pieces((("k", 2048, 2560), ("v", 2560, 3072)))
_NT = (((1,), (1,)), ((), ()))


def _dz_specs(tl):
    return [pl.BlockSpec((tl, 1536), lambda t: (t, 0))] + [pl.BlockSpec((tl, 512), lambda t: (t, 0))] * 4


def dh_bwd(dz_parts, w_full, x, dy, shift, scale, norm_g, dg_ctx):
    tl = 512

    def kern(a_ref, q_ref, k_ref, v_ref, g_ref, w_ref, x_ref, dy_ref, sh_ref, sc_ref, gn_ref, dgc_ref,
             gx_ref, dsh_ref, dsc_ref, dg_ref):
        src = dict(a=a_ref, q=q_ref, k=k_ref, v=v_ref, g=g_ref)
        dh = None
        for j, l0, l1, name, s0, s1 in DZ_PIECES:
            part = lax.dot_general(src[name][:, s0:s1], w_ref[j, :, l0:l1], _NT, preferred_element_type=F32)
            dh = part if dh is None else dh + part

        @pl.when(pl.program_id(0) == 0)
        def _():
            dsh_ref[...] = jnp.zeros_like(dsh_ref)
            dsc_ref[...] = jnp.zeros_like(dsc_ref)
            dg_ref[...] = dgc_ref[...]

        _, vjp = jax.vjp(_modulated, x_ref[...], gn_ref[...], sc_ref[...], sh_ref[...])
        dx, dg, dsc, dsh = vjp(dh)
        gx_ref[...] = dy_ref[...] + dx
        dg_ref[...] += dg
        dsc_ref[...] += dsc
        dsh_ref[...] += dsh

    tile = pl.BlockSpec((tl, DM), lambda t: (t, 0))
    return pl.pallas_call(
        kern, name="dh_bwd", grid=(SEQ // tl,),
        in_specs=_dz_specs(tl) + [pl.BlockSpec((NCHIP, DM, SHARD_IN), lambda t: (0, 0, 0)), tile, tile, _row(DM),
                                  _row(DM), _row(DM), _row(DM)],
        out_specs=[tile, _row(DM), _row(DM), _row(DM)],
        out_shape=[jax.ShapeDtypeStruct((SEQ, DM), F32)] + [jax.ShapeDtypeStruct((1, DM), F32)] * 3,
        compiler_params=_cparams(("arbitrary",), 48 * 1024 * 1024),
    )(*dz_parts, w_full, x, dy, shift, scale, norm_g, dg_ctx)


def dw_bwd(h, dz_parts, hc, dck, dcv):
    tl = 256

    def kern(h_ref, a_ref, q_ref, k_ref, v_ref, g_ref, hc_ref, dck_ref, dcv_ref, dw_ref):
        @pl.when(pl.program_id(0) == 0)
        def _():
            dw_ref[...] = jnp.zeros_like(dw_ref)
            hct = hc_ref[...].T
            csrc = dict(k=dck_ref, v=dcv_ref)
            for j, l0, l1, name, s0, s1 in DZC_PIECES:
                dw_ref[j, :, l0:l1] += jnp.dot(hct, csrc[name][:, s0:s1].astype(BF16), preferred_element_type=F32)

        ht = h_ref[...].T
        src = dict(a=a_ref, q=q_ref, k=k_ref, v=v_ref, g=g_ref)
        for j, l0, l1, name, s0, s1 in DZ_PIECES:
            dw_ref[j, :, l0:l1] += jnp.dot(ht, src[name][:, s0:s1], preferred_element_type=F32)

    whole = lambda r, c: pl.BlockSpec((r, c), lambda t: (0, 0))
    return pl.pallas_call(
        kern, name="dw_bwd", grid=(SEQ // tl,),
        in_specs=[pl.BlockSpec((tl, DM), lambda t: (t, 0))] + _dz_specs(tl) + [whole(CTX, DM), whole(CTX, 512),
                                                                              whole(CTX, 512)],
        out_specs=pl.BlockSpec((NCHIP, DM, SHARD_IN), lambda t: (0, 0, 0)),
        out_shape=jax.ShapeDtypeStruct((NCHIP, DM, SHARD_IN), F32),
        compiler_params=_cparams(("arbitrary",), VMEM_BIG),
    )(h, *dz_parts, hc, dck, dcv)


def ctx_bwd(dck, dcv, w_full, ctx, cshift, cscale, norm_g):
    def kern(dck_ref, dcv_ref, w_ref, c_ref, sh_ref, sc_ref, g_ref, dsh_ref, dsc_ref, dg_ref):
        csrc = dict(k=dck_ref, v=dcv_ref)
        dhc = None
        for j, l0, l1, name, s0, s1 in DZC_PIECES:
            part = lax.dot_general(csrc[name][:, s0:s1].astype(BF16), w_ref[j, :, l0:l1], _NT,
                                   preferred_element_type=F32)
            dhc = part if dhc is None else dhc + part
        _, vjp = jax.vjp(lambda g, sc, sh: _modulated(c_ref[...], g, sc, sh), g_ref[...], sc_ref[...], sh_ref[...])
        dg_ref[...], dsc_ref[...], dsh_ref[...] = vjp(dhc)

    whole = lambda r, c: pl.BlockSpec((r, c), lambda i: (0, 0))
    return pl.pallas_call(
        kern, name="ctx_bwd", grid=(1,),
        in_specs=[whole(CTX, 512), whole(CTX, 512), pl.BlockSpec((NCHIP, DM, SHARD_IN), lambda i: (0, 0, 0)),
                  whole(CTX, DM), _row(DM), _row(DM), _row(DM)],
        out_specs=[_row(DM), _row(DM), _row(DM)],
        out_shape=[jax.ShapeDtypeStruct((1, DM), F32)] * 3,
        compiler_params=_cparams(("arbitrary",), 40 * 1024 * 1024),
    )(dck, dcv, w_full, ctx, cshift, cscale, norm_g)


def _lane_pad_rpb(rpb):
    r = jnp.pad(rpb, ((0, 0), (0, 0), (0, GRID_W - rpb.shape[-1])))
    return jnp.concatenate([r, r], axis=-1)


def local_step(x, ctx, target, mod, cmod, norm_g, sgu_g, w_s, b_s, q_g, k_g, rpb, w_in_full, w_out_full):
    shift, scale, gate = mod[:, :DM], mod[:, DM:2 * DM], mod[:, 2 * DM:]
    cshift, cscale = cmod[:, :DM], cmod[:, DM:2 * DM]
    bsb = jnp.broadcast_to(b_s[:, :, None], (4, 128, 128))
    qg2, kg2 = jnp.tile(q_g, (1, 2)), jnp.tile(k_g, (1, 2))

    z, h = inproj_fwd(x, shift, scale, norm_g, w_in_full)
    zc, hc = ctx_fwd(ctx, cshift, cscale, norm_g, w_in_full)
    bias = rpb_tables(_lane_pad_rpb(rpb))
    out_a = sgu_fwd(z, sgu_g, w_s, bsb)
    out_b = attn_fwd(z, zc, bias, qg2, kg2)
    loss8, dy, dcat, dgate, dwo = outproj(out_a, out_b, x, target, gate, w_out_full)
    dz_a, dsg, dws, dbsb = sgu_bwd(z, sgu_g, w_s, bsb, dcat)
    dq, dk, dv, dbg, dck, dcv, dbias, dqg2, dkg2 = attn_bwd(z, zc, bias, qg2, kg2, dcat)
    drpb = rpb_bwd(dbias)[:, :, :rpb.shape[-1]]
    dz_parts = (dz_a, dq, dk, dv, dbg)
    dcshift, dcscale, dng_c = ctx_bwd(dck, dcv, w_in_full, ctx, cshift, cscale, norm_g)
    dw_in = dw_bwd(h, dz_parts, hc, dck, dcv)
    grad_x, dshift, dscale, dng = dh_bwd(dz_parts, w_in_full, x, dy, shift, scale, norm_g, dng_c)
    return dict(
        loss=loss8[0:1, 0:1], grad_x=grad_x, dw_in=dw_in, dw_out=dwo,
        dmod=jnp.concatenate([dshift, dscale, dgate], axis=-1),
        dcmod=jnp.concatenate([dcshift, dcscale, jnp.zeros((1, DM), F32)], axis=-1),
        d_norm_g=dng, d_sgu_g=dsg, d_w_s=dws, d_b_s=dbsb[:, :, 0],
        d_q_g=dqg2[:, :HDIM], d_k_g=dkg2[:, :HDIM], d_rpb=drpb)


def _me():
    return lax.axis_index("x"), lax.axis_index("y"), lax.axis_index("c")


def _flip(q):
    x, y, c = _me()
    return ((1 - x) if q & 4 else x, (1 - y) if q & 2 else y, (1 - c) if q & 1 else c)


def _chip_of(dev):
    return 2 * dev[0] + dev[1]


def _rcopy(src, dst, send_sems, recv_sems, k, dev):
    return pltpu.make_async_remote_copy(src_ref=src, dst_ref=dst, send_sem=send_sems.at[k], recv_sem=recv_sems.at[k],
                                        device_id=dev, device_id_type=MESH_ID)


_VMEM_SPEC = pl.BlockSpec(memory_space=pltpu.VMEM)
CS_ROWS = 8 * NDEV + 8


def gather_fwd(c, c_ctx, w_ada, b_shard, w_in, w_out):
    hin, hout = DM // 2, SHARD_OUT // 2
    n_w, n_c, n_m = 12, NDEV - 1, 3

    def kern(c_ref, cc_ref, wa_ref, b_ref, wi_ref, wo_ref, oi_ref, oo_ref, mod_ref, cs_ref, mine, send_sems, recv_sems):
        x, y, cc = _me()
        k, me = 2 * x + y, 4 * x + 2 * y + cc
        sib = _flip(1)

        def blocks(chip, half):
            return (oi_ref.at[chip, pl.ds(pl.multiple_of(half * hin, hin), hin), :],
                    oo_ref.at[chip, pl.ds(pl.multiple_of(half * hout, hout), hout), :])

        slot = lambda d: pl.ds(pl.multiple_of(8 * d, 8), 8)
        first = lax.broadcasted_iota(jnp.int32, (8, DM), 0) == 0
        mine[...] = jnp.where(first, jnp.broadcast_to(c_ref[...], (8, DM)), 0.0)
        cs_ref[slot(me), :] = mine[...]
        cs_ref[slot(NDEV), :] = jnp.where(first, jnp.broadcast_to(cc_ref[...], (8, DM)), 0.0)
        csends = [_rcopy(mine, cs_ref.at[slot(me), :], send_sems, recv_sems, n_w + q - 1, _flip(q))
                  for q in range(1, NDEV)]
        for cp in csends:
            cp.start()
        oi_ref[k] = wi_ref[...].astype(BF16)
        oo_ref[k] = wo_ref[...].astype(BF16)
        for q in range(1, NDEV):
            px, py, pc = _flip(q)
            _rcopy(mine, cs_ref.at[slot(4 * px + 2 * py + pc), :], send_sems, recv_sems, n_w + q - 1,
                   _flip(q)).wait_recv()
        act = jax.nn.silu(cs_ref[...]).astype(BF16)
        mod_ref[k] = jnp.dot(act, wa_ref[...].astype(BF16), preferred_element_type=F32) + b_ref[...]
        msends = [_rcopy(mod_ref.at[k], mod_ref.at[k], send_sems, recv_sems, n_w + n_c + q // 2 - 1, _flip(q))
                  for q in (2, 4, 6)]
        for cp in msends:
            cp.start()
        sends = []
        for q in (2, 4, 6):
            for n, blk in enumerate(blocks(k, cc)):
                sends.append(_rcopy(blk, blk, send_sems, recv_sems, 2 * (q // 2 - 1) + n, _flip(q)))
        for cp in sends:
            cp.start()
        for q in (2, 4, 6):
            kq = _chip_of(_flip(q))
            _rcopy(mod_ref.at[kq], mod_ref.at[kq], send_sems, recv_sems, n_w + n_c + q // 2 - 1, _flip(q)).wait_recv()

        passed = []
        for q in (2, 4, 6):
            for n, blk in enumerate(blocks(_chip_of(_flip(q)), cc)):
                _rcopy(blk, blk, send_sems, recv_sems, 2 * (q // 2 - 1) + n, _flip(q)).wait_recv()
                cp = _rcopy(blk, blk, send_sems, recv_sems, 6 + 2 * (q // 2 - 1) + n, sib)
                cp.start()
                passed.append(cp)
        for q in (2, 4, 6):
            for n, blk in enumerate(blocks(_chip_of(_flip(q)), 1 - cc)):
                _rcopy(blk, blk, send_sems, recv_sems, 6 + 2 * (q // 2 - 1) + n, sib).wait_recv()
        for cp in sends + csends + msends + passed:
            cp.wait_send()

    n_sem = n_w + n_c + n_m
    return pl.pallas_call(
        kern, name="gather_fwd", in_specs=[_VMEM_SPEC] * 6, out_specs=[_VMEM_SPEC] * 4,
        out_shape=[jax.ShapeDtypeStruct((NCHIP, DM, SHARD_IN), BF16), jax.ShapeDtypeStruct((NCHIP, SHARD_OUT, DM), BF16),
                   jax.ShapeDtypeStruct((NCHIP, CS_ROWS, SHARD_ADA), F32), jax.ShapeDtypeStruct((CS_ROWS, DM), F32)],
        scratch_shapes=[pltpu.VMEM((8, DM), F32), pltpu.SemaphoreType.DMA((n_sem,)), pltpu.SemaphoreType.DMA((n_sem,))],
        compiler_params=pltpu.CompilerParams(vmem_limit_bytes=48 * 1024 * 1024),
    )(c, c_ctx, w_ada, b_shard, w_in, w_out)


SLAB_ROWS = 80


def small_gather(slab):
    def kern(s_ref, all_ref, tot_ref, send_sems, recv_sems):
        x, y, c = _me()
        me = 4 * x + 2 * y + c
        all_ref[me] = s_ref[...]
        sends = [_rcopy(s_ref, all_ref.at[me], send_sems, recv_sems, q - 1, _flip(q)) for q in range(1, NDEV)]
        for cp in sends:
            cp.start()
        for q in range(1, NDEV):
            px, py, pc = _flip(q)
            d = 4 * px + 2 * py + pc
            _rcopy(s_ref, all_ref.at[d], send_sems, recv_sems, q - 1, _flip(q)).wait_recv()
        tot = all_ref[0]
        for d in range(1, NDEV):
            tot = tot + all_ref[d]
        tot_ref[...] = tot
        for cp in sends:
            cp.wait_send()

    return pl.pallas_call(
        kern, name="small_gather", in_specs=[_VMEM_SPEC], out_specs=[_VMEM_SPEC] * 2,
        out_shape=[jax.ShapeDtypeStruct((NDEV, SLAB_ROWS, DM), F32), jax.ShapeDtypeStruct((SLAB_ROWS, DM), F32)],
        scratch_shapes=[pltpu.SemaphoreType.DMA((NDEV - 1,)), pltpu.SemaphoreType.DMA((NDEV - 1,))],
    )(slab)


def ada_bwd(a_in, dm, dm_shard, w_ada, c_ctx):
    def kern(a_ref, dm_ref, dms_ref, w_ref, cc_ref, dw_ref, db_ref, dcc_ref, parts, send_sems, recv_sems):
        x, y, c = _me()
        k = 2 * x + y
        act = jax.nn.silu(a_ref[...]).astype(BF16)
        dms = dms_ref[...].astype(BF16)
        dw_ref[...] = lax.dot_general(act, dms, (((0,), (0,)), ((), ())), preferred_element_type=F32)
        db_ref[...] = jnp.sum(dm_ref[...], axis=0, keepdims=True)
        parts[k] = lax.dot_general(dms, w_ref[...].astype(BF16), (((1,), (1,)), ((), ())), preferred_element_type=F32)
        sends = [_rcopy(parts.at[k], parts.at[k], send_sems, recv_sems, q // 2 - 1, _flip(q)) for q in (2, 4, 6)]
        for cp in sends:
            cp.start()
        for q in (2, 4, 6):
            kq = _chip_of(_flip(q))
            _rcopy(parts.at[kq], parts.at[kq], send_sems, recv_sems, q // 2 - 1, _flip(q)).wait_recv()
        dact = ((parts[0] + parts[1]) + parts[2]) + parts[3]
        _, vjp = jax.vjp(jax.nn.silu, cc_ref[...])
        dcc_ref[...] = vjp(dact[8:9, :])[0]
        for cp in sends:
            cp.wait_send()

    return pl.pallas_call(
        kern, name="ada_bwd", in_specs=[_VMEM_SPEC] * 5, out_specs=[_VMEM_SPEC] * 3,
        out_shape=[jax.ShapeDtypeStruct((DM, SHARD_ADA), F32), jax.ShapeDtypeStruct((1, 3 * DM), F32),
                   jax.ShapeDtypeStruct((1, DM), F32)],
        scratch_shapes=[pltpu.VMEM((NCHIP, 16, DM), F32), pltpu.SemaphoreType.DMA((3,)), pltpu.SemaphoreType.DMA((3,))],
    )(a_in, dm, dm_shard, w_ada, c_ctx)


def reduce_scatter(g, name):
    _, rows, width = g.shape
    rh = rows // 2

    def kern(g_hbm, out_ref, mine, rcv1, wire, rcv2, load_sem, send_sems, recv_sems):
        x, y, c = _me()
        k = 2 * x + y
        sib = _flip(1)
        half = lambda h: pl.ds(pl.multiple_of(h * rh, rh), rh)
        load = pltpu.make_async_copy(g_hbm.at[:, half(c), :], mine, load_sem)
        load.start()
        pair = _rcopy(g_hbm.at[:, half(1 - c), :], rcv1, send_sems, recv_sems, 0, sib)
        pair.start()
        load.wait()
        pair.wait_recv()
        for j in range(NCHIP):
            pair_sum = mine[j] + rcv1[j]
            mine[j] = pair_sum
            wire[j] = pair_sum.astype(BF16)
        sends = [_rcopy(wire.at[_chip_of(_flip(q))], rcv2.at[q // 2 - 1], send_sems, recv_sems, q // 2, _flip(q))
                 for q in (2, 4, 6)]
        for cp in sends:
            cp.start()
        for q in (2, 4, 6):
            _rcopy(wire.at[0], rcv2.at[q // 2 - 1], send_sems, recv_sems, q // 2, _flip(q)).wait_recv()
        out_ref[half(c), :] = ((mine[k] + rcv2[0].astype(F32)) + rcv2[1].astype(F32)) + rcv2[2].astype(F32)
        share = _rcopy(out_ref.at[half(c), :], out_ref.at[half(c), :], send_sems, recv_sems, 4, sib)
        share.start()
        _rcopy(out_ref.at[half(1 - c), :], out_ref.at[half(1 - c), :], send_sems, recv_sems, 4, sib).wait_recv()
        for cp in [pair, share] + sends:
            cp.wait_send()

    return pl.pallas_call(
        kern, name=name, in_specs=[pl.BlockSpec(memory_space=pl.ANY)], out_specs=_VMEM_SPEC,
        out_shape=jax.ShapeDtypeStruct((rows, width), F32),
        scratch_shapes=[pltpu.VMEM((NCHIP, rh, width), F32), pltpu.VMEM((NCHIP, rh, width), F32),
                        pltpu.VMEM((NCHIP, rh, width), BF16), pltpu.VMEM((NCHIP - 1, rh, width), BF16),
                        pltpu.SemaphoreType.DMA(()),
                        pltpu.SemaphoreType.DMA((5,)), pltpu.SemaphoreType.DMA((5,))],
        compiler_params=pltpu.CompilerParams(vmem_limit_bytes=40 * 1024 * 1024),
    )(g)


def _adamw_math(w, g, m, v):
    m = B1 * m + (1.0 - B1) * g
    v = B2 * v + (1.0 - B2) * (g * g)
    m_hat = m / (1.0 - B1 ** STEP)
    v_hat = v / (1.0 - B2 ** STEP)
    return -LR * (m_hat / (jnp.sqrt(v_hat) + ADAM_EPS) + WD * w), m, v


def adamw_big(w, g, m, v, name, block_rows=256):
    rows, width = w.shape

    def kern(w_ref, g_ref, m_ref, v_ref, d_ref, nm_ref, nv_ref):
        d_ref[...], nm_ref[...], nv_ref[...] = _adamw_math(w_ref[...], g_ref[...], m_ref[...], v_ref[...])

    spec = pl.BlockSpec((block_rows, width), lambda i: (i, 0))
    return pl.pallas_call(
        kern, name=name, grid=(rows // block_rows,), in_specs=[spec] * 4, out_specs=[spec] * 3,
        out_shape=[jax.ShapeDtypeStruct((rows, width), F32)] * 3,
        compiler_params=_cparams(("arbitrary",)),
    )(w, g, m, v)


def adamw_small(quads):
    n = len(quads)

    def kern(*refs):
        ins, outs = refs[:4 * n], refs[4 * n:]
        for i in range(n):
            w, g, m, v = (r[...] for r in ins[4 * i:4 * i + 4])
            outs[3 * i][...], outs[3 * i + 1][...], outs[3 * i + 2][...] = _adamw_math(w, g, m, v)

    flat = [a for quad in quads for a in quad]
    res = pl.pallas_call(
        kern, name="adamw_small", in_specs=[_VMEM_SPEC] * (4 * n), out_specs=[_VMEM_SPEC] * (3 * n),
        out_shape=[jax.ShapeDtypeStruct(q[0].shape, F32) for q in quads for _ in range(3)],
    )(*flat)
    return [tuple(res[3 * i:3 * i + 3]) for i in range(n)]


def _rows_of(a, rows):
    flat = a.reshape(-1)
    return jnp.pad(flat, (0, rows * DM - flat.shape[0])).reshape(rows, DM)


def kernel(x, c, ctx, c_ctx, w_ada, b_ada, norm_g, w_in, sgu_norm_g, w_spatial, b_spatial, q_norm_g, k_norm_g, rpb, w_out, loss_target, m_c_ctx, m_w_ada, m_b_ada, m_norm_g, m_w_in, m_sgu_norm_g, m_w_spatial, m_b_spatial, m_q_norm_g, m_k_norm_g, m_rpb, m_w_out, v_c_ctx, v_w_ada, v_b_ada, v_norm_g, v_w_in, v_sgu_norm_g, v_w_spatial, v_b_spatial, v_q_norm_g, v_k_norm_g, v_rpb, v_w_out):
    xi, yi, ci = lax.axis_index("x"), lax.axis_index("y"), lax.axis_index("c")
    chip, dev = 2 * xi + yi, 4 * xi + 2 * yi + ci
    c_ctx2 = c_ctx.reshape(1, DM)

    b_shard = lax.dynamic_slice(b_ada, (0, chip * SHARD_ADA), (1, SHARD_ADA))
    w_in_full, w_out_full, mod_all, cs = gather_fwd(c, c_ctx2, w_ada[0], b_shard, w_in[0], w_out[0])
    mods = mod_all.transpose(1, 0, 2).reshape(CS_ROWS, 3 * DM)
    mod = lax.dynamic_slice(mods, (8 * dev, 0), (1, 3 * DM))
    cmod = mods[8 * NDEV:8 * NDEV + 1]

    part = local_step(x[0], ctx[0], loss_target[0], mod, cmod, norm_g, sgu_norm_g, w_spatial[0], b_spatial[0],
                      q_norm_g, k_norm_g, rpb[0], w_in_full, w_out_full.reshape(DM, DM))

    slab = jnp.concatenate([
        part["d_norm_g"], _rows_of(part["d_sgu_g"], 1), _rows_of(part["d_b_s"], 1),
        _rows_of(jnp.concatenate([part["d_q_g"], part["d_k_g"]], axis=-1), 1), _rows_of(part["d_rpb"], 4),
        _rows_of(part["loss"], 1), _rows_of(part["dcmod"], 3), _rows_of(part["dmod"], 3), jnp.zeros((1, DM), F32),
        _rows_of(part["d_w_s"], 64)], axis=0)
    gathered, tot = small_gather(slab)
    dm = jnp.concatenate([gathered[:, 12:15, :].reshape(NDEV, 3 * DM), tot[9:12].reshape(1, 3 * DM),
                          jnp.zeros((7, 3 * DM), F32)], axis=0)
    a_in = jnp.concatenate([cs[0:8 * NDEV:8], cs[8 * NDEV:8 * NDEV + 1], jnp.zeros((7, DM), F32)], axis=0)
    dm_shard = lax.dynamic_slice(dm, (0, chip * SHARD_ADA), (16, SHARD_ADA))
    g_w_ada, g_b_ada, g_c_ctx = ada_bwd(a_in, dm, dm_shard, w_ada[0], c_ctx2)
    g_w_in = reduce_scatter(part["dw_in"], "rs_w_in")
    g_w_out = reduce_scatter(part["dw_out"].reshape(NCHIP, SHARD_OUT, DM), "rs_w_out")

    loss = tot[8, 0]
    g_small = dict(
        c_ctx=g_c_ctx, b_ada=g_b_ada, norm_g=tot[0:1], sgu_norm_g=tot[1:2, :512], w_spatial=tot[16:80].reshape(512, 128),
        b_spatial=tot[2:3, :512].reshape(4, 128), q_norm_g=tot[3:4, :HDIM], k_norm_g=tot[3:4, HDIM:2 * HDIM],
        rpb=tot[4:8].reshape(-1)[:HEADS * 15 * 31].reshape(HEADS * 15, 31))
    shapes = dict(c_ctx=(DM,), w_ada=(1, DM, SHARD_ADA), b_ada=(1, 3 * DM), norm_g=(1, DM), w_in=(1, DM, SHARD_IN),
                  sgu_norm_g=(1, 512), w_spatial=(1, 4, 128, 128), b_spatial=(1, 4, 128), q_norm_g=(1, HDIM),
                  k_norm_g=(1, HDIM), rpb=(1, HEADS, 15, 31), w_out=(1, SHARD_OUT, DM))
    names = list(shapes)
    weights = dict(c_ctx=c_ctx, w_ada=w_ada, b_ada=b_ada, norm_g=norm_g, w_in=w_in, sgu_norm_g=sgu_norm_g,
                   w_spatial=w_spatial, b_spatial=b_spatial, q_norm_g=q_norm_g, k_norm_g=k_norm_g, rpb=rpb, w_out=w_out)
    m_in = dict(zip(names, (m_c_ctx, m_w_ada, m_b_ada, m_norm_g, m_w_in, m_sgu_norm_g, m_w_spatial, m_b_spatial,
                            m_q_norm_g, m_k_norm_g, m_rpb, m_w_out)))
    v_in = dict(zip(names, (v_c_ctx, v_w_ada, v_b_ada, v_norm_g, v_w_in, v_sgu_norm_g, v_w_spatial, v_b_spatial,
                            v_q_norm_g, v_k_norm_g, v_rpb, v_w_out)))
    grads = dict(g_small, w_ada=g_w_ada, w_in=g_w_in, w_out=g_w_out)
    upd = {}
    for n in ("w_ada", "w_in", "w_out"):
        g = grads[n]
        upd[n] = adamw_big(weights[n].reshape(g.shape), g, m_in[n].reshape(g.shape), v_in[n].reshape(g.shape),
                           "adamw_" + n)
    small = [n for n in names if n not in upd]
    res = adamw_small([(weights[n].reshape(grads[n].shape), grads[n], m_in[n].reshape(grads[n].shape),
                        v_in[n].reshape(grads[n].shape)) for n in small])
    upd.update(zip(small, res))
    out = [loss, part["grad_x"].reshape(1, SEQ, DM)]
    out += [grads[n].reshape(shapes[n]) for n in names]
    for slot in range(3):
        out += [upd[n][slot].reshape(shapes[n]) for n in names]
    return tuple(out)
```

```python
import functools

import jax
import jax.numpy as jnp
from jax import lax
from jax.experimental import pallas as pl
from jax.experimental.pallas import tpu as pltpu

F32, BF16 = jnp.float32, jnp.bfloat16
SEQ, DM, CTX, DIN = 4096, 1024, 256, 3584
NCHIP, NDEV = 4, 8
SHARD_IN = DIN // NCHIP
SHARD_ADA = 3 * DM // NCHIP
SHARD_OUT = DM // NCHIP
GRID_W = 64
QROWS = 4
KROWS = 12
QBLK, KBLK = QROWS * GRID_W, KROWS * GRID_W
NQBLK = SEQ // QBLK
HEADS, HDIM, NPAIR = 8, 64, 4
EPS = 1e-6
NEG_INF = -1e30
ZQ, ZK, ZV, ZG = 12, 16, 20, 24
LR, B1, B2, ADAM_EPS, WD, STEP = 0.001, 0.9, 0.999, 1e-08, 0.01, 10
VMEM_BIG = 56 * 1024 * 1024
MESH_ID = pl.DeviceIdType.MESH


def _dot(a, b, lhs_c, rhs_c):
    return lax.dot_general(a.astype(BF16), b.astype(BF16), (((lhs_c,), (rhs_c,)), ((), ())),
                           preferred_element_type=F32)


@jax.custom_vjp
def mm(a, b):
    return _dot(a, b, 1, 0)


@jax.custom_vjp
def mm_nt(a, b):
    return _dot(a, b, 1, 1)


@jax.custom_vjp
def mm_tn(a, b):
    return _dot(a, b, 0, 0)


mm.defvjp(lambda a, b: (mm(a, b), (a, b)), lambda r, ct: (mm_nt(ct, r[1]), mm_tn(r[0], ct)))
mm_nt.defvjp(lambda a, b: (mm_nt(a, b), (a, b)), lambda r, ct: (mm(ct, r[1]), mm_tn(ct, r[0])))
mm_tn.defvjp(lambda a, b: (mm_tn(a, b), (a, b)), lambda r, ct: (mm_nt(r[1], ct), mm(r[0], ct)))


def _rms(x, g):
    return x * lax.rsqrt(jnp.mean(x * x, axis=-1, keepdims=True) + EPS) * g


def _modulated(x, g, scale, shift):
    return _rms(x, g) * (1.0 + scale) + shift


def _pair_rms(x, g2):
    lo = lax.broadcasted_iota(jnp.int32, (1, 2 * HDIM), 1) < HDIM
    sq = x * x
    s_lo = jnp.sum(jnp.where(lo, sq, 0.0), axis=-1, keepdims=True)
    s_hi = jnp.sum(jnp.where(lo, 0.0, sq), axis=-1, keepdims=True)
    rs = jnp.where(lo, lax.rsqrt(s_lo / HDIM + EPS), lax.rsqrt(s_hi / HDIM + EPS))
    return x * rs * g2


def _cparams(sem, vmem=None):
    return pltpu.CompilerParams(dimension_semantics=sem, vmem_limit_bytes=vmem)


def _row(n):
    return pl.BlockSpec((1, n), lambda *_: (0, 0))


def inproj_fwd(chip, x, shift, scale, norm_g, w_shard):
    tl = 512
    nt = SEQ // tl
    hin = DM // 2

    def kern(k_ref, x_ref, sh_ref, sc_ref, g_ref, w_ref, z_ref, h_ref, wfull_ref, w_scr, h_scr, send_sems, recv_sems):
        s, t = pl.program_id(0), pl.program_id(1)
        xi, yi, c = _me()
        k = 2 * xi + yi
        sib = _flip(1)
        half = lambda hh: pl.ds(pl.multiple_of(hh * hin, hin), hin)
        rows = pl.ds(pl.multiple_of(t * tl, tl), tl)

        def ici(q, chip_of_block):
            blk = w_scr.at[chip_of_block, half(c), :]
            return _rcopy(blk, blk, send_sems, recv_sems, q // 2 - 1, _flip(q))

        def d2d(q, chip_of_block, hh):
            blk = w_scr.at[chip_of_block, half(hh), :]
            return _rcopy(blk, blk, send_sems, recv_sems, 3 + q // 2 - 1, sib)

        @pl.when((s == 0) & (t == 0))
        def _():
            w_scr[k] = w_ref[...].astype(BF16)
            for q in (2, 4, 6):
                ici(q, k).start()

        for sweep in (1, 2, 3):
            @pl.when((s == sweep) & (t == 0))
            def _():
                q = 2 * sweep
                src = _chip_of(_flip(q))
                ici(q, src).wait_recv()
                d2d(q, src, c).start()
                d2d(q, src, 1 - c).wait_recv()

        @pl.when(s == 0)
        def _():
            hb = _modulated(x_ref[...], g_ref[...], sc_ref[...], sh_ref[...]).astype(BF16)
            h_scr[rows, :] = hb
            h_ref[...] = hb

        z_ref[...] = jnp.dot(h_scr[rows, :], w_scr[lax.bitwise_xor(k, s)], preferred_element_type=F32)

        @pl.when((s == NCHIP - 1) & (t == nt - 1))
        def _():
            for q in (2, 4, 6):
                ici(q, k).wait_send()
                d2d(q, _chip_of(_flip(q)), c).wait_send()
            pltpu.sync_copy(w_scr, wfull_ref)

    once = lambda s, t, k: (jnp.where(s == 0, t, nt - 1), 0)
    row = lambda n: pl.BlockSpec((1, n), lambda s, t, k: (0, 0))
    return pl.pallas_call(
        kern, name="inproj_fwd",
        grid_spec=pltpu.PrefetchScalarGridSpec(
            num_scalar_prefetch=1, grid=(NCHIP, nt),
            in_specs=[pl.BlockSpec((tl, DM), once), row(DM), row(DM), row(DM), _VMEM_SPEC],
            out_specs=[pl.BlockSpec((tl, SHARD_IN), lambda s, t, k: (t, lax.bitwise_xor(k[0], s))),
                       pl.BlockSpec((tl, DM), once), pl.BlockSpec(memory_space=pl.ANY)],
            scratch_shapes=[pltpu.VMEM((NCHIP, DM, SHARD_IN), BF16), pltpu.VMEM((SEQ, DM), BF16),
                            pltpu.SemaphoreType.DMA((6,)), pltpu.SemaphoreType.DMA((6,))]),
        out_shape=[jax.ShapeDtypeStruct((SEQ, DIN), F32), jax.ShapeDtypeStruct((SEQ, DM), BF16),
                   jax.ShapeDtypeStruct((NCHIP, DM, SHARD_IN), BF16)],
        compiler_params=_cparams(("arbitrary", "arbitrary"), 48 * 1024 * 1024),
    )(chip, x, shift, scale, norm_g, w_shard)


def ctx_fwd(ctx, cshift, cscale, norm_g, w_full):
    def kern(c_ref, sh_ref, sc_ref, g_ref, w2_ref, w3_ref, zc_ref, hc_ref):
        hc = _modulated(c_ref[...], g_ref[...], sc_ref[...], sh_ref[...]).astype(BF16)
        hc_ref[...] = hc
        zc_ref[:, :SHARD_IN] = jnp.dot(hc, w2_ref[0], preferred_element_type=F32)
        zc_ref[:, SHARD_IN:] = jnp.dot(hc, w3_ref[0], preferred_element_type=F32)

    return pl.pallas_call(
        kern, name="ctx_fwd", grid=(1,),
        in_specs=[pl.BlockSpec((CTX, DM), lambda i: (0, 0)), _row(DM), _row(DM), _row(DM),
                  pl.BlockSpec((1, DM, SHARD_IN), lambda i: (2, 0, 0)),
                  pl.BlockSpec((1, DM, SHARD_IN), lambda i: (3, 0, 0))],
        out_specs=[pl.BlockSpec((CTX, 2 * SHARD_IN), lambda i: (0, 0)),
                   pl.BlockSpec((CTX, DM), lambda i: (0, 0))],
        out_shape=[jax.ShapeDtypeStruct((CTX, 2 * SHARD_IN), F32), jax.ShapeDtypeStruct((CTX, DM), BF16)],
        compiler_params=_cparams(("arbitrary",)),
    )(ctx, cshift, cscale, norm_g, w_full, w_full)


SGU_CHUNK, SGU_PER_STEP = 128, 4


def _gelu(x):
    return 0.5 * x * (1.0 + lax.erf(x * 0.7071067811865476))


def _sgu_chunk(au, av, ag, sg, ws, bsb):
    u, v = _gelu(au), _gelu(av)
    outs = []
    for g in range(4):
        sl = slice(128 * g, 128 * (g + 1))
        mixed = mm(ws[g], _rms(v[:, sl], sg[:, sl])) + bsb[g]
        outs.append(u[:, sl] * mixed * jax.nn.silu(ag[:, sl]))
    return jnp.concatenate(outs, axis=-1)


def _sgu_specs():
    rows = SGU_CHUNK * SGU_PER_STEP
    zspec = lambda c: pl.BlockSpec((rows, 512), lambda n: (n, c))
    wspec = pl.BlockSpec((4, 128, 128), lambda n: (0, 0, 0))
    return rows, [zspec(0), zspec(1), zspec(2), _row(512), wspec, wspec]


def sgu_fwd(z, sg, ws, bsb):
    rows, in_specs = _sgu_specs()

    def kern(au_ref, av_ref, ag_ref, sg_ref, ws_ref, bs_ref, o_ref):
        for c in range(SGU_PER_STEP):
            sl = slice(c * SGU_CHUNK, (c + 1) * SGU_CHUNK)
            o_ref[sl, :] = _sgu_chunk(au_ref[sl, :], av_ref[sl, :], ag_ref[sl, :], sg_ref[...], ws_ref[...],
                                      bs_ref[...])

    return pl.pallas_call(
        kern, name="sgu_fwd", grid=(SEQ // rows,), in_specs=in_specs,
        out_specs=pl.BlockSpec((rows, 512), lambda n: (n, 0)),
        out_shape=jax.ShapeDtypeStruct((SEQ, 512), F32),
        compiler_params=_cparams(("arbitrary",)),
    )(z, z, z, sg, ws, bsb)


def sgu_bwd(z, sg, ws, bsb, dcat):
    rows, in_specs = _sgu_specs()

    def kern(au_ref, av_ref, ag_ref, sg_ref, ws_ref, bs_ref, do_ref, dz_ref, dsg_ref, dws_ref, dbs_ref):
        @pl.when(pl.program_id(0) == 0)
        def _():
            dsg_ref[...] = jnp.zeros_like(dsg_ref)
            dws_ref[...] = jnp.zeros_like(dws_ref)
            dbs_ref[...] = jnp.zeros_like(dbs_ref)

        for c in range(SGU_PER_STEP):
            sl = slice(c * SGU_CHUNK, (c + 1) * SGU_CHUNK)
            _, vjp = jax.vjp(_sgu_chunk, au_ref[sl, :], av_ref[sl, :], ag_ref[sl, :], sg_ref[...], ws_ref[...],
                             bs_ref[...])
            dau, dav, dag, dsg, dws, dbs = vjp(do_ref[sl, :])
            dz_ref[sl, 0:512] = dau.astype(BF16)
            dz_ref[sl, 512:1024] = dav.astype(BF16)
            dz_ref[sl, 1024:1536] = dag.astype(BF16)
            dsg_ref[...] += dsg
            dws_ref[...] += dws
            dbs_ref[...] += dbs

        @pl.when(pl.program_id(0) == pl.num_programs(0) - 1)
        def _():
            dbs_ref[...] = jnp.broadcast_to(jnp.sum(dbs_ref[...], axis=-1, keepdims=True), dbs_ref.shape)

    wspec = pl.BlockSpec((4, 128, 128), lambda n: (0, 0, 0))
    return pl.pallas_call(
        kern, name="sgu_bwd", grid=(SEQ // rows,),
        in_specs=in_specs + [pl.BlockSpec((rows, 512), lambda n: (n, 0))],
        out_specs=[pl.BlockSpec((rows, 1536), lambda n: (n, 0)), _row(512), wspec, wspec],
        out_shape=[jax.ShapeDtypeStruct((SEQ, 1536), BF16), jax.ShapeDtypeStruct((1, 512), F32),
                   jax.ShapeDtypeStruct((4, 128, 128), F32), jax.ShapeDtypeStruct((4, 128, 128), F32)],
        compiler_params=_cparams(("arbitrary",)),
    )(z, z, z, sg, ws, bsb, dcat)


_DR_OFF = (7, 3, -1)


def _row_valid(v, rr, j):
    return (j < 8, rr <= j < rr + 8, 4 <= j < 12)[v]


def _col_window():
    q = lax.broadcasted_iota(jnp.int32, (GRID_W, 128), 0)
    kc = lax.broadcasted_iota(jnp.int32, (GRID_W, 128), 1) % GRID_W
    c0 = jnp.clip(q - 8, 0, GRID_W - 16)
    return (kc >= c0) & (kc < c0 + 16)


def rpb_tables(rpb2):
    def kern(r_ref, b_ref):
        base = r_ref[0]
        lo = lax.broadcasted_iota(jnp.int32, (1, 128), 1) < GRID_W
        win = _col_window()
        neg = jnp.full((GRID_W, 128), NEG_INF, F32)
        for v in range(3):
            for rr in range(QROWS):
                for jp in range(KROWS // 2):
                    j0, j1 = 2 * jp, 2 * jp + 1
                    ok0, ok1 = _row_valid(v, rr, j0), _row_valid(v, rr, j1)
                    if not (ok0 or ok1):
                        tile = neg
                    else:
                        d0 = j0 - rr + _DR_OFF[v]
                        r0 = base[d0:d0 + 1, :] if ok0 else jnp.zeros((1, 128), F32)
                        r1 = base[d0 + 1:d0 + 2, :] if ok1 else jnp.zeros((1, 128), F32)
                        y = jnp.broadcast_to(jnp.where(lo, r0, r1), (GRID_W, 128))
                        y = pltpu.roll(pltpu.roll(y, 128 - 15, 1), 0, 1, stride=1, stride_axis=0)
                        ok = win & jnp.where(lo, ok0, ok1)
                        tile = jnp.where(ok, y, NEG_INF)
                    b_ref[v, 0, rr * GRID_W:(rr + 1) * GRID_W, jp * 128:(jp + 1) * 128] = tile

    return pl.pallas_call(
        kern, name="rpb_tables", grid=(HEADS,),
        in_specs=[pl.BlockSpec((1, 15, 128), lambda h: (h, 0, 0))],
        out_specs=pl.BlockSpec((3, 1, QBLK, KBLK), lambda h: (0, h, 0, 0)),
        out_shape=jax.ShapeDtypeStruct((3, HEADS, QBLK, KBLK), F32),
        compiler_params=_cparams(("arbitrary",)),
    )(rpb2)


def rpb_bwd(dbias):
    def kern(g_ref, o_ref):
        lo = lax.broadcasted_iota(jnp.int32, (1, 128), 1) < GRID_W
        ri = lax.broadcasted_iota(jnp.int32, (GRID_W, GRID_W), 0)
        ci = lax.broadcasted_iota(jnp.int32, (GRID_W, GRID_W), 1)
        flip = (ri + ci == GRID_W - 1).astype(F32)
        acc = [jnp.zeros((1, 128), F32) for _ in range(15)]
        for v in range(3):
            for rr in range(QROWS):
                for jp in range(KROWS // 2):
                    j0, j1 = 2 * jp, 2 * jp + 1
                    ok0, ok1 = _row_valid(v, rr, j0), _row_valid(v, rr, j1)
                    if not (ok0 or ok1):
                        continue
                    g = g_ref[v, 0, rr * GRID_W:(rr + 1) * GRID_W, jp * 128:(jp + 1) * 128]
                    g = lax.dot_general(flip, g, (((1,), (0,)), ((), ())), precision=lax.Precision.HIGHEST,
                                        preferred_element_type=F32)
                    g = pltpu.roll(pltpu.roll(g, 128 - 48, 1), 0, 1, stride=1, stride_axis=0)
                    s = jnp.sum(g, axis=0, keepdims=True)
                    d0 = j0 - rr + _DR_OFF[v]
                    if ok0:
                        acc[d0] = acc[d0] + jnp.where(lo, s, 0.0)
                    if ok1:
                        acc[d0 + 1] = acc[d0 + 1] + jnp.where(lo, 0.0, s)
        for d in range(15):
            o_ref[0, d:d + 1, :] = acc[d] + pltpu.roll(acc[d], GRID_W, 1)

    return pl.pallas_call(
        kern, name="rpb_bwd", grid=(HEADS,),
        in_specs=[pl.BlockSpec((3, 1, QBLK, KBLK), lambda h: (0, h, 0, 0))],
        out_specs=pl.BlockSpec((1, 15, 128), lambda h: (h, 0, 0)),
        out_shape=jax.ShapeDtypeStruct((HEADS, 15, 128), F32),
        compiler_params=_cparams(("arbitrary",)),
    )(dbias)


def _attn_step(q_raw, kn, v, ckn, cv, bias2, qg):
    qn = _pair_rms(q_raw, qg) * (HDIM ** -0.5)
    lo = lax.broadcasted_iota(jnp.int32, (1, 2 * HDIM), 1) < HDIM
    out = None
    for a in range(2):
        mine = lo if a == 0 else jnp.logical_not(lo)
        qa = jnp.where(mine, qn, 0.0)
        s_lat = mm_nt(qa, kn) + bias2[a]
        s_ctx = mm_nt(qa, ckn)
        m = lax.stop_gradient(jnp.maximum(jnp.max(s_lat, axis=-1, keepdims=True),
                                          jnp.max(s_ctx, axis=-1, keepdims=True)))
        p_lat = jnp.exp(s_lat - m)
        p_ctx = jnp.exp(s_ctx - m)
        den = jnp.sum(p_lat, axis=-1, keepdims=True) + jnp.sum(p_ctx, axis=-1, keepdims=True)
        o = jnp.where(mine, (mm(p_lat, v) + mm(p_ctx, cv)) / den, 0.0)
        out = o if out is None else out + o
    return out


def _attn_gated(q_raw, kn, v, ckn, cv, bias2, qg, bg):
    return _attn_step(q_raw, kn, v, ckn, cv, bias2, qg) * jax.nn.silu(bg)


def _kstart(i):
    return pl.multiple_of(jnp.clip((i - 1) * QBLK, 0, SEQ - KBLK), QBLK)


def _bias_variant(i):
    return jnp.where(i == 0, 0, jnp.where(i == NQBLK - 1, 2, 1))


def _attn_in_specs():
    return [
        pl.BlockSpec((QBLK, 128), lambda p, i: (i, ZQ + p)),
        pl.BlockSpec((SEQ, 128), lambda p, i: (0, ZK + p)),
        pl.BlockSpec((SEQ, 128), lambda p, i: (0, ZV + p)),
        pl.BlockSpec((QBLK, 128), lambda p, i: (i, ZG + p)),
        pl.BlockSpec((CTX, 128), lambda p, i: (0, 2 + p)),
        pl.BlockSpec((CTX, 128), lambda p, i: (0, 6 + p)),
        pl.BlockSpec((1, 2, QBLK, KBLK), lambda p, i: (_bias_variant(i), p, 0, 0)),
        _row(128), _row(128),
    ]


NORM_ROWS = 512


def _norm_keys(k_ref, ck_ref, kg_ref, kn_scr, ckn_scr):
    def body(c, carry):
        sl = pl.ds(pl.multiple_of(c * NORM_ROWS, NORM_ROWS), NORM_ROWS)
        kn_scr[sl, :] = _pair_rms(k_ref[sl, :], kg_ref[...])
        return carry

    lax.fori_loop(0, SEQ // NORM_ROWS, body, 0)
    ckn_scr[...] = _pair_rms(ck_ref[...], kg_ref[...])


def attn_fwd(z, zc, bias, qg2, kg2):
    def kern(q_ref, k_ref, v_ref, bg_ref, ck_ref, cv_ref, b_ref, qg_ref, kg_ref, o_ref, kn_scr, ckn_scr):
        i = pl.program_id(1)

        @pl.when(i == 0)
        def _():
            _norm_keys(k_ref, ck_ref, kg_ref, kn_scr, ckn_scr)

        ks = pl.ds(_kstart(i), KBLK)
        o_ref[...] = _attn_gated(q_ref[...], kn_scr[ks, :], v_ref[ks, :], ckn_scr[...], cv_ref[...], b_ref[0],
                                 qg_ref[...], bg_ref[...])

    return pl.pallas_call(
        kern, name="attn_fwd", grid=(NPAIR, NQBLK), in_specs=_attn_in_specs(),
        out_specs=pl.BlockSpec((QBLK, 128), lambda p, i: (i, p)),
        out_shape=jax.ShapeDtypeStruct((SEQ, 512), F32),
        scratch_shapes=[pltpu.VMEM((SEQ, 128), F32), pltpu.VMEM((CTX, 128), F32)],
        compiler_params=_cparams(("arbitrary", "arbitrary"), 40 * 1024 * 1024),
    )(z, z, z, z, zc, zc, bias, qg2, kg2)


def attn_bwd(z, zc, bias, qg2, kg2, dcat):
    def kern(q_ref, k_ref, v_ref, bg_ref, ck_ref, cv_ref, b_ref, qg_ref, kg_ref, do_ref,
             dq_ref, dk_ref, dv_ref, dbg_ref, dck_ref, dcv_ref, db_ref, dqg_ref, dkg_ref,
             kn_scr, ckn_scr, dkn_scr, dckn_scr, dv_scr):
        p, i = pl.program_id(0), pl.program_id(1)
        last = i == NQBLK - 1

        @pl.when(i == 0)
        def _():
            _norm_keys(k_ref, ck_ref, kg_ref, kn_scr, ckn_scr)
            dkn_scr[...] = jnp.zeros_like(dkn_scr)
            dv_scr[...] = jnp.zeros_like(dv_scr)
            dckn_scr[...] = jnp.zeros_like(dckn_scr)
            dcv_ref[...] = jnp.zeros_like(dcv_ref)

        @pl.when((i == 0) & (p == 0))
        def _():
            dqg_ref[...] = jnp.zeros_like(dqg_ref)
            dkg_ref[...] = jnp.zeros_like(dkg_ref)

        ks = pl.ds(_kstart(i), KBLK)
        _, vjp = jax.vjp(_attn_gated, q_ref[...], kn_scr[ks, :], v_ref[ks, :], ckn_scr[...], cv_ref[...], b_ref[0],
                         qg_ref[...], bg_ref[...])
        dq, dkn, dv, dckn, dcv, db, dqg, dbg = vjp(do_ref[...])
        dq_ref[...] = dq.astype(BF16)
        dbg_ref[...] = dbg.astype(BF16)
        dkn_scr[ks, :] += dkn
        dv_scr[ks, :] += dv
        dckn_scr[...] += dckn
        dcv_ref[...] += dcv
        dqg_ref[...] += dqg
        fresh = (i == 0) | (i == 1) | last

        @pl.when(fresh)
        def _():
            db_ref[0] = db

        @pl.when(jnp.logical_not(fresh))
        def _():
            db_ref[0] += db

        @pl.when(last)
        def _():
            def body(c, dkg):
                sl = pl.ds(pl.multiple_of(c * NORM_ROWS, NORM_ROWS), NORM_ROWS)
                _, nvjp = jax.vjp(_pair_rms, k_ref[sl, :], kg_ref[...])
                dk, dg = nvjp(dkn_scr[sl, :])
                dk_ref[sl, :] = dk.astype(BF16)
                dv_ref[sl, :] = dv_scr[sl, :].astype(BF16)
                return dkg + dg

            dkg = lax.fori_loop(0, SEQ // NORM_ROWS, body, jnp.zeros((1, 128), F32))
            _, nvjp = jax.vjp(_pair_rms, ck_ref[...], kg_ref[...])
            dck, dg = nvjp(dckn_scr[...])
            dck_ref[...] = dck
            dkg_ref[...] += dkg + dg

        @pl.when(last & (p == NPAIR - 1))
        def _():
            dqg_ref[...] = dqg_ref[...] + pltpu.roll(dqg_ref[...], HDIM, 1)
            dkg_ref[...] = dkg_ref[...] + pltpu.roll(dkg_ref[...], HDIM, 1)

    blk = lambda rows: pl.BlockSpec((rows, 128), lambda p, i: (0, p))
    qblk = pl.BlockSpec((QBLK, 128), lambda p, i: (i, p))
    return pl.pallas_call(
        kern, name="attn_bwd", grid=(NPAIR, NQBLK),
        in_specs=_attn_in_specs() + [pl.BlockSpec((QBLK, 128), lambda p, i: (i, 4 + p))],
        out_specs=[qblk, blk(SEQ), blk(SEQ), qblk, blk(CTX), blk(CTX),
                   pl.BlockSpec((1, 2, QBLK, KBLK), lambda p, i: (_bias_variant(i), p, 0, 0)),
                   _row(128), _row(128)],
        out_shape=[jax.ShapeDtypeStruct((SEQ, 512), BF16)] * 4 + [jax.ShapeDtypeStruct((CTX, 512), F32)] * 2
        + [jax.ShapeDtypeStruct((3, HEADS, QBLK, KBLK), F32), jax.ShapeDtypeStruct((1, 128), F32),
           jax.ShapeDtypeStruct((1, 128), F32)],
        scratch_shapes=[pltpu.VMEM((SEQ, 128), F32), pltpu.VMEM((CTX, 128), F32),
                        pltpu.VMEM((SEQ, 128), F32), pltpu.VMEM((CTX, 128), F32), pltpu.VMEM((SEQ, 128), F32)],
        compiler_params=_cparams(("arbitrary", "arbitrary"), VMEM_BIG),
    )(z, z, z, z, zc, zc, bias, qg2, kg2, dcat)


def outproj(out_a, out_b, x, target, gate, wo):
    tl = 512

    def kern(a_ref, b_ref, x_ref, t_ref, g_ref, w_ref, loss_ref, dy_ref, dcat_ref, dg_ref, dw_ref):
        @pl.when(pl.program_id(0) == 0)
        def _():
            loss_ref[...] = jnp.zeros_like(loss_ref)
            dg_ref[...] = jnp.zeros_like(dg_ref)
            dw_ref[...] = jnp.zeros_like(dw_ref)

        a, b = a_ref[...].astype(BF16), b_ref[...].astype(BF16)
        mix = (jnp.dot(a, w_ref[0:512, :], preferred_element_type=F32)
               + jnp.dot(b, w_ref[512:1024, :], preferred_element_type=F32))
        err = x_ref[...] + g_ref[...] * mix - t_ref[...]
        loss_ref[...] += 0.5 * jnp.sum(jnp.mean(err * err, axis=-1))
        dy = err * (1.0 / DM)
        dy_ref[...] = dy
        dg_ref[...] += jnp.sum(dy * mix, axis=0, keepdims=True)
        dmix = (g_ref[...] * dy).astype(BF16)
        dcat_ref[...] = lax.dot_general(dmix, w_ref[...], (((1,), (1,)), ((), ())), preferred_element_type=F32)
        dw_ref[0:512, :] += lax.dot_general(a, dmix, (((0,), (0,)), ((), ())), preferred_element_type=F32)
        dw_ref[512:1024, :] += lax.dot_general(b, dmix, (((0,), (0,)), ((), ())), preferred_element_type=F32)

    tile = lambda w: pl.BlockSpec((tl, w), lambda t: (t, 0))
    whole = pl.BlockSpec((DM, DM), lambda t: (0, 0))
    return pl.pallas_call(
        kern, name="outproj", grid=(SEQ // tl,),
        in_specs=[tile(512), tile(512), tile(DM), tile(DM), _row(DM), whole],
        out_specs=[pl.BlockSpec((8, 128), lambda t: (0, 0)), tile(DM), tile(DM), _row(DM), whole],
        out_shape=[jax.ShapeDtypeStruct((8, 128), F32), jax.ShapeDtypeStruct((SEQ, DM), F32),
                   jax.ShapeDtypeStruct((SEQ, DM), F32), jax.ShapeDtypeStruct((1, DM), F32),
                   jax.ShapeDtypeStruct((DM, DM), F32)],
        compiler_params=_cparams(("arbitrary",), 48 * 1024 * 1024),
    )(out_a, out_b, x, target, gate, wo)


def _pieces(sources):
    out = []
    for name, c0, c1 in sources:
        for j in range(NCHIP):
            lo, hi = max(c0, j * SHARD_IN), min(c1, (j + 1) * SHARD_IN)
            if lo < hi:
                out.append((j, lo - j * SHARD_IN, hi - j * SHARD_IN, name, lo - c0, hi - c0))
    return out


DZ_PIECES = _pieces((("a", 0, 1536), ("q", 1536, 2048), ("k", 2048, 2560), ("v", 2560, 3072), ("g", 3072, DIN)))
DZC_PIECES = _pieces((("k", 2048, 2560), ("v", 2560, 3072)))
_NT = (((1,), (1,)), ((), ()))


def _dz_specs(tl):
    return [pl.BlockSpec((tl, 1536), lambda t: (t, 0))] + [pl.BlockSpec((tl, 512), lambda t: (t, 0))] * 4


def dh_bwd(dz_parts, w_full, x, dy, shift, scale, norm_g, dg_ctx):
    tl = 512

    def kern(a_ref, q_ref, k_ref, v_ref, g_ref, w_ref, x_ref, dy_ref, sh_ref, sc_ref, gn_ref, dgc_ref,
             gx_ref, dsh_ref, dsc_ref, dg_ref):
        src = dict(a=a_ref, q=q_ref, k=k_ref, v=v_ref, g=g_ref)
        dh = None
        for j, l0, l1, name, s0, s1 in DZ_PIECES:
            part = lax.dot_general(src[name][:, s0:s1], w_ref[j, :, l0:l1], _NT, preferred_element_type=F32)
            dh = part if dh is None else dh + part

        @pl.when(pl.program_id(0) == 0)
        def _():
            dsh_ref[...] = jnp.zeros_like(dsh_ref)
            dsc_ref[...] = jnp.zeros_like(dsc_ref)
            dg_ref[...] = dgc_ref[...]

        _, vjp = jax.vjp(_modulated, x_ref[...], gn_ref[...], sc_ref[...], sh_ref[...])
        dx, dg, dsc, dsh = vjp(dh)
        gx_ref[...] = dy_ref[...] + dx
        dg_ref[...] += dg
        dsc_ref[...] += dsc
        dsh_ref[...] += dsh

    tile = pl.BlockSpec((tl, DM), lambda t: (t, 0))
    return pl.pallas_call(
        kern, name="dh_bwd", grid=(SEQ // tl,),
        in_specs=_dz_specs(tl) + [pl.BlockSpec((NCHIP, DM, SHARD_IN), lambda t: (0, 0, 0)), tile, tile, _row(DM),
                                  _row(DM), _row(DM), _row(DM)],
        out_specs=[tile, _row(DM), _row(DM), _row(DM)],
        out_shape=[jax.ShapeDtypeStruct((SEQ, DM), F32)] + [jax.ShapeDtypeStruct((1, DM), F32)] * 3,
        compiler_params=_cparams(("arbitrary",), 48 * 1024 * 1024),
    )(*dz_parts, w_full, x, dy, shift, scale, norm_g, dg_ctx)


def dw_bwd(h, dz_parts, hc, dck, dcv):
    tl = 256

    def kern(h_ref, a_ref, q_ref, k_ref, v_ref, g_ref, hc_ref, dck_ref, dcv_ref, dw_ref):
        @pl.when(pl.program_id(0) == 0)
        def _():
            dw_ref[...] = jnp.zeros_like(dw_ref)
            hct = hc_ref[...].T
            csrc = dict(k=dck_ref, v=dcv_ref)
            for j, l0, l1, name, s0, s1 in DZC_PIECES:
                dw_ref[j, :, l0:l1] += jnp.dot(hct, csrc[name][:, s0:s1].astype(BF16), preferred_element_type=F32)

        ht = h_ref[...].T
        src = dict(a=a_ref, q=q_ref, k=k_ref, v=v_ref, g=g_ref)
        for j, l0, l1, name, s0, s1 in DZ_PIECES:
            dw_ref[j, :, l0:l1] += jnp.dot(ht, src[name][:, s0:s1], preferred_element_type=F32)

    whole = lambda r, c: pl.BlockSpec((r, c), lambda t: (0, 0))
    return pl.pallas_call(
        kern, name="dw_bwd", grid=(SEQ // tl,),
        in_specs=[pl.BlockSpec((tl, DM), lambda t: (t, 0))] + _dz_specs(tl) + [whole(CTX, DM), whole(CTX, 512),
                                                                              whole(CTX, 512)],
        out_specs=pl.BlockSpec((NCHIP, DM, SHARD_IN), lambda t: (0, 0, 0)),
        out_shape=jax.ShapeDtypeStruct((NCHIP, DM, SHARD_IN), F32),
        compiler_params=_cparams(("arbitrary",), VMEM_BIG),
    )(h, *dz_parts, hc, dck, dcv)


def ctx_bwd(dck, dcv, w_full, ctx, cshift, cscale, norm_g):
    def kern(dck_ref, dcv_ref, w_ref, c_ref, sh_ref, sc_ref, g_ref, dsh_ref, dsc_ref, dg_ref):
        csrc = dict(k=dck_ref, v=dcv_ref)
        dhc = None
        for j, l0, l1, name, s0, s1 in DZC_PIECES:
            part = lax.dot_general(csrc[name][:, s0:s1].astype(BF16), w_ref[j, :, l0:l1], _NT,
                                   preferred_element_type=F32)
            dhc = part if dhc is None else dhc + part
        _, vjp = jax.vjp(lambda g, sc, sh: _modulated(c_ref[...], g, sc, sh), g_ref[...], sc_ref[...], sh_ref[...])
        dg_ref[...], dsc_ref[...], dsh_ref[...] = vjp(dhc)

    whole = lambda r, c: pl.BlockSpec((r, c), lambda i: (0, 0))
    return pl.pallas_call(
        kern, name="ctx_bwd", grid=(1,),
        in_specs=[whole(CTX, 512), whole(CTX, 512), pl.BlockSpec((NCHIP, DM, SHARD_IN), lambda i: (0, 0, 0)),
                  whole(CTX, DM), _row(DM), _row(DM), _row(DM)],
        out_specs=[_row(DM), _row(DM), _row(DM)],
        out_shape=[jax.ShapeDtypeStruct((1, DM), F32)] * 3,
        compiler_params=_cparams(("arbitrary",), 40 * 1024 * 1024),
    )(dck, dcv, w_full, ctx, cshift, cscale, norm_g)


def _lane_pad_rpb(rpb):
    r = jnp.pad(rpb, ((0, 0), (0, 0), (0, GRID_W - rpb.shape[-1])))
    return jnp.concatenate([r, r], axis=-1)


def local_step(chip, x, ctx, target, mod, cmod, norm_g, sgu_g, w_s, b_s, q_g, k_g, rpb, w_in_shard, w_out_full):
    shift, scale, gate = mod[:, :DM], mod[:, DM:2 * DM], mod[:, 2 * DM:]
    cshift, cscale = cmod[:, :DM], cmod[:, DM:2 * DM]
    bsb = jnp.broadcast_to(b_s[:, :, None], (4, 128, 128))
    qg2, kg2 = jnp.tile(q_g, (1, 2)), jnp.tile(k_g, (1, 2))

    z, h, w_in_full = inproj_fwd(chip, x, shift, scale, norm_g, w_in_shard)
    zc, hc = ctx_fwd(ctx, cshift, cscale, norm_g, w_in_full)
    bias = rpb_tables(_lane_pad_rpb(rpb))
    out_a = sgu_fwd(z, sgu_g, w_s, bsb)
    out_b = attn_fwd(z, zc, bias, qg2, kg2)
    loss8, dy, dcat, dgate, dwo = outproj(out_a, out_b, x, target, gate, w_out_full)
    dz_a, dsg, dws, dbsb = sgu_bwd(z, sgu_g, w_s, bsb, dcat)
    dq, dk, dv, dbg, dck, dcv, dbias, dqg2, dkg2 = attn_bwd(z, zc, bias, qg2, kg2, dcat)
    drpb = rpb_bwd(dbias)[:, :, :rpb.shape[-1]]
    dz_parts = (dz_a, dq, dk, dv, dbg)
    dcshift, dcscale, dng_c = ctx_bwd(dck, dcv, w_in_full, ctx, cshift, cscale, norm_g)
    dw_in = dw_bwd(h, dz_parts, hc, dck, dcv)
    grad_x, dshift, dscale, dng = dh_bwd(dz_parts, w_in_full, x, dy, shift, scale, norm_g, dng_c)
    return dict(
        loss=loss8[0:1, 0:1], grad_x=grad_x, dw_in=dw_in, dw_out=dwo,
        dmod=jnp.concatenate([dshift, dscale, dgate], axis=-1),
        dcmod=jnp.concatenate([dcshift, dcscale, jnp.zeros((1, DM), F32)], axis=-1),
        d_norm_g=dng, d_sgu_g=dsg, d_w_s=dws, d_b_s=dbsb[:, :, 0],
        d_q_g=dqg2[:, :HDIM], d_k_g=dkg2[:, :HDIM], d_rpb=drpb)


def _me():
    return lax.axis_index("x"), lax.axis_index("y"), lax.axis_index("c")


def _flip(q):
    x, y, c = _me()
    return ((1 - x) if q & 4 else x, (1 - y) if q & 2 else y, (1 - c) if q & 1 else c)


def _chip_of(dev):
    return 2 * dev[0] + dev[1]


def _rcopy(src, dst, send_sems, recv_sems, k, dev):
    return pltpu.make_async_remote_copy(src_ref=src, dst_ref=dst, send_sem=send_sems.at[k], recv_sem=recv_sems.at[k],
                                        device_id=dev, device_id_type=MESH_ID)


_VMEM_SPEC = pl.BlockSpec(memory_space=pltpu.VMEM)
CS_ROWS = 8 * NDEV + 8


def gather_fwd(c, c_ctx, w_ada, b_shard, w_out):
    hout = SHARD_OUT // 2
    n_w, n_c, n_m = 6, NDEV - 1, 3

    def kern(c_ref, cc_ref, wa_ref, b_ref, wo_ref, oo_ref, mod_ref, cs_ref, mine, send_sems, recv_sems):
        x, y, cc = _me()
        k, me = 2 * x + y, 4 * x + 2 * y + cc
        sib = _flip(1)

        def blocks(chip, half):
            return (oo_ref.at[chip, pl.ds(pl.multiple_of(half * hout, hout), hout), :],)

        slot = lambda d: pl.ds(pl.multiple_of(8 * d, 8), 8)
        first = lax.broadcasted_iota(jnp.int32, (8, DM), 0) == 0
        mine[...] = jnp.where(first, jnp.broadcast_to(c_ref[...], (8, DM)), 0.0)
        cs_ref[slot(me), :] = mine[...]
        cs_ref[slot(NDEV), :] = jnp.where(first, jnp.broadcast_to(cc_ref[...], (8, DM)), 0.0)
        csends = [_rcopy(mine, cs_ref.at[slot(me), :], send_sems, recv_sems, n_w + q - 1, _flip(q))
                  for q in range(1, NDEV)]
        for cp in csends:
            cp.start()
        oo_ref[k] = wo_ref[...].astype(BF16)
        for q in range(1, NDEV):
            px, py, pc = _flip(q)
            _rcopy(mine, cs_ref.at[slot(4 * px + 2 * py + pc), :], send_sems, recv_sems, n_w + q - 1,
                   _flip(q)).wait_recv()
        act = jax.nn.silu(cs_ref[...]).astype(BF16)
        mod_ref[k] = jnp.dot(act, wa_ref[...].astype(BF16), preferred_element_type=F32) + b_ref[...]
        msends = [_rcopy(mod_ref.at[k], mod_ref.at[k], send_sems, recv_sems, n_w + n_c + q // 2 - 1, _flip(q))
                  for q in (2, 4, 6)]
        for cp in msends:
            cp.start()
        sends = []
        for q in (2, 4, 6):
            for n, blk in enumerate(blocks(k, cc)):
                sends.append(_rcopy(blk, blk, send_sems, recv_sems, q // 2 - 1 + n, _flip(q)))
        for cp in sends:
            cp.start()
        for q in (2, 4, 6):
            kq = _chip_of(_flip(q))
            _rcopy(mod_ref.at[kq], mod_ref.at[kq], send_sems, recv_sems, n_w + n_c + q // 2 - 1, _flip(q)).wait_recv()

        passed = []
        for q in (2, 4, 6):
            for n, blk in enumerate(blocks(_chip_of(_flip(q)), cc)):
                _rcopy(blk, blk, send_sems, recv_sems, q // 2 - 1 + n, _flip(q)).wait_recv()
                cp = _rcopy(blk, blk, send_sems, recv_sems, 3 + q // 2 - 1 + n, sib)
                cp.start()
                passed.append(cp)
        for q in (2, 4, 6):
            for n, blk in enumerate(blocks(_chip_of(_flip(q)), 1 - cc)):
                _rcopy(blk, blk, send_sems, recv_sems, 3 + q // 2 - 1 + n, sib).wait_recv()
        for cp in sends + csends + msends + passed:
            cp.wait_send()

    n_sem = n_w + n_c + n_m
    return pl.pallas_call(
        kern, name="gather_fwd", in_specs=[_VMEM_SPEC] * 5, out_specs=[_VMEM_SPEC] * 3,
        out_shape=[jax.ShapeDtypeStruct((NCHIP, SHARD_OUT, DM), BF16),
                   jax.ShapeDtypeStruct((NCHIP, CS_ROWS, SHARD_ADA), F32), jax.ShapeDtypeStruct((CS_ROWS, DM), F32)],
        scratch_shapes=[pltpu.VMEM((8, DM), F32), pltpu.SemaphoreType.DMA((n_sem,)), pltpu.SemaphoreType.DMA((n_sem,))],
    )(c, c_ctx, w_ada, b_shard, w_out)


SLAB_ROWS = 80


def small_gather(slab):
    def kern(s_ref, all_ref, tot_ref, send_sems, recv_sems):
        x, y, c = _me()
        me = 4 * x + 2 * y + c
        all_ref[me] = s_ref[...]
        sends = [_rcopy(s_ref, all_ref.at[me], send_sems, recv_sems, q - 1, _flip(q)) for q in range(1, NDEV)]
        for cp in sends:
            cp.start()
        for q in range(1, NDEV):
            px, py, pc = _flip(q)
            d = 4 * px + 2 * py + pc
            _rcopy(s_ref, all_ref.at[d], send_sems, recv_sems, q - 1, _flip(q)).wait_recv()
        tot = all_ref[0]
        for d in range(1, NDEV):
            tot = tot + all_ref[d]
        tot_ref[...] = tot
        for cp in sends:
            cp.wait_send()

    return pl.pallas_call(
        kern, name="small_gather", in_specs=[_VMEM_SPEC], out_specs=[_VMEM_SPEC] * 2,
        out_shape=[jax.ShapeDtypeStruct((NDEV, SLAB_ROWS, DM), F32), jax.ShapeDtypeStruct((SLAB_ROWS, DM), F32)],
        scratch_shapes=[pltpu.SemaphoreType.DMA((NDEV - 1,)), pltpu.SemaphoreType.DMA((NDEV - 1,))],
    )(slab)


def ada_bwd(a_in, dm, dm_shard, w_ada, c_ctx):
    def kern(a_ref, dm_ref, dms_ref, w_ref, cc_ref, dw_ref, db_ref, dcc_ref, parts, send_sems, recv_sems):
        x, y, c = _me()
        k = 2 * x + y
        act = jax.nn.silu(a_ref[...]).astype(BF16)
        dms = dms_ref[...].astype(BF16)
        dw_ref[...] = lax.dot_general(act, dms, (((0,), (0,)), ((), ())), preferred_element_type=F32)
        db_ref[...] = jnp.sum(dm_ref[...], axis=0, keepdims=True)
        parts[k] = lax.dot_general(dms, w_ref[...].astype(BF16), (((1,), (1,)), ((), ())), preferred_element_type=F32)
        sends = [_rcopy(parts.at[k], parts.at[k], send_sems, recv_sems, q // 2 - 1, _flip(q)) for q in (2, 4, 6)]
        for cp in sends:
            cp.start()
        for q in (2, 4, 6):
            kq = _chip_of(_flip(q))
            _rcopy(parts.at[kq], parts.at[kq], send_sems, recv_sems, q // 2 - 1, _flip(q)).wait_recv()
        dact = ((parts[0] + parts[1]) + parts[2]) + parts[3]
        _, vjp = jax.vjp(jax.nn.silu, cc_ref[...])
        dcc_ref[...] = vjp(dact[8:9, :])[0]
        for cp in sends:
            cp.wait_send()

    return pl.pallas_call(
        kern, name="ada_bwd", in_specs=[_VMEM_SPEC] * 5, out_specs=[_VMEM_SPEC] * 3,
        out_shape=[jax.ShapeDtypeStruct((DM, SHARD_ADA), F32), jax.ShapeDtypeStruct((1, 3 * DM), F32),
                   jax.ShapeDtypeStruct((1, DM), F32)],
        scratch_shapes=[pltpu.VMEM((NCHIP, 16, DM), F32), pltpu.SemaphoreType.DMA((3,)), pltpu.SemaphoreType.DMA((3,))],
    )(a_in, dm, dm_shard, w_ada, c_ctx)


def reduce_scatter(g, name):
    _, rows, width = g.shape
    rh = rows // 2

    def kern(g_hbm, out_ref, mine, rcv1, wire, rcv2, load_sem, send_sems, recv_sems):
        x, y, c = _me()
        k = 2 * x + y
        sib = _flip(1)
        half = lambda h: pl.ds(pl.multiple_of(h * rh, rh), rh)
        load = pltpu.make_async_copy(g_hbm.at[:, half(c), :], mine, load_sem)
        load.start()
        pair = _rcopy(g_hbm.at[:, half(1 - c), :], rcv1, send_sems, recv_sems, 0, sib)
        pair.start()
        load.wait()
        pair.wait_recv()
        for j in range(NCHIP):
            pair_sum = mine[j] + rcv1[j]
            mine[j] = pair_sum
            wire[j] = pair_sum.astype(BF16)
        sends = [_rcopy(wire.at[_chip_of(_flip(q))], rcv2.at[q // 2 - 1], send_sems, recv_sems, q // 2, _flip(q))
                 for q in (2, 4, 6)]
        for cp in sends:
            cp.start()
        for q in (2, 4, 6):
            _rcopy(wire.at[0], rcv2.at[q // 2 - 1], send_sems, recv_sems, q // 2, _flip(q)).wait_recv()
        out_ref[half(c), :] = ((mine[k] + rcv2[0].astype(F32)) + rcv2[1].astype(F32)) + rcv2[2].astype(F32)
        share = _rcopy(out_ref.at[half(c), :], out_ref.at[half(c), :], send_sems, recv_sems, 4, sib)
        share.start()
        _rcopy(out_ref.at[half(1 - c), :], out_ref.at[half(1 - c), :], send_sems, recv_sems, 4, sib).wait_recv()
        for cp in [pair, share] + sends:
            cp.wait_send()

    return pl.pallas_call(
        kern, name=name, in_specs=[pl.BlockSpec(memory_space=pl.ANY)], out_specs=_VMEM_SPEC,
        out_shape=jax.ShapeDtypeStruct((rows, width), F32),
        scratch_shapes=[pltpu.VMEM((NCHIP, rh, width), F32), pltpu.VMEM((NCHIP, rh, width), F32),
                        pltpu.VMEM((NCHIP, rh, width), BF16), pltpu.VMEM((NCHIP - 1, rh, width), BF16),
                        pltpu.SemaphoreType.DMA(()),
                        pltpu.SemaphoreType.DMA((5,)), pltpu.SemaphoreType.DMA((5,))],
        compiler_params=pltpu.CompilerParams(vmem_limit_bytes=40 * 1024 * 1024),
    )(g)


def _adamw_math(w, g, m, v):
    m = B1 * m + (1.0 - B1) * g
    v = B2 * v + (1.0 - B2) * (g * g)
    m_hat = m / (1.0 - B1 ** STEP)
    v_hat = v / (1.0 - B2 ** STEP)
    return -LR * (m_hat / (jnp.sqrt(v_hat) + ADAM_EPS) + WD * w), m, v


def adamw_big(w, g, m, v, name, block_rows=256):
    rows, width = w.shape

    def kern(w_ref, g_ref, m_ref, v_ref, d_ref, nm_ref, nv_ref):
        d_ref[...], nm_ref[...], nv_ref[...] = _adamw_math(w_ref[...], g_ref[...], m_ref[...], v_ref[...])

    spec = pl.BlockSpec((block_rows, width), lambda i: (i, 0))
    return pl.pallas_call(
        kern, name=name, grid=(rows // block_rows,), in_specs=[spec] * 4, out_specs=[spec] * 3,
        out_shape=[jax.ShapeDtypeStruct((rows, width), F32)] * 3,
        compiler_params=_cparams(("arbitrary",)),
    )(w, g, m, v)


def adamw_small(quads):
    n = len(quads)

    def kern(*refs):
        ins, outs = refs[:4 * n], refs[4 * n:]
        for i in range(n):
            w, g, m, v = (r[...] for r in ins[4 * i:4 * i + 4])
            outs[3 * i][...], outs[3 * i + 1][...], outs[3 * i + 2][...] = _adamw_math(w, g, m, v)

    flat = [a for quad in quads for a in quad]
    res = pl.pallas_call(
        kern, name="adamw_small", in_specs=[_VMEM_SPEC] * (4 * n), out_specs=[_VMEM_SPEC] * (3 * n),
        out_shape=[jax.ShapeDtypeStruct(q[0].shape, F32) for q in quads for _ in range(3)],
    )(*flat)
    return [tuple(res[3 * i:3 * i + 3]) for i in range(n)]


def _rows_of(a, rows):
    flat = a.reshape(-1)
    return jnp.pad(flat, (0, rows * DM - flat.shape[0])).reshape(rows, DM)


def kernel(x, c, ctx, c_ctx, w_ada, b_ada, norm_g, w_in, sgu_norm_g, w_spatial, b_spatial, q_norm_g, k_norm_g, rpb, w_out, loss_target, m_c_ctx, m_w_ada, m_b_ada, m_norm_g, m_w_in, m_sgu_norm_g, m_w_spatial, m_b_spatial, m_q_norm_g, m_k_norm_g, m_rpb, m_w_out, v_c_ctx, v_w_ada, v_b_ada, v_norm_g, v_w_in, v_sgu_norm_g, v_w_spatial, v_b_spatial, v_q_norm_g, v_k_norm_g, v_rpb, v_w_out):
    xi, yi, ci = lax.axis_index("x"), lax.axis_index("y"), lax.axis_index("c")
    chip, dev = 2 * xi + yi, 4 * xi + 2 * yi + ci
    c_ctx2 = c_ctx.reshape(1, DM)

    b_shard = lax.dynamic_slice(b_ada, (0, chip * SHARD_ADA), (1, SHARD_ADA))
    w_out_full, mod_all, cs = gather_fwd(c, c_ctx2, w_ada[0], b_shard, w_out[0])
    mods = mod_all.transpose(1, 0, 2).reshape(CS_ROWS, 3 * DM)
    mod = lax.dynamic_slice(mods, (8 * dev, 0), (1, 3 * DM))
    cmod = mods[8 * NDEV:8 * NDEV + 1]

    part = local_step(chip.reshape(1).astype(jnp.int32), x[0], ctx[0], loss_target[0], mod, cmod, norm_g, sgu_norm_g,
                      w_spatial[0], b_spatial[0], q_norm_g, k_norm_g, rpb[0], w_in[0], w_out_full.reshape(DM, DM))

    slab = jnp.concatenate([
        part["d_norm_g"], _rows_of(part["d_sgu_g"], 1), _rows_of(part["d_b_s"], 1),
        _rows_of(jnp.concatenate([part["d_q_g"], part["d_k_g"]], axis=-1), 1), _rows_of(part["d_rpb"], 4),
        _rows_of(part["loss"], 1), _rows_of(part["dcmod"], 3), _rows_of(part["dmod"], 3), jnp.zeros((1, DM), F32),
        _rows_of(part["d_w_s"], 64)], axis=0)
    gathered, tot = small_gather(slab)
    dm = jnp.concatenate([gathered[:, 12:15, :].reshape(NDEV, 3 * DM), tot[9:12].reshape(1, 3 * DM),
                          jnp.zeros((7, 3 * DM), F32)], axis=0)
    a_in = jnp.concatenate([cs[0:8 * NDEV:8], cs[8 * NDEV:8 * NDEV + 1], jnp.zeros((7, DM), F32)], axis=0)
    dm_shard = lax.dynamic_slice(dm, (0, chip * SHARD_ADA), (16, SHARD_ADA))
    g_w_ada, g_b_ada, g_c_ctx = ada_bwd(a_in, dm, dm_shard, w_ada[0], c_ctx2)
    g_w_in = reduce_scatter(part["dw_in"], "rs_w_in")
    g_w_out = reduce_scatter(part["dw_out"].reshape(NCHIP, SHARD_OUT, DM), "rs_w_out")

    loss = tot[8, 0]
    g_small = dict(
        c_ctx=g_c_ctx, b_ada=g_b_ada, norm_g=tot[0:1], sgu_norm_g=tot[1:2, :512], w_spatial=tot[16:80].reshape(512, 128),
        b_spatial=tot[2:3, :512].reshape(4, 128), q_norm_g=tot[3:4, :HDIM], k_norm_g=tot[3:4, HDIM:2 * HDIM],
        rpb=tot[4:8].reshape(-1)[:HEADS * 15 * 31].reshape(HEADS * 15, 31))
    shapes = dict(c_ctx=(DM,), w_ada=(1, DM, SHARD_ADA), b_ada=(1, 3 * DM), norm_g=(1, DM), w_in=(1, DM, SHARD_IN),
                  sgu_norm_g=(1, 512), w_spatial=(1, 4, 128, 128), b_spatial=(1, 4, 128), q_norm_g=(1, HDIM),
                  k_norm_g=(1, HDIM), rpb=(1, HEADS, 15, 31), w_out=(1, SHARD_OUT, DM))
    names = list(shapes)
    weights = dict(c_ctx=c_ctx, w_ada=w_ada, b_ada=b_ada, norm_g=norm_g, w_in=w_in, sgu_norm_g=sgu_norm_g,
                   w_spatial=w_spatial, b_spatial=b_spatial, q_norm_g=q_norm_g, k_norm_g=k_norm_g, rpb=rpb, w_out=w_out)
    m_in = dict(zip(names, (m_c_ctx, m_w_ada, m_b_ada, m_norm_g, m_w_in, m_sgu_norm_g, m_w_spatial, m_b_spatial,
                            m_q_norm_g, m_k_norm_g, m_rpb, m_w_out)))
    v_in = dict(zip(names, (v_c_ctx, v_w_ada, v_b_ada, v_norm_g, v_w_in, v_sgu_norm_g, v_w_spatial, v_b_spatial,
                            v_q_norm_g, v_k_norm_g, v_rpb, v_w_out)))
    grads = dict(g_small, w_ada=g_w_ada, w_in=g_w_in, w_out=g_w_out)
    upd = {}
    for n in ("w_ada", "w_in", "w_out"):
        g = grads[n]
        upd[n] = adamw_big(weights[n].reshape(g.shape), g, m_in[n].reshape(g.shape), v_in[n].reshape(g.shape),
                           "adamw_" + n)
    small = [n for n in names if n not in upd]
    res = adamw_small([(weights[n].reshape(grads[n].shape), grads[n], m_in[n].reshape(grads[n].shape),
                        v_in[n].reshape(grads[n].shape)) for n in small])
    upd.update(zip(small, res))
    out = [loss, part["grad_x"].reshape(1, SEQ, DM)]
    out += [grads[n].reshape(shapes[n]) for n in names]
    for slot in range(3):
        out += [upd[n][slot].reshape(shapes[n]) for n in names]
    return tuple(out)
```

```python
import jax
import jax.numpy as jnp
from jax import lax
from jax.experimental import pallas as pl
from jax.experimental.pallas import tpu as pltpu

F32, BF16 = jnp.float32, jnp.bfloat16
SEQ, DM, CTX, DIN = 4096, 1024, 256, 3584
NCHIP, NDEV = 4, 8
SHARD_IN = DIN // NCHIP
SHARD_ADA = 3 * DM // NCHIP
SHARD_OUT = DM // NCHIP
GRID_W = 64
QROWS = 4
KROWS = 12
QBLK, KBLK = QROWS * GRID_W, KROWS * GRID_W
NQBLK = SEQ // QBLK
HEADS, HDIM, NPAIR = 8, 64, 4
EPS = 1e-6
NEG_INF = -1e30
ZQ, ZK, ZV, ZG = 12, 16, 20, 24
LR, B1, B2, ADAM_EPS, WD, STEP = 0.001, 0.9, 0.999, 1e-08, 0.01, 10
VMEM_BIG = 56 * 1024 * 1024
MESH_ID = pl.DeviceIdType.MESH


def _dot(a, b, lhs_c, rhs_c):
    return lax.dot_general(a.astype(BF16), b.astype(BF16), (((lhs_c,), (rhs_c,)), ((), ())),
                           preferred_element_type=F32)


@jax.custom_vjp
def mm(a, b):
    return _dot(a, b, 1, 0)


@jax.custom_vjp
def mm_nt(a, b):
    return _dot(a, b, 1, 1)


@jax.custom_vjp
def mm_tn(a, b):
    return _dot(a, b, 0, 0)


mm.defvjp(lambda a, b: (mm(a, b), (a, b)), lambda r, ct: (mm_nt(ct, r[1]), mm_tn(r[0], ct)))
mm_nt.defvjp(lambda a, b: (mm_nt(a, b), (a, b)), lambda r, ct: (mm(ct, r[1]), mm_tn(ct, r[0])))
mm_tn.defvjp(lambda a, b: (mm_tn(a, b), (a, b)), lambda r, ct: (mm_nt(r[1], ct), mm(r[0], ct)))


def _rms(x, g):
    return x * lax.rsqrt(jnp.mean(x * x, axis=-1, keepdims=True) + EPS) * g


def _modulated(x, g, scale, shift):
    return _rms(x, g) * (1.0 + scale) + shift


def _pair_rms(x, g2):
    lo = lax.broadcasted_iota(jnp.int32, (1, 2 * HDIM), 1) < HDIM
    sq = x * x
    s_lo = jnp.sum(jnp.where(lo, sq, 0.0), axis=-1, keepdims=True)
    s_hi = jnp.sum(jnp.where(lo, 0.0, sq), axis=-1, keepdims=True)
    rs = jnp.where(lo, lax.rsqrt(s_lo / HDIM + EPS), lax.rsqrt(s_hi / HDIM + EPS))
    return x * rs * g2


def _cparams(sem, vmem=None):
    return pltpu.CompilerParams(dimension_semantics=sem, vmem_limit_bytes=vmem)


def _row(n):
    return pl.BlockSpec((1, n), lambda *_: (0, 0))


def inproj_fwd(chip, x, shift, scale, norm_g, w_shard, wo_shard):
    tl = 512
    nt = SEQ // tl
    halves = (DM // 2, SHARD_OUT // 2)

    def kern(k_ref, x_ref, sh_ref, sc_ref, g_ref, w_ref, wo_ref, z_ref, h_ref, wfull_ref, wofull_ref,
             w_scr, wo_scr, h_scr, send_sems, recv_sems):
        s, t = pl.program_id(0), pl.program_id(1)
        xi, yi, c = _me()
        k = 2 * xi + yi
        sib = _flip(1)
        rows = pl.ds(pl.multiple_of(t * tl, tl), tl)
        gathered = (w_scr, wo_scr)

        def block(n, chip_of_block, hh):
            return gathered[n].at[chip_of_block, pl.ds(pl.multiple_of(hh * halves[n], halves[n]), halves[n]), :]

        def ici(n, q, chip_of_block):
            blk = block(n, chip_of_block, c)
            return _rcopy(blk, blk, send_sems, recv_sems, 6 * n + q // 2 - 1, _flip(q))

        def d2d(n, q, chip_of_block, hh):
            blk = block(n, chip_of_block, hh)
            return _rcopy(blk, blk, send_sems, recv_sems, 6 * n + 3 + q // 2 - 1, sib)

        @pl.when((s == 0) & (t == 0))
        def _():
            w_scr[k] = w_ref[...].astype(BF16)
            wo_scr[k] = wo_ref[...].astype(BF16)
            for q in (2, 4, 6):
                ici(0, q, k).start()
            for q in (2, 4, 6):
                ici(1, q, k).start()

        for sweep in (1, 2, 3):
            @pl.when((s == sweep) & (t == 0))
            def _():
                q = 2 * sweep
                src = _chip_of(_flip(q))
                for n in (0, 1):
                    ici(n, q, src).wait_recv()
                    d2d(n, q, src, c).start()
                for n in (0, 1):
                    d2d(n, q, src, 1 - c).wait_recv()

        @pl.when(s == 0)
        def _():
            hb = _modulated(x_ref[...], g_ref[...], sc_ref[...], sh_ref[...]).astype(BF16)
            h_scr[rows, :] = hb
            h_ref[...] = hb

        z_ref[...] = jnp.dot(h_scr[rows, :], w_scr[lax.bitwise_xor(k, s)], preferred_element_type=F32)

        @pl.when((s == NCHIP - 1) & (t == nt - 1))
        def _():
            for n in (0, 1):
                for q in (2, 4, 6):
                    ici(n, q, k).wait_send()
                    d2d(n, q, _chip_of(_flip(q)), c).wait_send()
            pltpu.sync_copy(w_scr, wfull_ref)
            pltpu.sync_copy(wo_scr, wofull_ref)

    once = lambda s, t, k: (jnp.where(s == 0, t, nt - 1), 0)
    row = lambda n: pl.BlockSpec((1, n), lambda s, t, k: (0, 0))
    hbm = pl.BlockSpec(memory_space=pl.ANY)
    return pl.pallas_call(
        kern, name="inproj_fwd",
        grid_spec=pltpu.PrefetchScalarGridSpec(
            num_scalar_prefetch=1, grid=(NCHIP, nt),
            in_specs=[pl.BlockSpec((tl, DM), once), row(DM), row(DM), row(DM), _VMEM_SPEC, _VMEM_SPEC],
            out_specs=[pl.BlockSpec((tl, SHARD_IN), lambda s, t, k: (t, lax.bitwise_xor(k[0], s))),
                       pl.BlockSpec((tl, DM), once), hbm, hbm],
            scratch_shapes=[pltpu.VMEM((NCHIP, DM, SHARD_IN), BF16), pltpu.VMEM((NCHIP, SHARD_OUT, DM), BF16),
                            pltpu.VMEM((SEQ, DM), BF16), pltpu.SemaphoreType.DMA((12,)), pltpu.SemaphoreType.DMA((12,))]),
        out_shape=[jax.ShapeDtypeStruct((SEQ, DIN), F32), jax.ShapeDtypeStruct((SEQ, DM), BF16),
                   jax.ShapeDtypeStruct((NCHIP, DM, SHARD_IN), BF16), jax.ShapeDtypeStruct((NCHIP, SHARD_OUT, DM), BF16)],
        compiler_params=_cparams(("arbitrary", "arbitrary"), 48 * 1024 * 1024),
    )(chip, x, shift, scale, norm_g, w_shard, wo_shard)


def ctx_fwd(ctx, cshift, cscale, norm_g, w_full):
    def kern(c_ref, sh_ref, sc_ref, g_ref, w2_ref, w3_ref, zc_ref, hc_ref):
        hc = _modulated(c_ref[...], g_ref[...], sc_ref[...], sh_ref[...]).astype(BF16)
        hc_ref[...] = hc
        zc_ref[:, :SHARD_IN] = jnp.dot(hc, w2_ref[0], preferred_element_type=F32)
        zc_ref[:, SHARD_IN:] = jnp.dot(hc, w3_ref[0], preferred_element_type=F32)

    return pl.pallas_call(
        kern, name="ctx_fwd", grid=(1,),
        in_specs=[pl.BlockSpec((CTX, DM), lambda i: (0, 0)), _row(DM), _row(DM), _row(DM),
                  pl.BlockSpec((1, DM, SHARD_IN), lambda i: (2, 0, 0)),
                  pl.BlockSpec((1, DM, SHARD_IN), lambda i: (3, 0, 0))],
        out_specs=[pl.BlockSpec((CTX, 2 * SHARD_IN), lambda i: (0, 0)),
                   pl.BlockSpec((CTX, DM), lambda i: (0, 0))],
        out_shape=[jax.ShapeDtypeStruct((CTX, 2 * SHARD_IN), F32), jax.ShapeDtypeStruct((CTX, DM), BF16)],
        compiler_params=_cparams(("arbitrary",)),
    )(ctx, cshift, cscale, norm_g, w_full, w_full)


SGU_CHUNK, SGU_PER_STEP = 128, 4


def _gelu(x):
    return 0.5 * x * (1.0 + lax.erf(x * 0.7071067811865476))


def _sgu_chunk(au, av, ag, sg, ws, bsb):
    u, v = _gelu(au), _gelu(av)
    outs = []
    for g in range(4):
        sl = slice(128 * g, 128 * (g + 1))
        mixed = mm(ws[g], _rms(v[:, sl], sg[:, sl])) + bsb[g]
        outs.append(u[:, sl] * mixed * jax.nn.silu(ag[:, sl]))
    return jnp.concatenate(outs, axis=-1)


def _sgu_specs():
    rows = SGU_CHUNK * SGU_PER_STEP
    zspec = lambda c: pl.BlockSpec((rows, 512), lambda n: (n, c))
    wspec = pl.BlockSpec((4, 128, 128), lambda n: (0, 0, 0))
    return rows, [zspec(0), zspec(1), zspec(2), _row(512), wspec, wspec]


def sgu_fwd(z, sg, ws, bsb):
    rows, in_specs = _sgu_specs()

    def kern(au_ref, av_ref, ag_ref, sg_ref, ws_ref, bs_ref, o_ref):
        for c in range(SGU_PER_STEP):
            sl = slice(c * SGU_CHUNK, (c + 1) * SGU_CHUNK)
            o_ref[sl, :] = _sgu_chunk(au_ref[sl, :], av_ref[sl, :], ag_ref[sl, :], sg_ref[...], ws_ref[...],
                                      bs_ref[...])

    return pl.pallas_call(
        kern, name="sgu_fwd", grid=(SEQ // rows,), in_specs=in_specs,
        out_specs=pl.BlockSpec((rows, 512), lambda n: (n, 0)),
        out_shape=jax.ShapeDtypeStruct((SEQ, 512), F32),
        compiler_params=_cparams(("arbitrary",)),
    )(z, z, z, sg, ws, bsb)


def sgu_bwd(z, sg, ws, bsb, dcat):
    rows, in_specs = _sgu_specs()

    def kern(au_ref, av_ref, ag_ref, sg_ref, ws_ref, bs_ref, do_ref, dz_ref, dsg_ref, dws_ref, dbs_ref):
        @pl.when(pl.program_id(0) == 0)
        def _():
            dsg_ref[...] = jnp.zeros_like(dsg_ref)
            dws_ref[...] = jnp.zeros_like(dws_ref)
            dbs_ref[...] = jnp.zeros_like(dbs_ref)

        for c in range(SGU_PER_STEP):
            sl = slice(c * SGU_CHUNK, (c + 1) * SGU_CHUNK)
            _, vjp = jax.vjp(_sgu_chunk, au_ref[sl, :], av_ref[sl, :], ag_ref[sl, :], sg_ref[...], ws_ref[...],
                             bs_ref[...])
            dau, dav, dag, dsg, dws, dbs = vjp(do_ref[sl, :])
            dz_ref[sl, 0:512] = dau.astype(BF16)
            dz_ref[sl, 512:1024] = dav.astype(BF16)
            dz_ref[sl, 1024:1536] = dag.astype(BF16)
            dsg_ref[...] += dsg
            dws_ref[...] += dws
            dbs_ref[...] += dbs

        @pl.when(pl.program_id(0) == pl.num_programs(0) - 1)
        def _():
            dbs_ref[...] = jnp.broadcast_to(jnp.sum(dbs_ref[...], axis=-1, keepdims=True), dbs_ref.shape)

    wspec = pl.BlockSpec((4, 128, 128), lambda n: (0, 0, 0))
    return pl.pallas_call(
        kern, name="sgu_bwd", grid=(SEQ // rows,),
        in_specs=in_specs + [pl.BlockSpec((rows, 512), lambda n: (n, 0))],
        out_specs=[pl.BlockSpec((rows, 1536), lambda n: (n, 0)), _row(512), wspec, wspec],
        out_shape=[jax.ShapeDtypeStruct((SEQ, 1536), BF16), jax.ShapeDtypeStruct((1, 512), F32),
                   jax.ShapeDtypeStruct((4, 128, 128), F32), jax.ShapeDtypeStruct((4, 128, 128), F32)],
        compiler_params=_cparams(("arbitrary",)),
    )(z, z, z, sg, ws, bsb, dcat)


_DR_OFF = (7, 3, -1)


def _row_valid(v, rr, j):
    return (j < 8, rr <= j < rr + 8, 4 <= j < 12)[v]


def _col_window():
    q = lax.broadcasted_iota(jnp.int32, (GRID_W, 128), 0)
    kc = lax.broadcasted_iota(jnp.int32, (GRID_W, 128), 1) % GRID_W
    c0 = jnp.clip(q - 8, 0, GRID_W - 16)
    return (kc >= c0) & (kc < c0 + 16)


def rpb_tables(rpb2):
    def kern(r_ref, b_ref):
        base = r_ref[0]
        lo = lax.broadcasted_iota(jnp.int32, (1, 128), 1) < GRID_W
        win = _col_window()
        neg = jnp.full((GRID_W, 128), NEG_INF, F32)
        for v in range(3):
            for rr in range(QROWS):
                for jp in range(KROWS // 2):
                    j0, j1 = 2 * jp, 2 * jp + 1
                    ok0, ok1 = _row_valid(v, rr, j0), _row_valid(v, rr, j1)
                    if not (ok0 or ok1):
                        tile = neg
                    else:
                        d0 = j0 - rr + _DR_OFF[v]
                        r0 = base[d0:d0 + 1, :] if ok0 else jnp.zeros((1, 128), F32)
                        r1 = base[d0 + 1:d0 + 2, :] if ok1 else jnp.zeros((1, 128), F32)
                        y = jnp.broadcast_to(jnp.where(lo, r0, r1), (GRID_W, 128))
                        y = pltpu.roll(pltpu.roll(y, 128 - 15, 1), 0, 1, stride=1, stride_axis=0)
                        ok = win & jnp.where(lo, ok0, ok1)
                        tile = jnp.where(ok, y, NEG_INF)
                    b_ref[v, 0, rr * GRID_W:(rr + 1) * GRID_W, jp * 128:(jp + 1) * 128] = tile

    return pl.pallas_call(
        kern, name="rpb_tables", grid=(HEADS,),
        in_specs=[pl.BlockSpec((1, 15, 128), lambda h: (h, 0, 0))],
        out_specs=pl.BlockSpec((3, 1, QBLK, KBLK), lambda h: (0, h, 0, 0)),
        out_shape=jax.ShapeDtypeStruct((3, HEADS, QBLK, KBLK), F32),
        compiler_params=_cparams(("arbitrary",)),
    )(rpb2)


def rpb_bwd(dbias):
    def kern(g_ref, o_ref):
        lo = lax.broadcasted_iota(jnp.int32, (1, 128), 1) < GRID_W
        ri = lax.broadcasted_iota(jnp.int32, (GRID_W, GRID_W), 0)
        ci = lax.broadcasted_iota(jnp.int32, (GRID_W, GRID_W), 1)
        flip = (ri + ci == GRID_W - 1).astype(F32)
        acc = [jnp.zeros((1, 128), F32) for _ in range(15)]
        for v in range(3):
            for rr in range(QROWS):
                for jp in range(KROWS // 2):
                    j0, j1 = 2 * jp, 2 * jp + 1
                    ok0, ok1 = _row_valid(v, rr, j0), _row_valid(v, rr, j1)
                    if not (ok0 or ok1):
                        continue
                    g = g_ref[v, 0, rr * GRID_W:(rr + 1) * GRID_W, jp * 128:(jp + 1) * 128]
                    g = lax.dot_general(flip, g, (((1,), (0,)), ((), ())), precision=lax.Precision.HIGHEST,
                                        preferred_element_type=F32)
                    g = pltpu.roll(pltpu.roll(g, 128 - 48, 1), 0, 1, stride=1, stride_axis=0)
                    s = jnp.sum(g, axis=0, keepdims=True)
                    d0 = j0 - rr + _DR_OFF[v]
                    if ok0:
                        acc[d0] = acc[d0] + jnp.where(lo, s, 0.0)
                    if ok1:
                        acc[d0 + 1] = acc[d0 + 1] + jnp.where(lo, 0.0, s)
        for d in range(15):
            o_ref[0, d:d + 1, :] = acc[d] + pltpu.roll(acc[d], GRID_W, 1)

    return pl.pallas_call(
        kern, name="rpb_bwd", grid=(HEADS,),
        in_specs=[pl.BlockSpec((3, 1, QBLK, KBLK), lambda h: (0, h, 0, 0))],
        out_specs=pl.BlockSpec((1, 15, 128), lambda h: (h, 0, 0)),
        out_shape=jax.ShapeDtypeStruct((HEADS, 15, 128), F32),
        compiler_params=_cparams(("arbitrary",)),
    )(dbias)


def _attn_step(q_raw, kn, v, ckn, cv, bias2, qg):
    qn = _pair_rms(q_raw, qg) * (HDIM ** -0.5)
    lo = lax.broadcasted_iota(jnp.int32, (1, 2 * HDIM), 1) < HDIM
    out = None
    for a in range(2):
        mine = lo if a == 0 else jnp.logical_not(lo)
        qa = jnp.where(mine, qn, 0.0)
        s_lat = mm_nt(qa, kn) + bias2[a]
        s_ctx = mm_nt(qa, ckn)
        m = lax.stop_gradient(jnp.maximum(jnp.max(s_lat, axis=-1, keepdims=True),
                                          jnp.max(s_ctx, axis=-1, keepdims=True)))
        p_lat = jnp.exp(s_lat - m)
        p_ctx = jnp.exp(s_ctx - m)
        den = jnp.sum(p_lat, axis=-1, keepdims=True) + jnp.sum(p_ctx, axis=-1, keepdims=True)
        o = jnp.where(mine, (mm(p_lat, v) + mm(p_ctx, cv)) / den, 0.0)
        out = o if out is None else out + o
    return out


def _attn_gated(q_raw, kn, v, ckn, cv, bias2, qg, bg):
    return _attn_step(q_raw, kn, v, ckn, cv, bias2, qg) * jax.nn.silu(bg)


def _kstart(i):
    return pl.multiple_of(jnp.clip((i - 1) * QBLK, 0, SEQ - KBLK), QBLK)


def _bias_variant(i):
    return jnp.where(i == 0, 0, jnp.where(i == NQBLK - 1, 2, 1))


def _attn_in_specs():
    return [
        pl.BlockSpec((QBLK, 128), lambda p, i: (i, ZQ + p)),
        pl.BlockSpec((SEQ, 128), lambda p, i: (0, ZK + p)),
        pl.BlockSpec((SEQ, 128), lambda p, i: (0, ZV + p)),
        pl.BlockSpec((QBLK, 128), lambda p, i: (i, ZG + p)),
        pl.BlockSpec((CTX, 128), lambda p, i: (0, 2 + p)),
        pl.BlockSpec((CTX, 128), lambda p, i: (0, 6 + p)),
        pl.BlockSpec((1, 2, QBLK, KBLK), lambda p, i: (_bias_variant(i), p, 0, 0)),
        _row(128), _row(128),
    ]


NORM_ROWS = 512


def _norm_keys(k_ref, ck_ref, kg_ref, kn_scr, ckn_scr):
    def body(c, carry):
        sl = pl.ds(pl.multiple_of(c * NORM_ROWS, NORM_ROWS), NORM_ROWS)
        kn_scr[sl, :] = _pair_rms(k_ref[sl, :], kg_ref[...])
        return carry

    lax.fori_loop(0, SEQ // NORM_ROWS, body, 0)
    ckn_scr[...] = _pair_rms(ck_ref[...], kg_ref[...])


def attn_fwd(z, zc, bias, qg2, kg2):
    def kern(q_ref, k_ref, v_ref, bg_ref, ck_ref, cv_ref, b_ref, qg_ref, kg_ref, o_ref, kn_scr, ckn_scr):
        i = pl.program_id(1)

        @pl.when(i == 0)
        def _():
            _norm_keys(k_ref, ck_ref, kg_ref, kn_scr, ckn_scr)

        ks = pl.ds(_kstart(i), KBLK)
        o_ref[...] = _attn_gated(q_ref[...], kn_scr[ks, :], v_ref[ks, :], ckn_scr[...], cv_ref[...], b_ref[0],
                                 qg_ref[...], bg_ref[...])

    return pl.pallas_call(
        kern, name="attn_fwd", grid=(NPAIR, NQBLK), in_specs=_attn_in_specs(),
        out_specs=pl.BlockSpec((QBLK, 128), lambda p, i: (i, p)),
        out_shape=jax.ShapeDtypeStruct((SEQ, 512), F32),
        scratch_shapes=[pltpu.VMEM((SEQ, 128), F32), pltpu.VMEM((CTX, 128), F32)],
        compiler_params=_cparams(("arbitrary", "arbitrary"), 40 * 1024 * 1024),
    )(z, z, z, z, zc, zc, bias, qg2, kg2)


def attn_bwd(z, zc, bias, qg2, kg2, dcat):
    def kern(q_ref, k_ref, v_ref, bg_ref, ck_ref, cv_ref, b_ref, qg_ref, kg_ref, do_ref,
             dq_ref, dk_ref, dv_ref, dbg_ref, dck_ref, dcv_ref, db_ref, dqg_ref, dkg_ref,
             kn_scr, ckn_scr, dkn_scr, dckn_scr, dv_scr):
        p, i = pl.program_id(0), pl.program_id(1)
        last = i == NQBLK - 1

        @pl.when(i == 0)
        def _():
            _norm_keys(k_ref, ck_ref, kg_ref, kn_scr, ckn_scr)
            dkn_scr[...] = jnp.zeros_like(dkn_scr)
            dv_scr[...] = jnp.zeros_like(dv_scr)
            dckn_scr[...] = jnp.zeros_like(dckn_scr)
            dcv_ref[...] = jnp.zeros_like(dcv_ref)

        @pl.when((i == 0) & (p == 0))
        def _():
            dqg_ref[...] = jnp.zeros_like(dqg_ref)
            dkg_ref[...] = jnp.zeros_like(dkg_ref)

        ks = pl.ds(_kstart(i), KBLK)
        _, vjp = jax.vjp(_attn_gated, q_ref[...], kn_scr[ks, :], v_ref[ks, :], ckn_scr[...], cv_ref[...], b_ref[0],
                         qg_ref[...], bg_ref[...])
        dq, dkn, dv, dckn, dcv, db, dqg, dbg = vjp(do_ref[...])
        dq_ref[...] = dq.astype(BF16)
        dbg_ref[...] = dbg.astype(BF16)
        dkn_scr[ks, :] += dkn
        dv_scr[ks, :] += dv
        dckn_scr[...] += dckn
        dcv_ref[...] += dcv
        dqg_ref[...] += dqg
        fresh = (i == 0) | (i == 1) | last

        @pl.when(fresh)
        def _():
            db_ref[0] = db

        @pl.when(jnp.logical_not(fresh))
        def _():
            db_ref[0] += db

        @pl.when(last)
        def _():
            def body(c, dkg):
                sl = pl.ds(pl.multiple_of(c * NORM_ROWS, NORM_ROWS), NORM_ROWS)
                _, nvjp = jax.vjp(_pair_rms, k_ref[sl, :], kg_ref[...])
                dk, dg = nvjp(dkn_scr[sl, :])
                dk_ref[sl, :] = dk.astype(BF16)
                dv_ref[sl, :] = dv_scr[sl, :].astype(BF16)
                return dkg + dg

            dkg = lax.fori_loop(0, SEQ // NORM_ROWS, body, jnp.zeros((1, 128), F32))
            _, nvjp = jax.vjp(_pair_rms, ck_ref[...], kg_ref[...])
            dck, dg = nvjp(dckn_scr[...])
            dck_ref[...] = dck
            dkg_ref[...] += dkg + dg

        @pl.when(last & (p == NPAIR - 1))
        def _():
            dqg_ref[...] = dqg_ref[...] + pltpu.roll(dqg_ref[...], HDIM, 1)
            dkg_ref[...] = dkg_ref[...] + pltpu.roll(dkg_ref[...], HDIM, 1)

    blk = lambda rows: pl.BlockSpec((rows, 128), lambda p, i: (0, p))
    qblk = pl.BlockSpec((QBLK, 128), lambda p, i: (i, p))
    return pl.pallas_call(
        kern, name="attn_bwd", grid=(NPAIR, NQBLK),
        in_specs=_attn_in_specs() + [pl.BlockSpec((QBLK, 128), lambda p, i: (i, 4 + p))],
        out_specs=[qblk, blk(SEQ), blk(SEQ), qblk, blk(CTX), blk(CTX),
                   pl.BlockSpec((1, 2, QBLK, KBLK), lambda p, i: (_bias_variant(i), p, 0, 0)),
                   _row(128), _row(128)],
        out_shape=[jax.ShapeDtypeStruct((SEQ, 512), BF16)] * 4 + [jax.ShapeDtypeStruct((CTX, 512), F32)] * 2
        + [jax.ShapeDtypeStruct((3, HEADS, QBLK, KBLK), F32), jax.ShapeDtypeStruct((1, 128), F32),
           jax.ShapeDtypeStruct((1, 128), F32)],
        scratch_shapes=[pltpu.VMEM((SEQ, 128), F32), pltpu.VMEM((CTX, 128), F32),
                        pltpu.VMEM((SEQ, 128), F32), pltpu.VMEM((CTX, 128), F32), pltpu.VMEM((SEQ, 128), F32)],
        compiler_params=_cparams(("arbitrary", "arbitrary"), VMEM_BIG),
    )(z, z, z, z, zc, zc, bias, qg2, kg2, dcat)


def outproj(out_a, out_b, x, target, gate, wo):
    tl = 512

    def kern(a_ref, b_ref, x_ref, t_ref, g_ref, w_ref, loss_ref, dy_ref, dcat_ref, dg_ref, dw_ref):
        @pl.when(pl.program_id(0) == 0)
        def _():
            loss_ref[...] = jnp.zeros_like(loss_ref)
            dg_ref[...] = jnp.zeros_like(dg_ref)
            dw_ref[...] = jnp.zeros_like(dw_ref)

        a, b = a_ref[...].astype(BF16), b_ref[...].astype(BF16)
        mix = (jnp.dot(a, w_ref[0:512, :], preferred_element_type=F32)
               + jnp.dot(b, w_ref[512:1024, :], preferred_element_type=F32))
        err = x_ref[...] + g_ref[...] * mix - t_ref[...]
        loss_ref[...] += 0.5 * jnp.sum(jnp.mean(err * err, axis=-1))
        dy = err * (1.0 / DM)
        dy_ref[...] = dy
        dg_ref[...] += jnp.sum(dy * mix, axis=0, keepdims=True)
        dmix = (g_ref[...] * dy).astype(BF16)
        dcat_ref[...] = lax.dot_general(dmix, w_ref[...], (((1,), (1,)), ((), ())), preferred_element_type=F32)
        dw_ref[0:512, :] += lax.dot_general(a, dmix, (((0,), (0,)), ((), ())), preferred_element_type=F32)
        dw_ref[512:1024, :] += lax.dot_general(b, dmix, (((0,), (0,)), ((), ())), preferred_element_type=F32)

    tile = lambda w: pl.BlockSpec((tl, w), lambda t: (t, 0))
    whole = pl.BlockSpec((DM, DM), lambda t: (0, 0))
    return pl.pallas_call(
        kern, name="outproj", grid=(SEQ // tl,),
        in_specs=[tile(512), tile(512), tile(DM), tile(DM), _row(DM), whole],
        out_specs=[pl.BlockSpec((8, 128), lambda t: (0, 0)), tile(DM), tile(DM), _row(DM), whole],
        out_shape=[jax.ShapeDtypeStruct((8, 128), F32), jax.ShapeDtypeStruct((SEQ, DM), F32),
                   jax.ShapeDtypeStruct((SEQ, DM), F32), jax.ShapeDtypeStruct((1, DM), F32),
                   jax.ShapeDtypeStruct((DM, DM), F32)],
        compiler_params=_cparams(("arbitrary",), 48 * 1024 * 1024),
    )(out_a, out_b, x, target, gate, wo)


def _pieces(sources):
    out = []
    for name, c0, c1 in sources:
        for j in range(NCHIP):
            lo, hi = max(c0, j * SHARD_IN), min(c1, (j + 1) * SHARD_IN)
            if lo < hi:
                out.append((j, lo - j * SHARD_IN, hi - j * SHARD_IN, name, lo - c0, hi - c0))
    return out


DZ_PIECES = _pieces((("a", 0, 1536), ("q", 1536, 2048), ("k", 2048, 2560), ("v", 2560, 3072), ("g", 3072, DIN)))
DZC_PIECES = _pieces((("k", 2048, 2560), ("v", 2560, 3072)))
_NT = (((1,), (1,)), ((), ()))


def _dz_specs(tl):
    return [pl.BlockSpec((tl, 1536), lambda t: (t, 0))] + [pl.BlockSpec((tl, 512), lambda t: (t, 0))] * 4


def dh_bwd(dz_parts, w_full, x, dy, shift, scale, norm_g, dg_ctx, wire_i, wire_o):
    tl = 512
    nt = SEQ // tl

    def kern(a_ref, q_ref, k_ref, v_ref, g_ref, w_ref, x_ref, dy_ref, sh_ref, sc_ref, gn_ref, dgc_ref, wi_hbm, wo_hbm,
             gx_ref, dsh_ref, dsc_ref, dg_ref, goti_ref, goto_ref, rcv_i, rcv_o, send_sems, recv_sems):
        def ici(n, q):
            wire, rcv = ((wi_hbm, rcv_i), (wo_hbm, rcv_o))[n]
            return _rcopy(wire.at[_chip_of(_flip(q))], rcv.at[q // 2 - 1], send_sems, recv_sems, 3 * n + q // 2 - 1,
                          _flip(q))

        @pl.when(pl.program_id(0) == 0)
        def _():
            for n in (0, 1):
                for q in (2, 4, 6):
                    ici(n, q).start()

        src = dict(a=a_ref, q=q_ref, k=k_ref, v=v_ref, g=g_ref)
        dh = None
        for j, l0, l1, name, s0, s1 in DZ_PIECES:
            part = lax.dot_general(src[name][:, s0:s1], w_ref[j, :, l0:l1], _NT, preferred_element_type=F32)
            dh = part if dh is None else dh + part

        @pl.when(pl.program_id(0) == 0)
        def _():
            dsh_ref[...] = jnp.zeros_like(dsh_ref)
            dsc_ref[...] = jnp.zeros_like(dsc_ref)
            dg_ref[...] = dgc_ref[...]

        _, vjp = jax.vjp(_modulated, x_ref[...], gn_ref[...], sc_ref[...], sh_ref[...])
        dx, dg, dsc, dsh = vjp(dh)
        gx_ref[...] = dy_ref[...] + dx
        dg_ref[...] += dg
        dsc_ref[...] += dsc
        dsh_ref[...] += dsh

        @pl.when(pl.program_id(0) == nt - 1)
        def _():
            for n in (0, 1):
                for q in (2, 4, 6):
                    ici(n, q).wait_recv()
                    ici(n, q).wait_send()
            goti_ref[...] = rcv_i[...]
            goto_ref[...] = rcv_o[...]

    tile = pl.BlockSpec((tl, DM), lambda t: (t, 0))
    hbm = pl.BlockSpec(memory_space=pl.ANY)
    got = [(NCHIP - 1, rh, w) for rh, w in RS_SHAPES]
    return pl.pallas_call(
        kern, name="dh_bwd", grid=(nt,),
        in_specs=_dz_specs(tl) + [pl.BlockSpec((NCHIP, DM, SHARD_IN), lambda t: (0, 0, 0)), tile, tile, _row(DM),
                                  _row(DM), _row(DM), _row(DM), hbm, hbm],
        out_specs=[tile, _row(DM), _row(DM), _row(DM)] + [pl.BlockSpec(s, lambda t: (0, 0, 0)) for s in got],
        out_shape=[jax.ShapeDtypeStruct((SEQ, DM), F32)] + [jax.ShapeDtypeStruct((1, DM), F32)] * 3
        + [jax.ShapeDtypeStruct(s, BF16) for s in got],
        scratch_shapes=[pltpu.VMEM(s, BF16) for s in got] + [pltpu.SemaphoreType.DMA((6,)), pltpu.SemaphoreType.DMA((6,))],
        compiler_params=_cparams(("arbitrary",), VMEM_BIG),
    )(*dz_parts, w_full, x, dy, shift, scale, norm_g, dg_ctx, wire_i, wire_o)


def dw_bwd(h, dz_parts, hc, dck, dcv):
    tl = 256

    def kern(h_ref, a_ref, q_ref, k_ref, v_ref, g_ref, hc_ref, dck_ref, dcv_ref, dw_ref):
        @pl.when(pl.program_id(0) == 0)
        def _():
            dw_ref[...] = jnp.zeros_like(dw_ref)
            hct = hc_ref[...].T
            csrc = dict(k=dck_ref, v=dcv_ref)
            for j, l0, l1, name, s0, s1 in DZC_PIECES:
                dw_ref[j, :, l0:l1] += jnp.dot(hct, csrc[name][:, s0:s1].astype(BF16), preferred_element_type=F32)

        ht = h_ref[...].T
        src = dict(a=a_ref, q=q_ref, k=k_ref, v=v_ref, g=g_ref)
        for j, l0, l1, name, s0, s1 in DZ_PIECES:
            dw_ref[j, :, l0:l1] += jnp.dot(ht, src[name][:, s0:s1], preferred_element_type=F32)

    whole = lambda r, c: pl.BlockSpec((r, c), lambda t: (0, 0))
    return pl.pallas_call(
        kern, name="dw_bwd", grid=(SEQ // tl,),
        in_specs=[pl.BlockSpec((tl, DM), lambda t: (t, 0))] + _dz_specs(tl) + [whole(CTX, DM), whole(CTX, 512),
                                                                              whole(CTX, 512)],
        out_specs=pl.BlockSpec((NCHIP, DM, SHARD_IN), lambda t: (0, 0, 0)),
        out_shape=jax.ShapeDtypeStruct((NCHIP, DM, SHARD_IN), F32),
        compiler_params=_cparams(("arbitrary",), VMEM_BIG),
    )(h, *dz_parts, hc, dck, dcv)


def ctx_bwd(dck, dcv, w_full, ctx, cshift, cscale, norm_g):
    def kern(dck_ref, dcv_ref, w_ref, c_ref, sh_ref, sc_ref, g_ref, dsh_ref, dsc_ref, dg_ref):
        csrc = dict(k=dck_ref, v=dcv_ref)
        dhc = None
        for j, l0, l1, name, s0, s1 in DZC_PIECES:
            part = lax.dot_general(csrc[name][:, s0:s1].astype(BF16), w_ref[j, :, l0:l1], _NT,
                                   preferred_element_type=F32)
            dhc = part if dhc is None else dhc + part
        _, vjp = jax.vjp(lambda g, sc, sh: _modulated(c_ref[...], g, sc, sh), g_ref[...], sc_ref[...], sh_ref[...])
        dg_ref[...], dsc_ref[...], dsh_ref[...] = vjp(dhc)

    whole = lambda r, c: pl.BlockSpec((r, c), lambda i: (0, 0))
    return pl.pallas_call(
        kern, name="ctx_bwd", grid=(1,),
        in_specs=[whole(CTX, 512), whole(CTX, 512), pl.BlockSpec((NCHIP, DM, SHARD_IN), lambda i: (0, 0, 0)),
                  whole(CTX, DM), _row(DM), _row(DM), _row(DM)],
        out_specs=[_row(DM), _row(DM), _row(DM)],
        out_shape=[jax.ShapeDtypeStruct((1, DM), F32)] * 3,
        compiler_params=_cparams(("arbitrary",), 40 * 1024 * 1024),
    )(dck, dcv, w_full, ctx, cshift, cscale, norm_g)


def _lane_pad_rpb(rpb):
    r = jnp.pad(rpb, ((0, 0), (0, 0), (0, GRID_W - rpb.shape[-1])))
    return jnp.concatenate([r, r], axis=-1)


def local_step(chip, x, ctx, target, mod, cmod, norm_g, sgu_g, w_s, b_s, q_g, k_g, rpb, w_in_shard, w_out_shard):
    shift, scale, gate = mod[:, :DM], mod[:, DM:2 * DM], mod[:, 2 * DM:]
    cshift, cscale = cmod[:, :DM], cmod[:, DM:2 * DM]
    bsb = jnp.broadcast_to(b_s[:, :, None], (4, 128, 128))
    qg2, kg2 = jnp.tile(q_g, (1, 2)), jnp.tile(k_g, (1, 2))

    z, h, w_in_full, w_out_full = inproj_fwd(chip, x, shift, scale, norm_g, w_in_shard, w_out_shard)
    zc, hc = ctx_fwd(ctx, cshift, cscale, norm_g, w_in_full)
    bias = rpb_tables(_lane_pad_rpb(rpb))
    out_a = sgu_fwd(z, sgu_g, w_s, bsb)
    out_b = attn_fwd(z, zc, bias, qg2, kg2)
    loss8, dy, dcat, dgate, dwo = outproj(out_a, out_b, x, target, gate, w_out_full.reshape(DM, DM))
    dz_a, dsg, dws, dbsb = sgu_bwd(z, sgu_g, w_s, bsb, dcat)
    dq, dk, dv, dbg, dck, dcv, dbias, dqg2, dkg2 = attn_bwd(z, zc, bias, qg2, kg2, dcat)
    drpb = rpb_bwd(dbias)[:, :, :rpb.shape[-1]]
    dz_parts = (dz_a, dq, dk, dv, dbg)
    dcshift, dcscale, dng_c = ctx_bwd(dck, dcv, w_in_full, ctx, cshift, cscale, norm_g)
    dw_in = dw_bwd(h, dz_parts, hc, dck, dcv)
    wire_i, keep_i, wire_o, keep_o = pair_sum(dw_in, dwo.reshape(NCHIP, SHARD_OUT, DM))
    grad_x, dshift, dscale, dng, got_i, got_o = dh_bwd(dz_parts, w_in_full, x, dy, shift, scale, norm_g, dng_c,
                                                       wire_i, wire_o)
    return dict(
        loss=loss8[0:1, 0:1], grad_x=grad_x, rs=(keep_i, got_i, keep_o, got_o),
        dmod=jnp.concatenate([dshift, dscale, dgate], axis=-1),
        dcmod=jnp.concatenate([dcshift, dcscale, jnp.zeros((1, DM), F32)], axis=-1),
        d_norm_g=dng, d_sgu_g=dsg, d_w_s=dws, d_b_s=dbsb[:, :, 0],
        d_q_g=dqg2[:, :HDIM], d_k_g=dkg2[:, :HDIM], d_rpb=drpb)


def _me():
    return lax.axis_index("x"), lax.axis_index("y"), lax.axis_index("c")


def _flip(q):
    x, y, c = _me()
    return ((1 - x) if q & 4 else x, (1 - y) if q & 2 else y, (1 - c) if q & 1 else c)


def _chip_of(dev):
    return 2 * dev[0] + dev[1]


def _rcopy(src, dst, send_sems, recv_sems, k, dev):
    return pltpu.make_async_remote_copy(src_ref=src, dst_ref=dst, send_sem=send_sems.at[k], recv_sem=recv_sems.at[k],
                                        device_id=dev, device_id_type=MESH_ID)


_VMEM_SPEC = pl.BlockSpec(memory_space=pltpu.VMEM)
CS_ROWS = 8 * NDEV + 8


def ada_fwd(c, c_ctx, w_ada, b_shard):
    n_c = NDEV - 1

    def kern(c_ref, cc_ref, wa_ref, b_ref, mod_ref, cs_ref, mine, send_sems, recv_sems):
        x, y, cc = _me()
        k, me = 2 * x + y, 4 * x + 2 * y + cc
        slot = lambda d: pl.ds(pl.multiple_of(8 * d, 8), 8)
        first = lax.broadcasted_iota(jnp.int32, (8, DM), 0) == 0
        mine[...] = jnp.where(first, jnp.broadcast_to(c_ref[...], (8, DM)), 0.0)
        cs_ref[slot(me), :] = mine[...]
        cs_ref[slot(NDEV), :] = jnp.where(first, jnp.broadcast_to(cc_ref[...], (8, DM)), 0.0)
        csends = [_rcopy(mine, cs_ref.at[slot(me), :], send_sems, recv_sems, q - 1, _flip(q)) for q in range(1, NDEV)]
        for cp in csends:
            cp.start()
        wa = wa_ref[...].astype(BF16)
        for q in range(1, NDEV):
            px, py, pc = _flip(q)
            _rcopy(mine, cs_ref.at[slot(4 * px + 2 * py + pc), :], send_sems, recv_sems, q - 1, _flip(q)).wait_recv()
        act = jax.nn.silu(cs_ref[...]).astype(BF16)
        mod_ref[k] = jnp.dot(act, wa, preferred_element_type=F32) + b_ref[...]
        msends = [_rcopy(mod_ref.at[k], mod_ref.at[k], send_sems, recv_sems, n_c + q // 2 - 1, _flip(q))
                  for q in (2, 4, 6)]
        for cp in msends:
            cp.start()
        for q in (2, 4, 6):
            kq = _chip_of(_flip(q))
            _rcopy(mod_ref.at[kq], mod_ref.at[kq], send_sems, recv_sems, n_c + q // 2 - 1, _flip(q)).wait_recv()
        for cp in csends + msends:
            cp.wait_send()

    return pl.pallas_call(
        kern, name="ada_fwd", in_specs=[_VMEM_SPEC] * 4, out_specs=[_VMEM_SPEC] * 2,
        out_shape=[jax.ShapeDtypeStruct((NCHIP, CS_ROWS, SHARD_ADA), F32), jax.ShapeDtypeStruct((CS_ROWS, DM), F32)],
        scratch_shapes=[pltpu.VMEM((8, DM), F32), pltpu.SemaphoreType.DMA((n_c + 3,)),
                        pltpu.SemaphoreType.DMA((n_c + 3,))],
    )(c, c_ctx, w_ada, b_shard)


SLAB_ROWS = 80


RS_SHAPES = ((DM // 2, SHARD_IN), (SHARD_OUT // 2, DM))


def pair_sum(g_in, g_out):
    def kern(gi_hbm, go_hbm, wire_i, keep_i, wire_o, keep_o, mine_i, rcv_i, mine_o, rcv_o, load_sems, send_sems,
             recv_sems):
        x, y, c = _me()
        k = 2 * x + y
        sib = _flip(1)
        work = ((gi_hbm, mine_i, rcv_i, wire_i, keep_i), (go_hbm, mine_o, rcv_o, wire_o, keep_o))
        copies = []
        for n, (g, mine, rcv, _, _) in enumerate(work):
            rh = RS_SHAPES[n][0]
            half = lambda hh, rh=rh: pl.ds(pl.multiple_of(hh * rh, rh), rh)
            load = pltpu.make_async_copy(g.at[:, half(c), :], mine, load_sems.at[n])
            load.start()
            pair = _rcopy(g.at[:, half(1 - c), :], rcv, send_sems, recv_sems, n, sib)
            pair.start()
            copies.append((load, pair))
        for (load, pair), (_, mine, rcv, wire, keep) in zip(copies, work):
            load.wait()
            pair.wait_recv()
            for j in range(NCHIP):
                wire[j] = (mine[j] + rcv[j]).astype(BF16)
            keep[...] = mine[k] + rcv[k]
        for _, pair in copies:
            pair.wait_send()

    (rhi, wi), (rho, wo) = RS_SHAPES
    hbm = pl.BlockSpec(memory_space=pl.ANY)
    return pl.pallas_call(
        kern, name="pair_sum", in_specs=[hbm, hbm], out_specs=[_VMEM_SPEC] * 4,
        out_shape=[jax.ShapeDtypeStruct((NCHIP, rhi, wi), BF16), jax.ShapeDtypeStruct((rhi, wi), F32),
                   jax.ShapeDtypeStruct((NCHIP, rho, wo), BF16), jax.ShapeDtypeStruct((rho, wo), F32)],
        scratch_shapes=[pltpu.VMEM((NCHIP, rhi, wi), F32), pltpu.VMEM((NCHIP, rhi, wi), F32),
                        pltpu.VMEM((NCHIP, rho, wo), F32), pltpu.VMEM((NCHIP, rho, wo), F32),
                        pltpu.SemaphoreType.DMA((2,)), pltpu.SemaphoreType.DMA((2,)), pltpu.SemaphoreType.DMA((2,))],
        compiler_params=pltpu.CompilerParams(vmem_limit_bytes=48 * 1024 * 1024),
    )(g_in, g_out)


def final_reduce(keep_i, got_i, keep_o, got_o, slab):
    def kern(ki_ref, gi_ref, ko_ref, go_ref, s_ref, gin_ref, gout_ref, all_ref, tot_ref, send_sems, recv_sems):
        x, y, c = _me()
        sib = _flip(1)
        dev = lambda d: 4 * d[0] + 2 * d[1] + d[2]
        me = dev((x, y, c))

        def slab_copy(idx, owner, to):
            return _rcopy(all_ref.at[dev(owner)], all_ref.at[dev(owner)], send_sems, recv_sems, idx, to)

        all_ref[me] = s_ref[...]
        first = [slab_copy(0, (x, y, c), sib)] + [slab_copy(q // 2, (x, y, c), _flip(q)) for q in (2, 4, 6)]
        for cp in first:
            cp.start()

        shares = []
        for n, (keep, got, out) in enumerate(((ki_ref, gi_ref, gin_ref), (ko_ref, go_ref, gout_ref))):
            rh = RS_SHAPES[n][0]
            half = lambda hh, rh=rh: pl.ds(pl.multiple_of(hh * rh, rh), rh)
            out[half(c), :] = ((keep[...] + got[0].astype(F32)) + got[1].astype(F32)) + got[2].astype(F32)
            share = _rcopy(out.at[half(c), :], out.at[half(c), :], send_sems, recv_sems, 7 + n, sib)
            share.start()
            shares.append((share, _rcopy(out.at[half(1 - c), :], out.at[half(1 - c), :], send_sems, recv_sems, 7 + n,
                                         sib)))

        passed = []
        for q in (2, 4, 6):
            slab_copy(q // 2, _flip(q), (x, y, c)).wait_recv()
            cp = slab_copy(3 + q // 2, _flip(q), sib)
            cp.start()
            passed.append(cp)
        slab_copy(0, sib, (x, y, c)).wait_recv()
        for q in (2, 4, 6):
            slab_copy(3 + q // 2, _flip(q | 1), (x, y, c)).wait_recv()
        tot = all_ref[0]
        for d in range(1, NDEV):
            tot = tot + all_ref[d]
        tot_ref[...] = tot
        for share, arrival in shares:
            arrival.wait_recv()
            share.wait_send()
        for cp in first + passed:
            cp.wait_send()

    (rhi, wi), (rho, wo) = RS_SHAPES
    return pl.pallas_call(
        kern, name="final_reduce", in_specs=[_VMEM_SPEC] * 5, out_specs=[_VMEM_SPEC] * 4,
        out_shape=[jax.ShapeDtypeStruct((2 * rhi, wi), F32), jax.ShapeDtypeStruct((2 * rho, wo), F32),
                   jax.ShapeDtypeStruct((NDEV, SLAB_ROWS, DM), F32), jax.ShapeDtypeStruct((SLAB_ROWS, DM), F32)],
        scratch_shapes=[pltpu.SemaphoreType.DMA((9,)), pltpu.SemaphoreType.DMA((9,))],
        compiler_params=pltpu.CompilerParams(vmem_limit_bytes=40 * 1024 * 1024),
    )(keep_i, got_i, keep_o, got_o, slab)


def ada_bwd(a_in, dm, dm_shard, w_ada, c_ctx):
    def kern(a_ref, dm_ref, dms_ref, w_ref, cc_ref, dw_ref, db_ref, dcc_ref, parts, send_sems, recv_sems):
        x, y, c = _me()
        k = 2 * x + y
        act = jax.nn.silu(a_ref[...]).astype(BF16)
        dms = dms_ref[...].astype(BF16)
        dw_ref[...] = lax.dot_general(act, dms, (((0,), (0,)), ((), ())), preferred_element_type=F32)
        db_ref[...] = jnp.sum(dm_ref[...], axis=0, keepdims=True)
        parts[k] = lax.dot_general(dms, w_ref[...].astype(BF16), (((1,), (1,)), ((), ())), preferred_element_type=F32)
        sends = [_rcopy(parts.at[k], parts.at[k], send_sems, recv_sems, q // 2 - 1, _flip(q)) for q in (2, 4, 6)]
        for cp in sends:
            cp.start()
        for q in (2, 4, 6):
            kq = _chip_of(_flip(q))
            _rcopy(parts.at[kq], parts.at[kq], send_sems, recv_sems, q // 2 - 1, _flip(q)).wait_recv()
        dact = ((parts[0] + parts[1]) + parts[2]) + parts[3]
        _, vjp = jax.vjp(jax.nn.silu, cc_ref[...])
        dcc_ref[...] = vjp(dact[8:9, :])[0]
        for cp in sends:
            cp.wait_send()

    return pl.pallas_call(
        kern, name="ada_bwd", in_specs=[_VMEM_SPEC] * 5, out_specs=[_VMEM_SPEC] * 3,
        out_shape=[jax.ShapeDtypeStruct((DM, SHARD_ADA), F32), jax.ShapeDtypeStruct((1, 3 * DM), F32),
                   jax.ShapeDtypeStruct((1, DM), F32)],
        scratch_shapes=[pltpu.VMEM((NCHIP, 16, DM), F32), pltpu.SemaphoreType.DMA((3,)), pltpu.SemaphoreType.DMA((3,))],
    )(a_in, dm, dm_shard, w_ada, c_ctx)


def _adamw_math(w, g, m, v):
    m = B1 * m + (1.0 - B1) * g
    v = B2 * v + (1.0 - B2) * (g * g)
    m_hat = m / (1.0 - B1 ** STEP)
    v_hat = v / (1.0 - B2 ** STEP)
    return -LR * (m_hat / (jnp.sqrt(v_hat) + ADAM_EPS) + WD * w), m, v


def adamw_big(w, g, m, v, name, block_rows=256):
    rows, width = w.shape

    def kern(w_ref, g_ref, m_ref, v_ref, d_ref, nm_ref, nv_ref):
        d_ref[...], nm_ref[...], nv_ref[...] = _adamw_math(w_ref[...], g_ref[...], m_ref[...], v_ref[...])

    spec = pl.BlockSpec((block_rows, width), lambda i: (i, 0))
    return pl.pallas_call(
        kern, name=name, grid=(rows // block_rows,), in_specs=[spec] * 4, out_specs=[spec] * 3,
        out_shape=[jax.ShapeDtypeStruct((rows, width), F32)] * 3,
        compiler_params=_cparams(("arbitrary",)),
    )(w, g, m, v)


def adamw_small(quads):
    n = len(quads)

    def kern(*refs):
        ins, outs = refs[:4 * n], refs[4 * n:]
        for i in range(n):
            w, g, m, v = (r[...] for r in ins[4 * i:4 * i + 4])
            outs[3 * i][...], outs[3 * i + 1][...], outs[3 * i + 2][...] = _adamw_math(w, g, m, v)

    flat = [a for quad in quads for a in quad]
    res = pl.pallas_call(
        kern, name="adamw_small", in_specs=[_VMEM_SPEC] * (4 * n), out_specs=[_VMEM_SPEC] * (3 * n),
        out_shape=[jax.ShapeDtypeStruct(q[0].shape, F32) for q in quads for _ in range(3)],
    )(*flat)
    return [tuple(res[3 * i:3 * i + 3]) for i in range(n)]


def _rows_of(a, rows):
    flat = a.reshape(-1)
    return jnp.pad(flat, (0, rows * DM - flat.shape[0])).reshape(rows, DM)


def kernel(x, c, ctx, c_ctx, w_ada, b_ada, norm_g, w_in, sgu_norm_g, w_spatial, b_spatial, q_norm_g, k_norm_g, rpb, w_out, loss_target, m_c_ctx, m_w_ada, m_b_ada, m_norm_g, m_w_in, m_sgu_norm_g, m_w_spatial, m_b_spatial, m_q_norm_g, m_k_norm_g, m_rpb, m_w_out, v_c_ctx, v_w_ada, v_b_ada, v_norm_g, v_w_in, v_sgu_norm_g, v_w_spatial, v_b_spatial, v_q_norm_g, v_k_norm_g, v_rpb, v_w_out):
    xi, yi, ci = lax.axis_index("x"), lax.axis_index("y"), lax.axis_index("c")
    chip, dev = 2 * xi + yi, 4 * xi + 2 * yi + ci
    c_ctx2 = c_ctx.reshape(1, DM)

    b_shard = lax.dynamic_slice(b_ada, (0, chip * SHARD_ADA), (1, SHARD_ADA))
    mod_all, cs = ada_fwd(c, c_ctx2, w_ada[0], b_shard)
    mods = mod_all.transpose(1, 0, 2).reshape(CS_ROWS, 3 * DM)
    mod = lax.dynamic_slice(mods, (8 * dev, 0), (1, 3 * DM))
    cmod = mods[8 * NDEV:8 * NDEV + 1]

    part = local_step(chip.reshape(1).astype(jnp.int32), x[0], ctx[0], loss_target[0], mod, cmod, norm_g, sgu_norm_g,
                      w_spatial[0], b_spatial[0], q_norm_g, k_norm_g, rpb[0], w_in[0], w_out[0])

    slab = jnp.concatenate([
        part["d_norm_g"], _rows_of(part["d_sgu_g"], 1), _rows_of(part["d_b_s"], 1),
        _rows_of(jnp.concatenate([part["d_q_g"], part["d_k_g"]], axis=-1), 1), _rows_of(part["d_rpb"], 4),
        _rows_of(part["loss"], 1), _rows_of(part["dcmod"], 3), _rows_of(part["dmod"], 3), jnp.zeros((1, DM), F32),
        _rows_of(part["d_w_s"], 64)], axis=0)
    g_w_in, g_w_out, gathered, tot = final_reduce(*part["rs"], slab)
    dm = jnp.concatenate([gathered[:, 12:15, :].reshape(NDEV, 3 * DM), tot[9:12].reshape(1, 3 * DM),
                          jnp.zeros((7, 3 * DM), F32)], axis=0)
    a_in = jnp.concatenate([cs[0:8 * NDEV:8], cs[8 * NDEV:8 * NDEV + 1], jnp.zeros((7, DM), F32)], axis=0)
    dm_shard = lax.dynamic_slice(dm, (0, chip * SHARD_ADA), (16, SHARD_ADA))
    g_w_ada, g_b_ada, g_c_ctx = ada_bwd(a_in, dm, dm_shard, w_ada[0], c_ctx2)

    loss = tot[8, 0]
    g_small = dict(
        c_ctx=g_c_ctx, b_ada=g_b_ada, norm_g=tot[0:1], sgu_norm_g=tot[1:2, :512], w_spatial=tot[16:80].reshape(512, 128),
        b_spatial=tot[2:3, :512].reshape(4, 128), q_norm_g=tot[3:4, :HDIM], k_norm_g=tot[3:4, HDIM:2 * HDIM],
        rpb=tot[4:8].reshape(-1)[:HEADS * 15 * 31].reshape(HEADS * 15, 31))
    shapes = dict(c_ctx=(DM,), w_ada=(1, DM, SHARD_ADA), b_ada=(1, 3 * DM), norm_g=(1, DM), w_in=(1, DM, SHARD_IN),
                  sgu_norm_g=(1, 512), w_spatial=(1, 4, 128, 128), b_spatial=(1, 4, 128), q_norm_g=(1, HDIM),
                  k_norm_g=(1, HDIM), rpb=(1, HEADS, 15, 31), w_out=(1, SHARD_OUT, DM))
    names = list(shapes)
    weights = dict(c_ctx=c_ctx, w_ada=w_ada, b_ada=b_ada, norm_g=norm_g, w_in=w_in, sgu_norm_g=sgu_norm_g,
                   w_spatial=w_spatial, b_spatial=b_spatial, q_norm_g=q_norm_g, k_norm_g=k_norm_g, rpb=rpb, w_out=w_out)
    m_in = dict(zip(names, (m_c_ctx, m_w_ada, m_b_ada, m_norm_g, m_w_in, m_sgu_norm_g, m_w_spatial, m_b_spatial,
                            m_q_norm_g, m_k_norm_g, m_rpb, m_w_out)))
    v_in = dict(zip(names, (v_c_ctx, v_w_ada, v_b_ada, v_norm_g, v_w_in, v_sgu_norm_g, v_w_spatial, v_b_spatial,
                            v_q_norm_g, v_k_norm_g, v_rpb, v_w_out)))
    grads = dict(g_small, w_ada=g_w_ada, w_in=g_w_in, w_out=g_w_out)
    upd = {}
    for n in ("w_ada", "w_in", "w_out"):
        g = grads[n]
        upd[n] = adamw_big(weights[n].reshape(g.shape), g, m_in[n].reshape(g.shape), v_in[n].reshape(g.shape),
                           "adamw_" + n)
    small = [n for n in names if n not in upd]
    res = adamw_small([(weights[n].reshape(grads[n].shape), grads[n], m_in[n].reshape(grads[n].shape),
                        v_in[n].reshape(grads[n].shape)) for n in small])
    upd.update(zip(small, res))
    out = [loss, part["grad_x"].reshape(1, SEQ, DM)]
    out += [grads[n].reshape(shapes[n]) for n in names]
    for slot in range(3):
        out += [upd[n][slot].reshape(shapes[n]) for n in names]
    return tuple(out)
```

```python
import jax
import jax.numpy as jnp
from jax import lax
from jax.experimental import pallas as pl
from jax.experimental.pallas import tpu as pltpu

F32, BF16 = jnp.float32, jnp.bfloat16
SEQ, DM, CTX, DIN = 4096, 1024, 256, 3584
NCHIP, NDEV = 4, 8
SHARD_IN = DIN // NCHIP
SHARD_ADA = 3 * DM // NCHIP
SHARD_OUT = DM // NCHIP
GRID_W = 64
QROWS = 4
KROWS = 12
QBLK, KBLK = QROWS * GRID_W, KROWS * GRID_W
NQBLK = SEQ // QBLK
HEADS, HDIM, NPAIR = 8, 64, 4
EPS = 1e-6
NEG_INF = -1e30
ZQ, ZK, ZV, ZG = 12, 16, 20, 24
LR, B1, B2, ADAM_EPS, WD, STEP = 0.001, 0.9, 0.999, 1e-08, 0.01, 10
VMEM_BIG = 56 * 1024 * 1024
MESH_ID = pl.DeviceIdType.MESH


def _dot(a, b, lhs_c, rhs_c):
    return lax.dot_general(a.astype(BF16), b.astype(BF16), (((lhs_c,), (rhs_c,)), ((), ())),
                           preferred_element_type=F32)


@jax.custom_vjp
def mm(a, b):
    return _dot(a, b, 1, 0)


@jax.custom_vjp
def mm_nt(a, b):
    return _dot(a, b, 1, 1)


@jax.custom_vjp
def mm_tn(a, b):
    return _dot(a, b, 0, 0)


mm.defvjp(lambda a, b: (mm(a, b), (a, b)), lambda r, ct: (mm_nt(ct, r[1]), mm_tn(r[0], ct)))
mm_nt.defvjp(lambda a, b: (mm_nt(a, b), (a, b)), lambda r, ct: (mm(ct, r[1]), mm_tn(ct, r[0])))
mm_tn.defvjp(lambda a, b: (mm_tn(a, b), (a, b)), lambda r, ct: (mm_nt(r[1], ct), mm(r[0], ct)))


def _rms(x, g):
    return x * lax.rsqrt(jnp.mean(x * x, axis=-1, keepdims=True) + EPS) * g


def _modulated(x, g, scale, shift):
    return _rms(x, g) * (1.0 + scale) + shift


def _pair_rms(x, g2):
    lo = lax.broadcasted_iota(jnp.int32, (1, 2 * HDIM), 1) < HDIM
    sq = x * x
    s_lo = jnp.sum(jnp.where(lo, sq, 0.0), axis=-1, keepdims=True)
    s_hi = jnp.sum(jnp.where(lo, 0.0, sq), axis=-1, keepdims=True)
    rs = jnp.where(lo, lax.rsqrt(s_lo / HDIM + EPS), lax.rsqrt(s_hi / HDIM + EPS))
    return x * rs * g2


def _cparams(sem, vmem=None):
    return pltpu.CompilerParams(dimension_semantics=sem, vmem_limit_bytes=vmem)


def _row(n):
    return pl.BlockSpec((1, n), lambda *_: (0, 0))


def inproj_fwd(chip, x, shift, scale, norm_g, w_shard, wo_shard):
    tl = 512
    nt = SEQ // tl
    halves = (DM // 2, SHARD_OUT // 2)

    def kern(k_ref, x_ref, sh_ref, sc_ref, g_ref, w_ref, wo_ref, z_ref, h_ref, wfull_ref, wofull_ref,
             w_scr, wo_scr, h_scr, send_sems, recv_sems):
        s, t = pl.program_id(0), pl.program_id(1)
        xi, yi, c = _me()
        k = 2 * xi + yi
        sib = _flip(1)
        rows = pl.ds(pl.multiple_of(t * tl, tl), tl)
        gathered = (w_scr, wo_scr)

        def block(n, chip_of_block, hh):
            return gathered[n].at[chip_of_block, pl.ds(pl.multiple_of(hh * halves[n], halves[n]), halves[n]), :]

        def ici(n, q, chip_of_block):
            blk = block(n, chip_of_block, c)
            return _rcopy(blk, blk, send_sems, recv_sems, 6 * n + q // 2 - 1, _flip(q))

        def d2d(n, q, chip_of_block, hh):
            blk = block(n, chip_of_block, hh)
            return _rcopy(blk, blk, send_sems, recv_sems, 6 * n + 3 + q // 2 - 1, sib)

        @pl.when((s == 0) & (t == 0))
        def _():
            w_scr[k] = w_ref[...].astype(BF16)
            wo_scr[k] = wo_ref[...].astype(BF16)
            for q in (2, 4, 6):
                ici(0, q, k).start()
                ici(1, q, k).start()

        for sweep in (1, 2, 3):
            @pl.when((s == sweep) & (t == 0))
            def _():
                q = 2 * sweep
                src = _chip_of(_flip(q))
                for n in (0, 1):
                    ici(n, q, src).wait_recv()
                    d2d(n, q, src, c).start()
                for n in (0, 1):
                    d2d(n, q, src, 1 - c).wait_recv()

        @pl.when(s == 0)
        def _():
            hb = _modulated(x_ref[...], g_ref[...], sc_ref[...], sh_ref[...]).astype(BF16)
            h_scr[rows, :] = hb
            h_ref[...] = hb

        z_ref[...] = jnp.dot(h_scr[rows, :], w_scr[lax.bitwise_xor(k, s)], preferred_element_type=F32)

        @pl.when((s == NCHIP - 1) & (t == nt - 1))
        def _():
            for n in (0, 1):
                for q in (2, 4, 6):
                    ici(n, q, k).wait_send()
                    d2d(n, q, _chip_of(_flip(q)), c).wait_send()
            pltpu.sync_copy(w_scr, wfull_ref)
            pltpu.sync_copy(wo_scr, wofull_ref)

    once = lambda s, t, k: (jnp.where(s == 0, t, nt - 1), 0)
    row = lambda n: pl.BlockSpec((1, n), lambda s, t, k: (0, 0))
    hbm = pl.BlockSpec(memory_space=pl.ANY)
    return pl.pallas_call(
        kern, name="inproj_fwd",
        grid_spec=pltpu.PrefetchScalarGridSpec(
            num_scalar_prefetch=1, grid=(NCHIP, nt),
            in_specs=[pl.BlockSpec((tl, DM), once), row(DM), row(DM), row(DM), _VMEM_SPEC, _VMEM_SPEC],
            out_specs=[pl.BlockSpec((tl, SHARD_IN), lambda s, t, k: (t, lax.bitwise_xor(k[0], s))),
                       pl.BlockSpec((tl, DM), once), hbm, hbm],
            scratch_shapes=[pltpu.VMEM((NCHIP, DM, SHARD_IN), BF16), pltpu.VMEM((NCHIP, SHARD_OUT, DM), BF16),
                            pltpu.VMEM((SEQ, DM), BF16), pltpu.SemaphoreType.DMA((12,)), pltpu.SemaphoreType.DMA((12,))]),
        out_shape=[jax.ShapeDtypeStruct((SEQ, DIN), F32), jax.ShapeDtypeStruct((SEQ, DM), BF16),
                   jax.ShapeDtypeStruct((NCHIP, DM, SHARD_IN), BF16), jax.ShapeDtypeStruct((NCHIP, SHARD_OUT, DM), BF16)],
        compiler_params=_cparams(("arbitrary", "arbitrary"), 48 * 1024 * 1024),
    )(chip, x, shift, scale, norm_g, w_shard, wo_shard)


def ctx_fwd(ctx, cshift, cscale, norm_g, w_full):
    def kern(c_ref, sh_ref, sc_ref, g_ref, w2_ref, w3_ref, zc_ref, hc_ref):
        hc = _modulated(c_ref[...], g_ref[...], sc_ref[...], sh_ref[...]).astype(BF16)
        hc_ref[...] = hc
        zc_ref[:, :SHARD_IN] = jnp.dot(hc, w2_ref[0], preferred_element_type=F32)
        zc_ref[:, SHARD_IN:] = jnp.dot(hc, w3_ref[0], preferred_element_type=F32)

    return pl.pallas_call(
        kern, name="ctx_fwd", grid=(1,),
        in_specs=[pl.BlockSpec((CTX, DM), lambda i: (0, 0)), _row(DM), _row(DM), _row(DM),
                  pl.BlockSpec((1, DM, SHARD_IN), lambda i: (2, 0, 0)),
                  pl.BlockSpec((1, DM, SHARD_IN), lambda i: (3, 0, 0))],
        out_specs=[pl.BlockSpec((CTX, 2 * SHARD_IN), lambda i: (0, 0)),
                   pl.BlockSpec((CTX, DM), lambda i: (0, 0))],
        out_shape=[jax.ShapeDtypeStruct((CTX, 2 * SHARD_IN), F32), jax.ShapeDtypeStruct((CTX, DM), BF16)],
        compiler_params=_cparams(("arbitrary",)),
    )(ctx, cshift, cscale, norm_g, w_full, w_full)


SGU_CHUNK, SGU_PER_STEP = 128, 4


def _gelu(x):
    return 0.5 * x * (1.0 + lax.erf(x * 0.7071067811865476))


def _sgu_chunk(au, av, ag, sg, ws, bsb):
    u, v = _gelu(au), _gelu(av)
    outs = []
    for g in range(4):
        sl = slice(128 * g, 128 * (g + 1))
        mixed = mm(ws[g], _rms(v[:, sl], sg[:, sl])) + bsb[g]
        outs.append(u[:, sl] * mixed * jax.nn.silu(ag[:, sl]))
    return jnp.concatenate(outs, axis=-1)


def _sgu_specs():
    rows = SGU_CHUNK * SGU_PER_STEP
    zspec = lambda c: pl.BlockSpec((rows, 512), lambda n: (n, c))
    wspec = pl.BlockSpec((4, 128, 128), lambda n: (0, 0, 0))
    return rows, [zspec(0), zspec(1), zspec(2), _row(512), wspec, wspec]


def sgu_fwd(z, sg, ws, bsb):
    rows, in_specs = _sgu_specs()

    def kern(au_ref, av_ref, ag_ref, sg_ref, ws_ref, bs_ref, o_ref):
        for c in range(SGU_PER_STEP):
            sl = slice(c * SGU_CHUNK, (c + 1) * SGU_CHUNK)
            o_ref[sl, :] = _sgu_chunk(au_ref[sl, :], av_ref[sl, :], ag_ref[sl, :], sg_ref[...], ws_ref[...],
                                      bs_ref[...])

    return pl.pallas_call(
        kern, name="sgu_fwd", grid=(SEQ // rows,), in_specs=in_specs,
        out_specs=pl.BlockSpec((rows, 512), lambda n: (n, 0)),
        out_shape=jax.ShapeDtypeStruct((SEQ, 512), F32),
        compiler_params=_cparams(("arbitrary",)),
    )(z, z, z, sg, ws, bsb)


def sgu_bwd(z, sg, ws, bsb, dcat):
    rows, in_specs = _sgu_specs()

    def kern(au_ref, av_ref, ag_ref, sg_ref, ws_ref, bs_ref, do_ref, dz_ref, dsg_ref, dws_ref, dbs_ref):
        @pl.when(pl.program_id(0) == 0)
        def _():
            dsg_ref[...] = jnp.zeros_like(dsg_ref)
            dws_ref[...] = jnp.zeros_like(dws_ref)
            dbs_ref[...] = jnp.zeros_like(dbs_ref)

        for c in range(SGU_PER_STEP):
            sl = slice(c * SGU_CHUNK, (c + 1) * SGU_CHUNK)
            _, vjp = jax.vjp(_sgu_chunk, au_ref[sl, :], av_ref[sl, :], ag_ref[sl, :], sg_ref[...], ws_ref[...],
                             bs_ref[...])
            dau, dav, dag, dsg, dws, dbs = vjp(do_ref[sl, :])
            dz_ref[sl, 0:512] = dau.astype(BF16)
            dz_ref[sl, 512:1024] = dav.astype(BF16)
            dz_ref[sl, 1024:1536] = dag.astype(BF16)
            dsg_ref[...] += dsg
            dws_ref[...] += dws
            dbs_ref[...] += dbs

        @pl.when(pl.program_id(0) == pl.num_programs(0) - 1)
        def _():
            dbs_ref[...] = jnp.broadcast_to(jnp.sum(dbs_ref[...], axis=-1, keepdims=True), dbs_ref.shape)

    wspec = pl.BlockSpec((4, 128, 128), lambda n: (0, 0, 0))
    return pl.pallas_call(
        kern, name="sgu_bwd", grid=(SEQ // rows,),
        in_specs=in_specs + [pl.BlockSpec((rows, 512), lambda n: (n, 0))],
        out_specs=[pl.BlockSpec((rows, 1536), lambda n: (n, 0)), _row(512), wspec, wspec],
        out_shape=[jax.ShapeDtypeStruct((SEQ, 1536), BF16), jax.ShapeDtypeStruct((1, 512), F32),
                   jax.ShapeDtypeStruct((4, 128, 128), F32), jax.ShapeDtypeStruct((4, 128, 128), F32)],
        compiler_params=_cparams(("arbitrary",)),
    )(z, z, z, sg, ws, bsb, dcat)


_DR_OFF = (7, 3, -1)


def _row_valid(v, rr, j):
    return (j < 8, rr <= j < rr + 8, 4 <= j < 12)[v]


def _col_window():
    q = lax.broadcasted_iota(jnp.int32, (GRID_W, 128), 0)
    kc = lax.broadcasted_iota(jnp.int32, (GRID_W, 128), 1) % GRID_W
    c0 = jnp.clip(q - 8, 0, GRID_W - 16)
    return (kc >= c0) & (kc < c0 + 16)


def rpb_tables(rpb2):
    def kern(r_ref, b_ref):
        base = r_ref[0]
        lo = lax.broadcasted_iota(jnp.int32, (1, 128), 1) < GRID_W
        win = _col_window()
        neg = jnp.full((GRID_W, 128), NEG_INF, F32)
        for v in range(3):
            for rr in range(QROWS):
                for jp in range(KROWS // 2):
                    j0, j1 = 2 * jp, 2 * jp + 1
                    ok0, ok1 = _row_valid(v, rr, j0), _row_valid(v, rr, j1)
                    if not (ok0 or ok1):
                        tile = neg
                    else:
                        d0 = j0 - rr + _DR_OFF[v]
                        r0 = base[d0:d0 + 1, :] if ok0 else jnp.zeros((1, 128), F32)
                        r1 = base[d0 + 1:d0 + 2, :] if ok1 else jnp.zeros((1, 128), F32)
                        y = jnp.broadcast_to(jnp.where(lo, r0, r1), (GRID_W, 128))
                        y = pltpu.roll(pltpu.roll(y, 128 - 15, 1), 0, 1, stride=1, stride_axis=0)
                        ok = win & jnp.where(lo, ok0, ok1)
                        tile = jnp.where(ok, y, NEG_INF)
                    b_ref[v, 0, rr * GRID_W:(rr + 1) * GRID_W, jp * 128:(jp + 1) * 128] = tile

    return pl.pallas_call(
        kern, name="rpb_tables", grid=(HEADS,),
        in_specs=[pl.BlockSpec((1, 15, 128), lambda h: (h, 0, 0))],
        out_specs=pl.BlockSpec((3, 1, QBLK, KBLK), lambda h: (0, h, 0, 0)),
        out_shape=jax.ShapeDtypeStruct((3, HEADS, QBLK, KBLK), F32),
        compiler_params=_cparams(("arbitrary",)),
    )(rpb2)


def rpb_bwd(dbias):
    def kern(g_ref, o_ref):
        lo = lax.broadcasted_iota(jnp.int32, (1, 128), 1) < GRID_W
        ri = lax.broadcasted_iota(jnp.int32, (GRID_W, GRID_W), 0)
        ci = lax.broadcasted_iota(jnp.int32, (GRID_W, GRID_W), 1)
        flip = (ri + ci == GRID_W - 1).astype(F32)
        acc = [jnp.zeros((1, 128), F32) for _ in range(15)]
        for v in range(3):
            for rr in range(QROWS):
                for jp in range(KROWS // 2):
                    j0, j1 = 2 * jp, 2 * jp + 1
                    ok0, ok1 = _row_valid(v, rr, j0), _row_valid(v, rr, j1)
                    if not (ok0 or ok1):
                        continue
                    g = g_ref[v, 0, rr * GRID_W:(rr + 1) * GRID_W, jp * 128:(jp + 1) * 128]
                    g = lax.dot_general(flip, g, (((1,), (0,)), ((), ())), precision=lax.Precision.HIGHEST,
                                        preferred_element_type=F32)
                    g = pltpu.roll(pltpu.roll(g, 128 - 48, 1), 0, 1, stride=1, stride_axis=0)
                    s = jnp.sum(g, axis=0, keepdims=True)
                    d0 = j0 - rr + _DR_OFF[v]
                    if ok0:
                        acc[d0] = acc[d0] + jnp.where(lo, s, 0.0)
                    if ok1:
                        acc[d0 + 1] = acc[d0 + 1] + jnp.where(lo, 0.0, s)
        for d in range(15):
            o_ref[0, d:d + 1, :] = acc[d] + pltpu.roll(acc[d], GRID_W, 1)

    return pl.pallas_call(
        kern, name="rpb_bwd", grid=(HEADS,),
        in_specs=[pl.BlockSpec((3, 1, QBLK, KBLK), lambda h: (0, h, 0, 0))],
        out_specs=pl.BlockSpec((1, 15, 128), lambda h: (h, 0, 0)),
        out_shape=jax.ShapeDtypeStruct((HEADS, 15, 128), F32),
        compiler_params=_cparams(("arbitrary",)),
    )(dbias)


def _scaled_q(q_raw, qg):
    return _pair_rms(q_raw, qg) * (HDIM ** -0.5)


def _head_lanes():
    lo = lax.broadcasted_iota(jnp.int32, (1, 2 * HDIM), 1) < HDIM
    return lo, jnp.logical_not(lo)


def _attn_step(q_raw, kn, v, ckn, cv, bias2, qg):
    qn = _scaled_q(q_raw, qg)
    out = lse = None
    for a, mine in enumerate(_head_lanes()):
        qa = jnp.where(mine, qn, 0.0)
        s_lat = mm_nt(qa, kn) + bias2[a]
        s_ctx = mm_nt(qa, ckn)
        m = jnp.maximum(jnp.max(s_lat, axis=-1, keepdims=True), jnp.max(s_ctx, axis=-1, keepdims=True))
        p_lat = jnp.exp(s_lat - m)
        p_ctx = jnp.exp(s_ctx - m)
        den = jnp.sum(p_lat, axis=-1, keepdims=True) + jnp.sum(p_ctx, axis=-1, keepdims=True)
        o = jnp.where(mine, (mm(p_lat, v) + mm(p_ctx, cv)) / den, 0.0)
        l = jnp.where(mine, m + jnp.log(den), 0.0)
        out, lse = (o, l) if out is None else (out + o, lse + l)
    return out, lse


def _attn_step_bwd(q_raw, kn, v, ckn, cv, bias2, qg, bg, o, lse, dout):
    sig = jax.nn.sigmoid(bg)
    do = dout * (bg * sig)
    dbg = dout * o * (sig * (1.0 + bg * (1.0 - sig)))
    qn, qn_vjp = jax.vjp(_scaled_q, q_raw, qg)
    row_dot = do * o
    dqn = dkn = dv = dckn = dcv = None
    dbias = []
    for mine in _head_lanes():
        qa = jnp.where(mine, qn, 0.0)
        doa = jnp.where(mine, do, 0.0)
        l = jnp.max(jnp.where(mine, lse, NEG_INF), axis=-1, keepdims=True)
        delta = jnp.sum(jnp.where(mine, row_dot, 0.0), axis=-1, keepdims=True)
        p_lat = jnp.exp(mm_nt(qa, kn) + bias2[len(dbias)] - l)
        p_ctx = jnp.exp(mm_nt(qa, ckn) - l)
        ds_lat = p_lat * (mm_nt(doa, v) - delta)
        ds_ctx = p_ctx * (mm_nt(doa, cv) - delta)
        parts = (jnp.where(mine, mm(ds_lat, kn) + mm(ds_ctx, ckn), 0.0), mm_tn(ds_lat, qa), mm_tn(p_lat, doa),
                 mm_tn(ds_ctx, qa), mm_tn(p_ctx, doa))
        if dqn is None:
            dqn, dkn, dv, dckn, dcv = parts
        else:
            dqn, dkn, dv, dckn, dcv = (acc + new for acc, new in zip((dqn, dkn, dv, dckn, dcv), parts))
        dbias.append(ds_lat)
    dq, dqg = qn_vjp(dqn)
    return dq, dkn, dv, dckn, dcv, dbias, dqg, dbg


def _kstart(i):
    return pl.multiple_of(jnp.clip((i - 1) * QBLK, 0, SEQ - KBLK), QBLK)


def _bias_variant(i):
    return jnp.where(i == 0, 0, jnp.where(i == NQBLK - 1, 2, 1))


def _attn_in_specs():
    return [
        pl.BlockSpec((QBLK, 128), lambda p, i: (i, ZQ + p)),
        pl.BlockSpec((SEQ, 128), lambda p, i: (0, ZK + p)),
        pl.BlockSpec((SEQ, 128), lambda p, i: (0, ZV + p)),
        pl.BlockSpec((QBLK, 128), lambda p, i: (i, ZG + p)),
        pl.BlockSpec((CTX, 128), lambda p, i: (0, 2 + p)),
        pl.BlockSpec((CTX, 128), lambda p, i: (0, 6 + p)),
        pl.BlockSpec((1, 2, QBLK, KBLK), lambda p, i: (_bias_variant(i), p, 0, 0)),
        _row(128), _row(128),
    ]


NORM_ROWS = 512


def _norm_keys(k_ref, ck_ref, kg_ref, kn_scr, ckn_scr):
    def body(c, carry):
        sl = pl.ds(pl.multiple_of(c * NORM_ROWS, NORM_ROWS), NORM_ROWS)
        kn_scr[sl, :] = _pair_rms(k_ref[sl, :], kg_ref[...])
        return carry

    lax.fori_loop(0, SEQ // NORM_ROWS, body, 0)
    ckn_scr[...] = _pair_rms(ck_ref[...], kg_ref[...])


def attn_fwd(z, zc, bias, qg2, kg2):
    def kern(q_ref, k_ref, v_ref, bg_ref, ck_ref, cv_ref, b_ref, qg_ref, kg_ref, ob_ref, o_ref, lse_ref, kn_scr,
             ckn_scr):
        i = pl.program_id(1)

        @pl.when(i == 0)
        def _():
            _norm_keys(k_ref, ck_ref, kg_ref, kn_scr, ckn_scr)

        ks = pl.ds(_kstart(i), KBLK)
        o, lse = _attn_step(q_ref[...], kn_scr[ks, :], v_ref[ks, :], ckn_scr[...], cv_ref[...], b_ref[0], qg_ref[...])
        ob_ref[...] = o * jax.nn.silu(bg_ref[...])
        o_ref[...] = o
        lse_ref[...] = lse

    qblk = pl.BlockSpec((QBLK, 128), lambda p, i: (i, p))
    return pl.pallas_call(
        kern, name="attn_fwd", grid=(NPAIR, NQBLK), in_specs=_attn_in_specs(), out_specs=[qblk] * 3,
        out_shape=[jax.ShapeDtypeStruct((SEQ, 512), F32)] * 3,
        scratch_shapes=[pltpu.VMEM((SEQ, 128), F32), pltpu.VMEM((CTX, 128), F32)],
        compiler_params=_cparams(("arbitrary", "arbitrary"), 40 * 1024 * 1024),
    )(z, z, z, z, zc, zc, bias, qg2, kg2)


def attn_bwd(z, zc, bias, qg2, kg2, dcat, o_raw, lse):
    def kern(q_ref, k_ref, v_ref, bg_ref, ck_ref, cv_ref, b_ref, qg_ref, kg_ref, do_ref, o_ref, lse_ref,
             dq_ref, dk_ref, dv_ref, dbg_ref, dck_ref, dcv_ref, db_ref, dqg_ref, dkg_ref,
             kn_scr, ckn_scr, dkn_scr, dckn_scr, dv_scr):
        p, i = pl.program_id(0), pl.program_id(1)
        last = i == NQBLK - 1

        @pl.when(i == 0)
        def _():
            _norm_keys(k_ref, ck_ref, kg_ref, kn_scr, ckn_scr)
            dkn_scr[...] = jnp.zeros_like(dkn_scr)
            dv_scr[...] = jnp.zeros_like(dv_scr)
            dckn_scr[...] = jnp.zeros_like(dckn_scr)
            dcv_ref[...] = jnp.zeros_like(dcv_ref)

        @pl.when((i == 0) & (p == 0))
        def _():
            dqg_ref[...] = jnp.zeros_like(dqg_ref)
            dkg_ref[...] = jnp.zeros_like(dkg_ref)

        ks = pl.ds(_kstart(i), KBLK)
        dq, dkn, dv, dckn, dcv, db, dqg, dbg = _attn_step_bwd(
            q_ref[...], kn_scr[ks, :], v_ref[ks, :], ckn_scr[...], cv_ref[...], b_ref[0], qg_ref[...], bg_ref[...],
            o_ref[...], lse_ref[...], do_ref[...])
        dq_ref[...] = dq.astype(BF16)
        dbg_ref[...] = dbg.astype(BF16)
        dkn_scr[ks, :] += dkn
        dv_scr[ks, :] += dv
        dckn_scr[...] += dckn
        dcv_ref[...] += dcv
        dqg_ref[...] += dqg
        fresh = (i == 0) | (i == 1) | last

        @pl.when(fresh)
        def _():
            for a in range(2):
                db_ref[0, a] = db[a]

        @pl.when(jnp.logical_not(fresh))
        def _():
            for a in range(2):
                db_ref[0, a] += db[a]

        @pl.when(last)
        def _():
            def body(c, dkg):
                sl = pl.ds(pl.multiple_of(c * NORM_ROWS, NORM_ROWS), NORM_ROWS)
                _, nvjp = jax.vjp(_pair_rms, k_ref[sl, :], kg_ref[...])
                dk, dg = nvjp(dkn_scr[sl, :])
                dk_ref[sl, :] = dk.astype(BF16)
                dv_ref[sl, :] = dv_scr[sl, :].astype(BF16)
                return dkg + dg

            dkg = lax.fori_loop(0, SEQ // NORM_ROWS, body, jnp.zeros((1, 128), F32))
            _, nvjp = jax.vjp(_pair_rms, ck_ref[...], kg_ref[...])
            dck, dg = nvjp(dckn_scr[...])
            dck_ref[...] = dck
            dkg_ref[...] += dkg + dg

        @pl.when(last & (p == NPAIR - 1))
        def _():
            dqg_ref[...] = dqg_ref[...] + pltpu.roll(dqg_ref[...], HDIM, 1)
            dkg_ref[...] = dkg_ref[...] + pltpu.roll(dkg_ref[...], HDIM, 1)

    blk = lambda rows: pl.BlockSpec((rows, 128), lambda p, i: (0, p))
    qblk = pl.BlockSpec((QBLK, 128), lambda p, i: (i, p))
    return pl.pallas_call(
        kern, name="attn_bwd", grid=(NPAIR, NQBLK),
        in_specs=_attn_in_specs() + [pl.BlockSpec((QBLK, 128), lambda p, i: (i, 4 + p)), qblk, qblk],
        out_specs=[qblk, blk(SEQ), blk(SEQ), qblk, blk(CTX), blk(CTX),
                   pl.BlockSpec((1, 2, QBLK, KBLK), lambda p, i: (_bias_variant(i), p, 0, 0)),
                   _row(128), _row(128)],
        out_shape=[jax.ShapeDtypeStruct((SEQ, 512), BF16)] * 4 + [jax.ShapeDtypeStruct((CTX, 512), F32)] * 2
        + [jax.ShapeDtypeStruct((3, HEADS, QBLK, KBLK), F32), jax.ShapeDtypeStruct((1, 128), F32),
           jax.ShapeDtypeStruct((1, 128), F32)],
        scratch_shapes=[pltpu.VMEM((SEQ, 128), F32), pltpu.VMEM((CTX, 128), F32),
                        pltpu.VMEM((SEQ, 128), F32), pltpu.VMEM((CTX, 128), F32), pltpu.VMEM((SEQ, 128), F32)],
        compiler_params=_cparams(("arbitrary", "arbitrary"), VMEM_BIG),
    )(z, z, z, z, zc, zc, bias, qg2, kg2, dcat, o_raw, lse)


def outproj(out_a, out_b, x, target, gate, wo):
    tl = 512

    def kern(a_ref, b_ref, x_ref, t_ref, g_ref, w_ref, loss_ref, dy_ref, dcat_ref, dg_ref, dw_ref):
        @pl.when(pl.program_id(0) == 0)
        def _():
            loss_ref[...] = jnp.zeros_like(loss_ref)
            dg_ref[...] = jnp.zeros_like(dg_ref)
            dw_ref[...] = jnp.zeros_like(dw_ref)

        a, b = a_ref[...].astype(BF16), b_ref[...].astype(BF16)
        mix = (jnp.dot(a, w_ref[0:512, :], preferred_element_type=F32)
               + jnp.dot(b, w_ref[512:1024, :], preferred_element_type=F32))
        err = x_ref[...] + g_ref[...] * mix - t_ref[...]
        loss_ref[...] += 0.5 * jnp.sum(jnp.mean(err * err, axis=-1))
        dy = err * (1.0 / DM)
        dy_ref[...] = dy
        dg_ref[...] += jnp.sum(dy * mix, axis=0, keepdims=True)
        dmix = (g_ref[...] * dy).astype(BF16)
        dcat_ref[...] = lax.dot_general(dmix, w_ref[...], (((1,), (1,)), ((), ())), preferred_element_type=F32)
        dw_ref[0:512, :] += lax.dot_general(a, dmix, (((0,), (0,)), ((), ())), preferred_element_type=F32)
        dw_ref[512:1024, :] += lax.dot_general(b, dmix, (((0,), (0,)), ((), ())), preferred_element_type=F32)

    tile = lambda w: pl.BlockSpec((tl, w), lambda t: (t, 0))
    whole = pl.BlockSpec((DM, DM), lambda t: (0, 0))
    return pl.pallas_call(
        kern, name="outproj", grid=(SEQ // tl,),
        in_specs=[tile(512), tile(512), tile(DM), tile(DM), _row(DM), whole],
        out_specs=[pl.BlockSpec((8, 128), lambda t: (0, 0)), tile(DM), tile(DM), _row(DM), whole],
        out_shape=[jax.ShapeDtypeStruct((8, 128), F32), jax.ShapeDtypeStruct((SEQ, DM), F32),
                   jax.ShapeDtypeStruct((SEQ, DM), F32), jax.ShapeDtypeStruct((1, DM), F32),
                   jax.ShapeDtypeStruct((DM, DM), F32)],
        compiler_params=_cparams(("arbitrary",), 48 * 1024 * 1024),
    )(out_a, out_b, x, target, gate, wo)


def _pieces(sources):
    out = []
    for name, c0, c1 in sources:
        for j in range(NCHIP):
            lo, hi = max(c0, j * SHARD_IN), min(c1, (j + 1) * SHARD_IN)
            if lo < hi:
                out.append((j, lo - j * SHARD_IN, hi - j * SHARD_IN, name, lo - c0, hi - c0))
    return out


DZ_PIECES = _pieces((("a", 0, 1536), ("q", 1536, 2048), ("k", 2048, 2560), ("v", 2560, 3072), ("g", 3072, DIN)))
DZC_PIECES = _pieces((("k", 2048, 2560), ("v", 2560, 3072)))
_NT = (((1,), (1,)), ((), ()))


def _dz_specs(tl):
    return [pl.BlockSpec((tl, 1536), lambda t: (t, 0))] + [pl.BlockSpec((tl, 512), lambda t: (t, 0))] * 4


def dh_bwd(dz_parts, w_full, x, dy, shift, scale, norm_g, dg_ctx, wire_i, wire_o):
    tl = 512
    nt = SEQ // tl

    def kern(a_ref, q_ref, k_ref, v_ref, g_ref, w_ref, x_ref, dy_ref, sh_ref, sc_ref, gn_ref, dgc_ref, wi_hbm, wo_hbm,
             gx_ref, dsh_ref, dsc_ref, dg_ref, goti_ref, goto_ref, rcv_i, rcv_o, send_sems, recv_sems):
        def ici(n, q):
            wire, rcv = ((wi_hbm, rcv_i), (wo_hbm, rcv_o))[n]
            return _rcopy(wire.at[_chip_of(_flip(q))], rcv.at[q // 2 - 1], send_sems, recv_sems, 3 * n + q // 2 - 1,
                          _flip(q))

        @pl.when(pl.program_id(0) == 0)
        def _():
            for n in (0, 1):
                for q in (2, 4, 6):
                    ici(n, q).start()

        src = dict(a=a_ref, q=q_ref, k=k_ref, v=v_ref, g=g_ref)
        dh = None
        for j, l0, l1, name, s0, s1 in DZ_PIECES:
            part = lax.dot_general(src[name][:, s0:s1], w_ref[j, :, l0:l1], _NT, preferred_element_type=F32)
            dh = part if dh is None else dh + part

        @pl.when(pl.program_id(0) == 0)
        def _():
            dsh_ref[...] = jnp.zeros_like(dsh_ref)
            dsc_ref[...] = jnp.zeros_like(dsc_ref)
            dg_ref[...] = dgc_ref[...]

        _, vjp = jax.vjp(_modulated, x_ref[...], gn_ref[...], sc_ref[...], sh_ref[...])
        dx, dg, dsc, dsh = vjp(dh)
        gx_ref[...] = dy_ref[...] + dx
        dg_ref[...] += dg
        dsc_ref[...] += dsc
        dsh_ref[...] += dsh

        @pl.when(pl.program_id(0) == nt - 1)
        def _():
            for n in (0, 1):
                for q in (2, 4, 6):
                    ici(n, q).wait_recv()
                    ici(n, q).wait_send()
            goti_ref[...] = rcv_i[...]
            goto_ref[...] = rcv_o[...]

    tile = pl.BlockSpec((tl, DM), lambda t: (t, 0))
    hbm = pl.BlockSpec(memory_space=pl.ANY)
    got = [(NCHIP - 1, rh, w) for rh, w in RS_SHAPES]
    return pl.pallas_call(
        kern, name="dh_bwd", grid=(nt,),
        in_specs=_dz_specs(tl) + [pl.BlockSpec((NCHIP, DM, SHARD_IN), lambda t: (0, 0, 0)), tile, tile, _row(DM),
                                  _row(DM), _row(DM), _row(DM), hbm, hbm],
        out_specs=[tile, _row(DM), _row(DM), _row(DM)] + [pl.BlockSpec(s, lambda t: (0, 0, 0)) for s in got],
        out_shape=[jax.ShapeDtypeStruct((SEQ, DM), F32)] + [jax.ShapeDtypeStruct((1, DM), F32)] * 3
        + [jax.ShapeDtypeStruct(s, BF16) for s in got],
        scratch_shapes=[pltpu.VMEM(s, BF16) for s in got] + [pltpu.SemaphoreType.DMA((6,)), pltpu.SemaphoreType.DMA((6,))],
        compiler_params=_cparams(("arbitrary",), VMEM_BIG),
    )(*dz_parts, w_full, x, dy, shift, scale, norm_g, dg_ctx, wire_i, wire_o)


def dw_bwd(h, dz_parts, hc, dck, dcv):
    tl = 256

    def kern(h_ref, a_ref, q_ref, k_ref, v_ref, g_ref, hc_ref, dck_ref, dcv_ref, dw_ref):
        @pl.when(pl.program_id(0) == 0)
        def _():
            dw_ref[...] = jnp.zeros_like(dw_ref)
            hct = hc_ref[...].T
            csrc = dict(k=dck_ref, v=dcv_ref)
            for j, l0, l1, name, s0, s1 in DZC_PIECES:
                dw_ref[j, :, l0:l1] += jnp.dot(hct, csrc[name][:, s0:s1].astype(BF16), preferred_element_type=F32)

        ht = h_ref[...].T
        src = dict(a=a_ref, q=q_ref, k=k_ref, v=v_ref, g=g_ref)
        for j, l0, l1, name, s0, s1 in DZ_PIECES:
            dw_ref[j, :, l0:l1] += jnp.dot(ht, src[name][:, s0:s1], preferred_element_type=F32)

    whole = lambda r, c: pl.BlockSpec((r, c), lambda t: (0, 0))
    return pl.pallas_call(
        kern, name="dw_bwd", grid=(SEQ // tl,),
        in_specs=[pl.BlockSpec((tl, DM), lambda t: (t, 0))] + _dz_specs(tl) + [whole(CTX, DM), whole(CTX, 512),
                                                                              whole(CTX, 512)],
        out_specs=pl.BlockSpec((NCHIP, DM, SHARD_IN), lambda t: (0, 0, 0)),
        out_shape=jax.ShapeDtypeStruct((NCHIP, DM, SHARD_IN), F32),
        compiler_params=_cparams(("arbitrary",), VMEM_BIG),
    )(h, *dz_parts, hc, dck, dcv)


def ctx_bwd(dck, dcv, w_full, ctx, cshift, cscale, norm_g):
    def kern(dck_ref, dcv_ref, w_ref, c_ref, sh_ref, sc_ref, g_ref, dsh_ref, dsc_ref, dg_ref):
        csrc = dict(k=dck_ref, v=dcv_ref)
        dhc = None
        for j, l0, l1, name, s0, s1 in DZC_PIECES:
            part = lax.dot_general(csrc[name][:, s0:s1].astype(BF16), w_ref[j, :, l0:l1], _NT,
                                   preferred_element_type=F32)
            dhc = part if dhc is None else dhc + part
        _, vjp = jax.vjp(lambda g, sc, sh: _modulated(c_ref[...], g, sc, sh), g_ref[...], sc_ref[...], sh_ref[...])
        dg_ref[...], dsc_ref[...], dsh_ref[...] = vjp(dhc)

    whole = lambda r, c: pl.BlockSpec((r, c), lambda i: (0, 0))
    return pl.pallas_call(
        kern, name="ctx_bwd", grid=(1,),
        in_specs=[whole(CTX, 512), whole(CTX, 512), pl.BlockSpec((NCHIP, DM, SHARD_IN), lambda i: (0, 0, 0)),
                  whole(CTX, DM), _row(DM), _row(DM), _row(DM)],
        out_specs=[_row(DM), _row(DM), _row(DM)],
        out_shape=[jax.ShapeDtypeStruct((1, DM), F32)] * 3,
        compiler_params=_cparams(("arbitrary",), 40 * 1024 * 1024),
    )(dck, dcv, w_full, ctx, cshift, cscale, norm_g)


def _lane_pad_rpb(rpb):
    r = jnp.pad(rpb, ((0, 0), (0, 0), (0, GRID_W - rpb.shape[-1])))
    return jnp.concatenate([r, r], axis=-1)


def local_step(chip, x, ctx, target, mod, cmod, norm_g, sgu_g, w_s, b_s, q_g, k_g, rpb, w_in_shard, w_out_shard):
    shift, scale, gate = mod[:, :DM], mod[:, DM:2 * DM], mod[:, 2 * DM:]
    cshift, cscale = cmod[:, :DM], cmod[:, DM:2 * DM]
    bsb = jnp.broadcast_to(b_s[:, :, None], (4, 128, 128))
    qg2, kg2 = jnp.tile(q_g, (1, 2)), jnp.tile(k_g, (1, 2))

    z, h, w_in_full, w_out_full = inproj_fwd(chip, x, shift, scale, norm_g, w_in_shard, w_out_shard)
    zc, hc = ctx_fwd(ctx, cshift, cscale, norm_g, w_in_full)
    bias = rpb_tables(_lane_pad_rpb(rpb))
    out_a = sgu_fwd(z, sgu_g, w_s, bsb)
    out_b, o_raw, lse = attn_fwd(z, zc, bias, qg2, kg2)
    loss8, dy, dcat, dgate, dwo = outproj(out_a, out_b, x, target, gate, w_out_full.reshape(DM, DM))
    dz_a, dsg, dws, dbsb = sgu_bwd(z, sgu_g, w_s, bsb, dcat)
    dq, dk, dv, dbg, dck, dcv, dbias, dqg2, dkg2 = attn_bwd(z, zc, bias, qg2, kg2, dcat, o_raw, lse)
    drpb = rpb_bwd(dbias)[:, :, :rpb.shape[-1]]
    dz_parts = (dz_a, dq, dk, dv, dbg)
    dcshift, dcscale, dng_c = ctx_bwd(dck, dcv, w_in_full, ctx, cshift, cscale, norm_g)
    dw_in = dw_bwd(h, dz_parts, hc, dck, dcv)
    wire_i, keep_i, wire_o, keep_o = pair_sum(dw_in, dwo.reshape(NCHIP, SHARD_OUT, DM))
    grad_x, dshift, dscale, dng, got_i, got_o = dh_bwd(dz_parts, w_in_full, x, dy, shift, scale, norm_g, dng_c,
                                                       wire_i, wire_o)
    return dict(
        loss=loss8[0:1, 0:1], grad_x=grad_x, rs=(keep_i, got_i, keep_o, got_o),
        dmod=jnp.concatenate([dshift, dscale, dgate], axis=-1),
        dcmod=jnp.concatenate([dcshift, dcscale, jnp.zeros((1, DM), F32)], axis=-1),
        d_norm_g=dng, d_sgu_g=dsg, d_w_s=dws, d_b_s=dbsb[:, :, 0],
        d_q_g=dqg2[:, :HDIM], d_k_g=dkg2[:, :HDIM], d_rpb=drpb)


def _me():
    return lax.axis_index("x"), lax.axis_index("y"), lax.axis_index("c")


def _flip(q):
    x, y, c = _me()
    return ((1 - x) if q & 4 else x, (1 - y) if q & 2 else y, (1 - c) if q & 1 else c)


def _chip_of(dev):
    return 2 * dev[0] + dev[1]


def _rcopy(src, dst, send_sems, recv_sems, k, dev):
    return pltpu.make_async_remote_copy(src_ref=src, dst_ref=dst, send_sem=send_sems.at[k], recv_sem=recv_sems.at[k],
                                        device_id=dev, device_id_type=MESH_ID)


_VMEM_SPEC = pl.BlockSpec(memory_space=pltpu.VMEM)
CS_ROWS = 8 * NDEV + 8


def ada_fwd(c, c_ctx, w_ada, b_shard):
    n_c = NDEV - 1

    def kern(c_ref, cc_ref, wa_ref, b_ref, mod_ref, cs_ref, mine, send_sems, recv_sems):
        x, y, cc = _me()
        k, me = 2 * x + y, 4 * x + 2 * y + cc
        slot = lambda d: pl.ds(pl.multiple_of(8 * d, 8), 8)
        first = lax.broadcasted_iota(jnp.int32, (8, DM), 0) == 0
        mine[...] = jnp.where(first, jnp.broadcast_to(c_ref[...], (8, DM)), 0.0)
        cs_ref[slot(me), :] = mine[...]
        cs_ref[slot(NDEV), :] = jnp.where(first, jnp.broadcast_to(cc_ref[...], (8, DM)), 0.0)
        csends = [_rcopy(mine, cs_ref.at[slot(me), :], send_sems, recv_sems, q - 1, _flip(q)) for q in range(1, NDEV)]
        for cp in csends:
            cp.start()
        wa = wa_ref[...].astype(BF16)
        for q in range(1, NDEV):
            px, py, pc = _flip(q)
            _rcopy(mine, cs_ref.at[slot(4 * px + 2 * py + pc), :], send_sems, recv_sems, q - 1, _flip(q)).wait_recv()
        act = jax.nn.silu(cs_ref[...]).astype(BF16)
        mod_ref[k] = jnp.dot(act, wa, preferred_element_type=F32) + b_ref[...]
        msends = [_rcopy(mod_ref.at[k], mod_ref.at[k], send_sems, recv_sems, n_c + q // 2 - 1, _flip(q))
                  for q in (2, 4, 6)]
        for cp in msends:
            cp.start()
        for q in (2, 4, 6):
            kq = _chip_of(_flip(q))
            _rcopy(mod_ref.at[kq], mod_ref.at[kq], send_sems, recv_sems, n_c + q // 2 - 1, _flip(q)).wait_recv()
        for cp in csends + msends:
            cp.wait_send()

    return pl.pallas_call(
        kern, name="ada_fwd", in_specs=[_VMEM_SPEC] * 4, out_specs=[_VMEM_SPEC] * 2,
        out_shape=[jax.ShapeDtypeStruct((NCHIP, CS_ROWS, SHARD_ADA), F32), jax.ShapeDtypeStruct((CS_ROWS, DM), F32)],
        scratch_shapes=[pltpu.VMEM((8, DM), F32), pltpu.SemaphoreType.DMA((n_c + 3,)),
                        pltpu.SemaphoreType.DMA((n_c + 3,))],
    )(c, c_ctx, w_ada, b_shard)


SLAB_ROWS = 80


RS_SHAPES = ((DM // 2, SHARD_IN), (SHARD_OUT // 2, DM))


def pair_sum(g_in, g_out):
    def kern(gi_hbm, go_hbm, wire_i, keep_i, wire_o, keep_o, mine_i, rcv_i, mine_o, rcv_o, load_sems, send_sems,
             recv_sems):
        x, y, c = _me()
        k = 2 * x + y
        sib = _flip(1)
        work = ((gi_hbm, mine_i, rcv_i, wire_i, keep_i), (go_hbm, mine_o, rcv_o, wire_o, keep_o))
        copies = []
        for n, (g, mine, rcv, _, _) in enumerate(work):
            rh = RS_SHAPES[n][0]
            half = lambda hh, rh=rh: pl.ds(pl.multiple_of(hh * rh, rh), rh)
            load = pltpu.make_async_copy(g.at[:, half(c), :], mine, load_sems.at[n])
            load.start()
            pair = _rcopy(g.at[:, half(1 - c), :], rcv, send_sems, recv_sems, n, sib)
            pair.start()
            copies.append((load, pair))
        for (load, pair), (_, mine, rcv, wire, keep) in zip(copies, work):
            load.wait()
            pair.wait_recv()
            for j in range(NCHIP):
                wire[j] = (mine[j] + rcv[j]).astype(BF16)
            keep[...] = mine[k] + rcv[k]
        for _, pair in copies:
            pair.wait_send()

    (rhi, wi), (rho, wo) = RS_SHAPES
    hbm = pl.BlockSpec(memory_space=pl.ANY)
    return pl.pallas_call(
        kern, name="pair_sum", in_specs=[hbm, hbm], out_specs=[_VMEM_SPEC] * 4,
        out_shape=[jax.ShapeDtypeStruct((NCHIP, rhi, wi), BF16), jax.ShapeDtypeStruct((rhi, wi), F32),
                   jax.ShapeDtypeStruct((NCHIP, rho, wo), BF16), jax.ShapeDtypeStruct((rho, wo), F32)],
        scratch_shapes=[pltpu.VMEM((NCHIP, rhi, wi), F32), pltpu.VMEM((NCHIP, rhi, wi), F32),
                        pltpu.VMEM((NCHIP, rho, wo), F32), pltpu.VMEM((NCHIP, rho, wo), F32),
                        pltpu.SemaphoreType.DMA((2,)), pltpu.SemaphoreType.DMA((2,)), pltpu.SemaphoreType.DMA((2,))],
        compiler_params=pltpu.CompilerParams(vmem_limit_bytes=48 * 1024 * 1024),
    )(g_in, g_out)


def final_reduce(keep_i, got_i, keep_o, got_o, slab):
    def kern(ki_ref, gi_ref, ko_ref, go_ref, s_ref, gin_ref, gout_ref, all_ref, tot_ref, send_sems, recv_sems):
        x, y, c = _me()
        sib = _flip(1)
        dev = lambda d: 4 * d[0] + 2 * d[1] + d[2]
        me = dev((x, y, c))

        def slab_copy(idx, owner, to):
            return _rcopy(all_ref.at[dev(owner)], all_ref.at[dev(owner)], send_sems, recv_sems, idx, to)

        all_ref[me] = s_ref[...]
        first = [slab_copy(0, (x, y, c), sib)] + [slab_copy(q // 2, (x, y, c), _flip(q)) for q in (2, 4, 6)]
        for cp in first:
            cp.start()

        shares = []
        for n, (keep, got, out) in enumerate(((ki_ref, gi_ref, gin_ref), (ko_ref, go_ref, gout_ref))):
            rh = RS_SHAPES[n][0]
            half = lambda hh, rh=rh: pl.ds(pl.multiple_of(hh * rh, rh), rh)
            out[half(c), :] = ((keep[...] + got[0].astype(F32)) + got[1].astype(F32)) + got[2].astype(F32)
            share = _rcopy(out.at[half(c), :], out.at[half(c), :], send_sems, recv_sems, 7 + n, sib)
            share.start()
            shares.append((share, _rcopy(out.at[half(1 - c), :], out.at[half(1 - c), :], send_sems, recv_sems, 7 + n,
                                         sib)))

        passed = []
        for q in (2, 4, 6):
            slab_copy(q // 2, _flip(q), (x, y, c)).wait_recv()
            cp = slab_copy(3 + q // 2, _flip(q), sib)
            cp.start()
            passed.append(cp)
        slab_copy(0, sib, (x, y, c)).wait_recv()
        for q in (2, 4, 6):
            slab_copy(3 + q // 2, _flip(q | 1), (x, y, c)).wait_recv()
        tot = all_ref[0]
        for d in range(1, NDEV):
            tot = tot + all_ref[d]
        tot_ref[...] = tot
        for share, arrival in shares:
            arrival.wait_recv()
            share.wait_send()
        for cp in first + passed:
            cp.wait_send()

    (rhi, wi), (rho, wo) = RS_SHAPES
    return pl.pallas_call(
        kern, name="final_reduce", in_specs=[_VMEM_SPEC] * 5, out_specs=[_VMEM_SPEC] * 4,
        out_shape=[jax.ShapeDtypeStruct((2 * rhi, wi), F32), jax.ShapeDtypeStruct((2 * rho, wo), F32),
                   jax.ShapeDtypeStruct((NDEV, SLAB_ROWS, DM), F32), jax.ShapeDtypeStruct((SLAB_ROWS, DM), F32)],
        scratch_shapes=[pltpu.SemaphoreType.DMA((9,)), pltpu.SemaphoreType.DMA((9,))],
        compiler_params=pltpu.CompilerParams(vmem_limit_bytes=40 * 1024 * 1024),
    )(keep_i, got_i, keep_o, got_o, slab)


def ada_bwd(a_in, dm, dm_shard, w_ada, c_ctx):
    def kern(a_ref, dm_ref, dms_ref, w_ref, cc_ref, dw_ref, db_ref, dcc_ref, parts, send_sems, recv_sems):
        x, y, c = _me()
        k = 2 * x + y
        act = jax.nn.silu(a_ref[...]).astype(BF16)
        dms = dms_ref[...].astype(BF16)
        dw_ref[...] = lax.dot_general(act, dms, (((0,), (0,)), ((), ())), preferred_element_type=F32)
        db_ref[...] = jnp.sum(dm_ref[...], axis=0, keepdims=True)
        parts[k] = lax.dot_general(dms, w_ref[...].astype(BF16), (((1,), (1,)), ((), ())), preferred_element_type=F32)
        sends = [_rcopy(parts.at[k], parts.at[k], send_sems, recv_sems, q // 2 - 1, _flip(q)) for q in (2, 4, 6)]
        for cp in sends:
            cp.start()
        for q in (2, 4, 6):
            kq = _chip_of(_flip(q))
            _rcopy(parts.at[kq], parts.at[kq], send_sems, recv_sems, q // 2 - 1, _flip(q)).wait_recv()
        dact = ((parts[0] + parts[1]) + parts[2]) + parts[3]
        _, vjp = jax.vjp(jax.nn.silu, cc_ref[...])
        dcc_ref[...] = vjp(dact[8:9, :])[0]
        for cp in sends:
            cp.wait_send()

    return pl.pallas_call(
        kern, name="ada_bwd", in_specs=[_VMEM_SPEC] * 5, out_specs=[_VMEM_SPEC] * 3,
        out_shape=[jax.ShapeDtypeStruct((DM, SHARD_ADA), F32), jax.ShapeDtypeStruct((1, 3 * DM), F32),
                   jax.ShapeDtypeStruct((1, DM), F32)],
        scratch_shapes=[pltpu.VMEM((NCHIP, 16, DM), F32), pltpu.SemaphoreType.DMA((3,)), pltpu.SemaphoreType.DMA((3,))],
    )(a_in, dm, dm_shard, w_ada, c_ctx)


def _adamw_math(w, g, m, v):
    m = B1 * m + (1.0 - B1) * g
    v = B2 * v + (1.0 - B2) * (g * g)
    m_hat = m / (1.0 - B1 ** STEP)
    v_hat = v / (1.0 - B2 ** STEP)
    return -LR * (m_hat / (jnp.sqrt(v_hat) + ADAM_EPS) + WD * w), m, v


def adamw_big(w, g, m, v, name, block_rows=256):
    rows, width = w.shape

    def kern(w_ref, g_ref, m_ref, v_ref, d_ref, nm_ref, nv_ref):
        d_ref[...], nm_ref[...], nv_ref[...] = _adamw_math(w_ref[...], g_ref[...], m_ref[...], v_ref[...])

    spec = pl.BlockSpec((block_rows, width), lambda i: (i, 0))
    return pl.pallas_call(
        kern, name=name, grid=(rows // block_rows,), in_specs=[spec] * 4, out_specs=[spec] * 3,
        out_shape=[jax.ShapeDtypeStruct((rows, width), F32)] * 3,
        compiler_params=_cparams(("arbitrary",)),
    )(w, g, m, v)


def adamw_small(quads):
    n = len(quads)

    def kern(*refs):
        ins, outs = refs[:4 * n], refs[4 * n:]
        for i in range(n):
            w, g, m, v = (r[...] for r in ins[4 * i:4 * i + 4])
            outs[3 * i][...], outs[3 * i + 1][...], outs[3 * i + 2][...] = _adamw_math(w, g, m, v)

    flat = [a for quad in quads for a in quad]
    res = pl.pallas_call(
        kern, name="adamw_small", in_specs=[_VMEM_SPEC] * (4 * n), out_specs=[_VMEM_SPEC] * (3 * n),
        out_shape=[jax.ShapeDtypeStruct(q[0].shape, F32) for q in quads for _ in range(3)],
    )(*flat)
    return [tuple(res[3 * i:3 * i + 3]) for i in range(n)]


def _rows_of(a, rows):
    flat = a.reshape(-1)
    return jnp.pad(flat, (0, rows * DM - flat.shape[0])).reshape(rows, DM)


def kernel(x, c, ctx, c_ctx, w_ada, b_ada, norm_g, w_in, sgu_norm_g, w_spatial, b_spatial, q_norm_g, k_norm_g, rpb, w_out, loss_target, m_c_ctx, m_w_ada, m_b_ada, m_norm_g, m_w_in, m_sgu_norm_g, m_w_spatial, m_b_spatial, m_q_norm_g, m_k_norm_g, m_rpb, m_w_out, v_c_ctx, v_w_ada, v_b_ada, v_norm_g, v_w_in, v_sgu_norm_g, v_w_spatial, v_b_spatial, v_q_norm_g, v_k_norm_g, v_rpb, v_w_out):
    xi, yi, ci = lax.axis_index("x"), lax.axis_index("y"), lax.axis_index("c")
    chip, dev = 2 * xi + yi, 4 * xi + 2 * yi + ci
    c_ctx2 = c_ctx.reshape(1, DM)

    b_shard = lax.dynamic_slice(b_ada, (0, chip * SHARD_ADA), (1, SHARD_ADA))
    mod_all, cs = ada_fwd(c, c_ctx2, w_ada[0], b_shard)
    mods = mod_all.transpose(1, 0, 2).reshape(CS_ROWS, 3 * DM)
    mod = lax.dynamic_slice(mods, (8 * dev, 0), (1, 3 * DM))
    cmod = mods[8 * NDEV:8 * NDEV + 1]

    part = local_step(chip.reshape(1).astype(jnp.int32), x[0], ctx[0], loss_target[0], mod, cmod, norm_g, sgu_norm_g,
                      w_spatial[0], b_spatial[0], q_norm_g, k_norm_g, rpb[0], w_in[0], w_out[0])

    slab = jnp.concatenate([
        part["d_norm_g"], _rows_of(part["d_sgu_g"], 1), _rows_of(part["d_b_s"], 1),
        _rows_of(jnp.concatenate([part["d_q_g"], part["d_k_g"]], axis=-1), 1), _rows_of(part["d_rpb"], 4),
        _rows_of(part["loss"], 1), _rows_of(part["dcmod"], 3), _rows_of(part["dmod"], 3), jnp.zeros((1, DM), F32),
        _rows_of(part["d_w_s"], 64)], axis=0)
    g_w_in, g_w_out, gathered, tot = final_reduce(*part["rs"], slab)
    dm = jnp.concatenate([gathered[:, 12:15, :].reshape(NDEV, 3 * DM), tot[9:12].reshape(1, 3 * DM),
                          jnp.zeros((7, 3 * DM), F32)], axis=0)
    a_in = jnp.concatenate([cs[0:8 * NDEV:8], cs[8 * NDEV:8 * NDEV + 1], jnp.zeros((7, DM), F32)], axis=0)
    dm_shard = lax.dynamic_slice(dm, (0, chip * SHARD_ADA), (16, SHARD_ADA))
    g_w_ada, g_b_ada, g_c_ctx = ada_bwd(a_in, dm, dm_shard, w_ada[0], c_ctx2)

    loss = tot[8, 0]
    g_small = dict(
        c_ctx=g_c_ctx, b_ada=g_b_ada, norm_g=tot[0:1], sgu_norm_g=tot[1:2, :512], w_spatial=tot[16:80].reshape(512, 128),
        b_spatial=tot[2:3, :512].reshape(4, 128), q_norm_g=tot[3:4, :HDIM], k_norm_g=tot[3:4, HDIM:2 * HDIM],
        rpb=tot[4:8].reshape(-1)[:HEADS * 15 * 31].reshape(HEADS * 15, 31))
    shapes = dict(c_ctx=(DM,), w_ada=(1, DM, SHARD_ADA), b_ada=(1, 3 * DM), norm_g=(1, DM), w_in=(1, DM, SHARD_IN),
                  sgu_norm_g=(1, 512), w_spatial=(1, 4, 128, 128), b_spatial=(1, 4, 128), q_norm_g=(1, HDIM),
                  k_norm_g=(1, HDIM), rpb=(1, HEADS, 15, 31), w_out=(1, SHARD_OUT, DM))
    names = list(shapes)
    weights = dict(c_ctx=c_ctx, w_ada=w_ada, b_ada=b_ada, norm_g=norm_g, w_in=w_in, sgu_norm_g=sgu_norm_g,
                   w_spatial=w_spatial, b_spatial=b_spatial, q_norm_g=q_norm_g, k_norm_g=k_norm_g, rpb=rpb, w_out=w_out)
    m_in = dict(zip(names, (m_c_ctx, m_w_ada, m_b_ada, m_norm_g, m_w_in, m_sgu_norm_g, m_w_spatial, m_b_spatial,
                            m_q_norm_g, m_k_norm_g, m_rpb, m_w_out)))
    v_in = dict(zip(names, (v_c_ctx, v_w_ada, v_b_ada, v_norm_g, v_w_in, v_sgu_norm_g, v_w_spatial, v_b_spatial,
                            v_q_norm_g, v_k_norm_g, v_rpb, v_w_out)))
    grads = dict(g_small, w_ada=g_w_ada, w_in=g_w_in, w_out=g_w_out)
    upd = {}
    for n in ("w_ada", "w_in", "w_out"):
        g = grads[n]
        upd[n] = adamw_big(weights[n].reshape(g.shape), g, m_in[n].reshape(g.shape), v_in[n].reshape(g.shape),
                           "adamw_" + n)
    small = [n for n in names if n not in upd]
    res = adamw_small([(weights[n].reshape(grads[n].shape), grads[n], m_in[n].reshape(grads[n].shape),
                        v_in[n].reshape(grads[n].shape)) for n in small])
    upd.update(zip(small, res))
    out = [loss, part["grad_x"].reshape(1, SEQ, DM)]
    out += [grads[n].reshape(shapes[n]) for n in names]
    for slot in range(3):
        out += [upd[n][slot].reshape(shapes[n]) for n in names]
    return tuple(out)
```

```python
import jax
import jax.numpy as jnp
from jax import lax
from jax.experimental import pallas as pl
from jax.experimental.pallas import tpu as pltpu

F32, BF16 = jnp.float32, jnp.bfloat16
SEQ, DM, CTX, DIN = 4096, 1024, 256, 3584
NCHIP, NDEV = 4, 8
SHARD_IN = DIN // NCHIP
SHARD_ADA = 3 * DM // NCHIP
SHARD_OUT = DM // NCHIP
GRID_W = 64
QROWS = 4
KROWS = 12
QBLK, KBLK = QROWS * GRID_W, KROWS * GRID_W
NQBLK = SEQ // QBLK
HEADS, HDIM, NPAIR = 8, 64, 4
EPS = 1e-6
NEG_INF = -1e30
ZQ, ZK, ZV, ZG = 12, 16, 20, 24
LR, B1, B2, ADAM_EPS, WD, STEP = 0.001, 0.9, 0.999, 1e-08, 0.01, 10
VMEM_BIG = 56 * 1024 * 1024
MESH_ID = pl.DeviceIdType.MESH


def _dot(a, b, lhs_c, rhs_c):
    return lax.dot_general(a.astype(BF16), b.astype(BF16), (((lhs_c,), (rhs_c,)), ((), ())),
                           preferred_element_type=F32)


@jax.custom_vjp
def mm(a, b):
    return _dot(a, b, 1, 0)


@jax.custom_vjp
def mm_nt(a, b):
    return _dot(a, b, 1, 1)


@jax.custom_vjp
def mm_tn(a, b):
    return _dot(a, b, 0, 0)


mm.defvjp(lambda a, b: (mm(a, b), (a, b)), lambda r, ct: (mm_nt(ct, r[1]), mm_tn(r[0], ct)))
mm_nt.defvjp(lambda a, b: (mm_nt(a, b), (a, b)), lambda r, ct: (mm(ct, r[1]), mm_tn(ct, r[0])))
mm_tn.defvjp(lambda a, b: (mm_tn(a, b), (a, b)), lambda r, ct: (mm_nt(r[1], ct), mm(r[0], ct)))


def _rms(x, g):
    return x * lax.rsqrt(jnp.mean(x * x, axis=-1, keepdims=True) + EPS) * g


def _modulated(x, g, scale, shift):
    return _rms(x, g) * (1.0 + scale) + shift


def _pair_rms(x, g2):
    lo = lax.broadcasted_iota(jnp.int32, (1, 2 * HDIM), 1) < HDIM
    sq = x * x
    s_lo = jnp.sum(jnp.where(lo, sq, 0.0), axis=-1, keepdims=True)
    s_hi = jnp.sum(jnp.where(lo, 0.0, sq), axis=-1, keepdims=True)
    rs = jnp.where(lo, lax.rsqrt(s_lo / HDIM + EPS), lax.rsqrt(s_hi / HDIM + EPS))
    return x * rs * g2


def _cparams(sem, vmem=None):
    return pltpu.CompilerParams(dimension_semantics=sem, vmem_limit_bytes=vmem)


def _row(n):
    return pl.BlockSpec((1, n), lambda *_: (0, 0))


def inproj_fwd(chip, x, shift, scale, norm_g, w_shard, wo_shard):
    tl = 1024
    nt = SEQ // tl
    halves = (DM // 2, SHARD_OUT // 2)

    def kern(k_ref, x_ref, sh_ref, sc_ref, g_ref, w_ref, wo_ref, z_ref, h_ref, wfull_ref, wofull_ref,
             w_scr, wo_scr, h_scr, send_sems, recv_sems):
        s, t = pl.program_id(0), pl.program_id(1)
        xi, yi, c = _me()
        k = 2 * xi + yi
        sib = _flip(1)
        rows = pl.ds(pl.multiple_of(t * tl, tl), tl)
        gathered = (w_scr, wo_scr)

        def block(n, chip_of_block, hh):
            return gathered[n].at[chip_of_block, pl.ds(pl.multiple_of(hh * halves[n], halves[n]), halves[n]), :]

        def ici(n, q, chip_of_block):
            blk = block(n, chip_of_block, c)
            return _rcopy(blk, blk, send_sems, recv_sems, 6 * n + q // 2 - 1, _flip(q))

        def d2d(n, q, chip_of_block, hh):
            blk = block(n, chip_of_block, hh)
            return _rcopy(blk, blk, send_sems, recv_sems, 6 * n + 3 + q // 2 - 1, sib)

        @pl.when((s == 0) & (t == 0))
        def _():
            w_scr[k] = w_ref[...].astype(BF16)
            wo_scr[k] = wo_ref[...].astype(BF16)
            for q in (2, 4, 6):
                ici(0, q, k).start()
                ici(1, q, k).start()

        for sweep in (1, 2, 3):
            @pl.when((s == sweep) & (t == 0))
            def _():
                q = 2 * sweep
                src = _chip_of(_flip(q))
                for n in (0, 1):
                    ici(n, q, src).wait_recv()
                    d2d(n, q, src, c).start()
                for n in (0, 1):
                    d2d(n, q, src, 1 - c).wait_recv()

        @pl.when(s == 0)
        def _():
            hb = _modulated(x_ref[...], g_ref[...], sc_ref[...], sh_ref[...]).astype(BF16)
            h_scr[rows, :] = hb
            h_ref[...] = hb

        z_ref[...] = jnp.dot(h_scr[rows, :], w_scr[lax.bitwise_xor(k, s)], preferred_element_type=F32)

        @pl.when((s == NCHIP - 1) & (t == nt - 1))
        def _():
            for n in (0, 1):
                for q in (2, 4, 6):
                    ici(n, q, k).wait_send()
                    d2d(n, q, _chip_of(_flip(q)), c).wait_send()
            pltpu.sync_copy(w_scr, wfull_ref)
            pltpu.sync_copy(wo_scr, wofull_ref)

    once = lambda s, t, k: (jnp.where(s == 0, t, nt - 1), 0)
    row = lambda n: pl.BlockSpec((1, n), lambda s, t, k: (0, 0))
    hbm = pl.BlockSpec(memory_space=pl.ANY)
    return pl.pallas_call(
        kern, name="inproj_fwd",
        grid_spec=pltpu.PrefetchScalarGridSpec(
            num_scalar_prefetch=1, grid=(NCHIP, nt),
            in_specs=[pl.BlockSpec((tl, DM), once), row(DM), row(DM), row(DM), _VMEM_SPEC, _VMEM_SPEC],
            out_specs=[pl.BlockSpec((tl, SHARD_IN), lambda s, t, k: (t, lax.bitwise_xor(k[0], s))),
                       pl.BlockSpec((tl, DM), once), hbm, hbm],
            scratch_shapes=[pltpu.VMEM((NCHIP, DM, SHARD_IN), BF16), pltpu.VMEM((NCHIP, SHARD_OUT, DM), BF16),
                            pltpu.VMEM((SEQ, DM), BF16), pltpu.SemaphoreType.DMA((12,)), pltpu.SemaphoreType.DMA((12,))]),
        out_shape=[jax.ShapeDtypeStruct((SEQ, DIN), F32), jax.ShapeDtypeStruct((SEQ, DM), BF16),
                   jax.ShapeDtypeStruct((NCHIP, DM, SHARD_IN), BF16), jax.ShapeDtypeStruct((NCHIP, SHARD_OUT, DM), BF16)],
        compiler_params=_cparams(("arbitrary", "arbitrary"), 48 * 1024 * 1024),
    )(chip, x, shift, scale, norm_g, w_shard, wo_shard)


def ctx_fwd(ctx, cshift, cscale, norm_g, w_full):
    def kern(c_ref, sh_ref, sc_ref, g_ref, w2_ref, w3_ref, zc_ref, hc_ref):
        hc = _modulated(c_ref[...], g_ref[...], sc_ref[...], sh_ref[...]).astype(BF16)
        hc_ref[...] = hc
        zc_ref[:, :SHARD_IN] = jnp.dot(hc, w2_ref[0], preferred_element_type=F32)
        zc_ref[:, SHARD_IN:] = jnp.dot(hc, w3_ref[0], preferred_element_type=F32)

    return pl.pallas_call(
        kern, name="ctx_fwd", grid=(1,),
        in_specs=[pl.BlockSpec((CTX, DM), lambda i: (0, 0)), _row(DM), _row(DM), _row(DM),
                  pl.BlockSpec((1, DM, SHARD_IN), lambda i: (2, 0, 0)),
                  pl.BlockSpec((1, DM, SHARD_IN), lambda i: (3, 0, 0))],
        out_specs=[pl.BlockSpec((CTX, 2 * SHARD_IN), lambda i: (0, 0)),
                   pl.BlockSpec((CTX, DM), lambda i: (0, 0))],
        out_shape=[jax.ShapeDtypeStruct((CTX, 2 * SHARD_IN), F32), jax.ShapeDtypeStruct((CTX, DM), BF16)],
        compiler_params=_cparams(("arbitrary",)),
    )(ctx, cshift, cscale, norm_g, w_full, w_full)


SGU_CHUNK, SGU_PER_STEP = 128, 4


def _gelu(x):
    return 0.5 * x * (1.0 + lax.erf(x * 0.7071067811865476))


def _sgu_chunk(au, av, ag, sg, ws, bsb):
    u, v = _gelu(au), _gelu(av)
    outs = []
    for g in range(4):
        sl = slice(128 * g, 128 * (g + 1))
        mixed = mm(ws[g], _rms(v[:, sl], sg[:, sl])) + bsb[g]
        outs.append(u[:, sl] * mixed * jax.nn.silu(ag[:, sl]))
    return jnp.concatenate(outs, axis=-1)


def _sgu_specs():
    rows = SGU_CHUNK * SGU_PER_STEP
    zspec = lambda c: pl.BlockSpec((rows, 512), lambda n: (n, c))
    wspec = pl.BlockSpec((4, 128, 128), lambda n: (0, 0, 0))
    return rows, [zspec(0), zspec(1), zspec(2), _row(512), wspec, wspec]


def sgu_fwd(z, sg, ws, bsb):
    rows, in_specs = _sgu_specs()

    def kern(au_ref, av_ref, ag_ref, sg_ref, ws_ref, bs_ref, o_ref):
        for c in range(SGU_PER_STEP):
            sl = slice(c * SGU_CHUNK, (c + 1) * SGU_CHUNK)
            o_ref[sl, :] = _sgu_chunk(au_ref[sl, :], av_ref[sl, :], ag_ref[sl, :], sg_ref[...], ws_ref[...],
                                      bs_ref[...])

    return pl.pallas_call(
        kern, name="sgu_fwd", grid=(SEQ // rows,), in_specs=in_specs,
        out_specs=pl.BlockSpec((rows, 512), lambda n: (n, 0)),
        out_shape=jax.ShapeDtypeStruct((SEQ, 512), F32),
        compiler_params=_cparams(("arbitrary",)),
    )(z, z, z, sg, ws, bsb)


def sgu_bwd(z, sg, ws, bsb, dcat):
    rows, in_specs = _sgu_specs()

    def kern(au_ref, av_ref, ag_ref, sg_ref, ws_ref, bs_ref, do_ref, dz_ref, dsg_ref, dws_ref, dbs_ref):
        @pl.when(pl.program_id(0) == 0)
        def _():
            dsg_ref[...] = jnp.zeros_like(dsg_ref)
            dws_ref[...] = jnp.zeros_like(dws_ref)
            dbs_ref[...] = jnp.zeros_like(dbs_ref)

        for c in range(SGU_PER_STEP):
            sl = slice(c * SGU_CHUNK, (c + 1) * SGU_CHUNK)
            _, vjp = jax.vjp(_sgu_chunk, au_ref[sl, :], av_ref[sl, :], ag_ref[sl, :], sg_ref[...], ws_ref[...],
                             bs_ref[...])
            dau, dav, dag, dsg, dws, dbs = vjp(do_ref[sl, :])
            dz_ref[sl, 0:512] = dau.astype(BF16)
            dz_ref[sl, 512:1024] = dav.astype(BF16)
            dz_ref[sl, 1024:1536] = dag.astype(BF16)
            dsg_ref[...] += dsg
            dws_ref[...] += dws
            dbs_ref[...] += dbs

        @pl.when(pl.program_id(0) == pl.num_programs(0) - 1)
        def _():
            dbs_ref[...] = jnp.broadcast_to(jnp.sum(dbs_ref[...], axis=-1, keepdims=True), dbs_ref.shape)

    wspec = pl.BlockSpec((4, 128, 128), lambda n: (0, 0, 0))
    return pl.pallas_call(
        kern, name="sgu_bwd", grid=(SEQ // rows,),
        in_specs=in_specs + [pl.BlockSpec((rows, 512), lambda n: (n, 0))],
        out_specs=[pl.BlockSpec((rows, 1536), lambda n: (n, 0)), _row(512), wspec, wspec],
        out_shape=[jax.ShapeDtypeStruct((SEQ, 1536), BF16), jax.ShapeDtypeStruct((1, 512), F32),
                   jax.ShapeDtypeStruct((4, 128, 128), F32), jax.ShapeDtypeStruct((4, 128, 128), F32)],
        compiler_params=_cparams(("arbitrary",)),
    )(z, z, z, sg, ws, bsb, dcat)


_DR_OFF = (7, 3, -1)


def _row_valid(v, rr, j):
    return (j < 8, rr <= j < rr + 8, 4 <= j < 12)[v]


def _col_window():
    q = lax.broadcasted_iota(jnp.int32, (GRID_W, 128), 0)
    kc = lax.broadcasted_iota(jnp.int32, (GRID_W, 128), 1) % GRID_W
    c0 = jnp.clip(q - 8, 0, GRID_W - 16)
    return (kc >= c0) & (kc < c0 + 16)


def rpb_tables(rpb2):
    def kern(r_ref, b_ref):
        base = r_ref[0]
        lo = lax.broadcasted_iota(jnp.int32, (1, 128), 1) < GRID_W
        win = _col_window()
        neg = jnp.full((GRID_W, 128), NEG_INF, F32)
        for v in range(3):
            for rr in range(QROWS):
                for jp in range(KROWS // 2):
                    j0, j1 = 2 * jp, 2 * jp + 1
                    ok0, ok1 = _row_valid(v, rr, j0), _row_valid(v, rr, j1)
                    if not (ok0 or ok1):
                        tile = neg
                    else:
                        d0 = j0 - rr + _DR_OFF[v]
                        r0 = base[d0:d0 + 1, :] if ok0 else jnp.zeros((1, 128), F32)
                        r1 = base[d0 + 1:d0 + 2, :] if ok1 else jnp.zeros((1, 128), F32)
                        y = jnp.broadcast_to(jnp.where(lo, r0, r1), (GRID_W, 128))
                        y = pltpu.roll(pltpu.roll(y, 128 - 15, 1), 0, 1, stride=1, stride_axis=0)
                        ok = win & jnp.where(lo, ok0, ok1)
                        tile = jnp.where(ok, y, NEG_INF)
                    b_ref[v, 0, rr * GRID_W:(rr + 1) * GRID_W, jp * 128:(jp + 1) * 128] = tile

    return pl.pallas_call(
        kern, name="rpb_tables", grid=(HEADS,),
        in_specs=[pl.BlockSpec((1, 15, 128), lambda h: (h, 0, 0))],
        out_specs=pl.BlockSpec((3, 1, QBLK, KBLK), lambda h: (0, h, 0, 0)),
        out_shape=jax.ShapeDtypeStruct((3, HEADS, QBLK, KBLK), F32),
        compiler_params=_cparams(("arbitrary",)),
    )(rpb2)


def rpb_bwd(dbias):
    def kern(g_ref, o_ref):
        lo = lax.broadcasted_iota(jnp.int32, (1, 128), 1) < GRID_W
        ri = lax.broadcasted_iota(jnp.int32, (GRID_W, GRID_W), 0)
        ci = lax.broadcasted_iota(jnp.int32, (GRID_W, GRID_W), 1)
        flip = (ri + ci == GRID_W - 1).astype(F32)
        acc = [jnp.zeros((1, 128), F32) for _ in range(15)]
        for v in range(3):
            for rr in range(QROWS):
                for jp in range(KROWS // 2):
                    j0, j1 = 2 * jp, 2 * jp + 1
                    ok0, ok1 = _row_valid(v, rr, j0), _row_valid(v, rr, j1)
                    if not (ok0 or ok1):
                        continue
                    g = g_ref[v, 0, rr * GRID_W:(rr + 1) * GRID_W, jp * 128:(jp + 1) * 128]
                    g = lax.dot_general(flip, g, (((1,), (0,)), ((), ())), precision=lax.Precision.HIGHEST,
                                        preferred_element_type=F32)
                    g = pltpu.roll(pltpu.roll(g, 128 - 48, 1), 0, 1, stride=1, stride_axis=0)
                    s = jnp.sum(g, axis=0, keepdims=True)
                    d0 = j0 - rr + _DR_OFF[v]
                    if ok0:
                        acc[d0] = acc[d0] + jnp.where(lo, s, 0.0)
                    if ok1:
                        acc[d0 + 1] = acc[d0 + 1] + jnp.where(lo, 0.0, s)
        for d in range(15):
            o_ref[0, d:d + 1, :] = acc[d] + pltpu.roll(acc[d], GRID_W, 1)

    return pl.pallas_call(
        kern, name="rpb_bwd", grid=(HEADS,),
        in_specs=[pl.BlockSpec((3, 1, QBLK, KBLK), lambda h: (0, h, 0, 0))],
        out_specs=pl.BlockSpec((1, 15, 128), lambda h: (h, 0, 0)),
        out_shape=jax.ShapeDtypeStruct((HEADS, 15, 128), F32),
        compiler_params=_cparams(("arbitrary",)),
    )(dbias)


def _scaled_q(q_raw, qg):
    return _pair_rms(q_raw, qg) * (HDIM ** -0.5)


def _head_lanes():
    lo = lax.broadcasted_iota(jnp.int32, (1, 2 * HDIM), 1) < HDIM
    return lo, jnp.logical_not(lo)


def _attn_step(q_raw, kn, v, ckn, cv, bias2, qg):
    qn = _scaled_q(q_raw, qg)
    out = lse = None
    for a, mine in enumerate(_head_lanes()):
        qa = jnp.where(mine, qn, 0.0)
        s_lat = mm_nt(qa, kn) + bias2[a]
        s_ctx = mm_nt(qa, ckn)
        m = jnp.maximum(jnp.max(s_lat, axis=-1, keepdims=True), jnp.max(s_ctx, axis=-1, keepdims=True))
        p_lat = jnp.exp(s_lat - m)
        p_ctx = jnp.exp(s_ctx - m)
        den = jnp.sum(p_lat, axis=-1, keepdims=True) + jnp.sum(p_ctx, axis=-1, keepdims=True)
        o = jnp.where(mine, (mm(p_lat, v) + mm(p_ctx, cv)) / den, 0.0)
        l = jnp.where(mine, m + jnp.log(den), 0.0)
        out, lse = (o, l) if out is None else (out + o, lse + l)
    return out, lse


def _attn_step_bwd(q_raw, kn, v, ckn, cv, bias2, qg, bg, o, lse, dout):
    sig = jax.nn.sigmoid(bg)
    do = dout * (bg * sig)
    dbg = dout * o * (sig * (1.0 + bg * (1.0 - sig)))
    qn, qn_vjp = jax.vjp(_scaled_q, q_raw, qg)
    row_dot = do * o
    dqn = dkn = dv = dckn = dcv = None
    dbias = []
    for mine in _head_lanes():
        qa = jnp.where(mine, qn, 0.0)
        doa = jnp.where(mine, do, 0.0)
        l = jnp.max(jnp.where(mine, lse, NEG_INF), axis=-1, keepdims=True)
        delta = jnp.sum(jnp.where(mine, row_dot, 0.0), axis=-1, keepdims=True)
        p_lat = jnp.exp(mm_nt(qa, kn) + bias2[len(dbias)] - l)
        p_ctx = jnp.exp(mm_nt(qa, ckn) - l)
        ds_lat = p_lat * (mm_nt(doa, v) - delta)
        ds_ctx = p_ctx * (mm_nt(doa, cv) - delta)
        parts = (jnp.where(mine, mm(ds_lat, kn) + mm(ds_ctx, ckn), 0.0), mm_tn(ds_lat, qa), mm_tn(p_lat, doa),
                 mm_tn(ds_ctx, qa), mm_tn(p_ctx, doa))
        if dqn is None:
            dqn, dkn, dv, dckn, dcv = parts
        else:
            dqn, dkn, dv, dckn, dcv = (acc + new for acc, new in zip((dqn, dkn, dv, dckn, dcv), parts))
        dbias.append(ds_lat)
    dq, dqg = qn_vjp(dqn)
    return dq, dkn, dv, dckn, dcv, dbias, dqg, dbg


def _kstart(i):
    return pl.multiple_of(jnp.clip((i - 1) * QBLK, 0, SEQ - KBLK), QBLK)


def _bias_variant(i):
    return jnp.where(i == 0, 0, jnp.where(i == NQBLK - 1, 2, 1))


def _attn_in_specs():
    return [
        pl.BlockSpec((QBLK, 128), lambda p, i: (i, ZQ + p)),
        pl.BlockSpec((SEQ, 128), lambda p, i: (0, ZK + p)),
        pl.BlockSpec((SEQ, 128), lambda p, i: (0, ZV + p)),
        pl.BlockSpec((QBLK, 128), lambda p, i: (i, ZG + p)),
        pl.BlockSpec((CTX, 128), lambda p, i: (0, 2 + p)),
        pl.BlockSpec((CTX, 128), lambda p, i: (0, 6 + p)),
        pl.BlockSpec((1, 2, QBLK, KBLK), lambda p, i: (_bias_variant(i), p, 0, 0)),
        _row(128), _row(128),
    ]


NORM_ROWS = 512


def _norm_keys(k_ref, ck_ref, kg_ref, kn_scr, ckn_scr):
    def body(c, carry):
        sl = pl.ds(pl.multiple_of(c * NORM_ROWS, NORM_ROWS), NORM_ROWS)
        kn_scr[sl, :] = _pair_rms(k_ref[sl, :], kg_ref[...])
        return carry

    lax.fori_loop(0, SEQ // NORM_ROWS, body, 0)
    ckn_scr[...] = _pair_rms(ck_ref[...], kg_ref[...])


def attn_fwd(z, zc, bias, qg2, kg2):
    def kern(q_ref, k_ref, v_ref, bg_ref, ck_ref, cv_ref, b_ref, qg_ref, kg_ref, ob_ref, o_ref, lse_ref, kn_scr,
             ckn_scr):
        i = pl.program_id(1)

        @pl.when(i == 0)
        def _():
            _norm_keys(k_ref, ck_ref, kg_ref, kn_scr, ckn_scr)

        ks = pl.ds(_kstart(i), KBLK)
        o, lse = _attn_step(q_ref[...], kn_scr[ks, :], v_ref[ks, :], ckn_scr[...], cv_ref[...], b_ref[0], qg_ref[...])
        ob_ref[...] = o * jax.nn.silu(bg_ref[...])
        o_ref[...] = o
        lse_ref[...] = lse

    qblk = pl.BlockSpec((QBLK, 128), lambda p, i: (i, p))
    return pl.pallas_call(
        kern, name="attn_fwd", grid=(NPAIR, NQBLK), in_specs=_attn_in_specs(), out_specs=[qblk] * 3,
        out_shape=[jax.ShapeDtypeStruct((SEQ, 512), F32)] * 3,
        scratch_shapes=[pltpu.VMEM((SEQ, 128), F32), pltpu.VMEM((CTX, 128), F32)],
        compiler_params=_cparams(("arbitrary", "arbitrary"), 40 * 1024 * 1024),
    )(z, z, z, z, zc, zc, bias, qg2, kg2)


def attn_bwd(z, zc, bias, qg2, kg2, dcat, o_raw, lse):
    def kern(q_ref, k_ref, v_ref, bg_ref, ck_ref, cv_ref, b_ref, qg_ref, kg_ref, do_ref, o_ref, lse_ref,
             dq_ref, dk_ref, dv_ref, dbg_ref, dck_ref, dcv_ref, db_ref, dqg_ref, dkg_ref,
             kn_scr, ckn_scr, dkn_scr, dckn_scr, dv_scr):
        p, i = pl.program_id(0), pl.program_id(1)
        last = i == NQBLK - 1

        @pl.when(i == 0)
        def _():
            _norm_keys(k_ref, ck_ref, kg_ref, kn_scr, ckn_scr)
            dkn_scr[...] = jnp.zeros_like(dkn_scr)
            dv_scr[...] = jnp.zeros_like(dv_scr)
            dckn_scr[...] = jnp.zeros_like(dckn_scr)
            dcv_ref[...] = jnp.zeros_like(dcv_ref)

        @pl.when((i == 0) & (p == 0))
        def _():
            dqg_ref[...] = jnp.zeros_like(dqg_ref)
            dkg_ref[...] = jnp.zeros_like(dkg_ref)

        ks = pl.ds(_kstart(i), KBLK)
        dq, dkn, dv, dckn, dcv, db, dqg, dbg = _attn_step_bwd(
            q_ref[...], kn_scr[ks, :], v_ref[ks, :], ckn_scr[...], cv_ref[...], b_ref[0], qg_ref[...], bg_ref[...],
            o_ref[...], lse_ref[...], do_ref[...])
        dq_ref[...] = dq.astype(BF16)
        dbg_ref[...] = dbg.astype(BF16)
        dkn_scr[ks, :] += dkn
        dv_scr[ks, :] += dv
        dckn_scr[...] += dckn
        dcv_ref[...] += dcv
        dqg_ref[...] += dqg
        fresh = (i == 0) | (i == 1) | last

        @pl.when(fresh)
        def _():
            for a in range(2):
                db_ref[0, a] = db[a]

        @pl.when(jnp.logical_not(fresh))
        def _():
            for a in range(2):
                db_ref[0, a] += db[a]

        @pl.when(last)
        def _():
            def body(c, dkg):
                sl = pl.ds(pl.multiple_of(c * NORM_ROWS, NORM_ROWS), NORM_ROWS)
                _, nvjp = jax.vjp(_pair_rms, k_ref[sl, :], kg_ref[...])
                dk, dg = nvjp(dkn_scr[sl, :])
                dk_ref[sl, :] = dk.astype(BF16)
                dv_ref[sl, :] = dv_scr[sl, :].astype(BF16)
                return dkg + dg

            dkg = lax.fori_loop(0, SEQ // NORM_ROWS, body, jnp.zeros((1, 128), F32))
            _, nvjp = jax.vjp(_pair_rms, ck_ref[...], kg_ref[...])
            dck, dg = nvjp(dckn_scr[...])
            dck_ref[...] = dck
            dkg_ref[...] += dkg + dg

        @pl.when(last & (p == NPAIR - 1))
        def _():
            dqg_ref[...] = dqg_ref[...] + pltpu.roll(dqg_ref[...], HDIM, 1)
            dkg_ref[...] = dkg_ref[...] + pltpu.roll(dkg_ref[...], HDIM, 1)

    blk = lambda rows: pl.BlockSpec((rows, 128), lambda p, i: (0, p))
    qblk = pl.BlockSpec((QBLK, 128), lambda p, i: (i, p))
    return pl.pallas_call(
        kern, name="attn_bwd", grid=(NPAIR, NQBLK),
        in_specs=_attn_in_specs() + [pl.BlockSpec((QBLK, 128), lambda p, i: (i, 4 + p)), qblk, qblk],
        out_specs=[qblk, blk(SEQ), blk(SEQ), qblk, blk(CTX), blk(CTX),
                   pl.BlockSpec((1, 2, QBLK, KBLK), lambda p, i: (_bias_variant(i), p, 0, 0)),
                   _row(128), _row(128)],
        out_shape=[jax.ShapeDtypeStruct((SEQ, 512), BF16)] * 4 + [jax.ShapeDtypeStruct((CTX, 512), F32)] * 2
        + [jax.ShapeDtypeStruct((3, HEADS, QBLK, KBLK), F32), jax.ShapeDtypeStruct((1, 128), F32),
           jax.ShapeDtypeStruct((1, 128), F32)],
        scratch_shapes=[pltpu.VMEM((SEQ, 128), F32), pltpu.VMEM((CTX, 128), F32),
                        pltpu.VMEM((SEQ, 128), F32), pltpu.VMEM((CTX, 128), F32), pltpu.VMEM((SEQ, 128), F32)],
        compiler_params=_cparams(("arbitrary", "arbitrary"), VMEM_BIG),
    )(z, z, z, z, zc, zc, bias, qg2, kg2, dcat, o_raw, lse)


def outproj(out_a, out_b, x, target, gate, wo):
    tl = 512

    def kern(a_ref, b_ref, x_ref, t_ref, g_ref, w_ref, loss_ref, dy_ref, dcat_ref, dg_ref, dw_ref):
        @pl.when(pl.program_id(0) == 0)
        def _():
            loss_ref[...] = jnp.zeros_like(loss_ref)
            dg_ref[...] = jnp.zeros_like(dg_ref)
            dw_ref[...] = jnp.zeros_like(dw_ref)

        a, b = a_ref[...].astype(BF16), b_ref[...].astype(BF16)
        mix = (jnp.dot(a, w_ref[0:512, :], preferred_element_type=F32)
               + jnp.dot(b, w_ref[512:1024, :], preferred_element_type=F32))
        err = x_ref[...] + g_ref[...] * mix - t_ref[...]
        loss_ref[...] += 0.5 * jnp.sum(jnp.mean(err * err, axis=-1))
        dy = err * (1.0 / DM)
        dy_ref[...] = dy
        dg_ref[...] += jnp.sum(dy * mix, axis=0, keepdims=True)
        dmix = (g_ref[...] * dy).astype(BF16)
        dcat_ref[...] = lax.dot_general(dmix, w_ref[...], (((1,), (1,)), ((), ())), preferred_element_type=F32)
        dw_ref[0:512, :] += lax.dot_general(a, dmix, (((0,), (0,)), ((), ())), preferred_element_type=F32)
        dw_ref[512:1024, :] += lax.dot_general(b, dmix, (((0,), (0,)), ((), ())), preferred_element_type=F32)

    tile = lambda w: pl.BlockSpec((tl, w), lambda t: (t, 0))
    whole = pl.BlockSpec((DM, DM), lambda t: (0, 0))
    return pl.pallas_call(
        kern, name="outproj", grid=(SEQ // tl,),
        in_specs=[tile(512), tile(512), tile(DM), tile(DM), _row(DM), whole],
        out_specs=[pl.BlockSpec((8, 128), lambda t: (0, 0)), tile(DM), tile(DM), _row(DM), whole],
        out_shape=[jax.ShapeDtypeStruct((8, 128), F32), jax.ShapeDtypeStruct((SEQ, DM), F32),
                   jax.ShapeDtypeStruct((SEQ, DM), F32), jax.ShapeDtypeStruct((1, DM), F32),
                   jax.ShapeDtypeStruct((DM, DM), F32)],
        compiler_params=_cparams(("arbitrary",), 48 * 1024 * 1024),
    )(out_a, out_b, x, target, gate, wo)


def _pieces(sources):
    out = []
    for name, c0, c1 in sources:
        for j in range(NCHIP):
            lo, hi = max(c0, j * SHARD_IN), min(c1, (j + 1) * SHARD_IN)
            if lo < hi:
                out.append((j, lo - j * SHARD_IN, hi - j * SHARD_IN, name, lo - c0, hi - c0))
    return out


DZ_PIECES = _pieces((("a", 0, 1536), ("q", 1536, 2048), ("k", 2048, 2560), ("v", 2560, 3072), ("g", 3072, DIN)))
DZC_PIECES = _pieces((("k", 2048, 2560), ("v", 2560, 3072)))
_NT = (((1,), (1,)), ((), ()))


DH_SUBTILES = 2


def _dz_specs(tl):
    return [pl.BlockSpec((tl, 1536), lambda t: (t, 0))] + [pl.BlockSpec((tl, 512), lambda t: (t, 0))] * 4


def dh_bwd(dz_parts, w_full, x, dy, shift, scale, norm_g, dg_ctx, wire_i, wire_o):
    tl = 512
    nt = SEQ // tl

    def kern(a_ref, q_ref, k_ref, v_ref, g_ref, w_ref, x_ref, dy_ref, sh_ref, sc_ref, gn_ref, dgc_ref, wi_hbm, wo_hbm,
             gx_ref, dsh_ref, dsc_ref, dg_ref, goti_ref, goto_ref, rcv_i, rcv_o, send_sems, recv_sems):
        def ici(n, q):
            wire, rcv = ((wi_hbm, rcv_i), (wo_hbm, rcv_o))[n]
            return _rcopy(wire.at[_chip_of(_flip(q))], rcv.at[q // 2 - 1], send_sems, recv_sems, 3 * n + q // 2 - 1,
                          _flip(q))

        @pl.when(pl.program_id(0) == 0)
        def _():
            for n in (0, 1):
                for q in (2, 4, 6):
                    ici(n, q).start()

        @pl.when(pl.program_id(0) == 0)
        def _():
            dsh_ref[...] = jnp.zeros_like(dsh_ref)
            dsc_ref[...] = jnp.zeros_like(dsc_ref)
            dg_ref[...] = dgc_ref[...]

        src = dict(a=a_ref, q=q_ref, k=k_ref, v=v_ref, g=g_ref)
        for sub in range(DH_SUBTILES):
            rows = slice(sub * tl // DH_SUBTILES, (sub + 1) * tl // DH_SUBTILES)
            dh = None
            for j, l0, l1, name, s0, s1 in DZ_PIECES:
                part = lax.dot_general(src[name][rows, s0:s1], w_ref[j, :, l0:l1], _NT, preferred_element_type=F32)
                dh = part if dh is None else dh + part
            _, vjp = jax.vjp(_modulated, x_ref[rows, :], gn_ref[...], sc_ref[...], sh_ref[...])
            dx, dg, dsc, dsh = vjp(dh)
            gx_ref[rows, :] = dy_ref[rows, :] + dx
            dg_ref[...] += dg
            dsc_ref[...] += dsc
            dsh_ref[...] += dsh

        @pl.when(pl.program_id(0) == nt - 1)
        def _():
            for n in (0, 1):
                for q in (2, 4, 6):
                    ici(n, q).wait_recv()
                    ici(n, q).wait_send()
            goti_ref[...] = rcv_i[...]
            goto_ref[...] = rcv_o[...]

    tile = pl.BlockSpec((tl, DM), lambda t: (t, 0))
    hbm = pl.BlockSpec(memory_space=pl.ANY)
    got = [(NCHIP - 1, rh, w) for rh, w in RS_SHAPES]
    return pl.pallas_call(
        kern, name="dh_bwd", grid=(nt,),
        in_specs=_dz_specs(tl) + [pl.BlockSpec((NCHIP, DM, SHARD_IN), lambda t: (0, 0, 0)), tile, tile, _row(DM),
                                  _row(DM), _row(DM), _row(DM), hbm, hbm],
        out_specs=[tile, _row(DM), _row(DM), _row(DM)] + [pl.BlockSpec(s, lambda t: (0, 0, 0)) for s in got],
        out_shape=[jax.ShapeDtypeStruct((SEQ, DM), F32)] + [jax.ShapeDtypeStruct((1, DM), F32)] * 3
        + [jax.ShapeDtypeStruct(s, BF16) for s in got],
        scratch_shapes=[pltpu.VMEM(s, BF16) for s in got] + [pltpu.SemaphoreType.DMA((6,)), pltpu.SemaphoreType.DMA((6,))],
        compiler_params=_cparams(("arbitrary",), VMEM_BIG),
    )(*dz_parts, w_full, x, dy, shift, scale, norm_g, dg_ctx, wire_i, wire_o)


def dw_bwd(h, dz_parts, hc, dck, dcv):
    tl = 512

    def kern(h_ref, a_ref, q_ref, k_ref, v_ref, g_ref, hc_ref, dck_ref, dcv_ref, dw_ref):
        @pl.when(pl.program_id(0) == 0)
        def _():
            dw_ref[...] = jnp.zeros_like(dw_ref)
            hct = hc_ref[...].T
            csrc = dict(k=dck_ref, v=dcv_ref)
            for j, l0, l1, name, s0, s1 in DZC_PIECES:
                dw_ref[j, :, l0:l1] += jnp.dot(hct, csrc[name][:, s0:s1].astype(BF16), preferred_element_type=F32)

        ht = h_ref[...].T
        src = dict(a=a_ref, q=q_ref, k=k_ref, v=v_ref, g=g_ref)
        for j, l0, l1, name, s0, s1 in DZ_PIECES:
            dw_ref[j, :, l0:l1] += jnp.dot(ht, src[name][:, s0:s1], preferred_element_type=F32)

    whole = lambda r, c: pl.BlockSpec((r, c), lambda t: (0, 0))
    return pl.pallas_call(
        kern, name="dw_bwd", grid=(SEQ // tl,),
        in_specs=[pl.BlockSpec((tl, DM), lambda t: (t, 0))] + _dz_specs(tl) + [whole(CTX, DM), whole(CTX, 512),
                                                                              whole(CTX, 512)],
        out_specs=pl.BlockSpec((NCHIP, DM, SHARD_IN), lambda t: (0, 0, 0)),
        out_shape=jax.ShapeDtypeStruct((NCHIP, DM, SHARD_IN), F32),
        compiler_params=_cparams(("arbitrary",), VMEM_BIG),
    )(h, *dz_parts, hc, dck, dcv)


def ctx_bwd(dck, dcv, w_full, ctx, cshift, cscale, norm_g):
    def kern(dck_ref, dcv_ref, w_ref, c_ref, sh_ref, sc_ref, g_ref, dsh_ref, dsc_ref, dg_ref):
        csrc = dict(k=dck_ref, v=dcv_ref)
        dhc = None
        for j, l0, l1, name, s0, s1 in DZC_PIECES:
            part = lax.dot_general(csrc[name][:, s0:s1].astype(BF16), w_ref[j, :, l0:l1], _NT,
                                   preferred_element_type=F32)
            dhc = part if dhc is None else dhc + part
        _, vjp = jax.vjp(lambda g, sc, sh: _modulated(c_ref[...], g, sc, sh), g_ref[...], sc_ref[...], sh_ref[...])
        dg_ref[...], dsc_ref[...], dsh_ref[...] = vjp(dhc)

    whole = lambda r, c: pl.BlockSpec((r, c), lambda i: (0, 0))
    return pl.pallas_call(
        kern, name="ctx_bwd", grid=(1,),
        in_specs=[whole(CTX, 512), whole(CTX, 512), pl.BlockSpec((NCHIP, DM, SHARD_IN), lambda i: (0, 0, 0)),
                  whole(CTX, DM), _row(DM), _row(DM), _row(DM)],
        out_specs=[_row(DM), _row(DM), _row(DM)],
        out_shape=[jax.ShapeDtypeStruct((1, DM), F32)] * 3,
        compiler_params=_cparams(("arbitrary",), 40 * 1024 * 1024),
    )(dck, dcv, w_full, ctx, cshift, cscale, norm_g)


def _lane_pad_rpb(rpb):
    r = jnp.pad(rpb, ((0, 0), (0, 0), (0, GRID_W - rpb.shape[-1])))
    return jnp.concatenate([r, r], axis=-1)


def local_step(chip, x, ctx, target, mod, cmod, norm_g, sgu_g, w_s, b_s, q_g, k_g, rpb, w_in_shard, w_out_shard):
    shift, scale, gate = mod[:, :DM], mod[:, DM:2 * DM], mod[:, 2 * DM:]
    cshift, cscale = cmod[:, :DM], cmod[:, DM:2 * DM]
    bsb = jnp.broadcast_to(b_s[:, :, None], (4, 128, 128))
    qg2, kg2 = jnp.tile(q_g, (1, 2)), jnp.tile(k_g, (1, 2))

    z, h, w_in_full, w_out_full = inproj_fwd(chip, x, shift, scale, norm_g, w_in_shard, w_out_shard)
    zc, hc = ctx_fwd(ctx, cshift, cscale, norm_g, w_in_full)
    bias = rpb_tables(_lane_pad_rpb(rpb))
    out_a = sgu_fwd(z, sgu_g, w_s, bsb)
    out_b, o_raw, lse = attn_fwd(z, zc, bias, qg2, kg2)
    loss8, dy, dcat, dgate, dwo = outproj(out_a, out_b, x, target, gate, w_out_full.reshape(DM, DM))
    dz_a, dsg, dws, dbsb = sgu_bwd(z, sgu_g, w_s, bsb, dcat)
    dq, dk, dv, dbg, dck, dcv, dbias, dqg2, dkg2 = attn_bwd(z, zc, bias, qg2, kg2, dcat, o_raw, lse)
    drpb = rpb_bwd(dbias)[:, :, :rpb.shape[-1]]
    dz_parts = (dz_a, dq, dk, dv, dbg)
    dcshift, dcscale, dng_c = ctx_bwd(dck, dcv, w_in_full, ctx, cshift, cscale, norm_g)
    dw_in = dw_bwd(h, dz_parts, hc, dck, dcv)
    wire_i, keep_i, wire_o, keep_o = pair_sum(dw_in, dwo.reshape(NCHIP, SHARD_OUT, DM))
    grad_x, dshift, dscale, dng, got_i, got_o = dh_bwd(dz_parts, w_in_full, x, dy, shift, scale, norm_g, dng_c,
                                                       wire_i, wire_o)
    return dict(
        loss=loss8[0:1, 0:1], grad_x=grad_x, rs=(keep_i, got_i, keep_o, got_o),
        dmod=jnp.concatenate([dshift, dscale, dgate], axis=-1),
        dcmod=jnp.concatenate([dcshift, dcscale, jnp.zeros((1, DM), F32)], axis=-1),
        d_norm_g=dng, d_sgu_g=dsg, d_w_s=dws, d_b_s=dbsb[:, :, 0],
        d_q_g=dqg2[:, :HDIM], d_k_g=dkg2[:, :HDIM], d_rpb=drpb)


def _me():
    return lax.axis_index("x"), lax.axis_index("y"), lax.axis_index("c")


def _flip(q):
    x, y, c = _me()
    return ((1 - x) if q & 4 else x, (1 - y) if q & 2 else y, (1 - c) if q & 1 else c)


def _chip_of(dev):
    return 2 * dev[0] + dev[1]


def _rcopy(src, dst, send_sems, recv_sems, k, dev):
    return pltpu.make_async_remote_copy(src_ref=src, dst_ref=dst, send_sem=send_sems.at[k], recv_sem=recv_sems.at[k],
                                        device_id=dev, device_id_type=MESH_ID)


_VMEM_SPEC = pl.BlockSpec(memory_space=pltpu.VMEM)
CS_ROWS = 8 * NDEV + 8


def ada_fwd(c, c_ctx, w_ada, b_shard):
    n_c = NDEV - 1

    def kern(c_ref, cc_ref, wa_ref, b_ref, mod_ref, cs_ref, mine, send_sems, recv_sems):
        x, y, cc = _me()
        k, me = 2 * x + y, 4 * x + 2 * y + cc
        slot = lambda d: pl.ds(pl.multiple_of(8 * d, 8), 8)
        first = lax.broadcasted_iota(jnp.int32, (8, DM), 0) == 0
        mine[...] = jnp.where(first, jnp.broadcast_to(c_ref[...], (8, DM)), 0.0)
        cs_ref[slot(me), :] = mine[...]
        cs_ref[slot(NDEV), :] = jnp.where(first, jnp.broadcast_to(cc_ref[...], (8, DM)), 0.0)
        csends = [_rcopy(mine, cs_ref.at[slot(me), :], send_sems, recv_sems, q - 1, _flip(q)) for q in range(1, NDEV)]
        for cp in csends:
            cp.start()
        wa = wa_ref[...].astype(BF16)
        for q in range(1, NDEV):
            px, py, pc = _flip(q)
            _rcopy(mine, cs_ref.at[slot(4 * px + 2 * py + pc), :], send_sems, recv_sems, q - 1, _flip(q)).wait_recv()
        act = jax.nn.silu(cs_ref[...]).astype(BF16)
        mod_ref[k] = jnp.dot(act, wa, preferred_element_type=F32) + b_ref[...]
        msends = [_rcopy(mod_ref.at[k], mod_ref.at[k], send_sems, recv_sems, n_c + q // 2 - 1, _flip(q))
                  for q in (2, 4, 6)]
        for cp in msends:
            cp.start()
        for q in (2, 4, 6):
            kq = _chip_of(_flip(q))
            _rcopy(mod_ref.at[kq], mod_ref.at[kq], send_sems, recv_sems, n_c + q // 2 - 1, _flip(q)).wait_recv()
        for cp in csends + msends:
            cp.wait_send()

    return pl.pallas_call(
        kern, name="ada_fwd", in_specs=[_VMEM_SPEC] * 4, out_specs=[_VMEM_SPEC] * 2,
        out_shape=[jax.ShapeDtypeStruct((NCHIP, CS_ROWS, SHARD_ADA), F32), jax.ShapeDtypeStruct((CS_ROWS, DM), F32)],
        scratch_shapes=[pltpu.VMEM((8, DM), F32), pltpu.SemaphoreType.DMA((n_c + 3,)),
                        pltpu.SemaphoreType.DMA((n_c + 3,))],
    )(c, c_ctx, w_ada, b_shard)


SLAB_ROWS = 80


RS_SHAPES = ((DM // 2, SHARD_IN), (SHARD_OUT // 2, DM))


def pair_sum(g_in, g_out):
    def kern(gi_hbm, go_hbm, wire_i, keep_i, wire_o, keep_o, mine_i, rcv_i, mine_o, rcv_o, load_sems, send_sems,
             recv_sems):
        x, y, c = _me()
        k = 2 * x + y
        sib = _flip(1)
        work = ((gi_hbm, mine_i, rcv_i, wire_i, keep_i), (go_hbm, mine_o, rcv_o, wire_o, keep_o))
        copies = []
        for n, (g, mine, rcv, _, _) in enumerate(work):
            rh = RS_SHAPES[n][0]
            half = lambda hh, rh=rh: pl.ds(pl.multiple_of(hh * rh, rh), rh)
            load = pltpu.make_async_copy(g.at[:, half(c), :], mine, load_sems.at[n])
            load.start()
            pair = _rcopy(g.at[:, half(1 - c), :], rcv, send_sems, recv_sems, n, sib)
            pair.start()
            copies.append((load, pair))
        for (load, pair), (_, mine, rcv, wire, keep) in zip(copies, work):
            load.wait()
            pair.wait_recv()
            for j in range(NCHIP):
                wire[j] = (mine[j] + rcv[j]).astype(BF16)
            keep[...] = mine[k] + rcv[k]
        for _, pair in copies:
            pair.wait_send()

    (rhi, wi), (rho, wo) = RS_SHAPES
    hbm = pl.BlockSpec(memory_space=pl.ANY)
    return pl.pallas_call(
        kern, name="pair_sum", in_specs=[hbm, hbm], out_specs=[_VMEM_SPEC] * 4,
        out_shape=[jax.ShapeDtypeStruct((NCHIP, rhi, wi), BF16), jax.ShapeDtypeStruct((rhi, wi), F32),
                   jax.ShapeDtypeStruct((NCHIP, rho, wo), BF16), jax.ShapeDtypeStruct((rho, wo), F32)],
        scratch_shapes=[pltpu.VMEM((NCHIP, rhi, wi), F32), pltpu.VMEM((NCHIP, rhi, wi), F32),
                        pltpu.VMEM((NCHIP, rho, wo), F32), pltpu.VMEM((NCHIP, rho, wo), F32),
                        pltpu.SemaphoreType.DMA((2,)), pltpu.SemaphoreType.DMA((2,)), pltpu.SemaphoreType.DMA((2,))],
        compiler_params=pltpu.CompilerParams(vmem_limit_bytes=48 * 1024 * 1024),
    )(g_in, g_out)


def final_reduce(keep_i, got_i, keep_o, got_o, slab):
    def kern(ki_ref, gi_ref, ko_ref, go_ref, s_ref, gin_ref, gout_ref, all_ref, tot_ref, send_sems, recv_sems):
        x, y, c = _me()
        sib = _flip(1)
        dev = lambda d: 4 * d[0] + 2 * d[1] + d[2]
        me = dev((x, y, c))

        def slab_copy(idx, owner, to):
            return _rcopy(all_ref.at[dev(owner)], all_ref.at[dev(owner)], send_sems, recv_sems, idx, to)

        all_ref[me] = s_ref[...]
        first = [slab_copy(0, (x, y, c), sib)] + [slab_copy(q // 2, (x, y, c), _flip(q)) for q in (2, 4, 6)]
        for cp in first:
            cp.start()

        shares = []
        for n, (keep, got, out) in enumerate(((ki_ref, gi_ref, gin_ref), (ko_ref, go_ref, gout_ref))):
            rh = RS_SHAPES[n][0]
            half = lambda hh, rh=rh: pl.ds(pl.multiple_of(hh * rh, rh), rh)
            out[half(c), :] = ((keep[...] + got[0].astype(F32)) + got[1].astype(F32)) + got[2].astype(F32)
            share = _rcopy(out.at[half(c), :], out.at[half(c), :], send_sems, recv_sems, 7 + n, sib)
            share.start()
            shares.append((share, _rcopy(out.at[half(1 - c), :], out.at[half(1 - c), :], send_sems, recv_sems, 7 + n,
                                         sib)))

        passed = []
        for q in (2, 4, 6):
            slab_copy(q // 2, _flip(q), (x, y, c)).wait_recv()
            cp = slab_copy(3 + q // 2, _flip(q), sib)
            cp.start()
            passed.append(cp)
        slab_copy(0, sib, (x, y, c)).wait_recv()
        for q in (2, 4, 6):
            slab_copy(3 + q // 2, _flip(q | 1), (x, y, c)).wait_recv()
        tot = all_ref[0]
        for d in range(1, NDEV):
            tot = tot + all_ref[d]
        tot_ref[...] = tot
        for share, arrival in shares:
            arrival.wait_recv()
            share.wait_send()
        for cp in first + passed:
            cp.wait_send()

    (rhi, wi), (rho, wo) = RS_SHAPES
    return pl.pallas_call(
        kern, name="final_reduce", in_specs=[_VMEM_SPEC] * 5, out_specs=[_VMEM_SPEC] * 4,
        out_shape=[jax.ShapeDtypeStruct((2 * rhi, wi), F32), jax.ShapeDtypeStruct((2 * rho, wo), F32),
                   jax.ShapeDtypeStruct((NDEV, SLAB_ROWS, DM), F32), jax.ShapeDtypeStruct((SLAB_ROWS, DM), F32)],
        scratch_shapes=[pltpu.SemaphoreType.DMA((9,)), pltpu.SemaphoreType.DMA((9,))],
        compiler_params=pltpu.CompilerParams(vmem_limit_bytes=40 * 1024 * 1024),
    )(keep_i, got_i, keep_o, got_o, slab)


def ada_bwd(a_in, dm, dm_shard, w_ada, c_ctx):
    def kern(a_ref, dm_ref, dms_ref, w_ref, cc_ref, dw_ref, db_ref, dcc_ref, parts, send_sems, recv_sems):
        x, y, c = _me()
        k = 2 * x + y
        act = jax.nn.silu(a_ref[...]).astype(BF16)
        dms = dms_ref[...].astype(BF16)
        dw_ref[...] = lax.dot_general(act, dms, (((0,), (0,)), ((), ())), preferred_element_type=F32)
        db_ref[...] = jnp.sum(dm_ref[...], axis=0, keepdims=True)
        parts[k] = lax.dot_general(dms, w_ref[...].astype(BF16), (((1,), (1,)), ((), ())), preferred_element_type=F32)
        sends = [_rcopy(parts.at[k], parts.at[k], send_sems, recv_sems, q // 2 - 1, _flip(q)) for q in (2, 4, 6)]
        for cp in sends:
            cp.start()
        for q in (2, 4, 6):
            kq = _chip_of(_flip(q))
            _rcopy(parts.at[kq], parts.at[kq], send_sems, recv_sems, q // 2 - 1, _flip(q)).wait_recv()
        dact = ((parts[0] + parts[1]) + parts[2]) + parts[3]
        _, vjp = jax.vjp(jax.nn.silu, cc_ref[...])
        dcc_ref[...] = vjp(dact[8:9, :])[0]
        for cp in sends:
            cp.wait_send()

    return pl.pallas_call(
        kern, name="ada_bwd", in_specs=[_VMEM_SPEC] * 5, out_specs=[_VMEM_SPEC] * 3,
        out_shape=[jax.ShapeDtypeStruct((DM, SHARD_ADA), F32), jax.ShapeDtypeStruct((1, 3 * DM), F32),
                   jax.ShapeDtypeStruct((1, DM), F32)],
        scratch_shapes=[pltpu.VMEM((NCHIP, 16, DM), F32), pltpu.SemaphoreType.DMA((3,)), pltpu.SemaphoreType.DMA((3,))],
    )(a_in, dm, dm_shard, w_ada, c_ctx)


def _adamw_math(w, g, m, v):
    m = B1 * m + (1.0 - B1) * g
    v = B2 * v + (1.0 - B2) * (g * g)
    m_hat = m / (1.0 - B1 ** STEP)
    v_hat = v / (1.0 - B2 ** STEP)
    return -LR * (m_hat / (jnp.sqrt(v_hat) + ADAM_EPS) + WD * w), m, v


def adamw_big(w, g, m, v, name, block_rows=256):
    rows, width = w.shape

    def kern(w_ref, g_ref, m_ref, v_ref, d_ref, nm_ref, nv_ref):
        d_ref[...], nm_ref[...], nv_ref[...] = _adamw_math(w_ref[...], g_ref[...], m_ref[...], v_ref[...])

    spec = pl.BlockSpec((block_rows, width), lambda i: (i, 0))
    return pl.pallas_call(
        kern, name=name, grid=(rows // block_rows,), in_specs=[spec] * 4, out_specs=[spec] * 3,
        out_shape=[jax.ShapeDtypeStruct((rows, width), F32)] * 3,
        compiler_params=_cparams(("arbitrary",)),
    )(w, g, m, v)


def adamw_small(quads):
    n = len(quads)

    def kern(*refs):
        ins, outs = refs[:4 * n], refs[4 * n:]
        for i in range(n):
            w, g, m, v = (r[...] for r in ins[4 * i:4 * i + 4])
            outs[3 * i][...], outs[3 * i + 1][...], outs[3 * i + 2][...] = _adamw_math(w, g, m, v)

    flat = [a for quad in quads for a in quad]
    res = pl.pallas_call(
        kern, name="adamw_small", in_specs=[_VMEM_SPEC] * (4 * n), out_specs=[_VMEM_SPEC] * (3 * n),
        out_shape=[jax.ShapeDtypeStruct(q[0].shape, F32) for q in quads for _ in range(3)],
    )(*flat)
    return [tuple(res[3 * i:3 * i + 3]) for i in range(n)]


def _rows_of(a, rows):
    flat = a.reshape(-1)
    return jnp.pad(flat, (0, rows * DM - flat.shape[0])).reshape(rows, DM)


def kernel(x, c, ctx, c_ctx, w_ada, b_ada, norm_g, w_in, sgu_norm_g, w_spatial, b_spatial, q_norm_g, k_norm_g, rpb, w_out, loss_target, m_c_ctx, m_w_ada, m_b_ada, m_norm_g, m_w_in, m_sgu_norm_g, m_w_spatial, m_b_spatial, m_q_norm_g, m_k_norm_g, m_rpb, m_w_out, v_c_ctx, v_w_ada, v_b_ada, v_norm_g, v_w_in, v_sgu_norm_g, v_w_spatial, v_b_spatial, v_q_norm_g, v_k_norm_g, v_rpb, v_w_out):
    xi, yi, ci = lax.axis_index("x"), lax.axis_index("y"), lax.axis_index("c")
    chip, dev = 2 * xi + yi, 4 * xi + 2 * yi + ci
    c_ctx2 = c_ctx.reshape(1, DM)

    b_shard = lax.dynamic_slice(b_ada, (0, chip * SHARD_ADA), (1, SHARD_ADA))
    mod_all, cs = ada_fwd(c, c_ctx2, w_ada[0], b_shard)
    mods = mod_all.transpose(1, 0, 2).reshape(CS_ROWS, 3 * DM)
    mod = lax.dynamic_slice(mods, (8 * dev, 0), (1, 3 * DM))
    cmod = mods[8 * NDEV:8 * NDEV + 1]

    part = local_step(chip.reshape(1).astype(jnp.int32), x[0], ctx[0], loss_target[0], mod, cmod, norm_g, sgu_norm_g,
                      w_spatial[0], b_spatial[0], q_norm_g, k_norm_g, rpb[0], w_in[0], w_out[0])

    slab = jnp.concatenate([
        part["d_norm_g"], _rows_of(part["d_sgu_g"], 1), _rows_of(part["d_b_s"], 1),
        _rows_of(jnp.concatenate([part["d_q_g"], part["d_k_g"]], axis=-1), 1), _rows_of(part["d_rpb"], 4),
        _rows_of(part["loss"], 1), _rows_of(part["dcmod"], 3), _rows_of(part["dmod"], 3), jnp.zeros((1, DM), F32),
        _rows_of(part["d_w_s"], 64)], axis=0)
    g_w_in, g_w_out, gathered, tot = final_reduce(*part["rs"], slab)
    dm = jnp.concatenate([gathered[:, 12:15, :].reshape(NDEV, 3 * DM), tot[9:12].reshape(1, 3 * DM),
                          jnp.zeros((7, 3 * DM), F32)], axis=0)
    a_in = jnp.concatenate([cs[0:8 * NDEV:8], cs[8 * NDEV:8 * NDEV + 1], jnp.zeros((7, DM), F32)], axis=0)
    dm_shard = lax.dynamic_slice(dm, (0, chip * SHARD_ADA), (16, SHARD_ADA))
    g_w_ada, g_b_ada, g_c_ctx = ada_bwd(a_in, dm, dm_shard, w_ada[0], c_ctx2)

    loss = tot[8, 0]
    g_small = dict(
        c_ctx=g_c_ctx, b_ada=g_b_ada, norm_g=tot[0:1], sgu_norm_g=tot[1:2, :512], w_spatial=tot[16:80].reshape(512, 128),
        b_spatial=tot[2:3, :512].reshape(4, 128), q_norm_g=tot[3:4, :HDIM], k_norm_g=tot[3:4, HDIM:2 * HDIM],
        rpb=tot[4:8].reshape(-1)[:HEADS * 15 * 31].reshape(HEADS * 15, 31))
    shapes = dict(c_ctx=(DM,), w_ada=(1, DM, SHARD_ADA), b_ada=(1, 3 * DM), norm_g=(1, DM), w_in=(1, DM, SHARD_IN),
                  sgu_norm_g=(1, 512), w_spatial=(1, 4, 128, 128), b_spatial=(1, 4, 128), q_norm_g=(1, HDIM),
                  k_norm_g=(1, HDIM), rpb=(1, HEADS, 15, 31), w_out=(1, SHARD_OUT, DM))
    names = list(shapes)
    weights = dict(c_ctx=c_ctx, w_ada=w_ada, b_ada=b_ada, norm_g=norm_g, w_in=w_in, sgu_norm_g=sgu_norm_g,
                   w_spatial=w_spatial, b_spatial=b_spatial, q_norm_g=q_norm_g, k_norm_g=k_norm_g, rpb=rpb, w_out=w_out)
    m_in = dict(zip(names, (m_c_ctx, m_w_ada, m_b_ada, m_norm_g, m_w_in, m_sgu_norm_g, m_w_spatial, m_b_spatial,
                            m_q_norm_g, m_k_norm_g, m_rpb, m_w_out)))
    v_in = dict(zip(names, (v_c_ctx, v_w_ada, v_b_ada, v_norm_g, v_w_in, v_sgu_norm_g, v_w_spatial, v_b_spatial,
                            v_q_norm_g, v_k_norm_g, v_rpb, v_w_out)))
    grads = dict(g_small, w_ada=g_w_ada, w_in=g_w_in, w_out=g_w_out)
    upd = {}
    for n in ("w_ada", "w_in", "w_out"):
        g = grads[n]
        upd[n] = adamw_big(weights[n].reshape(g.shape), g, m_in[n].reshape(g.shape), v_in[n].reshape(g.shape),
                           "adamw_" + n)
    small = [n for n in names if n not in upd]
    res = adamw_small([(weights[n].reshape(grads[n].shape), grads[n], m_in[n].reshape(grads[n].shape),
                        v_in[n].reshape(grads[n].shape)) for n in small])
    upd.update(zip(small, res))
    out = [loss, part["grad_x"].reshape(1, SEQ, DM)]
    out += [grads[n].reshape(shapes[n]) for n in names]
    for slot in range(3):
        out += [upd[n][slot].reshape(shapes[n]) for n in names]
    return tuple(out)
```

```python
import jax
import jax.numpy as jnp
from jax import lax
from jax.experimental import pallas as pl
from jax.experimental.pallas import tpu as pltpu

F32, BF16 = jnp.float32, jnp.bfloat16
SEQ, DM, CTX, DIN = 4096, 1024, 256, 3584
NCHIP, NDEV = 4, 8
SHARD_IN = DIN // NCHIP
SHARD_ADA = 3 * DM // NCHIP
SHARD_OUT = DM // NCHIP
GRID_W = 64
QROWS = 4
KROWS = 12
QBLK, KBLK = QROWS * GRID_W, KROWS * GRID_W
NQBLK = SEQ // QBLK
HEADS, HDIM, NPAIR = 8, 64, 4
EPS = 1e-6
NEG_INF = -1e30
ZQ, ZK, ZV, ZG = 12, 16, 20, 24
LR, B1, B2, ADAM_EPS, WD, STEP = 0.001, 0.9, 0.999, 1e-08, 0.01, 10
VMEM_BIG = 56 * 1024 * 1024
MESH_ID = pl.DeviceIdType.MESH


def _dot(a, b, lhs_c, rhs_c):
    return lax.dot_general(a.astype(BF16), b.astype(BF16), (((lhs_c,), (rhs_c,)), ((), ())),
                           preferred_element_type=F32)


@jax.custom_vjp
def mm(a, b):
    return _dot(a, b, 1, 0)


@jax.custom_vjp
def mm_nt(a, b):
    return _dot(a, b, 1, 1)


@jax.custom_vjp
def mm_tn(a, b):
    return _dot(a, b, 0, 0)


mm.defvjp(lambda a, b: (mm(a, b), (a, b)), lambda r, ct: (mm_nt(ct, r[1]), mm_tn(r[0], ct)))
mm_nt.defvjp(lambda a, b: (mm_nt(a, b), (a, b)), lambda r, ct: (mm(ct, r[1]), mm_tn(ct, r[0])))
mm_tn.defvjp(lambda a, b: (mm_tn(a, b), (a, b)), lambda r, ct: (mm_nt(r[1], ct), mm(r[0], ct)))


def _rms(x, g):
    return x * lax.rsqrt(jnp.mean(x * x, axis=-1, keepdims=True) + EPS) * g


def _modulated(x, g, scale, shift):
    return _rms(x, g) * (1.0 + scale) + shift


def _pair_rms(x, g2):
    lo = lax.broadcasted_iota(jnp.int32, (1, 2 * HDIM), 1) < HDIM
    sq = x * x
    s_lo = jnp.sum(jnp.where(lo, sq, 0.0), axis=-1, keepdims=True)
    s_hi = jnp.sum(jnp.where(lo, 0.0, sq), axis=-1, keepdims=True)
    rs = jnp.where(lo, lax.rsqrt(s_lo / HDIM + EPS), lax.rsqrt(s_hi / HDIM + EPS))
    return x * rs * g2


def _cparams(sem, vmem=None):
    return pltpu.CompilerParams(dimension_semantics=sem, vmem_limit_bytes=vmem)


def _row(n):
    return pl.BlockSpec((1, n), lambda *_: (0, 0))


def inproj_fwd(chip, x, shift, scale, norm_g, w_shard, wo_shard):
    tl = 1024
    nt = SEQ // tl
    halves = (DM // 2, SHARD_OUT // 2)

    def kern(k_ref, x_ref, sh_ref, sc_ref, g_ref, w_ref, wo_ref, z_ref, h_ref, wfull_ref, wofull_ref,
             w_scr, wo_scr, h_scr, send_sems, recv_sems):
        s, t = pl.program_id(0), pl.program_id(1)
        xi, yi, c = _me()
        k = 2 * xi + yi
        sib = _flip(1)
        rows = pl.ds(pl.multiple_of(t * tl, tl), tl)
        gathered = (w_scr, wo_scr)

        def block(n, chip_of_block, hh):
            return gathered[n].at[chip_of_block, pl.ds(pl.multiple_of(hh * halves[n], halves[n]), halves[n]), :]

        def ici(n, q, chip_of_block):
            blk = block(n, chip_of_block, c)
            return _rcopy(blk, blk, send_sems, recv_sems, 6 * n + q // 2 - 1, _flip(q))

        def d2d(n, q, chip_of_block, hh):
            blk = block(n, chip_of_block, hh)
            return _rcopy(blk, blk, send_sems, recv_sems, 6 * n + 3 + q // 2 - 1, sib)

        @pl.when((s == 0) & (t == 0))
        def _():
            w_scr[k] = w_ref[...].astype(BF16)
            wo_scr[k] = wo_ref[...].astype(BF16)
            for q in (2, 4, 6):
                ici(0, q, k).start()
                ici(1, q, k).start()

        for sweep in (1, 2, 3):
            @pl.when((s == sweep) & (t == 0))
            def _():
                q = 2 * sweep
                src = _chip_of(_flip(q))
                for n in (0, 1):
                    ici(n, q, src).wait_recv()
                    d2d(n, q, src, c).start()
                for n in (0, 1):
                    d2d(n, q, src, 1 - c).wait_recv()

        @pl.when(s == 0)
        def _():
            hb = _modulated(x_ref[...], g_ref[...], sc_ref[...], sh_ref[...]).astype(BF16)
            h_scr[rows, :] = hb
            h_ref[...] = hb

        z_ref[...] = jnp.dot(h_scr[rows, :], w_scr[lax.bitwise_xor(k, s)], preferred_element_type=F32)

        @pl.when((s == NCHIP - 1) & (t == nt - 1))
        def _():
            for n in (0, 1):
                for q in (2, 4, 6):
                    ici(n, q, k).wait_send()
                    d2d(n, q, _chip_of(_flip(q)), c).wait_send()
            pltpu.sync_copy(w_scr, wfull_ref)
            pltpu.sync_copy(wo_scr, wofull_ref)

    once = lambda s, t, k: (jnp.where(s == 0, t, nt - 1), 0)
    row = lambda n: pl.BlockSpec((1, n), lambda s, t, k: (0, 0))
    hbm = pl.BlockSpec(memory_space=pl.ANY)
    return pl.pallas_call(
        kern, name="inproj_fwd",
        grid_spec=pltpu.PrefetchScalarGridSpec(
            num_scalar_prefetch=1, grid=(NCHIP, nt),
            in_specs=[pl.BlockSpec((tl, DM), once), row(DM), row(DM), row(DM), _VMEM_SPEC, _VMEM_SPEC],
            out_specs=[pl.BlockSpec((tl, SHARD_IN), lambda s, t, k: (t, lax.bitwise_xor(k[0], s))),
                       pl.BlockSpec((tl, DM), once), hbm, hbm],
            scratch_shapes=[pltpu.VMEM((NCHIP, DM, SHARD_IN), BF16), pltpu.VMEM((NCHIP, SHARD_OUT, DM), BF16),
                            pltpu.VMEM((SEQ, DM), BF16), pltpu.SemaphoreType.DMA((12,)), pltpu.SemaphoreType.DMA((12,))]),
        out_shape=[jax.ShapeDtypeStruct((SEQ, DIN), F32), jax.ShapeDtypeStruct((SEQ, DM), BF16),
                   jax.ShapeDtypeStruct((NCHIP, DM, SHARD_IN), BF16), jax.ShapeDtypeStruct((NCHIP, SHARD_OUT, DM), BF16)],
        compiler_params=_cparams(("arbitrary", "arbitrary"), 48 * 1024 * 1024),
    )(chip, x, shift, scale, norm_g, w_shard, wo_shard)


def ctx_fwd(ctx, cshift, cscale, norm_g, w_full):
    def kern(c_ref, sh_ref, sc_ref, g_ref, w2_ref, w3_ref, zc_ref, hc_ref):
        hc = _modulated(c_ref[...], g_ref[...], sc_ref[...], sh_ref[...]).astype(BF16)
        hc_ref[...] = hc
        zc_ref[:, :SHARD_IN] = jnp.dot(hc, w2_ref[0], preferred_element_type=F32)
        zc_ref[:, SHARD_IN:] = jnp.dot(hc, w3_ref[0], preferred_element_type=F32)

    return pl.pallas_call(
        kern, name="ctx_fwd", grid=(1,),
        in_specs=[pl.BlockSpec((CTX, DM), lambda i: (0, 0)), _row(DM), _row(DM), _row(DM),
                  pl.BlockSpec((1, DM, SHARD_IN), lambda i: (2, 0, 0)),
                  pl.BlockSpec((1, DM, SHARD_IN), lambda i: (3, 0, 0))],
        out_specs=[pl.BlockSpec((CTX, 2 * SHARD_IN), lambda i: (0, 0)),
                   pl.BlockSpec((CTX, DM), lambda i: (0, 0))],
        out_shape=[jax.ShapeDtypeStruct((CTX, 2 * SHARD_IN), F32), jax.ShapeDtypeStruct((CTX, DM), BF16)],
        compiler_params=_cparams(("arbitrary",)),
    )(ctx, cshift, cscale, norm_g, w_full, w_full)


SGU_CHUNK, SGU_PER_STEP = 128, 4


def _gelu(x):
    return 0.5 * x * (1.0 + lax.erf(x * 0.7071067811865476))


def _sgu_chunk(au, av, ag, sg, ws, bsb):
    u, v = _gelu(au), _gelu(av)
    outs = []
    for g in range(4):
        sl = slice(128 * g, 128 * (g + 1))
        mixed = mm(ws[g], _rms(v[:, sl], sg[:, sl])) + bsb[g]
        outs.append(u[:, sl] * mixed * jax.nn.silu(ag[:, sl]))
    return jnp.concatenate(outs, axis=-1)


def _sgu_specs():
    rows = SGU_CHUNK * SGU_PER_STEP
    zspec = lambda c: pl.BlockSpec((rows, 512), lambda n: (n, c))
    wspec = pl.BlockSpec((4, 128, 128), lambda n: (0, 0, 0))
    return rows, [zspec(0), zspec(1), zspec(2), _row(512), wspec, wspec]


def sgu_fwd(z, sg, ws, bsb):
    rows, in_specs = _sgu_specs()

    def kern(au_ref, av_ref, ag_ref, sg_ref, ws_ref, bs_ref, o_ref):
        for c in range(SGU_PER_STEP):
            sl = slice(c * SGU_CHUNK, (c + 1) * SGU_CHUNK)
            o_ref[sl, :] = _sgu_chunk(au_ref[sl, :], av_ref[sl, :], ag_ref[sl, :], sg_ref[...], ws_ref[...],
                                      bs_ref[...])

    return pl.pallas_call(
        kern, name="sgu_fwd", grid=(SEQ // rows,), in_specs=in_specs,
        out_specs=pl.BlockSpec((rows, 512), lambda n: (n, 0)),
        out_shape=jax.ShapeDtypeStruct((SEQ, 512), F32),
        compiler_params=_cparams(("arbitrary",)),
    )(z, z, z, sg, ws, bsb)


def sgu_bwd(z, sg, ws, bsb, dcat):
    rows, in_specs = _sgu_specs()

    def kern(au_ref, av_ref, ag_ref, sg_ref, ws_ref, bs_ref, do_ref, dz_ref, dsg_ref, dws_ref, dbs_ref):
        @pl.when(pl.program_id(0) == 0)
        def _():
            dsg_ref[...] = jnp.zeros_like(dsg_ref)
            dws_ref[...] = jnp.zeros_like(dws_ref)
            dbs_ref[...] = jnp.zeros_like(dbs_ref)

        for c in range(SGU_PER_STEP):
            sl = slice(c * SGU_CHUNK, (c + 1) * SGU_CHUNK)
            _, vjp = jax.vjp(_sgu_chunk, au_ref[sl, :], av_ref[sl, :], ag_ref[sl, :], sg_ref[...], ws_ref[...],
                             bs_ref[...])
            dau, dav, dag, dsg, dws, dbs = vjp(do_ref[sl, :])
            dz_ref[sl, 0:512] = dau.astype(BF16)
            dz_ref[sl, 512:1024] = dav.astype(BF16)
            dz_ref[sl, 1024:1536] = dag.astype(BF16)
            dsg_ref[...] += dsg
            dws_ref[...] += dws
            dbs_ref[...] += dbs

        @pl.when(pl.program_id(0) == pl.num_programs(0) - 1)
        def _():
            dbs_ref[...] = jnp.broadcast_to(jnp.sum(dbs_ref[...], axis=-1, keepdims=True), dbs_ref.shape)

    wspec = pl.BlockSpec((4, 128, 128), lambda n: (0, 0, 0))
    return pl.pallas_call(
        kern, name="sgu_bwd", grid=(SEQ // rows,),
        in_specs=in_specs + [pl.BlockSpec((rows, 512), lambda n: (n, 0))],
        out_specs=[pl.BlockSpec((rows, 1536), lambda n: (n, 0)), _row(512), wspec, wspec],
        out_shape=[jax.ShapeDtypeStruct((SEQ, 1536), BF16), jax.ShapeDtypeStruct((1, 512), F32),
                   jax.ShapeDtypeStruct((4, 128, 128), F32), jax.ShapeDtypeStruct((4, 128, 128), F32)],
        compiler_params=_cparams(("arbitrary",)),
    )(z, z, z, sg, ws, bsb, dcat)


_DR_OFF = (7, 3, -1)


def _row_valid(v, rr, j):
    return (j < 8, rr <= j < rr + 8, 4 <= j < 12)[v]


def _col_window():
    q = lax.broadcasted_iota(jnp.int32, (GRID_W, 128), 0)
    kc = lax.broadcasted_iota(jnp.int32, (GRID_W, 128), 1) % GRID_W
    c0 = jnp.clip(q - 8, 0, GRID_W - 16)
    return (kc >= c0) & (kc < c0 + 16)


def rpb_tables(rpb2):
    def kern(r_ref, b_ref):
        base = r_ref[0]
        lo = lax.broadcasted_iota(jnp.int32, (1, 128), 1) < GRID_W
        win = _col_window()
        tiles = {}
        for v in range(3):
            for rr in range(QROWS):
                for jp in range(KROWS // 2):
                    j0, j1 = 2 * jp, 2 * jp + 1
                    ok0, ok1 = _row_valid(v, rr, j0), _row_valid(v, rr, j1)
                    key = (j0 - rr + _DR_OFF[v], ok0, ok1) if (ok0 or ok1) else None
                    if key not in tiles:
                        if key is None:
                            tiles[key] = jnp.full((GRID_W, 128), NEG_INF, F32)
                        else:
                            d0 = key[0]
                            r0 = base[d0:d0 + 1, :] if ok0 else jnp.zeros((1, 128), F32)
                            r1 = base[d0 + 1:d0 + 2, :] if ok1 else jnp.zeros((1, 128), F32)
                            y = jnp.broadcast_to(jnp.where(lo, r0, r1), (GRID_W, 128))
                            y = pltpu.roll(pltpu.roll(y, 128 - 15, 1), 0, 1, stride=1, stride_axis=0)
                            tiles[key] = jnp.where(win & jnp.where(lo, ok0, ok1), y, NEG_INF)
                    b_ref[v, 0, rr * GRID_W:(rr + 1) * GRID_W, jp * 128:(jp + 1) * 128] = tiles[key]

    return pl.pallas_call(
        kern, name="rpb_tables", grid=(HEADS,),
        in_specs=[pl.BlockSpec((1, 15, 128), lambda h: (h, 0, 0))],
        out_specs=pl.BlockSpec((3, 1, QBLK, KBLK), lambda h: (0, h, 0, 0)),
        out_shape=jax.ShapeDtypeStruct((3, HEADS, QBLK, KBLK), F32),
        compiler_params=_cparams(("arbitrary",)),
    )(rpb2)


def rpb_bwd(dbias):
    def kern(g_ref, o_ref):
        lo = lax.broadcasted_iota(jnp.int32, (1, 128), 1) < GRID_W
        ri = lax.broadcasted_iota(jnp.int32, (GRID_W, GRID_W), 0)
        ci = lax.broadcasted_iota(jnp.int32, (GRID_W, GRID_W), 1)
        flip = (ri + ci == GRID_W - 1).astype(F32)
        groups = {}
        for v in range(3):
            for rr in range(QROWS):
                for jp in range(KROWS // 2):
                    j0, j1 = 2 * jp, 2 * jp + 1
                    ok0, ok1 = _row_valid(v, rr, j0), _row_valid(v, rr, j1)
                    if not (ok0 or ok1):
                        continue
                    g = g_ref[v, 0, rr * GRID_W:(rr + 1) * GRID_W, jp * 128:(jp + 1) * 128]
                    key = (j0 - rr + _DR_OFF[v], ok0, ok1)
                    groups[key] = g if key not in groups else groups[key] + g
        acc = [jnp.zeros((1, 128), F32) for _ in range(15)]
        for (d0, ok0, ok1), g in groups.items():
            g = lax.dot_general(flip, g, (((1,), (0,)), ((), ())), precision=lax.Precision.HIGHEST,
                                preferred_element_type=F32)
            g = pltpu.roll(pltpu.roll(g, 128 - 48, 1), 0, 1, stride=1, stride_axis=0)
            s = jnp.sum(g, axis=0, keepdims=True)
            if ok0:
                acc[d0] = acc[d0] + jnp.where(lo, s, 0.0)
            if ok1:
                acc[d0 + 1] = acc[d0 + 1] + jnp.where(lo, 0.0, s)
        for d in range(15):
            o_ref[0, d:d + 1, :] = acc[d] + pltpu.roll(acc[d], GRID_W, 1)

    return pl.pallas_call(
        kern, name="rpb_bwd", grid=(HEADS,),
        in_specs=[pl.BlockSpec((3, 1, QBLK, KBLK), lambda h: (0, h, 0, 0))],
        out_specs=pl.BlockSpec((1, 15, 128), lambda h: (h, 0, 0)),
        out_shape=jax.ShapeDtypeStruct((HEADS, 15, 128), F32),
        compiler_params=_cparams(("arbitrary",)),
    )(dbias)


def _scaled_q(q_raw, qg):
    return _pair_rms(q_raw, qg) * (HDIM ** -0.5)


def _head_lanes():
    lo = lax.broadcasted_iota(jnp.int32, (1, 2 * HDIM), 1) < HDIM
    return lo, jnp.logical_not(lo)


def _attn_step(qn, kn, v, ckn, cv, bias2):
    out = lse = None
    for a, mine in enumerate(_head_lanes()):
        qa = jnp.where(mine, qn, 0.0)
        s_lat = mm_nt(qa, kn) + bias2[a]
        s_ctx = mm_nt(qa, ckn)
        m = jnp.maximum(jnp.max(s_lat, axis=-1, keepdims=True), jnp.max(s_ctx, axis=-1, keepdims=True))
        p_lat = jnp.exp(s_lat - m)
        p_ctx = jnp.exp(s_ctx - m)
        den = jnp.sum(p_lat, axis=-1, keepdims=True) + jnp.sum(p_ctx, axis=-1, keepdims=True)
        o = jnp.where(mine, (mm(p_lat, v) + mm(p_ctx, cv)) / den, 0.0)
        l = jnp.where(mine, m + jnp.log(den), 0.0)
        out, lse = (o, l) if out is None else (out + o, lse + l)
    return out, lse


def _attn_step_bwd(qn, kn, v, ckn, cv, bias2, bg, o, lse, dout):
    sig = jax.nn.sigmoid(bg)
    do = dout * (bg * sig)
    dbg = dout * o * (sig * (1.0 + bg * (1.0 - sig)))
    row_dot = do * o
    dqn = dkn = dv = dckn = dcv = None
    dbias = []
    for mine in _head_lanes():
        qa = jnp.where(mine, qn, 0.0)
        doa = jnp.where(mine, do, 0.0)
        l = jnp.max(jnp.where(mine, lse, NEG_INF), axis=-1, keepdims=True)
        delta = jnp.sum(jnp.where(mine, row_dot, 0.0), axis=-1, keepdims=True)
        p_lat = jnp.exp(mm_nt(qa, kn) + bias2[len(dbias)] - l)
        p_ctx = jnp.exp(mm_nt(qa, ckn) - l)
        ds_lat = p_lat * (mm_nt(doa, v) - delta)
        ds_ctx = p_ctx * (mm_nt(doa, cv) - delta)
        parts = (jnp.where(mine, mm(ds_lat, kn) + mm(ds_ctx, ckn), 0.0), mm_tn(ds_lat, qa), mm_tn(p_lat, doa),
                 mm_tn(ds_ctx, qa), mm_tn(p_ctx, doa))
        if dqn is None:
            dqn, dkn, dv, dckn, dcv = parts
        else:
            dqn, dkn, dv, dckn, dcv = (acc + new for acc, new in zip((dqn, dkn, dv, dckn, dcv), parts))
        dbias.append(ds_lat)
    return dqn, dkn, dv, dckn, dcv, dbias, dbg


def _kstart(i):
    return pl.multiple_of(jnp.clip((i - 1) * QBLK, 0, SEQ - KBLK), QBLK)


def _bias_variant(i):
    return jnp.where(i == 0, 0, jnp.where(i == NQBLK - 1, 2, 1))


def _attn_in_specs():
    return [
        pl.BlockSpec((SEQ, 128), lambda p, i: (0, ZQ + p)),
        pl.BlockSpec((SEQ, 128), lambda p, i: (0, ZK + p)),
        pl.BlockSpec((SEQ, 128), lambda p, i: (0, ZV + p)),
        pl.BlockSpec((QBLK, 128), lambda p, i: (i, ZG + p)),
        pl.BlockSpec((CTX, 128), lambda p, i: (0, 2 + p)),
        pl.BlockSpec((CTX, 128), lambda p, i: (0, 6 + p)),
        pl.BlockSpec((1, 2, QBLK, KBLK), lambda p, i: (_bias_variant(i), p, 0, 0)),
        _row(128), _row(128),
    ]


NORM_ROWS = 512


def _norm_qk(q_ref, k_ref, ck_ref, qg_ref, kg_ref, qn_scr, kn_scr, ckn_scr):
    def body(c, carry):
        sl = pl.ds(pl.multiple_of(c * NORM_ROWS, NORM_ROWS), NORM_ROWS)
        qn_scr[sl, :] = _scaled_q(q_ref[sl, :], qg_ref[...])
        kn_scr[sl, :] = _pair_rms(k_ref[sl, :], kg_ref[...])
        return carry

    lax.fori_loop(0, SEQ // NORM_ROWS, body, 0)
    ckn_scr[...] = _pair_rms(ck_ref[...], kg_ref[...])


def _qrows(i):
    return pl.ds(pl.multiple_of(i * QBLK, QBLK), QBLK)


def attn_fwd(z, zc, bias, qg2, kg2):
    def kern(q_ref, k_ref, v_ref, bg_ref, ck_ref, cv_ref, b_ref, qg_ref, kg_ref, ob_ref, o_ref, lse_ref, qn_scr,
             kn_scr, ckn_scr):
        i = pl.program_id(1)

        @pl.when(i == 0)
        def _():
            _norm_qk(q_ref, k_ref, ck_ref, qg_ref, kg_ref, qn_scr, kn_scr, ckn_scr)

        ks = pl.ds(_kstart(i), KBLK)
        o, lse = _attn_step(qn_scr[_qrows(i), :], kn_scr[ks, :], v_ref[ks, :], ckn_scr[...], cv_ref[...], b_ref[0])
        ob_ref[...] = o * jax.nn.silu(bg_ref[...])
        o_ref[...] = o
        lse_ref[...] = lse

    qblk = pl.BlockSpec((QBLK, 128), lambda p, i: (i, p))
    return pl.pallas_call(
        kern, name="attn_fwd", grid=(NPAIR, NQBLK), in_specs=_attn_in_specs(), out_specs=[qblk] * 3,
        out_shape=[jax.ShapeDtypeStruct((SEQ, 512), F32)] * 3,
        scratch_shapes=[pltpu.VMEM((SEQ, 128), F32), pltpu.VMEM((SEQ, 128), F32), pltpu.VMEM((CTX, 128), F32)],
        compiler_params=_cparams(("arbitrary", "arbitrary"), 40 * 1024 * 1024),
    )(z, z, z, z, zc, zc, bias, qg2, kg2)


def attn_bwd(z, zc, bias, qg2, kg2, dcat, o_raw, lse):
    def kern(q_ref, k_ref, v_ref, bg_ref, ck_ref, cv_ref, b_ref, qg_ref, kg_ref, do_ref, o_ref, lse_ref,
             dq_ref, dk_ref, dv_ref, dbg_ref, dck_ref, dcv_ref, db_ref, dqg_ref, dkg_ref,
             qn_scr, kn_scr, ckn_scr, dqn_scr, dkn_scr, dckn_scr, dv_scr):
        p, i = pl.program_id(0), pl.program_id(1)
        last = i == NQBLK - 1

        @pl.when(i == 0)
        def _():
            _norm_qk(q_ref, k_ref, ck_ref, qg_ref, kg_ref, qn_scr, kn_scr, ckn_scr)
            dkn_scr[...] = jnp.zeros_like(dkn_scr)
            dv_scr[...] = jnp.zeros_like(dv_scr)
            dckn_scr[...] = jnp.zeros_like(dckn_scr)
            dcv_ref[...] = jnp.zeros_like(dcv_ref)

        @pl.when((i == 0) & (p == 0))
        def _():
            dqg_ref[...] = jnp.zeros_like(dqg_ref)
            dkg_ref[...] = jnp.zeros_like(dkg_ref)

        ks = pl.ds(_kstart(i), KBLK)
        dqn, dkn, dv, dckn, dcv, db, dbg = _attn_step_bwd(
            qn_scr[_qrows(i), :], kn_scr[ks, :], v_ref[ks, :], ckn_scr[...], cv_ref[...], b_ref[0], bg_ref[...],
            o_ref[...], lse_ref[...], do_ref[...])
        dqn_scr[_qrows(i), :] = dqn
        dbg_ref[...] = dbg.astype(BF16)
        dkn_scr[ks, :] += dkn
        dv_scr[ks, :] += dv
        dckn_scr[...] += dckn
        dcv_ref[...] += dcv
        fresh = (i == 0) | (i == 1) | last

        @pl.when(fresh)
        def _():
            for a in range(2):
                db_ref[0, a] = db[a]

        @pl.when(jnp.logical_not(fresh))
        def _():
            for a in range(2):
                db_ref[0, a] += db[a]

        @pl.when(last)
        def _():
            def body(c, carry):
                dqg, dkg = carry
                sl = pl.ds(pl.multiple_of(c * NORM_ROWS, NORM_ROWS), NORM_ROWS)
                _, qvjp = jax.vjp(_scaled_q, q_ref[sl, :], qg_ref[...])
                dq, dgq = qvjp(dqn_scr[sl, :])
                _, kvjp = jax.vjp(_pair_rms, k_ref[sl, :], kg_ref[...])
                dk, dgk = kvjp(dkn_scr[sl, :])
                dq_ref[sl, :] = dq.astype(BF16)
                dk_ref[sl, :] = dk.astype(BF16)
                dv_ref[sl, :] = dv_scr[sl, :].astype(BF16)
                return dqg + dgq, dkg + dgk

            zero = jnp.zeros((1, 128), F32)
            dqg, dkg = lax.fori_loop(0, SEQ // NORM_ROWS, body, (zero, zero))
            _, nvjp = jax.vjp(_pair_rms, ck_ref[...], kg_ref[...])
            dck, dg = nvjp(dckn_scr[...])
            dck_ref[...] = dck
            dqg_ref[...] += dqg
            dkg_ref[...] += dkg + dg

        @pl.when(last & (p == NPAIR - 1))
        def _():
            dqg_ref[...] = dqg_ref[...] + pltpu.roll(dqg_ref[...], HDIM, 1)
            dkg_ref[...] = dkg_ref[...] + pltpu.roll(dkg_ref[...], HDIM, 1)

    blk = lambda rows: pl.BlockSpec((rows, 128), lambda p, i: (0, p))
    qblk = pl.BlockSpec((QBLK, 128), lambda p, i: (i, p))
    return pl.pallas_call(
        kern, name="attn_bwd", grid=(NPAIR, NQBLK),
        in_specs=_attn_in_specs() + [pl.BlockSpec((QBLK, 128), lambda p, i: (i, 4 + p)), qblk, qblk],
        out_specs=[blk(SEQ), blk(SEQ), blk(SEQ), qblk, blk(CTX), blk(CTX),
                   pl.BlockSpec((1, 2, QBLK, KBLK), lambda p, i: (_bias_variant(i), p, 0, 0)),
                   _row(128), _row(128)],
        out_shape=[jax.ShapeDtypeStruct((SEQ, 512), BF16)] * 4 + [jax.ShapeDtypeStruct((CTX, 512), F32)] * 2
        + [jax.ShapeDtypeStruct((3, HEADS, QBLK, KBLK), F32), jax.ShapeDtypeStruct((1, 128), F32),
           jax.ShapeDtypeStruct((1, 128), F32)],
        scratch_shapes=[pltpu.VMEM((SEQ, 128), F32), pltpu.VMEM((SEQ, 128), F32), pltpu.VMEM((CTX, 128), F32),
                        pltpu.VMEM((SEQ, 128), F32), pltpu.VMEM((SEQ, 128), F32), pltpu.VMEM((CTX, 128), F32),
                        pltpu.VMEM((SEQ, 128), F32)],
        compiler_params=_cparams(("arbitrary", "arbitrary"), VMEM_BIG),
    )(z, z, z, z, zc, zc, bias, qg2, kg2, dcat, o_raw, lse)


def outproj(out_a, out_b, x, target, gate, wo):
    tl = 512

    def kern(a_ref, b_ref, x_ref, t_ref, g_ref, w_ref, loss_ref, dy_ref, dcat_ref, dg_ref, dw_ref):
        @pl.when(pl.program_id(0) == 0)
        def _():
            loss_ref[...] = jnp.zeros_like(loss_ref)
            dg_ref[...] = jnp.zeros_like(dg_ref)
            dw_ref[...] = jnp.zeros_like(dw_ref)

        a, b = a_ref[...].astype(BF16), b_ref[...].astype(BF16)
        mix = (jnp.dot(a, w_ref[0:512, :], preferred_element_type=F32)
               + jnp.dot(b, w_ref[512:1024, :], preferred_element_type=F32))
        err = x_ref[...] + g_ref[...] * mix - t_ref[...]
        loss_ref[...] += 0.5 * jnp.sum(jnp.mean(err * err, axis=-1))
        dy = err * (1.0 / DM)
        dy_ref[...] = dy
        dg_ref[...] += jnp.sum(dy * mix, axis=0, keepdims=True)
        dmix = (g_ref[...] * dy).astype(BF16)
        dcat_ref[...] = lax.dot_general(dmix, w_ref[...], (((1,), (1,)), ((), ())), preferred_element_type=F32)
        dw_ref[0:512, :] += lax.dot_general(a, dmix, (((0,), (0,)), ((), ())), preferred_element_type=F32)
        dw_ref[512:1024, :] += lax.dot_general(b, dmix, (((0,), (0,)), ((), ())), preferred_element_type=F32)

    tile = lambda w: pl.BlockSpec((tl, w), lambda t: (t, 0))
    whole = pl.BlockSpec((DM, DM), lambda t: (0, 0))
    return pl.pallas_call(
        kern, name="outproj", grid=(SEQ // tl,),
        in_specs=[tile(512), tile(512), tile(DM), tile(DM), _row(DM), whole],
        out_specs=[pl.BlockSpec((8, 128), lambda t: (0, 0)), tile(DM), tile(DM), _row(DM), whole],
        out_shape=[jax.ShapeDtypeStruct((8, 128), F32), jax.ShapeDtypeStruct((SEQ, DM), F32),
                   jax.ShapeDtypeStruct((SEQ, DM), F32), jax.ShapeDtypeStruct((1, DM), F32),
                   jax.ShapeDtypeStruct((DM, DM), F32)],
        compiler_params=_cparams(("arbitrary",), 48 * 1024 * 1024),
    )(out_a, out_b, x, target, gate, wo)


def _pieces(sources):
    out = []
    for name, c0, c1 in sources:
        for j in range(NCHIP):
            lo, hi = max(c0, j * SHARD_IN), min(c1, (j + 1) * SHARD_IN)
            if lo < hi:
                out.append((j, lo - j * SHARD_IN, hi - j * SHARD_IN, name, lo - c0, hi - c0))
    return out


DZ_PIECES = _pieces((("a", 0, 1536), ("q", 1536, 2048), ("k", 2048, 2560), ("v", 2560, 3072), ("g", 3072, DIN)))
DZC_PIECES = _pieces((("k", 2048, 2560), ("v", 2560, 3072)))
_NT = (((1,), (1,)), ((), ()))


DH_SUBTILES = 2


def _dz_specs(tl):
    return [pl.BlockSpec((tl, 1536), lambda t: (t, 0))] + [pl.BlockSpec((tl, 512), lambda t: (t, 0))] * 4


def dh_bwd(dz_parts, w_full, x, dy, shift, scale, norm_g, dg_ctx, wire_i, wire_o):
    tl = 512
    nt = SEQ // tl

    def kern(a_ref, q_ref, k_ref, v_ref, g_ref, w_ref, x_ref, dy_ref, sh_ref, sc_ref, gn_ref, dgc_ref, wi_hbm, wo_hbm,
             gx_ref, dsh_ref, dsc_ref, dg_ref, goti_ref, goto_ref, rcv_i, rcv_o, send_sems, recv_sems):
        def ici(n, q):
            wire, rcv = ((wi_hbm, rcv_i), (wo_hbm, rcv_o))[n]
            return _rcopy(wire.at[_chip_of(_flip(q))], rcv.at[q // 2 - 1], send_sems, recv_sems, 3 * n + q // 2 - 1,
                          _flip(q))

        @pl.when(pl.program_id(0) == 0)
        def _():
            for n in (0, 1):
                for q in (2, 4, 6):
                    ici(n, q).start()

        @pl.when(pl.program_id(0) == 0)
        def _():
            dsh_ref[...] = jnp.zeros_like(dsh_ref)
            dsc_ref[...] = jnp.zeros_like(dsc_ref)
            dg_ref[...] = dgc_ref[...]

        src = dict(a=a_ref, q=q_ref, k=k_ref, v=v_ref, g=g_ref)
        for sub in range(DH_SUBTILES):
            rows = slice(sub * tl // DH_SUBTILES, (sub + 1) * tl // DH_SUBTILES)
            dh = None
            for j, l0, l1, name, s0, s1 in DZ_PIECES:
                part = lax.dot_general(src[name][rows, s0:s1], w_ref[j, :, l0:l1], _NT, preferred_element_type=F32)
                dh = part if dh is None else dh + part
            _, vjp = jax.vjp(_modulated, x_ref[rows, :], gn_ref[...], sc_ref[...], sh_ref[...])
            dx, dg, dsc, dsh = vjp(dh)
            gx_ref[rows, :] = dy_ref[rows, :] + dx
            dg_ref[...] += dg
            dsc_ref[...] += dsc
            dsh_ref[...] += dsh

        @pl.when(pl.program_id(0) == nt - 1)
        def _():
            for n in (0, 1):
                for q in (2, 4, 6):
                    ici(n, q).wait_recv()
                    ici(n, q).wait_send()
            goti_ref[...] = rcv_i[...]
            goto_ref[...] = rcv_o[...]

    tile = pl.BlockSpec((tl, DM), lambda t: (t, 0))
    hbm = pl.BlockSpec(memory_space=pl.ANY)
    got = [(NCHIP - 1, rh, w) for rh, w in RS_SHAPES]
    return pl.pallas_call(
        kern, name="dh_bwd", grid=(nt,),
        in_specs=_dz_specs(tl) + [pl.BlockSpec((NCHIP, DM, SHARD_IN), lambda t: (0, 0, 0)), tile, tile, _row(DM),
                                  _row(DM), _row(DM), _row(DM), hbm, hbm],
        out_specs=[tile, _row(DM), _row(DM), _row(DM)] + [pl.BlockSpec(s, lambda t: (0, 0, 0)) for s in got],
        out_shape=[jax.ShapeDtypeStruct((SEQ, DM), F32)] + [jax.ShapeDtypeStruct((1, DM), F32)] * 3
        + [jax.ShapeDtypeStruct(s, BF16) for s in got],
        scratch_shapes=[pltpu.VMEM(s, BF16) for s in got] + [pltpu.SemaphoreType.DMA((6,)), pltpu.SemaphoreType.DMA((6,))],
        compiler_params=_cparams(("arbitrary",), VMEM_BIG),
    )(*dz_parts, w_full, x, dy, shift, scale, norm_g, dg_ctx, wire_i, wire_o)


def dw_bwd(h, dz_parts, hc, dck, dcv):
    tl = 512

    def kern(h_ref, a_ref, q_ref, k_ref, v_ref, g_ref, hc_ref, dck_ref, dcv_ref, dw_ref):
        @pl.when(pl.program_id(0) == 0)
        def _():
            dw_ref[...] = jnp.zeros_like(dw_ref)
            hct = hc_ref[...].T
            csrc = dict(k=dck_ref, v=dcv_ref)
            for j, l0, l1, name, s0, s1 in DZC_PIECES:
                dw_ref[j, :, l0:l1] += jnp.dot(hct, csrc[name][:, s0:s1].astype(BF16), preferred_element_type=F32)

        ht = h_ref[...].T
        src = dict(a=a_ref, q=q_ref, k=k_ref, v=v_ref, g=g_ref)
        for j, l0, l1, name, s0, s1 in DZ_PIECES:
            dw_ref[j, :, l0:l1] += jnp.dot(ht, src[name][:, s0:s1], preferred_element_type=F32)

    whole = lambda r, c: pl.BlockSpec((r, c), lambda t: (0, 0))
    return pl.pallas_call(
        kern, name="dw_bwd", grid=(SEQ // tl,),
        in_specs=[pl.BlockSpec((tl, DM), lambda t: (t, 0))] + _dz_specs(tl) + [whole(CTX, DM), whole(CTX, 512),
                                                                              whole(CTX, 512)],
        out_specs=pl.BlockSpec((NCHIP, DM, SHARD_IN), lambda t: (0, 0, 0)),
        out_shape=jax.ShapeDtypeStruct((NCHIP, DM, SHARD_IN), F32),
        compiler_params=_cparams(("arbitrary",), VMEM_BIG),
    )(h, *dz_parts, hc, dck, dcv)


def ctx_bwd(dck, dcv, w_full, ctx, cshift, cscale, norm_g):
    def kern(dck_ref, dcv_ref, w_ref, c_ref, sh_ref, sc_ref, g_ref, dsh_ref, dsc_ref, dg_ref):
        csrc = dict(k=dck_ref, v=dcv_ref)
        dhc = None
        for j, l0, l1, name, s0, s1 in DZC_PIECES:
            part = lax.dot_general(csrc[name][:, s0:s1].astype(BF16), w_ref[j, :, l0:l1], _NT,
                                   preferred_element_type=F32)
            dhc = part if dhc is None else dhc + part
        _, vjp = jax.vjp(lambda g, sc, sh: _modulated(c_ref[...], g, sc, sh), g_ref[...], sc_ref[...], sh_ref[...])
        dg_ref[...], dsc_ref[...], dsh_ref[...] = vjp(dhc)

    whole = lambda r, c: pl.BlockSpec((r, c), lambda i: (0, 0))
    return pl.pallas_call(
        kern, name="ctx_bwd", grid=(1,),
        in_specs=[whole(CTX, 512), whole(CTX, 512), pl.BlockSpec((NCHIP, DM, SHARD_IN), lambda i: (0, 0, 0)),
                  whole(CTX, DM), _row(DM), _row(DM), _row(DM)],
        out_specs=[_row(DM), _row(DM), _row(DM)],
        out_shape=[jax.ShapeDtypeStruct((1, DM), F32)] * 3,
        compiler_params=_cparams(("arbitrary",), 40 * 1024 * 1024),
    )(dck, dcv, w_full, ctx, cshift, cscale, norm_g)


def _lane_pad_rpb(rpb):
    r = jnp.pad(rpb, ((0, 0), (0, 0), (0, GRID_W - rpb.shape[-1])))
    return jnp.concatenate([r, r], axis=-1)


def local_step(chip, x, ctx, target, mod, cmod, norm_g, sgu_g, w_s, b_s, q_g, k_g, rpb, w_in_shard, w_out_shard):
    shift, scale, gate = mod[:, :DM], mod[:, DM:2 * DM], mod[:, 2 * DM:]
    cshift, cscale = cmod[:, :DM], cmod[:, DM:2 * DM]
    bsb = jnp.broadcast_to(b_s[:, :, None], (4, 128, 128))
    qg2, kg2 = jnp.tile(q_g, (1, 2)), jnp.tile(k_g, (1, 2))

    z, h, w_in_full, w_out_full = inproj_fwd(chip, x, shift, scale, norm_g, w_in_shard, w_out_shard)
    zc, hc = ctx_fwd(ctx, cshift, cscale, norm_g, w_in_full)
    bias = rpb_tables(_lane_pad_rpb(rpb))
    out_a = sgu_fwd(z, sgu_g, w_s, bsb)
    out_b, o_raw, lse = attn_fwd(z, zc, bias, qg2, kg2)
    loss8, dy, dcat, dgate, dwo = outproj(out_a, out_b, x, target, gate, w_out_full.reshape(DM, DM))
    dz_a, dsg, dws, dbsb = sgu_bwd(z, sgu_g, w_s, bsb, dcat)
    dq, dk, dv, dbg, dck, dcv, dbias, dqg2, dkg2 = attn_bwd(z, zc, bias, qg2, kg2, dcat, o_raw, lse)
    drpb = rpb_bwd(dbias)[:, :, :rpb.shape[-1]]
    dz_parts = (dz_a, dq, dk, dv, dbg)
    dcshift, dcscale, dng_c = ctx_bwd(dck, dcv, w_in_full, ctx, cshift, cscale, norm_g)
    dw_in = dw_bwd(h, dz_parts, hc, dck, dcv)
    wire_i, keep_i, wire_o, keep_o = pair_sum(dw_in, dwo.reshape(NCHIP, SHARD_OUT, DM))
    grad_x, dshift, dscale, dng, got_i, got_o = dh_bwd(dz_parts, w_in_full, x, dy, shift, scale, norm_g, dng_c,
                                                       wire_i, wire_o)
    return dict(
        loss=loss8[0:1, 0:1], grad_x=grad_x, rs=(keep_i, got_i, keep_o, got_o),
        dmod=jnp.concatenate([dshift, dscale, dgate], axis=-1),
        dcmod=jnp.concatenate([dcshift, dcscale, jnp.zeros((1, DM), F32)], axis=-1),
        d_norm_g=dng, d_sgu_g=dsg, d_w_s=dws, d_b_s=dbsb[:, :, 0],
        d_q_g=dqg2[:, :HDIM], d_k_g=dkg2[:, :HDIM], d_rpb=drpb)


def _me():
    return lax.axis_index("x"), lax.axis_index("y"), lax.axis_index("c")


def _flip(q):
    x, y, c = _me()
    return ((1 - x) if q & 4 else x, (1 - y) if q & 2 else y, (1 - c) if q & 1 else c)


def _chip_of(dev):
    return 2 * dev[0] + dev[1]


def _rcopy(src, dst, send_sems, recv_sems, k, dev):
    return pltpu.make_async_remote_copy(src_ref=src, dst_ref=dst, send_sem=send_sems.at[k], recv_sem=recv_sems.at[k],
                                        device_id=dev, device_id_type=MESH_ID)


_VMEM_SPEC = pl.BlockSpec(memory_space=pltpu.VMEM)
CS_ROWS = 8 * NDEV + 8


def ada_fwd(c, c_ctx, w_ada, b_shard):
    n_c = NDEV - 1

    def kern(c_ref, cc_ref, wa_ref, b_ref, mod_ref, cs_ref, mine, send_sems, recv_sems):
        x, y, cc = _me()
        k, me = 2 * x + y, 4 * x + 2 * y + cc
        slot = lambda d: pl.ds(pl.multiple_of(8 * d, 8), 8)
        first = lax.broadcasted_iota(jnp.int32, (8, DM), 0) == 0
        mine[...] = jnp.where(first, jnp.broadcast_to(c_ref[...], (8, DM)), 0.0)
        cs_ref[slot(me), :] = mine[...]
        cs_ref[slot(NDEV), :] = jnp.where(first, jnp.broadcast_to(cc_ref[...], (8, DM)), 0.0)
        csends = [_rcopy(mine, cs_ref.at[slot(me), :], send_sems, recv_sems, q - 1, _flip(q)) for q in range(1, NDEV)]
        for cp in csends:
            cp.start()
        wa = wa_ref[...].astype(BF16)
        for q in range(1, NDEV):
            px, py, pc = _flip(q)
            _rcopy(mine, cs_ref.at[slot(4 * px + 2 * py + pc), :], send_sems, recv_sems, q - 1, _flip(q)).wait_recv()
        act = jax.nn.silu(cs_ref[...]).astype(BF16)
        mod_ref[k] = jnp.dot(act, wa, preferred_element_type=F32) + b_ref[...]
        msends = [_rcopy(mod_ref.at[k], mod_ref.at[k], send_sems, recv_sems, n_c + q // 2 - 1, _flip(q))
                  for q in (2, 4, 6)]
        for cp in msends:
            cp.start()
        for q in (2, 4, 6):
            kq = _chip_of(_flip(q))
            _rcopy(mod_ref.at[kq], mod_ref.at[kq], send_sems, recv_sems, n_c + q // 2 - 1, _flip(q)).wait_recv()
        for cp in csends + msends:
            cp.wait_send()

    return pl.pallas_call(
        kern, name="ada_fwd", in_specs=[_VMEM_SPEC] * 4, out_specs=[_VMEM_SPEC] * 2,
        out_shape=[jax.ShapeDtypeStruct((NCHIP, CS_ROWS, SHARD_ADA), F32), jax.ShapeDtypeStruct((CS_ROWS, DM), F32)],
        scratch_shapes=[pltpu.VMEM((8, DM), F32), pltpu.SemaphoreType.DMA((n_c + 3,)),
                        pltpu.SemaphoreType.DMA((n_c + 3,))],
    )(c, c_ctx, w_ada, b_shard)


SLAB_ROWS = 80


RS_SHAPES = ((DM // 2, SHARD_IN), (SHARD_OUT // 2, DM))


def pair_sum(g_in, g_out):
    def kern(gi_hbm, go_hbm, wire_i, keep_i, wire_o, keep_o, mine_i, rcv_i, mine_o, rcv_o, load_sems, send_sems,
             recv_sems):
        x, y, c = _me()
        k = 2 * x + y
        sib = _flip(1)
        work = ((gi_hbm, mine_i, rcv_i, wire_i, keep_i), (go_hbm, mine_o, rcv_o, wire_o, keep_o))
        copies = []
        for n, (g, mine, rcv, _, _) in enumerate(work):
            rh = RS_SHAPES[n][0]
            half = lambda hh, rh=rh: pl.ds(pl.multiple_of(hh * rh, rh), rh)
            load = pltpu.make_async_copy(g.at[:, half(c), :], mine, load_sems.at[n])
            load.start()
            pair = _rcopy(g.at[:, half(1 - c), :], rcv, send_sems, recv_sems, n, sib)
            pair.start()
            copies.append((load, pair))
        for (load, pair), (_, mine, rcv, wire, keep) in zip(copies, work):
            load.wait()
            pair.wait_recv()
            for j in range(NCHIP):
                wire[j] = (mine[j] + rcv[j]).astype(BF16)
            keep[...] = mine[k] + rcv[k]
        for _, pair in copies:
            pair.wait_send()

    (rhi, wi), (rho, wo) = RS_SHAPES
    hbm = pl.BlockSpec(memory_space=pl.ANY)
    return pl.pallas_call(
        kern, name="pair_sum", in_specs=[hbm, hbm], out_specs=[_VMEM_SPEC] * 4,
        out_shape=[jax.ShapeDtypeStruct((NCHIP, rhi, wi), BF16), jax.ShapeDtypeStruct((rhi, wi), F32),
                   jax.ShapeDtypeStruct((NCHIP, rho, wo), BF16), jax.ShapeDtypeStruct((rho, wo), F32)],
        scratch_shapes=[pltpu.VMEM((NCHIP, rhi, wi), F32), pltpu.VMEM((NCHIP, rhi, wi), F32),
                        pltpu.VMEM((NCHIP, rho, wo), F32), pltpu.VMEM((NCHIP, rho, wo), F32),
                        pltpu.SemaphoreType.DMA((2,)), pltpu.SemaphoreType.DMA((2,)), pltpu.SemaphoreType.DMA((2,))],
        compiler_params=pltpu.CompilerParams(vmem_limit_bytes=48 * 1024 * 1024),
    )(g_in, g_out)


def final_reduce(keep_i, got_i, keep_o, got_o, slab):
    def kern(ki_ref, gi_ref, ko_ref, go_ref, s_ref, gin_ref, gout_ref, all_ref, tot_ref, send_sems, recv_sems):
        x, y, c = _me()
        sib = _flip(1)
        dev = lambda d: 4 * d[0] + 2 * d[1] + d[2]
        me = dev((x, y, c))

        def slab_copy(idx, owner, to):
            return _rcopy(all_ref.at[dev(owner)], all_ref.at[dev(owner)], send_sems, recv_sems, idx, to)

        all_ref[me] = s_ref[...]
        first = [slab_copy(0, (x, y, c), sib)] + [slab_copy(q // 2, (x, y, c), _flip(q)) for q in (2, 4, 6)]
        for cp in first:
            cp.start()

        shares = []
        for n, (keep, got, out) in enumerate(((ki_ref, gi_ref, gin_ref), (ko_ref, go_ref, gout_ref))):
            rh = RS_SHAPES[n][0]
            half = lambda hh, rh=rh: pl.ds(pl.multiple_of(hh * rh, rh), rh)
            out[half(c), :] = ((keep[...] + got[0].astype(F32)) + got[1].astype(F32)) + got[2].astype(F32)
            share = _rcopy(out.at[half(c), :], out.at[half(c), :], send_sems, recv_sems, 7 + n, sib)
            share.start()
            shares.append((share, _rcopy(out.at[half(1 - c), :], out.at[half(1 - c), :], send_sems, recv_sems, 7 + n,
                                         sib)))

        passed = []
        for q in (2, 4, 6):
            slab_copy(q // 2, _flip(q), (x, y, c)).wait_recv()
            cp = slab_copy(3 + q // 2, _flip(q), sib)
            cp.start()
            passed.append(cp)
        slab_copy(0, sib, (x, y, c)).wait_recv()
        for q in (2, 4, 6):
            slab_copy(3 + q // 2, _flip(q | 1), (x, y, c)).wait_recv()
        tot = all_ref[0]
        for d in range(1, NDEV):
            tot = tot + all_ref[d]
        tot_ref[...] = tot
        for share, arrival in shares:
            arrival.wait_recv()
            share.wait_send()
        for cp in first + passed:
            cp.wait_send()

    (rhi, wi), (rho, wo) = RS_SHAPES
    return pl.pallas_call(
        kern, name="final_reduce", in_specs=[_VMEM_SPEC] * 5, out_specs=[_VMEM_SPEC] * 4,
        out_shape=[jax.ShapeDtypeStruct((2 * rhi, wi), F32), jax.ShapeDtypeStruct((2 * rho, wo), F32),
                   jax.ShapeDtypeStruct((NDEV, SLAB_ROWS, DM), F32), jax.ShapeDtypeStruct((SLAB_ROWS, DM), F32)],
        scratch_shapes=[pltpu.SemaphoreType.DMA((9,)), pltpu.SemaphoreType.DMA((9,))],
        compiler_params=pltpu.CompilerParams(vmem_limit_bytes=40 * 1024 * 1024),
    )(keep_i, got_i, keep_o, got_o, slab)


def ada_bwd(a_in, dm, dm_shard, w_ada, c_ctx):
    def kern(a_ref, dm_ref, dms_ref, w_ref, cc_ref, dw_ref, db_ref, dcc_ref, parts, send_sems, recv_sems):
        x, y, c = _me()
        k = 2 * x + y
        act = jax.nn.silu(a_ref[...]).astype(BF16)
        dms = dms_ref[...].astype(BF16)
        dw_ref[...] = lax.dot_general(act, dms, (((0,), (0,)), ((), ())), preferred_element_type=F32)
        db_ref[...] = jnp.sum(dm_ref[...], axis=0, keepdims=True)
        parts[k] = lax.dot_general(dms, w_ref[...].astype(BF16), (((1,), (1,)), ((), ())), preferred_element_type=F32)
        sends = [_rcopy(parts.at[k], parts.at[k], send_sems, recv_sems, q // 2 - 1, _flip(q)) for q in (2, 4, 6)]
        for cp in sends:
            cp.start()
        for q in (2, 4, 6):
            kq = _chip_of(_flip(q))
            _rcopy(parts.at[kq], parts.at[kq], send_sems, recv_sems, q // 2 - 1, _flip(q)).wait_recv()
        dact = ((parts[0] + parts[1]) + parts[2]) + parts[3]
        _, vjp = jax.vjp(jax.nn.silu, cc_ref[...])
        dcc_ref[...] = vjp(dact[8:9, :])[0]
        for cp in sends:
            cp.wait_send()

    return pl.pallas_call(
        kern, name="ada_bwd", in_specs=[_VMEM_SPEC] * 5, out_specs=[_VMEM_SPEC] * 3,
        out_shape=[jax.ShapeDtypeStruct((DM, SHARD_ADA), F32), jax.ShapeDtypeStruct((1, 3 * DM), F32),
                   jax.ShapeDtypeStruct((1, DM), F32)],
        scratch_shapes=[pltpu.VMEM((NCHIP, 16, DM), F32), pltpu.SemaphoreType.DMA((3,)), pltpu.SemaphoreType.DMA((3,))],
    )(a_in, dm, dm_shard, w_ada, c_ctx)


def _adamw_math(w, g, m, v):
    m = B1 * m + (1.0 - B1) * g
    v = B2 * v + (1.0 - B2) * (g * g)
    m_hat = m / (1.0 - B1 ** STEP)
    v_hat = v / (1.0 - B2 ** STEP)
    return -LR * (m_hat / (jnp.sqrt(v_hat) + ADAM_EPS) + WD * w), m, v


def adamw_big(w, g, m, v, name, block_rows=256):
    rows, width = w.shape

    def kern(w_ref, g_ref, m_ref, v_ref, d_ref, nm_ref, nv_ref):
        d_ref[...], nm_ref[...], nv_ref[...] = _adamw_math(w_ref[...], g_ref[...], m_ref[...], v_ref[...])

    spec = pl.BlockSpec((block_rows, width), lambda i: (i, 0))
    return pl.pallas_call(
        kern, name=name, grid=(rows // block_rows,), in_specs=[spec] * 4, out_specs=[spec] * 3,
        out_shape=[jax.ShapeDtypeStruct((rows, width), F32)] * 3,
        compiler_params=_cparams(("arbitrary",)),
    )(w, g, m, v)


def adamw_small(quads):
    n = len(quads)

    def kern(*refs):
        ins, outs = refs[:4 * n], refs[4 * n:]
        for i in range(n):
            w, g, m, v = (r[...] for r in ins[4 * i:4 * i + 4])
            outs[3 * i][...], outs[3 * i + 1][...], outs[3 * i + 2][...] = _adamw_math(w, g, m, v)

    flat = [a for quad in quads for a in quad]
    res = pl.pallas_call(
        kern, name="adamw_small", in_specs=[_VMEM_SPEC] * (4 * n), out_specs=[_VMEM_SPEC] * (3 * n),
        out_shape=[jax.ShapeDtypeStruct(q[0].shape, F32) for q in quads for _ in range(3)],
    )(*flat)
    return [tuple(res[3 * i:3 * i + 3]) for i in range(n)]


def _rows_of(a, rows):
    flat = a.reshape(-1)
    return jnp.pad(flat, (0, rows * DM - flat.shape[0])).reshape(rows, DM)


def kernel(x, c, ctx, c_ctx, w_ada, b_ada, norm_g, w_in, sgu_norm_g, w_spatial, b_spatial, q_norm_g, k_norm_g, rpb, w_out, loss_target, m_c_ctx, m_w_ada, m_b_ada, m_norm_g, m_w_in, m_sgu_norm_g, m_w_spatial, m_b_spatial, m_q_norm_g, m_k_norm_g, m_rpb, m_w_out, v_c_ctx, v_w_ada, v_b_ada, v_norm_g, v_w_in, v_sgu_norm_g, v_w_spatial, v_b_spatial, v_q_norm_g, v_k_norm_g, v_rpb, v_w_out):
    xi, yi, ci = lax.axis_index("x"), lax.axis_index("y"), lax.axis_index("c")
    chip, dev = 2 * xi + yi, 4 * xi + 2 * yi + ci
    c_ctx2 = c_ctx.reshape(1, DM)

    b_shard = lax.dynamic_slice(b_ada, (0, chip * SHARD_ADA), (1, SHARD_ADA))
    mod_all, cs = ada_fwd(c, c_ctx2, w_ada[0], b_shard)
    mods = mod_all.transpose(1, 0, 2).reshape(CS_ROWS, 3 * DM)
    mod = lax.dynamic_slice(mods, (8 * dev, 0), (1, 3 * DM))
    cmod = mods[8 * NDEV:8 * NDEV + 1]

    part = local_step(chip.reshape(1).astype(jnp.int32), x[0], ctx[0], loss_target[0], mod, cmod, norm_g, sgu_norm_g,
                      w_spatial[0], b_spatial[0], q_norm_g, k_norm_g, rpb[0], w_in[0], w_out[0])

    slab = jnp.concatenate([
        part["d_norm_g"], _rows_of(part["d_sgu_g"], 1), _rows_of(part["d_b_s"], 1),
        _rows_of(jnp.concatenate([part["d_q_g"], part["d_k_g"]], axis=-1), 1), _rows_of(part["d_rpb"], 4),
        _rows_of(part["loss"], 1), _rows_of(part["dcmod"], 3), _rows_of(part["dmod"], 3), jnp.zeros((1, DM), F32),
        _rows_of(part["d_w_s"], 64)], axis=0)
    g_w_in, g_w_out, gathered, tot = final_reduce(*part["rs"], slab)
    dm = jnp.concatenate([gathered[:, 12:15, :].reshape(NDEV, 3 * DM), tot[9:12].reshape(1, 3 * DM),
                          jnp.zeros((7, 3 * DM), F32)], axis=0)
    a_in = jnp.concatenate([cs[0:8 * NDEV:8], cs[8 * NDEV:8 * NDEV + 1], jnp.zeros((7, DM), F32)], axis=0)
    dm_shard = lax.dynamic_slice(dm, (0, chip * SHARD_ADA), (16, SHARD_ADA))
    g_w_ada, g_b_ada, g_c_ctx = ada_bwd(a_in, dm, dm_shard, w_ada[0], c_ctx2)

    loss = tot[8, 0]
    g_small = dict(
        c_ctx=g_c_ctx, b_ada=g_b_ada, norm_g=tot[0:1], sgu_norm_g=tot[1:2, :512], w_spatial=tot[16:80].reshape(512, 128),
        b_spatial=tot[2:3, :512].reshape(4, 128), q_norm_g=tot[3:4, :HDIM], k_norm_g=tot[3:4, HDIM:2 * HDIM],
        rpb=tot[4:8].reshape(-1)[:HEADS * 15 * 31].reshape(HEADS * 15, 31))
    shapes = dict(c_ctx=(DM,), w_ada=(1, DM, SHARD_ADA), b_ada=(1, 3 * DM), norm_g=(1, DM), w_in=(1, DM, SHARD_IN),
                  sgu_norm_g=(1, 512), w_spatial=(1, 4, 128, 128), b_spatial=(1, 4, 128), q_norm_g=(1, HDIM),
                  k_norm_g=(1, HDIM), rpb=(1, HEADS, 15, 31), w_out=(1, SHARD_OUT, DM))
    names = list(shapes)
    weights = dict(c_ctx=c_ctx, w_ada=w_ada, b_ada=b_ada, norm_g=norm_g, w_in=w_in, sgu_norm_g=sgu_norm_g,
                   w_spatial=w_spatial, b_spatial=b_spatial, q_norm_g=q_norm_g, k_norm_g=k_norm_g, rpb=rpb, w_out=w_out)
    m_in = dict(zip(names, (m_c_ctx, m_w_ada, m_b_ada, m_norm_g, m_w_in, m_sgu_norm_g, m_w_spatial, m_b_spatial,
                            m_q_norm_g, m_k_norm_g, m_rpb, m_w_out)))
    v_in = dict(zip(names, (v_c_ctx, v_w_ada, v_b_ada, v_norm_g, v_w_in, v_sgu_norm_g, v_w_spatial, v_b_spatial,
                            v_q_norm_g, v_k_norm_g, v_rpb, v_w_out)))
    grads = dict(g_small, w_ada=g_w_ada, w_in=g_w_in, w_out=g_w_out)
    upd = {}
    for n in ("w_ada", "w_in", "w_out"):
        g = grads[n]
        upd[n] = adamw_big(weights[n].reshape(g.shape), g, m_in[n].reshape(g.shape), v_in[n].reshape(g.shape),
                           "adamw_" + n)
    small = [n for n in names if n not in upd]
    res = adamw_small([(weights[n].reshape(grads[n].shape), grads[n], m_in[n].reshape(grads[n].shape),
                        v_in[n].reshape(grads[n].shape)) for n in small])
    upd.update(zip(small, res))
    out = [loss, part["grad_x"].reshape(1, SEQ, DM)]
    out += [grads[n].reshape(shapes[n]) for n in names]
    for slot in range(3):
        out += [upd[n][slot].reshape(shapes[n]) for n in names]
    return tuple(out)
```

```python
import jax
import jax.numpy as jnp
from jax import lax
from jax.experimental import pallas as pl
from jax.experimental.pallas import tpu as pltpu

F32, BF16 = jnp.float32, jnp.bfloat16
SEQ, DM, CTX, DIN = 4096, 1024, 256, 3584
NCHIP, NDEV = 4, 8
SHARD_IN = DIN // NCHIP
SHARD_ADA = 3 * DM // NCHIP
SHARD_OUT = DM // NCHIP
GRID_W = 64
QROWS = 4
KROWS = 12
QBLK, KBLK = QROWS * GRID_W, KROWS * GRID_W
NQBLK = SEQ // QBLK
HEADS, HDIM, NPAIR = 8, 64, 4
EPS = 1e-6
NEG_INF = -1e30
ZQ, ZK, ZV, ZG = 12, 16, 20, 24
LR, B1, B2, ADAM_EPS, WD, STEP = 0.001, 0.9, 0.999, 1e-08, 0.01, 10
VMEM_BIG = 56 * 1024 * 1024
MESH_ID = pl.DeviceIdType.MESH


def _dot(a, b, lhs_c, rhs_c):
    return lax.dot_general(a.astype(BF16), b.astype(BF16), (((lhs_c,), (rhs_c,)), ((), ())),
                           preferred_element_type=F32)


@jax.custom_vjp
def mm(a, b):
    return _dot(a, b, 1, 0)


@jax.custom_vjp
def mm_nt(a, b):
    return _dot(a, b, 1, 1)


@jax.custom_vjp
def mm_tn(a, b):
    return _dot(a, b, 0, 0)


mm.defvjp(lambda a, b: (mm(a, b), (a, b)), lambda r, ct: (mm_nt(ct, r[1]), mm_tn(r[0], ct)))
mm_nt.defvjp(lambda a, b: (mm_nt(a, b), (a, b)), lambda r, ct: (mm(ct, r[1]), mm_tn(ct, r[0])))
mm_tn.defvjp(lambda a, b: (mm_tn(a, b), (a, b)), lambda r, ct: (mm_nt(r[1], ct), mm(r[0], ct)))


def _rms(x, g):
    return x * lax.rsqrt(jnp.mean(x * x, axis=-1, keepdims=True) + EPS) * g


def _modulated(x, g, scale, shift):
    return _rms(x, g) * (1.0 + scale) + shift


def _pair_rms(x, g2):
    lo = lax.broadcasted_iota(jnp.int32, (1, 2 * HDIM), 1) < HDIM
    sq = x * x
    s_lo = jnp.sum(jnp.where(lo, sq, 0.0), axis=-1, keepdims=True)
    s_hi = jnp.sum(jnp.where(lo, 0.0, sq), axis=-1, keepdims=True)
    rs = jnp.where(lo, lax.rsqrt(s_lo / HDIM + EPS), lax.rsqrt(s_hi / HDIM + EPS))
    return x * rs * g2


def _cparams(sem, vmem=None):
    return pltpu.CompilerParams(dimension_semantics=sem, vmem_limit_bytes=vmem)


def _row(n):
    return pl.BlockSpec((1, n), lambda *_: (0, 0))


CS_ROWS = 8 * NDEV + 8


def _mod_part(mod_ref, row, part):
    pieces = []
    for j in range(NCHIP):
        lo, hi = max(part * DM, j * SHARD_ADA), min((part + 1) * DM, (j + 1) * SHARD_ADA)
        if lo < hi:
            pieces.append(mod_ref[j, row, lo - j * SHARD_ADA:hi - j * SHARD_ADA])
    return jnp.concatenate(pieces, axis=-1)


def inproj_fwd(chip, x, c_vec, c_ctx, w_ada, b_shard, norm_g, w_shard, wo_shard):
    tl = 1024
    nt = SEQ // tl
    halves = (DM // 2, SHARD_OUT // 2)
    n_w, n_c = 12, NDEV - 1

    def kern(k_ref, x_ref, cv_ref, cc_ref, wa_ref, b_ref, g_ref, w_ref, wo_ref,
             z_ref, h_ref, wfull_ref, wofull_ref, modall_ref, csall_ref,
             w_scr, wo_scr, h_scr, mine, cs_scr, mod_scr, shsc_scr, send_sems, recv_sems):
        s, t = pl.program_id(0), pl.program_id(1)
        xi, yi, c = _me()
        k, me = 2 * xi + yi, 4 * xi + 2 * yi + c
        sib = _flip(1)
        rows = pl.ds(pl.multiple_of(t * tl, tl), tl)
        gathered = (w_scr, wo_scr)
        slot = lambda d: pl.ds(pl.multiple_of(8 * d, 8), 8)

        def c_copy(q, owner):
            return _rcopy(mine, cs_scr.at[slot(owner), :], send_sems, recv_sems, n_w + q - 1, _flip(q))

        def m_copy(q, chip_of_block):
            return _rcopy(mod_scr.at[chip_of_block], mod_scr.at[chip_of_block], send_sems, recv_sems,
                          n_w + n_c + q // 2 - 1, _flip(q))

        def adaln():
            first = lax.broadcasted_iota(jnp.int32, (8, DM), 0) == 0
            mine[...] = jnp.where(first, jnp.broadcast_to(cv_ref[...], (8, DM)), 0.0)
            cs_scr[slot(me), :] = mine[...]
            cs_scr[slot(NDEV), :] = jnp.where(first, jnp.broadcast_to(cc_ref[...], (8, DM)), 0.0)
            for q in range(1, NDEV):
                c_copy(q, me).start()
            wa = wa_ref[...].astype(BF16)
            for q in range(1, NDEV):
                px, py, pc = _flip(q)
                c_copy(q, 4 * px + 2 * py + pc).wait_recv()
            act = jax.nn.silu(cs_scr[...]).astype(BF16)
            mod_scr[k] = jnp.dot(act, wa, preferred_element_type=F32) + b_ref[...]
            for q in (2, 4, 6):
                m_copy(q, k).start()
            for q in (2, 4, 6):
                m_copy(q, _chip_of(_flip(q))).wait_recv()
            row = pl.ds(8 * me, 1)
            shsc_scr[0:1, :] = _mod_part(mod_scr, row, 0)
            shsc_scr[1:2, :] = _mod_part(mod_scr, row, 1)
            pltpu.sync_copy(mod_scr, modall_ref)
            pltpu.sync_copy(cs_scr, csall_ref)

        def block(n, chip_of_block, hh):
            return gathered[n].at[chip_of_block, pl.ds(pl.multiple_of(hh * halves[n], halves[n]), halves[n]), :]

        def ici(n, q, chip_of_block):
            blk = block(n, chip_of_block, c)
            return _rcopy(blk, blk, send_sems, recv_sems, 6 * n + q // 2 - 1, _flip(q))

        def d2d(n, q, chip_of_block, hh):
            blk = block(n, chip_of_block, hh)
            return _rcopy(blk, blk, send_sems, recv_sems, 6 * n + 3 + q // 2 - 1, sib)

        @pl.when((s == 0) & (t == 0))
        def _():
            adaln()
            w_scr[k] = w_ref[...].astype(BF16)
            wo_scr[k] = wo_ref[...].astype(BF16)
            for q in (2, 4, 6):
                ici(0, q, k).start()
                ici(1, q, k).start()

        for sweep in (1, 2, 3):
            @pl.when((s == sweep) & (t == 0))
            def _():
                q = 2 * sweep
                src = _chip_of(_flip(q))
                for n in (0, 1):
                    ici(n, q, src).wait_recv()
                    d2d(n, q, src, c).start()
                for n in (0, 1):
                    d2d(n, q, src, 1 - c).wait_recv()

        @pl.when(s == 0)
        def _():
            hb = _modulated(x_ref[...], g_ref[...], shsc_scr[1:2, :], shsc_scr[0:1, :]).astype(BF16)
            h_scr[rows, :] = hb
            h_ref[...] = hb

        z_ref[...] = jnp.dot(h_scr[rows, :], w_scr[lax.bitwise_xor(k, s)], preferred_element_type=F32)

        @pl.when((s == NCHIP - 1) & (t == nt - 1))
        def _():
            for q in range(1, NDEV):
                c_copy(q, me).wait_send()
            for q in (2, 4, 6):
                m_copy(q, k).wait_send()
            for n in (0, 1):
                for q in (2, 4, 6):
                    ici(n, q, k).wait_send()
                    d2d(n, q, _chip_of(_flip(q)), c).wait_send()
            pltpu.sync_copy(w_scr, wfull_ref)
            pltpu.sync_copy(wo_scr, wofull_ref)

    once = lambda s, t, k: (jnp.where(s == 0, t, nt - 1), 0)
    hbm = pl.BlockSpec(memory_space=pl.ANY)
    n_sem = n_w + n_c + 3
    return pl.pallas_call(
        kern, name="inproj_fwd",
        grid_spec=pltpu.PrefetchScalarGridSpec(
            num_scalar_prefetch=1, grid=(NCHIP, nt),
            in_specs=[pl.BlockSpec((tl, DM), once)] + [_VMEM_SPEC] * 7,
            out_specs=[pl.BlockSpec((tl, SHARD_IN), lambda s, t, k: (t, lax.bitwise_xor(k[0], s))),
                       pl.BlockSpec((tl, DM), once), hbm, hbm, hbm, hbm],
            scratch_shapes=[pltpu.VMEM((NCHIP, DM, SHARD_IN), BF16), pltpu.VMEM((NCHIP, SHARD_OUT, DM), BF16),
                            pltpu.VMEM((SEQ, DM), BF16), pltpu.VMEM((8, DM), F32), pltpu.VMEM((CS_ROWS, DM), F32),
                            pltpu.VMEM((NCHIP, CS_ROWS, SHARD_ADA), F32), pltpu.VMEM((8, DM), F32),
                            pltpu.SemaphoreType.DMA((n_sem,)), pltpu.SemaphoreType.DMA((n_sem,))]),
        out_shape=[jax.ShapeDtypeStruct((SEQ, DIN), F32), jax.ShapeDtypeStruct((SEQ, DM), BF16),
                   jax.ShapeDtypeStruct((NCHIP, DM, SHARD_IN), BF16), jax.ShapeDtypeStruct((NCHIP, SHARD_OUT, DM), BF16),
                   jax.ShapeDtypeStruct((NCHIP, CS_ROWS, SHARD_ADA), F32), jax.ShapeDtypeStruct((CS_ROWS, DM), F32)],
        compiler_params=_cparams(("arbitrary", "arbitrary"), VMEM_BIG),
    )(chip, x, c_vec, c_ctx, w_ada, b_shard, norm_g, w_shard, wo_shard)


def ctx_fwd(ctx, cshift, cscale, norm_g, w_full):
    def kern(c_ref, sh_ref, sc_ref, g_ref, w2_ref, w3_ref, zc_ref, hc_ref):
        hc = _modulated(c_ref[...], g_ref[...], sc_ref[...], sh_ref[...]).astype(BF16)
        hc_ref[...] = hc
        zc_ref[:, :SHARD_IN] = jnp.dot(hc, w2_ref[0], preferred_element_type=F32)
        zc_ref[:, SHARD_IN:] = jnp.dot(hc, w3_ref[0], preferred_element_type=F32)

    return pl.pallas_call(
        kern, name="ctx_fwd", grid=(1,),
        in_specs=[pl.BlockSpec((CTX, DM), lambda i: (0, 0)), _row(DM), _row(DM), _row(DM),
                  pl.BlockSpec((1, DM, SHARD_IN), lambda i: (2, 0, 0)),
                  pl.BlockSpec((1, DM, SHARD_IN), lambda i: (3, 0, 0))],
        out_specs=[pl.BlockSpec((CTX, 2 * SHARD_IN), lambda i: (0, 0)),
                   pl.BlockSpec((CTX, DM), lambda i: (0, 0))],
        out_shape=[jax.ShapeDtypeStruct((CTX, 2 * SHARD_IN), F32), jax.ShapeDtypeStruct((CTX, DM), BF16)],
        compiler_params=_cparams(("arbitrary",)),
    )(ctx, cshift, cscale, norm_g, w_full, w_full)


SGU_CHUNK, SGU_PER_STEP = 128, 4


def _gelu(x):
    return 0.5 * x * (1.0 + lax.erf(x * 0.7071067811865476))


def _sgu_chunk(au, av, ag, sg, ws, bsb):
    u, v = _gelu(au), _gelu(av)
    outs = []
    for g in range(4):
        sl = slice(128 * g, 128 * (g + 1))
        mixed = mm(ws[g], _rms(v[:, sl], sg[:, sl])) + bsb[g]
        outs.append(u[:, sl] * mixed * jax.nn.silu(ag[:, sl]))
    return jnp.concatenate(outs, axis=-1)


def _sgu_specs():
    rows = SGU_CHUNK * SGU_PER_STEP
    zspec = lambda c: pl.BlockSpec((rows, 512), lambda n: (n, c))
    wspec = pl.BlockSpec((4, 128, 128), lambda n: (0, 0, 0))
    return rows, [zspec(0), zspec(1), zspec(2), _row(512), wspec, wspec]


def sgu_fwd(z, sg, ws, bsb):
    rows, in_specs = _sgu_specs()

    def kern(au_ref, av_ref, ag_ref, sg_ref, ws_ref, bs_ref, o_ref):
        for c in range(SGU_PER_STEP):
            sl = slice(c * SGU_CHUNK, (c + 1) * SGU_CHUNK)
            o_ref[sl, :] = _sgu_chunk(au_ref[sl, :], av_ref[sl, :], ag_ref[sl, :], sg_ref[...], ws_ref[...],
                                      bs_ref[...])

    return pl.pallas_call(
        kern, name="sgu_fwd", grid=(SEQ // rows,), in_specs=in_specs,
        out_specs=pl.BlockSpec((rows, 512), lambda n: (n, 0)),
        out_shape=jax.ShapeDtypeStruct((SEQ, 512), F32),
        compiler_params=_cparams(("arbitrary",)),
    )(z, z, z, sg, ws, bsb)


def sgu_bwd(z, sg, ws, bsb, dcat):
    rows, in_specs = _sgu_specs()

    def kern(au_ref, av_ref, ag_ref, sg_ref, ws_ref, bs_ref, do_ref, dz_ref, dsg_ref, dws_ref, dbs_ref):
        @pl.when(pl.program_id(0) == 0)
        def _():
            dsg_ref[...] = jnp.zeros_like(dsg_ref)
            dws_ref[...] = jnp.zeros_like(dws_ref)
            dbs_ref[...] = jnp.zeros_like(dbs_ref)

        for c in range(SGU_PER_STEP):
            sl = slice(c * SGU_CHUNK, (c + 1) * SGU_CHUNK)
            _, vjp = jax.vjp(_sgu_chunk, au_ref[sl, :], av_ref[sl, :], ag_ref[sl, :], sg_ref[...], ws_ref[...],
                             bs_ref[...])
            dau, dav, dag, dsg, dws, dbs = vjp(do_ref[sl, :])
            dz_ref[sl, 0:512] = dau.astype(BF16)
            dz_ref[sl, 512:1024] = dav.astype(BF16)
            dz_ref[sl, 1024:1536] = dag.astype(BF16)
            dsg_ref[...] += dsg
            dws_ref[...] += dws
            dbs_ref[...] += dbs

        @pl.when(pl.program_id(0) == pl.num_programs(0) - 1)
        def _():
            dbs_ref[...] = jnp.broadcast_to(jnp.sum(dbs_ref[...], axis=-1, keepdims=True), dbs_ref.shape)

    wspec = pl.BlockSpec((4, 128, 128), lambda n: (0, 0, 0))
    return pl.pallas_call(
        kern, name="sgu_bwd", grid=(SEQ // rows,),
        in_specs=in_specs + [pl.BlockSpec((rows, 512), lambda n: (n, 0))],
        out_specs=[pl.BlockSpec((rows, 1536), lambda n: (n, 0)), _row(512), wspec, wspec],
        out_shape=[jax.ShapeDtypeStruct((SEQ, 1536), BF16), jax.ShapeDtypeStruct((1, 512), F32),
                   jax.ShapeDtypeStruct((4, 128, 128), F32), jax.ShapeDtypeStruct((4, 128, 128), F32)],
        compiler_params=_cparams(("arbitrary",)),
    )(z, z, z, sg, ws, bsb, dcat)


_DR_OFF = (7, 3, -1)


def _row_valid(v, rr, j):
    return (j < 8, rr <= j < rr + 8, 4 <= j < 12)[v]


def _col_window():
    q = lax.broadcasted_iota(jnp.int32, (GRID_W, 128), 0)
    kc = lax.broadcasted_iota(jnp.int32, (GRID_W, 128), 1) % GRID_W
    c0 = jnp.clip(q - 8, 0, GRID_W - 16)
    return (kc >= c0) & (kc < c0 + 16)


def rpb_tables(rpb2):
    def kern(r_ref, b_ref):
        base = r_ref[0]
        lo = lax.broadcasted_iota(jnp.int32, (1, 128), 1) < GRID_W
        win = _col_window()
        tiles = {}
        for v in range(3):
            for rr in range(QROWS):
                for jp in range(KROWS // 2):
                    j0, j1 = 2 * jp, 2 * jp + 1
                    ok0, ok1 = _row_valid(v, rr, j0), _row_valid(v, rr, j1)
                    key = (j0 - rr + _DR_OFF[v], ok0, ok1) if (ok0 or ok1) else None
                    if key not in tiles:
                        if key is None:
                            tiles[key] = jnp.full((GRID_W, 128), NEG_INF, F32)
                        else:
                            d0 = key[0]
                            r0 = base[d0:d0 + 1, :] if ok0 else jnp.zeros((1, 128), F32)
                            r1 = base[d0 + 1:d0 + 2, :] if ok1 else jnp.zeros((1, 128), F32)
                            y = jnp.broadcast_to(jnp.where(lo, r0, r1), (GRID_W, 128))
                            y = pltpu.roll(pltpu.roll(y, 128 - 15, 1), 0, 1, stride=1, stride_axis=0)
                            tiles[key] = jnp.where(win & jnp.where(lo, ok0, ok1), y, NEG_INF)
                    b_ref[v, 0, rr * GRID_W:(rr + 1) * GRID_W, jp * 128:(jp + 1) * 128] = tiles[key]

    return pl.pallas_call(
        kern, name="rpb_tables", grid=(HEADS,),
        in_specs=[pl.BlockSpec((1, 15, 128), lambda h: (h, 0, 0))],
        out_specs=pl.BlockSpec((3, 1, QBLK, KBLK), lambda h: (0, h, 0, 0)),
        out_shape=jax.ShapeDtypeStruct((3, HEADS, QBLK, KBLK), F32),
        compiler_params=_cparams(("arbitrary",)),
    )(rpb2)


def rpb_bwd(dbias):
    def kern(g_ref, o_ref):
        lo = lax.broadcasted_iota(jnp.int32, (1, 128), 1) < GRID_W
        ri = lax.broadcasted_iota(jnp.int32, (GRID_W, GRID_W), 0)
        ci = lax.broadcasted_iota(jnp.int32, (GRID_W, GRID_W), 1)
        flip = (ri + ci == GRID_W - 1).astype(F32)
        groups = {}
        for v in range(3):
            for rr in range(QROWS):
                for jp in range(KROWS // 2):
                    j0, j1 = 2 * jp, 2 * jp + 1
                    ok0, ok1 = _row_valid(v, rr, j0), _row_valid(v, rr, j1)
                    if not (ok0 or ok1):
                        continue
                    g = g_ref[v, 0, rr * GRID_W:(rr + 1) * GRID_W, jp * 128:(jp + 1) * 128]
                    key = (j0 - rr + _DR_OFF[v], ok0, ok1)
                    groups[key] = g if key not in groups else groups[key] + g
        acc = [jnp.zeros((1, 128), F32) for _ in range(15)]
        for (d0, ok0, ok1), g in groups.items():
            g = lax.dot_general(flip, g, (((1,), (0,)), ((), ())), precision=lax.Precision.HIGHEST,
                                preferred_element_type=F32)
            g = pltpu.roll(pltpu.roll(g, 128 - 48, 1), 0, 1, stride=1, stride_axis=0)
            s = jnp.sum(g, axis=0, keepdims=True)
            if ok0:
                acc[d0] = acc[d0] + jnp.where(lo, s, 0.0)
            if ok1:
                acc[d0 + 1] = acc[d0 + 1] + jnp.where(lo, 0.0, s)
        for d in range(15):
            o_ref[0, d:d + 1, :] = acc[d] + pltpu.roll(acc[d], GRID_W, 1)

    return pl.pallas_call(
        kern, name="rpb_bwd", grid=(HEADS,),
        in_specs=[pl.BlockSpec((3, 1, QBLK, KBLK), lambda h: (0, h, 0, 0))],
        out_specs=pl.BlockSpec((1, 15, 128), lambda h: (h, 0, 0)),
        out_shape=jax.ShapeDtypeStruct((HEADS, 15, 128), F32),
        compiler_params=_cparams(("arbitrary",)),
    )(dbias)


def _scaled_q(q_raw, qg):
    return _pair_rms(q_raw, qg) * (HDIM ** -0.5)


def _head_lanes():
    lo = lax.broadcasted_iota(jnp.int32, (1, 2 * HDIM), 1) < HDIM
    return lo, jnp.logical_not(lo)


def _attn_step(q_raw, kn, v, ckn, cv, bias2, qg):
    qn = _scaled_q(q_raw, qg)
    out = lse = None
    for a, mine in enumerate(_head_lanes()):
        qa = jnp.where(mine, qn, 0.0)
        s_lat = mm_nt(qa, kn) + bias2[a]
        s_ctx = mm_nt(qa, ckn)
        m = jnp.maximum(jnp.max(s_lat, axis=-1, keepdims=True), jnp.max(s_ctx, axis=-1, keepdims=True))
        p_lat = jnp.exp(s_lat - m)
        p_ctx = jnp.exp(s_ctx - m)
        den = jnp.sum(p_lat, axis=-1, keepdims=True) + jnp.sum(p_ctx, axis=-1, keepdims=True)
        o = jnp.where(mine, (mm(p_lat, v) + mm(p_ctx, cv)) / den, 0.0)
        l = jnp.where(mine, m + jnp.log(den), 0.0)
        out, lse = (o, l) if out is None else (out + o, lse + l)
    return out, lse


def _attn_step_bwd(q_raw, kn, v, ckn, cv, bias2, qg, bg, o, lse, dout):
    sig = jax.nn.sigmoid(bg)
    do = dout * (bg * sig)
    dbg = dout * o * (sig * (1.0 + bg * (1.0 - sig)))
    qn, qn_vjp = jax.vjp(_scaled_q, q_raw, qg)
    row_dot = do * o
    dqn = dkn = dv = dckn = dcv = None
    dbias = []
    for mine in _head_lanes():
        qa = jnp.where(mine, qn, 0.0)
        doa = jnp.where(mine, do, 0.0)
        l = jnp.max(jnp.where(mine, lse, NEG_INF), axis=-1, keepdims=True)
        delta = jnp.sum(jnp.where(mine, row_dot, 0.0), axis=-1, keepdims=True)
        p_lat = jnp.exp(mm_nt(qa, kn) + bias2[len(dbias)] - l)
        p_ctx = jnp.exp(mm_nt(qa, ckn) - l)
        ds_lat = p_lat * (mm_nt(doa, v) - delta)
        ds_ctx = p_ctx * (mm_nt(doa, cv) - delta)
        parts = (jnp.where(mine, mm(ds_lat, kn) + mm(ds_ctx, ckn), 0.0), mm_tn(ds_lat, qa), mm_tn(p_lat, doa),
                 mm_tn(ds_ctx, qa), mm_tn(p_ctx, doa))
        if dqn is None:
            dqn, dkn, dv, dckn, dcv = parts
        else:
            dqn, dkn, dv, dckn, dcv = (acc + new for acc, new in zip((dqn, dkn, dv, dckn, dcv), parts))
        dbias.append(ds_lat)
    dq, dqg = qn_vjp(dqn)
    return dq, dkn, dv, dckn, dcv, dbias, dqg, dbg


def _kstart(i):
    return pl.multiple_of(jnp.clip((i - 1) * QBLK, 0, SEQ - KBLK), QBLK)


def _bias_variant(i):
    return jnp.where(i == 0, 0, jnp.where(i == NQBLK - 1, 2, 1))


def _attn_in_specs():
    return [
        pl.BlockSpec((QBLK, 128), lambda p, i: (i, ZQ + p)),
        pl.BlockSpec((SEQ, 128), lambda p, i: (0, ZK + p)),
        pl.BlockSpec((SEQ, 128), lambda p, i: (0, ZV + p)),
        pl.BlockSpec((QBLK, 128), lambda p, i: (i, ZG + p)),
        pl.BlockSpec((CTX, 128), lambda p, i: (0, 2 + p)),
        pl.BlockSpec((CTX, 128), lambda p, i: (0, 6 + p)),
        pl.BlockSpec((1, 2, QBLK, KBLK), lambda p, i: (_bias_variant(i), p, 0, 0)),
        _row(128), _row(128),
    ]


NORM_ROWS = 512


def _norm_keys(k_ref, ck_ref, kg_ref, kn_scr, ckn_scr):
    def body(c, carry):
        sl = pl.ds(pl.multiple_of(c * NORM_ROWS, NORM_ROWS), NORM_ROWS)
        kn_scr[sl, :] = _pair_rms(k_ref[sl, :], kg_ref[...])
        return carry

    lax.fori_loop(0, SEQ // NORM_ROWS, body, 0)
    ckn_scr[...] = _pair_rms(ck_ref[...], kg_ref[...])


def attn_fwd(z, zc, bias, qg2, kg2):
    def kern(q_ref, k_ref, v_ref, bg_ref, ck_ref, cv_ref, b_ref, qg_ref, kg_ref, ob_ref, o_ref, lse_ref, kn_scr,
             ckn_scr):
        i = pl.program_id(1)

        @pl.when(i == 0)
        def _():
            _norm_keys(k_ref, ck_ref, kg_ref, kn_scr, ckn_scr)

        ks = pl.ds(_kstart(i), KBLK)
        o, lse = _attn_step(q_ref[...], kn_scr[ks, :], v_ref[ks, :], ckn_scr[...], cv_ref[...], b_ref[0], qg_ref[...])
        ob_ref[...] = o * jax.nn.silu(bg_ref[...])
        o_ref[...] = o
        lse_ref[...] = lse

    qblk = pl.BlockSpec((QBLK, 128), lambda p, i: (i, p))
    return pl.pallas_call(
        kern, name="attn_fwd", grid=(NPAIR, NQBLK), in_specs=_attn_in_specs(), out_specs=[qblk] * 3,
        out_shape=[jax.ShapeDtypeStruct((SEQ, 512), F32)] * 3,
        scratch_shapes=[pltpu.VMEM((SEQ, 128), F32), pltpu.VMEM((CTX, 128), F32)],
        compiler_params=_cparams(("arbitrary", "arbitrary"), 40 * 1024 * 1024),
    )(z, z, z, z, zc, zc, bias, qg2, kg2)


def attn_bwd(z, zc, bias, qg2, kg2, dcat, o_raw, lse):
    def kern(q_ref, k_ref, v_ref, bg_ref, ck_ref, cv_ref, b_ref, qg_ref, kg_ref, do_ref, o_ref, lse_ref,
             dq_ref, dk_ref, dv_ref, dbg_ref, dck_ref, dcv_ref, db_ref, dqg_ref, dkg_ref,
             kn_scr, ckn_scr, dkn_scr, dckn_scr, dv_scr):
        p, i = pl.program_id(0), pl.program_id(1)
        last = i == NQBLK - 1

        @pl.when(i == 0)
        def _():
            _norm_keys(k_ref, ck_ref, kg_ref, kn_scr, ckn_scr)
            dkn_scr[...] = jnp.zeros_like(dkn_scr)
            dv_scr[...] = jnp.zeros_like(dv_scr)
            dckn_scr[...] = jnp.zeros_like(dckn_scr)
            dcv_ref[...] = jnp.zeros_like(dcv_ref)

        @pl.when((i == 0) & (p == 0))
        def _():
            dqg_ref[...] = jnp.zeros_like(dqg_ref)
            dkg_ref[...] = jnp.zeros_like(dkg_ref)

        ks = pl.ds(_kstart(i), KBLK)
        dq, dkn, dv, dckn, dcv, db, dqg, dbg = _attn_step_bwd(
            q_ref[...], kn_scr[ks, :], v_ref[ks, :], ckn_scr[...], cv_ref[...], b_ref[0], qg_ref[...], bg_ref[...],
            o_ref[...], lse_ref[...], do_ref[...])
        dq_ref[...] = dq.astype(BF16)
        dbg_ref[...] = dbg.astype(BF16)
        dkn_scr[ks, :] += dkn
        dv_scr[ks, :] += dv
        dckn_scr[...] += dckn
        dcv_ref[...] += dcv
        dqg_ref[...] += dqg
        fresh = (i == 0) | (i == 1) | last

        @pl.when(fresh)
        def _():
            for a in range(2):
                db_ref[0, a] = db[a]

        @pl.when(jnp.logical_not(fresh))
        def _():
            for a in range(2):
                db_ref[0, a] += db[a]

        @pl.when(last)
        def _():
            def body(c, dkg):
                sl = pl.ds(pl.multiple_of(c * NORM_ROWS, NORM_ROWS), NORM_ROWS)
                _, nvjp = jax.vjp(_pair_rms, k_ref[sl, :], kg_ref[...])
                dk, dg = nvjp(dkn_scr[sl, :])
                dk_ref[sl, :] = dk.astype(BF16)
                dv_ref[sl, :] = dv_scr[sl, :].astype(BF16)
                return dkg + dg

            dkg = lax.fori_loop(0, SEQ // NORM_ROWS, body, jnp.zeros((1, 128), F32))
            _, nvjp = jax.vjp(_pair_rms, ck_ref[...], kg_ref[...])
            dck, dg = nvjp(dckn_scr[...])
            dck_ref[...] = dck
            dkg_ref[...] += dkg + dg

        @pl.when(last & (p == NPAIR - 1))
        def _():
            dqg_ref[...] = dqg_ref[...] + pltpu.roll(dqg_ref[...], HDIM, 1)
            dkg_ref[...] = dkg_ref[...] + pltpu.roll(dkg_ref[...], HDIM, 1)

    blk = lambda rows: pl.BlockSpec((rows, 128), lambda p, i: (0, p))
    qblk = pl.BlockSpec((QBLK, 128), lambda p, i: (i, p))
    return pl.pallas_call(
        kern, name="attn_bwd", grid=(NPAIR, NQBLK),
        in_specs=_attn_in_specs() + [pl.BlockSpec((QBLK, 128), lambda p, i: (i, 4 + p)), qblk, qblk],
        out_specs=[qblk, blk(SEQ), blk(SEQ), qblk, blk(CTX), blk(CTX),
                   pl.BlockSpec((1, 2, QBLK, KBLK), lambda p, i: (_bias_variant(i), p, 0, 0)),
                   _row(128), _row(128)],
        out_shape=[jax.ShapeDtypeStruct((SEQ, 512), BF16)] * 4 + [jax.ShapeDtypeStruct((CTX, 512), F32)] * 2
        + [jax.ShapeDtypeStruct((3, HEADS, QBLK, KBLK), F32), jax.ShapeDtypeStruct((1, 128), F32),
           jax.ShapeDtypeStruct((1, 128), F32)],
        scratch_shapes=[pltpu.VMEM((SEQ, 128), F32), pltpu.VMEM((CTX, 128), F32),
                        pltpu.VMEM((SEQ, 128), F32), pltpu.VMEM((CTX, 128), F32), pltpu.VMEM((SEQ, 128), F32)],
        compiler_params=_cparams(("arbitrary", "arbitrary"), VMEM_BIG),
    )(z, z, z, z, zc, zc, bias, qg2, kg2, dcat, o_raw, lse)


def outproj(out_a, out_b, x, target, gate, wo):
    tl = 512

    def kern(a_ref, b_ref, x_ref, t_ref, g_ref, w_ref, loss_ref, dy_ref, dcat_ref, dg_ref, dw_ref):
        @pl.when(pl.program_id(0) == 0)
        def _():
            loss_ref[...] = jnp.zeros_like(loss_ref)
            dg_ref[...] = jnp.zeros_like(dg_ref)
            dw_ref[...] = jnp.zeros_like(dw_ref)

        a, b = a_ref[...].astype(BF16), b_ref[...].astype(BF16)
        mix = (jnp.dot(a, w_ref[0:512, :], preferred_element_type=F32)
               + jnp.dot(b, w_ref[512:1024, :], preferred_element_type=F32))
        err = x_ref[...] + g_ref[...] * mix - t_ref[...]
        loss_ref[...] += 0.5 * jnp.sum(jnp.mean(err * err, axis=-1))
        dy = err * (1.0 / DM)
        dy_ref[...] = dy
        dg_ref[...] += jnp.sum(dy * mix, axis=0, keepdims=True)
        dmix = (g_ref[...] * dy).astype(BF16)
        dcat_ref[...] = lax.dot_general(dmix, w_ref[...], (((1,), (1,)), ((), ())), preferred_element_type=F32)
        dw_ref[0:512, :] += lax.dot_general(a, dmix, (((0,), (0,)), ((), ())), preferred_element_type=F32)
        dw_ref[512:1024, :] += lax.dot_general(b, dmix, (((0,), (0,)), ((), ())), preferred_element_type=F32)

    tile = lambda w: pl.BlockSpec((tl, w), lambda t: (t, 0))
    whole = pl.BlockSpec((DM, DM), lambda t: (0, 0))
    return pl.pallas_call(
        kern, name="outproj", grid=(SEQ // tl,),
        in_specs=[tile(512), tile(512), tile(DM), tile(DM), _row(DM), whole],
        out_specs=[pl.BlockSpec((8, 128), lambda t: (0, 0)), tile(DM), tile(DM), _row(DM), whole],
        out_shape=[jax.ShapeDtypeStruct((8, 128), F32), jax.ShapeDtypeStruct((SEQ, DM), F32),
                   jax.ShapeDtypeStruct((SEQ, DM), F32), jax.ShapeDtypeStruct((1, DM), F32),
                   jax.ShapeDtypeStruct((DM, DM), F32)],
        compiler_params=_cparams(("arbitrary",), 48 * 1024 * 1024),
    )(out_a, out_b, x, target, gate, wo)


def _pieces(sources):
    out = []
    for name, c0, c1 in sources:
        for j in range(NCHIP):
            lo, hi = max(c0, j * SHARD_IN), min(c1, (j + 1) * SHARD_IN)
            if lo < hi:
                out.append((j, lo - j * SHARD_IN, hi - j * SHARD_IN, name, lo - c0, hi - c0))
    return out


DZ_PIECES = _pieces((("a", 0, 1536), ("q", 1536, 2048), ("k", 2048, 2560), ("v", 2560, 3072), ("g", 3072, DIN)))
DZC_PIECES = _pieces((("k", 2048, 2560), ("v", 2560, 3072)))
_NT = (((1,), (1,)), ((), ()))


DH_SUBTILES = 2


def _dz_specs(tl):
    return [pl.BlockSpec((tl, 1536), lambda t: (t, 0))] + [pl.BlockSpec((tl, 512), lambda t: (t, 0))] * 4


def dh_bwd(dz_parts, w_full, x, dy, shift, scale, norm_g, dg_ctx, wire_i, wire_o):
    tl = 512
    nt = SEQ // tl

    def kern(a_ref, q_ref, k_ref, v_ref, g_ref, w_ref, x_ref, dy_ref, sh_ref, sc_ref, gn_ref, dgc_ref, wi_hbm, wo_hbm,
             gx_ref, dsh_ref, dsc_ref, dg_ref, goti_ref, goto_ref, rcv_i, rcv_o, send_sems, recv_sems):
        def ici(n, q):
            wire, rcv = ((wi_hbm, rcv_i), (wo_hbm, rcv_o))[n]
            return _rcopy(wire.at[_chip_of(_flip(q))], rcv.at[q // 2 - 1], send_sems, recv_sems, 3 * n + q // 2 - 1,
                          _flip(q))

        @pl.when(pl.program_id(0) == 0)
        def _():
            for n in (0, 1):
                for q in (2, 4, 6):
                    ici(n, q).start()

        @pl.when(pl.program_id(0) == 0)
        def _():
            dsh_ref[...] = jnp.zeros_like(dsh_ref)
            dsc_ref[...] = jnp.zeros_like(dsc_ref)
            dg_ref[...] = dgc_ref[...]

        src = dict(a=a_ref, q=q_ref, k=k_ref, v=v_ref, g=g_ref)
        for sub in range(DH_SUBTILES):
            rows = slice(sub * tl // DH_SUBTILES, (sub + 1) * tl // DH_SUBTILES)
            dh = None
            for j, l0, l1, name, s0, s1 in DZ_PIECES:
                part = lax.dot_general(src[name][rows, s0:s1], w_ref[j, :, l0:l1], _NT, preferred_element_type=F32)
                dh = part if dh is None else dh + part
            _, vjp = jax.vjp(_modulated, x_ref[rows, :], gn_ref[...], sc_ref[...], sh_ref[...])
            dx, dg, dsc, dsh = vjp(dh)
            gx_ref[rows, :] = dy_ref[rows, :] + dx
            dg_ref[...] += dg
            dsc_ref[...] += dsc
            dsh_ref[...] += dsh

        @pl.when(pl.program_id(0) == nt - 1)
        def _():
            for n in (0, 1):
                for q in (2, 4, 6):
                    ici(n, q).wait_recv()
                    ici(n, q).wait_send()
            goti_ref[...] = rcv_i[...]
            goto_ref[...] = rcv_o[...]

    tile = pl.BlockSpec((tl, DM), lambda t: (t, 0))
    hbm = pl.BlockSpec(memory_space=pl.ANY)
    got = [(NCHIP - 1, rh, w) for rh, w in RS_SHAPES]
    return pl.pallas_call(
        kern, name="dh_bwd", grid=(nt,),
        in_specs=_dz_specs(tl) + [pl.BlockSpec((NCHIP, DM, SHARD_IN), lambda t: (0, 0, 0)), tile, tile, _row(DM),
                                  _row(DM), _row(DM), _row(DM), hbm, hbm],
        out_specs=[tile, _row(DM), _row(DM), _row(DM)] + [pl.BlockSpec(s, lambda t: (0, 0, 0)) for s in got],
        out_shape=[jax.ShapeDtypeStruct((SEQ, DM), F32)] + [jax.ShapeDtypeStruct((1, DM), F32)] * 3
        + [jax.ShapeDtypeStruct(s, BF16) for s in got],
        scratch_shapes=[pltpu.VMEM(s, BF16) for s in got] + [pltpu.SemaphoreType.DMA((6,)), pltpu.SemaphoreType.DMA((6,))],
        compiler_params=_cparams(("arbitrary",), VMEM_BIG),
    )(*dz_parts, w_full, x, dy, shift, scale, norm_g, dg_ctx, wire_i, wire_o)


def dw_bwd(h, dz_parts, hc, dck, dcv):
    tl = 512

    def kern(h_ref, a_ref, q_ref, k_ref, v_ref, g_ref, hc_ref, dck_ref, dcv_ref, dw_ref):
        @pl.when(pl.program_id(0) == 0)
        def _():
            dw_ref[...] = jnp.zeros_like(dw_ref)
            hct = hc_ref[...].T
            csrc = dict(k=dck_ref, v=dcv_ref)
            for j, l0, l1, name, s0, s1 in DZC_PIECES:
                dw_ref[j, :, l0:l1] += jnp.dot(hct, csrc[name][:, s0:s1].astype(BF16), preferred_element_type=F32)

        ht = h_ref[...].T
        src = dict(a=a_ref, q=q_ref, k=k_ref, v=v_ref, g=g_ref)
        for j, l0, l1, name, s0, s1 in DZ_PIECES:
            dw_ref[j, :, l0:l1] += jnp.dot(ht, src[name][:, s0:s1], preferred_element_type=F32)

    whole = lambda r, c: pl.BlockSpec((r, c), lambda t: (0, 0))
    return pl.pallas_call(
        kern, name="dw_bwd", grid=(SEQ // tl,),
        in_specs=[pl.BlockSpec((tl, DM), lambda t: (t, 0))] + _dz_specs(tl) + [whole(CTX, DM), whole(CTX, 512),
                                                                              whole(CTX, 512)],
        out_specs=pl.BlockSpec((NCHIP, DM, SHARD_IN), lambda t: (0, 0, 0)),
        out_shape=jax.ShapeDtypeStruct((NCHIP, DM, SHARD_IN), F32),
        compiler_params=_cparams(("arbitrary",), VMEM_BIG),
    )(h, *dz_parts, hc, dck, dcv)


def ctx_bwd(dck, dcv, w_full, ctx, cshift, cscale, norm_g):
    def kern(dck_ref, dcv_ref, w_ref, c_ref, sh_ref, sc_ref, g_ref, dsh_ref, dsc_ref, dg_ref):
        csrc = dict(k=dck_ref, v=dcv_ref)
        dhc = None
        for j, l0, l1, name, s0, s1 in DZC_PIECES:
            part = lax.dot_general(csrc[name][:, s0:s1].astype(BF16), w_ref[j, :, l0:l1], _NT,
                                   preferred_element_type=F32)
            dhc = part if dhc is None else dhc + part
        _, vjp = jax.vjp(lambda g, sc, sh: _modulated(c_ref[...], g, sc, sh), g_ref[...], sc_ref[...], sh_ref[...])
        dg_ref[...], dsc_ref[...], dsh_ref[...] = vjp(dhc)

    whole = lambda r, c: pl.BlockSpec((r, c), lambda i: (0, 0))
    return pl.pallas_call(
        kern, name="ctx_bwd", grid=(1,),
        in_specs=[whole(CTX, 512), whole(CTX, 512), pl.BlockSpec((NCHIP, DM, SHARD_IN), lambda i: (0, 0, 0)),
                  whole(CTX, DM), _row(DM), _row(DM), _row(DM)],
        out_specs=[_row(DM), _row(DM), _row(DM)],
        out_shape=[jax.ShapeDtypeStruct((1, DM), F32)] * 3,
        compiler_params=_cparams(("arbitrary",), 40 * 1024 * 1024),
    )(dck, dcv, w_full, ctx, cshift, cscale, norm_g)


def _lane_pad_rpb(rpb):
    r = jnp.pad(rpb, ((0, 0), (0, 0), (0, GRID_W - rpb.shape[-1])))
    return jnp.concatenate([r, r], axis=-1)


def local_step(chip, dev, x, c_vec, c_ctx, w_ada, b_shard, ctx, target, norm_g, sgu_g, w_s, b_s, q_g, k_g, rpb,
               w_in_shard, w_out_shard):
    bsb = jnp.broadcast_to(b_s[:, :, None], (4, 128, 128))
    qg2, kg2 = jnp.tile(q_g, (1, 2)), jnp.tile(k_g, (1, 2))

    z, h, w_in_full, w_out_full, mod_all, cs = inproj_fwd(chip, x, c_vec, c_ctx, w_ada, b_shard, norm_g, w_in_shard,
                                                          w_out_shard)
    mods = mod_all.transpose(1, 0, 2).reshape(CS_ROWS, 3 * DM)
    mod = lax.dynamic_slice(mods, (8 * dev, 0), (1, 3 * DM))
    shift, scale, gate = mod[:, :DM], mod[:, DM:2 * DM], mod[:, 2 * DM:]
    cshift, cscale = mods[8 * NDEV:8 * NDEV + 1, :DM], mods[8 * NDEV:8 * NDEV + 1, DM:2 * DM]
    zc, hc = ctx_fwd(ctx, cshift, cscale, norm_g, w_in_full)
    bias = rpb_tables(_lane_pad_rpb(rpb))
    out_a = sgu_fwd(z, sgu_g, w_s, bsb)
    out_b, o_raw, lse = attn_fwd(z, zc, bias, qg2, kg2)
    loss8, dy, dcat, dgate, dwo = outproj(out_a, out_b, x, target, gate, w_out_full.reshape(DM, DM))
    dz_a, dsg, dws, dbsb = sgu_bwd(z, sgu_g, w_s, bsb, dcat)
    dq, dk, dv, dbg, dck, dcv, dbias, dqg2, dkg2 = attn_bwd(z, zc, bias, qg2, kg2, dcat, o_raw, lse)
    drpb = rpb_bwd(dbias)[:, :, :rpb.shape[-1]]
    dz_parts = (dz_a, dq, dk, dv, dbg)
    dcshift, dcscale, dng_c = ctx_bwd(dck, dcv, w_in_full, ctx, cshift, cscale, norm_g)
    dw_in = dw_bwd(h, dz_parts, hc, dck, dcv)
    wire_i, keep_i, wire_o, keep_o = pair_sum(dw_in, dwo.reshape(NCHIP, SHARD_OUT, DM))
    grad_x, dshift, dscale, dng, got_i, got_o = dh_bwd(dz_parts, w_in_full, x, dy, shift, scale, norm_g, dng_c,
                                                       wire_i, wire_o)
    return dict(
        loss=loss8[0:1, 0:1], grad_x=grad_x, rs=(keep_i, got_i, keep_o, got_o), cs=cs,
        dmod=jnp.concatenate([dshift, dscale, dgate], axis=-1),
        dcmod=jnp.concatenate([dcshift, dcscale, jnp.zeros((1, DM), F32)], axis=-1),
        d_norm_g=dng, d_sgu_g=dsg, d_w_s=dws, d_b_s=dbsb[:, :, 0],
        d_q_g=dqg2[:, :HDIM], d_k_g=dkg2[:, :HDIM], d_rpb=drpb)


def _me():
    return lax.axis_index("x"), lax.axis_index("y"), lax.axis_index("c")


def _flip(q):
    x, y, c = _me()
    return ((1 - x) if q & 4 else x, (1 - y) if q & 2 else y, (1 - c) if q & 1 else c)


def _chip_of(dev):
    return 2 * dev[0] + dev[1]


def _rcopy(src, dst, send_sems, recv_sems, k, dev):
    return pltpu.make_async_remote_copy(src_ref=src, dst_ref=dst, send_sem=send_sems.at[k], recv_sem=recv_sems.at[k],
                                        device_id=dev, device_id_type=MESH_ID)


_VMEM_SPEC = pl.BlockSpec(memory_space=pltpu.VMEM)
SLAB_ROWS = 80


RS_SHAPES = ((DM // 2, SHARD_IN), (SHARD_OUT // 2, DM))


def pair_sum(g_in, g_out):
    def kern(gi_hbm, go_hbm, wire_i, keep_i, wire_o, keep_o, mine_i, rcv_i, mine_o, rcv_o, load_sems, send_sems,
             recv_sems):
        x, y, c = _me()
        k = 2 * x + y
        sib = _flip(1)
        work = ((gi_hbm, mine_i, rcv_i, wire_i, keep_i), (go_hbm, mine_o, rcv_o, wire_o, keep_o))
        copies = []
        for n, (g, mine, rcv, _, _) in enumerate(work):
            rh = RS_SHAPES[n][0]
            half = lambda hh, rh=rh: pl.ds(pl.multiple_of(hh * rh, rh), rh)
            load = pltpu.make_async_copy(g.at[:, half(c), :], mine, load_sems.at[n])
            load.start()
            pair = _rcopy(g.at[:, half(1 - c), :], rcv, send_sems, recv_sems, n, sib)
            pair.start()
            copies.append((load, pair))
        for (load, pair), (_, mine, rcv, wire, keep) in zip(copies, work):
            load.wait()
            pair.wait_recv()
            for j in range(NCHIP):
                wire[j] = (mine[j] + rcv[j]).astype(BF16)
            keep[...] = mine[k] + rcv[k]
        for _, pair in copies:
            pair.wait_send()

    (rhi, wi), (rho, wo) = RS_SHAPES
    hbm = pl.BlockSpec(memory_space=pl.ANY)
    return pl.pallas_call(
        kern, name="pair_sum", in_specs=[hbm, hbm], out_specs=[_VMEM_SPEC] * 4,
        out_shape=[jax.ShapeDtypeStruct((NCHIP, rhi, wi), BF16), jax.ShapeDtypeStruct((rhi, wi), F32),
                   jax.ShapeDtypeStruct((NCHIP, rho, wo), BF16), jax.ShapeDtypeStruct((rho, wo), F32)],
        scratch_shapes=[pltpu.VMEM((NCHIP, rhi, wi), F32), pltpu.VMEM((NCHIP, rhi, wi), F32),
                        pltpu.VMEM((NCHIP, rho, wo), F32), pltpu.VMEM((NCHIP, rho, wo), F32),
                        pltpu.SemaphoreType.DMA((2,)), pltpu.SemaphoreType.DMA((2,)), pltpu.SemaphoreType.DMA((2,))],
        compiler_params=pltpu.CompilerParams(vmem_limit_bytes=48 * 1024 * 1024),
    )(g_in, g_out)


def final_reduce(keep_i, got_i, keep_o, got_o, slab):
    def kern(ki_ref, gi_ref, ko_ref, go_ref, s_ref, gin_ref, gout_ref, all_ref, tot_ref, send_sems, recv_sems):
        x, y, c = _me()
        sib = _flip(1)
        dev = lambda d: 4 * d[0] + 2 * d[1] + d[2]
        me = dev((x, y, c))

        def slab_copy(idx, owner, to):
            return _rcopy(all_ref.at[dev(owner)], all_ref.at[dev(owner)], send_sems, recv_sems, idx, to)

        all_ref[me] = s_ref[...]
        first = [slab_copy(0, (x, y, c), sib)] + [slab_copy(q // 2, (x, y, c), _flip(q)) for q in (2, 4, 6)]
        for cp in first:
            cp.start()

        shares = []
        for n, (keep, got, out) in enumerate(((ki_ref, gi_ref, gin_ref), (ko_ref, go_ref, gout_ref))):
            rh = RS_SHAPES[n][0]
            half = lambda hh, rh=rh: pl.ds(pl.multiple_of(hh * rh, rh), rh)
            out[half(c), :] = ((keep[...] + got[0].astype(F32)) + got[1].astype(F32)) + got[2].astype(F32)
            share = _rcopy(out.at[half(c), :], out.at[half(c), :], send_sems, recv_sems, 7 + n, sib)
            share.start()
            shares.append((share, _rcopy(out.at[half(1 - c), :], out.at[half(1 - c), :], send_sems, recv_sems, 7 + n,
                                         sib)))

        passed = []
        for q in (2, 4, 6):
            slab_copy(q // 2, _flip(q), (x, y, c)).wait_recv()
            cp = slab_copy(3 + q // 2, _flip(q), sib)
            cp.start()
            passed.append(cp)
        slab_copy(0, sib, (x, y, c)).wait_recv()
        for q in (2, 4, 6):
            slab_copy(3 + q // 2, _flip(q | 1), (x, y, c)).wait_recv()
        tot = all_ref[0]
        for d in range(1, NDEV):
            tot = tot + all_ref[d]
        tot_ref[...] = tot
        for share, arrival in shares:
            arrival.wait_recv()
            share.wait_send()
        for cp in first + passed:
            cp.wait_send()

    (rhi, wi), (rho, wo) = RS_SHAPES
    return pl.pallas_call(
        kern, name="final_reduce", in_specs=[_VMEM_SPEC] * 5, out_specs=[_VMEM_SPEC] * 4,
        out_shape=[jax.ShapeDtypeStruct((2 * rhi, wi), F32), jax.ShapeDtypeStruct((2 * rho, wo), F32),
                   jax.ShapeDtypeStruct((NDEV, SLAB_ROWS, DM), F32), jax.ShapeDtypeStruct((SLAB_ROWS, DM), F32)],
        scratch_shapes=[pltpu.SemaphoreType.DMA((9,)), pltpu.SemaphoreType.DMA((9,))],
        compiler_params=pltpu.CompilerParams(vmem_limit_bytes=40 * 1024 * 1024),
    )(keep_i, got_i, keep_o, got_o, slab)


def ada_bwd(a_in, dm, dm_shard, w_ada, c_ctx):
    def kern(a_ref, dm_ref, dms_ref, w_ref, cc_ref, dw_ref, db_ref, dcc_ref, parts, send_sems, recv_sems):
        x, y, c = _me()
        k = 2 * x + y
        act = jax.nn.silu(a_ref[...]).astype(BF16)
        dms = dms_ref[...].astype(BF16)
        dw_ref[...] = lax.dot_general(act, dms, (((0,), (0,)), ((), ())), preferred_element_type=F32)
        db_ref[...] = jnp.sum(dm_ref[...], axis=0, keepdims=True)
        parts[k] = lax.dot_general(dms, w_ref[...].astype(BF16), (((1,), (1,)), ((), ())), preferred_element_type=F32)
        sends = [_rcopy(parts.at[k], parts.at[k], send_sems, recv_sems, q // 2 - 1, _flip(q)) for q in (2, 4, 6)]
        for cp in sends:
            cp.start()
        for q in (2, 4, 6):
            kq = _chip_of(_flip(q))
            _rcopy(parts.at[kq], parts.at[kq], send_sems, recv_sems, q // 2 - 1, _flip(q)).wait_recv()
        dact = ((parts[0] + parts[1]) + parts[2]) + parts[3]
        _, vjp = jax.vjp(jax.nn.silu, cc_ref[...])
        dcc_ref[...] = vjp(dact[8:9, :])[0]
        for cp in sends:
            cp.wait_send()

    return pl.pallas_call(
        kern, name="ada_bwd", in_specs=[_VMEM_SPEC] * 5, out_specs=[_VMEM_SPEC] * 3,
        out_shape=[jax.ShapeDtypeStruct((DM, SHARD_ADA), F32), jax.ShapeDtypeStruct((1, 3 * DM), F32),
                   jax.ShapeDtypeStruct((1, DM), F32)],
        scratch_shapes=[pltpu.VMEM((NCHIP, 16, DM), F32), pltpu.SemaphoreType.DMA((3,)), pltpu.SemaphoreType.DMA((3,))],
    )(a_in, dm, dm_shard, w_ada, c_ctx)


def _adamw_math(w, g, m, v):
    m = B1 * m + (1.0 - B1) * g
    v = B2 * v + (1.0 - B2) * (g * g)
    m_hat = m / (1.0 - B1 ** STEP)
    v_hat = v / (1.0 - B2 ** STEP)
    return -LR * (m_hat / (jnp.sqrt(v_hat) + ADAM_EPS) + WD * w), m, v


def adamw_big(w, g, m, v, name, block_rows=256):
    rows, width = w.shape

    def kern(w_ref, g_ref, m_ref, v_ref, d_ref, nm_ref, nv_ref):
        d_ref[...], nm_ref[...], nv_ref[...] = _adamw_math(w_ref[...], g_ref[...], m_ref[...], v_ref[...])

    spec = pl.BlockSpec((block_rows, width), lambda i: (i, 0))
    return pl.pallas_call(
        kern, name=name, grid=(rows // block_rows,), in_specs=[spec] * 4, out_specs=[spec] * 3,
        out_shape=[jax.ShapeDtypeStruct((rows, width), F32)] * 3,
        compiler_params=_cparams(("arbitrary",)),
    )(w, g, m, v)


def adamw_small(quads):
    n = len(quads)

    def kern(*refs):
        ins, outs = refs[:4 * n], refs[4 * n:]
        for i in range(n):
            w, g, m, v = (r[...] for r in ins[4 * i:4 * i + 4])
            outs[3 * i][...], outs[3 * i + 1][...], outs[3 * i + 2][...] = _adamw_math(w, g, m, v)

    flat = [a for quad in quads for a in quad]
    res = pl.pallas_call(
        kern, name="adamw_small", in_specs=[_VMEM_SPEC] * (4 * n), out_specs=[_VMEM_SPEC] * (3 * n),
        out_shape=[jax.ShapeDtypeStruct(q[0].shape, F32) for q in quads for _ in range(3)],
    )(*flat)
    return [tuple(res[3 * i:3 * i + 3]) for i in range(n)]


def _rows_of(a, rows):
    flat = a.reshape(-1)
    return jnp.pad(flat, (0, rows * DM - flat.shape[0])).reshape(rows, DM)


def kernel(x, c, ctx, c_ctx, w_ada, b_ada, norm_g, w_in, sgu_norm_g, w_spatial, b_spatial, q_norm_g, k_norm_g, rpb, w_out, loss_target, m_c_ctx, m_w_ada, m_b_ada, m_norm_g, m_w_in, m_sgu_norm_g, m_w_spatial, m_b_spatial, m_q_norm_g, m_k_norm_g, m_rpb, m_w_out, v_c_ctx, v_w_ada, v_b_ada, v_norm_g, v_w_in, v_sgu_norm_g, v_w_spatial, v_b_spatial, v_q_norm_g, v_k_norm_g, v_rpb, v_w_out):
    xi, yi, ci = lax.axis_index("x"), lax.axis_index("y"), lax.axis_index("c")
    chip, dev = 2 * xi + yi, 4 * xi + 2 * yi + ci
    c_ctx2 = c_ctx.reshape(1, DM)

    b_shard = lax.dynamic_slice(b_ada, (0, chip * SHARD_ADA), (1, SHARD_ADA))
    part = local_step(chip.reshape(1).astype(jnp.int32), dev, x[0], c, c_ctx2, w_ada[0], b_shard, ctx[0], loss_target[0],
                      norm_g, sgu_norm_g, w_spatial[0], b_spatial[0], q_norm_g, k_norm_g, rpb[0], w_in[0], w_out[0])
    cs = part["cs"]

    slab = jnp.concatenate([
        part["d_norm_g"], _rows_of(part["d_sgu_g"], 1), _rows_of(part["d_b_s"], 1),
        _rows_of(jnp.concatenate([part["d_q_g"], part["d_k_g"]], axis=-1), 1), _rows_of(part["d_rpb"], 4),
        _rows_of(part["loss"], 1), _rows_of(part["dcmod"], 3), _rows_of(part["dmod"], 3), jnp.zeros((1, DM), F32),
        _rows_of(part["d_w_s"], 64)], axis=0)
    g_w_in, g_w_out, gathered, tot = final_reduce(*part["rs"], slab)
    dm = jnp.concatenate([gathered[:, 12:15, :].reshape(NDEV, 3 * DM), tot[9:12].reshape(1, 3 * DM),
                          jnp.zeros((7, 3 * DM), F32)], axis=0)
    a_in = jnp.concatenate([cs[0:8 * NDEV:8], cs[8 * NDEV:8 * NDEV + 1], jnp.zeros((7, DM), F32)], axis=0)
    dm_shard = lax.dynamic_slice(dm, (0, chip * SHARD_ADA), (16, SHARD_ADA))
    g_w_ada, g_b_ada, g_c_ctx = ada_bwd(a_in, dm, dm_shard, w_ada[0], c_ctx2)

    loss = tot[8, 0]
    g_small = dict(
        c_ctx=g_c_ctx, b_ada=g_b_ada, norm_g=tot[0:1], sgu_norm_g=tot[1:2, :512], w_spatial=tot[16:80].reshape(512, 128),
        b_spatial=tot[2:3, :512].reshape(4, 128), q_norm_g=tot[3:4, :HDIM], k_norm_g=tot[3:4, HDIM:2 * HDIM],
        rpb=tot[4:8].reshape(-1)[:HEADS * 15 * 31].reshape(HEADS * 15, 31))
    shapes = dict(c_ctx=(DM,), w_ada=(1, DM, SHARD_ADA), b_ada=(1, 3 * DM), norm_g=(1, DM), w_in=(1, DM, SHARD_IN),
                  sgu_norm_g=(1, 512), w_spatial=(1, 4, 128, 128), b_spatial=(1, 4, 128), q_norm_g=(1, HDIM),
                  k_norm_g=(1, HDIM), rpb=(1, HEADS, 15, 31), w_out=(1, SHARD_OUT, DM))
    names = list(shapes)
    weights = dict(c_ctx=c_ctx, w_ada=w_ada, b_ada=b_ada, norm_g=norm_g, w_in=w_in, sgu_norm_g=sgu_norm_g,
                   w_spatial=w_spatial, b_spatial=b_spatial, q_norm_g=q_norm_g, k_norm_g=k_norm_g, rpb=rpb, w_out=w_out)
    m_in = dict(zip(names, (m_c_ctx, m_w_ada, m_b_ada, m_norm_g, m_w_in, m_sgu_norm_g, m_w_spatial, m_b_spatial,
                            m_q_norm_g, m_k_norm_g, m_rpb, m_w_out)))
    v_in = dict(zip(names, (v_c_ctx, v_w_ada, v_b_ada, v_norm_g, v_w_in, v_sgu_norm_g, v_w_spatial, v_b_spatial,
                            v_q_norm_g, v_k_norm_g, v_rpb, v_w_out)))
    grads = dict(g_small, w_ada=g_w_ada, w_in=g_w_in, w_out=g_w_out)
    upd = {}
    for n in ("w_ada", "w_in", "w_out"):
        g = grads[n]
        upd[n] = adamw_big(weights[n].reshape(g.shape), g, m_in[n].reshape(g.shape), v_in[n].reshape(g.shape),
                           "adamw_" + n)
    small = [n for n in names if n not in upd]
    res = adamw_small([(weights[n].reshape(grads[n].shape), grads[n], m_in[n].reshape(grads[n].shape),
                        v_in[n].reshape(grads[n].shape)) for n in small])
    upd.update(zip(small, res))
    out = [loss, part["grad_x"].reshape(1, SEQ, DM)]
    out += [grads[n].reshape(shapes[n]) for n in names]
    for slot in range(3):
        out += [upd[n][slot].reshape(shapes[n]) for n in names]
    return tuple(out)
```

```python
import jax
import jax.numpy as jnp
from jax import lax
from jax.experimental import pallas as pl
from jax.experimental.pallas import tpu as pltpu

F32, BF16 = jnp.float32, jnp.bfloat16
SEQ, DM, CTX, DIN = 4096, 1024, 256, 3584
NCHIP, NDEV = 4, 8
SHARD_IN = DIN // NCHIP
SHARD_ADA = 3 * DM // NCHIP
SHARD_OUT = DM // NCHIP
GRID_W = 64
QROWS = 4
KROWS = 12
QBLK, KBLK = QROWS * GRID_W, KROWS * GRID_W
NQBLK = SEQ // QBLK
HEADS, HDIM, NPAIR = 8, 64, 4
EPS = 1e-6
NEG_INF = -1e30
ZQ, ZK, ZV, ZG = 12, 16, 20, 24
LR, B1, B2, ADAM_EPS, WD, STEP = 0.001, 0.9, 0.999, 1e-08, 0.01, 10
VMEM_BIG = 56 * 1024 * 1024
MESH_ID = pl.DeviceIdType.MESH


def _dot(a, b, lhs_c, rhs_c):
    return lax.dot_general(a.astype(BF16), b.astype(BF16), (((lhs_c,), (rhs_c,)), ((), ())),
                           preferred_element_type=F32)


@jax.custom_vjp
def mm(a, b):
    return _dot(a, b, 1, 0)


@jax.custom_vjp
def mm_nt(a, b):
    return _dot(a, b, 1, 1)


@jax.custom_vjp
def mm_tn(a, b):
    return _dot(a, b, 0, 0)


mm.defvjp(lambda a, b: (mm(a, b), (a, b)), lambda r, ct: (mm_nt(ct, r[1]), mm_tn(r[0], ct)))
mm_nt.defvjp(lambda a, b: (mm_nt(a, b), (a, b)), lambda r, ct: (mm(ct, r[1]), mm_tn(ct, r[0])))
mm_tn.defvjp(lambda a, b: (mm_tn(a, b), (a, b)), lambda r, ct: (mm_nt(r[1], ct), mm(r[0], ct)))


def _rms(x, g):
    return x * lax.rsqrt(jnp.mean(x * x, axis=-1, keepdims=True) + EPS) * g


def _modulated(x, g, scale, shift):
    return _rms(x, g) * (1.0 + scale) + shift


def _pair_rms(x, g2):
    lo = lax.broadcasted_iota(jnp.int32, (1, 2 * HDIM), 1) < HDIM
    sq = x * x
    s_lo = jnp.sum(jnp.where(lo, sq, 0.0), axis=-1, keepdims=True)
    s_hi = jnp.sum(jnp.where(lo, 0.0, sq), axis=-1, keepdims=True)
    rs = jnp.where(lo, lax.rsqrt(s_lo / HDIM + EPS), lax.rsqrt(s_hi / HDIM + EPS))
    return x * rs * g2


def _cparams(sem, vmem=None):
    return pltpu.CompilerParams(dimension_semantics=sem, vmem_limit_bytes=vmem)


def _row(n):
    return pl.BlockSpec((1, n), lambda *_: (0, 0))


CS_ROWS = 8 * NDEV + 8


def _mod_part(mod_ref, row, part):
    pieces = []
    for j in range(NCHIP):
        lo, hi = max(part * DM, j * SHARD_ADA), min((part + 1) * DM, (j + 1) * SHARD_ADA)
        if lo < hi:
            pieces.append(mod_ref[j, row, lo - j * SHARD_ADA:hi - j * SHARD_ADA])
    return jnp.concatenate(pieces, axis=-1)


def inproj_fwd(chip, x, c_vec, c_ctx, w_ada, b_shard, norm_g, w_shard, wo_shard):
    tl = 1024
    nt = SEQ // tl
    halves = (DM // 2, SHARD_OUT // 2)
    n_w, n_c = 12, NDEV - 1

    def kern(k_ref, x_ref, cv_ref, cc_ref, wa_ref, b_ref, g_ref, w_ref, wo_ref,
             z_ref, h_ref, wfull_ref, wofull_ref, modall_ref, csall_ref,
             w_scr, wo_scr, h_scr, mine, cs_scr, mod_scr, shsc_scr, send_sems, recv_sems):
        s, t = pl.program_id(0), pl.program_id(1)
        xi, yi, c = _me()
        k, me = 2 * xi + yi, 4 * xi + 2 * yi + c
        sib = _flip(1)
        rows = pl.ds(pl.multiple_of(t * tl, tl), tl)
        gathered = (w_scr, wo_scr)
        slot = lambda d: pl.ds(pl.multiple_of(8 * d, 8), 8)

        def c_copy(q, owner):
            return _rcopy(mine, cs_scr.at[slot(owner), :], send_sems, recv_sems, n_w + q - 1, _flip(q))

        def m_copy(q, chip_of_block):
            return _rcopy(mod_scr.at[chip_of_block], mod_scr.at[chip_of_block], send_sems, recv_sems,
                          n_w + n_c + q // 2 - 1, _flip(q))

        def adaln():
            first = lax.broadcasted_iota(jnp.int32, (8, DM), 0) == 0
            mine[...] = jnp.where(first, jnp.broadcast_to(cv_ref[...], (8, DM)), 0.0)
            cs_scr[slot(me), :] = mine[...]
            cs_scr[slot(NDEV), :] = jnp.where(first, jnp.broadcast_to(cc_ref[...], (8, DM)), 0.0)
            for q in range(1, NDEV):
                c_copy(q, me).start()
            wa = wa_ref[...].astype(BF16)
            for q in range(1, NDEV):
                px, py, pc = _flip(q)
                c_copy(q, 4 * px + 2 * py + pc).wait_recv()
            act = jax.nn.silu(cs_scr[...]).astype(BF16)
            mod_scr[k] = jnp.dot(act, wa, preferred_element_type=F32) + b_ref[...]
            for q in (2, 4, 6):
                m_copy(q, k).start()
            for q in (2, 4, 6):
                m_copy(q, _chip_of(_flip(q))).wait_recv()
            row = pl.ds(8 * me, 1)
            shsc_scr[0:1, :] = _mod_part(mod_scr, row, 0)
            shsc_scr[1:2, :] = _mod_part(mod_scr, row, 1)
            pltpu.sync_copy(mod_scr, modall_ref)
            pltpu.sync_copy(cs_scr, csall_ref)

        def block(n, chip_of_block, hh):
            return gathered[n].at[chip_of_block, pl.ds(pl.multiple_of(hh * halves[n], halves[n]), halves[n]), :]

        def ici(n, q, chip_of_block):
            blk = block(n, chip_of_block, c)
            return _rcopy(blk, blk, send_sems, recv_sems, 6 * n + q // 2 - 1, _flip(q))

        def d2d(n, q, chip_of_block, hh):
            blk = block(n, chip_of_block, hh)
            return _rcopy(blk, blk, send_sems, recv_sems, 6 * n + 3 + q // 2 - 1, sib)

        @pl.when((s == 0) & (t == 0))
        def _():
            adaln()
            w_scr[k] = w_ref[...].astype(BF16)
            wo_scr[k] = wo_ref[...].astype(BF16)
            for q in (2, 4, 6):
                ici(0, q, k).start()
                ici(1, q, k).start()

        for sweep in (1, 2, 3):
            @pl.when((s == sweep) & (t == 0))
            def _():
                q = 2 * sweep
                src = _chip_of(_flip(q))
                for n in (0, 1):
                    ici(n, q, src).wait_recv()
                    d2d(n, q, src, c).start()
                for n in (0, 1):
                    d2d(n, q, src, 1 - c).wait_recv()

        @pl.when(s == 0)
        def _():
            hb = _modulated(x_ref[...], g_ref[...], shsc_scr[1:2, :], shsc_scr[0:1, :]).astype(BF16)
            h_scr[rows, :] = hb
            h_ref[...] = hb

        z_ref[...] = jnp.dot(h_scr[rows, :], w_scr[lax.bitwise_xor(k, s)], preferred_element_type=F32)

        @pl.when((s == NCHIP - 1) & (t == nt - 1))
        def _():
            for q in range(1, NDEV):
                c_copy(q, me).wait_send()
            for q in (2, 4, 6):
                m_copy(q, k).wait_send()
            for n in (0, 1):
                for q in (2, 4, 6):
                    ici(n, q, k).wait_send()
                    d2d(n, q, _chip_of(_flip(q)), c).wait_send()
            pltpu.sync_copy(w_scr, wfull_ref)
            pltpu.sync_copy(wo_scr, wofull_ref)

    once = lambda s, t, k: (jnp.where(s == 0, t, nt - 1), 0)
    hbm = pl.BlockSpec(memory_space=pl.ANY)
    n_sem = n_w + n_c + 3
    return pl.pallas_call(
        kern, name="inproj_fwd",
        grid_spec=pltpu.PrefetchScalarGridSpec(
            num_scalar_prefetch=1, grid=(NCHIP, nt),
            in_specs=[pl.BlockSpec((tl, DM), once)] + [_VMEM_SPEC] * 7,
            out_specs=[pl.BlockSpec((tl, SHARD_IN), lambda s, t, k: (t, lax.bitwise_xor(k[0], s))),
                       pl.BlockSpec((tl, DM), once), hbm, hbm, hbm, hbm],
            scratch_shapes=[pltpu.VMEM((NCHIP, DM, SHARD_IN), BF16), pltpu.VMEM((NCHIP, SHARD_OUT, DM), BF16),
                            pltpu.VMEM((SEQ, DM), BF16), pltpu.VMEM((8, DM), F32), pltpu.VMEM((CS_ROWS, DM), F32),
                            pltpu.VMEM((NCHIP, CS_ROWS, SHARD_ADA), F32), pltpu.VMEM((8, DM), F32),
                            pltpu.SemaphoreType.DMA((n_sem,)), pltpu.SemaphoreType.DMA((n_sem,))]),
        out_shape=[jax.ShapeDtypeStruct((SEQ, DIN), F32), jax.ShapeDtypeStruct((SEQ, DM), BF16),
                   jax.ShapeDtypeStruct((NCHIP, DM, SHARD_IN), BF16), jax.ShapeDtypeStruct((NCHIP, SHARD_OUT, DM), BF16),
                   jax.ShapeDtypeStruct((NCHIP, CS_ROWS, SHARD_ADA), F32), jax.ShapeDtypeStruct((CS_ROWS, DM), F32)],
        compiler_params=_cparams(("arbitrary", "arbitrary"), VMEM_BIG),
    )(chip, x, c_vec, c_ctx, w_ada, b_shard, norm_g, w_shard, wo_shard)


def ctx_fwd(ctx, cshift, cscale, norm_g, w_full):
    def kern(c_ref, sh_ref, sc_ref, g_ref, w2_ref, w3_ref, zc_ref, hc_ref):
        hc = _modulated(c_ref[...], g_ref[...], sc_ref[...], sh_ref[...]).astype(BF16)
        hc_ref[...] = hc
        zc_ref[:, :SHARD_IN] = jnp.dot(hc, w2_ref[0], preferred_element_type=F32)
        zc_ref[:, SHARD_IN:] = jnp.dot(hc, w3_ref[0], preferred_element_type=F32)

    return pl.pallas_call(
        kern, name="ctx_fwd", grid=(1,),
        in_specs=[pl.BlockSpec((CTX, DM), lambda i: (0, 0)), _row(DM), _row(DM), _row(DM),
                  pl.BlockSpec((1, DM, SHARD_IN), lambda i: (2, 0, 0)),
                  pl.BlockSpec((1, DM, SHARD_IN), lambda i: (3, 0, 0))],
        out_specs=[pl.BlockSpec((CTX, 2 * SHARD_IN), lambda i: (0, 0)),
                   pl.BlockSpec((CTX, DM), lambda i: (0, 0))],
        out_shape=[jax.ShapeDtypeStruct((CTX, 2 * SHARD_IN), F32), jax.ShapeDtypeStruct((CTX, DM), BF16)],
        compiler_params=_cparams(("arbitrary",)),
    )(ctx, cshift, cscale, norm_g, w_full, w_full)


SGU_CHUNK, SGU_PER_STEP = 128, 4


def _gelu(x):
    return 0.5 * x * (1.0 + lax.erf(x * 0.7071067811865476))


def _sgu_chunk(au, av, ag, sg, ws, bsb):
    u, v = _gelu(au), _gelu(av)
    outs = []
    for g in range(4):
        sl = slice(128 * g, 128 * (g + 1))
        mixed = mm(ws[g], _rms(v[:, sl], sg[:, sl])) + bsb[g]
        outs.append(u[:, sl] * mixed * jax.nn.silu(ag[:, sl]))
    return jnp.concatenate(outs, axis=-1)


def _sgu_specs():
    rows = SGU_CHUNK * SGU_PER_STEP
    zspec = lambda c: pl.BlockSpec((rows, 512), lambda n: (n, c))
    wspec = pl.BlockSpec((4, 128, 128), lambda n: (0, 0, 0))
    return rows, [zspec(0), zspec(1), zspec(2), _row(512), wspec, wspec]


def sgu_fwd(z, sg, ws, bsb):
    rows, in_specs = _sgu_specs()

    def kern(au_ref, av_ref, ag_ref, sg_ref, ws_ref, bs_ref, o_ref):
        for c in range(SGU_PER_STEP):
            sl = slice(c * SGU_CHUNK, (c + 1) * SGU_CHUNK)
            o_ref[sl, :] = _sgu_chunk(au_ref[sl, :], av_ref[sl, :], ag_ref[sl, :], sg_ref[...], ws_ref[...],
                                      bs_ref[...])

    return pl.pallas_call(
        kern, name="sgu_fwd", grid=(SEQ // rows,), in_specs=in_specs,
        out_specs=pl.BlockSpec((rows, 512), lambda n: (n, 0)),
        out_shape=jax.ShapeDtypeStruct((SEQ, 512), F32),
        compiler_params=_cparams(("arbitrary",)),
    )(z, z, z, sg, ws, bsb)


def sgu_bwd(z, sg, ws, bsb, dcat):
    rows, in_specs = _sgu_specs()

    def kern(au_ref, av_ref, ag_ref, sg_ref, ws_ref, bs_ref, do_ref, dz_ref, dsg_ref, dws_ref, dbs_ref):
        @pl.when(pl.program_id(0) == 0)
        def _():
            dsg_ref[...] = jnp.zeros_like(dsg_ref)
            dws_ref[...] = jnp.zeros_like(dws_ref)
            dbs_ref[...] = jnp.zeros_like(dbs_ref)

        for c in range(SGU_PER_STEP):
            sl = slice(c * SGU_CHUNK, (c + 1) * SGU_CHUNK)
            _, vjp = jax.vjp(_sgu_chunk, au_ref[sl, :], av_ref[sl, :], ag_ref[sl, :], sg_ref[...], ws_ref[...],
                             bs_ref[...])
            dau, dav, dag, dsg, dws, dbs = vjp(do_ref[sl, :])
            dz_ref[sl, 0:512] = dau.astype(BF16)
            dz_ref[sl, 512:1024] = dav.astype(BF16)
            dz_ref[sl, 1024:1536] = dag.astype(BF16)
            dsg_ref[...] += dsg
            dws_ref[...] += dws
            dbs_ref[...] += dbs

        @pl.when(pl.program_id(0) == pl.num_programs(0) - 1)
        def _():
            dbs_ref[...] = jnp.broadcast_to(jnp.sum(dbs_ref[...], axis=-1, keepdims=True), dbs_ref.shape)

    wspec = pl.BlockSpec((4, 128, 128), lambda n: (0, 0, 0))
    return pl.pallas_call(
        kern, name="sgu_bwd", grid=(SEQ // rows,),
        in_specs=in_specs + [pl.BlockSpec((rows, 512), lambda n: (n, 0))],
        out_specs=[pl.BlockSpec((rows, 1536), lambda n: (n, 0)), _row(512), wspec, wspec],
        out_shape=[jax.ShapeDtypeStruct((SEQ, 1536), BF16), jax.ShapeDtypeStruct((1, 512), F32),
                   jax.ShapeDtypeStruct((4, 128, 128), F32), jax.ShapeDtypeStruct((4, 128, 128), F32)],
        compiler_params=_cparams(("arbitrary",)),
    )(z, z, z, sg, ws, bsb, dcat)


_DR_OFF = (7, 3, -1)


def _row_valid(v, rr, j):
    return (j < 8, rr <= j < rr + 8, 4 <= j < 12)[v]


def _col_window():
    q = lax.broadcasted_iota(jnp.int32, (GRID_W, 128), 0)
    kc = lax.broadcasted_iota(jnp.int32, (GRID_W, 128), 1) % GRID_W
    c0 = jnp.clip(q - 8, 0, GRID_W - 16)
    return (kc >= c0) & (kc < c0 + 16)


def rpb_tables(rpb2):
    def kern(r_ref, b_ref):
        base = r_ref[0]
        lo = lax.broadcasted_iota(jnp.int32, (1, 128), 1) < GRID_W
        win = _col_window()
        tiles = {}
        for v in range(3):
            for rr in range(QROWS):
                for jp in range(KROWS // 2):
                    j0, j1 = 2 * jp, 2 * jp + 1
                    ok0, ok1 = _row_valid(v, rr, j0), _row_valid(v, rr, j1)
                    key = (j0 - rr + _DR_OFF[v], ok0, ok1) if (ok0 or ok1) else None
                    if key not in tiles:
                        if key is None:
                            tiles[key] = jnp.full((GRID_W, 128), NEG_INF, F32)
                        else:
                            d0 = key[0]
                            r0 = base[d0:d0 + 1, :] if ok0 else jnp.zeros((1, 128), F32)
                            r1 = base[d0 + 1:d0 + 2, :] if ok1 else jnp.zeros((1, 128), F32)
                            y = jnp.broadcast_to(jnp.where(lo, r0, r1), (GRID_W, 128))
                            y = pltpu.roll(pltpu.roll(y, 128 - 15, 1), 0, 1, stride=1, stride_axis=0)
                            tiles[key] = jnp.where(win & jnp.where(lo, ok0, ok1), y, NEG_INF)
                    b_ref[v, 0, rr * GRID_W:(rr + 1) * GRID_W, jp * 128:(jp + 1) * 128] = tiles[key]

    return pl.pallas_call(
        kern, name="rpb_tables", grid=(HEADS,),
        in_specs=[pl.BlockSpec((1, 15, 128), lambda h: (h, 0, 0))],
        out_specs=pl.BlockSpec((3, 1, QBLK, KBLK), lambda h: (0, h, 0, 0)),
        out_shape=jax.ShapeDtypeStruct((3, HEADS, QBLK, KBLK), F32),
        compiler_params=_cparams(("arbitrary",)),
    )(rpb2)


def rpb_bwd(dbias):
    def kern(g0_ref, g1_ref, g2_ref, o_ref):
        g_refs = (g0_ref, g1_ref, g2_ref)
        lo = lax.broadcasted_iota(jnp.int32, (1, 128), 1) < GRID_W
        ri = lax.broadcasted_iota(jnp.int32, (GRID_W, GRID_W), 0)
        ci = lax.broadcasted_iota(jnp.int32, (GRID_W, GRID_W), 1)
        flip = (ri + ci == GRID_W - 1).astype(F32)
        groups = {}
        for v in range(3):
            for rr in range(QROWS):
                for jp in range(KROWS // 2):
                    j0, j1 = 2 * jp, 2 * jp + 1
                    ok0, ok1 = _row_valid(v, rr, j0), _row_valid(v, rr, j1)
                    if not (ok0 or ok1):
                        continue
                    g = g_refs[v][0, rr * GRID_W:(rr + 1) * GRID_W, jp * 128:(jp + 1) * 128]
                    key = (j0 - rr + _DR_OFF[v], ok0, ok1)
                    groups[key] = g if key not in groups else groups[key] + g
        acc = [jnp.zeros((1, 128), F32) for _ in range(15)]
        for (d0, ok0, ok1), g in groups.items():
            g = lax.dot_general(flip, g, (((1,), (0,)), ((), ())), precision=lax.Precision.HIGHEST,
                                preferred_element_type=F32)
            g = pltpu.roll(pltpu.roll(g, 128 - 48, 1), 0, 1, stride=1, stride_axis=0)
            s = jnp.sum(g, axis=0, keepdims=True)
            if ok0:
                acc[d0] = acc[d0] + jnp.where(lo, s, 0.0)
            if ok1:
                acc[d0 + 1] = acc[d0 + 1] + jnp.where(lo, 0.0, s)
        for d in range(15):
            o_ref[0, d:d + 1, :] = acc[d] + pltpu.roll(acc[d], GRID_W, 1)

    return pl.pallas_call(
        kern, name="rpb_bwd", grid=(HEADS,),
        in_specs=[pl.BlockSpec((1, QBLK, KBLK), lambda h: (h, 0, 0))] * 3,
        out_specs=pl.BlockSpec((1, 15, 128), lambda h: (h, 0, 0)),
        out_shape=jax.ShapeDtypeStruct((HEADS, 15, 128), F32),
        compiler_params=_cparams(("arbitrary",)),
    )(*dbias)


def _scaled_q(q_raw, qg):
    return _pair_rms(q_raw, qg) * (HDIM ** -0.5)


def _head_lanes():
    lo = lax.broadcasted_iota(jnp.int32, (1, 2 * HDIM), 1) < HDIM
    return lo, jnp.logical_not(lo)


def _attn_step(q_raw, kn, v, ckn, cv, bias2, qg):
    qn = _scaled_q(q_raw, qg)
    out = lse = None
    for a, mine in enumerate(_head_lanes()):
        qa = jnp.where(mine, qn, 0.0)
        s_lat = mm_nt(qa, kn) + bias2[a]
        s_ctx = mm_nt(qa, ckn)
        m = jnp.maximum(jnp.max(s_lat, axis=-1, keepdims=True), jnp.max(s_ctx, axis=-1, keepdims=True))
        p_lat = jnp.exp(s_lat - m)
        p_ctx = jnp.exp(s_ctx - m)
        den = jnp.sum(p_lat, axis=-1, keepdims=True) + jnp.sum(p_ctx, axis=-1, keepdims=True)
        o = jnp.where(mine, (mm(p_lat, v) + mm(p_ctx, cv)) / den, 0.0)
        l = jnp.where(mine, m + jnp.log(den), 0.0)
        out, lse = (o, l) if out is None else (out + o, lse + l)
    return out, lse


def _attn_step_bwd(q_raw, kn, v, ckn, cv, bias2, qg, bg, o, lse, dout):
    sig = jax.nn.sigmoid(bg)
    do = dout * (bg * sig)
    dbg = dout * o * (sig * (1.0 + bg * (1.0 - sig)))
    qn, qn_vjp = jax.vjp(_scaled_q, q_raw, qg)
    row_dot = do * o
    dqn = dkn = dv = dckn = dcv = None
    dbias = []
    for mine in _head_lanes():
        qa = jnp.where(mine, qn, 0.0)
        doa = jnp.where(mine, do, 0.0)
        l = jnp.max(jnp.where(mine, lse, NEG_INF), axis=-1, keepdims=True)
        delta = jnp.sum(jnp.where(mine, row_dot, 0.0), axis=-1, keepdims=True)
        p_lat = jnp.exp(mm_nt(qa, kn) + bias2[len(dbias)] - l)
        p_ctx = jnp.exp(mm_nt(qa, ckn) - l)
        ds_lat = p_lat * (mm_nt(doa, v) - delta)
        ds_ctx = p_ctx * (mm_nt(doa, cv) - delta)
        parts = (jnp.where(mine, mm(ds_lat, kn) + mm(ds_ctx, ckn), 0.0), mm_tn(ds_lat, qa), mm_tn(p_lat, doa),
                 mm_tn(ds_ctx, qa), mm_tn(p_ctx, doa))
        if dqn is None:
            dqn, dkn, dv, dckn, dcv = parts
        else:
            dqn, dkn, dv, dckn, dcv = (acc + new for acc, new in zip((dqn, dkn, dv, dckn, dcv), parts))
        dbias.append(ds_lat)
    dq, dqg = qn_vjp(dqn)
    return dq, dkn, dv, dckn, dcv, dbias, dqg, dbg


def _kstart(i):
    return pl.multiple_of(jnp.clip((i - 1) * QBLK, 0, SEQ - KBLK), QBLK)


ATTN_STEPS = NQBLK // 2
ATTN_ROWS = 2 * QBLK


def _attn_in_specs():
    bias_spec = lambda variant: pl.BlockSpec((1, 2, QBLK, KBLK), lambda p, i: (variant(i), p, 0, 0))
    return [
        pl.BlockSpec((ATTN_ROWS, 128), lambda p, i: (i, ZQ + p)),
        pl.BlockSpec((SEQ, 128), lambda p, i: (0, ZK + p)),
        pl.BlockSpec((SEQ, 128), lambda p, i: (0, ZV + p)),
        pl.BlockSpec((ATTN_ROWS, 128), lambda p, i: (i, ZG + p)),
        pl.BlockSpec((CTX, 128), lambda p, i: (0, 2 + p)),
        pl.BlockSpec((CTX, 128), lambda p, i: (0, 6 + p)),
        bias_spec(lambda i: jnp.where(i == 0, 0, 1)),
        bias_spec(lambda i: jnp.where(i == ATTN_STEPS - 1, 2, 1)),
        _row(128), _row(128),
    ]


NORM_ROWS = 512


def _norm_keys(k_ref, ck_ref, kg_ref, kn_scr, ckn_scr):
    def body(c, carry):
        sl = pl.ds(pl.multiple_of(c * NORM_ROWS, NORM_ROWS), NORM_ROWS)
        kn_scr[sl, :] = _pair_rms(k_ref[sl, :], kg_ref[...])
        return carry

    lax.fori_loop(0, SEQ // NORM_ROWS, body, 0)
    ckn_scr[...] = _pair_rms(ck_ref[...], kg_ref[...])


def attn_fwd(z, zc, bias, qg2, kg2):
    def kern(q_ref, k_ref, v_ref, bg_ref, ck_ref, cv_ref, be_ref, bo_ref, qg_ref, kg_ref, ob_ref, o_ref, lse_ref,
             kn_scr, ckn_scr):
        i = pl.program_id(1)

        @pl.when(i == 0)
        def _():
            _norm_keys(k_ref, ck_ref, kg_ref, kn_scr, ckn_scr)

        for b, b_ref in enumerate((be_ref, bo_ref)):
            rows = slice(b * QBLK, (b + 1) * QBLK)
            ks = pl.ds(_kstart(2 * i + b), KBLK)
            o, lse = _attn_step(q_ref[rows, :], kn_scr[ks, :], v_ref[ks, :], ckn_scr[...], cv_ref[...], b_ref[0],
                                qg_ref[...])
            ob_ref[rows, :] = o * jax.nn.silu(bg_ref[rows, :])
            o_ref[rows, :] = o
            lse_ref[rows, :] = lse

    qblk = pl.BlockSpec((ATTN_ROWS, 128), lambda p, i: (i, p))
    return pl.pallas_call(
        kern, name="attn_fwd", grid=(NPAIR, ATTN_STEPS), in_specs=_attn_in_specs(), out_specs=[qblk] * 3,
        out_shape=[jax.ShapeDtypeStruct((SEQ, 512), F32)] * 3,
        scratch_shapes=[pltpu.VMEM((SEQ, 128), F32), pltpu.VMEM((CTX, 128), F32)],
        compiler_params=_cparams(("arbitrary", "arbitrary"), 40 * 1024 * 1024),
    )(z, z, z, z, zc, zc, bias, bias, qg2, kg2)


def attn_bwd(z, zc, bias, qg2, kg2, dcat, o_raw, lse):
    def kern(q_ref, k_ref, v_ref, bg_ref, ck_ref, cv_ref, be_ref, bo_ref, qg_ref, kg_ref, do_ref, o_ref, lse_ref,
             dq_ref, dk_ref, dv_ref, dbg_ref, dck_ref, dcv_ref, db0_ref, db1_ref, db2_ref, dqg_ref, dkg_ref,
             kn_scr, ckn_scr, dkn_scr, dckn_scr, dv_scr):
        p, i = pl.program_id(0), pl.program_id(1)
        last = i == ATTN_STEPS - 1

        @pl.when(i == 0)
        def _():
            _norm_keys(k_ref, ck_ref, kg_ref, kn_scr, ckn_scr)
            dkn_scr[...] = jnp.zeros_like(dkn_scr)
            dv_scr[...] = jnp.zeros_like(dv_scr)
            dckn_scr[...] = jnp.zeros_like(dckn_scr)
            dcv_ref[...] = jnp.zeros_like(dcv_ref)

        @pl.when((i == 0) & (p == 0))
        def _():
            dqg_ref[...] = jnp.zeros_like(dqg_ref)
            dkg_ref[...] = jnp.zeros_like(dkg_ref)

        db = []
        for b, b_ref in enumerate((be_ref, bo_ref)):
            rows = slice(b * QBLK, (b + 1) * QBLK)
            ks = pl.ds(_kstart(2 * i + b), KBLK)
            dq, dkn, dv, dckn, dcv, dbb, dqg, dbg = _attn_step_bwd(
                q_ref[rows, :], kn_scr[ks, :], v_ref[ks, :], ckn_scr[...], cv_ref[...], b_ref[0], qg_ref[...],
                bg_ref[rows, :], o_ref[rows, :], lse_ref[rows, :], do_ref[rows, :])
            dq_ref[rows, :] = dq.astype(BF16)
            dbg_ref[rows, :] = dbg.astype(BF16)
            dkn_scr[ks, :] += dkn
            dv_scr[ks, :] += dv
            dckn_scr[...] += dckn
            dcv_ref[...] += dcv
            dqg_ref[...] += dqg
            db.append(dbb)

        @pl.when(i == 0)
        def _():
            for a in range(2):
                db0_ref[a] = db[0][a]
                db1_ref[a] = db[1][a]

        @pl.when((i > 0) & jnp.logical_not(last))
        def _():
            for a in range(2):
                db1_ref[a] += db[0][a] + db[1][a]

        @pl.when(last)
        def _():
            for a in range(2):
                db1_ref[a] += db[0][a]
                db2_ref[a] = db[1][a]

        @pl.when(last)
        def _():
            def body(c, dkg):
                sl = pl.ds(pl.multiple_of(c * NORM_ROWS, NORM_ROWS), NORM_ROWS)
                _, nvjp = jax.vjp(_pair_rms, k_ref[sl, :], kg_ref[...])
                dk, dg = nvjp(dkn_scr[sl, :])
                dk_ref[sl, :] = dk.astype(BF16)
                dv_ref[sl, :] = dv_scr[sl, :].astype(BF16)
                return dkg + dg

            dkg = lax.fori_loop(0, SEQ // NORM_ROWS, body, jnp.zeros((1, 128), F32))
            _, nvjp = jax.vjp(_pair_rms, ck_ref[...], kg_ref[...])
            dck, dg = nvjp(dckn_scr[...])
            dck_ref[...] = dck
            dkg_ref[...] += dkg + dg

        @pl.when(last & (p == NPAIR - 1))
        def _():
            dqg_ref[...] = dqg_ref[...] + pltpu.roll(dqg_ref[...], HDIM, 1)
            dkg_ref[...] = dkg_ref[...] + pltpu.roll(dkg_ref[...], HDIM, 1)

    blk = lambda rows: pl.BlockSpec((rows, 128), lambda p, i: (0, p))
    qblk = pl.BlockSpec((ATTN_ROWS, 128), lambda p, i: (i, p))
    dbias = pl.BlockSpec((2, QBLK, KBLK), lambda p, i: (p, 0, 0))
    return pl.pallas_call(
        kern, name="attn_bwd", grid=(NPAIR, ATTN_STEPS),
        in_specs=_attn_in_specs() + [pl.BlockSpec((ATTN_ROWS, 128), lambda p, i: (i, 4 + p)), qblk, qblk],
        out_specs=[qblk, blk(SEQ), blk(SEQ), qblk, blk(CTX), blk(CTX), dbias, dbias, dbias, _row(128), _row(128)],
        out_shape=[jax.ShapeDtypeStruct((SEQ, 512), BF16)] * 4 + [jax.ShapeDtypeStruct((CTX, 512), F32)] * 2
        + [jax.ShapeDtypeStruct((HEADS, QBLK, KBLK), F32)] * 3
        + [jax.ShapeDtypeStruct((1, 128), F32), jax.ShapeDtypeStruct((1, 128), F32)],
        scratch_shapes=[pltpu.VMEM((SEQ, 128), F32), pltpu.VMEM((CTX, 128), F32),
                        pltpu.VMEM((SEQ, 128), F32), pltpu.VMEM((CTX, 128), F32), pltpu.VMEM((SEQ, 128), F32)],
        compiler_params=_cparams(("arbitrary", "arbitrary"), VMEM_BIG),
    )(z, z, z, z, zc, zc, bias, bias, qg2, kg2, dcat, o_raw, lse)


def outproj(out_a, out_b, x, target, gate, wo):
    tl = 512

    def kern(a_ref, b_ref, x_ref, t_ref, g_ref, w_ref, loss_ref, dy_ref, dcat_ref, dg_ref, dw_ref):
        @pl.when(pl.program_id(0) == 0)
        def _():
            loss_ref[...] = jnp.zeros_like(loss_ref)
            dg_ref[...] = jnp.zeros_like(dg_ref)
            dw_ref[...] = jnp.zeros_like(dw_ref)

        a, b = a_ref[...].astype(BF16), b_ref[...].astype(BF16)
        mix = (jnp.dot(a, w_ref[0:512, :], preferred_element_type=F32)
               + jnp.dot(b, w_ref[512:1024, :], preferred_element_type=F32))
        err = x_ref[...] + g_ref[...] * mix - t_ref[...]
        loss_ref[...] += 0.5 * jnp.sum(jnp.mean(err * err, axis=-1))
        dy = err * (1.0 / DM)
        dy_ref[...] = dy
        dg_ref[...] += jnp.sum(dy * mix, axis=0, keepdims=True)
        dmix = (g_ref[...] * dy).astype(BF16)
        dcat_ref[...] = lax.dot_general(dmix, w_ref[...], (((1,), (1,)), ((), ())), preferred_element_type=F32)
        dw_ref[0:512, :] += lax.dot_general(a, dmix, (((0,), (0,)), ((), ())), preferred_element_type=F32)
        dw_ref[512:1024, :] += lax.dot_general(b, dmix, (((0,), (0,)), ((), ())), preferred_element_type=F32)

    tile = lambda w: pl.BlockSpec((tl, w), lambda t: (t, 0))
    whole = pl.BlockSpec((DM, DM), lambda t: (0, 0))
    return pl.pallas_call(
        kern, name="outproj", grid=(SEQ // tl,),
        in_specs=[tile(512), tile(512), tile(DM), tile(DM), _row(DM), whole],
        out_specs=[pl.BlockSpec((8, 128), lambda t: (0, 0)), tile(DM), tile(DM), _row(DM), whole],
        out_shape=[jax.ShapeDtypeStruct((8, 128), F32), jax.ShapeDtypeStruct((SEQ, DM), F32),
                   jax.ShapeDtypeStruct((SEQ, DM), F32), jax.ShapeDtypeStruct((1, DM), F32),
                   jax.ShapeDtypeStruct((DM, DM), F32)],
        compiler_params=_cparams(("arbitrary",), 48 * 1024 * 1024),
    )(out_a, out_b, x, target, gate, wo)


def _pieces(sources):
    out = []
    for name, c0, c1 in sources:
        for j in range(NCHIP):
            lo, hi = max(c0, j * SHARD_IN), min(c1, (j + 1) * SHARD_IN)
            if lo < hi:
                out.append((j, lo - j * SHARD_IN, hi - j * SHARD_IN, name, lo - c0, hi - c0))
    return out


DZ_PIECES = _pieces((("a", 0, 1536), ("q", 1536, 2048), ("k", 2048, 2560), ("v", 2560, 3072), ("g", 3072, DIN)))
DZC_PIECES = _pieces((("k", 2048, 2560), ("v", 2560, 3072)))
_NT = (((1,), (1,)), ((), ()))


DH_SUBTILES = 2


def _dz_specs(tl):
    return [pl.BlockSpec((tl, 1536), lambda t: (t, 0))] + [pl.BlockSpec((tl, 512), lambda t: (t, 0))] * 4


def dh_bwd(dz_parts, w_full, x, dy, shift, scale, norm_g, dg_ctx, wire_i, wire_o):
    tl = 512
    nt = SEQ // tl

    def kern(a_ref, q_ref, k_ref, v_ref, g_ref, w_ref, x_ref, dy_ref, sh_ref, sc_ref, gn_ref, dgc_ref, wi_hbm, wo_hbm,
             gx_ref, dsh_ref, dsc_ref, dg_ref, goti_ref, goto_ref, rcv_i, rcv_o, send_sems, recv_sems):
        def ici(n, q):
            wire, rcv = ((wi_hbm, rcv_i), (wo_hbm, rcv_o))[n]
            return _rcopy(wire.at[_chip_of(_flip(q))], rcv.at[q // 2 - 1], send_sems, recv_sems, 3 * n + q // 2 - 1,
                          _flip(q))

        @pl.when(pl.program_id(0) == 0)
        def _():
            for n in (0, 1):
                for q in (2, 4, 6):
                    ici(n, q).start()

        @pl.when(pl.program_id(0) == 0)
        def _():
            dsh_ref[...] = jnp.zeros_like(dsh_ref)
            dsc_ref[...] = jnp.zeros_like(dsc_ref)
            dg_ref[...] = dgc_ref[...]

        src = dict(a=a_ref, q=q_ref, k=k_ref, v=v_ref, g=g_ref)
        for sub in range(DH_SUBTILES):
            rows = slice(sub * tl // DH_SUBTILES, (sub + 1) * tl // DH_SUBTILES)
            dh = None
            for j, l0, l1, name, s0, s1 in DZ_PIECES:
                part = lax.dot_general(src[name][rows, s0:s1], w_ref[j, :, l0:l1], _NT, preferred_element_type=F32)
                dh = part if dh is None else dh + part
            _, vjp = jax.vjp(_modulated, x_ref[rows, :], gn_ref[...], sc_ref[...], sh_ref[...])
            dx, dg, dsc, dsh = vjp(dh)
            gx_ref[rows, :] = dy_ref[rows, :] + dx
            dg_ref[...] += dg
            dsc_ref[...] += dsc
            dsh_ref[...] += dsh

        @pl.when(pl.program_id(0) == nt - 1)
        def _():
            for n in (0, 1):
                for q in (2, 4, 6):
                    ici(n, q).wait_recv()
                    ici(n, q).wait_send()
            goti_ref[...] = rcv_i[...]
            goto_ref[...] = rcv_o[...]

    tile = pl.BlockSpec((tl, DM), lambda t: (t, 0))
    hbm = pl.BlockSpec(memory_space=pl.ANY)
    got = [(NCHIP - 1, rh, w) for rh, w in RS_SHAPES]
    return pl.pallas_call(
        kern, name="dh_bwd", grid=(nt,),
        in_specs=_dz_specs(tl) + [pl.BlockSpec((NCHIP, DM, SHARD_IN), lambda t: (0, 0, 0)), tile, tile, _row(DM),
                                  _row(DM), _row(DM), _row(DM), hbm, hbm],
        out_specs=[tile, _row(DM), _row(DM), _row(DM)] + [pl.BlockSpec(s, lambda t: (0, 0, 0)) for s in got],
        out_shape=[jax.ShapeDtypeStruct((SEQ, DM), F32)] + [jax.ShapeDtypeStruct((1, DM), F32)] * 3
        + [jax.ShapeDtypeStruct(s, BF16) for s in got],
        scratch_shapes=[pltpu.VMEM(s, BF16) for s in got] + [pltpu.SemaphoreType.DMA((6,)), pltpu.SemaphoreType.DMA((6,))],
        compiler_params=_cparams(("arbitrary",), VMEM_BIG),
    )(*dz_parts, w_full, x, dy, shift, scale, norm_g, dg_ctx, wire_i, wire_o)


def dw_bwd(h, dz_parts, hc, dck, dcv):
    tl = 512

    def kern(h_ref, a_ref, q_ref, k_ref, v_ref, g_ref, hc_ref, dck_ref, dcv_ref, dw_ref):
        @pl.when(pl.program_id(0) == 0)
        def _():
            dw_ref[...] = jnp.zeros_like(dw_ref)
            hct = hc_ref[...].T
            csrc = dict(k=dck_ref, v=dcv_ref)
            for j, l0, l1, name, s0, s1 in DZC_PIECES:
                dw_ref[j, :, l0:l1] += jnp.dot(hct, csrc[name][:, s0:s1].astype(BF16), preferred_element_type=F32)

        ht = h_ref[...].T
        src = dict(a=a_ref, q=q_ref, k=k_ref, v=v_ref, g=g_ref)
        for j, l0, l1, name, s0, s1 in DZ_PIECES:
            dw_ref[j, :, l0:l1] += jnp.dot(ht, src[name][:, s0:s1], preferred_element_type=F32)

    whole = lambda r, c: pl.BlockSpec((r, c), lambda t: (0, 0))
    return pl.pallas_call(
        kern, name="dw_bwd", grid=(SEQ // tl,),
        in_specs=[pl.BlockSpec((tl, DM), lambda t: (t, 0))] + _dz_specs(tl) + [whole(CTX, DM), whole(CTX, 512),
                                                                              whole(CTX, 512)],
        out_specs=pl.BlockSpec((NCHIP, DM, SHARD_IN), lambda t: (0, 0, 0)),
        out_shape=jax.ShapeDtypeStruct((NCHIP, DM, SHARD_IN), F32),
        compiler_params=_cparams(("arbitrary",), VMEM_BIG),
    )(h, *dz_parts, hc, dck, dcv)


def ctx_bwd(dck, dcv, w_full, ctx, cshift, cscale, norm_g):
    def kern(dck_ref, dcv_ref, w_ref, c_ref, sh_ref, sc_ref, g_ref, dsh_ref, dsc_ref, dg_ref):
        csrc = dict(k=dck_ref, v=dcv_ref)
        dhc = None
        for j, l0, l1, name, s0, s1 in DZC_PIECES:
            part = lax.dot_general(csrc[name][:, s0:s1].astype(BF16), w_ref[j, :, l0:l1], _NT,
                                   preferred_element_type=F32)
            dhc = part if dhc is None else dhc + part
        _, vjp = jax.vjp(lambda g, sc, sh: _modulated(c_ref[...], g, sc, sh), g_ref[...], sc_ref[...], sh_ref[...])
        dg_ref[...], dsc_ref[...], dsh_ref[...] = vjp(dhc)

    whole = lambda r, c: pl.BlockSpec((r, c), lambda i: (0, 0))
    return pl.pallas_call(
        kern, name="ctx_bwd", grid=(1,),
        in_specs=[whole(CTX, 512), whole(CTX, 512), pl.BlockSpec((NCHIP, DM, SHARD_IN), lambda i: (0, 0, 0)),
                  whole(CTX, DM), _row(DM), _row(DM), _row(DM)],
        out_specs=[_row(DM), _row(DM), _row(DM)],
        out_shape=[jax.ShapeDtypeStruct((1, DM), F32)] * 3,
        compiler_params=_cparams(("arbitrary",), 40 * 1024 * 1024),
    )(dck, dcv, w_full, ctx, cshift, cscale, norm_g)


def _lane_pad_rpb(rpb):
    r = jnp.pad(rpb, ((0, 0), (0, 0), (0, GRID_W - rpb.shape[-1])))
    return jnp.concatenate([r, r], axis=-1)


def local_step(chip, dev, x, c_vec, c_ctx, w_ada, b_shard, ctx, target, norm_g, sgu_g, w_s, b_s, q_g, k_g, rpb,
               w_in_shard, w_out_shard):
    bsb = jnp.broadcast_to(b_s[:, :, None], (4, 128, 128))
    qg2, kg2 = jnp.tile(q_g, (1, 2)), jnp.tile(k_g, (1, 2))

    z, h, w_in_full, w_out_full, mod_all, cs = inproj_fwd(chip, x, c_vec, c_ctx, w_ada, b_shard, norm_g, w_in_shard,
                                                          w_out_shard)
    mods = mod_all.transpose(1, 0, 2).reshape(CS_ROWS, 3 * DM)
    mod = lax.dynamic_slice(mods, (8 * dev, 0), (1, 3 * DM))
    shift, scale, gate = mod[:, :DM], mod[:, DM:2 * DM], mod[:, 2 * DM:]
    cshift, cscale = mods[8 * NDEV:8 * NDEV + 1, :DM], mods[8 * NDEV:8 * NDEV + 1, DM:2 * DM]
    zc, hc = ctx_fwd(ctx, cshift, cscale, norm_g, w_in_full)
    bias = rpb_tables(_lane_pad_rpb(rpb))
    out_a = sgu_fwd(z, sgu_g, w_s, bsb)
    out_b, o_raw, lse = attn_fwd(z, zc, bias, qg2, kg2)
    loss8, dy, dcat, dgate, dwo = outproj(out_a, out_b, x, target, gate, w_out_full.reshape(DM, DM))
    dz_a, dsg, dws, dbsb = sgu_bwd(z, sgu_g, w_s, bsb, dcat)
    dq, dk, dv, dbg, dck, dcv, db0, db1, db2, dqg2, dkg2 = attn_bwd(z, zc, bias, qg2, kg2, dcat, o_raw, lse)
    drpb = rpb_bwd((db0, db1, db2))[:, :, :rpb.shape[-1]]
    dz_parts = (dz_a, dq, dk, dv, dbg)
    dcshift, dcscale, dng_c = ctx_bwd(dck, dcv, w_in_full, ctx, cshift, cscale, norm_g)
    dw_in = dw_bwd(h, dz_parts, hc, dck, dcv)
    wire_i, keep_i, wire_o, keep_o = pair_sum(dw_in, dwo.reshape(NCHIP, SHARD_OUT, DM))
    grad_x, dshift, dscale, dng, got_i, got_o = dh_bwd(dz_parts, w_in_full, x, dy, shift, scale, norm_g, dng_c,
                                                       wire_i, wire_o)
    return dict(
        loss=loss8[0:1, 0:1], grad_x=grad_x, rs=(keep_i, got_i, keep_o, got_o), cs=cs,
        dmod=jnp.concatenate([dshift, dscale, dgate], axis=-1),
        dcmod=jnp.concatenate([dcshift, dcscale, jnp.zeros((1, DM), F32)], axis=-1),
        d_norm_g=dng, d_sgu_g=dsg, d_w_s=dws, d_b_s=dbsb[:, :, 0],
        d_q_g=dqg2[:, :HDIM], d_k_g=dkg2[:, :HDIM], d_rpb=drpb)


def _me():
    return lax.axis_index("x"), lax.axis_index("y"), lax.axis_index("c")


def _flip(q):
    x, y, c = _me()
    return ((1 - x) if q & 4 else x, (1 - y) if q & 2 else y, (1 - c) if q & 1 else c)


def _chip_of(dev):
    return 2 * dev[0] + dev[1]


def _rcopy(src, dst, send_sems, recv_sems, k, dev):
    return pltpu.make_async_remote_copy(src_ref=src, dst_ref=dst, send_sem=send_sems.at[k], recv_sem=recv_sems.at[k],
                                        device_id=dev, device_id_type=MESH_ID)


_VMEM_SPEC = pl.BlockSpec(memory_space=pltpu.VMEM)
SLAB_ROWS = 80


RS_SHAPES = ((DM // 2, SHARD_IN), (SHARD_OUT // 2, DM))


def pair_sum(g_in, g_out):
    def kern(gi_hbm, go_hbm, wire_i, keep_i, wire_o, keep_o, mine_i, rcv_i, mine_o, rcv_o, load_sems, send_sems,
             recv_sems):
        x, y, c = _me()
        k = 2 * x + y
        sib = _flip(1)
        work = ((gi_hbm, mine_i, rcv_i, wire_i, keep_i), (go_hbm, mine_o, rcv_o, wire_o, keep_o))
        copies = []
        for n, (g, mine, rcv, _, _) in enumerate(work):
            rh = RS_SHAPES[n][0]
            half = lambda hh, rh=rh: pl.ds(pl.multiple_of(hh * rh, rh), rh)
            load = pltpu.make_async_copy(g.at[:, half(c), :], mine, load_sems.at[n])
            load.start()
            pair = _rcopy(g.at[:, half(1 - c), :], rcv, send_sems, recv_sems, n, sib)
            pair.start()
            copies.append((load, pair))
        for (load, pair), (_, mine, rcv, wire, keep) in zip(copies, work):
            load.wait()
            pair.wait_recv()
            for j in range(NCHIP):
                wire[j] = (mine[j] + rcv[j]).astype(BF16)
            keep[...] = mine[k] + rcv[k]
        for _, pair in copies:
            pair.wait_send()

    (rhi, wi), (rho, wo) = RS_SHAPES
    hbm = pl.BlockSpec(memory_space=pl.ANY)
    return pl.pallas_call(
        kern, name="pair_sum", in_specs=[hbm, hbm], out_specs=[_VMEM_SPEC] * 4,
        out_shape=[jax.ShapeDtypeStruct((NCHIP, rhi, wi), BF16), jax.ShapeDtypeStruct((rhi, wi), F32),
                   jax.ShapeDtypeStruct((NCHIP, rho, wo), BF16), jax.ShapeDtypeStruct((rho, wo), F32)],
        scratch_shapes=[pltpu.VMEM((NCHIP, rhi, wi), F32), pltpu.VMEM((NCHIP, rhi, wi), F32),
                        pltpu.VMEM((NCHIP, rho, wo), F32), pltpu.VMEM((NCHIP, rho, wo), F32),
                        pltpu.SemaphoreType.DMA((2,)), pltpu.SemaphoreType.DMA((2,)), pltpu.SemaphoreType.DMA((2,))],
        compiler_params=pltpu.CompilerParams(vmem_limit_bytes=48 * 1024 * 1024),
    )(g_in, g_out)


def final_reduce(keep_i, got_i, keep_o, got_o, slab):
    def kern(ki_ref, gi_ref, ko_ref, go_ref, s_ref, gin_ref, gout_ref, all_ref, tot_ref, send_sems, recv_sems):
        x, y, c = _me()
        sib = _flip(1)
        dev = lambda d: 4 * d[0] + 2 * d[1] + d[2]
        me = dev((x, y, c))

        def slab_copy(idx, owner, to):
            return _rcopy(all_ref.at[dev(owner)], all_ref.at[dev(owner)], send_sems, recv_sems, idx, to)

        all_ref[me] = s_ref[...]
        first = [slab_copy(0, (x, y, c), sib)] + [slab_copy(q // 2, (x, y, c), _flip(q)) for q in (2, 4, 6)]
        for cp in first:
            cp.start()

        shares = []
        for n, (keep, got, out) in enumerate(((ki_ref, gi_ref, gin_ref), (ko_ref, go_ref, gout_ref))):
            rh = RS_SHAPES[n][0]
            half = lambda hh, rh=rh: pl.ds(pl.multiple_of(hh * rh, rh), rh)
            out[half(c), :] = ((keep[...] + got[0].astype(F32)) + got[1].astype(F32)) + got[2].astype(F32)
            share = _rcopy(out.at[half(c), :], out.at[half(c), :], send_sems, recv_sems, 7 + n, sib)
            share.start()
            shares.append((share, _rcopy(out.at[half(1 - c), :], out.at[half(1 - c), :], send_sems, recv_sems, 7 + n,
                                         sib)))

        passed = []
        for q in (2, 4, 6):
            slab_copy(q // 2, _flip(q), (x, y, c)).wait_recv()
            cp = slab_copy(3 + q // 2, _flip(q), sib)
            cp.start()
            passed.append(cp)
        slab_copy(0, sib, (x, y, c)).wait_recv()
        for q in (2, 4, 6):
            slab_copy(3 + q // 2, _flip(q | 1), (x, y, c)).wait_recv()
        tot = all_ref[0]
        for d in range(1, NDEV):
            tot = tot + all_ref[d]
        tot_ref[...] = tot
        for share, arrival in shares:
            arrival.wait_recv()
            share.wait_send()
        for cp in first + passed:
            cp.wait_send()

    (rhi, wi), (rho, wo) = RS_SHAPES
    return pl.pallas_call(
        kern, name="final_reduce", in_specs=[_VMEM_SPEC] * 5, out_specs=[_VMEM_SPEC] * 4,
        out_shape=[jax.ShapeDtypeStruct((2 * rhi, wi), F32), jax.ShapeDtypeStruct((2 * rho, wo), F32),
                   jax.ShapeDtypeStruct((NDEV, SLAB_ROWS, DM), F32), jax.ShapeDtypeStruct((SLAB_ROWS, DM), F32)],
        scratch_shapes=[pltpu.SemaphoreType.DMA((9,)), pltpu.SemaphoreType.DMA((9,))],
        compiler_params=pltpu.CompilerParams(vmem_limit_bytes=40 * 1024 * 1024),
    )(keep_i, got_i, keep_o, got_o, slab)


def ada_bwd(a_in, dm, dm_shard, w_ada, c_ctx):
    def kern(a_ref, dm_ref, dms_ref, w_ref, cc_ref, dw_ref, db_ref, dcc_ref, parts, send_sems, recv_sems):
        x, y, c = _me()
        k = 2 * x + y
        act = jax.nn.silu(a_ref[...]).astype(BF16)
        dms = dms_ref[...].astype(BF16)
        dw_ref[...] = lax.dot_general(act, dms, (((0,), (0,)), ((), ())), preferred_element_type=F32)
        db_ref[...] = jnp.sum(dm_ref[...], axis=0, keepdims=True)
        parts[k] = lax.dot_general(dms, w_ref[...].astype(BF16), (((1,), (1,)), ((), ())), preferred_element_type=F32)
        sends = [_rcopy(parts.at[k], parts.at[k], send_sems, recv_sems, q // 2 - 1, _flip(q)) for q in (2, 4, 6)]
        for cp in sends:
            cp.start()
        for q in (2, 4, 6):
            kq = _chip_of(_flip(q))
            _rcopy(parts.at[kq], parts.at[kq], send_sems, recv_sems, q // 2 - 1, _flip(q)).wait_recv()
        dact = ((parts[0] + parts[1]) + parts[2]) + parts[3]
        _, vjp = jax.vjp(jax.nn.silu, cc_ref[...])
        dcc_ref[...] = vjp(dact[8:9, :])[0]
        for cp in sends:
            cp.wait_send()

    return pl.pallas_call(
        kern, name="ada_bwd", in_specs=[_VMEM_SPEC] * 5, out_specs=[_VMEM_SPEC] * 3,
        out_shape=[jax.ShapeDtypeStruct((DM, SHARD_ADA), F32), jax.ShapeDtypeStruct((1, 3 * DM), F32),
                   jax.ShapeDtypeStruct((1, DM), F32)],
        scratch_shapes=[pltpu.VMEM((NCHIP, 16, DM), F32), pltpu.SemaphoreType.DMA((3,)), pltpu.SemaphoreType.DMA((3,))],
    )(a_in, dm, dm_shard, w_ada, c_ctx)


def _adamw_math(w, g, m, v):
    m = B1 * m + (1.0 - B1) * g
    v = B2 * v + (1.0 - B2) * (g * g)
    m_hat = m / (1.0 - B1 ** STEP)
    v_hat = v / (1.0 - B2 ** STEP)
    return -LR * (m_hat / (jnp.sqrt(v_hat) + ADAM_EPS) + WD * w), m, v


def adamw_big(w, g, m, v, name, block_rows=256):
    rows, width = w.shape

    def kern(w_ref, g_ref, m_ref, v_ref, d_ref, nm_ref, nv_ref):
        d_ref[...], nm_ref[...], nv_ref[...] = _adamw_math(w_ref[...], g_ref[...], m_ref[...], v_ref[...])

    spec = pl.BlockSpec((block_rows, width), lambda i: (i, 0))
    return pl.pallas_call(
        kern, name=name, grid=(rows // block_rows,), in_specs=[spec] * 4, out_specs=[spec] * 3,
        out_shape=[jax.ShapeDtypeStruct((rows, width), F32)] * 3,
        compiler_params=_cparams(("arbitrary",)),
    )(w, g, m, v)


def adamw_small(quads):
    n = len(quads)

    def kern(*refs):
        ins, outs = refs[:4 * n], refs[4 * n:]
        for i in range(n):
            w, g, m, v = (r[...] for r in ins[4 * i:4 * i + 4])
            outs[3 * i][...], outs[3 * i + 1][...], outs[3 * i + 2][...] = _adamw_math(w, g, m, v)

    flat = [a for quad in quads for a in quad]
    res = pl.pallas_call(
        kern, name="adamw_small", in_specs=[_VMEM_SPEC] * (4 * n), out_specs=[_VMEM_SPEC] * (3 * n),
        out_shape=[jax.ShapeDtypeStruct(q[0].shape, F32) for q in quads for _ in range(3)],
    )(*flat)
    return [tuple(res[3 * i:3 * i + 3]) for i in range(n)]


def _rows_of(a, rows):
    flat = a.reshape(-1)
    return jnp.pad(flat, (0, rows * DM - flat.shape[0])).reshape(rows, DM)


def kernel(x, c, ctx, c_ctx, w_ada, b_ada, norm_g, w_in, sgu_norm_g, w_spatial, b_spatial, q_norm_g, k_norm_g, rpb, w_out, loss_target, m_c_ctx, m_w_ada, m_b_ada, m_norm_g, m_w_in, m_sgu_norm_g, m_w_spatial, m_b_spatial, m_q_norm_g, m_k_norm_g, m_rpb, m_w_out, v_c_ctx, v_w_ada, v_b_ada, v_norm_g, v_w_in, v_sgu_norm_g, v_w_spatial, v_b_spatial, v_q_norm_g, v_k_norm_g, v_rpb, v_w_out):
    xi, yi, ci = lax.axis_index("x"), lax.axis_index("y"), lax.axis_index("c")
    chip, dev = 2 * xi + yi, 4 * xi + 2 * yi + ci
    c_ctx2 = c_ctx.reshape(1, DM)

    b_shard = lax.dynamic_slice(b_ada, (0, chip * SHARD_ADA), (1, SHARD_ADA))
    part = local_step(chip.reshape(1).astype(jnp.int32), dev, x[0], c, c_ctx2, w_ada[0], b_shard, ctx[0], loss_target[0],
                      norm_g, sgu_norm_g, w_spatial[0], b_spatial[0], q_norm_g, k_norm_g, rpb[0], w_in[0], w_out[0])
    cs = part["cs"]

    slab = jnp.concatenate([
        part["d_norm_g"], _rows_of(part["d_sgu_g"], 1), _rows_of(part["d_b_s"], 1),
        _rows_of(jnp.concatenate([part["d_q_g"], part["d_k_g"]], axis=-1), 1), _rows_of(part["d_rpb"], 4),
        _rows_of(part["loss"], 1), _rows_of(part["dcmod"], 3), _rows_of(part["dmod"], 3), jnp.zeros((1, DM), F32),
        _rows_of(part["d_w_s"], 64)], axis=0)
    g_w_in, g_w_out, gathered, tot = final_reduce(*part["rs"], slab)
    dm = jnp.concatenate([gathered[:, 12:15, :].reshape(NDEV, 3 * DM), tot[9:12].reshape(1, 3 * DM),
                          jnp.zeros((7, 3 * DM), F32)], axis=0)
    a_in = jnp.concatenate([cs[0:8 * NDEV:8], cs[8 * NDEV:8 * NDEV + 1], jnp.zeros((7, DM), F32)], axis=0)
    dm_shard = lax.dynamic_slice(dm, (0, chip * SHARD_ADA), (16, SHARD_ADA))
    g_w_ada, g_b_ada, g_c_ctx = ada_bwd(a_in, dm, dm_shard, w_ada[0], c_ctx2)

    loss = tot[8, 0]
    g_small = dict(
        c_ctx=g_c_ctx, b_ada=g_b_ada, norm_g=tot[0:1], sgu_norm_g=tot[1:2, :512], w_spatial=tot[16:80].reshape(512, 128),
        b_spatial=tot[2:3, :512].reshape(4, 128), q_norm_g=tot[3:4, :HDIM], k_norm_g=tot[3:4, HDIM:2 * HDIM],
        rpb=tot[4:8].reshape(-1)[:HEADS * 15 * 31].reshape(HEADS * 15, 31))
    shapes = dict(c_ctx=(DM,), w_ada=(1, DM, SHARD_ADA), b_ada=(1, 3 * DM), norm_g=(1, DM), w_in=(1, DM, SHARD_IN),
                  sgu_norm_g=(1, 512), w_spatial=(1, 4, 128, 128), b_spatial=(1, 4, 128), q_norm_g=(1, HDIM),
                  k_norm_g=(1, HDIM), rpb=(1, HEADS, 15, 31), w_out=(1, SHARD_OUT, DM))
    names = list(shapes)
    weights = dict(c_ctx=c_ctx, w_ada=w_ada, b_ada=b_ada, norm_g=norm_g, w_in=w_in, sgu_norm_g=sgu_norm_g,
                   w_spatial=w_spatial, b_spatial=b_spatial, q_norm_g=q_norm_g, k_norm_g=k_norm_g, rpb=rpb, w_out=w_out)
    m_in = dict(zip(names, (m_c_ctx, m_w_ada, m_b_ada, m_norm_g, m_w_in, m_sgu_norm_g, m_w_spatial, m_b_spatial,
                            m_q_norm_g, m_k_norm_g, m_rpb, m_w_out)))
    v_in = dict(zip(names, (v_c_ctx, v_w_ada, v_b_ada, v_norm_g, v_w_in, v_sgu_norm_g, v_w_spatial, v_b_spatial,
                            v_q_norm_g, v_k_norm_g, v_rpb, v_w_out)))
    grads = dict(g_small, w_ada=g_w_ada, w_in=g_w_in, w_out=g_w_out)
    upd = {}
    for n in ("w_ada", "w_in", "w_out"):
        g = grads[n]
        upd[n] = adamw_big(weights[n].reshape(g.shape), g, m_in[n].reshape(g.shape), v_in[n].reshape(g.shape),
                           "adamw_" + n)
    small = [n for n in names if n not in upd]
    res = adamw_small([(weights[n].reshape(grads[n].shape), grads[n], m_in[n].reshape(grads[n].shape),
                        v_in[n].reshape(grads[n].shape)) for n in small])
    upd.update(zip(small, res))
    out = [loss, part["grad_x"].reshape(1, SEQ, DM)]
    out += [grads[n].reshape(shapes[n]) for n in names]
    for slot in range(3):
        out += [upd[n][slot].reshape(shapes[n]) for n in names]
    return tuple(out)
```

```python
import jax
import jax.numpy as jnp
from jax import lax
from jax.experimental import pallas as pl
from jax.experimental.pallas import tpu as pltpu

F32, BF16 = jnp.float32, jnp.bfloat16
SEQ, DM, CTX, DIN = 4096, 1024, 256, 3584
NCHIP, NDEV = 4, 8
SHARD_IN = DIN // NCHIP
SHARD_ADA = 3 * DM // NCHIP
SHARD_OUT = DM // NCHIP
GRID_W = 64
QROWS = 4
KROWS = 12
QBLK, KBLK = QROWS * GRID_W, KROWS * GRID_W
NQBLK = SEQ // QBLK
HEADS, HDIM, NPAIR = 8, 64, 4
EPS = 1e-6
NEG_INF = -1e30
ZQ, ZK, ZV, ZG = 12, 16, 20, 24
LR, B1, B2, ADAM_EPS, WD, STEP = 0.001, 0.9, 0.999, 1e-08, 0.01, 10
VMEM_BIG = 56 * 1024 * 1024
MESH_ID = pl.DeviceIdType.MESH


def _dot(a, b, lhs_c, rhs_c):
    return lax.dot_general(a.astype(BF16), b.astype(BF16), (((lhs_c,), (rhs_c,)), ((), ())),
                           preferred_element_type=F32)


@jax.custom_vjp
def mm(a, b):
    return _dot(a, b, 1, 0)


@jax.custom_vjp
def mm_nt(a, b):
    return _dot(a, b, 1, 1)


@jax.custom_vjp
def mm_tn(a, b):
    return _dot(a, b, 0, 0)


mm.defvjp(lambda a, b: (mm(a, b), (a, b)), lambda r, ct: (mm_nt(ct, r[1]), mm_tn(r[0], ct)))
mm_nt.defvjp(lambda a, b: (mm_nt(a, b), (a, b)), lambda r, ct: (mm(ct, r[1]), mm_tn(ct, r[0])))
mm_tn.defvjp(lambda a, b: (mm_tn(a, b), (a, b)), lambda r, ct: (mm_nt(r[1], ct), mm(r[0], ct)))


def _rms(x, g):
    return x * lax.rsqrt(jnp.mean(x * x, axis=-1, keepdims=True) + EPS) * g


def _modulated(x, g, scale, shift):
    return _rms(x, g) * (1.0 + scale) + shift


def _pair_rms(x, g2):
    lo = lax.broadcasted_iota(jnp.int32, (1, 2 * HDIM), 1) < HDIM
    sq = x * x
    s_lo = jnp.sum(jnp.where(lo, sq, 0.0), axis=-1, keepdims=True)
    s_hi = jnp.sum(jnp.where(lo, 0.0, sq), axis=-1, keepdims=True)
    rs = jnp.where(lo, lax.rsqrt(s_lo / HDIM + EPS), lax.rsqrt(s_hi / HDIM + EPS))
    return x * rs * g2


def _cparams(sem, vmem=None):
    return pltpu.CompilerParams(dimension_semantics=sem, vmem_limit_bytes=vmem)


def _row(n):
    return pl.BlockSpec((1, n), lambda *_: (0, 0))


CS_ROWS = 8 * NDEV + 8


def _mod_part(mod_ref, row, part):
    pieces = []
    for j in range(NCHIP):
        lo, hi = max(part * DM, j * SHARD_ADA), min((part + 1) * DM, (j + 1) * SHARD_ADA)
        if lo < hi:
            pieces.append(mod_ref[j, row, lo - j * SHARD_ADA:hi - j * SHARD_ADA])
    return jnp.concatenate(pieces, axis=-1)


def inproj_fwd(chip, x, c_vec, c_ctx, w_ada, b_shard, norm_g, w_shard, wo_shard):
    tl = 1024
    nt = SEQ // tl
    halves = (DM // 2, SHARD_OUT // 2)
    n_w, n_c = 12, NDEV - 1

    def kern(k_ref, x_ref, cv_ref, cc_ref, wa_ref, b_ref, g_ref, w_ref, wo_ref,
             z_ref, h_ref, wfull_ref, wofull_ref, modall_ref, csall_ref,
             w_scr, wo_scr, h_scr, mine, cs_scr, mod_scr, shsc_scr, send_sems, recv_sems, hbm_sems):
        s, t = pl.program_id(0), pl.program_id(1)
        xi, yi, c = _me()
        k, me = 2 * xi + yi, 4 * xi + 2 * yi + c
        sib = _flip(1)
        rows = pl.ds(pl.multiple_of(t * tl, tl), tl)
        gathered = (w_scr, wo_scr)
        slot = lambda d: pl.ds(pl.multiple_of(8 * d, 8), 8)

        def c_copy(q, owner):
            return _rcopy(mine, cs_scr.at[slot(owner), :], send_sems, recv_sems, n_w + q - 1, _flip(q))

        def m_copy(q, chip_of_block):
            return _rcopy(mod_scr.at[chip_of_block], mod_scr.at[chip_of_block], send_sems, recv_sems,
                          n_w + n_c + q // 2 - 1, _flip(q))

        def adaln():
            first = lax.broadcasted_iota(jnp.int32, (8, DM), 0) == 0
            mine[...] = jnp.where(first, jnp.broadcast_to(cv_ref[...], (8, DM)), 0.0)
            cs_scr[slot(me), :] = mine[...]
            cs_scr[slot(NDEV), :] = jnp.where(first, jnp.broadcast_to(cc_ref[...], (8, DM)), 0.0)
            for q in range(1, NDEV):
                c_copy(q, me).start()
            wa = wa_ref[...].astype(BF16)
            for q in range(1, NDEV):
                px, py, pc = _flip(q)
                c_copy(q, 4 * px + 2 * py + pc).wait_recv()
            act = jax.nn.silu(cs_scr[...]).astype(BF16)
            mod_scr[k] = jnp.dot(act, wa, preferred_element_type=F32) + b_ref[...]
            for q in (2, 4, 6):
                m_copy(q, k).start()
            for q in (2, 4, 6):
                m_copy(q, _chip_of(_flip(q))).wait_recv()
            row = pl.ds(8 * me, 1)
            shsc_scr[0:1, :] = _mod_part(mod_scr, row, 0)
            shsc_scr[1:2, :] = _mod_part(mod_scr, row, 1)
            pltpu.sync_copy(mod_scr, modall_ref)
            pltpu.sync_copy(cs_scr, csall_ref)

        def block(n, chip_of_block, hh):
            return gathered[n].at[chip_of_block, pl.ds(pl.multiple_of(hh * halves[n], halves[n]), halves[n]), :]

        def ici(n, q, chip_of_block):
            blk = block(n, chip_of_block, c)
            return _rcopy(blk, blk, send_sems, recv_sems, 6 * n + q // 2 - 1, _flip(q))

        def d2d(n, q, chip_of_block, hh):
            blk = block(n, chip_of_block, hh)
            return _rcopy(blk, blk, send_sems, recv_sems, 6 * n + 3 + q // 2 - 1, sib)

        @pl.when((s == 0) & (t == 0))
        def _():
            adaln()
            w_scr[k] = w_ref[...].astype(BF16)
            wo_scr[k] = wo_ref[...].astype(BF16)
            for n in (0, 1):
                for q in (2, 4, 6):
                    ici(n, q, k).start()

        def pass_on(n, q):
            src = _chip_of(_flip(q))
            ici(n, q, src).wait_recv()
            d2d(n, q, src, c).start()
            d2d(n, q, src, 1 - c).wait_recv()

        for sweep in (1, 2, 3):
            @pl.when((s == sweep) & (t == 0))
            def _():
                pass_on(0, 2 * sweep)

        hbm_copies = (pltpu.make_async_copy(w_scr, wfull_ref, hbm_sems.at[0]),
                      pltpu.make_async_copy(wo_scr, wofull_ref, hbm_sems.at[1]))

        @pl.when((s == NCHIP - 1) & (t == 0))
        def _():
            hbm_copies[0].start()

        @pl.when(s == 0)
        def _():
            hb = _modulated(x_ref[...], g_ref[...], shsc_scr[1:2, :], shsc_scr[0:1, :]).astype(BF16)
            h_scr[rows, :] = hb
            h_ref[...] = hb

        z_ref[...] = jnp.dot(h_scr[rows, :], w_scr[lax.bitwise_xor(k, s)], preferred_element_type=F32)

        @pl.when((s == NCHIP - 1) & (t == nt - 1))
        def _():
            for q in (2, 4, 6):
                pass_on(1, q)
            hbm_copies[1].start()
            for q in range(1, NDEV):
                c_copy(q, me).wait_send()
            for q in (2, 4, 6):
                m_copy(q, k).wait_send()
            for n in (0, 1):
                for q in (2, 4, 6):
                    ici(n, q, k).wait_send()
                    d2d(n, q, _chip_of(_flip(q)), c).wait_send()
            hbm_copies[0].wait()
            hbm_copies[1].wait()

    once = lambda s, t, k: (jnp.where(s == 0, t, nt - 1), 0)
    hbm = pl.BlockSpec(memory_space=pl.ANY)
    n_sem = n_w + n_c + 3
    return pl.pallas_call(
        kern, name="inproj_fwd",
        grid_spec=pltpu.PrefetchScalarGridSpec(
            num_scalar_prefetch=1, grid=(NCHIP, nt),
            in_specs=[pl.BlockSpec((tl, DM), once)] + [_VMEM_SPEC] * 7,
            out_specs=[pl.BlockSpec((tl, SHARD_IN), lambda s, t, k: (t, lax.bitwise_xor(k[0], s))),
                       pl.BlockSpec((tl, DM), once), hbm, hbm, hbm, hbm],
            scratch_shapes=[pltpu.VMEM((NCHIP, DM, SHARD_IN), BF16), pltpu.VMEM((NCHIP, SHARD_OUT, DM), BF16),
                            pltpu.VMEM((SEQ, DM), BF16), pltpu.VMEM((8, DM), F32), pltpu.VMEM((CS_ROWS, DM), F32),
                            pltpu.VMEM((NCHIP, CS_ROWS, SHARD_ADA), F32), pltpu.VMEM((8, DM), F32),
                            pltpu.SemaphoreType.DMA((n_sem,)), pltpu.SemaphoreType.DMA((n_sem,)),
                            pltpu.SemaphoreType.DMA((2,))]),
        out_shape=[jax.ShapeDtypeStruct((SEQ, DIN), F32), jax.ShapeDtypeStruct((SEQ, DM), BF16),
                   jax.ShapeDtypeStruct((NCHIP, DM, SHARD_IN), BF16), jax.ShapeDtypeStruct((NCHIP, SHARD_OUT, DM), BF16),
                   jax.ShapeDtypeStruct((NCHIP, CS_ROWS, SHARD_ADA), F32), jax.ShapeDtypeStruct((CS_ROWS, DM), F32)],
        compiler_params=_cparams(("arbitrary", "arbitrary"), VMEM_BIG),
    )(chip, x, c_vec, c_ctx, w_ada, b_shard, norm_g, w_shard, wo_shard)


def ctx_fwd(ctx, cshift, cscale, norm_g, w_full):
    def kern(c_ref, sh_ref, sc_ref, g_ref, w2_ref, w3_ref, zc_ref, hc_ref):
        hc = _modulated(c_ref[...], g_ref[...], sc_ref[...], sh_ref[...]).astype(BF16)
        hc_ref[...] = hc
        zc_ref[:, :SHARD_IN] = jnp.dot(hc, w2_ref[0], preferred_element_type=F32)
        zc_ref[:, SHARD_IN:] = jnp.dot(hc, w3_ref[0], preferred_element_type=F32)

    return pl.pallas_call(
        kern, name="ctx_fwd", grid=(1,),
        in_specs=[pl.BlockSpec((CTX, DM), lambda i: (0, 0)), _row(DM), _row(DM), _row(DM),
                  pl.BlockSpec((1, DM, SHARD_IN), lambda i: (2, 0, 0)),
                  pl.BlockSpec((1, DM, SHARD_IN), lambda i: (3, 0, 0))],
        out_specs=[pl.BlockSpec((CTX, 2 * SHARD_IN), lambda i: (0, 0)),
                   pl.BlockSpec((CTX, DM), lambda i: (0, 0))],
        out_shape=[jax.ShapeDtypeStruct((CTX, 2 * SHARD_IN), F32), jax.ShapeDtypeStruct((CTX, DM), BF16)],
        compiler_params=_cparams(("arbitrary",)),
    )(ctx, cshift, cscale, norm_g, w_full, w_full)


SGU_CHUNK, SGU_PER_STEP = 128, 4


def _gelu(x):
    return 0.5 * x * (1.0 + lax.erf(x * 0.7071067811865476))


def _sgu_chunk(au, av, ag, sg, ws, bsb):
    u, v = _gelu(au), _gelu(av)
    outs = []
    for g in range(4):
        sl = slice(128 * g, 128 * (g + 1))
        mixed = mm(ws[g], _rms(v[:, sl], sg[:, sl])) + bsb[g]
        outs.append(u[:, sl] * mixed * jax.nn.silu(ag[:, sl]))
    return jnp.concatenate(outs, axis=-1)


def _sgu_specs():
    rows = SGU_CHUNK * SGU_PER_STEP
    zspec = lambda c: pl.BlockSpec((rows, 512), lambda n: (n, c))
    wspec = pl.BlockSpec((4, 128, 128), lambda n: (0, 0, 0))
    return rows, [zspec(0), zspec(1), zspec(2), _row(512), wspec, wspec]


def sgu_fwd(z, sg, ws, bsb):
    rows, in_specs = _sgu_specs()

    def kern(au_ref, av_ref, ag_ref, sg_ref, ws_ref, bs_ref, o_ref):
        for c in range(SGU_PER_STEP):
            sl = slice(c * SGU_CHUNK, (c + 1) * SGU_CHUNK)
            o_ref[sl, :] = _sgu_chunk(au_ref[sl, :], av_ref[sl, :], ag_ref[sl, :], sg_ref[...], ws_ref[...],
                                      bs_ref[...])

    return pl.pallas_call(
        kern, name="sgu_fwd", grid=(SEQ // rows,), in_specs=in_specs,
        out_specs=pl.BlockSpec((rows, 512), lambda n: (n, 0)),
        out_shape=jax.ShapeDtypeStruct((SEQ, 512), F32),
        compiler_params=_cparams(("arbitrary",)),
    )(z, z, z, sg, ws, bsb)


def sgu_bwd(z, sg, ws, bsb, dcat):
    rows, in_specs = _sgu_specs()

    def kern(au_ref, av_ref, ag_ref, sg_ref, ws_ref, bs_ref, do_ref, dz_ref, dsg_ref, dws_ref, dbs_ref):
        @pl.when(pl.program_id(0) == 0)
        def _():
            dsg_ref[...] = jnp.zeros_like(dsg_ref)
            dws_ref[...] = jnp.zeros_like(dws_ref)
            dbs_ref[...] = jnp.zeros_like(dbs_ref)

        for c in range(SGU_PER_STEP):
            sl = slice(c * SGU_CHUNK, (c + 1) * SGU_CHUNK)
            _, vjp = jax.vjp(_sgu_chunk, au_ref[sl, :], av_ref[sl, :], ag_ref[sl, :], sg_ref[...], ws_ref[...],
                             bs_ref[...])
            dau, dav, dag, dsg, dws, dbs = vjp(do_ref[sl, :])
            dz_ref[sl, 0:512] = dau.astype(BF16)
            dz_ref[sl, 512:1024] = dav.astype(BF16)
            dz_ref[sl, 1024:1536] = dag.astype(BF16)
            dsg_ref[...] += dsg
            dws_ref[...] += dws
            dbs_ref[...] += dbs

        @pl.when(pl.program_id(0) == pl.num_programs(0) - 1)
        def _():
            dbs_ref[...] = jnp.broadcast_to(jnp.sum(dbs_ref[...], axis=-1, keepdims=True), dbs_ref.shape)

    wspec = pl.BlockSpec((4, 128, 128), lambda n: (0, 0, 0))
    return pl.pallas_call(
        kern, name="sgu_bwd", grid=(SEQ // rows,),
        in_specs=in_specs + [pl.BlockSpec((rows, 512), lambda n: (n, 0))],
        out_specs=[pl.BlockSpec((rows, 1536), lambda n: (n, 0)), _row(512), wspec, wspec],
        out_shape=[jax.ShapeDtypeStruct((SEQ, 1536), BF16), jax.ShapeDtypeStruct((1, 512), F32),
                   jax.ShapeDtypeStruct((4, 128, 128), F32), jax.ShapeDtypeStruct((4, 128, 128), F32)],
        compiler_params=_cparams(("arbitrary",)),
    )(z, z, z, sg, ws, bsb, dcat)


_DR_OFF = (7, 3, -1)


def _row_valid(v, rr, j):
    return (j < 8, rr <= j < rr + 8, 4 <= j < 12)[v]


def _col_window():
    q = lax.broadcasted_iota(jnp.int32, (GRID_W, 128), 0)
    kc = lax.broadcasted_iota(jnp.int32, (GRID_W, 128), 1) % GRID_W
    c0 = jnp.clip(q - 8, 0, GRID_W - 16)
    return (kc >= c0) & (kc < c0 + 16)


def rpb_tables(rpb2):
    def kern(r_ref, b_ref):
        base = r_ref[0]
        lo = lax.broadcasted_iota(jnp.int32, (1, 128), 1) < GRID_W
        win = _col_window()
        tiles = {}
        for v in range(3):
            for rr in range(QROWS):
                for jp in range(KROWS // 2):
                    j0, j1 = 2 * jp, 2 * jp + 1
                    ok0, ok1 = _row_valid(v, rr, j0), _row_valid(v, rr, j1)
                    key = (j0 - rr + _DR_OFF[v], ok0, ok1) if (ok0 or ok1) else None
                    if key not in tiles:
                        if key is None:
                            tiles[key] = jnp.full((GRID_W, 128), NEG_INF, F32)
                        else:
                            d0 = key[0]
                            r0 = base[d0:d0 + 1, :] if ok0 else jnp.zeros((1, 128), F32)
                            r1 = base[d0 + 1:d0 + 2, :] if ok1 else jnp.zeros((1, 128), F32)
                            y = jnp.broadcast_to(jnp.where(lo, r0, r1), (GRID_W, 128))
                            y = pltpu.roll(pltpu.roll(y, 128 - 15, 1), 0, 1, stride=1, stride_axis=0)
                            tiles[key] = jnp.where(win & jnp.where(lo, ok0, ok1), y, NEG_INF)
                    b_ref[v, 0, rr * GRID_W:(rr + 1) * GRID_W, jp * 128:(jp + 1) * 128] = tiles[key]

    return pl.pallas_call(
        kern, name="rpb_tables", grid=(HEADS,),
        in_specs=[pl.BlockSpec((1, 15, 128), lambda h: (h, 0, 0))],
        out_specs=pl.BlockSpec((3, 1, QBLK, KBLK), lambda h: (0, h, 0, 0)),
        out_shape=jax.ShapeDtypeStruct((3, HEADS, QBLK, KBLK), F32),
        compiler_params=_cparams(("arbitrary",)),
    )(rpb2)


def rpb_bwd(dbias):
    def kern(g0_ref, g1_ref, g2_ref, o_ref):
        g_refs = (g0_ref, g1_ref, g2_ref)
        lo = lax.broadcasted_iota(jnp.int32, (1, 128), 1) < GRID_W
        ri = lax.broadcasted_iota(jnp.int32, (GRID_W, GRID_W), 0)
        ci = lax.broadcasted_iota(jnp.int32, (GRID_W, GRID_W), 1)
        flip = (ri + ci == GRID_W - 1).astype(F32)
        groups = {}
        for v in range(3):
            for rr in range(QROWS):
                for jp in range(KROWS // 2):
                    j0, j1 = 2 * jp, 2 * jp + 1
                    ok0, ok1 = _row_valid(v, rr, j0), _row_valid(v, rr, j1)
                    if not (ok0 or ok1):
                        continue
                    g = g_refs[v][0, rr * GRID_W:(rr + 1) * GRID_W, jp * 128:(jp + 1) * 128]
                    key = (j0 - rr + _DR_OFF[v], ok0, ok1)
                    groups[key] = g if key not in groups else groups[key] + g
        acc = [jnp.zeros((1, 128), F32) for _ in range(15)]
        for (d0, ok0, ok1), g in groups.items():
            g = lax.dot_general(flip, g, (((1,), (0,)), ((), ())), precision=lax.Precision.HIGHEST,
                                preferred_element_type=F32)
            g = pltpu.roll(pltpu.roll(g, 128 - 48, 1), 0, 1, stride=1, stride_axis=0)
            s = jnp.sum(g, axis=0, keepdims=True)
            if ok0:
                acc[d0] = acc[d0] + jnp.where(lo, s, 0.0)
            if ok1:
                acc[d0 + 1] = acc[d0 + 1] + jnp.where(lo, 0.0, s)
        for d in range(15):
            o_ref[0, d:d + 1, :] = acc[d] + pltpu.roll(acc[d], GRID_W, 1)

    return pl.pallas_call(
        kern, name="rpb_bwd", grid=(HEADS,),
        in_specs=[pl.BlockSpec((1, QBLK, KBLK), lambda h: (h, 0, 0))] * 3,
        out_specs=pl.BlockSpec((1, 15, 128), lambda h: (h, 0, 0)),
        out_shape=jax.ShapeDtypeStruct((HEADS, 15, 128), F32),
        compiler_params=_cparams(("arbitrary",)),
    )(*dbias)


def _scaled_q(q_raw, qg):
    return _pair_rms(q_raw, qg) * (HDIM ** -0.5)


def _head_lanes():
    lo = lax.broadcasted_iota(jnp.int32, (1, 2 * HDIM), 1) < HDIM
    return lo, jnp.logical_not(lo)


def _attn_step(q_raw, kn, v, ckn, cv, bias2, qg):
    qn = _scaled_q(q_raw, qg)
    out = lse = None
    for a, mine in enumerate(_head_lanes()):
        qa = jnp.where(mine, qn, 0.0)
        s_lat = mm_nt(qa, kn) + bias2[a]
        s_ctx = mm_nt(qa, ckn)
        m = jnp.maximum(jnp.max(s_lat, axis=-1, keepdims=True), jnp.max(s_ctx, axis=-1, keepdims=True))
        p_lat = jnp.exp(s_lat - m)
        p_ctx = jnp.exp(s_ctx - m)
        den = jnp.sum(p_lat, axis=-1, keepdims=True) + jnp.sum(p_ctx, axis=-1, keepdims=True)
        o = jnp.where(mine, (mm(p_lat, v) + mm(p_ctx, cv)) / den, 0.0)
        l = jnp.where(mine, m + jnp.log(den), 0.0)
        out, lse = (o, l) if out is None else (out + o, lse + l)
    return out, lse


def _attn_step_bwd(q_raw, kn, v, ckn, cv, bias2, qg, bg, o, lse, dout):
    sig = jax.nn.sigmoid(bg)
    do = dout * (bg * sig)
    dbg = dout * o * (sig * (1.0 + bg * (1.0 - sig)))
    qn, qn_vjp = jax.vjp(_scaled_q, q_raw, qg)
    row_dot = do * o
    dqn = dkn = dv = dckn = dcv = None
    dbias = []
    for mine in _head_lanes():
        qa = jnp.where(mine, qn, 0.0)
        doa = jnp.where(mine, do, 0.0)
        l = jnp.max(jnp.where(mine, lse, NEG_INF), axis=-1, keepdims=True)
        delta = jnp.sum(jnp.where(mine, row_dot, 0.0), axis=-1, keepdims=True)
        p_lat = jnp.exp(mm_nt(qa, kn) + bias2[len(dbias)] - l)
        p_ctx = jnp.exp(mm_nt(qa, ckn) - l)
        ds_lat = p_lat * (mm_nt(doa, v) - delta)
        ds_ctx = p_ctx * (mm_nt(doa, cv) - delta)
        parts = (jnp.where(mine, mm(ds_lat, kn) + mm(ds_ctx, ckn), 0.0), mm_tn(ds_lat, qa), mm_tn(p_lat, doa),
                 mm_tn(ds_ctx, qa), mm_tn(p_ctx, doa))
        if dqn is None:
            dqn, dkn, dv, dckn, dcv = parts
        else:
            dqn, dkn, dv, dckn, dcv = (acc + new for acc, new in zip((dqn, dkn, dv, dckn, dcv), parts))
        dbias.append(ds_lat)
    dq, dqg = qn_vjp(dqn)
    return dq, dkn, dv, dckn, dcv, dbias, dqg, dbg


def _kstart(i):
    return pl.multiple_of(jnp.clip((i - 1) * QBLK, 0, SEQ - KBLK), QBLK)


ATTN_STEPS = NQBLK // 2
ATTN_ROWS = 2 * QBLK


def _attn_in_specs():
    bias_spec = lambda variant: pl.BlockSpec((1, 2, QBLK, KBLK), lambda p, i: (variant(i), p, 0, 0))
    return [
        pl.BlockSpec((ATTN_ROWS, 128), lambda p, i: (i, ZQ + p)),
        pl.BlockSpec((SEQ, 128), lambda p, i: (0, ZK + p)),
        pl.BlockSpec((SEQ, 128), lambda p, i: (0, ZV + p)),
        pl.BlockSpec((ATTN_ROWS, 128), lambda p, i: (i, ZG + p)),
        pl.BlockSpec((CTX, 128), lambda p, i: (0, 2 + p)),
        pl.BlockSpec((CTX, 128), lambda p, i: (0, 6 + p)),
        bias_spec(lambda i: jnp.where(i == 0, 0, 1)),
        bias_spec(lambda i: jnp.where(i == ATTN_STEPS - 1, 2, 1)),
        _row(128), _row(128),
    ]


NORM_ROWS = 512


def _norm_keys(k_ref, ck_ref, kg_ref, kn_scr, ckn_scr):
    def body(c, carry):
        sl = pl.ds(pl.multiple_of(c * NORM_ROWS, NORM_ROWS), NORM_ROWS)
        kn_scr[sl, :] = _pair_rms(k_ref[sl, :], kg_ref[...])
        return carry

    lax.fori_loop(0, SEQ // NORM_ROWS, body, 0)
    ckn_scr[...] = _pair_rms(ck_ref[...], kg_ref[...])


def attn_fwd(z, zc, bias, qg2, kg2):
    def kern(q_ref, k_ref, v_ref, bg_ref, ck_ref, cv_ref, be_ref, bo_ref, qg_ref, kg_ref, ob_ref, o_ref, lse_ref,
             kn_scr, ckn_scr):
        i = pl.program_id(1)

        @pl.when(i == 0)
        def _():
            _norm_keys(k_ref, ck_ref, kg_ref, kn_scr, ckn_scr)

        for b, b_ref in enumerate((be_ref, bo_ref)):
            rows = slice(b * QBLK, (b + 1) * QBLK)
            ks = pl.ds(_kstart(2 * i + b), KBLK)
            o, lse = _attn_step(q_ref[rows, :], kn_scr[ks, :], v_ref[ks, :], ckn_scr[...], cv_ref[...], b_ref[0],
                                qg_ref[...])
            ob_ref[rows, :] = o * jax.nn.silu(bg_ref[rows, :])
            o_ref[rows, :] = o
            lse_ref[rows, :] = lse

    qblk = pl.BlockSpec((ATTN_ROWS, 128), lambda p, i: (i, p))
    return pl.pallas_call(
        kern, name="attn_fwd", grid=(NPAIR, ATTN_STEPS), in_specs=_attn_in_specs(), out_specs=[qblk] * 3,
        out_shape=[jax.ShapeDtypeStruct((SEQ, 512), F32)] * 3,
        scratch_shapes=[pltpu.VMEM((SEQ, 128), F32), pltpu.VMEM((CTX, 128), F32)],
        compiler_params=_cparams(("arbitrary", "arbitrary"), 40 * 1024 * 1024),
    )(z, z, z, z, zc, zc, bias, bias, qg2, kg2)


def attn_bwd(z, zc, bias, qg2, kg2, dcat, o_raw, lse):
    def kern(q_ref, k_ref, v_ref, bg_ref, ck_ref, cv_ref, be_ref, bo_ref, qg_ref, kg_ref, do_ref, o_ref, lse_ref,
             dq_ref, dk_ref, dv_ref, dbg_ref, dck_ref, dcv_ref, db0_ref, db1_ref, db2_ref, dqg_ref, dkg_ref,
             kn_scr, ckn_scr, dkn_scr, dckn_scr, dv_scr):
        p, i = pl.program_id(0), pl.program_id(1)
        last = i == ATTN_STEPS - 1

        @pl.when(i == 0)
        def _():
            _norm_keys(k_ref, ck_ref, kg_ref, kn_scr, ckn_scr)
            dkn_scr[...] = jnp.zeros_like(dkn_scr)
            dv_scr[...] = jnp.zeros_like(dv_scr)
            dckn_scr[...] = jnp.zeros_like(dckn_scr)
            dcv_ref[...] = jnp.zeros_like(dcv_ref)

        @pl.when((i == 0) & (p == 0))
        def _():
            dqg_ref[...] = jnp.zeros_like(dqg_ref)
            dkg_ref[...] = jnp.zeros_like(dkg_ref)

        db = []
        for b, b_ref in enumerate((be_ref, bo_ref)):
            rows = slice(b * QBLK, (b + 1) * QBLK)
            ks = pl.ds(_kstart(2 * i + b), KBLK)
            dq, dkn, dv, dckn, dcv, dbb, dqg, dbg = _attn_step_bwd(
                q_ref[rows, :], kn_scr[ks, :], v_ref[ks, :], ckn_scr[...], cv_ref[...], b_ref[0], qg_ref[...],
                bg_ref[rows, :], o_ref[rows, :], lse_ref[rows, :], do_ref[rows, :])
            dq_ref[rows, :] = dq.astype(BF16)
            dbg_ref[rows, :] = dbg.astype(BF16)
            dkn_scr[ks, :] += dkn
            dv_scr[ks, :] += dv
            dckn_scr[...] += dckn
            dcv_ref[...] += dcv
            dqg_ref[...] += dqg
            db.append(dbb)

        @pl.when(i == 0)
        def _():
            for a in range(2):
                db0_ref[a] = db[0][a]
                db1_ref[a] = db[1][a]

        @pl.when((i > 0) & jnp.logical_not(last))
        def _():
            for a in range(2):
                db1_ref[a] += db[0][a] + db[1][a]

        @pl.when(last)
        def _():
            for a in range(2):
                db1_ref[a] += db[0][a]
                db2_ref[a] = db[1][a]

        @pl.when(last)
        def _():
            def body(c, dkg):
                sl = pl.ds(pl.multiple_of(c * NORM_ROWS, NORM_ROWS), NORM_ROWS)
                _, nvjp = jax.vjp(_pair_rms, k_ref[sl, :], kg_ref[...])
                dk, dg = nvjp(dkn_scr[sl, :])
                dk_ref[sl, :] = dk.astype(BF16)
                dv_ref[sl, :] = dv_scr[sl, :].astype(BF16)
                return dkg + dg

            dkg = lax.fori_loop(0, SEQ // NORM_ROWS, body, jnp.zeros((1, 128), F32))
            _, nvjp = jax.vjp(_pair_rms, ck_ref[...], kg_ref[...])
            dck, dg = nvjp(dckn_scr[...])
            dck_ref[...] = dck
            dkg_ref[...] += dkg + dg

        @pl.when(last & (p == NPAIR - 1))
        def _():
            dqg_ref[...] = dqg_ref[...] + pltpu.roll(dqg_ref[...], HDIM, 1)
            dkg_ref[...] = dkg_ref[...] + pltpu.roll(dkg_ref[...], HDIM, 1)

    blk = lambda rows: pl.BlockSpec((rows, 128), lambda p, i: (0, p))
    qblk = pl.BlockSpec((ATTN_ROWS, 128), lambda p, i: (i, p))
    dbias = pl.BlockSpec((2, QBLK, KBLK), lambda p, i: (p, 0, 0))
    return pl.pallas_call(
        kern, name="attn_bwd", grid=(NPAIR, ATTN_STEPS),
        in_specs=_attn_in_specs() + [pl.BlockSpec((ATTN_ROWS, 128), lambda p, i: (i, 4 + p)), qblk, qblk],
        out_specs=[qblk, blk(SEQ), blk(SEQ), qblk, blk(CTX), blk(CTX), dbias, dbias, dbias, _row(128), _row(128)],
        out_shape=[jax.ShapeDtypeStruct((SEQ, 512), BF16)] * 4 + [jax.ShapeDtypeStruct((CTX, 512), F32)] * 2
        + [jax.ShapeDtypeStruct((HEADS, QBLK, KBLK), F32)] * 3
        + [jax.ShapeDtypeStruct((1, 128), F32), jax.ShapeDtypeStruct((1, 128), F32)],
        scratch_shapes=[pltpu.VMEM((SEQ, 128), F32), pltpu.VMEM((CTX, 128), F32),
                        pltpu.VMEM((SEQ, 128), F32), pltpu.VMEM((CTX, 128), F32), pltpu.VMEM((SEQ, 128), F32)],
        compiler_params=_cparams(("arbitrary", "arbitrary"), VMEM_BIG),
    )(z, z, z, z, zc, zc, bias, bias, qg2, kg2, dcat, o_raw, lse)


def outproj(out_a, out_b, x, target, gate, wo):
    tl = 512

    def kern(a_ref, b_ref, x_ref, t_ref, g_ref, w_ref, loss_ref, dy_ref, dcat_ref, dg_ref, dw_ref):
        @pl.when(pl.program_id(0) == 0)
        def _():
            loss_ref[...] = jnp.zeros_like(loss_ref)
            dg_ref[...] = jnp.zeros_like(dg_ref)
            dw_ref[...] = jnp.zeros_like(dw_ref)

        a, b = a_ref[...].astype(BF16), b_ref[...].astype(BF16)
        mix = (jnp.dot(a, w_ref[0:512, :], preferred_element_type=F32)
               + jnp.dot(b, w_ref[512:1024, :], preferred_element_type=F32))
        err = x_ref[...] + g_ref[...] * mix - t_ref[...]
        loss_ref[...] += 0.5 * jnp.sum(jnp.mean(err * err, axis=-1))
        dy = err * (1.0 / DM)
        dy_ref[...] = dy
        dg_ref[...] += jnp.sum(dy * mix, axis=0, keepdims=True)
        dmix = (g_ref[...] * dy).astype(BF16)
        dcat_ref[...] = lax.dot_general(dmix, w_ref[...], (((1,), (1,)), ((), ())), preferred_element_type=F32)
        dw_ref[0:512, :] += lax.dot_general(a, dmix, (((0,), (0,)), ((), ())), preferred_element_type=F32)
        dw_ref[512:1024, :] += lax.dot_general(b, dmix, (((0,), (0,)), ((), ())), preferred_element_type=F32)

    tile = lambda w: pl.BlockSpec((tl, w), lambda t: (t, 0))
    whole = pl.BlockSpec((DM, DM), lambda t: (0, 0))
    return pl.pallas_call(
        kern, name="outproj", grid=(SEQ // tl,),
        in_specs=[tile(512), tile(512), tile(DM), tile(DM), _row(DM), whole],
        out_specs=[pl.BlockSpec((8, 128), lambda t: (0, 0)), tile(DM), tile(DM), _row(DM), whole],
        out_shape=[jax.ShapeDtypeStruct((8, 128), F32), jax.ShapeDtypeStruct((SEQ, DM), F32),
                   jax.ShapeDtypeStruct((SEQ, DM), F32), jax.ShapeDtypeStruct((1, DM), F32),
                   jax.ShapeDtypeStruct((DM, DM), F32)],
        compiler_params=_cparams(("arbitrary",), 48 * 1024 * 1024),
    )(out_a, out_b, x, target, gate, wo)


def _pieces(sources):
    out = []
    for name, c0, c1 in sources:
        for j in range(NCHIP):
            lo, hi = max(c0, j * SHARD_IN), min(c1, (j + 1) * SHARD_IN)
            if lo < hi:
                out.append((j, lo - j * SHARD_IN, hi - j * SHARD_IN, name, lo - c0, hi - c0))
    return out


DZ_PIECES = _pieces((("a", 0, 1536), ("q", 1536, 2048), ("k", 2048, 2560), ("v", 2560, 3072), ("g", 3072, DIN)))
DZC_PIECES = _pieces((("k", 2048, 2560), ("v", 2560, 3072)))
_NT = (((1,), (1,)), ((), ()))


DH_SUBTILES = 2


def _dz_specs(tl):
    return [pl.BlockSpec((tl, 1536), lambda t: (t, 0))] + [pl.BlockSpec((tl, 512), lambda t: (t, 0))] * 4


def dh_bwd(dz_parts, w_full, x, dy, shift, scale, norm_g, dg_ctx, wire_i, wire_o):
    tl = 512
    nt = SEQ // tl

    def kern(a_ref, q_ref, k_ref, v_ref, g_ref, w_ref, x_ref, dy_ref, sh_ref, sc_ref, gn_ref, dgc_ref, wi_hbm, wo_hbm,
             gx_ref, dsh_ref, dsc_ref, dg_ref, goti_ref, goto_ref, rcv_i, rcv_o, send_sems, recv_sems):
        def ici(n, q):
            wire, rcv = ((wi_hbm, rcv_i), (wo_hbm, rcv_o))[n]
            return _rcopy(wire.at[_chip_of(_flip(q))], rcv.at[q // 2 - 1], send_sems, recv_sems, 3 * n + q // 2 - 1,
                          _flip(q))

        @pl.when(pl.program_id(0) == 0)
        def _():
            for n in (0, 1):
                for q in (2, 4, 6):
                    ici(n, q).start()

        @pl.when(pl.program_id(0) == 0)
        def _():
            dsh_ref[...] = jnp.zeros_like(dsh_ref)
            dsc_ref[...] = jnp.zeros_like(dsc_ref)
            dg_ref[...] = dgc_ref[...]

        src = dict(a=a_ref, q=q_ref, k=k_ref, v=v_ref, g=g_ref)
        for sub in range(DH_SUBTILES):
            rows = slice(sub * tl // DH_SUBTILES, (sub + 1) * tl // DH_SUBTILES)
            dh = None
            for j, l0, l1, name, s0, s1 in DZ_PIECES:
                part = lax.dot_general(src[name][rows, s0:s1], w_ref[j, :, l0:l1], _NT, preferred_element_type=F32)
                dh = part if dh is None else dh + part
            _, vjp = jax.vjp(_modulated, x_ref[rows, :], gn_ref[...], sc_ref[...], sh_ref[...])
            dx, dg, dsc, dsh = vjp(dh)
            gx_ref[rows, :] = dy_ref[rows, :] + dx
            dg_ref[...] += dg
            dsc_ref[...] += dsc
            dsh_ref[...] += dsh

        @pl.when(pl.program_id(0) == nt - 1)
        def _():
            for n in (0, 1):
                for q in (2, 4, 6):
                    ici(n, q).wait_recv()
                    ici(n, q).wait_send()
            goti_ref[...] = rcv_i[...]
            goto_ref[...] = rcv_o[...]

    tile = pl.BlockSpec((tl, DM), lambda t: (t, 0))
    hbm = pl.BlockSpec(memory_space=pl.ANY)
    got = [(NCHIP - 1, rh, w) for rh, w in RS_SHAPES]
    return pl.pallas_call(
        kern, name="dh_bwd", grid=(nt,),
        in_specs=_dz_specs(tl) + [pl.BlockSpec((NCHIP, DM, SHARD_IN), lambda t: (0, 0, 0)), tile, tile, _row(DM),
                                  _row(DM), _row(DM), _row(DM), hbm, hbm],
        out_specs=[tile, _row(DM), _row(DM), _row(DM)] + [pl.BlockSpec(s, lambda t: (0, 0, 0)) for s in got],
        out_shape=[jax.ShapeDtypeStruct((SEQ, DM), F32)] + [jax.ShapeDtypeStruct((1, DM), F32)] * 3
        + [jax.ShapeDtypeStruct(s, BF16) for s in got],
        scratch_shapes=[pltpu.VMEM(s, BF16) for s in got] + [pltpu.SemaphoreType.DMA((6,)), pltpu.SemaphoreType.DMA((6,))],
        compiler_params=_cparams(("arbitrary",), VMEM_BIG),
    )(*dz_parts, w_full, x, dy, shift, scale, norm_g, dg_ctx, wire_i, wire_o)


def dw_bwd(h, dz_parts, hc, dck, dcv):
    tl = 512

    def kern(h_ref, a_ref, q_ref, k_ref, v_ref, g_ref, hc_ref, dck_ref, dcv_ref, dw_ref):
        @pl.when(pl.program_id(0) == 0)
        def _():
            dw_ref[...] = jnp.zeros_like(dw_ref)
            hct = hc_ref[...].T
            csrc = dict(k=dck_ref, v=dcv_ref)
            for j, l0, l1, name, s0, s1 in DZC_PIECES:
                dw_ref[j, :, l0:l1] += jnp.dot(hct, csrc[name][:, s0:s1].astype(BF16), preferred_element_type=F32)

        ht = h_ref[...].T
        src = dict(a=a_ref, q=q_ref, k=k_ref, v=v_ref, g=g_ref)
        for j, l0, l1, name, s0, s1 in DZ_PIECES:
            dw_ref[j, :, l0:l1] += jnp.dot(ht, src[name][:, s0:s1], preferred_element_type=F32)

    whole = lambda r, c: pl.BlockSpec((r, c), lambda t: (0, 0))
    return pl.pallas_call(
        kern, name="dw_bwd", grid=(SEQ // tl,),
        in_specs=[pl.BlockSpec((tl, DM), lambda t: (t, 0))] + _dz_specs(tl) + [whole(CTX, DM), whole(CTX, 512),
                                                                              whole(CTX, 512)],
        out_specs=pl.BlockSpec((NCHIP, DM, SHARD_IN), lambda t: (0, 0, 0)),
        out_shape=jax.ShapeDtypeStruct((NCHIP, DM, SHARD_IN), F32),
        compiler_params=_cparams(("arbitrary",), VMEM_BIG),
    )(h, *dz_parts, hc, dck, dcv)


def ctx_bwd(dck, dcv, w_full, ctx, cshift, cscale, norm_g):
    def kern(dck_ref, dcv_ref, w_ref, c_ref, sh_ref, sc_ref, g_ref, dsh_ref, dsc_ref, dg_ref):
        csrc = dict(k=dck_ref, v=dcv_ref)
        dhc = None
        for j, l0, l1, name, s0, s1 in DZC_PIECES:
            part = lax.dot_general(csrc[name][:, s0:s1].astype(BF16), w_ref[j, :, l0:l1], _NT,
                                   preferred_element_type=F32)
            dhc = part if dhc is None else dhc + part
        _, vjp = jax.vjp(lambda g, sc, sh: _modulated(c_ref[...], g, sc, sh), g_ref[...], sc_ref[...], sh_ref[...])
        dg_ref[...], dsc_ref[...], dsh_ref[...] = vjp(dhc)

    whole = lambda r, c: pl.BlockSpec((r, c), lambda i: (0, 0))
    return pl.pallas_call(
        kern, name="ctx_bwd", grid=(1,),
        in_specs=[whole(CTX, 512), whole(CTX, 512), pl.BlockSpec((NCHIP, DM, SHARD_IN), lambda i: (0, 0, 0)),
                  whole(CTX, DM), _row(DM), _row(DM), _row(DM)],
        out_specs=[_row(DM), _row(DM), _row(DM)],
        out_shape=[jax.ShapeDtypeStruct((1, DM), F32)] * 3,
        compiler_params=_cparams(("arbitrary",), 40 * 1024 * 1024),
    )(dck, dcv, w_full, ctx, cshift, cscale, norm_g)


def _lane_pad_rpb(rpb):
    r = jnp.pad(rpb, ((0, 0), (0, 0), (0, GRID_W - rpb.shape[-1])))
    return jnp.concatenate([r, r], axis=-1)


def local_step(chip, dev, x, c_vec, c_ctx, w_ada, b_shard, ctx, target, norm_g, sgu_g, w_s, b_s, q_g, k_g, rpb,
               w_in_shard, w_out_shard):
    bsb = jnp.broadcast_to(b_s[:, :, None], (4, 128, 128))
    qg2, kg2 = jnp.tile(q_g, (1, 2)), jnp.tile(k_g, (1, 2))

    z, h, w_in_full, w_out_full, mod_all, cs = inproj_fwd(chip, x, c_vec, c_ctx, w_ada, b_shard, norm_g, w_in_shard,
                                                          w_out_shard)
    mods = mod_all.transpose(1, 0, 2).reshape(CS_ROWS, 3 * DM)
    mod = lax.dynamic_slice(mods, (8 * dev, 0), (1, 3 * DM))
    shift, scale, gate = mod[:, :DM], mod[:, DM:2 * DM], mod[:, 2 * DM:]
    cshift, cscale = mods[8 * NDEV:8 * NDEV + 1, :DM], mods[8 * NDEV:8 * NDEV + 1, DM:2 * DM]
    zc, hc = ctx_fwd(ctx, cshift, cscale, norm_g, w_in_full)
    bias = rpb_tables(_lane_pad_rpb(rpb))
    out_a = sgu_fwd(z, sgu_g, w_s, bsb)
    out_b, o_raw, lse = attn_fwd(z, zc, bias, qg2, kg2)
    loss8, dy, dcat, dgate, dwo = outproj(out_a, out_b, x, target, gate, w_out_full.reshape(DM, DM))
    dz_a, dsg, dws, dbsb = sgu_bwd(z, sgu_g, w_s, bsb, dcat)
    dq, dk, dv, dbg, dck, dcv, db0, db1, db2, dqg2, dkg2 = attn_bwd(z, zc, bias, qg2, kg2, dcat, o_raw, lse)
    drpb = rpb_bwd((db0, db1, db2))[:, :, :rpb.shape[-1]]
    dz_parts = (dz_a, dq, dk, dv, dbg)
    dcshift, dcscale, dng_c = ctx_bwd(dck, dcv, w_in_full, ctx, cshift, cscale, norm_g)
    dw_in = dw_bwd(h, dz_parts, hc, dck, dcv)
    wire_i, keep_i, wire_o, keep_o = pair_sum(dw_in, dwo.reshape(NCHIP, SHARD_OUT, DM))
    grad_x, dshift, dscale, dng, got_i, got_o = dh_bwd(dz_parts, w_in_full, x, dy, shift, scale, norm_g, dng_c,
                                                       wire_i, wire_o)
    return dict(
        loss=loss8[0:1, 0:1], grad_x=grad_x, rs=(keep_i, got_i, keep_o, got_o), cs=cs,
        dmod=jnp.concatenate([dshift, dscale, dgate], axis=-1),
        dcmod=jnp.concatenate([dcshift, dcscale, jnp.zeros((1, DM), F32)], axis=-1),
        d_norm_g=dng, d_sgu_g=dsg, d_w_s=dws, d_b_s=dbsb[:, :, 0],
        d_q_g=dqg2[:, :HDIM], d_k_g=dkg2[:, :HDIM], d_rpb=drpb)


def _me():
    return lax.axis_index("x"), lax.axis_index("y"), lax.axis_index("c")


def _flip(q):
    x, y, c = _me()
    return ((1 - x) if q & 4 else x, (1 - y) if q & 2 else y, (1 - c) if q & 1 else c)


def _chip_of(dev):
    return 2 * dev[0] + dev[1]


def _rcopy(src, dst, send_sems, recv_sems, k, dev):
    return pltpu.make_async_remote_copy(src_ref=src, dst_ref=dst, send_sem=send_sems.at[k], recv_sem=recv_sems.at[k],
                                        device_id=dev, device_id_type=MESH_ID)


_VMEM_SPEC = pl.BlockSpec(memory_space=pltpu.VMEM)
SLAB_ROWS = 80


RS_SHAPES = ((DM // 2, SHARD_IN), (SHARD_OUT // 2, DM))


def pair_sum(g_in, g_out):
    def kern(gi_hbm, go_hbm, wire_i, keep_i, wire_o, keep_o, mine_i, rcv_i, mine_o, rcv_o, load_sems, send_sems,
             recv_sems):
        x, y, c = _me()
        k = 2 * x + y
        sib = _flip(1)
        work = ((gi_hbm, mine_i, rcv_i, wire_i, keep_i), (go_hbm, mine_o, rcv_o, wire_o, keep_o))
        copies = []
        for n, (g, mine, rcv, _, _) in enumerate(work):
            rh = RS_SHAPES[n][0]
            half = lambda hh, rh=rh: pl.ds(pl.multiple_of(hh * rh, rh), rh)
            load = pltpu.make_async_copy(g.at[:, half(c), :], mine, load_sems.at[n])
            load.start()
            pair = _rcopy(g.at[:, half(1 - c), :], rcv, send_sems, recv_sems, n, sib)
            pair.start()
            copies.append((load, pair))
        for (load, pair), (_, mine, rcv, wire, keep) in zip(copies, work):
            load.wait()
            pair.wait_recv()
            for j in range(NCHIP):
                wire[j] = (mine[j] + rcv[j]).astype(BF16)
            keep[...] = mine[k] + rcv[k]
        for _, pair in copies:
            pair.wait_send()

    (rhi, wi), (rho, wo) = RS_SHAPES
    hbm = pl.BlockSpec(memory_space=pl.ANY)
    return pl.pallas_call(
        kern, name="pair_sum", in_specs=[hbm, hbm], out_specs=[_VMEM_SPEC] * 4,
        out_shape=[jax.ShapeDtypeStruct((NCHIP, rhi, wi), BF16), jax.ShapeDtypeStruct((rhi, wi), F32),
                   jax.ShapeDtypeStruct((NCHIP, rho, wo), BF16), jax.ShapeDtypeStruct((rho, wo), F32)],
        scratch_shapes=[pltpu.VMEM((NCHIP, rhi, wi), F32), pltpu.VMEM((NCHIP, rhi, wi), F32),
                        pltpu.VMEM((NCHIP, rho, wo), F32), pltpu.VMEM((NCHIP, rho, wo), F32),
                        pltpu.SemaphoreType.DMA((2,)), pltpu.SemaphoreType.DMA((2,)), pltpu.SemaphoreType.DMA((2,))],
        compiler_params=pltpu.CompilerParams(vmem_limit_bytes=48 * 1024 * 1024),
    )(g_in, g_out)


def final_reduce(keep_i, got_i, keep_o, got_o, slab):
    def kern(ki_ref, gi_ref, ko_ref, go_ref, s_ref, gin_ref, gout_ref, all_ref, tot_ref, send_sems, recv_sems):
        x, y, c = _me()
        sib = _flip(1)
        dev = lambda d: 4 * d[0] + 2 * d[1] + d[2]
        me = dev((x, y, c))

        def slab_copy(idx, owner, to):
            return _rcopy(all_ref.at[dev(owner)], all_ref.at[dev(owner)], send_sems, recv_sems, idx, to)

        all_ref[me] = s_ref[...]
        first = [slab_copy(0, (x, y, c), sib)] + [slab_copy(q // 2, (x, y, c), _flip(q)) for q in (2, 4, 6)]
        for cp in first:
            cp.start()

        shares = []
        for n, (keep, got, out) in enumerate(((ki_ref, gi_ref, gin_ref), (ko_ref, go_ref, gout_ref))):
            rh = RS_SHAPES[n][0]
            half = lambda hh, rh=rh: pl.ds(pl.multiple_of(hh * rh, rh), rh)
            out[half(c), :] = ((keep[...] + got[0].astype(F32)) + got[1].astype(F32)) + got[2].astype(F32)
            share = _rcopy(out.at[half(c), :], out.at[half(c), :], send_sems, recv_sems, 7 + n, sib)
            share.start()
            shares.append((share, _rcopy(out.at[half(1 - c), :], out.at[half(1 - c), :], send_sems, recv_sems, 7 + n,
                                         sib)))

        passed = []
        for q in (2, 4, 6):
            slab_copy(q // 2, _flip(q), (x, y, c)).wait_recv()
            cp = slab_copy(3 + q // 2, _flip(q), sib)
            cp.start()
            passed.append(cp)
        slab_copy(0, sib, (x, y, c)).wait_recv()
        for q in (2, 4, 6):
            slab_copy(3 + q // 2, _flip(q | 1), (x, y, c)).wait_recv()
        tot = all_ref[0]
        for d in range(1, NDEV):
            tot = tot + all_ref[d]
        tot_ref[...] = tot
        for share, arrival in shares:
            arrival.wait_recv()
            share.wait_send()
        for cp in first + passed:
            cp.wait_send()

    (rhi, wi), (rho, wo) = RS_SHAPES
    return pl.pallas_call(
        kern, name="final_reduce", in_specs=[_VMEM_SPEC] * 5, out_specs=[_VMEM_SPEC] * 4,
        out_shape=[jax.ShapeDtypeStruct((2 * rhi, wi), F32), jax.ShapeDtypeStruct((2 * rho, wo), F32),
                   jax.ShapeDtypeStruct((NDEV, SLAB_ROWS, DM), F32), jax.ShapeDtypeStruct((SLAB_ROWS, DM), F32)],
        scratch_shapes=[pltpu.SemaphoreType.DMA((9,)), pltpu.SemaphoreType.DMA((9,))],
        compiler_params=pltpu.CompilerParams(vmem_limit_bytes=40 * 1024 * 1024),
    )(keep_i, got_i, keep_o, got_o, slab)


def ada_bwd(a_in, dm, dm_shard, w_ada, c_ctx):
    def kern(a_ref, dm_ref, dms_ref, w_ref, cc_ref, dw_ref, db_ref, dcc_ref, parts, send_sems, recv_sems):
        x, y, c = _me()
        k = 2 * x + y
        act = jax.nn.silu(a_ref[...]).astype(BF16)
        dms = dms_ref[...].astype(BF16)
        dw_ref[...] = lax.dot_general(act, dms, (((0,), (0,)), ((), ())), preferred_element_type=F32)
        db_ref[...] = jnp.sum(dm_ref[...], axis=0, keepdims=True)
        parts[k] = lax.dot_general(dms, w_ref[...].astype(BF16), (((1,), (1,)), ((), ())), preferred_element_type=F32)
        sends = [_rcopy(parts.at[k], parts.at[k], send_sems, recv_sems, q // 2 - 1, _flip(q)) for q in (2, 4, 6)]
        for cp in sends:
            cp.start()
        for q in (2, 4, 6):
            kq = _chip_of(_flip(q))
            _rcopy(parts.at[kq], parts.at[kq], send_sems, recv_sems, q // 2 - 1, _flip(q)).wait_recv()
        dact = ((parts[0] + parts[1]) + parts[2]) + parts[3]
        _, vjp = jax.vjp(jax.nn.silu, cc_ref[...])
        dcc_ref[...] = vjp(dact[8:9, :])[0]
        for cp in sends:
            cp.wait_send()

    return pl.pallas_call(
        kern, name="ada_bwd", in_specs=[_VMEM_SPEC] * 5, out_specs=[_VMEM_SPEC] * 3,
        out_shape=[jax.ShapeDtypeStruct((DM, SHARD_ADA), F32), jax.ShapeDtypeStruct((1, 3 * DM), F32),
                   jax.ShapeDtypeStruct((1, DM), F32)],
        scratch_shapes=[pltpu.VMEM((NCHIP, 16, DM), F32), pltpu.SemaphoreType.DMA((3,)), pltpu.SemaphoreType.DMA((3,))],
    )(a_in, dm, dm_shard, w_ada, c_ctx)


def _adamw_math(w, g, m, v):
    m = B1 * m + (1.0 - B1) * g
    v = B2 * v + (1.0 - B2) * (g * g)
    m_hat = m / (1.0 - B1 ** STEP)
    v_hat = v / (1.0 - B2 ** STEP)
    return -LR * (m_hat / (jnp.sqrt(v_hat) + ADAM_EPS) + WD * w), m, v


def adamw_big(w, g, m, v, name, block_rows=256):
    rows, width = w.shape

    def kern(w_ref, g_ref, m_ref, v_ref, d_ref, nm_ref, nv_ref):
        d_ref[...], nm_ref[...], nv_ref[...] = _adamw_math(w_ref[...], g_ref[...], m_ref[...], v_ref[...])

    spec = pl.BlockSpec((block_rows, width), lambda i: (i, 0))
    return pl.pallas_call(
        kern, name=name, grid=(rows // block_rows,), in_specs=[spec] * 4, out_specs=[spec] * 3,
        out_shape=[jax.ShapeDtypeStruct((rows, width), F32)] * 3,
        compiler_params=_cparams(("arbitrary",)),
    )(w, g, m, v)


def adamw_small(quads):
    n = len(quads)

    def kern(*refs):
        ins, outs = refs[:4 * n], refs[4 * n:]
        for i in range(n):
            w, g, m, v = (r[...] for r in ins[4 * i:4 * i + 4])
            outs[3 * i][...], outs[3 * i + 1][...], outs[3 * i + 2][...] = _adamw_math(w, g, m, v)

    flat = [a for quad in quads for a in quad]
    res = pl.pallas_call(
        kern, name="adamw_small", in_specs=[_VMEM_SPEC] * (4 * n), out_specs=[_VMEM_SPEC] * (3 * n),
        out_shape=[jax.ShapeDtypeStruct(q[0].shape, F32) for q in quads for _ in range(3)],
    )(*flat)
    return [tuple(res[3 * i:3 * i + 3]) for i in range(n)]


def _rows_of(a, rows):
    flat = a.reshape(-1)
    return jnp.pad(flat, (0, rows * DM - flat.shape[0])).reshape(rows, DM)


def kernel(x, c, ctx, c_ctx, w_ada, b_ada, norm_g, w_in, sgu_norm_g, w_spatial, b_spatial, q_norm_g, k_norm_g, rpb, w_out, loss_target, m_c_ctx, m_w_ada, m_b_ada, m_norm_g, m_w_in, m_sgu_norm_g, m_w_spatial, m_b_spatial, m_q_norm_g, m_k_norm_g, m_rpb, m_w_out, v_c_ctx, v_w_ada, v_b_ada, v_norm_g, v_w_in, v_sgu_norm_g, v_w_spatial, v_b_spatial, v_q_norm_g, v_k_norm_g, v_rpb, v_w_out):
    xi, yi, ci = lax.axis_index("x"), lax.axis_index("y"), lax.axis_index("c")
    chip, dev = 2 * xi + yi, 4 * xi + 2 * yi + ci
    c_ctx2 = c_ctx.reshape(1, DM)

    b_shard = lax.dynamic_slice(b_ada, (0, chip * SHARD_ADA), (1, SHARD_ADA))
    part = local_step(chip.reshape(1).astype(jnp.int32), dev, x[0], c, c_ctx2, w_ada[0], b_shard, ctx[0], loss_target[0],
                      norm_g, sgu_norm_g, w_spatial[0], b_spatial[0], q_norm_g, k_norm_g, rpb[0], w_in[0], w_out[0])
    cs = part["cs"]

    slab = jnp.concatenate([
        part["d_norm_g"], _rows_of(part["d_sgu_g"], 1), _rows_of(part["d_b_s"], 1),
        _rows_of(jnp.concatenate([part["d_q_g"], part["d_k_g"]], axis=-1), 1), _rows_of(part["d_rpb"], 4),
        _rows_of(part["loss"], 1), _rows_of(part["dcmod"], 3), _rows_of(part["dmod"], 3), jnp.zeros((1, DM), F32),
        _rows_of(part["d_w_s"], 64)], axis=0)
    g_w_in, g_w_out, gathered, tot = final_reduce(*part["rs"], slab)
    dm = jnp.concatenate([gathered[:, 12:15, :].reshape(NDEV, 3 * DM), tot[9:12].reshape(1, 3 * DM),
                          jnp.zeros((7, 3 * DM), F32)], axis=0)
    a_in = jnp.concatenate([cs[0:8 * NDEV:8], cs[8 * NDEV:8 * NDEV + 1], jnp.zeros((7, DM), F32)], axis=0)
    dm_shard = lax.dynamic_slice(dm, (0, chip * SHARD_ADA), (16, SHARD_ADA))
    g_w_ada, g_b_ada, g_c_ctx = ada_bwd(a_in, dm, dm_shard, w_ada[0], c_ctx2)

    loss = tot[8, 0]
    g_small = dict(
        c_ctx=g_c_ctx, b_ada=g_b_ada, norm_g=tot[0:1], sgu_norm_g=tot[1:2, :512], w_spatial=tot[16:80].reshape(512, 128),
        b_spatial=tot[2:3, :512].reshape(4, 128), q_norm_g=tot[3:4, :HDIM], k_norm_g=tot[3:4, HDIM:2 * HDIM],
        rpb=tot[4:8].reshape(-1)[:HEADS * 15 * 31].reshape(HEADS * 15, 31))
    shapes = dict(c_ctx=(DM,), w_ada=(1, DM, SHARD_ADA), b_ada=(1, 3 * DM), norm_g=(1, DM), w_in=(1, DM, SHARD_IN),
                  sgu_norm_g=(1, 512), w_spatial=(1, 4, 128, 128), b_spatial=(1, 4, 128), q_norm_g=(1, HDIM),
                  k_norm_g=(1, HDIM), rpb=(1, HEADS, 15, 31), w_out=(1, SHARD_OUT, DM))
    names = list(shapes)
    weights = dict(c_ctx=c_ctx, w_ada=w_ada, b_ada=b_ada, norm_g=norm_g, w_in=w_in, sgu_norm_g=sgu_norm_g,
                   w_spatial=w_spatial, b_spatial=b_spatial, q_norm_g=q_norm_g, k_norm_g=k_norm_g, rpb=rpb, w_out=w_out)
    m_in = dict(zip(names, (m_c_ctx, m_w_ada, m_b_ada, m_norm_g, m_w_in, m_sgu_norm_g, m_w_spatial, m_b_spatial,
                            m_q_norm_g, m_k_norm_g, m_rpb, m_w_out)))
    v_in = dict(zip(names, (v_c_ctx, v_w_ada, v_b_ada, v_norm_g, v_w_in, v_sgu_norm_g, v_w_spatial, v_b_spatial,
                            v_q_norm_g, v_k_norm_g, v_rpb, v_w_out)))
    grads = dict(g_small, w_ada=g_w_ada, w_in=g_w_in, w_out=g_w_out)
    upd = {}
    for n in ("w_ada", "w_in", "w_out"):
        g = grads[n]
        upd[n] = adamw_big(weights[n].reshape(g.shape), g, m_in[n].reshape(g.shape), v_in[n].reshape(g.shape),
                           "adamw_" + n)
    small = [n for n in names if n not in upd]
    res = adamw_small([(weights[n].reshape(grads[n].shape), grads[n], m_in[n].reshape(grads[n].shape),
                        v_in[n].reshape(grads[n].shape)) for n in small])
    upd.update(zip(small, res))
    out = [loss, part["grad_x"].reshape(1, SEQ, DM)]
    out += [grads[n].reshape(shapes[n]) for n in names]
    for slot in range(3):
        out += [upd[n][slot].reshape(shapes[n]) for n in names]
    return tuple(out)
```

```python
import jax
import jax.numpy as jnp
from jax import lax
from jax.experimental import pallas as pl
from jax.experimental.pallas import tpu as pltpu

F32, BF16 = jnp.float32, jnp.bfloat16
SEQ, DM, CTX, DIN = 4096, 1024, 256, 3584
NCHIP, NDEV = 4, 8
SHARD_IN = DIN // NCHIP
SHARD_ADA = 3 * DM // NCHIP
SHARD_OUT = DM // NCHIP
GRID_W = 64
QROWS = 4
KROWS = 12
QBLK, KBLK = QROWS * GRID_W, KROWS * GRID_W
NQBLK = SEQ // QBLK
HEADS, HDIM, NPAIR = 8, 64, 4
EPS = 1e-6
NEG_INF = -1e30
ZQ, ZK, ZV, ZG = 12, 16, 20, 24
LR, B1, B2, ADAM_EPS, WD, STEP = 0.001, 0.9, 0.999, 1e-08, 0.01, 10
VMEM_BIG = 56 * 1024 * 1024
MESH_ID = pl.DeviceIdType.MESH


def _dot(a, b, lhs_c, rhs_c):
    return lax.dot_general(a.astype(BF16), b.astype(BF16), (((lhs_c,), (rhs_c,)), ((), ())),
                           preferred_element_type=F32)


@jax.custom_vjp
def mm(a, b):
    return _dot(a, b, 1, 0)


@jax.custom_vjp
def mm_nt(a, b):
    return _dot(a, b, 1, 1)


@jax.custom_vjp
def mm_tn(a, b):
    return _dot(a, b, 0, 0)


mm.defvjp(lambda a, b: (mm(a, b), (a, b)), lambda r, ct: (mm_nt(ct, r[1]), mm_tn(r[0], ct)))
mm_nt.defvjp(lambda a, b: (mm_nt(a, b), (a, b)), lambda r, ct: (mm(ct, r[1]), mm_tn(ct, r[0])))
mm_tn.defvjp(lambda a, b: (mm_tn(a, b), (a, b)), lambda r, ct: (mm_nt(r[1], ct), mm(r[0], ct)))


def _rms(x, g):
    return x * lax.rsqrt(jnp.mean(x * x, axis=-1, keepdims=True) + EPS) * g


def _modulated(x, g, scale, shift):
    return _rms(x, g) * (1.0 + scale) + shift


def _pair_rms(x, g2):
    lo = lax.broadcasted_iota(jnp.int32, (1, 2 * HDIM), 1) < HDIM
    sq = x * x
    s_lo = jnp.sum(jnp.where(lo, sq, 0.0), axis=-1, keepdims=True)
    s_hi = jnp.sum(jnp.where(lo, 0.0, sq), axis=-1, keepdims=True)
    rs = jnp.where(lo, lax.rsqrt(s_lo / HDIM + EPS), lax.rsqrt(s_hi / HDIM + EPS))
    return x * rs * g2


def _cparams(sem, vmem=None):
    return pltpu.CompilerParams(dimension_semantics=sem, vmem_limit_bytes=vmem)


def _row(n):
    return pl.BlockSpec((1, n), lambda *_: (0, 0))


CS_ROWS = 8 * NDEV + 8


def _mod_part(mod_ref, row, part):
    pieces = []
    for j in range(NCHIP):
        lo, hi = max(part * DM, j * SHARD_ADA), min((part + 1) * DM, (j + 1) * SHARD_ADA)
        if lo < hi:
            pieces.append(mod_ref[j, row, lo - j * SHARD_ADA:hi - j * SHARD_ADA])
    return jnp.concatenate(pieces, axis=-1)


def inproj_fwd(chip, x, c_vec, c_ctx, w_ada, b_shard, norm_g, w_shard, wo_shard):
    tl = 1024
    nt = SEQ // tl
    halves = (DM // 2, SHARD_OUT // 2)
    n_w, n_c = 12, NDEV - 1

    def kern(k_ref, x_ref, cv_ref, cc_ref, wa_ref, b_ref, g_ref, w_ref, wo_ref,
             z_ref, h_ref, wfull_ref, wofull_ref, modall_ref, csall_ref,
             w_scr, wo_scr, h_scr, mine, cs_scr, mod_scr, shsc_scr, send_sems, recv_sems):
        s, t = pl.program_id(0), pl.program_id(1)
        xi, yi, c = _me()
        k, me = 2 * xi + yi, 4 * xi + 2 * yi + c
        sib = _flip(1)
        rows = pl.ds(pl.multiple_of(t * tl, tl), tl)
        gathered = (w_scr, wo_scr)
        slot = lambda d: pl.ds(pl.multiple_of(8 * d, 8), 8)

        def c_copy(q, owner):
            return _rcopy(mine, cs_scr.at[slot(owner), :], send_sems, recv_sems, n_w + q - 1, _flip(q))

        def m_copy(q, chip_of_block):
            return _rcopy(mod_scr.at[chip_of_block], mod_scr.at[chip_of_block], send_sems, recv_sems,
                          n_w + n_c + q // 2 - 1, _flip(q))

        def adaln():
            first = lax.broadcasted_iota(jnp.int32, (8, DM), 0) == 0
            mine[...] = jnp.where(first, jnp.broadcast_to(cv_ref[...], (8, DM)), 0.0)
            cs_scr[slot(me), :] = mine[...]
            cs_scr[slot(NDEV), :] = jnp.where(first, jnp.broadcast_to(cc_ref[...], (8, DM)), 0.0)
            for q in range(1, NDEV):
                c_copy(q, me).start()
            wa = wa_ref[...].astype(BF16)
            for q in range(1, NDEV):
                px, py, pc = _flip(q)
                c_copy(q, 4 * px + 2 * py + pc).wait_recv()
            act = jax.nn.silu(cs_scr[...]).astype(BF16)
            mod_scr[k] = jnp.dot(act, wa, preferred_element_type=F32) + b_ref[...]
            for q in (2, 4, 6):
                m_copy(q, k).start()
            for q in (2, 4, 6):
                m_copy(q, _chip_of(_flip(q))).wait_recv()
            row = pl.ds(8 * me, 1)
            shsc_scr[0:1, :] = _mod_part(mod_scr, row, 0)
            shsc_scr[1:2, :] = _mod_part(mod_scr, row, 1)
            pltpu.sync_copy(mod_scr, modall_ref)
            pltpu.sync_copy(cs_scr, csall_ref)

        def block(n, chip_of_block, hh):
            return gathered[n].at[chip_of_block, pl.ds(pl.multiple_of(hh * halves[n], halves[n]), halves[n]), :]

        def ici(n, q, chip_of_block):
            blk = block(n, chip_of_block, c)
            return _rcopy(blk, blk, send_sems, recv_sems, 6 * n + q // 2 - 1, _flip(q))

        def d2d(n, q, chip_of_block, hh):
            blk = block(n, chip_of_block, hh)
            return _rcopy(blk, blk, send_sems, recv_sems, 6 * n + 3 + q // 2 - 1, sib)

        @pl.when((s == 0) & (t == 0))
        def _():
            adaln()
            w_scr[k] = w_ref[...].astype(BF16)
            wo_scr[k] = wo_ref[...].astype(BF16)
            for q in (2, 4, 6):
                ici(0, q, k).start()
                ici(1, q, k).start()

        for sweep in (1, 2, 3):
            @pl.when((s == sweep) & (t == 0))
            def _():
                q = 2 * sweep
                src = _chip_of(_flip(q))
                for n in (0, 1):
                    ici(n, q, src).wait_recv()
                    d2d(n, q, src, c).start()
                for n in (0, 1):
                    d2d(n, q, src, 1 - c).wait_recv()

        @pl.when(s == 0)
        def _():
            hb = _modulated(x_ref[...], g_ref[...], shsc_scr[1:2, :], shsc_scr[0:1, :]).astype(BF16)
            h_scr[rows, :] = hb
            h_ref[...] = hb

        z_ref[...] = jnp.dot(h_scr[rows, :], w_scr[lax.bitwise_xor(k, s)], preferred_element_type=F32)

        @pl.when((s == NCHIP - 1) & (t == nt - 1))
        def _():
            for q in range(1, NDEV):
                c_copy(q, me).wait_send()
            for q in (2, 4, 6):
                m_copy(q, k).wait_send()
            for n in (0, 1):
                for q in (2, 4, 6):
                    ici(n, q, k).wait_send()
                    d2d(n, q, _chip_of(_flip(q)), c).wait_send()
            pltpu.sync_copy(w_scr, wfull_ref)
            pltpu.sync_copy(wo_scr, wofull_ref)

    once = lambda s, t, k: (jnp.where(s == 0, t, nt - 1), 0)
    hbm = pl.BlockSpec(memory_space=pl.ANY)
    n_sem = n_w + n_c + 3
    return pl.pallas_call(
        kern, name="inproj_fwd",
        grid_spec=pltpu.PrefetchScalarGridSpec(
            num_scalar_prefetch=1, grid=(NCHIP, nt),
            in_specs=[pl.BlockSpec((tl, DM), once)] + [_VMEM_SPEC] * 7,
            out_specs=[pl.BlockSpec((tl, SHARD_IN), lambda s, t, k: (t, lax.bitwise_xor(k[0], s))),
                       pl.BlockSpec((tl, DM), once), hbm, hbm, hbm, hbm],
            scratch_shapes=[pltpu.VMEM((NCHIP, DM, SHARD_IN), BF16), pltpu.VMEM((NCHIP, SHARD_OUT, DM), BF16),
                            pltpu.VMEM((SEQ, DM), BF16), pltpu.VMEM((8, DM), F32), pltpu.VMEM((CS_ROWS, DM), F32),
                            pltpu.VMEM((NCHIP, CS_ROWS, SHARD_ADA), F32), pltpu.VMEM((8, DM), F32),
                            pltpu.SemaphoreType.DMA((n_sem,)), pltpu.SemaphoreType.DMA((n_sem,))]),
        out_shape=[jax.ShapeDtypeStruct((SEQ, DIN), F32), jax.ShapeDtypeStruct((SEQ, DM), BF16),
                   jax.ShapeDtypeStruct((NCHIP, DM, SHARD_IN), BF16), jax.ShapeDtypeStruct((NCHIP, SHARD_OUT, DM), BF16),
                   jax.ShapeDtypeStruct((NCHIP, CS_ROWS, SHARD_ADA), F32), jax.ShapeDtypeStruct((CS_ROWS, DM), F32)],
        compiler_params=_cparams(("arbitrary", "arbitrary"), VMEM_BIG),
    )(chip, x, c_vec, c_ctx, w_ada, b_shard, norm_g, w_shard, wo_shard)


def ctx_fwd(ctx, cshift, cscale, norm_g, w_full):
    def kern(c_ref, sh_ref, sc_ref, g_ref, w2_ref, w3_ref, zc_ref, hc_ref):
        hc = _modulated(c_ref[...], g_ref[...], sc_ref[...], sh_ref[...]).astype(BF16)
        hc_ref[...] = hc
        zc_ref[:, :SHARD_IN] = jnp.dot(hc, w2_ref[0], preferred_element_type=F32)
        zc_ref[:, SHARD_IN:] = jnp.dot(hc, w3_ref[0], preferred_element_type=F32)

    return pl.pallas_call(
        kern, name="ctx_fwd", grid=(1,),
        in_specs=[pl.BlockSpec((CTX, DM), lambda i: (0, 0)), _row(DM), _row(DM), _row(DM),
                  pl.BlockSpec((1, DM, SHARD_IN), lambda i: (2, 0, 0)),
                  pl.BlockSpec((1, DM, SHARD_IN), lambda i: (3, 0, 0))],
        out_specs=[pl.BlockSpec((CTX, 2 * SHARD_IN), lambda i: (0, 0)),
                   pl.BlockSpec((CTX, DM), lambda i: (0, 0))],
        out_shape=[jax.ShapeDtypeStruct((CTX, 2 * SHARD_IN), F32), jax.ShapeDtypeStruct((CTX, DM), BF16)],
        compiler_params=_cparams(("arbitrary",)),
    )(ctx, cshift, cscale, norm_g, w_full, w_full)


SGU_CHUNK, SGU_PER_STEP = 128, 4


def _gelu(x):
    return 0.5 * x * (1.0 + lax.erf(x * 0.7071067811865476))


def _sgu_chunk(au, av, ag, sg, ws, bsb):
    u, v = _gelu(au), _gelu(av)
    outs = []
    for g in range(4):
        sl = slice(128 * g, 128 * (g + 1))
        mixed = mm(ws[g], _rms(v[:, sl], sg[:, sl])) + bsb[g]
        outs.append(u[:, sl] * mixed * jax.nn.silu(ag[:, sl]))
    return jnp.concatenate(outs, axis=-1)


def _sgu_specs():
    rows = SGU_CHUNK * SGU_PER_STEP
    zspec = lambda c: pl.BlockSpec((rows, 512), lambda n: (n, c))
    wspec = pl.BlockSpec((4, 128, 128), lambda n: (0, 0, 0))
    return rows, [zspec(0), zspec(1), zspec(2), _row(512), wspec, wspec]


def sgu_fwd(z, sg, ws, bsb):
    rows, in_specs = _sgu_specs()

    def kern(au_ref, av_ref, ag_ref, sg_ref, ws_ref, bs_ref, o_ref):
        for c in range(SGU_PER_STEP):
            sl = slice(c * SGU_CHUNK, (c + 1) * SGU_CHUNK)
            o_ref[sl, :] = _sgu_chunk(au_ref[sl, :], av_ref[sl, :], ag_ref[sl, :], sg_ref[...], ws_ref[...],
                                      bs_ref[...])

    return pl.pallas_call(
        kern, name="sgu_fwd", grid=(SEQ // rows,), in_specs=in_specs,
        out_specs=pl.BlockSpec((rows, 512), lambda n: (n, 0)),
        out_shape=jax.ShapeDtypeStruct((SEQ, 512), F32),
        compiler_params=_cparams(("arbitrary",)),
    )(z, z, z, sg, ws, bsb)


def sgu_bwd(z, sg, ws, bsb, dcat):
    rows, in_specs = _sgu_specs()

    def kern(au_ref, av_ref, ag_ref, sg_ref, ws_ref, bs_ref, do_ref, dz_ref, dsg_ref, dws_ref, dbs_ref):
        @pl.when(pl.program_id(0) == 0)
        def _():
            dsg_ref[...] = jnp.zeros_like(dsg_ref)
            dws_ref[...] = jnp.zeros_like(dws_ref)
            dbs_ref[...] = jnp.zeros_like(dbs_ref)

        for c in range(SGU_PER_STEP):
            sl = slice(c * SGU_CHUNK, (c + 1) * SGU_CHUNK)
            _, vjp = jax.vjp(_sgu_chunk, au_ref[sl, :], av_ref[sl, :], ag_ref[sl, :], sg_ref[...], ws_ref[...],
                             bs_ref[...])
            dau, dav, dag, dsg, dws, dbs = vjp(do_ref[sl, :])
            dz_ref[sl, 0:512] = dau.astype(BF16)
            dz_ref[sl, 512:1024] = dav.astype(BF16)
            dz_ref[sl, 1024:1536] = dag.astype(BF16)
            dsg_ref[...] += dsg
            dws_ref[...] += dws
            dbs_ref[...] += dbs

        @pl.when(pl.program_id(0) == pl.num_programs(0) - 1)
        def _():
            dbs_ref[...] = jnp.broadcast_to(jnp.sum(dbs_ref[...], axis=-1, keepdims=True), dbs_ref.shape)

    wspec = pl.BlockSpec((4, 128, 128), lambda n: (0, 0, 0))
    return pl.pallas_call(
        kern, name="sgu_bwd", grid=(SEQ // rows,),
        in_specs=in_specs + [pl.BlockSpec((rows, 512), lambda n: (n, 0))],
        out_specs=[pl.BlockSpec((rows, 1536), lambda n: (n, 0)), _row(512), wspec, wspec],
        out_shape=[jax.ShapeDtypeStruct((SEQ, 1536), BF16), jax.ShapeDtypeStruct((1, 512), F32),
                   jax.ShapeDtypeStruct((4, 128, 128), F32), jax.ShapeDtypeStruct((4, 128, 128), F32)],
        compiler_params=_cparams(("arbitrary",)),
    )(z, z, z, sg, ws, bsb, dcat)


_DR_OFF = (7, 3, -1)


def _row_valid(v, rr, j):
    return (j < 8, rr <= j < rr + 8, 4 <= j < 12)[v]


def _col_window():
    q = lax.broadcasted_iota(jnp.int32, (GRID_W, 128), 0)
    kc = lax.broadcasted_iota(jnp.int32, (GRID_W, 128), 1) % GRID_W
    c0 = jnp.clip(q - 8, 0, GRID_W - 16)
    return (kc >= c0) & (kc < c0 + 16)


def rpb_tables(rpb2):
    def kern(r_ref, b_ref):
        base = r_ref[0]
        lo = lax.broadcasted_iota(jnp.int32, (1, 128), 1) < GRID_W
        win = _col_window()
        tiles = {}
        for v in range(3):
            for rr in range(QROWS):
                for jp in range(KROWS // 2):
                    j0, j1 = 2 * jp, 2 * jp + 1
                    ok0, ok1 = _row_valid(v, rr, j0), _row_valid(v, rr, j1)
                    key = (j0 - rr + _DR_OFF[v], ok0, ok1) if (ok0 or ok1) else None
                    if key not in tiles:
                        if key is None:
                            tiles[key] = jnp.full((GRID_W, 128), NEG_INF, F32)
                        else:
                            d0 = key[0]
                            r0 = base[d0:d0 + 1, :] if ok0 else jnp.zeros((1, 128), F32)
                            r1 = base[d0 + 1:d0 + 2, :] if ok1 else jnp.zeros((1, 128), F32)
                            y = jnp.broadcast_to(jnp.where(lo, r0, r1), (GRID_W, 128))
                            y = pltpu.roll(pltpu.roll(y, 128 - 15, 1), 0, 1, stride=1, stride_axis=0)
                            tiles[key] = jnp.where(win & jnp.where(lo, ok0, ok1), y, NEG_INF)
                    b_ref[v, 0, rr * GRID_W:(rr + 1) * GRID_W, jp * 128:(jp + 1) * 128] = tiles[key]

    return pl.pallas_call(
        kern, name="rpb_tables", grid=(HEADS,),
        in_specs=[pl.BlockSpec((1, 15, 128), lambda h: (h, 0, 0))],
        out_specs=pl.BlockSpec((3, 1, QBLK, KBLK), lambda h: (0, h, 0, 0)),
        out_shape=jax.ShapeDtypeStruct((3, HEADS, QBLK, KBLK), F32),
        compiler_params=_cparams(("arbitrary",)),
    )(rpb2)


def rpb_bwd(dbias):
    def kern(g0_ref, g1_ref, g2_ref, o_ref):
        g_refs = (g0_ref, g1_ref, g2_ref)
        lo = lax.broadcasted_iota(jnp.int32, (1, 128), 1) < GRID_W
        ri = lax.broadcasted_iota(jnp.int32, (GRID_W, GRID_W), 0)
        ci = lax.broadcasted_iota(jnp.int32, (GRID_W, GRID_W), 1)
        flip = (ri + ci == GRID_W - 1).astype(F32)
        groups = {}
        for v in range(3):
            for rr in range(QROWS):
                for jp in range(KROWS // 2):
                    j0, j1 = 2 * jp, 2 * jp + 1
                    ok0, ok1 = _row_valid(v, rr, j0), _row_valid(v, rr, j1)
                    if not (ok0 or ok1):
                        continue
                    g = g_refs[v][0, rr * GRID_W:(rr + 1) * GRID_W, jp * 128:(jp + 1) * 128]
                    key = (j0 - rr + _DR_OFF[v], ok0, ok1)
                    groups[key] = g if key not in groups else groups[key] + g
        acc = [jnp.zeros((1, 128), F32) for _ in range(15)]
        for (d0, ok0, ok1), g in groups.items():
            g = lax.dot_general(flip, g, (((1,), (0,)), ((), ())), precision=lax.Precision.HIGHEST,
                                preferred_element_type=F32)
            g = pltpu.roll(pltpu.roll(g, 128 - 48, 1), 0, 1, stride=1, stride_axis=0)
            s = jnp.sum(g, axis=0, keepdims=True)
            if ok0:
                acc[d0] = acc[d0] + jnp.where(lo, s, 0.0)
            if ok1:
                acc[d0 + 1] = acc[d0 + 1] + jnp.where(lo, 0.0, s)
        for d in range(15):
            o_ref[0, d:d + 1, :] = acc[d] + pltpu.roll(acc[d], GRID_W, 1)

    return pl.pallas_call(
        kern, name="rpb_bwd", grid=(HEADS,),
        in_specs=[pl.BlockSpec((1, QBLK, KBLK), lambda h: (h, 0, 0))] * 3,
        out_specs=pl.BlockSpec((1, 15, 128), lambda h: (h, 0, 0)),
        out_shape=jax.ShapeDtypeStruct((HEADS, 15, 128), F32),
        compiler_params=_cparams(("arbitrary",)),
    )(*dbias)


def _scaled_q(q_raw, qg):
    return _pair_rms(q_raw, qg) * (HDIM ** -0.5)


def _head_lanes():
    lo = lax.broadcasted_iota(jnp.int32, (1, 2 * HDIM), 1) < HDIM
    return lo, jnp.logical_not(lo)


def _attn_step(q_raw, kn, v, ckn, cv, bias2, qg):
    qn = _scaled_q(q_raw, qg)
    out = lse = None
    for a, mine in enumerate(_head_lanes()):
        qa = jnp.where(mine, qn, 0.0)
        s_lat = mm_nt(qa, kn) + bias2[a]
        s_ctx = mm_nt(qa, ckn)
        m = jnp.maximum(jnp.max(s_lat, axis=-1, keepdims=True), jnp.max(s_ctx, axis=-1, keepdims=True))
        p_lat = jnp.exp(s_lat - m)
        p_ctx = jnp.exp(s_ctx - m)
        den = jnp.sum(p_lat, axis=-1, keepdims=True) + jnp.sum(p_ctx, axis=-1, keepdims=True)
        o = jnp.where(mine, (mm(p_lat, v) + mm(p_ctx, cv)) / den, 0.0)
        l = jnp.where(mine, m + jnp.log(den), 0.0)
        out, lse = (o, l) if out is None else (out + o, lse + l)
    return out, lse


def _attn_step_bwd(q_raw, kn, v, ckn, cv, bias2, qg, bg, o, lse, dout):
    sig = jax.nn.sigmoid(bg)
    do = dout * (bg * sig)
    dbg = dout * o * (sig * (1.0 + bg * (1.0 - sig)))
    qn, qn_vjp = jax.vjp(_scaled_q, q_raw, qg)
    row_dot = do * o
    dqn = dkn = dv = dckn = dcv = None
    dbias = []
    for mine in _head_lanes():
        qa = jnp.where(mine, qn, 0.0)
        doa = jnp.where(mine, do, 0.0)
        l = jnp.max(jnp.where(mine, lse, NEG_INF), axis=-1, keepdims=True)
        delta = jnp.sum(jnp.where(mine, row_dot, 0.0), axis=-1, keepdims=True)
        p_lat = jnp.exp(mm_nt(qa, kn) + bias2[len(dbias)] - l)
        p_ctx = jnp.exp(mm_nt(qa, ckn) - l)
        ds_lat = p_lat * (mm_nt(doa, v) - delta)
        ds_ctx = p_ctx * (mm_nt(doa, cv) - delta)
        parts = (jnp.where(mine, mm(ds_lat, kn) + mm(ds_ctx, ckn), 0.0), mm_tn(ds_lat, qa), mm_tn(p_lat, doa),
                 mm_tn(ds_ctx, qa), mm_tn(p_ctx, doa))
        if dqn is None:
            dqn, dkn, dv, dckn, dcv = parts
        else:
            dqn, dkn, dv, dckn, dcv = (acc + new for acc, new in zip((dqn, dkn, dv, dckn, dcv), parts))
        dbias.append(ds_lat)
    dq, dqg = qn_vjp(dqn)
    return dq, dkn, dv, dckn, dcv, dbias, dqg, dbg


def _kstart(i):
    return pl.multiple_of(jnp.clip((i - 1) * QBLK, 0, SEQ - KBLK), QBLK)


ATTN_STEPS = NQBLK // 2
ATTN_ROWS = 2 * QBLK


def _attn_in_specs():
    bias_spec = lambda variant: pl.BlockSpec((1, 2, QBLK, KBLK), lambda p, i: (variant(i), p, 0, 0))
    return [
        pl.BlockSpec((ATTN_ROWS, 128), lambda p, i: (i, ZQ + p)),
        pl.BlockSpec((SEQ, 128), lambda p, i: (0, ZK + p)),
        pl.BlockSpec((SEQ, 128), lambda p, i: (0, ZV + p)),
        pl.BlockSpec((ATTN_ROWS, 128), lambda p, i: (i, ZG + p)),
        pl.BlockSpec((CTX, 128), lambda p, i: (0, 2 + p)),
        pl.BlockSpec((CTX, 128), lambda p, i: (0, 6 + p)),
        bias_spec(lambda i: jnp.where(i == 0, 0, 1)),
        bias_spec(lambda i: jnp.where(i == ATTN_STEPS - 1, 2, 1)),
        _row(128), _row(128),
    ]


NORM_ROWS = 512


def _norm_keys(k_ref, ck_ref, kg_ref, kn_scr, ckn_scr):
    def body(c, carry):
        sl = pl.ds(pl.multiple_of(c * NORM_ROWS, NORM_ROWS), NORM_ROWS)
        kn_scr[sl, :] = _pair_rms(k_ref[sl, :], kg_ref[...])
        return carry

    lax.fori_loop(0, SEQ // NORM_ROWS, body, 0)
    ckn_scr[...] = _pair_rms(ck_ref[...], kg_ref[...])


def attn_fwd(z, zc, bias, qg2, kg2):
    def kern(q_ref, k_ref, v_ref, bg_ref, ck_ref, cv_ref, be_ref, bo_ref, qg_ref, kg_ref, ob_ref, o_ref, lse_ref,
             kn_scr, ckn_scr):
        i = pl.program_id(1)

        @pl.when(i == 0)
        def _():
            _norm_keys(k_ref, ck_ref, kg_ref, kn_scr, ckn_scr)

        for b, b_ref in enumerate((be_ref, bo_ref)):
            rows = slice(b * QBLK, (b + 1) * QBLK)
            ks = pl.ds(_kstart(2 * i + b), KBLK)
            o, lse = _attn_step(q_ref[rows, :], kn_scr[ks, :], v_ref[ks, :], ckn_scr[...], cv_ref[...], b_ref[0],
                                qg_ref[...])
            ob_ref[rows, :] = o * jax.nn.silu(bg_ref[rows, :])
            o_ref[rows, :] = o
            lse_ref[rows, :] = lse

    qblk = pl.BlockSpec((ATTN_ROWS, 128), lambda p, i: (i, p))
    return pl.pallas_call(
        kern, name="attn_fwd", grid=(NPAIR, ATTN_STEPS), in_specs=_attn_in_specs(), out_specs=[qblk] * 3,
        out_shape=[jax.ShapeDtypeStruct((SEQ, 512), F32)] * 3,
        scratch_shapes=[pltpu.VMEM((SEQ, 128), F32), pltpu.VMEM((CTX, 128), F32)],
        compiler_params=_cparams(("arbitrary", "arbitrary"), 40 * 1024 * 1024),
    )(z, z, z, z, zc, zc, bias, bias, qg2, kg2)


def attn_bwd(z, zc, bias, qg2, kg2, dcat, o_raw, lse):
    def kern(q_ref, k_ref, v_ref, bg_ref, ck_ref, cv_ref, be_ref, bo_ref, qg_ref, kg_ref, do_ref, o_ref, lse_ref,
             dq_ref, dk_ref, dv_ref, dbg_ref, dck_ref, dcv_ref, db0_ref, db1_ref, db2_ref, dqg_ref, dkg_ref,
             kn_scr, ckn_scr, dkn_scr, dckn_scr, dv_scr):
        p, i = pl.program_id(0), pl.program_id(1)
        last = i == ATTN_STEPS - 1

        @pl.when(i == 0)
        def _():
            _norm_keys(k_ref, ck_ref, kg_ref, kn_scr, ckn_scr)
            dkn_scr[...] = jnp.zeros_like(dkn_scr)
            dv_scr[...] = jnp.zeros_like(dv_scr)
            dckn_scr[...] = jnp.zeros_like(dckn_scr)
            dcv_ref[...] = jnp.zeros_like(dcv_ref)

        @pl.when((i == 0) & (p == 0))
        def _():
            dqg_ref[...] = jnp.zeros_like(dqg_ref)
            dkg_ref[...] = jnp.zeros_like(dkg_ref)

        db = []
        for b, b_ref in enumerate((be_ref, bo_ref)):
            rows = slice(b * QBLK, (b + 1) * QBLK)
            ks = pl.ds(_kstart(2 * i + b), KBLK)
            dq, dkn, dv, dckn, dcv, dbb, dqg, dbg = _attn_step_bwd(
                q_ref[rows, :], kn_scr[ks, :], v_ref[ks, :], ckn_scr[...], cv_ref[...], b_ref[0], qg_ref[...],
                bg_ref[rows, :], o_ref[rows, :], lse_ref[rows, :], do_ref[rows, :])
            dq_ref[rows, :] = dq.astype(BF16)
            dbg_ref[rows, :] = dbg.astype(BF16)
            dkn_scr[ks, :] += dkn
            dv_scr[ks, :] += dv
            dckn_scr[...] += dckn
            dcv_ref[...] += dcv
            dqg_ref[...] += dqg
            db.append(dbb)

        @pl.when(i == 0)
        def _():
            for a in range(2):
                db0_ref[a] = db[0][a]
                db1_ref[a] = db[1][a]

        @pl.when((i > 0) & jnp.logical_not(last))
        def _():
            for a in range(2):
                db1_ref[a] += db[0][a] + db[1][a]

        @pl.when(last)
        def _():
            for a in range(2):
                db1_ref[a] += db[0][a]
                db2_ref[a] = db[1][a]

        @pl.when(last)
        def _():
            def body(c, dkg):
                sl = pl.ds(pl.multiple_of(c * NORM_ROWS, NORM_ROWS), NORM_ROWS)
                _, nvjp = jax.vjp(_pair_rms, k_ref[sl, :], kg_ref[...])
                dk, dg = nvjp(dkn_scr[sl, :])
                dk_ref[sl, :] = dk.astype(BF16)
                dv_ref[sl, :] = dv_scr[sl, :].astype(BF16)
                return dkg + dg

            dkg = lax.fori_loop(0, SEQ // NORM_ROWS, body, jnp.zeros((1, 128), F32))
            _, nvjp = jax.vjp(_pair_rms, ck_ref[...], kg_ref[...])
            dck, dg = nvjp(dckn_scr[...])
            dck_ref[...] = dck
            dkg_ref[...] += dkg + dg

        @pl.when(last & (p == NPAIR - 1))
        def _():
            dqg_ref[...] = dqg_ref[...] + pltpu.roll(dqg_ref[...], HDIM, 1)
            dkg_ref[...] = dkg_ref[...] + pltpu.roll(dkg_ref[...], HDIM, 1)

    blk = lambda rows: pl.BlockSpec((rows, 128), lambda p, i: (0, p))
    qblk = pl.BlockSpec((ATTN_ROWS, 128), lambda p, i: (i, p))
    dbias = pl.BlockSpec((2, QBLK, KBLK), lambda p, i: (p, 0, 0))
    return pl.pallas_call(
        kern, name="attn_bwd", grid=(NPAIR, ATTN_STEPS),
        in_specs=_attn_in_specs() + [pl.BlockSpec((ATTN_ROWS, 128), lambda p, i: (i, 4 + p)), qblk, qblk],
        out_specs=[qblk, blk(SEQ), blk(SEQ), qblk, blk(CTX), blk(CTX), dbias, dbias, dbias, _row(128), _row(128)],
        out_shape=[jax.ShapeDtypeStruct((SEQ, 512), BF16)] * 4 + [jax.ShapeDtypeStruct((CTX, 512), F32)] * 2
        + [jax.ShapeDtypeStruct((HEADS, QBLK, KBLK), F32)] * 3
        + [jax.ShapeDtypeStruct((1, 128), F32), jax.ShapeDtypeStruct((1, 128), F32)],
        scratch_shapes=[pltpu.VMEM((SEQ, 128), F32), pltpu.VMEM((CTX, 128), F32),
                        pltpu.VMEM((SEQ, 128), F32), pltpu.VMEM((CTX, 128), F32), pltpu.VMEM((SEQ, 128), F32)],
        compiler_params=_cparams(("arbitrary", "arbitrary"), VMEM_BIG),
    )(z, z, z, z, zc, zc, bias, bias, qg2, kg2, dcat, o_raw, lse)


def outproj(out_a, out_b, x, target, gate, wo):
    tl = 512

    def kern(a_ref, b_ref, x_ref, t_ref, g_ref, w_ref, loss_ref, dy_ref, dcat_ref, dg_ref, dw_ref):
        @pl.when(pl.program_id(0) == 0)
        def _():
            loss_ref[...] = jnp.zeros_like(loss_ref)
            dg_ref[...] = jnp.zeros_like(dg_ref)
            dw_ref[...] = jnp.zeros_like(dw_ref)

        a, b = a_ref[...].astype(BF16), b_ref[...].astype(BF16)
        mix = (jnp.dot(a, w_ref[0:512, :], preferred_element_type=F32)
               + jnp.dot(b, w_ref[512:1024, :], preferred_element_type=F32))
        err = x_ref[...] + g_ref[...] * mix - t_ref[...]
        loss_ref[...] += 0.5 * jnp.sum(jnp.mean(err * err, axis=-1))
        dy = err * (1.0 / DM)
        dy_ref[...] = dy
        dg_ref[...] += jnp.sum(dy * mix, axis=0, keepdims=True)
        dmix = (g_ref[...] * dy).astype(BF16)
        dcat_ref[...] = lax.dot_general(dmix, w_ref[...], (((1,), (1,)), ((), ())), preferred_element_type=F32)
        dw_ref[0:512, :] += lax.dot_general(a, dmix, (((0,), (0,)), ((), ())), preferred_element_type=F32)
        dw_ref[512:1024, :] += lax.dot_general(b, dmix, (((0,), (0,)), ((), ())), preferred_element_type=F32)

    tile = lambda w: pl.BlockSpec((tl, w), lambda t: (t, 0))
    whole = pl.BlockSpec((DM, DM), lambda t: (0, 0))
    return pl.pallas_call(
        kern, name="outproj", grid=(SEQ // tl,),
        in_specs=[tile(512), tile(512), tile(DM), tile(DM), _row(DM), whole],
        out_specs=[pl.BlockSpec((8, 128), lambda t: (0, 0)), tile(DM), tile(DM), _row(DM), whole],
        out_shape=[jax.ShapeDtypeStruct((8, 128), F32), jax.ShapeDtypeStruct((SEQ, DM), F32),
                   jax.ShapeDtypeStruct((SEQ, DM), F32), jax.ShapeDtypeStruct((1, DM), F32),
                   jax.ShapeDtypeStruct((DM, DM), F32)],
        compiler_params=_cparams(("arbitrary",), 48 * 1024 * 1024),
    )(out_a, out_b, x, target, gate, wo)


def _pieces(sources):
    out = []
    for name, c0, c1 in sources:
        for j in range(NCHIP):
            lo, hi = max(c0, j * SHARD_IN), min(c1, (j + 1) * SHARD_IN)
            if lo < hi:
                out.append((j, lo - j * SHARD_IN, hi - j * SHARD_IN, name, lo - c0, hi - c0))
    return out


DZ_PIECES = _pieces((("a", 0, 1536), ("q", 1536, 2048), ("k", 2048, 2560), ("v", 2560, 3072), ("g", 3072, DIN)))
DZC_PIECES = _pieces((("k", 2048, 2560), ("v", 2560, 3072)))
_NT = (((1,), (1,)), ((), ()))


DH_SUBTILES = 2


def _dz_specs(tl):
    return [pl.BlockSpec((tl, 1536), lambda t: (t, 0))] + [pl.BlockSpec((tl, 512), lambda t: (t, 0))] * 4


def dh_bwd(dz_parts, w_full, x, dy, shift, scale, norm_g, dg_ctx, wire_i, wire_o):
    tl = 512
    nt = SEQ // tl

    def kern(a_ref, q_ref, k_ref, v_ref, g_ref, w_ref, x_ref, dy_ref, sh_ref, sc_ref, gn_ref, dgc_ref, wi_hbm, wo_hbm,
             gx_ref, dsh_ref, dsc_ref, dg_ref, goti_ref, goto_ref, rcv_i, rcv_o, send_sems, recv_sems):
        def ici(n, q):
            wire, rcv = ((wi_hbm, rcv_i), (wo_hbm, rcv_o))[n]
            return _rcopy(wire.at[_chip_of(_flip(q))], rcv.at[q // 2 - 1], send_sems, recv_sems, 3 * n + q // 2 - 1,
                          _flip(q))

        @pl.when(pl.program_id(0) == 0)
        def _():
            for n in (0, 1):
                for q in (2, 4, 6):
                    ici(n, q).start()

        @pl.when(pl.program_id(0) == 0)
        def _():
            dsh_ref[...] = jnp.zeros_like(dsh_ref)
            dsc_ref[...] = jnp.zeros_like(dsc_ref)
            dg_ref[...] = dgc_ref[...]

        src = dict(a=a_ref, q=q_ref, k=k_ref, v=v_ref, g=g_ref)
        for sub in range(DH_SUBTILES):
            rows = slice(sub * tl // DH_SUBTILES, (sub + 1) * tl // DH_SUBTILES)
            dh = None
            for j, l0, l1, name, s0, s1 in DZ_PIECES:
                part = lax.dot_general(src[name][rows, s0:s1], w_ref[j, :, l0:l1], _NT, preferred_element_type=F32)
                dh = part if dh is None else dh + part
            _, vjp = jax.vjp(_modulated, x_ref[rows, :], gn_ref[...], sc_ref[...], sh_ref[...])
            dx, dg, dsc, dsh = vjp(dh)
            gx_ref[rows, :] = dy_ref[rows, :] + dx
            dg_ref[...] += dg
            dsc_ref[...] += dsc
            dsh_ref[...] += dsh

        @pl.when(pl.program_id(0) == nt - 1)
        def _():
            for n in (0, 1):
                for q in (2, 4, 6):
                    ici(n, q).wait_recv()
                    ici(n, q).wait_send()
            goti_ref[...] = rcv_i[...]
            goto_ref[...] = rcv_o[...]

    tile = pl.BlockSpec((tl, DM), lambda t: (t, 0))
    hbm = pl.BlockSpec(memory_space=pl.ANY)
    got = [(NCHIP - 1, rh, w) for rh, w in RS_SHAPES]
    return pl.pallas_call(
        kern, name="dh_bwd", grid=(nt,),
        in_specs=_dz_specs(tl) + [pl.BlockSpec((NCHIP, DM, SHARD_IN), lambda t: (0, 0, 0)), tile, tile, _row(DM),
                                  _row(DM), _row(DM), _row(DM), hbm, hbm],
        out_specs=[tile, _row(DM), _row(DM), _row(DM)] + [pl.BlockSpec(s, lambda t: (0, 0, 0)) for s in got],
        out_shape=[jax.ShapeDtypeStruct((SEQ, DM), F32)] + [jax.ShapeDtypeStruct((1, DM), F32)] * 3
        + [jax.ShapeDtypeStruct(s, BF16) for s in got],
        scratch_shapes=[pltpu.VMEM(s, BF16) for s in got] + [pltpu.SemaphoreType.DMA((6,)), pltpu.SemaphoreType.DMA((6,))],
        compiler_params=_cparams(("arbitrary",), VMEM_BIG),
    )(*dz_parts, w_full, x, dy, shift, scale, norm_g, dg_ctx, wire_i, wire_o)


def dw_bwd(h, dz_parts, hc, dck, dcv, g_out):
    tl = 512
    nt = SEQ // tl
    (rhi, wi), (rho, wo) = RS_SHAPES

    def kern(h_ref, a_ref, q_ref, k_ref, v_ref, g_ref, hc_ref, dck_ref, dcv_ref, go_hbm,
             wire_i, keep_i, wire_o, keep_o, acc, rcv_i, mine_o, rcv_o, load_sem, send_sems, recv_sems):
        t = pl.program_id(0)
        x, y, c = _me()
        k = 2 * x + y
        sib = _flip(1)
        half = lambda hh, rh: pl.ds(pl.multiple_of(hh * rh, rh), rh)
        load_o = pltpu.make_async_copy(go_hbm.at[:, half(c, rho), :], mine_o, load_sem)
        pair_o = _rcopy(go_hbm.at[:, half(1 - c, rho), :], rcv_o, send_sems, recv_sems, 0, sib)
        pair_i = _rcopy(acc.at[:, half(1 - c, rhi), :], rcv_i, send_sems, recv_sems, 1, sib)

        @pl.when(t == 0)
        def _():
            load_o.start()
            pair_o.start()
            acc[...] = jnp.zeros_like(acc)
            hct = hc_ref[...].T
            csrc = dict(k=dck_ref, v=dcv_ref)
            for j, l0, l1, name, s0, s1 in DZC_PIECES:
                acc[j, :, l0:l1] += jnp.dot(hct, csrc[name][:, s0:s1].astype(BF16), preferred_element_type=F32)

        ht = h_ref[...].T
        src = dict(a=a_ref, q=q_ref, k=k_ref, v=v_ref, g=g_ref)
        for j, l0, l1, name, s0, s1 in DZ_PIECES:
            acc[j, :, l0:l1] += jnp.dot(ht, src[name][:, s0:s1], preferred_element_type=F32)

        @pl.when(t == nt - 1)
        def _():
            pair_i.start()
            load_o.wait()
            pair_o.wait_recv()
            for j in range(NCHIP):
                wire_o[j] = (mine_o[j] + rcv_o[j]).astype(BF16)
            keep_o[...] = mine_o[k] + rcv_o[k]
            pair_i.wait_recv()
            mine = half(c, rhi)
            for j in range(NCHIP):
                wire_i[j] = (acc[j, mine, :] + rcv_i[j]).astype(BF16)
            keep_i[...] = acc[k, mine, :] + rcv_i[k]
            pair_o.wait_send()
            pair_i.wait_send()

    whole = lambda *shape: pl.BlockSpec(shape, lambda t: (0,) * len(shape))
    return pl.pallas_call(
        kern, name="dw_bwd", grid=(nt,),
        in_specs=[pl.BlockSpec((tl, DM), lambda t: (t, 0))] + _dz_specs(tl)
        + [whole(CTX, DM), whole(CTX, 512), whole(CTX, 512), pl.BlockSpec(memory_space=pl.ANY)],
        out_specs=[whole(NCHIP, rhi, wi), whole(rhi, wi), whole(NCHIP, rho, wo), whole(rho, wo)],
        out_shape=[jax.ShapeDtypeStruct((NCHIP, rhi, wi), BF16), jax.ShapeDtypeStruct((rhi, wi), F32),
                   jax.ShapeDtypeStruct((NCHIP, rho, wo), BF16), jax.ShapeDtypeStruct((rho, wo), F32)],
        scratch_shapes=[pltpu.VMEM((NCHIP, DM, SHARD_IN), F32), pltpu.VMEM((NCHIP, rhi, wi), F32),
                        pltpu.VMEM((NCHIP, rho, wo), F32), pltpu.VMEM((NCHIP, rho, wo), F32),
                        pltpu.SemaphoreType.DMA(()), pltpu.SemaphoreType.DMA((2,)), pltpu.SemaphoreType.DMA((2,))],
        compiler_params=_cparams(("arbitrary",), VMEM_BIG),
    )(h, *dz_parts, hc, dck, dcv, g_out)


def ctx_bwd(dck, dcv, w_full, ctx, cshift, cscale, norm_g):
    def kern(dck_ref, dcv_ref, w_ref, c_ref, sh_ref, sc_ref, g_ref, dsh_ref, dsc_ref, dg_ref):
        csrc = dict(k=dck_ref, v=dcv_ref)
        dhc = None
        for j, l0, l1, name, s0, s1 in DZC_PIECES:
            part = lax.dot_general(csrc[name][:, s0:s1].astype(BF16), w_ref[j, :, l0:l1], _NT,
                                   preferred_element_type=F32)
            dhc = part if dhc is None else dhc + part
        _, vjp = jax.vjp(lambda g, sc, sh: _modulated(c_ref[...], g, sc, sh), g_ref[...], sc_ref[...], sh_ref[...])
        dg_ref[...], dsc_ref[...], dsh_ref[...] = vjp(dhc)

    whole = lambda r, c: pl.BlockSpec((r, c), lambda i: (0, 0))
    return pl.pallas_call(
        kern, name="ctx_bwd", grid=(1,),
        in_specs=[whole(CTX, 512), whole(CTX, 512), pl.BlockSpec((NCHIP, DM, SHARD_IN), lambda i: (0, 0, 0)),
                  whole(CTX, DM), _row(DM), _row(DM), _row(DM)],
        out_specs=[_row(DM), _row(DM), _row(DM)],
        out_shape=[jax.ShapeDtypeStruct((1, DM), F32)] * 3,
        compiler_params=_cparams(("arbitrary",), 40 * 1024 * 1024),
    )(dck, dcv, w_full, ctx, cshift, cscale, norm_g)


def _lane_pad_rpb(rpb):
    r = jnp.pad(rpb, ((0, 0), (0, 0), (0, GRID_W - rpb.shape[-1])))
    return jnp.concatenate([r, r], axis=-1)


def local_step(chip, dev, x, c_vec, c_ctx, w_ada, b_shard, ctx, target, norm_g, sgu_g, w_s, b_s, q_g, k_g, rpb,
               w_in_shard, w_out_shard):
    bsb = jnp.broadcast_to(b_s[:, :, None], (4, 128, 128))
    qg2, kg2 = jnp.tile(q_g, (1, 2)), jnp.tile(k_g, (1, 2))

    z, h, w_in_full, w_out_full, mod_all, cs = inproj_fwd(chip, x, c_vec, c_ctx, w_ada, b_shard, norm_g, w_in_shard,
                                                          w_out_shard)
    mods = mod_all.transpose(1, 0, 2).reshape(CS_ROWS, 3 * DM)
    mod = lax.dynamic_slice(mods, (8 * dev, 0), (1, 3 * DM))
    shift, scale, gate = mod[:, :DM], mod[:, DM:2 * DM], mod[:, 2 * DM:]
    cshift, cscale = mods[8 * NDEV:8 * NDEV + 1, :DM], mods[8 * NDEV:8 * NDEV + 1, DM:2 * DM]
    zc, hc = ctx_fwd(ctx, cshift, cscale, norm_g, w_in_full)
    bias = rpb_tables(_lane_pad_rpb(rpb))
    out_a = sgu_fwd(z, sgu_g, w_s, bsb)
    out_b, o_raw, lse = attn_fwd(z, zc, bias, qg2, kg2)
    loss8, dy, dcat, dgate, dwo = outproj(out_a, out_b, x, target, gate, w_out_full.reshape(DM, DM))
    dz_a, dsg, dws, dbsb = sgu_bwd(z, sgu_g, w_s, bsb, dcat)
    dq, dk, dv, dbg, dck, dcv, db0, db1, db2, dqg2, dkg2 = attn_bwd(z, zc, bias, qg2, kg2, dcat, o_raw, lse)
    drpb = rpb_bwd((db0, db1, db2))[:, :, :rpb.shape[-1]]
    dz_parts = (dz_a, dq, dk, dv, dbg)
    dcshift, dcscale, dng_c = ctx_bwd(dck, dcv, w_in_full, ctx, cshift, cscale, norm_g)
    wire_i, keep_i, wire_o, keep_o = dw_bwd(h, dz_parts, hc, dck, dcv, dwo.reshape(NCHIP, SHARD_OUT, DM))
    grad_x, dshift, dscale, dng, got_i, got_o = dh_bwd(dz_parts, w_in_full, x, dy, shift, scale, norm_g, dng_c,
                                                       wire_i, wire_o)
    return dict(
        loss=loss8[0:1, 0:1], grad_x=grad_x, rs=(keep_i, got_i, keep_o, got_o), cs=cs,
        dmod=jnp.concatenate([dshift, dscale, dgate], axis=-1),
        dcmod=jnp.concatenate([dcshift, dcscale, jnp.zeros((1, DM), F32)], axis=-1),
        d_norm_g=dng, d_sgu_g=dsg, d_w_s=dws, d_b_s=dbsb[:, :, 0],
        d_q_g=dqg2[:, :HDIM], d_k_g=dkg2[:, :HDIM], d_rpb=drpb)


def _me():
    return lax.axis_index("x"), lax.axis_index("y"), lax.axis_index("c")


def _flip(q):
    x, y, c = _me()
    return ((1 - x) if q & 4 else x, (1 - y) if q & 2 else y, (1 - c) if q & 1 else c)


def _chip_of(dev):
    return 2 * dev[0] + dev[1]


def _rcopy(src, dst, send_sems, recv_sems, k, dev):
    return pltpu.make_async_remote_copy(src_ref=src, dst_ref=dst, send_sem=send_sems.at[k], recv_sem=recv_sems.at[k],
                                        device_id=dev, device_id_type=MESH_ID)


_VMEM_SPEC = pl.BlockSpec(memory_space=pltpu.VMEM)
SLAB_ROWS = 80


RS_SHAPES = ((DM // 2, SHARD_IN), (SHARD_OUT // 2, DM))


def final_reduce(keep_i, got_i, keep_o, got_o, slab):
    def kern(ki_ref, gi_ref, ko_ref, go_ref, s_ref, gin_ref, gout_ref, all_ref, tot_ref, send_sems, recv_sems):
        x, y, c = _me()
        sib = _flip(1)
        dev = lambda d: 4 * d[0] + 2 * d[1] + d[2]
        me = dev((x, y, c))

        def slab_copy(idx, owner, to):
            return _rcopy(all_ref.at[dev(owner)], all_ref.at[dev(owner)], send_sems, recv_sems, idx, to)

        all_ref[me] = s_ref[...]
        first = [slab_copy(0, (x, y, c), sib)] + [slab_copy(q // 2, (x, y, c), _flip(q)) for q in (2, 4, 6)]
        for cp in first:
            cp.start()

        shares = []
        for n, (keep, got, out) in enumerate(((ki_ref, gi_ref, gin_ref), (ko_ref, go_ref, gout_ref))):
            rh = RS_SHAPES[n][0]
            half = lambda hh, rh=rh: pl.ds(pl.multiple_of(hh * rh, rh), rh)
            out[half(c), :] = ((keep[...] + got[0].astype(F32)) + got[1].astype(F32)) + got[2].astype(F32)
            share = _rcopy(out.at[half(c), :], out.at[half(c), :], send_sems, recv_sems, 7 + n, sib)
            share.start()
            shares.append((share, _rcopy(out.at[half(1 - c), :], out.at[half(1 - c), :], send_sems, recv_sems, 7 + n,
                                         sib)))

        passed = []
        for q in (2, 4, 6):
            slab_copy(q // 2, _flip(q), (x, y, c)).wait_recv()
            cp = slab_copy(3 + q // 2, _flip(q), sib)
            cp.start()
            passed.append(cp)
        slab_copy(0, sib, (x, y, c)).wait_recv()
        for q in (2, 4, 6):
            slab_copy(3 + q // 2, _flip(q | 1), (x, y, c)).wait_recv()
        tot = all_ref[0]
        for d in range(1, NDEV):
            tot = tot + all_ref[d]
        tot_ref[...] = tot
        for share, arrival in shares:
            arrival.wait_recv()
            share.wait_send()
        for cp in first + passed:
            cp.wait_send()

    (rhi, wi), (rho, wo) = RS_SHAPES
    return pl.pallas_call(
        kern, name="final_reduce", in_specs=[_VMEM_SPEC] * 5, out_specs=[_VMEM_SPEC] * 4,
        out_shape=[jax.ShapeDtypeStruct((2 * rhi, wi), F32), jax.ShapeDtypeStruct((2 * rho, wo), F32),
                   jax.ShapeDtypeStruct((NDEV, SLAB_ROWS, DM), F32), jax.ShapeDtypeStruct((SLAB_ROWS, DM), F32)],
        scratch_shapes=[pltpu.SemaphoreType.DMA((9,)), pltpu.SemaphoreType.DMA((9,))],
        compiler_params=pltpu.CompilerParams(vmem_limit_bytes=40 * 1024 * 1024),
    )(keep_i, got_i, keep_o, got_o, slab)


def ada_bwd(a_in, dm, dm_shard, w_ada, c_ctx):
    def kern(a_ref, dm_ref, dms_ref, w_ref, cc_ref, dw_ref, db_ref, dcc_ref, parts, send_sems, recv_sems):
        x, y, c = _me()
        k = 2 * x + y
        act = jax.nn.silu(a_ref[...]).astype(BF16)
        dms = dms_ref[...].astype(BF16)
        dw_ref[...] = lax.dot_general(act, dms, (((0,), (0,)), ((), ())), preferred_element_type=F32)
        db_ref[...] = jnp.sum(dm_ref[...], axis=0, keepdims=True)
        parts[k] = lax.dot_general(dms, w_ref[...].astype(BF16), (((1,), (1,)), ((), ())), preferred_element_type=F32)
        sends = [_rcopy(parts.at[k], parts.at[k], send_sems, recv_sems, q // 2 - 1, _flip(q)) for q in (2, 4, 6)]
        for cp in sends:
            cp.start()
        for q in (2, 4, 6):
            kq = _chip_of(_flip(q))
            _rcopy(parts.at[kq], parts.at[kq], send_sems, recv_sems, q // 2 - 1, _flip(q)).wait_recv()
        dact = ((parts[0] + parts[1]) + parts[2]) + parts[3]
        _, vjp = jax.vjp(jax.nn.silu, cc_ref[...])
        dcc_ref[...] = vjp(dact[8:9, :])[0]
        for cp in sends:
            cp.wait_send()

    return pl.pallas_call(
        kern, name="ada_bwd", in_specs=[_VMEM_SPEC] * 5, out_specs=[_VMEM_SPEC] * 3,
        out_shape=[jax.ShapeDtypeStruct((DM, SHARD_ADA), F32), jax.ShapeDtypeStruct((1, 3 * DM), F32),
                   jax.ShapeDtypeStruct((1, DM), F32)],
        scratch_shapes=[pltpu.VMEM((NCHIP, 16, DM), F32), pltpu.SemaphoreType.DMA((3,)), pltpu.SemaphoreType.DMA((3,))],
    )(a_in, dm, dm_shard, w_ada, c_ctx)


def _adamw_math(w, g, m, v):
    m = B1 * m + (1.0 - B1) * g
    v = B2 * v + (1.0 - B2) * (g * g)
    m_hat = m / (1.0 - B1 ** STEP)
    v_hat = v / (1.0 - B2 ** STEP)
    return -LR * (m_hat / (jnp.sqrt(v_hat) + ADAM_EPS) + WD * w), m, v


def adamw_big(w, g, m, v, name, block_rows=256):
    rows, width = w.shape

    def kern(w_ref, g_ref, m_ref, v_ref, d_ref, nm_ref, nv_ref):
        d_ref[...], nm_ref[...], nv_ref[...] = _adamw_math(w_ref[...], g_ref[...], m_ref[...], v_ref[...])

    spec = pl.BlockSpec((block_rows, width), lambda i: (i, 0))
    return pl.pallas_call(
        kern, name=name, grid=(rows // block_rows,), in_specs=[spec] * 4, out_specs=[spec] * 3,
        out_shape=[jax.ShapeDtypeStruct((rows, width), F32)] * 3,
        compiler_params=_cparams(("arbitrary",)),
    )(w, g, m, v)


def adamw_small(quads):
    n = len(quads)

    def kern(*refs):
        ins, outs = refs[:4 * n], refs[4 * n:]
        for i in range(n):
            w, g, m, v = (r[...] for r in ins[4 * i:4 * i + 4])
            outs[3 * i][...], outs[3 * i + 1][...], outs[3 * i + 2][...] = _adamw_math(w, g, m, v)

    flat = [a for quad in quads for a in quad]
    res = pl.pallas_call(
        kern, name="adamw_small", in_specs=[_VMEM_SPEC] * (4 * n), out_specs=[_VMEM_SPEC] * (3 * n),
        out_shape=[jax.ShapeDtypeStruct(q[0].shape, F32) for q in quads for _ in range(3)],
    )(*flat)
    return [tuple(res[3 * i:3 * i + 3]) for i in range(n)]


def _rows_of(a, rows):
    flat = a.reshape(-1)
    return jnp.pad(flat, (0, rows * DM - flat.shape[0])).reshape(rows, DM)


def kernel(x, c, ctx, c_ctx, w_ada, b_ada, norm_g, w_in, sgu_norm_g, w_spatial, b_spatial, q_norm_g, k_norm_g, rpb, w_out, loss_target, m_c_ctx, m_w_ada, m_b_ada, m_norm_g, m_w_in, m_sgu_norm_g, m_w_spatial, m_b_spatial, m_q_norm_g, m_k_norm_g, m_rpb, m_w_out, v_c_ctx, v_w_ada, v_b_ada, v_norm_g, v_w_in, v_sgu_norm_g, v_w_spatial, v_b_spatial, v_q_norm_g, v_k_norm_g, v_rpb, v_w_out):
    xi, yi, ci = lax.axis_index("x"), lax.axis_index("y"), lax.axis_index("c")
    chip, dev = 2 * xi + yi, 4 * xi + 2 * yi + ci
    c_ctx2 = c_ctx.reshape(1, DM)

    b_shard = lax.dynamic_slice(b_ada, (0, chip * SHARD_ADA), (1, SHARD_ADA))
    part = local_step(chip.reshape(1).astype(jnp.int32), dev, x[0], c, c_ctx2, w_ada[0], b_shard, ctx[0], loss_target[0],
                      norm_g, sgu_norm_g, w_spatial[0], b_spatial[0], q_norm_g, k_norm_g, rpb[0], w_in[0], w_out[0])
    cs = part["cs"]

    slab = jnp.concatenate([
        part["d_norm_g"], _rows_of(part["d_sgu_g"], 1), _rows_of(part["d_b_s"], 1),
        _rows_of(jnp.concatenate([part["d_q_g"], part["d_k_g"]], axis=-1), 1), _rows_of(part["d_rpb"], 4),
        _rows_of(part["loss"], 1), _rows_of(part["dcmod"], 3), _rows_of(part["dmod"], 3), jnp.zeros((1, DM), F32),
        _rows_of(part["d_w_s"], 64)], axis=0)
    g_w_in, g_w_out, gathered, tot = final_reduce(*part["rs"], slab)
    dm = jnp.concatenate([gathered[:, 12:15, :].reshape(NDEV, 3 * DM), tot[9:12].reshape(1, 3 * DM),
                          jnp.zeros((7, 3 * DM), F32)], axis=0)
    a_in = jnp.concatenate([cs[0:8 * NDEV:8], cs[8 * NDEV:8 * NDEV + 1], jnp.zeros((7, DM), F32)], axis=0)
    dm_shard = lax.dynamic_slice(dm, (0, chip * SHARD_ADA), (16, SHARD_ADA))
    g_w_ada, g_b_ada, g_c_ctx = ada_bwd(a_in, dm, dm_shard, w_ada[0], c_ctx2)

    loss = tot[8, 0]
    g_small = dict(
        c_ctx=g_c_ctx, b_ada=g_b_ada, norm_g=tot[0:1], sgu_norm_g=tot[1:2, :512], w_spatial=tot[16:80].reshape(512, 128),
        b_spatial=tot[2:3, :512].reshape(4, 128), q_norm_g=tot[3:4, :HDIM], k_norm_g=tot[3:4, HDIM:2 * HDIM],
        rpb=tot[4:8].reshape(-1)[:HEADS * 15 * 31].reshape(HEADS * 15, 31))
    shapes = dict(c_ctx=(DM,), w_ada=(1, DM, SHARD_ADA), b_ada=(1, 3 * DM), norm_g=(1, DM), w_in=(1, DM, SHARD_IN),
                  sgu_norm_g=(1, 512), w_spatial=(1, 4, 128, 128), b_spatial=(1, 4, 128), q_norm_g=(1, HDIM),
                  k_norm_g=(1, HDIM), rpb=(1, HEADS, 15, 31), w_out=(1, SHARD_OUT, DM))
    names = list(shapes)
    weights = dict(c_ctx=c_ctx, w_ada=w_ada, b_ada=b_ada, norm_g=norm_g, w_in=w_in, sgu_norm_g=sgu_norm_g,
                   w_spatial=w_spatial, b_spatial=b_spatial, q_norm_g=q_norm_g, k_norm_g=k_norm_g, rpb=rpb, w_out=w_out)
    m_in = dict(zip(names, (m_c_ctx, m_w_ada, m_b_ada, m_norm_g, m_w_in, m_sgu_norm_g, m_w_spatial, m_b_spatial,
                            m_q_norm_g, m_k_norm_g, m_rpb, m_w_out)))
    v_in = dict(zip(names, (v_c_ctx, v_w_ada, v_b_ada, v_norm_g, v_w_in, v_sgu_norm_g, v_w_spatial, v_b_spatial,
                            v_q_norm_g, v_k_norm_g, v_rpb, v_w_out)))
    grads = dict(g_small, w_ada=g_w_ada, w_in=g_w_in, w_out=g_w_out)
    upd = {}
    for n in ("w_ada", "w_in", "w_out"):
        g = grads[n]
        upd[n] = adamw_big(weights[n].reshape(g.shape), g, m_in[n].reshape(g.shape), v_in[n].reshape(g.shape),
                           "adamw_" + n)
    small = [n for n in names if n not in upd]
    res = adamw_small([(weights[n].reshape(grads[n].shape), grads[n], m_in[n].reshape(grads[n].shape),
                        v_in[n].reshape(grads[n].shape)) for n in small])
    upd.update(zip(small, res))
    out = [loss, part["grad_x"].reshape(1, SEQ, DM)]
    out += [grads[n].reshape(shapes[n]) for n in names]
    for slot in range(3):
        out += [upd[n][slot].reshape(shapes[n]) for n in names]
    return tuple(out)
```

```python
import jax
import jax.numpy as jnp
from jax import lax
from jax.experimental import pallas as pl
from jax.experimental.pallas import tpu as pltpu

F32, BF16 = jnp.float32, jnp.bfloat16
SEQ, DM, CTX, DIN = 4096, 1024, 256, 3584
NCHIP, NDEV = 4, 8
SHARD_IN = DIN // NCHIP
SHARD_ADA = 3 * DM // NCHIP
SHARD_OUT = DM // NCHIP
GRID_W = 64
QROWS = 4
KROWS = 12
QBLK, KBLK = QROWS * GRID_W, KROWS * GRID_W
NQBLK = SEQ // QBLK
HEADS, HDIM, NPAIR = 8, 64, 4
EPS = 1e-6
NEG_INF = -1e30
ZQ, ZK, ZV, ZG = 12, 16, 20, 24
LR, B1, B2, ADAM_EPS, WD, STEP = 0.001, 0.9, 0.999, 1e-08, 0.01, 10
VMEM_BIG = 56 * 1024 * 1024
MESH_ID = pl.DeviceIdType.MESH


def _dot(a, b, lhs_c, rhs_c):
    return lax.dot_general(a.astype(BF16), b.astype(BF16), (((lhs_c,), (rhs_c,)), ((), ())),
                           preferred_element_type=F32)


@jax.custom_vjp
def mm(a, b):
    return _dot(a, b, 1, 0)


@jax.custom_vjp
def mm_nt(a, b):
    return _dot(a, b, 1, 1)


@jax.custom_vjp
def mm_tn(a, b):
    return _dot(a, b, 0, 0)


mm.defvjp(lambda a, b: (mm(a, b), (a, b)), lambda r, ct: (mm_nt(ct, r[1]), mm_tn(r[0], ct)))
mm_nt.defvjp(lambda a, b: (mm_nt(a, b), (a, b)), lambda r, ct: (mm(ct, r[1]), mm_tn(ct, r[0])))
mm_tn.defvjp(lambda a, b: (mm_tn(a, b), (a, b)), lambda r, ct: (mm_nt(r[1], ct), mm(r[0], ct)))


def _rms(x, g):
    return x * lax.rsqrt(jnp.mean(x * x, axis=-1, keepdims=True) + EPS) * g


def _modulated(x, g, scale, shift):
    return _rms(x, g) * (1.0 + scale) + shift


def _pair_rms(x, g2):
    lo = lax.broadcasted_iota(jnp.int32, (1, 2 * HDIM), 1) < HDIM
    sq = x * x
    s_lo = jnp.sum(jnp.where(lo, sq, 0.0), axis=-1, keepdims=True)
    s_hi = jnp.sum(jnp.where(lo, 0.0, sq), axis=-1, keepdims=True)
    rs = jnp.where(lo, lax.rsqrt(s_lo / HDIM + EPS), lax.rsqrt(s_hi / HDIM + EPS))
    return x * rs * g2


def _cparams(sem, vmem=None):
    return pltpu.CompilerParams(dimension_semantics=sem, vmem_limit_bytes=vmem)


def _row(n):
    return pl.BlockSpec((1, n), lambda *_: (0, 0))


CS_ROWS = 8 * NDEV + 8


def _mod_part(mod_ref, row, part):
    pieces = []
    for j in range(NCHIP):
        lo, hi = max(part * DM, j * SHARD_ADA), min((part + 1) * DM, (j + 1) * SHARD_ADA)
        if lo < hi:
            pieces.append(mod_ref[j, row, lo - j * SHARD_ADA:hi - j * SHARD_ADA])
    return jnp.concatenate(pieces, axis=-1)


def inproj_fwd(chip, x, c_vec, c_ctx, w_ada, b_shard, norm_g, w_shard, wo_shard):
    tl = 1024
    nt = SEQ // tl
    halves = (DM // 2, SHARD_OUT // 2)
    n_w, n_c = 12, NDEV - 1

    def kern(k_ref, x_ref, cv_ref, cc_ref, wa_ref, b_ref, g_ref, w_ref, wo_ref,
             z_ref, h_ref, wfull_ref, wofull_ref, modall_ref, csall_ref,
             w_scr, wo_scr, h_scr, mine, cs_scr, mod_scr, shsc_scr, send_sems, recv_sems):
        s, t = pl.program_id(0), pl.program_id(1)
        xi, yi, c = _me()
        k, me = 2 * xi + yi, 4 * xi + 2 * yi + c
        sib = _flip(1)
        rows = pl.ds(pl.multiple_of(t * tl, tl), tl)
        gathered = (w_scr, wo_scr)
        slot = lambda d: pl.ds(pl.multiple_of(8 * d, 8), 8)

        def c_copy(q, owner):
            return _rcopy(mine, cs_scr.at[slot(owner), :], send_sems, recv_sems, n_w + q - 1, _flip(q))

        def m_copy(q, chip_of_block):
            return _rcopy(mod_scr.at[chip_of_block], mod_scr.at[chip_of_block], send_sems, recv_sems,
                          n_w + n_c + q // 2 - 1, _flip(q))

        def adaln():
            first = lax.broadcasted_iota(jnp.int32, (8, DM), 0) == 0
            mine[...] = jnp.where(first, jnp.broadcast_to(cv_ref[...], (8, DM)), 0.0)
            cs_scr[slot(me), :] = mine[...]
            cs_scr[slot(NDEV), :] = jnp.where(first, jnp.broadcast_to(cc_ref[...], (8, DM)), 0.0)
            for q in range(1, NDEV):
                c_copy(q, me).start()
            wa = wa_ref[...].astype(BF16)
            for q in range(1, NDEV):
                px, py, pc = _flip(q)
                c_copy(q, 4 * px + 2 * py + pc).wait_recv()
            act = jax.nn.silu(cs_scr[...]).astype(BF16)
            mod_scr[k] = jnp.dot(act, wa, preferred_element_type=F32) + b_ref[...]
            for q in (2, 4, 6):
                m_copy(q, k).start()
            for q in (2, 4, 6):
                m_copy(q, _chip_of(_flip(q))).wait_recv()
            row = pl.ds(8 * me, 1)
            shsc_scr[0:1, :] = _mod_part(mod_scr, row, 0)
            shsc_scr[1:2, :] = _mod_part(mod_scr, row, 1)
            pltpu.sync_copy(mod_scr, modall_ref)
            pltpu.sync_copy(cs_scr, csall_ref)

        def block(n, chip_of_block, hh):
            return gathered[n].at[chip_of_block, pl.ds(pl.multiple_of(hh * halves[n], halves[n]), halves[n]), :]

        def ici(n, q, chip_of_block):
            blk = block(n, chip_of_block, c)
            return _rcopy(blk, blk, send_sems, recv_sems, 6 * n + q // 2 - 1, _flip(q))

        def d2d(n, q, chip_of_block, hh):
            blk = block(n, chip_of_block, hh)
            return _rcopy(blk, blk, send_sems, recv_sems, 6 * n + 3 + q // 2 - 1, sib)

        @pl.when((s == 0) & (t == 0))
        def _():
            adaln()
            w_scr[k] = w_ref[...].astype(BF16)
            wo_scr[k] = wo_ref[...].astype(BF16)
            for q in (2, 4, 6):
                ici(0, q, k).start()
                ici(1, q, k).start()

        for sweep in (1, 2, 3):
            @pl.when((s == sweep) & (t == 0))
            def _():
                q = 2 * sweep
                src = _chip_of(_flip(q))
                for n in (0, 1):
                    ici(n, q, src).wait_recv()
                    d2d(n, q, src, c).start()
                for n in (0, 1):
                    d2d(n, q, src, 1 - c).wait_recv()

        @pl.when(s == 0)
        def _():
            hb = _modulated(x_ref[...], g_ref[...], shsc_scr[1:2, :], shsc_scr[0:1, :]).astype(BF16)
            h_scr[rows, :] = hb
            h_ref[...] = hb

        z_ref[...] = jnp.dot(h_scr[rows, :], w_scr[lax.bitwise_xor(k, s)], preferred_element_type=F32)

        @pl.when((s == NCHIP - 1) & (t == nt - 1))
        def _():
            for q in range(1, NDEV):
                c_copy(q, me).wait_send()
            for q in (2, 4, 6):
                m_copy(q, k).wait_send()
            for n in (0, 1):
                for q in (2, 4, 6):
                    ici(n, q, k).wait_send()
                    d2d(n, q, _chip_of(_flip(q)), c).wait_send()
            pltpu.sync_copy(w_scr, wfull_ref)
            pltpu.sync_copy(wo_scr, wofull_ref)

    once = lambda s, t, k: (jnp.where(s == 0, t, nt - 1), 0)
    hbm = pl.BlockSpec(memory_space=pl.ANY)
    n_sem = n_w + n_c + 3
    return pl.pallas_call(
        kern, name="inproj_fwd",
        grid_spec=pltpu.PrefetchScalarGridSpec(
            num_scalar_prefetch=1, grid=(NCHIP, nt),
            in_specs=[pl.BlockSpec((tl, DM), once)] + [_VMEM_SPEC] * 7,
            out_specs=[pl.BlockSpec((tl, SHARD_IN), lambda s, t, k: (t, lax.bitwise_xor(k[0], s))),
                       pl.BlockSpec((tl, DM), once), hbm, hbm, hbm, hbm],
            scratch_shapes=[pltpu.VMEM((NCHIP, DM, SHARD_IN), BF16), pltpu.VMEM((NCHIP, SHARD_OUT, DM), BF16),
                            pltpu.VMEM((SEQ, DM), BF16), pltpu.VMEM((8, DM), F32), pltpu.VMEM((CS_ROWS, DM), F32),
                            pltpu.VMEM((NCHIP, CS_ROWS, SHARD_ADA), F32), pltpu.VMEM((8, DM), F32),
                            pltpu.SemaphoreType.DMA((n_sem,)), pltpu.SemaphoreType.DMA((n_sem,))]),
        out_shape=[jax.ShapeDtypeStruct((SEQ, DIN), F32), jax.ShapeDtypeStruct((SEQ, DM), BF16),
                   jax.ShapeDtypeStruct((NCHIP, DM, SHARD_IN), BF16), jax.ShapeDtypeStruct((NCHIP, SHARD_OUT, DM), BF16),
                   jax.ShapeDtypeStruct((NCHIP, CS_ROWS, SHARD_ADA), F32), jax.ShapeDtypeStruct((CS_ROWS, DM), F32)],
        compiler_params=_cparams(("arbitrary", "arbitrary"), VMEM_BIG),
    )(chip, x, c_vec, c_ctx, w_ada, b_shard, norm_g, w_shard, wo_shard)


def ctx_fwd(ctx, cshift, cscale, norm_g, w_full):
    def kern(c_ref, sh_ref, sc_ref, g_ref, w2_ref, w3_ref, zc_ref, hc_ref):
        hc = _modulated(c_ref[...], g_ref[...], sc_ref[...], sh_ref[...]).astype(BF16)
        hc_ref[...] = hc
        zc_ref[:, :SHARD_IN] = jnp.dot(hc, w2_ref[0], preferred_element_type=F32)
        zc_ref[:, SHARD_IN:] = jnp.dot(hc, w3_ref[0], preferred_element_type=F32)

    return pl.pallas_call(
        kern, name="ctx_fwd", grid=(1,),
        in_specs=[pl.BlockSpec((CTX, DM), lambda i: (0, 0)), _row(DM), _row(DM), _row(DM),
                  pl.BlockSpec((1, DM, SHARD_IN), lambda i: (2, 0, 0)),
                  pl.BlockSpec((1, DM, SHARD_IN), lambda i: (3, 0, 0))],
        out_specs=[pl.BlockSpec((CTX, 2 * SHARD_IN), lambda i: (0, 0)),
                   pl.BlockSpec((CTX, DM), lambda i: (0, 0))],
        out_shape=[jax.ShapeDtypeStruct((CTX, 2 * SHARD_IN), F32), jax.ShapeDtypeStruct((CTX, DM), BF16)],
        compiler_params=_cparams(("arbitrary",)),
    )(ctx, cshift, cscale, norm_g, w_full, w_full)


SGU_CHUNK, SGU_PER_STEP = 128, 4


def _gelu(x):
    return 0.5 * x * (1.0 + lax.erf(x * 0.7071067811865476))


def _sgu_chunk(au, av, ag, sg, ws, bsb):
    u, v = _gelu(au), _gelu(av)
    outs = []
    for g in range(4):
        sl = slice(128 * g, 128 * (g + 1))
        mixed = mm(ws[g], _rms(v[:, sl], sg[:, sl])) + bsb[g]
        outs.append(u[:, sl] * mixed * jax.nn.silu(ag[:, sl]))
    return jnp.concatenate(outs, axis=-1)


def _sgu_specs():
    rows = SGU_CHUNK * SGU_PER_STEP
    zspec = lambda c: pl.BlockSpec((rows, 512), lambda n: (n, c))
    wspec = pl.BlockSpec((4, 128, 128), lambda n: (0, 0, 0))
    return rows, [zspec(0), zspec(1), zspec(2), _row(512), wspec, wspec]


def sgu_fwd(z, sg, ws, bsb):
    rows, in_specs = _sgu_specs()

    def kern(au_ref, av_ref, ag_ref, sg_ref, ws_ref, bs_ref, o_ref):
        for c in range(SGU_PER_STEP):
            sl = slice(c * SGU_CHUNK, (c + 1) * SGU_CHUNK)
            o_ref[sl, :] = _sgu_chunk(au_ref[sl, :], av_ref[sl, :], ag_ref[sl, :], sg_ref[...], ws_ref[...],
                                      bs_ref[...])

    return pl.pallas_call(
        kern, name="sgu_fwd", grid=(SEQ // rows,), in_specs=in_specs,
        out_specs=pl.BlockSpec((rows, 512), lambda n: (n, 0)),
        out_shape=jax.ShapeDtypeStruct((SEQ, 512), F32),
        compiler_params=_cparams(("arbitrary",)),
    )(z, z, z, sg, ws, bsb)


def sgu_bwd(z, sg, ws, bsb, dcat):
    rows, in_specs = _sgu_specs()

    def kern(au_ref, av_ref, ag_ref, sg_ref, ws_ref, bs_ref, do_ref, dz_ref, dsg_ref, dws_ref, dbs_ref):
        @pl.when(pl.program_id(0) == 0)
        def _():
            dsg_ref[...] = jnp.zeros_like(dsg_ref)
            dws_ref[...] = jnp.zeros_like(dws_ref)
            dbs_ref[...] = jnp.zeros_like(dbs_ref)

        for c in range(SGU_PER_STEP):
            sl = slice(c * SGU_CHUNK, (c + 1) * SGU_CHUNK)
            _, vjp = jax.vjp(_sgu_chunk, au_ref[sl, :], av_ref[sl, :], ag_ref[sl, :], sg_ref[...], ws_ref[...],
                             bs_ref[...])
            dau, dav, dag, dsg, dws, dbs = vjp(do_ref[sl, :])
            dz_ref[sl, 0:512] = dau.astype(BF16)
            dz_ref[sl, 512:1024] = dav.astype(BF16)
            dz_ref[sl, 1024:1536] = dag.astype(BF16)
            dsg_ref[...] += dsg
            dws_ref[...] += dws
            dbs_ref[...] += dbs

        @pl.when(pl.program_id(0) == pl.num_programs(0) - 1)
        def _():
            dbs_ref[...] = jnp.broadcast_to(jnp.sum(dbs_ref[...], axis=-1, keepdims=True), dbs_ref.shape)

    wspec = pl.BlockSpec((4, 128, 128), lambda n: (0, 0, 0))
    return pl.pallas_call(
        kern, name="sgu_bwd", grid=(SEQ // rows,),
        in_specs=in_specs + [pl.BlockSpec((rows, 512), lambda n: (n, 0))],
        out_specs=[pl.BlockSpec((rows, 1536), lambda n: (n, 0)), _row(512), wspec, wspec],
        out_shape=[jax.ShapeDtypeStruct((SEQ, 1536), BF16), jax.ShapeDtypeStruct((1, 512), F32),
                   jax.ShapeDtypeStruct((4, 128, 128), F32), jax.ShapeDtypeStruct((4, 128, 128), F32)],
        compiler_params=_cparams(("arbitrary",)),
    )(z, z, z, sg, ws, bsb, dcat)


_DR_OFF = (7, 3, -1)


def _row_valid(v, rr, j):
    return (j < 8, rr <= j < rr + 8, 4 <= j < 12)[v]


def _col_window():
    q = lax.broadcasted_iota(jnp.int32, (GRID_W, 128), 0)
    kc = lax.broadcasted_iota(jnp.int32, (GRID_W, 128), 1) % GRID_W
    c0 = jnp.clip(q - 8, 0, GRID_W - 16)
    return (kc >= c0) & (kc < c0 + 16)


def rpb_tables(rpb2):
    def kern(r_ref, b_ref):
        base = r_ref[0]
        lo = lax.broadcasted_iota(jnp.int32, (1, 128), 1) < GRID_W
        win = _col_window()
        tiles = {}
        for v in range(3):
            for rr in range(QROWS):
                for jp in range(KROWS // 2):
                    j0, j1 = 2 * jp, 2 * jp + 1
                    ok0, ok1 = _row_valid(v, rr, j0), _row_valid(v, rr, j1)
                    key = (j0 - rr + _DR_OFF[v], ok0, ok1) if (ok0 or ok1) else None
                    if key not in tiles:
                        if key is None:
                            tiles[key] = jnp.full((GRID_W, 128), NEG_INF, F32)
                        else:
                            d0 = key[0]
                            r0 = base[d0:d0 + 1, :] if ok0 else jnp.zeros((1, 128), F32)
                            r1 = base[d0 + 1:d0 + 2, :] if ok1 else jnp.zeros((1, 128), F32)
                            y = jnp.broadcast_to(jnp.where(lo, r0, r1), (GRID_W, 128))
                            y = pltpu.roll(pltpu.roll(y, 128 - 15, 1), 0, 1, stride=1, stride_axis=0)
                            tiles[key] = jnp.where(win & jnp.where(lo, ok0, ok1), y, NEG_INF)
                    b_ref[v, 0, rr * GRID_W:(rr + 1) * GRID_W, jp * 128:(jp + 1) * 128] = tiles[key]

    return pl.pallas_call(
        kern, name="rpb_tables", grid=(HEADS,),
        in_specs=[pl.BlockSpec((1, 15, 128), lambda h: (h, 0, 0))],
        out_specs=pl.BlockSpec((3, 1, QBLK, KBLK), lambda h: (0, h, 0, 0)),
        out_shape=jax.ShapeDtypeStruct((3, HEADS, QBLK, KBLK), F32),
        compiler_params=_cparams(("arbitrary",)),
    )(rpb2)


def rpb_bwd(dbias):
    def kern(g0_ref, g1_ref, g2_ref, o_ref):
        g_refs = (g0_ref, g1_ref, g2_ref)
        lo = lax.broadcasted_iota(jnp.int32, (1, 128), 1) < GRID_W
        ri = lax.broadcasted_iota(jnp.int32, (GRID_W, GRID_W), 0)
        ci = lax.broadcasted_iota(jnp.int32, (GRID_W, GRID_W), 1)
        flip = (ri + ci == GRID_W - 1).astype(F32)
        groups = {}
        for v in range(3):
            for rr in range(QROWS):
                for jp in range(KROWS // 2):
                    j0, j1 = 2 * jp, 2 * jp + 1
                    ok0, ok1 = _row_valid(v, rr, j0), _row_valid(v, rr, j1)
                    if not (ok0 or ok1):
                        continue
                    g = g_refs[v][0, rr * GRID_W:(rr + 1) * GRID_W, jp * 128:(jp + 1) * 128]
                    key = (j0 - rr + _DR_OFF[v], ok0, ok1)
                    groups[key] = g if key not in groups else groups[key] + g
        acc = [jnp.zeros((1, 128), F32) for _ in range(15)]
        for (d0, ok0, ok1), g in groups.items():
            g = lax.dot_general(flip, g, (((1,), (0,)), ((), ())), precision=lax.Precision.HIGHEST,
                                preferred_element_type=F32)
            g = pltpu.roll(pltpu.roll(g, 128 - 48, 1), 0, 1, stride=1, stride_axis=0)
            s = jnp.sum(g, axis=0, keepdims=True)
            if ok0:
                acc[d0] = acc[d0] + jnp.where(lo, s, 0.0)
            if ok1:
                acc[d0 + 1] = acc[d0 + 1] + jnp.where(lo, 0.0, s)
        for d in range(15):
            o_ref[0, d:d + 1, :] = acc[d] + pltpu.roll(acc[d], GRID_W, 1)

    return pl.pallas_call(
        kern, name="rpb_bwd", grid=(HEADS,),
        in_specs=[pl.BlockSpec((1, QBLK, KBLK), lambda h: (h, 0, 0))] * 3,
        out_specs=pl.BlockSpec((1, 15, 128), lambda h: (h, 0, 0)),
        out_shape=jax.ShapeDtypeStruct((HEADS, 15, 128), F32),
        compiler_params=_cparams(("arbitrary",)),
    )(*dbias)


def _scaled_q(q_raw, qg):
    return _pair_rms(q_raw, qg) * (HDIM ** -0.5)


def _head_lanes():
    lo = lax.broadcasted_iota(jnp.int32, (1, 2 * HDIM), 1) < HDIM
    return lo, jnp.logical_not(lo)


def _attn_step(q_raw, kn, v, ckn, cv, bias2, qg):
    qn = _scaled_q(q_raw, qg)
    out = rden = None
    probs = []
    for a, mine in enumerate(_head_lanes()):
        qa = jnp.where(mine, qn, 0.0)
        s_lat = mm_nt(qa, kn) + bias2[a]
        s_ctx = mm_nt(qa, ckn)
        m = jnp.maximum(jnp.max(s_lat, axis=-1, keepdims=True), jnp.max(s_ctx, axis=-1, keepdims=True))
        p_lat = jnp.exp(s_lat - m)
        p_ctx = jnp.exp(s_ctx - m)
        den = jnp.sum(p_lat, axis=-1, keepdims=True) + jnp.sum(p_ctx, axis=-1, keepdims=True)
        p_lat, p_ctx = p_lat.astype(BF16), p_ctx.astype(BF16)
        o = jnp.where(mine, (mm(p_lat, v) + mm(p_ctx, cv)) / den, 0.0)
        rr = jnp.where(mine, 1.0 / den, 0.0)
        out, rden = (o, rr) if out is None else (out + o, rden + rr)
        probs.append((p_lat, p_ctx))
    return out, rden, probs


def _attn_step_bwd(q_raw, kn, v, ckn, cv, qg, bg, o, rden, probs, dout):
    sig = jax.nn.sigmoid(bg)
    do = dout * (bg * sig)
    dbg = dout * o * (sig * (1.0 + bg * (1.0 - sig)))
    qn, qn_vjp = jax.vjp(_scaled_q, q_raw, qg)
    row_dot = do * o
    dqn = dkn = dv = dckn = dcv = None
    dbias = []
    for mine, (p_lat, p_ctx) in zip(_head_lanes(), probs):
        qa = jnp.where(mine, qn, 0.0)
        r = jnp.max(jnp.where(mine, rden, 0.0), axis=-1, keepdims=True)
        doa = jnp.where(mine, do, 0.0) * r
        delta = jnp.sum(jnp.where(mine, row_dot, 0.0), axis=-1, keepdims=True) * r
        ds_lat = p_lat.astype(F32) * (mm_nt(doa, v) - delta)
        ds_ctx = p_ctx.astype(F32) * (mm_nt(doa, cv) - delta)
        parts = (jnp.where(mine, mm(ds_lat, kn) + mm(ds_ctx, ckn), 0.0), mm_tn(ds_lat, qa), mm_tn(p_lat, doa),
                 mm_tn(ds_ctx, qa), mm_tn(p_ctx, doa))
        if dqn is None:
            dqn, dkn, dv, dckn, dcv = parts
        else:
            dqn, dkn, dv, dckn, dcv = (acc + new for acc, new in zip((dqn, dkn, dv, dckn, dcv), parts))
        dbias.append(ds_lat)
    dq, dqg = qn_vjp(dqn)
    return dq, dkn, dv, dckn, dcv, dbias, dqg, dbg


def _kstart(i):
    return pl.multiple_of(jnp.clip((i - 1) * QBLK, 0, SEQ - KBLK), QBLK)


ATTN_STEPS = NQBLK // 2
ATTN_ROWS = 2 * QBLK


def _attn_in_specs():
    return [
        pl.BlockSpec((ATTN_ROWS, 128), lambda p, i: (i, ZQ + p)),
        pl.BlockSpec((SEQ, 128), lambda p, i: (0, ZK + p)),
        pl.BlockSpec((SEQ, 128), lambda p, i: (0, ZV + p)),
        pl.BlockSpec((ATTN_ROWS, 128), lambda p, i: (i, ZG + p)),
        pl.BlockSpec((CTX, 128), lambda p, i: (0, 2 + p)),
        pl.BlockSpec((CTX, 128), lambda p, i: (0, 6 + p)),
    ]


def _bias_specs():
    bias_spec = lambda variant: pl.BlockSpec((1, 2, QBLK, KBLK), lambda p, i: (variant(i), p, 0, 0))
    return [bias_spec(lambda i: jnp.where(i == 0, 0, 1)),
            bias_spec(lambda i: jnp.where(i == ATTN_STEPS - 1, 2, 1))]


def _prob_specs():
    return [pl.BlockSpec((2, ATTN_ROWS, KBLK), lambda p, i: (p, i, 0)),
            pl.BlockSpec((2, ATTN_ROWS, CTX), lambda p, i: (p, i, 0))]


NORM_ROWS = 512


def _norm_keys(k_ref, ck_ref, kg_ref, kn_scr, ckn_scr):
    def body(c, carry):
        sl = pl.ds(pl.multiple_of(c * NORM_ROWS, NORM_ROWS), NORM_ROWS)
        kn_scr[sl, :] = _pair_rms(k_ref[sl, :], kg_ref[...])
        return carry

    lax.fori_loop(0, SEQ // NORM_ROWS, body, 0)
    ckn_scr[...] = _pair_rms(ck_ref[...], kg_ref[...])


def attn_fwd(z, zc, bias, qg2, kg2):
    def kern(q_ref, k_ref, v_ref, bg_ref, ck_ref, cv_ref, be_ref, bo_ref, qg_ref, kg_ref,
             ob_ref, o_ref, rden_ref, pl_ref, pc_ref, kn_scr, ckn_scr):
        i = pl.program_id(1)

        @pl.when(i == 0)
        def _():
            _norm_keys(k_ref, ck_ref, kg_ref, kn_scr, ckn_scr)

        for b, b_ref in enumerate((be_ref, bo_ref)):
            rows = slice(b * QBLK, (b + 1) * QBLK)
            ks = pl.ds(_kstart(2 * i + b), KBLK)
            o, rden, probs = _attn_step(q_ref[rows, :], kn_scr[ks, :], v_ref[ks, :], ckn_scr[...], cv_ref[...],
                                        b_ref[0], qg_ref[...])
            ob_ref[rows, :] = o * jax.nn.silu(bg_ref[rows, :])
            o_ref[rows, :] = o
            rden_ref[rows, :] = rden
            for a, (p_lat, p_ctx) in enumerate(probs):
                pl_ref[a, rows, :] = p_lat
                pc_ref[a, rows, :] = p_ctx

    qblk = pl.BlockSpec((ATTN_ROWS, 128), lambda p, i: (i, p))
    return pl.pallas_call(
        kern, name="attn_fwd", grid=(NPAIR, ATTN_STEPS),
        in_specs=_attn_in_specs() + _bias_specs() + [_row(128), _row(128)], out_specs=[qblk] * 3 + _prob_specs(),
        out_shape=[jax.ShapeDtypeStruct((SEQ, 512), F32)] * 3
        + [jax.ShapeDtypeStruct((HEADS, SEQ, KBLK), BF16), jax.ShapeDtypeStruct((HEADS, SEQ, CTX), BF16)],
        scratch_shapes=[pltpu.VMEM((SEQ, 128), F32), pltpu.VMEM((CTX, 128), F32)],
        compiler_params=_cparams(("arbitrary", "arbitrary"), 40 * 1024 * 1024),
    )(z, z, z, z, zc, zc, bias, bias, qg2, kg2)


def attn_bwd(z, zc, qg2, kg2, dcat, saved):
    def kern(q_ref, k_ref, v_ref, bg_ref, ck_ref, cv_ref, qg_ref, kg_ref, do_ref, o_ref, rden_ref, pl_ref, pc_ref,
             dq_ref, dk_ref, dv_ref, dbg_ref, dck_ref, dcv_ref, db0_ref, db1_ref, db2_ref, dqg_ref, dkg_ref,
             kn_scr, ckn_scr, dkn_scr, dckn_scr, dv_scr):
        p, i = pl.program_id(0), pl.program_id(1)
        last = i == ATTN_STEPS - 1

        @pl.when(i == 0)
        def _():
            _norm_keys(k_ref, ck_ref, kg_ref, kn_scr, ckn_scr)
            dkn_scr[...] = jnp.zeros_like(dkn_scr)
            dv_scr[...] = jnp.zeros_like(dv_scr)
            dckn_scr[...] = jnp.zeros_like(dckn_scr)
            dcv_ref[...] = jnp.zeros_like(dcv_ref)

        @pl.when((i == 0) & (p == 0))
        def _():
            dqg_ref[...] = jnp.zeros_like(dqg_ref)
            dkg_ref[...] = jnp.zeros_like(dkg_ref)

        db = []
        for b in range(2):
            rows = slice(b * QBLK, (b + 1) * QBLK)
            ks = pl.ds(_kstart(2 * i + b), KBLK)
            probs = [(pl_ref[a, rows, :], pc_ref[a, rows, :]) for a in range(2)]
            dq, dkn, dv, dckn, dcv, dbb, dqg, dbg = _attn_step_bwd(
                q_ref[rows, :], kn_scr[ks, :], v_ref[ks, :], ckn_scr[...], cv_ref[...], qg_ref[...],
                bg_ref[rows, :], o_ref[rows, :], rden_ref[rows, :], probs, do_ref[rows, :])
            dq_ref[rows, :] = dq.astype(BF16)
            dbg_ref[rows, :] = dbg.astype(BF16)
            dkn_scr[ks, :] += dkn
            dv_scr[ks, :] += dv
            dckn_scr[...] += dckn
            dcv_ref[...] += dcv
            dqg_ref[...] += dqg
            db.append(dbb)

        @pl.when(i == 0)
        def _():
            for a in range(2):
                db0_ref[a] = db[0][a]
                db1_ref[a] = db[1][a]

        @pl.when((i > 0) & jnp.logical_not(last))
        def _():
            for a in range(2):
                db1_ref[a] += db[0][a] + db[1][a]

        @pl.when(last)
        def _():
            for a in range(2):
                db1_ref[a] += db[0][a]
                db2_ref[a] = db[1][a]

        @pl.when(last)
        def _():
            def body(c, dkg):
                sl = pl.ds(pl.multiple_of(c * NORM_ROWS, NORM_ROWS), NORM_ROWS)
                _, nvjp = jax.vjp(_pair_rms, k_ref[sl, :], kg_ref[...])
                dk, dg = nvjp(dkn_scr[sl, :])
                dk_ref[sl, :] = dk.astype(BF16)
                dv_ref[sl, :] = dv_scr[sl, :].astype(BF16)
                return dkg + dg

            dkg = lax.fori_loop(0, SEQ // NORM_ROWS, body, jnp.zeros((1, 128), F32))
            _, nvjp = jax.vjp(_pair_rms, ck_ref[...], kg_ref[...])
            dck, dg = nvjp(dckn_scr[...])
            dck_ref[...] = dck
            dkg_ref[...] += dkg + dg

        @pl.when(last & (p == NPAIR - 1))
        def _():
            dqg_ref[...] = dqg_ref[...] + pltpu.roll(dqg_ref[...], HDIM, 1)
            dkg_ref[...] = dkg_ref[...] + pltpu.roll(dkg_ref[...], HDIM, 1)

    blk = lambda rows: pl.BlockSpec((rows, 128), lambda p, i: (0, p))
    qblk = pl.BlockSpec((ATTN_ROWS, 128), lambda p, i: (i, p))
    dbias = pl.BlockSpec((2, QBLK, KBLK), lambda p, i: (p, 0, 0))
    return pl.pallas_call(
        kern, name="attn_bwd", grid=(NPAIR, ATTN_STEPS),
        in_specs=_attn_in_specs() + [_row(128), _row(128), pl.BlockSpec((ATTN_ROWS, 128), lambda p, i: (i, 4 + p)),
                                     qblk, qblk] + _prob_specs(),
        out_specs=[qblk, blk(SEQ), blk(SEQ), qblk, blk(CTX), blk(CTX), dbias, dbias, dbias, _row(128), _row(128)],
        out_shape=[jax.ShapeDtypeStruct((SEQ, 512), BF16)] * 4 + [jax.ShapeDtypeStruct((CTX, 512), F32)] * 2
        + [jax.ShapeDtypeStruct((HEADS, QBLK, KBLK), F32)] * 3
        + [jax.ShapeDtypeStruct((1, 128), F32), jax.ShapeDtypeStruct((1, 128), F32)],
        scratch_shapes=[pltpu.VMEM((SEQ, 128), F32), pltpu.VMEM((CTX, 128), F32),
                        pltpu.VMEM((SEQ, 128), F32), pltpu.VMEM((CTX, 128), F32), pltpu.VMEM((SEQ, 128), F32)],
        compiler_params=_cparams(("arbitrary", "arbitrary"), VMEM_BIG),
    )(z, z, z, z, zc, zc, qg2, kg2, dcat, *saved)


def outproj(out_a, out_b, x, target, gate, wo):
    tl = 512

    def kern(a_ref, b_ref, x_ref, t_ref, g_ref, w_ref, loss_ref, dy_ref, dcat_ref, dg_ref, dw_ref):
        @pl.when(pl.program_id(0) == 0)
        def _():
            loss_ref[...] = jnp.zeros_like(loss_ref)
            dg_ref[...] = jnp.zeros_like(dg_ref)
            dw_ref[...] = jnp.zeros_like(dw_ref)

        a, b = a_ref[...].astype(BF16), b_ref[...].astype(BF16)
        mix = (jnp.dot(a, w_ref[0:512, :], preferred_element_type=F32)
               + jnp.dot(b, w_ref[512:1024, :], preferred_element_type=F32))
        err = x_ref[...] + g_ref[...] * mix - t_ref[...]
        loss_ref[...] += 0.5 * jnp.sum(jnp.mean(err * err, axis=-1))
        dy = err * (1.0 / DM)
        dy_ref[...] = dy
        dg_ref[...] += jnp.sum(dy * mix, axis=0, keepdims=True)
        dmix = (g_ref[...] * dy).astype(BF16)
        dcat_ref[...] = lax.dot_general(dmix, w_ref[...], (((1,), (1,)), ((), ())), preferred_element_type=F32)
        dw_ref[0:512, :] += lax.dot_general(a, dmix, (((0,), (0,)), ((), ())), preferred_element_type=F32)
        dw_ref[512:1024, :] += lax.dot_general(b, dmix, (((0,), (0,)), ((), ())), preferred_element_type=F32)

    tile = lambda w: pl.BlockSpec((tl, w), lambda t: (t, 0))
    whole = pl.BlockSpec((DM, DM), lambda t: (0, 0))
    return pl.pallas_call(
        kern, name="outproj", grid=(SEQ // tl,),
        in_specs=[tile(512), tile(512), tile(DM), tile(DM), _row(DM), whole],
        out_specs=[pl.BlockSpec((8, 128), lambda t: (0, 0)), tile(DM), tile(DM), _row(DM), whole],
        out_shape=[jax.ShapeDtypeStruct((8, 128), F32), jax.ShapeDtypeStruct((SEQ, DM), F32),
                   jax.ShapeDtypeStruct((SEQ, DM), F32), jax.ShapeDtypeStruct((1, DM), F32),
                   jax.ShapeDtypeStruct((DM, DM), F32)],
        compiler_params=_cparams(("arbitrary",), 48 * 1024 * 1024),
    )(out_a, out_b, x, target, gate, wo)


def _pieces(sources):
    out = []
    for name, c0, c1 in sources:
        for j in range(NCHIP):
            lo, hi = max(c0, j * SHARD_IN), min(c1, (j + 1) * SHARD_IN)
            if lo < hi:
                out.append((j, lo - j * SHARD_IN, hi - j * SHARD_IN, name, lo - c0, hi - c0))
    return out


DZ_PIECES = _pieces((("a", 0, 1536), ("q", 1536, 2048), ("k", 2048, 2560), ("v", 2560, 3072), ("g", 3072, DIN)))
DZC_PIECES = _pieces((("k", 2048, 2560), ("v", 2560, 3072)))
_NT = (((1,), (1,)), ((), ()))


DH_SUBTILES = 2


def _dz_specs(tl):
    return [pl.BlockSpec((tl, 1536), lambda t: (t, 0))] + [pl.BlockSpec((tl, 512), lambda t: (t, 0))] * 4


def dh_bwd(dz_parts, w_full, x, dy, shift, scale, norm_g, dg_ctx, wire_i, wire_o):
    tl = 512
    nt = SEQ // tl

    def kern(a_ref, q_ref, k_ref, v_ref, g_ref, w_ref, x_ref, dy_ref, sh_ref, sc_ref, gn_ref, dgc_ref, wi_hbm, wo_hbm,
             gx_ref, dsh_ref, dsc_ref, dg_ref, goti_ref, goto_ref, rcv_i, rcv_o, send_sems, recv_sems):
        def ici(n, q):
            wire, rcv = ((wi_hbm, rcv_i), (wo_hbm, rcv_o))[n]
            return _rcopy(wire.at[_chip_of(_flip(q))], rcv.at[q // 2 - 1], send_sems, recv_sems, 3 * n + q // 2 - 1,
                          _flip(q))

        @pl.when(pl.program_id(0) == 0)
        def _():
            for n in (0, 1):
                for q in (2, 4, 6):
                    ici(n, q).start()

        @pl.when(pl.program_id(0) == 0)
        def _():
            dsh_ref[...] = jnp.zeros_like(dsh_ref)
            dsc_ref[...] = jnp.zeros_like(dsc_ref)
            dg_ref[...] = dgc_ref[...]

        src = dict(a=a_ref, q=q_ref, k=k_ref, v=v_ref, g=g_ref)
        for sub in range(DH_SUBTILES):
            rows = slice(sub * tl // DH_SUBTILES, (sub + 1) * tl // DH_SUBTILES)
            dh = None
            for j, l0, l1, name, s0, s1 in DZ_PIECES:
                part = lax.dot_general(src[name][rows, s0:s1], w_ref[j, :, l0:l1], _NT, preferred_element_type=F32)
                dh = part if dh is None else dh + part
            _, vjp = jax.vjp(_modulated, x_ref[rows, :], gn_ref[...], sc_ref[...], sh_ref[...])
            dx, dg, dsc, dsh = vjp(dh)
            gx_ref[rows, :] = dy_ref[rows, :] + dx
            dg_ref[...] += dg
            dsc_ref[...] += dsc
            dsh_ref[...] += dsh

        @pl.when(pl.program_id(0) == nt - 1)
        def _():
            for n in (0, 1):
                for q in (2, 4, 6):
                    ici(n, q).wait_recv()
                    ici(n, q).wait_send()
            goti_ref[...] = rcv_i[...]
            goto_ref[...] = rcv_o[...]

    tile = pl.BlockSpec((tl, DM), lambda t: (t, 0))
    hbm = pl.BlockSpec(memory_space=pl.ANY)
    got = [(NCHIP - 1, rh, w) for rh, w in RS_SHAPES]
    return pl.pallas_call(
        kern, name="dh_bwd", grid=(nt,),
        in_specs=_dz_specs(tl) + [pl.BlockSpec((NCHIP, DM, SHARD_IN), lambda t: (0, 0, 0)), tile, tile, _row(DM),
                                  _row(DM), _row(DM), _row(DM), hbm, hbm],
        out_specs=[tile, _row(DM), _row(DM), _row(DM)] + [pl.BlockSpec(s, lambda t: (0, 0, 0)) for s in got],
        out_shape=[jax.ShapeDtypeStruct((SEQ, DM), F32)] + [jax.ShapeDtypeStruct((1, DM), F32)] * 3
        + [jax.ShapeDtypeStruct(s, BF16) for s in got],
        scratch_shapes=[pltpu.VMEM(s, BF16) for s in got] + [pltpu.SemaphoreType.DMA((6,)), pltpu.SemaphoreType.DMA((6,))],
        compiler_params=_cparams(("arbitrary",), VMEM_BIG),
    )(*dz_parts, w_full, x, dy, shift, scale, norm_g, dg_ctx, wire_i, wire_o)


def dw_bwd(h, dz_parts, hc, dck, dcv, g_out):
    tl = 512
    nt = SEQ // tl
    (rhi, wi), (rho, wo) = RS_SHAPES

    def kern(h_ref, a_ref, q_ref, k_ref, v_ref, g_ref, hc_ref, dck_ref, dcv_ref, go_hbm,
             wire_i, keep_i, wire_o, keep_o, acc, rcv_i, mine_o, rcv_o, load_sem, send_sems, recv_sems):
        t = pl.program_id(0)
        x, y, c = _me()
        k = 2 * x + y
        sib = _flip(1)
        half = lambda hh, rh: pl.ds(pl.multiple_of(hh * rh, rh), rh)
        load_o = pltpu.make_async_copy(go_hbm.at[:, half(c, rho), :], mine_o, load_sem)
        pair_o = _rcopy(go_hbm.at[:, half(1 - c, rho), :], rcv_o, send_sems, recv_sems, 0, sib)
        pair_i = _rcopy(acc.at[:, half(1 - c, rhi), :], rcv_i, send_sems, recv_sems, 1, sib)

        @pl.when(t == 0)
        def _():
            load_o.start()
            pair_o.start()
            acc[...] = jnp.zeros_like(acc)
            hct = hc_ref[...].T
            csrc = dict(k=dck_ref, v=dcv_ref)
            for j, l0, l1, name, s0, s1 in DZC_PIECES:
                acc[j, :, l0:l1] += jnp.dot(hct, csrc[name][:, s0:s1].astype(BF16), preferred_element_type=F32)

        ht = h_ref[...].T
        src = dict(a=a_ref, q=q_ref, k=k_ref, v=v_ref, g=g_ref)
        for j, l0, l1, name, s0, s1 in DZ_PIECES:
            acc[j, :, l0:l1] += jnp.dot(ht, src[name][:, s0:s1], preferred_element_type=F32)

        @pl.when(t == nt - 1)
        def _():
            pair_i.start()
            load_o.wait()
            pair_o.wait_recv()
            for j in range(NCHIP):
                wire_o[j] = (mine_o[j] + rcv_o[j]).astype(BF16)
            keep_o[...] = mine_o[k] + rcv_o[k]
            pair_i.wait_recv()
            mine = half(c, rhi)
            for j in range(NCHIP):
                wire_i[j] = (acc[j, mine, :] + rcv_i[j]).astype(BF16)
            keep_i[...] = acc[k, mine, :] + rcv_i[k]
            pair_o.wait_send()
            pair_i.wait_send()

    whole = lambda *shape: pl.BlockSpec(shape, lambda t: (0,) * len(shape))
    return pl.pallas_call(
        kern, name="dw_bwd", grid=(nt,),
        in_specs=[pl.BlockSpec((tl, DM), lambda t: (t, 0))] + _dz_specs(tl)
        + [whole(CTX, DM), whole(CTX, 512), whole(CTX, 512), pl.BlockSpec(memory_space=pl.ANY)],
        out_specs=[whole(NCHIP, rhi, wi), whole(rhi, wi), whole(NCHIP, rho, wo), whole(rho, wo)],
        out_shape=[jax.ShapeDtypeStruct((NCHIP, rhi, wi), BF16), jax.ShapeDtypeStruct((rhi, wi), F32),
                   jax.ShapeDtypeStruct((NCHIP, rho, wo), BF16), jax.ShapeDtypeStruct((rho, wo), F32)],
        scratch_shapes=[pltpu.VMEM((NCHIP, DM, SHARD_IN), F32), pltpu.VMEM((NCHIP, rhi, wi), F32),
                        pltpu.VMEM((NCHIP, rho, wo), F32), pltpu.VMEM((NCHIP, rho, wo), F32),
                        pltpu.SemaphoreType.DMA(()), pltpu.SemaphoreType.DMA((2,)), pltpu.SemaphoreType.DMA((2,))],
        compiler_params=_cparams(("arbitrary",), VMEM_BIG),
    )(h, *dz_parts, hc, dck, dcv, g_out)


def ctx_bwd(dck, dcv, w_full, ctx, cshift, cscale, norm_g):
    def kern(dck_ref, dcv_ref, w_ref, c_ref, sh_ref, sc_ref, g_ref, dsh_ref, dsc_ref, dg_ref):
        csrc = dict(k=dck_ref, v=dcv_ref)
        dhc = None
        for j, l0, l1, name, s0, s1 in DZC_PIECES:
            part = lax.dot_general(csrc[name][:, s0:s1].astype(BF16), w_ref[j, :, l0:l1], _NT,
                                   preferred_element_type=F32)
            dhc = part if dhc is None else dhc + part
        _, vjp = jax.vjp(lambda g, sc, sh: _modulated(c_ref[...], g, sc, sh), g_ref[...], sc_ref[...], sh_ref[...])
        dg_ref[...], dsc_ref[...], dsh_ref[...] = vjp(dhc)

    whole = lambda r, c: pl.BlockSpec((r, c), lambda i: (0, 0))
    return pl.pallas_call(
        kern, name="ctx_bwd", grid=(1,),
        in_specs=[whole(CTX, 512), whole(CTX, 512), pl.BlockSpec((NCHIP, DM, SHARD_IN), lambda i: (0, 0, 0)),
                  whole(CTX, DM), _row(DM), _row(DM), _row(DM)],
        out_specs=[_row(DM), _row(DM), _row(DM)],
        out_shape=[jax.ShapeDtypeStruct((1, DM), F32)] * 3,
        compiler_params=_cparams(("arbitrary",), 40 * 1024 * 1024),
    )(dck, dcv, w_full, ctx, cshift, cscale, norm_g)


def _lane_pad_rpb(rpb):
    r = jnp.pad(rpb, ((0, 0), (0, 0), (0, GRID_W - rpb.shape[-1])))
    return jnp.concatenate([r, r], axis=-1)


def local_step(chip, dev, x, c_vec, c_ctx, w_ada, b_shard, ctx, target, norm_g, sgu_g, w_s, b_s, q_g, k_g, rpb,
               w_in_shard, w_out_shard):
    bsb = jnp.broadcast_to(b_s[:, :, None], (4, 128, 128))
    qg2, kg2 = jnp.tile(q_g, (1, 2)), jnp.tile(k_g, (1, 2))

    z, h, w_in_full, w_out_full, mod_all, cs = inproj_fwd(chip, x, c_vec, c_ctx, w_ada, b_shard, norm_g, w_in_shard,
                                                          w_out_shard)
    mods = mod_all.transpose(1, 0, 2).reshape(CS_ROWS, 3 * DM)
    mod = lax.dynamic_slice(mods, (8 * dev, 0), (1, 3 * DM))
    shift, scale, gate = mod[:, :DM], mod[:, DM:2 * DM], mod[:, 2 * DM:]
    cshift, cscale = mods[8 * NDEV:8 * NDEV + 1, :DM], mods[8 * NDEV:8 * NDEV + 1, DM:2 * DM]
    zc, hc = ctx_fwd(ctx, cshift, cscale, norm_g, w_in_full)
    bias = rpb_tables(_lane_pad_rpb(rpb))
    out_a = sgu_fwd(z, sgu_g, w_s, bsb)
    out_b, *saved = attn_fwd(z, zc, bias, qg2, kg2)
    loss8, dy, dcat, dgate, dwo = outproj(out_a, out_b, x, target, gate, w_out_full.reshape(DM, DM))
    dz_a, dsg, dws, dbsb = sgu_bwd(z, sgu_g, w_s, bsb, dcat)
    dq, dk, dv, dbg, dck, dcv, db0, db1, db2, dqg2, dkg2 = attn_bwd(z, zc, qg2, kg2, dcat, saved)
    drpb = rpb_bwd((db0, db1, db2))[:, :, :rpb.shape[-1]]
    dz_parts = (dz_a, dq, dk, dv, dbg)
    dcshift, dcscale, dng_c = ctx_bwd(dck, dcv, w_in_full, ctx, cshift, cscale, norm_g)
    wire_i, keep_i, wire_o, keep_o = dw_bwd(h, dz_parts, hc, dck, dcv, dwo.reshape(NCHIP, SHARD_OUT, DM))
    grad_x, dshift, dscale, dng, got_i, got_o = dh_bwd(dz_parts, w_in_full, x, dy, shift, scale, norm_g, dng_c,
                                                       wire_i, wire_o)
    return dict(
        loss=loss8[0:1, 0:1], grad_x=grad_x, rs=(keep_i, got_i, keep_o, got_o), cs=cs,
        dmod=jnp.concatenate([dshift, dscale, dgate], axis=-1),
        dcmod=jnp.concatenate([dcshift, dcscale, jnp.zeros((1, DM), F32)], axis=-1),
        d_norm_g=dng, d_sgu_g=dsg, d_w_s=dws, d_b_s=dbsb[:, :, 0],
        d_q_g=dqg2[:, :HDIM], d_k_g=dkg2[:, :HDIM], d_rpb=drpb)


def _me():
    return lax.axis_index("x"), lax.axis_index("y"), lax.axis_index("c")


def _flip(q):
    x, y, c = _me()
    return ((1 - x) if q & 4 else x, (1 - y) if q & 2 else y, (1 - c) if q & 1 else c)


def _chip_of(dev):
    return 2 * dev[0] + dev[1]


def _rcopy(src, dst, send_sems, recv_sems, k, dev):
    return pltpu.make_async_remote_copy(src_ref=src, dst_ref=dst, send_sem=send_sems.at[k], recv_sem=recv_sems.at[k],
                                        device_id=dev, device_id_type=MESH_ID)


_VMEM_SPEC = pl.BlockSpec(memory_space=pltpu.VMEM)
SLAB_ROWS = 80


RS_SHAPES = ((DM // 2, SHARD_IN), (SHARD_OUT // 2, DM))


def final_reduce(keep_i, got_i, keep_o, got_o, slab):
    def kern(ki_ref, gi_ref, ko_ref, go_ref, s_ref, gin_ref, gout_ref, all_ref, tot_ref, send_sems, recv_sems):
        x, y, c = _me()
        sib = _flip(1)
        dev = lambda d: 4 * d[0] + 2 * d[1] + d[2]
        me = dev((x, y, c))

        def slab_copy(idx, owner, to):
            return _rcopy(all_ref.at[dev(owner)], all_ref.at[dev(owner)], send_sems, recv_sems, idx, to)

        all_ref[me] = s_ref[...]
        first = [slab_copy(0, (x, y, c), sib)] + [slab_copy(q // 2, (x, y, c), _flip(q)) for q in (2, 4, 6)]
        for cp in first:
            cp.start()

        shares = []
        for n, (keep, got, out) in enumerate(((ki_ref, gi_ref, gin_ref), (ko_ref, go_ref, gout_ref))):
            rh = RS_SHAPES[n][0]
            half = lambda hh, rh=rh: pl.ds(pl.multiple_of(hh * rh, rh), rh)
            out[half(c), :] = ((keep[...] + got[0].astype(F32)) + got[1].astype(F32)) + got[2].astype(F32)
            share = _rcopy(out.at[half(c), :], out.at[half(c), :], send_sems, recv_sems, 7 + n, sib)
            share.start()
            shares.append((share, _rcopy(out.at[half(1 - c), :], out.at[half(1 - c), :], send_sems, recv_sems, 7 + n,
                                         sib)))

        passed = []
        for q in (2, 4, 6):
            slab_copy(q // 2, _flip(q), (x, y, c)).wait_recv()
            cp = slab_copy(3 + q // 2, _flip(q), sib)
            cp.start()
            passed.append(cp)
        slab_copy(0, sib, (x, y, c)).wait_recv()
        for q in (2, 4, 6):
            slab_copy(3 + q // 2, _flip(q | 1), (x, y, c)).wait_recv()
        tot = all_ref[0]
        for d in range(1, NDEV):
            tot = tot + all_ref[d]
        tot_ref[...] = tot
        for share, arrival in shares:
            arrival.wait_recv()
            share.wait_send()
        for cp in first + passed:
            cp.wait_send()

    (rhi, wi), (rho, wo) = RS_SHAPES
    return pl.pallas_call(
        kern, name="final_reduce", in_specs=[_VMEM_SPEC] * 5, out_specs=[_VMEM_SPEC] * 4,
        out_shape=[jax.ShapeDtypeStruct((2 * rhi, wi), F32), jax.ShapeDtypeStruct((2 * rho, wo), F32),
                   jax.ShapeDtypeStruct((NDEV, SLAB_ROWS, DM), F32), jax.ShapeDtypeStruct((SLAB_ROWS, DM), F32)],
        scratch_shapes=[pltpu.SemaphoreType.DMA((9,)), pltpu.SemaphoreType.DMA((9,))],
        compiler_params=pltpu.CompilerParams(vmem_limit_bytes=40 * 1024 * 1024),
    )(keep_i, got_i, keep_o, got_o, slab)


def ada_bwd(a_in, dm, dm_shard, w_ada, c_ctx):
    def kern(a_ref, dm_ref, dms_ref, w_ref, cc_ref, dw_ref, db_ref, dcc_ref, parts, send_sems, recv_sems):
        x, y, c = _me()
        k = 2 * x + y
        act = jax.nn.silu(a_ref[...]).astype(BF16)
        dms = dms_ref[...].astype(BF16)
        dw_ref[...] = lax.dot_general(act, dms, (((0,), (0,)), ((), ())), preferred_element_type=F32)
        db_ref[...] = jnp.sum(dm_ref[...], axis=0, keepdims=True)
        parts[k] = lax.dot_general(dms, w_ref[...].astype(BF16), (((1,), (1,)), ((), ())), preferred_element_type=F32)
        sends = [_rcopy(parts.at[k], parts.at[k], send_sems, recv_sems, q // 2 - 1, _flip(q)) for q in (2, 4, 6)]
        for cp in sends:
            cp.start()
        for q in (2, 4, 6):
            kq = _chip_of(_flip(q))
            _rcopy(parts.at[kq], parts.at[kq], send_sems, recv_sems, q // 2 - 1, _flip(q)).wait_recv()
        dact = ((parts[0] + parts[1]) + parts[2]) + parts[3]
        _, vjp = jax.vjp(jax.nn.silu, cc_ref[...])
        dcc_ref[...] = vjp(dact[8:9, :])[0]
        for cp in sends:
            cp.wait_send()

    return pl.pallas_call(
        kern, name="ada_bwd", in_specs=[_VMEM_SPEC] * 5, out_specs=[_VMEM_SPEC] * 3,
        out_shape=[jax.ShapeDtypeStruct((DM, SHARD_ADA), F32), jax.ShapeDtypeStruct((1, 3 * DM), F32),
                   jax.ShapeDtypeStruct((1, DM), F32)],
        scratch_shapes=[pltpu.VMEM((NCHIP, 16, DM), F32), pltpu.SemaphoreType.DMA((3,)), pltpu.SemaphoreType.DMA((3,))],
    )(a_in, dm, dm_shard, w_ada, c_ctx)


def _adamw_math(w, g, m, v):
    m = B1 * m + (1.0 - B1) * g
    v = B2 * v + (1.0 - B2) * (g * g)
    m_hat = m / (1.0 - B1 ** STEP)
    v_hat = v / (1.0 - B2 ** STEP)
    return -LR * (m_hat / (jnp.sqrt(v_hat) + ADAM_EPS) + WD * w), m, v


def adamw_big(w, g, m, v, name, block_rows=256):
    rows, width = w.shape

    def kern(w_ref, g_ref, m_ref, v_ref, d_ref, nm_ref, nv_ref):
        d_ref[...], nm_ref[...], nv_ref[...] = _adamw_math(w_ref[...], g_ref[...], m_ref[...], v_ref[...])

    spec = pl.BlockSpec((block_rows, width), lambda i: (i, 0))
    return pl.pallas_call(
        kern, name=name, grid=(rows // block_rows,), in_specs=[spec] * 4, out_specs=[spec] * 3,
        out_shape=[jax.ShapeDtypeStruct((rows, width), F32)] * 3,
        compiler_params=_cparams(("arbitrary",)),
    )(w, g, m, v)


def adamw_small(quads):
    n = len(quads)

    def kern(*refs):
        ins, outs = refs[:4 * n], refs[4 * n:]
        for i in range(n):
            w, g, m, v = (r[...] for r in ins[4 * i:4 * i + 4])
            outs[3 * i][...], outs[3 * i + 1][...], outs[3 * i + 2][...] = _adamw_math(w, g, m, v)

    flat = [a for quad in quads for a in quad]
    res = pl.pallas_call(
        kern, name="adamw_small", in_specs=[_VMEM_SPEC] * (4 * n), out_specs=[_VMEM_SPEC] * (3 * n),
        out_shape=[jax.ShapeDtypeStruct(q[0].shape, F32) for q in quads for _ in range(3)],
    )(*flat)
    return [tuple(res[3 * i:3 * i + 3]) for i in range(n)]


def _rows_of(a, rows):
    flat = a.reshape(-1)
    return jnp.pad(flat, (0, rows * DM - flat.shape[0])).reshape(rows, DM)


def kernel(x, c, ctx, c_ctx, w_ada, b_ada, norm_g, w_in, sgu_norm_g, w_spatial, b_spatial, q_norm_g, k_norm_g, rpb, w_out, loss_target, m_c_ctx, m_w_ada, m_b_ada, m_norm_g, m_w_in, m_sgu_norm_g, m_w_spatial, m_b_spatial, m_q_norm_g, m_k_norm_g, m_rpb, m_w_out, v_c_ctx, v_w_ada, v_b_ada, v_norm_g, v_w_in, v_sgu_norm_g, v_w_spatial, v_b_spatial, v_q_norm_g, v_k_norm_g, v_rpb, v_w_out):
    xi, yi, ci = lax.axis_index("x"), lax.axis_index("y"), lax.axis_index("c")
    chip, dev = 2 * xi + yi, 4 * xi + 2 * yi + ci
    c_ctx2 = c_ctx.reshape(1, DM)

    b_shard = lax.dynamic_slice(b_ada, (0, chip * SHARD_ADA), (1, SHARD_ADA))
    part = local_step(chip.reshape(1).astype(jnp.int32), dev, x[0], c, c_ctx2, w_ada[0], b_shard, ctx[0], loss_target[0],
                      norm_g, sgu_norm_g, w_spatial[0], b_spatial[0], q_norm_g, k_norm_g, rpb[0], w_in[0], w_out[0])
    cs = part["cs"]

    slab = jnp.concatenate([
        part["d_norm_g"], _rows_of(part["d_sgu_g"], 1), _rows_of(part["d_b_s"], 1),
        _rows_of(jnp.concatenate([part["d_q_g"], part["d_k_g"]], axis=-1), 1), _rows_of(part["d_rpb"], 4),
        _rows_of(part["loss"], 1), _rows_of(part["dcmod"], 3), _rows_of(part["dmod"], 3), jnp.zeros((1, DM), F32),
        _rows_of(part["d_w_s"], 64)], axis=0)
    g_w_in, g_w_out, gathered, tot = final_reduce(*part["rs"], slab)
    dm = jnp.concatenate([gathered[:, 12:15, :].reshape(NDEV, 3 * DM), tot[9:12].reshape(1, 3 * DM),
                          jnp.zeros((7, 3 * DM), F32)], axis=0)
    a_in = jnp.concatenate([cs[0:8 * NDEV:8], cs[8 * NDEV:8 * NDEV + 1], jnp.zeros((7, DM), F32)], axis=0)
    dm_shard = lax.dynamic_slice(dm, (0, chip * SHARD_ADA), (16, SHARD_ADA))
    g_w_ada, g_b_ada, g_c_ctx = ada_bwd(a_in, dm, dm_shard, w_ada[0], c_ctx2)

    loss = tot[8, 0]
    g_small = dict(
        c_ctx=g_c_ctx, b_ada=g_b_ada, norm_g=tot[0:1], sgu_norm_g=tot[1:2, :512], w_spatial=tot[16:80].reshape(512, 128),
        b_spatial=tot[2:3, :512].reshape(4, 128), q_norm_g=tot[3:4, :HDIM], k_norm_g=tot[3:4, HDIM:2 * HDIM],
        rpb=tot[4:8].reshape(-1)[:HEADS * 15 * 31].reshape(HEADS * 15, 31))
    shapes = dict(c_ctx=(DM,), w_ada=(1, DM, SHARD_ADA), b_ada=(1, 3 * DM), norm_g=(1, DM), w_in=(1, DM, SHARD_IN),
                  sgu_norm_g=(1, 512), w_spatial=(1, 4, 128, 128), b_spatial=(1, 4, 128), q_norm_g=(1, HDIM),
                  k_norm_g=(1, HDIM), rpb=(1, HEADS, 15, 31), w_out=(1, SHARD_OUT, DM))
    names = list(shapes)
    weights = dict(c_ctx=c_ctx, w_ada=w_ada, b_ada=b_ada, norm_g=norm_g, w_in=w_in, sgu_norm_g=sgu_norm_g,
                   w_spatial=w_spatial, b_spatial=b_spatial, q_norm_g=q_norm_g, k_norm_g=k_norm_g, rpb=rpb, w_out=w_out)
    m_in = dict(zip(names, (m_c_ctx, m_w_ada, m_b_ada, m_norm_g, m_w_in, m_sgu_norm_g, m_w_spatial, m_b_spatial,
                            m_q_norm_g, m_k_norm_g, m_rpb, m_w_out)))
    v_in = dict(zip(names, (v_c_ctx, v_w_ada, v_b_ada, v_norm_g, v_w_in, v_sgu_norm_g, v_w_spatial, v_b_spatial,
                            v_q_norm_g, v_k_norm_g, v_rpb, v_w_out)))
    grads = dict(g_small, w_ada=g_w_ada, w_in=g_w_in, w_out=g_w_out)
    upd = {}
    for n in ("w_ada", "w_in", "w_out"):
        g = grads[n]
        upd[n] = adamw_big(weights[n].reshape(g.shape), g, m_in[n].reshape(g.shape), v_in[n].reshape(g.shape),
                           "adamw_" + n)
    small = [n for n in names if n not in upd]
    res = adamw_small([(weights[n].reshape(grads[n].shape), grads[n], m_in[n].reshape(grads[n].shape),
                        v_in[n].reshape(grads[n].shape)) for n in small])
    upd.update(zip(small, res))
    out = [loss, part["grad_x"].reshape(1, SEQ, DM)]
    out += [grads[n].reshape(shapes[n]) for n in names]
    for slot in range(3):
        out += [upd[n][slot].reshape(shapes[n]) for n in names]
    return tuple(out)
```

```python
import jax
import jax.numpy as jnp
from jax import lax
from jax.experimental import pallas as pl
from jax.experimental.pallas import tpu as pltpu

F32, BF16 = jnp.float32, jnp.bfloat16
SEQ, DM, CTX, DIN = 4096, 1024, 256, 3584
NCHIP, NDEV = 4, 8
SHARD_IN = DIN // NCHIP
SHARD_ADA = 3 * DM // NCHIP
SHARD_OUT = DM // NCHIP
GRID_W = 64
QROWS = 4
KROWS = 12
QBLK, KBLK = QROWS * GRID_W, KROWS * GRID_W
NQBLK = SEQ // QBLK
HEADS, HDIM, NPAIR = 8, 64, 4
EPS = 1e-6
NEG_INF = -1e30
ZQ, ZK, ZV, ZG = 12, 16, 20, 24
LR, B1, B2, ADAM_EPS, WD, STEP = 0.001, 0.9, 0.999, 1e-08, 0.01, 10
VMEM_BIG = 56 * 1024 * 1024
MESH_ID = pl.DeviceIdType.MESH


def _dot(a, b, lhs_c, rhs_c):
    return lax.dot_general(a.astype(BF16), b.astype(BF16), (((lhs_c,), (rhs_c,)), ((), ())),
                           preferred_element_type=F32)


@jax.custom_vjp
def mm(a, b):
    return _dot(a, b, 1, 0)


@jax.custom_vjp
def mm_nt(a, b):
    return _dot(a, b, 1, 1)


@jax.custom_vjp
def mm_tn(a, b):
    return _dot(a, b, 0, 0)


mm.defvjp(lambda a, b: (mm(a, b), (a, b)), lambda r, ct: (mm_nt(ct, r[1]), mm_tn(r[0], ct)))
mm_nt.defvjp(lambda a, b: (mm_nt(a, b), (a, b)), lambda r, ct: (mm(ct, r[1]), mm_tn(ct, r[0])))
mm_tn.defvjp(lambda a, b: (mm_tn(a, b), (a, b)), lambda r, ct: (mm_nt(r[1], ct), mm(r[0], ct)))


def _rms(x, g):
    return x * lax.rsqrt(jnp.mean(x * x, axis=-1, keepdims=True) + EPS) * g


def _modulated(x, g, scale, shift):
    return _rms(x, g) * (1.0 + scale) + shift


def _pair_rms(x, g2):
    lo = lax.broadcasted_iota(jnp.int32, (1, 2 * HDIM), 1) < HDIM
    sq = x * x
    s_lo = jnp.sum(jnp.where(lo, sq, 0.0), axis=-1, keepdims=True)
    s_hi = jnp.sum(jnp.where(lo, 0.0, sq), axis=-1, keepdims=True)
    rs = jnp.where(lo, lax.rsqrt(s_lo / HDIM + EPS), lax.rsqrt(s_hi / HDIM + EPS))
    return x * rs * g2


def _cparams(sem, vmem=None):
    return pltpu.CompilerParams(dimension_semantics=sem, vmem_limit_bytes=vmem)


def _row(n):
    return pl.BlockSpec((1, n), lambda *_: (0, 0))


CS_ROWS = 8 * NDEV + 8


def _mod_part(mod_ref, row, part):
    pieces = []
    for j in range(NCHIP):
        lo, hi = max(part * DM, j * SHARD_ADA), min((part + 1) * DM, (j + 1) * SHARD_ADA)
        if lo < hi:
            pieces.append(mod_ref[j, row, lo - j * SHARD_ADA:hi - j * SHARD_ADA])
    return jnp.concatenate(pieces, axis=-1)


def inproj_fwd(chip, x, c_vec, c_ctx, w_ada, b_shard, norm_g, w_shard, wo_shard):
    tl = 1024
    nt = SEQ // tl
    halves = (DM // 2, SHARD_OUT // 2)
    n_w, n_c = 12, NDEV - 1

    def kern(k_ref, x_ref, cv_ref, cc_ref, wa_ref, b_ref, g_ref, w_ref, wo_ref,
             z_ref, h_ref, wfull_ref, wofull_ref, modall_ref, csall_ref,
             w_scr, wo_scr, h_scr, mine, cs_scr, mod_scr, shsc_scr, send_sems, recv_sems):
        s, t = pl.program_id(0), pl.program_id(1)
        xi, yi, c = _me()
        k, me = 2 * xi + yi, 4 * xi + 2 * yi + c
        sib = _flip(1)
        rows = pl.ds(pl.multiple_of(t * tl, tl), tl)
        gathered = (w_scr, wo_scr)
        slot = lambda d: pl.ds(pl.multiple_of(8 * d, 8), 8)

        def c_copy(q, owner):
            return _rcopy(mine, cs_scr.at[slot(owner), :], send_sems, recv_sems, n_w + q - 1, _flip(q))

        def m_copy(q, chip_of_block):
            return _rcopy(mod_scr.at[chip_of_block], mod_scr.at[chip_of_block], send_sems, recv_sems,
                          n_w + n_c + q // 2 - 1, _flip(q))

        def adaln():
            first = lax.broadcasted_iota(jnp.int32, (8, DM), 0) == 0
            mine[...] = jnp.where(first, jnp.broadcast_to(cv_ref[...], (8, DM)), 0.0)
            cs_scr[slot(me), :] = mine[...]
            cs_scr[slot(NDEV), :] = jnp.where(first, jnp.broadcast_to(cc_ref[...], (8, DM)), 0.0)
            for q in range(1, NDEV):
                c_copy(q, me).start()
            wa = wa_ref[...].astype(BF16)
            for q in range(1, NDEV):
                px, py, pc = _flip(q)
                c_copy(q, 4 * px + 2 * py + pc).wait_recv()
            act = jax.nn.silu(cs_scr[...]).astype(BF16)
            mod_scr[k] = jnp.dot(act, wa, preferred_element_type=F32) + b_ref[...]
            for q in (2, 4, 6):
                m_copy(q, k).start()
            for q in (2, 4, 6):
                m_copy(q, _chip_of(_flip(q))).wait_recv()
            row = pl.ds(8 * me, 1)
            shsc_scr[0:1, :] = _mod_part(mod_scr, row, 0)
            shsc_scr[1:2, :] = _mod_part(mod_scr, row, 1)
            pltpu.sync_copy(mod_scr, modall_ref)
            pltpu.sync_copy(cs_scr, csall_ref)

        def block(n, chip_of_block, hh):
            return gathered[n].at[chip_of_block, pl.ds(pl.multiple_of(hh * halves[n], halves[n]), halves[n]), :]

        def ici(n, q, chip_of_block):
            blk = block(n, chip_of_block, c)
            return _rcopy(blk, blk, send_sems, recv_sems, 6 * n + q // 2 - 1, _flip(q))

        def d2d(n, q, chip_of_block, hh):
            blk = block(n, chip_of_block, hh)
            return _rcopy(blk, blk, send_sems, recv_sems, 6 * n + 3 + q // 2 - 1, sib)

        @pl.when((s == 0) & (t == 0))
        def _():
            adaln()
            w_scr[k] = w_ref[...].astype(BF16)
            wo_scr[k] = wo_ref[...].astype(BF16)
            for q in (2, 4, 6):
                ici(0, q, k).start()
                ici(1, q, k).start()

        for sweep in (1, 2, 3):
            @pl.when((s == sweep) & (t == 0))
            def _():
                q = 2 * sweep
                src = _chip_of(_flip(q))
                for n in (0, 1):
                    ici(n, q, src).wait_recv()
                    d2d(n, q, src, c).start()
                for n in (0, 1):
                    d2d(n, q, src, 1 - c).wait_recv()

        @pl.when(s == 0)
        def _():
            hb = _modulated(x_ref[...], g_ref[...], shsc_scr[1:2, :], shsc_scr[0:1, :]).astype(BF16)
            h_scr[rows, :] = hb
            h_ref[...] = hb

        z_ref[...] = jnp.dot(h_scr[rows, :], w_scr[lax.bitwise_xor(k, s)], preferred_element_type=F32)

        @pl.when((s == NCHIP - 1) & (t == nt - 1))
        def _():
            for q in range(1, NDEV):
                c_copy(q, me).wait_send()
            for q in (2, 4, 6):
                m_copy(q, k).wait_send()
            for n in (0, 1):
                for q in (2, 4, 6):
                    ici(n, q, k).wait_send()
                    d2d(n, q, _chip_of(_flip(q)), c).wait_send()
            pltpu.sync_copy(w_scr, wfull_ref)
            pltpu.sync_copy(wo_scr, wofull_ref)

    once = lambda s, t, k: (jnp.where(s == 0, t, nt - 1), 0)
    hbm = pl.BlockSpec(memory_space=pl.ANY)
    n_sem = n_w + n_c + 3
    return pl.pallas_call(
        kern, name="inproj_fwd",
        grid_spec=pltpu.PrefetchScalarGridSpec(
            num_scalar_prefetch=1, grid=(NCHIP, nt),
            in_specs=[pl.BlockSpec((tl, DM), once)] + [_VMEM_SPEC] * 7,
            out_specs=[pl.BlockSpec((tl, SHARD_IN), lambda s, t, k: (t, lax.bitwise_xor(k[0], s))),
                       pl.BlockSpec((tl, DM), once), hbm, hbm, hbm, hbm],
            scratch_shapes=[pltpu.VMEM((NCHIP, DM, SHARD_IN), BF16), pltpu.VMEM((NCHIP, SHARD_OUT, DM), BF16),
                            pltpu.VMEM((SEQ, DM), BF16), pltpu.VMEM((8, DM), F32), pltpu.VMEM((CS_ROWS, DM), F32),
                            pltpu.VMEM((NCHIP, CS_ROWS, SHARD_ADA), F32), pltpu.VMEM((8, DM), F32),
                            pltpu.SemaphoreType.DMA((n_sem,)), pltpu.SemaphoreType.DMA((n_sem,))]),
        out_shape=[jax.ShapeDtypeStruct((SEQ, DIN), F32), jax.ShapeDtypeStruct((SEQ, DM), BF16),
                   jax.ShapeDtypeStruct((NCHIP, DM, SHARD_IN), BF16), jax.ShapeDtypeStruct((NCHIP, SHARD_OUT, DM), BF16),
                   jax.ShapeDtypeStruct((NCHIP, CS_ROWS, SHARD_ADA), F32), jax.ShapeDtypeStruct((CS_ROWS, DM), F32)],
        compiler_params=_cparams(("arbitrary", "arbitrary"), VMEM_BIG),
    )(chip, x, c_vec, c_ctx, w_ada, b_shard, norm_g, w_shard, wo_shard)


def ctx_fwd(ctx, cshift, cscale, norm_g, w_full):
    def kern(c_ref, sh_ref, sc_ref, g_ref, w2_ref, w3_ref, zc_ref, hc_ref):
        hc = _modulated(c_ref[...], g_ref[...], sc_ref[...], sh_ref[...]).astype(BF16)
        hc_ref[...] = hc
        zc_ref[:, :SHARD_IN] = jnp.dot(hc, w2_ref[0], preferred_element_type=F32)
        zc_ref[:, SHARD_IN:] = jnp.dot(hc, w3_ref[0], preferred_element_type=F32)

    return pl.pallas_call(
        kern, name="ctx_fwd", grid=(1,),
        in_specs=[pl.BlockSpec((CTX, DM), lambda i: (0, 0)), _row(DM), _row(DM), _row(DM),
                  pl.BlockSpec((1, DM, SHARD_IN), lambda i: (2, 0, 0)),
                  pl.BlockSpec((1, DM, SHARD_IN), lambda i: (3, 0, 0))],
        out_specs=[pl.BlockSpec((CTX, 2 * SHARD_IN), lambda i: (0, 0)),
                   pl.BlockSpec((CTX, DM), lambda i: (0, 0))],
        out_shape=[jax.ShapeDtypeStruct((CTX, 2 * SHARD_IN), F32), jax.ShapeDtypeStruct((CTX, DM), BF16)],
        compiler_params=_cparams(("arbitrary",)),
    )(ctx, cshift, cscale, norm_g, w_full, w_full)


SGU_CHUNK, SGU_PER_STEP = 128, 4


def _gelu(x):
    return 0.5 * x * (1.0 + lax.erf(x * 0.7071067811865476))


def _sgu_chunk(au, av, ag, sg, ws, bsb):
    u, v = _gelu(au), _gelu(av)
    outs = []
    for g in range(4):
        sl = slice(128 * g, 128 * (g + 1))
        mixed = mm(ws[g], _rms(v[:, sl], sg[:, sl])) + bsb[g]
        outs.append(u[:, sl] * mixed * jax.nn.silu(ag[:, sl]))
    return jnp.concatenate(outs, axis=-1)


def _sgu_specs():
    rows = SGU_CHUNK * SGU_PER_STEP
    zspec = lambda c: pl.BlockSpec((rows, 512), lambda n: (n, c))
    wspec = pl.BlockSpec((4, 128, 128), lambda n: (0, 0, 0))
    return rows, [zspec(0), zspec(1), zspec(2), _row(512), wspec, wspec]


def sgu_fwd(z, sg, ws, bsb):
    rows, in_specs = _sgu_specs()

    def kern(au_ref, av_ref, ag_ref, sg_ref, ws_ref, bs_ref, o_ref):
        for c in range(SGU_PER_STEP):
            sl = slice(c * SGU_CHUNK, (c + 1) * SGU_CHUNK)
            o_ref[sl, :] = _sgu_chunk(au_ref[sl, :], av_ref[sl, :], ag_ref[sl, :], sg_ref[...], ws_ref[...],
                                      bs_ref[...])

    return pl.pallas_call(
        kern, name="sgu_fwd", grid=(SEQ // rows,), in_specs=in_specs,
        out_specs=pl.BlockSpec((rows, 512), lambda n: (n, 0)),
        out_shape=jax.ShapeDtypeStruct((SEQ, 512), F32),
        compiler_params=_cparams(("arbitrary",)),
    )(z, z, z, sg, ws, bsb)


def sgu_bwd(z, sg, ws, bsb, dcat):
    rows, in_specs = _sgu_specs()

    def kern(au_ref, av_ref, ag_ref, sg_ref, ws_ref, bs_ref, do_ref, dz_ref, dsg_ref, dws_ref, dbs_ref):
        @pl.when(pl.program_id(0) == 0)
        def _():
            dsg_ref[...] = jnp.zeros_like(dsg_ref)
            dws_ref[...] = jnp.zeros_like(dws_ref)
            dbs_ref[...] = jnp.zeros_like(dbs_ref)

        for c in range(SGU_PER_STEP):
            sl = slice(c * SGU_CHUNK, (c + 1) * SGU_CHUNK)
            _, vjp = jax.vjp(_sgu_chunk, au_ref[sl, :], av_ref[sl, :], ag_ref[sl, :], sg_ref[...], ws_ref[...],
                             bs_ref[...])
            dau, dav, dag, dsg, dws, dbs = vjp(do_ref[sl, :])
            dz_ref[sl, 0:512] = dau.astype(BF16)
            dz_ref[sl, 512:1024] = dav.astype(BF16)
            dz_ref[sl, 1024:1536] = dag.astype(BF16)
            dsg_ref[...] += dsg
            dws_ref[...] += dws
            dbs_ref[...] += dbs

        @pl.when(pl.program_id(0) == pl.num_programs(0) - 1)
        def _():
            dbs_ref[...] = jnp.broadcast_to(jnp.sum(dbs_ref[...], axis=-1, keepdims=True), dbs_ref.shape)

    wspec = pl.BlockSpec((4, 128, 128), lambda n: (0, 0, 0))
    return pl.pallas_call(
        kern, name="sgu_bwd", grid=(SEQ // rows,),
        in_specs=in_specs + [pl.BlockSpec((rows, 512), lambda n: (n, 0))],
        out_specs=[pl.BlockSpec((rows, 1536), lambda n: (n, 0)), _row(512), wspec, wspec],
        out_shape=[jax.ShapeDtypeStruct((SEQ, 1536), BF16), jax.ShapeDtypeStruct((1, 512), F32),
                   jax.ShapeDtypeStruct((4, 128, 128), F32), jax.ShapeDtypeStruct((4, 128, 128), F32)],
        compiler_params=_cparams(("arbitrary",)),
    )(z, z, z, sg, ws, bsb, dcat)


_DR_OFF = (7, 3, -1)


def _row_valid(v, rr, j):
    return (j < 8, rr <= j < rr + 8, 4 <= j < 12)[v]


def _col_window():
    q = lax.broadcasted_iota(jnp.int32, (GRID_W, 128), 0)
    kc = lax.broadcasted_iota(jnp.int32, (GRID_W, 128), 1) % GRID_W
    c0 = jnp.clip(q - 8, 0, GRID_W - 16)
    return (kc >= c0) & (kc < c0 + 16)


def rpb_tables(rpb2):
    def kern(r_ref, b_ref):
        base = r_ref[0]
        lo = lax.broadcasted_iota(jnp.int32, (1, 128), 1) < GRID_W
        win = _col_window()
        tiles = {}
        for v in range(3):
            for rr in range(QROWS):
                for jp in range(KROWS // 2):
                    j0, j1 = 2 * jp, 2 * jp + 1
                    ok0, ok1 = _row_valid(v, rr, j0), _row_valid(v, rr, j1)
                    key = (j0 - rr + _DR_OFF[v], ok0, ok1) if (ok0 or ok1) else None
                    if key not in tiles:
                        if key is None:
                            tiles[key] = jnp.full((GRID_W, 128), NEG_INF, F32)
                        else:
                            d0 = key[0]
                            r0 = base[d0:d0 + 1, :] if ok0 else jnp.zeros((1, 128), F32)
                            r1 = base[d0 + 1:d0 + 2, :] if ok1 else jnp.zeros((1, 128), F32)
                            y = jnp.broadcast_to(jnp.where(lo, r0, r1), (GRID_W, 128))
                            y = pltpu.roll(pltpu.roll(y, 128 - 15, 1), 0, 1, stride=1, stride_axis=0)
                            tiles[key] = jnp.where(win & jnp.where(lo, ok0, ok1), y, NEG_INF)
                    b_ref[v, 0, rr * GRID_W:(rr + 1) * GRID_W, jp * 128:(jp + 1) * 128] = tiles[key]

    return pl.pallas_call(
        kern, name="rpb_tables", grid=(HEADS,),
        in_specs=[pl.BlockSpec((1, 15, 128), lambda h: (h, 0, 0))],
        out_specs=pl.BlockSpec((3, 1, QBLK, KBLK), lambda h: (0, h, 0, 0)),
        out_shape=jax.ShapeDtypeStruct((3, HEADS, QBLK, KBLK), F32),
        compiler_params=_cparams(("arbitrary",)),
    )(rpb2)


def rpb_bwd(dbias):
    def kern(g0_ref, g1_ref, g2_ref, o_ref):
        g_refs = (g0_ref, g1_ref, g2_ref)
        lo = lax.broadcasted_iota(jnp.int32, (1, 128), 1) < GRID_W
        ri = lax.broadcasted_iota(jnp.int32, (GRID_W, GRID_W), 0)
        ci = lax.broadcasted_iota(jnp.int32, (GRID_W, GRID_W), 1)
        flip = (ri + ci == GRID_W - 1).astype(F32)
        groups = {}
        for v in range(3):
            for rr in range(QROWS):
                for jp in range(KROWS // 2):
                    j0, j1 = 2 * jp, 2 * jp + 1
                    ok0, ok1 = _row_valid(v, rr, j0), _row_valid(v, rr, j1)
                    if not (ok0 or ok1):
                        continue
                    g = g_refs[v][0, rr * GRID_W:(rr + 1) * GRID_W, jp * 128:(jp + 1) * 128]
                    key = (j0 - rr + _DR_OFF[v], ok0, ok1)
                    groups[key] = g if key not in groups else groups[key] + g
        acc = [jnp.zeros((1, 128), F32) for _ in range(15)]
        for (d0, ok0, ok1), g in groups.items():
            g = lax.dot_general(flip, g, (((1,), (0,)), ((), ())), precision=lax.Precision.HIGHEST,
                                preferred_element_type=F32)
            g = pltpu.roll(pltpu.roll(g, 128 - 48, 1), 0, 1, stride=1, stride_axis=0)
            s = jnp.sum(g, axis=0, keepdims=True)
            if ok0:
                acc[d0] = acc[d0] + jnp.where(lo, s, 0.0)
            if ok1:
                acc[d0 + 1] = acc[d0 + 1] + jnp.where(lo, 0.0, s)
        for d in range(15):
            o_ref[0, d:d + 1, :] = acc[d] + pltpu.roll(acc[d], GRID_W, 1)

    return pl.pallas_call(
        kern, name="rpb_bwd", grid=(HEADS,),
        in_specs=[pl.BlockSpec((1, QBLK, KBLK), lambda h: (h, 0, 0))] * 3,
        out_specs=pl.BlockSpec((1, 15, 128), lambda h: (h, 0, 0)),
        out_shape=jax.ShapeDtypeStruct((HEADS, 15, 128), F32),
        compiler_params=_cparams(("arbitrary",)),
    )(*dbias)


def _scaled_q(q_raw, qg):
    return _pair_rms(q_raw, qg) * (HDIM ** -0.5)


def _head_lanes():
    lo = lax.broadcasted_iota(jnp.int32, (1, 2 * HDIM), 1) < HDIM
    return lo, jnp.logical_not(lo)


def _attn_step(q_raw, kn, v, ckn, cv, bias2, qg):
    qn = _scaled_q(q_raw, qg)
    out = rden = None
    probs = []
    for a, mine in enumerate(_head_lanes()):
        qa = jnp.where(mine, qn, 0.0)
        s_lat = mm_nt(qa, kn) + bias2[a]
        s_ctx = mm_nt(qa, ckn)
        m = jnp.maximum(jnp.max(s_lat, axis=-1, keepdims=True), jnp.max(s_ctx, axis=-1, keepdims=True))
        p_lat = jnp.exp(s_lat - m)
        p_ctx = jnp.exp(s_ctx - m)
        den = jnp.sum(p_lat, axis=-1, keepdims=True) + jnp.sum(p_ctx, axis=-1, keepdims=True)
        p_lat, p_ctx = p_lat.astype(BF16), p_ctx.astype(BF16)
        o = jnp.where(mine, (mm(p_lat, v) + mm(p_ctx, cv)) / den, 0.0)
        rr = jnp.where(mine, 1.0 / den, 0.0)
        out, rden = (o, rr) if out is None else (out + o, rden + rr)
        probs.append((p_lat, p_ctx))
    return out, rden, probs


def _attn_step_bwd(q_raw, kn, v, ckn, cv, qg, bg, o, rden, probs, dout):
    sig = jax.nn.sigmoid(bg)
    do = dout * (bg * sig)
    dbg = dout * o * (sig * (1.0 + bg * (1.0 - sig)))
    qn, qn_vjp = jax.vjp(_scaled_q, q_raw, qg)
    row_dot = do * o
    dqn = dkn = dv = dckn = dcv = None
    dbias = []
    for mine, (p_lat, p_ctx) in zip(_head_lanes(), probs):
        qa = jnp.where(mine, qn, 0.0)
        r = jnp.max(jnp.where(mine, rden, 0.0), axis=-1, keepdims=True)
        doa = jnp.where(mine, do, 0.0) * r
        delta = jnp.sum(jnp.where(mine, row_dot, 0.0), axis=-1, keepdims=True) * r
        ds_lat = p_lat.astype(F32) * (mm_nt(doa, v) - delta)
        ds_ctx = p_ctx.astype(F32) * (mm_nt(doa, cv) - delta)
        parts = (jnp.where(mine, mm(ds_lat, kn) + mm(ds_ctx, ckn), 0.0), mm_tn(ds_lat, qa), mm_tn(p_lat, doa),
                 mm_tn(ds_ctx, qa), mm_tn(p_ctx, doa))
        if dqn is None:
            dqn, dkn, dv, dckn, dcv = parts
        else:
            dqn, dkn, dv, dckn, dcv = (acc + new for acc, new in zip((dqn, dkn, dv, dckn, dcv), parts))
        dbias.append(ds_lat)
    dq, dqg = qn_vjp(dqn)
    return dq, dkn, dv, dckn, dcv, dbias, dqg, dbg


def _kstart(i):
    return pl.multiple_of(jnp.clip((i - 1) * QBLK, 0, SEQ - KBLK), QBLK)


ATTN_STEPS = NQBLK // 2
ATTN_ROWS = 2 * QBLK


def _attn_in_specs():
    return [
        pl.BlockSpec((ATTN_ROWS, 128), lambda p, i: (i, ZQ + p)),
        pl.BlockSpec((SEQ, 128), lambda p, i: (0, ZK + p)),
        pl.BlockSpec((SEQ, 128), lambda p, i: (0, ZV + p)),
        pl.BlockSpec((ATTN_ROWS, 128), lambda p, i: (i, ZG + p)),
        pl.BlockSpec((CTX, 128), lambda p, i: (0, 2 + p)),
        pl.BlockSpec((CTX, 128), lambda p, i: (0, 6 + p)),
    ]


def _bias_specs():
    bias_spec = lambda variant: pl.BlockSpec((1, 2, QBLK, KBLK), lambda p, i: (variant(i), p, 0, 0))
    return [bias_spec(lambda i: jnp.where(i == 0, 0, 1)),
            bias_spec(lambda i: jnp.where(i == ATTN_STEPS - 1, 2, 1))]


def _prob_specs():
    return [pl.BlockSpec((2, ATTN_ROWS, KBLK), lambda p, i: (p, i, 0)),
            pl.BlockSpec((2, ATTN_ROWS, CTX), lambda p, i: (p, i, 0))]


NORM_ROWS = 512


def _norm_keys(k_ref, ck_ref, kg_ref, kn_scr, ckn_scr):
    def body(c, carry):
        sl = pl.ds(pl.multiple_of(c * NORM_ROWS, NORM_ROWS), NORM_ROWS)
        kn_scr[sl, :] = _pair_rms(k_ref[sl, :], kg_ref[...])
        return carry

    lax.fori_loop(0, SEQ // NORM_ROWS, body, 0)
    ckn_scr[...] = _pair_rms(ck_ref[...], kg_ref[...])


def attn_fwd(z, zc, bias, qg2, kg2):
    def kern(q_ref, k_ref, v_ref, bg_ref, ck_ref, cv_ref, be_ref, bo_ref, qg_ref, kg_ref,
             ob_ref, o_ref, rden_ref, pl_ref, pc_ref, kn_scr, ckn_scr):
        i = pl.program_id(1)

        @pl.when(i == 0)
        def _():
            _norm_keys(k_ref, ck_ref, kg_ref, kn_scr, ckn_scr)

        for b, b_ref in enumerate((be_ref, bo_ref)):
            rows = slice(b * QBLK, (b + 1) * QBLK)
            ks = pl.ds(_kstart(2 * i + b), KBLK)
            o, rden, probs = _attn_step(q_ref[rows, :], kn_scr[ks, :], v_ref[ks, :], ckn_scr[...], cv_ref[...],
                                        b_ref[0], qg_ref[...])
            ob_ref[rows, :] = o * jax.nn.silu(bg_ref[rows, :])
            o_ref[rows, :] = o
            rden_ref[rows, :] = rden
            for a, (p_lat, p_ctx) in enumerate(probs):
                pl_ref[a, rows, :] = p_lat
                pc_ref[a, rows, :] = p_ctx

    qblk = pl.BlockSpec((ATTN_ROWS, 128), lambda p, i: (i, p))
    return pl.pallas_call(
        kern, name="attn_fwd", grid=(NPAIR, ATTN_STEPS),
        in_specs=_attn_in_specs() + _bias_specs() + [_row(128), _row(128)], out_specs=[qblk] * 3 + _prob_specs(),
        out_shape=[jax.ShapeDtypeStruct((SEQ, 512), F32)] * 3
        + [jax.ShapeDtypeStruct((HEADS, SEQ, KBLK), BF16), jax.ShapeDtypeStruct((HEADS, SEQ, CTX), BF16)],
        scratch_shapes=[pltpu.VMEM((SEQ, 128), F32), pltpu.VMEM((CTX, 128), F32)],
        compiler_params=_cparams(("arbitrary", "arbitrary"), 40 * 1024 * 1024),
    )(z, z, z, z, zc, zc, bias, bias, qg2, kg2)


def attn_bwd(z, zc, qg2, kg2, dcat, saved):
    def kern(q_ref, k_ref, v_ref, bg_ref, ck_ref, cv_ref, qg_ref, kg_ref, do_ref, o_ref, rden_ref, pl_ref, pc_ref,
             dq_ref, dk_ref, dv_ref, dbg_ref, dck_ref, dcv_ref, db0_ref, db1_ref, db2_ref, dqg_ref, dkg_ref,
             kn_scr, ckn_scr, dkn_scr, dckn_scr, dv_scr):
        p, i = pl.program_id(0), pl.program_id(1)
        last = i == ATTN_STEPS - 1

        @pl.when(i == 0)
        def _():
            _norm_keys(k_ref, ck_ref, kg_ref, kn_scr, ckn_scr)
            dkn_scr[...] = jnp.zeros_like(dkn_scr)
            dv_scr[...] = jnp.zeros_like(dv_scr)
            dckn_scr[...] = jnp.zeros_like(dckn_scr)
            dcv_ref[...] = jnp.zeros_like(dcv_ref)

        @pl.when((i == 0) & (p == 0))
        def _():
            dqg_ref[...] = jnp.zeros_like(dqg_ref)
            dkg_ref[...] = jnp.zeros_like(dkg_ref)

        db = []
        for b in range(2):
            rows = slice(b * QBLK, (b + 1) * QBLK)
            ks = pl.ds(_kstart(2 * i + b), KBLK)
            probs = [(pl_ref[a, rows, :], pc_ref[a, rows, :]) for a in range(2)]
            dq, dkn, dv, dckn, dcv, dbb, dqg, dbg = _attn_step_bwd(
                q_ref[rows, :], kn_scr[ks, :], v_ref[ks, :], ckn_scr[...], cv_ref[...], qg_ref[...],
                bg_ref[rows, :], o_ref[rows, :], rden_ref[rows, :], probs, do_ref[rows, :])
            dq_ref[rows, :] = dq.astype(BF16)
            dbg_ref[rows, :] = dbg.astype(BF16)
            dkn_scr[ks, :] += dkn
            dv_scr[ks, :] += dv
            dckn_scr[...] += dckn
            dcv_ref[...] += dcv
            dqg_ref[...] += dqg
            db.append(dbb)

        @pl.when(i == 0)
        def _():
            for a in range(2):
                db0_ref[a] = db[0][a]
                db1_ref[a] = db[1][a]

        @pl.when((i > 0) & jnp.logical_not(last))
        def _():
            for a in range(2):
                db1_ref[a] += db[0][a] + db[1][a]

        @pl.when(last)
        def _():
            for a in range(2):
                db1_ref[a] += db[0][a]
                db2_ref[a] = db[1][a]

        @pl.when(last)
        def _():
            def body(c, dkg):
                sl = pl.ds(pl.multiple_of(c * NORM_ROWS, NORM_ROWS), NORM_ROWS)
                _, nvjp = jax.vjp(_pair_rms, k_ref[sl, :], kg_ref[...])
                dk, dg = nvjp(dkn_scr[sl, :])
                dk_ref[sl, :] = dk.astype(BF16)
                dv_ref[sl, :] = dv_scr[sl, :].astype(BF16)
                return dkg + dg

            dkg = lax.fori_loop(0, SEQ // NORM_ROWS, body, jnp.zeros((1, 128), F32))
            _, nvjp = jax.vjp(_pair_rms, ck_ref[...], kg_ref[...])
            dck, dg = nvjp(dckn_scr[...])
            dck_ref[...] = dck
            dkg_ref[...] += dkg + dg

        @pl.when(last & (p == NPAIR - 1))
        def _():
            dqg_ref[...] = dqg_ref[...] + pltpu.roll(dqg_ref[...], HDIM, 1)
            dkg_ref[...] = dkg_ref[...] + pltpu.roll(dkg_ref[...], HDIM, 1)

    blk = lambda rows: pl.BlockSpec((rows, 128), lambda p, i: (0, p))
    qblk = pl.BlockSpec((ATTN_ROWS, 128), lambda p, i: (i, p))
    dbias = pl.BlockSpec((2, QBLK, KBLK), lambda p, i: (p, 0, 0))
    return pl.pallas_call(
        kern, name="attn_bwd", grid=(NPAIR, ATTN_STEPS),
        in_specs=_attn_in_specs() + [_row(128), _row(128), pl.BlockSpec((ATTN_ROWS, 128), lambda p, i: (i, 4 + p)),
                                     qblk, qblk] + _prob_specs(),
        out_specs=[qblk, blk(SEQ), blk(SEQ), qblk, blk(CTX), blk(CTX), dbias, dbias, dbias, _row(128), _row(128)],
        out_shape=[jax.ShapeDtypeStruct((SEQ, 512), BF16)] * 4 + [jax.ShapeDtypeStruct((CTX, 512), F32)] * 2
        + [jax.ShapeDtypeStruct((HEADS, QBLK, KBLK), F32)] * 3
        + [jax.ShapeDtypeStruct((1, 128), F32), jax.ShapeDtypeStruct((1, 128), F32)],
        scratch_shapes=[pltpu.VMEM((SEQ, 128), F32), pltpu.VMEM((CTX, 128), F32),
                        pltpu.VMEM((SEQ, 128), F32), pltpu.VMEM((CTX, 128), F32), pltpu.VMEM((SEQ, 128), F32)],
        compiler_params=_cparams(("arbitrary", "arbitrary"), VMEM_BIG),
    )(z, z, z, z, zc, zc, qg2, kg2, dcat, *saved)


def outproj(out_a, out_b, x, target, gate, wo):
    tl = 512

    def kern(a_ref, b_ref, x_ref, t_ref, g_ref, w_ref, loss_ref, dy_ref, dcat_ref, dg_ref, dw_ref):
        @pl.when(pl.program_id(0) == 0)
        def _():
            loss_ref[...] = jnp.zeros_like(loss_ref)
            dg_ref[...] = jnp.zeros_like(dg_ref)
            dw_ref[...] = jnp.zeros_like(dw_ref)

        a, b = a_ref[...].astype(BF16), b_ref[...].astype(BF16)
        mix = (jnp.dot(a, w_ref[0:512, :], preferred_element_type=F32)
               + jnp.dot(b, w_ref[512:1024, :], preferred_element_type=F32))
        err = x_ref[...] + g_ref[...] * mix - t_ref[...]
        loss_ref[...] += 0.5 * jnp.sum(jnp.mean(err * err, axis=-1))
        dy = err * (1.0 / DM)
        dy_ref[...] = dy
        dg_ref[...] += jnp.sum(dy * mix, axis=0, keepdims=True)
        dmix = (g_ref[...] * dy).astype(BF16)
        dcat_ref[...] = lax.dot_general(dmix, w_ref[...], (((1,), (1,)), ((), ())), preferred_element_type=F32)
        dw_ref[0:512, :] += lax.dot_general(a, dmix, (((0,), (0,)), ((), ())), preferred_element_type=F32)
        dw_ref[512:1024, :] += lax.dot_general(b, dmix, (((0,), (0,)), ((), ())), preferred_element_type=F32)

    tile = lambda w: pl.BlockSpec((tl, w), lambda t: (t, 0))
    whole = pl.BlockSpec((DM, DM), lambda t: (0, 0))
    return pl.pallas_call(
        kern, name="outproj", grid=(SEQ // tl,),
        in_specs=[tile(512), tile(512), tile(DM), tile(DM), _row(DM), whole],
        out_specs=[pl.BlockSpec((8, 128), lambda t: (0, 0)), tile(DM), tile(DM), _row(DM), whole],
        out_shape=[jax.ShapeDtypeStruct((8, 128), F32), jax.ShapeDtypeStruct((SEQ, DM), F32),
                   jax.ShapeDtypeStruct((SEQ, DM), F32), jax.ShapeDtypeStruct((1, DM), F32),
                   jax.ShapeDtypeStruct((DM, DM), F32)],
        compiler_params=_cparams(("arbitrary",), 48 * 1024 * 1024),
    )(out_a, out_b, x, target, gate, wo)


def _pieces(sources):
    out = []
    for name, c0, c1 in sources:
        for j in range(NCHIP):
            lo, hi = max(c0, j * SHARD_IN), min(c1, (j + 1) * SHARD_IN)
            if lo < hi:
                out.append((j, lo - j * SHARD_IN, hi - j * SHARD_IN, name, lo - c0, hi - c0))
    return out


DZ_PIECES = _pieces((("a", 0, 1536), ("q", 1536, 2048), ("k", 2048, 2560), ("v", 2560, 3072), ("g", 3072, DIN)))
DZC_PIECES = _pieces((("k", 2048, 2560), ("v", 2560, 3072)))
_NT = (((1,), (1,)), ((), ()))


DH_SUBTILES = 2


def _dz_specs(tl):
    return [pl.BlockSpec((tl, 1536), lambda t: (t, 0))] + [pl.BlockSpec((tl, 512), lambda t: (t, 0))] * 4


def dh_bwd(dz_parts, w_full, x, dy, shift, scale, norm_g, dg_ctx):
    tl = 512
    nt = SEQ // tl

    def kern(a_ref, q_ref, k_ref, v_ref, g_ref, w_ref, x_ref, dy_ref, sh_ref, sc_ref, gn_ref, dgc_ref,
             gx_ref, dsh_ref, dsc_ref, dg_ref):
        @pl.when(pl.program_id(0) == 0)
        def _():
            dsh_ref[...] = jnp.zeros_like(dsh_ref)
            dsc_ref[...] = jnp.zeros_like(dsc_ref)
            dg_ref[...] = dgc_ref[...]

        src = dict(a=a_ref, q=q_ref, k=k_ref, v=v_ref, g=g_ref)
        for sub in range(DH_SUBTILES):
            rows = slice(sub * tl // DH_SUBTILES, (sub + 1) * tl // DH_SUBTILES)
            dh = None
            for j, l0, l1, name, s0, s1 in DZ_PIECES:
                part = lax.dot_general(src[name][rows, s0:s1], w_ref[j, :, l0:l1], _NT, preferred_element_type=F32)
                dh = part if dh is None else dh + part
            _, vjp = jax.vjp(_modulated, x_ref[rows, :], gn_ref[...], sc_ref[...], sh_ref[...])
            dx, dg, dsc, dsh = vjp(dh)
            gx_ref[rows, :] = dy_ref[rows, :] + dx
            dg_ref[...] += dg
            dsc_ref[...] += dsc
            dsh_ref[...] += dsh

    tile = pl.BlockSpec((tl, DM), lambda t: (t, 0))
    return pl.pallas_call(
        kern, name="dh_bwd", grid=(nt,),
        in_specs=_dz_specs(tl) + [pl.BlockSpec((NCHIP, DM, SHARD_IN), lambda t: (0, 0, 0)), tile, tile, _row(DM),
                                  _row(DM), _row(DM), _row(DM)],
        out_specs=[tile, _row(DM), _row(DM), _row(DM)],
        out_shape=[jax.ShapeDtypeStruct((SEQ, DM), F32)] + [jax.ShapeDtypeStruct((1, DM), F32)] * 3,
        compiler_params=_cparams(("arbitrary",), 48 * 1024 * 1024),
    )(*dz_parts, w_full, x, dy, shift, scale, norm_g, dg_ctx)


def dw_bwd(h, dz_parts, hc, dck, dcv, g_out):
    tl = 512
    nt = SEQ // tl
    (rhi, wi), (rho, wo) = RS_SHAPES

    def kern(h_ref, a_ref, q_ref, k_ref, v_ref, g_ref, hc_ref, dck_ref, dcv_ref, go_hbm,
             wire_i, keep_i, wire_o, keep_o, acc, rcv_i, mine_o, rcv_o, load_sem, send_sems, recv_sems):
        t = pl.program_id(0)
        x, y, c = _me()
        k = 2 * x + y
        sib = _flip(1)
        half = lambda hh, rh: pl.ds(pl.multiple_of(hh * rh, rh), rh)
        load_o = pltpu.make_async_copy(go_hbm.at[:, half(c, rho), :], mine_o, load_sem)
        pair_o = _rcopy(go_hbm.at[:, half(1 - c, rho), :], rcv_o, send_sems, recv_sems, 0, sib)
        pair_i = _rcopy(acc.at[:, half(1 - c, rhi), :], rcv_i, send_sems, recv_sems, 1, sib)

        @pl.when(t == 0)
        def _():
            load_o.start()
            pair_o.start()
            acc[...] = jnp.zeros_like(acc)
            hct = hc_ref[...].T
            csrc = dict(k=dck_ref, v=dcv_ref)
            for j, l0, l1, name, s0, s1 in DZC_PIECES:
                acc[j, :, l0:l1] += jnp.dot(hct, csrc[name][:, s0:s1].astype(BF16), preferred_element_type=F32)

        ht = h_ref[...].T
        src = dict(a=a_ref, q=q_ref, k=k_ref, v=v_ref, g=g_ref)
        for j, l0, l1, name, s0, s1 in DZ_PIECES:
            acc[j, :, l0:l1] += jnp.dot(ht, src[name][:, s0:s1], preferred_element_type=F32)

        @pl.when(t == nt - 1)
        def _():
            pair_i.start()
            load_o.wait()
            pair_o.wait_recv()
            for j in range(NCHIP):
                wire_o[j] = (mine_o[j] + rcv_o[j]).astype(BF16)
            keep_o[...] = mine_o[k] + rcv_o[k]
            pair_i.wait_recv()
            mine = half(c, rhi)
            for j in range(NCHIP):
                wire_i[j] = (acc[j, mine, :] + rcv_i[j]).astype(BF16)
            keep_i[...] = acc[k, mine, :] + rcv_i[k]
            pair_o.wait_send()
            pair_i.wait_send()

    whole = lambda *shape: pl.BlockSpec(shape, lambda t: (0,) * len(shape))
    return pl.pallas_call(
        kern, name="dw_bwd", grid=(nt,),
        in_specs=[pl.BlockSpec((tl, DM), lambda t: (t, 0))] + _dz_specs(tl)
        + [whole(CTX, DM), whole(CTX, 512), whole(CTX, 512), pl.BlockSpec(memory_space=pl.ANY)],
        out_specs=[whole(NCHIP, rhi, wi), whole(rhi, wi), whole(NCHIP, rho, wo), whole(rho, wo)],
        out_shape=[jax.ShapeDtypeStruct((NCHIP, rhi, wi), BF16), jax.ShapeDtypeStruct((rhi, wi), F32),
                   jax.ShapeDtypeStruct((NCHIP, rho, wo), BF16), jax.ShapeDtypeStruct((rho, wo), F32)],
        scratch_shapes=[pltpu.VMEM((NCHIP, DM, SHARD_IN), F32), pltpu.VMEM((NCHIP, rhi, wi), F32),
                        pltpu.VMEM((NCHIP, rho, wo), F32), pltpu.VMEM((NCHIP, rho, wo), F32),
                        pltpu.SemaphoreType.DMA(()), pltpu.SemaphoreType.DMA((2,)), pltpu.SemaphoreType.DMA((2,))],
        compiler_params=_cparams(("arbitrary",), VMEM_BIG),
    )(h, *dz_parts, hc, dck, dcv, g_out)


def ctx_bwd(dck, dcv, w_full, ctx, cshift, cscale, norm_g):
    def kern(dck_ref, dcv_ref, w_ref, c_ref, sh_ref, sc_ref, g_ref, dsh_ref, dsc_ref, dg_ref):
        csrc = dict(k=dck_ref, v=dcv_ref)
        dhc = None
        for j, l0, l1, name, s0, s1 in DZC_PIECES:
            part = lax.dot_general(csrc[name][:, s0:s1].astype(BF16), w_ref[j, :, l0:l1], _NT,
                                   preferred_element_type=F32)
            dhc = part if dhc is None else dhc + part
        _, vjp = jax.vjp(lambda g, sc, sh: _modulated(c_ref[...], g, sc, sh), g_ref[...], sc_ref[...], sh_ref[...])
        dg_ref[...], dsc_ref[...], dsh_ref[...] = vjp(dhc)

    whole = lambda r, c: pl.BlockSpec((r, c), lambda i: (0, 0))
    return pl.pallas_call(
        kern, name="ctx_bwd", grid=(1,),
        in_specs=[whole(CTX, 512), whole(CTX, 512), pl.BlockSpec((NCHIP, DM, SHARD_IN), lambda i: (0, 0, 0)),
                  whole(CTX, DM), _row(DM), _row(DM), _row(DM)],
        out_specs=[_row(DM), _row(DM), _row(DM)],
        out_shape=[jax.ShapeDtypeStruct((1, DM), F32)] * 3,
        compiler_params=_cparams(("arbitrary",), 40 * 1024 * 1024),
    )(dck, dcv, w_full, ctx, cshift, cscale, norm_g)


def _lane_pad_rpb(rpb):
    r = jnp.pad(rpb, ((0, 0), (0, 0), (0, GRID_W - rpb.shape[-1])))
    return jnp.concatenate([r, r], axis=-1)


def local_step(chip, dev, x, c_vec, c_ctx, w_ada, b_shard, ctx, target, norm_g, sgu_g, w_s, b_s, q_g, k_g, rpb,
               w_in_shard, w_out_shard):
    bsb = jnp.broadcast_to(b_s[:, :, None], (4, 128, 128))
    qg2, kg2 = jnp.tile(q_g, (1, 2)), jnp.tile(k_g, (1, 2))

    z, h, w_in_full, w_out_full, mod_all, cs = inproj_fwd(chip, x, c_vec, c_ctx, w_ada, b_shard, norm_g, w_in_shard,
                                                          w_out_shard)
    mods = mod_all.transpose(1, 0, 2).reshape(CS_ROWS, 3 * DM)
    mod = lax.dynamic_slice(mods, (8 * dev, 0), (1, 3 * DM))
    shift, scale, gate = mod[:, :DM], mod[:, DM:2 * DM], mod[:, 2 * DM:]
    cshift, cscale = mods[8 * NDEV:8 * NDEV + 1, :DM], mods[8 * NDEV:8 * NDEV + 1, DM:2 * DM]
    zc, hc = ctx_fwd(ctx, cshift, cscale, norm_g, w_in_full)
    bias = rpb_tables(_lane_pad_rpb(rpb))
    out_a = sgu_fwd(z, sgu_g, w_s, bsb)
    out_b, *saved = attn_fwd(z, zc, bias, qg2, kg2)
    loss8, dy, dcat, dgate, dwo = outproj(out_a, out_b, x, target, gate, w_out_full.reshape(DM, DM))
    dz_a, dsg, dws, dbsb = sgu_bwd(z, sgu_g, w_s, bsb, dcat)
    dq, dk, dv, dbg, dck, dcv, db0, db1, db2, dqg2, dkg2 = attn_bwd(z, zc, qg2, kg2, dcat, saved)
    drpb = rpb_bwd((db0, db1, db2))[:, :, :rpb.shape[-1]]
    dz_parts = (dz_a, dq, dk, dv, dbg)
    dcshift, dcscale, dng_c = ctx_bwd(dck, dcv, w_in_full, ctx, cshift, cscale, norm_g)
    wire_i, keep_i, wire_o, keep_o = dw_bwd(h, dz_parts, hc, dck, dcv, dwo.reshape(NCHIP, SHARD_OUT, DM))
    *in_flight, token = rs_start(wire_i, wire_o)
    grad_x, dshift, dscale, dng = dh_bwd(dz_parts, w_in_full, x, dy, shift, scale, norm_g, dng_c + token[0, 0])
    got_i, got_o = rs_wait(*in_flight, dshift)
    return dict(
        loss=loss8[0:1, 0:1], grad_x=grad_x, rs=(keep_i, got_i, keep_o, got_o), cs=cs,
        dmod=jnp.concatenate([dshift, dscale, dgate], axis=-1),
        dcmod=jnp.concatenate([dcshift, dcscale, jnp.zeros((1, DM), F32)], axis=-1),
        d_norm_g=dng, d_sgu_g=dsg, d_w_s=dws, d_b_s=dbsb[:, :, 0],
        d_q_g=dqg2[:, :HDIM], d_k_g=dkg2[:, :HDIM], d_rpb=drpb)


def _me():
    return lax.axis_index("x"), lax.axis_index("y"), lax.axis_index("c")


def _flip(q):
    x, y, c = _me()
    return ((1 - x) if q & 4 else x, (1 - y) if q & 2 else y, (1 - c) if q & 1 else c)


def _chip_of(dev):
    return 2 * dev[0] + dev[1]


def _rcopy(src, dst, send_sems, recv_sems, k, dev):
    return pltpu.make_async_remote_copy(src_ref=src, dst_ref=dst, send_sem=send_sems.at[k], recv_sem=recv_sems.at[k],
                                        device_id=dev, device_id_type=MESH_ID)


_VMEM_SPEC = pl.BlockSpec(memory_space=pltpu.VMEM)
SLAB_ROWS = 80


RS_SHAPES = ((DM // 2, SHARD_IN), (SHARD_OUT // 2, DM))
_HBM_SPEC = pl.BlockSpec(memory_space=pltpu.HBM)
_SEM_SPEC = pl.BlockSpec(memory_space=pltpu.SEMAPHORE)
_IN_FLIGHT = pltpu.SideEffectType.DATAFLOW_SIDE_EFFECTING


def _rs_copies(wires, lands, send_sems, recv_sems):
    return [pltpu.make_async_remote_copy(
        src_ref=wires[n].at[_chip_of(_flip(q))], dst_ref=lands[n].at[q // 2 - 1],
        send_sem=send_sems.at[3 * n + q // 2 - 1], recv_sem=recv_sems.at[3 * n + q // 2 - 1],
        device_id=_flip(q), device_id_type=MESH_ID) for n in (0, 1) for q in (2, 4, 6)]


def rs_start(wire_i, wire_o):
    lands = [lax.empty((NCHIP - 1, rh, w), BF16) for rh, w in RS_SHAPES]

    def body(wi_ref, wo_ref, li_ref, lo_ref, send_sems, recv_sems, wi_thru, wo_thru, li_thru, lo_thru, token):
        for cp in _rs_copies((wi_ref, wo_ref), (li_ref, lo_ref), send_sems, recv_sems):
            cp.start()
        token[...] = jnp.zeros_like(token)

    hbm = lambda a: pltpu.HBM(a.shape, a.dtype)
    return pl.pallas_call(
        body, name="rs_start",
        out_shape=(pltpu.SemaphoreType.DMA((6,)), pltpu.SemaphoreType.DMA((6,)), hbm(wire_i), hbm(wire_o),
                   hbm(lands[0]), hbm(lands[1]), jax.ShapeDtypeStruct((8, 128), F32)),
        in_specs=(_HBM_SPEC,) * 4, out_specs=(_SEM_SPEC, _SEM_SPEC) + (_HBM_SPEC,) * 4 + (_VMEM_SPEC,),
        input_output_aliases={0: 2, 1: 3, 2: 4, 3: 5},
        compiler_params=pltpu.CompilerParams(has_side_effects=_IN_FLIGHT),
    )(*[pltpu.with_memory_space_constraint(a, pltpu.HBM) for a in (wire_i, wire_o, *lands)])


def rs_wait(send_sems, recv_sems, wire_i, wire_o, land_i, land_o, after):
    def body(wi_ref, wo_ref, li_ref, lo_ref, send_sems, recv_sems, after_ref, wi_dead, wo_dead, gi_ref, go_ref):
        for cp in _rs_copies((wi_ref, wo_ref), (li_ref, lo_ref), send_sems, recv_sems):
            cp.wait_send()
            cp.wait_recv()

    hbm = lambda a: pltpu.HBM(a.shape, a.dtype)
    return pl.pallas_call(
        body, name="rs_wait", out_shape=(hbm(wire_i), hbm(wire_o), hbm(land_i), hbm(land_o)),
        in_specs=(_HBM_SPEC,) * 4 + (_SEM_SPEC, _SEM_SPEC, pl.BlockSpec(memory_space=pl.ANY)),
        out_specs=(_HBM_SPEC,) * 4, input_output_aliases={0: 0, 1: 1, 2: 2, 3: 3},
        compiler_params=pltpu.CompilerParams(has_side_effects=_IN_FLIGHT),
    )(wire_i, wire_o, land_i, land_o, send_sems, recv_sems, after)[2:]


def final_reduce(keep_i, got_i, keep_o, got_o, slab):
    def kern(ki_ref, gi_ref, ko_ref, go_ref, s_ref, gin_ref, gout_ref, all_ref, tot_ref, send_sems, recv_sems):
        x, y, c = _me()
        sib = _flip(1)
        dev = lambda d: 4 * d[0] + 2 * d[1] + d[2]
        me = dev((x, y, c))

        def slab_copy(idx, owner, to):
            return _rcopy(all_ref.at[dev(owner)], all_ref.at[dev(owner)], send_sems, recv_sems, idx, to)

        all_ref[me] = s_ref[...]
        first = [slab_copy(0, (x, y, c), sib)] + [slab_copy(q // 2, (x, y, c), _flip(q)) for q in (2, 4, 6)]
        for cp in first:
            cp.start()

        shares = []
        for n, (keep, got, out) in enumerate(((ki_ref, gi_ref, gin_ref), (ko_ref, go_ref, gout_ref))):
            rh = RS_SHAPES[n][0]
            half = lambda hh, rh=rh: pl.ds(pl.multiple_of(hh * rh, rh), rh)
            out[half(c), :] = ((keep[...] + got[0].astype(F32)) + got[1].astype(F32)) + got[2].astype(F32)
            share = _rcopy(out.at[half(c), :], out.at[half(c), :], send_sems, recv_sems, 7 + n, sib)
            share.start()
            shares.append((share, _rcopy(out.at[half(1 - c), :], out.at[half(1 - c), :], send_sems, recv_sems, 7 + n,
                                         sib)))

        passed = []
        for q in (2, 4, 6):
            slab_copy(q // 2, _flip(q), (x, y, c)).wait_recv()
            cp = slab_copy(3 + q // 2, _flip(q), sib)
            cp.start()
            passed.append(cp)
        slab_copy(0, sib, (x, y, c)).wait_recv()
        for q in (2, 4, 6):
            slab_copy(3 + q // 2, _flip(q | 1), (x, y, c)).wait_recv()
        tot = all_ref[0]
        for d in range(1, NDEV):
            tot = tot + all_ref[d]
        tot_ref[...] = tot
        for share, arrival in shares:
            arrival.wait_recv()
            share.wait_send()
        for cp in first + passed:
            cp.wait_send()

    (rhi, wi), (rho, wo) = RS_SHAPES
    return pl.pallas_call(
        kern, name="final_reduce", in_specs=[_VMEM_SPEC] * 5, out_specs=[_VMEM_SPEC] * 4,
        out_shape=[jax.ShapeDtypeStruct((2 * rhi, wi), F32), jax.ShapeDtypeStruct((2 * rho, wo), F32),
                   jax.ShapeDtypeStruct((NDEV, SLAB_ROWS, DM), F32), jax.ShapeDtypeStruct((SLAB_ROWS, DM), F32)],
        scratch_shapes=[pltpu.SemaphoreType.DMA((9,)), pltpu.SemaphoreType.DMA((9,))],
        compiler_params=pltpu.CompilerParams(vmem_limit_bytes=40 * 1024 * 1024),
    )(keep_i, got_i, keep_o, got_o, slab)


def ada_bwd(a_in, dm, dm_shard, w_ada, c_ctx):
    def kern(a_ref, dm_ref, dms_ref, w_ref, cc_ref, dw_ref, db_ref, dcc_ref, parts, send_sems, recv_sems):
        x, y, c = _me()
        k = 2 * x + y
        act = jax.nn.silu(a_ref[...]).astype(BF16)
        dms = dms_ref[...].astype(BF16)
        dw_ref[...] = lax.dot_general(act, dms, (((0,), (0,)), ((), ())), preferred_element_type=F32)
        db_ref[...] = jnp.sum(dm_ref[...], axis=0, keepdims=True)
        parts[k] = lax.dot_general(dms, w_ref[...].astype(BF16), (((1,), (1,)), ((), ())), preferred_element_type=F32)
        sends = [_rcopy(parts.at[k], parts.at[k], send_sems, recv_sems, q // 2 - 1, _flip(q)) for q in (2, 4, 6)]
        for cp in sends:
            cp.start()
        for q in (2, 4, 6):
            kq = _chip_of(_flip(q))
            _rcopy(parts.at[kq], parts.at[kq], send_sems, recv_sems, q // 2 - 1, _flip(q)).wait_recv()
        dact = ((parts[0] + parts[1]) + parts[2]) + parts[3]
        _, vjp = jax.vjp(jax.nn.silu, cc_ref[...])
        dcc_ref[...] = vjp(dact[8:9, :])[0]
        for cp in sends:
            cp.wait_send()

    return pl.pallas_call(
        kern, name="ada_bwd", in_specs=[_VMEM_SPEC] * 5, out_specs=[_VMEM_SPEC] * 3,
        out_shape=[jax.ShapeDtypeStruct((DM, SHARD_ADA), F32), jax.ShapeDtypeStruct((1, 3 * DM), F32),
                   jax.ShapeDtypeStruct((1, DM), F32)],
        scratch_shapes=[pltpu.VMEM((NCHIP, 16, DM), F32), pltpu.SemaphoreType.DMA((3,)), pltpu.SemaphoreType.DMA((3,))],
    )(a_in, dm, dm_shard, w_ada, c_ctx)


def _adamw_math(w, g, m, v):
    m = B1 * m + (1.0 - B1) * g
    v = B2 * v + (1.0 - B2) * (g * g)
    m_hat = m / (1.0 - B1 ** STEP)
    v_hat = v / (1.0 - B2 ** STEP)
    return -LR * (m_hat / (jnp.sqrt(v_hat) + ADAM_EPS) + WD * w), m, v


def adamw_big(w, g, m, v, name, block_rows=256):
    rows, width = w.shape

    def kern(w_ref, g_ref, m_ref, v_ref, d_ref, nm_ref, nv_ref):
        d_ref[...], nm_ref[...], nv_ref[...] = _adamw_math(w_ref[...], g_ref[...], m_ref[...], v_ref[...])

    spec = pl.BlockSpec((block_rows, width), lambda i: (i, 0))
    return pl.pallas_call(
        kern, name=name, grid=(rows // block_rows,), in_specs=[spec] * 4, out_specs=[spec] * 3,
        out_shape=[jax.ShapeDtypeStruct((rows, width), F32)] * 3,
        compiler_params=_cparams(("arbitrary",)),
    )(w, g, m, v)


def adamw_small(quads):
    n = len(quads)

    def kern(*refs):
        ins, outs = refs[:4 * n], refs[4 * n:]
        for i in range(n):
            w, g, m, v = (r[...] for r in ins[4 * i:4 * i + 4])
            outs[3 * i][...], outs[3 * i + 1][...], outs[3 * i + 2][...] = _adamw_math(w, g, m, v)

    flat = [a for quad in quads for a in quad]
    res = pl.pallas_call(
        kern, name="adamw_small", in_specs=[_VMEM_SPEC] * (4 * n), out_specs=[_VMEM_SPEC] * (3 * n),
        out_shape=[jax.ShapeDtypeStruct(q[0].shape, F32) for q in quads for _ in range(3)],
    )(*flat)
    return [tuple(res[3 * i:3 * i + 3]) for i in range(n)]


def _rows_of(a, rows):
    flat = a.reshape(-1)
    return jnp.pad(flat, (0, rows * DM - flat.shape[0])).reshape(rows, DM)


def kernel(x, c, ctx, c_ctx, w_ada, b_ada, norm_g, w_in, sgu_norm_g, w_spatial, b_spatial, q_norm_g, k_norm_g, rpb, w_out, loss_target, m_c_ctx, m_w_ada, m_b_ada, m_norm_g, m_w_in, m_sgu_norm_g, m_w_spatial, m_b_spatial, m_q_norm_g, m_k_norm_g, m_rpb, m_w_out, v_c_ctx, v_w_ada, v_b_ada, v_norm_g, v_w_in, v_sgu_norm_g, v_w_spatial, v_b_spatial, v_q_norm_g, v_k_norm_g, v_rpb, v_w_out):
    xi, yi, ci = lax.axis_index("x"), lax.axis_index("y"), lax.axis_index("c")
    chip, dev = 2 * xi + yi, 4 * xi + 2 * yi + ci
    c_ctx2 = c_ctx.reshape(1, DM)

    b_shard = lax.dynamic_slice(b_ada, (0, chip * SHARD_ADA), (1, SHARD_ADA))
    part = local_step(chip.reshape(1).astype(jnp.int32), dev, x[0], c, c_ctx2, w_ada[0], b_shard, ctx[0], loss_target[0],
                      norm_g, sgu_norm_g, w_spatial[0], b_spatial[0], q_norm_g, k_norm_g, rpb[0], w_in[0], w_out[0])
    cs = part["cs"]

    slab = jnp.concatenate([
        part["d_norm_g"], _rows_of(part["d_sgu_g"], 1), _rows_of(part["d_b_s"], 1),
        _rows_of(jnp.concatenate([part["d_q_g"], part["d_k_g"]], axis=-1), 1), _rows_of(part["d_rpb"], 4),
        _rows_of(part["loss"], 1), _rows_of(part["dcmod"], 3), _rows_of(part["dmod"], 3), jnp.zeros((1, DM), F32),
        _rows_of(part["d_w_s"], 64)], axis=0)
    g_w_in, g_w_out, gathered, tot = final_reduce(*part["rs"], slab)
    dm = jnp.concatenate([gathered[:, 12:15, :].reshape(NDEV, 3 * DM), tot[9:12].reshape(1, 3 * DM),
                          jnp.zeros((7, 3 * DM), F32)], axis=0)
    a_in = jnp.concatenate([cs[0:8 * NDEV:8], cs[8 * NDEV:8 * NDEV + 1], jnp.zeros((7, DM), F32)], axis=0)
    dm_shard = lax.dynamic_slice(dm, (0, chip * SHARD_ADA), (16, SHARD_ADA))
    g_w_ada, g_b_ada, g_c_ctx = ada_bwd(a_in, dm, dm_shard, w_ada[0], c_ctx2)

    loss = tot[8, 0]
    g_small = dict(
        c_ctx=g_c_ctx, b_ada=g_b_ada, norm_g=tot[0:1], sgu_norm_g=tot[1:2, :512], w_spatial=tot[16:80].reshape(512, 128),
        b_spatial=tot[2:3, :512].reshape(4, 128), q_norm_g=tot[3:4, :HDIM], k_norm_g=tot[3:4, HDIM:2 * HDIM],
        rpb=tot[4:8].reshape(-1)[:HEADS * 15 * 31].reshape(HEADS * 15, 31))
    shapes = dict(c_ctx=(DM,), w_ada=(1, DM, SHARD_ADA), b_ada=(1, 3 * DM), norm_g=(1, DM), w_in=(1, DM, SHARD_IN),
                  sgu_norm_g=(1, 512), w_spatial=(1, 4, 128, 128), b_spatial=(1, 4, 128), q_norm_g=(1, HDIM),
                  k_norm_g=(1, HDIM), rpb=(1, HEADS, 15, 31), w_out=(1, SHARD_OUT, DM))
    names = list(shapes)
    weights = dict(c_ctx=c_ctx, w_ada=w_ada, b_ada=b_ada, norm_g=norm_g, w_in=w_in, sgu_norm_g=sgu_norm_g,
                   w_spatial=w_spatial, b_spatial=b_spatial, q_norm_g=q_norm_g, k_norm_g=k_norm_g, rpb=rpb, w_out=w_out)
    m_in = dict(zip(names, (m_c_ctx, m_w_ada, m_b_ada, m_norm_g, m_w_in, m_sgu_norm_g, m_w_spatial, m_b_spatial,
                            m_q_norm_g, m_k_norm_g, m_rpb, m_w_out)))
    v_in = dict(zip(names, (v_c_ctx, v_w_ada, v_b_ada, v_norm_g, v_w_in, v_sgu_norm_g, v_w_spatial, v_b_spatial,
                            v_q_norm_g, v_k_norm_g, v_rpb, v_w_out)))
    grads = dict(g_small, w_ada=g_w_ada, w_in=g_w_in, w_out=g_w_out)
    upd = {}
    for n in ("w_ada", "w_in", "w_out"):
        g = grads[n]
        upd[n] = adamw_big(weights[n].reshape(g.shape), g, m_in[n].reshape(g.shape), v_in[n].reshape(g.shape),
                           "adamw_" + n)
    small = [n for n in names if n not in upd]
    res = adamw_small([(weights[n].reshape(grads[n].shape), grads[n], m_in[n].reshape(grads[n].shape),
                        v_in[n].reshape(grads[n].shape)) for n in small])
    upd.update(zip(small, res))
    out = [loss, part["grad_x"].reshape(1, SEQ, DM)]
    out += [grads[n].reshape(shapes[n]) for n in names]
    for slot in range(3):
        out += [upd[n][slot].reshape(shapes[n]) for n in names]
    return tuple(out)
```

```python
import jax
import jax.numpy as jnp
from jax import lax
from jax.experimental import pallas as pl
from jax.experimental.pallas import tpu as pltpu

F32, BF16 = jnp.float32, jnp.bfloat16
SEQ, DM, CTX, DIN = 4096, 1024, 256, 3584
NCHIP, NDEV = 4, 8
SHARD_IN = DIN // NCHIP
SHARD_ADA = 3 * DM // NCHIP
SHARD_OUT = DM // NCHIP
GRID_W = 64
QROWS = 4
KROWS = 12
QBLK, KBLK = QROWS * GRID_W, KROWS * GRID_W
NQBLK = SEQ // QBLK
HEADS, HDIM, NPAIR = 8, 64, 4
EPS = 1e-6
NEG_INF = -1e30
ZQ, ZK, ZV, ZG = 12, 16, 20, 24
LR, B1, B2, ADAM_EPS, WD, STEP = 0.001, 0.9, 0.999, 1e-08, 0.01, 10
VMEM_BIG = 56 * 1024 * 1024
MESH_ID = pl.DeviceIdType.MESH


def _dot(a, b, lhs_c, rhs_c):
    return lax.dot_general(a.astype(BF16), b.astype(BF16), (((lhs_c,), (rhs_c,)), ((), ())),
                           preferred_element_type=F32)


@jax.custom_vjp
def mm(a, b):
    return _dot(a, b, 1, 0)


@jax.custom_vjp
def mm_nt(a, b):
    return _dot(a, b, 1, 1)


@jax.custom_vjp
def mm_tn(a, b):
    return _dot(a, b, 0, 0)


mm.defvjp(lambda a, b: (mm(a, b), (a, b)), lambda r, ct: (mm_nt(ct, r[1]), mm_tn(r[0], ct)))
mm_nt.defvjp(lambda a, b: (mm_nt(a, b), (a, b)), lambda r, ct: (mm(ct, r[1]), mm_tn(ct, r[0])))
mm_tn.defvjp(lambda a, b: (mm_tn(a, b), (a, b)), lambda r, ct: (mm_nt(r[1], ct), mm(r[0], ct)))


def _rms(x, g):
    return x * lax.rsqrt(jnp.mean(x * x, axis=-1, keepdims=True) + EPS) * g


def _modulated(x, g, scale, shift):
    return _rms(x, g) * (1.0 + scale) + shift


def _pair_rms(x, g2):
    lo = lax.broadcasted_iota(jnp.int32, (1, 2 * HDIM), 1) < HDIM
    sq = x * x
    s_lo = jnp.sum(jnp.where(lo, sq, 0.0), axis=-1, keepdims=True)
    s_hi = jnp.sum(jnp.where(lo, 0.0, sq), axis=-1, keepdims=True)
    rs = jnp.where(lo, lax.rsqrt(s_lo / HDIM + EPS), lax.rsqrt(s_hi / HDIM + EPS))
    return x * rs * g2


def _cparams(sem, vmem=None):
    return pltpu.CompilerParams(dimension_semantics=sem, vmem_limit_bytes=vmem)


def _row(n):
    return pl.BlockSpec((1, n), lambda *_: (0, 0))


CS_ROWS = 8 * NDEV + 8


def _mod_part(mod_ref, row, part):
    pieces = []
    for j in range(NCHIP):
        lo, hi = max(part * DM, j * SHARD_ADA), min((part + 1) * DM, (j + 1) * SHARD_ADA)
        if lo < hi:
            pieces.append(mod_ref[j, row, lo - j * SHARD_ADA:hi - j * SHARD_ADA])
    return jnp.concatenate(pieces, axis=-1)


def inproj_fwd(chip, x, c_vec, c_ctx, w_ada, b_shard, norm_g, w_shard, wo_shard):
    tl = 1024
    nt = SEQ // tl
    halves = (DM // 2, SHARD_OUT // 2)
    n_w, n_c = 12, NDEV - 1

    def kern(k_ref, x_ref, cv_ref, cc_ref, wa_ref, b_ref, g_ref, w_ref, wo_ref,
             z_ref, h_ref, wfull_ref, wofull_ref, modall_ref, csall_ref,
             w_scr, wo_scr, h_scr, mine, cs_scr, mod_scr, shsc_scr, send_sems, recv_sems):
        s, t = pl.program_id(0), pl.program_id(1)
        xi, yi, c = _me()
        k, me = 2 * xi + yi, 4 * xi + 2 * yi + c
        sib = _flip(1)
        rows = pl.ds(pl.multiple_of(t * tl, tl), tl)
        gathered = (w_scr, wo_scr)
        slot = lambda d: pl.ds(pl.multiple_of(8 * d, 8), 8)

        def c_copy(q, owner):
            return _rcopy(mine, cs_scr.at[slot(owner), :], send_sems, recv_sems, n_w + q - 1, _flip(q))

        def m_copy(q, chip_of_block):
            return _rcopy(mod_scr.at[chip_of_block], mod_scr.at[chip_of_block], send_sems, recv_sems,
                          n_w + n_c + q // 2 - 1, _flip(q))

        def adaln():
            first = lax.broadcasted_iota(jnp.int32, (8, DM), 0) == 0
            mine[...] = jnp.where(first, jnp.broadcast_to(cv_ref[...], (8, DM)), 0.0)
            cs_scr[slot(me), :] = mine[...]
            cs_scr[slot(NDEV), :] = jnp.where(first, jnp.broadcast_to(cc_ref[...], (8, DM)), 0.0)
            for q in range(1, NDEV):
                c_copy(q, me).start()
            wa = wa_ref[...].astype(BF16)
            for q in range(1, NDEV):
                px, py, pc = _flip(q)
                c_copy(q, 4 * px + 2 * py + pc).wait_recv()
            act = jax.nn.silu(cs_scr[...]).astype(BF16)
            mod_scr[k] = jnp.dot(act, wa, preferred_element_type=F32) + b_ref[...]
            for q in (2, 4, 6):
                m_copy(q, k).start()
            for q in (2, 4, 6):
                m_copy(q, _chip_of(_flip(q))).wait_recv()
            row = pl.ds(8 * me, 1)
            shsc_scr[0:1, :] = _mod_part(mod_scr, row, 0)
            shsc_scr[1:2, :] = _mod_part(mod_scr, row, 1)
            pltpu.sync_copy(mod_scr, modall_ref)
            pltpu.sync_copy(cs_scr, csall_ref)

        def block(n, chip_of_block, hh):
            return gathered[n].at[chip_of_block, pl.ds(pl.multiple_of(hh * halves[n], halves[n]), halves[n]), :]

        def ici(n, q, chip_of_block):
            blk = block(n, chip_of_block, c)
            return _rcopy(blk, blk, send_sems, recv_sems, 6 * n + q // 2 - 1, _flip(q))

        def d2d(n, q, chip_of_block, hh):
            blk = block(n, chip_of_block, hh)
            return _rcopy(blk, blk, send_sems, recv_sems, 6 * n + 3 + q // 2 - 1, sib)

        @pl.when((s == 0) & (t == 0))
        def _():
            adaln()
            w_scr[k] = w_ref[...].astype(BF16)
            wo_scr[k] = wo_ref[...].astype(BF16)
            for q in (2, 4, 6):
                ici(0, q, k).start()
                ici(1, q, k).start()

        for sweep in (1, 2, 3):
            @pl.when((s == sweep) & (t == 0))
            def _():
                q = 2 * sweep
                src = _chip_of(_flip(q))
                for n in (0, 1):
                    ici(n, q, src).wait_recv()
                    d2d(n, q, src, c).start()
                for n in (0, 1):
                    d2d(n, q, src, 1 - c).wait_recv()

        @pl.when(s == 0)
        def _():
            hb = _modulated(x_ref[...], g_ref[...], shsc_scr[1:2, :], shsc_scr[0:1, :]).astype(BF16)
            h_scr[rows, :] = hb
            h_ref[...] = hb

        z_ref[...] = jnp.dot(h_scr[rows, :], w_scr[lax.bitwise_xor(k, s)], preferred_element_type=F32)

        @pl.when((s == NCHIP - 1) & (t == nt - 1))
        def _():
            for q in range(1, NDEV):
                c_copy(q, me).wait_send()
            for q in (2, 4, 6):
                m_copy(q, k).wait_send()
            for n in (0, 1):
                for q in (2, 4, 6):
                    ici(n, q, k).wait_send()
                    d2d(n, q, _chip_of(_flip(q)), c).wait_send()
            pltpu.sync_copy(w_scr, wfull_ref)
            pltpu.sync_copy(wo_scr, wofull_ref)

    once = lambda s, t, k: (jnp.where(s == 0, t, nt - 1), 0)
    hbm = pl.BlockSpec(memory_space=pl.ANY)
    n_sem = n_w + n_c + 3
    return pl.pallas_call(
        kern, name="inproj_fwd",
        grid_spec=pltpu.PrefetchScalarGridSpec(
            num_scalar_prefetch=1, grid=(NCHIP, nt),
            in_specs=[pl.BlockSpec((tl, DM), once)] + [_VMEM_SPEC] * 7,
            out_specs=[pl.BlockSpec((tl, SHARD_IN), lambda s, t, k: (t, lax.bitwise_xor(k[0], s))),
                       pl.BlockSpec((tl, DM), once), hbm, hbm, hbm, hbm],
            scratch_shapes=[pltpu.VMEM((NCHIP, DM, SHARD_IN), BF16), pltpu.VMEM((NCHIP, SHARD_OUT, DM), BF16),
                            pltpu.VMEM((SEQ, DM), BF16), pltpu.VMEM((8, DM), F32), pltpu.VMEM((CS_ROWS, DM), F32),
                            pltpu.VMEM((NCHIP, CS_ROWS, SHARD_ADA), F32), pltpu.VMEM((8, DM), F32),
                            pltpu.SemaphoreType.DMA((n_sem,)), pltpu.SemaphoreType.DMA((n_sem,))]),
        out_shape=[jax.ShapeDtypeStruct((SEQ, DIN), F32), jax.ShapeDtypeStruct((SEQ, DM), BF16),
                   jax.ShapeDtypeStruct((NCHIP, DM, SHARD_IN), BF16), jax.ShapeDtypeStruct((NCHIP, SHARD_OUT, DM), BF16),
                   jax.ShapeDtypeStruct((NCHIP, CS_ROWS, SHARD_ADA), F32), jax.ShapeDtypeStruct((CS_ROWS, DM), F32)],
        compiler_params=_cparams(("arbitrary", "arbitrary"), VMEM_BIG),
    )(chip, x, c_vec, c_ctx, w_ada, b_shard, norm_g, w_shard, wo_shard)


def ctx_fwd(ctx, cshift, cscale, norm_g, w_full):
    def kern(c_ref, sh_ref, sc_ref, g_ref, w2_ref, w3_ref, zc_ref, hc_ref):
        hc = _modulated(c_ref[...], g_ref[...], sc_ref[...], sh_ref[...]).astype(BF16)
        hc_ref[...] = hc
        zc_ref[:, :SHARD_IN] = jnp.dot(hc, w2_ref[0], preferred_element_type=F32)
        zc_ref[:, SHARD_IN:] = jnp.dot(hc, w3_ref[0], preferred_element_type=F32)

    return pl.pallas_call(
        kern, name="ctx_fwd", grid=(1,),
        in_specs=[pl.BlockSpec((CTX, DM), lambda i: (0, 0)), _row(DM), _row(DM), _row(DM),
                  pl.BlockSpec((1, DM, SHARD_IN), lambda i: (2, 0, 0)),
                  pl.BlockSpec((1, DM, SHARD_IN), lambda i: (3, 0, 0))],
        out_specs=[pl.BlockSpec((CTX, 2 * SHARD_IN), lambda i: (0, 0)),
                   pl.BlockSpec((CTX, DM), lambda i: (0, 0))],
        out_shape=[jax.ShapeDtypeStruct((CTX, 2 * SHARD_IN), F32), jax.ShapeDtypeStruct((CTX, DM), BF16)],
        compiler_params=_cparams(("arbitrary",)),
    )(ctx, cshift, cscale, norm_g, w_full, w_full)


SGU_CHUNK, SGU_PER_STEP = 128, 4


def _gelu(x):
    return 0.5 * x * (1.0 + lax.erf(x * 0.7071067811865476))


def _sgu_chunk(au, av, ag, sg, ws, bsb):
    u, v = _gelu(au), _gelu(av)
    outs = []
    for g in range(4):
        sl = slice(128 * g, 128 * (g + 1))
        mixed = mm(ws[g], _rms(v[:, sl], sg[:, sl])) + bsb[g]
        outs.append(u[:, sl] * mixed * jax.nn.silu(ag[:, sl]))
    return jnp.concatenate(outs, axis=-1)


def _sgu_specs():
    rows = SGU_CHUNK * SGU_PER_STEP
    zspec = lambda c: pl.BlockSpec((rows, 512), lambda n: (n, c))
    wspec = pl.BlockSpec((4, 128, 128), lambda n: (0, 0, 0))
    return rows, [zspec(0), zspec(1), zspec(2), _row(512), wspec, wspec]


def sgu_fwd(z, sg, ws, bsb):
    rows, in_specs = _sgu_specs()

    def kern(au_ref, av_ref, ag_ref, sg_ref, ws_ref, bs_ref, o_ref):
        for c in range(SGU_PER_STEP):
            sl = slice(c * SGU_CHUNK, (c + 1) * SGU_CHUNK)
            o_ref[sl, :] = _sgu_chunk(au_ref[sl, :], av_ref[sl, :], ag_ref[sl, :], sg_ref[...], ws_ref[...],
                                      bs_ref[...])

    return pl.pallas_call(
        kern, name="sgu_fwd", grid=(SEQ // rows,), in_specs=in_specs,
        out_specs=pl.BlockSpec((rows, 512), lambda n: (n, 0)),
        out_shape=jax.ShapeDtypeStruct((SEQ, 512), F32),
        compiler_params=_cparams(("arbitrary",)),
    )(z, z, z, sg, ws, bsb)


def sgu_bwd(z, sg, ws, bsb, dcat):
    rows, in_specs = _sgu_specs()

    def kern(au_ref, av_ref, ag_ref, sg_ref, ws_ref, bs_ref, do_ref, dz_ref, dsg_ref, dws_ref, dbs_ref):
        @pl.when(pl.program_id(0) == 0)
        def _():
            dsg_ref[...] = jnp.zeros_like(dsg_ref)
            dws_ref[...] = jnp.zeros_like(dws_ref)
            dbs_ref[...] = jnp.zeros_like(dbs_ref)

        for c in range(SGU_PER_STEP):
            sl = slice(c * SGU_CHUNK, (c + 1) * SGU_CHUNK)
            _, vjp = jax.vjp(_sgu_chunk, au_ref[sl, :], av_ref[sl, :], ag_ref[sl, :], sg_ref[...], ws_ref[...],
                             bs_ref[...])
            dau, dav, dag, dsg, dws, dbs = vjp(do_ref[sl, :])
            dz_ref[sl, 0:512] = dau.astype(BF16)
            dz_ref[sl, 512:1024] = dav.astype(BF16)
            dz_ref[sl, 1024:1536] = dag.astype(BF16)
            dsg_ref[...] += dsg
            dws_ref[...] += dws
            dbs_ref[...] += dbs

        @pl.when(pl.program_id(0) == pl.num_programs(0) - 1)
        def _():
            dbs_ref[...] = jnp.broadcast_to(jnp.sum(dbs_ref[...], axis=-1, keepdims=True), dbs_ref.shape)

    wspec = pl.BlockSpec((4, 128, 128), lambda n: (0, 0, 0))
    return pl.pallas_call(
        kern, name="sgu_bwd", grid=(SEQ // rows,),
        in_specs=in_specs + [pl.BlockSpec((rows, 512), lambda n: (n, 0))],
        out_specs=[pl.BlockSpec((rows, 1536), lambda n: (n, 0)), _row(512), wspec, wspec],
        out_shape=[jax.ShapeDtypeStruct((SEQ, 1536), BF16), jax.ShapeDtypeStruct((1, 512), F32),
                   jax.ShapeDtypeStruct((4, 128, 128), F32), jax.ShapeDtypeStruct((4, 128, 128), F32)],
        compiler_params=_cparams(("arbitrary",)),
    )(z, z, z, sg, ws, bsb, dcat)


_DR_OFF = (7, 3, -1)


def _row_valid(v, rr, j):
    return (j < 8, rr <= j < rr + 8, 4 <= j < 12)[v]


def _col_window():
    q = lax.broadcasted_iota(jnp.int32, (GRID_W, 128), 0)
    kc = lax.broadcasted_iota(jnp.int32, (GRID_W, 128), 1) % GRID_W
    c0 = jnp.clip(q - 8, 0, GRID_W - 16)
    return (kc >= c0) & (kc < c0 + 16)


def rpb_tables(rpb2):
    def kern(r_ref, b_ref):
        base = r_ref[0]
        lo = lax.broadcasted_iota(jnp.int32, (1, 128), 1) < GRID_W
        win = _col_window()
        tiles = {}
        for v in range(3):
            for rr in range(QROWS):
                for jp in range(KROWS // 2):
                    j0, j1 = 2 * jp, 2 * jp + 1
                    ok0, ok1 = _row_valid(v, rr, j0), _row_valid(v, rr, j1)
                    key = (j0 - rr + _DR_OFF[v], ok0, ok1) if (ok0 or ok1) else None
                    if key not in tiles:
                        if key is None:
                            tiles[key] = jnp.full((GRID_W, 128), NEG_INF, F32)
                        else:
                            d0 = key[0]
                            r0 = base[d0:d0 + 1, :] if ok0 else jnp.zeros((1, 128), F32)
                            r1 = base[d0 + 1:d0 + 2, :] if ok1 else jnp.zeros((1, 128), F32)
                            y = jnp.broadcast_to(jnp.where(lo, r0, r1), (GRID_W, 128))
                            y = pltpu.roll(pltpu.roll(y, 128 - 15, 1), 0, 1, stride=1, stride_axis=0)
                            tiles[key] = jnp.where(win & jnp.where(lo, ok0, ok1), y, NEG_INF)
                    b_ref[v, 0, rr * GRID_W:(rr + 1) * GRID_W, jp * 128:(jp + 1) * 128] = tiles[key]

    return pl.pallas_call(
        kern, name="rpb_tables", grid=(HEADS,),
        in_specs=[pl.BlockSpec((1, 15, 128), lambda h: (h, 0, 0))],
        out_specs=pl.BlockSpec((3, 1, QBLK, KBLK), lambda h: (0, h, 0, 0)),
        out_shape=jax.ShapeDtypeStruct((3, HEADS, QBLK, KBLK), F32),
        compiler_params=_cparams(("arbitrary",)),
    )(rpb2)


def rpb_bwd(dbias):
    def kern(g0_ref, g1_ref, g2_ref, o_ref):
        g_refs = (g0_ref, g1_ref, g2_ref)
        lo = lax.broadcasted_iota(jnp.int32, (1, 128), 1) < GRID_W
        ri = lax.broadcasted_iota(jnp.int32, (GRID_W, GRID_W), 0)
        ci = lax.broadcasted_iota(jnp.int32, (GRID_W, GRID_W), 1)
        flip = (ri + ci == GRID_W - 1).astype(F32)
        groups = {}
        for v in range(3):
            for rr in range(QROWS):
                for jp in range(KROWS // 2):
                    j0, j1 = 2 * jp, 2 * jp + 1
                    ok0, ok1 = _row_valid(v, rr, j0), _row_valid(v, rr, j1)
                    if not (ok0 or ok1):
                        continue
                    g = g_refs[v][0, rr * GRID_W:(rr + 1) * GRID_W, jp * 128:(jp + 1) * 128]
                    key = (j0 - rr + _DR_OFF[v], ok0, ok1)
                    groups[key] = g if key not in groups else groups[key] + g
        acc = [jnp.zeros((1, 128), F32) for _ in range(15)]
        for (d0, ok0, ok1), g in groups.items():
            g = lax.dot_general(flip, g, (((1,), (0,)), ((), ())), precision=lax.Precision.HIGHEST,
                                preferred_element_type=F32)
            g = pltpu.roll(pltpu.roll(g, 128 - 48, 1), 0, 1, stride=1, stride_axis=0)
            s = jnp.sum(g, axis=0, keepdims=True)
            if ok0:
                acc[d0] = acc[d0] + jnp.where(lo, s, 0.0)
            if ok1:
                acc[d0 + 1] = acc[d0 + 1] + jnp.where(lo, 0.0, s)
        for d in range(15):
            o_ref[0, d:d + 1, :] = acc[d] + pltpu.roll(acc[d], GRID_W, 1)

    return pl.pallas_call(
        kern, name="rpb_bwd", grid=(HEADS,),
        in_specs=[pl.BlockSpec((1, QBLK, KBLK), lambda h: (h, 0, 0))] * 3,
        out_specs=pl.BlockSpec((1, 15, 128), lambda h: (h, 0, 0)),
        out_shape=jax.ShapeDtypeStruct((HEADS, 15, 128), F32),
        compiler_params=_cparams(("arbitrary",)),
    )(*dbias)


def _scaled_q(q_raw, qg):
    return _pair_rms(q_raw, qg) * (HDIM ** -0.5)


def _head_lanes():
    lo = lax.broadcasted_iota(jnp.int32, (1, 2 * HDIM), 1) < HDIM
    return lo, jnp.logical_not(lo)


def _attn_step(q_raw, kn, v, ckn, cv, bias2, qg):
    qn = _scaled_q(q_raw, qg)
    out = rden = None
    probs = []
    for a, mine in enumerate(_head_lanes()):
        qa = jnp.where(mine, qn, 0.0)
        s_lat = mm_nt(qa, kn) + bias2[a]
        s_ctx = mm_nt(qa, ckn)
        m = jnp.maximum(jnp.max(s_lat, axis=-1, keepdims=True), jnp.max(s_ctx, axis=-1, keepdims=True))
        p_lat = jnp.exp(s_lat - m)
        p_ctx = jnp.exp(s_ctx - m)
        den = jnp.sum(p_lat, axis=-1, keepdims=True) + jnp.sum(p_ctx, axis=-1, keepdims=True)
        p_lat, p_ctx = p_lat.astype(BF16), p_ctx.astype(BF16)
        o = jnp.where(mine, (mm(p_lat, v) + mm(p_ctx, cv)) / den, 0.0)
        rr = jnp.where(mine, 1.0 / den, 0.0)
        out, rden = (o, rr) if out is None else (out + o, rden + rr)
        probs.append((p_lat, p_ctx))
    return out, rden, probs


def _attn_step_bwd(q_raw, kn, v, ckn, cv, qg, bg, o, rden, probs, dout):
    sig = jax.nn.sigmoid(bg)
    do = dout * (bg * sig)
    dbg = dout * o * (sig * (1.0 + bg * (1.0 - sig)))
    qn, qn_vjp = jax.vjp(_scaled_q, q_raw, qg)
    row_dot = do * o
    dqn = dkn = dv = dckn = dcv = None
    dbias = []
    for mine, (p_lat, p_ctx) in zip(_head_lanes(), probs):
        qa = jnp.where(mine, qn, 0.0)
        r = jnp.max(jnp.where(mine, rden, 0.0), axis=-1, keepdims=True)
        doa = jnp.where(mine, do, 0.0) * r
        delta = jnp.sum(jnp.where(mine, row_dot, 0.0), axis=-1, keepdims=True) * r
        ds_lat = p_lat.astype(F32) * (mm_nt(doa, v) - delta)
        ds_ctx = p_ctx.astype(F32) * (mm_nt(doa, cv) - delta)
        parts = (jnp.where(mine, mm(ds_lat, kn) + mm(ds_ctx, ckn), 0.0), mm_tn(qa, ds_lat), mm_tn(doa, p_lat),
                 mm_tn(qa, ds_ctx), mm_tn(doa, p_ctx))
        if dqn is None:
            dqn, dkn, dv, dckn, dcv = parts
        else:
            dqn, dkn, dv, dckn, dcv = (acc + new for acc, new in zip((dqn, dkn, dv, dckn, dcv), parts))
        dbias.append(ds_lat)
    dq, dqg = qn_vjp(dqn)
    return dq, dkn, dv, dckn, dcv, dbias, dqg, dbg


def _kblock(i):
    return jnp.clip(i - 1, 0, (SEQ - KBLK) // QBLK)


def _kstart(i):
    return pl.multiple_of(_kblock(i) * QBLK, QBLK)


ATTN_STEPS = NQBLK // 2
ATTN_ROWS = 2 * QBLK
KCOLS = QBLK


def _attn_in_specs():
    return [
        pl.BlockSpec((ATTN_ROWS, 128), lambda p, i: (i, ZQ + p)),
        pl.BlockSpec((SEQ, 128), lambda p, i: (0, ZK + p)),
        pl.BlockSpec((SEQ, 128), lambda p, i: (0, ZV + p)),
        pl.BlockSpec((ATTN_ROWS, 128), lambda p, i: (i, ZG + p)),
        pl.BlockSpec((CTX, 128), lambda p, i: (0, 2 + p)),
        pl.BlockSpec((CTX, 128), lambda p, i: (0, 6 + p)),
    ]


def _bias_specs():
    bias_spec = lambda variant: pl.BlockSpec((1, 2, QBLK, KBLK), lambda p, i: (variant(i), p, 0, 0))
    return [bias_spec(lambda i: jnp.where(i == 0, 0, 1)),
            bias_spec(lambda i: jnp.where(i == ATTN_STEPS - 1, 2, 1))]


def _prob_specs():
    return [pl.BlockSpec((2, ATTN_ROWS, KBLK), lambda p, i: (p, i, 0)),
            pl.BlockSpec((2, ATTN_ROWS, CTX), lambda p, i: (p, i, 0))]


NORM_ROWS = 512


def _norm_keys(k_ref, ck_ref, kg_ref, kn_scr, ckn_scr):
    def body(c, carry):
        sl = pl.ds(pl.multiple_of(c * NORM_ROWS, NORM_ROWS), NORM_ROWS)
        kn_scr[sl, :] = _pair_rms(k_ref[sl, :], kg_ref[...])
        return carry

    lax.fori_loop(0, SEQ // NORM_ROWS, body, 0)
    ckn_scr[...] = _pair_rms(ck_ref[...], kg_ref[...])


def attn_fwd(z, zc, bias, qg2, kg2):
    def kern(q_ref, k_ref, v_ref, bg_ref, ck_ref, cv_ref, be_ref, bo_ref, qg_ref, kg_ref,
             ob_ref, o_ref, rden_ref, pl_ref, pc_ref, kn_scr, ckn_scr):
        i = pl.program_id(1)

        @pl.when(i == 0)
        def _():
            _norm_keys(k_ref, ck_ref, kg_ref, kn_scr, ckn_scr)

        for b, b_ref in enumerate((be_ref, bo_ref)):
            rows = slice(b * QBLK, (b + 1) * QBLK)
            ks = pl.ds(_kstart(2 * i + b), KBLK)
            o, rden, probs = _attn_step(q_ref[rows, :], kn_scr[ks, :], v_ref[ks, :], ckn_scr[...], cv_ref[...],
                                        b_ref[0], qg_ref[...])
            ob_ref[rows, :] = o * jax.nn.silu(bg_ref[rows, :])
            o_ref[rows, :] = o
            rden_ref[rows, :] = rden
            for a, (p_lat, p_ctx) in enumerate(probs):
                pl_ref[a, rows, :] = p_lat
                pc_ref[a, rows, :] = p_ctx

    qblk = pl.BlockSpec((ATTN_ROWS, 128), lambda p, i: (i, p))
    return pl.pallas_call(
        kern, name="attn_fwd", grid=(NPAIR, ATTN_STEPS),
        in_specs=_attn_in_specs() + _bias_specs() + [_row(128), _row(128)], out_specs=[qblk] * 3 + _prob_specs(),
        out_shape=[jax.ShapeDtypeStruct((SEQ, 512), F32)] * 3
        + [jax.ShapeDtypeStruct((HEADS, SEQ, KBLK), BF16), jax.ShapeDtypeStruct((HEADS, SEQ, CTX), BF16)],
        scratch_shapes=[pltpu.VMEM((SEQ, 128), F32), pltpu.VMEM((CTX, 128), F32)],
        compiler_params=_cparams(("arbitrary", "arbitrary"), 40 * 1024 * 1024),
    )(z, z, z, z, zc, zc, bias, bias, qg2, kg2)


def attn_bwd(z, zc, qg2, kg2, dcat, saved):
    def kern(q_ref, k_ref, v_ref, bg_ref, ck_ref, cv_ref, qg_ref, kg_ref, do_ref, o_ref, rden_ref, pl_ref, pc_ref,
             dq_ref, dk_ref, dv_ref, dbg_ref, dck_ref, dcv_ref, db0_ref, db1_ref, db2_ref, dqg_ref, dkg_ref,
             kn_scr, ckn_scr, dknt_scr, dvt_scr, dcknt_scr, dcvt_scr):
        p, i = pl.program_id(0), pl.program_id(1)
        last = i == ATTN_STEPS - 1

        @pl.when(i == 0)
        def _():
            _norm_keys(k_ref, ck_ref, kg_ref, kn_scr, ckn_scr)
            dknt_scr[...] = jnp.zeros_like(dknt_scr)
            dvt_scr[...] = jnp.zeros_like(dvt_scr)
            dcknt_scr[...] = jnp.zeros_like(dcknt_scr)
            dcvt_scr[...] = jnp.zeros_like(dcvt_scr)

        @pl.when((i == 0) & (p == 0))
        def _():
            dqg_ref[...] = jnp.zeros_like(dqg_ref)
            dkg_ref[...] = jnp.zeros_like(dkg_ref)

        db = []
        for b in range(2):
            rows = slice(b * QBLK, (b + 1) * QBLK)
            kb = _kblock(2 * i + b)
            ks = pl.ds(_kstart(2 * i + b), KBLK)
            probs = [(pl_ref[a, rows, :], pc_ref[a, rows, :]) for a in range(2)]
            dq, dknt, dvt, dcknt, dcvt, dbb, dqg, dbg = _attn_step_bwd(
                q_ref[rows, :], kn_scr[ks, :], v_ref[ks, :], ckn_scr[...], cv_ref[...], qg_ref[...],
                bg_ref[rows, :], o_ref[rows, :], rden_ref[rows, :], probs, do_ref[rows, :])
            dq_ref[rows, :] = dq.astype(BF16)
            dbg_ref[rows, :] = dbg.astype(BF16)
            for n in range(KBLK // KCOLS):
                cols = slice(n * KCOLS, (n + 1) * KCOLS)
                dknt_scr[kb + n] += dknt[:, cols]
                dvt_scr[kb + n] += dvt[:, cols]
            dcknt_scr[...] += dcknt
            dcvt_scr[...] += dcvt
            dqg_ref[...] += dqg
            db.append(dbb)

        @pl.when(i == 0)
        def _():
            for a in range(2):
                db0_ref[a] = db[0][a]
                db1_ref[a] = db[1][a]

        @pl.when((i > 0) & jnp.logical_not(last))
        def _():
            for a in range(2):
                db1_ref[a] += db[0][a] + db[1][a]

        @pl.when(last)
        def _():
            for a in range(2):
                db1_ref[a] += db[0][a]
                db2_ref[a] = db[1][a]

        @pl.when(last)
        def _():
            def body(c, dkg):
                sl = pl.ds(pl.multiple_of(c * NORM_ROWS, NORM_ROWS), NORM_ROWS)
                blocks = range(NORM_ROWS // KCOLS)
                dkn = jnp.concatenate([dknt_scr[c * len(blocks) + n].T for n in blocks], axis=0)
                dv = jnp.concatenate([dvt_scr[c * len(blocks) + n].T for n in blocks], axis=0)
                _, nvjp = jax.vjp(_pair_rms, k_ref[sl, :], kg_ref[...])
                dk, dg = nvjp(dkn)
                dk_ref[sl, :] = dk.astype(BF16)
                dv_ref[sl, :] = dv.astype(BF16)
                return dkg + dg

            dkg = lax.fori_loop(0, SEQ // NORM_ROWS, body, jnp.zeros((1, 128), F32))
            _, nvjp = jax.vjp(_pair_rms, ck_ref[...], kg_ref[...])
            dck, dg = nvjp(dcknt_scr[...].T)
            dck_ref[...] = dck
            dcv_ref[...] = dcvt_scr[...].T
            dkg_ref[...] += dkg + dg

        @pl.when(last & (p == NPAIR - 1))
        def _():
            dqg_ref[...] = dqg_ref[...] + pltpu.roll(dqg_ref[...], HDIM, 1)
            dkg_ref[...] = dkg_ref[...] + pltpu.roll(dkg_ref[...], HDIM, 1)

    blk = lambda rows: pl.BlockSpec((rows, 128), lambda p, i: (0, p))
    qblk = pl.BlockSpec((ATTN_ROWS, 128), lambda p, i: (i, p))
    dbias = pl.BlockSpec((2, QBLK, KBLK), lambda p, i: (p, 0, 0))
    return pl.pallas_call(
        kern, name="attn_bwd", grid=(NPAIR, ATTN_STEPS),
        in_specs=_attn_in_specs() + [_row(128), _row(128), pl.BlockSpec((ATTN_ROWS, 128), lambda p, i: (i, 4 + p)),
                                     qblk, qblk] + _prob_specs(),
        out_specs=[qblk, blk(SEQ), blk(SEQ), qblk, blk(CTX), blk(CTX), dbias, dbias, dbias, _row(128), _row(128)],
        out_shape=[jax.ShapeDtypeStruct((SEQ, 512), BF16)] * 4 + [jax.ShapeDtypeStruct((CTX, 512), F32)] * 2
        + [jax.ShapeDtypeStruct((HEADS, QBLK, KBLK), F32)] * 3
        + [jax.ShapeDtypeStruct((1, 128), F32), jax.ShapeDtypeStruct((1, 128), F32)],
        scratch_shapes=[pltpu.VMEM((SEQ, 128), F32), pltpu.VMEM((CTX, 128), F32),
                        pltpu.VMEM((SEQ // KCOLS, 128, KCOLS), F32), pltpu.VMEM((SEQ // KCOLS, 128, KCOLS), F32),
                        pltpu.VMEM((128, CTX), F32), pltpu.VMEM((128, CTX), F32)],
        compiler_params=_cparams(("arbitrary", "arbitrary"), VMEM_BIG),
    )(z, z, z, z, zc, zc, qg2, kg2, dcat, *saved)


def outproj(out_a, out_b, x, target, gate, wo):
    tl = 512

    def kern(a_ref, b_ref, x_ref, t_ref, g_ref, w_ref, loss_ref, dy_ref, dcat_ref, dg_ref, dw_ref):
        @pl.when(pl.program_id(0) == 0)
        def _():
            loss_ref[...] = jnp.zeros_like(loss_ref)
            dg_ref[...] = jnp.zeros_like(dg_ref)
            dw_ref[...] = jnp.zeros_like(dw_ref)

        a, b = a_ref[...].astype(BF16), b_ref[...].astype(BF16)
        mix = (jnp.dot(a, w_ref[0:512, :], preferred_element_type=F32)
               + jnp.dot(b, w_ref[512:1024, :], preferred_element_type=F32))
        err = x_ref[...] + g_ref[...] * mix - t_ref[...]
        loss_ref[...] += 0.5 * jnp.sum(jnp.mean(err * err, axis=-1))
        dy = err * (1.0 / DM)
        dy_ref[...] = dy
        dg_ref[...] += jnp.sum(dy * mix, axis=0, keepdims=True)
        dmix = (g_ref[...] * dy).astype(BF16)
        dcat_ref[...] = lax.dot_general(dmix, w_ref[...], (((1,), (1,)), ((), ())), preferred_element_type=F32)
        dw_ref[0:512, :] += lax.dot_general(a, dmix, (((0,), (0,)), ((), ())), preferred_element_type=F32)
        dw_ref[512:1024, :] += lax.dot_general(b, dmix, (((0,), (0,)), ((), ())), preferred_element_type=F32)

    tile = lambda w: pl.BlockSpec((tl, w), lambda t: (t, 0))
    whole = pl.BlockSpec((DM, DM), lambda t: (0, 0))
    return pl.pallas_call(
        kern, name="outproj", grid=(SEQ // tl,),
        in_specs=[tile(512), tile(512), tile(DM), tile(DM), _row(DM), whole],
        out_specs=[pl.BlockSpec((8, 128), lambda t: (0, 0)), tile(DM), tile(DM), _row(DM), whole],
        out_shape=[jax.ShapeDtypeStruct((8, 128), F32), jax.ShapeDtypeStruct((SEQ, DM), F32),
                   jax.ShapeDtypeStruct((SEQ, DM), F32), jax.ShapeDtypeStruct((1, DM), F32),
                   jax.ShapeDtypeStruct((DM, DM), F32)],
        compiler_params=_cparams(("arbitrary",), 48 * 1024 * 1024),
    )(out_a, out_b, x, target, gate, wo)


def _pieces(sources):
    out = []
    for name, c0, c1 in sources:
        for j in range(NCHIP):
            lo, hi = max(c0, j * SHARD_IN), min(c1, (j + 1) * SHARD_IN)
            if lo < hi:
                out.append((j, lo - j * SHARD_IN, hi - j * SHARD_IN, name, lo - c0, hi - c0))
    return out


DZ_PIECES = _pieces((("a", 0, 1536), ("q", 1536, 2048), ("k", 2048, 2560), ("v", 2560, 3072), ("g", 3072, DIN)))
DZC_PIECES = _pieces((("k", 2048, 2560), ("v", 2560, 3072)))
_NT = (((1,), (1,)), ((), ()))


DH_SUBTILES = 2


def _dz_specs(tl):
    return [pl.BlockSpec((tl, 1536), lambda t: (t, 0))] + [pl.BlockSpec((tl, 512), lambda t: (t, 0))] * 4


def dh_bwd(dz_parts, w_full, x, dy, shift, scale, norm_g, dg_ctx):
    tl = 512
    nt = SEQ // tl

    def kern(a_ref, q_ref, k_ref, v_ref, g_ref, w_ref, x_ref, dy_ref, sh_ref, sc_ref, gn_ref, dgc_ref,
             gx_ref, dsh_ref, dsc_ref, dg_ref):
        @pl.when(pl.program_id(0) == 0)
        def _():
            dsh_ref[...] = jnp.zeros_like(dsh_ref)
            dsc_ref[...] = jnp.zeros_like(dsc_ref)
            dg_ref[...] = dgc_ref[...]

        src = dict(a=a_ref, q=q_ref, k=k_ref, v=v_ref, g=g_ref)
        for sub in range(DH_SUBTILES):
            rows = slice(sub * tl // DH_SUBTILES, (sub + 1) * tl // DH_SUBTILES)
            dh = None
            for j, l0, l1, name, s0, s1 in DZ_PIECES:
                part = lax.dot_general(src[name][rows, s0:s1], w_ref[j, :, l0:l1], _NT, preferred_element_type=F32)
                dh = part if dh is None else dh + part
            _, vjp = jax.vjp(_modulated, x_ref[rows, :], gn_ref[...], sc_ref[...], sh_ref[...])
            dx, dg, dsc, dsh = vjp(dh)
            gx_ref[rows, :] = dy_ref[rows, :] + dx
            dg_ref[...] += dg
            dsc_ref[...] += dsc
            dsh_ref[...] += dsh

    tile = pl.BlockSpec((tl, DM), lambda t: (t, 0))
    return pl.pallas_call(
        kern, name="dh_bwd", grid=(nt,),
        in_specs=_dz_specs(tl) + [pl.BlockSpec((NCHIP, DM, SHARD_IN), lambda t: (0, 0, 0)), tile, tile, _row(DM),
                                  _row(DM), _row(DM), _row(DM)],
        out_specs=[tile, _row(DM), _row(DM), _row(DM)],
        out_shape=[jax.ShapeDtypeStruct((SEQ, DM), F32)] + [jax.ShapeDtypeStruct((1, DM), F32)] * 3,
        compiler_params=_cparams(("arbitrary",), 48 * 1024 * 1024),
    )(*dz_parts, w_full, x, dy, shift, scale, norm_g, dg_ctx)


def dw_bwd(h, dz_parts, hc, dck, dcv, g_out):
    tl = 512
    nt = SEQ // tl
    (rhi, wi), (rho, wo) = RS_SHAPES

    def kern(h_ref, a_ref, q_ref, k_ref, v_ref, g_ref, hc_ref, dck_ref, dcv_ref, go_hbm,
             wire_i, keep_i, wire_o, keep_o, acc, rcv_i, mine_o, rcv_o, load_sem, send_sems, recv_sems):
        t = pl.program_id(0)
        x, y, c = _me()
        k = 2 * x + y
        sib = _flip(1)
        half = lambda hh, rh: pl.ds(pl.multiple_of(hh * rh, rh), rh)
        load_o = pltpu.make_async_copy(go_hbm.at[:, half(c, rho), :], mine_o, load_sem)
        pair_o = _rcopy(go_hbm.at[:, half(1 - c, rho), :], rcv_o, send_sems, recv_sems, 0, sib)
        pair_i = _rcopy(acc.at[:, half(1 - c, rhi), :], rcv_i, send_sems, recv_sems, 1, sib)

        @pl.when(t == 0)
        def _():
            load_o.start()
            pair_o.start()
            acc[...] = jnp.zeros_like(acc)
            hct = hc_ref[...].T
            csrc = dict(k=dck_ref, v=dcv_ref)
            for j, l0, l1, name, s0, s1 in DZC_PIECES:
                acc[j, :, l0:l1] += jnp.dot(hct, csrc[name][:, s0:s1].astype(BF16), preferred_element_type=F32)

        ht = h_ref[...].T
        src = dict(a=a_ref, q=q_ref, k=k_ref, v=v_ref, g=g_ref)
        for j, l0, l1, name, s0, s1 in DZ_PIECES:
            acc[j, :, l0:l1] += jnp.dot(ht, src[name][:, s0:s1], preferred_element_type=F32)

        @pl.when(t == nt - 1)
        def _():
            pair_i.start()
            load_o.wait()
            pair_o.wait_recv()
            for j in range(NCHIP):
                wire_o[j] = (mine_o[j] + rcv_o[j]).astype(BF16)
            keep_o[...] = mine_o[k] + rcv_o[k]
            pair_i.wait_recv()
            mine = half(c, rhi)
            for j in range(NCHIP):
                wire_i[j] = (acc[j, mine, :] + rcv_i[j]).astype(BF16)
            keep_i[...] = acc[k, mine, :] + rcv_i[k]
            pair_o.wait_send()
            pair_i.wait_send()

    whole = lambda *shape: pl.BlockSpec(shape, lambda t: (0,) * len(shape))
    return pl.pallas_call(
        kern, name="dw_bwd", grid=(nt,),
        in_specs=[pl.BlockSpec((tl, DM), lambda t: (t, 0))] + _dz_specs(tl)
        + [whole(CTX, DM), whole(CTX, 512), whole(CTX, 512), pl.BlockSpec(memory_space=pl.ANY)],
        out_specs=[whole(NCHIP, rhi, wi), whole(rhi, wi), whole(NCHIP, rho, wo), whole(rho, wo)],
        out_shape=[jax.ShapeDtypeStruct((NCHIP, rhi, wi), BF16), jax.ShapeDtypeStruct((rhi, wi), F32),
                   jax.ShapeDtypeStruct((NCHIP, rho, wo), BF16), jax.ShapeDtypeStruct((rho, wo), F32)],
        scratch_shapes=[pltpu.VMEM((NCHIP, DM, SHARD_IN), F32), pltpu.VMEM((NCHIP, rhi, wi), F32),
                        pltpu.VMEM((NCHIP, rho, wo), F32), pltpu.VMEM((NCHIP, rho, wo), F32),
                        pltpu.SemaphoreType.DMA(()), pltpu.SemaphoreType.DMA((2,)), pltpu.SemaphoreType.DMA((2,))],
        compiler_params=_cparams(("arbitrary",), VMEM_BIG),
    )(h, *dz_parts, hc, dck, dcv, g_out)


def ctx_bwd(dck, dcv, w_full, ctx, cshift, cscale, norm_g):
    def kern(dck_ref, dcv_ref, w_ref, c_ref, sh_ref, sc_ref, g_ref, dsh_ref, dsc_ref, dg_ref):
        csrc = dict(k=dck_ref, v=dcv_ref)
        dhc = None
        for j, l0, l1, name, s0, s1 in DZC_PIECES:
            part = lax.dot_general(csrc[name][:, s0:s1].astype(BF16), w_ref[j, :, l0:l1], _NT,
                                   preferred_element_type=F32)
            dhc = part if dhc is None else dhc + part
        _, vjp = jax.vjp(lambda g, sc, sh: _modulated(c_ref[...], g, sc, sh), g_ref[...], sc_ref[...], sh_ref[...])
        dg_ref[...], dsc_ref[...], dsh_ref[...] = vjp(dhc)

    whole = lambda r, c: pl.BlockSpec((r, c), lambda i: (0, 0))
    return pl.pallas_call(
        kern, name="ctx_bwd", grid=(1,),
        in_specs=[whole(CTX, 512), whole(CTX, 512), pl.BlockSpec((NCHIP, DM, SHARD_IN), lambda i: (0, 0, 0)),
                  whole(CTX, DM), _row(DM), _row(DM), _row(DM)],
        out_specs=[_row(DM), _row(DM), _row(DM)],
        out_shape=[jax.ShapeDtypeStruct((1, DM), F32)] * 3,
        compiler_params=_cparams(("arbitrary",), 40 * 1024 * 1024),
    )(dck, dcv, w_full, ctx, cshift, cscale, norm_g)


def _lane_pad_rpb(rpb):
    r = jnp.pad(rpb, ((0, 0), (0, 0), (0, GRID_W - rpb.shape[-1])))
    return jnp.concatenate([r, r], axis=-1)


def local_step(chip, dev, x, c_vec, c_ctx, w_ada, b_shard, ctx, target, norm_g, sgu_g, w_s, b_s, q_g, k_g, rpb,
               w_in_shard, w_out_shard):
    bsb = jnp.broadcast_to(b_s[:, :, None], (4, 128, 128))
    qg2, kg2 = jnp.tile(q_g, (1, 2)), jnp.tile(k_g, (1, 2))

    z, h, w_in_full, w_out_full, mod_all, cs = inproj_fwd(chip, x, c_vec, c_ctx, w_ada, b_shard, norm_g, w_in_shard,
                                                          w_out_shard)
    mods = mod_all.transpose(1, 0, 2).reshape(CS_ROWS, 3 * DM)
    mod = lax.dynamic_slice(mods, (8 * dev, 0), (1, 3 * DM))
    shift, scale, gate = mod[:, :DM], mod[:, DM:2 * DM], mod[:, 2 * DM:]
    cshift, cscale = mods[8 * NDEV:8 * NDEV + 1, :DM], mods[8 * NDEV:8 * NDEV + 1, DM:2 * DM]
    zc, hc = ctx_fwd(ctx, cshift, cscale, norm_g, w_in_full)
    bias = rpb_tables(_lane_pad_rpb(rpb))
    out_a = sgu_fwd(z, sgu_g, w_s, bsb)
    out_b, *saved = attn_fwd(z, zc, bias, qg2, kg2)
    loss8, dy, dcat, dgate, dwo = outproj(out_a, out_b, x, target, gate, w_out_full.reshape(DM, DM))
    dz_a, dsg, dws, dbsb = sgu_bwd(z, sgu_g, w_s, bsb, dcat)
    dq, dk, dv, dbg, dck, dcv, db0, db1, db2, dqg2, dkg2 = attn_bwd(z, zc, qg2, kg2, dcat, saved)
    drpb = rpb_bwd((db0, db1, db2))[:, :, :rpb.shape[-1]]
    dz_parts = (dz_a, dq, dk, dv, dbg)
    dcshift, dcscale, dng_c = ctx_bwd(dck, dcv, w_in_full, ctx, cshift, cscale, norm_g)
    wire_i, keep_i, wire_o, keep_o = dw_bwd(h, dz_parts, hc, dck, dcv, dwo.reshape(NCHIP, SHARD_OUT, DM))
    *in_flight, token = rs_start(wire_i, wire_o)
    grad_x, dshift, dscale, dng = dh_bwd(dz_parts, w_in_full, x, dy, shift, scale, norm_g, dng_c + token[0, 0])
    got_i, got_o = rs_wait(*in_flight, dshift)
    return dict(
        loss=loss8[0:1, 0:1], grad_x=grad_x, rs=(keep_i, got_i, keep_o, got_o), cs=cs,
        dmod=jnp.concatenate([dshift, dscale, dgate], axis=-1),
        dcmod=jnp.concatenate([dcshift, dcscale, jnp.zeros((1, DM), F32)], axis=-1),
        d_norm_g=dng, d_sgu_g=dsg, d_w_s=dws, d_b_s=dbsb[:, :, 0],
        d_q_g=dqg2[:, :HDIM], d_k_g=dkg2[:, :HDIM], d_rpb=drpb)


def _me():
    return lax.axis_index("x"), lax.axis_index("y"), lax.axis_index("c")


def _flip(q):
    x, y, c = _me()
    return ((1 - x) if q & 4 else x, (1 - y) if q & 2 else y, (1 - c) if q & 1 else c)


def _chip_of(dev):
    return 2 * dev[0] + dev[1]


def _rcopy(src, dst, send_sems, recv_sems, k, dev):
    return pltpu.make_async_remote_copy(src_ref=src, dst_ref=dst, send_sem=send_sems.at[k], recv_sem=recv_sems.at[k],
                                        device_id=dev, device_id_type=MESH_ID)


_VMEM_SPEC = pl.BlockSpec(memory_space=pltpu.VMEM)
SLAB_ROWS = 80


RS_SHAPES = ((DM // 2, SHARD_IN), (SHARD_OUT // 2, DM))
_HBM_SPEC = pl.BlockSpec(memory_space=pltpu.HBM)
_SEM_SPEC = pl.BlockSpec(memory_space=pltpu.SEMAPHORE)
_IN_FLIGHT = pltpu.SideEffectType.DATAFLOW_SIDE_EFFECTING


def _rs_copies(wires, lands, send_sems, recv_sems):
    return [pltpu.make_async_remote_copy(
        src_ref=wires[n].at[_chip_of(_flip(q))], dst_ref=lands[n].at[q // 2 - 1],
        send_sem=send_sems.at[3 * n + q // 2 - 1], recv_sem=recv_sems.at[3 * n + q // 2 - 1],
        device_id=_flip(q), device_id_type=MESH_ID) for n in (0, 1) for q in (2, 4, 6)]


def rs_start(wire_i, wire_o):
    lands = [lax.empty((NCHIP - 1, rh, w), BF16) for rh, w in RS_SHAPES]

    def body(wi_ref, wo_ref, li_ref, lo_ref, send_sems, recv_sems, wi_thru, wo_thru, li_thru, lo_thru, token):
        for cp in _rs_copies((wi_ref, wo_ref), (li_ref, lo_ref), send_sems, recv_sems):
            cp.start()
        token[...] = jnp.zeros_like(token)

    hbm = lambda a: pltpu.HBM(a.shape, a.dtype)
    return pl.pallas_call(
        body, name="rs_start",
        out_shape=(pltpu.SemaphoreType.DMA((6,)), pltpu.SemaphoreType.DMA((6,)), hbm(wire_i), hbm(wire_o),
                   hbm(lands[0]), hbm(lands[1]), jax.ShapeDtypeStruct((8, 128), F32)),
        in_specs=(_HBM_SPEC,) * 4, out_specs=(_SEM_SPEC, _SEM_SPEC) + (_HBM_SPEC,) * 4 + (_VMEM_SPEC,),
        input_output_aliases={0: 2, 1: 3, 2: 4, 3: 5},
        compiler_params=pltpu.CompilerParams(has_side_effects=_IN_FLIGHT),
    )(*[pltpu.with_memory_space_constraint(a, pltpu.HBM) for a in (wire_i, wire_o, *lands)])


def rs_wait(send_sems, recv_sems, wire_i, wire_o, land_i, land_o, after):
    def body(wi_ref, wo_ref, li_ref, lo_ref, send_sems, recv_sems, after_ref, wi_dead, wo_dead, gi_ref, go_ref):
        for cp in _rs_copies((wi_ref, wo_ref), (li_ref, lo_ref), send_sems, recv_sems):
            cp.wait_send()
            cp.wait_recv()

    hbm = lambda a: pltpu.HBM(a.shape, a.dtype)
    return pl.pallas_call(
        body, name="rs_wait", out_shape=(hbm(wire_i), hbm(wire_o), hbm(land_i), hbm(land_o)),
        in_specs=(_HBM_SPEC,) * 4 + (_SEM_SPEC, _SEM_SPEC, pl.BlockSpec(memory_space=pl.ANY)),
        out_specs=(_HBM_SPEC,) * 4, input_output_aliases={0: 0, 1: 1, 2: 2, 3: 3},
        compiler_params=pltpu.CompilerParams(has_side_effects=_IN_FLIGHT),
    )(wire_i, wire_o, land_i, land_o, send_sems, recv_sems, after)[2:]


def final_reduce(keep_i, got_i, keep_o, got_o, slab):
    def kern(ki_ref, gi_ref, ko_ref, go_ref, s_ref, gin_ref, gout_ref, all_ref, tot_ref, send_sems, recv_sems):
        x, y, c = _me()
        sib = _flip(1)
        dev = lambda d: 4 * d[0] + 2 * d[1] + d[2]
        me = dev((x, y, c))

        def slab_copy(idx, owner, to):
            return _rcopy(all_ref.at[dev(owner)], all_ref.at[dev(owner)], send_sems, recv_sems, idx, to)

        all_ref[me] = s_ref[...]
        first = [slab_copy(0, (x, y, c), sib)] + [slab_copy(q // 2, (x, y, c), _flip(q)) for q in (2, 4, 6)]
        for cp in first:
            cp.start()

        shares = []
        for n, (keep, got, out) in enumerate(((ki_ref, gi_ref, gin_ref), (ko_ref, go_ref, gout_ref))):
            rh = RS_SHAPES[n][0]
            half = lambda hh, rh=rh: pl.ds(pl.multiple_of(hh * rh, rh), rh)
            out[half(c), :] = ((keep[...] + got[0].astype(F32)) + got[1].astype(F32)) + got[2].astype(F32)
            share = _rcopy(out.at[half(c), :], out.at[half(c), :], send_sems, recv_sems, 7 + n, sib)
            share.start()
            shares.append((share, _rcopy(out.at[half(1 - c), :], out.at[half(1 - c), :], send_sems, recv_sems, 7 + n,
                                         sib)))

        passed = []
        for q in (2, 4, 6):
            slab_copy(q // 2, _flip(q), (x, y, c)).wait_recv()
            cp = slab_copy(3 + q // 2, _flip(q), sib)
            cp.start()
            passed.append(cp)
        slab_copy(0, sib, (x, y, c)).wait_recv()
        for q in (2, 4, 6):
            slab_copy(3 + q // 2, _flip(q | 1), (x, y, c)).wait_recv()
        tot = all_ref[0]
        for d in range(1, NDEV):
            tot = tot + all_ref[d]
        tot_ref[...] = tot
        for share, arrival in shares:
            arrival.wait_recv()
            share.wait_send()
        for cp in first + passed:
            cp.wait_send()

    (rhi, wi), (rho, wo) = RS_SHAPES
    return pl.pallas_call(
        kern, name="final_reduce", in_specs=[_VMEM_SPEC] * 5, out_specs=[_VMEM_SPEC] * 4,
        out_shape=[jax.ShapeDtypeStruct((2 * rhi, wi), F32), jax.ShapeDtypeStruct((2 * rho, wo), F32),
                   jax.ShapeDtypeStruct((NDEV, SLAB_ROWS, DM), F32), jax.ShapeDtypeStruct((SLAB_ROWS, DM), F32)],
        scratch_shapes=[pltpu.SemaphoreType.DMA((9,)), pltpu.SemaphoreType.DMA((9,))],
        compiler_params=pltpu.CompilerParams(vmem_limit_bytes=40 * 1024 * 1024),
    )(keep_i, got_i, keep_o, got_o, slab)


def ada_bwd(a_in, dm, dm_shard, w_ada, c_ctx):
    def kern(a_ref, dm_ref, dms_ref, w_ref, cc_ref, dw_ref, db_ref, dcc_ref, parts, send_sems, recv_sems):
        x, y, c = _me()
        k = 2 * x + y
        act = jax.nn.silu(a_ref[...]).astype(BF16)
        dms = dms_ref[...].astype(BF16)
        dw_ref[...] = lax.dot_general(act, dms, (((0,), (0,)), ((), ())), preferred_element_type=F32)
        db_ref[...] = jnp.sum(dm_ref[...], axis=0, keepdims=True)
        parts[k] = lax.dot_general(dms, w_ref[...].astype(BF16), (((1,), (1,)), ((), ())), preferred_element_type=F32)
        sends = [_rcopy(parts.at[k], parts.at[k], send_sems, recv_sems, q // 2 - 1, _flip(q)) for q in (2, 4, 6)]
        for cp in sends:
            cp.start()
        for q in (2, 4, 6):
            kq = _chip_of(_flip(q))
            _rcopy(parts.at[kq], parts.at[kq], send_sems, recv_sems, q // 2 - 1, _flip(q)).wait_recv()
        dact = ((parts[0] + parts[1]) + parts[2]) + parts[3]
        _, vjp = jax.vjp(jax.nn.silu, cc_ref[...])
        dcc_ref[...] = vjp(dact[8:9, :])[0]
        for cp in sends:
            cp.wait_send()

    return pl.pallas_call(
        kern, name="ada_bwd", in_specs=[_VMEM_SPEC] * 5, out_specs=[_VMEM_SPEC] * 3,
        out_shape=[jax.ShapeDtypeStruct((DM, SHARD_ADA), F32), jax.ShapeDtypeStruct((1, 3 * DM), F32),
                   jax.ShapeDtypeStruct((1, DM), F32)],
        scratch_shapes=[pltpu.VMEM((NCHIP, 16, DM), F32), pltpu.SemaphoreType.DMA((3,)), pltpu.SemaphoreType.DMA((3,))],
    )(a_in, dm, dm_shard, w_ada, c_ctx)


def _adamw_math(w, g, m, v):
    m = B1 * m + (1.0 - B1) * g
    v = B2 * v + (1.0 - B2) * (g * g)
    m_hat = m / (1.0 - B1 ** STEP)
    v_hat = v / (1.0 - B2 ** STEP)
    return -LR * (m_hat / (jnp.sqrt(v_hat) + ADAM_EPS) + WD * w), m, v


def adamw_big(w, g, m, v, name, block_rows=256):
    rows, width = w.shape

    def kern(w_ref, g_ref, m_ref, v_ref, d_ref, nm_ref, nv_ref):
        d_ref[...], nm_ref[...], nv_ref[...] = _adamw_math(w_ref[...], g_ref[...], m_ref[...], v_ref[...])

    spec = pl.BlockSpec((block_rows, width), lambda i: (i, 0))
    return pl.pallas_call(
        kern, name=name, grid=(rows // block_rows,), in_specs=[spec] * 4, out_specs=[spec] * 3,
        out_shape=[jax.ShapeDtypeStruct((rows, width), F32)] * 3,
        compiler_params=_cparams(("arbitrary",)),
    )(w, g, m, v)


def adamw_small(quads):
    n = len(quads)

    def kern(*refs):
        ins, outs = refs[:4 * n], refs[4 * n:]
        for i in range(n):
            w, g, m, v = (r[...] for r in ins[4 * i:4 * i + 4])
            outs[3 * i][...], outs[3 * i + 1][...], outs[3 * i + 2][...] = _adamw_math(w, g, m, v)

    flat = [a for quad in quads for a in quad]
    res = pl.pallas_call(
        kern, name="adamw_small", in_specs=[_VMEM_SPEC] * (4 * n), out_specs=[_VMEM_SPEC] * (3 * n),
        out_shape=[jax.ShapeDtypeStruct(q[0].shape, F32) for q in quads for _ in range(3)],
    )(*flat)
    return [tuple(res[3 * i:3 * i + 3]) for i in range(n)]


def _rows_of(a, rows):
    flat = a.reshape(-1)
    return jnp.pad(flat, (0, rows * DM - flat.shape[0])).reshape(rows, DM)


def kernel(x, c, ctx, c_ctx, w_ada, b_ada, norm_g, w_in, sgu_norm_g, w_spatial, b_spatial, q_norm_g, k_norm_g, rpb, w_out, loss_target, m_c_ctx, m_w_ada, m_b_ada, m_norm_g, m_w_in, m_sgu_norm_g, m_w_spatial, m_b_spatial, m_q_norm_g, m_k_norm_g, m_rpb, m_w_out, v_c_ctx, v_w_ada, v_b_ada, v_norm_g, v_w_in, v_sgu_norm_g, v_w_spatial, v_b_spatial, v_q_norm_g, v_k_norm_g, v_rpb, v_w_out):
    xi, yi, ci = lax.axis_index("x"), lax.axis_index("y"), lax.axis_index("c")
    chip, dev = 2 * xi + yi, 4 * xi + 2 * yi + ci
    c_ctx2 = c_ctx.reshape(1, DM)

    b_shard = lax.dynamic_slice(b_ada, (0, chip * SHARD_ADA), (1, SHARD_ADA))
    part = local_step(chip.reshape(1).astype(jnp.int32), dev, x[0], c, c_ctx2, w_ada[0], b_shard, ctx[0], loss_target[0],
                      norm_g, sgu_norm_g, w_spatial[0], b_spatial[0], q_norm_g, k_norm_g, rpb[0], w_in[0], w_out[0])
    cs = part["cs"]

    slab = jnp.concatenate([
        part["d_norm_g"], _rows_of(part["d_sgu_g"], 1), _rows_of(part["d_b_s"], 1),
        _rows_of(jnp.concatenate([part["d_q_g"], part["d_k_g"]], axis=-1), 1), _rows_of(part["d_rpb"], 4),
        _rows_of(part["loss"], 1), _rows_of(part["dcmod"], 3), _rows_of(part["dmod"], 3), jnp.zeros((1, DM), F32),
        _rows_of(part["d_w_s"], 64)], axis=0)
    g_w_in, g_w_out, gathered, tot = final_reduce(*part["rs"], slab)
    dm = jnp.concatenate([gathered[:, 12:15, :].reshape(NDEV, 3 * DM), tot[9:12].reshape(1, 3 * DM),
                          jnp.zeros((7, 3 * DM), F32)], axis=0)
    a_in = jnp.concatenate([cs[0:8 * NDEV:8], cs[8 * NDEV:8 * NDEV + 1], jnp.zeros((7, DM), F32)], axis=0)
    dm_shard = lax.dynamic_slice(dm, (0, chip * SHARD_ADA), (16, SHARD_ADA))
    g_w_ada, g_b_ada, g_c_ctx = ada_bwd(a_in, dm, dm_shard, w_ada[0], c_ctx2)

    loss = tot[8, 0]
    g_small = dict(
        c_ctx=g_c_ctx, b_ada=g_b_ada, norm_g=tot[0:1], sgu_norm_g=tot[1:2, :512], w_spatial=tot[16:80].reshape(512, 128),
        b_spatial=tot[2:3, :512].reshape(4, 128), q_norm_g=tot[3:4, :HDIM], k_norm_g=tot[3:4, HDIM:2 * HDIM],
        rpb=tot[4:8].reshape(-1)[:HEADS * 15 * 31].reshape(HEADS * 15, 31))
    shapes = dict(c_ctx=(DM,), w_ada=(1, DM, SHARD_ADA), b_ada=(1, 3 * DM), norm_g=(1, DM), w_in=(1, DM, SHARD_IN),
                  sgu_norm_g=(1, 512), w_spatial=(1, 4, 128, 128), b_spatial=(1, 4, 128), q_norm_g=(1, HDIM),
                  k_norm_g=(1, HDIM), rpb=(1, HEADS, 15, 31), w_out=(1, SHARD_OUT, DM))
    names = list(shapes)
    weights = dict(c_ctx=c_ctx, w_ada=w_ada, b_ada=b_ada, norm_g=norm_g, w_in=w_in, sgu_norm_g=sgu_norm_g,
                   w_spatial=w_spatial, b_spatial=b_spatial, q_norm_g=q_norm_g, k_norm_g=k_norm_g, rpb=rpb, w_out=w_out)
    m_in = dict(zip(names, (m_c_ctx, m_w_ada, m_b_ada, m_norm_g, m_w_in, m_sgu_norm_g, m_w_spatial, m_b_spatial,
                            m_q_norm_g, m_k_norm_g, m_rpb, m_w_out)))
    v_in = dict(zip(names, (v_c_ctx, v_w_ada, v_b_ada, v_norm_g, v_w_in, v_sgu_norm_g, v_w_spatial, v_b_spatial,
                            v_q_norm_g, v_k_norm_g, v_rpb, v_w_out)))
    grads = dict(g_small, w_ada=g_w_ada, w_in=g_w_in, w_out=g_w_out)
    upd = {}
    for n in ("w_ada", "w_in", "w_out"):
        g = grads[n]
        upd[n] = adamw_big(weights[n].reshape(g.shape), g, m_in[n].reshape(g.shape), v_in[n].reshape(g.shape),
                           "adamw_" + n)
    small = [n for n in names if n not in upd]
    res = adamw_small([(weights[n].reshape(grads[n].shape), grads[n], m_in[n].reshape(grads[n].shape),
                        v_in[n].reshape(grads[n].shape)) for n in small])
    upd.update(zip(small, res))
    out = [loss, part["grad_x"].reshape(1, SEQ, DM)]
    out += [grads[n].reshape(shapes[n]) for n in names]
    for slot in range(3):
        out += [upd[n][slot].reshape(shapes[n]) for n in names]
    return tuple(out)
```

```python
import functools

import jax
import jax.numpy as jnp
from jax import lax
from jax.experimental import pallas as pl
from jax.experimental.pallas import tpu as pltpu

F32, BF16 = jnp.float32, jnp.bfloat16
SEQ, DM, CTX, DIN = 4096, 1024, 256, 3584
NCHIP, NDEV = 4, 8
SHARD_IN = DIN // NCHIP
SHARD_ADA = 3 * DM // NCHIP
SHARD_OUT = DM // NCHIP
GRID_W = 64
QROWS = 4
KROWS = 12
QBLK, KBLK = QROWS * GRID_W, KROWS * GRID_W
NQBLK = SEQ // QBLK
HEADS, HDIM, NPAIR = 8, 64, 4
EPS = 1e-6
NEG_INF = -1e30
ZQ, ZK, ZV, ZG = 12, 16, 20, 24
LR, B1, B2, ADAM_EPS, WD, STEP = 0.001, 0.9, 0.999, 1e-08, 0.01, 10
VMEM_BIG = 56 * 1024 * 1024
MESH_ID = pl.DeviceIdType.MESH


def _dot(a, b, lhs_c, rhs_c):
    return lax.dot_general(a.astype(BF16), b.astype(BF16), (((lhs_c,), (rhs_c,)), ((), ())),
                           preferred_element_type=F32)


@jax.custom_vjp
def mm(a, b):
    return _dot(a, b, 1, 0)


@jax.custom_vjp
def mm_nt(a, b):
    return _dot(a, b, 1, 1)


@jax.custom_vjp
def mm_tn(a, b):
    return _dot(a, b, 0, 0)


mm.defvjp(lambda a, b: (mm(a, b), (a, b)), lambda r, ct: (mm_nt(ct, r[1]), mm_tn(r[0], ct)))
mm_nt.defvjp(lambda a, b: (mm_nt(a, b), (a, b)), lambda r, ct: (mm(ct, r[1]), mm_tn(ct, r[0])))
mm_tn.defvjp(lambda a, b: (mm_tn(a, b), (a, b)), lambda r, ct: (mm_nt(r[1], ct), mm(r[0], ct)))


def _rms(x, g):
    return x * lax.rsqrt(jnp.mean(x * x, axis=-1, keepdims=True) + EPS) * g


def _modulated(x, g, scale, shift):
    return _rms(x, g) * (1.0 + scale) + shift


def _pair_rms(x, g2):
    lo = lax.broadcasted_iota(jnp.int32, (1, 2 * HDIM), 1) < HDIM
    sq = x * x
    s_lo = jnp.sum(jnp.where(lo, sq, 0.0), axis=-1, keepdims=True)
    s_hi = jnp.sum(jnp.where(lo, 0.0, sq), axis=-1, keepdims=True)
    rs = jnp.where(lo, lax.rsqrt(s_lo / HDIM + EPS), lax.rsqrt(s_hi / HDIM + EPS))
    return x * rs * g2


def _cparams(sem, vmem=None):
    return pltpu.CompilerParams(dimension_semantics=sem, vmem_limit_bytes=vmem)


def _row(n):
    return pl.BlockSpec((1, n), lambda *_: (0, 0))


CS_ROWS = 8 * NDEV + 8


def _mod_part(mod_ref, row, part):
    pieces = []
    for j in range(NCHIP):
        lo, hi = max(part * DM, j * SHARD_ADA), min((part + 1) * DM, (j + 1) * SHARD_ADA)
        if lo < hi:
            pieces.append(mod_ref[j, row, lo - j * SHARD_ADA:hi - j * SHARD_ADA])
    return jnp.concatenate(pieces, axis=-1)


def inproj_fwd(chip, x, c_vec, c_ctx, w_ada, b_shard, norm_g, w_shard, wo_shard):
    tl = 1024
    nt = SEQ // tl
    halves = (DM // 2, SHARD_OUT // 2)
    n_w, n_c = 12, NDEV - 1

    def kern(k_ref, x_ref, cv_ref, cc_ref, wa_ref, b_ref, g_ref, w_ref, wo_ref,
             z_ref, h_ref, wfull_ref, wofull_ref, modall_ref, csall_ref,
             w_scr, wo_scr, h_scr, mine, cs_scr, mod_scr, shsc_scr, send_sems, recv_sems):
        s, t = pl.program_id(0), pl.program_id(1)
        xi, yi, c = _me()
        k, me = 2 * xi + yi, 4 * xi + 2 * yi + c
        sib = _flip(1)
        rows = pl.ds(pl.multiple_of(t * tl, tl), tl)
        gathered = (w_scr, wo_scr)
        slot = lambda d: pl.ds(pl.multiple_of(8 * d, 8), 8)

        def c_copy(q, owner):
            return _rcopy(mine, cs_scr.at[slot(owner), :], send_sems, recv_sems, n_w + q - 1, _flip(q))

        def m_copy(q, chip_of_block):
            return _rcopy(mod_scr.at[chip_of_block], mod_scr.at[chip_of_block], send_sems, recv_sems,
                          n_w + n_c + q // 2 - 1, _flip(q))

        def adaln():
            first = lax.broadcasted_iota(jnp.int32, (8, DM), 0) == 0
            mine[...] = jnp.where(first, jnp.broadcast_to(cv_ref[...], (8, DM)), 0.0)
            cs_scr[slot(me), :] = mine[...]
            cs_scr[slot(NDEV), :] = jnp.where(first, jnp.broadcast_to(cc_ref[...], (8, DM)), 0.0)
            for q in range(1, NDEV):
                c_copy(q, me).start()
            wa = wa_ref[...].astype(BF16)
            for q in range(1, NDEV):
                px, py, pc = _flip(q)
                c_copy(q, 4 * px + 2 * py + pc).wait_recv()
            act = jax.nn.silu(cs_scr[...]).astype(BF16)
            mod_scr[k] = jnp.dot(act, wa, preferred_element_type=F32) + b_ref[...]
            for q in (2, 4, 6):
                m_copy(q, k).start()
            for q in (2, 4, 6):
                m_copy(q, _chip_of(_flip(q))).wait_recv()
            row = pl.ds(8 * me, 1)
            shsc_scr[0:1, :] = _mod_part(mod_scr, row, 0)
            shsc_scr[1:2, :] = _mod_part(mod_scr, row, 1)
            pltpu.sync_copy(mod_scr, modall_ref)
            pltpu.sync_copy(cs_scr, csall_ref)

        def block(n, chip_of_block, hh):
            return gathered[n].at[chip_of_block, pl.ds(pl.multiple_of(hh * halves[n], halves[n]), halves[n]), :]

        def ici(n, q, chip_of_block):
            blk = block(n, chip_of_block, c)
            return _rcopy(blk, blk, send_sems, recv_sems, 6 * n + q // 2 - 1, _flip(q))

        def d2d(n, q, chip_of_block, hh):
            blk = block(n, chip_of_block, hh)
            return _rcopy(blk, blk, send_sems, recv_sems, 6 * n + 3 + q // 2 - 1, sib)

        @pl.when((s == 0) & (t == 0))
        def _():
            adaln()
            w_scr[k] = w_ref[...].astype(BF16)
            wo_scr[k] = wo_ref[...].astype(BF16)
            for q in (2, 4, 6):
                ici(0, q, k).start()
                ici(1, q, k).start()

        for sweep in (1, 2, 3):
            @pl.when((s == sweep) & (t == 0))
            def _():
                q = 2 * sweep
                src = _chip_of(_flip(q))
                for n in (0, 1):
                    ici(n, q, src).wait_recv()
                    d2d(n, q, src, c).start()
                for n in (0, 1):
                    d2d(n, q, src, 1 - c).wait_recv()

        @pl.when(s == 0)
        def _():
            hb = _modulated(x_ref[...], g_ref[...], shsc_scr[1:2, :], shsc_scr[0:1, :]).astype(BF16)
            h_scr[rows, :] = hb
            h_ref[...] = hb

        z_ref[...] = jnp.dot(h_scr[rows, :], w_scr[lax.bitwise_xor(k, s)], preferred_element_type=F32)

        @pl.when((s == NCHIP - 1) & (t == nt - 1))
        def _():
            for q in range(1, NDEV):
                c_copy(q, me).wait_send()
            for q in (2, 4, 6):
                m_copy(q, k).wait_send()
            for n in (0, 1):
                for q in (2, 4, 6):
                    ici(n, q, k).wait_send()
                    d2d(n, q, _chip_of(_flip(q)), c).wait_send()
            pltpu.sync_copy(w_scr, wfull_ref)
            pltpu.sync_copy(wo_scr, wofull_ref)

    once = lambda s, t, k: (jnp.where(s == 0, t, nt - 1), 0)
    hbm = pl.BlockSpec(memory_space=pl.ANY)
    n_sem = n_w + n_c + 3
    return pl.pallas_call(
        kern, name="inproj_fwd",
        grid_spec=pltpu.PrefetchScalarGridSpec(
            num_scalar_prefetch=1, grid=(NCHIP, nt),
            in_specs=[pl.BlockSpec((tl, DM), once)] + [_VMEM_SPEC] * 7,
            out_specs=[pl.BlockSpec((tl, SHARD_IN), lambda s, t, k: (t, lax.bitwise_xor(k[0], s))),
                       pl.BlockSpec((tl, DM), once), hbm, hbm, hbm, hbm],
            scratch_shapes=[pltpu.VMEM((NCHIP, DM, SHARD_IN), BF16), pltpu.VMEM((NCHIP, SHARD_OUT, DM), BF16),
                            pltpu.VMEM((SEQ, DM), BF16), pltpu.VMEM((8, DM), F32), pltpu.VMEM((CS_ROWS, DM), F32),
                            pltpu.VMEM((NCHIP, CS_ROWS, SHARD_ADA), F32), pltpu.VMEM((8, DM), F32),
                            pltpu.SemaphoreType.DMA((n_sem,)), pltpu.SemaphoreType.DMA((n_sem,))]),
        out_shape=[jax.ShapeDtypeStruct((SEQ, DIN), F32), jax.ShapeDtypeStruct((SEQ, DM), BF16),
                   jax.ShapeDtypeStruct((NCHIP, DM, SHARD_IN), BF16), jax.ShapeDtypeStruct((NCHIP, SHARD_OUT, DM), BF16),
                   jax.ShapeDtypeStruct((NCHIP, CS_ROWS, SHARD_ADA), F32), jax.ShapeDtypeStruct((CS_ROWS, DM), F32)],
        compiler_params=_cparams(("arbitrary", "arbitrary"), VMEM_BIG),
    )(chip, x, c_vec, c_ctx, w_ada, b_shard, norm_g, w_shard, wo_shard)


def ctx_fwd(ctx, cshift, cscale, norm_g, w_full):
    def kern(c_ref, sh_ref, sc_ref, g_ref, w2_ref, w3_ref, zc_ref, hc_ref):
        hc = _modulated(c_ref[...], g_ref[...], sc_ref[...], sh_ref[...]).astype(BF16)
        hc_ref[...] = hc
        zc_ref[:, :SHARD_IN] = jnp.dot(hc, w2_ref[0], preferred_element_type=F32)
        zc_ref[:, SHARD_IN:] = jnp.dot(hc, w3_ref[0], preferred_element_type=F32)

    return pl.pallas_call(
        kern, name="ctx_fwd", grid=(1,),
        in_specs=[pl.BlockSpec((CTX, DM), lambda i: (0, 0)), _row(DM), _row(DM), _row(DM),
                  pl.BlockSpec((1, DM, SHARD_IN), lambda i: (2, 0, 0)),
                  pl.BlockSpec((1, DM, SHARD_IN), lambda i: (3, 0, 0))],
        out_specs=[pl.BlockSpec((CTX, 2 * SHARD_IN), lambda i: (0, 0)),
                   pl.BlockSpec((CTX, DM), lambda i: (0, 0))],
        out_shape=[jax.ShapeDtypeStruct((CTX, 2 * SHARD_IN), F32), jax.ShapeDtypeStruct((CTX, DM), BF16)],
        compiler_params=_cparams(("arbitrary",)),
    )(ctx, cshift, cscale, norm_g, w_full, w_full)


SGU_CHUNK, SGU_PER_STEP = 128, 4


def _gelu(x):
    return 0.5 * x * (1.0 + lax.erf(x * 0.7071067811865476))


def _sgu_chunk(au, av, ag, sg, ws, bsb):
    u, v = _gelu(au), _gelu(av)
    outs = []
    for g in range(4):
        sl = slice(128 * g, 128 * (g + 1))
        mixed = mm(ws[g], _rms(v[:, sl], sg[:, sl])) + bsb[g]
        outs.append(u[:, sl] * mixed * jax.nn.silu(ag[:, sl]))
    return jnp.concatenate(outs, axis=-1)


def _sgu_specs():
    rows = SGU_CHUNK * SGU_PER_STEP
    zspec = lambda c: pl.BlockSpec((rows, 512), lambda n: (n, c))
    wspec = pl.BlockSpec((4, 128, 128), lambda n: (0, 0, 0))
    return rows, [zspec(0), zspec(1), zspec(2), _row(512), wspec, wspec]


def sgu_fwd(z, sg, ws, bsb):
    rows, in_specs = _sgu_specs()

    def kern(au_ref, av_ref, ag_ref, sg_ref, ws_ref, bs_ref, o_ref):
        for c in range(SGU_PER_STEP):
            sl = slice(c * SGU_CHUNK, (c + 1) * SGU_CHUNK)
            o_ref[sl, :] = _sgu_chunk(au_ref[sl, :], av_ref[sl, :], ag_ref[sl, :], sg_ref[...], ws_ref[...],
                                      bs_ref[...])

    return pl.pallas_call(
        kern, name="sgu_fwd", grid=(SEQ // rows,), in_specs=in_specs,
        out_specs=pl.BlockSpec((rows, 512), lambda n: (n, 0)),
        out_shape=jax.ShapeDtypeStruct((SEQ, 512), F32),
        compiler_params=_cparams(("arbitrary",)),
    )(z, z, z, sg, ws, bsb)


def sgu_bwd(z, sg, ws, bsb, dcat):
    rows, in_specs = _sgu_specs()

    def kern(au_ref, av_ref, ag_ref, sg_ref, ws_ref, bs_ref, do_ref, dz_ref, dsg_ref, dws_ref, dbs_ref):
        @pl.when(pl.program_id(0) == 0)
        def _():
            dsg_ref[...] = jnp.zeros_like(dsg_ref)
            dws_ref[...] = jnp.zeros_like(dws_ref)
            dbs_ref[...] = jnp.zeros_like(dbs_ref)

        for c in range(SGU_PER_STEP):
            sl = slice(c * SGU_CHUNK, (c + 1) * SGU_CHUNK)
            _, vjp = jax.vjp(_sgu_chunk, au_ref[sl, :], av_ref[sl, :], ag_ref[sl, :], sg_ref[...], ws_ref[...],
                             bs_ref[...])
            dau, dav, dag, dsg, dws, dbs = vjp(do_ref[sl, :])
            dz_ref[sl, 0:512] = dau.astype(BF16)
            dz_ref[sl, 512:1024] = dav.astype(BF16)
            dz_ref[sl, 1024:1536] = dag.astype(BF16)
            dsg_ref[...] += dsg
            dws_ref[...] += dws
            dbs_ref[...] += dbs

        @pl.when(pl.program_id(0) == pl.num_programs(0) - 1)
        def _():
            dbs_ref[...] = jnp.broadcast_to(jnp.sum(dbs_ref[...], axis=-1, keepdims=True), dbs_ref.shape)

    wspec = pl.BlockSpec((4, 128, 128), lambda n: (0, 0, 0))
    return pl.pallas_call(
        kern, name="sgu_bwd", grid=(SEQ // rows,),
        in_specs=in_specs + [pl.BlockSpec((rows, 512), lambda n: (n, 0))],
        out_specs=[pl.BlockSpec((rows, 1536), lambda n: (n, 0)), _row(512), wspec, wspec],
        out_shape=[jax.ShapeDtypeStruct((SEQ, 1536), BF16), jax.ShapeDtypeStruct((1, 512), F32),
                   jax.ShapeDtypeStruct((4, 128, 128), F32), jax.ShapeDtypeStruct((4, 128, 128), F32)],
        compiler_params=_cparams(("arbitrary",)),
    )(z, z, z, sg, ws, bsb, dcat)


_DR_OFF = (7, 3, -1)


def _row_valid(v, rr, j):
    return (j < 8, rr <= j < rr + 8, 4 <= j < 12)[v]


def _col_window():
    q = lax.broadcasted_iota(jnp.int32, (GRID_W, 128), 0)
    kc = lax.broadcasted_iota(jnp.int32, (GRID_W, 128), 1) % GRID_W
    c0 = jnp.clip(q - 8, 0, GRID_W - 16)
    return (kc >= c0) & (kc < c0 + 16)


def rpb_tables(rpb2):
    def kern(r_ref, b_ref):
        base = r_ref[0]
        lo = lax.broadcasted_iota(jnp.int32, (1, 128), 1) < GRID_W
        win = _col_window()
        tiles = {}
        for v in range(3):
            for rr in range(QROWS):
                for jp in range(KROWS // 2):
                    j0, j1 = 2 * jp, 2 * jp + 1
                    ok0, ok1 = _row_valid(v, rr, j0), _row_valid(v, rr, j1)
                    key = (j0 - rr + _DR_OFF[v], ok0, ok1) if (ok0 or ok1) else None
                    if key not in tiles:
                        if key is None:
                            tiles[key] = jnp.full((GRID_W, 128), NEG_INF, F32)
                        else:
                            d0 = key[0]
                            r0 = base[d0:d0 + 1, :] if ok0 else jnp.zeros((1, 128), F32)
                            r1 = base[d0 + 1:d0 + 2, :] if ok1 else jnp.zeros((1, 128), F32)
                            y = jnp.broadcast_to(jnp.where(lo, r0, r1), (GRID_W, 128))
                            y = pltpu.roll(pltpu.roll(y, 128 - 15, 1), 0, 1, stride=1, stride_axis=0)
                            tiles[key] = jnp.where(win & jnp.where(lo, ok0, ok1), y, NEG_INF)
                    b_ref[v, 0, rr * GRID_W:(rr + 1) * GRID_W, jp * 128:(jp + 1) * 128] = tiles[key]

    return pl.pallas_call(
        kern, name="rpb_tables", grid=(HEADS,),
        in_specs=[pl.BlockSpec((1, 15, 128), lambda h: (h, 0, 0))],
        out_specs=pl.BlockSpec((3, 1, QBLK, KBLK), lambda h: (0, h, 0, 0)),
        out_shape=jax.ShapeDtypeStruct((3, HEADS, QBLK, KBLK), F32),
        compiler_params=_cparams(("arbitrary",)),
    )(rpb2)


def rpb_bwd(dbias):
    def kern(g0_ref, g1_ref, g2_ref, o_ref):
        g_refs = (g0_ref, g1_ref, g2_ref)
        lo = lax.broadcasted_iota(jnp.int32, (1, 128), 1) < GRID_W
        ri = lax.broadcasted_iota(jnp.int32, (GRID_W, GRID_W), 0)
        ci = lax.broadcasted_iota(jnp.int32, (GRID_W, GRID_W), 1)
        flip = (ri + ci == GRID_W - 1).astype(F32)
        groups = {}
        for v in range(3):
            for rr in range(QROWS):
                for jp in range(KROWS // 2):
                    j0, j1 = 2 * jp, 2 * jp + 1
                    ok0, ok1 = _row_valid(v, rr, j0), _row_valid(v, rr, j1)
                    if not (ok0 or ok1):
                        continue
                    g = g_refs[v][0, rr * GRID_W:(rr + 1) * GRID_W, jp * 128:(jp + 1) * 128]
                    key = (j0 - rr + _DR_OFF[v], ok0, ok1)
                    groups[key] = g if key not in groups else groups[key] + g
        acc = [jnp.zeros((1, 128), F32) for _ in range(15)]
        for (d0, ok0, ok1), g in groups.items():
            g = lax.dot_general(flip, g, (((1,), (0,)), ((), ())), precision=lax.Precision.HIGHEST,
                                preferred_element_type=F32)
            g = pltpu.roll(pltpu.roll(g, 128 - 48, 1), 0, 1, stride=1, stride_axis=0)
            s = jnp.sum(g, axis=0, keepdims=True)
            if ok0:
                acc[d0] = acc[d0] + jnp.where(lo, s, 0.0)
            if ok1:
                acc[d0 + 1] = acc[d0 + 1] + jnp.where(lo, 0.0, s)
        for d in range(15):
            o_ref[0, d:d + 1, :] = acc[d] + pltpu.roll(acc[d], GRID_W, 1)

    return pl.pallas_call(
        kern, name="rpb_bwd", grid=(HEADS,),
        in_specs=[pl.BlockSpec((1, QBLK, KBLK), lambda h: (h, 0, 0))] * 3,
        out_specs=pl.BlockSpec((1, 15, 128), lambda h: (h, 0, 0)),
        out_shape=jax.ShapeDtypeStruct((HEADS, 15, 128), F32),
        compiler_params=_cparams(("arbitrary",)),
    )(*dbias)


def _scaled_q(q_raw, qg):
    return _pair_rms(q_raw, qg) * (HDIM ** -0.5)


def _head_lanes():
    lo = lax.broadcasted_iota(jnp.int32, (1, 2 * HDIM), 1) < HDIM
    return lo, jnp.logical_not(lo)


SOFTMAX_ROWS = 32


def _emit_interleaved(vector_work, matmul_work):
    for j in range(max(len(vector_work), len(matmul_work))):
        for work in (vector_work, matmul_work):
            if j < len(work):
                work[j]()


def _attn_step_bwd(q_raw, kn, v, ckn, cv, qg, bg, o, rden, probs, dout):
    sig = jax.nn.sigmoid(bg)
    do = dout * (bg * sig)
    dbg = dout * o * (sig * (1.0 + bg * (1.0 - sig)))
    qn, qn_vjp = jax.vjp(_scaled_q, q_raw, qg)
    row_dot = do * o
    dqn = dkn = dv = dckn = dcv = None
    dbias = []
    for mine, (p_lat, p_ctx) in zip(_head_lanes(), probs):
        qa = jnp.where(mine, qn, 0.0)
        r = jnp.max(jnp.where(mine, rden, 0.0), axis=-1, keepdims=True)
        doa = jnp.where(mine, do, 0.0) * r
        delta = jnp.sum(jnp.where(mine, row_dot, 0.0), axis=-1, keepdims=True) * r
        ds_lat = p_lat.astype(F32) * (mm_nt(doa, v) - delta)
        ds_ctx = p_ctx.astype(F32) * (mm_nt(doa, cv) - delta)
        parts = (jnp.where(mine, mm(ds_lat, kn) + mm(ds_ctx, ckn), 0.0), mm_tn(qa, ds_lat), mm_tn(doa, p_lat),
                 mm_tn(qa, ds_ctx), mm_tn(doa, p_ctx))
        if dqn is None:
            dqn, dkn, dv, dckn, dcv = parts
        else:
            dqn, dkn, dv, dckn, dcv = (acc + new for acc, new in zip((dqn, dkn, dv, dckn, dcv), parts))
        dbias.append(ds_lat)
    dq, dqg = qn_vjp(dqn)
    return dq, dkn, dv, dckn, dcv, dbias, dqg, dbg


def _kblock(i):
    return jnp.clip(i - 1, 0, (SEQ - KBLK) // QBLK)


def _kstart(i):
    return pl.multiple_of(_kblock(i) * QBLK, QBLK)


ATTN_STEPS = NQBLK // 2
ATTN_ROWS = 2 * QBLK
KCOLS = QBLK


def _attn_in_specs():
    return [
        pl.BlockSpec((ATTN_ROWS, 128), lambda p, i: (i, ZQ + p)),
        pl.BlockSpec((SEQ, 128), lambda p, i: (0, ZK + p)),
        pl.BlockSpec((SEQ, 128), lambda p, i: (0, ZV + p)),
        pl.BlockSpec((ATTN_ROWS, 128), lambda p, i: (i, ZG + p)),
        pl.BlockSpec((CTX, 128), lambda p, i: (0, 2 + p)),
        pl.BlockSpec((CTX, 128), lambda p, i: (0, 6 + p)),
    ]


def _bias_specs():
    bias_spec = lambda variant: pl.BlockSpec((1, 2, QBLK, KBLK), lambda p, i: (variant(i), p, 0, 0))
    return [bias_spec(lambda i: jnp.where(i == 0, 0, 1)),
            bias_spec(lambda i: jnp.where(i == ATTN_STEPS - 1, 2, 1))]


def _prob_specs():
    return [pl.BlockSpec((2, ATTN_ROWS, KBLK), lambda p, i: (p, i, 0)),
            pl.BlockSpec((2, ATTN_ROWS, CTX), lambda p, i: (p, i, 0))]


NORM_ROWS = 512


def _norm_keys(k_ref, ck_ref, kg_ref, kn_scr, ckn_scr):
    def body(c, carry):
        sl = pl.ds(pl.multiple_of(c * NORM_ROWS, NORM_ROWS), NORM_ROWS)
        kn_scr[sl, :] = _pair_rms(k_ref[sl, :], kg_ref[...]).astype(BF16)
        return carry

    lax.fori_loop(0, SEQ // NORM_ROWS, body, 0)
    ckn_scr[...] = _pair_rms(ck_ref[...], kg_ref[...]).astype(BF16)


def _values_with_ones(v_ref, cv_ref, v1_scr, cv1_scr):
    for a, mine in enumerate(_head_lanes()):
        def body(c, carry):
            sl = pl.ds(pl.multiple_of(c * NORM_ROWS, NORM_ROWS), NORM_ROWS)
            v1_scr[a, sl, :] = jnp.where(mine, v_ref[sl, :], 1.0).astype(BF16)
            return carry

        lax.fori_loop(0, SEQ // NORM_ROWS, body, 0)
        cv1_scr[a] = jnp.where(mine, cv_ref[...], 1.0).astype(BF16)


def attn_fwd(z, zc, bias, qg2, kg2):
    def kern(q_ref, k_ref, v_ref, bg_ref, ck_ref, cv_ref, be_ref, bo_ref, qg_ref, kg_ref,
             ob_ref, o_ref, rden_ref, pl_ref, pc_ref, kn_scr, ckn_scr, v1_scr, cv1_scr, s_scr):
        i = pl.program_id(1)

        @pl.when(i == 0)
        def _():
            _norm_keys(k_ref, ck_ref, kg_ref, kn_scr, ckn_scr)
            _values_with_ones(v_ref, cv_ref, v1_scr, cv1_scr)

        heads = _head_lanes()
        bias_refs = (be_ref, bo_ref)
        tiles = [(b, a) for b in range(2) for a in range(2)]
        rows = [slice(b * QBLK, (b + 1) * QBLK) for b in range(2)]
        qn = [_scaled_q(q_ref[rows[b], :], qg_ref[...]) for b in range(2)]
        qa = [jnp.where(heads[a], qn[b], 0.0).astype(BF16) for b, a in tiles]
        pv = [None] * len(tiles)
        done = {}
        latent = KBLK // KCOLS

        def keys(b, n):
            return pl.ds(pl.multiple_of(_kstart(2 * i + b) + n * KCOLS, KCOLS), KCOLS)

        def score_piece(t, n):
            b, a = tiles[t]
            cols = slice(n * KCOLS, (n + 1) * KCOLS)
            if n < latent:
                s_scr[t, :, cols] = mm_nt(qa[t], kn_scr[keys(b, n), :]) + bias_refs[b][0, a, :, cols]
            else:
                s_scr[t, :, cols] = mm_nt(qa[t], ckn_scr[...])

        def softmax_rows(t, r):
            b, a = tiles[t]
            rs = slice(r * SOFTMAX_ROWS, (r + 1) * SOFTMAX_ROWS)
            out_rows = slice(b * QBLK + rs.start, b * QBLK + rs.stop)
            s = s_scr[t, rs, :]
            p = jnp.exp(s - jnp.max(s, axis=-1, keepdims=True)).astype(BF16)
            pl_ref[a, out_rows, :] = p[:, :KBLK]
            pc_ref[a, out_rows, :] = p[:, KBLK:]

        def value_piece(t, n):
            b, a = tiles[t]
            if n < latent:
                part = mm(pl_ref[a, rows[b], n * KCOLS:(n + 1) * KCOLS], v1_scr[a, keys(b, n), :])
            else:
                part = mm(pc_ref[a, rows[b], :], cv1_scr[a])
            pv[t] = part if pv[t] is None else pv[t] + part
            if n == latent:
                finish(t)

        def finish(t):
            b, a = tiles[t]
            r = jnp.where(heads[a], pltpu.roll(1.0 / pv[t], HDIM, 1), 0.0)
            done[t] = (pv[t] * r, r)
            if a == 1:
                o, rden = (lo + hi for lo, hi in zip(done[t - 1], done[t]))
                ob_ref[rows[b], :] = o * jax.nn.silu(bg_ref[rows[b], :])
                o_ref[rows[b], :] = o
                rden_ref[rows[b], :] = rden

        pieces = range(latent + 1)
        for n in pieces:
            score_piece(0, n)
        for t in range(len(tiles)):
            matmuls = []
            for n in pieces:
                if t + 1 < len(tiles):
                    matmuls.append(functools.partial(score_piece, t + 1, n))
                if t > 0:
                    matmuls.append(functools.partial(value_piece, t - 1, n))
            _emit_interleaved([functools.partial(softmax_rows, t, r) for r in range(QBLK // SOFTMAX_ROWS)], matmuls)
        for n in pieces:
            value_piece(len(tiles) - 1, n)

    qblk = pl.BlockSpec((ATTN_ROWS, 128), lambda p, i: (i, p))
    return pl.pallas_call(
        kern, name="attn_fwd", grid=(NPAIR, ATTN_STEPS),
        in_specs=_attn_in_specs() + _bias_specs() + [_row(128), _row(128)], out_specs=[qblk] * 3 + _prob_specs(),
        out_shape=[jax.ShapeDtypeStruct((SEQ, 512), F32)] * 3
        + [jax.ShapeDtypeStruct((HEADS, SEQ, KBLK), BF16), jax.ShapeDtypeStruct((HEADS, SEQ, CTX), BF16)],
        scratch_shapes=[pltpu.VMEM((SEQ, 128), BF16), pltpu.VMEM((CTX, 128), BF16),
                        pltpu.VMEM((2, SEQ, 128), BF16), pltpu.VMEM((2, CTX, 128), BF16),
                        pltpu.VMEM((4, QBLK, KBLK + CTX), F32)],
        compiler_params=_cparams(("arbitrary", "arbitrary"), 40 * 1024 * 1024),
    )(z, z, z, z, zc, zc, bias, bias, qg2, kg2)


def attn_bwd(z, zc, qg2, kg2, dcat, saved):
    def kern(q_ref, k_ref, v_ref, bg_ref, ck_ref, cv_ref, qg_ref, kg_ref, do_ref, o_ref, rden_ref, pl_ref, pc_ref,
             dq_ref, dk_ref, dv_ref, dbg_ref, dck_ref, dcv_ref, db0_ref, db1_ref, db2_ref, dqg_ref, dkg_ref,
             kn_scr, ckn_scr, dknt_scr, dvt_scr, dcknt_scr, dcvt_scr):
        p, i = pl.program_id(0), pl.program_id(1)
        last = i == ATTN_STEPS - 1

        @pl.when(i == 0)
        def _():
            _norm_keys(k_ref, ck_ref, kg_ref, kn_scr, ckn_scr)
            dknt_scr[...] = jnp.zeros_like(dknt_scr)
            dvt_scr[...] = jnp.zeros_like(dvt_scr)
            dcknt_scr[...] = jnp.zeros_like(dcknt_scr)
            dcvt_scr[...] = jnp.zeros_like(dcvt_scr)

        @pl.when((i == 0) & (p == 0))
        def _():
            dqg_ref[...] = jnp.zeros_like(dqg_ref)
            dkg_ref[...] = jnp.zeros_like(dkg_ref)

        db = []
        for b in range(2):
            rows = slice(b * QBLK, (b + 1) * QBLK)
            kb = _kblock(2 * i + b)
            ks = pl.ds(_kstart(2 * i + b), KBLK)
            probs = [(pl_ref[a, rows, :], pc_ref[a, rows, :]) for a in range(2)]
            dq, dknt, dvt, dcknt, dcvt, dbb, dqg, dbg = _attn_step_bwd(
                q_ref[rows, :], kn_scr[ks, :], v_ref[ks, :], ckn_scr[...], cv_ref[...], qg_ref[...],
                bg_ref[rows, :], o_ref[rows, :], rden_ref[rows, :], probs, do_ref[rows, :])
            dq_ref[rows, :] = dq.astype(BF16)
            dbg_ref[rows, :] = dbg.astype(BF16)
            for n in range(KBLK // KCOLS):
                cols = slice(n * KCOLS, (n + 1) * KCOLS)
                dknt_scr[kb + n] += dknt[:, cols]
                dvt_scr[kb + n] += dvt[:, cols]
            dcknt_scr[...] += dcknt
            dcvt_scr[...] += dcvt
            dqg_ref[...] += dqg
            db.append(dbb)

        @pl.when(i == 0)
        def _():
            for a in range(2):
                db0_ref[a] = db[0][a]
                db1_ref[a] = db[1][a]

        @pl.when((i > 0) & jnp.logical_not(last))
        def _():
            for a in range(2):
                db1_ref[a] += db[0][a] + db[1][a]

        @pl.when(last)
        def _():
            for a in range(2):
                db1_ref[a] += db[0][a]
                db2_ref[a] = db[1][a]

        @pl.when(last)
        def _():
            def body(c, dkg):
                sl = pl.ds(pl.multiple_of(c * NORM_ROWS, NORM_ROWS), NORM_ROWS)
                blocks = range(NORM_ROWS // KCOLS)
                dkn = jnp.concatenate([dknt_scr[c * len(blocks) + n].T for n in blocks], axis=0)
                dv = jnp.concatenate([dvt_scr[c * len(blocks) + n].T for n in blocks], axis=0)
                _, nvjp = jax.vjp(_pair_rms, k_ref[sl, :], kg_ref[...])
                dk, dg = nvjp(dkn)
                dk_ref[sl, :] = dk.astype(BF16)
                dv_ref[sl, :] = dv.astype(BF16)
                return dkg + dg

            dkg = lax.fori_loop(0, SEQ // NORM_ROWS, body, jnp.zeros((1, 128), F32))
            _, nvjp = jax.vjp(_pair_rms, ck_ref[...], kg_ref[...])
            dck, dg = nvjp(dcknt_scr[...].T)
            dck_ref[...] = dck
            dcv_ref[...] = dcvt_scr[...].T
            dkg_ref[...] += dkg + dg

        @pl.when(last & (p == NPAIR - 1))
        def _():
            dqg_ref[...] = dqg_ref[...] + pltpu.roll(dqg_ref[...], HDIM, 1)
            dkg_ref[...] = dkg_ref[...] + pltpu.roll(dkg_ref[...], HDIM, 1)

    blk = lambda rows: pl.BlockSpec((rows, 128), lambda p, i: (0, p))
    qblk = pl.BlockSpec((ATTN_ROWS, 128), lambda p, i: (i, p))
    dbias = pl.BlockSpec((2, QBLK, KBLK), lambda p, i: (p, 0, 0))
    return pl.pallas_call(
        kern, name="attn_bwd", grid=(NPAIR, ATTN_STEPS),
        in_specs=_attn_in_specs() + [_row(128), _row(128), pl.BlockSpec((ATTN_ROWS, 128), lambda p, i: (i, 4 + p)),
                                     qblk, qblk] + _prob_specs(),
        out_specs=[qblk, blk(SEQ), blk(SEQ), qblk, blk(CTX), blk(CTX), dbias, dbias, dbias, _row(128), _row(128)],
        out_shape=[jax.ShapeDtypeStruct((SEQ, 512), BF16)] * 4 + [jax.ShapeDtypeStruct((CTX, 512), F32)] * 2
        + [jax.ShapeDtypeStruct((HEADS, QBLK, KBLK), F32)] * 3
        + [jax.ShapeDtypeStruct((1, 128), F32), jax.ShapeDtypeStruct((1, 128), F32)],
        scratch_shapes=[pltpu.VMEM((SEQ, 128), BF16), pltpu.VMEM((CTX, 128), BF16),
                        pltpu.VMEM((SEQ // KCOLS, 128, KCOLS), F32), pltpu.VMEM((SEQ // KCOLS, 128, KCOLS), F32),
                        pltpu.VMEM((128, CTX), F32), pltpu.VMEM((128, CTX), F32)],
        compiler_params=_cparams(("arbitrary", "arbitrary"), VMEM_BIG),
    )(z, z, z, z, zc, zc, qg2, kg2, dcat, *saved)


def outproj(out_a, out_b, x, target, gate, wo):
    tl = 512

    def kern(a_ref, b_ref, x_ref, t_ref, g_ref, w_ref, loss_ref, dy_ref, dcat_ref, dg_ref, dw_ref):
        @pl.when(pl.program_id(0) == 0)
        def _():
            loss_ref[...] = jnp.zeros_like(loss_ref)
            dg_ref[...] = jnp.zeros_like(dg_ref)
            dw_ref[...] = jnp.zeros_like(dw_ref)

        a, b = a_ref[...].astype(BF16), b_ref[...].astype(BF16)
        mix = (jnp.dot(a, w_ref[0:512, :], preferred_element_type=F32)
               + jnp.dot(b, w_ref[512:1024, :], preferred_element_type=F32))
        err = x_ref[...] + g_ref[...] * mix - t_ref[...]
        loss_ref[...] += 0.5 * jnp.sum(jnp.mean(err * err, axis=-1))
        dy = err * (1.0 / DM)
        dy_ref[...] = dy
        dg_ref[...] += jnp.sum(dy * mix, axis=0, keepdims=True)
        dmix = (g_ref[...] * dy).astype(BF16)
        dcat_ref[...] = lax.dot_general(dmix, w_ref[...], (((1,), (1,)), ((), ())), preferred_element_type=F32)
        dw_ref[0:512, :] += lax.dot_general(a, dmix, (((0,), (0,)), ((), ())), preferred_element_type=F32)
        dw_ref[512:1024, :] += lax.dot_general(b, dmix, (((0,), (0,)), ((), ())), preferred_element_type=F32)

    tile = lambda w: pl.BlockSpec((tl, w), lambda t: (t, 0))
    whole = pl.BlockSpec((DM, DM), lambda t: (0, 0))
    return pl.pallas_call(
        kern, name="outproj", grid=(SEQ // tl,),
        in_specs=[tile(512), tile(512), tile(DM), tile(DM), _row(DM), whole],
        out_specs=[pl.BlockSpec((8, 128), lambda t: (0, 0)), tile(DM), tile(DM), _row(DM), whole],
        out_shape=[jax.ShapeDtypeStruct((8, 128), F32), jax.ShapeDtypeStruct((SEQ, DM), F32),
                   jax.ShapeDtypeStruct((SEQ, DM), F32), jax.ShapeDtypeStruct((1, DM), F32),
                   jax.ShapeDtypeStruct((DM, DM), F32)],
        compiler_params=_cparams(("arbitrary",), 48 * 1024 * 1024),
    )(out_a, out_b, x, target, gate, wo)


def _pieces(sources):
    out = []
    for name, c0, c1 in sources:
        for j in range(NCHIP):
            lo, hi = max(c0, j * SHARD_IN), min(c1, (j + 1) * SHARD_IN)
            if lo < hi:
                out.append((j, lo - j * SHARD_IN, hi - j * SHARD_IN, name, lo - c0, hi - c0))
    return out


DZ_PIECES = _pieces((("a", 0, 1536), ("q", 1536, 2048), ("k", 2048, 2560), ("v", 2560, 3072), ("g", 3072, DIN)))
DZC_PIECES = _pieces((("k", 2048, 2560), ("v", 2560, 3072)))
_NT = (((1,), (1,)), ((), ()))


DH_SUBTILES = 2


def _dz_specs(tl):
    return [pl.BlockSpec((tl, 1536), lambda t: (t, 0))] + [pl.BlockSpec((tl, 512), lambda t: (t, 0))] * 4


def dh_bwd(dz_parts, w_full, x, dy, shift, scale, norm_g, dg_ctx):
    tl = 512
    nt = SEQ // tl

    def kern(a_ref, q_ref, k_ref, v_ref, g_ref, w_ref, x_ref, dy_ref, sh_ref, sc_ref, gn_ref, dgc_ref,
             gx_ref, dsh_ref, dsc_ref, dg_ref):
        @pl.when(pl.program_id(0) == 0)
        def _():
            dsh_ref[...] = jnp.zeros_like(dsh_ref)
            dsc_ref[...] = jnp.zeros_like(dsc_ref)
            dg_ref[...] = dgc_ref[...]

        src = dict(a=a_ref, q=q_ref, k=k_ref, v=v_ref, g=g_ref)
        for sub in range(DH_SUBTILES):
            rows = slice(sub * tl // DH_SUBTILES, (sub + 1) * tl // DH_SUBTILES)
            dh = None
            for j, l0, l1, name, s0, s1 in DZ_PIECES:
                part = lax.dot_general(src[name][rows, s0:s1], w_ref[j, :, l0:l1], _NT, preferred_element_type=F32)
                dh = part if dh is None else dh + part
            _, vjp = jax.vjp(_modulated, x_ref[rows, :], gn_ref[...], sc_ref[...], sh_ref[...])
            dx, dg, dsc, dsh = vjp(dh)
            gx_ref[rows, :] = dy_ref[rows, :] + dx
            dg_ref[...] += dg
            dsc_ref[...] += dsc
            dsh_ref[...] += dsh

    tile = pl.BlockSpec((tl, DM), lambda t: (t, 0))
    return pl.pallas_call(
        kern, name="dh_bwd", grid=(nt,),
        in_specs=_dz_specs(tl) + [pl.BlockSpec((NCHIP, DM, SHARD_IN), lambda t: (0, 0, 0)), tile, tile, _row(DM),
                                  _row(DM), _row(DM), _row(DM)],
        out_specs=[tile, _row(DM), _row(DM), _row(DM)],
        out_shape=[jax.ShapeDtypeStruct((SEQ, DM), F32)] + [jax.ShapeDtypeStruct((1, DM), F32)] * 3,
        compiler_params=_cparams(("arbitrary",), 48 * 1024 * 1024),
    )(*dz_parts, w_full, x, dy, shift, scale, norm_g, dg_ctx)


def dw_bwd(h, dz_parts, hc, dck, dcv, g_out):
    tl = 512
    nt = SEQ // tl
    (rhi, wi), (rho, wo) = RS_SHAPES

    def kern(h_ref, a_ref, q_ref, k_ref, v_ref, g_ref, hc_ref, dck_ref, dcv_ref, go_hbm,
             wire_i, keep_i, wire_o, keep_o, acc, rcv_i, mine_o, rcv_o, load_sem, send_sems, recv_sems):
        t = pl.program_id(0)
        x, y, c = _me()
        k = 2 * x + y
        sib = _flip(1)
        half = lambda hh, rh: pl.ds(pl.multiple_of(hh * rh, rh), rh)
        load_o = pltpu.make_async_copy(go_hbm.at[:, half(c, rho), :], mine_o, load_sem)
        pair_o = _rcopy(go_hbm.at[:, half(1 - c, rho), :], rcv_o, send_sems, recv_sems, 0, sib)
        pair_i = _rcopy(acc.at[:, half(1 - c, rhi), :], rcv_i, send_sems, recv_sems, 1, sib)

        @pl.when(t == 0)
        def _():
            load_o.start()
            pair_o.start()
            acc[...] = jnp.zeros_like(acc)
            hct = hc_ref[...].T
            csrc = dict(k=dck_ref, v=dcv_ref)
            for j, l0, l1, name, s0, s1 in DZC_PIECES:
                acc[j, :, l0:l1] += jnp.dot(hct, csrc[name][:, s0:s1].astype(BF16), preferred_element_type=F32)

        ht = h_ref[...].T
        src = dict(a=a_ref, q=q_ref, k=k_ref, v=v_ref, g=g_ref)
        for j, l0, l1, name, s0, s1 in DZ_PIECES:
            acc[j, :, l0:l1] += jnp.dot(ht, src[name][:, s0:s1], preferred_element_type=F32)

        @pl.when(t == nt - 1)
        def _():
            pair_i.start()
            load_o.wait()
            pair_o.wait_recv()
            for j in range(NCHIP):
                wire_o[j] = (mine_o[j] + rcv_o[j]).astype(BF16)
            keep_o[...] = mine_o[k] + rcv_o[k]
            pair_i.wait_recv()
            mine = half(c, rhi)
            for j in range(NCHIP):
                wire_i[j] = (acc[j, mine, :] + rcv_i[j]).astype(BF16)
            keep_i[...] = acc[k, mine, :] + rcv_i[k]
            pair_o.wait_send()
            pair_i.wait_send()

    whole = lambda *shape: pl.BlockSpec(shape, lambda t: (0,) * len(shape))
    return pl.pallas_call(
        kern, name="dw_bwd", grid=(nt,),
        in_specs=[pl.BlockSpec((tl, DM), lambda t: (t, 0))] + _dz_specs(tl)
        + [whole(CTX, DM), whole(CTX, 512), whole(CTX, 512), pl.BlockSpec(memory_space=pl.ANY)],
        out_specs=[whole(NCHIP, rhi, wi), whole(rhi, wi), whole(NCHIP, rho, wo), whole(rho, wo)],
        out_shape=[jax.ShapeDtypeStruct((NCHIP, rhi, wi), BF16), jax.ShapeDtypeStruct((rhi, wi), F32),
                   jax.ShapeDtypeStruct((NCHIP, rho, wo), BF16), jax.ShapeDtypeStruct((rho, wo), F32)],
        scratch_shapes=[pltpu.VMEM((NCHIP, DM, SHARD_IN), F32), pltpu.VMEM((NCHIP, rhi, wi), F32),
                        pltpu.VMEM((NCHIP, rho, wo), F32), pltpu.VMEM((NCHIP, rho, wo), F32),
                        pltpu.SemaphoreType.DMA(()), pltpu.SemaphoreType.DMA((2,)), pltpu.SemaphoreType.DMA((2,))],
        compiler_params=_cparams(("arbitrary",), VMEM_BIG),
    )(h, *dz_parts, hc, dck, dcv, g_out)


def ctx_bwd(dck, dcv, w_full, ctx, cshift, cscale, norm_g):
    def kern(dck_ref, dcv_ref, w_ref, c_ref, sh_ref, sc_ref, g_ref, dsh_ref, dsc_ref, dg_ref):
        csrc = dict(k=dck_ref, v=dcv_ref)
        dhc = None
        for j, l0, l1, name, s0, s1 in DZC_PIECES:
            part = lax.dot_general(csrc[name][:, s0:s1].astype(BF16), w_ref[j, :, l0:l1], _NT,
                                   preferred_element_type=F32)
            dhc = part if dhc is None else dhc + part
        _, vjp = jax.vjp(lambda g, sc, sh: _modulated(c_ref[...], g, sc, sh), g_ref[...], sc_ref[...], sh_ref[...])
        dg_ref[...], dsc_ref[...], dsh_ref[...] = vjp(dhc)

    whole = lambda r, c: pl.BlockSpec((r, c), lambda i: (0, 0))
    return pl.pallas_call(
        kern, name="ctx_bwd", grid=(1,),
        in_specs=[whole(CTX, 512), whole(CTX, 512), pl.BlockSpec((NCHIP, DM, SHARD_IN), lambda i: (0, 0, 0)),
                  whole(CTX, DM), _row(DM), _row(DM), _row(DM)],
        out_specs=[_row(DM), _row(DM), _row(DM)],
        out_shape=[jax.ShapeDtypeStruct((1, DM), F32)] * 3,
        compiler_params=_cparams(("arbitrary",), 40 * 1024 * 1024),
    )(dck, dcv, w_full, ctx, cshift, cscale, norm_g)


def _lane_pad_rpb(rpb):
    r = jnp.pad(rpb, ((0, 0), (0, 0), (0, GRID_W - rpb.shape[-1])))
    return jnp.concatenate([r, r], axis=-1)


def local_step(chip, dev, x, c_vec, c_ctx, w_ada, b_shard, ctx, target, norm_g, sgu_g, w_s, b_s, q_g, k_g, rpb,
               w_in_shard, w_out_shard):
    bsb = jnp.broadcast_to(b_s[:, :, None], (4, 128, 128))
    qg2, kg2 = jnp.tile(q_g, (1, 2)), jnp.tile(k_g, (1, 2))

    z, h, w_in_full, w_out_full, mod_all, cs = inproj_fwd(chip, x, c_vec, c_ctx, w_ada, b_shard, norm_g, w_in_shard,
                                                          w_out_shard)
    mods = mod_all.transpose(1, 0, 2).reshape(CS_ROWS, 3 * DM)
    mod = lax.dynamic_slice(mods, (8 * dev, 0), (1, 3 * DM))
    shift, scale, gate = mod[:, :DM], mod[:, DM:2 * DM], mod[:, 2 * DM:]
    cshift, cscale = mods[8 * NDEV:8 * NDEV + 1, :DM], mods[8 * NDEV:8 * NDEV + 1, DM:2 * DM]
    zc, hc = ctx_fwd(ctx, cshift, cscale, norm_g, w_in_full)
    bias = rpb_tables(_lane_pad_rpb(rpb))
    out_a = sgu_fwd(z, sgu_g, w_s, bsb)
    out_b, *saved = attn_fwd(z, zc, bias, qg2, kg2)
    loss8, dy, dcat, dgate, dwo = outproj(out_a, out_b, x, target, gate, w_out_full.reshape(DM, DM))
    dz_a, dsg, dws, dbsb = sgu_bwd(z, sgu_g, w_s, bsb, dcat)
    dq, dk, dv, dbg, dck, dcv, db0, db1, db2, dqg2, dkg2 = attn_bwd(z, zc, qg2, kg2, dcat, saved)
    drpb = rpb_bwd((db0, db1, db2))[:, :, :rpb.shape[-1]]
    dz_parts = (dz_a, dq, dk, dv, dbg)
    dcshift, dcscale, dng_c = ctx_bwd(dck, dcv, w_in_full, ctx, cshift, cscale, norm_g)
    wire_i, keep_i, wire_o, keep_o = dw_bwd(h, dz_parts, hc, dck, dcv, dwo.reshape(NCHIP, SHARD_OUT, DM))
    *in_flight, token = rs_start(wire_i, wire_o)
    grad_x, dshift, dscale, dng = dh_bwd(dz_parts, w_in_full, x, dy, shift, scale, norm_g, dng_c + token[0, 0])
    got_i, got_o = rs_wait(*in_flight, dshift)
    return dict(
        loss=loss8[0:1, 0:1], grad_x=grad_x, rs=(keep_i, got_i, keep_o, got_o), cs=cs,
        dmod=jnp.concatenate([dshift, dscale, dgate], axis=-1),
        dcmod=jnp.concatenate([dcshift, dcscale, jnp.zeros((1, DM), F32)], axis=-1),
        d_norm_g=dng, d_sgu_g=dsg, d_w_s=dws, d_b_s=dbsb[:, :, 0],
        d_q_g=dqg2[:, :HDIM], d_k_g=dkg2[:, :HDIM], d_rpb=drpb)


def _me():
    return lax.axis_index("x"), lax.axis_index("y"), lax.axis_index("c")


def _flip(q):
    x, y, c = _me()
    return ((1 - x) if q & 4 else x, (1 - y) if q & 2 else y, (1 - c) if q & 1 else c)


def _chip_of(dev):
    return 2 * dev[0] + dev[1]


def _rcopy(src, dst, send_sems, recv_sems, k, dev):
    return pltpu.make_async_remote_copy(src_ref=src, dst_ref=dst, send_sem=send_sems.at[k], recv_sem=recv_sems.at[k],
                                        device_id=dev, device_id_type=MESH_ID)


_VMEM_SPEC = pl.BlockSpec(memory_space=pltpu.VMEM)
SLAB_ROWS = 80


RS_SHAPES = ((DM // 2, SHARD_IN), (SHARD_OUT // 2, DM))
_HBM_SPEC = pl.BlockSpec(memory_space=pltpu.HBM)
_SEM_SPEC = pl.BlockSpec(memory_space=pltpu.SEMAPHORE)
_IN_FLIGHT = pltpu.SideEffectType.DATAFLOW_SIDE_EFFECTING


def _rs_copies(wires, lands, send_sems, recv_sems):
    return [pltpu.make_async_remote_copy(
        src_ref=wires[n].at[_chip_of(_flip(q))], dst_ref=lands[n].at[q // 2 - 1],
        send_sem=send_sems.at[3 * n + q // 2 - 1], recv_sem=recv_sems.at[3 * n + q // 2 - 1],
        device_id=_flip(q), device_id_type=MESH_ID) for n in (0, 1) for q in (2, 4, 6)]


def rs_start(wire_i, wire_o):
    lands = [lax.empty((NCHIP - 1, rh, w), BF16) for rh, w in RS_SHAPES]

    def body(wi_ref, wo_ref, li_ref, lo_ref, send_sems, recv_sems, wi_thru, wo_thru, li_thru, lo_thru, token):
        for cp in _rs_copies((wi_ref, wo_ref), (li_ref, lo_ref), send_sems, recv_sems):
            cp.start()
        token[...] = jnp.zeros_like(token)

    hbm = lambda a: pltpu.HBM(a.shape, a.dtype)
    return pl.pallas_call(
        body, name="rs_start",
        out_shape=(pltpu.SemaphoreType.DMA((6,)), pltpu.SemaphoreType.DMA((6,)), hbm(wire_i), hbm(wire_o),
                   hbm(lands[0]), hbm(lands[1]), jax.ShapeDtypeStruct((8, 128), F32)),
        in_specs=(_HBM_SPEC,) * 4, out_specs=(_SEM_SPEC, _SEM_SPEC) + (_HBM_SPEC,) * 4 + (_VMEM_SPEC,),
        input_output_aliases={0: 2, 1: 3, 2: 4, 3: 5},
        compiler_params=pltpu.CompilerParams(has_side_effects=_IN_FLIGHT),
    )(*[pltpu.with_memory_space_constraint(a, pltpu.HBM) for a in (wire_i, wire_o, *lands)])


def rs_wait(send_sems, recv_sems, wire_i, wire_o, land_i, land_o, after):
    def body(wi_ref, wo_ref, li_ref, lo_ref, send_sems, recv_sems, after_ref, wi_dead, wo_dead, gi_ref, go_ref):
        for cp in _rs_copies((wi_ref, wo_ref), (li_ref, lo_ref), send_sems, recv_sems):
            cp.wait_send()
            cp.wait_recv()

    hbm = lambda a: pltpu.HBM(a.shape, a.dtype)
    return pl.pallas_call(
        body, name="rs_wait", out_shape=(hbm(wire_i), hbm(wire_o), hbm(land_i), hbm(land_o)),
        in_specs=(_HBM_SPEC,) * 4 + (_SEM_SPEC, _SEM_SPEC, pl.BlockSpec(memory_space=pl.ANY)),
        out_specs=(_HBM_SPEC,) * 4, input_output_aliases={0: 0, 1: 1, 2: 2, 3: 3},
        compiler_params=pltpu.CompilerParams(has_side_effects=_IN_FLIGHT),
    )(wire_i, wire_o, land_i, land_o, send_sems, recv_sems, after)[2:]


def final_reduce(keep_i, got_i, keep_o, got_o, slab):
    def kern(ki_ref, gi_ref, ko_ref, go_ref, s_ref, gin_ref, gout_ref, all_ref, tot_ref, send_sems, recv_sems):
        x, y, c = _me()
        sib = _flip(1)
        dev = lambda d: 4 * d[0] + 2 * d[1] + d[2]
        me = dev((x, y, c))

        def slab_copy(idx, owner, to):
            return _rcopy(all_ref.at[dev(owner)], all_ref.at[dev(owner)], send_sems, recv_sems, idx, to)

        all_ref[me] = s_ref[...]
        first = [slab_copy(0, (x, y, c), sib)] + [slab_copy(q // 2, (x, y, c), _flip(q)) for q in (2, 4, 6)]
        for cp in first:
            cp.start()

        shares = []
        for n, (keep, got, out) in enumerate(((ki_ref, gi_ref, gin_ref), (ko_ref, go_ref, gout_ref))):
            rh = RS_SHAPES[n][0]
            half = lambda hh, rh=rh: pl.ds(pl.multiple_of(hh * rh, rh), rh)
            out[half(c), :] = ((keep[...] + got[0].astype(F32)) + got[1].astype(F32)) + got[2].astype(F32)
            share = _rcopy(out.at[half(c), :], out.at[half(c), :], send_sems, recv_sems, 7 + n, sib)
            share.start()
            shares.append((share, _rcopy(out.at[half(1 - c), :], out.at[half(1 - c), :], send_sems, recv_sems, 7 + n,
                                         sib)))

        passed = []
        for q in (2, 4, 6):
            slab_copy(q // 2, _flip(q), (x, y, c)).wait_recv()
            cp = slab_copy(3 + q // 2, _flip(q), sib)
            cp.start()
            passed.append(cp)
        slab_copy(0, sib, (x, y, c)).wait_recv()
        for q in (2, 4, 6):
            slab_copy(3 + q // 2, _flip(q | 1), (x, y, c)).wait_recv()
        tot = all_ref[0]
        for d in range(1, NDEV):
            tot = tot + all_ref[d]
        tot_ref[...] = tot
        for share, arrival in shares:
            arrival.wait_recv()
            share.wait_send()
        for cp in first + passed:
            cp.wait_send()

    (rhi, wi), (rho, wo) = RS_SHAPES
    return pl.pallas_call(
        kern, name="final_reduce", in_specs=[_VMEM_SPEC] * 5, out_specs=[_VMEM_SPEC] * 4,
        out_shape=[jax.ShapeDtypeStruct((2 * rhi, wi), F32), jax.ShapeDtypeStruct((2 * rho, wo), F32),
                   jax.ShapeDtypeStruct((NDEV, SLAB_ROWS, DM), F32), jax.ShapeDtypeStruct((SLAB_ROWS, DM), F32)],
        scratch_shapes=[pltpu.SemaphoreType.DMA((9,)), pltpu.SemaphoreType.DMA((9,))],
        compiler_params=pltpu.CompilerParams(vmem_limit_bytes=40 * 1024 * 1024),
    )(keep_i, got_i, keep_o, got_o, slab)


def ada_bwd(a_in, dm, dm_shard, w_ada, c_ctx):
    def kern(a_ref, dm_ref, dms_ref, w_ref, cc_ref, dw_ref, db_ref, dcc_ref, parts, send_sems, recv_sems):
        x, y, c = _me()
        k = 2 * x + y
        act = jax.nn.silu(a_ref[...]).astype(BF16)
        dms = dms_ref[...].astype(BF16)
        dw_ref[...] = lax.dot_general(act, dms, (((0,), (0,)), ((), ())), preferred_element_type=F32)
        db_ref[...] = jnp.sum(dm_ref[...], axis=0, keepdims=True)
        parts[k] = lax.dot_general(dms, w_ref[...].astype(BF16), (((1,), (1,)), ((), ())), preferred_element_type=F32)
        sends = [_rcopy(parts.at[k], parts.at[k], send_sems, recv_sems, q // 2 - 1, _flip(q)) for q in (2, 4, 6)]
        for cp in sends:
            cp.start()
        for q in (2, 4, 6):
            kq = _chip_of(_flip(q))
            _rcopy(parts.at[kq], parts.at[kq], send_sems, recv_sems, q // 2 - 1, _flip(q)).wait_recv()
        dact = ((parts[0] + parts[1]) + parts[2]) + parts[3]
        _, vjp = jax.vjp(jax.nn.silu, cc_ref[...])
        dcc_ref[...] = vjp(dact[8:9, :])[0]
        for cp in sends:
            cp.wait_send()

    return pl.pallas_call(
        kern, name="ada_bwd", in_specs=[_VMEM_SPEC] * 5, out_specs=[_VMEM_SPEC] * 3,
        out_shape=[jax.ShapeDtypeStruct((DM, SHARD_ADA), F32), jax.ShapeDtypeStruct((1, 3 * DM), F32),
                   jax.ShapeDtypeStruct((1, DM), F32)],
        scratch_shapes=[pltpu.VMEM((NCHIP, 16, DM), F32), pltpu.SemaphoreType.DMA((3,)), pltpu.SemaphoreType.DMA((3,))],
    )(a_in, dm, dm_shard, w_ada, c_ctx)


def _adamw_math(w, g, m, v):
    m = B1 * m + (1.0 - B1) * g
    v = B2 * v + (1.0 - B2) * (g * g)
    m_hat = m / (1.0 - B1 ** STEP)
    v_hat = v / (1.0 - B2 ** STEP)
    return -LR * (m_hat / (jnp.sqrt(v_hat) + ADAM_EPS) + WD * w), m, v


def adamw_big(w, g, m, v, name, block_rows=256):
    rows, width = w.shape

    def kern(w_ref, g_ref, m_ref, v_ref, d_ref, nm_ref, nv_ref):
        d_ref[...], nm_ref[...], nv_ref[...] = _adamw_math(w_ref[...], g_ref[...], m_ref[...], v_ref[...])

    spec = pl.BlockSpec((block_rows, width), lambda i: (i, 0))
    return pl.pallas_call(
        kern, name=name, grid=(rows // block_rows,), in_specs=[spec] * 4, out_specs=[spec] * 3,
        out_shape=[jax.ShapeDtypeStruct((rows, width), F32)] * 3,
        compiler_params=_cparams(("arbitrary",)),
    )(w, g, m, v)


def adamw_small(quads):
    n = len(quads)

    def kern(*refs):
        ins, outs = refs[:4 * n], refs[4 * n:]
        for i in range(n):
            w, g, m, v = (r[...] for r in ins[4 * i:4 * i + 4])
            outs[3 * i][...], outs[3 * i + 1][...], outs[3 * i + 2][...] = _adamw_math(w, g, m, v)

    flat = [a for quad in quads for a in quad]
    res = pl.pallas_call(
        kern, name="adamw_small", in_specs=[_VMEM_SPEC] * (4 * n), out_specs=[_VMEM_SPEC] * (3 * n),
        out_shape=[jax.ShapeDtypeStruct(q[0].shape, F32) for q in quads for _ in range(3)],
    )(*flat)
    return [tuple(res[3 * i:3 * i + 3]) for i in range(n)]


def _rows_of(a, rows):
    flat = a.reshape(-1)
    return jnp.pad(flat, (0, rows * DM - flat.shape[0])).reshape(rows, DM)


def kernel(x, c, ctx, c_ctx, w_ada, b_ada, norm_g, w_in, sgu_norm_g, w_spatial, b_spatial, q_norm_g, k_norm_g, rpb, w_out, loss_target, m_c_ctx, m_w_ada, m_b_ada, m_norm_g, m_w_in, m_sgu_norm_g, m_w_spatial, m_b_spatial, m_q_norm_g, m_k_norm_g, m_rpb, m_w_out, v_c_ctx, v_w_ada, v_b_ada, v_norm_g, v_w_in, v_sgu_norm_g, v_w_spatial, v_b_spatial, v_q_norm_g, v_k_norm_g, v_rpb, v_w_out):
    xi, yi, ci = lax.axis_index("x"), lax.axis_index("y"), lax.axis_index("c")
    chip, dev = 2 * xi + yi, 4 * xi + 2 * yi + ci
    c_ctx2 = c_ctx.reshape(1, DM)

    b_shard = lax.dynamic_slice(b_ada, (0, chip * SHARD_ADA), (1, SHARD_ADA))
    part = local_step(chip.reshape(1).astype(jnp.int32), dev, x[0], c, c_ctx2, w_ada[0], b_shard, ctx[0], loss_target[0],
                      norm_g, sgu_norm_g, w_spatial[0], b_spatial[0], q_norm_g, k_norm_g, rpb[0], w_in[0], w_out[0])
    cs = part["cs"]

    slab = jnp.concatenate([
        part["d_norm_g"], _rows_of(part["d_sgu_g"], 1), _rows_of(part["d_b_s"], 1),
        _rows_of(jnp.concatenate([part["d_q_g"], part["d_k_g"]], axis=-1), 1), _rows_of(part["d_rpb"], 4),
        _rows_of(part["loss"], 1), _rows_of(part["dcmod"], 3), _rows_of(part["dmod"], 3), jnp.zeros((1, DM), F32),
        _rows_of(part["d_w_s"], 64)], axis=0)
    g_w_in, g_w_out, gathered, tot = final_reduce(*part["rs"], slab)
    dm = jnp.concatenate([gathered[:, 12:15, :].reshape(NDEV, 3 * DM), tot[9:12].reshape(1, 3 * DM),
                          jnp.zeros((7, 3 * DM), F32)], axis=0)
    a_in = jnp.concatenate([cs[0:8 * NDEV:8], cs[8 * NDEV:8 * NDEV + 1], jnp.zeros((7, DM), F32)], axis=0)
    dm_shard = lax.dynamic_slice(dm, (0, chip * SHARD_ADA), (16, SHARD_ADA))
    g_w_ada, g_b_ada, g_c_ctx = ada_bwd(a_in, dm, dm_shard, w_ada[0], c_ctx2)

    loss = tot[8, 0]
    g_small = dict(
        c_ctx=g_c_ctx, b_ada=g_b_ada, norm_g=tot[0:1], sgu_norm_g=tot[1:2, :512], w_spatial=tot[16:80].reshape(512, 128),
        b_spatial=tot[2:3, :512].reshape(4, 128), q_norm_g=tot[3:4, :HDIM], k_norm_g=tot[3:4, HDIM:2 * HDIM],
        rpb=tot[4:8].reshape(-1)[:HEADS * 15 * 31].reshape(HEADS * 15, 31))
    shapes = dict(c_ctx=(DM,), w_ada=(1, DM, SHARD_ADA), b_ada=(1, 3 * DM), norm_g=(1, DM), w_in=(1, DM, SHARD_IN),
                  sgu_norm_g=(1, 512), w_spatial=(1, 4, 128, 128), b_spatial=(1, 4, 128), q_norm_g=(1, HDIM),
                  k_norm_g=(1, HDIM), rpb=(1, HEADS, 15, 31), w_out=(1, SHARD_OUT, DM))
    names = list(shapes)
    weights = dict(c_ctx=c_ctx, w_ada=w_ada, b_ada=b_ada, norm_g=norm_g, w_in=w_in, sgu_norm_g=sgu_norm_g,
                   w_spatial=w_spatial, b_spatial=b_spatial, q_norm_g=q_norm_g, k_norm_g=k_norm_g, rpb=rpb, w_out=w_out)
    m_in = dict(zip(names, (m_c_ctx, m_w_ada, m_b_ada, m_norm_g, m_w_in, m_sgu_norm_g, m_w_spatial, m_b_spatial,
                            m_q_norm_g, m_k_norm_g, m_rpb, m_w_out)))
    v_in = dict(zip(names, (v_c_ctx, v_w_ada, v_b_ada, v_norm_g, v_w_in, v_sgu_norm_g, v_w_spatial, v_b_spatial,
                            v_q_norm_g, v_k_norm_g, v_rpb, v_w_out)))
    grads = dict(g_small, w_ada=g_w_ada, w_in=g_w_in, w_out=g_w_out)
    upd = {}
    for n in ("w_ada", "w_in", "w_out"):
        g = grads[n]
        upd[n] = adamw_big(weights[n].reshape(g.shape), g, m_in[n].reshape(g.shape), v_in[n].reshape(g.shape),
                           "adamw_" + n)
    small = [n for n in names if n not in upd]
    res = adamw_small([(weights[n].reshape(grads[n].shape), grads[n], m_in[n].reshape(grads[n].shape),
                        v_in[n].reshape(grads[n].shape)) for n in small])
    upd.update(zip(small, res))
    out = [loss, part["grad_x"].reshape(1, SEQ, DM)]
    out += [grads[n].reshape(shapes[n]) for n in names]
    for slot in range(3):
        out += [upd[n][slot].reshape(shapes[n]) for n in names]
    return tuple(out)
```

```python
import functools

import jax
import jax.numpy as jnp
from jax import lax
from jax.experimental import pallas as pl
from jax.experimental.pallas import tpu as pltpu

F32, BF16 = jnp.float32, jnp.bfloat16
SEQ, DM, CTX, DIN = 4096, 1024, 256, 3584
NCHIP, NDEV = 4, 8
SHARD_IN = DIN // NCHIP
SHARD_ADA = 3 * DM // NCHIP
SHARD_OUT = DM // NCHIP
GRID_W = 64
QROWS = 4
KROWS = 12
QBLK, KBLK = QROWS * GRID_W, KROWS * GRID_W
NQBLK = SEQ // QBLK
HEADS, HDIM, NPAIR = 8, 64, 4
EPS = 1e-6
NEG_INF = -1e30
ZQ, ZK, ZV, ZG = 12, 16, 20, 24
LR, B1, B2, ADAM_EPS, WD, STEP = 0.001, 0.9, 0.999, 1e-08, 0.01, 10
VMEM_BIG = 56 * 1024 * 1024
MESH_ID = pl.DeviceIdType.MESH


def _dot(a, b, lhs_c, rhs_c):
    return lax.dot_general(a.astype(BF16), b.astype(BF16), (((lhs_c,), (rhs_c,)), ((), ())),
                           preferred_element_type=F32)


@jax.custom_vjp
def mm(a, b):
    return _dot(a, b, 1, 0)


@jax.custom_vjp
def mm_nt(a, b):
    return _dot(a, b, 1, 1)


@jax.custom_vjp
def mm_tn(a, b):
    return _dot(a, b, 0, 0)


mm.defvjp(lambda a, b: (mm(a, b), (a, b)), lambda r, ct: (mm_nt(ct, r[1]), mm_tn(r[0], ct)))
mm_nt.defvjp(lambda a, b: (mm_nt(a, b), (a, b)), lambda r, ct: (mm(ct, r[1]), mm_tn(ct, r[0])))
mm_tn.defvjp(lambda a, b: (mm_tn(a, b), (a, b)), lambda r, ct: (mm_nt(r[1], ct), mm(r[0], ct)))


def _rms(x, g):
    return x * lax.rsqrt(jnp.mean(x * x, axis=-1, keepdims=True) + EPS) * g


def _modulated(x, g, scale, shift):
    return _rms(x, g) * (1.0 + scale) + shift


def _pair_rms(x, g2):
    lo = lax.broadcasted_iota(jnp.int32, (1, 2 * HDIM), 1) < HDIM
    sq = x * x
    s_lo = jnp.sum(jnp.where(lo, sq, 0.0), axis=-1, keepdims=True)
    s_hi = jnp.sum(jnp.where(lo, 0.0, sq), axis=-1, keepdims=True)
    rs = jnp.where(lo, lax.rsqrt(s_lo / HDIM + EPS), lax.rsqrt(s_hi / HDIM + EPS))
    return x * rs * g2


def _cparams(sem, vmem=None):
    return pltpu.CompilerParams(dimension_semantics=sem, vmem_limit_bytes=vmem)


def _row(n):
    return pl.BlockSpec((1, n), lambda *_: (0, 0))


CS_ROWS = 8 * NDEV + 8


def _mod_part(mod_ref, row, part):
    pieces = []
    for j in range(NCHIP):
        lo, hi = max(part * DM, j * SHARD_ADA), min((part + 1) * DM, (j + 1) * SHARD_ADA)
        if lo < hi:
            pieces.append(mod_ref[j, row, lo - j * SHARD_ADA:hi - j * SHARD_ADA])
    return jnp.concatenate(pieces, axis=-1)


def inproj_fwd(chip, x, c_vec, c_ctx, w_ada, b_shard, norm_g, w_shard, wo_shard):
    tl = 1024
    nt = SEQ // tl
    halves = (DM // 2, SHARD_OUT // 2)
    n_w, n_c = 12, NDEV - 1

    def kern(k_ref, x_ref, cv_ref, cc_ref, wa_ref, b_ref, g_ref, w_ref, wo_ref,
             z_ref, h_ref, wfull_ref, wofull_ref, modall_ref, csall_ref,
             w_scr, wo_scr, h_scr, mine, cs_scr, mod_scr, shsc_scr, send_sems, recv_sems):
        s, t = pl.program_id(0), pl.program_id(1)
        xi, yi, c = _me()
        k, me = 2 * xi + yi, 4 * xi + 2 * yi + c
        sib = _flip(1)
        rows = pl.ds(pl.multiple_of(t * tl, tl), tl)
        gathered = (w_scr, wo_scr)
        slot = lambda d: pl.ds(pl.multiple_of(8 * d, 8), 8)

        def c_copy(q, owner):
            return _rcopy(mine, cs_scr.at[slot(owner), :], send_sems, recv_sems, n_w + q - 1, _flip(q))

        def m_copy(q, chip_of_block):
            return _rcopy(mod_scr.at[chip_of_block], mod_scr.at[chip_of_block], send_sems, recv_sems,
                          n_w + n_c + q // 2 - 1, _flip(q))

        def adaln():
            first = lax.broadcasted_iota(jnp.int32, (8, DM), 0) == 0
            mine[...] = jnp.where(first, jnp.broadcast_to(cv_ref[...], (8, DM)), 0.0)
            cs_scr[slot(me), :] = mine[...]
            cs_scr[slot(NDEV), :] = jnp.where(first, jnp.broadcast_to(cc_ref[...], (8, DM)), 0.0)
            for q in range(1, NDEV):
                c_copy(q, me).start()
            wa = wa_ref[...].astype(BF16)
            for q in range(1, NDEV):
                px, py, pc = _flip(q)
                c_copy(q, 4 * px + 2 * py + pc).wait_recv()
            act = jax.nn.silu(cs_scr[...]).astype(BF16)
            mod_scr[k] = jnp.dot(act, wa, preferred_element_type=F32) + b_ref[...]
            for q in (2, 4, 6):
                m_copy(q, k).start()
            for q in (2, 4, 6):
                m_copy(q, _chip_of(_flip(q))).wait_recv()
            row = pl.ds(8 * me, 1)
            shsc_scr[0:1, :] = _mod_part(mod_scr, row, 0)
            shsc_scr[1:2, :] = _mod_part(mod_scr, row, 1)
            pltpu.sync_copy(mod_scr, modall_ref)
            pltpu.sync_copy(cs_scr, csall_ref)

        def block(n, chip_of_block, hh):
            return gathered[n].at[chip_of_block, pl.ds(pl.multiple_of(hh * halves[n], halves[n]), halves[n]), :]

        def ici(n, q, chip_of_block):
            blk = block(n, chip_of_block, c)
            return _rcopy(blk, blk, send_sems, recv_sems, 6 * n + q // 2 - 1, _flip(q))

        def d2d(n, q, chip_of_block, hh):
            blk = block(n, chip_of_block, hh)
            return _rcopy(blk, blk, send_sems, recv_sems, 6 * n + 3 + q // 2 - 1, sib)

        @pl.when((s == 0) & (t == 0))
        def _():
            adaln()
            w_scr[k] = w_ref[...].astype(BF16)
            wo_scr[k] = wo_ref[...].astype(BF16)
            for q in (2, 4, 6):
                ici(0, q, k).start()
                ici(1, q, k).start()

        for sweep in (1, 2, 3):
            @pl.when((s == sweep) & (t == 0))
            def _():
                q = 2 * sweep
                src = _chip_of(_flip(q))
                for n in (0, 1):
                    ici(n, q, src).wait_recv()
                    d2d(n, q, src, c).start()
                for n in (0, 1):
                    d2d(n, q, src, 1 - c).wait_recv()

        @pl.when(s == 0)
        def _():
            hb = _modulated(x_ref[...], g_ref[...], shsc_scr[1:2, :], shsc_scr[0:1, :]).astype(BF16)
            h_scr[rows, :] = hb
            h_ref[...] = hb

        z_ref[...] = jnp.dot(h_scr[rows, :], w_scr[lax.bitwise_xor(k, s)], preferred_element_type=F32)

        @pl.when((s == NCHIP - 1) & (t == nt - 1))
        def _():
            for q in range(1, NDEV):
                c_copy(q, me).wait_send()
            for q in (2, 4, 6):
                m_copy(q, k).wait_send()
            for n in (0, 1):
                for q in (2, 4, 6):
                    ici(n, q, k).wait_send()
                    d2d(n, q, _chip_of(_flip(q)), c).wait_send()
            pltpu.sync_copy(w_scr, wfull_ref)
            pltpu.sync_copy(wo_scr, wofull_ref)

    once = lambda s, t, k: (jnp.where(s == 0, t, nt - 1), 0)
    hbm = pl.BlockSpec(memory_space=pl.ANY)
    n_sem = n_w + n_c + 3
    return pl.pallas_call(
        kern, name="inproj_fwd",
        grid_spec=pltpu.PrefetchScalarGridSpec(
            num_scalar_prefetch=1, grid=(NCHIP, nt),
            in_specs=[pl.BlockSpec((tl, DM), once)] + [_VMEM_SPEC] * 7,
            out_specs=[pl.BlockSpec((tl, SHARD_IN), lambda s, t, k: (t, lax.bitwise_xor(k[0], s))),
                       pl.BlockSpec((tl, DM), once), hbm, hbm, hbm, hbm],
            scratch_shapes=[pltpu.VMEM((NCHIP, DM, SHARD_IN), BF16), pltpu.VMEM((NCHIP, SHARD_OUT, DM), BF16),
                            pltpu.VMEM((SEQ, DM), BF16), pltpu.VMEM((8, DM), F32), pltpu.VMEM((CS_ROWS, DM), F32),
                            pltpu.VMEM((NCHIP, CS_ROWS, SHARD_ADA), F32), pltpu.VMEM((8, DM), F32),
                            pltpu.SemaphoreType.DMA((n_sem,)), pltpu.SemaphoreType.DMA((n_sem,))]),
        out_shape=[jax.ShapeDtypeStruct((SEQ, DIN), F32), jax.ShapeDtypeStruct((SEQ, DM), BF16),
                   jax.ShapeDtypeStruct((NCHIP, DM, SHARD_IN), BF16), jax.ShapeDtypeStruct((NCHIP, SHARD_OUT, DM), BF16),
                   jax.ShapeDtypeStruct((NCHIP, CS_ROWS, SHARD_ADA), F32), jax.ShapeDtypeStruct((CS_ROWS, DM), F32)],
        compiler_params=_cparams(("arbitrary", "arbitrary"), VMEM_BIG),
    )(chip, x, c_vec, c_ctx, w_ada, b_shard, norm_g, w_shard, wo_shard)


def ctx_fwd(ctx, cshift, cscale, norm_g, w_full):
    def kern(c_ref, sh_ref, sc_ref, g_ref, w2_ref, w3_ref, zc_ref, hc_ref):
        hc = _modulated(c_ref[...], g_ref[...], sc_ref[...], sh_ref[...]).astype(BF16)
        hc_ref[...] = hc
        zc_ref[:, :SHARD_IN] = jnp.dot(hc, w2_ref[0], preferred_element_type=F32)
        zc_ref[:, SHARD_IN:] = jnp.dot(hc, w3_ref[0], preferred_element_type=F32)

    return pl.pallas_call(
        kern, name="ctx_fwd", grid=(1,),
        in_specs=[pl.BlockSpec((CTX, DM), lambda i: (0, 0)), _row(DM), _row(DM), _row(DM),
                  pl.BlockSpec((1, DM, SHARD_IN), lambda i: (2, 0, 0)),
                  pl.BlockSpec((1, DM, SHARD_IN), lambda i: (3, 0, 0))],
        out_specs=[pl.BlockSpec((CTX, 2 * SHARD_IN), lambda i: (0, 0)),
                   pl.BlockSpec((CTX, DM), lambda i: (0, 0))],
        out_shape=[jax.ShapeDtypeStruct((CTX, 2 * SHARD_IN), F32), jax.ShapeDtypeStruct((CTX, DM), BF16)],
        compiler_params=_cparams(("arbitrary",)),
    )(ctx, cshift, cscale, norm_g, w_full, w_full)


SGU_CHUNK, SGU_PER_STEP = 128, 4


def _gelu(x):
    return 0.5 * x * (1.0 + lax.erf(x * 0.7071067811865476))


def _sgu_chunk(au, av, ag, sg, ws, bsb):
    u, v = _gelu(au), _gelu(av)
    outs = []
    for g in range(4):
        sl = slice(128 * g, 128 * (g + 1))
        mixed = mm(ws[g], _rms(v[:, sl], sg[:, sl])) + bsb[g]
        outs.append(u[:, sl] * mixed * jax.nn.silu(ag[:, sl]))
    return jnp.concatenate(outs, axis=-1)


def _sgu_specs():
    rows = SGU_CHUNK * SGU_PER_STEP
    zspec = lambda c: pl.BlockSpec((rows, 512), lambda n: (n, c))
    wspec = pl.BlockSpec((4, 128, 128), lambda n: (0, 0, 0))
    return rows, [zspec(0), zspec(1), zspec(2), _row(512), wspec, wspec]


def sgu_fwd(z, sg, ws, bsb):
    rows, in_specs = _sgu_specs()

    def kern(au_ref, av_ref, ag_ref, sg_ref, ws_ref, bs_ref, o_ref):
        for c in range(SGU_PER_STEP):
            sl = slice(c * SGU_CHUNK, (c + 1) * SGU_CHUNK)
            o_ref[sl, :] = _sgu_chunk(au_ref[sl, :], av_ref[sl, :], ag_ref[sl, :], sg_ref[...], ws_ref[...],
                                      bs_ref[...])

    return pl.pallas_call(
        kern, name="sgu_fwd", grid=(SEQ // rows,), in_specs=in_specs,
        out_specs=pl.BlockSpec((rows, 512), lambda n: (n, 0)),
        out_shape=jax.ShapeDtypeStruct((SEQ, 512), F32),
        compiler_params=_cparams(("arbitrary",)),
    )(z, z, z, sg, ws, bsb)


def sgu_bwd(z, sg, ws, bsb, dcat):
    rows, in_specs = _sgu_specs()

    def kern(au_ref, av_ref, ag_ref, sg_ref, ws_ref, bs_ref, do_ref, dz_ref, dsg_ref, dws_ref, dbs_ref):
        @pl.when(pl.program_id(0) == 0)
        def _():
            dsg_ref[...] = jnp.zeros_like(dsg_ref)
            dws_ref[...] = jnp.zeros_like(dws_ref)
            dbs_ref[...] = jnp.zeros_like(dbs_ref)

        for c in range(SGU_PER_STEP):
            sl = slice(c * SGU_CHUNK, (c + 1) * SGU_CHUNK)
            _, vjp = jax.vjp(_sgu_chunk, au_ref[sl, :], av_ref[sl, :], ag_ref[sl, :], sg_ref[...], ws_ref[...],
                             bs_ref[...])
            dau, dav, dag, dsg, dws, dbs = vjp(do_ref[sl, :])
            dz_ref[sl, 0:512] = dau.astype(BF16)
            dz_ref[sl, 512:1024] = dav.astype(BF16)
            dz_ref[sl, 1024:1536] = dag.astype(BF16)
            dsg_ref[...] += dsg
            dws_ref[...] += dws
            dbs_ref[...] += dbs

        @pl.when(pl.program_id(0) == pl.num_programs(0) - 1)
        def _():
            dbs_ref[...] = jnp.broadcast_to(jnp.sum(dbs_ref[...], axis=-1, keepdims=True), dbs_ref.shape)

    wspec = pl.BlockSpec((4, 128, 128), lambda n: (0, 0, 0))
    return pl.pallas_call(
        kern, name="sgu_bwd", grid=(SEQ // rows,),
        in_specs=in_specs + [pl.BlockSpec((rows, 512), lambda n: (n, 0))],
        out_specs=[pl.BlockSpec((rows, 1536), lambda n: (n, 0)), _row(512), wspec, wspec],
        out_shape=[jax.ShapeDtypeStruct((SEQ, 1536), BF16), jax.ShapeDtypeStruct((1, 512), F32),
                   jax.ShapeDtypeStruct((4, 128, 128), F32), jax.ShapeDtypeStruct((4, 128, 128), F32)],
        compiler_params=_cparams(("arbitrary",)),
    )(z, z, z, sg, ws, bsb, dcat)


_DR_OFF = (7, 3, -1)


def _row_valid(v, rr, j):
    return (j < 8, rr <= j < rr + 8, 4 <= j < 12)[v]


def _col_window():
    q = lax.broadcasted_iota(jnp.int32, (GRID_W, 128), 0)
    kc = lax.broadcasted_iota(jnp.int32, (GRID_W, 128), 1) % GRID_W
    c0 = jnp.clip(q - 8, 0, GRID_W - 16)
    return (kc >= c0) & (kc < c0 + 16)


def rpb_tables(rpb2):
    def kern(r_ref, b_ref):
        base = r_ref[0]
        lo = lax.broadcasted_iota(jnp.int32, (1, 128), 1) < GRID_W
        win = _col_window()
        tiles = {}
        for v in range(3):
            for rr in range(QROWS):
                for jp in range(KROWS // 2):
                    j0, j1 = 2 * jp, 2 * jp + 1
                    ok0, ok1 = _row_valid(v, rr, j0), _row_valid(v, rr, j1)
                    key = (j0 - rr + _DR_OFF[v], ok0, ok1) if (ok0 or ok1) else None
                    if key not in tiles:
                        if key is None:
                            tiles[key] = jnp.full((GRID_W, 128), NEG_INF, F32)
                        else:
                            d0 = key[0]
                            r0 = base[d0:d0 + 1, :] if ok0 else jnp.zeros((1, 128), F32)
                            r1 = base[d0 + 1:d0 + 2, :] if ok1 else jnp.zeros((1, 128), F32)
                            y = jnp.broadcast_to(jnp.where(lo, r0, r1), (GRID_W, 128))
                            y = pltpu.roll(pltpu.roll(y, 128 - 15, 1), 0, 1, stride=1, stride_axis=0)
                            tiles[key] = jnp.where(win & jnp.where(lo, ok0, ok1), y, NEG_INF)
                    b_ref[v, 0, rr * GRID_W:(rr + 1) * GRID_W, jp * 128:(jp + 1) * 128] = tiles[key]

    return pl.pallas_call(
        kern, name="rpb_tables", grid=(HEADS,),
        in_specs=[pl.BlockSpec((1, 15, 128), lambda h: (h, 0, 0))],
        out_specs=pl.BlockSpec((3, 1, QBLK, KBLK), lambda h: (0, h, 0, 0)),
        out_shape=jax.ShapeDtypeStruct((3, HEADS, QBLK, KBLK), F32),
        compiler_params=_cparams(("arbitrary",)),
    )(rpb2)


def rpb_bwd(dbias):
    def kern(g0_ref, g1_ref, g2_ref, o_ref):
        g_refs = (g0_ref.at[0], g1_ref.at[0], g2_ref.at[0])
        lo = lax.broadcasted_iota(jnp.int32, (1, 128), 1) < GRID_W
        ri = lax.broadcasted_iota(jnp.int32, (GRID_W, GRID_W), 0)
        ci = lax.broadcasted_iota(jnp.int32, (GRID_W, GRID_W), 1)
        flip = (ri + ci == GRID_W - 1).astype(F32)
        groups = {}
        for v in range(3):
            for rr in range(QROWS):
                for jp in range(KROWS // 2):
                    j0, j1 = 2 * jp, 2 * jp + 1
                    ok0, ok1 = _row_valid(v, rr, j0), _row_valid(v, rr, j1)
                    if not (ok0 or ok1):
                        continue
                    g = g_refs[v][0, rr * GRID_W:(rr + 1) * GRID_W, jp * 128:(jp + 1) * 128]
                    key = (j0 - rr + _DR_OFF[v], ok0, ok1)
                    groups[key] = g if key not in groups else groups[key] + g
        acc = [jnp.zeros((1, 128), F32) for _ in range(15)]
        for (d0, ok0, ok1), g in groups.items():
            g = lax.dot_general(flip, g, (((1,), (0,)), ((), ())), precision=lax.Precision.HIGHEST,
                                preferred_element_type=F32)
            g = pltpu.roll(pltpu.roll(g, 128 - 48, 1), 0, 1, stride=1, stride_axis=0)
            s = jnp.sum(g, axis=0, keepdims=True)
            if ok0:
                acc[d0] = acc[d0] + jnp.where(lo, s, 0.0)
            if ok1:
                acc[d0 + 1] = acc[d0 + 1] + jnp.where(lo, 0.0, s)
        for d in range(15):
            o_ref[0, d:d + 1, :] = acc[d] + pltpu.roll(acc[d], GRID_W, 1)

    return pl.pallas_call(
        kern, name="rpb_bwd", grid=(HEADS,),
        in_specs=[pl.BlockSpec((1, 1, QBLK, KBLK), functools.partial(lambda v, h: (v, h, 0, 0), v)) for v in range(3)],
        out_specs=pl.BlockSpec((1, 15, 128), lambda h: (h, 0, 0)),
        out_shape=jax.ShapeDtypeStruct((HEADS, 15, 128), F32),
        compiler_params=_cparams(("arbitrary",)),
    )(dbias, dbias, dbias)


def _scaled_q(q_raw, qg):
    return _pair_rms(q_raw, qg) * (HDIM ** -0.5)


def _head_lanes():
    lo = lax.broadcasted_iota(jnp.int32, (1, 2 * HDIM), 1) < HDIM
    return lo, jnp.logical_not(lo)


SOFTMAX_ROWS = 32


def _emit_interleaved(vector_work, matmul_work):
    for j in range(max(len(vector_work), len(matmul_work))):
        for work in (vector_work, matmul_work):
            if j < len(work):
                work[j]()


def _kblock(i):
    return jnp.clip(i - 1, 0, (SEQ - KBLK) // QBLK)


def _kstart(i):
    return pl.multiple_of(_kblock(i) * QBLK, QBLK)


ATTN_STEPS = NQBLK // 2
ATTN_ROWS = 2 * QBLK
KCOLS = QBLK


def _attn_in_specs():
    return [
        pl.BlockSpec((ATTN_ROWS, 128), lambda p, i: (i, ZQ + p)),
        pl.BlockSpec((SEQ, 128), lambda p, i: (0, ZK + p)),
        pl.BlockSpec((SEQ, 128), lambda p, i: (0, ZV + p)),
        pl.BlockSpec((ATTN_ROWS, 128), lambda p, i: (i, ZG + p)),
        pl.BlockSpec((CTX, 128), lambda p, i: (0, 2 + p)),
        pl.BlockSpec((CTX, 128), lambda p, i: (0, 6 + p)),
    ]


def _bias_specs():
    bias_spec = lambda variant: pl.BlockSpec((1, 2, QBLK, KBLK), lambda p, i: (variant(i), p, 0, 0))
    return [bias_spec(lambda i: jnp.where(i == 0, 0, 1)),
            bias_spec(lambda i: jnp.where(i == ATTN_STEPS - 1, 2, 1))]


def _prob_specs():
    return [pl.BlockSpec((2, ATTN_ROWS, KBLK), lambda p, i: (p, i, 0)),
            pl.BlockSpec((2, ATTN_ROWS, CTX), lambda p, i: (p, i, 0))]


NORM_ROWS = 512


def _norm_keys(k_ref, ck_ref, kg_ref, kn_scr, ckn_scr):
    def body(c, carry):
        sl = pl.ds(pl.multiple_of(c * NORM_ROWS, NORM_ROWS), NORM_ROWS)
        kn_scr[sl, :] = _pair_rms(k_ref[sl, :], kg_ref[...]).astype(BF16)
        return carry

    lax.fori_loop(0, SEQ // NORM_ROWS, body, 0)
    ckn_scr[...] = _pair_rms(ck_ref[...], kg_ref[...]).astype(BF16)


def _values_with_ones(v_ref, cv_ref, v1_scr, cv1_scr):
    for a, mine in enumerate(_head_lanes()):
        def body(c, carry):
            sl = pl.ds(pl.multiple_of(c * NORM_ROWS, NORM_ROWS), NORM_ROWS)
            v1_scr[a, sl, :] = jnp.where(mine, v_ref[sl, :], 1.0).astype(BF16)
            return carry

        lax.fori_loop(0, SEQ // NORM_ROWS, body, 0)
        cv1_scr[a] = jnp.where(mine, cv_ref[...], 1.0).astype(BF16)


def attn_fwd(z, zc, bias, qg2, kg2):
    def kern(q_ref, k_ref, v_ref, bg_ref, ck_ref, cv_ref, be_ref, bo_ref, qg_ref, kg_ref,
             ob_ref, o_ref, rden_ref, pl_ref, pc_ref, kn_scr, ckn_scr, v1_scr, cv1_scr, s_scr):
        i = pl.program_id(1)

        @pl.when(i == 0)
        def _():
            _norm_keys(k_ref, ck_ref, kg_ref, kn_scr, ckn_scr)
            _values_with_ones(v_ref, cv_ref, v1_scr, cv1_scr)

        heads = _head_lanes()
        bias_refs = (be_ref, bo_ref)
        tiles = [(b, a) for b in range(2) for a in range(2)]
        rows = [slice(b * QBLK, (b + 1) * QBLK) for b in range(2)]
        qn = [_scaled_q(q_ref[rows[b], :], qg_ref[...]) for b in range(2)]
        qa = [jnp.where(heads[a], qn[b], 0.0).astype(BF16) for b, a in tiles]
        pv = [None] * len(tiles)
        done = {}
        latent = KBLK // KCOLS

        def keys(b, n):
            return pl.ds(pl.multiple_of(_kstart(2 * i + b) + n * KCOLS, KCOLS), KCOLS)

        def score_piece(t, n):
            b, a = tiles[t]
            cols = slice(n * KCOLS, (n + 1) * KCOLS)
            if n < latent:
                s_scr[t, :, cols] = mm_nt(qa[t], kn_scr[keys(b, n), :]) + bias_refs[b][0, a, :, cols]
            else:
                s_scr[t, :, cols] = mm_nt(qa[t], ckn_scr[...])

        def softmax_rows(t, r):
            b, a = tiles[t]
            rs = slice(r * SOFTMAX_ROWS, (r + 1) * SOFTMAX_ROWS)
            out_rows = slice(b * QBLK + rs.start, b * QBLK + rs.stop)
            s = s_scr[t, rs, :]
            p = jnp.exp(s - jnp.max(s, axis=-1, keepdims=True)).astype(BF16)
            pl_ref[a, out_rows, :] = p[:, :KBLK]
            pc_ref[a, out_rows, :] = p[:, KBLK:]

        def value_piece(t, n):
            b, a = tiles[t]
            if n < latent:
                part = mm(pl_ref[a, rows[b], n * KCOLS:(n + 1) * KCOLS], v1_scr[a, keys(b, n), :])
            else:
                part = mm(pc_ref[a, rows[b], :], cv1_scr[a])
            pv[t] = part if pv[t] is None else pv[t] + part
            if n == latent:
                finish(t)

        def finish(t):
            b, a = tiles[t]
            r = jnp.where(heads[a], pltpu.roll(1.0 / pv[t], HDIM, 1), 0.0)
            done[t] = (pv[t] * r, r)
            if a == 1:
                o, rden = (lo + hi for lo, hi in zip(done[t - 1], done[t]))
                ob_ref[rows[b], :] = o * jax.nn.silu(bg_ref[rows[b], :])
                o_ref[rows[b], :] = o
                rden_ref[rows[b], :] = rden

        pieces = range(latent + 1)
        for n in pieces:
            score_piece(0, n)
        for t in range(len(tiles)):
            matmuls = []
            for n in pieces:
                if t + 1 < len(tiles):
                    matmuls.append(functools.partial(score_piece, t + 1, n))
                if t > 0:
                    matmuls.append(functools.partial(value_piece, t - 1, n))
            _emit_interleaved([functools.partial(softmax_rows, t, r) for r in range(QBLK // SOFTMAX_ROWS)], matmuls)
        for n in pieces:
            value_piece(len(tiles) - 1, n)

    qblk = pl.BlockSpec((ATTN_ROWS, 128), lambda p, i: (i, p))
    return pl.pallas_call(
        kern, name="attn_fwd", grid=(NPAIR, ATTN_STEPS),
        in_specs=_attn_in_specs() + _bias_specs() + [_row(128), _row(128)], out_specs=[qblk] * 3 + _prob_specs(),
        out_shape=[jax.ShapeDtypeStruct((SEQ, 512), F32)] * 3
        + [jax.ShapeDtypeStruct((HEADS, SEQ, KBLK), BF16), jax.ShapeDtypeStruct((HEADS, SEQ, CTX), BF16)],
        scratch_shapes=[pltpu.VMEM((SEQ, 128), BF16), pltpu.VMEM((CTX, 128), BF16),
                        pltpu.VMEM((2, SEQ, 128), BF16), pltpu.VMEM((2, CTX, 128), BF16),
                        pltpu.VMEM((4, QBLK, KBLK + CTX), F32)],
        compiler_params=_cparams(("arbitrary", "arbitrary"), 40 * 1024 * 1024),
    )(z, z, z, z, zc, zc, bias, bias, qg2, kg2)


def attn_bwd(z, zc, qg2, kg2, dcat, saved):
    def kern(q_ref, k_ref, v_ref, bg_ref, ck_ref, cv_ref, qg_ref, kg_ref, do_ref, o_ref, rden_ref, pl_ref, pc_ref,
             dq_ref, dk_ref, dv_ref, dbg_ref, dck_ref, dcv_ref, db_ref, dqg_ref, dkg_ref,
             kn_scr, ckn_scr, v_scr, cv_scr, dknt_scr, dvt_scr, dcknt_scr, dcvt_scr, dp_scr, ds_scr):
        p, i = pl.program_id(0), pl.program_id(1)
        last = i == ATTN_STEPS - 1

        @pl.when(i == 0)
        def _():
            _norm_keys(k_ref, ck_ref, kg_ref, kn_scr, ckn_scr)

            def body(c, carry):
                sl = pl.ds(pl.multiple_of(c * NORM_ROWS, NORM_ROWS), NORM_ROWS)
                v_scr[sl, :] = v_ref[sl, :].astype(BF16)
                return carry

            lax.fori_loop(0, SEQ // NORM_ROWS, body, 0)
            cv_scr[...] = cv_ref[...].astype(BF16)
            for acc in (dknt_scr, dvt_scr, dcknt_scr, dcvt_scr, db_ref):
                acc[...] = jnp.zeros_like(acc)

        @pl.when((i == 0) & (p == 0))
        def _():
            dqg_ref[...] = jnp.zeros_like(dqg_ref)
            dkg_ref[...] = jnp.zeros_like(dkg_ref)

        heads = _head_lanes()
        tiles = [(b, a) for b in range(2) for a in range(2)]
        rows = [slice(b * QBLK, (b + 1) * QBLK) for b in range(2)]
        kb = [_kblock(2 * i + b) for b in range(2)]
        variant = [jnp.where(i == 0, 0, 1), jnp.where(last, 2, 1)]
        latent = KBLK // KCOLS

        def keys(b, n):
            return pl.ds(pl.multiple_of((kb[b] + n) * KCOLS, KCOLS), KCOLS)

        gated = []
        for b in range(2):
            bg, dout, o = bg_ref[rows[b], :], do_ref[rows[b], :], o_ref[rows[b], :]
            sig = jax.nn.sigmoid(bg)
            do = dout * (bg * sig)
            dbg_ref[rows[b], :] = (dout * o * (sig * (1.0 + bg * (1.0 - sig)))).astype(BF16)
            rden = rden_ref[rows[b], :]
            dr = do * rden
            qn = _scaled_q(q_ref[rows[b], :], qg_ref[...])
            gated.append((dr, dr.T.astype(BF16), qn.T.astype(BF16), do * o * rden))

        feats = [slice(a * HDIM, (a + 1) * HDIM) for a in range(2)]
        doa, doa_t, qa_t, delta = [], [], [], []
        for b, a in tiles:
            dr, dr_t, qn_t, weighted = gated[b]
            doa.append(jnp.where(heads[a], dr, 0.0).astype(BF16))
            doa_t.append(dr_t[feats[a], :])
            qa_t.append(qn_t[feats[a], :])
            delta.append(jnp.sum(jnp.where(heads[a], weighted, 0.0), axis=-1, keepdims=True))
        dqn = [None] * len(tiles)

        def cols(n):
            return slice(n * KCOLS, (n + 1) * KCOLS)

        def stage_a(t, n):
            b, a = tiles[t]
            if n < latent:
                dp_scr[t, :, cols(n)] = mm_nt(doa[t], v_scr[keys(b, n), :])
                dvt_scr[kb[b] + n, feats[a], :] += mm(doa_t[t], pl_ref[a, rows[b], cols(n)])
            else:
                dp_scr[t, :, cols(n)] = mm_nt(doa[t], cv_scr[...])
                dcvt_scr[feats[a], :] += mm(doa_t[t], pc_ref[a, rows[b], :])

        def stage_b(t, r):
            b, a = tiles[t]
            rs = slice(r * SOFTMAX_ROWS, (r + 1) * SOFTMAX_ROWS)
            in_rows = slice(b * QBLK + rs.start, b * QBLK + rs.stop)
            d = dp_scr[t, rs, :] - delta[t][rs, :]
            ds_lat = pl_ref[a, in_rows, :].astype(F32) * d[:, :KBLK]
            ds_ctx = pc_ref[a, in_rows, :].astype(F32) * d[:, KBLK:]
            db_ref[variant[b], a, rs, :] += ds_lat
            ds_scr[t, rs, :KBLK] = ds_lat.astype(BF16)
            ds_scr[t, rs, KBLK:] = ds_ctx.astype(BF16)

        def stage_c(t, n):
            b, a = tiles[t]
            ds = ds_scr[t, :, cols(n)]
            if n < latent:
                part = mm(ds, kn_scr[keys(b, n), :])
                dknt_scr[kb[b] + n, feats[a], :] += mm(qa_t[t], ds)
            else:
                part = mm(ds, ckn_scr[...])
                dcknt_scr[feats[a], :] += mm(qa_t[t], ds)
            dqn[t] = part if dqn[t] is None else dqn[t] + part
            if n == latent and a == 1:
                both = jnp.where(heads[0], dqn[t - 1], 0.0) + jnp.where(heads[1], dqn[t], 0.0)
                dq, dqg = jax.vjp(_scaled_q, q_ref[rows[b], :], qg_ref[...])[1](both)
                dq_ref[rows[b], :] = dq.astype(BF16)
                dqg_ref[...] += dqg

        pieces = range(latent + 1)
        for n in pieces:
            stage_a(0, n)
        for t in range(len(tiles)):
            matmuls = []
            for n in pieces:
                if t + 1 < len(tiles):
                    matmuls.append(functools.partial(stage_a, t + 1, n))
                if t > 0:
                    matmuls.append(functools.partial(stage_c, t - 1, n))
            _emit_interleaved([functools.partial(stage_b, t, r) for r in range(QBLK // SOFTMAX_ROWS)], matmuls)
        for n in pieces:
            stage_c(len(tiles) - 1, n)

        @pl.when(last)
        def _():
            def body(c, dkg):
                sl = pl.ds(pl.multiple_of(c * NORM_ROWS, NORM_ROWS), NORM_ROWS)
                blocks = range(NORM_ROWS // KCOLS)
                dkn = jnp.concatenate([dknt_scr[c * len(blocks) + n].T for n in blocks], axis=0)
                dv = jnp.concatenate([dvt_scr[c * len(blocks) + n].T for n in blocks], axis=0)
                _, nvjp = jax.vjp(_pair_rms, k_ref[sl, :], kg_ref[...])
                dk, dg = nvjp(dkn)
                dk_ref[sl, :] = dk.astype(BF16)
                dv_ref[sl, :] = dv.astype(BF16)
                return dkg + dg

            dkg = lax.fori_loop(0, SEQ // NORM_ROWS, body, jnp.zeros((1, 128), F32))
            _, nvjp = jax.vjp(_pair_rms, ck_ref[...], kg_ref[...])
            dck, dg = nvjp(dcknt_scr[...].T)
            dck_ref[...] = dck
            dcv_ref[...] = dcvt_scr[...].T
            dkg_ref[...] += dkg + dg

        @pl.when(last & (p == NPAIR - 1))
        def _():
            dqg_ref[...] = dqg_ref[...] + pltpu.roll(dqg_ref[...], HDIM, 1)
            dkg_ref[...] = dkg_ref[...] + pltpu.roll(dkg_ref[...], HDIM, 1)

    blk = lambda rows: pl.BlockSpec((rows, 128), lambda p, i: (0, p))
    qblk = pl.BlockSpec((ATTN_ROWS, 128), lambda p, i: (i, p))
    return pl.pallas_call(
        kern, name="attn_bwd", grid=(NPAIR, ATTN_STEPS),
        in_specs=_attn_in_specs() + [_row(128), _row(128), pl.BlockSpec((ATTN_ROWS, 128), lambda p, i: (i, 4 + p)),
                                     qblk, qblk] + _prob_specs(),
        out_specs=[qblk, blk(SEQ), blk(SEQ), qblk, blk(CTX), blk(CTX),
                   pl.BlockSpec((3, 2, QBLK, KBLK), lambda p, i: (0, p, 0, 0)), _row(128), _row(128)],
        out_shape=[jax.ShapeDtypeStruct((SEQ, 512), BF16)] * 4 + [jax.ShapeDtypeStruct((CTX, 512), F32)] * 2
        + [jax.ShapeDtypeStruct((3, HEADS, QBLK, KBLK), F32)]
        + [jax.ShapeDtypeStruct((1, 128), F32), jax.ShapeDtypeStruct((1, 128), F32)],
        scratch_shapes=[pltpu.VMEM((SEQ, 128), BF16), pltpu.VMEM((CTX, 128), BF16),
                        pltpu.VMEM((SEQ, 128), BF16), pltpu.VMEM((CTX, 128), BF16),
                        pltpu.VMEM((SEQ // KCOLS, 128, KCOLS), F32), pltpu.VMEM((SEQ // KCOLS, 128, KCOLS), F32),
                        pltpu.VMEM((128, CTX), F32), pltpu.VMEM((128, CTX), F32),
                        pltpu.VMEM((4, QBLK, KBLK + CTX), F32), pltpu.VMEM((4, QBLK, KBLK + CTX), BF16)],
        compiler_params=_cparams(("arbitrary", "arbitrary"), VMEM_BIG),
    )(z, z, z, z, zc, zc, qg2, kg2, dcat, *saved)


def outproj(out_a, out_b, x, target, gate, wo):
    tl = 512

    def kern(a_ref, b_ref, x_ref, t_ref, g_ref, w_ref, loss_ref, dy_ref, dcat_ref, dg_ref, dw_ref):
        @pl.when(pl.program_id(0) == 0)
        def _():
            loss_ref[...] = jnp.zeros_like(loss_ref)
            dg_ref[...] = jnp.zeros_like(dg_ref)
            dw_ref[...] = jnp.zeros_like(dw_ref)

        a, b = a_ref[...].astype(BF16), b_ref[...].astype(BF16)
        mix = (jnp.dot(a, w_ref[0:512, :], preferred_element_type=F32)
               + jnp.dot(b, w_ref[512:1024, :], preferred_element_type=F32))
        err = x_ref[...] + g_ref[...] * mix - t_ref[...]
        loss_ref[...] += 0.5 * jnp.sum(jnp.mean(err * err, axis=-1))
        dy = err * (1.0 / DM)
        dy_ref[...] = dy
        dg_ref[...] += jnp.sum(dy * mix, axis=0, keepdims=True)
        dmix = (g_ref[...] * dy).astype(BF16)
        dcat_ref[...] = lax.dot_general(dmix, w_ref[...], (((1,), (1,)), ((), ())), preferred_element_type=F32)
        dw_ref[0:512, :] += lax.dot_general(a, dmix, (((0,), (0,)), ((), ())), preferred_element_type=F32)
        dw_ref[512:1024, :] += lax.dot_general(b, dmix, (((0,), (0,)), ((), ())), preferred_element_type=F32)

    tile = lambda w: pl.BlockSpec((tl, w), lambda t: (t, 0))
    whole = pl.BlockSpec((DM, DM), lambda t: (0, 0))
    return pl.pallas_call(
        kern, name="outproj", grid=(SEQ // tl,),
        in_specs=[tile(512), tile(512), tile(DM), tile(DM), _row(DM), whole],
        out_specs=[pl.BlockSpec((8, 128), lambda t: (0, 0)), tile(DM), tile(DM), _row(DM), whole],
        out_shape=[jax.ShapeDtypeStruct((8, 128), F32), jax.ShapeDtypeStruct((SEQ, DM), F32),
                   jax.ShapeDtypeStruct((SEQ, DM), F32), jax.ShapeDtypeStruct((1, DM), F32),
                   jax.ShapeDtypeStruct((DM, DM), F32)],
        compiler_params=_cparams(("arbitrary",), 48 * 1024 * 1024),
    )(out_a, out_b, x, target, gate, wo)


def _pieces(sources):
    out = []
    for name, c0, c1 in sources:
        for j in range(NCHIP):
            lo, hi = max(c0, j * SHARD_IN), min(c1, (j + 1) * SHARD_IN)
            if lo < hi:
                out.append((j, lo - j * SHARD_IN, hi - j * SHARD_IN, name, lo - c0, hi - c0))
    return out


DZ_PIECES = _pieces((("a", 0, 1536), ("q", 1536, 2048), ("k", 2048, 2560), ("v", 2560, 3072), ("g", 3072, DIN)))
DZC_PIECES = _pieces((("k", 2048, 2560), ("v", 2560, 3072)))
_NT = (((1,), (1,)), ((), ()))


DH_SUBTILES = 2


def _dz_specs(tl):
    return [pl.BlockSpec((tl, 1536), lambda t: (t, 0))] + [pl.BlockSpec((tl, 512), lambda t: (t, 0))] * 4


def dh_bwd(dz_parts, w_full, x, dy, shift, scale, norm_g, dg_ctx):
    tl = 512
    nt = SEQ // tl

    def kern(a_ref, q_ref, k_ref, v_ref, g_ref, w_ref, x_ref, dy_ref, sh_ref, sc_ref, gn_ref, dgc_ref,
             gx_ref, dsh_ref, dsc_ref, dg_ref):
        @pl.when(pl.program_id(0) == 0)
        def _():
            dsh_ref[...] = jnp.zeros_like(dsh_ref)
            dsc_ref[...] = jnp.zeros_like(dsc_ref)
            dg_ref[...] = dgc_ref[...]

        src = dict(a=a_ref, q=q_ref, k=k_ref, v=v_ref, g=g_ref)
        for sub in range(DH_SUBTILES):
            rows = slice(sub * tl // DH_SUBTILES, (sub + 1) * tl // DH_SUBTILES)
            dh = None
            for j, l0, l1, name, s0, s1 in DZ_PIECES:
                part = lax.dot_general(src[name][rows, s0:s1], w_ref[j, :, l0:l1], _NT, preferred_element_type=F32)
                dh = part if dh is None else dh + part
            _, vjp = jax.vjp(_modulated, x_ref[rows, :], gn_ref[...], sc_ref[...], sh_ref[...])
            dx, dg, dsc, dsh = vjp(dh)
            gx_ref[rows, :] = dy_ref[rows, :] + dx
            dg_ref[...] += dg
            dsc_ref[...] += dsc
            dsh_ref[...] += dsh

    tile = pl.BlockSpec((tl, DM), lambda t: (t, 0))
    return pl.pallas_call(
        kern, name="dh_bwd", grid=(nt,),
        in_specs=_dz_specs(tl) + [pl.BlockSpec((NCHIP, DM, SHARD_IN), lambda t: (0, 0, 0)), tile, tile, _row(DM),
                                  _row(DM), _row(DM), _row(DM)],
        out_specs=[tile, _row(DM), _row(DM), _row(DM)],
        out_shape=[jax.ShapeDtypeStruct((SEQ, DM), F32)] + [jax.ShapeDtypeStruct((1, DM), F32)] * 3,
        compiler_params=_cparams(("arbitrary",), 48 * 1024 * 1024),
    )(*dz_parts, w_full, x, dy, shift, scale, norm_g, dg_ctx)


def dw_bwd(h, dz_parts, hc, dck, dcv, g_out):
    tl = 512
    nt = SEQ // tl
    (rhi, wi), (rho, wo) = RS_SHAPES

    def kern(h_ref, a_ref, q_ref, k_ref, v_ref, g_ref, hc_ref, dck_ref, dcv_ref, go_hbm,
             wire_i, keep_i, wire_o, keep_o, acc, rcv_i, mine_o, rcv_o, load_sem, send_sems, recv_sems):
        t = pl.program_id(0)
        x, y, c = _me()
        k = 2 * x + y
        sib = _flip(1)
        half = lambda hh, rh: pl.ds(pl.multiple_of(hh * rh, rh), rh)
        load_o = pltpu.make_async_copy(go_hbm.at[:, half(c, rho), :], mine_o, load_sem)
        pair_o = _rcopy(go_hbm.at[:, half(1 - c, rho), :], rcv_o, send_sems, recv_sems, 0, sib)
        pair_i = _rcopy(acc.at[:, half(1 - c, rhi), :], rcv_i, send_sems, recv_sems, 1, sib)

        @pl.when(t == 0)
        def _():
            load_o.start()
            pair_o.start()
            acc[...] = jnp.zeros_like(acc)
            hct = hc_ref[...].T
            csrc = dict(k=dck_ref, v=dcv_ref)
            for j, l0, l1, name, s0, s1 in DZC_PIECES:
                acc[j, :, l0:l1] += jnp.dot(hct, csrc[name][:, s0:s1].astype(BF16), preferred_element_type=F32)

        ht = h_ref[...].T
        src = dict(a=a_ref, q=q_ref, k=k_ref, v=v_ref, g=g_ref)
        for j, l0, l1, name, s0, s1 in DZ_PIECES:
            acc[j, :, l0:l1] += jnp.dot(ht, src[name][:, s0:s1], preferred_element_type=F32)

        @pl.when(t == nt - 1)
        def _():
            pair_i.start()
            load_o.wait()
            pair_o.wait_recv()
            for j in range(NCHIP):
                wire_o[j] = (mine_o[j] + rcv_o[j]).astype(BF16)
            keep_o[...] = mine_o[k] + rcv_o[k]
            pair_i.wait_recv()
            mine = half(c, rhi)
            for j in range(NCHIP):
                wire_i[j] = (acc[j, mine, :] + rcv_i[j]).astype(BF16)
            keep_i[...] = acc[k, mine, :] + rcv_i[k]
            pair_o.wait_send()
            pair_i.wait_send()

    whole = lambda *shape: pl.BlockSpec(shape, lambda t: (0,) * len(shape))
    return pl.pallas_call(
        kern, name="dw_bwd", grid=(nt,),
        in_specs=[pl.BlockSpec((tl, DM), lambda t: (t, 0))] + _dz_specs(tl)
        + [whole(CTX, DM), whole(CTX, 512), whole(CTX, 512), pl.BlockSpec(memory_space=pl.ANY)],
        out_specs=[whole(NCHIP, rhi, wi), whole(rhi, wi), whole(NCHIP, rho, wo), whole(rho, wo)],
        out_shape=[jax.ShapeDtypeStruct((NCHIP, rhi, wi), BF16), jax.ShapeDtypeStruct((rhi, wi), F32),
                   jax.ShapeDtypeStruct((NCHIP, rho, wo), BF16), jax.ShapeDtypeStruct((rho, wo), F32)],
        scratch_shapes=[pltpu.VMEM((NCHIP, DM, SHARD_IN), F32), pltpu.VMEM((NCHIP, rhi, wi), F32),
                        pltpu.VMEM((NCHIP, rho, wo), F32), pltpu.VMEM((NCHIP, rho, wo), F32),
                        pltpu.SemaphoreType.DMA(()), pltpu.SemaphoreType.DMA((2,)), pltpu.SemaphoreType.DMA((2,))],
        compiler_params=_cparams(("arbitrary",), VMEM_BIG),
    )(h, *dz_parts, hc, dck, dcv, g_out)


def ctx_bwd(dck, dcv, w_full, ctx, cshift, cscale, norm_g):
    def kern(dck_ref, dcv_ref, w_ref, c_ref, sh_ref, sc_ref, g_ref, dsh_ref, dsc_ref, dg_ref):
        csrc = dict(k=dck_ref, v=dcv_ref)
        dhc = None
        for j, l0, l1, name, s0, s1 in DZC_PIECES:
            part = lax.dot_general(csrc[name][:, s0:s1].astype(BF16), w_ref[j, :, l0:l1], _NT,
                                   preferred_element_type=F32)
            dhc = part if dhc is None else dhc + part
        _, vjp = jax.vjp(lambda g, sc, sh: _modulated(c_ref[...], g, sc, sh), g_ref[...], sc_ref[...], sh_ref[...])
        dg_ref[...], dsc_ref[...], dsh_ref[...] = vjp(dhc)

    whole = lambda r, c: pl.BlockSpec((r, c), lambda i: (0, 0))
    return pl.pallas_call(
        kern, name="ctx_bwd", grid=(1,),
        in_specs=[whole(CTX, 512), whole(CTX, 512), pl.BlockSpec((NCHIP, DM, SHARD_IN), lambda i: (0, 0, 0)),
                  whole(CTX, DM), _row(DM), _row(DM), _row(DM)],
        out_specs=[_row(DM), _row(DM), _row(DM)],
        out_shape=[jax.ShapeDtypeStruct((1, DM), F32)] * 3,
        compiler_params=_cparams(("arbitrary",), 40 * 1024 * 1024),
    )(dck, dcv, w_full, ctx, cshift, cscale, norm_g)


def _lane_pad_rpb(rpb):
    r = jnp.pad(rpb, ((0, 0), (0, 0), (0, GRID_W - rpb.shape[-1])))
    return jnp.concatenate([r, r], axis=-1)


def local_step(chip, dev, x, c_vec, c_ctx, w_ada, b_shard, ctx, target, norm_g, sgu_g, w_s, b_s, q_g, k_g, rpb,
               w_in_shard, w_out_shard):
    bsb = jnp.broadcast_to(b_s[:, :, None], (4, 128, 128))
    qg2, kg2 = jnp.tile(q_g, (1, 2)), jnp.tile(k_g, (1, 2))

    z, h, w_in_full, w_out_full, mod_all, cs = inproj_fwd(chip, x, c_vec, c_ctx, w_ada, b_shard, norm_g, w_in_shard,
                                                          w_out_shard)
    mods = mod_all.transpose(1, 0, 2).reshape(CS_ROWS, 3 * DM)
    mod = lax.dynamic_slice(mods, (8 * dev, 0), (1, 3 * DM))
    shift, scale, gate = mod[:, :DM], mod[:, DM:2 * DM], mod[:, 2 * DM:]
    cshift, cscale = mods[8 * NDEV:8 * NDEV + 1, :DM], mods[8 * NDEV:8 * NDEV + 1, DM:2 * DM]
    zc, hc = ctx_fwd(ctx, cshift, cscale, norm_g, w_in_full)
    bias = rpb_tables(_lane_pad_rpb(rpb))
    out_a = sgu_fwd(z, sgu_g, w_s, bsb)
    out_b, *saved = attn_fwd(z, zc, bias, qg2, kg2)
    loss8, dy, dcat, dgate, dwo = outproj(out_a, out_b, x, target, gate, w_out_full.reshape(DM, DM))
    dz_a, dsg, dws, dbsb = sgu_bwd(z, sgu_g, w_s, bsb, dcat)
    dq, dk, dv, dbg, dck, dcv, dbias, dqg2, dkg2 = attn_bwd(z, zc, qg2, kg2, dcat, saved)
    drpb = rpb_bwd(dbias)[:, :, :rpb.shape[-1]]
    dz_parts = (dz_a, dq, dk, dv, dbg)
    dcshift, dcscale, dng_c = ctx_bwd(dck, dcv, w_in_full, ctx, cshift, cscale, norm_g)
    wire_i, keep_i, wire_o, keep_o = dw_bwd(h, dz_parts, hc, dck, dcv, dwo.reshape(NCHIP, SHARD_OUT, DM))
    *in_flight, token = rs_start(wire_i, wire_o)
    grad_x, dshift, dscale, dng = dh_bwd(dz_parts, w_in_full, x, dy, shift, scale, norm_g, dng_c + token[0, 0])
    got_i, got_o = rs_wait(*in_flight, dshift)
    return dict(
        loss=loss8[0:1, 0:1], grad_x=grad_x, rs=(keep_i, got_i, keep_o, got_o), cs=cs,
        dmod=jnp.concatenate([dshift, dscale, dgate], axis=-1),
        dcmod=jnp.concatenate([dcshift, dcscale, jnp.zeros((1, DM), F32)], axis=-1),
        d_norm_g=dng, d_sgu_g=dsg, d_w_s=dws, d_b_s=dbsb[:, :, 0],
        d_q_g=dqg2[:, :HDIM], d_k_g=dkg2[:, :HDIM], d_rpb=drpb)


def _me():
    return lax.axis_index("x"), lax.axis_index("y"), lax.axis_index("c")


def _flip(q):
    x, y, c = _me()
    return ((1 - x) if q & 4 else x, (1 - y) if q & 2 else y, (1 - c) if q & 1 else c)


def _chip_of(dev):
    return 2 * dev[0] + dev[1]


def _rcopy(src, dst, send_sems, recv_sems, k, dev):
    return pltpu.make_async_remote_copy(src_ref=src, dst_ref=dst, send_sem=send_sems.at[k], recv_sem=recv_sems.at[k],
                                        device_id=dev, device_id_type=MESH_ID)


_VMEM_SPEC = pl.BlockSpec(memory_space=pltpu.VMEM)
SLAB_ROWS = 80


RS_SHAPES = ((DM // 2, SHARD_IN), (SHARD_OUT // 2, DM))
_HBM_SPEC = pl.BlockSpec(memory_space=pltpu.HBM)
_SEM_SPEC = pl.BlockSpec(memory_space=pltpu.SEMAPHORE)
_IN_FLIGHT = pltpu.SideEffectType.DATAFLOW_SIDE_EFFECTING


def _rs_copies(wires, lands, send_sems, recv_sems):
    return [pltpu.make_async_remote_copy(
        src_ref=wires[n].at[_chip_of(_flip(q))], dst_ref=lands[n].at[q // 2 - 1],
        send_sem=send_sems.at[3 * n + q // 2 - 1], recv_sem=recv_sems.at[3 * n + q // 2 - 1],
        device_id=_flip(q), device_id_type=MESH_ID) for n in (0, 1) for q in (2, 4, 6)]


def rs_start(wire_i, wire_o):
    lands = [lax.empty((NCHIP - 1, rh, w), BF16) for rh, w in RS_SHAPES]

    def body(wi_ref, wo_ref, li_ref, lo_ref, send_sems, recv_sems, wi_thru, wo_thru, li_thru, lo_thru, token):
        for cp in _rs_copies((wi_ref, wo_ref), (li_ref, lo_ref), send_sems, recv_sems):
            cp.start()
        token[...] = jnp.zeros_like(token)

    hbm = lambda a: pltpu.HBM(a.shape, a.dtype)
    return pl.pallas_call(
        body, name="rs_start",
        out_shape=(pltpu.SemaphoreType.DMA((6,)), pltpu.SemaphoreType.DMA((6,)), hbm(wire_i), hbm(wire_o),
                   hbm(lands[0]), hbm(lands[1]), jax.ShapeDtypeStruct((8, 128), F32)),
        in_specs=(_HBM_SPEC,) * 4, out_specs=(_SEM_SPEC, _SEM_SPEC) + (_HBM_SPEC,) * 4 + (_VMEM_SPEC,),
        input_output_aliases={0: 2, 1: 3, 2: 4, 3: 5},
        compiler_params=pltpu.CompilerParams(has_side_effects=_IN_FLIGHT),
    )(*[pltpu.with_memory_space_constraint(a, pltpu.HBM) for a in (wire_i, wire_o, *lands)])


def rs_wait(send_sems, recv_sems, wire_i, wire_o, land_i, land_o, after):
    def body(wi_ref, wo_ref, li_ref, lo_ref, send_sems, recv_sems, after_ref, wi_dead, wo_dead, gi_ref, go_ref):
        for cp in _rs_copies((wi_ref, wo_ref), (li_ref, lo_ref), send_sems, recv_sems):
            cp.wait_send()
            cp.wait_recv()

    hbm = lambda a: pltpu.HBM(a.shape, a.dtype)
    return pl.pallas_call(
        body, name="rs_wait", out_shape=(hbm(wire_i), hbm(wire_o), hbm(land_i), hbm(land_o)),
        in_specs=(_HBM_SPEC,) * 4 + (_SEM_SPEC, _SEM_SPEC, pl.BlockSpec(memory_space=pl.ANY)),
        out_specs=(_HBM_SPEC,) * 4, input_output_aliases={0: 0, 1: 1, 2: 2, 3: 3},
        compiler_params=pltpu.CompilerParams(has_side_effects=_IN_FLIGHT),
    )(wire_i, wire_o, land_i, land_o, send_sems, recv_sems, after)[2:]


def final_reduce(keep_i, got_i, keep_o, got_o, slab):
    def kern(ki_ref, gi_ref, ko_ref, go_ref, s_ref, gin_ref, gout_ref, all_ref, tot_ref, send_sems, recv_sems):
        x, y, c = _me()
        sib = _flip(1)
        dev = lambda d: 4 * d[0] + 2 * d[1] + d[2]
        me = dev((x, y, c))

        def slab_copy(idx, owner, to):
            return _rcopy(all_ref.at[dev(owner)], all_ref.at[dev(owner)], send_sems, recv_sems, idx, to)

        all_ref[me] = s_ref[...]
        first = [slab_copy(0, (x, y, c), sib)] + [slab_copy(q // 2, (x, y, c), _flip(q)) for q in (2, 4, 6)]
        for cp in first:
            cp.start()

        shares = []
        for n, (keep, got, out) in enumerate(((ki_ref, gi_ref, gin_ref), (ko_ref, go_ref, gout_ref))):
            rh = RS_SHAPES[n][0]
            half = lambda hh, rh=rh: pl.ds(pl.multiple_of(hh * rh, rh), rh)
            out[half(c), :] = ((keep[...] + got[0].astype(F32)) + got[1].astype(F32)) + got[2].astype(F32)
            share = _rcopy(out.at[half(c), :], out.at[half(c), :], send_sems, recv_sems, 7 + n, sib)
            share.start()
            shares.append((share, _rcopy(out.at[half(1 - c), :], out.at[half(1 - c), :], send_sems, recv_sems, 7 + n,
                                         sib)))

        passed = []
        for q in (2, 4, 6):
            slab_copy(q // 2, _flip(q), (x, y, c)).wait_recv()
            cp = slab_copy(3 + q // 2, _flip(q), sib)
            cp.start()
            passed.append(cp)
        slab_copy(0, sib, (x, y, c)).wait_recv()
        for q in (2, 4, 6):
            slab_copy(3 + q // 2, _flip(q | 1), (x, y, c)).wait_recv()
        tot = all_ref[0]
        for d in range(1, NDEV):
            tot = tot + all_ref[d]
        tot_ref[...] = tot
        for share, arrival in shares:
            arrival.wait_recv()
            share.wait_send()
        for cp in first + passed:
            cp.wait_send()

    (rhi, wi), (rho, wo) = RS_SHAPES
    return pl.pallas_call(
        kern, name="final_reduce", in_specs=[_VMEM_SPEC] * 5, out_specs=[_VMEM_SPEC] * 4,
        out_shape=[jax.ShapeDtypeStruct((2 * rhi, wi), F32), jax.ShapeDtypeStruct((2 * rho, wo), F32),
                   jax.ShapeDtypeStruct((NDEV, SLAB_ROWS, DM), F32), jax.ShapeDtypeStruct((SLAB_ROWS, DM), F32)],
        scratch_shapes=[pltpu.SemaphoreType.DMA((9,)), pltpu.SemaphoreType.DMA((9,))],
        compiler_params=pltpu.CompilerParams(vmem_limit_bytes=40 * 1024 * 1024),
    )(keep_i, got_i, keep_o, got_o, slab)


def ada_bwd(a_in, dm, dm_shard, w_ada, c_ctx):
    def kern(a_ref, dm_ref, dms_ref, w_ref, cc_ref, dw_ref, db_ref, dcc_ref, parts, send_sems, recv_sems):
        x, y, c = _me()
        k = 2 * x + y
        act = jax.nn.silu(a_ref[...]).astype(BF16)
        dms = dms_ref[...].astype(BF16)
        dw_ref[...] = lax.dot_general(act, dms, (((0,), (0,)), ((), ())), preferred_element_type=F32)
        db_ref[...] = jnp.sum(dm_ref[...], axis=0, keepdims=True)
        parts[k] = lax.dot_general(dms, w_ref[...].astype(BF16), (((1,), (1,)), ((), ())), preferred_element_type=F32)
        sends = [_rcopy(parts.at[k], parts.at[k], send_sems, recv_sems, q // 2 - 1, _flip(q)) for q in (2, 4, 6)]
        for cp in sends:
            cp.start()
        for q in (2, 4, 6):
            kq = _chip_of(_flip(q))
            _rcopy(parts.at[kq], parts.at[kq], send_sems, recv_sems, q // 2 - 1, _flip(q)).wait_recv()
        dact = ((parts[0] + parts[1]) + parts[2]) + parts[3]
        _, vjp = jax.vjp(jax.nn.silu, cc_ref[...])
        dcc_ref[...] = vjp(dact[8:9, :])[0]
        for cp in sends:
            cp.wait_send()

    return pl.pallas_call(
        kern, name="ada_bwd", in_specs=[_VMEM_SPEC] * 5, out_specs=[_VMEM_SPEC] * 3,
        out_shape=[jax.ShapeDtypeStruct((DM, SHARD_ADA), F32), jax.ShapeDtypeStruct((1, 3 * DM), F32),
                   jax.ShapeDtypeStruct((1, DM), F32)],
        scratch_shapes=[pltpu.VMEM((NCHIP, 16, DM), F32), pltpu.SemaphoreType.DMA((3,)), pltpu.SemaphoreType.DMA((3,))],
    )(a_in, dm, dm_shard, w_ada, c_ctx)


def _adamw_math(w, g, m, v):
    m = B1 * m + (1.0 - B1) * g
    v = B2 * v + (1.0 - B2) * (g * g)
    m_hat = m / (1.0 - B1 ** STEP)
    v_hat = v / (1.0 - B2 ** STEP)
    return -LR * (m_hat / (jnp.sqrt(v_hat) + ADAM_EPS) + WD * w), m, v


def adamw_big(w, g, m, v, name, block_rows=256):
    rows, width = w.shape

    def kern(w_ref, g_ref, m_ref, v_ref, d_ref, nm_ref, nv_ref):
        d_ref[...], nm_ref[...], nv_ref[...] = _adamw_math(w_ref[...], g_ref[...], m_ref[...], v_ref[...])

    spec = pl.BlockSpec((block_rows, width), lambda i: (i, 0))
    return pl.pallas_call(
        kern, name=name, grid=(rows // block_rows,), in_specs=[spec] * 4, out_specs=[spec] * 3,
        out_shape=[jax.ShapeDtypeStruct((rows, width), F32)] * 3,
        compiler_params=_cparams(("arbitrary",)),
    )(w, g, m, v)


def adamw_small(quads):
    n = len(quads)

    def kern(*refs):
        ins, outs = refs[:4 * n], refs[4 * n:]
        for i in range(n):
            w, g, m, v = (r[...] for r in ins[4 * i:4 * i + 4])
            outs[3 * i][...], outs[3 * i + 1][...], outs[3 * i + 2][...] = _adamw_math(w, g, m, v)

    flat = [a for quad in quads for a in quad]
    res = pl.pallas_call(
        kern, name="adamw_small", in_specs=[_VMEM_SPEC] * (4 * n), out_specs=[_VMEM_SPEC] * (3 * n),
        out_shape=[jax.ShapeDtypeStruct(q[0].shape, F32) for q in quads for _ in range(3)],
    )(*flat)
    return [tuple(res[3 * i:3 * i + 3]) for i in range(n)]


def _rows_of(a, rows):
    flat = a.reshape(-1)
    return jnp.pad(flat, (0, rows * DM - flat.shape[0])).reshape(rows, DM)


def kernel(x, c, ctx, c_ctx, w_ada, b_ada, norm_g, w_in, sgu_norm_g, w_spatial, b_spatial, q_norm_g, k_norm_g, rpb, w_out, loss_target, m_c_ctx, m_w_ada, m_b_ada, m_norm_g, m_w_in, m_sgu_norm_g, m_w_spatial, m_b_spatial, m_q_norm_g, m_k_norm_g, m_rpb, m_w_out, v_c_ctx, v_w_ada, v_b_ada, v_norm_g, v_w_in, v_sgu_norm_g, v_w_spatial, v_b_spatial, v_q_norm_g, v_k_norm_g, v_rpb, v_w_out):
    xi, yi, ci = lax.axis_index("x"), lax.axis_index("y"), lax.axis_index("c")
    chip, dev = 2 * xi + yi, 4 * xi + 2 * yi + ci
    c_ctx2 = c_ctx.reshape(1, DM)

    b_shard = lax.dynamic_slice(b_ada, (0, chip * SHARD_ADA), (1, SHARD_ADA))
    part = local_step(chip.reshape(1).astype(jnp.int32), dev, x[0], c, c_ctx2, w_ada[0], b_shard, ctx[0], loss_target[0],
                      norm_g, sgu_norm_g, w_spatial[0], b_spatial[0], q_norm_g, k_norm_g, rpb[0], w_in[0], w_out[0])
    cs = part["cs"]

    slab = jnp.concatenate([
        part["d_norm_g"], _rows_of(part["d_sgu_g"], 1), _rows_of(part["d_b_s"], 1),
        _rows_of(jnp.concatenate([part["d_q_g"], part["d_k_g"]], axis=-1), 1), _rows_of(part["d_rpb"], 4),
        _rows_of(part["loss"], 1), _rows_of(part["dcmod"], 3), _rows_of(part["dmod"], 3), jnp.zeros((1, DM), F32),
        _rows_of(part["d_w_s"], 64)], axis=0)
    g_w_in, g_w_out, gathered, tot = final_reduce(*part["rs"], slab)
    dm = jnp.concatenate([gathered[:, 12:15, :].reshape(NDEV, 3 * DM), tot[9:12].reshape(1, 3 * DM),
                          jnp.zeros((7, 3 * DM), F32)], axis=0)
    a_in = jnp.concatenate([cs[0:8 * NDEV:8], cs[8 * NDEV:8 * NDEV + 1], jnp.zeros((7, DM), F32)], axis=0)
    dm_shard = lax.dynamic_slice(dm, (0, chip * SHARD_ADA), (16, SHARD_ADA))
    g_w_ada, g_b_ada, g_c_ctx = ada_bwd(a_in, dm, dm_shard, w_ada[0], c_ctx2)

    loss = tot[8, 0]
    g_small = dict(
        c_ctx=g_c_ctx, b_ada=g_b_ada, norm_g=tot[0:1], sgu_norm_g=tot[1:2, :512], w_spatial=tot[16:80].reshape(512, 128),
        b_spatial=tot[2:3, :512].reshape(4, 128), q_norm_g=tot[3:4, :HDIM], k_norm_g=tot[3:4, HDIM:2 * HDIM],
        rpb=tot[4:8].reshape(-1)[:HEADS * 15 * 31].reshape(HEADS * 15, 31))
    shapes = dict(c_ctx=(DM,), w_ada=(1, DM, SHARD_ADA), b_ada=(1, 3 * DM), norm_g=(1, DM), w_in=(1, DM, SHARD_IN),
                  sgu_norm_g=(1, 512), w_spatial=(1, 4, 128, 128), b_spatial=(1, 4, 128), q_norm_g=(1, HDIM),
                  k_norm_g=(1, HDIM), rpb=(1, HEADS, 15, 31), w_out=(1, SHARD_OUT, DM))
    names = list(shapes)
    weights = dict(c_ctx=c_ctx, w_ada=w_ada, b_ada=b_ada, norm_g=norm_g, w_in=w_in, sgu_norm_g=sgu_norm_g,
                   w_spatial=w_spatial, b_spatial=b_spatial, q_norm_g=q_norm_g, k_norm_g=k_norm_g, rpb=rpb, w_out=w_out)
    m_in = dict(zip(names, (m_c_ctx, m_w_ada, m_b_ada, m_norm_g, m_w_in, m_sgu_norm_g, m_w_spatial, m_b_spatial,
                            m_q_norm_g, m_k_norm_g, m_rpb, m_w_out)))
    v_in = dict(zip(names, (v_c_ctx, v_w_ada, v_b_ada, v_norm_g, v_w_in, v_sgu_norm_g, v_w_spatial, v_b_spatial,
                            v_q_norm_g, v_k_norm_g, v_rpb, v_w_out)))
    grads = dict(g_small, w_ada=g_w_ada, w_in=g_w_in, w_out=g_w_out)
    upd = {}
    for n in ("w_ada", "w_in", "w_out"):
        g = grads[n]
        upd[n] = adamw_big(weights[n].reshape(g.shape), g, m_in[n].reshape(g.shape), v_in[n].reshape(g.shape),
                           "adamw_" + n)
    small = [n for n in names if n not in upd]
    res = adamw_small([(weights[n].reshape(grads[n].shape), grads[n], m_in[n].reshape(grads[n].shape),
                        v_in[n].reshape(grads[n].shape)) for n in small])
    upd.update(zip(small, res))
    out = [loss, part["grad_x"].reshape(1, SEQ, DM)]
    out += [grads[n].reshape(shapes[n]) for n in names]
    for slot in range(3):
        out += [upd[n][slot].reshape(shapes[n]) for n in names]
    return tuple(out)
```

```python
import functools

import jax
import jax.numpy as jnp
from jax import lax
from jax.experimental import pallas as pl
from jax.experimental.pallas import tpu as pltpu

F32, BF16 = jnp.float32, jnp.bfloat16
SEQ, DM, CTX, DIN = 4096, 1024, 256, 3584
NCHIP, NDEV = 4, 8
SHARD_IN = DIN // NCHIP
SHARD_ADA = 3 * DM // NCHIP
SHARD_OUT = DM // NCHIP
GRID_W = 64
QROWS = 4
KROWS = 12
QBLK, KBLK = QROWS * GRID_W, KROWS * GRID_W
NQBLK = SEQ // QBLK
HEADS, HDIM, NPAIR = 8, 64, 4
EPS = 1e-6
NEG_INF = -1e30
ZQ, ZK, ZV, ZG = 12, 16, 20, 24
LR, B1, B2, ADAM_EPS, WD, STEP = 0.001, 0.9, 0.999, 1e-08, 0.01, 10
VMEM_BIG = 56 * 1024 * 1024
MESH_ID = pl.DeviceIdType.MESH


def _dot(a, b, lhs_c, rhs_c):
    return lax.dot_general(a.astype(BF16), b.astype(BF16), (((lhs_c,), (rhs_c,)), ((), ())),
                           preferred_element_type=F32)


@jax.custom_vjp
def mm(a, b):
    return _dot(a, b, 1, 0)


@jax.custom_vjp
def mm_nt(a, b):
    return _dot(a, b, 1, 1)


@jax.custom_vjp
def mm_tn(a, b):
    return _dot(a, b, 0, 0)


mm.defvjp(lambda a, b: (mm(a, b), (a, b)), lambda r, ct: (mm_nt(ct, r[1]), mm_tn(r[0], ct)))
mm_nt.defvjp(lambda a, b: (mm_nt(a, b), (a, b)), lambda r, ct: (mm(ct, r[1]), mm_tn(ct, r[0])))
mm_tn.defvjp(lambda a, b: (mm_tn(a, b), (a, b)), lambda r, ct: (mm_nt(r[1], ct), mm(r[0], ct)))


def _rms(x, g):
    return x * lax.rsqrt(jnp.mean(x * x, axis=-1, keepdims=True) + EPS) * g


def _modulated(x, g, scale, shift):
    return _rms(x, g) * (1.0 + scale) + shift


def _pair_rms(x, g2):
    lo = lax.broadcasted_iota(jnp.int32, (1, 2 * HDIM), 1) < HDIM
    sq = x * x
    s_lo = jnp.sum(jnp.where(lo, sq, 0.0), axis=-1, keepdims=True)
    s_hi = jnp.sum(jnp.where(lo, 0.0, sq), axis=-1, keepdims=True)
    rs = jnp.where(lo, lax.rsqrt(s_lo / HDIM + EPS), lax.rsqrt(s_hi / HDIM + EPS))
    return x * rs * g2


def _cparams(sem, vmem=None):
    return pltpu.CompilerParams(dimension_semantics=sem, vmem_limit_bytes=vmem)


def _row(n):
    return pl.BlockSpec((1, n), lambda *_: (0, 0))


CS_ROWS = 8 * NDEV + 8


def _mod_part(mod_ref, row, part):
    pieces = []
    for j in range(NCHIP):
        lo, hi = max(part * DM, j * SHARD_ADA), min((part + 1) * DM, (j + 1) * SHARD_ADA)
        if lo < hi:
            pieces.append(mod_ref[j, row, lo - j * SHARD_ADA:hi - j * SHARD_ADA])
    return jnp.concatenate(pieces, axis=-1)


def inproj_fwd(chip, x, c_vec, c_ctx, w_ada, b_shard, norm_g, w_shard, wo_shard):
    tl = 1024
    nt = SEQ // tl
    halves = (DM // 2, SHARD_OUT // 2)
    n_w, n_c = 12, NDEV - 1

    def kern(k_ref, x_ref, cv_ref, cc_ref, wa_ref, b_ref, g_ref, w_ref, wo_ref,
             z_ref, h_ref, wfull_ref, wofull_ref, modall_ref, csall_ref,
             w_scr, wo_scr, h_scr, mine, cs_scr, mod_scr, shsc_scr, send_sems, recv_sems):
        s, t = pl.program_id(0), pl.program_id(1)
        xi, yi, c = _me()
        k, me = 2 * xi + yi, 4 * xi + 2 * yi + c
        sib = _flip(1)
        rows = pl.ds(pl.multiple_of(t * tl, tl), tl)
        gathered = (w_scr, wo_scr)
        slot = lambda d: pl.ds(pl.multiple_of(8 * d, 8), 8)

        def c_copy(q, owner):
            return _rcopy(mine, cs_scr.at[slot(owner), :], send_sems, recv_sems, n_w + q - 1, _flip(q))

        def m_copy(q, chip_of_block):
            return _rcopy(mod_scr.at[chip_of_block], mod_scr.at[chip_of_block], send_sems, recv_sems,
                          n_w + n_c + q // 2 - 1, _flip(q))

        def adaln():
            first = lax.broadcasted_iota(jnp.int32, (8, DM), 0) == 0
            mine[...] = jnp.where(first, jnp.broadcast_to(cv_ref[...], (8, DM)), 0.0)
            cs_scr[slot(me), :] = mine[...]
            cs_scr[slot(NDEV), :] = jnp.where(first, jnp.broadcast_to(cc_ref[...], (8, DM)), 0.0)
            for q in range(1, NDEV):
                c_copy(q, me).start()
            wa = wa_ref[...].astype(BF16)
            for q in range(1, NDEV):
                px, py, pc = _flip(q)
                c_copy(q, 4 * px + 2 * py + pc).wait_recv()
            act = jax.nn.silu(cs_scr[...]).astype(BF16)
            mod_scr[k] = jnp.dot(act, wa, preferred_element_type=F32) + b_ref[...]
            for q in (2, 4, 6):
                m_copy(q, k).start()
            for q in (2, 4, 6):
                m_copy(q, _chip_of(_flip(q))).wait_recv()
            row = pl.ds(8 * me, 1)
            shsc_scr[0:1, :] = _mod_part(mod_scr, row, 0)
            shsc_scr[1:2, :] = _mod_part(mod_scr, row, 1)
            pltpu.sync_copy(mod_scr, modall_ref)
            pltpu.sync_copy(cs_scr, csall_ref)

        def block(n, chip_of_block, hh):
            return gathered[n].at[chip_of_block, pl.ds(pl.multiple_of(hh * halves[n], halves[n]), halves[n]), :]

        def ici(n, q, chip_of_block):
            blk = block(n, chip_of_block, c)
            return _rcopy(blk, blk, send_sems, recv_sems, 6 * n + q // 2 - 1, _flip(q))

        def d2d(n, q, chip_of_block, hh):
            blk = block(n, chip_of_block, hh)
            return _rcopy(blk, blk, send_sems, recv_sems, 6 * n + 3 + q // 2 - 1, sib)

        @pl.when((s == 0) & (t == 0))
        def _():
            adaln()
            w_scr[k] = w_ref[...].astype(BF16)
            wo_scr[k] = wo_ref[...].astype(BF16)
            for q in (2, 4, 6):
                ici(0, q, k).start()
                ici(1, q, k).start()

        for sweep in (1, 2, 3):
            @pl.when((s == sweep) & (t == 0))
            def _():
                q = 2 * sweep
                src = _chip_of(_flip(q))
                for n in (0, 1):
                    ici(n, q, src).wait_recv()
                    d2d(n, q, src, c).start()
                for n in (0, 1):
                    d2d(n, q, src, 1 - c).wait_recv()

        @pl.when(s == 0)
        def _():
            hb = _modulated(x_ref[...], g_ref[...], shsc_scr[1:2, :], shsc_scr[0:1, :]).astype(BF16)
            h_scr[rows, :] = hb
            h_ref[...] = hb

        z_ref[...] = jnp.dot(h_scr[rows, :], w_scr[lax.bitwise_xor(k, s)], preferred_element_type=F32)

        @pl.when((s == NCHIP - 1) & (t == nt - 1))
        def _():
            for q in range(1, NDEV):
                c_copy(q, me).wait_send()
            for q in (2, 4, 6):
                m_copy(q, k).wait_send()
            for n in (0, 1):
                for q in (2, 4, 6):
                    ici(n, q, k).wait_send()
                    d2d(n, q, _chip_of(_flip(q)), c).wait_send()
            pltpu.sync_copy(w_scr, wfull_ref)
            pltpu.sync_copy(wo_scr, wofull_ref)

    once = lambda s, t, k: (jnp.where(s == 0, t, nt - 1), 0)
    hbm = pl.BlockSpec(memory_space=pl.ANY)
    n_sem = n_w + n_c + 3
    return pl.pallas_call(
        kern, name="inproj_fwd",
        grid_spec=pltpu.PrefetchScalarGridSpec(
            num_scalar_prefetch=1, grid=(NCHIP, nt),
            in_specs=[pl.BlockSpec((tl, DM), once)] + [_VMEM_SPEC] * 7,
            out_specs=[pl.BlockSpec((tl, SHARD_IN), lambda s, t, k: (t, lax.bitwise_xor(k[0], s))),
                       pl.BlockSpec((tl, DM), once), hbm, hbm, hbm, hbm],
            scratch_shapes=[pltpu.VMEM((NCHIP, DM, SHARD_IN), BF16), pltpu.VMEM((NCHIP, SHARD_OUT, DM), BF16),
                            pltpu.VMEM((SEQ, DM), BF16), pltpu.VMEM((8, DM), F32), pltpu.VMEM((CS_ROWS, DM), F32),
                            pltpu.VMEM((NCHIP, CS_ROWS, SHARD_ADA), F32), pltpu.VMEM((8, DM), F32),
                            pltpu.SemaphoreType.DMA((n_sem,)), pltpu.SemaphoreType.DMA((n_sem,))]),
        out_shape=[jax.ShapeDtypeStruct((SEQ, DIN), F32), jax.ShapeDtypeStruct((SEQ, DM), BF16),
                   jax.ShapeDtypeStruct((NCHIP, DM, SHARD_IN), BF16), jax.ShapeDtypeStruct((NCHIP, SHARD_OUT, DM), BF16),
                   jax.ShapeDtypeStruct((NCHIP, CS_ROWS, SHARD_ADA), F32), jax.ShapeDtypeStruct((CS_ROWS, DM), F32)],
        compiler_params=_cparams(("arbitrary", "arbitrary"), VMEM_BIG),
    )(chip, x, c_vec, c_ctx, w_ada, b_shard, norm_g, w_shard, wo_shard)


def ctx_fwd(ctx, cshift, cscale, norm_g, w_full):
    def kern(c_ref, sh_ref, sc_ref, g_ref, w2_ref, w3_ref, zc_ref, hc_ref):
        hc = _modulated(c_ref[...], g_ref[...], sc_ref[...], sh_ref[...]).astype(BF16)
        hc_ref[...] = hc
        zc_ref[:, :SHARD_IN] = jnp.dot(hc, w2_ref[0], preferred_element_type=F32)
        zc_ref[:, SHARD_IN:] = jnp.dot(hc, w3_ref[0], preferred_element_type=F32)

    return pl.pallas_call(
        kern, name="ctx_fwd", grid=(1,),
        in_specs=[pl.BlockSpec((CTX, DM), lambda i: (0, 0)), _row(DM), _row(DM), _row(DM),
                  pl.BlockSpec((1, DM, SHARD_IN), lambda i: (2, 0, 0)),
                  pl.BlockSpec((1, DM, SHARD_IN), lambda i: (3, 0, 0))],
        out_specs=[pl.BlockSpec((CTX, 2 * SHARD_IN), lambda i: (0, 0)),
                   pl.BlockSpec((CTX, DM), lambda i: (0, 0))],
        out_shape=[jax.ShapeDtypeStruct((CTX, 2 * SHARD_IN), F32), jax.ShapeDtypeStruct((CTX, DM), BF16)],
        compiler_params=_cparams(("arbitrary",)),
    )(ctx, cshift, cscale, norm_g, w_full, w_full)


SGU_CHUNK, SGU_PER_STEP = 128, 4


def _gelu(x):
    return 0.5 * x * (1.0 + lax.erf(x * 0.7071067811865476))


def _sgu_chunk(au, av, ag, sg, ws, bsb):
    u, v = _gelu(au), _gelu(av)
    outs = []
    for g in range(4):
        sl = slice(128 * g, 128 * (g + 1))
        mixed = mm(ws[g], _rms(v[:, sl], sg[:, sl])) + bsb[g]
        outs.append(u[:, sl] * mixed * jax.nn.silu(ag[:, sl]))
    return jnp.concatenate(outs, axis=-1)


def _sgu_specs():
    rows = SGU_CHUNK * SGU_PER_STEP
    zspec = lambda c: pl.BlockSpec((rows, 512), lambda n: (n, c))
    wspec = pl.BlockSpec((4, 128, 128), lambda n: (0, 0, 0))
    return rows, [zspec(0), zspec(1), zspec(2), _row(512), wspec, wspec]


def sgu_fwd(z, sg, ws, bsb):
    rows, in_specs = _sgu_specs()

    def kern(au_ref, av_ref, ag_ref, sg_ref, ws_ref, bs_ref, o_ref):
        for c in range(SGU_PER_STEP):
            sl = slice(c * SGU_CHUNK, (c + 1) * SGU_CHUNK)
            o_ref[sl, :] = _sgu_chunk(au_ref[sl, :], av_ref[sl, :], ag_ref[sl, :], sg_ref[...], ws_ref[...],
                                      bs_ref[...])

    return pl.pallas_call(
        kern, name="sgu_fwd", grid=(SEQ // rows,), in_specs=in_specs,
        out_specs=pl.BlockSpec((rows, 512), lambda n: (n, 0)),
        out_shape=jax.ShapeDtypeStruct((SEQ, 512), F32),
        compiler_params=_cparams(("arbitrary",)),
    )(z, z, z, sg, ws, bsb)


def sgu_bwd(z, sg, ws, bsb, dcat):
    rows, in_specs = _sgu_specs()

    def kern(au_ref, av_ref, ag_ref, sg_ref, ws_ref, bs_ref, do_ref, dz_ref, dsg_ref, dws_ref, dbs_ref):
        @pl.when(pl.program_id(0) == 0)
        def _():
            dsg_ref[...] = jnp.zeros_like(dsg_ref)
            dws_ref[...] = jnp.zeros_like(dws_ref)
            dbs_ref[...] = jnp.zeros_like(dbs_ref)

        for c in range(SGU_PER_STEP):
            sl = slice(c * SGU_CHUNK, (c + 1) * SGU_CHUNK)
            _, vjp = jax.vjp(_sgu_chunk, au_ref[sl, :], av_ref[sl, :], ag_ref[sl, :], sg_ref[...], ws_ref[...],
                             bs_ref[...])
            dau, dav, dag, dsg, dws, dbs = vjp(do_ref[sl, :])
            dz_ref[sl, 0:512] = dau.astype(BF16)
            dz_ref[sl, 512:1024] = dav.astype(BF16)
            dz_ref[sl, 1024:1536] = dag.astype(BF16)
            dsg_ref[...] += dsg
            dws_ref[...] += dws
            dbs_ref[...] += dbs

        @pl.when(pl.program_id(0) == pl.num_programs(0) - 1)
        def _():
            dbs_ref[...] = jnp.broadcast_to(jnp.sum(dbs_ref[...], axis=-1, keepdims=True), dbs_ref.shape)

    wspec = pl.BlockSpec((4, 128, 128), lambda n: (0, 0, 0))
    return pl.pallas_call(
        kern, name="sgu_bwd", grid=(SEQ // rows,),
        in_specs=in_specs + [pl.BlockSpec((rows, 512), lambda n: (n, 0))],
        out_specs=[pl.BlockSpec((rows, 1536), lambda n: (n, 0)), _row(512), wspec, wspec],
        out_shape=[jax.ShapeDtypeStruct((SEQ, 1536), BF16), jax.ShapeDtypeStruct((1, 512), F32),
                   jax.ShapeDtypeStruct((4, 128, 128), F32), jax.ShapeDtypeStruct((4, 128, 128), F32)],
        compiler_params=_cparams(("arbitrary",)),
    )(z, z, z, sg, ws, bsb, dcat)


_DR_OFF = (7, 3, -1)


def _row_valid(v, rr, j):
    return (j < 8, rr <= j < rr + 8, 4 <= j < 12)[v]


def _col_window():
    q = lax.broadcasted_iota(jnp.int32, (GRID_W, 128), 0)
    kc = lax.broadcasted_iota(jnp.int32, (GRID_W, 128), 1) % GRID_W
    c0 = jnp.clip(q - 8, 0, GRID_W - 16)
    return (kc >= c0) & (kc < c0 + 16)


def rpb_tables(rpb2):
    def kern(r_ref, b_ref):
        base = r_ref[0]
        lo = lax.broadcasted_iota(jnp.int32, (1, 128), 1) < GRID_W
        win = _col_window()
        tiles = {}
        for v in range(3):
            for rr in range(QROWS):
                for jp in range(KROWS // 2):
                    j0, j1 = 2 * jp, 2 * jp + 1
                    ok0, ok1 = _row_valid(v, rr, j0), _row_valid(v, rr, j1)
                    key = (j0 - rr + _DR_OFF[v], ok0, ok1) if (ok0 or ok1) else None
                    if key not in tiles:
                        if key is None:
                            tiles[key] = jnp.full((GRID_W, 128), NEG_INF, F32)
                        else:
                            d0 = key[0]
                            r0 = base[d0:d0 + 1, :] if ok0 else jnp.zeros((1, 128), F32)
                            r1 = base[d0 + 1:d0 + 2, :] if ok1 else jnp.zeros((1, 128), F32)
                            y = jnp.broadcast_to(jnp.where(lo, r0, r1), (GRID_W, 128))
                            y = pltpu.roll(pltpu.roll(y, 128 - 15, 1), 0, 1, stride=1, stride_axis=0)
                            tiles[key] = jnp.where(win & jnp.where(lo, ok0, ok1), y, NEG_INF)
                    b_ref[v, 0, rr * GRID_W:(rr + 1) * GRID_W, jp * 128:(jp + 1) * 128] = tiles[key]

    return pl.pallas_call(
        kern, name="rpb_tables", grid=(HEADS,),
        in_specs=[pl.BlockSpec((1, 15, 128), lambda h: (h, 0, 0))],
        out_specs=pl.BlockSpec((3, 1, QBLK, KBLK), lambda h: (0, h, 0, 0)),
        out_shape=jax.ShapeDtypeStruct((3, HEADS, QBLK, KBLK), F32),
        compiler_params=_cparams(("arbitrary",)),
    )(rpb2)


def rpb_bwd(dbias):
    def kern(g0_ref, g1_ref, g2_ref, o_ref):
        g_refs = (g0_ref.at[0], g1_ref.at[0], g2_ref.at[0])
        lo = lax.broadcasted_iota(jnp.int32, (1, 128), 1) < GRID_W
        ri = lax.broadcasted_iota(jnp.int32, (GRID_W, GRID_W), 0)
        ci = lax.broadcasted_iota(jnp.int32, (GRID_W, GRID_W), 1)
        flip = (ri + ci == GRID_W - 1).astype(F32)
        groups = {}
        for v in range(3):
            for rr in range(QROWS):
                for jp in range(KROWS // 2):
                    j0, j1 = 2 * jp, 2 * jp + 1
                    ok0, ok1 = _row_valid(v, rr, j0), _row_valid(v, rr, j1)
                    if not (ok0 or ok1):
                        continue
                    g = g_refs[v][0, rr * GRID_W:(rr + 1) * GRID_W, jp * 128:(jp + 1) * 128]
                    key = (j0 - rr + _DR_OFF[v], ok0, ok1)
                    groups[key] = g if key not in groups else groups[key] + g
        acc = [jnp.zeros((1, 128), F32) for _ in range(15)]
        for (d0, ok0, ok1), g in groups.items():
            g = lax.dot_general(flip, g, (((1,), (0,)), ((), ())), precision=lax.Precision.HIGHEST,
                                preferred_element_type=F32)
            g = pltpu.roll(pltpu.roll(g, 128 - 48, 1), 0, 1, stride=1, stride_axis=0)
            s = jnp.sum(g, axis=0, keepdims=True)
            if ok0:
                acc[d0] = acc[d0] + jnp.where(lo, s, 0.0)
            if ok1:
                acc[d0 + 1] = acc[d0 + 1] + jnp.where(lo, 0.0, s)
        for d in range(15):
            o_ref[0, d:d + 1, :] = acc[d] + pltpu.roll(acc[d], GRID_W, 1)

    return pl.pallas_call(
        kern, name="rpb_bwd", grid=(HEADS,),
        in_specs=[pl.BlockSpec((1, 1, QBLK, KBLK), functools.partial(lambda v, h: (v, h, 0, 0), v)) for v in range(3)],
        out_specs=pl.BlockSpec((1, 15, 128), lambda h: (h, 0, 0)),
        out_shape=jax.ShapeDtypeStruct((HEADS, 15, 128), F32),
        compiler_params=_cparams(("arbitrary",)),
    )(dbias, dbias, dbias)


def _scaled_q(q_raw, qg):
    return _pair_rms(q_raw, qg) * (HDIM ** -0.5)


def _head_lanes():
    lo = lax.broadcasted_iota(jnp.int32, (1, 2 * HDIM), 1) < HDIM
    return lo, jnp.logical_not(lo)


SOFTMAX_ROWS = 32


def _emit_interleaved(vector_work, matmul_work):
    for j in range(max(len(vector_work), len(matmul_work))):
        for work in (vector_work, matmul_work):
            if j < len(work):
                work[j]()


def _kblock(i):
    return jnp.clip(i - 1, 0, (SEQ - KBLK) // QBLK)


def _kstart(i):
    return pl.multiple_of(_kblock(i) * QBLK, QBLK)


ATTN_STEPS = NQBLK // 2
ATTN_ROWS = 2 * QBLK
KCOLS = QBLK


def _attn_in_specs():
    return [
        pl.BlockSpec((ATTN_ROWS, 128), lambda p, i: (i, ZQ + p)),
        pl.BlockSpec((SEQ, 128), lambda p, i: (0, ZK + p)),
        pl.BlockSpec((SEQ, 128), lambda p, i: (0, ZV + p)),
        pl.BlockSpec((ATTN_ROWS, 128), lambda p, i: (i, ZG + p)),
        pl.BlockSpec((CTX, 128), lambda p, i: (0, 2 + p)),
        pl.BlockSpec((CTX, 128), lambda p, i: (0, 6 + p)),
    ]


def _bias_specs():
    bias_spec = lambda variant: pl.BlockSpec((1, 2, QBLK, KBLK), lambda p, i: (variant(i), p, 0, 0))
    return [bias_spec(lambda i: jnp.where(i == 0, 0, 1)),
            bias_spec(lambda i: jnp.where(i == ATTN_STEPS - 1, 2, 1))]


def _prob_specs():
    return [pl.BlockSpec((2, ATTN_ROWS, KBLK), lambda p, i: (p, i, 0)),
            pl.BlockSpec((2, ATTN_ROWS, CTX), lambda p, i: (p, i, 0))]


NORM_ROWS = 512


def _norm_keys(k_ref, ck_ref, kg_ref, kn_scr, ckn_scr):
    def body(c, carry):
        sl = pl.ds(pl.multiple_of(c * NORM_ROWS, NORM_ROWS), NORM_ROWS)
        kn_scr[sl, :] = _pair_rms(k_ref[sl, :], kg_ref[...]).astype(BF16)
        return carry

    lax.fori_loop(0, SEQ // NORM_ROWS, body, 0)
    ckn_scr[...] = _pair_rms(ck_ref[...], kg_ref[...]).astype(BF16)


def _values_with_ones(v_ref, cv_ref, v1_scr, cv1_scr):
    for a, mine in enumerate(_head_lanes()):
        def body(c, carry):
            sl = pl.ds(pl.multiple_of(c * NORM_ROWS, NORM_ROWS), NORM_ROWS)
            v1_scr[a, sl, :] = jnp.where(mine, v_ref[sl, :], 1.0).astype(BF16)
            return carry

        lax.fori_loop(0, SEQ // NORM_ROWS, body, 0)
        cv1_scr[a] = jnp.where(mine, cv_ref[...], 1.0).astype(BF16)


def _pair_major_spec():
    return pl.BlockSpec((1, ATTN_ROWS, 128), lambda p, i: (p, i, 0))


def attn_fwd(z, zc, bias, qg2, kg2):
    def kern(q_ref, k_ref, v_ref, bg_ref, ck_ref, cv_ref, be_ref, bo_ref, qg_ref, kg_ref,
             ob_ref, o_ref, rden_ref, pl_ref, pc_ref, kn_scr, ckn_scr, v1_scr, cv1_scr, s_scr):
        i = pl.program_id(1)

        @pl.when(i == 0)
        def _():
            _norm_keys(k_ref, ck_ref, kg_ref, kn_scr, ckn_scr)
            _values_with_ones(v_ref, cv_ref, v1_scr, cv1_scr)

        heads = _head_lanes()
        bias_refs = (be_ref, bo_ref)
        tiles = [(b, a) for b in range(2) for a in range(2)]
        rows = [slice(b * QBLK, (b + 1) * QBLK) for b in range(2)]
        qn = [_scaled_q(q_ref[rows[b], :], qg_ref[...]) for b in range(2)]
        qa = [jnp.where(heads[a], qn[b], 0.0).astype(BF16) for b, a in tiles]
        pv = [None] * len(tiles)
        done = {}
        latent = KBLK // KCOLS

        def keys(b, n):
            return pl.ds(pl.multiple_of(_kstart(2 * i + b) + n * KCOLS, KCOLS), KCOLS)

        def score_piece(t, n):
            b, a = tiles[t]
            cols = slice(n * KCOLS, (n + 1) * KCOLS)
            if n < latent:
                s_scr[t, :, cols] = mm_nt(qa[t], kn_scr[keys(b, n), :]) + bias_refs[b][0, a, :, cols]
            else:
                s_scr[t, :, cols] = mm_nt(qa[t], ckn_scr[...])

        def softmax_rows(t, r):
            b, a = tiles[t]
            rs = slice(r * SOFTMAX_ROWS, (r + 1) * SOFTMAX_ROWS)
            out_rows = slice(b * QBLK + rs.start, b * QBLK + rs.stop)
            s = s_scr[t, rs, :]
            p = jnp.exp(s - jnp.max(s, axis=-1, keepdims=True)).astype(BF16)
            pl_ref[a, out_rows, :] = p[:, :KBLK]
            pc_ref[a, out_rows, :] = p[:, KBLK:]

        def value_piece(t, n):
            b, a = tiles[t]
            if n < latent:
                part = mm(pl_ref[a, rows[b], n * KCOLS:(n + 1) * KCOLS], v1_scr[a, keys(b, n), :])
            else:
                part = mm(pc_ref[a, rows[b], :], cv1_scr[a])
            pv[t] = part if pv[t] is None else pv[t] + part
            if n == latent:
                finish(t)

        def finish(t):
            b, a = tiles[t]
            r = jnp.where(heads[a], pltpu.roll(1.0 / pv[t], HDIM, 1), 0.0)
            done[t] = (pv[t] * r, r)
            if a == 1:
                o, rden = (lo + hi for lo, hi in zip(done[t - 1], done[t]))
                ob_ref[rows[b], :] = o * jax.nn.silu(bg_ref[rows[b], :])
                o_ref[0, rows[b], :] = o
                rden_ref[0, rows[b], :] = rden

        pieces = range(latent + 1)
        for n in pieces:
            score_piece(0, n)
        for t in range(len(tiles)):
            matmuls = []
            for n in pieces:
                if t + 1 < len(tiles):
                    matmuls.append(functools.partial(score_piece, t + 1, n))
                if t > 0:
                    matmuls.append(functools.partial(value_piece, t - 1, n))
            _emit_interleaved([functools.partial(softmax_rows, t, r) for r in range(QBLK // SOFTMAX_ROWS)], matmuls)
        for n in pieces:
            value_piece(len(tiles) - 1, n)

    qblk = pl.BlockSpec((ATTN_ROWS, 128), lambda p, i: (i, p))
    return pl.pallas_call(
        kern, name="attn_fwd", grid=(NPAIR, ATTN_STEPS),
        in_specs=_attn_in_specs() + _bias_specs() + [_row(128), _row(128)],
        out_specs=[qblk, _pair_major_spec(), _pair_major_spec()] + _prob_specs(),
        out_shape=[jax.ShapeDtypeStruct((SEQ, 512), F32)] + [jax.ShapeDtypeStruct((NPAIR, SEQ, 128), F32)] * 2
        + [jax.ShapeDtypeStruct((HEADS, SEQ, KBLK), BF16), jax.ShapeDtypeStruct((HEADS, SEQ, CTX), BF16)],
        scratch_shapes=[pltpu.VMEM((SEQ, 128), BF16), pltpu.VMEM((CTX, 128), BF16),
                        pltpu.VMEM((2, SEQ, 128), BF16), pltpu.VMEM((2, CTX, 128), BF16),
                        pltpu.VMEM((4, QBLK, KBLK + CTX), F32)],
        compiler_params=_cparams(("arbitrary", "arbitrary"), 40 * 1024 * 1024),
    )(z, z, z, z, zc, zc, bias, bias, qg2, kg2)


def attn_bwd(z, zc, qg2, kg2, dcat, saved):
    def kern(q_ref, k_ref, v_ref, bg_ref, ck_ref, cv_ref, qg_ref, kg_ref, do_ref, o_ref, rden_ref, pl_ref, pc_ref,
             dq_ref, dk_ref, dv_ref, dbg_ref, dck_ref, dcv_ref, db_ref, dqg_ref, dkg_ref,
             kn_scr, ckn_scr, v_scr, cv_scr, dknt_scr, dvt_scr, dcknt_scr, dcvt_scr, dp_scr, ds_scr):
        p, i = pl.program_id(0), pl.program_id(1)
        last = i == ATTN_STEPS - 1

        @pl.when(i == 0)
        def _():
            _norm_keys(k_ref, ck_ref, kg_ref, kn_scr, ckn_scr)

            def body(c, carry):
                sl = pl.ds(pl.multiple_of(c * NORM_ROWS, NORM_ROWS), NORM_ROWS)
                v_scr[sl, :] = v_ref[sl, :].astype(BF16)
                return carry

            lax.fori_loop(0, SEQ // NORM_ROWS, body, 0)
            cv_scr[...] = cv_ref[...].astype(BF16)
            for acc in (dknt_scr, dvt_scr, dcknt_scr, dcvt_scr, db_ref):
                acc[...] = jnp.zeros_like(acc)

        @pl.when((i == 0) & (p == 0))
        def _():
            dqg_ref[...] = jnp.zeros_like(dqg_ref)
            dkg_ref[...] = jnp.zeros_like(dkg_ref)

        heads = _head_lanes()
        tiles = [(b, a) for b in range(2) for a in range(2)]
        rows = [slice(b * QBLK, (b + 1) * QBLK) for b in range(2)]
        kb = [_kblock(2 * i + b) for b in range(2)]
        variant = [jnp.where(i == 0, 0, 1), jnp.where(last, 2, 1)]
        latent = KBLK // KCOLS

        def keys(b, n):
            return pl.ds(pl.multiple_of((kb[b] + n) * KCOLS, KCOLS), KCOLS)

        gated = []
        for b in range(2):
            bg, dout, o = bg_ref[rows[b], :], do_ref[rows[b], :], o_ref[0, rows[b], :]
            sig = jax.nn.sigmoid(bg)
            do = dout * (bg * sig)
            dbg_ref[rows[b], :] = (dout * o * (sig * (1.0 + bg * (1.0 - sig)))).astype(BF16)
            rden = rden_ref[0, rows[b], :]
            dr = do * rden
            qn = _scaled_q(q_ref[rows[b], :], qg_ref[...])
            gated.append((dr, dr.T.astype(BF16), qn.T.astype(BF16), do * o * rden))

        feats = [slice(a * HDIM, (a + 1) * HDIM) for a in range(2)]
        doa, doa_t, qa_t, delta = [], [], [], []
        for b, a in tiles:
            dr, dr_t, qn_t, weighted = gated[b]
            doa.append(jnp.where(heads[a], dr, 0.0).astype(BF16))
            doa_t.append(dr_t[feats[a], :])
            qa_t.append(qn_t[feats[a], :])
            delta.append(jnp.sum(jnp.where(heads[a], weighted, 0.0), axis=-1, keepdims=True))
        dqn = [None] * len(tiles)

        def cols(n):
            return slice(n * KCOLS, (n + 1) * KCOLS)

        def stage_a(t, n):
            b, a = tiles[t]
            if n < latent:
                dp_scr[t, :, cols(n)] = mm_nt(doa[t], v_scr[keys(b, n), :])
                dvt_scr[kb[b] + n, feats[a], :] += mm(doa_t[t], pl_ref[a, rows[b], cols(n)])
            else:
                dp_scr[t, :, cols(n)] = mm_nt(doa[t], cv_scr[...])
                dcvt_scr[feats[a], :] += mm(doa_t[t], pc_ref[a, rows[b], :])

        def stage_b(t, r):
            b, a = tiles[t]
            rs = slice(r * SOFTMAX_ROWS, (r + 1) * SOFTMAX_ROWS)
            in_rows = slice(b * QBLK + rs.start, b * QBLK + rs.stop)
            d = dp_scr[t, rs, :] - delta[t][rs, :]
            ds_lat = pl_ref[a, in_rows, :].astype(F32) * d[:, :KBLK]
            ds_ctx = pc_ref[a, in_rows, :].astype(F32) * d[:, KBLK:]
            db_ref[variant[b], a, rs, :] += ds_lat
            ds_scr[t, rs, :KBLK] = ds_lat.astype(BF16)
            ds_scr[t, rs, KBLK:] = ds_ctx.astype(BF16)

        def stage_c(t, n):
            b, a = tiles[t]
            ds = ds_scr[t, :, cols(n)]
            if n < latent:
                part = mm(ds, kn_scr[keys(b, n), :])
                dknt_scr[kb[b] + n, feats[a], :] += mm(qa_t[t], ds)
            else:
                part = mm(ds, ckn_scr[...])
                dcknt_scr[feats[a], :] += mm(qa_t[t], ds)
            dqn[t] = part if dqn[t] is None else dqn[t] + part
            if n == latent and a == 1:
                both = jnp.where(heads[0], dqn[t - 1], 0.0) + jnp.where(heads[1], dqn[t], 0.0)
                dq, dqg = jax.vjp(_scaled_q, q_ref[rows[b], :], qg_ref[...])[1](both)
                dq_ref[rows[b], :] = dq.astype(BF16)
                dqg_ref[...] += dqg

        pieces = range(latent + 1)
        for n in pieces:
            stage_a(0, n)
        for t in range(len(tiles)):
            matmuls = []
            for n in pieces:
                if t + 1 < len(tiles):
                    matmuls.append(functools.partial(stage_a, t + 1, n))
                if t > 0:
                    matmuls.append(functools.partial(stage_c, t - 1, n))
            _emit_interleaved([functools.partial(stage_b, t, r) for r in range(QBLK // SOFTMAX_ROWS)], matmuls)
        for n in pieces:
            stage_c(len(tiles) - 1, n)

        @pl.when(last)
        def _():
            def body(c, dkg):
                sl = pl.ds(pl.multiple_of(c * NORM_ROWS, NORM_ROWS), NORM_ROWS)
                blocks = range(NORM_ROWS // KCOLS)
                dkn = jnp.concatenate([dknt_scr[c * len(blocks) + n].T for n in blocks], axis=0)
                dv = jnp.concatenate([dvt_scr[c * len(blocks) + n].T for n in blocks], axis=0)
                _, nvjp = jax.vjp(_pair_rms, k_ref[sl, :], kg_ref[...])
                dk, dg = nvjp(dkn)
                dk_ref[sl, :] = dk.astype(BF16)
                dv_ref[sl, :] = dv.astype(BF16)
                return dkg + dg

            dkg = lax.fori_loop(0, SEQ // NORM_ROWS, body, jnp.zeros((1, 128), F32))
            _, nvjp = jax.vjp(_pair_rms, ck_ref[...], kg_ref[...])
            dck, dg = nvjp(dcknt_scr[...].T)
            dck_ref[...] = dck
            dcv_ref[...] = dcvt_scr[...].T
            dkg_ref[...] += dkg + dg

        @pl.when(last & (p == NPAIR - 1))
        def _():
            dqg_ref[...] = dqg_ref[...] + pltpu.roll(dqg_ref[...], HDIM, 1)
            dkg_ref[...] = dkg_ref[...] + pltpu.roll(dkg_ref[...], HDIM, 1)

    blk = lambda rows: pl.BlockSpec((rows, 128), lambda p, i: (0, p))
    qblk = pl.BlockSpec((ATTN_ROWS, 128), lambda p, i: (i, p))
    return pl.pallas_call(
        kern, name="attn_bwd", grid=(NPAIR, ATTN_STEPS),
        in_specs=_attn_in_specs() + [_row(128), _row(128), pl.BlockSpec((ATTN_ROWS, 128), lambda p, i: (i, 4 + p)),
                                     _pair_major_spec(), _pair_major_spec()] + _prob_specs(),
        out_specs=[qblk, blk(SEQ), blk(SEQ), qblk, blk(CTX), blk(CTX),
                   pl.BlockSpec((3, 2, QBLK, KBLK), lambda p, i: (0, p, 0, 0)), _row(128), _row(128)],
        out_shape=[jax.ShapeDtypeStruct((SEQ, 512), BF16)] * 4 + [jax.ShapeDtypeStruct((CTX, 512), F32)] * 2
        + [jax.ShapeDtypeStruct((3, HEADS, QBLK, KBLK), F32)]
        + [jax.ShapeDtypeStruct((1, 128), F32), jax.ShapeDtypeStruct((1, 128), F32)],
        scratch_shapes=[pltpu.VMEM((SEQ, 128), BF16), pltpu.VMEM((CTX, 128), BF16),
                        pltpu.VMEM((SEQ, 128), BF16), pltpu.VMEM((CTX, 128), BF16),
                        pltpu.VMEM((SEQ // KCOLS, 128, KCOLS), F32), pltpu.VMEM((SEQ // KCOLS, 128, KCOLS), F32),
                        pltpu.VMEM((128, CTX), F32), pltpu.VMEM((128, CTX), F32),
                        pltpu.VMEM((4, QBLK, KBLK + CTX), F32), pltpu.VMEM((4, QBLK, KBLK + CTX), BF16)],
        compiler_params=_cparams(("arbitrary", "arbitrary"), VMEM_BIG),
    )(z, z, z, z, zc, zc, qg2, kg2, dcat, *saved)


def outproj(out_a, out_b, x, target, gate, wo):
    tl = 512

    def kern(a_ref, b_ref, x_ref, t_ref, g_ref, w_ref, loss_ref, dy_ref, dcat_ref, dg_ref, dw_ref):
        @pl.when(pl.program_id(0) == 0)
        def _():
            loss_ref[...] = jnp.zeros_like(loss_ref)
            dg_ref[...] = jnp.zeros_like(dg_ref)
            dw_ref[...] = jnp.zeros_like(dw_ref)

        a, b = a_ref[...].astype(BF16), b_ref[...].astype(BF16)
        mix = (jnp.dot(a, w_ref[0:512, :], preferred_element_type=F32)
               + jnp.dot(b, w_ref[512:1024, :], preferred_element_type=F32))
        err = x_ref[...] + g_ref[...] * mix - t_ref[...]
        loss_ref[...] += 0.5 * jnp.sum(jnp.mean(err * err, axis=-1))
        dy = err * (1.0 / DM)
        dy_ref[...] = dy
        dg_ref[...] += jnp.sum(dy * mix, axis=0, keepdims=True)
        dmix = (g_ref[...] * dy).astype(BF16)
        dcat_ref[...] = lax.dot_general(dmix, w_ref[...], (((1,), (1,)), ((), ())), preferred_element_type=F32)
        dw_ref[0:512, :] += lax.dot_general(a, dmix, (((0,), (0,)), ((), ())), preferred_element_type=F32)
        dw_ref[512:1024, :] += lax.dot_general(b, dmix, (((0,), (0,)), ((), ())), preferred_element_type=F32)

    tile = lambda w: pl.BlockSpec((tl, w), lambda t: (t, 0))
    whole = pl.BlockSpec((DM, DM), lambda t: (0, 0))
    return pl.pallas_call(
        kern, name="outproj", grid=(SEQ // tl,),
        in_specs=[tile(512), tile(512), tile(DM), tile(DM), _row(DM), whole],
        out_specs=[pl.BlockSpec((8, 128), lambda t: (0, 0)), tile(DM), tile(DM), _row(DM), whole],
        out_shape=[jax.ShapeDtypeStruct((8, 128), F32), jax.ShapeDtypeStruct((SEQ, DM), F32),
                   jax.ShapeDtypeStruct((SEQ, DM), F32), jax.ShapeDtypeStruct((1, DM), F32),
                   jax.ShapeDtypeStruct((DM, DM), F32)],
        compiler_params=_cparams(("arbitrary",), 48 * 1024 * 1024),
    )(out_a, out_b, x, target, gate, wo)


def _pieces(sources):
    out = []
    for name, c0, c1 in sources:
        for j in range(NCHIP):
            lo, hi = max(c0, j * SHARD_IN), min(c1, (j + 1) * SHARD_IN)
            if lo < hi:
                out.append((j, lo - j * SHARD_IN, hi - j * SHARD_IN, name, lo - c0, hi - c0))
    return out


DZ_PIECES = _pieces((("a", 0, 1536), ("q", 1536, 2048), ("k", 2048, 2560), ("v", 2560, 3072), ("g", 3072, DIN)))
DZC_PIECES = _pieces((("k", 2048, 2560), ("v", 2560, 3072)))
_NT = (((1,), (1,)), ((), ()))


DH_SUBTILES = 2


def _dz_specs(tl):
    return [pl.BlockSpec((tl, 1536), lambda t: (t, 0))] + [pl.BlockSpec((tl, 512), lambda t: (t, 0))] * 4


def dh_bwd(dz_parts, w_full, x, dy, shift, scale, norm_g, dg_ctx):
    tl = 512
    nt = SEQ // tl

    def kern(a_ref, q_ref, k_ref, v_ref, g_ref, w_ref, x_ref, dy_ref, sh_ref, sc_ref, gn_ref, dgc_ref,
             gx_ref, dsh_ref, dsc_ref, dg_ref):
        @pl.when(pl.program_id(0) == 0)
        def _():
            dsh_ref[...] = jnp.zeros_like(dsh_ref)
            dsc_ref[...] = jnp.zeros_like(dsc_ref)
            dg_ref[...] = dgc_ref[...]

        src = dict(a=a_ref, q=q_ref, k=k_ref, v=v_ref, g=g_ref)
        for sub in range(DH_SUBTILES):
            rows = slice(sub * tl // DH_SUBTILES, (sub + 1) * tl // DH_SUBTILES)
            dh = None
            for j, l0, l1, name, s0, s1 in DZ_PIECES:
                part = lax.dot_general(src[name][rows, s0:s1], w_ref[j, :, l0:l1], _NT, preferred_element_type=F32)
                dh = part if dh is None else dh + part
            _, vjp = jax.vjp(_modulated, x_ref[rows, :], gn_ref[...], sc_ref[...], sh_ref[...])
            dx, dg, dsc, dsh = vjp(dh)
            gx_ref[rows, :] = dy_ref[rows, :] + dx
            dg_ref[...] += dg
            dsc_ref[...] += dsc
            dsh_ref[...] += dsh

    tile = pl.BlockSpec((tl, DM), lambda t: (t, 0))
    return pl.pallas_call(
        kern, name="dh_bwd", grid=(nt,),
        in_specs=_dz_specs(tl) + [pl.BlockSpec((NCHIP, DM, SHARD_IN), lambda t: (0, 0, 0)), tile, tile, _row(DM),
                                  _row(DM), _row(DM), _row(DM)],
        out_specs=[tile, _row(DM), _row(DM), _row(DM)],
        out_shape=[jax.ShapeDtypeStruct((SEQ, DM), F32)] + [jax.ShapeDtypeStruct((1, DM), F32)] * 3,
        compiler_params=_cparams(("arbitrary",), 48 * 1024 * 1024),
    )(*dz_parts, w_full, x, dy, shift, scale, norm_g, dg_ctx)


def dw_bwd(h, dz_parts, hc, dck, dcv, g_out):
    tl = 512
    nt = SEQ // tl
    (rhi, wi), (rho, wo) = RS_SHAPES

    def kern(h_ref, a_ref, q_ref, k_ref, v_ref, g_ref, hc_ref, dck_ref, dcv_ref, go_hbm,
             wire_i, keep_i, wire_o, keep_o, acc, rcv_i, mine_o, rcv_o, load_sem, send_sems, recv_sems):
        t = pl.program_id(0)
        x, y, c = _me()
        k = 2 * x + y
        sib = _flip(1)
        half = lambda hh, rh: pl.ds(pl.multiple_of(hh * rh, rh), rh)
        load_o = pltpu.make_async_copy(go_hbm.at[:, half(c, rho), :], mine_o, load_sem)
        pair_o = _rcopy(go_hbm.at[:, half(1 - c, rho), :], rcv_o, send_sems, recv_sems, 0, sib)
        pair_i = _rcopy(acc.at[:, half(1 - c, rhi), :], rcv_i, send_sems, recv_sems, 1, sib)

        @pl.when(t == 0)
        def _():
            load_o.start()
            pair_o.start()
            acc[...] = jnp.zeros_like(acc)
            hct = hc_ref[...].T
            csrc = dict(k=dck_ref, v=dcv_ref)
            for j, l0, l1, name, s0, s1 in DZC_PIECES:
                acc[j, :, l0:l1] += jnp.dot(hct, csrc[name][:, s0:s1].astype(BF16), preferred_element_type=F32)

        ht = h_ref[...].T
        src = dict(a=a_ref, q=q_ref, k=k_ref, v=v_ref, g=g_ref)
        for j, l0, l1, name, s0, s1 in DZ_PIECES:
            acc[j, :, l0:l1] += jnp.dot(ht, src[name][:, s0:s1], preferred_element_type=F32)

        @pl.when(t == nt - 1)
        def _():
            pair_i.start()
            load_o.wait()
            pair_o.wait_recv()
            for j in range(NCHIP):
                wire_o[j] = (mine_o[j] + rcv_o[j]).astype(BF16)
            keep_o[...] = mine_o[k] + rcv_o[k]
            pair_i.wait_recv()
            mine = half(c, rhi)
            for j in range(NCHIP):
                wire_i[j] = (acc[j, mine, :] + rcv_i[j]).astype(BF16)
            keep_i[...] = acc[k, mine, :] + rcv_i[k]
            pair_o.wait_send()
            pair_i.wait_send()

    whole = lambda *shape: pl.BlockSpec(shape, lambda t: (0,) * len(shape))
    return pl.pallas_call(
        kern, name="dw_bwd", grid=(nt,),
        in_specs=[pl.BlockSpec((tl, DM), lambda t: (t, 0))] + _dz_specs(tl)
        + [whole(CTX, DM), whole(CTX, 512), whole(CTX, 512), pl.BlockSpec(memory_space=pl.ANY)],
        out_specs=[whole(NCHIP, rhi, wi), whole(rhi, wi), whole(NCHIP, rho, wo), whole(rho, wo)],
        out_shape=[jax.ShapeDtypeStruct((NCHIP, rhi, wi), BF16), jax.ShapeDtypeStruct((rhi, wi), F32),
                   jax.ShapeDtypeStruct((NCHIP, rho, wo), BF16), jax.ShapeDtypeStruct((rho, wo), F32)],
        scratch_shapes=[pltpu.VMEM((NCHIP, DM, SHARD_IN), F32), pltpu.VMEM((NCHIP, rhi, wi), F32),
                        pltpu.VMEM((NCHIP, rho, wo), F32), pltpu.VMEM((NCHIP, rho, wo), F32),
                        pltpu.SemaphoreType.DMA(()), pltpu.SemaphoreType.DMA((2,)), pltpu.SemaphoreType.DMA((2,))],
        compiler_params=_cparams(("arbitrary",), VMEM_BIG),
    )(h, *dz_parts, hc, dck, dcv, g_out)


def ctx_bwd(dck, dcv, w_full, ctx, cshift, cscale, norm_g):
    def kern(dck_ref, dcv_ref, w_ref, c_ref, sh_ref, sc_ref, g_ref, dsh_ref, dsc_ref, dg_ref):
        csrc = dict(k=dck_ref, v=dcv_ref)
        dhc = None
        for j, l0, l1, name, s0, s1 in DZC_PIECES:
            part = lax.dot_general(csrc[name][:, s0:s1].astype(BF16), w_ref[j, :, l0:l1], _NT,
                                   preferred_element_type=F32)
            dhc = part if dhc is None else dhc + part
        _, vjp = jax.vjp(lambda g, sc, sh: _modulated(c_ref[...], g, sc, sh), g_ref[...], sc_ref[...], sh_ref[...])
        dg_ref[...], dsc_ref[...], dsh_ref[...] = vjp(dhc)

    whole = lambda r, c: pl.BlockSpec((r, c), lambda i: (0, 0))
    return pl.pallas_call(
        kern, name="ctx_bwd", grid=(1,),
        in_specs=[whole(CTX, 512), whole(CTX, 512), pl.BlockSpec((NCHIP, DM, SHARD_IN), lambda i: (0, 0, 0)),
                  whole(CTX, DM), _row(DM), _row(DM), _row(DM)],
        out_specs=[_row(DM), _row(DM), _row(DM)],
        out_shape=[jax.ShapeDtypeStruct((1, DM), F32)] * 3,
        compiler_params=_cparams(("arbitrary",), 40 * 1024 * 1024),
    )(dck, dcv, w_full, ctx, cshift, cscale, norm_g)


def _lane_pad_rpb(rpb):
    r = jnp.pad(rpb, ((0, 0), (0, 0), (0, GRID_W - rpb.shape[-1])))
    return jnp.concatenate([r, r], axis=-1)


def local_step(chip, dev, x, c_vec, c_ctx, w_ada, b_shard, ctx, target, norm_g, sgu_g, w_s, b_s, q_g, k_g, rpb,
               w_in_shard, w_out_shard):
    bsb = jnp.broadcast_to(b_s[:, :, None], (4, 128, 128))
    qg2, kg2 = jnp.tile(q_g, (1, 2)), jnp.tile(k_g, (1, 2))

    z, h, w_in_full, w_out_full, mod_all, cs = inproj_fwd(chip, x, c_vec, c_ctx, w_ada, b_shard, norm_g, w_in_shard,
                                                          w_out_shard)
    mods = mod_all.transpose(1, 0, 2).reshape(CS_ROWS, 3 * DM)
    mod = lax.dynamic_slice(mods, (8 * dev, 0), (1, 3 * DM))
    shift, scale, gate = mod[:, :DM], mod[:, DM:2 * DM], mod[:, 2 * DM:]
    cshift, cscale = mods[8 * NDEV:8 * NDEV + 1, :DM], mods[8 * NDEV:8 * NDEV + 1, DM:2 * DM]
    zc, hc = ctx_fwd(ctx, cshift, cscale, norm_g, w_in_full)
    bias = rpb_tables(_lane_pad_rpb(rpb))
    out_a = sgu_fwd(z, sgu_g, w_s, bsb)
    out_b, *saved = attn_fwd(z, zc, bias, qg2, kg2)
    loss8, dy, dcat, dgate, dwo = outproj(out_a, out_b, x, target, gate, w_out_full.reshape(DM, DM))
    dz_a, dsg, dws, dbsb = sgu_bwd(z, sgu_g, w_s, bsb, dcat)
    dq, dk, dv, dbg, dck, dcv, dbias, dqg2, dkg2 = attn_bwd(z, zc, qg2, kg2, dcat, saved)
    drpb = rpb_bwd(dbias)[:, :, :rpb.shape[-1]]
    dz_parts = (dz_a, dq, dk, dv, dbg)
    dcshift, dcscale, dng_c = ctx_bwd(dck, dcv, w_in_full, ctx, cshift, cscale, norm_g)
    wire_i, keep_i, wire_o, keep_o = dw_bwd(h, dz_parts, hc, dck, dcv, dwo.reshape(NCHIP, SHARD_OUT, DM))
    *in_flight, token = rs_start(wire_i, wire_o)
    grad_x, dshift, dscale, dng = dh_bwd(dz_parts, w_in_full, x, dy, shift, scale, norm_g, dng_c + token[0, 0])
    got_i, got_o = rs_wait(*in_flight, dshift)
    return dict(
        loss=loss8[0:1, 0:1], grad_x=grad_x, rs=(keep_i, got_i, keep_o, got_o), cs=cs,
        dmod=jnp.concatenate([dshift, dscale, dgate], axis=-1),
        dcmod=jnp.concatenate([dcshift, dcscale, jnp.zeros((1, DM), F32)], axis=-1),
        d_norm_g=dng, d_sgu_g=dsg, d_w_s=dws, d_b_s=dbsb[:, :, 0],
        d_q_g=dqg2[:, :HDIM], d_k_g=dkg2[:, :HDIM], d_rpb=drpb)


def _me():
    return lax.axis_index("x"), lax.axis_index("y"), lax.axis_index("c")


def _flip(q):
    x, y, c = _me()
    return ((1 - x) if q & 4 else x, (1 - y) if q & 2 else y, (1 - c) if q & 1 else c)


def _chip_of(dev):
    return 2 * dev[0] + dev[1]


def _rcopy(src, dst, send_sems, recv_sems, k, dev):
    return pltpu.make_async_remote_copy(src_ref=src, dst_ref=dst, send_sem=send_sems.at[k], recv_sem=recv_sems.at[k],
                                        device_id=dev, device_id_type=MESH_ID)


_VMEM_SPEC = pl.BlockSpec(memory_space=pltpu.VMEM)
SLAB_ROWS = 80


RS_SHAPES = ((DM // 2, SHARD_IN), (SHARD_OUT // 2, DM))
_HBM_SPEC = pl.BlockSpec(memory_space=pltpu.HBM)
_SEM_SPEC = pl.BlockSpec(memory_space=pltpu.SEMAPHORE)
_IN_FLIGHT = pltpu.SideEffectType.DATAFLOW_SIDE_EFFECTING


def _rs_copies(wires, lands, send_sems, recv_sems):
    return [pltpu.make_async_remote_copy(
        src_ref=wires[n].at[_chip_of(_flip(q))], dst_ref=lands[n].at[q // 2 - 1],
        send_sem=send_sems.at[3 * n + q // 2 - 1], recv_sem=recv_sems.at[3 * n + q // 2 - 1],
        device_id=_flip(q), device_id_type=MESH_ID) for n in (0, 1) for q in (2, 4, 6)]


def rs_start(wire_i, wire_o):
    lands = [lax.empty((NCHIP - 1, rh, w), BF16) for rh, w in RS_SHAPES]

    def body(wi_ref, wo_ref, li_ref, lo_ref, send_sems, recv_sems, wi_thru, wo_thru, li_thru, lo_thru, token):
        for cp in _rs_copies((wi_ref, wo_ref), (li_ref, lo_ref), send_sems, recv_sems):
            cp.start()
        token[...] = jnp.zeros_like(token)

    hbm = lambda a: pltpu.HBM(a.shape, a.dtype)
    return pl.pallas_call(
        body, name="rs_start",
        out_shape=(pltpu.SemaphoreType.DMA((6,)), pltpu.SemaphoreType.DMA((6,)), hbm(wire_i), hbm(wire_o),
                   hbm(lands[0]), hbm(lands[1]), jax.ShapeDtypeStruct((8, 128), F32)),
        in_specs=(_HBM_SPEC,) * 4, out_specs=(_SEM_SPEC, _SEM_SPEC) + (_HBM_SPEC,) * 4 + (_VMEM_SPEC,),
        input_output_aliases={0: 2, 1: 3, 2: 4, 3: 5},
        compiler_params=pltpu.CompilerParams(has_side_effects=_IN_FLIGHT),
    )(*[pltpu.with_memory_space_constraint(a, pltpu.HBM) for a in (wire_i, wire_o, *lands)])


def rs_wait(send_sems, recv_sems, wire_i, wire_o, land_i, land_o, after):
    def body(wi_ref, wo_ref, li_ref, lo_ref, send_sems, recv_sems, after_ref, wi_dead, wo_dead, gi_ref, go_ref):
        for cp in _rs_copies((wi_ref, wo_ref), (li_ref, lo_ref), send_sems, recv_sems):
            cp.wait_send()
            cp.wait_recv()

    hbm = lambda a: pltpu.HBM(a.shape, a.dtype)
    return pl.pallas_call(
        body, name="rs_wait", out_shape=(hbm(wire_i), hbm(wire_o), hbm(land_i), hbm(land_o)),
        in_specs=(_HBM_SPEC,) * 4 + (_SEM_SPEC, _SEM_SPEC, pl.BlockSpec(memory_space=pl.ANY)),
        out_specs=(_HBM_SPEC,) * 4, input_output_aliases={0: 0, 1: 1, 2: 2, 3: 3},
        compiler_params=pltpu.CompilerParams(has_side_effects=_IN_FLIGHT),
    )(wire_i, wire_o, land_i, land_o, send_sems, recv_sems, after)[2:]


def final_reduce(keep_i, got_i, keep_o, got_o, slab):
    def kern(ki_ref, gi_ref, ko_ref, go_ref, s_ref, gin_ref, gout_ref, all_ref, tot_ref, send_sems, recv_sems):
        x, y, c = _me()
        sib = _flip(1)
        dev = lambda d: 4 * d[0] + 2 * d[1] + d[2]
        me = dev((x, y, c))

        def slab_copy(idx, owner, to):
            return _rcopy(all_ref.at[dev(owner)], all_ref.at[dev(owner)], send_sems, recv_sems, idx, to)

        all_ref[me] = s_ref[...]
        first = [slab_copy(0, (x, y, c), sib)] + [slab_copy(q // 2, (x, y, c), _flip(q)) for q in (2, 4, 6)]
        for cp in first:
            cp.start()

        shares = []
        for n, (keep, got, out) in enumerate(((ki_ref, gi_ref, gin_ref), (ko_ref, go_ref, gout_ref))):
            rh = RS_SHAPES[n][0]
            half = lambda hh, rh=rh: pl.ds(pl.multiple_of(hh * rh, rh), rh)
            out[half(c), :] = ((keep[...] + got[0].astype(F32)) + got[1].astype(F32)) + got[2].astype(F32)
            share = _rcopy(out.at[half(c), :], out.at[half(c), :], send_sems, recv_sems, 7 + n, sib)
            share.start()
            shares.append((share, _rcopy(out.at[half(1 - c), :], out.at[half(1 - c), :], send_sems, recv_sems, 7 + n,
                                         sib)))

        passed = []
        for q in (2, 4, 6):
            slab_copy(q // 2, _flip(q), (x, y, c)).wait_recv()
            cp = slab_copy(3 + q // 2, _flip(q), sib)
            cp.start()
            passed.append(cp)
        slab_copy(0, sib, (x, y, c)).wait_recv()
        for q in (2, 4, 6):
            slab_copy(3 + q // 2, _flip(q | 1), (x, y, c)).wait_recv()
        tot = all_ref[0]
        for d in range(1, NDEV):
            tot = tot + all_ref[d]
        tot_ref[...] = tot
        for share, arrival in shares:
            arrival.wait_recv()
            share.wait_send()
        for cp in first + passed:
            cp.wait_send()

    (rhi, wi), (rho, wo) = RS_SHAPES
    return pl.pallas_call(
        kern, name="final_reduce", in_specs=[_VMEM_SPEC] * 5, out_specs=[_VMEM_SPEC] * 4,
        out_shape=[jax.ShapeDtypeStruct((2 * rhi, wi), F32), jax.ShapeDtypeStruct((2 * rho, wo), F32),
                   jax.ShapeDtypeStruct((NDEV, SLAB_ROWS, DM), F32), jax.ShapeDtypeStruct((SLAB_ROWS, DM), F32)],
        scratch_shapes=[pltpu.SemaphoreType.DMA((9,)), pltpu.SemaphoreType.DMA((9,))],
        compiler_params=pltpu.CompilerParams(vmem_limit_bytes=40 * 1024 * 1024),
    )(keep_i, got_i, keep_o, got_o, slab)


def ada_bwd(a_in, dm, dm_shard, w_ada, c_ctx):
    def kern(a_ref, dm_ref, dms_ref, w_ref, cc_ref, dw_ref, db_ref, dcc_ref, parts, send_sems, recv_sems):
        x, y, c = _me()
        k = 2 * x + y
        act = jax.nn.silu(a_ref[...]).astype(BF16)
        dms = dms_ref[...].astype(BF16)
        dw_ref[...] = lax.dot_general(act, dms, (((0,), (0,)), ((), ())), preferred_element_type=F32)
        db_ref[...] = jnp.sum(dm_ref[...], axis=0, keepdims=True)
        parts[k] = lax.dot_general(dms, w_ref[...].astype(BF16), (((1,), (1,)), ((), ())), preferred_element_type=F32)
        sends = [_rcopy(parts.at[k], parts.at[k], send_sems, recv_sems, q // 2 - 1, _flip(q)) for q in (2, 4, 6)]
        for cp in sends:
            cp.start()
        for q in (2, 4, 6):
            kq = _chip_of(_flip(q))
            _rcopy(parts.at[kq], parts.at[kq], send_sems, recv_sems, q // 2 - 1, _flip(q)).wait_recv()
        dact = ((parts[0] + parts[1]) + parts[2]) + parts[3]
        _, vjp = jax.vjp(jax.nn.silu, cc_ref[...])
        dcc_ref[...] = vjp(dact[8:9, :])[0]
        for cp in sends:
            cp.wait_send()

    return pl.pallas_call(
        kern, name="ada_bwd", in_specs=[_VMEM_SPEC] * 5, out_specs=[_VMEM_SPEC] * 3,
        out_shape=[jax.ShapeDtypeStruct((DM, SHARD_ADA), F32), jax.ShapeDtypeStruct((1, 3 * DM), F32),
                   jax.ShapeDtypeStruct((1, DM), F32)],
        scratch_shapes=[pltpu.VMEM((NCHIP, 16, DM), F32), pltpu.SemaphoreType.DMA((3,)), pltpu.SemaphoreType.DMA((3,))],
    )(a_in, dm, dm_shard, w_ada, c_ctx)


def _adamw_math(w, g, m, v):
    m = B1 * m + (1.0 - B1) * g
    v = B2 * v + (1.0 - B2) * (g * g)
    m_hat = m / (1.0 - B1 ** STEP)
    v_hat = v / (1.0 - B2 ** STEP)
    return -LR * (m_hat / (jnp.sqrt(v_hat) + ADAM_EPS) + WD * w), m, v


def adamw_big(w, g, m, v, name, block_rows=256):
    rows, width = w.shape

    def kern(w_ref, g_ref, m_ref, v_ref, d_ref, nm_ref, nv_ref):
        d_ref[...], nm_ref[...], nv_ref[...] = _adamw_math(w_ref[...], g_ref[...], m_ref[...], v_ref[...])

    spec = pl.BlockSpec((block_rows, width), lambda i: (i, 0))
    return pl.pallas_call(
        kern, name=name, grid=(rows // block_rows,), in_specs=[spec] * 4, out_specs=[spec] * 3,
        out_shape=[jax.ShapeDtypeStruct((rows, width), F32)] * 3,
        compiler_params=_cparams(("arbitrary",)),
    )(w, g, m, v)


def adamw_small(quads):
    n = len(quads)

    def kern(*refs):
        ins, outs = refs[:4 * n], refs[4 * n:]
        for i in range(n):
            w, g, m, v = (r[...] for r in ins[4 * i:4 * i + 4])
            outs[3 * i][...], outs[3 * i + 1][...], outs[3 * i + 2][...] = _adamw_math(w, g, m, v)

    flat = [a for quad in quads for a in quad]
    res = pl.pallas_call(
        kern, name="adamw_small", in_specs=[_VMEM_SPEC] * (4 * n), out_specs=[_VMEM_SPEC] * (3 * n),
        out_shape=[jax.ShapeDtypeStruct(q[0].shape, F32) for q in quads for _ in range(3)],
    )(*flat)
    return [tuple(res[3 * i:3 * i + 3]) for i in range(n)]


def _rows_of(a, rows):
    flat = a.reshape(-1)
    return jnp.pad(flat, (0, rows * DM - flat.shape[0])).reshape(rows, DM)


def kernel(x, c, ctx, c_ctx, w_ada, b_ada, norm_g, w_in, sgu_norm_g, w_spatial, b_spatial, q_norm_g, k_norm_g, rpb, w_out, loss_target, m_c_ctx, m_w_ada, m_b_ada, m_norm_g, m_w_in, m_sgu_norm_g, m_w_spatial, m_b_spatial, m_q_norm_g, m_k_norm_g, m_rpb, m_w_out, v_c_ctx, v_w_ada, v_b_ada, v_norm_g, v_w_in, v_sgu_norm_g, v_w_spatial, v_b_spatial, v_q_norm_g, v_k_norm_g, v_rpb, v_w_out):
    xi, yi, ci = lax.axis_index("x"), lax.axis_index("y"), lax.axis_index("c")
    chip, dev = 2 * xi + yi, 4 * xi + 2 * yi + ci
    c_ctx2 = c_ctx.reshape(1, DM)

    b_shard = lax.dynamic_slice(b_ada, (0, chip * SHARD_ADA), (1, SHARD_ADA))
    part = local_step(chip.reshape(1).astype(jnp.int32), dev, x[0], c, c_ctx2, w_ada[0], b_shard, ctx[0], loss_target[0],
                      norm_g, sgu_norm_g, w_spatial[0], b_spatial[0], q_norm_g, k_norm_g, rpb[0], w_in[0], w_out[0])
    cs = part["cs"]

    slab = jnp.concatenate([
        part["d_norm_g"], _rows_of(part["d_sgu_g"], 1), _rows_of(part["d_b_s"], 1),
        _rows_of(jnp.concatenate([part["d_q_g"], part["d_k_g"]], axis=-1), 1), _rows_of(part["d_rpb"], 4),
        _rows_of(part["loss"], 1), _rows_of(part["dcmod"], 3), _rows_of(part["dmod"], 3), jnp.zeros((1, DM), F32),
        _rows_of(part["d_w_s"], 64)], axis=0)
    g_w_in, g_w_out, gathered, tot = final_reduce(*part["rs"], slab)
    dm = jnp.concatenate([gathered[:, 12:15, :].reshape(NDEV, 3 * DM), tot[9:12].reshape(1, 3 * DM),
                          jnp.zeros((7, 3 * DM), F32)], axis=0)
    a_in = jnp.concatenate([cs[0:8 * NDEV:8], cs[8 * NDEV:8 * NDEV + 1], jnp.zeros((7, DM), F32)], axis=0)
    dm_shard = lax.dynamic_slice(dm, (0, chip * SHARD_ADA), (16, SHARD_ADA))
    g_w_ada, g_b_ada, g_c_ctx = ada_bwd(a_in, dm, dm_shard, w_ada[0], c_ctx2)

    loss = tot[8, 0]
    g_small = dict(
        c_ctx=g_c_ctx, b_ada=g_b_ada, norm_g=tot[0:1], sgu_norm_g=tot[1:2, :512], w_spatial=tot[16:80].reshape(512, 128),
        b_spatial=tot[2:3, :512].reshape(4, 128), q_norm_g=tot[3:4, :HDIM], k_norm_g=tot[3:4, HDIM:2 * HDIM],
        rpb=tot[4:8].reshape(-1)[:HEADS * 15 * 31].reshape(HEADS * 15, 31))
    shapes = dict(c_ctx=(DM,), w_ada=(1, DM, SHARD_ADA), b_ada=(1, 3 * DM), norm_g=(1, DM), w_in=(1, DM, SHARD_IN),
                  sgu_norm_g=(1, 512), w_spatial=(1, 4, 128, 128), b_spatial=(1, 4, 128), q_norm_g=(1, HDIM),
                  k_norm_g=(1, HDIM), rpb=(1, HEADS, 15, 31), w_out=(1, SHARD_OUT, DM))
    names = list(shapes)
    weights = dict(c_ctx=c_ctx, w_ada=w_ada, b_ada=b_ada, norm_g=norm_g, w_in=w_in, sgu_norm_g=sgu_norm_g,
                   w_spatial=w_spatial, b_spatial=b_spatial, q_norm_g=q_norm_g, k_norm_g=k_norm_g, rpb=rpb, w_out=w_out)
    m_in = dict(zip(names, (m_c_ctx, m_w_ada, m_b_ada, m_norm_g, m_w_in, m_sgu_norm_g, m_w_spatial, m_b_spatial,
                            m_q_norm_g, m_k_norm_g, m_rpb, m_w_out)))
    v_in = dict(zip(names, (v_c_ctx, v_w_ada, v_b_ada, v_norm_g, v_w_in, v_sgu_norm_g, v_w_spatial, v_b_spatial,
                            v_q_norm_g, v_k_norm_g, v_rpb, v_w_out)))
    grads = dict(g_small, w_ada=g_w_ada, w_in=g_w_in, w_out=g_w_out)
    upd = {}
    for n in ("w_ada", "w_in", "w_out"):
        g = grads[n]
        upd[n] = adamw_big(weights[n].reshape(g.shape), g, m_in[n].reshape(g.shape), v_in[n].reshape(g.shape),
                           "adamw_" + n)
    small = [n for n in names if n not in upd]
    res = adamw_small([(weights[n].reshape(grads[n].shape), grads[n], m_in[n].reshape(grads[n].shape),
                        v_in[n].reshape(grads[n].shape)) for n in small])
    upd.update(zip(small, res))
    out = [loss, part["grad_x"].reshape(1, SEQ, DM)]
    out += [grads[n].reshape(shapes[n]) for n in names]
    for slot in range(3):
        out += [upd[n][slot].reshape(shapes[n]) for n in names]
    return tuple(out)
```

```python
import functools

import jax
import jax.numpy as jnp
from jax import lax
from jax.experimental import pallas as pl
from jax.experimental.pallas import tpu as pltpu

F32, BF16 = jnp.float32, jnp.bfloat16
SEQ, DM, CTX, DIN = 4096, 1024, 256, 3584
NCHIP, NDEV = 4, 8
SHARD_IN = DIN // NCHIP
SHARD_ADA = 3 * DM // NCHIP
SHARD_OUT = DM // NCHIP
GRID_W = 64
QROWS = 4
KROWS = 12
QBLK, KBLK = QROWS * GRID_W, KROWS * GRID_W
NQBLK = SEQ // QBLK
HEADS, HDIM, NPAIR = 8, 64, 4
EPS = 1e-6
NEG_INF = -1e30
ZQ, ZK, ZV, ZG = 12, 16, 20, 24
LR, B1, B2, ADAM_EPS, WD, STEP = 0.001, 0.9, 0.999, 1e-08, 0.01, 10
VMEM_BIG = 56 * 1024 * 1024
MESH_ID = pl.DeviceIdType.MESH


def _dot(a, b, lhs_c, rhs_c):
    return lax.dot_general(a.astype(BF16), b.astype(BF16), (((lhs_c,), (rhs_c,)), ((), ())),
                           preferred_element_type=F32)


@jax.custom_vjp
def mm(a, b):
    return _dot(a, b, 1, 0)


@jax.custom_vjp
def mm_nt(a, b):
    return _dot(a, b, 1, 1)


@jax.custom_vjp
def mm_tn(a, b):
    return _dot(a, b, 0, 0)


mm.defvjp(lambda a, b: (mm(a, b), (a, b)), lambda r, ct: (mm_nt(ct, r[1]), mm_tn(r[0], ct)))
mm_nt.defvjp(lambda a, b: (mm_nt(a, b), (a, b)), lambda r, ct: (mm(ct, r[1]), mm_tn(ct, r[0])))
mm_tn.defvjp(lambda a, b: (mm_tn(a, b), (a, b)), lambda r, ct: (mm_nt(r[1], ct), mm(r[0], ct)))


def _rms(x, g):
    return x * lax.rsqrt(jnp.mean(x * x, axis=-1, keepdims=True) + EPS) * g


def _modulated(x, g, scale, shift):
    return _rms(x, g) * (1.0 + scale) + shift


def _pair_rms(x, g2):
    lo = lax.broadcasted_iota(jnp.int32, (1, 2 * HDIM), 1) < HDIM
    sq = x * x
    s_lo = jnp.sum(jnp.where(lo, sq, 0.0), axis=-1, keepdims=True)
    s_hi = jnp.sum(jnp.where(lo, 0.0, sq), axis=-1, keepdims=True)
    rs = jnp.where(lo, lax.rsqrt(s_lo / HDIM + EPS), lax.rsqrt(s_hi / HDIM + EPS))
    return x * rs * g2


def _cparams(sem, vmem=None):
    return pltpu.CompilerParams(dimension_semantics=sem, vmem_limit_bytes=vmem)


def _row(n):
    return pl.BlockSpec((1, n), lambda *_: (0, 0))


CS_ROWS = 8 * NDEV + 8


def _mod_part(mod_ref, row, part):
    pieces = []
    for j in range(NCHIP):
        lo, hi = max(part * DM, j * SHARD_ADA), min((part + 1) * DM, (j + 1) * SHARD_ADA)
        if lo < hi:
            pieces.append(mod_ref[j, row, lo - j * SHARD_ADA:hi - j * SHARD_ADA])
    return jnp.concatenate(pieces, axis=-1)


def inproj_fwd(chip, x, c_vec, c_ctx, w_ada, b_shard, norm_g, w_shard, wo_shard):
    tl = 1024
    nt = SEQ // tl
    halves = (DM // 2, SHARD_OUT // 2)
    n_w, n_c = 12, NDEV - 1

    def kern(k_ref, x_ref, cv_ref, cc_ref, wa_ref, b_ref, g_ref, w_ref, wo_ref,
             z_ref, h_ref, wfull_ref, wofull_ref, modall_ref, csall_ref,
             w_scr, wo_scr, h_scr, mine, cs_scr, mod_scr, shsc_scr, send_sems, recv_sems):
        s, t = pl.program_id(0), pl.program_id(1)
        xi, yi, c = _me()
        k, me = 2 * xi + yi, 4 * xi + 2 * yi + c
        sib = _flip(1)
        rows = pl.ds(pl.multiple_of(t * tl, tl), tl)
        gathered = (w_scr, wo_scr)
        slot = lambda d: pl.ds(pl.multiple_of(8 * d, 8), 8)

        def c_copy(q, owner):
            return _rcopy(mine, cs_scr.at[slot(owner), :], send_sems, recv_sems, n_w + q - 1, _flip(q))

        def m_copy(q, chip_of_block):
            return _rcopy(mod_scr.at[chip_of_block], mod_scr.at[chip_of_block], send_sems, recv_sems,
                          n_w + n_c + q // 2 - 1, _flip(q))

        def adaln():
            first = lax.broadcasted_iota(jnp.int32, (8, DM), 0) == 0
            mine[...] = jnp.where(first, jnp.broadcast_to(cv_ref[...], (8, DM)), 0.0)
            cs_scr[slot(me), :] = mine[...]
            cs_scr[slot(NDEV), :] = jnp.where(first, jnp.broadcast_to(cc_ref[...], (8, DM)), 0.0)
            for q in range(1, NDEV):
                c_copy(q, me).start()
            wa = wa_ref[...].astype(BF16)
            for q in range(1, NDEV):
                px, py, pc = _flip(q)
                c_copy(q, 4 * px + 2 * py + pc).wait_recv()
            act = jax.nn.silu(cs_scr[...]).astype(BF16)
            mod_scr[k] = jnp.dot(act, wa, preferred_element_type=F32) + b_ref[...]
            for q in (2, 4, 6):
                m_copy(q, k).start()
            for q in (2, 4, 6):
                m_copy(q, _chip_of(_flip(q))).wait_recv()
            row = pl.ds(8 * me, 1)
            shsc_scr[0:1, :] = _mod_part(mod_scr, row, 0)
            shsc_scr[1:2, :] = _mod_part(mod_scr, row, 1)
            pltpu.sync_copy(mod_scr, modall_ref)
            pltpu.sync_copy(cs_scr, csall_ref)

        def block(n, chip_of_block, hh):
            return gathered[n].at[chip_of_block, pl.ds(pl.multiple_of(hh * halves[n], halves[n]), halves[n]), :]

        def ici(n, q, chip_of_block):
            blk = block(n, chip_of_block, c)
            return _rcopy(blk, blk, send_sems, recv_sems, 6 * n + q // 2 - 1, _flip(q))

        def d2d(n, q, chip_of_block, hh):
            blk = block(n, chip_of_block, hh)
            return _rcopy(blk, blk, send_sems, recv_sems, 6 * n + 3 + q // 2 - 1, sib)

        @pl.when((s == 0) & (t == 0))
        def _():
            adaln()
            w_scr[k] = w_ref[...].astype(BF16)
            wo_scr[k] = wo_ref[...].astype(BF16)
            for q in (2, 4, 6):
                ici(0, q, k).start()
                ici(1, q, k).start()

        for sweep in (1, 2, 3):
            @pl.when((s == sweep) & (t == 0))
            def _():
                q = 2 * sweep
                src = _chip_of(_flip(q))
                for n in (0, 1):
                    ici(n, q, src).wait_recv()
                    d2d(n, q, src, c).start()
                for n in (0, 1):
                    d2d(n, q, src, 1 - c).wait_recv()

        @pl.when(s == 0)
        def _():
            hb = _modulated(x_ref[...], g_ref[...], shsc_scr[1:2, :], shsc_scr[0:1, :]).astype(BF16)
            h_scr[rows, :] = hb
            h_ref[...] = hb

        z_ref[...] = jnp.dot(h_scr[rows, :], w_scr[lax.bitwise_xor(k, s)], preferred_element_type=F32)

        @pl.when((s == NCHIP - 1) & (t == nt - 1))
        def _():
            for q in range(1, NDEV):
                c_copy(q, me).wait_send()
            for q in (2, 4, 6):
                m_copy(q, k).wait_send()
            for n in (0, 1):
                for q in (2, 4, 6):
                    ici(n, q, k).wait_send()
                    d2d(n, q, _chip_of(_flip(q)), c).wait_send()
            pltpu.sync_copy(w_scr, wfull_ref)
            pltpu.sync_copy(wo_scr, wofull_ref)

    once = lambda s, t, k: (jnp.where(s == 0, t, nt - 1), 0)
    hbm = pl.BlockSpec(memory_space=pl.ANY)
    n_sem = n_w + n_c + 3
    return pl.pallas_call(
        kern, name="inproj_fwd",
        grid_spec=pltpu.PrefetchScalarGridSpec(
            num_scalar_prefetch=1, grid=(NCHIP, nt),
            in_specs=[pl.BlockSpec((tl, DM), once)] + [_VMEM_SPEC] * 7,
            out_specs=[pl.BlockSpec((tl, SHARD_IN), lambda s, t, k: (t, lax.bitwise_xor(k[0], s))),
                       pl.BlockSpec((tl, DM), once), hbm, hbm, hbm, hbm],
            scratch_shapes=[pltpu.VMEM((NCHIP, DM, SHARD_IN), BF16), pltpu.VMEM((NCHIP, SHARD_OUT, DM), BF16),
                            pltpu.VMEM((SEQ, DM), BF16), pltpu.VMEM((8, DM), F32), pltpu.VMEM((CS_ROWS, DM), F32),
                            pltpu.VMEM((NCHIP, CS_ROWS, SHARD_ADA), F32), pltpu.VMEM((8, DM), F32),
                            pltpu.SemaphoreType.DMA((n_sem,)), pltpu.SemaphoreType.DMA((n_sem,))]),
        out_shape=[jax.ShapeDtypeStruct((SEQ, DIN), F32), jax.ShapeDtypeStruct((SEQ, DM), BF16),
                   jax.ShapeDtypeStruct((NCHIP, DM, SHARD_IN), BF16), jax.ShapeDtypeStruct((NCHIP, SHARD_OUT, DM), BF16),
                   jax.ShapeDtypeStruct((NCHIP, CS_ROWS, SHARD_ADA), F32), jax.ShapeDtypeStruct((CS_ROWS, DM), F32)],
        compiler_params=_cparams(("arbitrary", "arbitrary"), VMEM_BIG),
    )(chip, x, c_vec, c_ctx, w_ada, b_shard, norm_g, w_shard, wo_shard)


def ctx_fwd(ctx, cshift, cscale, norm_g, w_full):
    def kern(c_ref, sh_ref, sc_ref, g_ref, w2_ref, w3_ref, zc_ref, hc_ref):
        hc = _modulated(c_ref[...], g_ref[...], sc_ref[...], sh_ref[...]).astype(BF16)
        hc_ref[...] = hc
        zc_ref[:, :SHARD_IN] = jnp.dot(hc, w2_ref[0], preferred_element_type=F32)
        zc_ref[:, SHARD_IN:] = jnp.dot(hc, w3_ref[0], preferred_element_type=F32)

    return pl.pallas_call(
        kern, name="ctx_fwd", grid=(1,),
        in_specs=[pl.BlockSpec((CTX, DM), lambda i: (0, 0)), _row(DM), _row(DM), _row(DM),
                  pl.BlockSpec((1, DM, SHARD_IN), lambda i: (2, 0, 0)),
                  pl.BlockSpec((1, DM, SHARD_IN), lambda i: (3, 0, 0))],
        out_specs=[pl.BlockSpec((CTX, 2 * SHARD_IN), lambda i: (0, 0)),
                   pl.BlockSpec((CTX, DM), lambda i: (0, 0))],
        out_shape=[jax.ShapeDtypeStruct((CTX, 2 * SHARD_IN), F32), jax.ShapeDtypeStruct((CTX, DM), BF16)],
        compiler_params=_cparams(("arbitrary",)),
    )(ctx, cshift, cscale, norm_g, w_full, w_full)


SGU_CHUNK, SGU_PER_STEP = 128, 4


def _gelu(x):
    return 0.5 * x * (1.0 + lax.erf(x * 0.7071067811865476))


def _sgu_chunk(au, av, ag, sg, ws, bsb):
    u, v = _gelu(au), _gelu(av)
    outs = []
    for g in range(4):
        sl = slice(128 * g, 128 * (g + 1))
        mixed = mm(ws[g], _rms(v[:, sl], sg[:, sl])) + bsb[g]
        outs.append(u[:, sl] * mixed * jax.nn.silu(ag[:, sl]))
    return jnp.concatenate(outs, axis=-1)


def _sgu_specs():
    rows = SGU_CHUNK * SGU_PER_STEP
    zspec = lambda c: pl.BlockSpec((rows, 512), lambda n: (n, c))
    wspec = pl.BlockSpec((4, 128, 128), lambda n: (0, 0, 0))
    return rows, [zspec(0), zspec(1), zspec(2), _row(512), wspec, wspec]


def sgu_fwd(z, sg, ws, bsb):
    rows, in_specs = _sgu_specs()

    def kern(au_ref, av_ref, ag_ref, sg_ref, ws_ref, bs_ref, o_ref):
        for c in range(SGU_PER_STEP):
            sl = slice(c * SGU_CHUNK, (c + 1) * SGU_CHUNK)
            o_ref[sl, :] = _sgu_chunk(au_ref[sl, :], av_ref[sl, :], ag_ref[sl, :], sg_ref[...], ws_ref[...],
                                      bs_ref[...])

    return pl.pallas_call(
        kern, name="sgu_fwd", grid=(SEQ // rows,), in_specs=in_specs,
        out_specs=pl.BlockSpec((rows, 512), lambda n: (n, 0)),
        out_shape=jax.ShapeDtypeStruct((SEQ, 512), F32),
        compiler_params=_cparams(("arbitrary",)),
    )(z, z, z, sg, ws, bsb)


def sgu_bwd(z, sg, ws, bsb, dcat):
    rows, in_specs = _sgu_specs()

    def kern(au_ref, av_ref, ag_ref, sg_ref, ws_ref, bs_ref, do_ref, dz_ref, dsg_ref, dws_ref, dbs_ref):
        @pl.when(pl.program_id(0) == 0)
        def _():
            dsg_ref[...] = jnp.zeros_like(dsg_ref)
            dws_ref[...] = jnp.zeros_like(dws_ref)
            dbs_ref[...] = jnp.zeros_like(dbs_ref)

        for c in range(SGU_PER_STEP):
            sl = slice(c * SGU_CHUNK, (c + 1) * SGU_CHUNK)
            _, vjp = jax.vjp(_sgu_chunk, au_ref[sl, :], av_ref[sl, :], ag_ref[sl, :], sg_ref[...], ws_ref[...],
                             bs_ref[...])
            dau, dav, dag, dsg, dws, dbs = vjp(do_ref[sl, :])
            dz_ref[sl, 0:512] = dau.astype(BF16)
            dz_ref[sl, 512:1024] = dav.astype(BF16)
            dz_ref[sl, 1024:1536] = dag.astype(BF16)
            dsg_ref[...] += dsg
            dws_ref[...] += dws
            dbs_ref[...] += dbs

        @pl.when(pl.program_id(0) == pl.num_programs(0) - 1)
        def _():
            dbs_ref[...] = jnp.broadcast_to(jnp.sum(dbs_ref[...], axis=-1, keepdims=True), dbs_ref.shape)

    wspec = pl.BlockSpec((4, 128, 128), lambda n: (0, 0, 0))
    return pl.pallas_call(
        kern, name="sgu_bwd", grid=(SEQ // rows,),
        in_specs=in_specs + [pl.BlockSpec((rows, 512), lambda n: (n, 0))],
        out_specs=[pl.BlockSpec((rows, 1536), lambda n: (n, 0)), _row(512), wspec, wspec],
        out_shape=[jax.ShapeDtypeStruct((SEQ, 1536), BF16), jax.ShapeDtypeStruct((1, 512), F32),
                   jax.ShapeDtypeStruct((4, 128, 128), F32), jax.ShapeDtypeStruct((4, 128, 128), F32)],
        compiler_params=_cparams(("arbitrary",)),
    )(z, z, z, sg, ws, bsb, dcat)


_DR_OFF = (7, 3, -1)


def _row_valid(v, rr, j):
    return (j < 8, rr <= j < rr + 8, 4 <= j < 12)[v]


def _col_window():
    q = lax.broadcasted_iota(jnp.int32, (GRID_W, 128), 0)
    kc = lax.broadcasted_iota(jnp.int32, (GRID_W, 128), 1) % GRID_W
    c0 = jnp.clip(q - 8, 0, GRID_W - 16)
    return (kc >= c0) & (kc < c0 + 16)


def rpb_tables(rpb2):
    def kern(r_ref, b_ref):
        base = r_ref[0]
        lo = lax.broadcasted_iota(jnp.int32, (1, 128), 1) < GRID_W
        win = _col_window()
        tiles = {}
        for v in range(3):
            for rr in range(QROWS):
                for jp in range(KROWS // 2):
                    j0, j1 = 2 * jp, 2 * jp + 1
                    ok0, ok1 = _row_valid(v, rr, j0), _row_valid(v, rr, j1)
                    key = (j0 - rr + _DR_OFF[v], ok0, ok1) if (ok0 or ok1) else None
                    if key not in tiles:
                        if key is None:
                            tiles[key] = jnp.full((GRID_W, 128), NEG_INF, F32)
                        else:
                            d0 = key[0]
                            r0 = base[d0:d0 + 1, :] if ok0 else jnp.zeros((1, 128), F32)
                            r1 = base[d0 + 1:d0 + 2, :] if ok1 else jnp.zeros((1, 128), F32)
                            y = jnp.broadcast_to(jnp.where(lo, r0, r1), (GRID_W, 128))
                            y = pltpu.roll(pltpu.roll(y, 128 - 15, 1), 0, 1, stride=1, stride_axis=0)
                            tiles[key] = jnp.where(win & jnp.where(lo, ok0, ok1), y, NEG_INF)
                    b_ref[v, 0, rr * GRID_W:(rr + 1) * GRID_W, jp * 128:(jp + 1) * 128] = tiles[key]

    return pl.pallas_call(
        kern, name="rpb_tables", grid=(HEADS,),
        in_specs=[pl.BlockSpec((1, 15, 128), lambda h: (h, 0, 0))],
        out_specs=pl.BlockSpec((3, 1, QBLK, KBLK), lambda h: (0, h, 0, 0)),
        out_shape=jax.ShapeDtypeStruct((3, HEADS, QBLK, KBLK), F32),
        compiler_params=_cparams(("arbitrary",)),
    )(rpb2)


def rpb_bwd(dbias):
    def kern(g0_ref, g1_ref, g2_ref, o_ref):
        g_refs = (g0_ref.at[0], g1_ref.at[0], g2_ref.at[0])
        lo = lax.broadcasted_iota(jnp.int32, (1, 128), 1) < GRID_W
        ri = lax.broadcasted_iota(jnp.int32, (GRID_W, GRID_W), 0)
        ci = lax.broadcasted_iota(jnp.int32, (GRID_W, GRID_W), 1)
        flip = (ri + ci == GRID_W - 1).astype(F32)
        groups = {}
        for v in range(3):
            for rr in range(QROWS):
                for jp in range(KROWS // 2):
                    j0, j1 = 2 * jp, 2 * jp + 1
                    ok0, ok1 = _row_valid(v, rr, j0), _row_valid(v, rr, j1)
                    if not (ok0 or ok1):
                        continue
                    g = g_refs[v][0, rr * GRID_W:(rr + 1) * GRID_W, jp * 128:(jp + 1) * 128]
                    key = (j0 - rr + _DR_OFF[v], ok0, ok1)
                    groups[key] = g if key not in groups else groups[key] + g
        acc = [jnp.zeros((1, 128), F32) for _ in range(15)]
        for (d0, ok0, ok1), g in groups.items():
            g = lax.dot_general(flip, g, (((1,), (0,)), ((), ())), precision=lax.Precision.HIGHEST,
                                preferred_element_type=F32)
            g = pltpu.roll(pltpu.roll(g, 128 - 48, 1), 0, 1, stride=1, stride_axis=0)
            s = jnp.sum(g, axis=0, keepdims=True)
            if ok0:
                acc[d0] = acc[d0] + jnp.where(lo, s, 0.0)
            if ok1:
                acc[d0 + 1] = acc[d0 + 1] + jnp.where(lo, 0.0, s)
        for d in range(15):
            o_ref[0, d:d + 1, :] = acc[d] + pltpu.roll(acc[d], GRID_W, 1)

    return pl.pallas_call(
        kern, name="rpb_bwd", grid=(HEADS,),
        in_specs=[pl.BlockSpec((1, 1, QBLK, KBLK), functools.partial(lambda v, h: (v, h, 0, 0), v)) for v in range(3)],
        out_specs=pl.BlockSpec((1, 15, 128), lambda h: (h, 0, 0)),
        out_shape=jax.ShapeDtypeStruct((HEADS, 15, 128), F32),
        compiler_params=_cparams(("arbitrary",)),
    )(dbias, dbias, dbias)


def _scaled_q(q_raw, qg):
    return _pair_rms(q_raw, qg) * (HDIM ** -0.5)


def _head_lanes():
    lo = lax.broadcasted_iota(jnp.int32, (1, 2 * HDIM), 1) < HDIM
    return lo, jnp.logical_not(lo)


SOFTMAX_ROWS = 32


def _emit_interleaved(vector_work, matmul_work):
    for j in range(max(len(vector_work), len(matmul_work))):
        for work in (vector_work, matmul_work):
            if j < len(work):
                work[j]()


def _kblock(i):
    return jnp.clip(i - 1, 0, (SEQ - KBLK) // QBLK)


def _kstart(i):
    return pl.multiple_of(_kblock(i) * QBLK, QBLK)


ATTN_STEPS = NQBLK // 2
ATTN_ROWS = 2 * QBLK
KCOLS = QBLK


def _attn_in_specs():
    return [
        pl.BlockSpec((ATTN_ROWS, 128), lambda p, i: (i, ZQ + p)),
        pl.BlockSpec((SEQ, 128), lambda p, i: (0, ZK + p)),
        pl.BlockSpec((SEQ, 128), lambda p, i: (0, ZV + p)),
        pl.BlockSpec((ATTN_ROWS, 128), lambda p, i: (i, ZG + p)),
        pl.BlockSpec((CTX, 128), lambda p, i: (0, 2 + p)),
        pl.BlockSpec((CTX, 128), lambda p, i: (0, 6 + p)),
    ]


def _bias_specs():
    bias_spec = lambda variant: pl.BlockSpec((1, 2, QBLK, KBLK), lambda p, i: (variant(i), p, 0, 0))
    return [bias_spec(lambda i: jnp.where(i == 0, 0, 1)),
            bias_spec(lambda i: jnp.where(i == ATTN_STEPS - 1, 2, 1))]


def _prob_specs():
    return [pl.BlockSpec((2, ATTN_ROWS, KBLK), lambda p, i: (p, i, 0)),
            pl.BlockSpec((2, ATTN_ROWS, CTX), lambda p, i: (p, i, 0))]


NORM_ROWS = 512


def _norm_keys(k_ref, ck_ref, kg_ref, kn_scr, ckn_scr):
    def body(c, carry):
        sl = pl.ds(pl.multiple_of(c * NORM_ROWS, NORM_ROWS), NORM_ROWS)
        kn_scr[sl, :] = _pair_rms(k_ref[sl, :], kg_ref[...]).astype(BF16)
        return carry

    lax.fori_loop(0, SEQ // NORM_ROWS, body, 0)
    ckn_scr[...] = _pair_rms(ck_ref[...], kg_ref[...]).astype(BF16)


def _values_with_ones(v_ref, cv_ref, v1_scr, cv1_scr):
    for a, mine in enumerate(_head_lanes()):
        def body(c, carry):
            sl = pl.ds(pl.multiple_of(c * NORM_ROWS, NORM_ROWS), NORM_ROWS)
            v1_scr[a, sl, :] = jnp.where(mine, v_ref[sl, :], 1.0).astype(BF16)
            return carry

        lax.fori_loop(0, SEQ // NORM_ROWS, body, 0)
        cv1_scr[a] = jnp.where(mine, cv_ref[...], 1.0).astype(BF16)


def _pair_major_spec():
    return pl.BlockSpec((1, ATTN_ROWS, 128), lambda p, i: (p, i, 0))


def _normed_key_specs():
    return [pl.BlockSpec((None, SEQ, 128), lambda p, i: (p, 0, 0)), pl.BlockSpec((None, CTX, 128), lambda p, i: (p, 0, 0))]


def attn_fwd(z, zc, bias, qg2, kg2):
    def kern(q_ref, k_ref, v_ref, bg_ref, ck_ref, cv_ref, be_ref, bo_ref, qg_ref, kg_ref,
             ob_ref, o_ref, rden_ref, pl_ref, pc_ref, kn_ref, ckn_ref, kn_scr, ckn_scr, v1_scr, cv1_scr, s_scr):
        i = pl.program_id(1)

        @pl.when(i == 0)
        def _():
            _norm_keys(k_ref, ck_ref, kg_ref, kn_scr, ckn_scr)
            kn_ref[...] = kn_scr[...]
            ckn_ref[...] = ckn_scr[...]
            _values_with_ones(v_ref, cv_ref, v1_scr, cv1_scr)

        heads = _head_lanes()
        bias_refs = (be_ref, bo_ref)
        tiles = [(b, a) for b in range(2) for a in range(2)]
        rows = [slice(b * QBLK, (b + 1) * QBLK) for b in range(2)]
        qn = [_scaled_q(q_ref[rows[b], :], qg_ref[...]) for b in range(2)]
        qa = [jnp.where(heads[a], qn[b], 0.0).astype(BF16) for b, a in tiles]
        pv = [None] * len(tiles)
        done = {}
        latent = KBLK // KCOLS

        def keys(b, n):
            return pl.ds(pl.multiple_of(_kstart(2 * i + b) + n * KCOLS, KCOLS), KCOLS)

        def score_piece(t, n):
            b, a = tiles[t]
            cols = slice(n * KCOLS, (n + 1) * KCOLS)
            if n < latent:
                s_scr[t, :, cols] = mm_nt(qa[t], kn_scr[keys(b, n), :]) + bias_refs[b][0, a, :, cols]
            else:
                s_scr[t, :, cols] = mm_nt(qa[t], ckn_scr[...])

        def softmax_rows(t, r):
            b, a = tiles[t]
            rs = slice(r * SOFTMAX_ROWS, (r + 1) * SOFTMAX_ROWS)
            out_rows = slice(b * QBLK + rs.start, b * QBLK + rs.stop)
            s = s_scr[t, rs, :]
            p = jnp.exp(s - jnp.max(s, axis=-1, keepdims=True)).astype(BF16)
            pl_ref[a, out_rows, :] = p[:, :KBLK]
            pc_ref[a, out_rows, :] = p[:, KBLK:]

        def value_piece(t, n):
            b, a = tiles[t]
            if n < latent:
                part = mm(pl_ref[a, rows[b], n * KCOLS:(n + 1) * KCOLS], v1_scr[a, keys(b, n), :])
            else:
                part = mm(pc_ref[a, rows[b], :], cv1_scr[a])
            pv[t] = part if pv[t] is None else pv[t] + part
            if n == latent:
                finish(t)

        def finish(t):
            b, a = tiles[t]
            r = jnp.where(heads[a], pltpu.roll(1.0 / pv[t], HDIM, 1), 0.0)
            done[t] = (pv[t] * r, r)
            if a == 1:
                o, rden = (lo + hi for lo, hi in zip(done[t - 1], done[t]))
                ob_ref[rows[b], :] = o * jax.nn.silu(bg_ref[rows[b], :])
                o_ref[0, rows[b], :] = o
                rden_ref[0, rows[b], :] = rden

        pieces = range(latent + 1)
        for n in pieces:
            score_piece(0, n)
        for t in range(len(tiles)):
            matmuls = []
            for n in pieces:
                if t + 1 < len(tiles):
                    matmuls.append(functools.partial(score_piece, t + 1, n))
                if t > 0:
                    matmuls.append(functools.partial(value_piece, t - 1, n))
            _emit_interleaved([functools.partial(softmax_rows, t, r) for r in range(QBLK // SOFTMAX_ROWS)], matmuls)
        for n in pieces:
            value_piece(len(tiles) - 1, n)

    qblk = pl.BlockSpec((ATTN_ROWS, 128), lambda p, i: (i, p))
    return pl.pallas_call(
        kern, name="attn_fwd", grid=(NPAIR, ATTN_STEPS),
        in_specs=_attn_in_specs() + _bias_specs() + [_row(128), _row(128)],
        out_specs=[qblk, _pair_major_spec(), _pair_major_spec()] + _prob_specs() + _normed_key_specs(),
        out_shape=[jax.ShapeDtypeStruct((SEQ, 512), F32)] + [jax.ShapeDtypeStruct((NPAIR, SEQ, 128), F32)] * 2
        + [jax.ShapeDtypeStruct((HEADS, SEQ, KBLK), BF16), jax.ShapeDtypeStruct((HEADS, SEQ, CTX), BF16),
           jax.ShapeDtypeStruct((NPAIR, SEQ, 128), BF16), jax.ShapeDtypeStruct((NPAIR, CTX, 128), BF16)],
        scratch_shapes=[pltpu.VMEM((SEQ, 128), BF16), pltpu.VMEM((CTX, 128), BF16),
                        pltpu.VMEM((2, SEQ, 128), BF16), pltpu.VMEM((2, CTX, 128), BF16),
                        pltpu.VMEM((4, QBLK, KBLK + CTX), F32)],
        compiler_params=_cparams(("arbitrary", "arbitrary"), 40 * 1024 * 1024),
    )(z, z, z, z, zc, zc, bias, bias, qg2, kg2)


def attn_bwd(z, zc, qg2, kg2, dcat, saved):
    def kern(q_ref, k_ref, v_ref, bg_ref, ck_ref, cv_ref, qg_ref, kg_ref, do_ref, o_ref, rden_ref, pl_ref, pc_ref,
             kn_scr, ckn_scr, dq_ref, dk_ref, dv_ref, dbg_ref, dck_ref, dcv_ref, db_ref, dqg_ref, dkg_ref,
             v_scr, cv_scr, dknt_scr, dvt_scr, dcknt_scr, dcvt_scr, dp_scr, ds_scr):
        p, i = pl.program_id(0), pl.program_id(1)
        last = i == ATTN_STEPS - 1

        @pl.when(i == 0)
        def _():
            def body(c, carry):
                sl = pl.ds(pl.multiple_of(c * NORM_ROWS, NORM_ROWS), NORM_ROWS)
                v_scr[sl, :] = v_ref[sl, :].astype(BF16)
                return carry

            lax.fori_loop(0, SEQ // NORM_ROWS, body, 0)
            cv_scr[...] = cv_ref[...].astype(BF16)
            for acc in (dknt_scr, dvt_scr, dcknt_scr, dcvt_scr, db_ref):
                acc[...] = jnp.zeros_like(acc)

        @pl.when((i == 0) & (p == 0))
        def _():
            dqg_ref[...] = jnp.zeros_like(dqg_ref)
            dkg_ref[...] = jnp.zeros_like(dkg_ref)

        heads = _head_lanes()
        tiles = [(b, a) for b in range(2) for a in range(2)]
        rows = [slice(b * QBLK, (b + 1) * QBLK) for b in range(2)]
        kb = [_kblock(2 * i + b) for b in range(2)]
        variant = [jnp.where(i == 0, 0, 1), jnp.where(last, 2, 1)]
        latent = KBLK // KCOLS

        def keys(b, n):
            return pl.ds(pl.multiple_of((kb[b] + n) * KCOLS, KCOLS), KCOLS)

        gated = []
        for b in range(2):
            bg, dout, o = bg_ref[rows[b], :], do_ref[rows[b], :], o_ref[0, rows[b], :]
            sig = jax.nn.sigmoid(bg)
            do = dout * (bg * sig)
            dbg_ref[rows[b], :] = (dout * o * (sig * (1.0 + bg * (1.0 - sig)))).astype(BF16)
            rden = rden_ref[0, rows[b], :]
            dr = do * rden
            qn = _scaled_q(q_ref[rows[b], :], qg_ref[...])
            gated.append((dr, dr.T.astype(BF16), qn.T.astype(BF16), do * o * rden))

        feats = [slice(a * HDIM, (a + 1) * HDIM) for a in range(2)]
        doa, doa_t, qa_t, delta = [], [], [], []
        for b, a in tiles:
            dr, dr_t, qn_t, weighted = gated[b]
            doa.append(jnp.where(heads[a], dr, 0.0).astype(BF16))
            doa_t.append(dr_t[feats[a], :])
            qa_t.append(qn_t[feats[a], :])
            delta.append(jnp.sum(jnp.where(heads[a], weighted, 0.0), axis=-1, keepdims=True))
        dqn = [None] * len(tiles)

        def cols(n):
            return slice(n * KCOLS, (n + 1) * KCOLS)

        def stage_a(t, n):
            b, a = tiles[t]
            if n < latent:
                dp_scr[t, :, cols(n)] = mm_nt(doa[t], v_scr[keys(b, n), :])
                dvt_scr[kb[b] + n, feats[a], :] += mm(doa_t[t], pl_ref[a, rows[b], cols(n)])
            else:
                dp_scr[t, :, cols(n)] = mm_nt(doa[t], cv_scr[...])
                dcvt_scr[feats[a], :] += mm(doa_t[t], pc_ref[a, rows[b], :])

        def stage_b(t, r):
            b, a = tiles[t]
            rs = slice(r * SOFTMAX_ROWS, (r + 1) * SOFTMAX_ROWS)
            in_rows = slice(b * QBLK + rs.start, b * QBLK + rs.stop)
            d = dp_scr[t, rs, :] - delta[t][rs, :]
            ds_lat = pl_ref[a, in_rows, :].astype(F32) * d[:, :KBLK]
            ds_ctx = pc_ref[a, in_rows, :].astype(F32) * d[:, KBLK:]
            db_ref[variant[b], a, rs, :] += ds_lat
            ds_scr[t, rs, :KBLK] = ds_lat.astype(BF16)
            ds_scr[t, rs, KBLK:] = ds_ctx.astype(BF16)

        def stage_c(t, n):
            b, a = tiles[t]
            ds = ds_scr[t, :, cols(n)]
            if n < latent:
                part = mm(ds, kn_scr[keys(b, n), :])
                dknt_scr[kb[b] + n, feats[a], :] += mm(qa_t[t], ds)
            else:
                part = mm(ds, ckn_scr[...])
                dcknt_scr[feats[a], :] += mm(qa_t[t], ds)
            dqn[t] = part if dqn[t] is None else dqn[t] + part
            if n == latent and a == 1:
                both = jnp.where(heads[0], dqn[t - 1], 0.0) + jnp.where(heads[1], dqn[t], 0.0)
                dq, dqg = jax.vjp(_scaled_q, q_ref[rows[b], :], qg_ref[...])[1](both)
                dq_ref[rows[b], :] = dq.astype(BF16)
                dqg_ref[...] += dqg

        pieces = range(latent + 1)
        for n in pieces:
            stage_a(0, n)
        for t in range(len(tiles)):
            matmuls = []
            for n in pieces:
                if t + 1 < len(tiles):
                    matmuls.append(functools.partial(stage_a, t + 1, n))
                if t > 0:
                    matmuls.append(functools.partial(stage_c, t - 1, n))
            _emit_interleaved([functools.partial(stage_b, t, r) for r in range(QBLK // SOFTMAX_ROWS)], matmuls)
        for n in pieces:
            stage_c(len(tiles) - 1, n)

        @pl.when(last)
        def _():
            eye = (lax.broadcasted_iota(jnp.int32, (KCOLS, KCOLS), 0)
                   == lax.broadcasted_iota(jnp.int32, (KCOLS, KCOLS), 1)).astype(BF16)

            def turned(x):
                hi = x.astype(BF16)
                return mm_nt(eye, hi) + mm_nt(eye, x - hi.astype(F32))

            def body(c, dkg):
                sl = pl.ds(pl.multiple_of(c * NORM_ROWS, NORM_ROWS), NORM_ROWS)
                blocks = range(NORM_ROWS // KCOLS)
                dkn = jnp.concatenate([turned(dknt_scr[c * len(blocks) + n]) for n in blocks], axis=0)
                dv = jnp.concatenate([mm_nt(eye, dvt_scr[c * len(blocks) + n]) for n in blocks], axis=0)
                _, nvjp = jax.vjp(_pair_rms, k_ref[sl, :], kg_ref[...])
                dk, dg = nvjp(dkn)
                dk_ref[sl, :] = dk.astype(BF16)
                dv_ref[sl, :] = dv.astype(BF16)
                return dkg + dg

            dkg = lax.fori_loop(0, SEQ // NORM_ROWS, body, jnp.zeros((1, 128), F32))
            _, nvjp = jax.vjp(_pair_rms, ck_ref[...], kg_ref[...])
            dck, dg = nvjp(dcknt_scr[...].T)
            dck_ref[...] = dck
            dcv_ref[...] = dcvt_scr[...].T
            dkg_ref[...] += dkg + dg

        @pl.when(last & (p == NPAIR - 1))
        def _():
            dqg_ref[...] = dqg_ref[...] + pltpu.roll(dqg_ref[...], HDIM, 1)
            dkg_ref[...] = dkg_ref[...] + pltpu.roll(dkg_ref[...], HDIM, 1)

    blk = lambda rows: pl.BlockSpec((rows, 128), lambda p, i: (0, p))
    qblk = pl.BlockSpec((ATTN_ROWS, 128), lambda p, i: (i, p))
    return pl.pallas_call(
        kern, name="attn_bwd", grid=(NPAIR, ATTN_STEPS),
        in_specs=_attn_in_specs() + [_row(128), _row(128), pl.BlockSpec((ATTN_ROWS, 128), lambda p, i: (i, 4 + p)),
                                     _pair_major_spec(), _pair_major_spec()] + _prob_specs() + _normed_key_specs(),
        out_specs=[qblk, blk(SEQ), blk(SEQ), qblk, blk(CTX), blk(CTX),
                   pl.BlockSpec((3, 2, QBLK, KBLK), lambda p, i: (0, p, 0, 0)), _row(128), _row(128)],
        out_shape=[jax.ShapeDtypeStruct((SEQ, 512), BF16)] * 4 + [jax.ShapeDtypeStruct((CTX, 512), F32)] * 2
        + [jax.ShapeDtypeStruct((3, HEADS, QBLK, KBLK), F32)]
        + [jax.ShapeDtypeStruct((1, 128), F32), jax.ShapeDtypeStruct((1, 128), F32)],
        scratch_shapes=[pltpu.VMEM((SEQ, 128), BF16), pltpu.VMEM((CTX, 128), BF16),
                        pltpu.VMEM((SEQ // KCOLS, 128, KCOLS), F32), pltpu.VMEM((SEQ // KCOLS, 128, KCOLS), F32),
                        pltpu.VMEM((128, CTX), F32), pltpu.VMEM((128, CTX), F32),
                        pltpu.VMEM((4, QBLK, KBLK + CTX), F32), pltpu.VMEM((4, QBLK, KBLK + CTX), BF16)],
        compiler_params=_cparams(("arbitrary", "arbitrary"), VMEM_BIG),
    )(z, z, z, z, zc, zc, qg2, kg2, dcat, *saved)


def outproj(out_a, out_b, x, target, gate, wo):
    tl = 512

    def kern(a_ref, b_ref, x_ref, t_ref, g_ref, w_ref, loss_ref, dy_ref, dcat_ref, dg_ref, dw_ref):
        @pl.when(pl.program_id(0) == 0)
        def _():
            loss_ref[...] = jnp.zeros_like(loss_ref)
            dg_ref[...] = jnp.zeros_like(dg_ref)
            dw_ref[...] = jnp.zeros_like(dw_ref)

        a, b = a_ref[...].astype(BF16), b_ref[...].astype(BF16)
        mix = (jnp.dot(a, w_ref[0:512, :], preferred_element_type=F32)
               + jnp.dot(b, w_ref[512:1024, :], preferred_element_type=F32))
        err = x_ref[...] + g_ref[...] * mix - t_ref[...]
        loss_ref[...] += 0.5 * jnp.sum(jnp.mean(err * err, axis=-1))
        dy = err * (1.0 / DM)
        dy_ref[...] = dy
        dg_ref[...] += jnp.sum(dy * mix, axis=0, keepdims=True)
        dmix = (g_ref[...] * dy).astype(BF16)
        dcat_ref[...] = lax.dot_general(dmix, w_ref[...], (((1,), (1,)), ((), ())), preferred_element_type=F32)
        dw_ref[0:512, :] += lax.dot_general(a, dmix, (((0,), (0,)), ((), ())), preferred_element_type=F32)
        dw_ref[512:1024, :] += lax.dot_general(b, dmix, (((0,), (0,)), ((), ())), preferred_element_type=F32)

    tile = lambda w: pl.BlockSpec((tl, w), lambda t: (t, 0))
    whole = pl.BlockSpec((DM, DM), lambda t: (0, 0))
    return pl.pallas_call(
        kern, name="outproj", grid=(SEQ // tl,),
        in_specs=[tile(512), tile(512), tile(DM), tile(DM), _row(DM), whole],
        out_specs=[pl.BlockSpec((8, 128), lambda t: (0, 0)), tile(DM), tile(DM), _row(DM), whole],
        out_shape=[jax.ShapeDtypeStruct((8, 128), F32), jax.ShapeDtypeStruct((SEQ, DM), F32),
                   jax.ShapeDtypeStruct((SEQ, DM), F32), jax.ShapeDtypeStruct((1, DM), F32),
                   jax.ShapeDtypeStruct((DM, DM), F32)],
        compiler_params=_cparams(("arbitrary",), 48 * 1024 * 1024),
    )(out_a, out_b, x, target, gate, wo)


def _pieces(sources):
    out = []
    for name, c0, c1 in sources:
        for j in range(NCHIP):
            lo, hi = max(c0, j * SHARD_IN), min(c1, (j + 1) * SHARD_IN)
            if lo < hi:
                out.append((j, lo - j * SHARD_IN, hi - j * SHARD_IN, name, lo - c0, hi - c0))
    return out


DZ_PIECES = _pieces((("a", 0, 1536), ("q", 1536, 2048), ("k", 2048, 2560), ("v", 2560, 3072), ("g", 3072, DIN)))
DZC_PIECES = _pieces((("k", 2048, 2560), ("v", 2560, 3072)))
_NT = (((1,), (1,)), ((), ()))


DH_SUBTILES = 2


def _dz_specs(tl):
    return [pl.BlockSpec((tl, 1536), lambda t: (t, 0))] + [pl.BlockSpec((tl, 512), lambda t: (t, 0))] * 4


def dh_bwd(dz_parts, w_full, x, dy, shift, scale, norm_g, dg_ctx):
    tl = 512
    nt = SEQ // tl

    def kern(a_ref, q_ref, k_ref, v_ref, g_ref, w_ref, x_ref, dy_ref, sh_ref, sc_ref, gn_ref, dgc_ref,
             gx_ref, dsh_ref, dsc_ref, dg_ref):
        @pl.when(pl.program_id(0) == 0)
        def _():
            dsh_ref[...] = jnp.zeros_like(dsh_ref)
            dsc_ref[...] = jnp.zeros_like(dsc_ref)
            dg_ref[...] = dgc_ref[...]

        src = dict(a=a_ref, q=q_ref, k=k_ref, v=v_ref, g=g_ref)
        for sub in range(DH_SUBTILES):
            rows = slice(sub * tl // DH_SUBTILES, (sub + 1) * tl // DH_SUBTILES)
            dh = None
            for j, l0, l1, name, s0, s1 in DZ_PIECES:
                part = lax.dot_general(src[name][rows, s0:s1], w_ref[j, :, l0:l1], _NT, preferred_element_type=F32)
                dh = part if dh is None else dh + part
            _, vjp = jax.vjp(_modulated, x_ref[rows, :], gn_ref[...], sc_ref[...], sh_ref[...])
            dx, dg, dsc, dsh = vjp(dh)
            gx_ref[rows, :] = dy_ref[rows, :] + dx
            dg_ref[...] += dg
            dsc_ref[...] += dsc
            dsh_ref[...] += dsh

    tile = pl.BlockSpec((tl, DM), lambda t: (t, 0))
    return pl.pallas_call(
        kern, name="dh_bwd", grid=(nt,),
        in_specs=_dz_specs(tl) + [pl.BlockSpec((NCHIP, DM, SHARD_IN), lambda t: (0, 0, 0)), tile, tile, _row(DM),
                                  _row(DM), _row(DM), _row(DM)],
        out_specs=[tile, _row(DM), _row(DM), _row(DM)],
        out_shape=[jax.ShapeDtypeStruct((SEQ, DM), F32)] + [jax.ShapeDtypeStruct((1, DM), F32)] * 3,
        compiler_params=_cparams(("arbitrary",), 48 * 1024 * 1024),
    )(*dz_parts, w_full, x, dy, shift, scale, norm_g, dg_ctx)


def dw_bwd(h, dz_parts, hc, dck, dcv, g_out):
    tl = 512
    nt = SEQ // tl
    (rhi, wi), (rho, wo) = RS_SHAPES

    def kern(h_ref, a_ref, q_ref, k_ref, v_ref, g_ref, hc_ref, dck_ref, dcv_ref, go_hbm,
             wire_i, keep_i, wire_o, keep_o, acc, rcv_i, mine_o, rcv_o, load_sem, send_sems, recv_sems):
        t = pl.program_id(0)
        x, y, c = _me()
        k = 2 * x + y
        sib = _flip(1)
        half = lambda hh, rh: pl.ds(pl.multiple_of(hh * rh, rh), rh)
        load_o = pltpu.make_async_copy(go_hbm.at[:, half(c, rho), :], mine_o, load_sem)
        pair_o = _rcopy(go_hbm.at[:, half(1 - c, rho), :], rcv_o, send_sems, recv_sems, 0, sib)
        pair_i = _rcopy(acc.at[:, half(1 - c, rhi), :], rcv_i, send_sems, recv_sems, 1, sib)

        @pl.when(t == 0)
        def _():
            load_o.start()
            pair_o.start()
            acc[...] = jnp.zeros_like(acc)
            hct = hc_ref[...].T
            csrc = dict(k=dck_ref, v=dcv_ref)
            for j, l0, l1, name, s0, s1 in DZC_PIECES:
                acc[j, :, l0:l1] += jnp.dot(hct, csrc[name][:, s0:s1].astype(BF16), preferred_element_type=F32)

        ht = h_ref[...].T
        src = dict(a=a_ref, q=q_ref, k=k_ref, v=v_ref, g=g_ref)
        for j, l0, l1, name, s0, s1 in DZ_PIECES:
            acc[j, :, l0:l1] += jnp.dot(ht, src[name][:, s0:s1], preferred_element_type=F32)

        @pl.when(t == nt - 1)
        def _():
            pair_i.start()
            load_o.wait()
            pair_o.wait_recv()
            for j in range(NCHIP):
                wire_o[j] = (mine_o[j] + rcv_o[j]).astype(BF16)
            keep_o[...] = mine_o[k] + rcv_o[k]
            pair_i.wait_recv()
            mine = half(c, rhi)
            for j in range(NCHIP):
                wire_i[j] = (acc[j, mine, :] + rcv_i[j]).astype(BF16)
            keep_i[...] = acc[k, mine, :] + rcv_i[k]
            pair_o.wait_send()
            pair_i.wait_send()

    whole = lambda *shape: pl.BlockSpec(shape, lambda t: (0,) * len(shape))
    return pl.pallas_call(
        kern, name="dw_bwd", grid=(nt,),
        in_specs=[pl.BlockSpec((tl, DM), lambda t: (t, 0))] + _dz_specs(tl)
        + [whole(CTX, DM), whole(CTX, 512), whole(CTX, 512), pl.BlockSpec(memory_space=pl.ANY)],
        out_specs=[whole(NCHIP, rhi, wi), whole(rhi, wi), whole(NCHIP, rho, wo), whole(rho, wo)],
        out_shape=[jax.ShapeDtypeStruct((NCHIP, rhi, wi), BF16), jax.ShapeDtypeStruct((rhi, wi), F32),
                   jax.ShapeDtypeStruct((NCHIP, rho, wo), BF16), jax.ShapeDtypeStruct((rho, wo), F32)],
        scratch_shapes=[pltpu.VMEM((NCHIP, DM, SHARD_IN), F32), pltpu.VMEM((NCHIP, rhi, wi), F32),
                        pltpu.VMEM((NCHIP, rho, wo), F32), pltpu.VMEM((NCHIP, rho, wo), F32),
                        pltpu.SemaphoreType.DMA(()), pltpu.SemaphoreType.DMA((2,)), pltpu.SemaphoreType.DMA((2,))],
        compiler_params=_cparams(("arbitrary",), VMEM_BIG),
    )(h, *dz_parts, hc, dck, dcv, g_out)


def ctx_bwd(dck, dcv, w_full, ctx, cshift, cscale, norm_g):
    def kern(dck_ref, dcv_ref, w_ref, c_ref, sh_ref, sc_ref, g_ref, dsh_ref, dsc_ref, dg_ref):
        csrc = dict(k=dck_ref, v=dcv_ref)
        dhc = None
        for j, l0, l1, name, s0, s1 in DZC_PIECES:
            part = lax.dot_general(csrc[name][:, s0:s1].astype(BF16), w_ref[j, :, l0:l1], _NT,
                                   preferred_element_type=F32)
            dhc = part if dhc is None else dhc + part
        _, vjp = jax.vjp(lambda g, sc, sh: _modulated(c_ref[...], g, sc, sh), g_ref[...], sc_ref[...], sh_ref[...])
        dg_ref[...], dsc_ref[...], dsh_ref[...] = vjp(dhc)

    whole = lambda r, c: pl.BlockSpec((r, c), lambda i: (0, 0))
    return pl.pallas_call(
        kern, name="ctx_bwd", grid=(1,),
        in_specs=[whole(CTX, 512), whole(CTX, 512), pl.BlockSpec((NCHIP, DM, SHARD_IN), lambda i: (0, 0, 0)),
                  whole(CTX, DM), _row(DM), _row(DM), _row(DM)],
        out_specs=[_row(DM), _row(DM), _row(DM)],
        out_shape=[jax.ShapeDtypeStruct((1, DM), F32)] * 3,
        compiler_params=_cparams(("arbitrary",), 40 * 1024 * 1024),
    )(dck, dcv, w_full, ctx, cshift, cscale, norm_g)


def _lane_pad_rpb(rpb):
    r = jnp.pad(rpb, ((0, 0), (0, 0), (0, GRID_W - rpb.shape[-1])))
    return jnp.concatenate([r, r], axis=-1)


def local_step(chip, dev, x, c_vec, c_ctx, w_ada, b_shard, ctx, target, norm_g, sgu_g, w_s, b_s, q_g, k_g, rpb,
               w_in_shard, w_out_shard):
    bsb = jnp.broadcast_to(b_s[:, :, None], (4, 128, 128))
    qg2, kg2 = jnp.tile(q_g, (1, 2)), jnp.tile(k_g, (1, 2))

    z, h, w_in_full, w_out_full, mod_all, cs = inproj_fwd(chip, x, c_vec, c_ctx, w_ada, b_shard, norm_g, w_in_shard,
                                                          w_out_shard)
    mods = mod_all.transpose(1, 0, 2).reshape(CS_ROWS, 3 * DM)
    mod = lax.dynamic_slice(mods, (8 * dev, 0), (1, 3 * DM))
    shift, scale, gate = mod[:, :DM], mod[:, DM:2 * DM], mod[:, 2 * DM:]
    cshift, cscale = mods[8 * NDEV:8 * NDEV + 1, :DM], mods[8 * NDEV:8 * NDEV + 1, DM:2 * DM]
    zc, hc = ctx_fwd(ctx, cshift, cscale, norm_g, w_in_full)
    bias = rpb_tables(_lane_pad_rpb(rpb))
    out_a = sgu_fwd(z, sgu_g, w_s, bsb)
    out_b, *saved = attn_fwd(z, zc, bias, qg2, kg2)
    loss8, dy, dcat, dgate, dwo = outproj(out_a, out_b, x, target, gate, w_out_full.reshape(DM, DM))
    dz_a, dsg, dws, dbsb = sgu_bwd(z, sgu_g, w_s, bsb, dcat)
    dq, dk, dv, dbg, dck, dcv, dbias, dqg2, dkg2 = attn_bwd(z, zc, qg2, kg2, dcat, saved)
    drpb = rpb_bwd(dbias)[:, :, :rpb.shape[-1]]
    dz_parts = (dz_a, dq, dk, dv, dbg)
    dcshift, dcscale, dng_c = ctx_bwd(dck, dcv, w_in_full, ctx, cshift, cscale, norm_g)
    wire_i, keep_i, wire_o, keep_o = dw_bwd(h, dz_parts, hc, dck, dcv, dwo.reshape(NCHIP, SHARD_OUT, DM))
    *in_flight, token = rs_start(wire_i, wire_o)
    grad_x, dshift, dscale, dng = dh_bwd(dz_parts, w_in_full, x, dy, shift, scale, norm_g, dng_c + token[0, 0])
    got_i, got_o = rs_wait(*in_flight, dshift)
    return dict(
        loss=loss8[0:1, 0:1], grad_x=grad_x, rs=(keep_i, got_i, keep_o, got_o), cs=cs,
        dmod=jnp.concatenate([dshift, dscale, dgate], axis=-1),
        dcmod=jnp.concatenate([dcshift, dcscale, jnp.zeros((1, DM), F32)], axis=-1),
        d_norm_g=dng, d_sgu_g=dsg, d_w_s=dws, d_b_s=dbsb[:, :, 0],
        d_q_g=dqg2[:, :HDIM], d_k_g=dkg2[:, :HDIM], d_rpb=drpb)


def _me():
    return lax.axis_index("x"), lax.axis_index("y"), lax.axis_index("c")


def _flip(q):
    x, y, c = _me()
    return ((1 - x) if q & 4 else x, (1 - y) if q & 2 else y, (1 - c) if q & 1 else c)


def _chip_of(dev):
    return 2 * dev[0] + dev[1]


def _rcopy(src, dst, send_sems, recv_sems, k, dev):
    return pltpu.make_async_remote_copy(src_ref=src, dst_ref=dst, send_sem=send_sems.at[k], recv_sem=recv_sems.at[k],
                                        device_id=dev, device_id_type=MESH_ID)


_VMEM_SPEC = pl.BlockSpec(memory_space=pltpu.VMEM)
SLAB_ROWS = 80


RS_SHAPES = ((DM // 2, SHARD_IN), (SHARD_OUT // 2, DM))
_HBM_SPEC = pl.BlockSpec(memory_space=pltpu.HBM)
_SEM_SPEC = pl.BlockSpec(memory_space=pltpu.SEMAPHORE)
_IN_FLIGHT = pltpu.SideEffectType.DATAFLOW_SIDE_EFFECTING


def _rs_copies(wires, lands, send_sems, recv_sems):
    return [pltpu.make_async_remote_copy(
        src_ref=wires[n].at[_chip_of(_flip(q))], dst_ref=lands[n].at[q // 2 - 1],
        send_sem=send_sems.at[3 * n + q // 2 - 1], recv_sem=recv_sems.at[3 * n + q // 2 - 1],
        device_id=_flip(q), device_id_type=MESH_ID) for n in (0, 1) for q in (2, 4, 6)]


def rs_start(wire_i, wire_o):
    lands = [lax.empty((NCHIP - 1, rh, w), BF16) for rh, w in RS_SHAPES]

    def body(wi_ref, wo_ref, li_ref, lo_ref, send_sems, recv_sems, wi_thru, wo_thru, li_thru, lo_thru, token):
        for cp in _rs_copies((wi_ref, wo_ref), (li_ref, lo_ref), send_sems, recv_sems):
            cp.start()
        token[...] = jnp.zeros_like(token)

    hbm = lambda a: pltpu.HBM(a.shape, a.dtype)
    return pl.pallas_call(
        body, name="rs_start",
        out_shape=(pltpu.SemaphoreType.DMA((6,)), pltpu.SemaphoreType.DMA((6,)), hbm(wire_i), hbm(wire_o),
                   hbm(lands[0]), hbm(lands[1]), jax.ShapeDtypeStruct((8, 128), F32)),
        in_specs=(_HBM_SPEC,) * 4, out_specs=(_SEM_SPEC, _SEM_SPEC) + (_HBM_SPEC,) * 4 + (_VMEM_SPEC,),
        input_output_aliases={0: 2, 1: 3, 2: 4, 3: 5},
        compiler_params=pltpu.CompilerParams(has_side_effects=_IN_FLIGHT),
    )(*[pltpu.with_memory_space_constraint(a, pltpu.HBM) for a in (wire_i, wire_o, *lands)])


def rs_wait(send_sems, recv_sems, wire_i, wire_o, land_i, land_o, after):
    def body(wi_ref, wo_ref, li_ref, lo_ref, send_sems, recv_sems, after_ref, wi_dead, wo_dead, gi_ref, go_ref):
        for cp in _rs_copies((wi_ref, wo_ref), (li_ref, lo_ref), send_sems, recv_sems):
            cp.wait_send()
            cp.wait_recv()

    hbm = lambda a: pltpu.HBM(a.shape, a.dtype)
    return pl.pallas_call(
        body, name="rs_wait", out_shape=(hbm(wire_i), hbm(wire_o), hbm(land_i), hbm(land_o)),
        in_specs=(_HBM_SPEC,) * 4 + (_SEM_SPEC, _SEM_SPEC, pl.BlockSpec(memory_space=pl.ANY)),
        out_specs=(_HBM_SPEC,) * 4, input_output_aliases={0: 0, 1: 1, 2: 2, 3: 3},
        compiler_params=pltpu.CompilerParams(has_side_effects=_IN_FLIGHT),
    )(wire_i, wire_o, land_i, land_o, send_sems, recv_sems, after)[2:]


def final_reduce(keep_i, got_i, keep_o, got_o, slab):
    def kern(ki_ref, gi_ref, ko_ref, go_ref, s_ref, gin_ref, gout_ref, all_ref, tot_ref, send_sems, recv_sems):
        x, y, c = _me()
        sib = _flip(1)
        dev = lambda d: 4 * d[0] + 2 * d[1] + d[2]
        me = dev((x, y, c))

        def slab_copy(idx, owner, to):
            return _rcopy(all_ref.at[dev(owner)], all_ref.at[dev(owner)], send_sems, recv_sems, idx, to)

        all_ref[me] = s_ref[...]
        first = [slab_copy(0, (x, y, c), sib)] + [slab_copy(q // 2, (x, y, c), _flip(q)) for q in (2, 4, 6)]
        for cp in first:
            cp.start()

        shares = []
        for n, (keep, got, out) in enumerate(((ki_ref, gi_ref, gin_ref), (ko_ref, go_ref, gout_ref))):
            rh = RS_SHAPES[n][0]
            half = lambda hh, rh=rh: pl.ds(pl.multiple_of(hh * rh, rh), rh)
            out[half(c), :] = ((keep[...] + got[0].astype(F32)) + got[1].astype(F32)) + got[2].astype(F32)
            share = _rcopy(out.at[half(c), :], out.at[half(c), :], send_sems, recv_sems, 7 + n, sib)
            share.start()
            shares.append((share, _rcopy(out.at[half(1 - c), :], out.at[half(1 - c), :], send_sems, recv_sems, 7 + n,
                                         sib)))

        passed = []
        for q in (2, 4, 6):
            slab_copy(q // 2, _flip(q), (x, y, c)).wait_recv()
            cp = slab_copy(3 + q // 2, _flip(q), sib)
            cp.start()
            passed.append(cp)
        slab_copy(0, sib, (x, y, c)).wait_recv()
        for q in (2, 4, 6):
            slab_copy(3 + q // 2, _flip(q | 1), (x, y, c)).wait_recv()
        tot = all_ref[0]
        for d in range(1, NDEV):
            tot = tot + all_ref[d]
        tot_ref[...] = tot
        for share, arrival in shares:
            arrival.wait_recv()
            share.wait_send()
        for cp in first + passed:
            cp.wait_send()

    (rhi, wi), (rho, wo) = RS_SHAPES
    return pl.pallas_call(
        kern, name="final_reduce", in_specs=[_VMEM_SPEC] * 5, out_specs=[_VMEM_SPEC] * 4,
        out_shape=[jax.ShapeDtypeStruct((2 * rhi, wi), F32), jax.ShapeDtypeStruct((2 * rho, wo), F32),
                   jax.ShapeDtypeStruct((NDEV, SLAB_ROWS, DM), F32), jax.ShapeDtypeStruct((SLAB_ROWS, DM), F32)],
        scratch_shapes=[pltpu.SemaphoreType.DMA((9,)), pltpu.SemaphoreType.DMA((9,))],
        compiler_params=pltpu.CompilerParams(vmem_limit_bytes=40 * 1024 * 1024),
    )(keep_i, got_i, keep_o, got_o, slab)


def ada_bwd(a_in, dm, dm_shard, w_ada, c_ctx):
    def kern(a_ref, dm_ref, dms_ref, w_ref, cc_ref, dw_ref, db_ref, dcc_ref, parts, send_sems, recv_sems):
        x, y, c = _me()
        k = 2 * x + y
        act = jax.nn.silu(a_ref[...]).astype(BF16)
        dms = dms_ref[...].astype(BF16)
        dw_ref[...] = lax.dot_general(act, dms, (((0,), (0,)), ((), ())), preferred_element_type=F32)
        db_ref[...] = jnp.sum(dm_ref[...], axis=0, keepdims=True)
        parts[k] = lax.dot_general(dms, w_ref[...].astype(BF16), (((1,), (1,)), ((), ())), preferred_element_type=F32)
        sends = [_rcopy(parts.at[k], parts.at[k], send_sems, recv_sems, q // 2 - 1, _flip(q)) for q in (2, 4, 6)]
        for cp in sends:
            cp.start()
        for q in (2, 4, 6):
            kq = _chip_of(_flip(q))
            _rcopy(parts.at[kq], parts.at[kq], send_sems, recv_sems, q // 2 - 1, _flip(q)).wait_recv()
        dact = ((parts[0] + parts[1]) + parts[2]) + parts[3]
        _, vjp = jax.vjp(jax.nn.silu, cc_ref[...])
        dcc_ref[...] = vjp(dact[8:9, :])[0]
        for cp in sends:
            cp.wait_send()

    return pl.pallas_call(
        kern, name="ada_bwd", in_specs=[_VMEM_SPEC] * 5, out_specs=[_VMEM_SPEC] * 3,
        out_shape=[jax.ShapeDtypeStruct((DM, SHARD_ADA), F32), jax.ShapeDtypeStruct((1, 3 * DM), F32),
                   jax.ShapeDtypeStruct((1, DM), F32)],
        scratch_shapes=[pltpu.VMEM((NCHIP, 16, DM), F32), pltpu.SemaphoreType.DMA((3,)), pltpu.SemaphoreType.DMA((3,))],
    )(a_in, dm, dm_shard, w_ada, c_ctx)


def _adamw_math(w, g, m, v):
    m = B1 * m + (1.0 - B1) * g
    v = B2 * v + (1.0 - B2) * (g * g)
    m_hat = m / (1.0 - B1 ** STEP)
    v_hat = v / (1.0 - B2 ** STEP)
    return -LR * (m_hat / (jnp.sqrt(v_hat) + ADAM_EPS) + WD * w), m, v


def adamw_big(w, g, m, v, name, block_rows=256):
    rows, width = w.shape

    def kern(w_ref, g_ref, m_ref, v_ref, d_ref, nm_ref, nv_ref):
        d_ref[...], nm_ref[...], nv_ref[...] = _adamw_math(w_ref[...], g_ref[...], m_ref[...], v_ref[...])

    spec = pl.BlockSpec((block_rows, width), lambda i: (i, 0))
    return pl.pallas_call(
        kern, name=name, grid=(rows // block_rows,), in_specs=[spec] * 4, out_specs=[spec] * 3,
        out_shape=[jax.ShapeDtypeStruct((rows, width), F32)] * 3,
        compiler_params=_cparams(("arbitrary",)),
    )(w, g, m, v)


def adamw_small(quads):
    n = len(quads)

    def kern(*refs):
        ins, outs = refs[:4 * n], refs[4 * n:]
        for i in range(n):
            w, g, m, v = (r[...] for r in ins[4 * i:4 * i + 4])
            outs[3 * i][...], outs[3 * i + 1][...], outs[3 * i + 2][...] = _adamw_math(w, g, m, v)

    flat = [a for quad in quads for a in quad]
    res = pl.pallas_call(
        kern, name="adamw_small", in_specs=[_VMEM_SPEC] * (4 * n), out_specs=[_VMEM_SPEC] * (3 * n),
        out_shape=[jax.ShapeDtypeStruct(q[0].shape, F32) for q in quads for _ in range(3)],
    )(*flat)
    return [tuple(res[3 * i:3 * i + 3]) for i in range(n)]


def _rows_of(a, rows):
    flat = a.reshape(-1)
    return jnp.pad(flat, (0, rows * DM - flat.shape[0])).reshape(rows, DM)


def kernel(x, c, ctx, c_ctx, w_ada, b_ada, norm_g, w_in, sgu_norm_g, w_spatial, b_spatial, q_norm_g, k_norm_g, rpb, w_out, loss_target, m_c_ctx, m_w_ada, m_b_ada, m_norm_g, m_w_in, m_sgu_norm_g, m_w_spatial, m_b_spatial, m_q_norm_g, m_k_norm_g, m_rpb, m_w_out, v_c_ctx, v_w_ada, v_b_ada, v_norm_g, v_w_in, v_sgu_norm_g, v_w_spatial, v_b_spatial, v_q_norm_g, v_k_norm_g, v_rpb, v_w_out):
    xi, yi, ci = lax.axis_index("x"), lax.axis_index("y"), lax.axis_index("c")
    chip, dev = 2 * xi + yi, 4 * xi + 2 * yi + ci
    c_ctx2 = c_ctx.reshape(1, DM)

    b_shard = lax.dynamic_slice(b_ada, (0, chip * SHARD_ADA), (1, SHARD_ADA))
    part = local_step(chip.reshape(1).astype(jnp.int32), dev, x[0], c, c_ctx2, w_ada[0], b_shard, ctx[0], loss_target[0],
                      norm_g, sgu_norm_g, w_spatial[0], b_spatial[0], q_norm_g, k_norm_g, rpb[0], w_in[0], w_out[0])
    cs = part["cs"]

    slab = jnp.concatenate([
        part["d_norm_g"], _rows_of(part["d_sgu_g"], 1), _rows_of(part["d_b_s"], 1),
        _rows_of(jnp.concatenate([part["d_q_g"], part["d_k_g"]], axis=-1), 1), _rows_of(part["d_rpb"], 4),
        _rows_of(part["loss"], 1), _rows_of(part["dcmod"], 3), _rows_of(part["dmod"], 3), jnp.zeros((1, DM), F32),
        _rows_of(part["d_w_s"], 64)], axis=0)
    g_w_in, g_w_out, gathered, tot = final_reduce(*part["rs"], slab)
    dm = jnp.concatenate([gathered[:, 12:15, :].reshape(NDEV, 3 * DM), tot[9:12].reshape(1, 3 * DM),
                          jnp.zeros((7, 3 * DM), F32)], axis=0)
    a_in = jnp.concatenate([cs[0:8 * NDEV:8], cs[8 * NDEV:8 * NDEV + 1], jnp.zeros((7, DM), F32)], axis=0)
    dm_shard = lax.dynamic_slice(dm, (0, chip * SHARD_ADA), (16, SHARD_ADA))
    g_w_ada, g_b_ada, g_c_ctx = ada_bwd(a_in, dm, dm_shard, w_ada[0], c_ctx2)

    loss = tot[8, 0]
    g_small = dict(
        c_ctx=g_c_ctx, b_ada=g_b_ada, norm_g=tot[0:1], sgu_norm_g=tot[1:2, :512], w_spatial=tot[16:80].reshape(512, 128),
        b_spatial=tot[2:3, :512].reshape(4, 128), q_norm_g=tot[3:4, :HDIM], k_norm_g=tot[3:4, HDIM:2 * HDIM],
        rpb=tot[4:8].reshape(-1)[:HEADS * 15 * 31].reshape(HEADS * 15, 31))
    shapes = dict(c_ctx=(DM,), w_ada=(1, DM, SHARD_ADA), b_ada=(1, 3 * DM), norm_g=(1, DM), w_in=(1, DM, SHARD_IN),
                  sgu_norm_g=(1, 512), w_spatial=(1, 4, 128, 128), b_spatial=(1, 4, 128), q_norm_g=(1, HDIM),
                  k_norm_g=(1, HDIM), rpb=(1, HEADS, 15, 31), w_out=(1, SHARD_OUT, DM))
    names = list(shapes)
    weights = dict(c_ctx=c_ctx, w_ada=w_ada, b_ada=b_ada, norm_g=norm_g, w_in=w_in, sgu_norm_g=sgu_norm_g,
                   w_spatial=w_spatial, b_spatial=b_spatial, q_norm_g=q_norm_g, k_norm_g=k_norm_g, rpb=rpb, w_out=w_out)
    m_in = dict(zip(names, (m_c_ctx, m_w_ada, m_b_ada, m_norm_g, m_w_in, m_sgu_norm_g, m_w_spatial, m_b_spatial,
                            m_q_norm_g, m_k_norm_g, m_rpb, m_w_out)))
    v_in = dict(zip(names, (v_c_ctx, v_w_ada, v_b_ada, v_norm_g, v_w_in, v_sgu_norm_g, v_w_spatial, v_b_spatial,
                            v_q_norm_g, v_k_norm_g, v_rpb, v_w_out)))
    grads = dict(g_small, w_ada=g_w_ada, w_in=g_w_in, w_out=g_w_out)
    upd = {}
    for n in ("w_ada", "w_in", "w_out"):
        g = grads[n]
        upd[n] = adamw_big(weights[n].reshape(g.shape), g, m_in[n].reshape(g.shape), v_in[n].reshape(g.shape),
                           "adamw_" + n)
    small = [n for n in names if n not in upd]
    res = adamw_small([(weights[n].reshape(grads[n].shape), grads[n], m_in[n].reshape(grads[n].shape),
                        v_in[n].reshape(grads[n].shape)) for n in small])
    upd.update(zip(small, res))
    out = [loss, part["grad_x"].reshape(1, SEQ, DM)]
    out += [grads[n].reshape(shapes[n]) for n in names]
    for slot in range(3):
        out += [upd[n][slot].reshape(shapes[n]) for n in names]
    return tuple(out)
```

```python
import functools

import jax
import jax.numpy as jnp
from jax import lax
from jax.experimental import pallas as pl
from jax.experimental.pallas import tpu as pltpu

F32, BF16 = jnp.float32, jnp.bfloat16
SEQ, DM, CTX, DIN = 4096, 1024, 256, 3584
NCHIP, NDEV = 4, 8
SHARD_IN = DIN // NCHIP
SHARD_ADA = 3 * DM // NCHIP
SHARD_OUT = DM // NCHIP
GRID_W = 64
QROWS = 4
KROWS = 12
QBLK, KBLK = QROWS * GRID_W, KROWS * GRID_W
NQBLK = SEQ // QBLK
HEADS, HDIM, NPAIR = 8, 64, 4
EPS = 1e-6
NEG_INF = -1e30
ZQ, ZK, ZV, ZG = 12, 16, 20, 24
LR, B1, B2, ADAM_EPS, WD, STEP = 0.001, 0.9, 0.999, 1e-08, 0.01, 10
VMEM_BIG = 56 * 1024 * 1024
MESH_ID = pl.DeviceIdType.MESH


def _dot(a, b, lhs_c, rhs_c):
    return lax.dot_general(a.astype(BF16), b.astype(BF16), (((lhs_c,), (rhs_c,)), ((), ())),
                           preferred_element_type=F32)


@jax.custom_vjp
def mm(a, b):
    return _dot(a, b, 1, 0)


@jax.custom_vjp
def mm_nt(a, b):
    return _dot(a, b, 1, 1)


@jax.custom_vjp
def mm_tn(a, b):
    return _dot(a, b, 0, 0)


mm.defvjp(lambda a, b: (mm(a, b), (a, b)), lambda r, ct: (mm_nt(ct, r[1]), mm_tn(r[0], ct)))
mm_nt.defvjp(lambda a, b: (mm_nt(a, b), (a, b)), lambda r, ct: (mm(ct, r[1]), mm_tn(ct, r[0])))
mm_tn.defvjp(lambda a, b: (mm_tn(a, b), (a, b)), lambda r, ct: (mm_nt(r[1], ct), mm(r[0], ct)))


def _rms(x, g):
    return x * lax.rsqrt(jnp.mean(x * x, axis=-1, keepdims=True) + EPS) * g


def _modulated(x, g, scale, shift):
    return _rms(x, g) * (1.0 + scale) + shift


def _pair_rms(x, g2):
    lo = lax.broadcasted_iota(jnp.int32, (1, 2 * HDIM), 1) < HDIM
    sq = x * x
    s_lo = jnp.sum(jnp.where(lo, sq, 0.0), axis=-1, keepdims=True)
    s_hi = jnp.sum(jnp.where(lo, 0.0, sq), axis=-1, keepdims=True)
    rs = jnp.where(lo, lax.rsqrt(s_lo / HDIM + EPS), lax.rsqrt(s_hi / HDIM + EPS))
    return x * rs * g2


def _cparams(sem, vmem=None):
    return pltpu.CompilerParams(dimension_semantics=sem, vmem_limit_bytes=vmem)


def _row(n):
    return pl.BlockSpec((1, n), lambda *_: (0, 0))


CS_ROWS = 8 * NDEV + 8


def _mod_part(mod_ref, row, part):
    pieces = []
    for j in range(NCHIP):
        lo, hi = max(part * DM, j * SHARD_ADA), min((part + 1) * DM, (j + 1) * SHARD_ADA)
        if lo < hi:
            pieces.append(mod_ref[j, row, lo - j * SHARD_ADA:hi - j * SHARD_ADA])
    return jnp.concatenate(pieces, axis=-1)


def inproj_fwd(chip, x, c_vec, c_ctx, w_ada, b_shard, norm_g, w_shard, wo_shard):
    tl = 1024
    nt = SEQ // tl
    halves = (DM // 2, SHARD_OUT // 2)
    n_w, n_c = 12, NDEV - 1

    def kern(k_ref, x_ref, cv_ref, cc_ref, wa_ref, b_ref, g_ref, w_ref, wo_ref,
             z_ref, h_ref, wfull_ref, wofull_ref, modall_ref, csall_ref,
             w_scr, wo_scr, h_scr, mine, cs_scr, mod_scr, shsc_scr, send_sems, recv_sems):
        s, t = pl.program_id(0), pl.program_id(1)
        xi, yi, c = _me()
        k, me = 2 * xi + yi, 4 * xi + 2 * yi + c
        sib = _flip(1)
        rows = pl.ds(pl.multiple_of(t * tl, tl), tl)
        gathered = (w_scr, wo_scr)
        slot = lambda d: pl.ds(pl.multiple_of(8 * d, 8), 8)

        def c_copy(q, owner):
            return _rcopy(mine, cs_scr.at[slot(owner), :], send_sems, recv_sems, n_w + q - 1, _flip(q))

        def m_copy(q, chip_of_block):
            return _rcopy(mod_scr.at[chip_of_block], mod_scr.at[chip_of_block], send_sems, recv_sems,
                          n_w + n_c + q // 2 - 1, _flip(q))

        def adaln():
            first = lax.broadcasted_iota(jnp.int32, (8, DM), 0) == 0
            mine[...] = jnp.where(first, jnp.broadcast_to(cv_ref[...], (8, DM)), 0.0)
            cs_scr[slot(me), :] = mine[...]
            cs_scr[slot(NDEV), :] = jnp.where(first, jnp.broadcast_to(cc_ref[...], (8, DM)), 0.0)
            for q in range(1, NDEV):
                c_copy(q, me).start()
            wa = wa_ref[...].astype(BF16)
            for q in range(1, NDEV):
                px, py, pc = _flip(q)
                c_copy(q, 4 * px + 2 * py + pc).wait_recv()
            act = jax.nn.silu(cs_scr[...]).astype(BF16)
            mod_scr[k] = jnp.dot(act, wa, preferred_element_type=F32) + b_ref[...]
            for q in (2, 4, 6):
                m_copy(q, k).start()
            for q in (2, 4, 6):
                m_copy(q, _chip_of(_flip(q))).wait_recv()
            row = pl.ds(8 * me, 1)
            shsc_scr[0:1, :] = _mod_part(mod_scr, row, 0)
            shsc_scr[1:2, :] = _mod_part(mod_scr, row, 1)
            pltpu.sync_copy(mod_scr, modall_ref)
            pltpu.sync_copy(cs_scr, csall_ref)

        def block(n, chip_of_block, hh):
            return gathered[n].at[chip_of_block, pl.ds(pl.multiple_of(hh * halves[n], halves[n]), halves[n]), :]

        def ici(n, q, chip_of_block):
            blk = block(n, chip_of_block, c)
            return _rcopy(blk, blk, send_sems, recv_sems, 6 * n + q // 2 - 1, _flip(q))

        def d2d(n, q, chip_of_block, hh):
            blk = block(n, chip_of_block, hh)
            return _rcopy(blk, blk, send_sems, recv_sems, 6 * n + 3 + q // 2 - 1, sib)

        @pl.when((s == 0) & (t == 0))
        def _():
            adaln()
            w_scr[k] = w_ref[...].astype(BF16)
            wo_scr[k] = wo_ref[...].astype(BF16)
            for q in (2, 4, 6):
                ici(0, q, k).start()
                ici(1, q, k).start()

        for sweep in (1, 2, 3):
            @pl.when((s == sweep) & (t == 0))
            def _():
                q = 2 * sweep
                src = _chip_of(_flip(q))
                for n in (0, 1):
                    ici(n, q, src).wait_recv()
                    d2d(n, q, src, c).start()
                for n in (0, 1):
                    d2d(n, q, src, 1 - c).wait_recv()

        @pl.when(s == 0)
        def _():
            hb = _modulated(x_ref[...], g_ref[...], shsc_scr[1:2, :], shsc_scr[0:1, :]).astype(BF16)
            h_scr[rows, :] = hb
            h_ref[...] = hb

        z_ref[...] = jnp.dot(h_scr[rows, :], w_scr[lax.bitwise_xor(k, s)], preferred_element_type=F32)

        @pl.when((s == NCHIP - 1) & (t == nt - 1))
        def _():
            for q in range(1, NDEV):
                c_copy(q, me).wait_send()
            for q in (2, 4, 6):
                m_copy(q, k).wait_send()
            for n in (0, 1):
                for q in (2, 4, 6):
                    ici(n, q, k).wait_send()
                    d2d(n, q, _chip_of(_flip(q)), c).wait_send()
            pltpu.sync_copy(w_scr, wfull_ref)
            pltpu.sync_copy(wo_scr, wofull_ref)

    once = lambda s, t, k: (jnp.where(s == 0, t, nt - 1), 0)
    hbm = pl.BlockSpec(memory_space=pl.ANY)
    n_sem = n_w + n_c + 3
    return pl.pallas_call(
        kern, name="inproj_fwd",
        grid_spec=pltpu.PrefetchScalarGridSpec(
            num_scalar_prefetch=1, grid=(NCHIP, nt),
            in_specs=[pl.BlockSpec((tl, DM), once)] + [_VMEM_SPEC] * 7,
            out_specs=[pl.BlockSpec((tl, SHARD_IN), lambda s, t, k: (t, lax.bitwise_xor(k[0], s))),
                       pl.BlockSpec((tl, DM), once), hbm, hbm, hbm, hbm],
            scratch_shapes=[pltpu.VMEM((NCHIP, DM, SHARD_IN), BF16), pltpu.VMEM((NCHIP, SHARD_OUT, DM), BF16),
                            pltpu.VMEM((SEQ, DM), BF16), pltpu.VMEM((8, DM), F32), pltpu.VMEM((CS_ROWS, DM), F32),
                            pltpu.VMEM((NCHIP, CS_ROWS, SHARD_ADA), F32), pltpu.VMEM((8, DM), F32),
                            pltpu.SemaphoreType.DMA((n_sem,)), pltpu.SemaphoreType.DMA((n_sem,))]),
        out_shape=[jax.ShapeDtypeStruct((SEQ, DIN), F32), jax.ShapeDtypeStruct((SEQ, DM), BF16),
                   jax.ShapeDtypeStruct((NCHIP, DM, SHARD_IN), BF16), jax.ShapeDtypeStruct((NCHIP, SHARD_OUT, DM), BF16),
                   jax.ShapeDtypeStruct((NCHIP, CS_ROWS, SHARD_ADA), F32), jax.ShapeDtypeStruct((CS_ROWS, DM), F32)],
        compiler_params=_cparams(("arbitrary", "arbitrary"), VMEM_BIG),
    )(chip, x, c_vec, c_ctx, w_ada, b_shard, norm_g, w_shard, wo_shard)


def ctx_fwd(ctx, cshift, cscale, norm_g, w_full):
    def kern(c_ref, sh_ref, sc_ref, g_ref, w2_ref, w3_ref, zc_ref, hc_ref):
        hc = _modulated(c_ref[...], g_ref[...], sc_ref[...], sh_ref[...]).astype(BF16)
        hc_ref[...] = hc
        zc_ref[:, :SHARD_IN] = jnp.dot(hc, w2_ref[0], preferred_element_type=F32)
        zc_ref[:, SHARD_IN:] = jnp.dot(hc, w3_ref[0], preferred_element_type=F32)

    return pl.pallas_call(
        kern, name="ctx_fwd", grid=(1,),
        in_specs=[pl.BlockSpec((CTX, DM), lambda i: (0, 0)), _row(DM), _row(DM), _row(DM),
                  pl.BlockSpec((1, DM, SHARD_IN), lambda i: (2, 0, 0)),
                  pl.BlockSpec((1, DM, SHARD_IN), lambda i: (3, 0, 0))],
        out_specs=[pl.BlockSpec((CTX, 2 * SHARD_IN), lambda i: (0, 0)),
                   pl.BlockSpec((CTX, DM), lambda i: (0, 0))],
        out_shape=[jax.ShapeDtypeStruct((CTX, 2 * SHARD_IN), F32), jax.ShapeDtypeStruct((CTX, DM), BF16)],
        compiler_params=_cparams(("arbitrary",)),
    )(ctx, cshift, cscale, norm_g, w_full, w_full)


SGU_CHUNK, SGU_PER_STEP = 128, 4


def _gelu(x):
    return 0.5 * x * (1.0 + lax.erf(x * 0.7071067811865476))


def _sgu_chunk(au, av, ag, sg, ws, bsb):
    u, v = _gelu(au), _gelu(av)
    outs = []
    for g in range(4):
        sl = slice(128 * g, 128 * (g + 1))
        mixed = mm(ws[g], _rms(v[:, sl], sg[:, sl])) + bsb[g]
        outs.append(u[:, sl] * mixed * jax.nn.silu(ag[:, sl]))
    return jnp.concatenate(outs, axis=-1)


def _sgu_specs():
    rows = SGU_CHUNK * SGU_PER_STEP
    zspec = lambda c: pl.BlockSpec((rows, 512), lambda n: (n, c))
    wspec = pl.BlockSpec((4, 128, 128), lambda n: (0, 0, 0))
    return rows, [zspec(0), zspec(1), zspec(2), _row(512), wspec, wspec]


def sgu_fwd(z, sg, ws, bsb):
    rows, in_specs = _sgu_specs()

    def kern(au_ref, av_ref, ag_ref, sg_ref, ws_ref, bs_ref, o_ref):
        for c in range(SGU_PER_STEP):
            sl = slice(c * SGU_CHUNK, (c + 1) * SGU_CHUNK)
            o_ref[sl, :] = _sgu_chunk(au_ref[sl, :], av_ref[sl, :], ag_ref[sl, :], sg_ref[...], ws_ref[...],
                                      bs_ref[...])

    return pl.pallas_call(
        kern, name="sgu_fwd", grid=(SEQ // rows,), in_specs=in_specs,
        out_specs=pl.BlockSpec((rows, 512), lambda n: (n, 0)),
        out_shape=jax.ShapeDtypeStruct((SEQ, 512), F32),
        compiler_params=_cparams(("arbitrary",)),
    )(z, z, z, sg, ws, bsb)


def sgu_bwd(z, sg, ws, bsb, dcat):
    rows, in_specs = _sgu_specs()

    def kern(au_ref, av_ref, ag_ref, sg_ref, ws_ref, bs_ref, do_ref, dz_ref, dsg_ref, dws_ref, dbs_ref):
        @pl.when(pl.program_id(0) == 0)
        def _():
            dsg_ref[...] = jnp.zeros_like(dsg_ref)
            dws_ref[...] = jnp.zeros_like(dws_ref)
            dbs_ref[...] = jnp.zeros_like(dbs_ref)

        for c in range(SGU_PER_STEP):
            sl = slice(c * SGU_CHUNK, (c + 1) * SGU_CHUNK)
            _, vjp = jax.vjp(_sgu_chunk, au_ref[sl, :], av_ref[sl, :], ag_ref[sl, :], sg_ref[...], ws_ref[...],
                             bs_ref[...])
            dau, dav, dag, dsg, dws, dbs = vjp(do_ref[sl, :])
            dz_ref[sl, 0:512] = dau.astype(BF16)
            dz_ref[sl, 512:1024] = dav.astype(BF16)
            dz_ref[sl, 1024:1536] = dag.astype(BF16)
            dsg_ref[...] += dsg
            dws_ref[...] += dws
            dbs_ref[...] += dbs

        @pl.when(pl.program_id(0) == pl.num_programs(0) - 1)
        def _():
            dbs_ref[...] = jnp.broadcast_to(jnp.sum(dbs_ref[...], axis=-1, keepdims=True), dbs_ref.shape)

    wspec = pl.BlockSpec((4, 128, 128), lambda n: (0, 0, 0))
    return pl.pallas_call(
        kern, name="sgu_bwd", grid=(SEQ // rows,),
        in_specs=in_specs + [pl.BlockSpec((rows, 512), lambda n: (n, 0))],
        out_specs=[pl.BlockSpec((rows, 1536), lambda n: (n, 0)), _row(512), wspec, wspec],
        out_shape=[jax.ShapeDtypeStruct((SEQ, 1536), BF16), jax.ShapeDtypeStruct((1, 512), F32),
                   jax.ShapeDtypeStruct((4, 128, 128), F32), jax.ShapeDtypeStruct((4, 128, 128), F32)],
        compiler_params=_cparams(("arbitrary",)),
    )(z, z, z, sg, ws, bsb, dcat)


_DR_OFF = (7, 3, -1)


def _row_valid(v, rr, j):
    return (j < 8, rr <= j < rr + 8, 4 <= j < 12)[v]


def _col_window():
    q = lax.broadcasted_iota(jnp.int32, (GRID_W, 128), 0)
    kc = lax.broadcasted_iota(jnp.int32, (GRID_W, 128), 1) % GRID_W
    c0 = jnp.clip(q - 8, 0, GRID_W - 16)
    return (kc >= c0) & (kc < c0 + 16)


def rpb_tables(rpb2):
    def kern(r_ref, b_ref):
        base = r_ref[0]
        lo = lax.broadcasted_iota(jnp.int32, (1, 128), 1) < GRID_W
        win = _col_window()
        tiles = {}
        for v in range(3):
            for rr in range(QROWS):
                for jp in range(KROWS // 2):
                    j0, j1 = 2 * jp, 2 * jp + 1
                    ok0, ok1 = _row_valid(v, rr, j0), _row_valid(v, rr, j1)
                    key = (j0 - rr + _DR_OFF[v], ok0, ok1) if (ok0 or ok1) else None
                    if key not in tiles:
                        if key is None:
                            tiles[key] = jnp.full((GRID_W, 128), NEG_INF, F32)
                        else:
                            d0 = key[0]
                            r0 = base[d0:d0 + 1, :] if ok0 else jnp.zeros((1, 128), F32)
                            r1 = base[d0 + 1:d0 + 2, :] if ok1 else jnp.zeros((1, 128), F32)
                            y = jnp.broadcast_to(jnp.where(lo, r0, r1), (GRID_W, 128))
                            y = pltpu.roll(pltpu.roll(y, 128 - 15, 1), 0, 1, stride=1, stride_axis=0)
                            tiles[key] = jnp.where(win & jnp.where(lo, ok0, ok1), y, NEG_INF)
                    b_ref[v, 0, rr * GRID_W:(rr + 1) * GRID_W, jp * 128:(jp + 1) * 128] = tiles[key]

    return pl.pallas_call(
        kern, name="rpb_tables", grid=(HEADS,),
        in_specs=[pl.BlockSpec((1, 15, 128), lambda h: (h, 0, 0))],
        out_specs=pl.BlockSpec((3, 1, QBLK, KBLK), lambda h: (0, h, 0, 0)),
        out_shape=jax.ShapeDtypeStruct((3, HEADS, QBLK, KBLK), F32),
        compiler_params=_cparams(("arbitrary",)),
    )(rpb2)


def rpb_bwd(dbias):
    def kern(g0_ref, g1_ref, g2_ref, o_ref):
        g_refs = (g0_ref.at[0], g1_ref.at[0], g2_ref.at[0])
        lo = lax.broadcasted_iota(jnp.int32, (1, 128), 1) < GRID_W
        ri = lax.broadcasted_iota(jnp.int32, (GRID_W, GRID_W), 0)
        ci = lax.broadcasted_iota(jnp.int32, (GRID_W, GRID_W), 1)
        flip = (ri + ci == GRID_W - 1).astype(F32)
        groups = {}
        for v in range(3):
            for rr in range(QROWS):
                for jp in range(KROWS // 2):
                    j0, j1 = 2 * jp, 2 * jp + 1
                    ok0, ok1 = _row_valid(v, rr, j0), _row_valid(v, rr, j1)
                    if not (ok0 or ok1):
                        continue
                    g = g_refs[v][0, rr * GRID_W:(rr + 1) * GRID_W, jp * 128:(jp + 1) * 128]
                    key = (j0 - rr + _DR_OFF[v], ok0, ok1)
                    groups[key] = g if key not in groups else groups[key] + g
        acc = [jnp.zeros((1, 128), F32) for _ in range(15)]
        for (d0, ok0, ok1), g in groups.items():
            g = lax.dot_general(flip, g, (((1,), (0,)), ((), ())), precision=lax.Precision.HIGHEST,
                                preferred_element_type=F32)
            g = pltpu.roll(pltpu.roll(g, 128 - 48, 1), 0, 1, stride=1, stride_axis=0)
            s = jnp.sum(g, axis=0, keepdims=True)
            if ok0:
                acc[d0] = acc[d0] + jnp.where(lo, s, 0.0)
            if ok1:
                acc[d0 + 1] = acc[d0 + 1] + jnp.where(lo, 0.0, s)
        for d in range(15):
            o_ref[0, d:d + 1, :] = acc[d] + pltpu.roll(acc[d], GRID_W, 1)

    return pl.pallas_call(
        kern, name="rpb_bwd", grid=(HEADS,),
        in_specs=[pl.BlockSpec((1, 1, QBLK, KBLK), functools.partial(lambda v, h: (v, h, 0, 0), v)) for v in range(3)],
        out_specs=pl.BlockSpec((1, 15, 128), lambda h: (h, 0, 0)),
        out_shape=jax.ShapeDtypeStruct((HEADS, 15, 128), F32),
        compiler_params=_cparams(("arbitrary",)),
    )(dbias, dbias, dbias)


def _scaled_q(q_raw, qg):
    return _pair_rms(q_raw, qg) * (HDIM ** -0.5)


def _head_lanes():
    lo = lax.broadcasted_iota(jnp.int32, (1, 2 * HDIM), 1) < HDIM
    return lo, jnp.logical_not(lo)


SOFTMAX_ROWS = 32


def _emit_interleaved(vector_work, matmul_work):
    for j in range(max(len(vector_work), len(matmul_work))):
        for work in (vector_work, matmul_work):
            if j < len(work):
                work[j]()


def _kblock(i):
    return jnp.clip(i - 1, 0, (SEQ - KBLK) // QBLK)


def _kstart(i):
    return pl.multiple_of(_kblock(i) * QBLK, QBLK)


ATTN_STEPS = NQBLK // 2
ATTN_ROWS = 2 * QBLK
KCOLS = QBLK


def _attn_in_specs():
    return [
        pl.BlockSpec((ATTN_ROWS, 128), lambda p, i: (i, ZQ + p)),
        pl.BlockSpec((SEQ, 128), lambda p, i: (0, ZK + p)),
        pl.BlockSpec((SEQ, 128), lambda p, i: (0, ZV + p)),
        pl.BlockSpec((ATTN_ROWS, 128), lambda p, i: (i, ZG + p)),
        pl.BlockSpec((CTX, 128), lambda p, i: (0, 2 + p)),
        pl.BlockSpec((CTX, 128), lambda p, i: (0, 6 + p)),
    ]


def _bias_specs():
    bias_spec = lambda variant: pl.BlockSpec((1, 2, QBLK, KBLK), lambda p, i: (variant(i), p, 0, 0))
    return [bias_spec(lambda i: jnp.where(i == 0, 0, 1)),
            bias_spec(lambda i: jnp.where(i == ATTN_STEPS - 1, 2, 1))]


def _prob_specs():
    return [pl.BlockSpec((2, ATTN_ROWS, KBLK), lambda p, i: (p, i, 0)),
            pl.BlockSpec((2, ATTN_ROWS, CTX), lambda p, i: (p, i, 0))]


NORM_ROWS = 512


def _half_sums(x):
    lo = lax.broadcasted_iota(jnp.int32, (1, 2 * HDIM), 1) < HDIM
    return jnp.where(lo, jnp.sum(jnp.where(lo, x, 0.0), axis=-1, keepdims=True),
                     jnp.sum(jnp.where(lo, 0.0, x), axis=-1, keepdims=True))


def _pair_rms_bwd(x, g2, ct):
    rs = lax.rsqrt(_half_sums(x * x) / HDIM + EPS)
    y = x * rs
    dy = ct * g2
    return rs * (dy - y * (_half_sums(dy * y) / HDIM)), jnp.sum(ct * y, axis=0, keepdims=True)


def _norm_keys(k_ref, ck_ref, kg_ref, kn_scr, ckn_scr):
    def body(c, carry):
        sl = pl.ds(pl.multiple_of(c * NORM_ROWS, NORM_ROWS), NORM_ROWS)
        kn_scr[sl, :] = _pair_rms(k_ref[sl, :], kg_ref[...]).astype(BF16)
        return carry

    lax.fori_loop(0, SEQ // NORM_ROWS, body, 0)
    ckn_scr[...] = _pair_rms(ck_ref[...], kg_ref[...]).astype(BF16)


def _values_with_ones(v_ref, cv_ref, v1_scr, cv1_scr):
    for a, mine in enumerate(_head_lanes()):
        def body(c, carry):
            sl = pl.ds(pl.multiple_of(c * NORM_ROWS, NORM_ROWS), NORM_ROWS)
            v1_scr[a, sl, :] = jnp.where(mine, v_ref[sl, :], 1.0).astype(BF16)
            return carry

        lax.fori_loop(0, SEQ // NORM_ROWS, body, 0)
        cv1_scr[a] = jnp.where(mine, cv_ref[...], 1.0).astype(BF16)


def _pair_major_spec():
    return pl.BlockSpec((1, ATTN_ROWS, 128), lambda p, i: (p, i, 0))


def _normed_key_specs():
    return [pl.BlockSpec((None, SEQ, 128), lambda p, i: (p, 0, 0)), pl.BlockSpec((None, CTX, 128), lambda p, i: (p, 0, 0))]


def attn_fwd(z, zc, bias, qg2, kg2):
    def kern(q_ref, k_ref, v_ref, bg_ref, ck_ref, cv_ref, be_ref, bo_ref, qg_ref, kg_ref,
             ob_ref, o_ref, rden_ref, pl_ref, pc_ref, kn_ref, ckn_ref, kn_scr, ckn_scr, v1_scr, cv1_scr, s_scr):
        i = pl.program_id(1)

        @pl.when(i == 0)
        def _():
            _norm_keys(k_ref, ck_ref, kg_ref, kn_scr, ckn_scr)
            kn_ref[...] = kn_scr[...]
            ckn_ref[...] = ckn_scr[...]
            _values_with_ones(v_ref, cv_ref, v1_scr, cv1_scr)

        heads = _head_lanes()
        bias_refs = (be_ref, bo_ref)
        tiles = [(b, a) for b in range(2) for a in range(2)]
        rows = [slice(b * QBLK, (b + 1) * QBLK) for b in range(2)]
        qn = [_scaled_q(q_ref[rows[b], :], qg_ref[...]) for b in range(2)]
        qa = [jnp.where(heads[a], qn[b], 0.0).astype(BF16) for b, a in tiles]
        pv = [None] * len(tiles)
        done = {}
        latent = KBLK // KCOLS

        def keys(b, n):
            return pl.ds(pl.multiple_of(_kstart(2 * i + b) + n * KCOLS, KCOLS), KCOLS)

        def score_piece(t, n):
            b, a = tiles[t]
            cols = slice(n * KCOLS, (n + 1) * KCOLS)
            if n < latent:
                s_scr[t, :, cols] = mm_nt(qa[t], kn_scr[keys(b, n), :]) + bias_refs[b][0, a, :, cols]
            else:
                s_scr[t, :, cols] = mm_nt(qa[t], ckn_scr[...])

        def softmax_rows(t, r):
            b, a = tiles[t]
            rs = slice(r * SOFTMAX_ROWS, (r + 1) * SOFTMAX_ROWS)
            out_rows = slice(b * QBLK + rs.start, b * QBLK + rs.stop)
            s = s_scr[t, rs, :]
            p = jnp.exp(s - jnp.max(s, axis=-1, keepdims=True)).astype(BF16)
            pl_ref[a, out_rows, :] = p[:, :KBLK]
            pc_ref[a, out_rows, :] = p[:, KBLK:]

        def value_piece(t, n):
            b, a = tiles[t]
            if n < latent:
                part = mm(pl_ref[a, rows[b], n * KCOLS:(n + 1) * KCOLS], v1_scr[a, keys(b, n), :])
            else:
                part = mm(pc_ref[a, rows[b], :], cv1_scr[a])
            pv[t] = part if pv[t] is None else pv[t] + part
            if n == latent:
                finish(t)

        def finish(t):
            b, a = tiles[t]
            r = jnp.where(heads[a], pltpu.roll(1.0 / pv[t], HDIM, 1), 0.0)
            done[t] = (pv[t] * r, r)
            if a == 1:
                o, rden = (lo + hi for lo, hi in zip(done[t - 1], done[t]))
                ob_ref[rows[b], :] = o * jax.nn.silu(bg_ref[rows[b], :])
                o_ref[0, rows[b], :] = o
                rden_ref[0, rows[b], :] = rden

        pieces = range(latent + 1)
        for n in pieces:
            score_piece(0, n)
        for t in range(len(tiles)):
            matmuls = []
            for n in pieces:
                if t + 1 < len(tiles):
                    matmuls.append(functools.partial(score_piece, t + 1, n))
                if t > 0:
                    matmuls.append(functools.partial(value_piece, t - 1, n))
            _emit_interleaved([functools.partial(softmax_rows, t, r) for r in range(QBLK // SOFTMAX_ROWS)], matmuls)
        for n in pieces:
            value_piece(len(tiles) - 1, n)

    qblk = pl.BlockSpec((ATTN_ROWS, 128), lambda p, i: (i, p))
    return pl.pallas_call(
        kern, name="attn_fwd", grid=(NPAIR, ATTN_STEPS),
        in_specs=_attn_in_specs() + _bias_specs() + [_row(128), _row(128)],
        out_specs=[qblk, _pair_major_spec(), _pair_major_spec()] + _prob_specs() + _normed_key_specs(),
        out_shape=[jax.ShapeDtypeStruct((SEQ, 512), F32)] + [jax.ShapeDtypeStruct((NPAIR, SEQ, 128), F32)] * 2
        + [jax.ShapeDtypeStruct((HEADS, SEQ, KBLK), BF16), jax.ShapeDtypeStruct((HEADS, SEQ, CTX), BF16),
           jax.ShapeDtypeStruct((NPAIR, SEQ, 128), BF16), jax.ShapeDtypeStruct((NPAIR, CTX, 128), BF16)],
        scratch_shapes=[pltpu.VMEM((SEQ, 128), BF16), pltpu.VMEM((CTX, 128), BF16),
                        pltpu.VMEM((2, SEQ, 128), BF16), pltpu.VMEM((2, CTX, 128), BF16),
                        pltpu.VMEM((4, QBLK, KBLK + CTX), F32)],
        compiler_params=_cparams(("arbitrary", "arbitrary"), 40 * 1024 * 1024),
    )(z, z, z, z, zc, zc, bias, bias, qg2, kg2)


def attn_bwd(z, zc, qg2, kg2, dcat, saved):
    def kern(q_ref, k_ref, v_ref, bg_ref, ck_ref, cv_ref, qg_ref, kg_ref, do_ref, o_ref, rden_ref, pl_ref, pc_ref,
             kn_scr, ckn_scr, dq_ref, dk_ref, dv_ref, dbg_ref, dck_ref, dcv_ref, db_ref, dqg_ref, dkg_ref,
             v_scr, cv_scr, dknt_scr, dvt_scr, dcknt_scr, dcvt_scr, dp_scr, ds_scr):
        p, i = pl.program_id(0), pl.program_id(1)
        last = i == ATTN_STEPS - 1

        @pl.when(i == 0)
        def _():
            def body(c, carry):
                sl = pl.ds(pl.multiple_of(c * NORM_ROWS, NORM_ROWS), NORM_ROWS)
                v_scr[sl, :] = v_ref[sl, :].astype(BF16)
                return carry

            lax.fori_loop(0, SEQ // NORM_ROWS, body, 0)
            cv_scr[...] = cv_ref[...].astype(BF16)
            for acc in (dknt_scr, dvt_scr, dcknt_scr, dcvt_scr, db_ref):
                acc[...] = jnp.zeros_like(acc)

        @pl.when((i == 0) & (p == 0))
        def _():
            dqg_ref[...] = jnp.zeros_like(dqg_ref)
            dkg_ref[...] = jnp.zeros_like(dkg_ref)

        heads = _head_lanes()
        tiles = [(b, a) for b in range(2) for a in range(2)]
        rows = [slice(b * QBLK, (b + 1) * QBLK) for b in range(2)]
        kb = [_kblock(2 * i + b) for b in range(2)]
        variant = [jnp.where(i == 0, 0, 1), jnp.where(last, 2, 1)]
        latent = KBLK // KCOLS

        def keys(b, n):
            return pl.ds(pl.multiple_of((kb[b] + n) * KCOLS, KCOLS), KCOLS)

        gated = []
        for b in range(2):
            bg, dout, o = bg_ref[rows[b], :], do_ref[rows[b], :], o_ref[0, rows[b], :]
            sig = jax.nn.sigmoid(bg)
            do = dout * (bg * sig)
            dbg_ref[rows[b], :] = (dout * o * (sig * (1.0 + bg * (1.0 - sig)))).astype(BF16)
            rden = rden_ref[0, rows[b], :]
            dr = do * rden
            qn = _scaled_q(q_ref[rows[b], :], qg_ref[...])
            gated.append((dr, dr.T.astype(BF16), qn.T.astype(BF16), do * o * rden))

        feats = [slice(a * HDIM, (a + 1) * HDIM) for a in range(2)]
        doa, doa_t, qa_t, delta = [], [], [], []
        for b, a in tiles:
            dr, dr_t, qn_t, weighted = gated[b]
            doa.append(jnp.where(heads[a], dr, 0.0).astype(BF16))
            doa_t.append(dr_t[feats[a], :])
            qa_t.append(qn_t[feats[a], :])
            delta.append(jnp.sum(jnp.where(heads[a], weighted, 0.0), axis=-1, keepdims=True))
        dqn = [None] * len(tiles)

        def cols(n):
            return slice(n * KCOLS, (n + 1) * KCOLS)

        def stage_a(t, n):
            b, a = tiles[t]
            if n < latent:
                dp_scr[t, :, cols(n)] = mm_nt(doa[t], v_scr[keys(b, n), :])
                dvt_scr[kb[b] + n, feats[a], :] += mm(doa_t[t], pl_ref[a, rows[b], cols(n)])
            else:
                dp_scr[t, :, cols(n)] = mm_nt(doa[t], cv_scr[...])
                dcvt_scr[feats[a], :] += mm(doa_t[t], pc_ref[a, rows[b], :])

        def stage_b(t, r):
            b, a = tiles[t]
            rs = slice(r * SOFTMAX_ROWS, (r + 1) * SOFTMAX_ROWS)
            in_rows = slice(b * QBLK + rs.start, b * QBLK + rs.stop)
            d = dp_scr[t, rs, :] - delta[t][rs, :]
            ds_lat = pl_ref[a, in_rows, :].astype(F32) * d[:, :KBLK]
            ds_ctx = pc_ref[a, in_rows, :].astype(F32) * d[:, KBLK:]
            db_ref[variant[b], a, rs, :] += ds_lat
            ds_scr[t, rs, :KBLK] = ds_lat.astype(BF16)
            ds_scr[t, rs, KBLK:] = ds_ctx.astype(BF16)

        def stage_c(t, n):
            b, a = tiles[t]
            ds = ds_scr[t, :, cols(n)]
            if n < latent:
                part = mm(ds, kn_scr[keys(b, n), :])
                dknt_scr[kb[b] + n, feats[a], :] += mm(qa_t[t], ds)
            else:
                part = mm(ds, ckn_scr[...])
                dcknt_scr[feats[a], :] += mm(qa_t[t], ds)
            dqn[t] = part if dqn[t] is None else dqn[t] + part
            if n == latent and a == 1:
                both = jnp.where(heads[0], dqn[t - 1], 0.0) + jnp.where(heads[1], dqn[t], 0.0)
                dq, dqg = jax.vjp(_scaled_q, q_ref[rows[b], :], qg_ref[...])[1](both)
                dq_ref[rows[b], :] = dq.astype(BF16)
                dqg_ref[...] += dqg

        pieces = range(latent + 1)
        for n in pieces:
            stage_a(0, n)
        for t in range(len(tiles)):
            matmuls = []
            for n in pieces:
                if t + 1 < len(tiles):
                    matmuls.append(functools.partial(stage_a, t + 1, n))
                if t > 0:
                    matmuls.append(functools.partial(stage_c, t - 1, n))
            _emit_interleaved([functools.partial(stage_b, t, r) for r in range(QBLK // SOFTMAX_ROWS)], matmuls)
        for n in pieces:
            stage_c(len(tiles) - 1, n)

        @pl.when(last)
        def _():
            eye = (lax.broadcasted_iota(jnp.int32, (KCOLS, KCOLS), 0)
                   == lax.broadcasted_iota(jnp.int32, (KCOLS, KCOLS), 1)).astype(BF16)

            def turned(x):
                hi = x.astype(BF16)
                return mm_nt(eye, hi) + mm_nt(eye, x - hi.astype(F32))

            def body(c, dkg):
                sl = pl.ds(pl.multiple_of(c * NORM_ROWS, NORM_ROWS), NORM_ROWS)
                blocks = range(NORM_ROWS // KCOLS)
                dkn = jnp.concatenate([turned(dknt_scr[c * len(blocks) + n]) for n in blocks], axis=0)
                dv = jnp.concatenate([mm_nt(eye, dvt_scr[c * len(blocks) + n]) for n in blocks], axis=0)
                dk, dg = _pair_rms_bwd(k_ref[sl, :], kg_ref[...], dkn)
                dk_ref[sl, :] = dk.astype(BF16)
                dv_ref[sl, :] = dv.astype(BF16)
                return dkg + dg

            dkg = lax.fori_loop(0, SEQ // NORM_ROWS, body, jnp.zeros((1, 128), F32))
            dck, dg = _pair_rms_bwd(ck_ref[...], kg_ref[...], dcknt_scr[...].T)
            dck_ref[...] = dck
            dcv_ref[...] = dcvt_scr[...].T
            dkg_ref[...] += dkg + dg

        @pl.when(last & (p == NPAIR - 1))
        def _():
            dqg_ref[...] = dqg_ref[...] + pltpu.roll(dqg_ref[...], HDIM, 1)
            dkg_ref[...] = dkg_ref[...] + pltpu.roll(dkg_ref[...], HDIM, 1)

    blk = lambda rows: pl.BlockSpec((rows, 128), lambda p, i: (0, p))
    qblk = pl.BlockSpec((ATTN_ROWS, 128), lambda p, i: (i, p))
    return pl.pallas_call(
        kern, name="attn_bwd", grid=(NPAIR, ATTN_STEPS),
        in_specs=_attn_in_specs() + [_row(128), _row(128), pl.BlockSpec((ATTN_ROWS, 128), lambda p, i: (i, 4 + p)),
                                     _pair_major_spec(), _pair_major_spec()] + _prob_specs() + _normed_key_specs(),
        out_specs=[qblk, blk(SEQ), blk(SEQ), qblk, blk(CTX), blk(CTX),
                   pl.BlockSpec((3, 2, QBLK, KBLK), lambda p, i: (0, p, 0, 0)), _row(128), _row(128)],
        out_shape=[jax.ShapeDtypeStruct((SEQ, 512), BF16)] * 4 + [jax.ShapeDtypeStruct((CTX, 512), F32)] * 2
        + [jax.ShapeDtypeStruct((3, HEADS, QBLK, KBLK), F32)]
        + [jax.ShapeDtypeStruct((1, 128), F32), jax.ShapeDtypeStruct((1, 128), F32)],
        scratch_shapes=[pltpu.VMEM((SEQ, 128), BF16), pltpu.VMEM((CTX, 128), BF16),
                        pltpu.VMEM((SEQ // KCOLS, 128, KCOLS), F32), pltpu.VMEM((SEQ // KCOLS, 128, KCOLS), F32),
                        pltpu.VMEM((128, CTX), F32), pltpu.VMEM((128, CTX), F32),
                        pltpu.VMEM((4, QBLK, KBLK + CTX), F32), pltpu.VMEM((4, QBLK, KBLK + CTX), BF16)],
        compiler_params=_cparams(("arbitrary", "arbitrary"), VMEM_BIG),
    )(z, z, z, z, zc, zc, qg2, kg2, dcat, *saved)


def outproj(out_a, out_b, x, target, gate, wo):
    tl = 512

    def kern(a_ref, b_ref, x_ref, t_ref, g_ref, w_ref, loss_ref, dy_ref, dcat_ref, dg_ref, dw_ref):
        @pl.when(pl.program_id(0) == 0)
        def _():
            loss_ref[...] = jnp.zeros_like(loss_ref)
            dg_ref[...] = jnp.zeros_like(dg_ref)
            dw_ref[...] = jnp.zeros_like(dw_ref)

        a, b = a_ref[...].astype(BF16), b_ref[...].astype(BF16)
        mix = (jnp.dot(a, w_ref[0:512, :], preferred_element_type=F32)
               + jnp.dot(b, w_ref[512:1024, :], preferred_element_type=F32))
        err = x_ref[...] + g_ref[...] * mix - t_ref[...]
        loss_ref[...] += 0.5 * jnp.sum(jnp.mean(err * err, axis=-1))
        dy = err * (1.0 / DM)
        dy_ref[...] = dy
        dg_ref[...] += jnp.sum(dy * mix, axis=0, keepdims=True)
        dmix = (g_ref[...] * dy).astype(BF16)
        dcat_ref[...] = lax.dot_general(dmix, w_ref[...], (((1,), (1,)), ((), ())), preferred_element_type=F32)
        dw_ref[0:512, :] += lax.dot_general(a, dmix, (((0,), (0,)), ((), ())), preferred_element_type=F32)
        dw_ref[512:1024, :] += lax.dot_general(b, dmix, (((0,), (0,)), ((), ())), preferred_element_type=F32)

    tile = lambda w: pl.BlockSpec((tl, w), lambda t: (t, 0))
    whole = pl.BlockSpec((DM, DM), lambda t: (0, 0))
    return pl.pallas_call(
        kern, name="outproj", grid=(SEQ // tl,),
        in_specs=[tile(512), tile(512), tile(DM), tile(DM), _row(DM), whole],
        out_specs=[pl.BlockSpec((8, 128), lambda t: (0, 0)), tile(DM), tile(DM), _row(DM), whole],
        out_shape=[jax.ShapeDtypeStruct((8, 128), F32), jax.ShapeDtypeStruct((SEQ, DM), F32),
                   jax.ShapeDtypeStruct((SEQ, DM), F32), jax.ShapeDtypeStruct((1, DM), F32),
                   jax.ShapeDtypeStruct((DM, DM), F32)],
        compiler_params=_cparams(("arbitrary",), 48 * 1024 * 1024),
    )(out_a, out_b, x, target, gate, wo)


def _pieces(sources):
    out = []
    for name, c0, c1 in sources:
        for j in range(NCHIP):
            lo, hi = max(c0, j * SHARD_IN), min(c1, (j + 1) * SHARD_IN)
            if lo < hi:
                out.append((j, lo - j * SHARD_IN, hi - j * SHARD_IN, name, lo - c0, hi - c0))
    return out


DZ_PIECES = _pieces((("a", 0, 1536), ("q", 1536, 2048), ("k", 2048, 2560), ("v", 2560, 3072), ("g", 3072, DIN)))
DZC_PIECES = _pieces((("k", 2048, 2560), ("v", 2560, 3072)))
_NT = (((1,), (1,)), ((), ()))


DH_SUBTILES = 2


def _dz_specs(tl):
    return [pl.BlockSpec((tl, 1536), lambda t: (t, 0))] + [pl.BlockSpec((tl, 512), lambda t: (t, 0))] * 4


def dh_bwd(dz_parts, w_full, x, dy, shift, scale, norm_g, dg_ctx):
    tl = 512
    nt = SEQ // tl

    def kern(a_ref, q_ref, k_ref, v_ref, g_ref, w_ref, x_ref, dy_ref, sh_ref, sc_ref, gn_ref, dgc_ref,
             gx_ref, dsh_ref, dsc_ref, dg_ref):
        @pl.when(pl.program_id(0) == 0)
        def _():
            dsh_ref[...] = jnp.zeros_like(dsh_ref)
            dsc_ref[...] = jnp.zeros_like(dsc_ref)
            dg_ref[...] = dgc_ref[...]

        src = dict(a=a_ref, q=q_ref, k=k_ref, v=v_ref, g=g_ref)
        for sub in range(DH_SUBTILES):
            rows = slice(sub * tl // DH_SUBTILES, (sub + 1) * tl // DH_SUBTILES)
            dh = None
            for j, l0, l1, name, s0, s1 in DZ_PIECES:
                part = lax.dot_general(src[name][rows, s0:s1], w_ref[j, :, l0:l1], _NT, preferred_element_type=F32)
                dh = part if dh is None else dh + part
            _, vjp = jax.vjp(_modulated, x_ref[rows, :], gn_ref[...], sc_ref[...], sh_ref[...])
            dx, dg, dsc, dsh = vjp(dh)
            gx_ref[rows, :] = dy_ref[rows, :] + dx
            dg_ref[...] += dg
            dsc_ref[...] += dsc
            dsh_ref[...] += dsh

    tile = pl.BlockSpec((tl, DM), lambda t: (t, 0))
    return pl.pallas_call(
        kern, name="dh_bwd", grid=(nt,),
        in_specs=_dz_specs(tl) + [pl.BlockSpec((NCHIP, DM, SHARD_IN), lambda t: (0, 0, 0)), tile, tile, _row(DM),
                                  _row(DM), _row(DM), _row(DM)],
        out_specs=[tile, _row(DM), _row(DM), _row(DM)],
        out_shape=[jax.ShapeDtypeStruct((SEQ, DM), F32)] + [jax.ShapeDtypeStruct((1, DM), F32)] * 3,
        compiler_params=_cparams(("arbitrary",), 48 * 1024 * 1024),
    )(*dz_parts, w_full, x, dy, shift, scale, norm_g, dg_ctx)


def dw_bwd(h, dz_parts, hc, dck, dcv, g_out):
    tl = 512
    nt = SEQ // tl
    (rhi, wi), (rho, wo) = RS_SHAPES

    def kern(h_ref, a_ref, q_ref, k_ref, v_ref, g_ref, hc_ref, dck_ref, dcv_ref, go_hbm,
             wire_i, keep_i, wire_o, keep_o, acc, snd_i, rcv_i, mine_o, rcv_o, load_sem, send_sems, recv_sems):
        t = pl.program_id(0)
        x, y, c = _me()
        k = 2 * x + y
        sib = _flip(1)
        half = lambda hh, rh: pl.ds(pl.multiple_of(hh * rh, rh), rh)
        load_o = pltpu.make_async_copy(go_hbm.at[:, half(c, rho), :], mine_o, load_sem)
        pair_o = _rcopy(go_hbm.at[:, half(1 - c, rho), :], rcv_o, send_sems, recv_sems, 0, sib)
        pair_i = [_rcopy(snd_i.at[j], rcv_i.at[j], send_sems, recv_sems, 1 + j, sib) for j in range(NCHIP)]

        @pl.when(t == 0)
        def _():
            load_o.start()
            pair_o.start()
            acc[...] = jnp.zeros_like(acc)
            hct = hc_ref[...].T
            csrc = dict(k=dck_ref, v=dcv_ref)
            for j, l0, l1, name, s0, s1 in DZC_PIECES:
                acc[j, :, l0:l1] += jnp.dot(hct, csrc[name][:, s0:s1].astype(BF16), preferred_element_type=F32)

        ht = h_ref[...].T
        src = dict(a=a_ref, q=q_ref, k=k_ref, v=v_ref, g=g_ref)
        for j, l0, l1, name, s0, s1 in DZ_PIECES:
            acc[j, :, l0:l1] += jnp.dot(ht, src[name][:, s0:s1], preferred_element_type=F32)

        @pl.when(t == nt - 1)
        def _():
            for j in range(NCHIP):
                snd_i[j] = acc[j, half(1 - c, rhi), :].astype(BF16)
                pair_i[j].start()
            load_o.wait()
            pair_o.wait_recv()
            for j in range(NCHIP):
                wire_o[j] = (mine_o[j] + rcv_o[j]).astype(BF16)
            keep_o[...] = mine_o[k] + rcv_o[k]
            mine = half(c, rhi)
            for j in range(NCHIP):
                pair_i[j].wait_recv()
                wire_i[j] = (acc[j, mine, :] + rcv_i[j].astype(F32)).astype(BF16)
            keep_i[...] = acc[k, mine, :] + rcv_i[k].astype(F32)
            pair_o.wait_send()
            for j in range(NCHIP):
                pair_i[j].wait_send()

    whole = lambda *shape: pl.BlockSpec(shape, lambda t: (0,) * len(shape))
    return pl.pallas_call(
        kern, name="dw_bwd", grid=(nt,),
        in_specs=[pl.BlockSpec((tl, DM), lambda t: (t, 0))] + _dz_specs(tl)
        + [whole(CTX, DM), whole(CTX, 512), whole(CTX, 512), pl.BlockSpec(memory_space=pl.ANY)],
        out_specs=[whole(NCHIP, rhi, wi), whole(rhi, wi), whole(NCHIP, rho, wo), whole(rho, wo)],
        out_shape=[jax.ShapeDtypeStruct((NCHIP, rhi, wi), BF16), jax.ShapeDtypeStruct((rhi, wi), F32),
                   jax.ShapeDtypeStruct((NCHIP, rho, wo), BF16), jax.ShapeDtypeStruct((rho, wo), F32)],
        scratch_shapes=[pltpu.VMEM((NCHIP, DM, SHARD_IN), F32), pltpu.VMEM((NCHIP, rhi, wi), BF16),
                        pltpu.VMEM((NCHIP, rhi, wi), BF16),
                        pltpu.VMEM((NCHIP, rho, wo), F32), pltpu.VMEM((NCHIP, rho, wo), F32),
                        pltpu.SemaphoreType.DMA(()), pltpu.SemaphoreType.DMA((1 + NCHIP,)),
                        pltpu.SemaphoreType.DMA((1 + NCHIP,))],
        compiler_params=_cparams(("arbitrary",), VMEM_BIG),
    )(h, *dz_parts, hc, dck, dcv, g_out)


def ctx_bwd(dck, dcv, w_full, ctx, cshift, cscale, norm_g):
    def kern(dck_ref, dcv_ref, w_ref, c_ref, sh_ref, sc_ref, g_ref, dsh_ref, dsc_ref, dg_ref):
        csrc = dict(k=dck_ref, v=dcv_ref)
        dhc = None
        for j, l0, l1, name, s0, s1 in DZC_PIECES:
            part = lax.dot_general(csrc[name][:, s0:s1].astype(BF16), w_ref[j, :, l0:l1], _NT,
                                   preferred_element_type=F32)
            dhc = part if dhc is None else dhc + part
        _, vjp = jax.vjp(lambda g, sc, sh: _modulated(c_ref[...], g, sc, sh), g_ref[...], sc_ref[...], sh_ref[...])
        dg_ref[...], dsc_ref[...], dsh_ref[...] = vjp(dhc)

    whole = lambda r, c: pl.BlockSpec((r, c), lambda i: (0, 0))
    return pl.pallas_call(
        kern, name="ctx_bwd", grid=(1,),
        in_specs=[whole(CTX, 512), whole(CTX, 512), pl.BlockSpec((NCHIP, DM, SHARD_IN), lambda i: (0, 0, 0)),
                  whole(CTX, DM), _row(DM), _row(DM), _row(DM)],
        out_specs=[_row(DM), _row(DM), _row(DM)],
        out_shape=[jax.ShapeDtypeStruct((1, DM), F32)] * 3,
        compiler_params=_cparams(("arbitrary",), 40 * 1024 * 1024),
    )(dck, dcv, w_full, ctx, cshift, cscale, norm_g)


def _lane_pad_rpb(rpb):
    r = jnp.pad(rpb, ((0, 0), (0, 0), (0, GRID_W - rpb.shape[-1])))
    return jnp.concatenate([r, r], axis=-1)


def local_step(chip, dev, x, c_vec, c_ctx, w_ada, b_shard, ctx, target, norm_g, sgu_g, w_s, b_s, q_g, k_g, rpb,
               w_in_shard, w_out_shard):
    bsb = jnp.broadcast_to(b_s[:, :, None], (4, 128, 128))
    qg2, kg2 = jnp.tile(q_g, (1, 2)), jnp.tile(k_g, (1, 2))

    z, h, w_in_full, w_out_full, mod_all, cs = inproj_fwd(chip, x, c_vec, c_ctx, w_ada, b_shard, norm_g, w_in_shard,
                                                          w_out_shard)
    mods = mod_all.transpose(1, 0, 2).reshape(CS_ROWS, 3 * DM)
    mod = lax.dynamic_slice(mods, (8 * dev, 0), (1, 3 * DM))
    shift, scale, gate = mod[:, :DM], mod[:, DM:2 * DM], mod[:, 2 * DM:]
    cshift, cscale = mods[8 * NDEV:8 * NDEV + 1, :DM], mods[8 * NDEV:8 * NDEV + 1, DM:2 * DM]
    zc, hc = ctx_fwd(ctx, cshift, cscale, norm_g, w_in_full)
    bias = rpb_tables(_lane_pad_rpb(rpb))
    out_a = sgu_fwd(z, sgu_g, w_s, bsb)
    out_b, *saved = attn_fwd(z, zc, bias, qg2, kg2)
    loss8, dy, dcat, dgate, dwo = outproj(out_a, out_b, x, target, gate, w_out_full.reshape(DM, DM))
    dz_a, dsg, dws, dbsb = sgu_bwd(z, sgu_g, w_s, bsb, dcat)
    dq, dk, dv, dbg, dck, dcv, dbias, dqg2, dkg2 = attn_bwd(z, zc, qg2, kg2, dcat, saved)
    drpb = rpb_bwd(dbias)[:, :, :rpb.shape[-1]]
    dz_parts = (dz_a, dq, dk, dv, dbg)
    dcshift, dcscale, dng_c = ctx_bwd(dck, dcv, w_in_full, ctx, cshift, cscale, norm_g)
    wire_i, keep_i, wire_o, keep_o = dw_bwd(h, dz_parts, hc, dck, dcv, dwo.reshape(NCHIP, SHARD_OUT, DM))
    *in_flight, token = rs_start(wire_i, wire_o)
    grad_x, dshift, dscale, dng = dh_bwd(dz_parts, w_in_full, x, dy, shift, scale, norm_g, dng_c + token[0, 0])
    got_i, got_o = rs_wait(*in_flight, dshift)
    return dict(
        loss=loss8[0:1, 0:1], grad_x=grad_x, rs=(keep_i, got_i, keep_o, got_o), cs=cs,
        dmod=jnp.concatenate([dshift, dscale, dgate], axis=-1),
        dcmod=jnp.concatenate([dcshift, dcscale, jnp.zeros((1, DM), F32)], axis=-1),
        d_norm_g=dng, d_sgu_g=dsg, d_w_s=dws, d_b_s=dbsb[:, :, 0],
        d_q_g=dqg2[:, :HDIM], d_k_g=dkg2[:, :HDIM], d_rpb=drpb)


def _me():
    return lax.axis_index("x"), lax.axis_index("y"), lax.axis_index("c")


def _flip(q):
    x, y, c = _me()
    return ((1 - x) if q & 4 else x, (1 - y) if q & 2 else y, (1 - c) if q & 1 else c)


def _chip_of(dev):
    return 2 * dev[0] + dev[1]


def _rcopy(src, dst, send_sems, recv_sems, k, dev):
    return pltpu.make_async_remote_copy(src_ref=src, dst_ref=dst, send_sem=send_sems.at[k], recv_sem=recv_sems.at[k],
                                        device_id=dev, device_id_type=MESH_ID)


_VMEM_SPEC = pl.BlockSpec(memory_space=pltpu.VMEM)
SLAB_ROWS = 80


RS_SHAPES = ((DM // 2, SHARD_IN), (SHARD_OUT // 2, DM))
_HBM_SPEC = pl.BlockSpec(memory_space=pltpu.HBM)
_SEM_SPEC = pl.BlockSpec(memory_space=pltpu.SEMAPHORE)
_IN_FLIGHT = pltpu.SideEffectType.DATAFLOW_SIDE_EFFECTING


def _rs_copies(wires, lands, send_sems, recv_sems):
    return [pltpu.make_async_remote_copy(
        src_ref=wires[n].at[_chip_of(_flip(q))], dst_ref=lands[n].at[q // 2 - 1],
        send_sem=send_sems.at[3 * n + q // 2 - 1], recv_sem=recv_sems.at[3 * n + q // 2 - 1],
        device_id=_flip(q), device_id_type=MESH_ID) for n in (0, 1) for q in (2, 4, 6)]


def rs_start(wire_i, wire_o):
    lands = [lax.empty((NCHIP - 1, rh, w), BF16) for rh, w in RS_SHAPES]

    def body(wi_ref, wo_ref, li_ref, lo_ref, send_sems, recv_sems, wi_thru, wo_thru, li_thru, lo_thru, token):
        for cp in _rs_copies((wi_ref, wo_ref), (li_ref, lo_ref), send_sems, recv_sems):
            cp.start()
        token[...] = jnp.zeros_like(token)

    hbm = lambda a: pltpu.HBM(a.shape, a.dtype)
    return pl.pallas_call(
        body, name="rs_start",
        out_shape=(pltpu.SemaphoreType.DMA((6,)), pltpu.SemaphoreType.DMA((6,)), hbm(wire_i), hbm(wire_o),
                   hbm(lands[0]), hbm(lands[1]), jax.ShapeDtypeStruct((8, 128), F32)),
        in_specs=(_HBM_SPEC,) * 4, out_specs=(_SEM_SPEC, _SEM_SPEC) + (_HBM_SPEC,) * 4 + (_VMEM_SPEC,),
        input_output_aliases={0: 2, 1: 3, 2: 4, 3: 5},
        compiler_params=pltpu.CompilerParams(has_side_effects=_IN_FLIGHT),
    )(*[pltpu.with_memory_space_constraint(a, pltpu.HBM) for a in (wire_i, wire_o, *lands)])


def rs_wait(send_sems, recv_sems, wire_i, wire_o, land_i, land_o, after):
    def body(wi_ref, wo_ref, li_ref, lo_ref, send_sems, recv_sems, after_ref, wi_dead, wo_dead, gi_ref, go_ref):
        for cp in _rs_copies((wi_ref, wo_ref), (li_ref, lo_ref), send_sems, recv_sems):
            cp.wait_send()
            cp.wait_recv()

    hbm = lambda a: pltpu.HBM(a.shape, a.dtype)
    return pl.pallas_call(
        body, name="rs_wait", out_shape=(hbm(wire_i), hbm(wire_o), hbm(land_i), hbm(land_o)),
        in_specs=(_HBM_SPEC,) * 4 + (_SEM_SPEC, _SEM_SPEC, pl.BlockSpec(memory_space=pl.ANY)),
        out_specs=(_HBM_SPEC,) * 4, input_output_aliases={0: 0, 1: 1, 2: 2, 3: 3},
        compiler_params=pltpu.CompilerParams(has_side_effects=_IN_FLIGHT),
    )(wire_i, wire_o, land_i, land_o, send_sems, recv_sems, after)[2:]


def final_reduce(keep_i, got_i, keep_o, got_o, slab):
    def kern(ki_ref, gi_ref, ko_ref, go_ref, s_ref, gin_ref, gout_ref, all_ref, tot_ref, send_sems, recv_sems):
        x, y, c = _me()
        sib = _flip(1)
        dev = lambda d: 4 * d[0] + 2 * d[1] + d[2]
        me = dev((x, y, c))

        def slab_copy(idx, owner, to):
            return _rcopy(all_ref.at[dev(owner)], all_ref.at[dev(owner)], send_sems, recv_sems, idx, to)

        all_ref[me] = s_ref[...]
        first = [slab_copy(0, (x, y, c), sib)] + [slab_copy(q // 2, (x, y, c), _flip(q)) for q in (2, 4, 6)]
        for cp in first:
            cp.start()

        shares = []
        for n, (keep, got, out) in enumerate(((ki_ref, gi_ref, gin_ref), (ko_ref, go_ref, gout_ref))):
            rh = RS_SHAPES[n][0]
            half = lambda hh, rh=rh: pl.ds(pl.multiple_of(hh * rh, rh), rh)
            out[half(c), :] = ((keep[...] + got[0].astype(F32)) + got[1].astype(F32)) + got[2].astype(F32)
            share = _rcopy(out.at[half(c), :], out.at[half(c), :], send_sems, recv_sems, 7 + n, sib)
            share.start()
            shares.append((share, _rcopy(out.at[half(1 - c), :], out.at[half(1 - c), :], send_sems, recv_sems, 7 + n,
                                         sib)))

        passed = []
        for q in (2, 4, 6):
            slab_copy(q // 2, _flip(q), (x, y, c)).wait_recv()
            cp = slab_copy(3 + q // 2, _flip(q), sib)
            cp.start()
            passed.append(cp)
        slab_copy(0, sib, (x, y, c)).wait_recv()
        for q in (2, 4, 6):
            slab_copy(3 + q // 2, _flip(q | 1), (x, y, c)).wait_recv()
        tot = all_ref[0]
        for d in range(1, NDEV):
            tot = tot + all_ref[d]
        tot_ref[...] = tot
        for share, arrival in shares:
            arrival.wait_recv()
            share.wait_send()
        for cp in first + passed:
            cp.wait_send()

    (rhi, wi), (rho, wo) = RS_SHAPES
    return pl.pallas_call(
        kern, name="final_reduce", in_specs=[_VMEM_SPEC] * 5, out_specs=[_VMEM_SPEC] * 4,
        out_shape=[jax.ShapeDtypeStruct((2 * rhi, wi), F32), jax.ShapeDtypeStruct((2 * rho, wo), F32),
                   jax.ShapeDtypeStruct((NDEV, SLAB_ROWS, DM), F32), jax.ShapeDtypeStruct((SLAB_ROWS, DM), F32)],
        scratch_shapes=[pltpu.SemaphoreType.DMA((9,)), pltpu.SemaphoreType.DMA((9,))],
        compiler_params=pltpu.CompilerParams(vmem_limit_bytes=40 * 1024 * 1024),
    )(keep_i, got_i, keep_o, got_o, slab)


def ada_bwd(a_in, dm, dm_shard, w_ada, c_ctx):
    def kern(a_ref, dm_ref, dms_ref, w_ref, cc_ref, dw_ref, db_ref, dcc_ref, parts, send_sems, recv_sems):
        x, y, c = _me()
        k = 2 * x + y
        act = jax.nn.silu(a_ref[...]).astype(BF16)
        dms = dms_ref[...].astype(BF16)
        dw_ref[...] = lax.dot_general(act, dms, (((0,), (0,)), ((), ())), preferred_element_type=F32)
        db_ref[...] = jnp.sum(dm_ref[...], axis=0, keepdims=True)
        parts[k] = lax.dot_general(dms, w_ref[...].astype(BF16), (((1,), (1,)), ((), ())), preferred_element_type=F32)
        sends = [_rcopy(parts.at[k], parts.at[k], send_sems, recv_sems, q // 2 - 1, _flip(q)) for q in (2, 4, 6)]
        for cp in sends:
            cp.start()
        for q in (2, 4, 6):
            kq = _chip_of(_flip(q))
            _rcopy(parts.at[kq], parts.at[kq], send_sems, recv_sems, q // 2 - 1, _flip(q)).wait_recv()
        dact = ((parts[0] + parts[1]) + parts[2]) + parts[3]
        _, vjp = jax.vjp(jax.nn.silu, cc_ref[...])
        dcc_ref[...] = vjp(dact[8:9, :])[0]
        for cp in sends:
            cp.wait_send()

    return pl.pallas_call(
        kern, name="ada_bwd", in_specs=[_VMEM_SPEC] * 5, out_specs=[_VMEM_SPEC] * 3,
        out_shape=[jax.ShapeDtypeStruct((DM, SHARD_ADA), F32), jax.ShapeDtypeStruct((1, 3 * DM), F32),
                   jax.ShapeDtypeStruct((1, DM), F32)],
        scratch_shapes=[pltpu.VMEM((NCHIP, 16, DM), F32), pltpu.SemaphoreType.DMA((3,)), pltpu.SemaphoreType.DMA((3,))],
    )(a_in, dm, dm_shard, w_ada, c_ctx)


def _adamw_math(w, g, m, v):
    m = B1 * m + (1.0 - B1) * g
    v = B2 * v + (1.0 - B2) * (g * g)
    m_hat = m / (1.0 - B1 ** STEP)
    v_hat = v / (1.0 - B2 ** STEP)
    return -LR * (m_hat / (jnp.sqrt(v_hat) + ADAM_EPS) + WD * w), m, v


def adamw_big(w, g, m, v, name, block_rows=256):
    rows, width = w.shape

    def kern(w_ref, g_ref, m_ref, v_ref, d_ref, nm_ref, nv_ref):
        d_ref[...], nm_ref[...], nv_ref[...] = _adamw_math(w_ref[...], g_ref[...], m_ref[...], v_ref[...])

    spec = pl.BlockSpec((block_rows, width), lambda i: (i, 0))
    return pl.pallas_call(
        kern, name=name, grid=(rows // block_rows,), in_specs=[spec] * 4, out_specs=[spec] * 3,
        out_shape=[jax.ShapeDtypeStruct((rows, width), F32)] * 3,
        compiler_params=_cparams(("arbitrary",)),
    )(w, g, m, v)


def adamw_small(quads):
    n = len(quads)

    def kern(*refs):
        ins, outs = refs[:4 * n], refs[4 * n:]
        for i in range(n):
            w, g, m, v = (r[...] for r in ins[4 * i:4 * i + 4])
            outs[3 * i][...], outs[3 * i + 1][...], outs[3 * i + 2][...] = _adamw_math(w, g, m, v)

    flat = [a for quad in quads for a in quad]
    res = pl.pallas_call(
        kern, name="adamw_small", in_specs=[_VMEM_SPEC] * (4 * n), out_specs=[_VMEM_SPEC] * (3 * n),
        out_shape=[jax.ShapeDtypeStruct(q[0].shape, F32) for q in quads for _ in range(3)],
    )(*flat)
    return [tuple(res[3 * i:3 * i + 3]) for i in range(n)]


def _rows_of(a, rows):
    flat = a.reshape(-1)
    return jnp.pad(flat, (0, rows * DM - flat.shape[0])).reshape(rows, DM)


def kernel(x, c, ctx, c_ctx, w_ada, b_ada, norm_g, w_in, sgu_norm_g, w_spatial, b_spatial, q_norm_g, k_norm_g, rpb, w_out, loss_target, m_c_ctx, m_w_ada, m_b_ada, m_norm_g, m_w_in, m_sgu_norm_g, m_w_spatial, m_b_spatial, m_q_norm_g, m_k_norm_g, m_rpb, m_w_out, v_c_ctx, v_w_ada, v_b_ada, v_norm_g, v_w_in, v_sgu_norm_g, v_w_spatial, v_b_spatial, v_q_norm_g, v_k_norm_g, v_rpb, v_w_out):
    xi, yi, ci = lax.axis_index("x"), lax.axis_index("y"), lax.axis_index("c")
    chip, dev = 2 * xi + yi, 4 * xi + 2 * yi + ci
    c_ctx2 = c_ctx.reshape(1, DM)

    b_shard = lax.dynamic_slice(b_ada, (0, chip * SHARD_ADA), (1, SHARD_ADA))
    part = local_step(chip.reshape(1).astype(jnp.int32), dev, x[0], c, c_ctx2, w_ada[0], b_shard, ctx[0], loss_target[0],
                      norm_g, sgu_norm_g, w_spatial[0], b_spatial[0], q_norm_g, k_norm_g, rpb[0], w_in[0], w_out[0])
    cs = part["cs"]

    slab = jnp.concatenate([
        part["d_norm_g"], _rows_of(part["d_sgu_g"], 1), _rows_of(part["d_b_s"], 1),
        _rows_of(jnp.concatenate([part["d_q_g"], part["d_k_g"]], axis=-1), 1), _rows_of(part["d_rpb"], 4),
        _rows_of(part["loss"], 1), _rows_of(part["dcmod"], 3), _rows_of(part["dmod"], 3), jnp.zeros((1, DM), F32),
        _rows_of(part["d_w_s"], 64)], axis=0)
    g_w_in, g_w_out, gathered, tot = final_reduce(*part["rs"], slab)
    dm = jnp.concatenate([gathered[:, 12:15, :].reshape(NDEV, 3 * DM), tot[9:12].reshape(1, 3 * DM),
                          jnp.zeros((7, 3 * DM), F32)], axis=0)
    a_in = jnp.concatenate([cs[0:8 * NDEV:8], cs[8 * NDEV:8 * NDEV + 1], jnp.zeros((7, DM), F32)], axis=0)
    dm_shard = lax.dynamic_slice(dm, (0, chip * SHARD_ADA), (16, SHARD_ADA))
    g_w_ada, g_b_ada, g_c_ctx = ada_bwd(a_in, dm, dm_shard, w_ada[0], c_ctx2)

    loss = tot[8, 0]
    g_small = dict(
        c_ctx=g_c_ctx, b_ada=g_b_ada, norm_g=tot[0:1], sgu_norm_g=tot[1:2, :512], w_spatial=tot[16:80].reshape(512, 128),
        b_spatial=tot[2:3, :512].reshape(4, 128), q_norm_g=tot[3:4, :HDIM], k_norm_g=tot[3:4, HDIM:2 * HDIM],
        rpb=tot[4:8].reshape(-1)[:HEADS * 15 * 31].reshape(HEADS * 15, 31))
    shapes = dict(c_ctx=(DM,), w_ada=(1, DM, SHARD_ADA), b_ada=(1, 3 * DM), norm_g=(1, DM), w_in=(1, DM, SHARD_IN),
                  sgu_norm_g=(1, 512), w_spatial=(1, 4, 128, 128), b_spatial=(1, 4, 128), q_norm_g=(1, HDIM),
                  k_norm_g=(1, HDIM), rpb=(1, HEADS, 15, 31), w_out=(1, SHARD_OUT, DM))
    names = list(shapes)
    weights = dict(c_ctx=c_ctx, w_ada=w_ada, b_ada=b_ada, norm_g=norm_g, w_in=w_in, sgu_norm_g=sgu_norm_g,
                   w_spatial=w_spatial, b_spatial=b_spatial, q_norm_g=q_norm_g, k_norm_g=k_norm_g, rpb=rpb, w_out=w_out)
    m_in = dict(zip(names, (m_c_ctx, m_w_ada, m_b_ada, m_norm_g, m_w_in, m_sgu_norm_g, m_w_spatial, m_b_spatial,
                            m_q_norm_g, m_k_norm_g, m_rpb, m_w_out)))
    v_in = dict(zip(names, (v_c_ctx, v_w_ada, v_b_ada, v_norm_g, v_w_in, v_sgu_norm_g, v_w_spatial, v_b_spatial,
                            v_q_norm_g, v_k_norm_g, v_rpb, v_w_out)))
    grads = dict(g_small, w_ada=g_w_ada, w_in=g_w_in, w_out=g_w_out)
    upd = {}
    for n in ("w_ada", "w_in", "w_out"):
        g = grads[n]
        upd[n] = adamw_big(weights[n].reshape(g.shape), g, m_in[n].reshape(g.shape), v_in[n].reshape(g.shape),
                           "adamw_" + n)
    small = [n for n in names if n not in upd]
    res = adamw_small([(weights[n].reshape(grads[n].shape), grads[n], m_in[n].reshape(grads[n].shape),
                        v_in[n].reshape(grads[n].shape)) for n in small])
    upd.update(zip(small, res))
    out = [loss, part["grad_x"].reshape(1, SEQ, DM)]
    out += [grads[n].reshape(shapes[n]) for n in names]
    for slot in range(3):
        out += [upd[n][slot].reshape(shapes[n]) for n in names]
    return tuple(out)
```

```python
import functools

import jax
import jax.numpy as jnp
from jax import lax
from jax.experimental import pallas as pl
from jax.experimental.pallas import tpu as pltpu

F32, BF16 = jnp.float32, jnp.bfloat16
SEQ, DM, CTX, DIN = 4096, 1024, 256, 3584
NCHIP, NDEV = 4, 8
SHARD_IN = DIN // NCHIP
SHARD_ADA = 3 * DM // NCHIP
SHARD_OUT = DM // NCHIP
GRID_W = 64
QROWS = 4
KROWS = 12
QBLK, KBLK = QROWS * GRID_W, KROWS * GRID_W
NQBLK = SEQ // QBLK
HEADS, HDIM, NPAIR = 8, 64, 4
EPS = 1e-6
NEG_INF = -1e30
ZQ, ZK, ZV, ZG = 12, 16, 20, 24
LR, B1, B2, ADAM_EPS, WD, STEP = 0.001, 0.9, 0.999, 1e-08, 0.01, 10
VMEM_BIG = 56 * 1024 * 1024
MESH_ID = pl.DeviceIdType.MESH


def _dot(a, b, lhs_c, rhs_c):
    return lax.dot_general(a.astype(BF16), b.astype(BF16), (((lhs_c,), (rhs_c,)), ((), ())),
                           preferred_element_type=F32)


@jax.custom_vjp
def mm(a, b):
    return _dot(a, b, 1, 0)


@jax.custom_vjp
def mm_nt(a, b):
    return _dot(a, b, 1, 1)


@jax.custom_vjp
def mm_tn(a, b):
    return _dot(a, b, 0, 0)


mm.defvjp(lambda a, b: (mm(a, b), (a, b)), lambda r, ct: (mm_nt(ct, r[1]), mm_tn(r[0], ct)))
mm_nt.defvjp(lambda a, b: (mm_nt(a, b), (a, b)), lambda r, ct: (mm(ct, r[1]), mm_tn(ct, r[0])))
mm_tn.defvjp(lambda a, b: (mm_tn(a, b), (a, b)), lambda r, ct: (mm_nt(r[1], ct), mm(r[0], ct)))


def _rms(x, g):
    return x * lax.rsqrt(jnp.mean(x * x, axis=-1, keepdims=True) + EPS) * g


def _modulated(x, g, scale, shift):
    return _rms(x, g) * (1.0 + scale) + shift


def _pair_rms(x, g2):
    lo = lax.broadcasted_iota(jnp.int32, (1, 2 * HDIM), 1) < HDIM
    sq = x * x
    s_lo = jnp.sum(jnp.where(lo, sq, 0.0), axis=-1, keepdims=True)
    s_hi = jnp.sum(jnp.where(lo, 0.0, sq), axis=-1, keepdims=True)
    rs = jnp.where(lo, lax.rsqrt(s_lo / HDIM + EPS), lax.rsqrt(s_hi / HDIM + EPS))
    return x * rs * g2


def _cparams(sem, vmem=None):
    return pltpu.CompilerParams(dimension_semantics=sem, vmem_limit_bytes=vmem)


def _row(n):
    return pl.BlockSpec((1, n), lambda *_: (0, 0))


CS_ROWS = 8 * NDEV + 8


def _mod_part(mod_ref, row, part):
    pieces = []
    for j in range(NCHIP):
        lo, hi = max(part * DM, j * SHARD_ADA), min((part + 1) * DM, (j + 1) * SHARD_ADA)
        if lo < hi:
            pieces.append(mod_ref[j, row, lo - j * SHARD_ADA:hi - j * SHARD_ADA])
    return jnp.concatenate(pieces, axis=-1)


def inproj_fwd(chip, x, c_vec, c_ctx, w_ada, b_shard, norm_g, w_shard, wo_shard):
    tl = 1024
    nt = SEQ // tl
    halves = (DM // 2, SHARD_OUT // 2)
    n_w, n_c = 12, NDEV - 1

    def kern(k_ref, x_ref, cv_ref, cc_ref, wa_ref, b_ref, g_ref, w_ref, wo_ref,
             z_ref, h_ref, wfull_ref, wofull_ref, modall_ref, csall_ref,
             w_scr, wo_scr, h_scr, mine, cs_scr, mod_scr, shsc_scr, send_sems, recv_sems):
        s, t = pl.program_id(0), pl.program_id(1)
        xi, yi, c = _me()
        k, me = 2 * xi + yi, 4 * xi + 2 * yi + c
        sib = _flip(1)
        rows = pl.ds(pl.multiple_of(t * tl, tl), tl)
        gathered = (w_scr, wo_scr)
        slot = lambda d: pl.ds(pl.multiple_of(8 * d, 8), 8)

        def c_copy(q, owner):
            return _rcopy(mine, cs_scr.at[slot(owner), :], send_sems, recv_sems, n_w + q - 1, _flip(q))

        def m_copy(q, chip_of_block):
            return _rcopy(mod_scr.at[chip_of_block], mod_scr.at[chip_of_block], send_sems, recv_sems,
                          n_w + n_c + q // 2 - 1, _flip(q))

        def adaln():
            first = lax.broadcasted_iota(jnp.int32, (8, DM), 0) == 0
            mine[...] = jnp.where(first, jnp.broadcast_to(cv_ref[...], (8, DM)), 0.0)
            cs_scr[slot(me), :] = mine[...]
            cs_scr[slot(NDEV), :] = jnp.where(first, jnp.broadcast_to(cc_ref[...], (8, DM)), 0.0)
            for q in range(1, NDEV):
                c_copy(q, me).start()
            wa = wa_ref[...].astype(BF16)
            for q in range(1, NDEV):
                px, py, pc = _flip(q)
                c_copy(q, 4 * px + 2 * py + pc).wait_recv()
            act = jax.nn.silu(cs_scr[...]).astype(BF16)
            mod_scr[k] = jnp.dot(act, wa, preferred_element_type=F32) + b_ref[...]
            for q in (2, 4, 6):
                m_copy(q, k).start()
            for q in (2, 4, 6):
                m_copy(q, _chip_of(_flip(q))).wait_recv()
            row = pl.ds(8 * me, 1)
            shsc_scr[0:1, :] = _mod_part(mod_scr, row, 0)
            shsc_scr[1:2, :] = _mod_part(mod_scr, row, 1)
            pltpu.sync_copy(mod_scr, modall_ref)
            pltpu.sync_copy(cs_scr, csall_ref)

        def block(n, chip_of_block, hh):
            return gathered[n].at[chip_of_block, pl.ds(pl.multiple_of(hh * halves[n], halves[n]), halves[n]), :]

        def ici(n, q, chip_of_block):
            blk = block(n, chip_of_block, c)
            return _rcopy(blk, blk, send_sems, recv_sems, 6 * n + q // 2 - 1, _flip(q))

        def d2d(n, q, chip_of_block, hh):
            blk = block(n, chip_of_block, hh)
            return _rcopy(blk, blk, send_sems, recv_sems, 6 * n + 3 + q // 2 - 1, sib)

        @pl.when((s == 0) & (t == 0))
        def _():
            adaln()
            w_scr[k] = w_ref[...].astype(BF16)
            wo_scr[k] = wo_ref[...].astype(BF16)
            for q in (2, 4, 6):
                ici(0, q, k).start()
                ici(1, q, k).start()

        for sweep in (1, 2, 3):
            @pl.when((s == sweep) & (t == 0))
            def _():
                q = 2 * sweep
                src = _chip_of(_flip(q))
                for n in (0, 1):
                    ici(n, q, src).wait_recv()
                    d2d(n, q, src, c).start()
                for n in (0, 1):
                    d2d(n, q, src, 1 - c).wait_recv()

        @pl.when(s == 0)
        def _():
            hb = _modulated(x_ref[...], g_ref[...], shsc_scr[1:2, :], shsc_scr[0:1, :]).astype(BF16)
            h_scr[rows, :] = hb
            h_ref[...] = hb

        z_ref[...] = jnp.dot(h_scr[rows, :], w_scr[lax.bitwise_xor(k, s)], preferred_element_type=F32)

        @pl.when((s == NCHIP - 1) & (t == nt - 1))
        def _():
            for q in range(1, NDEV):
                c_copy(q, me).wait_send()
            for q in (2, 4, 6):
                m_copy(q, k).wait_send()
            for n in (0, 1):
                for q in (2, 4, 6):
                    ici(n, q, k).wait_send()
                    d2d(n, q, _chip_of(_flip(q)), c).wait_send()
            pltpu.sync_copy(w_scr, wfull_ref)
            pltpu.sync_copy(wo_scr, wofull_ref)

    once = lambda s, t, k: (jnp.where(s == 0, t, nt - 1), 0)
    hbm = pl.BlockSpec(memory_space=pl.ANY)
    n_sem = n_w + n_c + 3
    return pl.pallas_call(
        kern, name="inproj_fwd",
        grid_spec=pltpu.PrefetchScalarGridSpec(
            num_scalar_prefetch=1, grid=(NCHIP, nt),
            in_specs=[pl.BlockSpec((tl, DM), once)] + [_VMEM_SPEC] * 7,
            out_specs=[pl.BlockSpec((tl, SHARD_IN), lambda s, t, k: (t, lax.bitwise_xor(k[0], s))),
                       pl.BlockSpec((tl, DM), once), hbm, hbm, hbm, hbm],
            scratch_shapes=[pltpu.VMEM((NCHIP, DM, SHARD_IN), BF16), pltpu.VMEM((NCHIP, SHARD_OUT, DM), BF16),
                            pltpu.VMEM((SEQ, DM), BF16), pltpu.VMEM((8, DM), F32), pltpu.VMEM((CS_ROWS, DM), F32),
                            pltpu.VMEM((NCHIP, CS_ROWS, SHARD_ADA), F32), pltpu.VMEM((8, DM), F32),
                            pltpu.SemaphoreType.DMA((n_sem,)), pltpu.SemaphoreType.DMA((n_sem,))]),
        out_shape=[jax.ShapeDtypeStruct((SEQ, DIN), F32), jax.ShapeDtypeStruct((SEQ, DM), BF16),
                   jax.ShapeDtypeStruct((NCHIP, DM, SHARD_IN), BF16), jax.ShapeDtypeStruct((NCHIP, SHARD_OUT, DM), BF16),
                   jax.ShapeDtypeStruct((NCHIP, CS_ROWS, SHARD_ADA), F32), jax.ShapeDtypeStruct((CS_ROWS, DM), F32)],
        compiler_params=_cparams(("arbitrary", "arbitrary"), VMEM_BIG),
    )(chip, x, c_vec, c_ctx, w_ada, b_shard, norm_g, w_shard, wo_shard)


def ctx_fwd(ctx, cshift, cscale, norm_g, w_full):
    def kern(c_ref, sh_ref, sc_ref, g_ref, w2_ref, w3_ref, zc_ref, hc_ref):
        hc = _modulated(c_ref[...], g_ref[...], sc_ref[...], sh_ref[...]).astype(BF16)
        hc_ref[...] = hc
        zc_ref[:, :SHARD_IN] = jnp.dot(hc, w2_ref[0], preferred_element_type=F32)
        zc_ref[:, SHARD_IN:] = jnp.dot(hc, w3_ref[0], preferred_element_type=F32)

    return pl.pallas_call(
        kern, name="ctx_fwd", grid=(1,),
        in_specs=[pl.BlockSpec((CTX, DM), lambda i: (0, 0)), _row(DM), _row(DM), _row(DM),
                  pl.BlockSpec((1, DM, SHARD_IN), lambda i: (2, 0, 0)),
                  pl.BlockSpec((1, DM, SHARD_IN), lambda i: (3, 0, 0))],
        out_specs=[pl.BlockSpec((CTX, 2 * SHARD_IN), lambda i: (0, 0)),
                   pl.BlockSpec((CTX, DM), lambda i: (0, 0))],
        out_shape=[jax.ShapeDtypeStruct((CTX, 2 * SHARD_IN), F32), jax.ShapeDtypeStruct((CTX, DM), BF16)],
        compiler_params=_cparams(("arbitrary",)),
    )(ctx, cshift, cscale, norm_g, w_full, w_full)


SGU_CHUNK, SGU_PER_STEP = 128, 4


def _gelu(x):
    return 0.5 * x * (1.0 + lax.erf(x * 0.7071067811865476))


def _sgu_chunk(au, av, ag, sg, ws, bsb):
    u, v = _gelu(au), _gelu(av)
    outs = []
    for g in range(4):
        sl = slice(128 * g, 128 * (g + 1))
        mixed = mm(ws[g], _rms(v[:, sl], sg[:, sl])) + bsb[g]
        outs.append(u[:, sl] * mixed * jax.nn.silu(ag[:, sl]))
    return jnp.concatenate(outs, axis=-1)


def _sgu_specs():
    rows = SGU_CHUNK * SGU_PER_STEP
    zspec = lambda c: pl.BlockSpec((rows, 512), lambda n: (n, c))
    wspec = pl.BlockSpec((4, 128, 128), lambda n: (0, 0, 0))
    return rows, [zspec(0), zspec(1), zspec(2), _row(512), wspec, wspec]


def sgu_fwd(z, sg, ws, bsb):
    rows, in_specs = _sgu_specs()

    def kern(au_ref, av_ref, ag_ref, sg_ref, ws_ref, bs_ref, o_ref):
        for c in range(SGU_PER_STEP):
            sl = slice(c * SGU_CHUNK, (c + 1) * SGU_CHUNK)
            o_ref[sl, :] = _sgu_chunk(au_ref[sl, :], av_ref[sl, :], ag_ref[sl, :], sg_ref[...], ws_ref[...],
                                      bs_ref[...])

    return pl.pallas_call(
        kern, name="sgu_fwd", grid=(SEQ // rows,), in_specs=in_specs,
        out_specs=pl.BlockSpec((rows, 512), lambda n: (n, 0)),
        out_shape=jax.ShapeDtypeStruct((SEQ, 512), F32),
        compiler_params=_cparams(("arbitrary",)),
    )(z, z, z, sg, ws, bsb)


def sgu_bwd(z, sg, ws, bsb, dcat):
    rows, in_specs = _sgu_specs()

    def kern(au_ref, av_ref, ag_ref, sg_ref, ws_ref, bs_ref, do_ref, dz_ref, dsg_ref, dws_ref, dbs_ref):
        @pl.when(pl.program_id(0) == 0)
        def _():
            dsg_ref[...] = jnp.zeros_like(dsg_ref)
            dws_ref[...] = jnp.zeros_like(dws_ref)
            dbs_ref[...] = jnp.zeros_like(dbs_ref)

        for c in range(SGU_PER_STEP):
            sl = slice(c * SGU_CHUNK, (c + 1) * SGU_CHUNK)
            _, vjp = jax.vjp(_sgu_chunk, au_ref[sl, :], av_ref[sl, :], ag_ref[sl, :], sg_ref[...], ws_ref[...],
                             bs_ref[...])
            dau, dav, dag, dsg, dws, dbs = vjp(do_ref[sl, :])
            dz_ref[sl, 0:512] = dau.astype(BF16)
            dz_ref[sl, 512:1024] = dav.astype(BF16)
            dz_ref[sl, 1024:1536] = dag.astype(BF16)
            dsg_ref[...] += dsg
            dws_ref[...] += dws
            dbs_ref[...] += dbs

        @pl.when(pl.program_id(0) == pl.num_programs(0) - 1)
        def _():
            dbs_ref[...] = jnp.broadcast_to(jnp.sum(dbs_ref[...], axis=-1, keepdims=True), dbs_ref.shape)

    wspec = pl.BlockSpec((4, 128, 128), lambda n: (0, 0, 0))
    return pl.pallas_call(
        kern, name="sgu_bwd", grid=(SEQ // rows,),
        in_specs=in_specs + [pl.BlockSpec((rows, 512), lambda n: (n, 0))],
        out_specs=[pl.BlockSpec((rows, 1536), lambda n: (n, 0)), _row(512), wspec, wspec],
        out_shape=[jax.ShapeDtypeStruct((SEQ, 1536), BF16), jax.ShapeDtypeStruct((1, 512), F32),
                   jax.ShapeDtypeStruct((4, 128, 128), F32), jax.ShapeDtypeStruct((4, 128, 128), F32)],
        compiler_params=_cparams(("arbitrary",)),
    )(z, z, z, sg, ws, bsb, dcat)


_DR_OFF = (7, 3, -1)


def _row_valid(v, rr, j):
    return (j < 8, rr <= j < rr + 8, 4 <= j < 12)[v]


def _col_window():
    q = lax.broadcasted_iota(jnp.int32, (GRID_W, 128), 0)
    kc = lax.broadcasted_iota(jnp.int32, (GRID_W, 128), 1) % GRID_W
    c0 = jnp.clip(q - 8, 0, GRID_W - 16)
    return (kc >= c0) & (kc < c0 + 16)


def rpb_tables(rpb2):
    def kern(r_ref, b_ref):
        base = r_ref[0]
        lo = lax.broadcasted_iota(jnp.int32, (1, 128), 1) < GRID_W
        win = _col_window()
        tiles = {}
        for v in range(3):
            for rr in range(QROWS):
                for jp in range(KROWS // 2):
                    j0, j1 = 2 * jp, 2 * jp + 1
                    ok0, ok1 = _row_valid(v, rr, j0), _row_valid(v, rr, j1)
                    key = (j0 - rr + _DR_OFF[v], ok0, ok1) if (ok0 or ok1) else None
                    if key not in tiles:
                        if key is None:
                            tiles[key] = jnp.full((GRID_W, 128), NEG_INF, F32)
                        else:
                            d0 = key[0]
                            r0 = base[d0:d0 + 1, :] if ok0 else jnp.zeros((1, 128), F32)
                            r1 = base[d0 + 1:d0 + 2, :] if ok1 else jnp.zeros((1, 128), F32)
                            y = jnp.broadcast_to(jnp.where(lo, r0, r1), (GRID_W, 128))
                            y = pltpu.roll(pltpu.roll(y, 128 - 15, 1), 0, 1, stride=1, stride_axis=0)
                            tiles[key] = jnp.where(win & jnp.where(lo, ok0, ok1), y, NEG_INF)
                    b_ref[v, 0, rr * GRID_W:(rr + 1) * GRID_W, jp * 128:(jp + 1) * 128] = tiles[key]

    return pl.pallas_call(
        kern, name="rpb_tables", grid=(HEADS,),
        in_specs=[pl.BlockSpec((1, 15, 128), lambda h: (h, 0, 0))],
        out_specs=pl.BlockSpec((3, 1, QBLK, KBLK), lambda h: (0, h, 0, 0)),
        out_shape=jax.ShapeDtypeStruct((3, HEADS, QBLK, KBLK), F32),
        compiler_params=_cparams(("arbitrary",)),
    )(rpb2)


def rpb_bwd(dbias):
    def kern(g0_ref, g1_ref, g2_ref, o_ref):
        g_refs = (g0_ref.at[0], g1_ref.at[0], g2_ref.at[0])
        lo = lax.broadcasted_iota(jnp.int32, (1, 128), 1) < GRID_W
        ri = lax.broadcasted_iota(jnp.int32, (GRID_W, GRID_W), 0)
        ci = lax.broadcasted_iota(jnp.int32, (GRID_W, GRID_W), 1)
        flip = (ri + ci == GRID_W - 1).astype(F32)
        groups = {}
        for v in range(3):
            for rr in range(QROWS):
                for jp in range(KROWS // 2):
                    j0, j1 = 2 * jp, 2 * jp + 1
                    ok0, ok1 = _row_valid(v, rr, j0), _row_valid(v, rr, j1)
                    if not (ok0 or ok1):
                        continue
                    g = g_refs[v][0, rr * GRID_W:(rr + 1) * GRID_W, jp * 128:(jp + 1) * 128]
                    key = (j0 - rr + _DR_OFF[v], ok0, ok1)
                    groups[key] = g if key not in groups else groups[key] + g
        acc = [jnp.zeros((1, 128), F32) for _ in range(15)]
        for (d0, ok0, ok1), g in groups.items():
            g = lax.dot_general(flip, g, (((1,), (0,)), ((), ())), precision=lax.Precision.HIGHEST,
                                preferred_element_type=F32)
            g = pltpu.roll(pltpu.roll(g, 128 - 48, 1), 0, 1, stride=1, stride_axis=0)
            s = jnp.sum(g, axis=0, keepdims=True)
            if ok0:
                acc[d0] = acc[d0] + jnp.where(lo, s, 0.0)
            if ok1:
                acc[d0 + 1] = acc[d0 + 1] + jnp.where(lo, 0.0, s)
        for d in range(15):
            o_ref[0, d:d + 1, :] = acc[d] + pltpu.roll(acc[d], GRID_W, 1)

    return pl.pallas_call(
        kern, name="rpb_bwd", grid=(HEADS,),
        in_specs=[pl.BlockSpec((1, 1, QBLK, KBLK), functools.partial(lambda v, h: (v, h, 0, 0), v)) for v in range(3)],
        out_specs=pl.BlockSpec((1, 15, 128), lambda h: (h, 0, 0)),
        out_shape=jax.ShapeDtypeStruct((HEADS, 15, 128), F32),
        compiler_params=_cparams(("arbitrary",)),
    )(dbias, dbias, dbias)


def _scaled_q(q_raw, qg):
    return _pair_rms(q_raw, qg) * (HDIM ** -0.5)


def _head_lanes():
    lo = lax.broadcasted_iota(jnp.int32, (1, 2 * HDIM), 1) < HDIM
    return lo, jnp.logical_not(lo)


SOFTMAX_ROWS = 32


def _emit_interleaved(vector_work, matmul_work):
    for j in range(max(len(vector_work), len(matmul_work))):
        for work in (vector_work, matmul_work):
            if j < len(work):
                work[j]()


def _kblock(i):
    return jnp.clip(i - 1, 0, (SEQ - KBLK) // QBLK)


def _kstart(i):
    return pl.multiple_of(_kblock(i) * QBLK, QBLK)


ATTN_STEPS = NQBLK // 2
ATTN_ROWS = 2 * QBLK
KCOLS = QBLK


def _attn_in_specs():
    return [
        pl.BlockSpec((ATTN_ROWS, 128), lambda p, i: (i, ZQ + p)),
        pl.BlockSpec((SEQ, 128), lambda p, i: (0, ZK + p)),
        pl.BlockSpec((SEQ, 128), lambda p, i: (0, ZV + p)),
        pl.BlockSpec((ATTN_ROWS, 128), lambda p, i: (i, ZG + p)),
        pl.BlockSpec((CTX, 128), lambda p, i: (0, 2 + p)),
        pl.BlockSpec((CTX, 128), lambda p, i: (0, 6 + p)),
    ]


def _bias_specs():
    bias_spec = lambda variant: pl.BlockSpec((1, 2, QBLK, KBLK), lambda p, i: (variant(i), p, 0, 0))
    return [bias_spec(lambda i: jnp.where(i == 0, 0, 1)),
            bias_spec(lambda i: jnp.where(i == ATTN_STEPS - 1, 2, 1))]


def _prob_specs():
    return [pl.BlockSpec((2, ATTN_ROWS, KBLK), lambda p, i: (p, i, 0)),
            pl.BlockSpec((2, ATTN_ROWS, CTX), lambda p, i: (p, i, 0))]


NORM_ROWS = 512


def _half_sums(x):
    lo = lax.broadcasted_iota(jnp.int32, (1, 2 * HDIM), 1) < HDIM
    return jnp.where(lo, jnp.sum(jnp.where(lo, x, 0.0), axis=-1, keepdims=True),
                     jnp.sum(jnp.where(lo, 0.0, x), axis=-1, keepdims=True))


def _pair_rms_bwd(x, g2, ct):
    rs = lax.rsqrt(_half_sums(x * x) / HDIM + EPS)
    y = x * rs
    dy = ct * g2
    return rs * (dy - y * (_half_sums(dy * y) / HDIM)), jnp.sum(ct * y, axis=0, keepdims=True)


def _norm_keys(k_ref, ck_ref, kg_ref, kn_scr, ckn_scr):
    def body(c, carry):
        sl = pl.ds(pl.multiple_of(c * NORM_ROWS, NORM_ROWS), NORM_ROWS)
        kn_scr[sl, :] = _pair_rms(k_ref[sl, :], kg_ref[...]).astype(BF16)
        return carry

    lax.fori_loop(0, SEQ // NORM_ROWS, body, 0)
    ckn_scr[...] = _pair_rms(ck_ref[...], kg_ref[...]).astype(BF16)


def _values_with_ones(v_ref, cv_ref, v1_scr, cv1_scr):
    for a, mine in enumerate(_head_lanes()):
        def body(c, carry):
            sl = pl.ds(pl.multiple_of(c * NORM_ROWS, NORM_ROWS), NORM_ROWS)
            v1_scr[a, sl, :] = jnp.where(mine, v_ref[sl, :], 1.0).astype(BF16)
            return carry

        lax.fori_loop(0, SEQ // NORM_ROWS, body, 0)
        cv1_scr[a] = jnp.where(mine, cv_ref[...], 1.0).astype(BF16)


def _pair_major_spec():
    return pl.BlockSpec((1, ATTN_ROWS, 128), lambda p, i: (p, i, 0))


def _normed_key_specs():
    return [pl.BlockSpec((None, SEQ, 128), lambda p, i: (p, 0, 0)), pl.BlockSpec((None, CTX, 128), lambda p, i: (p, 0, 0))]


def attn_fwd(z, zc, bias, qg2, kg2):
    def kern(q_ref, k_ref, v_ref, bg_ref, ck_ref, cv_ref, be_ref, bo_ref, qg_ref, kg_ref,
             ob_ref, o_ref, rden_ref, pl_ref, pc_ref, kn_ref, ckn_ref, kn_scr, ckn_scr, v1_scr, cv1_scr, s_scr):
        i = pl.program_id(1)

        @pl.when(i == 0)
        def _():
            _norm_keys(k_ref, ck_ref, kg_ref, kn_scr, ckn_scr)
            kn_ref[...] = kn_scr[...]
            ckn_ref[...] = ckn_scr[...]
            _values_with_ones(v_ref, cv_ref, v1_scr, cv1_scr)

        heads = _head_lanes()
        bias_refs = (be_ref, bo_ref)
        tiles = [(b, a) for b in range(2) for a in range(2)]
        rows = [slice(b * QBLK, (b + 1) * QBLK) for b in range(2)]
        qn = [_scaled_q(q_ref[rows[b], :], qg_ref[...]) for b in range(2)]
        qa = [jnp.where(heads[a], qn[b], 0.0).astype(BF16) for b, a in tiles]
        pv = [None] * len(tiles)
        done = {}
        latent = KBLK // KCOLS

        def keys(b, n):
            return pl.ds(pl.multiple_of(_kstart(2 * i + b) + n * KCOLS, KCOLS), KCOLS)

        def score_piece(t, n):
            b, a = tiles[t]
            cols = slice(n * KCOLS, (n + 1) * KCOLS)
            if n < latent:
                s_scr[t, :, cols] = mm_nt(qa[t], kn_scr[keys(b, n), :]) + bias_refs[b][0, a, :, cols]
            else:
                s_scr[t, :, cols] = mm_nt(qa[t], ckn_scr[...])

        def softmax_rows(t, r):
            b, a = tiles[t]
            rs = slice(r * SOFTMAX_ROWS, (r + 1) * SOFTMAX_ROWS)
            out_rows = slice(b * QBLK + rs.start, b * QBLK + rs.stop)
            s = s_scr[t, rs, :]
            p = jnp.exp(s - jnp.max(s, axis=-1, keepdims=True)).astype(BF16)
            pl_ref[a, out_rows, :] = p[:, :KBLK]
            pc_ref[a, out_rows, :] = p[:, KBLK:]

        def value_piece(t, n):
            b, a = tiles[t]
            if n < latent:
                part = mm(pl_ref[a, rows[b], n * KCOLS:(n + 1) * KCOLS], v1_scr[a, keys(b, n), :])
            else:
                part = mm(pc_ref[a, rows[b], :], cv1_scr[a])
            pv[t] = part if pv[t] is None else pv[t] + part
            if n == latent:
                finish(t)

        def finish(t):
            b, a = tiles[t]
            r = jnp.where(heads[a], pltpu.roll(1.0 / pv[t], HDIM, 1), 0.0)
            done[t] = (pv[t] * r, r)
            if a == 1:
                o, rden = (lo + hi for lo, hi in zip(done[t - 1], done[t]))
                ob_ref[rows[b], :] = o * jax.nn.silu(bg_ref[rows[b], :])
                o_ref[0, rows[b], :] = o
                rden_ref[0, rows[b], :] = rden

        pieces = range(latent + 1)
        for n in pieces:
            score_piece(0, n)
        for t in range(len(tiles)):
            matmuls = []
            for n in pieces:
                if t + 1 < len(tiles):
                    matmuls.append(functools.partial(score_piece, t + 1, n))
                if t > 0:
                    matmuls.append(functools.partial(value_piece, t - 1, n))
            _emit_interleaved([functools.partial(softmax_rows, t, r) for r in range(QBLK // SOFTMAX_ROWS)], matmuls)
        for n in pieces:
            value_piece(len(tiles) - 1, n)

    qblk = pl.BlockSpec((ATTN_ROWS, 128), lambda p, i: (i, p))
    return pl.pallas_call(
        kern, name="attn_fwd", grid=(NPAIR, ATTN_STEPS),
        in_specs=_attn_in_specs() + _bias_specs() + [_row(128), _row(128)],
        out_specs=[qblk, _pair_major_spec(), _pair_major_spec()] + _prob_specs() + _normed_key_specs(),
        out_shape=[jax.ShapeDtypeStruct((SEQ, 512), F32)] + [jax.ShapeDtypeStruct((NPAIR, SEQ, 128), F32)] * 2
        + [jax.ShapeDtypeStruct((HEADS, SEQ, KBLK), BF16), jax.ShapeDtypeStruct((HEADS, SEQ, CTX), BF16),
           jax.ShapeDtypeStruct((NPAIR, SEQ, 128), BF16), jax.ShapeDtypeStruct((NPAIR, CTX, 128), BF16)],
        scratch_shapes=[pltpu.VMEM((SEQ, 128), BF16), pltpu.VMEM((CTX, 128), BF16),
                        pltpu.VMEM((2, SEQ, 128), BF16), pltpu.VMEM((2, CTX, 128), BF16),
                        pltpu.VMEM((4, QBLK, KBLK + CTX), F32)],
        compiler_params=_cparams(("arbitrary", "arbitrary"), 40 * 1024 * 1024),
    )(z, z, z, z, zc, zc, bias, bias, qg2, kg2)


def attn_bwd(z, zc, qg2, kg2, dcat, saved):
    def kern(q_ref, k_ref, v_ref, bg_ref, ck_ref, cv_ref, qg_ref, kg_ref, do_ref, o_ref, rden_ref, pl_ref, pc_ref,
             kn_scr, ckn_scr, dq_ref, dk_ref, dv_ref, dbg_ref, dck_ref, dcv_ref, db_ref, dqg_ref, dkg_ref,
             v_scr, cv_scr, dknt_scr, dvt_scr, dcknt_scr, dcvt_scr, dp_scr, ds_scr):
        p, i = pl.program_id(0), pl.program_id(1)
        last = i == ATTN_STEPS - 1

        @pl.when(i == 0)
        def _():
            def body(c, carry):
                sl = pl.ds(pl.multiple_of(c * NORM_ROWS, NORM_ROWS), NORM_ROWS)
                v_scr[sl, :] = v_ref[sl, :].astype(BF16)
                return carry

            lax.fori_loop(0, SEQ // NORM_ROWS, body, 0)
            cv_scr[...] = cv_ref[...].astype(BF16)
            for acc in (dknt_scr, dvt_scr, dcknt_scr, dcvt_scr, db_ref):
                acc[...] = jnp.zeros_like(acc)

        @pl.when((i == 0) & (p == 0))
        def _():
            dqg_ref[...] = jnp.zeros_like(dqg_ref)
            dkg_ref[...] = jnp.zeros_like(dkg_ref)

        heads = _head_lanes()
        tiles = [(b, a) for b in range(2) for a in range(2)]
        rows = [slice(b * QBLK, (b + 1) * QBLK) for b in range(2)]
        kb = [_kblock(2 * i + b) for b in range(2)]
        variant = [jnp.where(i == 0, 0, 1), jnp.where(last, 2, 1)]
        latent = KBLK // KCOLS

        def keys(b, n):
            return pl.ds(pl.multiple_of((kb[b] + n) * KCOLS, KCOLS), KCOLS)

        gated = []
        for b in range(2):
            bg, dout, o = bg_ref[rows[b], :], do_ref[rows[b], :], o_ref[0, rows[b], :]
            sig = jax.nn.sigmoid(bg)
            do = dout * (bg * sig)
            dbg_ref[rows[b], :] = (dout * o * (sig * (1.0 + bg * (1.0 - sig)))).astype(BF16)
            rden = rden_ref[0, rows[b], :]
            dr = do * rden
            qn = _scaled_q(q_ref[rows[b], :], qg_ref[...])
            gated.append((dr, dr.T.astype(BF16), qn.T.astype(BF16), do * o * rden))

        feats = [slice(a * HDIM, (a + 1) * HDIM) for a in range(2)]
        doa, doa_t, qa_t, delta = [], [], [], []
        for b, a in tiles:
            dr, dr_t, qn_t, weighted = gated[b]
            doa.append(jnp.where(heads[a], dr, 0.0).astype(BF16))
            doa_t.append(dr_t[feats[a], :])
            qa_t.append(qn_t[feats[a], :])
            delta.append(jnp.sum(jnp.where(heads[a], weighted, 0.0), axis=-1, keepdims=True))
        dqn = [None] * len(tiles)

        def cols(n):
            return slice(n * KCOLS, (n + 1) * KCOLS)

        def stage_a(t, n):
            b, a = tiles[t]
            if n < latent:
                dp_scr[t, :, cols(n)] = mm_nt(doa[t], v_scr[keys(b, n), :])
                dvt_scr[kb[b] + n, feats[a], :] += mm(doa_t[t], pl_ref[a, rows[b], cols(n)])
            else:
                dp_scr[t, :, cols(n)] = mm_nt(doa[t], cv_scr[...])
                dcvt_scr[feats[a], :] += mm(doa_t[t], pc_ref[a, rows[b], :])

        def stage_b(t, r):
            b, a = tiles[t]
            rs = slice(r * SOFTMAX_ROWS, (r + 1) * SOFTMAX_ROWS)
            in_rows = slice(b * QBLK + rs.start, b * QBLK + rs.stop)
            d = dp_scr[t, rs, :] - delta[t][rs, :]
            ds_lat = pl_ref[a, in_rows, :].astype(F32) * d[:, :KBLK]
            ds_ctx = pc_ref[a, in_rows, :].astype(F32) * d[:, KBLK:]
            db_ref[variant[b], a, rs, :] += ds_lat
            ds_scr[t, rs, :KBLK] = ds_lat.astype(BF16)
            ds_scr[t, rs, KBLK:] = ds_ctx.astype(BF16)

        def stage_c(t, n):
            b, a = tiles[t]
            ds = ds_scr[t, :, cols(n)]
            if n < latent:
                part = mm(ds, kn_scr[keys(b, n), :])
                dknt_scr[kb[b] + n, feats[a], :] += mm(qa_t[t], ds)
            else:
                part = mm(ds, ckn_scr[...])
                dcknt_scr[feats[a], :] += mm(qa_t[t], ds)
            dqn[t] = part if dqn[t] is None else dqn[t] + part
            if n == latent and a == 1:
                both = jnp.where(heads[0], dqn[t - 1], 0.0) + jnp.where(heads[1], dqn[t], 0.0)
                dq, dqg = jax.vjp(_scaled_q, q_ref[rows[b], :], qg_ref[...])[1](both)
                dq_ref[rows[b], :] = dq.astype(BF16)
                dqg_ref[...] += dqg

        pieces = range(latent + 1)
        for n in pieces:
            stage_a(0, n)
        for t in range(len(tiles)):
            matmuls = []
            for n in pieces:
                if t + 1 < len(tiles):
                    matmuls.append(functools.partial(stage_a, t + 1, n))
                if t > 0:
                    matmuls.append(functools.partial(stage_c, t - 1, n))
            _emit_interleaved([functools.partial(stage_b, t, r) for r in range(QBLK // SOFTMAX_ROWS)], matmuls)
        for n in pieces:
            stage_c(len(tiles) - 1, n)

        @pl.when(last)
        def _():
            eye = (lax.broadcasted_iota(jnp.int32, (KCOLS, KCOLS), 0)
                   == lax.broadcasted_iota(jnp.int32, (KCOLS, KCOLS), 1)).astype(BF16)

            def turned(x):
                hi = x.astype(BF16)
                return mm_nt(eye, hi) + mm_nt(eye, x - hi.astype(F32))

            def body(c, dkg):
                sl = pl.ds(pl.multiple_of(c * NORM_ROWS, NORM_ROWS), NORM_ROWS)
                blocks = range(NORM_ROWS // KCOLS)
                dkn = jnp.concatenate([turned(dknt_scr[c * len(blocks) + n]) for n in blocks], axis=0)
                dv = jnp.concatenate([mm_nt(eye, dvt_scr[c * len(blocks) + n]) for n in blocks], axis=0)
                dk, dg = _pair_rms_bwd(k_ref[sl, :], kg_ref[...], dkn)
                dk_ref[sl, :] = dk.astype(BF16)
                dv_ref[sl, :] = dv.astype(BF16)
                return dkg + dg

            dkg = lax.fori_loop(0, SEQ // NORM_ROWS, body, jnp.zeros((1, 128), F32))
            dck, dg = _pair_rms_bwd(ck_ref[...], kg_ref[...], dcknt_scr[...].T)
            dck_ref[...] = dck
            dcv_ref[...] = dcvt_scr[...].T
            dkg_ref[...] += dkg + dg

        @pl.when(last & (p == NPAIR - 1))
        def _():
            dqg_ref[...] = dqg_ref[...] + pltpu.roll(dqg_ref[...], HDIM, 1)
            dkg_ref[...] = dkg_ref[...] + pltpu.roll(dkg_ref[...], HDIM, 1)

    blk = lambda rows: pl.BlockSpec((rows, 128), lambda p, i: (0, p))
    qblk = pl.BlockSpec((ATTN_ROWS, 128), lambda p, i: (i, p))
    return pl.pallas_call(
        kern, name="attn_bwd", grid=(NPAIR, ATTN_STEPS),
        in_specs=_attn_in_specs() + [_row(128), _row(128), pl.BlockSpec((ATTN_ROWS, 128), lambda p, i: (i, 4 + p)),
                                     _pair_major_spec(), _pair_major_spec()] + _prob_specs() + _normed_key_specs(),
        out_specs=[qblk, blk(SEQ), blk(SEQ), qblk, blk(CTX), blk(CTX),
                   pl.BlockSpec((3, 2, QBLK, KBLK), lambda p, i: (0, p, 0, 0)), _row(128), _row(128)],
        out_shape=[jax.ShapeDtypeStruct((SEQ, 512), BF16)] * 4 + [jax.ShapeDtypeStruct((CTX, 512), F32)] * 2
        + [jax.ShapeDtypeStruct((3, HEADS, QBLK, KBLK), F32)]
        + [jax.ShapeDtypeStruct((1, 128), F32), jax.ShapeDtypeStruct((1, 128), F32)],
        scratch_shapes=[pltpu.VMEM((SEQ, 128), BF16), pltpu.VMEM((CTX, 128), BF16),
                        pltpu.VMEM((SEQ // KCOLS, 128, KCOLS), F32), pltpu.VMEM((SEQ // KCOLS, 128, KCOLS), F32),
                        pltpu.VMEM((128, CTX), F32), pltpu.VMEM((128, CTX), F32),
                        pltpu.VMEM((4, QBLK, KBLK + CTX), F32), pltpu.VMEM((4, QBLK, KBLK + CTX), BF16)],
        compiler_params=_cparams(("arbitrary", "arbitrary"), VMEM_BIG),
    )(z, z, z, z, zc, zc, qg2, kg2, dcat, *saved)


def outproj(out_a, out_b, x, target, gate, wo):
    tl = 512

    def kern(a_ref, b_ref, x_ref, t_ref, g_ref, w_ref, loss_ref, dy_ref, dcat_ref, dg_ref, dw_ref):
        @pl.when(pl.program_id(0) == 0)
        def _():
            loss_ref[...] = jnp.zeros_like(loss_ref)
            dg_ref[...] = jnp.zeros_like(dg_ref)
            dw_ref[...] = jnp.zeros_like(dw_ref)

        a, b = a_ref[...].astype(BF16), b_ref[...].astype(BF16)
        mix = (jnp.dot(a, w_ref[0:512, :], preferred_element_type=F32)
               + jnp.dot(b, w_ref[512:1024, :], preferred_element_type=F32))
        err = x_ref[...] + g_ref[...] * mix - t_ref[...]
        loss_ref[...] += 0.5 * jnp.sum(jnp.mean(err * err, axis=-1))
        dy = err * (1.0 / DM)
        dy_ref[...] = dy
        dg_ref[...] += jnp.sum(dy * mix, axis=0, keepdims=True)
        dmix = (g_ref[...] * dy).astype(BF16)
        dcat_ref[...] = lax.dot_general(dmix, w_ref[...], (((1,), (1,)), ((), ())), preferred_element_type=F32)
        dw_ref[0:512, :] += lax.dot_general(a, dmix, (((0,), (0,)), ((), ())), preferred_element_type=F32)
        dw_ref[512:1024, :] += lax.dot_general(b, dmix, (((0,), (0,)), ((), ())), preferred_element_type=F32)

    tile = lambda w: pl.BlockSpec((tl, w), lambda t: (t, 0))
    whole = pl.BlockSpec((DM, DM), lambda t: (0, 0))
    return pl.pallas_call(
        kern, name="outproj", grid=(SEQ // tl,),
        in_specs=[tile(512), tile(512), tile(DM), tile(DM), _row(DM), whole],
        out_specs=[pl.BlockSpec((8, 128), lambda t: (0, 0)), tile(DM), tile(DM), _row(DM), whole],
        out_shape=[jax.ShapeDtypeStruct((8, 128), F32), jax.ShapeDtypeStruct((SEQ, DM), F32),
                   jax.ShapeDtypeStruct((SEQ, DM), F32), jax.ShapeDtypeStruct((1, DM), F32),
                   jax.ShapeDtypeStruct((DM, DM), F32)],
        compiler_params=_cparams(("arbitrary",), 48 * 1024 * 1024),
    )(out_a, out_b, x, target, gate, wo)


def _pieces(sources):
    out = []
    for name, c0, c1 in sources:
        for j in range(NCHIP):
            lo, hi = max(c0, j * SHARD_IN), min(c1, (j + 1) * SHARD_IN)
            if lo < hi:
                out.append((j, lo - j * SHARD_IN, hi - j * SHARD_IN, name, lo - c0, hi - c0))
    return out


DZ_PIECES = _pieces((("a", 0, 1536), ("q", 1536, 2048), ("k", 2048, 2560), ("v", 2560, 3072), ("g", 3072, DIN)))
DZC_PIECES = _pieces((("k", 2048, 2560), ("v", 2560, 3072)))
_NT = (((1,), (1,)), ((), ()))


DH_SUBTILES = 2


def _dz_specs(tl):
    return [pl.BlockSpec((tl, 1536), lambda t: (t, 0))] + [pl.BlockSpec((tl, 512), lambda t: (t, 0))] * 4


def dh_bwd(dz_parts, w_full, x, dy, shift, scale, norm_g, dg_ctx):
    tl = 512
    nt = SEQ // tl

    def kern(a_ref, q_ref, k_ref, v_ref, g_ref, w_ref, x_ref, dy_ref, sh_ref, sc_ref, gn_ref, dgc_ref,
             gx_ref, dsh_ref, dsc_ref, dg_ref):
        @pl.when(pl.program_id(0) == 0)
        def _():
            dsh_ref[...] = jnp.zeros_like(dsh_ref)
            dsc_ref[...] = jnp.zeros_like(dsc_ref)
            dg_ref[...] = dgc_ref[...]

        src = dict(a=a_ref, q=q_ref, k=k_ref, v=v_ref, g=g_ref)
        for sub in range(DH_SUBTILES):
            rows = slice(sub * tl // DH_SUBTILES, (sub + 1) * tl // DH_SUBTILES)
            dh = None
            for j, l0, l1, name, s0, s1 in DZ_PIECES:
                part = lax.dot_general(src[name][rows, s0:s1], w_ref[j, :, l0:l1], _NT, preferred_element_type=F32)
                dh = part if dh is None else dh + part
            _, vjp = jax.vjp(_modulated, x_ref[rows, :], gn_ref[...], sc_ref[...], sh_ref[...])
            dx, dg, dsc, dsh = vjp(dh)
            gx_ref[rows, :] = dy_ref[rows, :] + dx
            dg_ref[...] += dg
            dsc_ref[...] += dsc
            dsh_ref[...] += dsh

    tile = pl.BlockSpec((tl, DM), lambda t: (t, 0))
    return pl.pallas_call(
        kern, name="dh_bwd", grid=(nt,),
        in_specs=_dz_specs(tl) + [pl.BlockSpec((NCHIP, DM, SHARD_IN), lambda t: (0, 0, 0)), tile, tile, _row(DM),
                                  _row(DM), _row(DM), _row(DM)],
        out_specs=[tile, _row(DM), _row(DM), _row(DM)],
        out_shape=[jax.ShapeDtypeStruct((SEQ, DM), F32)] + [jax.ShapeDtypeStruct((1, DM), F32)] * 3,
        compiler_params=_cparams(("arbitrary",), 48 * 1024 * 1024),
    )(*dz_parts, w_full, x, dy, shift, scale, norm_g, dg_ctx)


def dw_bwd(h, dz_parts, hc, dck, dcv, g_out):
    tl = 512
    nt = SEQ // tl
    (rhi, wi), (rho, wo) = RS_SHAPES

    def kern(h_ref, a_ref, q_ref, k_ref, v_ref, g_ref, hc_ref, dck_ref, dcv_ref, go_hbm,
             wire_i, keep_i, wire_o, keep_o, acc, snd_i, rcv_i, mine_o, rcv_o, load_sem, send_sems, recv_sems):
        t = pl.program_id(0)
        x, y, c = _me()
        k = 2 * x + y
        sib = _flip(1)
        half = lambda hh, rh: pl.ds(pl.multiple_of(hh * rh, rh), rh)
        load_o = pltpu.make_async_copy(go_hbm.at[:, half(c, rho), :], mine_o, load_sem)
        pair_o = _rcopy(go_hbm.at[:, half(1 - c, rho), :], rcv_o, send_sems, recv_sems, 0, sib)
        pair_i = [_rcopy(snd_i.at[j], rcv_i.at[j], send_sems, recv_sems, 1 + j, sib) for j in range(NCHIP)]

        @pl.when(t == 0)
        def _():
            load_o.start()
            pair_o.start()
            acc[...] = jnp.zeros_like(acc)
            hct = hc_ref[...].T
            csrc = dict(k=dck_ref, v=dcv_ref)
            for j, l0, l1, name, s0, s1 in DZC_PIECES:
                acc[j, :, l0:l1] += jnp.dot(hct, csrc[name][:, s0:s1].astype(BF16), preferred_element_type=F32)

        ht = h_ref[...].T
        src = dict(a=a_ref, q=q_ref, k=k_ref, v=v_ref, g=g_ref)
        for j, l0, l1, name, s0, s1 in DZ_PIECES:
            acc[j, :, l0:l1] += jnp.dot(ht, src[name][:, s0:s1], preferred_element_type=F32)

        @pl.when(t == nt - 1)
        def _():
            for j in range(NCHIP):
                snd_i[j] = acc[j, half(1 - c, rhi), :].astype(BF16)
                pair_i[j].start()
            load_o.wait()
            pair_o.wait_recv()
            for j in range(NCHIP):
                wire_o[j] = (mine_o[j] + rcv_o[j]).astype(BF16)
            keep_o[...] = mine_o[k] + rcv_o[k]
            mine = half(c, rhi)
            for j in range(NCHIP):
                pair_i[j].wait_recv()
                wire_i[j] = (acc[j, mine, :] + rcv_i[j].astype(F32)).astype(BF16)
            keep_i[...] = acc[k, mine, :] + rcv_i[k].astype(F32)
            pair_o.wait_send()
            for j in range(NCHIP):
                pair_i[j].wait_send()

    whole = lambda *shape: pl.BlockSpec(shape, lambda t: (0,) * len(shape))
    return pl.pallas_call(
        kern, name="dw_bwd", grid=(nt,),
        in_specs=[pl.BlockSpec((tl, DM), lambda t: (t, 0))] + _dz_specs(tl)
        + [whole(CTX, DM), whole(CTX, 512), whole(CTX, 512), pl.BlockSpec(memory_space=pl.ANY)],
        out_specs=[whole(NCHIP, rhi, wi), whole(rhi, wi), whole(NCHIP, rho, wo), whole(rho, wo)],
        out_shape=[jax.ShapeDtypeStruct((NCHIP, rhi, wi), BF16), jax.ShapeDtypeStruct((rhi, wi), F32),
                   jax.ShapeDtypeStruct((NCHIP, rho, wo), BF16), jax.ShapeDtypeStruct((rho, wo), F32)],
        scratch_shapes=[pltpu.VMEM((NCHIP, DM, SHARD_IN), F32), pltpu.VMEM((NCHIP, rhi, wi), BF16),
                        pltpu.VMEM((NCHIP, rhi, wi), BF16),
                        pltpu.VMEM((NCHIP, rho, wo), F32), pltpu.VMEM((NCHIP, rho, wo), F32),
                        pltpu.SemaphoreType.DMA(()), pltpu.SemaphoreType.DMA((1 + NCHIP,)),
                        pltpu.SemaphoreType.DMA((1 + NCHIP,))],
        compiler_params=_cparams(("arbitrary",), VMEM_BIG),
    )(h, *dz_parts, hc, dck, dcv, g_out)


def ctx_bwd(dck, dcv, w_full, ctx, cshift, cscale, norm_g):
    def kern(dck_ref, dcv_ref, w_ref, c_ref, sh_ref, sc_ref, g_ref, dsh_ref, dsc_ref, dg_ref):
        csrc = dict(k=dck_ref, v=dcv_ref)
        dhc = None
        for j, l0, l1, name, s0, s1 in DZC_PIECES:
            part = lax.dot_general(csrc[name][:, s0:s1].astype(BF16), w_ref[j, :, l0:l1], _NT,
                                   preferred_element_type=F32)
            dhc = part if dhc is None else dhc + part
        _, vjp = jax.vjp(lambda g, sc, sh: _modulated(c_ref[...], g, sc, sh), g_ref[...], sc_ref[...], sh_ref[...])
        dg_ref[...], dsc_ref[...], dsh_ref[...] = vjp(dhc)

    whole = lambda r, c: pl.BlockSpec((r, c), lambda i: (0, 0))
    return pl.pallas_call(
        kern, name="ctx_bwd", grid=(1,),
        in_specs=[whole(CTX, 512), whole(CTX, 512), pl.BlockSpec((NCHIP, DM, SHARD_IN), lambda i: (0, 0, 0)),
                  whole(CTX, DM), _row(DM), _row(DM), _row(DM)],
        out_specs=[_row(DM), _row(DM), _row(DM)],
        out_shape=[jax.ShapeDtypeStruct((1, DM), F32)] * 3,
        compiler_params=_cparams(("arbitrary",), 40 * 1024 * 1024),
    )(dck, dcv, w_full, ctx, cshift, cscale, norm_g)


def _lane_pad_rpb(rpb):
    r = jnp.pad(rpb, ((0, 0), (0, 0), (0, GRID_W - rpb.shape[-1])))
    return jnp.concatenate([r, r], axis=-1)


def local_step(chip, dev, x, c_vec, c_ctx, w_ada, b_shard, ctx, target, norm_g, sgu_g, w_s, b_s, q_g, k_g, rpb,
               w_in_shard, w_out_shard):
    bsb = jnp.broadcast_to(b_s[:, :, None], (4, 128, 128))
    qg2, kg2 = jnp.tile(q_g, (1, 2)), jnp.tile(k_g, (1, 2))

    z, h, w_in_full, w_out_full, mod_all, cs = inproj_fwd(chip, x, c_vec, c_ctx, w_ada, b_shard, norm_g, w_in_shard,
                                                          w_out_shard)
    mods = mod_all.transpose(1, 0, 2).reshape(CS_ROWS, 3 * DM)
    mod = lax.dynamic_slice(mods, (8 * dev, 0), (1, 3 * DM))
    shift, scale, gate = mod[:, :DM], mod[:, DM:2 * DM], mod[:, 2 * DM:]
    cshift, cscale = mods[8 * NDEV:8 * NDEV + 1, :DM], mods[8 * NDEV:8 * NDEV + 1, DM:2 * DM]
    zc, hc = ctx_fwd(ctx, cshift, cscale, norm_g, w_in_full)
    bias = rpb_tables(_lane_pad_rpb(rpb))
    out_a = sgu_fwd(z, sgu_g, w_s, bsb)
    out_b, *saved = attn_fwd(z, zc, bias, qg2, kg2)
    loss8, dy, dcat, dgate, dwo = outproj(out_a, out_b, x, target, gate, w_out_full.reshape(DM, DM))
    dz_a, dsg, dws, dbsb = sgu_bwd(z, sgu_g, w_s, bsb, dcat)
    dq, dk, dv, dbg, dck, dcv, dbias, dqg2, dkg2 = attn_bwd(z, zc, qg2, kg2, dcat, saved)
    drpb = rpb_bwd(dbias)[:, :, :rpb.shape[-1]]
    dz_parts = (dz_a, dq, dk, dv, dbg)
    dcshift, dcscale, dng_c = ctx_bwd(dck, dcv, w_in_full, ctx, cshift, cscale, norm_g)
    wire_i, keep_i, wire_o, keep_o = dw_bwd(h, dz_parts, hc, dck, dcv, dwo.reshape(NCHIP, SHARD_OUT, DM))
    *in_flight, token = rs_start(wire_i, wire_o)
    grad_x, dshift, dscale, dng = dh_bwd(dz_parts, w_in_full, x, dy, shift, scale, norm_g, dng_c + token[0, 0])
    got_i, got_o = rs_wait(*in_flight, dshift)
    return dict(
        loss=loss8[0:1, 0:1], grad_x=grad_x, rs=(keep_i, got_i, keep_o, got_o), cs=cs,
        dmod=jnp.concatenate([dshift, dscale, dgate], axis=-1),
        dcmod=jnp.concatenate([dcshift, dcscale, jnp.zeros((1, DM), F32)], axis=-1),
        d_norm_g=dng, d_sgu_g=dsg, d_w_s=dws, d_b_s=dbsb[:, :, 0],
        d_q_g=dqg2[:, :HDIM], d_k_g=dkg2[:, :HDIM], d_rpb=drpb)


def _me():
    return lax.axis_index("x"), lax.axis_index("y"), lax.axis_index("c")


def _flip(q):
    x, y, c = _me()
    return ((1 - x) if q & 4 else x, (1 - y) if q & 2 else y, (1 - c) if q & 1 else c)


def _chip_of(dev):
    return 2 * dev[0] + dev[1]


def _rcopy(src, dst, send_sems, recv_sems, k, dev):
    return pltpu.make_async_remote_copy(src_ref=src, dst_ref=dst, send_sem=send_sems.at[k], recv_sem=recv_sems.at[k],
                                        device_id=dev, device_id_type=MESH_ID)


_VMEM_SPEC = pl.BlockSpec(memory_space=pltpu.VMEM)
SLAB_ROWS = 80


RS_SHAPES = ((DM // 2, SHARD_IN), (SHARD_OUT // 2, DM))
_HBM_SPEC = pl.BlockSpec(memory_space=pltpu.HBM)
_SEM_SPEC = pl.BlockSpec(memory_space=pltpu.SEMAPHORE)
_IN_FLIGHT = pltpu.SideEffectType.DATAFLOW_SIDE_EFFECTING


def _rs_copies(wires, lands, send_sems, recv_sems):
    return [pltpu.make_async_remote_copy(
        src_ref=wires[n].at[_chip_of(_flip(q))], dst_ref=lands[n].at[q // 2 - 1],
        send_sem=send_sems.at[3 * n + q // 2 - 1], recv_sem=recv_sems.at[3 * n + q // 2 - 1],
        device_id=_flip(q), device_id_type=MESH_ID) for n in (0, 1) for q in (2, 4, 6)]


def rs_start(wire_i, wire_o):
    lands = [lax.empty((NCHIP - 1, rh, w), BF16) for rh, w in RS_SHAPES]

    def body(wi_ref, wo_ref, li_ref, lo_ref, send_sems, recv_sems, wi_thru, wo_thru, li_thru, lo_thru, token):
        for cp in _rs_copies((wi_ref, wo_ref), (li_ref, lo_ref), send_sems, recv_sems):
            cp.start()
        token[...] = jnp.zeros_like(token)

    hbm = lambda a: pltpu.HBM(a.shape, a.dtype)
    return pl.pallas_call(
        body, name="rs_start",
        out_shape=(pltpu.SemaphoreType.DMA((6,)), pltpu.SemaphoreType.DMA((6,)), hbm(wire_i), hbm(wire_o),
                   hbm(lands[0]), hbm(lands[1]), jax.ShapeDtypeStruct((8, 128), F32)),
        in_specs=(_HBM_SPEC,) * 4, out_specs=(_SEM_SPEC, _SEM_SPEC) + (_HBM_SPEC,) * 4 + (_VMEM_SPEC,),
        input_output_aliases={0: 2, 1: 3, 2: 4, 3: 5},
        compiler_params=pltpu.CompilerParams(has_side_effects=_IN_FLIGHT),
    )(*[pltpu.with_memory_space_constraint(a, pltpu.HBM) for a in (wire_i, wire_o, *lands)])


def rs_wait(send_sems, recv_sems, wire_i, wire_o, land_i, land_o, after):
    def body(wi_ref, wo_ref, li_ref, lo_ref, send_sems, recv_sems, after_ref, wi_dead, wo_dead, gi_ref, go_ref):
        for cp in _rs_copies((wi_ref, wo_ref), (li_ref, lo_ref), send_sems, recv_sems):
            cp.wait_send()
            cp.wait_recv()

    hbm = lambda a: pltpu.HBM(a.shape, a.dtype)
    return pl.pallas_call(
        body, name="rs_wait", out_shape=(hbm(wire_i), hbm(wire_o), hbm(land_i), hbm(land_o)),
        in_specs=(_HBM_SPEC,) * 4 + (_SEM_SPEC, _SEM_SPEC, pl.BlockSpec(memory_space=pl.ANY)),
        out_specs=(_HBM_SPEC,) * 4, input_output_aliases={0: 0, 1: 1, 2: 2, 3: 3},
        compiler_params=pltpu.CompilerParams(has_side_effects=_IN_FLIGHT),
    )(wire_i, wire_o, land_i, land_o, send_sems, recv_sems, after)[2:]


def final_reduce(keep_i, got_i, keep_o, got_o, slab, cs, w_ada, c_ctx):
    (rhi, wi), (rho, wo) = RS_SHAPES

    def kern(ki_hbm, gi_hbm, ko_hbm, go_hbm, s_ref, cs_ref, w_hbm, cc_ref,
             gin_ref, gout_ref, tot_ref, dw_ref, db_ref, dcc_ref,
             ki, gi, ko, go, w_scr, all_ref, dms_scr, parts, load_sems, send_sems, recv_sems):
        x, y, c = _me()
        k = 2 * x + y
        sib = _flip(1)
        dev = lambda d: 4 * d[0] + 2 * d[1] + d[2]
        me = dev((x, y, c))

        def slab_copy(idx, owner, to):
            return _rcopy(all_ref.at[dev(owner)], all_ref.at[dev(owner)], send_sems, recv_sems, idx, to)

        all_ref[me] = s_ref[...]
        first = [slab_copy(0, (x, y, c), sib)] + [slab_copy(q // 2, (x, y, c), _flip(q)) for q in (2, 4, 6)]
        for cp in first:
            cp.start()
        loads = [pltpu.make_async_copy(src, dst, load_sems.at[n]) for n, (src, dst) in enumerate(
            ((ki_hbm, ki), (gi_hbm, gi), (ko_hbm, ko), (go_hbm, go), (w_hbm, w_scr)))]
        for cp in loads:
            cp.start()

        shares = []
        for n, (keep, got, out) in enumerate(((ki, gi, gin_ref), (ko, go, gout_ref))):
            rh = RS_SHAPES[n][0]
            half = lambda hh, rh=rh: pl.ds(pl.multiple_of(hh * rh, rh), rh)
            loads[2 * n].wait()
            loads[2 * n + 1].wait()
            out[half(c), :] = ((keep[...] + got[0].astype(F32)) + got[1].astype(F32)) + got[2].astype(F32)
            share = _rcopy(out.at[half(c), :], out.at[half(c), :], send_sems, recv_sems, 7 + n, sib)
            share.start()
            shares.append((share, _rcopy(out.at[half(1 - c), :], out.at[half(1 - c), :], send_sems, recv_sems, 7 + n,
                                         sib)))

        passed = []
        for q in (2, 4, 6):
            slab_copy(q // 2, _flip(q), (x, y, c)).wait_recv()
            cp = slab_copy(3 + q // 2, _flip(q), sib)
            cp.start()
            passed.append(cp)
        slab_copy(0, sib, (x, y, c)).wait_recv()
        for q in (2, 4, 6):
            slab_copy(3 + q // 2, _flip(q | 1), (x, y, c)).wait_recv()
        tot = all_ref[0]
        for d in range(1, NDEV):
            tot = tot + all_ref[d]
        tot_ref[...] = tot

        pad = jnp.zeros((7, DM), F32)
        dm = [jnp.concatenate([all_ref[d, 12 + j:13 + j, :] for d in range(NDEV)] + [tot[9 + j:10 + j, :], pad], axis=0)
              for j in range(3)]
        db_ref[...] = jnp.concatenate([jnp.sum(part, axis=0, keepdims=True) for part in dm], axis=0)
        dm = jnp.concatenate(dm, axis=-1)
        for j in range(NCHIP):
            @pl.when(k == j)
            def _():
                dms_scr[...] = dm[:, j * SHARD_ADA:(j + 1) * SHARD_ADA].astype(BF16)

        a_in = jnp.concatenate([cs_ref[8 * d:8 * d + 1, :] for d in range(NDEV)]
                               + [cs_ref[8 * NDEV:8 * NDEV + 1, :], pad], axis=0)
        act = jax.nn.silu(a_in).astype(BF16)
        dms = dms_scr[...]
        dw_ref[...] = lax.dot_general(act, dms, (((0,), (0,)), ((), ())), preferred_element_type=F32)
        loads[4].wait()
        parts[k] = lax.dot_general(dms, w_scr[...].astype(BF16), (((1,), (1,)), ((), ())), preferred_element_type=F32)
        sends = [_rcopy(parts.at[k], parts.at[k], send_sems, recv_sems, 8 + q // 2, _flip(q)) for q in (2, 4, 6)]
        for cp in sends:
            cp.start()
        for q in (2, 4, 6):
            kq = _chip_of(_flip(q))
            _rcopy(parts.at[kq], parts.at[kq], send_sems, recv_sems, 8 + q // 2, _flip(q)).wait_recv()
        dact = ((parts[0] + parts[1]) + parts[2]) + parts[3]
        _, vjp = jax.vjp(jax.nn.silu, cc_ref[...])
        dcc_ref[...] = vjp(dact[8:9, :])[0]

        for share, arrival in shares:
            arrival.wait_recv()
            share.wait_send()
        for cp in first + passed + sends:
            cp.wait_send()

    any_spec = pl.BlockSpec(memory_space=pl.ANY)
    return pl.pallas_call(
        kern, name="final_reduce",
        in_specs=[any_spec] * 4 + [_VMEM_SPEC, _VMEM_SPEC, any_spec, _VMEM_SPEC], out_specs=[_VMEM_SPEC] * 6,
        out_shape=[jax.ShapeDtypeStruct((2 * rhi, wi), F32), jax.ShapeDtypeStruct((2 * rho, wo), F32),
                   jax.ShapeDtypeStruct((SLAB_ROWS, DM), F32), jax.ShapeDtypeStruct((DM, SHARD_ADA), F32),
                   jax.ShapeDtypeStruct((3, DM), F32), jax.ShapeDtypeStruct((1, DM), F32)],
        scratch_shapes=[pltpu.VMEM((rhi, wi), F32), pltpu.VMEM((NCHIP - 1, rhi, wi), BF16),
                        pltpu.VMEM((rho, wo), F32), pltpu.VMEM((NCHIP - 1, rho, wo), BF16),
                        pltpu.VMEM((DM, SHARD_ADA), F32), pltpu.VMEM((NDEV, SLAB_ROWS, DM), F32),
                        pltpu.VMEM((16, SHARD_ADA), BF16), pltpu.VMEM((NCHIP, 16, DM), F32),
                        pltpu.SemaphoreType.DMA((5,)), pltpu.SemaphoreType.DMA((12,)), pltpu.SemaphoreType.DMA((12,))],
        compiler_params=pltpu.CompilerParams(vmem_limit_bytes=40 * 1024 * 1024),
    )(keep_i, got_i, keep_o, got_o, slab, cs, w_ada, c_ctx)


def _adamw_math(w, g, m, v):
    m = B1 * m + (1.0 - B1) * g
    v = B2 * v + (1.0 - B2) * (g * g)
    m_hat = m / (1.0 - B1 ** STEP)
    v_hat = v / (1.0 - B2 ** STEP)
    return -LR * (m_hat / (jnp.sqrt(v_hat) + ADAM_EPS) + WD * w), m, v


def adamw_big(w, g, m, v, name, block_rows=256):
    rows, width = w.shape

    def kern(w_ref, g_ref, m_ref, v_ref, d_ref, nm_ref, nv_ref):
        d_ref[...], nm_ref[...], nv_ref[...] = _adamw_math(w_ref[...], g_ref[...], m_ref[...], v_ref[...])

    spec = pl.BlockSpec((block_rows, width), lambda i: (i, 0))
    return pl.pallas_call(
        kern, name=name, grid=(rows // block_rows,), in_specs=[spec] * 4, out_specs=[spec] * 3,
        out_shape=[jax.ShapeDtypeStruct((rows, width), F32)] * 3,
        compiler_params=_cparams(("arbitrary",)),
    )(w, g, m, v)


def adamw_small(quads):
    n = len(quads)

    def kern(*refs):
        ins, outs = refs[:4 * n], refs[4 * n:]
        for i in range(n):
            w, g, m, v = (r[...] for r in ins[4 * i:4 * i + 4])
            outs[3 * i][...], outs[3 * i + 1][...], outs[3 * i + 2][...] = _adamw_math(w, g, m, v)

    flat = [a for quad in quads for a in quad]
    res = pl.pallas_call(
        kern, name="adamw_small", in_specs=[_VMEM_SPEC] * (4 * n), out_specs=[_VMEM_SPEC] * (3 * n),
        out_shape=[jax.ShapeDtypeStruct(q[0].shape, F32) for q in quads for _ in range(3)],
    )(*flat)
    return [tuple(res[3 * i:3 * i + 3]) for i in range(n)]


def _rows_of(a, rows):
    flat = a.reshape(-1)
    return jnp.pad(flat, (0, rows * DM - flat.shape[0])).reshape(rows, DM)


def kernel(x, c, ctx, c_ctx, w_ada, b_ada, norm_g, w_in, sgu_norm_g, w_spatial, b_spatial, q_norm_g, k_norm_g, rpb, w_out, loss_target, m_c_ctx, m_w_ada, m_b_ada, m_norm_g, m_w_in, m_sgu_norm_g, m_w_spatial, m_b_spatial, m_q_norm_g, m_k_norm_g, m_rpb, m_w_out, v_c_ctx, v_w_ada, v_b_ada, v_norm_g, v_w_in, v_sgu_norm_g, v_w_spatial, v_b_spatial, v_q_norm_g, v_k_norm_g, v_rpb, v_w_out):
    xi, yi, ci = lax.axis_index("x"), lax.axis_index("y"), lax.axis_index("c")
    chip, dev = 2 * xi + yi, 4 * xi + 2 * yi + ci
    c_ctx2 = c_ctx.reshape(1, DM)

    b_shard = lax.dynamic_slice(b_ada, (0, chip * SHARD_ADA), (1, SHARD_ADA))
    part = local_step(chip.reshape(1).astype(jnp.int32), dev, x[0], c, c_ctx2, w_ada[0], b_shard, ctx[0], loss_target[0],
                      norm_g, sgu_norm_g, w_spatial[0], b_spatial[0], q_norm_g, k_norm_g, rpb[0], w_in[0], w_out[0])
    cs = part["cs"]

    slab = jnp.concatenate([
        part["d_norm_g"], _rows_of(part["d_sgu_g"], 1), _rows_of(part["d_b_s"], 1),
        _rows_of(jnp.concatenate([part["d_q_g"], part["d_k_g"]], axis=-1), 1), _rows_of(part["d_rpb"], 4),
        _rows_of(part["loss"], 1), _rows_of(part["dcmod"], 3), _rows_of(part["dmod"], 3), jnp.zeros((1, DM), F32),
        _rows_of(part["d_w_s"], 64)], axis=0)
    g_w_in, g_w_out, tot, g_w_ada, g_b_ada, g_c_ctx = final_reduce(*part["rs"], slab, cs, w_ada[0], c_ctx2)
    g_b_ada = g_b_ada.reshape(1, 3 * DM)

    loss = tot[8, 0]
    g_small = dict(
        c_ctx=g_c_ctx, b_ada=g_b_ada, norm_g=tot[0:1], sgu_norm_g=tot[1:2, :512], w_spatial=tot[16:80].reshape(512, 128),
        b_spatial=tot[2:3, :512].reshape(4, 128), q_norm_g=tot[3:4, :HDIM], k_norm_g=tot[3:4, HDIM:2 * HDIM],
        rpb=tot[4:8].reshape(-1)[:HEADS * 15 * 31].reshape(HEADS * 15, 31))
    shapes = dict(c_ctx=(DM,), w_ada=(1, DM, SHARD_ADA), b_ada=(1, 3 * DM), norm_g=(1, DM), w_in=(1, DM, SHARD_IN),
                  sgu_norm_g=(1, 512), w_spatial=(1, 4, 128, 128), b_spatial=(1, 4, 128), q_norm_g=(1, HDIM),
                  k_norm_g=(1, HDIM), rpb=(1, HEADS, 15, 31), w_out=(1, SHARD_OUT, DM))
    names = list(shapes)
    weights = dict(c_ctx=c_ctx, w_ada=w_ada, b_ada=b_ada, norm_g=norm_g, w_in=w_in, sgu_norm_g=sgu_norm_g,
                   w_spatial=w_spatial, b_spatial=b_spatial, q_norm_g=q_norm_g, k_norm_g=k_norm_g, rpb=rpb, w_out=w_out)
    m_in = dict(zip(names, (m_c_ctx, m_w_ada, m_b_ada, m_norm_g, m_w_in, m_sgu_norm_g, m_w_spatial, m_b_spatial,
                            m_q_norm_g, m_k_norm_g, m_rpb, m_w_out)))
    v_in = dict(zip(names, (v_c_ctx, v_w_ada, v_b_ada, v_norm_g, v_w_in, v_sgu_norm_g, v_w_spatial, v_b_spatial,
                            v_q_norm_g, v_k_norm_g, v_rpb, v_w_out)))
    grads = dict(g_small, w_ada=g_w_ada, w_in=g_w_in, w_out=g_w_out)
    upd = {}
    for n in ("w_ada", "w_in", "w_out"):
        g = grads[n]
        upd[n] = adamw_big(weights[n].reshape(g.shape), g, m_in[n].reshape(g.shape), v_in[n].reshape(g.shape),
                           "adamw_" + n)
    small = [n for n in names if n not in upd]
    res = adamw_small([(weights[n].reshape(grads[n].shape), grads[n], m_in[n].reshape(grads[n].shape),
                        v_in[n].reshape(grads[n].shape)) for n in small])
    upd.update(zip(small, res))
    out = [loss, part["grad_x"].reshape(1, SEQ, DM)]
    out += [grads[n].reshape(shapes[n]) for n in names]
    for slot in range(3):
        out += [upd[n][slot].reshape(shapes[n]) for n in names]
    return tuple(out)
```

```python
import functools

import jax
import jax.numpy as jnp
from jax import lax
from jax.experimental import pallas as pl
from jax.experimental.pallas import tpu as pltpu

F32, BF16 = jnp.float32, jnp.bfloat16
SEQ, DM, CTX, DIN = 4096, 1024, 256, 3584
NCHIP, NDEV = 4, 8
SHARD_IN = DIN // NCHIP
SHARD_ADA = 3 * DM // NCHIP
SHARD_OUT = DM // NCHIP
GRID_W = 64
QROWS = 4
KROWS = 12
QBLK, KBLK = QROWS * GRID_W, KROWS * GRID_W
NQBLK = SEQ // QBLK
HEADS, HDIM, NPAIR = 8, 64, 4
EPS = 1e-6
NEG_INF = -1e30
ZQ, ZK, ZV, ZG = 12, 16, 20, 24
LR, B1, B2, ADAM_EPS, WD, STEP = 0.001, 0.9, 0.999, 1e-08, 0.01, 10
VMEM_BIG = 56 * 1024 * 1024
MESH_ID = pl.DeviceIdType.MESH


def _dot(a, b, lhs_c, rhs_c):
    return lax.dot_general(a.astype(BF16), b.astype(BF16), (((lhs_c,), (rhs_c,)), ((), ())),
                           preferred_element_type=F32)


@jax.custom_vjp
def mm(a, b):
    return _dot(a, b, 1, 0)


@jax.custom_vjp
def mm_nt(a, b):
    return _dot(a, b, 1, 1)


@jax.custom_vjp
def mm_tn(a, b):
    return _dot(a, b, 0, 0)


mm.defvjp(lambda a, b: (mm(a, b), (a, b)), lambda r, ct: (mm_nt(ct, r[1]), mm_tn(r[0], ct)))
mm_nt.defvjp(lambda a, b: (mm_nt(a, b), (a, b)), lambda r, ct: (mm(ct, r[1]), mm_tn(ct, r[0])))
mm_tn.defvjp(lambda a, b: (mm_tn(a, b), (a, b)), lambda r, ct: (mm_nt(r[1], ct), mm(r[0], ct)))


def _rms(x, g):
    return x * lax.rsqrt(jnp.mean(x * x, axis=-1, keepdims=True) + EPS) * g


def _modulated(x, g, scale, shift):
    return _rms(x, g) * (1.0 + scale) + shift


def _pair_rms(x, g2):
    lo = lax.broadcasted_iota(jnp.int32, (1, 2 * HDIM), 1) < HDIM
    sq = x * x
    s_lo = jnp.sum(jnp.where(lo, sq, 0.0), axis=-1, keepdims=True)
    s_hi = jnp.sum(jnp.where(lo, 0.0, sq), axis=-1, keepdims=True)
    rs = jnp.where(lo, lax.rsqrt(s_lo / HDIM + EPS), lax.rsqrt(s_hi / HDIM + EPS))
    return x * rs * g2


def _cparams(sem, vmem=None):
    return pltpu.CompilerParams(dimension_semantics=sem, vmem_limit_bytes=vmem)


def _row(n):
    return pl.BlockSpec((1, n), lambda *_: (0, 0))


CS_ROWS = 8 * NDEV + 8


def _mod_part(mod_ref, row, part):
    pieces = []
    for j in range(NCHIP):
        lo, hi = max(part * DM, j * SHARD_ADA), min((part + 1) * DM, (j + 1) * SHARD_ADA)
        if lo < hi:
            pieces.append(mod_ref[j, row, lo - j * SHARD_ADA:hi - j * SHARD_ADA])
    return jnp.concatenate(pieces, axis=-1)


def inproj_fwd(chip, x, c_vec, c_ctx, w_ada, b_shard, norm_g, w_shard, wo_shard):
    tl = 1024
    nt = SEQ // tl
    halves = (DM // 2, SHARD_OUT // 2)
    n_w, n_c = 12, NDEV - 1

    def kern(k_ref, x_ref, cv_ref, cc_ref, wa_ref, b_ref, g_ref, w_ref, wo_ref,
             z_ref, h_ref, wfull_ref, wofull_ref, modall_ref, csall_ref,
             w_scr, wo_scr, h_scr, mine, cs_scr, mod_scr, shsc_scr, send_sems, recv_sems, out_sems):
        s, t = pl.program_id(0), pl.program_id(1)
        xi, yi, c = _me()
        k, me = 2 * xi + yi, 4 * xi + 2 * yi + c
        sib = _flip(1)
        rows = pl.ds(pl.multiple_of(t * tl, tl), tl)
        gathered = (w_scr, wo_scr)
        slot = lambda d: pl.ds(pl.multiple_of(8 * d, 8), 8)

        def c_copy(q, owner):
            return _rcopy(mine, cs_scr.at[slot(owner), :], send_sems, recv_sems, n_w + q - 1, _flip(q))

        def m_copy(q, chip_of_block):
            return _rcopy(mod_scr.at[chip_of_block], mod_scr.at[chip_of_block], send_sems, recv_sems,
                          n_w + n_c + q // 2 - 1, _flip(q))

        def adaln():
            first = lax.broadcasted_iota(jnp.int32, (8, DM), 0) == 0
            mine[...] = jnp.where(first, jnp.broadcast_to(cv_ref[...], (8, DM)), 0.0)
            cs_scr[slot(me), :] = mine[...]
            cs_scr[slot(NDEV), :] = jnp.where(first, jnp.broadcast_to(cc_ref[...], (8, DM)), 0.0)
            for q in range(1, NDEV):
                c_copy(q, me).start()
            wa = wa_ref[...].astype(BF16)
            for q in range(1, NDEV):
                px, py, pc = _flip(q)
                c_copy(q, 4 * px + 2 * py + pc).wait_recv()
            act = jax.nn.silu(cs_scr[...]).astype(BF16)
            mod_scr[k] = jnp.dot(act, wa, preferred_element_type=F32) + b_ref[...]
            for q in (2, 4, 6):
                m_copy(q, k).start()
            for q in (2, 4, 6):
                m_copy(q, _chip_of(_flip(q))).wait_recv()
            row = pl.ds(8 * me, 1)
            shsc_scr[0:1, :] = _mod_part(mod_scr, row, 0)
            shsc_scr[1:2, :] = _mod_part(mod_scr, row, 1)
            pltpu.sync_copy(mod_scr, modall_ref)
            pltpu.sync_copy(cs_scr, csall_ref)

        def block(n, chip_of_block, hh):
            return gathered[n].at[chip_of_block, pl.ds(pl.multiple_of(hh * halves[n], halves[n]), halves[n]), :]

        def ici(n, q, chip_of_block):
            blk = block(n, chip_of_block, c)
            return _rcopy(blk, blk, send_sems, recv_sems, 6 * n + q // 2 - 1, _flip(q))

        def d2d(n, q, chip_of_block, hh):
            blk = block(n, chip_of_block, hh)
            return _rcopy(blk, blk, send_sems, recv_sems, 6 * n + 3 + q // 2 - 1, sib)

        @pl.when((s == 0) & (t == 0))
        def _():
            adaln()
            w_scr[k] = w_ref[...].astype(BF16)
            wo_scr[k] = wo_ref[...].astype(BF16)
            for q in (2, 4, 6):
                ici(0, q, k).start()
                ici(1, q, k).start()

        for sweep in (1, 2, 3):
            @pl.when((s == sweep) & (t == 0))
            def _():
                q = 2 * sweep
                src = _chip_of(_flip(q))
                for n in (0, 1):
                    ici(n, q, src).wait_recv()
                    d2d(n, q, src, c).start()
                for n in (0, 1):
                    d2d(n, q, src, 1 - c).wait_recv()

        @pl.when(s == 0)
        def _():
            hb = _modulated(x_ref[...], g_ref[...], shsc_scr[1:2, :], shsc_scr[0:1, :]).astype(BF16)
            h_scr[rows, :] = hb
            h_ref[...] = hb

        z_ref[...] = jnp.dot(h_scr[rows, :], w_scr[lax.bitwise_xor(k, s)], preferred_element_type=F32)

        @pl.when((s == NCHIP - 1) & (t == nt - 1))
        def _():
            for q in range(1, NDEV):
                c_copy(q, me).wait_send()
            for q in (2, 4, 6):
                m_copy(q, k).wait_send()
            for n in (0, 1):
                for q in (2, 4, 6):
                    ici(n, q, k).wait_send()
                    d2d(n, q, _chip_of(_flip(q)), c).wait_send()
            outs = [pltpu.make_async_copy(w_scr.at[j], wfull_ref.at[:, j * SHARD_IN:(j + 1) * SHARD_IN], out_sems.at[j])
                    for j in range(NCHIP)] + [pltpu.make_async_copy(wo_scr, wofull_ref, out_sems.at[NCHIP])]
            for cp in outs:
                cp.start()
            for cp in outs:
                cp.wait()

    once = lambda s, t, k: (jnp.where(s == 0, t, nt - 1), 0)
    hbm = pl.BlockSpec(memory_space=pl.ANY)
    n_sem = n_w + n_c + 3
    return pl.pallas_call(
        kern, name="inproj_fwd",
        grid_spec=pltpu.PrefetchScalarGridSpec(
            num_scalar_prefetch=1, grid=(NCHIP, nt),
            in_specs=[pl.BlockSpec((tl, DM), once)] + [_VMEM_SPEC] * 7,
            out_specs=[pl.BlockSpec((tl, SHARD_IN), lambda s, t, k: (t, lax.bitwise_xor(k[0], s))),
                       pl.BlockSpec((tl, DM), once), hbm, hbm, hbm, hbm],
            scratch_shapes=[pltpu.VMEM((NCHIP, DM, SHARD_IN), BF16), pltpu.VMEM((NCHIP, SHARD_OUT, DM), BF16),
                            pltpu.VMEM((SEQ, DM), BF16), pltpu.VMEM((8, DM), F32), pltpu.VMEM((CS_ROWS, DM), F32),
                            pltpu.VMEM((NCHIP, CS_ROWS, SHARD_ADA), F32), pltpu.VMEM((8, DM), F32),
                            pltpu.SemaphoreType.DMA((n_sem,)), pltpu.SemaphoreType.DMA((n_sem,)),
                            pltpu.SemaphoreType.DMA((NCHIP + 1,))]),
        out_shape=[jax.ShapeDtypeStruct((SEQ, DIN), F32), jax.ShapeDtypeStruct((SEQ, DM), BF16),
                   jax.ShapeDtypeStruct((DM, DIN), BF16), jax.ShapeDtypeStruct((NCHIP, SHARD_OUT, DM), BF16),
                   jax.ShapeDtypeStruct((NCHIP, CS_ROWS, SHARD_ADA), F32), jax.ShapeDtypeStruct((CS_ROWS, DM), F32)],
        compiler_params=_cparams(("arbitrary", "arbitrary"), VMEM_BIG),
    )(chip, x, c_vec, c_ctx, w_ada, b_shard, norm_g, w_shard, wo_shard)


def ctx_fwd(ctx, cshift, cscale, norm_g, w_full):
    def kern(c_ref, sh_ref, sc_ref, g_ref, w_ref, zc_ref, hc_ref):
        hc = _modulated(c_ref[...], g_ref[...], sc_ref[...], sh_ref[...]).astype(BF16)
        hc_ref[...] = hc
        zc_ref[...] = jnp.dot(hc, w_ref[...], preferred_element_type=F32)

    return pl.pallas_call(
        kern, name="ctx_fwd", grid=(1,),
        in_specs=[pl.BlockSpec((CTX, DM), lambda i: (0, 0)), _row(DM), _row(DM), _row(DM),
                  pl.BlockSpec((DM, 2 * SHARD_IN), lambda i: (0, 1))],
        out_specs=[pl.BlockSpec((CTX, 2 * SHARD_IN), lambda i: (0, 0)),
                   pl.BlockSpec((CTX, DM), lambda i: (0, 0))],
        out_shape=[jax.ShapeDtypeStruct((CTX, 2 * SHARD_IN), F32), jax.ShapeDtypeStruct((CTX, DM), BF16)],
        compiler_params=_cparams(("arbitrary",)),
    )(ctx, cshift, cscale, norm_g, w_full)


SGU_CHUNK, SGU_PER_STEP = 128, 4


def _gelu(x):
    return 0.5 * x * (1.0 + lax.erf(x * 0.7071067811865476))


def _sgu_chunk(au, av, ag, sg, ws, bsb):
    u, v = _gelu(au), _gelu(av)
    outs = []
    for g in range(4):
        sl = slice(128 * g, 128 * (g + 1))
        mixed = mm(ws[g], _rms(v[:, sl], sg[:, sl])) + bsb[g]
        outs.append(u[:, sl] * mixed * jax.nn.silu(ag[:, sl]))
    return jnp.concatenate(outs, axis=-1)


def _sgu_specs():
    rows = SGU_CHUNK * SGU_PER_STEP
    zspec = lambda c: pl.BlockSpec((rows, 512), lambda n: (n, c))
    wspec = pl.BlockSpec((4, 128, 128), lambda n: (0, 0, 0))
    return rows, [zspec(0), zspec(1), zspec(2), _row(512), wspec, wspec]


def sgu_fwd(z, sg, ws, bsb):
    rows, in_specs = _sgu_specs()

    def kern(au_ref, av_ref, ag_ref, sg_ref, ws_ref, bs_ref, o_ref):
        for c in range(SGU_PER_STEP):
            sl = slice(c * SGU_CHUNK, (c + 1) * SGU_CHUNK)
            o_ref[sl, :] = _sgu_chunk(au_ref[sl, :], av_ref[sl, :], ag_ref[sl, :], sg_ref[...], ws_ref[...],
                                      bs_ref[...])

    return pl.pallas_call(
        kern, name="sgu_fwd", grid=(SEQ // rows,), in_specs=in_specs,
        out_specs=pl.BlockSpec((rows, 512), lambda n: (n, 0)),
        out_shape=jax.ShapeDtypeStruct((SEQ, 512), F32),
        compiler_params=_cparams(("arbitrary",)),
    )(z, z, z, sg, ws, bsb)


def sgu_bwd(z, sg, ws, bsb, dcat):
    rows, in_specs = _sgu_specs()

    def kern(au_ref, av_ref, ag_ref, sg_ref, ws_ref, bs_ref, do_ref, dz_ref, dsg_ref, dws_ref, dbs_ref):
        @pl.when(pl.program_id(0) == 0)
        def _():
            dsg_ref[...] = jnp.zeros_like(dsg_ref)
            dws_ref[...] = jnp.zeros_like(dws_ref)
            dbs_ref[...] = jnp.zeros_like(dbs_ref)

        for c in range(SGU_PER_STEP):
            sl = slice(c * SGU_CHUNK, (c + 1) * SGU_CHUNK)
            _, vjp = jax.vjp(_sgu_chunk, au_ref[sl, :], av_ref[sl, :], ag_ref[sl, :], sg_ref[...], ws_ref[...],
                             bs_ref[...])
            dau, dav, dag, dsg, dws, dbs = vjp(do_ref[sl, :])
            dz_ref[sl, 0:512] = dau.astype(BF16)
            dz_ref[sl, 512:1024] = dav.astype(BF16)
            dz_ref[sl, 1024:1536] = dag.astype(BF16)
            dsg_ref[...] += dsg
            dws_ref[...] += dws
            dbs_ref[...] += dbs

        @pl.when(pl.program_id(0) == pl.num_programs(0) - 1)
        def _():
            dbs_ref[...] = jnp.broadcast_to(jnp.sum(dbs_ref[...], axis=-1, keepdims=True), dbs_ref.shape)

    wspec = pl.BlockSpec((4, 128, 128), lambda n: (0, 0, 0))
    return pl.pallas_call(
        kern, name="sgu_bwd", grid=(SEQ // rows,),
        in_specs=in_specs + [pl.BlockSpec((rows, 512), lambda n: (n, 0))],
        out_specs=[pl.BlockSpec((rows, 1536), lambda n: (n, 0)), _row(512), wspec, wspec],
        out_shape=[jax.ShapeDtypeStruct((SEQ, 1536), BF16), jax.ShapeDtypeStruct((1, 512), F32),
                   jax.ShapeDtypeStruct((4, 128, 128), F32), jax.ShapeDtypeStruct((4, 128, 128), F32)],
        compiler_params=_cparams(("arbitrary",)),
    )(z, z, z, sg, ws, bsb, dcat)


_DR_OFF = (7, 3, -1)


def _row_valid(v, rr, j):
    return (j < 8, rr <= j < rr + 8, 4 <= j < 12)[v]


def _col_window():
    q = lax.broadcasted_iota(jnp.int32, (GRID_W, 128), 0)
    kc = lax.broadcasted_iota(jnp.int32, (GRID_W, 128), 1) % GRID_W
    c0 = jnp.clip(q - 8, 0, GRID_W - 16)
    return (kc >= c0) & (kc < c0 + 16)


def rpb_tables(rpb2):
    def kern(r_ref, b_ref):
        base = r_ref[0]
        lo = lax.broadcasted_iota(jnp.int32, (1, 128), 1) < GRID_W
        win = _col_window()
        tiles = {}
        for v in range(3):
            for rr in range(QROWS):
                for jp in range(KROWS // 2):
                    j0, j1 = 2 * jp, 2 * jp + 1
                    ok0, ok1 = _row_valid(v, rr, j0), _row_valid(v, rr, j1)
                    key = (j0 - rr + _DR_OFF[v], ok0, ok1) if (ok0 or ok1) else None
                    if key not in tiles:
                        if key is None:
                            tiles[key] = jnp.full((GRID_W, 128), NEG_INF, F32)
                        else:
                            d0 = key[0]
                            r0 = base[d0:d0 + 1, :] if ok0 else jnp.zeros((1, 128), F32)
                            r1 = base[d0 + 1:d0 + 2, :] if ok1 else jnp.zeros((1, 128), F32)
                            y = jnp.broadcast_to(jnp.where(lo, r0, r1), (GRID_W, 128))
                            y = pltpu.roll(pltpu.roll(y, 128 - 15, 1), 0, 1, stride=1, stride_axis=0)
                            tiles[key] = jnp.where(win & jnp.where(lo, ok0, ok1), y, NEG_INF)
                    b_ref[v, 0, rr * GRID_W:(rr + 1) * GRID_W, jp * 128:(jp + 1) * 128] = tiles[key]

    return pl.pallas_call(
        kern, name="rpb_tables", grid=(HEADS,),
        in_specs=[pl.BlockSpec((1, 15, 128), lambda h: (h, 0, 0))],
        out_specs=pl.BlockSpec((3, 1, QBLK, KBLK), lambda h: (0, h, 0, 0)),
        out_shape=jax.ShapeDtypeStruct((3, HEADS, QBLK, KBLK), F32),
        compiler_params=_cparams(("arbitrary",)),
    )(rpb2)


def rpb_bwd(dbias):
    def kern(g0_ref, g1_ref, g2_ref, o_ref):
        g_refs = (g0_ref.at[0], g1_ref.at[0], g2_ref.at[0])
        lo = lax.broadcasted_iota(jnp.int32, (1, 128), 1) < GRID_W
        ri = lax.broadcasted_iota(jnp.int32, (GRID_W, GRID_W), 0)
        ci = lax.broadcasted_iota(jnp.int32, (GRID_W, GRID_W), 1)
        flip = (ri + ci == GRID_W - 1).astype(F32)
        groups = {}
        for v in range(3):
            for rr in range(QROWS):
                for jp in range(KROWS // 2):
                    j0, j1 = 2 * jp, 2 * jp + 1
                    ok0, ok1 = _row_valid(v, rr, j0), _row_valid(v, rr, j1)
                    if not (ok0 or ok1):
                        continue
                    g = g_refs[v][0, rr * GRID_W:(rr + 1) * GRID_W, jp * 128:(jp + 1) * 128]
                    key = (j0 - rr + _DR_OFF[v], ok0, ok1)
                    groups[key] = g if key not in groups else groups[key] + g
        acc = [jnp.zeros((1, 128), F32) for _ in range(15)]
        for (d0, ok0, ok1), g in groups.items():
            g = lax.dot_general(flip, g, (((1,), (0,)), ((), ())), precision=lax.Precision.HIGHEST,
                                preferred_element_type=F32)
            g = pltpu.roll(pltpu.roll(g, 128 - 48, 1), 0, 1, stride=1, stride_axis=0)
            s = jnp.sum(g, axis=0, keepdims=True)
            if ok0:
                acc[d0] = acc[d0] + jnp.where(lo, s, 0.0)
            if ok1:
                acc[d0 + 1] = acc[d0 + 1] + jnp.where(lo, 0.0, s)
        for d in range(15):
            o_ref[0, d:d + 1, :] = acc[d] + pltpu.roll(acc[d], GRID_W, 1)

    return pl.pallas_call(
        kern, name="rpb_bwd", grid=(HEADS,),
        in_specs=[pl.BlockSpec((1, 1, QBLK, KBLK), functools.partial(lambda v, h: (v, h, 0, 0), v)) for v in range(3)],
        out_specs=pl.BlockSpec((1, 15, 128), lambda h: (h, 0, 0)),
        out_shape=jax.ShapeDtypeStruct((HEADS, 15, 128), F32),
        compiler_params=_cparams(("arbitrary",)),
    )(dbias, dbias, dbias)


def _scaled_q(q_raw, qg):
    return _pair_rms(q_raw, qg) * (HDIM ** -0.5)


def _head_lanes():
    lo = lax.broadcasted_iota(jnp.int32, (1, 2 * HDIM), 1) < HDIM
    return lo, jnp.logical_not(lo)


SOFTMAX_ROWS = 32


def _emit_interleaved(vector_work, matmul_work):
    for j in range(max(len(vector_work), len(matmul_work))):
        for work in (vector_work, matmul_work):
            if j < len(work):
                work[j]()


def _kblock(i):
    return jnp.clip(i - 1, 0, (SEQ - KBLK) // QBLK)


def _kstart(i):
    return pl.multiple_of(_kblock(i) * QBLK, QBLK)


ATTN_STEPS = NQBLK // 2
ATTN_ROWS = 2 * QBLK
KCOLS = QBLK


def _attn_in_specs():
    return [
        pl.BlockSpec((ATTN_ROWS, 128), lambda p, i: (i, ZQ + p)),
        pl.BlockSpec((SEQ, 128), lambda p, i: (0, ZK + p)),
        pl.BlockSpec((SEQ, 128), lambda p, i: (0, ZV + p)),
        pl.BlockSpec((ATTN_ROWS, 128), lambda p, i: (i, ZG + p)),
        pl.BlockSpec((CTX, 128), lambda p, i: (0, 2 + p)),
        pl.BlockSpec((CTX, 128), lambda p, i: (0, 6 + p)),
    ]


def _bias_specs():
    bias_spec = lambda variant: pl.BlockSpec((1, 2, QBLK, KBLK), lambda p, i: (variant(i), p, 0, 0))
    return [bias_spec(lambda i: jnp.where(i == 0, 0, 1)),
            bias_spec(lambda i: jnp.where(i == ATTN_STEPS - 1, 2, 1))]


def _prob_specs():
    return [pl.BlockSpec((2, ATTN_ROWS, KBLK), lambda p, i: (p, i, 0)),
            pl.BlockSpec((2, ATTN_ROWS, CTX), lambda p, i: (p, i, 0))]


NORM_ROWS = 512


def _half_sums(x):
    lo = lax.broadcasted_iota(jnp.int32, (1, 2 * HDIM), 1) < HDIM
    return jnp.where(lo, jnp.sum(jnp.where(lo, x, 0.0), axis=-1, keepdims=True),
                     jnp.sum(jnp.where(lo, 0.0, x), axis=-1, keepdims=True))


def _pair_rms_bwd(x, g2, ct):
    rs = lax.rsqrt(_half_sums(x * x) / HDIM + EPS)
    y = x * rs
    dy = ct * g2
    return rs * (dy - y * (_half_sums(dy * y) / HDIM)), jnp.sum(ct * y, axis=0, keepdims=True)


def _norm_keys(k_ref, ck_ref, kg_ref, kn_scr, ckn_scr):
    def body(c, carry):
        sl = pl.ds(pl.multiple_of(c * NORM_ROWS, NORM_ROWS), NORM_ROWS)
        kn_scr[sl, :] = _pair_rms(k_ref[sl, :], kg_ref[...]).astype(BF16)
        return carry

    lax.fori_loop(0, SEQ // NORM_ROWS, body, 0)
    ckn_scr[...] = _pair_rms(ck_ref[...], kg_ref[...]).astype(BF16)


def _values_with_ones(v_ref, cv_ref, v1_scr, cv1_scr):
    for a, mine in enumerate(_head_lanes()):
        def body(c, carry):
            sl = pl.ds(pl.multiple_of(c * NORM_ROWS, NORM_ROWS), NORM_ROWS)
            v1_scr[a, sl, :] = jnp.where(mine, v_ref[sl, :], 1.0).astype(BF16)
            return carry

        lax.fori_loop(0, SEQ // NORM_ROWS, body, 0)
        cv1_scr[a] = jnp.where(mine, cv_ref[...], 1.0).astype(BF16)


def _pair_major_spec():
    return pl.BlockSpec((1, ATTN_ROWS, 128), lambda p, i: (p, i, 0))


def _normed_key_specs():
    return [pl.BlockSpec((None, SEQ, 128), lambda p, i: (p, 0, 0)), pl.BlockSpec((None, CTX, 128), lambda p, i: (p, 0, 0))]


def attn_fwd(z, zc, bias, qg2, kg2):
    def kern(q_ref, k_ref, v_ref, bg_ref, ck_ref, cv_ref, be_ref, bo_ref, qg_ref, kg_ref,
             ob_ref, o_ref, rden_ref, pl_ref, pc_ref, kn_ref, ckn_ref, kn_scr, ckn_scr, v1_scr, cv1_scr, s_scr):
        i = pl.program_id(1)

        @pl.when(i == 0)
        def _():
            _norm_keys(k_ref, ck_ref, kg_ref, kn_scr, ckn_scr)
            kn_ref[...] = kn_scr[...]
            ckn_ref[...] = ckn_scr[...]
            _values_with_ones(v_ref, cv_ref, v1_scr, cv1_scr)

        heads = _head_lanes()
        bias_refs = (be_ref, bo_ref)
        tiles = [(b, a) for b in range(2) for a in range(2)]
        rows = [slice(b * QBLK, (b + 1) * QBLK) for b in range(2)]
        qn = [_scaled_q(q_ref[rows[b], :], qg_ref[...]) for b in range(2)]
        qa = [jnp.where(heads[a], qn[b], 0.0).astype(BF16) for b, a in tiles]
        pv = [None] * len(tiles)
        done = {}
        latent = KBLK // KCOLS

        def keys(b, n):
            return pl.ds(pl.multiple_of(_kstart(2 * i + b) + n * KCOLS, KCOLS), KCOLS)

        def score_piece(t, n):
            b, a = tiles[t]
            cols = slice(n * KCOLS, (n + 1) * KCOLS)
            if n < latent:
                s_scr[t, :, cols] = mm_nt(qa[t], kn_scr[keys(b, n), :]) + bias_refs[b][0, a, :, cols]
            else:
                s_scr[t, :, cols] = mm_nt(qa[t], ckn_scr[...])

        def softmax_rows(t, r):
            b, a = tiles[t]
            rs = slice(r * SOFTMAX_ROWS, (r + 1) * SOFTMAX_ROWS)
            out_rows = slice(b * QBLK + rs.start, b * QBLK + rs.stop)
            s = s_scr[t, rs, :]
            p = jnp.exp(s - jnp.max(s, axis=-1, keepdims=True)).astype(BF16)
            pl_ref[a, out_rows, :] = p[:, :KBLK]
            pc_ref[a, out_rows, :] = p[:, KBLK:]

        def value_piece(t, n):
            b, a = tiles[t]
            if n < latent:
                part = mm(pl_ref[a, rows[b], n * KCOLS:(n + 1) * KCOLS], v1_scr[a, keys(b, n), :])
            else:
                part = mm(pc_ref[a, rows[b], :], cv1_scr[a])
            pv[t] = part if pv[t] is None else pv[t] + part
            if n == latent:
                finish(t)

        def finish(t):
            b, a = tiles[t]
            r = jnp.where(heads[a], pltpu.roll(1.0 / pv[t], HDIM, 1), 0.0)
            done[t] = (pv[t] * r, r)
            if a == 1:
                o, rden = (lo + hi for lo, hi in zip(done[t - 1], done[t]))
                ob_ref[rows[b], :] = o * jax.nn.silu(bg_ref[rows[b], :])
                o_ref[0, rows[b], :] = o
                rden_ref[0, rows[b], :] = rden

        pieces = range(latent + 1)
        for n in pieces:
            score_piece(0, n)
        for t in range(len(tiles)):
            matmuls = []
            for n in pieces:
                if t + 1 < len(tiles):
                    matmuls.append(functools.partial(score_piece, t + 1, n))
                if t > 0:
                    matmuls.append(functools.partial(value_piece, t - 1, n))
            _emit_interleaved([functools.partial(softmax_rows, t, r) for r in range(QBLK // SOFTMAX_ROWS)], matmuls)
        for n in pieces:
            value_piece(len(tiles) - 1, n)

    qblk = pl.BlockSpec((ATTN_ROWS, 128), lambda p, i: (i, p))
    return pl.pallas_call(
        kern, name="attn_fwd", grid=(NPAIR, ATTN_STEPS),
        in_specs=_attn_in_specs() + _bias_specs() + [_row(128), _row(128)],
        out_specs=[qblk, _pair_major_spec(), _pair_major_spec()] + _prob_specs() + _normed_key_specs(),
        out_shape=[jax.ShapeDtypeStruct((SEQ, 512), F32)] + [jax.ShapeDtypeStruct((NPAIR, SEQ, 128), F32)] * 2
        + [jax.ShapeDtypeStruct((HEADS, SEQ, KBLK), BF16), jax.ShapeDtypeStruct((HEADS, SEQ, CTX), BF16),
           jax.ShapeDtypeStruct((NPAIR, SEQ, 128), BF16), jax.ShapeDtypeStruct((NPAIR, CTX, 128), BF16)],
        scratch_shapes=[pltpu.VMEM((SEQ, 128), BF16), pltpu.VMEM((CTX, 128), BF16),
                        pltpu.VMEM((2, SEQ, 128), BF16), pltpu.VMEM((2, CTX, 128), BF16),
                        pltpu.VMEM((4, QBLK, KBLK + CTX), F32)],
        compiler_params=_cparams(("arbitrary", "arbitrary"), 40 * 1024 * 1024),
    )(z, z, z, z, zc, zc, bias, bias, qg2, kg2)


def attn_bwd(z, zc, qg2, kg2, dcat, saved):
    def kern(q_ref, k_ref, v_ref, bg_ref, ck_ref, cv_ref, qg_ref, kg_ref, do_ref, o_ref, rden_ref, pl_ref, pc_ref,
             kn_scr, ckn_scr, dq_ref, dk_ref, dv_ref, dbg_ref, dck_ref, dcv_ref, db_ref, dqg_ref, dkg_ref,
             v_scr, cv_scr, dknt_scr, dvt_scr, dcknt_scr, dcvt_scr, dp_scr, ds_scr):
        p, i = pl.program_id(0), pl.program_id(1)
        last = i == ATTN_STEPS - 1

        @pl.when(i == 0)
        def _():
            def body(c, carry):
                sl = pl.ds(pl.multiple_of(c * NORM_ROWS, NORM_ROWS), NORM_ROWS)
                v_scr[sl, :] = v_ref[sl, :].astype(BF16)
                return carry

            lax.fori_loop(0, SEQ // NORM_ROWS, body, 0)
            cv_scr[...] = cv_ref[...].astype(BF16)
            for acc in (dknt_scr, dvt_scr, dcknt_scr, dcvt_scr, db_ref):
                acc[...] = jnp.zeros_like(acc)

        @pl.when((i == 0) & (p == 0))
        def _():
            dqg_ref[...] = jnp.zeros_like(dqg_ref)
            dkg_ref[...] = jnp.zeros_like(dkg_ref)

        heads = _head_lanes()
        tiles = [(b, a) for b in range(2) for a in range(2)]
        rows = [slice(b * QBLK, (b + 1) * QBLK) for b in range(2)]
        kb = [_kblock(2 * i + b) for b in range(2)]
        variant = [jnp.where(i == 0, 0, 1), jnp.where(last, 2, 1)]
        latent = KBLK // KCOLS

        def keys(b, n):
            return pl.ds(pl.multiple_of((kb[b] + n) * KCOLS, KCOLS), KCOLS)

        gated = []
        for b in range(2):
            bg, dout, o = bg_ref[rows[b], :], do_ref[rows[b], :], o_ref[0, rows[b], :]
            sig = jax.nn.sigmoid(bg)
            do = dout * (bg * sig)
            dbg_ref[rows[b], :] = (dout * o * (sig * (1.0 + bg * (1.0 - sig)))).astype(BF16)
            rden = rden_ref[0, rows[b], :]
            dr = do * rden
            qn = _scaled_q(q_ref[rows[b], :], qg_ref[...])
            gated.append((dr, dr.T.astype(BF16), qn.T.astype(BF16), do * o * rden))

        feats = [slice(a * HDIM, (a + 1) * HDIM) for a in range(2)]
        doa, doa_t, qa_t, delta = [], [], [], []
        for b, a in tiles:
            dr, dr_t, qn_t, weighted = gated[b]
            doa.append(jnp.where(heads[a], dr, 0.0).astype(BF16))
            doa_t.append(dr_t[feats[a], :])
            qa_t.append(qn_t[feats[a], :])
            delta.append(jnp.sum(jnp.where(heads[a], weighted, 0.0), axis=-1, keepdims=True))
        dqn = [None] * len(tiles)

        def cols(n):
            return slice(n * KCOLS, (n + 1) * KCOLS)

        def stage_a(t, n):
            b, a = tiles[t]
            if n < latent:
                dp_scr[t, :, cols(n)] = mm_nt(doa[t], v_scr[keys(b, n), :])
                dvt_scr[kb[b] + n, feats[a], :] += mm(doa_t[t], pl_ref[a, rows[b], cols(n)])
            else:
                dp_scr[t, :, cols(n)] = mm_nt(doa[t], cv_scr[...])
                dcvt_scr[feats[a], :] += mm(doa_t[t], pc_ref[a, rows[b], :])

        def stage_b(t, r):
            b, a = tiles[t]
            rs = slice(r * SOFTMAX_ROWS, (r + 1) * SOFTMAX_ROWS)
            in_rows = slice(b * QBLK + rs.start, b * QBLK + rs.stop)
            d = dp_scr[t, rs, :] - delta[t][rs, :]
            ds_lat = pl_ref[a, in_rows, :].astype(F32) * d[:, :KBLK]
            ds_ctx = pc_ref[a, in_rows, :].astype(F32) * d[:, KBLK:]
            db_ref[variant[b], a, rs, :] += ds_lat
            ds_scr[t, rs, :KBLK] = ds_lat.astype(BF16)
            ds_scr[t, rs, KBLK:] = ds_ctx.astype(BF16)

        def stage_c(t, n):
            b, a = tiles[t]
            ds = ds_scr[t, :, cols(n)]
            if n < latent:
                part = mm(ds, kn_scr[keys(b, n), :])
                dknt_scr[kb[b] + n, feats[a], :] += mm(qa_t[t], ds)
            else:
                part = mm(ds, ckn_scr[...])
                dcknt_scr[feats[a], :] += mm(qa_t[t], ds)
            dqn[t] = part if dqn[t] is None else dqn[t] + part
            if n == latent and a == 1:
                both = jnp.where(heads[0], dqn[t - 1], 0.0) + jnp.where(heads[1], dqn[t], 0.0)
                dq, dqg = jax.vjp(_scaled_q, q_ref[rows[b], :], qg_ref[...])[1](both)
                dq_ref[rows[b], :] = dq.astype(BF16)
                dqg_ref[...] += dqg

        pieces = range(latent + 1)
        for n in pieces:
            stage_a(0, n)
        for t in range(len(tiles)):
            matmuls = []
            for n in pieces:
                if t + 1 < len(tiles):
                    matmuls.append(functools.partial(stage_a, t + 1, n))
                if t > 0:
                    matmuls.append(functools.partial(stage_c, t - 1, n))
            _emit_interleaved([functools.partial(stage_b, t, r) for r in range(QBLK // SOFTMAX_ROWS)], matmuls)
        for n in pieces:
            stage_c(len(tiles) - 1, n)

        @pl.when(last)
        def _():
            eye = (lax.broadcasted_iota(jnp.int32, (KCOLS, KCOLS), 0)
                   == lax.broadcasted_iota(jnp.int32, (KCOLS, KCOLS), 1)).astype(BF16)

            def turned(x):
                hi = x.astype(BF16)
                return mm_nt(eye, hi) + mm_nt(eye, x - hi.astype(F32))

            def body(c, dkg):
                sl = pl.ds(pl.multiple_of(c * NORM_ROWS, NORM_ROWS), NORM_ROWS)
                blocks = range(NORM_ROWS // KCOLS)
                dkn = jnp.concatenate([turned(dknt_scr[c * len(blocks) + n]) for n in blocks], axis=0)
                dv = jnp.concatenate([mm_nt(eye, dvt_scr[c * len(blocks) + n]) for n in blocks], axis=0)
                dk, dg = _pair_rms_bwd(k_ref[sl, :], kg_ref[...], dkn)
                dk_ref[sl, :] = dk.astype(BF16)
                dv_ref[sl, :] = dv.astype(BF16)
                return dkg + dg

            dkg = lax.fori_loop(0, SEQ // NORM_ROWS, body, jnp.zeros((1, 128), F32))
            dck, dg = _pair_rms_bwd(ck_ref[...], kg_ref[...], dcknt_scr[...].T)
            dck_ref[...] = dck
            dcv_ref[...] = dcvt_scr[...].T
            dkg_ref[...] += dkg + dg

        @pl.when(last & (p == NPAIR - 1))
        def _():
            dqg_ref[...] = dqg_ref[...] + pltpu.roll(dqg_ref[...], HDIM, 1)
            dkg_ref[...] = dkg_ref[...] + pltpu.roll(dkg_ref[...], HDIM, 1)

    blk = lambda rows: pl.BlockSpec((rows, 128), lambda p, i: (0, p))
    qblk = pl.BlockSpec((ATTN_ROWS, 128), lambda p, i: (i, p))
    return pl.pallas_call(
        kern, name="attn_bwd", grid=(NPAIR, ATTN_STEPS),
        in_specs=_attn_in_specs() + [_row(128), _row(128), pl.BlockSpec((ATTN_ROWS, 128), lambda p, i: (i, 4 + p)),
                                     _pair_major_spec(), _pair_major_spec()] + _prob_specs() + _normed_key_specs(),
        out_specs=[qblk, blk(SEQ), blk(SEQ), qblk, blk(CTX), blk(CTX),
                   pl.BlockSpec((3, 2, QBLK, KBLK), lambda p, i: (0, p, 0, 0)), _row(128), _row(128)],
        out_shape=[jax.ShapeDtypeStruct((SEQ, 512), BF16)] * 4 + [jax.ShapeDtypeStruct((CTX, 512), F32)] * 2
        + [jax.ShapeDtypeStruct((3, HEADS, QBLK, KBLK), F32)]
        + [jax.ShapeDtypeStruct((1, 128), F32), jax.ShapeDtypeStruct((1, 128), F32)],
        scratch_shapes=[pltpu.VMEM((SEQ, 128), BF16), pltpu.VMEM((CTX, 128), BF16),
                        pltpu.VMEM((SEQ // KCOLS, 128, KCOLS), F32), pltpu.VMEM((SEQ // KCOLS, 128, KCOLS), F32),
                        pltpu.VMEM((128, CTX), F32), pltpu.VMEM((128, CTX), F32),
                        pltpu.VMEM((4, QBLK, KBLK + CTX), F32), pltpu.VMEM((4, QBLK, KBLK + CTX), BF16)],
        compiler_params=_cparams(("arbitrary", "arbitrary"), VMEM_BIG),
    )(z, z, z, z, zc, zc, qg2, kg2, dcat, *saved)


def outproj(out_a, out_b, x, target, gate, wo):
    tl = 512

    def kern(a_ref, b_ref, x_ref, t_ref, g_ref, w_ref, loss_ref, dy_ref, dcat_ref, dg_ref, dw_ref):
        @pl.when(pl.program_id(0) == 0)
        def _():
            loss_ref[...] = jnp.zeros_like(loss_ref)
            dg_ref[...] = jnp.zeros_like(dg_ref)
            dw_ref[...] = jnp.zeros_like(dw_ref)

        a, b = a_ref[...].astype(BF16), b_ref[...].astype(BF16)
        mix = (jnp.dot(a, w_ref[0:512, :], preferred_element_type=F32)
               + jnp.dot(b, w_ref[512:1024, :], preferred_element_type=F32))
        err = x_ref[...] + g_ref[...] * mix - t_ref[...]
        loss_ref[...] += 0.5 * jnp.sum(jnp.mean(err * err, axis=-1))
        dy = err * (1.0 / DM)
        dy_ref[...] = dy
        dg_ref[...] += jnp.sum(dy * mix, axis=0, keepdims=True)
        dmix = (g_ref[...] * dy).astype(BF16)
        dcat_ref[...] = lax.dot_general(dmix, w_ref[...], (((1,), (1,)), ((), ())), preferred_element_type=F32)
        dw_ref[0:512, :] += lax.dot_general(a, dmix, (((0,), (0,)), ((), ())), preferred_element_type=F32)
        dw_ref[512:1024, :] += lax.dot_general(b, dmix, (((0,), (0,)), ((), ())), preferred_element_type=F32)

    tile = lambda w: pl.BlockSpec((tl, w), lambda t: (t, 0))
    whole = pl.BlockSpec((DM, DM), lambda t: (0, 0))
    return pl.pallas_call(
        kern, name="outproj", grid=(SEQ // tl,),
        in_specs=[tile(512), tile(512), tile(DM), tile(DM), _row(DM), whole],
        out_specs=[pl.BlockSpec((8, 128), lambda t: (0, 0)), tile(DM), tile(DM), _row(DM), whole],
        out_shape=[jax.ShapeDtypeStruct((8, 128), F32), jax.ShapeDtypeStruct((SEQ, DM), F32),
                   jax.ShapeDtypeStruct((SEQ, DM), F32), jax.ShapeDtypeStruct((1, DM), F32),
                   jax.ShapeDtypeStruct((DM, DM), F32)],
        compiler_params=_cparams(("arbitrary",), 48 * 1024 * 1024),
    )(out_a, out_b, x, target, gate, wo)


DZ_COLS = (("a", 0, 1536), ("q", 1536, 2048), ("k", 2048, 2560), ("v", 2560, 3072), ("g", 3072, DIN))
DZC_COLS = (("k", 2048, 2560), ("v", 2560, 3072))
_NT = (((1,), (1,)), ((), ()))


DH_SUBTILES = 2


def _dz_specs(tl):
    return [pl.BlockSpec((tl, 1536), lambda t: (t, 0))] + [pl.BlockSpec((tl, 512), lambda t: (t, 0))] * 4


def dh_bwd(dz_parts, w_full, x, dy, shift, scale, norm_g, dg_ctx):
    tl = 512
    nt = SEQ // tl

    def kern(a_ref, q_ref, k_ref, v_ref, g_ref, w_ref, x_ref, dy_ref, sh_ref, sc_ref, gn_ref, dgc_ref,
             gx_ref, dsh_ref, dsc_ref, dg_ref):
        @pl.when(pl.program_id(0) == 0)
        def _():
            dsh_ref[...] = jnp.zeros_like(dsh_ref)
            dsc_ref[...] = jnp.zeros_like(dsc_ref)
            dg_ref[...] = dgc_ref[...]

        src = dict(a=a_ref, q=q_ref, k=k_ref, v=v_ref, g=g_ref)
        for sub in range(DH_SUBTILES):
            rows = slice(sub * tl // DH_SUBTILES, (sub + 1) * tl // DH_SUBTILES)
            dh = None
            for name, c0, c1 in DZ_COLS:
                part = lax.dot_general(src[name][rows, :], w_ref[:, c0:c1], _NT, preferred_element_type=F32)
                dh = part if dh is None else dh + part
            _, vjp = jax.vjp(_modulated, x_ref[rows, :], gn_ref[...], sc_ref[...], sh_ref[...])
            dx, dg, dsc, dsh = vjp(dh)
            gx_ref[rows, :] = dy_ref[rows, :] + dx
            dg_ref[...] += dg
            dsc_ref[...] += dsc
            dsh_ref[...] += dsh

    tile = pl.BlockSpec((tl, DM), lambda t: (t, 0))
    return pl.pallas_call(
        kern, name="dh_bwd", grid=(nt,),
        in_specs=_dz_specs(tl) + [pl.BlockSpec((DM, DIN), lambda t: (0, 0)), tile, tile, _row(DM),
                                  _row(DM), _row(DM), _row(DM)],
        out_specs=[tile, _row(DM), _row(DM), _row(DM)],
        out_shape=[jax.ShapeDtypeStruct((SEQ, DM), F32)] + [jax.ShapeDtypeStruct((1, DM), F32)] * 3,
        compiler_params=_cparams(("arbitrary",), 48 * 1024 * 1024),
    )(*dz_parts, w_full, x, dy, shift, scale, norm_g, dg_ctx)


def dw_bwd(h, dz_parts, hc, dck, dcv, g_out):
    tl = 512
    nt = SEQ // tl
    (rhi, wi), (rho, wo) = RS_SHAPES

    def kern(h_ref, a_ref, q_ref, k_ref, v_ref, g_ref, hc_ref, dck_ref, dcv_ref, go_hbm,
             wire_i, keep_i, wire_o, keep_o, acc, snd_i, rcv_i, mine_o, rcv_o, load_sem, send_sems, recv_sems):
        t = pl.program_id(0)
        x, y, c = _me()
        k = 2 * x + y
        sib = _flip(1)
        half = lambda hh, rh: pl.ds(pl.multiple_of(hh * rh, rh), rh)
        load_o = pltpu.make_async_copy(go_hbm.at[:, half(c, rho), :], mine_o, load_sem)
        pair_o = _rcopy(go_hbm.at[:, half(1 - c, rho), :], rcv_o, send_sems, recv_sems, 0, sib)
        pair_i = [_rcopy(snd_i.at[j], rcv_i.at[j], send_sems, recv_sems, 1 + j, sib) for j in range(NCHIP)]

        @pl.when(t == 0)
        def _():
            load_o.start()
            pair_o.start()
            acc[...] = jnp.zeros_like(acc)
            hct = hc_ref[...].T
            csrc = dict(k=dck_ref, v=dcv_ref)
            for name, c0, c1 in DZC_COLS:
                acc[:, c0:c1] += jnp.dot(hct, csrc[name][...].astype(BF16), preferred_element_type=F32)

        ht = h_ref[...].T
        src = dict(a=a_ref, q=q_ref, k=k_ref, v=v_ref, g=g_ref)
        for name, c0, c1 in DZ_COLS:
            acc[:, c0:c1] += jnp.dot(ht, src[name][...], preferred_element_type=F32)

        @pl.when(t == nt - 1)
        def _():
            shard = lambda j: slice(j * SHARD_IN, (j + 1) * SHARD_IN)
            for j in range(NCHIP):
                snd_i[j] = acc[half(1 - c, rhi), shard(j)].astype(BF16)
                pair_i[j].start()
            load_o.wait()
            pair_o.wait_recv()
            for j in range(NCHIP):
                wire_o[j] = (mine_o[j] + rcv_o[j]).astype(BF16)
            keep_o[...] = mine_o[k] + rcv_o[k]
            mine = half(c, rhi)
            for j in range(NCHIP):
                pair_i[j].wait_recv()
                pair_sum = acc[mine, shard(j)] + rcv_i[j].astype(F32)
                wire_i[j] = pair_sum.astype(BF16)

                @pl.when(k == j)
                def _():
                    keep_i[...] = pair_sum
            pair_o.wait_send()
            for j in range(NCHIP):
                pair_i[j].wait_send()

    whole = lambda *shape: pl.BlockSpec(shape, lambda t: (0,) * len(shape))
    return pl.pallas_call(
        kern, name="dw_bwd", grid=(nt,),
        in_specs=[pl.BlockSpec((tl, DM), lambda t: (t, 0))] + _dz_specs(tl)
        + [whole(CTX, DM), whole(CTX, 512), whole(CTX, 512), pl.BlockSpec(memory_space=pl.ANY)],
        out_specs=[whole(NCHIP, rhi, wi), whole(rhi, wi), whole(NCHIP, rho, wo), whole(rho, wo)],
        out_shape=[jax.ShapeDtypeStruct((NCHIP, rhi, wi), BF16), jax.ShapeDtypeStruct((rhi, wi), F32),
                   jax.ShapeDtypeStruct((NCHIP, rho, wo), BF16), jax.ShapeDtypeStruct((rho, wo), F32)],
        scratch_shapes=[pltpu.VMEM((DM, DIN), F32), pltpu.VMEM((NCHIP, rhi, wi), BF16),
                        pltpu.VMEM((NCHIP, rhi, wi), BF16),
                        pltpu.VMEM((NCHIP, rho, wo), F32), pltpu.VMEM((NCHIP, rho, wo), F32),
                        pltpu.SemaphoreType.DMA(()), pltpu.SemaphoreType.DMA((1 + NCHIP,)),
                        pltpu.SemaphoreType.DMA((1 + NCHIP,))],
        compiler_params=_cparams(("arbitrary",), VMEM_BIG),
    )(h, *dz_parts, hc, dck, dcv, g_out)


def ctx_bwd(dck, dcv, w_full, ctx, cshift, cscale, norm_g):
    def kern(dck_ref, dcv_ref, w_ref, c_ref, sh_ref, sc_ref, g_ref, dsh_ref, dsc_ref, dg_ref):
        csrc = dict(k=dck_ref, v=dcv_ref)
        dhc = None
        first = DZC_COLS[0][1]
        for name, c0, c1 in DZC_COLS:
            part = lax.dot_general(csrc[name][...].astype(BF16), w_ref[:, c0 - first:c1 - first], _NT,
                                   preferred_element_type=F32)
            dhc = part if dhc is None else dhc + part
        _, vjp = jax.vjp(lambda g, sc, sh: _modulated(c_ref[...], g, sc, sh), g_ref[...], sc_ref[...], sh_ref[...])
        dg_ref[...], dsc_ref[...], dsh_ref[...] = vjp(dhc)

    whole = lambda r, c: pl.BlockSpec((r, c), lambda i: (0, 0))
    return pl.pallas_call(
        kern, name="ctx_bwd", grid=(1,),
        in_specs=[whole(CTX, 512), whole(CTX, 512), pl.BlockSpec((DM, 1024), lambda i: (0, DZC_COLS[0][1] // 1024)),
                  whole(CTX, DM), _row(DM), _row(DM), _row(DM)],
        out_specs=[_row(DM), _row(DM), _row(DM)],
        out_shape=[jax.ShapeDtypeStruct((1, DM), F32)] * 3,
        compiler_params=_cparams(("arbitrary",), 40 * 1024 * 1024),
    )(dck, dcv, w_full, ctx, cshift, cscale, norm_g)


def _lane_pad_rpb(rpb):
    r = jnp.pad(rpb, ((0, 0), (0, 0), (0, GRID_W - rpb.shape[-1])))
    return jnp.concatenate([r, r], axis=-1)


def local_step(chip, dev, x, c_vec, c_ctx, w_ada, b_shard, ctx, target, norm_g, sgu_g, w_s, b_s, q_g, k_g, rpb,
               w_in_shard, w_out_shard):
    bsb = jnp.broadcast_to(b_s[:, :, None], (4, 128, 128))
    qg2, kg2 = jnp.tile(q_g, (1, 2)), jnp.tile(k_g, (1, 2))

    z, h, w_in_full, w_out_full, mod_all, cs = inproj_fwd(chip, x, c_vec, c_ctx, w_ada, b_shard, norm_g, w_in_shard,
                                                          w_out_shard)
    mods = mod_all.transpose(1, 0, 2).reshape(CS_ROWS, 3 * DM)
    mod = lax.dynamic_slice(mods, (8 * dev, 0), (1, 3 * DM))
    shift, scale, gate = mod[:, :DM], mod[:, DM:2 * DM], mod[:, 2 * DM:]
    cshift, cscale = mods[8 * NDEV:8 * NDEV + 1, :DM], mods[8 * NDEV:8 * NDEV + 1, DM:2 * DM]
    zc, hc = ctx_fwd(ctx, cshift, cscale, norm_g, w_in_full)
    bias = rpb_tables(_lane_pad_rpb(rpb))
    out_a = sgu_fwd(z, sgu_g, w_s, bsb)
    out_b, *saved = attn_fwd(z, zc, bias, qg2, kg2)
    loss8, dy, dcat, dgate, dwo = outproj(out_a, out_b, x, target, gate, w_out_full.reshape(DM, DM))
    dz_a, dsg, dws, dbsb = sgu_bwd(z, sgu_g, w_s, bsb, dcat)
    dq, dk, dv, dbg, dck, dcv, dbias, dqg2, dkg2 = attn_bwd(z, zc, qg2, kg2, dcat, saved)
    drpb = rpb_bwd(dbias)[:, :, :rpb.shape[-1]]
    dz_parts = (dz_a, dq, dk, dv, dbg)
    dcshift, dcscale, dng_c = ctx_bwd(dck, dcv, w_in_full, ctx, cshift, cscale, norm_g)
    wire_i, keep_i, wire_o, keep_o = dw_bwd(h, dz_parts, hc, dck, dcv, dwo.reshape(NCHIP, SHARD_OUT, DM))
    *in_flight, token = rs_start(wire_i, wire_o)
    grad_x, dshift, dscale, dng = dh_bwd(dz_parts, w_in_full, x, dy, shift, scale, norm_g, dng_c + token[0, 0])
    got_i, got_o = rs_wait(*in_flight, dshift)
    return dict(
        loss=loss8[0:1, 0:1], grad_x=grad_x, rs=(keep_i, got_i, keep_o, got_o), cs=cs,
        dmod=jnp.concatenate([dshift, dscale, dgate], axis=-1),
        dcmod=jnp.concatenate([dcshift, dcscale, jnp.zeros((1, DM), F32)], axis=-1),
        d_norm_g=dng, d_sgu_g=dsg, d_w_s=dws, d_b_s=dbsb[:, :, 0],
        d_q_g=dqg2[:, :HDIM], d_k_g=dkg2[:, :HDIM], d_rpb=drpb)


def _me():
    return lax.axis_index("x"), lax.axis_index("y"), lax.axis_index("c")


def _flip(q):
    x, y, c = _me()
    return ((1 - x) if q & 4 else x, (1 - y) if q & 2 else y, (1 - c) if q & 1 else c)


def _chip_of(dev):
    return 2 * dev[0] + dev[1]


def _rcopy(src, dst, send_sems, recv_sems, k, dev):
    return pltpu.make_async_remote_copy(src_ref=src, dst_ref=dst, send_sem=send_sems.at[k], recv_sem=recv_sems.at[k],
                                        device_id=dev, device_id_type=MESH_ID)


_VMEM_SPEC = pl.BlockSpec(memory_space=pltpu.VMEM)
SLAB_ROWS = 80


RS_SHAPES = ((DM // 2, SHARD_IN), (SHARD_OUT // 2, DM))
_HBM_SPEC = pl.BlockSpec(memory_space=pltpu.HBM)
_SEM_SPEC = pl.BlockSpec(memory_space=pltpu.SEMAPHORE)
_IN_FLIGHT = pltpu.SideEffectType.DATAFLOW_SIDE_EFFECTING


def _rs_copies(wires, lands, send_sems, recv_sems):
    return [pltpu.make_async_remote_copy(
        src_ref=wires[n].at[_chip_of(_flip(q))], dst_ref=lands[n].at[q // 2 - 1],
        send_sem=send_sems.at[3 * n + q // 2 - 1], recv_sem=recv_sems.at[3 * n + q // 2 - 1],
        device_id=_flip(q), device_id_type=MESH_ID) for n in (0, 1) for q in (2, 4, 6)]


def rs_start(wire_i, wire_o):
    lands = [lax.empty((NCHIP - 1, rh, w), BF16) for rh, w in RS_SHAPES]

    def body(wi_ref, wo_ref, li_ref, lo_ref, send_sems, recv_sems, wi_thru, wo_thru, li_thru, lo_thru, token):
        for cp in _rs_copies((wi_ref, wo_ref), (li_ref, lo_ref), send_sems, recv_sems):
            cp.start()
        token[...] = jnp.zeros_like(token)

    hbm = lambda a: pltpu.HBM(a.shape, a.dtype)
    return pl.pallas_call(
        body, name="rs_start",
        out_shape=(pltpu.SemaphoreType.DMA((6,)), pltpu.SemaphoreType.DMA((6,)), hbm(wire_i), hbm(wire_o),
                   hbm(lands[0]), hbm(lands[1]), jax.ShapeDtypeStruct((8, 128), F32)),
        in_specs=(_HBM_SPEC,) * 4, out_specs=(_SEM_SPEC, _SEM_SPEC) + (_HBM_SPEC,) * 4 + (_VMEM_SPEC,),
        input_output_aliases={0: 2, 1: 3, 2: 4, 3: 5},
        compiler_params=pltpu.CompilerParams(has_side_effects=_IN_FLIGHT),
    )(*[pltpu.with_memory_space_constraint(a, pltpu.HBM) for a in (wire_i, wire_o, *lands)])


def rs_wait(send_sems, recv_sems, wire_i, wire_o, land_i, land_o, after):
    def body(wi_ref, wo_ref, li_ref, lo_ref, send_sems, recv_sems, after_ref, wi_dead, wo_dead, gi_ref, go_ref):
        for cp in _rs_copies((wi_ref, wo_ref), (li_ref, lo_ref), send_sems, recv_sems):
            cp.wait_send()
            cp.wait_recv()

    hbm = lambda a: pltpu.HBM(a.shape, a.dtype)
    return pl.pallas_call(
        body, name="rs_wait", out_shape=(hbm(wire_i), hbm(wire_o), hbm(land_i), hbm(land_o)),
        in_specs=(_HBM_SPEC,) * 4 + (_SEM_SPEC, _SEM_SPEC, pl.BlockSpec(memory_space=pl.ANY)),
        out_specs=(_HBM_SPEC,) * 4, input_output_aliases={0: 0, 1: 1, 2: 2, 3: 3},
        compiler_params=pltpu.CompilerParams(has_side_effects=_IN_FLIGHT),
    )(wire_i, wire_o, land_i, land_o, send_sems, recv_sems, after)[2:]


def final_reduce(keep_i, got_i, keep_o, got_o, slab, cs, w_ada, c_ctx):
    (rhi, wi), (rho, wo) = RS_SHAPES

    def kern(ki_hbm, gi_hbm, ko_hbm, go_hbm, s_ref, cs_ref, w_hbm, cc_ref,
             gin_ref, gout_ref, tot_ref, dw_ref, db_ref, dcc_ref,
             ki, gi, ko, go, w_scr, all_ref, dms_scr, parts, load_sems, send_sems, recv_sems):
        x, y, c = _me()
        k = 2 * x + y
        sib = _flip(1)
        dev = lambda d: 4 * d[0] + 2 * d[1] + d[2]
        me = dev((x, y, c))

        def slab_copy(idx, owner, to):
            return _rcopy(all_ref.at[dev(owner)], all_ref.at[dev(owner)], send_sems, recv_sems, idx, to)

        all_ref[me] = s_ref[...]
        first = [slab_copy(0, (x, y, c), sib)] + [slab_copy(q // 2, (x, y, c), _flip(q)) for q in (2, 4, 6)]
        for cp in first:
            cp.start()
        loads = [pltpu.make_async_copy(src, dst, load_sems.at[n]) for n, (src, dst) in enumerate(
            ((ki_hbm, ki), (gi_hbm, gi), (ko_hbm, ko), (go_hbm, go), (w_hbm, w_scr)))]
        for cp in loads:
            cp.start()

        shares = []
        for n, (keep, got, out) in enumerate(((ki, gi, gin_ref), (ko, go, gout_ref))):
            rh = RS_SHAPES[n][0]
            half = lambda hh, rh=rh: pl.ds(pl.multiple_of(hh * rh, rh), rh)
            loads[2 * n].wait()
            loads[2 * n + 1].wait()
            out[half(c), :] = ((keep[...] + got[0].astype(F32)) + got[1].astype(F32)) + got[2].astype(F32)
            share = _rcopy(out.at[half(c), :], out.at[half(c), :], send_sems, recv_sems, 7 + n, sib)
            share.start()
            shares.append((share, _rcopy(out.at[half(1 - c), :], out.at[half(1 - c), :], send_sems, recv_sems, 7 + n,
                                         sib)))

        passed = []
        for q in (2, 4, 6):
            slab_copy(q // 2, _flip(q), (x, y, c)).wait_recv()
            cp = slab_copy(3 + q // 2, _flip(q), sib)
            cp.start()
            passed.append(cp)
        slab_copy(0, sib, (x, y, c)).wait_recv()
        for q in (2, 4, 6):
            slab_copy(3 + q // 2, _flip(q | 1), (x, y, c)).wait_recv()
        tot = all_ref[0]
        for d in range(1, NDEV):
            tot = tot + all_ref[d]
        tot_ref[...] = tot

        pad = jnp.zeros((7, DM), F32)
        dm = [jnp.concatenate([all_ref[d, 12 + j:13 + j, :] for d in range(NDEV)] + [tot[9 + j:10 + j, :], pad], axis=0)
              for j in range(3)]
        db_ref[...] = jnp.concatenate([jnp.sum(part, axis=0, keepdims=True) for part in dm], axis=0)
        dm = jnp.concatenate(dm, axis=-1)
        for j in range(NCHIP):
            @pl.when(k == j)
            def _():
                dms_scr[...] = dm[:, j * SHARD_ADA:(j + 1) * SHARD_ADA].astype(BF16)

        a_in = jnp.concatenate([cs_ref[8 * d:8 * d + 1, :] for d in range(NDEV)]
                               + [cs_ref[8 * NDEV:8 * NDEV + 1, :], pad], axis=0)
        act = jax.nn.silu(a_in).astype(BF16)
        dms = dms_scr[...]
        dw_ref[...] = lax.dot_general(act, dms, (((0,), (0,)), ((), ())), preferred_element_type=F32)
        loads[4].wait()
        parts[k] = lax.dot_general(dms, w_scr[...].astype(BF16), (((1,), (1,)), ((), ())), preferred_element_type=F32)
        sends = [_rcopy(parts.at[k], parts.at[k], send_sems, recv_sems, 8 + q // 2, _flip(q)) for q in (2, 4, 6)]
        for cp in sends:
            cp.start()
        for q in (2, 4, 6):
            kq = _chip_of(_flip(q))
            _rcopy(parts.at[kq], parts.at[kq], send_sems, recv_sems, 8 + q // 2, _flip(q)).wait_recv()
        dact = ((parts[0] + parts[1]) + parts[2]) + parts[3]
        _, vjp = jax.vjp(jax.nn.silu, cc_ref[...])
        dcc_ref[...] = vjp(dact[8:9, :])[0]

        for share, arrival in shares:
            arrival.wait_recv()
            share.wait_send()
        for cp in first + passed + sends:
            cp.wait_send()

    any_spec = pl.BlockSpec(memory_space=pl.ANY)
    return pl.pallas_call(
        kern, name="final_reduce",
        in_specs=[any_spec] * 4 + [_VMEM_SPEC, _VMEM_SPEC, any_spec, _VMEM_SPEC], out_specs=[_VMEM_SPEC] * 6,
        out_shape=[jax.ShapeDtypeStruct((2 * rhi, wi), F32), jax.ShapeDtypeStruct((2 * rho, wo), F32),
                   jax.ShapeDtypeStruct((SLAB_ROWS, DM), F32), jax.ShapeDtypeStruct((DM, SHARD_ADA), F32),
                   jax.ShapeDtypeStruct((3, DM), F32), jax.ShapeDtypeStruct((1, DM), F32)],
        scratch_shapes=[pltpu.VMEM((rhi, wi), F32), pltpu.VMEM((NCHIP - 1, rhi, wi), BF16),
                        pltpu.VMEM((rho, wo), F32), pltpu.VMEM((NCHIP - 1, rho, wo), BF16),
                        pltpu.VMEM((DM, SHARD_ADA), F32), pltpu.VMEM((NDEV, SLAB_ROWS, DM), F32),
                        pltpu.VMEM((16, SHARD_ADA), BF16), pltpu.VMEM((NCHIP, 16, DM), F32),
                        pltpu.SemaphoreType.DMA((5,)), pltpu.SemaphoreType.DMA((12,)), pltpu.SemaphoreType.DMA((12,))],
        compiler_params=pltpu.CompilerParams(vmem_limit_bytes=40 * 1024 * 1024),
    )(keep_i, got_i, keep_o, got_o, slab, cs, w_ada, c_ctx)


def _adamw_math(w, g, m, v):
    m = B1 * m + (1.0 - B1) * g
    v = B2 * v + (1.0 - B2) * (g * g)
    m_hat = m / (1.0 - B1 ** STEP)
    v_hat = v / (1.0 - B2 ** STEP)
    return -LR * (m_hat / (jnp.sqrt(v_hat) + ADAM_EPS) + WD * w), m, v


def adamw_big(w, g, m, v, name, block_rows=256):
    rows, width = w.shape

    def kern(w_ref, g_ref, m_ref, v_ref, d_ref, nm_ref, nv_ref):
        d_ref[...], nm_ref[...], nv_ref[...] = _adamw_math(w_ref[...], g_ref[...], m_ref[...], v_ref[...])

    spec = pl.BlockSpec((block_rows, width), lambda i: (i, 0))
    return pl.pallas_call(
        kern, name=name, grid=(rows // block_rows,), in_specs=[spec] * 4, out_specs=[spec] * 3,
        out_shape=[jax.ShapeDtypeStruct((rows, width), F32)] * 3,
        compiler_params=_cparams(("arbitrary",)),
    )(w, g, m, v)


def adamw_small(quads):
    n = len(quads)

    def kern(*refs):
        ins, outs = refs[:4 * n], refs[4 * n:]
        for i in range(n):
            w, g, m, v = (r[...] for r in ins[4 * i:4 * i + 4])
            outs[3 * i][...], outs[3 * i + 1][...], outs[3 * i + 2][...] = _adamw_math(w, g, m, v)

    flat = [a for quad in quads for a in quad]
    res = pl.pallas_call(
        kern, name="adamw_small", in_specs=[_VMEM_SPEC] * (4 * n), out_specs=[_VMEM_SPEC] * (3 * n),
        out_shape=[jax.ShapeDtypeStruct(q[0].shape, F32) for q in quads for _ in range(3)],
    )(*flat)
    return [tuple(res[3 * i:3 * i + 3]) for i in range(n)]


def _rows_of(a, rows):
    flat = a.reshape(-1)
    return jnp.pad(flat, (0, rows * DM - flat.shape[0])).reshape(rows, DM)


def kernel(x, c, ctx, c_ctx, w_ada, b_ada, norm_g, w_in, sgu_norm_g, w_spatial, b_spatial, q_norm_g, k_norm_g, rpb, w_out, loss_target, m_c_ctx, m_w_ada, m_b_ada, m_norm_g, m_w_in, m_sgu_norm_g, m_w_spatial, m_b_spatial, m_q_norm_g, m_k_norm_g, m_rpb, m_w_out, v_c_ctx, v_w_ada, v_b_ada, v_norm_g, v_w_in, v_sgu_norm_g, v_w_spatial, v_b_spatial, v_q_norm_g, v_k_norm_g, v_rpb, v_w_out):
    xi, yi, ci = lax.axis_index("x"), lax.axis_index("y"), lax.axis_index("c")
    chip, dev = 2 * xi + yi, 4 * xi + 2 * yi + ci
    c_ctx2 = c_ctx.reshape(1, DM)

    b_shard = lax.dynamic_slice(b_ada, (0, chip * SHARD_ADA), (1, SHARD_ADA))
    part = local_step(chip.reshape(1).astype(jnp.int32), dev, x[0], c, c_ctx2, w_ada[0], b_shard, ctx[0], loss_target[0],
                      norm_g, sgu_norm_g, w_spatial[0], b_spatial[0], q_norm_g, k_norm_g, rpb[0], w_in[0], w_out[0])
    cs = part["cs"]

    slab = jnp.concatenate([
        part["d_norm_g"], _rows_of(part["d_sgu_g"], 1), _rows_of(part["d_b_s"], 1),
        _rows_of(jnp.concatenate([part["d_q_g"], part["d_k_g"]], axis=-1), 1), _rows_of(part["d_rpb"], 4),
        _rows_of(part["loss"], 1), _rows_of(part["dcmod"], 3), _rows_of(part["dmod"], 3), jnp.zeros((1, DM), F32),
        _rows_of(part["d_w_s"], 64)], axis=0)
    g_w_in, g_w_out, tot, g_w_ada, g_b_ada, g_c_ctx = final_reduce(*part["rs"], slab, cs, w_ada[0], c_ctx2)
    g_b_ada = g_b_ada.reshape(1, 3 * DM)

    loss = tot[8, 0]
    g_small = dict(
        c_ctx=g_c_ctx, b_ada=g_b_ada, norm_g=tot[0:1], sgu_norm_g=tot[1:2, :512], w_spatial=tot[16:80].reshape(512, 128),
        b_spatial=tot[2:3, :512].reshape(4, 128), q_norm_g=tot[3:4, :HDIM], k_norm_g=tot[3:4, HDIM:2 * HDIM],
        rpb=tot[4:8].reshape(-1)[:HEADS * 15 * 31].reshape(HEADS * 15, 31))
    shapes = dict(c_ctx=(DM,), w_ada=(1, DM, SHARD_ADA), b_ada=(1, 3 * DM), norm_g=(1, DM), w_in=(1, DM, SHARD_IN),
                  sgu_norm_g=(1, 512), w_spatial=(1, 4, 128, 128), b_spatial=(1, 4, 128), q_norm_g=(1, HDIM),
                  k_norm_g=(1, HDIM), rpb=(1, HEADS, 15, 31), w_out=(1, SHARD_OUT, DM))
    names = list(shapes)
    weights = dict(c_ctx=c_ctx, w_ada=w_ada, b_ada=b_ada, norm_g=norm_g, w_in=w_in, sgu_norm_g=sgu_norm_g,
                   w_spatial=w_spatial, b_spatial=b_spatial, q_norm_g=q_norm_g, k_norm_g=k_norm_g, rpb=rpb, w_out=w_out)
    m_in = dict(zip(names, (m_c_ctx, m_w_ada, m_b_ada, m_norm_g, m_w_in, m_sgu_norm_g, m_w_spatial, m_b_spatial,
                            m_q_norm_g, m_k_norm_g, m_rpb, m_w_out)))
    v_in = dict(zip(names, (v_c_ctx, v_w_ada, v_b_ada, v_norm_g, v_w_in, v_sgu_norm_g, v_w_spatial, v_b_spatial,
                            v_q_norm_g, v_k_norm_g, v_rpb, v_w_out)))
    grads = dict(g_small, w_ada=g_w_ada, w_in=g_w_in, w_out=g_w_out)
    upd = {}
    for n in ("w_ada", "w_in", "w_out"):
        g = grads[n]
        upd[n] = adamw_big(weights[n].reshape(g.shape), g, m_in[n].reshape(g.shape), v_in[n].reshape(g.shape),
                           "adamw_" + n)
    small = [n for n in names if n not in upd]
    res = adamw_small([(weights[n].reshape(grads[n].shape), grads[n], m_in[n].reshape(grads[n].shape),
                        v_in[n].reshape(grads[n].shape)) for n in small])
    upd.update(zip(small, res))
    out = [loss, part["grad_x"].reshape(1, SEQ, DM)]
    out += [grads[n].reshape(shapes[n]) for n in names]
    for slot in range(3):
        out += [upd[n][slot].reshape(shapes[n]) for n in names]
    return tuple(out)
```

```python
import functools

import jax
import jax.numpy as jnp
from jax import lax
from jax.experimental import pallas as pl
from jax.experimental.pallas import tpu as pltpu

F32, BF16 = jnp.float32, jnp.bfloat16
SEQ, DM, CTX, DIN = 4096, 1024, 256, 3584
NCHIP, NDEV = 4, 8
SHARD_IN = DIN // NCHIP
SHARD_ADA = 3 * DM // NCHIP
SHARD_OUT = DM // NCHIP
GRID_W = 64
QROWS = 4
KROWS = 12
QBLK, KBLK = QROWS * GRID_W, KROWS * GRID_W
NQBLK = SEQ // QBLK
HEADS, HDIM, NPAIR = 8, 64, 4
EPS = 1e-6
NEG_INF = -1e30
ZQ, ZK, ZV, ZG = 12, 16, 20, 24
LR, B1, B2, ADAM_EPS, WD, STEP = 0.001, 0.9, 0.999, 1e-08, 0.01, 10
VMEM_BIG = 56 * 1024 * 1024
MESH_ID = pl.DeviceIdType.MESH


def _dot(a, b, lhs_c, rhs_c):
    return lax.dot_general(a.astype(BF16), b.astype(BF16), (((lhs_c,), (rhs_c,)), ((), ())),
                           preferred_element_type=F32)


@jax.custom_vjp
def mm(a, b):
    return _dot(a, b, 1, 0)


@jax.custom_vjp
def mm_nt(a, b):
    return _dot(a, b, 1, 1)


@jax.custom_vjp
def mm_tn(a, b):
    return _dot(a, b, 0, 0)


mm.defvjp(lambda a, b: (mm(a, b), (a, b)), lambda r, ct: (mm_nt(ct, r[1]), mm_tn(r[0], ct)))
mm_nt.defvjp(lambda a, b: (mm_nt(a, b), (a, b)), lambda r, ct: (mm(ct, r[1]), mm_tn(ct, r[0])))
mm_tn.defvjp(lambda a, b: (mm_tn(a, b), (a, b)), lambda r, ct: (mm_nt(r[1], ct), mm(r[0], ct)))


def _rms(x, g):
    return x * lax.rsqrt(jnp.mean(x * x, axis=-1, keepdims=True) + EPS) * g


def _modulated(x, g, scale, shift):
    return _rms(x, g) * (1.0 + scale) + shift


def _pair_rms(x, g2):
    lo = lax.broadcasted_iota(jnp.int32, (1, 2 * HDIM), 1) < HDIM
    sq = x * x
    s_lo = jnp.sum(jnp.where(lo, sq, 0.0), axis=-1, keepdims=True)
    s_hi = jnp.sum(jnp.where(lo, 0.0, sq), axis=-1, keepdims=True)
    rs = jnp.where(lo, lax.rsqrt(s_lo / HDIM + EPS), lax.rsqrt(s_hi / HDIM + EPS))
    return x * rs * g2


def _cparams(sem, vmem=None):
    return pltpu.CompilerParams(dimension_semantics=sem, vmem_limit_bytes=vmem)


def _row(n):
    return pl.BlockSpec((1, n), lambda *_: (0, 0))


CS_ROWS = 8 * NDEV + 8


def _mod_part(mod_ref, row, part):
    pieces = []
    for j in range(NCHIP):
        lo, hi = max(part * DM, j * SHARD_ADA), min((part + 1) * DM, (j + 1) * SHARD_ADA)
        if lo < hi:
            pieces.append(mod_ref[j, row, lo - j * SHARD_ADA:hi - j * SHARD_ADA])
    return jnp.concatenate(pieces, axis=-1)


def inproj_fwd(chip, x, c_vec, c_ctx, w_ada, b_shard, norm_g, w_shard, wo_shard):
    tl = 1024
    nt = SEQ // tl
    halves = (DM // 2, SHARD_OUT // 2)
    n_w, n_c = 12, NDEV - 1

    def kern(k_ref, x_ref, cv_ref, cc_ref, wa_ref, b_ref, g_ref, w_ref, wo_ref,
             z_ref, h_ref, wfull_ref, wofull_ref, modall_ref, csall_ref,
             w_scr, wo_scr, h_scr, mine, cs_scr, mod_scr, shsc_scr, send_sems, recv_sems, out_sems):
        s, t = pl.program_id(0), pl.program_id(1)
        xi, yi, c = _me()
        k, me = 2 * xi + yi, 4 * xi + 2 * yi + c
        sib = _flip(1)
        rows = pl.ds(pl.multiple_of(t * tl, tl), tl)
        gathered = (w_scr, wo_scr)
        slot = lambda d: pl.ds(pl.multiple_of(8 * d, 8), 8)

        def c_copy(q, owner):
            return _rcopy(mine, cs_scr.at[slot(owner), :], send_sems, recv_sems, n_w + q - 1, _flip(q))

        def m_copy(q, chip_of_block):
            return _rcopy(mod_scr.at[chip_of_block], mod_scr.at[chip_of_block], send_sems, recv_sems,
                          n_w + n_c + q // 2 - 1, _flip(q))

        def m_cross(q):
            return pltpu.make_async_remote_copy(
                src_ref=mod_scr.at[k], dst_ref=mod_scr.at[k], send_sem=send_sems.at[n_w + n_c + 3 + q // 2 - 1],
                recv_sem=recv_sems.at[n_w + n_c + q // 2 - 1], device_id=_flip(q | 1), device_id_type=MESH_ID)

        def start_weights():
            for q in (2, 4, 6):
                ici(0, q, k).start()
                ici(1, q, k).start()

        def adaln():
            first = lax.broadcasted_iota(jnp.int32, (8, DM), 0) == 0
            mine[...] = jnp.where(first, jnp.broadcast_to(cv_ref[...], (8, DM)), 0.0)
            cs_scr[slot(me), :] = mine[...]
            cs_scr[slot(NDEV), :] = jnp.where(first, jnp.broadcast_to(cc_ref[...], (8, DM)), 0.0)
            for q in range(1, NDEV):
                c_copy(q, me).start()
            w_scr[k] = w_ref[...].astype(BF16)
            wo_scr[k] = wo_ref[...].astype(BF16)

            @pl.when(c == 0)
            def _():
                start_weights()

            wa = wa_ref[...].astype(BF16)
            for q in range(1, NDEV):
                px, py, pc = _flip(q)
                c_copy(q, 4 * px + 2 * py + pc).wait_recv()
            act = jax.nn.silu(cs_scr[...]).astype(BF16)
            mod_scr[k] = jnp.dot(act, wa, preferred_element_type=F32) + b_ref[...]

            @pl.when(c == 1)
            def _():
                for q in (2, 4, 6):
                    m_copy(q, k).start()
                    m_cross(q).start()
                start_weights()

            for q in (2, 4, 6):
                m_copy(q, _chip_of(_flip(q))).wait_recv()
            row = pl.ds(8 * me, 1)
            shsc_scr[0:1, :] = _mod_part(mod_scr, row, 0)
            shsc_scr[1:2, :] = _mod_part(mod_scr, row, 1)
            pltpu.sync_copy(mod_scr, modall_ref)
            pltpu.sync_copy(cs_scr, csall_ref)

        def block(n, chip_of_block, hh):
            return gathered[n].at[chip_of_block, pl.ds(pl.multiple_of(hh * halves[n], halves[n]), halves[n]), :]

        def ici(n, q, chip_of_block):
            blk = block(n, chip_of_block, c)
            return _rcopy(blk, blk, send_sems, recv_sems, 6 * n + q // 2 - 1, _flip(q))

        def d2d(n, q, chip_of_block, hh):
            blk = block(n, chip_of_block, hh)
            return _rcopy(blk, blk, send_sems, recv_sems, 6 * n + 3 + q // 2 - 1, sib)

        @pl.when((s == 0) & (t == 0))
        def _():
            adaln()

        for sweep in (1, 2, 3):
            @pl.when((s == sweep) & (t == 0))
            def _():
                q = 2 * sweep
                src = _chip_of(_flip(q))
                for n in (0, 1):
                    ici(n, q, src).wait_recv()
                    d2d(n, q, src, c).start()
                for n in (0, 1):
                    d2d(n, q, src, 1 - c).wait_recv()

        @pl.when(s == 0)
        def _():
            hb = _modulated(x_ref[...], g_ref[...], shsc_scr[1:2, :], shsc_scr[0:1, :]).astype(BF16)
            h_scr[rows, :] = hb
            h_ref[...] = hb

        z_ref[...] = jnp.dot(h_scr[rows, :], w_scr[lax.bitwise_xor(k, s)], preferred_element_type=F32)

        @pl.when((s == NCHIP - 1) & (t == nt - 1))
        def _():
            for q in range(1, NDEV):
                c_copy(q, me).wait_send()
            @pl.when(c == 1)
            def _():
                for q in (2, 4, 6):
                    m_copy(q, k).wait_send()
                    m_cross(q).wait_send()

            for n in (0, 1):
                for q in (2, 4, 6):
                    ici(n, q, k).wait_send()
                    d2d(n, q, _chip_of(_flip(q)), c).wait_send()
            outs = [pltpu.make_async_copy(w_scr.at[j], wfull_ref.at[:, j * SHARD_IN:(j + 1) * SHARD_IN], out_sems.at[j])
                    for j in range(NCHIP)] + [pltpu.make_async_copy(wo_scr, wofull_ref, out_sems.at[NCHIP])]
            for cp in outs:
                cp.start()
            for cp in outs:
                cp.wait()

    once = lambda s, t, k: (jnp.where(s == 0, t, nt - 1), 0)
    hbm = pl.BlockSpec(memory_space=pl.ANY)
    n_sem = n_w + n_c + 6
    return pl.pallas_call(
        kern, name="inproj_fwd",
        grid_spec=pltpu.PrefetchScalarGridSpec(
            num_scalar_prefetch=1, grid=(NCHIP, nt),
            in_specs=[pl.BlockSpec((tl, DM), once)] + [_VMEM_SPEC] * 7,
            out_specs=[pl.BlockSpec((tl, SHARD_IN), lambda s, t, k: (t, lax.bitwise_xor(k[0], s))),
                       pl.BlockSpec((tl, DM), once), hbm, hbm, hbm, hbm],
            scratch_shapes=[pltpu.VMEM((NCHIP, DM, SHARD_IN), BF16), pltpu.VMEM((NCHIP, SHARD_OUT, DM), BF16),
                            pltpu.VMEM((SEQ, DM), BF16), pltpu.VMEM((8, DM), F32), pltpu.VMEM((CS_ROWS, DM), F32),
                            pltpu.VMEM((NCHIP, CS_ROWS, SHARD_ADA), F32), pltpu.VMEM((8, DM), F32),
                            pltpu.SemaphoreType.DMA((n_sem,)), pltpu.SemaphoreType.DMA((n_sem,)),
                            pltpu.SemaphoreType.DMA((NCHIP + 1,))]),
        out_shape=[jax.ShapeDtypeStruct((SEQ, DIN), F32), jax.ShapeDtypeStruct((SEQ, DM), BF16),
                   jax.ShapeDtypeStruct((DM, DIN), BF16), jax.ShapeDtypeStruct((NCHIP, SHARD_OUT, DM), BF16),
                   jax.ShapeDtypeStruct((NCHIP, CS_ROWS, SHARD_ADA), F32), jax.ShapeDtypeStruct((CS_ROWS, DM), F32)],
        compiler_params=_cparams(("arbitrary", "arbitrary"), VMEM_BIG),
    )(chip, x, c_vec, c_ctx, w_ada, b_shard, norm_g, w_shard, wo_shard)


def ctx_fwd(ctx, cshift, cscale, norm_g, w_full):
    def kern(c_ref, sh_ref, sc_ref, g_ref, w_ref, zc_ref, hc_ref):
        hc = _modulated(c_ref[...], g_ref[...], sc_ref[...], sh_ref[...]).astype(BF16)
        hc_ref[...] = hc
        zc_ref[...] = jnp.dot(hc, w_ref[...], preferred_element_type=F32)

    return pl.pallas_call(
        kern, name="ctx_fwd", grid=(1,),
        in_specs=[pl.BlockSpec((CTX, DM), lambda i: (0, 0)), _row(DM), _row(DM), _row(DM),
                  pl.BlockSpec((DM, 2 * SHARD_IN), lambda i: (0, 1))],
        out_specs=[pl.BlockSpec((CTX, 2 * SHARD_IN), lambda i: (0, 0)),
                   pl.BlockSpec((CTX, DM), lambda i: (0, 0))],
        out_shape=[jax.ShapeDtypeStruct((CTX, 2 * SHARD_IN), F32), jax.ShapeDtypeStruct((CTX, DM), BF16)],
        compiler_params=_cparams(("arbitrary",)),
    )(ctx, cshift, cscale, norm_g, w_full)


SGU_CHUNK, SGU_PER_STEP = 128, 4


def _gelu(x):
    return 0.5 * x * (1.0 + lax.erf(x * 0.7071067811865476))


def _sgu_chunk(au, av, ag, sg, ws, bsb):
    u, v = _gelu(au), _gelu(av)
    outs = []
    for g in range(4):
        sl = slice(128 * g, 128 * (g + 1))
        mixed = mm(ws[g], _rms(v[:, sl], sg[:, sl])) + bsb[g]
        outs.append(u[:, sl] * mixed * jax.nn.silu(ag[:, sl]))
    return jnp.concatenate(outs, axis=-1)


def _sgu_specs():
    rows = SGU_CHUNK * SGU_PER_STEP
    zspec = lambda c: pl.BlockSpec((rows, 512), lambda n: (n, c))
    wspec = pl.BlockSpec((4, 128, 128), lambda n: (0, 0, 0))
    return rows, [zspec(0), zspec(1), zspec(2), _row(512), wspec, wspec]


def sgu_fwd(z, sg, ws, bsb):
    rows, in_specs = _sgu_specs()

    def kern(au_ref, av_ref, ag_ref, sg_ref, ws_ref, bs_ref, o_ref):
        for c in range(SGU_PER_STEP):
            sl = slice(c * SGU_CHUNK, (c + 1) * SGU_CHUNK)
            o_ref[sl, :] = _sgu_chunk(au_ref[sl, :], av_ref[sl, :], ag_ref[sl, :], sg_ref[...], ws_ref[...],
                                      bs_ref[...])

    return pl.pallas_call(
        kern, name="sgu_fwd", grid=(SEQ // rows,), in_specs=in_specs,
        out_specs=pl.BlockSpec((rows, 512), lambda n: (n, 0)),
        out_shape=jax.ShapeDtypeStruct((SEQ, 512), F32),
        compiler_params=_cparams(("arbitrary",)),
    )(z, z, z, sg, ws, bsb)


def sgu_bwd(z, sg, ws, bsb, dcat):
    rows, in_specs = _sgu_specs()

    def kern(au_ref, av_ref, ag_ref, sg_ref, ws_ref, bs_ref, do_ref, dz_ref, dsg_ref, dws_ref, dbs_ref):
        @pl.when(pl.program_id(0) == 0)
        def _():
            dsg_ref[...] = jnp.zeros_like(dsg_ref)
            dws_ref[...] = jnp.zeros_like(dws_ref)
            dbs_ref[...] = jnp.zeros_like(dbs_ref)

        for c in range(SGU_PER_STEP):
            sl = slice(c * SGU_CHUNK, (c + 1) * SGU_CHUNK)
            _, vjp = jax.vjp(_sgu_chunk, au_ref[sl, :], av_ref[sl, :], ag_ref[sl, :], sg_ref[...], ws_ref[...],
                             bs_ref[...])
            dau, dav, dag, dsg, dws, dbs = vjp(do_ref[sl, :])
            dz_ref[sl, 0:512] = dau.astype(BF16)
            dz_ref[sl, 512:1024] = dav.astype(BF16)
            dz_ref[sl, 1024:1536] = dag.astype(BF16)
            dsg_ref[...] += dsg
            dws_ref[...] += dws
            dbs_ref[...] += dbs

        @pl.when(pl.program_id(0) == pl.num_programs(0) - 1)
        def _():
            dbs_ref[...] = jnp.broadcast_to(jnp.sum(dbs_ref[...], axis=-1, keepdims=True), dbs_ref.shape)

    wspec = pl.BlockSpec((4, 128, 128), lambda n: (0, 0, 0))
    return pl.pallas_call(
        kern, name="sgu_bwd", grid=(SEQ // rows,),
        in_specs=in_specs + [pl.BlockSpec((rows, 512), lambda n: (n, 0))],
        out_specs=[pl.BlockSpec((rows, 1536), lambda n: (n, 0)), _row(512), wspec, wspec],
        out_shape=[jax.ShapeDtypeStruct((SEQ, 1536), BF16), jax.ShapeDtypeStruct((1, 512), F32),
                   jax.ShapeDtypeStruct((4, 128, 128), F32), jax.ShapeDtypeStruct((4, 128, 128), F32)],
        compiler_params=_cparams(("arbitrary",)),
    )(z, z, z, sg, ws, bsb, dcat)


_DR_OFF = (7, 3, -1)


def _row_valid(v, rr, j):
    return (j < 8, rr <= j < rr + 8, 4 <= j < 12)[v]


def _col_window():
    q = lax.broadcasted_iota(jnp.int32, (GRID_W, 128), 0)
    kc = lax.broadcasted_iota(jnp.int32, (GRID_W, 128), 1) % GRID_W
    c0 = jnp.clip(q - 8, 0, GRID_W - 16)
    return (kc >= c0) & (kc < c0 + 16)


def rpb_tables(rpb2):
    def kern(r_ref, b_ref):
        base = r_ref[0]
        lo = lax.broadcasted_iota(jnp.int32, (1, 128), 1) < GRID_W
        win = _col_window()
        tiles = {}
        for v in range(3):
            for rr in range(QROWS):
                for jp in range(KROWS // 2):
                    j0, j1 = 2 * jp, 2 * jp + 1
                    ok0, ok1 = _row_valid(v, rr, j0), _row_valid(v, rr, j1)
                    key = (j0 - rr + _DR_OFF[v], ok0, ok1) if (ok0 or ok1) else None
                    if key not in tiles:
                        if key is None:
                            tiles[key] = jnp.full((GRID_W, 128), NEG_INF, F32)
                        else:
                            d0 = key[0]
                            r0 = base[d0:d0 + 1, :] if ok0 else jnp.zeros((1, 128), F32)
                            r1 = base[d0 + 1:d0 + 2, :] if ok1 else jnp.zeros((1, 128), F32)
                            y = jnp.broadcast_to(jnp.where(lo, r0, r1), (GRID_W, 128))
                            y = pltpu.roll(pltpu.roll(y, 128 - 15, 1), 0, 1, stride=1, stride_axis=0)
                            tiles[key] = jnp.where(win & jnp.where(lo, ok0, ok1), y, NEG_INF)
                    b_ref[v, 0, rr * GRID_W:(rr + 1) * GRID_W, jp * 128:(jp + 1) * 128] = tiles[key]

    return pl.pallas_call(
        kern, name="rpb_tables", grid=(HEADS,),
        in_specs=[pl.BlockSpec((1, 15, 128), lambda h: (h, 0, 0))],
        out_specs=pl.BlockSpec((3, 1, QBLK, KBLK), lambda h: (0, h, 0, 0)),
        out_shape=jax.ShapeDtypeStruct((3, HEADS, QBLK, KBLK), F32),
        compiler_params=_cparams(("arbitrary",)),
    )(rpb2)


def rpb_bwd(dbias):
    def kern(g0_ref, g1_ref, g2_ref, o_ref):
        g_refs = (g0_ref.at[0], g1_ref.at[0], g2_ref.at[0])
        lo = lax.broadcasted_iota(jnp.int32, (1, 128), 1) < GRID_W
        ri = lax.broadcasted_iota(jnp.int32, (GRID_W, GRID_W), 0)
        ci = lax.broadcasted_iota(jnp.int32, (GRID_W, GRID_W), 1)
        flip = (ri + ci == GRID_W - 1).astype(F32)
        groups = {}
        for v in range(3):
            for rr in range(QROWS):
                for jp in range(KROWS // 2):
                    j0, j1 = 2 * jp, 2 * jp + 1
                    ok0, ok1 = _row_valid(v, rr, j0), _row_valid(v, rr, j1)
                    if not (ok0 or ok1):
                        continue
                    g = g_refs[v][0, rr * GRID_W:(rr + 1) * GRID_W, jp * 128:(jp + 1) * 128]
                    key = (j0 - rr + _DR_OFF[v], ok0, ok1)
                    groups[key] = g if key not in groups else groups[key] + g
        acc = [jnp.zeros((1, 128), F32) for _ in range(15)]
        for (d0, ok0, ok1), g in groups.items():
            g = lax.dot_general(flip, g, (((1,), (0,)), ((), ())), precision=lax.Precision.HIGHEST,
                                preferred_element_type=F32)
            g = pltpu.roll(pltpu.roll(g, 128 - 48, 1), 0, 1, stride=1, stride_axis=0)
            s = jnp.sum(g, axis=0, keepdims=True)
            if ok0:
                acc[d0] = acc[d0] + jnp.where(lo, s, 0.0)
            if ok1:
                acc[d0 + 1] = acc[d0 + 1] + jnp.where(lo, 0.0, s)
        for d in range(15):
            o_ref[0, d:d + 1, :] = acc[d] + pltpu.roll(acc[d], GRID_W, 1)

    return pl.pallas_call(
        kern, name="rpb_bwd", grid=(HEADS,),
        in_specs=[pl.BlockSpec((1, 1, QBLK, KBLK), functools.partial(lambda v, h: (v, h, 0, 0), v)) for v in range(3)],
        out_specs=pl.BlockSpec((1, 15, 128), lambda h: (h, 0, 0)),
        out_shape=jax.ShapeDtypeStruct((HEADS, 15, 128), F32),
        compiler_params=_cparams(("arbitrary",)),
    )(dbias, dbias, dbias)


def _scaled_q(q_raw, qg):
    return _pair_rms(q_raw, qg) * (HDIM ** -0.5)


def _head_lanes():
    lo = lax.broadcasted_iota(jnp.int32, (1, 2 * HDIM), 1) < HDIM
    return lo, jnp.logical_not(lo)


SOFTMAX_ROWS = 32


def _emit_interleaved(vector_work, matmul_work):
    for j in range(max(len(vector_work), len(matmul_work))):
        for work in (vector_work, matmul_work):
            if j < len(work):
                work[j]()


def _kblock(i):
    return jnp.clip(i - 1, 0, (SEQ - KBLK) // QBLK)


def _kstart(i):
    return pl.multiple_of(_kblock(i) * QBLK, QBLK)


ATTN_STEPS = NQBLK // 2
ATTN_ROWS = 2 * QBLK
KCOLS = QBLK


def _attn_in_specs():
    return [
        pl.BlockSpec((ATTN_ROWS, 128), lambda p, i: (i, ZQ + p)),
        pl.BlockSpec((SEQ, 128), lambda p, i: (0, ZK + p)),
        pl.BlockSpec((SEQ, 128), lambda p, i: (0, ZV + p)),
        pl.BlockSpec((ATTN_ROWS, 128), lambda p, i: (i, ZG + p)),
        pl.BlockSpec((CTX, 128), lambda p, i: (0, 2 + p)),
        pl.BlockSpec((CTX, 128), lambda p, i: (0, 6 + p)),
    ]


def _bias_specs():
    bias_spec = lambda variant: pl.BlockSpec((1, 2, QBLK, KBLK), lambda p, i: (variant(i), p, 0, 0))
    return [bias_spec(lambda i: jnp.where(i == 0, 0, 1)),
            bias_spec(lambda i: jnp.where(i == ATTN_STEPS - 1, 2, 1))]


def _prob_specs():
    return [pl.BlockSpec((2, ATTN_ROWS, KBLK), lambda p, i: (p, i, 0)),
            pl.BlockSpec((2, ATTN_ROWS, CTX), lambda p, i: (p, i, 0))]


NORM_ROWS = 512


def _half_sums(x):
    lo = lax.broadcasted_iota(jnp.int32, (1, 2 * HDIM), 1) < HDIM
    return jnp.where(lo, jnp.sum(jnp.where(lo, x, 0.0), axis=-1, keepdims=True),
                     jnp.sum(jnp.where(lo, 0.0, x), axis=-1, keepdims=True))


def _pair_rms_bwd(x, g2, ct):
    rs = lax.rsqrt(_half_sums(x * x) / HDIM + EPS)
    y = x * rs
    dy = ct * g2
    return rs * (dy - y * (_half_sums(dy * y) / HDIM)), jnp.sum(ct * y, axis=0, keepdims=True)


def _norm_keys(k_ref, ck_ref, kg_ref, kn_scr, ckn_scr):
    def body(c, carry):
        sl = pl.ds(pl.multiple_of(c * NORM_ROWS, NORM_ROWS), NORM_ROWS)
        kn_scr[sl, :] = _pair_rms(k_ref[sl, :], kg_ref[...]).astype(BF16)
        return carry

    lax.fori_loop(0, SEQ // NORM_ROWS, body, 0)
    ckn_scr[...] = _pair_rms(ck_ref[...], kg_ref[...]).astype(BF16)


def _values_with_ones(v_ref, cv_ref, v1_scr, cv1_scr):
    for a, mine in enumerate(_head_lanes()):
        def body(c, carry):
            sl = pl.ds(pl.multiple_of(c * NORM_ROWS, NORM_ROWS), NORM_ROWS)
            v1_scr[a, sl, :] = jnp.where(mine, v_ref[sl, :], 1.0).astype(BF16)
            return carry

        lax.fori_loop(0, SEQ // NORM_ROWS, body, 0)
        cv1_scr[a] = jnp.where(mine, cv_ref[...], 1.0).astype(BF16)


def _pair_major_spec():
    return pl.BlockSpec((1, ATTN_ROWS, 128), lambda p, i: (p, i, 0))


def _normed_key_specs():
    return [pl.BlockSpec((None, SEQ, 128), lambda p, i: (p, 0, 0)), pl.BlockSpec((None, CTX, 128), lambda p, i: (p, 0, 0))]


def attn_fwd(z, zc, bias, qg2, kg2):
    def kern(q_ref, k_ref, v_ref, bg_ref, ck_ref, cv_ref, be_ref, bo_ref, qg_ref, kg_ref,
             ob_ref, o_ref, rden_ref, pl_ref, pc_ref, kn_ref, ckn_ref, kn_scr, ckn_scr, v1_scr, cv1_scr, s_scr):
        i = pl.program_id(1)

        @pl.when(i == 0)
        def _():
            _norm_keys(k_ref, ck_ref, kg_ref, kn_scr, ckn_scr)
            kn_ref[...] = kn_scr[...]
            ckn_ref[...] = ckn_scr[...]
            _values_with_ones(v_ref, cv_ref, v1_scr, cv1_scr)

        heads = _head_lanes()
        bias_refs = (be_ref, bo_ref)
        tiles = [(b, a) for b in range(2) for a in range(2)]
        rows = [slice(b * QBLK, (b + 1) * QBLK) for b in range(2)]
        qn = [_scaled_q(q_ref[rows[b], :], qg_ref[...]) for b in range(2)]
        qa = [jnp.where(heads[a], qn[b], 0.0).astype(BF16) for b, a in tiles]
        pv = [None] * len(tiles)
        done = {}
        latent = KBLK // KCOLS

        def keys(b, n):
            return pl.ds(pl.multiple_of(_kstart(2 * i + b) + n * KCOLS, KCOLS), KCOLS)

        def score_piece(t, n):
            b, a = tiles[t]
            cols = slice(n * KCOLS, (n + 1) * KCOLS)
            if n < latent:
                s_scr[t, :, cols] = mm_nt(qa[t], kn_scr[keys(b, n), :]) + bias_refs[b][0, a, :, cols]
            else:
                s_scr[t, :, cols] = mm_nt(qa[t], ckn_scr[...])

        def softmax_rows(t, r):
            b, a = tiles[t]
            rs = slice(r * SOFTMAX_ROWS, (r + 1) * SOFTMAX_ROWS)
            out_rows = slice(b * QBLK + rs.start, b * QBLK + rs.stop)
            s = s_scr[t, rs, :]
            p = jnp.exp(s - jnp.max(s, axis=-1, keepdims=True)).astype(BF16)
            pl_ref[a, out_rows, :] = p[:, :KBLK]
            pc_ref[a, out_rows, :] = p[:, KBLK:]

        def value_piece(t, n):
            b, a = tiles[t]
            if n < latent:
                part = mm(pl_ref[a, rows[b], n * KCOLS:(n + 1) * KCOLS], v1_scr[a, keys(b, n), :])
            else:
                part = mm(pc_ref[a, rows[b], :], cv1_scr[a])
            pv[t] = part if pv[t] is None else pv[t] + part
            if n == latent:
                finish(t)

        def finish(t):
            b, a = tiles[t]
            r = jnp.where(heads[a], pltpu.roll(1.0 / pv[t], HDIM, 1), 0.0)
            done[t] = (pv[t] * r, r)
            if a == 1:
                o, rden = (lo + hi for lo, hi in zip(done[t - 1], done[t]))
                ob_ref[rows[b], :] = o * jax.nn.silu(bg_ref[rows[b], :])
                o_ref[0, rows[b], :] = o
                rden_ref[0, rows[b], :] = rden

        pieces = range(latent + 1)
        for n in pieces:
            score_piece(0, n)
        for t in range(len(tiles)):
            matmuls = []
            for n in pieces:
                if t + 1 < len(tiles):
                    matmuls.append(functools.partial(score_piece, t + 1, n))
                if t > 0:
                    matmuls.append(functools.partial(value_piece, t - 1, n))
            _emit_interleaved([functools.partial(softmax_rows, t, r) for r in range(QBLK // SOFTMAX_ROWS)], matmuls)
        for n in pieces:
            value_piece(len(tiles) - 1, n)

    qblk = pl.BlockSpec((ATTN_ROWS, 128), lambda p, i: (i, p))
    return pl.pallas_call(
        kern, name="attn_fwd", grid=(NPAIR, ATTN_STEPS),
        in_specs=_attn_in_specs() + _bias_specs() + [_row(128), _row(128)],
        out_specs=[qblk, _pair_major_spec(), _pair_major_spec()] + _prob_specs() + _normed_key_specs(),
        out_shape=[jax.ShapeDtypeStruct((SEQ, 512), F32)] + [jax.ShapeDtypeStruct((NPAIR, SEQ, 128), F32)] * 2
        + [jax.ShapeDtypeStruct((HEADS, SEQ, KBLK), BF16), jax.ShapeDtypeStruct((HEADS, SEQ, CTX), BF16),
           jax.ShapeDtypeStruct((NPAIR, SEQ, 128), BF16), jax.ShapeDtypeStruct((NPAIR, CTX, 128), BF16)],
        scratch_shapes=[pltpu.VMEM((SEQ, 128), BF16), pltpu.VMEM((CTX, 128), BF16),
                        pltpu.VMEM((2, SEQ, 128), BF16), pltpu.VMEM((2, CTX, 128), BF16),
                        pltpu.VMEM((4, QBLK, KBLK + CTX), F32)],
        compiler_params=_cparams(("arbitrary", "arbitrary"), 40 * 1024 * 1024),
    )(z, z, z, z, zc, zc, bias, bias, qg2, kg2)


def attn_bwd(z, zc, qg2, kg2, dcat, saved):
    def kern(q_ref, k_ref, v_ref, bg_ref, ck_ref, cv_ref, qg_ref, kg_ref, do_ref, o_ref, rden_ref, pl_ref, pc_ref,
             kn_scr, ckn_scr, dq_ref, dk_ref, dv_ref, dbg_ref, dck_ref, dcv_ref, db_ref, dqg_ref, dkg_ref,
             v_scr, cv_scr, dknt_scr, dvt_scr, dcknt_scr, dcvt_scr, dp_scr, ds_scr):
        p, i = pl.program_id(0), pl.program_id(1)
        last = i == ATTN_STEPS - 1

        @pl.when(i == 0)
        def _():
            def body(c, carry):
                sl = pl.ds(pl.multiple_of(c * NORM_ROWS, NORM_ROWS), NORM_ROWS)
                v_scr[sl, :] = v_ref[sl, :].astype(BF16)
                return carry

            lax.fori_loop(0, SEQ // NORM_ROWS, body, 0)
            cv_scr[...] = cv_ref[...].astype(BF16)
            for acc in (dknt_scr, dvt_scr, dcknt_scr, dcvt_scr, db_ref):
                acc[...] = jnp.zeros_like(acc)

        @pl.when((i == 0) & (p == 0))
        def _():
            dqg_ref[...] = jnp.zeros_like(dqg_ref)
            dkg_ref[...] = jnp.zeros_like(dkg_ref)

        heads = _head_lanes()
        tiles = [(b, a) for b in range(2) for a in range(2)]
        rows = [slice(b * QBLK, (b + 1) * QBLK) for b in range(2)]
        kb = [_kblock(2 * i + b) for b in range(2)]
        variant = [jnp.where(i == 0, 0, 1), jnp.where(last, 2, 1)]
        latent = KBLK // KCOLS

        def keys(b, n):
            return pl.ds(pl.multiple_of((kb[b] + n) * KCOLS, KCOLS), KCOLS)

        gated = []
        for b in range(2):
            bg, dout, o = bg_ref[rows[b], :], do_ref[rows[b], :], o_ref[0, rows[b], :]
            sig = jax.nn.sigmoid(bg)
            do = dout * (bg * sig)
            dbg_ref[rows[b], :] = (dout * o * (sig * (1.0 + bg * (1.0 - sig)))).astype(BF16)
            rden = rden_ref[0, rows[b], :]
            dr = do * rden
            qn = _scaled_q(q_ref[rows[b], :], qg_ref[...])
            gated.append((dr, dr.T.astype(BF16), qn.T.astype(BF16), do * o * rden))

        feats = [slice(a * HDIM, (a + 1) * HDIM) for a in range(2)]
        doa, doa_t, qa_t, delta = [], [], [], []
        for b, a in tiles:
            dr, dr_t, qn_t, weighted = gated[b]
            doa.append(jnp.where(heads[a], dr, 0.0).astype(BF16))
            doa_t.append(dr_t[feats[a], :])
            qa_t.append(qn_t[feats[a], :])
            delta.append(jnp.sum(jnp.where(heads[a], weighted, 0.0), axis=-1, keepdims=True))
        dqn = [None] * len(tiles)

        def cols(n):
            return slice(n * KCOLS, (n + 1) * KCOLS)

        def stage_a(t, n):
            b, a = tiles[t]
            if n < latent:
                dp_scr[t, :, cols(n)] = mm_nt(doa[t], v_scr[keys(b, n), :])
                dvt_scr[kb[b] + n, feats[a], :] += mm(doa_t[t], pl_ref[a, rows[b], cols(n)])
            else:
                dp_scr[t, :, cols(n)] = mm_nt(doa[t], cv_scr[...])
                dcvt_scr[feats[a], :] += mm(doa_t[t], pc_ref[a, rows[b], :])

        def stage_b(t, r):
            b, a = tiles[t]
            rs = slice(r * SOFTMAX_ROWS, (r + 1) * SOFTMAX_ROWS)
            in_rows = slice(b * QBLK + rs.start, b * QBLK + rs.stop)
            d = dp_scr[t, rs, :] - delta[t][rs, :]
            ds_lat = pl_ref[a, in_rows, :].astype(F32) * d[:, :KBLK]
            ds_ctx = pc_ref[a, in_rows, :].astype(F32) * d[:, KBLK:]
            db_ref[variant[b], a, rs, :] += ds_lat
            ds_scr[t, rs, :KBLK] = ds_lat.astype(BF16)
            ds_scr[t, rs, KBLK:] = ds_ctx.astype(BF16)

        def stage_c(t, n):
            b, a = tiles[t]
            ds = ds_scr[t, :, cols(n)]
            if n < latent:
                part = mm(ds, kn_scr[keys(b, n), :])
                dknt_scr[kb[b] + n, feats[a], :] += mm(qa_t[t], ds)
            else:
                part = mm(ds, ckn_scr[...])
                dcknt_scr[feats[a], :] += mm(qa_t[t], ds)
            dqn[t] = part if dqn[t] is None else dqn[t] + part
            if n == latent and a == 1:
                both = jnp.where(heads[0], dqn[t - 1], 0.0) + jnp.where(heads[1], dqn[t], 0.0)
                dq, dqg = jax.vjp(_scaled_q, q_ref[rows[b], :], qg_ref[...])[1](both)
                dq_ref[rows[b], :] = dq.astype(BF16)
                dqg_ref[...] += dqg

        pieces = range(latent + 1)
        for n in pieces:
            stage_a(0, n)
        for t in range(len(tiles)):
            matmuls = []
            for n in pieces:
                if t + 1 < len(tiles):
                    matmuls.append(functools.partial(stage_a, t + 1, n))
                if t > 0:
                    matmuls.append(functools.partial(stage_c, t - 1, n))
            _emit_interleaved([functools.partial(stage_b, t, r) for r in range(QBLK // SOFTMAX_ROWS)], matmuls)
        for n in pieces:
            stage_c(len(tiles) - 1, n)

        @pl.when(last)
        def _():
            eye = (lax.broadcasted_iota(jnp.int32, (KCOLS, KCOLS), 0)
                   == lax.broadcasted_iota(jnp.int32, (KCOLS, KCOLS), 1)).astype(BF16)

            def turned(x):
                hi = x.astype(BF16)
                return mm_nt(eye, hi) + mm_nt(eye, x - hi.astype(F32))

            def body(c, dkg):
                sl = pl.ds(pl.multiple_of(c * NORM_ROWS, NORM_ROWS), NORM_ROWS)
                blocks = range(NORM_ROWS // KCOLS)
                dkn = jnp.concatenate([turned(dknt_scr[c * len(blocks) + n]) for n in blocks], axis=0)
                dv = jnp.concatenate([mm_nt(eye, dvt_scr[c * len(blocks) + n]) for n in blocks], axis=0)
                dk, dg = _pair_rms_bwd(k_ref[sl, :], kg_ref[...], dkn)
                dk_ref[sl, :] = dk.astype(BF16)
                dv_ref[sl, :] = dv.astype(BF16)
                return dkg + dg

            dkg = lax.fori_loop(0, SEQ // NORM_ROWS, body, jnp.zeros((1, 128), F32))
            dck, dg = _pair_rms_bwd(ck_ref[...], kg_ref[...], dcknt_scr[...].T)
            dck_ref[...] = dck
            dcv_ref[...] = dcvt_scr[...].T
            dkg_ref[...] += dkg + dg

        @pl.when(last & (p == NPAIR - 1))
        def _():
            dqg_ref[...] = dqg_ref[...] + pltpu.roll(dqg_ref[...], HDIM, 1)
            dkg_ref[...] = dkg_ref[...] + pltpu.roll(dkg_ref[...], HDIM, 1)

    blk = lambda rows: pl.BlockSpec((rows, 128), lambda p, i: (0, p))
    qblk = pl.BlockSpec((ATTN_ROWS, 128), lambda p, i: (i, p))
    return pl.pallas_call(
        kern, name="attn_bwd", grid=(NPAIR, ATTN_STEPS),
        in_specs=_attn_in_specs() + [_row(128), _row(128), pl.BlockSpec((ATTN_ROWS, 128), lambda p, i: (i, 4 + p)),
                                     _pair_major_spec(), _pair_major_spec()] + _prob_specs() + _normed_key_specs(),
        out_specs=[qblk, blk(SEQ), blk(SEQ), qblk, blk(CTX), blk(CTX),
                   pl.BlockSpec((3, 2, QBLK, KBLK), lambda p, i: (0, p, 0, 0)), _row(128), _row(128)],
        out_shape=[jax.ShapeDtypeStruct((SEQ, 512), BF16)] * 4 + [jax.ShapeDtypeStruct((CTX, 512), F32)] * 2
        + [jax.ShapeDtypeStruct((3, HEADS, QBLK, KBLK), F32)]
        + [jax.ShapeDtypeStruct((1, 128), F32), jax.ShapeDtypeStruct((1, 128), F32)],
        scratch_shapes=[pltpu.VMEM((SEQ, 128), BF16), pltpu.VMEM((CTX, 128), BF16),
                        pltpu.VMEM((SEQ // KCOLS, 128, KCOLS), F32), pltpu.VMEM((SEQ // KCOLS, 128, KCOLS), F32),
                        pltpu.VMEM((128, CTX), F32), pltpu.VMEM((128, CTX), F32),
                        pltpu.VMEM((4, QBLK, KBLK + CTX), F32), pltpu.VMEM((4, QBLK, KBLK + CTX), BF16)],
        compiler_params=_cparams(("arbitrary", "arbitrary"), VMEM_BIG),
    )(z, z, z, z, zc, zc, qg2, kg2, dcat, *saved)


def outproj(out_a, out_b, x, target, gate, wo):
    tl = 512

    def kern(a_ref, b_ref, x_ref, t_ref, g_ref, w_ref, loss_ref, dy_ref, dcat_ref, dg_ref, dw_ref):
        @pl.when(pl.program_id(0) == 0)
        def _():
            loss_ref[...] = jnp.zeros_like(loss_ref)
            dg_ref[...] = jnp.zeros_like(dg_ref)
            dw_ref[...] = jnp.zeros_like(dw_ref)

        a, b = a_ref[...].astype(BF16), b_ref[...].astype(BF16)
        mix = (jnp.dot(a, w_ref[0:512, :], preferred_element_type=F32)
               + jnp.dot(b, w_ref[512:1024, :], preferred_element_type=F32))
        err = x_ref[...] + g_ref[...] * mix - t_ref[...]
        loss_ref[...] += 0.5 * jnp.sum(jnp.mean(err * err, axis=-1))
        dy = err * (1.0 / DM)
        dy_ref[...] = dy
        dg_ref[...] += jnp.sum(dy * mix, axis=0, keepdims=True)
        dmix = (g_ref[...] * dy).astype(BF16)
        dcat_ref[...] = lax.dot_general(dmix, w_ref[...], (((1,), (1,)), ((), ())), preferred_element_type=F32)
        dw_ref[0:512, :] += lax.dot_general(a, dmix, (((0,), (0,)), ((), ())), preferred_element_type=F32)
        dw_ref[512:1024, :] += lax.dot_general(b, dmix, (((0,), (0,)), ((), ())), preferred_element_type=F32)

    tile = lambda w: pl.BlockSpec((tl, w), lambda t: (t, 0))
    whole = pl.BlockSpec((DM, DM), lambda t: (0, 0))
    return pl.pallas_call(
        kern, name="outproj", grid=(SEQ // tl,),
        in_specs=[tile(512), tile(512), tile(DM), tile(DM), _row(DM), whole],
        out_specs=[pl.BlockSpec((8, 128), lambda t: (0, 0)), tile(DM), tile(DM), _row(DM), whole],
        out_shape=[jax.ShapeDtypeStruct((8, 128), F32), jax.ShapeDtypeStruct((SEQ, DM), F32),
                   jax.ShapeDtypeStruct((SEQ, DM), F32), jax.ShapeDtypeStruct((1, DM), F32),
                   jax.ShapeDtypeStruct((DM, DM), F32)],
        compiler_params=_cparams(("arbitrary",), 48 * 1024 * 1024),
    )(out_a, out_b, x, target, gate, wo)


DZ_COLS = (("a", 0, 1536), ("q", 1536, 2048), ("k", 2048, 2560), ("v", 2560, 3072), ("g", 3072, DIN))
DZC_COLS = (("k", 2048, 2560), ("v", 2560, 3072))
_NT = (((1,), (1,)), ((), ()))


DH_SUBTILES = 2


def _dz_specs(tl):
    return [pl.BlockSpec((tl, 1536), lambda t: (t, 0))] + [pl.BlockSpec((tl, 512), lambda t: (t, 0))] * 4


def dh_bwd(dz_parts, w_full, x, dy, shift, scale, norm_g, dg_ctx):
    tl = 512
    nt = SEQ // tl

    def kern(a_ref, q_ref, k_ref, v_ref, g_ref, w_ref, x_ref, dy_ref, sh_ref, sc_ref, gn_ref, dgc_ref,
             gx_ref, dsh_ref, dsc_ref, dg_ref):
        @pl.when(pl.program_id(0) == 0)
        def _():
            dsh_ref[...] = jnp.zeros_like(dsh_ref)
            dsc_ref[...] = jnp.zeros_like(dsc_ref)
            dg_ref[...] = dgc_ref[...]

        src = dict(a=a_ref, q=q_ref, k=k_ref, v=v_ref, g=g_ref)
        for sub in range(DH_SUBTILES):
            rows = slice(sub * tl // DH_SUBTILES, (sub + 1) * tl // DH_SUBTILES)
            dh = None
            for name, c0, c1 in DZ_COLS:
                part = lax.dot_general(src[name][rows, :], w_ref[:, c0:c1], _NT, preferred_element_type=F32)
                dh = part if dh is None else dh + part
            _, vjp = jax.vjp(_modulated, x_ref[rows, :], gn_ref[...], sc_ref[...], sh_ref[...])
            dx, dg, dsc, dsh = vjp(dh)
            gx_ref[rows, :] = dy_ref[rows, :] + dx
            dg_ref[...] += dg
            dsc_ref[...] += dsc
            dsh_ref[...] += dsh

    tile = pl.BlockSpec((tl, DM), lambda t: (t, 0))
    return pl.pallas_call(
        kern, name="dh_bwd", grid=(nt,),
        in_specs=_dz_specs(tl) + [pl.BlockSpec((DM, DIN), lambda t: (0, 0)), tile, tile, _row(DM),
                                  _row(DM), _row(DM), _row(DM)],
        out_specs=[tile, _row(DM), _row(DM), _row(DM)],
        out_shape=[jax.ShapeDtypeStruct((SEQ, DM), F32)] + [jax.ShapeDtypeStruct((1, DM), F32)] * 3,
        compiler_params=_cparams(("arbitrary",), 48 * 1024 * 1024),
    )(*dz_parts, w_full, x, dy, shift, scale, norm_g, dg_ctx)


def dw_bwd(h, dz_parts, hc, dck, dcv, g_out):
    tl = 512
    nt = SEQ // tl
    (rhi, wi), (rho, wo) = RS_SHAPES

    def kern(h_ref, a_ref, q_ref, k_ref, v_ref, g_ref, hc_ref, dck_ref, dcv_ref, go_hbm,
             wire_i, keep_i, wire_o, keep_o, acc, snd_i, rcv_i, mine_o, rcv_o, load_sem, send_sems, recv_sems):
        t = pl.program_id(0)
        x, y, c = _me()
        k = 2 * x + y
        sib = _flip(1)
        half = lambda hh, rh: pl.ds(pl.multiple_of(hh * rh, rh), rh)
        load_o = pltpu.make_async_copy(go_hbm.at[:, half(c, rho), :], mine_o, load_sem)
        pair_o = _rcopy(go_hbm.at[:, half(1 - c, rho), :], rcv_o, send_sems, recv_sems, 0, sib)
        pair_i = [_rcopy(snd_i.at[j], rcv_i.at[j], send_sems, recv_sems, 1 + j, sib) for j in range(NCHIP)]

        @pl.when(t == 0)
        def _():
            load_o.start()
            pair_o.start()
            acc[...] = jnp.zeros_like(acc)
            hct = hc_ref[...].T
            csrc = dict(k=dck_ref, v=dcv_ref)
            for name, c0, c1 in DZC_COLS:
                acc[:, c0:c1] += jnp.dot(hct, csrc[name][...].astype(BF16), preferred_element_type=F32)

        ht = h_ref[...].T
        src = dict(a=a_ref, q=q_ref, k=k_ref, v=v_ref, g=g_ref)
        for name, c0, c1 in DZ_COLS:
            acc[:, c0:c1] += jnp.dot(ht, src[name][...], preferred_element_type=F32)

        @pl.when(t == nt - 1)
        def _():
            shard = lambda j: slice(j * SHARD_IN, (j + 1) * SHARD_IN)
            for j in range(NCHIP):
                snd_i[j] = acc[half(1 - c, rhi), shard(j)].astype(BF16)
                pair_i[j].start()
            load_o.wait()
            pair_o.wait_recv()
            for j in range(NCHIP):
                wire_o[j] = (mine_o[j] + rcv_o[j]).astype(BF16)
            keep_o[...] = mine_o[k] + rcv_o[k]
            mine = half(c, rhi)
            for j in range(NCHIP):
                pair_i[j].wait_recv()
                pair_sum = acc[mine, shard(j)] + rcv_i[j].astype(F32)
                wire_i[j] = pair_sum.astype(BF16)

                @pl.when(k == j)
                def _():
                    keep_i[...] = pair_sum
            pair_o.wait_send()
            for j in range(NCHIP):
                pair_i[j].wait_send()

    whole = lambda *shape: pl.BlockSpec(shape, lambda t: (0,) * len(shape))
    return pl.pallas_call(
        kern, name="dw_bwd", grid=(nt,),
        in_specs=[pl.BlockSpec((tl, DM), lambda t: (t, 0))] + _dz_specs(tl)
        + [whole(CTX, DM), whole(CTX, 512), whole(CTX, 512), pl.BlockSpec(memory_space=pl.ANY)],
        out_specs=[whole(NCHIP, rhi, wi), whole(rhi, wi), whole(NCHIP, rho, wo), whole(rho, wo)],
        out_shape=[jax.ShapeDtypeStruct((NCHIP, rhi, wi), BF16), jax.ShapeDtypeStruct((rhi, wi), F32),
                   jax.ShapeDtypeStruct((NCHIP, rho, wo), BF16), jax.ShapeDtypeStruct((rho, wo), F32)],
        scratch_shapes=[pltpu.VMEM((DM, DIN), F32), pltpu.VMEM((NCHIP, rhi, wi), BF16),
                        pltpu.VMEM((NCHIP, rhi, wi), BF16),
                        pltpu.VMEM((NCHIP, rho, wo), F32), pltpu.VMEM((NCHIP, rho, wo), F32),
                        pltpu.SemaphoreType.DMA(()), pltpu.SemaphoreType.DMA((1 + NCHIP,)),
                        pltpu.SemaphoreType.DMA((1 + NCHIP,))],
        compiler_params=_cparams(("arbitrary",), VMEM_BIG),
    )(h, *dz_parts, hc, dck, dcv, g_out)


def ctx_bwd(dck, dcv, w_full, ctx, cshift, cscale, norm_g):
    def kern(dck_ref, dcv_ref, w_ref, c_ref, sh_ref, sc_ref, g_ref, dsh_ref, dsc_ref, dg_ref):
        csrc = dict(k=dck_ref, v=dcv_ref)
        dhc = None
        first = DZC_COLS[0][1]
        for name, c0, c1 in DZC_COLS:
            part = lax.dot_general(csrc[name][...].astype(BF16), w_ref[:, c0 - first:c1 - first], _NT,
                                   preferred_element_type=F32)
            dhc = part if dhc is None else dhc + part
        _, vjp = jax.vjp(lambda g, sc, sh: _modulated(c_ref[...], g, sc, sh), g_ref[...], sc_ref[...], sh_ref[...])
        dg_ref[...], dsc_ref[...], dsh_ref[...] = vjp(dhc)

    whole = lambda r, c: pl.BlockSpec((r, c), lambda i: (0, 0))
    return pl.pallas_call(
        kern, name="ctx_bwd", grid=(1,),
        in_specs=[whole(CTX, 512), whole(CTX, 512), pl.BlockSpec((DM, 1024), lambda i: (0, DZC_COLS[0][1] // 1024)),
                  whole(CTX, DM), _row(DM), _row(DM), _row(DM)],
        out_specs=[_row(DM), _row(DM), _row(DM)],
        out_shape=[jax.ShapeDtypeStruct((1, DM), F32)] * 3,
        compiler_params=_cparams(("arbitrary",), 40 * 1024 * 1024),
    )(dck, dcv, w_full, ctx, cshift, cscale, norm_g)


def _lane_pad_rpb(rpb):
    r = jnp.pad(rpb, ((0, 0), (0, 0), (0, GRID_W - rpb.shape[-1])))
    return jnp.concatenate([r, r], axis=-1)


def local_step(chip, dev, x, c_vec, c_ctx, w_ada, b_shard, ctx, target, norm_g, sgu_g, w_s, b_s, q_g, k_g, rpb,
               w_in_shard, w_out_shard):
    bsb = jnp.broadcast_to(b_s[:, :, None], (4, 128, 128))
    qg2, kg2 = jnp.tile(q_g, (1, 2)), jnp.tile(k_g, (1, 2))

    z, h, w_in_full, w_out_full, mod_all, cs = inproj_fwd(chip, x, c_vec, c_ctx, w_ada, b_shard, norm_g, w_in_shard,
                                                          w_out_shard)
    mods = mod_all.transpose(1, 0, 2).reshape(CS_ROWS, 3 * DM)
    mod = lax.dynamic_slice(mods, (8 * dev, 0), (1, 3 * DM))
    shift, scale, gate = mod[:, :DM], mod[:, DM:2 * DM], mod[:, 2 * DM:]
    cshift, cscale = mods[8 * NDEV:8 * NDEV + 1, :DM], mods[8 * NDEV:8 * NDEV + 1, DM:2 * DM]
    zc, hc = ctx_fwd(ctx, cshift, cscale, norm_g, w_in_full)
    bias = rpb_tables(_lane_pad_rpb(rpb))
    out_a = sgu_fwd(z, sgu_g, w_s, bsb)
    out_b, *saved = attn_fwd(z, zc, bias, qg2, kg2)
    loss8, dy, dcat, dgate, dwo = outproj(out_a, out_b, x, target, gate, w_out_full.reshape(DM, DM))
    dz_a, dsg, dws, dbsb = sgu_bwd(z, sgu_g, w_s, bsb, dcat)
    dq, dk, dv, dbg, dck, dcv, dbias, dqg2, dkg2 = attn_bwd(z, zc, qg2, kg2, dcat, saved)
    drpb = rpb_bwd(dbias)[:, :, :rpb.shape[-1]]
    dz_parts = (dz_a, dq, dk, dv, dbg)
    dcshift, dcscale, dng_c = ctx_bwd(dck, dcv, w_in_full, ctx, cshift, cscale, norm_g)
    wire_i, keep_i, wire_o, keep_o = dw_bwd(h, dz_parts, hc, dck, dcv, dwo.reshape(NCHIP, SHARD_OUT, DM))
    *in_flight, token = rs_start(wire_i, wire_o)
    grad_x, dshift, dscale, dng = dh_bwd(dz_parts, w_in_full, x, dy, shift, scale, norm_g, dng_c + token[0, 0])
    got_i, got_o = rs_wait(*in_flight, dshift)
    return dict(
        loss=loss8[0:1, 0:1], grad_x=grad_x, rs=(keep_i, got_i, keep_o, got_o), cs=cs,
        dmod=jnp.concatenate([dshift, dscale, dgate], axis=-1),
        dcmod=jnp.concatenate([dcshift, dcscale, jnp.zeros((1, DM), F32)], axis=-1),
        d_norm_g=dng, d_sgu_g=dsg, d_w_s=dws, d_b_s=dbsb[:, :, 0],
        d_q_g=dqg2[:, :HDIM], d_k_g=dkg2[:, :HDIM], d_rpb=drpb)


def _me():
    return lax.axis_index("x"), lax.axis_index("y"), lax.axis_index("c")


def _flip(q):
    x, y, c = _me()
    return ((1 - x) if q & 4 else x, (1 - y) if q & 2 else y, (1 - c) if q & 1 else c)


def _chip_of(dev):
    return 2 * dev[0] + dev[1]


def _rcopy(src, dst, send_sems, recv_sems, k, dev):
    return pltpu.make_async_remote_copy(src_ref=src, dst_ref=dst, send_sem=send_sems.at[k], recv_sem=recv_sems.at[k],
                                        device_id=dev, device_id_type=MESH_ID)


_VMEM_SPEC = pl.BlockSpec(memory_space=pltpu.VMEM)
SLAB_ROWS = 80


RS_SHAPES = ((DM // 2, SHARD_IN), (SHARD_OUT // 2, DM))
_HBM_SPEC = pl.BlockSpec(memory_space=pltpu.HBM)
_SEM_SPEC = pl.BlockSpec(memory_space=pltpu.SEMAPHORE)
_IN_FLIGHT = pltpu.SideEffectType.DATAFLOW_SIDE_EFFECTING


def _rs_copies(wires, lands, send_sems, recv_sems):
    return [pltpu.make_async_remote_copy(
        src_ref=wires[n].at[_chip_of(_flip(q))], dst_ref=lands[n].at[q // 2 - 1],
        send_sem=send_sems.at[3 * n + q // 2 - 1], recv_sem=recv_sems.at[3 * n + q // 2 - 1],
        device_id=_flip(q), device_id_type=MESH_ID) for n in (0, 1) for q in (2, 4, 6)]


def rs_start(wire_i, wire_o):
    lands = [lax.empty((NCHIP - 1, rh, w), BF16) for rh, w in RS_SHAPES]

    def body(wi_ref, wo_ref, li_ref, lo_ref, send_sems, recv_sems, wi_thru, wo_thru, li_thru, lo_thru, token):
        for cp in _rs_copies((wi_ref, wo_ref), (li_ref, lo_ref), send_sems, recv_sems):
            cp.start()
        token[...] = jnp.zeros_like(token)

    hbm = lambda a: pltpu.HBM(a.shape, a.dtype)
    return pl.pallas_call(
        body, name="rs_start",
        out_shape=(pltpu.SemaphoreType.DMA((6,)), pltpu.SemaphoreType.DMA((6,)), hbm(wire_i), hbm(wire_o),
                   hbm(lands[0]), hbm(lands[1]), jax.ShapeDtypeStruct((8, 128), F32)),
        in_specs=(_HBM_SPEC,) * 4, out_specs=(_SEM_SPEC, _SEM_SPEC) + (_HBM_SPEC,) * 4 + (_VMEM_SPEC,),
        input_output_aliases={0: 2, 1: 3, 2: 4, 3: 5},
        compiler_params=pltpu.CompilerParams(has_side_effects=_IN_FLIGHT),
    )(*[pltpu.with_memory_space_constraint(a, pltpu.HBM) for a in (wire_i, wire_o, *lands)])


def rs_wait(send_sems, recv_sems, wire_i, wire_o, land_i, land_o, after):
    def body(wi_ref, wo_ref, li_ref, lo_ref, send_sems, recv_sems, after_ref, wi_dead, wo_dead, gi_ref, go_ref):
        for cp in _rs_copies((wi_ref, wo_ref), (li_ref, lo_ref), send_sems, recv_sems):
            cp.wait_send()
            cp.wait_recv()

    hbm = lambda a: pltpu.HBM(a.shape, a.dtype)
    return pl.pallas_call(
        body, name="rs_wait", out_shape=(hbm(wire_i), hbm(wire_o), hbm(land_i), hbm(land_o)),
        in_specs=(_HBM_SPEC,) * 4 + (_SEM_SPEC, _SEM_SPEC, pl.BlockSpec(memory_space=pl.ANY)),
        out_specs=(_HBM_SPEC,) * 4, input_output_aliases={0: 0, 1: 1, 2: 2, 3: 3},
        compiler_params=pltpu.CompilerParams(has_side_effects=_IN_FLIGHT),
    )(wire_i, wire_o, land_i, land_o, send_sems, recv_sems, after)[2:]


def final_reduce(keep_i, got_i, keep_o, got_o, slab, cs, w_ada, c_ctx):
    (rhi, wi), (rho, wo) = RS_SHAPES

    def kern(ki_hbm, gi_hbm, ko_hbm, go_hbm, s_ref, cs_ref, w_hbm, cc_ref,
             gin_ref, gout_ref, tot_ref, dw_ref, db_ref, dcc_ref,
             ki, gi, ko, go, w_scr, all_ref, dms_scr, parts, load_sems, send_sems, recv_sems):
        x, y, c = _me()
        k = 2 * x + y
        sib = _flip(1)
        dev = lambda d: 4 * d[0] + 2 * d[1] + d[2]
        me = dev((x, y, c))

        def slab_copy(idx, owner, to):
            return _rcopy(all_ref.at[dev(owner)], all_ref.at[dev(owner)], send_sems, recv_sems, idx, to)

        all_ref[me] = s_ref[...]
        first = [slab_copy(0, (x, y, c), sib)] + [slab_copy(q // 2, (x, y, c), _flip(q)) for q in (2, 4, 6)]
        for cp in first:
            cp.start()
        loads = [pltpu.make_async_copy(src, dst, load_sems.at[n]) for n, (src, dst) in enumerate(
            ((ki_hbm, ki), (gi_hbm, gi), (ko_hbm, ko), (go_hbm, go), (w_hbm, w_scr)))]
        for cp in loads:
            cp.start()

        shares = []
        for n, (keep, got, out) in enumerate(((ki, gi, gin_ref), (ko, go, gout_ref))):
            rh = RS_SHAPES[n][0]
            half = lambda hh, rh=rh: pl.ds(pl.multiple_of(hh * rh, rh), rh)
            loads[2 * n].wait()
            loads[2 * n + 1].wait()
            out[half(c), :] = ((keep[...] + got[0].astype(F32)) + got[1].astype(F32)) + got[2].astype(F32)
            share = _rcopy(out.at[half(c), :], out.at[half(c), :], send_sems, recv_sems, 7 + n, sib)
            share.start()
            shares.append((share, _rcopy(out.at[half(1 - c), :], out.at[half(1 - c), :], send_sems, recv_sems, 7 + n,
                                         sib)))

        passed = []
        for q in (2, 4, 6):
            slab_copy(q // 2, _flip(q), (x, y, c)).wait_recv()
            cp = slab_copy(3 + q // 2, _flip(q), sib)
            cp.start()
            passed.append(cp)
        slab_copy(0, sib, (x, y, c)).wait_recv()
        for q in (2, 4, 6):
            slab_copy(3 + q // 2, _flip(q | 1), (x, y, c)).wait_recv()
        tot = all_ref[0]
        for d in range(1, NDEV):
            tot = tot + all_ref[d]
        tot_ref[...] = tot

        pad = jnp.zeros((7, DM), F32)
        dm = [jnp.concatenate([all_ref[d, 12 + j:13 + j, :] for d in range(NDEV)] + [tot[9 + j:10 + j, :], pad], axis=0)
              for j in range(3)]
        db_ref[...] = jnp.concatenate([jnp.sum(part, axis=0, keepdims=True) for part in dm], axis=0)
        dm = jnp.concatenate(dm, axis=-1)
        for j in range(NCHIP):
            @pl.when(k == j)
            def _():
                dms_scr[...] = dm[:, j * SHARD_ADA:(j + 1) * SHARD_ADA].astype(BF16)

        a_in = jnp.concatenate([cs_ref[8 * d:8 * d + 1, :] for d in range(NDEV)]
                               + [cs_ref[8 * NDEV:8 * NDEV + 1, :], pad], axis=0)
        act = jax.nn.silu(a_in).astype(BF16)
        dms = dms_scr[...]
        dw_ref[...] = lax.dot_general(act, dms, (((0,), (0,)), ((), ())), preferred_element_type=F32)
        loads[4].wait()
        parts[k] = lax.dot_general(dms, w_scr[...].astype(BF16), (((1,), (1,)), ((), ())), preferred_element_type=F32)
        sends = [_rcopy(parts.at[k], parts.at[k], send_sems, recv_sems, 8 + q // 2, _flip(q)) for q in (2, 4, 6)]
        for cp in sends:
            cp.start()
        for q in (2, 4, 6):
            kq = _chip_of(_flip(q))
            _rcopy(parts.at[kq], parts.at[kq], send_sems, recv_sems, 8 + q // 2, _flip(q)).wait_recv()
        dact = ((parts[0] + parts[1]) + parts[2]) + parts[3]
        _, vjp = jax.vjp(jax.nn.silu, cc_ref[...])
        dcc_ref[...] = vjp(dact[8:9, :])[0]

        for share, arrival in shares:
            arrival.wait_recv()
            share.wait_send()
        for cp in first + passed + sends:
            cp.wait_send()

    any_spec = pl.BlockSpec(memory_space=pl.ANY)
    return pl.pallas_call(
        kern, name="final_reduce",
        in_specs=[any_spec] * 4 + [_VMEM_SPEC, _VMEM_SPEC, any_spec, _VMEM_SPEC], out_specs=[_VMEM_SPEC] * 6,
        out_shape=[jax.ShapeDtypeStruct((2 * rhi, wi), F32), jax.ShapeDtypeStruct((2 * rho, wo), F32),
                   jax.ShapeDtypeStruct((SLAB_ROWS, DM), F32), jax.ShapeDtypeStruct((DM, SHARD_ADA), F32),
                   jax.ShapeDtypeStruct((3, DM), F32), jax.ShapeDtypeStruct((1, DM), F32)],
        scratch_shapes=[pltpu.VMEM((rhi, wi), F32), pltpu.VMEM((NCHIP - 1, rhi, wi), BF16),
                        pltpu.VMEM((rho, wo), F32), pltpu.VMEM((NCHIP - 1, rho, wo), BF16),
                        pltpu.VMEM((DM, SHARD_ADA), F32), pltpu.VMEM((NDEV, SLAB_ROWS, DM), F32),
                        pltpu.VMEM((16, SHARD_ADA), BF16), pltpu.VMEM((NCHIP, 16, DM), F32),
                        pltpu.SemaphoreType.DMA((5,)), pltpu.SemaphoreType.DMA((12,)), pltpu.SemaphoreType.DMA((12,))],
        compiler_params=pltpu.CompilerParams(vmem_limit_bytes=40 * 1024 * 1024),
    )(keep_i, got_i, keep_o, got_o, slab, cs, w_ada, c_ctx)


def _adamw_math(w, g, m, v):
    m = B1 * m + (1.0 - B1) * g
    v = B2 * v + (1.0 - B2) * (g * g)
    m_hat = m / (1.0 - B1 ** STEP)
    v_hat = v / (1.0 - B2 ** STEP)
    return -LR * (m_hat / (jnp.sqrt(v_hat) + ADAM_EPS) + WD * w), m, v


def adamw_big(w, g, m, v, name, block_rows=256):
    rows, width = w.shape

    def kern(w_ref, g_ref, m_ref, v_ref, d_ref, nm_ref, nv_ref):
        d_ref[...], nm_ref[...], nv_ref[...] = _adamw_math(w_ref[...], g_ref[...], m_ref[...], v_ref[...])

    spec = pl.BlockSpec((block_rows, width), lambda i: (i, 0))
    return pl.pallas_call(
        kern, name=name, grid=(rows // block_rows,), in_specs=[spec] * 4, out_specs=[spec] * 3,
        out_shape=[jax.ShapeDtypeStruct((rows, width), F32)] * 3,
        compiler_params=_cparams(("arbitrary",)),
    )(w, g, m, v)


def adamw_small(quads):
    n = len(quads)

    def kern(*refs):
        ins, outs = refs[:4 * n], refs[4 * n:]
        for i in range(n):
            w, g, m, v = (r[...] for r in ins[4 * i:4 * i + 4])
            outs[3 * i][...], outs[3 * i + 1][...], outs[3 * i + 2][...] = _adamw_math(w, g, m, v)

    flat = [a for quad in quads for a in quad]
    res = pl.pallas_call(
        kern, name="adamw_small", in_specs=[_VMEM_SPEC] * (4 * n), out_specs=[_VMEM_SPEC] * (3 * n),
        out_shape=[jax.ShapeDtypeStruct(q[0].shape, F32) for q in quads for _ in range(3)],
    )(*flat)
    return [tuple(res[3 * i:3 * i + 3]) for i in range(n)]


def _rows_of(a, rows):
    flat = a.reshape(-1)
    return jnp.pad(flat, (0, rows * DM - flat.shape[0])).reshape(rows, DM)


def kernel(x, c, ctx, c_ctx, w_ada, b_ada, norm_g, w_in, sgu_norm_g, w_spatial, b_spatial, q_norm_g, k_norm_g, rpb, w_out, loss_target, m_c_ctx, m_w_ada, m_b_ada, m_norm_g, m_w_in, m_sgu_norm_g, m_w_spatial, m_b_spatial, m_q_norm_g, m_k_norm_g, m_rpb, m_w_out, v_c_ctx, v_w_ada, v_b_ada, v_norm_g, v_w_in, v_sgu_norm_g, v_w_spatial, v_b_spatial, v_q_norm_g, v_k_norm_g, v_rpb, v_w_out):
    xi, yi, ci = lax.axis_index("x"), lax.axis_index("y"), lax.axis_index("c")
    chip, dev = 2 * xi + yi, 4 * xi + 2 * yi + ci
    c_ctx2 = c_ctx.reshape(1, DM)

    b_shard = lax.dynamic_slice(b_ada, (0, chip * SHARD_ADA), (1, SHARD_ADA))
    part = local_step(chip.reshape(1).astype(jnp.int32), dev, x[0], c, c_ctx2, w_ada[0], b_shard, ctx[0], loss_target[0],
                      norm_g, sgu_norm_g, w_spatial[0], b_spatial[0], q_norm_g, k_norm_g, rpb[0], w_in[0], w_out[0])
    cs = part["cs"]

    slab = jnp.concatenate([
        part["d_norm_g"], _rows_of(part["d_sgu_g"], 1), _rows_of(part["d_b_s"], 1),
        _rows_of(jnp.concatenate([part["d_q_g"], part["d_k_g"]], axis=-1), 1), _rows_of(part["d_rpb"], 4),
        _rows_of(part["loss"], 1), _rows_of(part["dcmod"], 3), _rows_of(part["dmod"], 3), jnp.zeros((1, DM), F32),
        _rows_of(part["d_w_s"], 64)], axis=0)
    g_w_in, g_w_out, tot, g_w_ada, g_b_ada, g_c_ctx = final_reduce(*part["rs"], slab, cs, w_ada[0], c_ctx2)
    g_b_ada = g_b_ada.reshape(1, 3 * DM)

    loss = tot[8, 0]
    g_small = dict(
        c_ctx=g_c_ctx, b_ada=g_b_ada, norm_g=tot[0:1], sgu_norm_g=tot[1:2, :512], w_spatial=tot[16:80].reshape(512, 128),
        b_spatial=tot[2:3, :512].reshape(4, 128), q_norm_g=tot[3:4, :HDIM], k_norm_g=tot[3:4, HDIM:2 * HDIM],
        rpb=tot[4:8].reshape(-1)[:HEADS * 15 * 31].reshape(HEADS * 15, 31))
    shapes = dict(c_ctx=(DM,), w_ada=(1, DM, SHARD_ADA), b_ada=(1, 3 * DM), norm_g=(1, DM), w_in=(1, DM, SHARD_IN),
                  sgu_norm_g=(1, 512), w_spatial=(1, 4, 128, 128), b_spatial=(1, 4, 128), q_norm_g=(1, HDIM),
                  k_norm_g=(1, HDIM), rpb=(1, HEADS, 15, 31), w_out=(1, SHARD_OUT, DM))
    names = list(shapes)
    weights = dict(c_ctx=c_ctx, w_ada=w_ada, b_ada=b_ada, norm_g=norm_g, w_in=w_in, sgu_norm_g=sgu_norm_g,
                   w_spatial=w_spatial, b_spatial=b_spatial, q_norm_g=q_norm_g, k_norm_g=k_norm_g, rpb=rpb, w_out=w_out)
    m_in = dict(zip(names, (m_c_ctx, m_w_ada, m_b_ada, m_norm_g, m_w_in, m_sgu_norm_g, m_w_spatial, m_b_spatial,
                            m_q_norm_g, m_k_norm_g, m_rpb, m_w_out)))
    v_in = dict(zip(names, (v_c_ctx, v_w_ada, v_b_ada, v_norm_g, v_w_in, v_sgu_norm_g, v_w_spatial, v_b_spatial,
                            v_q_norm_g, v_k_norm_g, v_rpb, v_w_out)))
    grads = dict(g_small, w_ada=g_w_ada, w_in=g_w_in, w_out=g_w_out)
    upd = {}
    for n in ("w_ada", "w_in", "w_out"):
        g = grads[n]
        upd[n] = adamw_big(weights[n].reshape(g.shape), g, m_in[n].reshape(g.shape), v_in[n].reshape(g.shape),
                           "adamw_" + n)
    small = [n for n in names if n not in upd]
    res = adamw_small([(weights[n].reshape(grads[n].shape), grads[n], m_in[n].reshape(grads[n].shape),
                        v_in[n].reshape(grads[n].shape)) for n in small])
    upd.update(zip(small, res))
    out = [loss, part["grad_x"].reshape(1, SEQ, DM)]
    out += [grads[n].reshape(shapes[n]) for n in names]
    for slot in range(3):
        out += [upd[n][slot].reshape(shapes[n]) for n in names]
    return tuple(out)
```

```python
import functools

import jax
import jax.numpy as jnp
from jax import lax
from jax.experimental import pallas as pl
from jax.experimental.pallas import tpu as pltpu

F32, BF16 = jnp.float32, jnp.bfloat16
SEQ, DM, CTX, DIN = 4096, 1024, 256, 3584
NCHIP, NDEV = 4, 8
SHARD_IN = DIN // NCHIP
SHARD_ADA = 3 * DM // NCHIP
SHARD_OUT = DM // NCHIP
GRID_W = 64
QROWS = 4
KROWS = 12
QBLK, KBLK = QROWS * GRID_W, KROWS * GRID_W
NQBLK = SEQ // QBLK
HEADS, HDIM, NPAIR = 8, 64, 4
EPS = 1e-6
NEG_INF = -1e30
ZQ, ZK, ZV, ZG = 12, 16, 20, 24
LR, B1, B2, ADAM_EPS, WD, STEP = 0.001, 0.9, 0.999, 1e-08, 0.01, 10
VMEM_BIG = 56 * 1024 * 1024
MESH_ID = pl.DeviceIdType.MESH


def _dot(a, b, lhs_c, rhs_c):
    return lax.dot_general(a.astype(BF16), b.astype(BF16), (((lhs_c,), (rhs_c,)), ((), ())),
                           preferred_element_type=F32)


@jax.custom_vjp
def mm(a, b):
    return _dot(a, b, 1, 0)


@jax.custom_vjp
def mm_nt(a, b):
    return _dot(a, b, 1, 1)


@jax.custom_vjp
def mm_tn(a, b):
    return _dot(a, b, 0, 0)


mm.defvjp(lambda a, b: (mm(a, b), (a, b)), lambda r, ct: (mm_nt(ct, r[1]), mm_tn(r[0], ct)))
mm_nt.defvjp(lambda a, b: (mm_nt(a, b), (a, b)), lambda r, ct: (mm(ct, r[1]), mm_tn(ct, r[0])))
mm_tn.defvjp(lambda a, b: (mm_tn(a, b), (a, b)), lambda r, ct: (mm_nt(r[1], ct), mm(r[0], ct)))


def _rms(x, g):
    return x * lax.rsqrt(jnp.mean(x * x, axis=-1, keepdims=True) + EPS) * g


def _modulated(x, g, scale, shift):
    return _rms(x, g) * (1.0 + scale) + shift


def _pair_rms(x, g2):
    lo = lax.broadcasted_iota(jnp.int32, (1, 2 * HDIM), 1) < HDIM
    sq = x * x
    s_lo = jnp.sum(jnp.where(lo, sq, 0.0), axis=-1, keepdims=True)
    s_hi = jnp.sum(jnp.where(lo, 0.0, sq), axis=-1, keepdims=True)
    rs = jnp.where(lo, lax.rsqrt(s_lo / HDIM + EPS), lax.rsqrt(s_hi / HDIM + EPS))
    return x * rs * g2


def _cparams(sem, vmem=None):
    return pltpu.CompilerParams(dimension_semantics=sem, vmem_limit_bytes=vmem)


def _row(n):
    return pl.BlockSpec((1, n), lambda *_: (0, 0))


CS_ROWS = 8 * NDEV + 8


def _mod_part(mod_ref, row, part):
    pieces = []
    for j in range(NCHIP):
        lo, hi = max(part * DM, j * SHARD_ADA), min((part + 1) * DM, (j + 1) * SHARD_ADA)
        if lo < hi:
            pieces.append(mod_ref[j, row, lo - j * SHARD_ADA:hi - j * SHARD_ADA])
    return jnp.concatenate(pieces, axis=-1)


def inproj_fwd(chip, x, c_vec, c_ctx, w_ada, b_shard, norm_g, w_shard, wo_shard):
    tl = 1024
    nt = SEQ // tl
    halves = (DM // 2, SHARD_OUT // 2)
    n_w, n_c = 12, NDEV - 1

    def kern(k_ref, x_ref, cv_ref, cc_ref, wa_ref, b_ref, g_ref, w_ref, wo_ref,
             z_ref, h_ref, wfull_ref, wofull_ref, modall_ref, csall_ref,
             w_scr, wo_scr, h_scr, mine, cs_scr, mod_scr, shsc_scr, send_sems, recv_sems, out_sems):
        s, t = pl.program_id(0), pl.program_id(1)
        xi, yi, c = _me()
        k, me = 2 * xi + yi, 4 * xi + 2 * yi + c
        sib = _flip(1)
        rows = pl.ds(pl.multiple_of(t * tl, tl), tl)
        gathered = (w_scr, wo_scr)
        slot = lambda d: pl.ds(pl.multiple_of(8 * d, 8), 8)

        def c_copy(q, owner):
            return _rcopy(mine, cs_scr.at[slot(owner), :], send_sems, recv_sems, n_w + q - 1, _flip(q))

        def m_copy(q, chip_of_block):
            return _rcopy(mod_scr.at[chip_of_block], mod_scr.at[chip_of_block], send_sems, recv_sems,
                          n_w + n_c + q // 2 - 1, _flip(q))

        def adaln():
            first = lax.broadcasted_iota(jnp.int32, (8, DM), 0) == 0
            mine[...] = jnp.where(first, jnp.broadcast_to(cv_ref[...], (8, DM)), 0.0)
            cs_scr[slot(me), :] = mine[...]
            cs_scr[slot(NDEV), :] = jnp.where(first, jnp.broadcast_to(cc_ref[...], (8, DM)), 0.0)
            for q in range(1, NDEV):
                c_copy(q, me).start()
            wa = wa_ref[...].astype(BF16)
            for q in range(1, NDEV):
                px, py, pc = _flip(q)
                c_copy(q, 4 * px + 2 * py + pc).wait_recv()
            act = jax.nn.silu(cs_scr[...]).astype(BF16)
            mod_scr[k] = jnp.dot(act, wa, preferred_element_type=F32) + b_ref[...]
            for q in (2, 4, 6):
                m_copy(q, k).start()
            for q in (2, 4, 6):
                m_copy(q, _chip_of(_flip(q))).wait_recv()
            row = pl.ds(8 * me, 1)
            shsc_scr[0:1, :] = _mod_part(mod_scr, row, 0)
            shsc_scr[1:2, :] = _mod_part(mod_scr, row, 1)
            pltpu.sync_copy(mod_scr, modall_ref)
            pltpu.sync_copy(cs_scr, csall_ref)

        def block(n, chip_of_block, hh):
            return gathered[n].at[chip_of_block, pl.ds(pl.multiple_of(hh * halves[n], halves[n]), halves[n]), :]

        def ici(n, q, chip_of_block):
            blk = block(n, chip_of_block, c)
            return _rcopy(blk, blk, send_sems, recv_sems, 6 * n + q // 2 - 1, _flip(q))

        def d2d(n, q, chip_of_block, hh):
            blk = block(n, chip_of_block, hh)
            return _rcopy(blk, blk, send_sems, recv_sems, 6 * n + 3 + q // 2 - 1, sib)

        @pl.when((s == 0) & (t == 0))
        def _():
            adaln()
            w_scr[k] = w_ref[...].astype(BF16)
            wo_scr[k] = wo_ref[...].astype(BF16)
            for q in (2, 4, 6):
                ici(0, q, k).start()
                ici(1, q, k).start()

        for sweep in (1, 2, 3):
            @pl.when((s == sweep) & (t == 0))
            def _():
                q = 2 * sweep
                src = _chip_of(_flip(q))
                for n in (0, 1):
                    ici(n, q, src).wait_recv()
                    d2d(n, q, src, c).start()
                for n in (0, 1):
                    d2d(n, q, src, 1 - c).wait_recv()

        @pl.when(s == 0)
        def _():
            hb = _modulated(x_ref[...], g_ref[...], shsc_scr[1:2, :], shsc_scr[0:1, :]).astype(BF16)
            h_scr[rows, :] = hb
            h_ref[...] = hb

        z_ref[...] = jnp.dot(h_scr[rows, :], w_scr[lax.bitwise_xor(k, s)], preferred_element_type=F32)

        @pl.when((s == NCHIP - 1) & (t == nt - 1))
        def _():
            for q in range(1, NDEV):
                c_copy(q, me).wait_send()
            for q in (2, 4, 6):
                m_copy(q, k).wait_send()
            for n in (0, 1):
                for q in (2, 4, 6):
                    ici(n, q, k).wait_send()
                    d2d(n, q, _chip_of(_flip(q)), c).wait_send()
            outs = [pltpu.make_async_copy(w_scr.at[j], wfull_ref.at[:, j * SHARD_IN:(j + 1) * SHARD_IN], out_sems.at[j])
                    for j in range(NCHIP)] + [pltpu.make_async_copy(wo_scr, wofull_ref, out_sems.at[NCHIP])]
            for cp in outs:
                cp.start()
            for cp in outs:
                cp.wait()

    once = lambda s, t, k: (jnp.where(s == 0, t, nt - 1), 0)
    hbm = pl.BlockSpec(memory_space=pl.ANY)
    n_sem = n_w + n_c + 3
    return pl.pallas_call(
        kern, name="inproj_fwd",
        grid_spec=pltpu.PrefetchScalarGridSpec(
            num_scalar_prefetch=1, grid=(NCHIP, nt),
            in_specs=[pl.BlockSpec((tl, DM), once)] + [_VMEM_SPEC] * 7,
            out_specs=[pl.BlockSpec((tl, SHARD_IN), lambda s, t, k: (t, lax.bitwise_xor(k[0], s))),
                       pl.BlockSpec((tl, DM), once), hbm, hbm, hbm, hbm],
            scratch_shapes=[pltpu.VMEM((NCHIP, DM, SHARD_IN), BF16), pltpu.VMEM((NCHIP, SHARD_OUT, DM), BF16),
                            pltpu.VMEM((SEQ, DM), BF16), pltpu.VMEM((8, DM), F32), pltpu.VMEM((CS_ROWS, DM), F32),
                            pltpu.VMEM((NCHIP, CS_ROWS, SHARD_ADA), F32), pltpu.VMEM((8, DM), F32),
                            pltpu.SemaphoreType.DMA((n_sem,)), pltpu.SemaphoreType.DMA((n_sem,)),
                            pltpu.SemaphoreType.DMA((NCHIP + 1,))]),
        out_shape=[jax.ShapeDtypeStruct((SEQ, DIN), F32), jax.ShapeDtypeStruct((SEQ, DM), BF16),
                   jax.ShapeDtypeStruct((DM, DIN), BF16), jax.ShapeDtypeStruct((NCHIP, SHARD_OUT, DM), BF16),
                   jax.ShapeDtypeStruct((NCHIP, CS_ROWS, SHARD_ADA), F32), jax.ShapeDtypeStruct((CS_ROWS, DM), F32)],
        compiler_params=_cparams(("arbitrary", "arbitrary"), VMEM_BIG),
    )(chip, x, c_vec, c_ctx, w_ada, b_shard, norm_g, w_shard, wo_shard)


def ctx_fwd(ctx, cshift, cscale, norm_g, w_full):
    def kern(c_ref, sh_ref, sc_ref, g_ref, w_ref, zc_ref, hc_ref):
        hc = _modulated(c_ref[...], g_ref[...], sc_ref[...], sh_ref[...]).astype(BF16)
        hc_ref[...] = hc
        zc_ref[...] = jnp.dot(hc, w_ref[...], preferred_element_type=F32)

    return pl.pallas_call(
        kern, name="ctx_fwd", grid=(1,),
        in_specs=[pl.BlockSpec((CTX, DM), lambda i: (0, 0)), _row(DM), _row(DM), _row(DM),
                  pl.BlockSpec((DM, 2 * SHARD_IN), lambda i: (0, 1))],
        out_specs=[pl.BlockSpec((CTX, 2 * SHARD_IN), lambda i: (0, 0)),
                   pl.BlockSpec((CTX, DM), lambda i: (0, 0))],
        out_shape=[jax.ShapeDtypeStruct((CTX, 2 * SHARD_IN), F32), jax.ShapeDtypeStruct((CTX, DM), BF16)],
        compiler_params=_cparams(("arbitrary",)),
    )(ctx, cshift, cscale, norm_g, w_full)


SGU_CHUNK, SGU_PER_STEP = 128, 4


def _gelu(x):
    return 0.5 * x * (1.0 + lax.erf(x * 0.7071067811865476))


def _sgu_chunk(au, av, ag, sg, ws, bsb):
    u, v = _gelu(au), _gelu(av)
    outs = []
    for g in range(4):
        sl = slice(128 * g, 128 * (g + 1))
        mixed = mm(ws[g], _rms(v[:, sl], sg[:, sl])) + bsb[g]
        outs.append(u[:, sl] * mixed * jax.nn.silu(ag[:, sl]))
    return jnp.concatenate(outs, axis=-1)


def _sgu_specs():
    rows = SGU_CHUNK * SGU_PER_STEP
    zspec = lambda c: pl.BlockSpec((rows, 512), lambda n: (n, c))
    wspec = pl.BlockSpec((4, 128, 128), lambda n: (0, 0, 0))
    return rows, [zspec(0), zspec(1), zspec(2), _row(512), wspec, wspec]


def sgu_fwd(z, sg, ws, bsb):
    rows, in_specs = _sgu_specs()

    def kern(au_ref, av_ref, ag_ref, sg_ref, ws_ref, bs_ref, o_ref):
        for c in range(SGU_PER_STEP):
            sl = slice(c * SGU_CHUNK, (c + 1) * SGU_CHUNK)
            o_ref[sl, :] = _sgu_chunk(au_ref[sl, :], av_ref[sl, :], ag_ref[sl, :], sg_ref[...], ws_ref[...],
                                      bs_ref[...])

    return pl.pallas_call(
        kern, name="sgu_fwd", grid=(SEQ // rows,), in_specs=in_specs,
        out_specs=pl.BlockSpec((rows, 512), lambda n: (n, 0)),
        out_shape=jax.ShapeDtypeStruct((SEQ, 512), F32),
        compiler_params=_cparams(("arbitrary",)),
    )(z, z, z, sg, ws, bsb)


def sgu_bwd(z, sg, ws, bsb, dcat):
    rows, in_specs = _sgu_specs()

    def kern(au_ref, av_ref, ag_ref, sg_ref, ws_ref, bs_ref, do_ref, dz_ref, dsg_ref, dws_ref, dbs_ref):
        @pl.when(pl.program_id(0) == 0)
        def _():
            dsg_ref[...] = jnp.zeros_like(dsg_ref)
            dws_ref[...] = jnp.zeros_like(dws_ref)
            dbs_ref[...] = jnp.zeros_like(dbs_ref)

        for c in range(SGU_PER_STEP):
            sl = slice(c * SGU_CHUNK, (c + 1) * SGU_CHUNK)
            _, vjp = jax.vjp(_sgu_chunk, au_ref[sl, :], av_ref[sl, :], ag_ref[sl, :], sg_ref[...], ws_ref[...],
                             bs_ref[...])
            dau, dav, dag, dsg, dws, dbs = vjp(do_ref[sl, :])
            dz_ref[sl, 0:512] = dau.astype(BF16)
            dz_ref[sl, 512:1024] = dav.astype(BF16)
            dz_ref[sl, 1024:1536] = dag.astype(BF16)
            dsg_ref[...] += dsg
            dws_ref[...] += dws
            dbs_ref[...] += dbs

        @pl.when(pl.program_id(0) == pl.num_programs(0) - 1)
        def _():
            dbs_ref[...] = jnp.broadcast_to(jnp.sum(dbs_ref[...], axis=-1, keepdims=True), dbs_ref.shape)

    wspec = pl.BlockSpec((4, 128, 128), lambda n: (0, 0, 0))
    return pl.pallas_call(
        kern, name="sgu_bwd", grid=(SEQ // rows,),
        in_specs=in_specs + [pl.BlockSpec((rows, 512), lambda n: (n, 0))],
        out_specs=[pl.BlockSpec((rows, 1536), lambda n: (n, 0)), _row(512), wspec, wspec],
        out_shape=[jax.ShapeDtypeStruct((SEQ, 1536), BF16), jax.ShapeDtypeStruct((1, 512), F32),
                   jax.ShapeDtypeStruct((4, 128, 128), F32), jax.ShapeDtypeStruct((4, 128, 128), F32)],
        compiler_params=_cparams(("arbitrary",)),
    )(z, z, z, sg, ws, bsb, dcat)


_DR_OFF = (7, 3, -1)


def _row_valid(v, rr, j):
    return (j < 8, rr <= j < rr + 8, 4 <= j < 12)[v]


def _col_window():
    q = lax.broadcasted_iota(jnp.int32, (GRID_W, 128), 0)
    kc = lax.broadcasted_iota(jnp.int32, (GRID_W, 128), 1) % GRID_W
    c0 = jnp.clip(q - 8, 0, GRID_W - 16)
    return (kc >= c0) & (kc < c0 + 16)


def rpb_tables(rpb2):
    def kern(r_ref, b_ref):
        base = r_ref[0]
        lo = lax.broadcasted_iota(jnp.int32, (1, 128), 1) < GRID_W
        win = _col_window()
        tiles = {}
        for v in range(3):
            for rr in range(QROWS):
                for jp in range(KROWS // 2):
                    j0, j1 = 2 * jp, 2 * jp + 1
                    ok0, ok1 = _row_valid(v, rr, j0), _row_valid(v, rr, j1)
                    key = (j0 - rr + _DR_OFF[v], ok0, ok1) if (ok0 or ok1) else None
                    if key not in tiles:
                        if key is None:
                            tiles[key] = jnp.full((GRID_W, 128), NEG_INF, F32)
                        else:
                            d0 = key[0]
                            r0 = base[d0:d0 + 1, :] if ok0 else jnp.zeros((1, 128), F32)
                            r1 = base[d0 + 1:d0 + 2, :] if ok1 else jnp.zeros((1, 128), F32)
                            y = jnp.broadcast_to(jnp.where(lo, r0, r1), (GRID_W, 128))
                            y = pltpu.roll(pltpu.roll(y, 128 - 15, 1), 0, 1, stride=1, stride_axis=0)
                            tiles[key] = jnp.where(win & jnp.where(lo, ok0, ok1), y, NEG_INF)
                    b_ref[v, 0, rr * GRID_W:(rr + 1) * GRID_W, jp * 128:(jp + 1) * 128] = tiles[key]

    return pl.pallas_call(
        kern, name="rpb_tables", grid=(HEADS,),
        in_specs=[pl.BlockSpec((1, 15, 128), lambda h: (h, 0, 0))],
        out_specs=pl.BlockSpec((3, 1, QBLK, KBLK), lambda h: (0, h, 0, 0)),
        out_shape=jax.ShapeDtypeStruct((3, HEADS, QBLK, KBLK), F32),
        compiler_params=_cparams(("arbitrary",)),
    )(rpb2)


def rpb_bwd(dbias):
    def kern(g0_ref, g1_ref, g2_ref, o_ref):
        g_refs = (g0_ref.at[0], g1_ref.at[0], g2_ref.at[0])
        lo = lax.broadcasted_iota(jnp.int32, (1, 128), 1) < GRID_W
        ri = lax.broadcasted_iota(jnp.int32, (GRID_W, GRID_W), 0)
        ci = lax.broadcasted_iota(jnp.int32, (GRID_W, GRID_W), 1)
        flip = (ri + ci == GRID_W - 1).astype(F32)
        groups = {}
        for v in range(3):
            for rr in range(QROWS):
                for jp in range(KROWS // 2):
                    j0, j1 = 2 * jp, 2 * jp + 1
                    ok0, ok1 = _row_valid(v, rr, j0), _row_valid(v, rr, j1)
                    if not (ok0 or ok1):
                        continue
                    g = g_refs[v][0, rr * GRID_W:(rr + 1) * GRID_W, jp * 128:(jp + 1) * 128]
                    key = (j0 - rr + _DR_OFF[v], ok0, ok1)
                    groups[key] = g if key not in groups else groups[key] + g
        acc = [jnp.zeros((1, 128), F32) for _ in range(15)]
        for (d0, ok0, ok1), g in groups.items():
            g = lax.dot_general(flip, g, (((1,), (0,)), ((), ())), precision=lax.Precision.HIGHEST,
                                preferred_element_type=F32)
            g = pltpu.roll(pltpu.roll(g, 128 - 48, 1), 0, 1, stride=1, stride_axis=0)
            s = jnp.sum(g, axis=0, keepdims=True)
            if ok0:
                acc[d0] = acc[d0] + jnp.where(lo, s, 0.0)
            if ok1:
                acc[d0 + 1] = acc[d0 + 1] + jnp.where(lo, 0.0, s)
        for d in range(15):
            o_ref[0, d:d + 1, :] = acc[d] + pltpu.roll(acc[d], GRID_W, 1)

    return pl.pallas_call(
        kern, name="rpb_bwd", grid=(HEADS,),
        in_specs=[pl.BlockSpec((1, 1, QBLK, KBLK), functools.partial(lambda v, h: (v, h, 0, 0), v)) for v in range(3)],
        out_specs=pl.BlockSpec((1, 15, 128), lambda h: (h, 0, 0)),
        out_shape=jax.ShapeDtypeStruct((HEADS, 15, 128), F32),
        compiler_params=_cparams(("arbitrary",)),
    )(dbias, dbias, dbias)


def _scaled_q(q_raw, qg):
    return _pair_rms(q_raw, qg) * (HDIM ** -0.5)


def _head_lanes():
    lo = lax.broadcasted_iota(jnp.int32, (1, 2 * HDIM), 1) < HDIM
    return lo, jnp.logical_not(lo)


SOFTMAX_ROWS = 32


def _emit_interleaved(vector_work, matmul_work):
    for j in range(max(len(vector_work), len(matmul_work))):
        for work in (vector_work, matmul_work):
            if j < len(work):
                work[j]()


def _kblock(i):
    return jnp.clip(i - 1, 0, (SEQ - KBLK) // QBLK)


def _kstart(i):
    return pl.multiple_of(_kblock(i) * QBLK, QBLK)


ATTN_BLOCKS = 4
TILE_BUFFERS = 4
ATTN_STEPS = NQBLK // ATTN_BLOCKS
ATTN_ROWS = ATTN_BLOCKS * QBLK


def _bias_variant(i, b):
    if b == 0:
        return jnp.where(i == 0, 0, 1)
    if b == ATTN_BLOCKS - 1:
        return jnp.where(i == ATTN_STEPS - 1, 2, 1)
    return 1
KCOLS = QBLK


def _attn_in_specs():
    return [
        pl.BlockSpec((ATTN_ROWS, 128), lambda p, i: (i, ZQ + p)),
        pl.BlockSpec((SEQ, 128), lambda p, i: (0, ZK + p)),
        pl.BlockSpec((SEQ, 128), lambda p, i: (0, ZV + p)),
        pl.BlockSpec((ATTN_ROWS, 128), lambda p, i: (i, ZG + p)),
        pl.BlockSpec((CTX, 128), lambda p, i: (0, 2 + p)),
        pl.BlockSpec((CTX, 128), lambda p, i: (0, 6 + p)),
    ]


def _bias_specs():
    return [pl.BlockSpec((3, 2, QBLK, KBLK), lambda p, i: (0, p, 0, 0))]


def _prob_specs():
    return [pl.BlockSpec((2, ATTN_ROWS, KBLK), lambda p, i: (p, i, 0)),
            pl.BlockSpec((2, ATTN_ROWS, CTX), lambda p, i: (p, i, 0))]


NORM_ROWS = 512


def _half_sums(x):
    lo = lax.broadcasted_iota(jnp.int32, (1, 2 * HDIM), 1) < HDIM
    return jnp.where(lo, jnp.sum(jnp.where(lo, x, 0.0), axis=-1, keepdims=True),
                     jnp.sum(jnp.where(lo, 0.0, x), axis=-1, keepdims=True))


def _pair_rms_bwd(x, g2, ct):
    rs = lax.rsqrt(_half_sums(x * x) / HDIM + EPS)
    y = x * rs
    dy = ct * g2
    return rs * (dy - y * (_half_sums(dy * y) / HDIM)), jnp.sum(ct * y, axis=0, keepdims=True)


def _norm_keys(k_ref, ck_ref, kg_ref, kn_scr, ckn_scr):
    def body(c, carry):
        sl = pl.ds(pl.multiple_of(c * NORM_ROWS, NORM_ROWS), NORM_ROWS)
        kn_scr[sl, :] = _pair_rms(k_ref[sl, :], kg_ref[...]).astype(BF16)
        return carry

    lax.fori_loop(0, SEQ // NORM_ROWS, body, 0)
    ckn_scr[...] = _pair_rms(ck_ref[...], kg_ref[...]).astype(BF16)


def _values_with_ones(v_ref, cv_ref, v1_scr, cv1_scr):
    for a, mine in enumerate(_head_lanes()):
        def body(c, carry):
            sl = pl.ds(pl.multiple_of(c * NORM_ROWS, NORM_ROWS), NORM_ROWS)
            v1_scr[a, sl, :] = jnp.where(mine, v_ref[sl, :], 1.0).astype(BF16)
            return carry

        lax.fori_loop(0, SEQ // NORM_ROWS, body, 0)
        cv1_scr[a] = jnp.where(mine, cv_ref[...], 1.0).astype(BF16)


def _pair_major_spec():
    return pl.BlockSpec((1, ATTN_ROWS, 128), lambda p, i: (p, i, 0))


def _normed_key_specs():
    return [pl.BlockSpec((None, SEQ, 128), lambda p, i: (p, 0, 0)), pl.BlockSpec((None, CTX, 128), lambda p, i: (p, 0, 0))]


def attn_fwd(z, zc, bias, qg2, kg2):
    def kern(q_ref, k_ref, v_ref, bg_ref, ck_ref, cv_ref, bias_ref, qg_ref, kg_ref,
             ob_ref, o_ref, rden_ref, pl_ref, pc_ref, kn_ref, ckn_ref, kn_scr, ckn_scr, v1_scr, cv1_scr, s_scr):
        i = pl.program_id(1)

        @pl.when(i == 0)
        def _():
            _norm_keys(k_ref, ck_ref, kg_ref, kn_scr, ckn_scr)
            kn_ref[...] = kn_scr[...]
            ckn_ref[...] = ckn_scr[...]
            _values_with_ones(v_ref, cv_ref, v1_scr, cv1_scr)

        heads = _head_lanes()
        tiles = [(b, a) for b in range(ATTN_BLOCKS) for a in range(2)]
        rows = [slice(b * QBLK, (b + 1) * QBLK) for b in range(ATTN_BLOCKS)]
        variant = [_bias_variant(i, b) for b in range(ATTN_BLOCKS)]
        pv = [None] * len(tiles)
        qa, done = {}, {}
        latent = KBLK // KCOLS
        buf = lambda t: t % TILE_BUFFERS

        def keys(b, n):
            return pl.ds(pl.multiple_of(_kstart(ATTN_BLOCKS * i + b) + n * KCOLS, KCOLS), KCOLS)

        def score_piece(t, n):
            b, a = tiles[t]
            cols = slice(n * KCOLS, (n + 1) * KCOLS)
            if n == 0:
                if a == 0:
                    done["qn", b] = _scaled_q(q_ref[rows[b], :], qg_ref[...])
                qa[t] = jnp.where(heads[a], done["qn", b], 0.0).astype(BF16)
            if n < latent:
                s_scr[buf(t), :, cols] = mm_nt(qa[t], kn_scr[keys(b, n), :]) + bias_ref[variant[b], a, :, cols]
            else:
                s_scr[buf(t), :, cols] = mm_nt(qa[t], ckn_scr[...])

        def softmax_rows(t, r):
            b, a = tiles[t]
            rs = slice(r * SOFTMAX_ROWS, (r + 1) * SOFTMAX_ROWS)
            out_rows = slice(b * QBLK + rs.start, b * QBLK + rs.stop)
            s = s_scr[buf(t), rs, :]
            p = jnp.exp(s - jnp.max(s, axis=-1, keepdims=True)).astype(BF16)
            pl_ref[a, out_rows, :] = p[:, :KBLK]
            pc_ref[a, out_rows, :] = p[:, KBLK:]

        def value_piece(t, n):
            b, a = tiles[t]
            if n < latent:
                part = mm(pl_ref[a, rows[b], n * KCOLS:(n + 1) * KCOLS], v1_scr[a, keys(b, n), :])
            else:
                part = mm(pc_ref[a, rows[b], :], cv1_scr[a])
            pv[t] = part if pv[t] is None else pv[t] + part
            if n == latent:
                finish(t)

        def finish(t):
            b, a = tiles[t]
            r = jnp.where(heads[a], pltpu.roll(1.0 / pv[t], HDIM, 1), 0.0)
            done[t] = (pv[t] * r, r)
            if a == 1:
                o, rden = (lo + hi for lo, hi in zip(done[t - 1], done[t]))
                ob_ref[rows[b], :] = o * jax.nn.silu(bg_ref[rows[b], :])
                o_ref[0, rows[b], :] = o
                rden_ref[0, rows[b], :] = rden

        pieces = range(latent + 1)
        for n in pieces:
            score_piece(0, n)
        for t in range(len(tiles)):
            matmuls = []
            for n in pieces:
                if t + 1 < len(tiles):
                    matmuls.append(functools.partial(score_piece, t + 1, n))
                if t > 0:
                    matmuls.append(functools.partial(value_piece, t - 1, n))
            _emit_interleaved([functools.partial(softmax_rows, t, r) for r in range(QBLK // SOFTMAX_ROWS)], matmuls)
        for n in pieces:
            value_piece(len(tiles) - 1, n)

    qblk = pl.BlockSpec((ATTN_ROWS, 128), lambda p, i: (i, p))
    return pl.pallas_call(
        kern, name="attn_fwd", grid=(NPAIR, ATTN_STEPS),
        in_specs=_attn_in_specs() + _bias_specs() + [_row(128), _row(128)],
        out_specs=[qblk, _pair_major_spec(), _pair_major_spec()] + _prob_specs() + _normed_key_specs(),
        out_shape=[jax.ShapeDtypeStruct((SEQ, 512), F32)] + [jax.ShapeDtypeStruct((NPAIR, SEQ, 128), F32)] * 2
        + [jax.ShapeDtypeStruct((HEADS, SEQ, KBLK), BF16), jax.ShapeDtypeStruct((HEADS, SEQ, CTX), BF16),
           jax.ShapeDtypeStruct((NPAIR, SEQ, 128), BF16), jax.ShapeDtypeStruct((NPAIR, CTX, 128), BF16)],
        scratch_shapes=[pltpu.VMEM((SEQ, 128), BF16), pltpu.VMEM((CTX, 128), BF16),
                        pltpu.VMEM((2, SEQ, 128), BF16), pltpu.VMEM((2, CTX, 128), BF16),
                        pltpu.VMEM((TILE_BUFFERS, QBLK, KBLK + CTX), F32)],
        compiler_params=_cparams(("arbitrary", "arbitrary"), VMEM_BIG),
    )(z, z, z, z, zc, zc, bias, qg2, kg2)


def attn_bwd(z, zc, qg2, kg2, dcat, saved):
    def kern(q_ref, k_ref, v_ref, bg_ref, ck_ref, cv_ref, qg_ref, kg_ref, do_ref, o_ref, rden_ref, pl_ref, pc_ref,
             kn_scr, ckn_scr, dq_ref, dk_ref, dv_ref, dbg_ref, dck_ref, dcv_ref, db_ref, dqg_ref, dkg_ref,
             v_scr, cv_scr, dknt_scr, dvt_scr, dcknt_scr, dcvt_scr, dp_scr, ds_scr):
        p, i = pl.program_id(0), pl.program_id(1)
        last = i == ATTN_STEPS - 1

        @pl.when(i == 0)
        def _():
            def body(c, carry):
                sl = pl.ds(pl.multiple_of(c * NORM_ROWS, NORM_ROWS), NORM_ROWS)
                v_scr[sl, :] = v_ref[sl, :].astype(BF16)
                return carry

            lax.fori_loop(0, SEQ // NORM_ROWS, body, 0)
            cv_scr[...] = cv_ref[...].astype(BF16)
            for acc in (dknt_scr, dvt_scr, dcknt_scr, dcvt_scr, db_ref):
                acc[...] = jnp.zeros_like(acc)

        @pl.when((i == 0) & (p == 0))
        def _():
            dqg_ref[...] = jnp.zeros_like(dqg_ref)
            dkg_ref[...] = jnp.zeros_like(dkg_ref)

        heads = _head_lanes()
        tiles = [(b, a) for b in range(ATTN_BLOCKS) for a in range(2)]
        rows = [slice(b * QBLK, (b + 1) * QBLK) for b in range(ATTN_BLOCKS)]
        kb = [_kblock(ATTN_BLOCKS * i + b) for b in range(ATTN_BLOCKS)]
        variant = [_bias_variant(i, b) for b in range(ATTN_BLOCKS)]
        latent = KBLK // KCOLS
        buf = lambda t: t % TILE_BUFFERS

        def keys(b, n):
            return pl.ds(pl.multiple_of((kb[b] + n) * KCOLS, KCOLS), KCOLS)

        gated = {}

        def gate_backward(b):
            bg, dout, o = bg_ref[rows[b], :], do_ref[rows[b], :], o_ref[0, rows[b], :]
            sig = jax.nn.sigmoid(bg)
            do = dout * (bg * sig)
            dbg_ref[rows[b], :] = (dout * o * (sig * (1.0 + bg * (1.0 - sig)))).astype(BF16)
            rden = rden_ref[0, rows[b], :]
            dr = do * rden
            qn = _scaled_q(q_ref[rows[b], :], qg_ref[...])
            gated[b] = (dr, dr.T.astype(BF16), qn.T.astype(BF16), do * o * rden)

        feats = [slice(a * HDIM, (a + 1) * HDIM) for a in range(2)]
        doa, doa_t, qa_t, delta = {}, {}, {}, {}
        dqn = [None] * len(tiles)

        def cols(n):
            return slice(n * KCOLS, (n + 1) * KCOLS)

        def stage_a(t, n):
            b, a = tiles[t]
            if n == 0:
                if a == 0:
                    gate_backward(b)
                dr, dr_t, qn_t, weighted = gated[b]
                doa[t] = jnp.where(heads[a], dr, 0.0).astype(BF16)
                doa_t[t] = dr_t[feats[a], :]
                qa_t[t] = qn_t[feats[a], :]
                delta[t] = jnp.sum(jnp.where(heads[a], weighted, 0.0), axis=-1, keepdims=True)
            if n < latent:
                dp_scr[buf(t), :, cols(n)] = mm_nt(doa[t], v_scr[keys(b, n), :])
                dvt_scr[kb[b] + n, feats[a], :] += mm(doa_t[t], pl_ref[a, rows[b], cols(n)])
            else:
                dp_scr[buf(t), :, cols(n)] = mm_nt(doa[t], cv_scr[...])
                dcvt_scr[feats[a], :] += mm(doa_t[t], pc_ref[a, rows[b], :])

        def stage_b(t, r):
            b, a = tiles[t]
            rs = slice(r * SOFTMAX_ROWS, (r + 1) * SOFTMAX_ROWS)
            in_rows = slice(b * QBLK + rs.start, b * QBLK + rs.stop)
            d = dp_scr[buf(t), rs, :] - delta[t][rs, :]
            ds_lat = pl_ref[a, in_rows, :].astype(F32) * d[:, :KBLK]
            ds_ctx = pc_ref[a, in_rows, :].astype(F32) * d[:, KBLK:]
            db_ref[variant[b], a, rs, :] += ds_lat
            ds_scr[buf(t), rs, :KBLK] = ds_lat.astype(BF16)
            ds_scr[buf(t), rs, KBLK:] = ds_ctx.astype(BF16)

        def stage_c(t, n):
            b, a = tiles[t]
            ds = ds_scr[buf(t), :, cols(n)]
            if n < latent:
                part = mm(ds, kn_scr[keys(b, n), :])
                dknt_scr[kb[b] + n, feats[a], :] += mm(qa_t[t], ds)
            else:
                part = mm(ds, ckn_scr[...])
                dcknt_scr[feats[a], :] += mm(qa_t[t], ds)
            dqn[t] = part if dqn[t] is None else dqn[t] + part
            if n == latent and a == 1:
                both = jnp.where(heads[0], dqn[t - 1], 0.0) + jnp.where(heads[1], dqn[t], 0.0)
                dq, dqg = jax.vjp(_scaled_q, q_ref[rows[b], :], qg_ref[...])[1](both)
                dq_ref[rows[b], :] = dq.astype(BF16)
                dqg_ref[...] += dqg

        pieces = range(latent + 1)
        for n in pieces:
            stage_a(0, n)
        for t in range(len(tiles)):
            matmuls = []
            for n in pieces:
                if t + 1 < len(tiles):
                    matmuls.append(functools.partial(stage_a, t + 1, n))
                if t > 0:
                    matmuls.append(functools.partial(stage_c, t - 1, n))
            _emit_interleaved([functools.partial(stage_b, t, r) for r in range(QBLK // SOFTMAX_ROWS)], matmuls)
        for n in pieces:
            stage_c(len(tiles) - 1, n)

        @pl.when(last)
        def _():
            eye = (lax.broadcasted_iota(jnp.int32, (KCOLS, KCOLS), 0)
                   == lax.broadcasted_iota(jnp.int32, (KCOLS, KCOLS), 1)).astype(BF16)

            def turned(x):
                hi = x.astype(BF16)
                return mm_nt(eye, hi) + mm_nt(eye, x - hi.astype(F32))

            def body(c, dkg):
                sl = pl.ds(pl.multiple_of(c * NORM_ROWS, NORM_ROWS), NORM_ROWS)
                blocks = range(NORM_ROWS // KCOLS)
                dkn = jnp.concatenate([turned(dknt_scr[c * len(blocks) + n]) for n in blocks], axis=0)
                dv = jnp.concatenate([mm_nt(eye, dvt_scr[c * len(blocks) + n]) for n in blocks], axis=0)
                dk, dg = _pair_rms_bwd(k_ref[sl, :], kg_ref[...], dkn)
                dk_ref[sl, :] = dk.astype(BF16)
                dv_ref[sl, :] = dv.astype(BF16)
                return dkg + dg

            dkg = lax.fori_loop(0, SEQ // NORM_ROWS, body, jnp.zeros((1, 128), F32))
            dck, dg = _pair_rms_bwd(ck_ref[...], kg_ref[...], dcknt_scr[...].T)
            dck_ref[...] = dck
            dcv_ref[...] = dcvt_scr[...].T
            dkg_ref[...] += dkg + dg

        @pl.when(last & (p == NPAIR - 1))
        def _():
            dqg_ref[...] = dqg_ref[...] + pltpu.roll(dqg_ref[...], HDIM, 1)
            dkg_ref[...] = dkg_ref[...] + pltpu.roll(dkg_ref[...], HDIM, 1)

    blk = lambda rows: pl.BlockSpec((rows, 128), lambda p, i: (0, p))
    qblk = pl.BlockSpec((ATTN_ROWS, 128), lambda p, i: (i, p))
    return pl.pallas_call(
        kern, name="attn_bwd", grid=(NPAIR, ATTN_STEPS),
        in_specs=_attn_in_specs() + [_row(128), _row(128), pl.BlockSpec((ATTN_ROWS, 128), lambda p, i: (i, 4 + p)),
                                     _pair_major_spec(), _pair_major_spec()] + _prob_specs() + _normed_key_specs(),
        out_specs=[qblk, blk(SEQ), blk(SEQ), qblk, blk(CTX), blk(CTX),
                   pl.BlockSpec((3, 2, QBLK, KBLK), lambda p, i: (0, p, 0, 0)), _row(128), _row(128)],
        out_shape=[jax.ShapeDtypeStruct((SEQ, 512), BF16)] * 4 + [jax.ShapeDtypeStruct((CTX, 512), F32)] * 2
        + [jax.ShapeDtypeStruct((3, HEADS, QBLK, KBLK), F32)]
        + [jax.ShapeDtypeStruct((1, 128), F32), jax.ShapeDtypeStruct((1, 128), F32)],
        scratch_shapes=[pltpu.VMEM((SEQ, 128), BF16), pltpu.VMEM((CTX, 128), BF16),
                        pltpu.VMEM((SEQ // KCOLS, 128, KCOLS), F32), pltpu.VMEM((SEQ // KCOLS, 128, KCOLS), F32),
                        pltpu.VMEM((128, CTX), F32), pltpu.VMEM((128, CTX), F32),
                        pltpu.VMEM((TILE_BUFFERS, QBLK, KBLK + CTX), F32),
                        pltpu.VMEM((TILE_BUFFERS, QBLK, KBLK + CTX), BF16)],
        compiler_params=_cparams(("arbitrary", "arbitrary"), VMEM_BIG),
    )(z, z, z, z, zc, zc, qg2, kg2, dcat, *saved)


def outproj(out_a, out_b, x, target, gate, wo):
    tl = 512

    def kern(a_ref, b_ref, x_ref, t_ref, g_ref, w_ref, loss_ref, dy_ref, dcat_ref, dg_ref, dw_ref):
        @pl.when(pl.program_id(0) == 0)
        def _():
            loss_ref[...] = jnp.zeros_like(loss_ref)
            dg_ref[...] = jnp.zeros_like(dg_ref)
            dw_ref[...] = jnp.zeros_like(dw_ref)

        a, b = a_ref[...].astype(BF16), b_ref[...].astype(BF16)
        mix = (jnp.dot(a, w_ref[0:512, :], preferred_element_type=F32)
               + jnp.dot(b, w_ref[512:1024, :], preferred_element_type=F32))
        err = x_ref[...] + g_ref[...] * mix - t_ref[...]
        loss_ref[...] += 0.5 * jnp.sum(jnp.mean(err * err, axis=-1))
        dy = err * (1.0 / DM)
        dy_ref[...] = dy
        dg_ref[...] += jnp.sum(dy * mix, axis=0, keepdims=True)
        dmix = (g_ref[...] * dy).astype(BF16)
        dcat_ref[...] = lax.dot_general(dmix, w_ref[...], (((1,), (1,)), ((), ())), preferred_element_type=F32)
        dw_ref[0:512, :] += lax.dot_general(a, dmix, (((0,), (0,)), ((), ())), preferred_element_type=F32)
        dw_ref[512:1024, :] += lax.dot_general(b, dmix, (((0,), (0,)), ((), ())), preferred_element_type=F32)

    tile = lambda w: pl.BlockSpec((tl, w), lambda t: (t, 0))
    whole = pl.BlockSpec((DM, DM), lambda t: (0, 0))
    return pl.pallas_call(
        kern, name="outproj", grid=(SEQ // tl,),
        in_specs=[tile(512), tile(512), tile(DM), tile(DM), _row(DM), whole],
        out_specs=[pl.BlockSpec((8, 128), lambda t: (0, 0)), tile(DM), tile(DM), _row(DM), whole],
        out_shape=[jax.ShapeDtypeStruct((8, 128), F32), jax.ShapeDtypeStruct((SEQ, DM), F32),
                   jax.ShapeDtypeStruct((SEQ, DM), F32), jax.ShapeDtypeStruct((1, DM), F32),
                   jax.ShapeDtypeStruct((DM, DM), F32)],
        compiler_params=_cparams(("arbitrary",), 48 * 1024 * 1024),
    )(out_a, out_b, x, target, gate, wo)


DZ_COLS = (("a", 0, 1536), ("q", 1536, 2048), ("k", 2048, 2560), ("v", 2560, 3072), ("g", 3072, DIN))
DZC_COLS = (("k", 2048, 2560), ("v", 2560, 3072))
_NT = (((1,), (1,)), ((), ()))


DH_SUBTILES = 2


def _dz_specs(tl):
    return [pl.BlockSpec((tl, 1536), lambda t: (t, 0))] + [pl.BlockSpec((tl, 512), lambda t: (t, 0))] * 4


def dh_bwd(dz_parts, w_full, x, dy, shift, scale, norm_g, dg_ctx):
    tl = 512
    nt = SEQ // tl

    def kern(a_ref, q_ref, k_ref, v_ref, g_ref, w_ref, x_ref, dy_ref, sh_ref, sc_ref, gn_ref, dgc_ref,
             gx_ref, dsh_ref, dsc_ref, dg_ref):
        @pl.when(pl.program_id(0) == 0)
        def _():
            dsh_ref[...] = jnp.zeros_like(dsh_ref)
            dsc_ref[...] = jnp.zeros_like(dsc_ref)
            dg_ref[...] = dgc_ref[...]

        src = dict(a=a_ref, q=q_ref, k=k_ref, v=v_ref, g=g_ref)
        for sub in range(DH_SUBTILES):
            rows = slice(sub * tl // DH_SUBTILES, (sub + 1) * tl // DH_SUBTILES)
            dh = None
            for name, c0, c1 in DZ_COLS:
                part = lax.dot_general(src[name][rows, :], w_ref[:, c0:c1], _NT, preferred_element_type=F32)
                dh = part if dh is None else dh + part
            _, vjp = jax.vjp(_modulated, x_ref[rows, :], gn_ref[...], sc_ref[...], sh_ref[...])
            dx, dg, dsc, dsh = vjp(dh)
            gx_ref[rows, :] = dy_ref[rows, :] + dx
            dg_ref[...] += dg
            dsc_ref[...] += dsc
            dsh_ref[...] += dsh

    tile = pl.BlockSpec((tl, DM), lambda t: (t, 0))
    return pl.pallas_call(
        kern, name="dh_bwd", grid=(nt,),
        in_specs=_dz_specs(tl) + [pl.BlockSpec((DM, DIN), lambda t: (0, 0)), tile, tile, _row(DM),
                                  _row(DM), _row(DM), _row(DM)],
        out_specs=[tile, _row(DM), _row(DM), _row(DM)],
        out_shape=[jax.ShapeDtypeStruct((SEQ, DM), F32)] + [jax.ShapeDtypeStruct((1, DM), F32)] * 3,
        compiler_params=_cparams(("arbitrary",), 48 * 1024 * 1024),
    )(*dz_parts, w_full, x, dy, shift, scale, norm_g, dg_ctx)


def dw_bwd(h, dz_parts, hc, dck, dcv, g_out):
    tl = 512
    nt = SEQ // tl
    (rhi, wi), (rho, wo) = RS_SHAPES

    def kern(h_ref, a_ref, q_ref, k_ref, v_ref, g_ref, hc_ref, dck_ref, dcv_ref, go_hbm,
             wire_i, keep_i, wire_o, keep_o, acc, snd_i, rcv_i, mine_o, rcv_o, load_sem, send_sems, recv_sems):
        t = pl.program_id(0)
        x, y, c = _me()
        k = 2 * x + y
        sib = _flip(1)
        half = lambda hh, rh: pl.ds(pl.multiple_of(hh * rh, rh), rh)
        load_o = pltpu.make_async_copy(go_hbm.at[:, half(c, rho), :], mine_o, load_sem)
        pair_o = _rcopy(go_hbm.at[:, half(1 - c, rho), :], rcv_o, send_sems, recv_sems, 0, sib)
        pair_i = [_rcopy(snd_i.at[j], rcv_i.at[j], send_sems, recv_sems, 1 + j, sib) for j in range(NCHIP)]

        @pl.when(t == 0)
        def _():
            load_o.start()
            pair_o.start()
            acc[...] = jnp.zeros_like(acc)
            hct = hc_ref[...].T
            csrc = dict(k=dck_ref, v=dcv_ref)
            for name, c0, c1 in DZC_COLS:
                acc[:, c0:c1] += jnp.dot(hct, csrc[name][...].astype(BF16), preferred_element_type=F32)

        ht = h_ref[...].T
        src = dict(a=a_ref, q=q_ref, k=k_ref, v=v_ref, g=g_ref)
        for name, c0, c1 in DZ_COLS:
            acc[:, c0:c1] += jnp.dot(ht, src[name][...], preferred_element_type=F32)

        @pl.when(t == nt - 1)
        def _():
            shard = lambda j: slice(j * SHARD_IN, (j + 1) * SHARD_IN)
            for j in range(NCHIP):
                snd_i[j] = acc[half(1 - c, rhi), shard(j)].astype(BF16)
                pair_i[j].start()
            load_o.wait()
            pair_o.wait_recv()
            for j in range(NCHIP):
                wire_o[j] = (mine_o[j] + rcv_o[j]).astype(BF16)
            keep_o[...] = mine_o[k] + rcv_o[k]
            mine = half(c, rhi)
            for j in range(NCHIP):
                pair_i[j].wait_recv()
                pair_sum = acc[mine, shard(j)] + rcv_i[j].astype(F32)
                wire_i[j] = pair_sum.astype(BF16)

                @pl.when(k == j)
                def _():
                    keep_i[...] = pair_sum
            pair_o.wait_send()
            for j in range(NCHIP):
                pair_i[j].wait_send()

    whole = lambda *shape: pl.BlockSpec(shape, lambda t: (0,) * len(shape))
    return pl.pallas_call(
        kern, name="dw_bwd", grid=(nt,),
        in_specs=[pl.BlockSpec((tl, DM), lambda t: (t, 0))] + _dz_specs(tl)
        + [whole(CTX, DM), whole(CTX, 512), whole(CTX, 512), pl.BlockSpec(memory_space=pl.ANY)],
        out_specs=[whole(NCHIP, rhi, wi), whole(rhi, wi), whole(NCHIP, rho, wo), whole(rho, wo)],
        out_shape=[jax.ShapeDtypeStruct((NCHIP, rhi, wi), BF16), jax.ShapeDtypeStruct((rhi, wi), F32),
                   jax.ShapeDtypeStruct((NCHIP, rho, wo), BF16), jax.ShapeDtypeStruct((rho, wo), F32)],
        scratch_shapes=[pltpu.VMEM((DM, DIN), F32), pltpu.VMEM((NCHIP, rhi, wi), BF16),
                        pltpu.VMEM((NCHIP, rhi, wi), BF16),
                        pltpu.VMEM((NCHIP, rho, wo), F32), pltpu.VMEM((NCHIP, rho, wo), F32),
                        pltpu.SemaphoreType.DMA(()), pltpu.SemaphoreType.DMA((1 + NCHIP,)),
                        pltpu.SemaphoreType.DMA((1 + NCHIP,))],
        compiler_params=_cparams(("arbitrary",), VMEM_BIG),
    )(h, *dz_parts, hc, dck, dcv, g_out)


def ctx_bwd(dck, dcv, w_full, ctx, cshift, cscale, norm_g):
    def kern(dck_ref, dcv_ref, w_ref, c_ref, sh_ref, sc_ref, g_ref, dsh_ref, dsc_ref, dg_ref):
        csrc = dict(k=dck_ref, v=dcv_ref)
        dhc = None
        first = DZC_COLS[0][1]
        for name, c0, c1 in DZC_COLS:
            part = lax.dot_general(csrc[name][...].astype(BF16), w_ref[:, c0 - first:c1 - first], _NT,
                                   preferred_element_type=F32)
            dhc = part if dhc is None else dhc + part
        _, vjp = jax.vjp(lambda g, sc, sh: _modulated(c_ref[...], g, sc, sh), g_ref[...], sc_ref[...], sh_ref[...])
        dg_ref[...], dsc_ref[...], dsh_ref[...] = vjp(dhc)

    whole = lambda r, c: pl.BlockSpec((r, c), lambda i: (0, 0))
    return pl.pallas_call(
        kern, name="ctx_bwd", grid=(1,),
        in_specs=[whole(CTX, 512), whole(CTX, 512), pl.BlockSpec((DM, 1024), lambda i: (0, DZC_COLS[0][1] // 1024)),
                  whole(CTX, DM), _row(DM), _row(DM), _row(DM)],
        out_specs=[_row(DM), _row(DM), _row(DM)],
        out_shape=[jax.ShapeDtypeStruct((1, DM), F32)] * 3,
        compiler_params=_cparams(("arbitrary",), 40 * 1024 * 1024),
    )(dck, dcv, w_full, ctx, cshift, cscale, norm_g)


def _lane_pad_rpb(rpb):
    r = jnp.pad(rpb, ((0, 0), (0, 0), (0, GRID_W - rpb.shape[-1])))
    return jnp.concatenate([r, r], axis=-1)


def local_step(chip, dev, x, c_vec, c_ctx, w_ada, b_shard, ctx, target, norm_g, sgu_g, w_s, b_s, q_g, k_g, rpb,
               w_in_shard, w_out_shard):
    bsb = jnp.broadcast_to(b_s[:, :, None], (4, 128, 128))
    qg2, kg2 = jnp.tile(q_g, (1, 2)), jnp.tile(k_g, (1, 2))

    z, h, w_in_full, w_out_full, mod_all, cs = inproj_fwd(chip, x, c_vec, c_ctx, w_ada, b_shard, norm_g, w_in_shard,
                                                          w_out_shard)
    mods = mod_all.transpose(1, 0, 2).reshape(CS_ROWS, 3 * DM)
    mod = lax.dynamic_slice(mods, (8 * dev, 0), (1, 3 * DM))
    shift, scale, gate = mod[:, :DM], mod[:, DM:2 * DM], mod[:, 2 * DM:]
    cshift, cscale = mods[8 * NDEV:8 * NDEV + 1, :DM], mods[8 * NDEV:8 * NDEV + 1, DM:2 * DM]
    zc, hc = ctx_fwd(ctx, cshift, cscale, norm_g, w_in_full)
    bias = rpb_tables(_lane_pad_rpb(rpb))
    out_a = sgu_fwd(z, sgu_g, w_s, bsb)
    out_b, *saved = attn_fwd(z, zc, bias, qg2, kg2)
    loss8, dy, dcat, dgate, dwo = outproj(out_a, out_b, x, target, gate, w_out_full.reshape(DM, DM))
    dz_a, dsg, dws, dbsb = sgu_bwd(z, sgu_g, w_s, bsb, dcat)
    dq, dk, dv, dbg, dck, dcv, dbias, dqg2, dkg2 = attn_bwd(z, zc, qg2, kg2, dcat, saved)
    drpb = rpb_bwd(dbias)[:, :, :rpb.shape[-1]]
    dz_parts = (dz_a, dq, dk, dv, dbg)
    dcshift, dcscale, dng_c = ctx_bwd(dck, dcv, w_in_full, ctx, cshift, cscale, norm_g)
    wire_i, keep_i, wire_o, keep_o = dw_bwd(h, dz_parts, hc, dck, dcv, dwo.reshape(NCHIP, SHARD_OUT, DM))
    *in_flight, token = rs_start(wire_i, wire_o)
    grad_x, dshift, dscale, dng = dh_bwd(dz_parts, w_in_full, x, dy, shift, scale, norm_g, dng_c + token[0, 0])
    got_i, got_o = rs_wait(*in_flight, dshift)
    return dict(
        loss=loss8[0:1, 0:1], grad_x=grad_x, rs=(keep_i, got_i, keep_o, got_o), cs=cs,
        dmod=jnp.concatenate([dshift, dscale, dgate], axis=-1),
        dcmod=jnp.concatenate([dcshift, dcscale, jnp.zeros((1, DM), F32)], axis=-1),
        d_norm_g=dng, d_sgu_g=dsg, d_w_s=dws, d_b_s=dbsb[:, :, 0],
        d_q_g=dqg2[:, :HDIM], d_k_g=dkg2[:, :HDIM], d_rpb=drpb)


def _me():
    return lax.axis_index("x"), lax.axis_index("y"), lax.axis_index("c")


def _flip(q):
    x, y, c = _me()
    return ((1 - x) if q & 4 else x, (1 - y) if q & 2 else y, (1 - c) if q & 1 else c)


def _chip_of(dev):
    return 2 * dev[0] + dev[1]


def _rcopy(src, dst, send_sems, recv_sems, k, dev):
    return pltpu.make_async_remote_copy(src_ref=src, dst_ref=dst, send_sem=send_sems.at[k], recv_sem=recv_sems.at[k],
                                        device_id=dev, device_id_type=MESH_ID)


_VMEM_SPEC = pl.BlockSpec(memory_space=pltpu.VMEM)
SLAB_ROWS = 80


RS_SHAPES = ((DM // 2, SHARD_IN), (SHARD_OUT // 2, DM))
_HBM_SPEC = pl.BlockSpec(memory_space=pltpu.HBM)
_SEM_SPEC = pl.BlockSpec(memory_space=pltpu.SEMAPHORE)
_IN_FLIGHT = pltpu.SideEffectType.DATAFLOW_SIDE_EFFECTING


def _rs_copies(wires, lands, send_sems, recv_sems):
    return [pltpu.make_async_remote_copy(
        src_ref=wires[n].at[_chip_of(_flip(q))], dst_ref=lands[n].at[q // 2 - 1],
        send_sem=send_sems.at[3 * n + q // 2 - 1], recv_sem=recv_sems.at[3 * n + q // 2 - 1],
        device_id=_flip(q), device_id_type=MESH_ID) for n in (0, 1) for q in (2, 4, 6)]


def rs_start(wire_i, wire_o):
    lands = [lax.empty((NCHIP - 1, rh, w), BF16) for rh, w in RS_SHAPES]

    def body(wi_ref, wo_ref, li_ref, lo_ref, send_sems, recv_sems, wi_thru, wo_thru, li_thru, lo_thru, token):
        for cp in _rs_copies((wi_ref, wo_ref), (li_ref, lo_ref), send_sems, recv_sems):
            cp.start()
        token[...] = jnp.zeros_like(token)

    hbm = lambda a: pltpu.HBM(a.shape, a.dtype)
    return pl.pallas_call(
        body, name="rs_start",
        out_shape=(pltpu.SemaphoreType.DMA((6,)), pltpu.SemaphoreType.DMA((6,)), hbm(wire_i), hbm(wire_o),
                   hbm(lands[0]), hbm(lands[1]), jax.ShapeDtypeStruct((8, 128), F32)),
        in_specs=(_HBM_SPEC,) * 4, out_specs=(_SEM_SPEC, _SEM_SPEC) + (_HBM_SPEC,) * 4 + (_VMEM_SPEC,),
        input_output_aliases={0: 2, 1: 3, 2: 4, 3: 5},
        compiler_params=pltpu.CompilerParams(has_side_effects=_IN_FLIGHT),
    )(*[pltpu.with_memory_space_constraint(a, pltpu.HBM) for a in (wire_i, wire_o, *lands)])


def rs_wait(send_sems, recv_sems, wire_i, wire_o, land_i, land_o, after):
    def body(wi_ref, wo_ref, li_ref, lo_ref, send_sems, recv_sems, after_ref, wi_dead, wo_dead, gi_ref, go_ref):
        for cp in _rs_copies((wi_ref, wo_ref), (li_ref, lo_ref), send_sems, recv_sems):
            cp.wait_send()
            cp.wait_recv()

    hbm = lambda a: pltpu.HBM(a.shape, a.dtype)
    return pl.pallas_call(
        body, name="rs_wait", out_shape=(hbm(wire_i), hbm(wire_o), hbm(land_i), hbm(land_o)),
        in_specs=(_HBM_SPEC,) * 4 + (_SEM_SPEC, _SEM_SPEC, pl.BlockSpec(memory_space=pl.ANY)),
        out_specs=(_HBM_SPEC,) * 4, input_output_aliases={0: 0, 1: 1, 2: 2, 3: 3},
        compiler_params=pltpu.CompilerParams(has_side_effects=_IN_FLIGHT),
    )(wire_i, wire_o, land_i, land_o, send_sems, recv_sems, after)[2:]


def final_reduce(keep_i, got_i, keep_o, got_o, slab, cs, w_ada, c_ctx):
    (rhi, wi), (rho, wo) = RS_SHAPES

    def kern(ki_hbm, gi_hbm, ko_hbm, go_hbm, s_ref, cs_ref, w_hbm, cc_ref,
             gin_ref, gout_ref, tot_ref, dw_ref, db_ref, dcc_ref,
             ki, gi, ko, go, w_scr, all_ref, dms_scr, parts, load_sems, send_sems, recv_sems):
        x, y, c = _me()
        k = 2 * x + y
        sib = _flip(1)
        dev = lambda d: 4 * d[0] + 2 * d[1] + d[2]
        me = dev((x, y, c))

        def slab_copy(idx, owner, to):
            return _rcopy(all_ref.at[dev(owner)], all_ref.at[dev(owner)], send_sems, recv_sems, idx, to)

        all_ref[me] = s_ref[...]
        first = [slab_copy(0, (x, y, c), sib)] + [slab_copy(q // 2, (x, y, c), _flip(q)) for q in (2, 4, 6)]
        for cp in first:
            cp.start()
        loads = [pltpu.make_async_copy(src, dst, load_sems.at[n]) for n, (src, dst) in enumerate(
            ((ki_hbm, ki), (gi_hbm, gi), (ko_hbm, ko), (go_hbm, go), (w_hbm, w_scr)))]
        for cp in loads:
            cp.start()

        shares = []
        for n, (keep, got, out) in enumerate(((ki, gi, gin_ref), (ko, go, gout_ref))):
            rh = RS_SHAPES[n][0]
            half = lambda hh, rh=rh: pl.ds(pl.multiple_of(hh * rh, rh), rh)
            loads[2 * n].wait()
            loads[2 * n + 1].wait()
            out[half(c), :] = ((keep[...] + got[0].astype(F32)) + got[1].astype(F32)) + got[2].astype(F32)
            share = _rcopy(out.at[half(c), :], out.at[half(c), :], send_sems, recv_sems, 7 + n, sib)
            share.start()
            shares.append((share, _rcopy(out.at[half(1 - c), :], out.at[half(1 - c), :], send_sems, recv_sems, 7 + n,
                                         sib)))

        passed = []
        for q in (2, 4, 6):
            slab_copy(q // 2, _flip(q), (x, y, c)).wait_recv()
            cp = slab_copy(3 + q // 2, _flip(q), sib)
            cp.start()
            passed.append(cp)
        slab_copy(0, sib, (x, y, c)).wait_recv()
        for q in (2, 4, 6):
            slab_copy(3 + q // 2, _flip(q | 1), (x, y, c)).wait_recv()
        tot = all_ref[0]
        for d in range(1, NDEV):
            tot = tot + all_ref[d]
        tot_ref[...] = tot

        pad = jnp.zeros((7, DM), F32)
        dm = [jnp.concatenate([all_ref[d, 12 + j:13 + j, :] for d in range(NDEV)] + [tot[9 + j:10 + j, :], pad], axis=0)
              for j in range(3)]
        db_ref[...] = jnp.concatenate([jnp.sum(part, axis=0, keepdims=True) for part in dm], axis=0)
        dm = jnp.concatenate(dm, axis=-1)
        for j in range(NCHIP):
            @pl.when(k == j)
            def _():
                dms_scr[...] = dm[:, j * SHARD_ADA:(j + 1) * SHARD_ADA].astype(BF16)

        a_in = jnp.concatenate([cs_ref[8 * d:8 * d + 1, :] for d in range(NDEV)]
                               + [cs_ref[8 * NDEV:8 * NDEV + 1, :], pad], axis=0)
        act = jax.nn.silu(a_in).astype(BF16)
        dms = dms_scr[...]
        dw_ref[...] = lax.dot_general(act, dms, (((0,), (0,)), ((), ())), preferred_element_type=F32)
        loads[4].wait()
        parts[k] = lax.dot_general(dms, w_scr[...].astype(BF16), (((1,), (1,)), ((), ())), preferred_element_type=F32)
        sends = [_rcopy(parts.at[k], parts.at[k], send_sems, recv_sems, 8 + q // 2, _flip(q)) for q in (2, 4, 6)]
        for cp in sends:
            cp.start()
        for q in (2, 4, 6):
            kq = _chip_of(_flip(q))
            _rcopy(parts.at[kq], parts.at[kq], send_sems, recv_sems, 8 + q // 2, _flip(q)).wait_recv()
        dact = ((parts[0] + parts[1]) + parts[2]) + parts[3]
        _, vjp = jax.vjp(jax.nn.silu, cc_ref[...])
        dcc_ref[...] = vjp(dact[8:9, :])[0]

        for share, arrival in shares:
            arrival.wait_recv()
            share.wait_send()
        for cp in first + passed + sends:
            cp.wait_send()

    any_spec = pl.BlockSpec(memory_space=pl.ANY)
    return pl.pallas_call(
        kern, name="final_reduce",
        in_specs=[any_spec] * 4 + [_VMEM_SPEC, _VMEM_SPEC, any_spec, _VMEM_SPEC], out_specs=[_VMEM_SPEC] * 6,
        out_shape=[jax.ShapeDtypeStruct((2 * rhi, wi), F32), jax.ShapeDtypeStruct((2 * rho, wo), F32),
                   jax.ShapeDtypeStruct((SLAB_ROWS, DM), F32), jax.ShapeDtypeStruct((DM, SHARD_ADA), F32),
                   jax.ShapeDtypeStruct((3, DM), F32), jax.ShapeDtypeStruct((1, DM), F32)],
        scratch_shapes=[pltpu.VMEM((rhi, wi), F32), pltpu.VMEM((NCHIP - 1, rhi, wi), BF16),
                        pltpu.VMEM((rho, wo), F32), pltpu.VMEM((NCHIP - 1, rho, wo), BF16),
                        pltpu.VMEM((DM, SHARD_ADA), F32), pltpu.VMEM((NDEV, SLAB_ROWS, DM), F32),
                        pltpu.VMEM((16, SHARD_ADA), BF16), pltpu.VMEM((NCHIP, 16, DM), F32),
                        pltpu.SemaphoreType.DMA((5,)), pltpu.SemaphoreType.DMA((12,)), pltpu.SemaphoreType.DMA((12,))],
        compiler_params=pltpu.CompilerParams(vmem_limit_bytes=40 * 1024 * 1024),
    )(keep_i, got_i, keep_o, got_o, slab, cs, w_ada, c_ctx)


def _adamw_math(w, g, m, v):
    m = B1 * m + (1.0 - B1) * g
    v = B2 * v + (1.0 - B2) * (g * g)
    m_hat = m / (1.0 - B1 ** STEP)
    v_hat = v / (1.0 - B2 ** STEP)
    return -LR * (m_hat / (jnp.sqrt(v_hat) + ADAM_EPS) + WD * w), m, v


def adamw_big(w, g, m, v, name, block_rows=256):
    rows, width = w.shape

    def kern(w_ref, g_ref, m_ref, v_ref, d_ref, nm_ref, nv_ref):
        d_ref[...], nm_ref[...], nv_ref[...] = _adamw_math(w_ref[...], g_ref[...], m_ref[...], v_ref[...])

    spec = pl.BlockSpec((block_rows, width), lambda i: (i, 0))
    return pl.pallas_call(
        kern, name=name, grid=(rows // block_rows,), in_specs=[spec] * 4, out_specs=[spec] * 3,
        out_shape=[jax.ShapeDtypeStruct((rows, width), F32)] * 3,
        compiler_params=_cparams(("arbitrary",)),
    )(w, g, m, v)


def adamw_small(quads):
    n = len(quads)

    def kern(*refs):
        ins, outs = refs[:4 * n], refs[4 * n:]
        for i in range(n):
            w, g, m, v = (r[...] for r in ins[4 * i:4 * i + 4])
            outs[3 * i][...], outs[3 * i + 1][...], outs[3 * i + 2][...] = _adamw_math(w, g, m, v)

    flat = [a for quad in quads for a in quad]
    res = pl.pallas_call(
        kern, name="adamw_small", in_specs=[_VMEM_SPEC] * (4 * n), out_specs=[_VMEM_SPEC] * (3 * n),
        out_shape=[jax.ShapeDtypeStruct(q[0].shape, F32) for q in quads for _ in range(3)],
    )(*flat)
    return [tuple(res[3 * i:3 * i + 3]) for i in range(n)]


def _rows_of(a, rows):
    flat = a.reshape(-1)
    return jnp.pad(flat, (0, rows * DM - flat.shape[0])).reshape(rows, DM)


def kernel(x, c, ctx, c_ctx, w_ada, b_ada, norm_g, w_in, sgu_norm_g, w_spatial, b_spatial, q_norm_g, k_norm_g, rpb, w_out, loss_target, m_c_ctx, m_w_ada, m_b_ada, m_norm_g, m_w_in, m_sgu_norm_g, m_w_spatial, m_b_spatial, m_q_norm_g, m_k_norm_g, m_rpb, m_w_out, v_c_ctx, v_w_ada, v_b_ada, v_norm_g, v_w_in, v_sgu_norm_g, v_w_spatial, v_b_spatial, v_q_norm_g, v_k_norm_g, v_rpb, v_w_out):
    xi, yi, ci = lax.axis_index("x"), lax.axis_index("y"), lax.axis_index("c")
    chip, dev = 2 * xi + yi, 4 * xi + 2 * yi + ci
    c_ctx2 = c_ctx.reshape(1, DM)

    b_shard = lax.dynamic_slice(b_ada, (0, chip * SHARD_ADA), (1, SHARD_ADA))
    part = local_step(chip.reshape(1).astype(jnp.int32), dev, x[0], c, c_ctx2, w_ada[0], b_shard, ctx[0], loss_target[0],
                      norm_g, sgu_norm_g, w_spatial[0], b_spatial[0], q_norm_g, k_norm_g, rpb[0], w_in[0], w_out[0])
    cs = part["cs"]

    slab = jnp.concatenate([
        part["d_norm_g"], _rows_of(part["d_sgu_g"], 1), _rows_of(part["d_b_s"], 1),
        _rows_of(jnp.concatenate([part["d_q_g"], part["d_k_g"]], axis=-1), 1), _rows_of(part["d_rpb"], 4),
        _rows_of(part["loss"], 1), _rows_of(part["dcmod"], 3), _rows_of(part["dmod"], 3), jnp.zeros((1, DM), F32),
        _rows_of(part["d_w_s"], 64)], axis=0)
    g_w_in, g_w_out, tot, g_w_ada, g_b_ada, g_c_ctx = final_reduce(*part["rs"], slab, cs, w_ada[0], c_ctx2)
    g_b_ada = g_b_ada.reshape(1, 3 * DM)

    loss = tot[8, 0]
    g_small = dict(
        c_ctx=g_c_ctx, b_ada=g_b_ada, norm_g=tot[0:1], sgu_norm_g=tot[1:2, :512], w_spatial=tot[16:80].reshape(512, 128),
        b_spatial=tot[2:3, :512].reshape(4, 128), q_norm_g=tot[3:4, :HDIM], k_norm_g=tot[3:4, HDIM:2 * HDIM],
        rpb=tot[4:8].reshape(-1)[:HEADS * 15 * 31].reshape(HEADS * 15, 31))
    shapes = dict(c_ctx=(DM,), w_ada=(1, DM, SHARD_ADA), b_ada=(1, 3 * DM), norm_g=(1, DM), w_in=(1, DM, SHARD_IN),
                  sgu_norm_g=(1, 512), w_spatial=(1, 4, 128, 128), b_spatial=(1, 4, 128), q_norm_g=(1, HDIM),
                  k_norm_g=(1, HDIM), rpb=(1, HEADS, 15, 31), w_out=(1, SHARD_OUT, DM))
    names = list(shapes)
    weights = dict(c_ctx=c_ctx, w_ada=w_ada, b_ada=b_ada, norm_g=norm_g, w_in=w_in, sgu_norm_g=sgu_norm_g,
                   w_spatial=w_spatial, b_spatial=b_spatial, q_norm_g=q_norm_g, k_norm_g=k_norm_g, rpb=rpb, w_out=w_out)
    m_in = dict(zip(names, (m_c_ctx, m_w_ada, m_b_ada, m_norm_g, m_w_in, m_sgu_norm_g, m_w_spatial, m_b_spatial,
                            m_q_norm_g, m_k_norm_g, m_rpb, m_w_out)))
    v_in = dict(zip(names, (v_c_ctx, v_w_ada, v_b_ada, v_norm_g, v_w_in, v_sgu_norm_g, v_w_spatial, v_b_spatial,
                            v_q_norm_g, v_k_norm_g, v_rpb, v_w_out)))
    grads = dict(g_small, w_ada=g_w_ada, w_in=g_w_in, w_out=g_w_out)
    upd = {}
    for n in ("w_ada", "w_in", "w_out"):
        g = grads[n]
        upd[n] = adamw_big(weights[n].reshape(g.shape), g, m_in[n].reshape(g.shape), v_in[n].reshape(g.shape),
                           "adamw_" + n)
    small = [n for n in names if n not in upd]
    res = adamw_small([(weights[n].reshape(grads[n].shape), grads[n], m_in[n].reshape(grads[n].shape),
                        v_in[n].reshape(grads[n].shape)) for n in small])
    upd.update(zip(small, res))
    out = [loss, part["grad_x"].reshape(1, SEQ, DM)]
    out += [grads[n].reshape(shapes[n]) for n in names]
    for slot in range(3):
        out += [upd[n][slot].reshape(shapes[n]) for n in names]
    return tuple(out)
```

```python
import functools

import jax
import jax.numpy as jnp
from jax import lax
from jax.experimental import pallas as pl
from jax.experimental.pallas import tpu as pltpu

F32, BF16 = jnp.float32, jnp.bfloat16
SEQ, DM, CTX, DIN = 4096, 1024, 256, 3584
NCHIP, NDEV = 4, 8
SHARD_IN = DIN // NCHIP
SHARD_ADA = 3 * DM // NCHIP
SHARD_OUT = DM // NCHIP
GRID_W = 64
QROWS = 4
KROWS = 12
QBLK, KBLK = QROWS * GRID_W, KROWS * GRID_W
NQBLK = SEQ // QBLK
HEADS, HDIM, NPAIR = 8, 64, 4
EPS = 1e-6
NEG_INF = -1e30
ZQ, ZK, ZV, ZG = 12, 16, 20, 24
LR, B1, B2, ADAM_EPS, WD, STEP = 0.001, 0.9, 0.999, 1e-08, 0.01, 10
VMEM_BIG = 56 * 1024 * 1024
MESH_ID = pl.DeviceIdType.MESH


def _dot(a, b, lhs_c, rhs_c):
    return lax.dot_general(a.astype(BF16), b.astype(BF16), (((lhs_c,), (rhs_c,)), ((), ())),
                           preferred_element_type=F32)


@jax.custom_vjp
def mm(a, b):
    return _dot(a, b, 1, 0)


@jax.custom_vjp
def mm_nt(a, b):
    return _dot(a, b, 1, 1)


@jax.custom_vjp
def mm_tn(a, b):
    return _dot(a, b, 0, 0)


mm.defvjp(lambda a, b: (mm(a, b), (a, b)), lambda r, ct: (mm_nt(ct, r[1]), mm_tn(r[0], ct)))
mm_nt.defvjp(lambda a, b: (mm_nt(a, b), (a, b)), lambda r, ct: (mm(ct, r[1]), mm_tn(ct, r[0])))
mm_tn.defvjp(lambda a, b: (mm_tn(a, b), (a, b)), lambda r, ct: (mm_nt(r[1], ct), mm(r[0], ct)))


def _rms(x, g):
    return x * lax.rsqrt(jnp.mean(x * x, axis=-1, keepdims=True) + EPS) * g


def _modulated(x, g, scale, shift):
    return _rms(x, g) * (1.0 + scale) + shift


def _pair_rms(x, g2):
    lo = lax.broadcasted_iota(jnp.int32, (1, 2 * HDIM), 1) < HDIM
    sq = x * x
    s_lo = jnp.sum(jnp.where(lo, sq, 0.0), axis=-1, keepdims=True)
    s_hi = jnp.sum(jnp.where(lo, 0.0, sq), axis=-1, keepdims=True)
    rs = jnp.where(lo, lax.rsqrt(s_lo / HDIM + EPS), lax.rsqrt(s_hi / HDIM + EPS))
    return x * rs * g2


def _cparams(sem, vmem=None):
    return pltpu.CompilerParams(dimension_semantics=sem, vmem_limit_bytes=vmem)


def _row(n):
    return pl.BlockSpec((1, n), lambda *_: (0, 0))


CS_ROWS = 8 * NDEV + 8


def _mod_part(mod_ref, row, part):
    pieces = []
    for j in range(NCHIP):
        lo, hi = max(part * DM, j * SHARD_ADA), min((part + 1) * DM, (j + 1) * SHARD_ADA)
        if lo < hi:
            pieces.append(mod_ref[j, row, lo - j * SHARD_ADA:hi - j * SHARD_ADA])
    return jnp.concatenate(pieces, axis=-1)


def inproj_fwd(chip, x, c_vec, c_ctx, w_ada, b_shard, norm_g, w_shard, wo_shard):
    tl = 1024
    nt = SEQ // tl
    halves = (DM // 2, SHARD_OUT // 2)
    n_w, n_c = 12, NDEV - 1

    def kern(k_ref, x_ref, cv_ref, cc_ref, wa_ref, b_ref, g_ref, w_ref, wo_ref,
             z_ref, h_ref, wfull_ref, wofull_ref, modall_ref, csall_ref,
             w_scr, wo_scr, h_scr, mine, cs_scr, mod_scr, shsc_scr, send_sems, recv_sems, out_sems):
        s, t = pl.program_id(0), pl.program_id(1)
        xi, yi, c = _me()
        k, me = 2 * xi + yi, 4 * xi + 2 * yi + c
        sib = _flip(1)
        rows = pl.ds(pl.multiple_of(t * tl, tl), tl)
        gathered = (w_scr, wo_scr)
        slot = lambda d: pl.ds(pl.multiple_of(8 * d, 8), 8)

        def c_copy(q, owner):
            return _rcopy(mine, cs_scr.at[slot(owner), :], send_sems, recv_sems, n_w + q - 1, _flip(q))

        def m_copy(q, chip_of_block):
            return _rcopy(mod_scr.at[chip_of_block], mod_scr.at[chip_of_block], send_sems, recv_sems,
                          n_w + n_c + q // 2 - 1, _flip(q))

        def adaln():
            first = lax.broadcasted_iota(jnp.int32, (8, DM), 0) == 0
            mine[...] = jnp.where(first, jnp.broadcast_to(cv_ref[...], (8, DM)), 0.0)
            cs_scr[slot(me), :] = mine[...]
            cs_scr[slot(NDEV), :] = jnp.where(first, jnp.broadcast_to(cc_ref[...], (8, DM)), 0.0)
            for q in range(1, NDEV):
                c_copy(q, me).start()
            wa = wa_ref[...].astype(BF16)
            for q in range(1, NDEV):
                px, py, pc = _flip(q)
                c_copy(q, 4 * px + 2 * py + pc).wait_recv()
            act = jax.nn.silu(cs_scr[...]).astype(BF16)
            mod_scr[k] = jnp.dot(act, wa, preferred_element_type=F32) + b_ref[...]
            for q in (2, 4, 6):
                m_copy(q, k).start()
            for q in (2, 4, 6):
                m_copy(q, _chip_of(_flip(q))).wait_recv()
            row = pl.ds(8 * me, 1)
            shsc_scr[0:1, :] = _mod_part(mod_scr, row, 0)
            shsc_scr[1:2, :] = _mod_part(mod_scr, row, 1)
            pltpu.sync_copy(mod_scr, modall_ref)
            pltpu.sync_copy(cs_scr, csall_ref)

        def block(n, chip_of_block, hh):
            return gathered[n].at[chip_of_block, pl.ds(pl.multiple_of(hh * halves[n], halves[n]), halves[n]), :]

        def ici(n, q, chip_of_block):
            blk = block(n, chip_of_block, c)
            return _rcopy(blk, blk, send_sems, recv_sems, 6 * n + q // 2 - 1, _flip(q))

        def d2d(n, q, chip_of_block, hh):
            blk = block(n, chip_of_block, hh)
            return _rcopy(blk, blk, send_sems, recv_sems, 6 * n + 3 + q // 2 - 1, sib)

        @pl.when((s == 0) & (t == 0))
        def _():
            adaln()
            w_scr[k] = w_ref[...].astype(BF16)
            wo_scr[k] = wo_ref[...].astype(BF16)
            for q in (2, 4, 6):
                ici(0, q, k).start()
                ici(1, q, k).start()

        for sweep in (1, 2, 3):
            @pl.when((s == sweep) & (t == 0))
            def _():
                q = 2 * sweep
                src = _chip_of(_flip(q))
                for n in (0, 1):
                    ici(n, q, src).wait_recv()
                    d2d(n, q, src, c).start()
                for n in (0, 1):
                    d2d(n, q, src, 1 - c).wait_recv()

        @pl.when(s == 0)
        def _():
            hb = _modulated(x_ref[...], g_ref[...], shsc_scr[1:2, :], shsc_scr[0:1, :]).astype(BF16)
            h_scr[rows, :] = hb
            h_ref[...] = hb

        z_ref[...] = jnp.dot(h_scr[rows, :], w_scr[lax.bitwise_xor(k, s)], preferred_element_type=F32)

        @pl.when((s == NCHIP - 1) & (t == nt - 1))
        def _():
            for q in range(1, NDEV):
                c_copy(q, me).wait_send()
            for q in (2, 4, 6):
                m_copy(q, k).wait_send()
            for n in (0, 1):
                for q in (2, 4, 6):
                    ici(n, q, k).wait_send()
                    d2d(n, q, _chip_of(_flip(q)), c).wait_send()
            outs = [pltpu.make_async_copy(w_scr.at[j], wfull_ref.at[:, j * SHARD_IN:(j + 1) * SHARD_IN], out_sems.at[j])
                    for j in range(NCHIP)] + [pltpu.make_async_copy(wo_scr, wofull_ref, out_sems.at[NCHIP])]
            for cp in outs:
                cp.start()
            for cp in outs:
                cp.wait()

    once = lambda s, t, k: (jnp.where(s == 0, t, nt - 1), 0)
    hbm = pl.BlockSpec(memory_space=pl.ANY)
    n_sem = n_w + n_c + 3
    return pl.pallas_call(
        kern, name="inproj_fwd",
        grid_spec=pltpu.PrefetchScalarGridSpec(
            num_scalar_prefetch=1, grid=(NCHIP, nt),
            in_specs=[pl.BlockSpec((tl, DM), once)] + [_VMEM_SPEC] * 7,
            out_specs=[pl.BlockSpec((tl, SHARD_IN), lambda s, t, k: (t, lax.bitwise_xor(k[0], s))),
                       pl.BlockSpec((tl, DM), once), hbm, hbm, hbm, hbm],
            scratch_shapes=[pltpu.VMEM((NCHIP, DM, SHARD_IN), BF16), pltpu.VMEM((NCHIP, SHARD_OUT, DM), BF16),
                            pltpu.VMEM((SEQ, DM), BF16), pltpu.VMEM((8, DM), F32), pltpu.VMEM((CS_ROWS, DM), F32),
                            pltpu.VMEM((NCHIP, CS_ROWS, SHARD_ADA), F32), pltpu.VMEM((8, DM), F32),
                            pltpu.SemaphoreType.DMA((n_sem,)), pltpu.SemaphoreType.DMA((n_sem,)),
                            pltpu.SemaphoreType.DMA((NCHIP + 1,))]),
        out_shape=[jax.ShapeDtypeStruct((SEQ, DIN), F32), jax.ShapeDtypeStruct((SEQ, DM), BF16),
                   jax.ShapeDtypeStruct((DM, DIN), BF16), jax.ShapeDtypeStruct((NCHIP, SHARD_OUT, DM), BF16),
                   jax.ShapeDtypeStruct((NCHIP, CS_ROWS, SHARD_ADA), F32), jax.ShapeDtypeStruct((CS_ROWS, DM), F32)],
        compiler_params=_cparams(("arbitrary", "arbitrary"), VMEM_BIG),
    )(chip, x, c_vec, c_ctx, w_ada, b_shard, norm_g, w_shard, wo_shard)


def ctx_fwd(ctx, cshift, cscale, norm_g, w_full):
    def kern(c_ref, sh_ref, sc_ref, g_ref, w_ref, zc_ref, hc_ref):
        hc = _modulated(c_ref[...], g_ref[...], sc_ref[...], sh_ref[...]).astype(BF16)
        hc_ref[...] = hc
        zc_ref[...] = jnp.dot(hc, w_ref[...], preferred_element_type=F32)

    return pl.pallas_call(
        kern, name="ctx_fwd", grid=(1,),
        in_specs=[pl.BlockSpec((CTX, DM), lambda i: (0, 0)), _row(DM), _row(DM), _row(DM),
                  pl.BlockSpec((DM, 2 * SHARD_IN), lambda i: (0, 1))],
        out_specs=[pl.BlockSpec((CTX, 2 * SHARD_IN), lambda i: (0, 0)),
                   pl.BlockSpec((CTX, DM), lambda i: (0, 0))],
        out_shape=[jax.ShapeDtypeStruct((CTX, 2 * SHARD_IN), F32), jax.ShapeDtypeStruct((CTX, DM), BF16)],
        compiler_params=_cparams(("arbitrary",)),
    )(ctx, cshift, cscale, norm_g, w_full)


SGU_CHUNK, SGU_PER_STEP = 128, 4


def _gelu(x):
    return 0.5 * x * (1.0 + lax.erf(x * 0.7071067811865476))


def _sgu_chunk(au, av, ag, sg, ws, bsb):
    u, v = _gelu(au), _gelu(av)
    outs = []
    for g in range(4):
        sl = slice(128 * g, 128 * (g + 1))
        mixed = mm(ws[g], _rms(v[:, sl], sg[:, sl])) + bsb[g]
        outs.append(u[:, sl] * mixed * jax.nn.silu(ag[:, sl]))
    return jnp.concatenate(outs, axis=-1)


def _sgu_specs():
    rows = SGU_CHUNK * SGU_PER_STEP
    zspec = lambda c: pl.BlockSpec((rows, 512), lambda n: (n, c))
    wspec = pl.BlockSpec((4, 128, 128), lambda n: (0, 0, 0))
    return rows, [zspec(0), zspec(1), zspec(2), _row(512), wspec, wspec]


def sgu_fwd(z, sg, ws, bsb):
    rows, in_specs = _sgu_specs()

    def kern(au_ref, av_ref, ag_ref, sg_ref, ws_ref, bs_ref, o_ref):
        for c in range(SGU_PER_STEP):
            sl = slice(c * SGU_CHUNK, (c + 1) * SGU_CHUNK)
            o_ref[sl, :] = _sgu_chunk(au_ref[sl, :], av_ref[sl, :], ag_ref[sl, :], sg_ref[...], ws_ref[...],
                                      bs_ref[...])

    return pl.pallas_call(
        kern, name="sgu_fwd", grid=(SEQ // rows,), in_specs=in_specs,
        out_specs=pl.BlockSpec((rows, 512), lambda n: (n, 0)),
        out_shape=jax.ShapeDtypeStruct((SEQ, 512), F32),
        compiler_params=_cparams(("arbitrary",)),
    )(z, z, z, sg, ws, bsb)


def sgu_bwd(z, sg, ws, bsb, dcat):
    rows, in_specs = _sgu_specs()

    def kern(au_ref, av_ref, ag_ref, sg_ref, ws_ref, bs_ref, do_ref, dz_ref, dsg_ref, dws_ref, dbs_ref):
        @pl.when(pl.program_id(0) == 0)
        def _():
            dsg_ref[...] = jnp.zeros_like(dsg_ref)
            dws_ref[...] = jnp.zeros_like(dws_ref)
            dbs_ref[...] = jnp.zeros_like(dbs_ref)

        for c in range(SGU_PER_STEP):
            sl = slice(c * SGU_CHUNK, (c + 1) * SGU_CHUNK)
            _, vjp = jax.vjp(_sgu_chunk, au_ref[sl, :], av_ref[sl, :], ag_ref[sl, :], sg_ref[...], ws_ref[...],
                             bs_ref[...])
            dau, dav, dag, dsg, dws, dbs = vjp(do_ref[sl, :])
            dz_ref[sl, 0:512] = dau.astype(BF16)
            dz_ref[sl, 512:1024] = dav.astype(BF16)
            dz_ref[sl, 1024:1536] = dag.astype(BF16)
            dsg_ref[...] += dsg
            dws_ref[...] += dws
            dbs_ref[...] += dbs

        @pl.when(pl.program_id(0) == pl.num_programs(0) - 1)
        def _():
            dbs_ref[...] = jnp.broadcast_to(jnp.sum(dbs_ref[...], axis=-1, keepdims=True), dbs_ref.shape)

    wspec = pl.BlockSpec((4, 128, 128), lambda n: (0, 0, 0))
    return pl.pallas_call(
        kern, name="sgu_bwd", grid=(SEQ // rows,),
        in_specs=in_specs + [pl.BlockSpec((rows, 512), lambda n: (n, 0))],
        out_specs=[pl.BlockSpec((rows, 1536), lambda n: (n, 0)), _row(512), wspec, wspec],
        out_shape=[jax.ShapeDtypeStruct((SEQ, 1536), BF16), jax.ShapeDtypeStruct((1, 512), F32),
                   jax.ShapeDtypeStruct((4, 128, 128), F32), jax.ShapeDtypeStruct((4, 128, 128), F32)],
        compiler_params=_cparams(("arbitrary",)),
    )(z, z, z, sg, ws, bsb, dcat)


_DR_OFF = (7, 3, -1)


def _row_valid(v, rr, j):
    return (j < 8, rr <= j < rr + 8, 4 <= j < 12)[v]


def _col_window():
    q = lax.broadcasted_iota(jnp.int32, (GRID_W, 128), 0)
    kc = lax.broadcasted_iota(jnp.int32, (GRID_W, 128), 1) % GRID_W
    c0 = jnp.clip(q - 8, 0, GRID_W - 16)
    return (kc >= c0) & (kc < c0 + 16)


def _bias_tiles(base, store):
    lo = lax.broadcasted_iota(jnp.int32, (1, 128), 1) < GRID_W
    win = _col_window()
    tiles = {}
    for v in range(3):
        for rr in range(QROWS):
            for jp in range(KROWS // 2):
                j0, j1 = 2 * jp, 2 * jp + 1
                ok0, ok1 = _row_valid(v, rr, j0), _row_valid(v, rr, j1)
                key = (j0 - rr + _DR_OFF[v], ok0, ok1) if (ok0 or ok1) else None
                if key not in tiles:
                    if key is None:
                        tiles[key] = jnp.full((GRID_W, 128), NEG_INF, F32)
                    else:
                        d0 = key[0]
                        r0 = base[d0:d0 + 1, :] if ok0 else jnp.zeros((1, 128), F32)
                        r1 = base[d0 + 1:d0 + 2, :] if ok1 else jnp.zeros((1, 128), F32)
                        y = jnp.broadcast_to(jnp.where(lo, r0, r1), (GRID_W, 128))
                        y = pltpu.roll(pltpu.roll(y, 128 - 15, 1), 0, 1, stride=1, stride_axis=0)
                        tiles[key] = jnp.where(win & jnp.where(lo, ok0, ok1), y, NEG_INF)
                store(v, slice(rr * GRID_W, (rr + 1) * GRID_W), slice(jp * 128, (jp + 1) * 128), tiles[key])


def _rpb_grad(load):
    lo = lax.broadcasted_iota(jnp.int32, (1, 128), 1) < GRID_W
    ri = lax.broadcasted_iota(jnp.int32, (GRID_W, GRID_W), 0)
    ci = lax.broadcasted_iota(jnp.int32, (GRID_W, GRID_W), 1)
    flip = (ri + ci == GRID_W - 1).astype(F32)
    groups = {}
    for v in range(3):
        for rr in range(QROWS):
            for jp in range(KROWS // 2):
                j0, j1 = 2 * jp, 2 * jp + 1
                ok0, ok1 = _row_valid(v, rr, j0), _row_valid(v, rr, j1)
                if not (ok0 or ok1):
                    continue
                g = load(v, slice(rr * GRID_W, (rr + 1) * GRID_W), slice(jp * 128, (jp + 1) * 128))
                key = (j0 - rr + _DR_OFF[v], ok0, ok1)
                groups[key] = g if key not in groups else groups[key] + g
    acc = [jnp.zeros((1, 128), F32) for _ in range(15)]
    for (d0, ok0, ok1), g in groups.items():
        g = lax.dot_general(flip, g, (((1,), (0,)), ((), ())), precision=lax.Precision.HIGHEST,
                            preferred_element_type=F32)
        g = pltpu.roll(pltpu.roll(g, 128 - 48, 1), 0, 1, stride=1, stride_axis=0)
        s = jnp.sum(g, axis=0, keepdims=True)
        if ok0:
            acc[d0] = acc[d0] + jnp.where(lo, s, 0.0)
        if ok1:
            acc[d0 + 1] = acc[d0 + 1] + jnp.where(lo, 0.0, s)
    return [row + pltpu.roll(row, GRID_W, 1) for row in acc]


def _scaled_q(q_raw, qg):
    return _pair_rms(q_raw, qg) * (HDIM ** -0.5)


def _head_lanes():
    lo = lax.broadcasted_iota(jnp.int32, (1, 2 * HDIM), 1) < HDIM
    return lo, jnp.logical_not(lo)


SOFTMAX_ROWS = 32


def _emit_interleaved(vector_work, matmul_work):
    for j in range(max(len(vector_work), len(matmul_work))):
        for work in (vector_work, matmul_work):
            if j < len(work):
                work[j]()


def _kblock(i):
    return jnp.clip(i - 1, 0, (SEQ - KBLK) // QBLK)


def _kstart(i):
    return pl.multiple_of(_kblock(i) * QBLK, QBLK)


ATTN_BLOCKS = 4
TILE_BUFFERS = 4
ATTN_STEPS = NQBLK // ATTN_BLOCKS
ATTN_ROWS = ATTN_BLOCKS * QBLK


def _bias_variant(i, b):
    if b == 0:
        return jnp.where(i == 0, 0, 1)
    if b == ATTN_BLOCKS - 1:
        return jnp.where(i == ATTN_STEPS - 1, 2, 1)
    return 1
KCOLS = QBLK


def _attn_in_specs():
    return [
        pl.BlockSpec((ATTN_ROWS, 128), lambda p, i: (i, ZQ + p)),
        pl.BlockSpec((SEQ, 128), lambda p, i: (0, ZK + p)),
        pl.BlockSpec((SEQ, 128), lambda p, i: (0, ZV + p)),
        pl.BlockSpec((ATTN_ROWS, 128), lambda p, i: (i, ZG + p)),
        pl.BlockSpec((CTX, 128), lambda p, i: (0, 2 + p)),
        pl.BlockSpec((CTX, 128), lambda p, i: (0, 6 + p)),
    ]


def _rpb_spec():
    return pl.BlockSpec((2, 15, 128), lambda p, i: (p, 0, 0))


def _prob_specs():
    return [pl.BlockSpec((2, ATTN_ROWS, KBLK), lambda p, i: (p, i, 0)),
            pl.BlockSpec((2, ATTN_ROWS, CTX), lambda p, i: (p, i, 0))]


NORM_ROWS = 512


def _half_sums(x):
    lo = lax.broadcasted_iota(jnp.int32, (1, 2 * HDIM), 1) < HDIM
    return jnp.where(lo, jnp.sum(jnp.where(lo, x, 0.0), axis=-1, keepdims=True),
                     jnp.sum(jnp.where(lo, 0.0, x), axis=-1, keepdims=True))


def _pair_rms_bwd(x, g2, ct):
    rs = lax.rsqrt(_half_sums(x * x) / HDIM + EPS)
    y = x * rs
    dy = ct * g2
    return rs * (dy - y * (_half_sums(dy * y) / HDIM)), jnp.sum(ct * y, axis=0, keepdims=True)


def _norm_keys(k_ref, ck_ref, kg_ref, kn_scr, ckn_scr):
    def body(c, carry):
        sl = pl.ds(pl.multiple_of(c * NORM_ROWS, NORM_ROWS), NORM_ROWS)
        kn_scr[sl, :] = _pair_rms(k_ref[sl, :], kg_ref[...]).astype(BF16)
        return carry

    lax.fori_loop(0, SEQ // NORM_ROWS, body, 0)
    ckn_scr[...] = _pair_rms(ck_ref[...], kg_ref[...]).astype(BF16)


def _values_with_ones(v_ref, cv_ref, v1_scr, cv1_scr):
    for a, mine in enumerate(_head_lanes()):
        def body(c, carry):
            sl = pl.ds(pl.multiple_of(c * NORM_ROWS, NORM_ROWS), NORM_ROWS)
            v1_scr[a, sl, :] = jnp.where(mine, v_ref[sl, :], 1.0).astype(BF16)
            return carry

        lax.fori_loop(0, SEQ // NORM_ROWS, body, 0)
        cv1_scr[a] = jnp.where(mine, cv_ref[...], 1.0).astype(BF16)


def _pair_major_spec():
    return pl.BlockSpec((1, ATTN_ROWS, 128), lambda p, i: (p, i, 0))


def _normed_key_specs():
    return [pl.BlockSpec((None, SEQ, 128), lambda p, i: (p, 0, 0)), pl.BlockSpec((None, CTX, 128), lambda p, i: (p, 0, 0))]


def attn_fwd(z, zc, rpb2, qg2, kg2):
    def kern(q_ref, k_ref, v_ref, bg_ref, ck_ref, cv_ref, rpb_ref, qg_ref, kg_ref,
             ob_ref, o_ref, rden_ref, pl_ref, pc_ref, kn_ref, ckn_ref, kn_scr, ckn_scr, v1_scr, cv1_scr, s_scr,
             bias_ref):
        i = pl.program_id(1)

        @pl.when(i == 0)
        def _():
            for a in range(2):
                def store(v, tile_rows, tile_cols, tile, a=a):
                    bias_ref[v, a, tile_rows, tile_cols] = tile

                _bias_tiles(rpb_ref[a], store)
            _norm_keys(k_ref, ck_ref, kg_ref, kn_scr, ckn_scr)
            kn_ref[...] = kn_scr[...]
            ckn_ref[...] = ckn_scr[...]
            _values_with_ones(v_ref, cv_ref, v1_scr, cv1_scr)

        heads = _head_lanes()
        tiles = [(b, a) for b in range(ATTN_BLOCKS) for a in range(2)]
        rows = [slice(b * QBLK, (b + 1) * QBLK) for b in range(ATTN_BLOCKS)]
        variant = [_bias_variant(i, b) for b in range(ATTN_BLOCKS)]
        pv = [None] * len(tiles)
        qa, done = {}, {}
        latent = KBLK // KCOLS
        buf = lambda t: t % TILE_BUFFERS

        def keys(b, n):
            return pl.ds(pl.multiple_of(_kstart(ATTN_BLOCKS * i + b) + n * KCOLS, KCOLS), KCOLS)

        def score_piece(t, n):
            b, a = tiles[t]
            cols = slice(n * KCOLS, (n + 1) * KCOLS)
            if n == 0:
                if a == 0:
                    done["qn", b] = _scaled_q(q_ref[rows[b], :], qg_ref[...])
                qa[t] = jnp.where(heads[a], done["qn", b], 0.0).astype(BF16)
            if n < latent:
                s_scr[buf(t), :, cols] = mm_nt(qa[t], kn_scr[keys(b, n), :]) + bias_ref[variant[b], a, :, cols]
            else:
                s_scr[buf(t), :, cols] = mm_nt(qa[t], ckn_scr[...])

        def softmax_rows(t, r):
            b, a = tiles[t]
            rs = slice(r * SOFTMAX_ROWS, (r + 1) * SOFTMAX_ROWS)
            out_rows = slice(b * QBLK + rs.start, b * QBLK + rs.stop)
            s = s_scr[buf(t), rs, :]
            p = jnp.exp(s - jnp.max(s, axis=-1, keepdims=True)).astype(BF16)
            pl_ref[a, out_rows, :] = p[:, :KBLK]
            pc_ref[a, out_rows, :] = p[:, KBLK:]

        def value_piece(t, n):
            b, a = tiles[t]
            if n < latent:
                part = mm(pl_ref[a, rows[b], n * KCOLS:(n + 1) * KCOLS], v1_scr[a, keys(b, n), :])
            else:
                part = mm(pc_ref[a, rows[b], :], cv1_scr[a])
            pv[t] = part if pv[t] is None else pv[t] + part
            if n == latent:
                finish(t)

        def finish(t):
            b, a = tiles[t]
            r = jnp.where(heads[a], pltpu.roll(1.0 / pv[t], HDIM, 1), 0.0)
            done[t] = (pv[t] * r, r)
            if a == 1:
                o, rden = (lo + hi for lo, hi in zip(done[t - 1], done[t]))
                ob_ref[rows[b], :] = o * jax.nn.silu(bg_ref[rows[b], :])
                o_ref[0, rows[b], :] = o
                rden_ref[0, rows[b], :] = rden

        pieces = range(latent + 1)
        for n in pieces:
            score_piece(0, n)
        for t in range(len(tiles)):
            matmuls = []
            for n in pieces:
                if t + 1 < len(tiles):
                    matmuls.append(functools.partial(score_piece, t + 1, n))
                if t > 0:
                    matmuls.append(functools.partial(value_piece, t - 1, n))
            _emit_interleaved([functools.partial(softmax_rows, t, r) for r in range(QBLK // SOFTMAX_ROWS)], matmuls)
        for n in pieces:
            value_piece(len(tiles) - 1, n)

    qblk = pl.BlockSpec((ATTN_ROWS, 128), lambda p, i: (i, p))
    return pl.pallas_call(
        kern, name="attn_fwd", grid=(NPAIR, ATTN_STEPS),
        in_specs=_attn_in_specs() + [_rpb_spec(), _row(128), _row(128)],
        out_specs=[qblk, _pair_major_spec(), _pair_major_spec()] + _prob_specs() + _normed_key_specs(),
        out_shape=[jax.ShapeDtypeStruct((SEQ, 512), F32)] + [jax.ShapeDtypeStruct((NPAIR, SEQ, 128), F32)] * 2
        + [jax.ShapeDtypeStruct((HEADS, SEQ, KBLK), BF16), jax.ShapeDtypeStruct((HEADS, SEQ, CTX), BF16),
           jax.ShapeDtypeStruct((NPAIR, SEQ, 128), BF16), jax.ShapeDtypeStruct((NPAIR, CTX, 128), BF16)],
        scratch_shapes=[pltpu.VMEM((SEQ, 128), BF16), pltpu.VMEM((CTX, 128), BF16),
                        pltpu.VMEM((2, SEQ, 128), BF16), pltpu.VMEM((2, CTX, 128), BF16),
                        pltpu.VMEM((TILE_BUFFERS, QBLK, KBLK + CTX), F32),
                        pltpu.VMEM((3, 2, QBLK, KBLK), F32)],
        compiler_params=_cparams(("arbitrary", "arbitrary"), VMEM_BIG),
    )(z, z, z, z, zc, zc, rpb2, qg2, kg2)


def attn_bwd(z, zc, qg2, kg2, dcat, saved):
    def kern(q_ref, k_ref, v_ref, bg_ref, ck_ref, cv_ref, qg_ref, kg_ref, do_ref, o_ref, rden_ref, pl_ref, pc_ref,
             kn_scr, ckn_scr, dq_ref, dk_ref, dv_ref, dbg_ref, dck_ref, dcv_ref, drpb_ref, dqg_ref, dkg_ref,
             v_scr, cv_scr, dknt_scr, dvt_scr, dcknt_scr, dcvt_scr, dp_scr, ds_scr, db_ref):
        p, i = pl.program_id(0), pl.program_id(1)
        last = i == ATTN_STEPS - 1

        @pl.when(i == 0)
        def _():
            def body(c, carry):
                sl = pl.ds(pl.multiple_of(c * NORM_ROWS, NORM_ROWS), NORM_ROWS)
                v_scr[sl, :] = v_ref[sl, :].astype(BF16)
                return carry

            lax.fori_loop(0, SEQ // NORM_ROWS, body, 0)
            cv_scr[...] = cv_ref[...].astype(BF16)
            for acc in (dknt_scr, dvt_scr, dcknt_scr, dcvt_scr, db_ref):
                acc[...] = jnp.zeros_like(acc)

        @pl.when((i == 0) & (p == 0))
        def _():
            dqg_ref[...] = jnp.zeros_like(dqg_ref)
            dkg_ref[...] = jnp.zeros_like(dkg_ref)

        heads = _head_lanes()
        tiles = [(b, a) for b in range(ATTN_BLOCKS) for a in range(2)]
        rows = [slice(b * QBLK, (b + 1) * QBLK) for b in range(ATTN_BLOCKS)]
        kb = [_kblock(ATTN_BLOCKS * i + b) for b in range(ATTN_BLOCKS)]
        variant = [_bias_variant(i, b) for b in range(ATTN_BLOCKS)]
        latent = KBLK // KCOLS
        buf = lambda t: t % TILE_BUFFERS

        def keys(b, n):
            return pl.ds(pl.multiple_of((kb[b] + n) * KCOLS, KCOLS), KCOLS)

        gated = {}

        def gate_backward(b):
            bg, dout, o = bg_ref[rows[b], :], do_ref[rows[b], :], o_ref[0, rows[b], :]
            sig = jax.nn.sigmoid(bg)
            do = dout * (bg * sig)
            dbg_ref[rows[b], :] = (dout * o * (sig * (1.0 + bg * (1.0 - sig)))).astype(BF16)
            rden = rden_ref[0, rows[b], :]
            dr = do * rden
            qn = _scaled_q(q_ref[rows[b], :], qg_ref[...])
            gated[b] = (dr, dr.T.astype(BF16), qn.T.astype(BF16), do * o * rden)

        feats = [slice(a * HDIM, (a + 1) * HDIM) for a in range(2)]
        doa, doa_t, qa_t, delta = {}, {}, {}, {}
        dqn = [None] * len(tiles)

        def cols(n):
            return slice(n * KCOLS, (n + 1) * KCOLS)

        def stage_a(t, n):
            b, a = tiles[t]
            if n == 0:
                if a == 0:
                    gate_backward(b)
                dr, dr_t, qn_t, weighted = gated[b]
                doa[t] = jnp.where(heads[a], dr, 0.0).astype(BF16)
                doa_t[t] = dr_t[feats[a], :]
                qa_t[t] = qn_t[feats[a], :]
                delta[t] = jnp.sum(jnp.where(heads[a], weighted, 0.0), axis=-1, keepdims=True)
            if n < latent:
                dp_scr[buf(t), :, cols(n)] = mm_nt(doa[t], v_scr[keys(b, n), :])
                dvt_scr[kb[b] + n, feats[a], :] += mm(doa_t[t], pl_ref[a, rows[b], cols(n)])
            else:
                dp_scr[buf(t), :, cols(n)] = mm_nt(doa[t], cv_scr[...])
                dcvt_scr[feats[a], :] += mm(doa_t[t], pc_ref[a, rows[b], :])

        def stage_b(t, r):
            b, a = tiles[t]
            rs = slice(r * SOFTMAX_ROWS, (r + 1) * SOFTMAX_ROWS)
            in_rows = slice(b * QBLK + rs.start, b * QBLK + rs.stop)
            d = dp_scr[buf(t), rs, :] - delta[t][rs, :]
            ds_lat = pl_ref[a, in_rows, :].astype(F32) * d[:, :KBLK]
            ds_ctx = pc_ref[a, in_rows, :].astype(F32) * d[:, KBLK:]
            db_ref[variant[b], a, rs, :] += ds_lat
            ds_scr[buf(t), rs, :KBLK] = ds_lat.astype(BF16)
            ds_scr[buf(t), rs, KBLK:] = ds_ctx.astype(BF16)

        def stage_c(t, n):
            b, a = tiles[t]
            ds = ds_scr[buf(t), :, cols(n)]
            if n < latent:
                part = mm(ds, kn_scr[keys(b, n), :])
                dknt_scr[kb[b] + n, feats[a], :] += mm(qa_t[t], ds)
            else:
                part = mm(ds, ckn_scr[...])
                dcknt_scr[feats[a], :] += mm(qa_t[t], ds)
            dqn[t] = part if dqn[t] is None else dqn[t] + part
            if n == latent and a == 1:
                both = jnp.where(heads[0], dqn[t - 1], 0.0) + jnp.where(heads[1], dqn[t], 0.0)
                dq, dqg = jax.vjp(_scaled_q, q_ref[rows[b], :], qg_ref[...])[1](both)
                dq_ref[rows[b], :] = dq.astype(BF16)
                dqg_ref[...] += dqg

        pieces = range(latent + 1)
        for n in pieces:
            stage_a(0, n)
        for t in range(len(tiles)):
            matmuls = []
            for n in pieces:
                if t + 1 < len(tiles):
                    matmuls.append(functools.partial(stage_a, t + 1, n))
                if t > 0:
                    matmuls.append(functools.partial(stage_c, t - 1, n))
            _emit_interleaved([functools.partial(stage_b, t, r) for r in range(QBLK // SOFTMAX_ROWS)], matmuls)
        for n in pieces:
            stage_c(len(tiles) - 1, n)

        @pl.when(last)
        def _():
            eye = (lax.broadcasted_iota(jnp.int32, (KCOLS, KCOLS), 0)
                   == lax.broadcasted_iota(jnp.int32, (KCOLS, KCOLS), 1)).astype(BF16)

            def turned(x):
                hi = x.astype(BF16)
                return mm_nt(eye, hi) + mm_nt(eye, x - hi.astype(F32))

            def body(c, dkg):
                sl = pl.ds(pl.multiple_of(c * NORM_ROWS, NORM_ROWS), NORM_ROWS)
                blocks = range(NORM_ROWS // KCOLS)
                dkn = jnp.concatenate([turned(dknt_scr[c * len(blocks) + n]) for n in blocks], axis=0)
                dv = jnp.concatenate([mm_nt(eye, dvt_scr[c * len(blocks) + n]) for n in blocks], axis=0)
                dk, dg = _pair_rms_bwd(k_ref[sl, :], kg_ref[...], dkn)
                dk_ref[sl, :] = dk.astype(BF16)
                dv_ref[sl, :] = dv.astype(BF16)
                return dkg + dg

            dkg = lax.fori_loop(0, SEQ // NORM_ROWS, body, jnp.zeros((1, 128), F32))
            dck, dg = _pair_rms_bwd(ck_ref[...], kg_ref[...], dcknt_scr[...].T)
            dck_ref[...] = dck
            dcv_ref[...] = dcvt_scr[...].T
            dkg_ref[...] += dkg + dg
            for a in range(2):
                rows_of_rpb = _rpb_grad(lambda v, tile_rows, tile_cols, a=a: db_ref[v, a, tile_rows, tile_cols])
                for d, row in enumerate(rows_of_rpb):
                    drpb_ref[a, d:d + 1, :] = row

        @pl.when(last & (p == NPAIR - 1))
        def _():
            dqg_ref[...] = dqg_ref[...] + pltpu.roll(dqg_ref[...], HDIM, 1)
            dkg_ref[...] = dkg_ref[...] + pltpu.roll(dkg_ref[...], HDIM, 1)

    blk = lambda rows: pl.BlockSpec((rows, 128), lambda p, i: (0, p))
    qblk = pl.BlockSpec((ATTN_ROWS, 128), lambda p, i: (i, p))
    return pl.pallas_call(
        kern, name="attn_bwd", grid=(NPAIR, ATTN_STEPS),
        in_specs=_attn_in_specs() + [_row(128), _row(128), pl.BlockSpec((ATTN_ROWS, 128), lambda p, i: (i, 4 + p)),
                                     _pair_major_spec(), _pair_major_spec()] + _prob_specs() + _normed_key_specs(),
        out_specs=[qblk, blk(SEQ), blk(SEQ), qblk, blk(CTX), blk(CTX), _rpb_spec(), _row(128), _row(128)],
        out_shape=[jax.ShapeDtypeStruct((SEQ, 512), BF16)] * 4 + [jax.ShapeDtypeStruct((CTX, 512), F32)] * 2
        + [jax.ShapeDtypeStruct((HEADS, 15, 128), F32)]
        + [jax.ShapeDtypeStruct((1, 128), F32), jax.ShapeDtypeStruct((1, 128), F32)],
        scratch_shapes=[pltpu.VMEM((SEQ, 128), BF16), pltpu.VMEM((CTX, 128), BF16),
                        pltpu.VMEM((SEQ // KCOLS, 128, KCOLS), F32), pltpu.VMEM((SEQ // KCOLS, 128, KCOLS), F32),
                        pltpu.VMEM((128, CTX), F32), pltpu.VMEM((128, CTX), F32),
                        pltpu.VMEM((TILE_BUFFERS, QBLK, KBLK + CTX), F32),
                        pltpu.VMEM((TILE_BUFFERS, QBLK, KBLK + CTX), BF16),
                        pltpu.VMEM((3, 2, QBLK, KBLK), F32)],
        compiler_params=_cparams(("arbitrary", "arbitrary"), VMEM_BIG),
    )(z, z, z, z, zc, zc, qg2, kg2, dcat, *saved)


def outproj(out_a, out_b, x, target, gate, wo):
    tl = 512

    def kern(a_ref, b_ref, x_ref, t_ref, g_ref, w_ref, loss_ref, dy_ref, dcat_ref, dg_ref, dw_ref):
        @pl.when(pl.program_id(0) == 0)
        def _():
            loss_ref[...] = jnp.zeros_like(loss_ref)
            dg_ref[...] = jnp.zeros_like(dg_ref)
            dw_ref[...] = jnp.zeros_like(dw_ref)

        a, b = a_ref[...].astype(BF16), b_ref[...].astype(BF16)
        mix = (jnp.dot(a, w_ref[0:512, :], preferred_element_type=F32)
               + jnp.dot(b, w_ref[512:1024, :], preferred_element_type=F32))
        err = x_ref[...] + g_ref[...] * mix - t_ref[...]
        loss_ref[...] += 0.5 * jnp.sum(jnp.mean(err * err, axis=-1))
        dy = err * (1.0 / DM)
        dy_ref[...] = dy
        dg_ref[...] += jnp.sum(dy * mix, axis=0, keepdims=True)
        dmix = (g_ref[...] * dy).astype(BF16)
        dcat_ref[...] = lax.dot_general(dmix, w_ref[...], (((1,), (1,)), ((), ())), preferred_element_type=F32)
        dw_ref[0:512, :] += lax.dot_general(a, dmix, (((0,), (0,)), ((), ())), preferred_element_type=F32)
        dw_ref[512:1024, :] += lax.dot_general(b, dmix, (((0,), (0,)), ((), ())), preferred_element_type=F32)

    tile = lambda w: pl.BlockSpec((tl, w), lambda t: (t, 0))
    whole = pl.BlockSpec((DM, DM), lambda t: (0, 0))
    return pl.pallas_call(
        kern, name="outproj", grid=(SEQ // tl,),
        in_specs=[tile(512), tile(512), tile(DM), tile(DM), _row(DM), whole],
        out_specs=[pl.BlockSpec((8, 128), lambda t: (0, 0)), tile(DM), tile(DM), _row(DM), whole],
        out_shape=[jax.ShapeDtypeStruct((8, 128), F32), jax.ShapeDtypeStruct((SEQ, DM), F32),
                   jax.ShapeDtypeStruct((SEQ, DM), F32), jax.ShapeDtypeStruct((1, DM), F32),
                   jax.ShapeDtypeStruct((DM, DM), F32)],
        compiler_params=_cparams(("arbitrary",), 48 * 1024 * 1024),
    )(out_a, out_b, x, target, gate, wo)


DZ_COLS = (("a", 0, 1536), ("q", 1536, 2048), ("k", 2048, 2560), ("v", 2560, 3072), ("g", 3072, DIN))
DZC_COLS = (("k", 2048, 2560), ("v", 2560, 3072))
_NT = (((1,), (1,)), ((), ()))


DH_SUBTILES = 2


def _dz_specs(tl):
    return [pl.BlockSpec((tl, 1536), lambda t: (t, 0))] + [pl.BlockSpec((tl, 512), lambda t: (t, 0))] * 4


def dh_bwd(dz_parts, w_full, x, dy, shift, scale, norm_g, dg_ctx):
    tl = 512
    nt = SEQ // tl

    def kern(a_ref, q_ref, k_ref, v_ref, g_ref, w_ref, x_ref, dy_ref, sh_ref, sc_ref, gn_ref, dgc_ref,
             gx_ref, dsh_ref, dsc_ref, dg_ref):
        @pl.when(pl.program_id(0) == 0)
        def _():
            dsh_ref[...] = jnp.zeros_like(dsh_ref)
            dsc_ref[...] = jnp.zeros_like(dsc_ref)
            dg_ref[...] = dgc_ref[...]

        src = dict(a=a_ref, q=q_ref, k=k_ref, v=v_ref, g=g_ref)
        for sub in range(DH_SUBTILES):
            rows = slice(sub * tl // DH_SUBTILES, (sub + 1) * tl // DH_SUBTILES)
            dh = None
            for name, c0, c1 in DZ_COLS:
                part = lax.dot_general(src[name][rows, :], w_ref[:, c0:c1], _NT, preferred_element_type=F32)
                dh = part if dh is None else dh + part
            _, vjp = jax.vjp(_modulated, x_ref[rows, :], gn_ref[...], sc_ref[...], sh_ref[...])
            dx, dg, dsc, dsh = vjp(dh)
            gx_ref[rows, :] = dy_ref[rows, :] + dx
            dg_ref[...] += dg
            dsc_ref[...] += dsc
            dsh_ref[...] += dsh

    tile = pl.BlockSpec((tl, DM), lambda t: (t, 0))
    return pl.pallas_call(
        kern, name="dh_bwd", grid=(nt,),
        in_specs=_dz_specs(tl) + [pl.BlockSpec((DM, DIN), lambda t: (0, 0)), tile, tile, _row(DM),
                                  _row(DM), _row(DM), _row(DM)],
        out_specs=[tile, _row(DM), _row(DM), _row(DM)],
        out_shape=[jax.ShapeDtypeStruct((SEQ, DM), F32)] + [jax.ShapeDtypeStruct((1, DM), F32)] * 3,
        compiler_params=_cparams(("arbitrary",), 48 * 1024 * 1024),
    )(*dz_parts, w_full, x, dy, shift, scale, norm_g, dg_ctx)


def dw_bwd(h, dz_parts, hc, dck, dcv, g_out):
    tl = 512
    nt = SEQ // tl
    (rhi, wi), (rho, wo) = RS_SHAPES

    def kern(h_ref, a_ref, q_ref, k_ref, v_ref, g_ref, hc_ref, dck_ref, dcv_ref, go_hbm,
             wire_i, keep_i, wire_o, keep_o, acc, snd_i, rcv_i, mine_o, rcv_o, load_sem, send_sems, recv_sems):
        t = pl.program_id(0)
        x, y, c = _me()
        k = 2 * x + y
        sib = _flip(1)
        half = lambda hh, rh: pl.ds(pl.multiple_of(hh * rh, rh), rh)
        load_o = pltpu.make_async_copy(go_hbm.at[:, half(c, rho), :], mine_o, load_sem)
        pair_o = _rcopy(go_hbm.at[:, half(1 - c, rho), :], rcv_o, send_sems, recv_sems, 0, sib)
        pair_i = [_rcopy(snd_i.at[j], rcv_i.at[j], send_sems, recv_sems, 1 + j, sib) for j in range(NCHIP)]

        @pl.when(t == 0)
        def _():
            load_o.start()
            pair_o.start()
            acc[...] = jnp.zeros_like(acc)
            hct = hc_ref[...].T
            csrc = dict(k=dck_ref, v=dcv_ref)
            for name, c0, c1 in DZC_COLS:
                acc[:, c0:c1] += jnp.dot(hct, csrc[name][...].astype(BF16), preferred_element_type=F32)

        ht = h_ref[...].T
        src = dict(a=a_ref, q=q_ref, k=k_ref, v=v_ref, g=g_ref)
        for name, c0, c1 in DZ_COLS:
            acc[:, c0:c1] += jnp.dot(ht, src[name][...], preferred_element_type=F32)

        @pl.when(t == nt - 1)
        def _():
            shard = lambda j: slice(j * SHARD_IN, (j + 1) * SHARD_IN)
            for j in range(NCHIP):
                snd_i[j] = acc[half(1 - c, rhi), shard(j)].astype(BF16)
                pair_i[j].start()
            load_o.wait()
            pair_o.wait_recv()
            for j in range(NCHIP):
                wire_o[j] = (mine_o[j] + rcv_o[j]).astype(BF16)
            keep_o[...] = mine_o[k] + rcv_o[k]
            mine = half(c, rhi)
            for j in range(NCHIP):
                pair_i[j].wait_recv()
                pair_sum = acc[mine, shard(j)] + rcv_i[j].astype(F32)
                wire_i[j] = pair_sum.astype(BF16)

                @pl.when(k == j)
                def _():
                    keep_i[...] = pair_sum
            pair_o.wait_send()
            for j in range(NCHIP):
                pair_i[j].wait_send()

    whole = lambda *shape: pl.BlockSpec(shape, lambda t: (0,) * len(shape))
    return pl.pallas_call(
        kern, name="dw_bwd", grid=(nt,),
        in_specs=[pl.BlockSpec((tl, DM), lambda t: (t, 0))] + _dz_specs(tl)
        + [whole(CTX, DM), whole(CTX, 512), whole(CTX, 512), pl.BlockSpec(memory_space=pl.ANY)],
        out_specs=[whole(NCHIP, rhi, wi), whole(rhi, wi), whole(NCHIP, rho, wo), whole(rho, wo)],
        out_shape=[jax.ShapeDtypeStruct((NCHIP, rhi, wi), BF16), jax.ShapeDtypeStruct((rhi, wi), F32),
                   jax.ShapeDtypeStruct((NCHIP, rho, wo), BF16), jax.ShapeDtypeStruct((rho, wo), F32)],
        scratch_shapes=[pltpu.VMEM((DM, DIN), F32), pltpu.VMEM((NCHIP, rhi, wi), BF16),
                        pltpu.VMEM((NCHIP, rhi, wi), BF16),
                        pltpu.VMEM((NCHIP, rho, wo), F32), pltpu.VMEM((NCHIP, rho, wo), F32),
                        pltpu.SemaphoreType.DMA(()), pltpu.SemaphoreType.DMA((1 + NCHIP,)),
                        pltpu.SemaphoreType.DMA((1 + NCHIP,))],
        compiler_params=_cparams(("arbitrary",), VMEM_BIG),
    )(h, *dz_parts, hc, dck, dcv, g_out)


def ctx_bwd(dck, dcv, w_full, ctx, cshift, cscale, norm_g):
    def kern(dck_ref, dcv_ref, w_ref, c_ref, sh_ref, sc_ref, g_ref, dsh_ref, dsc_ref, dg_ref):
        csrc = dict(k=dck_ref, v=dcv_ref)
        dhc = None
        first = DZC_COLS[0][1]
        for name, c0, c1 in DZC_COLS:
            part = lax.dot_general(csrc[name][...].astype(BF16), w_ref[:, c0 - first:c1 - first], _NT,
                                   preferred_element_type=F32)
            dhc = part if dhc is None else dhc + part
        _, vjp = jax.vjp(lambda g, sc, sh: _modulated(c_ref[...], g, sc, sh), g_ref[...], sc_ref[...], sh_ref[...])
        dg_ref[...], dsc_ref[...], dsh_ref[...] = vjp(dhc)

    whole = lambda r, c: pl.BlockSpec((r, c), lambda i: (0, 0))
    return pl.pallas_call(
        kern, name="ctx_bwd", grid=(1,),
        in_specs=[whole(CTX, 512), whole(CTX, 512), pl.BlockSpec((DM, 1024), lambda i: (0, DZC_COLS[0][1] // 1024)),
                  whole(CTX, DM), _row(DM), _row(DM), _row(DM)],
        out_specs=[_row(DM), _row(DM), _row(DM)],
        out_shape=[jax.ShapeDtypeStruct((1, DM), F32)] * 3,
        compiler_params=_cparams(("arbitrary",), 40 * 1024 * 1024),
    )(dck, dcv, w_full, ctx, cshift, cscale, norm_g)


def _lane_pad_rpb(rpb):
    r = jnp.pad(rpb, ((0, 0), (0, 0), (0, GRID_W - rpb.shape[-1])))
    return jnp.concatenate([r, r], axis=-1)


def local_step(chip, dev, x, c_vec, c_ctx, w_ada, b_shard, ctx, target, norm_g, sgu_g, w_s, b_s, q_g, k_g, rpb,
               w_in_shard, w_out_shard):
    bsb = jnp.broadcast_to(b_s[:, :, None], (4, 128, 128))
    qg2, kg2 = jnp.tile(q_g, (1, 2)), jnp.tile(k_g, (1, 2))

    z, h, w_in_full, w_out_full, mod_all, cs = inproj_fwd(chip, x, c_vec, c_ctx, w_ada, b_shard, norm_g, w_in_shard,
                                                          w_out_shard)
    mods = mod_all.transpose(1, 0, 2).reshape(CS_ROWS, 3 * DM)
    mod = lax.dynamic_slice(mods, (8 * dev, 0), (1, 3 * DM))
    shift, scale, gate = mod[:, :DM], mod[:, DM:2 * DM], mod[:, 2 * DM:]
    cshift, cscale = mods[8 * NDEV:8 * NDEV + 1, :DM], mods[8 * NDEV:8 * NDEV + 1, DM:2 * DM]
    zc, hc = ctx_fwd(ctx, cshift, cscale, norm_g, w_in_full)
    out_a = sgu_fwd(z, sgu_g, w_s, bsb)
    out_b, *saved = attn_fwd(z, zc, _lane_pad_rpb(rpb), qg2, kg2)
    loss8, dy, dcat, dgate, dwo = outproj(out_a, out_b, x, target, gate, w_out_full.reshape(DM, DM))
    dz_a, dsg, dws, dbsb = sgu_bwd(z, sgu_g, w_s, bsb, dcat)
    dq, dk, dv, dbg, dck, dcv, drpb, dqg2, dkg2 = attn_bwd(z, zc, qg2, kg2, dcat, saved)
    drpb = drpb[:, :, :rpb.shape[-1]]
    dz_parts = (dz_a, dq, dk, dv, dbg)
    dcshift, dcscale, dng_c = ctx_bwd(dck, dcv, w_in_full, ctx, cshift, cscale, norm_g)
    wire_i, keep_i, wire_o, keep_o = dw_bwd(h, dz_parts, hc, dck, dcv, dwo.reshape(NCHIP, SHARD_OUT, DM))
    *in_flight, token = rs_start(wire_i, wire_o)
    grad_x, dshift, dscale, dng = dh_bwd(dz_parts, w_in_full, x, dy, shift, scale, norm_g, dng_c + token[0, 0])
    got_i, got_o = rs_wait(*in_flight, dshift)
    return dict(
        loss=loss8[0:1, 0:1], grad_x=grad_x, rs=(keep_i, got_i, keep_o, got_o), cs=cs,
        dmod=jnp.concatenate([dshift, dscale, dgate], axis=-1),
        dcmod=jnp.concatenate([dcshift, dcscale, jnp.zeros((1, DM), F32)], axis=-1),
        d_norm_g=dng, d_sgu_g=dsg, d_w_s=dws, d_b_s=dbsb[:, :, 0],
        d_q_g=dqg2[:, :HDIM], d_k_g=dkg2[:, :HDIM], d_rpb=drpb)


def _me():
    return lax.axis_index("x"), lax.axis_index("y"), lax.axis_index("c")


def _flip(q):
    x, y, c = _me()
    return ((1 - x) if q & 4 else x, (1 - y) if q & 2 else y, (1 - c) if q & 1 else c)


def _chip_of(dev):
    return 2 * dev[0] + dev[1]


def _rcopy(src, dst, send_sems, recv_sems, k, dev):
    return pltpu.make_async_remote_copy(src_ref=src, dst_ref=dst, send_sem=send_sems.at[k], recv_sem=recv_sems.at[k],
                                        device_id=dev, device_id_type=MESH_ID)


_VMEM_SPEC = pl.BlockSpec(memory_space=pltpu.VMEM)
SLAB_ROWS = 80


RS_SHAPES = ((DM // 2, SHARD_IN), (SHARD_OUT // 2, DM))
_HBM_SPEC = pl.BlockSpec(memory_space=pltpu.HBM)
_SEM_SPEC = pl.BlockSpec(memory_space=pltpu.SEMAPHORE)
_IN_FLIGHT = pltpu.SideEffectType.DATAFLOW_SIDE_EFFECTING


def _rs_copies(wires, lands, send_sems, recv_sems):
    return [pltpu.make_async_remote_copy(
        src_ref=wires[n].at[_chip_of(_flip(q))], dst_ref=lands[n].at[q // 2 - 1],
        send_sem=send_sems.at[3 * n + q // 2 - 1], recv_sem=recv_sems.at[3 * n + q // 2 - 1],
        device_id=_flip(q), device_id_type=MESH_ID) for n in (0, 1) for q in (2, 4, 6)]


def rs_start(wire_i, wire_o):
    lands = [lax.empty((NCHIP - 1, rh, w), BF16) for rh, w in RS_SHAPES]

    def body(wi_ref, wo_ref, li_ref, lo_ref, send_sems, recv_sems, wi_thru, wo_thru, li_thru, lo_thru, token):
        for cp in _rs_copies((wi_ref, wo_ref), (li_ref, lo_ref), send_sems, recv_sems):
            cp.start()
        token[...] = jnp.zeros_like(token)

    hbm = lambda a: pltpu.HBM(a.shape, a.dtype)
    return pl.pallas_call(
        body, name="rs_start",
        out_shape=(pltpu.SemaphoreType.DMA((6,)), pltpu.SemaphoreType.DMA((6,)), hbm(wire_i), hbm(wire_o),
                   hbm(lands[0]), hbm(lands[1]), jax.ShapeDtypeStruct((8, 128), F32)),
        in_specs=(_HBM_SPEC,) * 4, out_specs=(_SEM_SPEC, _SEM_SPEC) + (_HBM_SPEC,) * 4 + (_VMEM_SPEC,),
        input_output_aliases={0: 2, 1: 3, 2: 4, 3: 5},
        compiler_params=pltpu.CompilerParams(has_side_effects=_IN_FLIGHT),
    )(*[pltpu.with_memory_space_constraint(a, pltpu.HBM) for a in (wire_i, wire_o, *lands)])


def rs_wait(send_sems, recv_sems, wire_i, wire_o, land_i, land_o, after):
    def body(wi_ref, wo_ref, li_ref, lo_ref, send_sems, recv_sems, after_ref, wi_dead, wo_dead, gi_ref, go_ref):
        for cp in _rs_copies((wi_ref, wo_ref), (li_ref, lo_ref), send_sems, recv_sems):
            cp.wait_send()
            cp.wait_recv()

    hbm = lambda a: pltpu.HBM(a.shape, a.dtype)
    return pl.pallas_call(
        body, name="rs_wait", out_shape=(hbm(wire_i), hbm(wire_o), hbm(land_i), hbm(land_o)),
        in_specs=(_HBM_SPEC,) * 4 + (_SEM_SPEC, _SEM_SPEC, pl.BlockSpec(memory_space=pl.ANY)),
        out_specs=(_HBM_SPEC,) * 4, input_output_aliases={0: 0, 1: 1, 2: 2, 3: 3},
        compiler_params=pltpu.CompilerParams(has_side_effects=_IN_FLIGHT),
    )(wire_i, wire_o, land_i, land_o, send_sems, recv_sems, after)[2:]


def final_reduce(keep_i, got_i, keep_o, got_o, slab, cs, w_ada, c_ctx):
    (rhi, wi), (rho, wo) = RS_SHAPES

    def kern(ki_hbm, gi_hbm, ko_hbm, go_hbm, s_ref, cs_ref, w_hbm, cc_ref,
             gin_ref, gout_ref, tot_ref, dw_ref, db_ref, dcc_ref,
             ki, gi, ko, go, w_scr, all_ref, dms_scr, parts, load_sems, send_sems, recv_sems):
        x, y, c = _me()
        k = 2 * x + y
        sib = _flip(1)
        dev = lambda d: 4 * d[0] + 2 * d[1] + d[2]
        me = dev((x, y, c))

        def slab_copy(idx, owner, to):
            return _rcopy(all_ref.at[dev(owner)], all_ref.at[dev(owner)], send_sems, recv_sems, idx, to)

        all_ref[me] = s_ref[...]
        first = [slab_copy(0, (x, y, c), sib)] + [slab_copy(q // 2, (x, y, c), _flip(q)) for q in (2, 4, 6)]
        for cp in first:
            cp.start()
        loads = [pltpu.make_async_copy(src, dst, load_sems.at[n]) for n, (src, dst) in enumerate(
            ((ki_hbm, ki), (gi_hbm, gi), (ko_hbm, ko), (go_hbm, go), (w_hbm, w_scr)))]
        for cp in loads:
            cp.start()

        shares = []
        for n, (keep, got, out) in enumerate(((ki, gi, gin_ref), (ko, go, gout_ref))):
            rh = RS_SHAPES[n][0]
            half = lambda hh, rh=rh: pl.ds(pl.multiple_of(hh * rh, rh), rh)
            loads[2 * n].wait()
            loads[2 * n + 1].wait()
            out[half(c), :] = ((keep[...] + got[0].astype(F32)) + got[1].astype(F32)) + got[2].astype(F32)
            share = _rcopy(out.at[half(c), :], out.at[half(c), :], send_sems, recv_sems, 7 + n, sib)
            share.start()
            shares.append((share, _rcopy(out.at[half(1 - c), :], out.at[half(1 - c), :], send_sems, recv_sems, 7 + n,
                                         sib)))

        passed = []
        for q in (2, 4, 6):
            slab_copy(q // 2, _flip(q), (x, y, c)).wait_recv()
            cp = slab_copy(3 + q // 2, _flip(q), sib)
            cp.start()
            passed.append(cp)
        slab_copy(0, sib, (x, y, c)).wait_recv()
        for q in (2, 4, 6):
            slab_copy(3 + q // 2, _flip(q | 1), (x, y, c)).wait_recv()
        tot = all_ref[0]
        for d in range(1, NDEV):
            tot = tot + all_ref[d]
        tot_ref[...] = tot

        pad = jnp.zeros((7, DM), F32)
        dm = [jnp.concatenate([all_ref[d, 12 + j:13 + j, :] for d in range(NDEV)] + [tot[9 + j:10 + j, :], pad], axis=0)
              for j in range(3)]
        db_ref[...] = jnp.concatenate([jnp.sum(part, axis=0, keepdims=True) for part in dm], axis=0)
        dm = jnp.concatenate(dm, axis=-1)
        for j in range(NCHIP):
            @pl.when(k == j)
            def _():
                dms_scr[...] = dm[:, j * SHARD_ADA:(j + 1) * SHARD_ADA].astype(BF16)

        a_in = jnp.concatenate([cs_ref[8 * d:8 * d + 1, :] for d in range(NDEV)]
                               + [cs_ref[8 * NDEV:8 * NDEV + 1, :], pad], axis=0)
        act = jax.nn.silu(a_in).astype(BF16)
        dms = dms_scr[...]
        dw_ref[...] = lax.dot_general(act, dms, (((0,), (0,)), ((), ())), preferred_element_type=F32)
        loads[4].wait()
        parts[k] = lax.dot_general(dms, w_scr[...].astype(BF16), (((1,), (1,)), ((), ())), preferred_element_type=F32)
        sends = [_rcopy(parts.at[k], parts.at[k], send_sems, recv_sems, 8 + q // 2, _flip(q)) for q in (2, 4, 6)]
        for cp in sends:
            cp.start()
        for q in (2, 4, 6):
            kq = _chip_of(_flip(q))
            _rcopy(parts.at[kq], parts.at[kq], send_sems, recv_sems, 8 + q // 2, _flip(q)).wait_recv()
        dact = ((parts[0] + parts[1]) + parts[2]) + parts[3]
        _, vjp = jax.vjp(jax.nn.silu, cc_ref[...])
        dcc_ref[...] = vjp(dact[8:9, :])[0]

        for share, arrival in shares:
            arrival.wait_recv()
            share.wait_send()
        for cp in first + passed + sends:
            cp.wait_send()

    any_spec = pl.BlockSpec(memory_space=pl.ANY)
    return pl.pallas_call(
        kern, name="final_reduce",
        in_specs=[any_spec] * 4 + [_VMEM_SPEC, _VMEM_SPEC, any_spec, _VMEM_SPEC], out_specs=[_VMEM_SPEC] * 6,
        out_shape=[jax.ShapeDtypeStruct((2 * rhi, wi), F32), jax.ShapeDtypeStruct((2 * rho, wo), F32),
                   jax.ShapeDtypeStruct((SLAB_ROWS, DM), F32), jax.ShapeDtypeStruct((DM, SHARD_ADA), F32),
                   jax.ShapeDtypeStruct((3, DM), F32), jax.ShapeDtypeStruct((1, DM), F32)],
        scratch_shapes=[pltpu.VMEM((rhi, wi), F32), pltpu.VMEM((NCHIP - 1, rhi, wi), BF16),
                        pltpu.VMEM((rho, wo), F32), pltpu.VMEM((NCHIP - 1, rho, wo), BF16),
                        pltpu.VMEM((DM, SHARD_ADA), F32), pltpu.VMEM((NDEV, SLAB_ROWS, DM), F32),
                        pltpu.VMEM((16, SHARD_ADA), BF16), pltpu.VMEM((NCHIP, 16, DM), F32),
                        pltpu.SemaphoreType.DMA((5,)), pltpu.SemaphoreType.DMA((12,)), pltpu.SemaphoreType.DMA((12,))],
        compiler_params=pltpu.CompilerParams(vmem_limit_bytes=40 * 1024 * 1024),
    )(keep_i, got_i, keep_o, got_o, slab, cs, w_ada, c_ctx)


def _adamw_math(w, g, m, v):
    m = B1 * m + (1.0 - B1) * g
    v = B2 * v + (1.0 - B2) * (g * g)
    m_hat = m / (1.0 - B1 ** STEP)
    v_hat = v / (1.0 - B2 ** STEP)
    return -LR * (m_hat / (jnp.sqrt(v_hat) + ADAM_EPS) + WD * w), m, v


def adamw_big(w, g, m, v, name, block_rows=256):
    rows, width = w.shape

    def kern(w_ref, g_ref, m_ref, v_ref, d_ref, nm_ref, nv_ref):
        d_ref[...], nm_ref[...], nv_ref[...] = _adamw_math(w_ref[...], g_ref[...], m_ref[...], v_ref[...])

    spec = pl.BlockSpec((block_rows, width), lambda i: (i, 0))
    return pl.pallas_call(
        kern, name=name, grid=(rows // block_rows,), in_specs=[spec] * 4, out_specs=[spec] * 3,
        out_shape=[jax.ShapeDtypeStruct((rows, width), F32)] * 3,
        compiler_params=_cparams(("arbitrary",)),
    )(w, g, m, v)


def adamw_small(quads):
    n = len(quads)

    def kern(*refs):
        ins, outs = refs[:4 * n], refs[4 * n:]
        for i in range(n):
            w, g, m, v = (r[...] for r in ins[4 * i:4 * i + 4])
            outs[3 * i][...], outs[3 * i + 1][...], outs[3 * i + 2][...] = _adamw_math(w, g, m, v)

    flat = [a for quad in quads for a in quad]
    res = pl.pallas_call(
        kern, name="adamw_small", in_specs=[_VMEM_SPEC] * (4 * n), out_specs=[_VMEM_SPEC] * (3 * n),
        out_shape=[jax.ShapeDtypeStruct(q[0].shape, F32) for q in quads for _ in range(3)],
    )(*flat)
    return [tuple(res[3 * i:3 * i + 3]) for i in range(n)]


def _rows_of(a, rows):
    flat = a.reshape(-1)
    return jnp.pad(flat, (0, rows * DM - flat.shape[0])).reshape(rows, DM)


def kernel(x, c, ctx, c_ctx, w_ada, b_ada, norm_g, w_in, sgu_norm_g, w_spatial, b_spatial, q_norm_g, k_norm_g, rpb, w_out, loss_target, m_c_ctx, m_w_ada, m_b_ada, m_norm_g, m_w_in, m_sgu_norm_g, m_w_spatial, m_b_spatial, m_q_norm_g, m_k_norm_g, m_rpb, m_w_out, v_c_ctx, v_w_ada, v_b_ada, v_norm_g, v_w_in, v_sgu_norm_g, v_w_spatial, v_b_spatial, v_q_norm_g, v_k_norm_g, v_rpb, v_w_out):
    xi, yi, ci = lax.axis_index("x"), lax.axis_index("y"), lax.axis_index("c")
    chip, dev = 2 * xi + yi, 4 * xi + 2 * yi + ci
    c_ctx2 = c_ctx.reshape(1, DM)

    b_shard = lax.dynamic_slice(b_ada, (0, chip * SHARD_ADA), (1, SHARD_ADA))
    part = local_step(chip.reshape(1).astype(jnp.int32), dev, x[0], c, c_ctx2, w_ada[0], b_shard, ctx[0], loss_target[0],
                      norm_g, sgu_norm_g, w_spatial[0], b_spatial[0], q_norm_g, k_norm_g, rpb[0], w_in[0], w_out[0])
    cs = part["cs"]

    slab = jnp.concatenate([
        part["d_norm_g"], _rows_of(part["d_sgu_g"], 1), _rows_of(part["d_b_s"], 1),
        _rows_of(jnp.concatenate([part["d_q_g"], part["d_k_g"]], axis=-1), 1), _rows_of(part["d_rpb"], 4),
        _rows_of(part["loss"], 1), _rows_of(part["dcmod"], 3), _rows_of(part["dmod"], 3), jnp.zeros((1, DM), F32),
        _rows_of(part["d_w_s"], 64)], axis=0)
    g_w_in, g_w_out, tot, g_w_ada, g_b_ada, g_c_ctx = final_reduce(*part["rs"], slab, cs, w_ada[0], c_ctx2)
    g_b_ada = g_b_ada.reshape(1, 3 * DM)

    loss = tot[8, 0]
    g_small = dict(
        c_ctx=g_c_ctx, b_ada=g_b_ada, norm_g=tot[0:1], sgu_norm_g=tot[1:2, :512], w_spatial=tot[16:80].reshape(512, 128),
        b_spatial=tot[2:3, :512].reshape(4, 128), q_norm_g=tot[3:4, :HDIM], k_norm_g=tot[3:4, HDIM:2 * HDIM],
        rpb=tot[4:8].reshape(-1)[:HEADS * 15 * 31].reshape(HEADS * 15, 31))
    shapes = dict(c_ctx=(DM,), w_ada=(1, DM, SHARD_ADA), b_ada=(1, 3 * DM), norm_g=(1, DM), w_in=(1, DM, SHARD_IN),
                  sgu_norm_g=(1, 512), w_spatial=(1, 4, 128, 128), b_spatial=(1, 4, 128), q_norm_g=(1, HDIM),
                  k_norm_g=(1, HDIM), rpb=(1, HEADS, 15, 31), w_out=(1, SHARD_OUT, DM))
    names = list(shapes)
    weights = dict(c_ctx=c_ctx, w_ada=w_ada, b_ada=b_ada, norm_g=norm_g, w_in=w_in, sgu_norm_g=sgu_norm_g,
                   w_spatial=w_spatial, b_spatial=b_spatial, q_norm_g=q_norm_g, k_norm_g=k_norm_g, rpb=rpb, w_out=w_out)
    m_in = dict(zip(names, (m_c_ctx, m_w_ada, m_b_ada, m_norm_g, m_w_in, m_sgu_norm_g, m_w_spatial, m_b_spatial,
                            m_q_norm_g, m_k_norm_g, m_rpb, m_w_out)))
    v_in = dict(zip(names, (v_c_ctx, v_w_ada, v_b_ada, v_norm_g, v_w_in, v_sgu_norm_g, v_w_spatial, v_b_spatial,
                            v_q_norm_g, v_k_norm_g, v_rpb, v_w_out)))
    grads = dict(g_small, w_ada=g_w_ada, w_in=g_w_in, w_out=g_w_out)
    upd = {}
    for n in ("w_ada", "w_in", "w_out"):
        g = grads[n]
        upd[n] = adamw_big(weights[n].reshape(g.shape), g, m_in[n].reshape(g.shape), v_in[n].reshape(g.shape),
                           "adamw_" + n)
    small = [n for n in names if n not in upd]
    res = adamw_small([(weights[n].reshape(grads[n].shape), grads[n], m_in[n].reshape(grads[n].shape),
                        v_in[n].reshape(grads[n].shape)) for n in small])
    upd.update(zip(small, res))
    out = [loss, part["grad_x"].reshape(1, SEQ, DM)]
    out += [grads[n].reshape(shapes[n]) for n in names]
    for slot in range(3):
        out += [upd[n][slot].reshape(shapes[n]) for n in names]
    return tuple(out)
```

```python
import functools

import jax
import jax.numpy as jnp
from jax import lax
from jax.experimental import pallas as pl
from jax.experimental.pallas import tpu as pltpu

F32, BF16 = jnp.float32, jnp.bfloat16
SEQ, DM, CTX, DIN = 4096, 1024, 256, 3584
NCHIP, NDEV = 4, 8
SHARD_IN = DIN // NCHIP
SHARD_ADA = 3 * DM // NCHIP
SHARD_OUT = DM // NCHIP
GRID_W = 64
QROWS = 4
KROWS = 12
QBLK, KBLK = QROWS * GRID_W, KROWS * GRID_W
NQBLK = SEQ // QBLK
HEADS, HDIM, NPAIR = 8, 64, 4
EPS = 1e-6
NEG_INF = -1e30
ZQ, ZK, ZV, ZG = 12, 16, 20, 24
LR, B1, B2, ADAM_EPS, WD, STEP = 0.001, 0.9, 0.999, 1e-08, 0.01, 10
VMEM_BIG = 56 * 1024 * 1024
MESH_ID = pl.DeviceIdType.MESH


def _dot(a, b, lhs_c, rhs_c):
    return lax.dot_general(a.astype(BF16), b.astype(BF16), (((lhs_c,), (rhs_c,)), ((), ())),
                           preferred_element_type=F32)


@jax.custom_vjp
def mm(a, b):
    return _dot(a, b, 1, 0)


@jax.custom_vjp
def mm_nt(a, b):
    return _dot(a, b, 1, 1)


@jax.custom_vjp
def mm_tn(a, b):
    return _dot(a, b, 0, 0)


mm.defvjp(lambda a, b: (mm(a, b), (a, b)), lambda r, ct: (mm_nt(ct, r[1]), mm_tn(r[0], ct)))
mm_nt.defvjp(lambda a, b: (mm_nt(a, b), (a, b)), lambda r, ct: (mm(ct, r[1]), mm_tn(ct, r[0])))
mm_tn.defvjp(lambda a, b: (mm_tn(a, b), (a, b)), lambda r, ct: (mm_nt(r[1], ct), mm(r[0], ct)))


def _rms(x, g):
    return x * lax.rsqrt(jnp.mean(x * x, axis=-1, keepdims=True) + EPS) * g


def _modulated(x, g, scale, shift):
    return _rms(x, g) * (1.0 + scale) + shift


def _pair_rms(x, g2):
    lo = lax.broadcasted_iota(jnp.int32, (1, 2 * HDIM), 1) < HDIM
    sq = x * x
    s_lo = jnp.sum(jnp.where(lo, sq, 0.0), axis=-1, keepdims=True)
    s_hi = jnp.sum(jnp.where(lo, 0.0, sq), axis=-1, keepdims=True)
    rs = jnp.where(lo, lax.rsqrt(s_lo / HDIM + EPS), lax.rsqrt(s_hi / HDIM + EPS))
    return x * rs * g2


def _cparams(sem, vmem=None):
    return pltpu.CompilerParams(dimension_semantics=sem, vmem_limit_bytes=vmem)


def _row(n):
    return pl.BlockSpec((1, n), lambda *_: (0, 0))


CS_ROWS = 8 * NDEV + 8


def _mod_part(mod_ref, row, part):
    pieces = []
    for j in range(NCHIP):
        lo, hi = max(part * DM, j * SHARD_ADA), min((part + 1) * DM, (j + 1) * SHARD_ADA)
        if lo < hi:
            pieces.append(mod_ref[j, row, lo - j * SHARD_ADA:hi - j * SHARD_ADA])
    return jnp.concatenate(pieces, axis=-1)


def inproj_fwd(chip, x, c_vec, c_ctx, w_ada, b_shard, norm_g, w_shard, wo_shard):
    tl = 1024
    nt = SEQ // tl
    halves = (DM // 2, SHARD_OUT // 2)
    n_w, n_c = 12, NDEV - 1

    def kern(k_ref, x_ref, cv_ref, cc_ref, wa_ref, b_ref, g_ref, w_ref, wo_ref,
             z_ref, h_ref, wfull_ref, wofull_ref, modall_ref, csall_ref,
             w_scr, wo_scr, h_scr, mine, cs_scr, mod_scr, shsc_scr, send_sems, recv_sems, out_sems):
        s, t = pl.program_id(0), pl.program_id(1)
        xi, yi, c = _me()
        k, me = 2 * xi + yi, 4 * xi + 2 * yi + c
        sib = _flip(1)
        rows = pl.ds(pl.multiple_of(t * tl, tl), tl)
        gathered = (w_scr, wo_scr)
        slot = lambda d: pl.ds(pl.multiple_of(8 * d, 8), 8)

        def c_copy(q, owner):
            return _rcopy(mine, cs_scr.at[slot(owner), :], send_sems, recv_sems, n_w + q - 1, _flip(q))

        def m_copy(q, chip_of_block):
            return _rcopy(mod_scr.at[chip_of_block], mod_scr.at[chip_of_block], send_sems, recv_sems,
                          n_w + n_c + q // 2 - 1, _flip(q))

        def adaln():
            first = lax.broadcasted_iota(jnp.int32, (8, DM), 0) == 0
            mine[...] = jnp.where(first, jnp.broadcast_to(cv_ref[...], (8, DM)), 0.0)
            cs_scr[slot(me), :] = mine[...]
            cs_scr[slot(NDEV), :] = jnp.where(first, jnp.broadcast_to(cc_ref[...], (8, DM)), 0.0)
            for q in range(1, NDEV):
                c_copy(q, me).start()
            wa = wa_ref[...].astype(BF16)
            for q in range(1, NDEV):
                px, py, pc = _flip(q)
                c_copy(q, 4 * px + 2 * py + pc).wait_recv()
            act = jax.nn.silu(cs_scr[...]).astype(BF16)
            mod_scr[k] = jnp.dot(act, wa, preferred_element_type=F32) + b_ref[...]
            for q in (2, 4, 6):
                m_copy(q, k).start()
            for q in (2, 4, 6):
                m_copy(q, _chip_of(_flip(q))).wait_recv()
            row = pl.ds(8 * me, 1)
            shsc_scr[0:1, :] = _mod_part(mod_scr, row, 0)
            shsc_scr[1:2, :] = _mod_part(mod_scr, row, 1)
            pltpu.sync_copy(mod_scr, modall_ref)
            pltpu.sync_copy(cs_scr, csall_ref)

        def block(n, chip_of_block, hh):
            return gathered[n].at[chip_of_block, pl.ds(pl.multiple_of(hh * halves[n], halves[n]), halves[n]), :]

        def ici(n, q, chip_of_block):
            blk = block(n, chip_of_block, c)
            return _rcopy(blk, blk, send_sems, recv_sems, 6 * n + q // 2 - 1, _flip(q))

        def d2d(n, q, chip_of_block, hh):
            blk = block(n, chip_of_block, hh)
            return _rcopy(blk, blk, send_sems, recv_sems, 6 * n + 3 + q // 2 - 1, sib)

        @pl.when((s == 0) & (t == 0))
        def _():
            adaln()
            w_scr[k] = w_ref[...].astype(BF16)
            wo_scr[k] = wo_ref[...].astype(BF16)
            for q in (2, 4, 6):
                ici(0, q, k).start()
                ici(1, q, k).start()

        for sweep in (1, 2, 3):
            @pl.when((s == sweep) & (t == 0))
            def _():
                q = 2 * sweep
                src = _chip_of(_flip(q))
                for n in (0, 1):
                    ici(n, q, src).wait_recv()
                    d2d(n, q, src, c).start()
                for n in (0, 1):
                    d2d(n, q, src, 1 - c).wait_recv()

        @pl.when(s == 0)
        def _():
            hb = _modulated(x_ref[...], g_ref[...], shsc_scr[1:2, :], shsc_scr[0:1, :]).astype(BF16)
            h_scr[rows, :] = hb
            h_ref[...] = hb

        z_ref[...] = jnp.dot(h_scr[rows, :], w_scr[lax.bitwise_xor(k, s)], preferred_element_type=F32)

        @pl.when((s == NCHIP - 1) & (t == nt - 1))
        def _():
            for q in range(1, NDEV):
                c_copy(q, me).wait_send()
            for q in (2, 4, 6):
                m_copy(q, k).wait_send()
            for n in (0, 1):
                for q in (2, 4, 6):
                    ici(n, q, k).wait_send()
                    d2d(n, q, _chip_of(_flip(q)), c).wait_send()
            outs = [pltpu.make_async_copy(w_scr.at[j], wfull_ref.at[:, j * SHARD_IN:(j + 1) * SHARD_IN], out_sems.at[j])
                    for j in range(NCHIP)] + [pltpu.make_async_copy(wo_scr, wofull_ref, out_sems.at[NCHIP])]
            for cp in outs:
                cp.start()
            for cp in outs:
                cp.wait()

    once = lambda s, t, k: (jnp.where(s == 0, t, nt - 1), 0)
    hbm = pl.BlockSpec(memory_space=pl.ANY)
    n_sem = n_w + n_c + 3
    return pl.pallas_call(
        kern, name="inproj_fwd",
        grid_spec=pltpu.PrefetchScalarGridSpec(
            num_scalar_prefetch=1, grid=(NCHIP, nt),
            in_specs=[pl.BlockSpec((tl, DM), once)] + [_VMEM_SPEC] * 7,
            out_specs=[pl.BlockSpec((tl, SHARD_IN), lambda s, t, k: (t, lax.bitwise_xor(k[0], s))),
                       pl.BlockSpec((tl, DM), once), hbm, hbm, hbm, hbm],
            scratch_shapes=[pltpu.VMEM((NCHIP, DM, SHARD_IN), BF16), pltpu.VMEM((NCHIP, SHARD_OUT, DM), BF16),
                            pltpu.VMEM((SEQ, DM), BF16), pltpu.VMEM((8, DM), F32), pltpu.VMEM((CS_ROWS, DM), F32),
                            pltpu.VMEM((NCHIP, CS_ROWS, SHARD_ADA), F32), pltpu.VMEM((8, DM), F32),
                            pltpu.SemaphoreType.DMA((n_sem,)), pltpu.SemaphoreType.DMA((n_sem,)),
                            pltpu.SemaphoreType.DMA((NCHIP + 1,))]),
        out_shape=[jax.ShapeDtypeStruct((SEQ, DIN), F32), jax.ShapeDtypeStruct((SEQ, DM), BF16),
                   jax.ShapeDtypeStruct((DM, DIN), BF16), jax.ShapeDtypeStruct((NCHIP, SHARD_OUT, DM), BF16),
                   jax.ShapeDtypeStruct((NCHIP, CS_ROWS, SHARD_ADA), F32), jax.ShapeDtypeStruct((CS_ROWS, DM), F32)],
        compiler_params=_cparams(("arbitrary", "arbitrary"), VMEM_BIG),
    )(chip, x, c_vec, c_ctx, w_ada, b_shard, norm_g, w_shard, wo_shard)


def ctx_fwd(ctx, cshift, cscale, norm_g, w_full):
    def kern(c_ref, sh_ref, sc_ref, g_ref, w_ref, zc_ref, hc_ref):
        hc = _modulated(c_ref[...], g_ref[...], sc_ref[...], sh_ref[...]).astype(BF16)
        hc_ref[...] = hc
        zc_ref[...] = jnp.dot(hc, w_ref[...], preferred_element_type=F32)

    return pl.pallas_call(
        kern, name="ctx_fwd", grid=(1,),
        in_specs=[pl.BlockSpec((CTX, DM), lambda i: (0, 0)), _row(DM), _row(DM), _row(DM),
                  pl.BlockSpec((DM, 2 * SHARD_IN), lambda i: (0, 1))],
        out_specs=[pl.BlockSpec((CTX, 2 * SHARD_IN), lambda i: (0, 0)),
                   pl.BlockSpec((CTX, DM), lambda i: (0, 0))],
        out_shape=[jax.ShapeDtypeStruct((CTX, 2 * SHARD_IN), F32), jax.ShapeDtypeStruct((CTX, DM), BF16)],
        compiler_params=_cparams(("arbitrary",)),
    )(ctx, cshift, cscale, norm_g, w_full)


SGU_CHUNK, SGU_PER_STEP = 128, 4


def _gelu(x):
    return 0.5 * x * (1.0 + lax.erf(x * 0.7071067811865476))


def _sgu_chunk(au, av, ag, sg, ws, bsb):
    u, v = _gelu(au), _gelu(av)
    outs = []
    for g in range(4):
        sl = slice(128 * g, 128 * (g + 1))
        mixed = mm(ws[g], _rms(v[:, sl], sg[:, sl])) + bsb[g]
        outs.append(u[:, sl] * mixed * jax.nn.silu(ag[:, sl]))
    return jnp.concatenate(outs, axis=-1)


def _sgu_specs():
    rows = SGU_CHUNK * SGU_PER_STEP
    zspec = lambda c: pl.BlockSpec((rows, 512), lambda n: (n, c))
    wspec = pl.BlockSpec((4, 128, 128), lambda n: (0, 0, 0))
    return rows, [zspec(0), zspec(1), zspec(2), _row(512), wspec, wspec]


def sgu_fwd(z, sg, ws, bsb):
    rows, in_specs = _sgu_specs()

    def kern(au_ref, av_ref, ag_ref, sg_ref, ws_ref, bs_ref, o_ref):
        for c in range(SGU_PER_STEP):
            sl = slice(c * SGU_CHUNK, (c + 1) * SGU_CHUNK)
            o_ref[sl, :] = _sgu_chunk(au_ref[sl, :], av_ref[sl, :], ag_ref[sl, :], sg_ref[...], ws_ref[...],
                                      bs_ref[...])

    return pl.pallas_call(
        kern, name="sgu_fwd", grid=(SEQ // rows,), in_specs=in_specs,
        out_specs=pl.BlockSpec((rows, 512), lambda n: (n, 0)),
        out_shape=jax.ShapeDtypeStruct((SEQ, 512), F32),
        compiler_params=_cparams(("arbitrary",)),
    )(z, z, z, sg, ws, bsb)


_DR_OFF = (7, 3, -1)


def _row_valid(v, rr, j):
    return (j < 8, rr <= j < rr + 8, 4 <= j < 12)[v]


def _col_window():
    q = lax.broadcasted_iota(jnp.int32, (GRID_W, 128), 0)
    kc = lax.broadcasted_iota(jnp.int32, (GRID_W, 128), 1) % GRID_W
    c0 = jnp.clip(q - 8, 0, GRID_W - 16)
    return (kc >= c0) & (kc < c0 + 16)


def _bias_tiles(base, store):
    lo = lax.broadcasted_iota(jnp.int32, (1, 128), 1) < GRID_W
    win = _col_window()
    tiles = {}
    for v in range(3):
        for rr in range(QROWS):
            for jp in range(KROWS // 2):
                j0, j1 = 2 * jp, 2 * jp + 1
                ok0, ok1 = _row_valid(v, rr, j0), _row_valid(v, rr, j1)
                key = (j0 - rr + _DR_OFF[v], ok0, ok1) if (ok0 or ok1) else None
                if key not in tiles:
                    if key is None:
                        tiles[key] = jnp.full((GRID_W, 128), NEG_INF, F32)
                    else:
                        d0 = key[0]
                        r0 = base[d0:d0 + 1, :] if ok0 else jnp.zeros((1, 128), F32)
                        r1 = base[d0 + 1:d0 + 2, :] if ok1 else jnp.zeros((1, 128), F32)
                        y = jnp.broadcast_to(jnp.where(lo, r0, r1), (GRID_W, 128))
                        y = pltpu.roll(pltpu.roll(y, 128 - 15, 1), 0, 1, stride=1, stride_axis=0)
                        tiles[key] = jnp.where(win & jnp.where(lo, ok0, ok1), y, NEG_INF)
                store(v, slice(rr * GRID_W, (rr + 1) * GRID_W), slice(jp * 128, (jp + 1) * 128), tiles[key])


def _rpb_grad(load):
    lo = lax.broadcasted_iota(jnp.int32, (1, 128), 1) < GRID_W
    ri = lax.broadcasted_iota(jnp.int32, (GRID_W, GRID_W), 0)
    ci = lax.broadcasted_iota(jnp.int32, (GRID_W, GRID_W), 1)
    flip = (ri + ci == GRID_W - 1).astype(F32)
    groups = {}
    for v in range(3):
        for rr in range(QROWS):
            for jp in range(KROWS // 2):
                j0, j1 = 2 * jp, 2 * jp + 1
                ok0, ok1 = _row_valid(v, rr, j0), _row_valid(v, rr, j1)
                if not (ok0 or ok1):
                    continue
                g = load(v, slice(rr * GRID_W, (rr + 1) * GRID_W), slice(jp * 128, (jp + 1) * 128))
                key = (j0 - rr + _DR_OFF[v], ok0, ok1)
                groups[key] = g if key not in groups else groups[key] + g
    acc = [jnp.zeros((1, 128), F32) for _ in range(15)]
    for (d0, ok0, ok1), g in groups.items():
        g = lax.dot_general(flip, g, (((1,), (0,)), ((), ())), precision=lax.Precision.HIGHEST,
                            preferred_element_type=F32)
        g = pltpu.roll(pltpu.roll(g, 128 - 48, 1), 0, 1, stride=1, stride_axis=0)
        s = jnp.sum(g, axis=0, keepdims=True)
        if ok0:
            acc[d0] = acc[d0] + jnp.where(lo, s, 0.0)
        if ok1:
            acc[d0 + 1] = acc[d0 + 1] + jnp.where(lo, 0.0, s)
    return [row + pltpu.roll(row, GRID_W, 1) for row in acc]


def _scaled_q(q_raw, qg):
    return _pair_rms(q_raw, qg) * (HDIM ** -0.5)


def _head_lanes():
    lo = lax.broadcasted_iota(jnp.int32, (1, 2 * HDIM), 1) < HDIM
    return lo, jnp.logical_not(lo)


SOFTMAX_ROWS = 32


def _emit_interleaved(vector_work, matmul_work):
    for j in range(max(len(vector_work), len(matmul_work))):
        for work in (vector_work, matmul_work):
            if j < len(work):
                work[j]()


def _kblock(i):
    return jnp.clip(i - 1, 0, (SEQ - KBLK) // QBLK)


def _kstart(i):
    return pl.multiple_of(_kblock(i) * QBLK, QBLK)


ATTN_BLOCKS = 4
TILE_BUFFERS = 4
ATTN_STEPS = NQBLK // ATTN_BLOCKS
ATTN_ROWS = ATTN_BLOCKS * QBLK


def _bias_variant(i, b):
    if b == 0:
        return jnp.where(i == 0, 0, 1)
    if b == ATTN_BLOCKS - 1:
        return jnp.where(i == ATTN_STEPS - 1, 2, 1)
    return 1
KCOLS = QBLK


def _attn_in_specs():
    return [
        pl.BlockSpec((ATTN_ROWS, 128), lambda p, i: (i, ZQ + p)),
        pl.BlockSpec((SEQ, 128), lambda p, i: (0, ZK + p)),
        pl.BlockSpec((SEQ, 128), lambda p, i: (0, ZV + p)),
        pl.BlockSpec((ATTN_ROWS, 128), lambda p, i: (i, ZG + p)),
        pl.BlockSpec((CTX, 128), lambda p, i: (0, 2 + p)),
        pl.BlockSpec((CTX, 128), lambda p, i: (0, 6 + p)),
    ]


def _rpb_spec():
    return pl.BlockSpec((2, 15, 128), lambda p, i: (p, 0, 0))


def _prob_specs():
    return [pl.BlockSpec((2, ATTN_ROWS, KBLK), lambda p, i: (p, i, 0)),
            pl.BlockSpec((2, ATTN_ROWS, CTX), lambda p, i: (p, i, 0))]


NORM_ROWS = 512


def _half_sums(x):
    lo = lax.broadcasted_iota(jnp.int32, (1, 2 * HDIM), 1) < HDIM
    return jnp.where(lo, jnp.sum(jnp.where(lo, x, 0.0), axis=-1, keepdims=True),
                     jnp.sum(jnp.where(lo, 0.0, x), axis=-1, keepdims=True))


def _pair_rms_bwd(x, g2, ct):
    rs = lax.rsqrt(_half_sums(x * x) / HDIM + EPS)
    y = x * rs
    dy = ct * g2
    return rs * (dy - y * (_half_sums(dy * y) / HDIM)), jnp.sum(ct * y, axis=0, keepdims=True)


def _norm_keys(k_ref, ck_ref, kg_ref, kn_scr, ckn_scr):
    def body(c, carry):
        sl = pl.ds(pl.multiple_of(c * NORM_ROWS, NORM_ROWS), NORM_ROWS)
        kn_scr[sl, :] = _pair_rms(k_ref[sl, :], kg_ref[...]).astype(BF16)
        return carry

    lax.fori_loop(0, SEQ // NORM_ROWS, body, 0)
    ckn_scr[...] = _pair_rms(ck_ref[...], kg_ref[...]).astype(BF16)


def _values_with_ones(v_ref, cv_ref, v1_scr, cv1_scr):
    for a, mine in enumerate(_head_lanes()):
        def body(c, carry):
            sl = pl.ds(pl.multiple_of(c * NORM_ROWS, NORM_ROWS), NORM_ROWS)
            v1_scr[a, sl, :] = jnp.where(mine, v_ref[sl, :], 1.0).astype(BF16)
            return carry

        lax.fori_loop(0, SEQ // NORM_ROWS, body, 0)
        cv1_scr[a] = jnp.where(mine, cv_ref[...], 1.0).astype(BF16)


def _pair_major_spec():
    return pl.BlockSpec((1, ATTN_ROWS, 128), lambda p, i: (p, i, 0))


def _normed_key_specs():
    return [pl.BlockSpec((None, SEQ, 128), lambda p, i: (p, 0, 0)), pl.BlockSpec((None, CTX, 128), lambda p, i: (p, 0, 0))]


def attn_fwd(z, zc, rpb2, qg2, kg2):
    def kern(q_ref, k_ref, v_ref, bg_ref, ck_ref, cv_ref, rpb_ref, qg_ref, kg_ref,
             ob_ref, o_ref, rden_ref, pl_ref, pc_ref, kn_ref, ckn_ref, kn_scr, ckn_scr, v1_scr, cv1_scr, s_scr,
             bias_ref):
        i = pl.program_id(1)

        @pl.when(i == 0)
        def _():
            for a in range(2):
                def store(v, tile_rows, tile_cols, tile, a=a):
                    bias_ref[v, a, tile_rows, tile_cols] = tile

                _bias_tiles(rpb_ref[a], store)
            _norm_keys(k_ref, ck_ref, kg_ref, kn_scr, ckn_scr)
            kn_ref[...] = kn_scr[...]
            ckn_ref[...] = ckn_scr[...]
            _values_with_ones(v_ref, cv_ref, v1_scr, cv1_scr)

        heads = _head_lanes()
        tiles = [(b, a) for b in range(ATTN_BLOCKS) for a in range(2)]
        rows = [slice(b * QBLK, (b + 1) * QBLK) for b in range(ATTN_BLOCKS)]
        variant = [_bias_variant(i, b) for b in range(ATTN_BLOCKS)]
        pv = [None] * len(tiles)
        qa, done = {}, {}
        latent = KBLK // KCOLS
        buf = lambda t: t % TILE_BUFFERS

        def keys(b, n):
            return pl.ds(pl.multiple_of(_kstart(ATTN_BLOCKS * i + b) + n * KCOLS, KCOLS), KCOLS)

        def score_piece(t, n):
            b, a = tiles[t]
            cols = slice(n * KCOLS, (n + 1) * KCOLS)
            if n == 0:
                if a == 0:
                    done["qn", b] = _scaled_q(q_ref[rows[b], :], qg_ref[...])
                qa[t] = jnp.where(heads[a], done["qn", b], 0.0).astype(BF16)
            if n < latent:
                s_scr[buf(t), :, cols] = mm_nt(qa[t], kn_scr[keys(b, n), :]) + bias_ref[variant[b], a, :, cols]
            else:
                s_scr[buf(t), :, cols] = mm_nt(qa[t], ckn_scr[...])

        def softmax_rows(t, r):
            b, a = tiles[t]
            rs = slice(r * SOFTMAX_ROWS, (r + 1) * SOFTMAX_ROWS)
            out_rows = slice(b * QBLK + rs.start, b * QBLK + rs.stop)
            s = s_scr[buf(t), rs, :]
            p = jnp.exp(s - jnp.max(s, axis=-1, keepdims=True)).astype(BF16)
            pl_ref[a, out_rows, :] = p[:, :KBLK]
            pc_ref[a, out_rows, :] = p[:, KBLK:]

        def value_piece(t, n):
            b, a = tiles[t]
            if n < latent:
                part = mm(pl_ref[a, rows[b], n * KCOLS:(n + 1) * KCOLS], v1_scr[a, keys(b, n), :])
            else:
                part = mm(pc_ref[a, rows[b], :], cv1_scr[a])
            pv[t] = part if pv[t] is None else pv[t] + part
            if n == latent:
                finish(t)

        def finish(t):
            b, a = tiles[t]
            r = jnp.where(heads[a], pltpu.roll(1.0 / pv[t], HDIM, 1), 0.0)
            done[t] = (pv[t] * r, r)
            if a == 1:
                o, rden = (lo + hi for lo, hi in zip(done[t - 1], done[t]))
                ob_ref[rows[b], :] = o * jax.nn.silu(bg_ref[rows[b], :])
                o_ref[0, rows[b], :] = o
                rden_ref[0, rows[b], :] = rden

        pieces = range(latent + 1)
        for n in pieces:
            score_piece(0, n)
        for t in range(len(tiles)):
            matmuls = []
            for n in pieces:
                if t + 1 < len(tiles):
                    matmuls.append(functools.partial(score_piece, t + 1, n))
                if t > 0:
                    matmuls.append(functools.partial(value_piece, t - 1, n))
            _emit_interleaved([functools.partial(softmax_rows, t, r) for r in range(QBLK // SOFTMAX_ROWS)], matmuls)
        for n in pieces:
            value_piece(len(tiles) - 1, n)

    qblk = pl.BlockSpec((ATTN_ROWS, 128), lambda p, i: (i, p))
    return pl.pallas_call(
        kern, name="attn_fwd", grid=(NPAIR, ATTN_STEPS),
        in_specs=_attn_in_specs() + [_rpb_spec(), _row(128), _row(128)],
        out_specs=[qblk, _pair_major_spec(), _pair_major_spec()] + _prob_specs() + _normed_key_specs(),
        out_shape=[jax.ShapeDtypeStruct((SEQ, 512), F32)] + [jax.ShapeDtypeStruct((NPAIR, SEQ, 128), F32)] * 2
        + [jax.ShapeDtypeStruct((HEADS, SEQ, KBLK), BF16), jax.ShapeDtypeStruct((HEADS, SEQ, CTX), BF16),
           jax.ShapeDtypeStruct((NPAIR, SEQ, 128), BF16), jax.ShapeDtypeStruct((NPAIR, CTX, 128), BF16)],
        scratch_shapes=[pltpu.VMEM((SEQ, 128), BF16), pltpu.VMEM((CTX, 128), BF16),
                        pltpu.VMEM((2, SEQ, 128), BF16), pltpu.VMEM((2, CTX, 128), BF16),
                        pltpu.VMEM((TILE_BUFFERS, QBLK, KBLK + CTX), F32),
                        pltpu.VMEM((3, 2, QBLK, KBLK), F32)],
        compiler_params=_cparams(("arbitrary", "arbitrary"), VMEM_BIG),
    )(z, z, z, z, zc, zc, rpb2, qg2, kg2)


def attn_bwd(z, zc, qg2, kg2, dcat, saved):
    def kern(q_ref, k_ref, v_ref, bg_ref, ck_ref, cv_ref, qg_ref, kg_ref, do_ref, o_ref, rden_ref, pl_ref, pc_ref,
             kn_scr, ckn_scr, dq_ref, dk_ref, dv_ref, dbg_ref, dck_ref, dcv_ref, drpb_ref, dqg_ref, dkg_ref,
             v_scr, cv_scr, dknt_scr, dvt_scr, dcknt_scr, dcvt_scr, dp_scr, ds_scr, db_ref):
        p, i = pl.program_id(0), pl.program_id(1)
        last = i == ATTN_STEPS - 1

        @pl.when(i == 0)
        def _():
            def body(c, carry):
                sl = pl.ds(pl.multiple_of(c * NORM_ROWS, NORM_ROWS), NORM_ROWS)
                v_scr[sl, :] = v_ref[sl, :].astype(BF16)
                return carry

            lax.fori_loop(0, SEQ // NORM_ROWS, body, 0)
            cv_scr[...] = cv_ref[...].astype(BF16)
            for acc in (dknt_scr, dvt_scr, dcknt_scr, dcvt_scr, db_ref):
                acc[...] = jnp.zeros_like(acc)

        @pl.when((i == 0) & (p == 0))
        def _():
            dqg_ref[...] = jnp.zeros_like(dqg_ref)
            dkg_ref[...] = jnp.zeros_like(dkg_ref)

        heads = _head_lanes()
        tiles = [(b, a) for b in range(ATTN_BLOCKS) for a in range(2)]
        rows = [slice(b * QBLK, (b + 1) * QBLK) for b in range(ATTN_BLOCKS)]
        kb = [_kblock(ATTN_BLOCKS * i + b) for b in range(ATTN_BLOCKS)]
        variant = [_bias_variant(i, b) for b in range(ATTN_BLOCKS)]
        latent = KBLK // KCOLS
        buf = lambda t: t % TILE_BUFFERS

        def keys(b, n):
            return pl.ds(pl.multiple_of((kb[b] + n) * KCOLS, KCOLS), KCOLS)

        gated = {}

        def gate_backward(b):
            bg, dout, o = bg_ref[rows[b], :], do_ref[rows[b], :], o_ref[0, rows[b], :]
            sig = jax.nn.sigmoid(bg)
            do = dout * (bg * sig)
            dbg_ref[rows[b], :] = (dout * o * (sig * (1.0 + bg * (1.0 - sig)))).astype(BF16)
            rden = rden_ref[0, rows[b], :]
            dr = do * rden
            qn = _scaled_q(q_ref[rows[b], :], qg_ref[...])
            gated[b] = (dr, dr.T.astype(BF16), qn.T.astype(BF16), do * o * rden)

        feats = [slice(a * HDIM, (a + 1) * HDIM) for a in range(2)]
        doa, doa_t, qa_t, delta = {}, {}, {}, {}
        dqn = [None] * len(tiles)

        def cols(n):
            return slice(n * KCOLS, (n + 1) * KCOLS)

        def stage_a(t, n):
            b, a = tiles[t]
            if n == 0:
                if a == 0:
                    gate_backward(b)
                dr, dr_t, qn_t, weighted = gated[b]
                doa[t] = jnp.where(heads[a], dr, 0.0).astype(BF16)
                doa_t[t] = dr_t[feats[a], :]
                qa_t[t] = qn_t[feats[a], :]
                delta[t] = jnp.sum(jnp.where(heads[a], weighted, 0.0), axis=-1, keepdims=True)
            if n < latent:
                dp_scr[buf(t), :, cols(n)] = mm_nt(doa[t], v_scr[keys(b, n), :])
                dvt_scr[kb[b] + n, feats[a], :] += mm(doa_t[t], pl_ref[a, rows[b], cols(n)])
            else:
                dp_scr[buf(t), :, cols(n)] = mm_nt(doa[t], cv_scr[...])
                dcvt_scr[feats[a], :] += mm(doa_t[t], pc_ref[a, rows[b], :])

        def stage_b(t, r):
            b, a = tiles[t]
            rs = slice(r * SOFTMAX_ROWS, (r + 1) * SOFTMAX_ROWS)
            in_rows = slice(b * QBLK + rs.start, b * QBLK + rs.stop)
            d = dp_scr[buf(t), rs, :] - delta[t][rs, :]
            ds_lat = pl_ref[a, in_rows, :].astype(F32) * d[:, :KBLK]
            ds_ctx = pc_ref[a, in_rows, :].astype(F32) * d[:, KBLK:]
            db_ref[variant[b], a, rs, :] += ds_lat
            ds_scr[buf(t), rs, :KBLK] = ds_lat.astype(BF16)
            ds_scr[buf(t), rs, KBLK:] = ds_ctx.astype(BF16)

        def stage_c(t, n):
            b, a = tiles[t]
            ds = ds_scr[buf(t), :, cols(n)]
            if n < latent:
                part = mm(ds, kn_scr[keys(b, n), :])
                dknt_scr[kb[b] + n, feats[a], :] += mm(qa_t[t], ds)
            else:
                part = mm(ds, ckn_scr[...])
                dcknt_scr[feats[a], :] += mm(qa_t[t], ds)
            dqn[t] = part if dqn[t] is None else dqn[t] + part
            if n == latent and a == 1:
                both = jnp.where(heads[0], dqn[t - 1], 0.0) + jnp.where(heads[1], dqn[t], 0.0)
                dq, dqg = jax.vjp(_scaled_q, q_ref[rows[b], :], qg_ref[...])[1](both)
                dq_ref[rows[b], :] = dq.astype(BF16)
                dqg_ref[...] += dqg

        pieces = range(latent + 1)
        for n in pieces:
            stage_a(0, n)
        for t in range(len(tiles)):
            matmuls = []
            for n in pieces:
                if t + 1 < len(tiles):
                    matmuls.append(functools.partial(stage_a, t + 1, n))
                if t > 0:
                    matmuls.append(functools.partial(stage_c, t - 1, n))
            _emit_interleaved([functools.partial(stage_b, t, r) for r in range(QBLK // SOFTMAX_ROWS)], matmuls)
        for n in pieces:
            stage_c(len(tiles) - 1, n)

        @pl.when(last)
        def _():
            eye = (lax.broadcasted_iota(jnp.int32, (KCOLS, KCOLS), 0)
                   == lax.broadcasted_iota(jnp.int32, (KCOLS, KCOLS), 1)).astype(BF16)

            def turned(x):
                hi = x.astype(BF16)
                return mm_nt(eye, hi) + mm_nt(eye, x - hi.astype(F32))

            def body(c, dkg):
                sl = pl.ds(pl.multiple_of(c * NORM_ROWS, NORM_ROWS), NORM_ROWS)
                blocks = range(NORM_ROWS // KCOLS)
                dkn = jnp.concatenate([turned(dknt_scr[c * len(blocks) + n]) for n in blocks], axis=0)
                dv = jnp.concatenate([mm_nt(eye, dvt_scr[c * len(blocks) + n]) for n in blocks], axis=0)
                dk, dg = _pair_rms_bwd(k_ref[sl, :], kg_ref[...], dkn)
                dk_ref[sl, :] = dk.astype(BF16)
                dv_ref[sl, :] = dv.astype(BF16)
                return dkg + dg

            dkg = lax.fori_loop(0, SEQ // NORM_ROWS, body, jnp.zeros((1, 128), F32))
            dck, dg = _pair_rms_bwd(ck_ref[...], kg_ref[...], dcknt_scr[...].T)
            dck_ref[...] = dck
            dcv_ref[...] = dcvt_scr[...].T
            dkg_ref[...] += dkg + dg
            for a in range(2):
                rows_of_rpb = _rpb_grad(lambda v, tile_rows, tile_cols, a=a: db_ref[v, a, tile_rows, tile_cols])
                for d, row in enumerate(rows_of_rpb):
                    drpb_ref[a, d:d + 1, :] = row

        @pl.when(last & (p == NPAIR - 1))
        def _():
            dqg_ref[...] = dqg_ref[...] + pltpu.roll(dqg_ref[...], HDIM, 1)
            dkg_ref[...] = dkg_ref[...] + pltpu.roll(dkg_ref[...], HDIM, 1)

    blk = lambda rows: pl.BlockSpec((rows, 128), lambda p, i: (0, p))
    qblk = pl.BlockSpec((ATTN_ROWS, 128), lambda p, i: (i, p))
    return pl.pallas_call(
        kern, name="attn_bwd", grid=(NPAIR, ATTN_STEPS),
        in_specs=_attn_in_specs() + [_row(128), _row(128), pl.BlockSpec((ATTN_ROWS, 128), lambda p, i: (i, 4 + p)),
                                     _pair_major_spec(), _pair_major_spec()] + _prob_specs() + _normed_key_specs(),
        out_specs=[qblk, blk(SEQ), blk(SEQ), qblk, blk(CTX), blk(CTX), _rpb_spec(), _row(128), _row(128)],
        out_shape=[jax.ShapeDtypeStruct((SEQ, 512), BF16)] * 4 + [jax.ShapeDtypeStruct((CTX, 512), F32)] * 2
        + [jax.ShapeDtypeStruct((HEADS, 15, 128), F32)]
        + [jax.ShapeDtypeStruct((1, 128), F32), jax.ShapeDtypeStruct((1, 128), F32)],
        scratch_shapes=[pltpu.VMEM((SEQ, 128), BF16), pltpu.VMEM((CTX, 128), BF16),
                        pltpu.VMEM((SEQ // KCOLS, 128, KCOLS), F32), pltpu.VMEM((SEQ // KCOLS, 128, KCOLS), F32),
                        pltpu.VMEM((128, CTX), F32), pltpu.VMEM((128, CTX), F32),
                        pltpu.VMEM((TILE_BUFFERS, QBLK, KBLK + CTX), F32),
                        pltpu.VMEM((TILE_BUFFERS, QBLK, KBLK + CTX), BF16),
                        pltpu.VMEM((3, 2, QBLK, KBLK), F32)],
        compiler_params=_cparams(("arbitrary", "arbitrary"), VMEM_BIG),
    )(z, z, z, z, zc, zc, qg2, kg2, dcat, *saved)


def outproj(out_a, out_b, x, target, gate, wo):
    tl = 512

    def kern(a_ref, b_ref, x_ref, t_ref, g_ref, w_ref, loss_ref, dy_ref, dcat_ref, dg_ref, dw_ref):
        @pl.when(pl.program_id(0) == 0)
        def _():
            loss_ref[...] = jnp.zeros_like(loss_ref)
            dg_ref[...] = jnp.zeros_like(dg_ref)
            dw_ref[...] = jnp.zeros_like(dw_ref)

        a, b = a_ref[...].astype(BF16), b_ref[...].astype(BF16)
        mix = (jnp.dot(a, w_ref[0:512, :], preferred_element_type=F32)
               + jnp.dot(b, w_ref[512:1024, :], preferred_element_type=F32))
        err = x_ref[...] + g_ref[...] * mix - t_ref[...]
        loss_ref[...] += 0.5 * jnp.sum(jnp.mean(err * err, axis=-1))
        dy = err * (1.0 / DM)
        dy_ref[...] = dy
        dg_ref[...] += jnp.sum(dy * mix, axis=0, keepdims=True)
        dmix = (g_ref[...] * dy).astype(BF16)
        dcat_ref[...] = lax.dot_general(dmix, w_ref[...], (((1,), (1,)), ((), ())), preferred_element_type=F32)
        dw_ref[0:512, :] += lax.dot_general(a, dmix, (((0,), (0,)), ((), ())), preferred_element_type=F32)
        dw_ref[512:1024, :] += lax.dot_general(b, dmix, (((0,), (0,)), ((), ())), preferred_element_type=F32)

    tile = lambda w: pl.BlockSpec((tl, w), lambda t: (t, 0))
    whole = pl.BlockSpec((DM, DM), lambda t: (0, 0))
    return pl.pallas_call(
        kern, name="outproj", grid=(SEQ // tl,),
        in_specs=[tile(512), tile(512), tile(DM), tile(DM), _row(DM), whole],
        out_specs=[pl.BlockSpec((8, 128), lambda t: (0, 0)), tile(DM), tile(DM), _row(DM), whole],
        out_shape=[jax.ShapeDtypeStruct((8, 128), F32), jax.ShapeDtypeStruct((SEQ, DM), F32),
                   jax.ShapeDtypeStruct((SEQ, DM), F32), jax.ShapeDtypeStruct((1, DM), F32),
                   jax.ShapeDtypeStruct((DM, DM), F32)],
        compiler_params=_cparams(("arbitrary",), 48 * 1024 * 1024),
    )(out_a, out_b, x, target, gate, wo)


DZ_COLS = (("a", 0, 1536), ("q", 1536, 2048), ("k", 2048, 2560), ("v", 2560, 3072), ("g", 3072, DIN))
DZC_COLS = (("k", 2048, 2560), ("v", 2560, 3072))
_NT = (((1,), (1,)), ((), ()))


DH_SUBTILES = 2


def _dz_specs(tl):
    return [pl.BlockSpec((tl, 1536), lambda t: (t, 0))] + [pl.BlockSpec((tl, 512), lambda t: (t, 0))] * 4


def dh_bwd(dz_parts, w_full, x, dy, shift, scale, norm_g, dg_ctx):
    tl = 512
    nt = SEQ // tl

    def kern(a_ref, q_ref, k_ref, v_ref, g_ref, w_ref, x_ref, dy_ref, sh_ref, sc_ref, gn_ref, dgc_ref,
             gx_ref, dsh_ref, dsc_ref, dg_ref):
        @pl.when(pl.program_id(0) == 0)
        def _():
            dsh_ref[...] = jnp.zeros_like(dsh_ref)
            dsc_ref[...] = jnp.zeros_like(dsc_ref)
            dg_ref[...] = dgc_ref[...]

        src = dict(a=a_ref, q=q_ref, k=k_ref, v=v_ref, g=g_ref)
        for sub in range(DH_SUBTILES):
            rows = slice(sub * tl // DH_SUBTILES, (sub + 1) * tl // DH_SUBTILES)
            dh = None
            for name, c0, c1 in DZ_COLS:
                part = lax.dot_general(src[name][rows, :], w_ref[:, c0:c1], _NT, preferred_element_type=F32)
                dh = part if dh is None else dh + part
            _, vjp = jax.vjp(_modulated, x_ref[rows, :], gn_ref[...], sc_ref[...], sh_ref[...])
            dx, dg, dsc, dsh = vjp(dh)
            gx_ref[rows, :] = dy_ref[rows, :] + dx
            dg_ref[...] += dg
            dsc_ref[...] += dsc
            dsh_ref[...] += dsh

    tile = pl.BlockSpec((tl, DM), lambda t: (t, 0))
    return pl.pallas_call(
        kern, name="dh_bwd", grid=(nt,),
        in_specs=_dz_specs(tl) + [pl.BlockSpec((DM, DIN), lambda t: (0, 0)), tile, tile, _row(DM),
                                  _row(DM), _row(DM), _row(DM)],
        out_specs=[tile, _row(DM), _row(DM), _row(DM)],
        out_shape=[jax.ShapeDtypeStruct((SEQ, DM), F32)] + [jax.ShapeDtypeStruct((1, DM), F32)] * 3,
        compiler_params=_cparams(("arbitrary",), 48 * 1024 * 1024),
    )(*dz_parts, w_full, x, dy, shift, scale, norm_g, dg_ctx)


def dw_bwd(h, z, sg, ws, bsb, dcat, dz_attn, hc, dck, dcv, g_out):
    tl = SGU_CHUNK * SGU_PER_STEP
    nt = SEQ // tl
    (rhi, wi), (rho, wo) = RS_SHAPES

    def kern(h_ref, au_ref, av_ref, ag_ref, sg_ref, ws_ref, bs_ref, do_ref, q_ref, k_ref, v_ref, g_ref,
             hc_ref, dck_ref, dcv_ref, go_hbm,
             wire_i, keep_i, wire_o, keep_o, a_ref, dsg_ref, dws_ref, dbs_ref,
             acc, rcv_i, mine_o, rcv_o, load_sem, send_sems, recv_sems):
        t = pl.program_id(0)
        x, y, c = _me()
        k = 2 * x + y
        sib = _flip(1)
        half = lambda hh, rh: pl.ds(pl.multiple_of(hh * rh, rh), rh)
        load_o = pltpu.make_async_copy(go_hbm.at[:, half(c, rho), :], mine_o, load_sem)
        pair_o = _rcopy(go_hbm.at[:, half(1 - c, rho), :], rcv_o, send_sems, recv_sems, 0, sib)
        pair_i = [_rcopy(wire_i.at[j], rcv_i.at[j], send_sems, recv_sems, 1 + j, sib) for j in range(NCHIP)]

        @pl.when(t == 0)
        def _():
            load_o.start()
            pair_o.start()
            acc[...] = jnp.zeros_like(acc)
            dsg_ref[...] = jnp.zeros_like(dsg_ref)
            dws_ref[...] = jnp.zeros_like(dws_ref)
            dbs_ref[...] = jnp.zeros_like(dbs_ref)
            hct = hc_ref[...].T
            csrc = dict(k=dck_ref, v=dcv_ref)
            for name, c0, c1 in DZC_COLS:
                acc[:, c0:c1] += jnp.dot(hct, csrc[name][...].astype(BF16), preferred_element_type=F32)

        ht = h_ref[...].T
        src = dict(a=a_ref, q=q_ref, k=k_ref, v=v_ref, g=g_ref)

        def gating_backward(cn):
            sl = slice(cn * SGU_CHUNK, (cn + 1) * SGU_CHUNK)
            _, vjp = jax.vjp(_sgu_chunk, au_ref[sl, :], av_ref[sl, :], ag_ref[sl, :], sg_ref[...], ws_ref[...],
                             bs_ref[...])
            dau, dav, dag, dsg, dws, dbs = vjp(do_ref[sl, :])
            a_ref[sl, 0:512] = dau.astype(BF16)
            a_ref[sl, 512:1024] = dav.astype(BF16)
            a_ref[sl, 1024:1536] = dag.astype(BF16)
            dsg_ref[...] += dsg
            dws_ref[...] += dws
            dbs_ref[...] += dbs

        def product(name, c0, c1):
            acc[:, c0:c1] += jnp.dot(ht, src[name][...], preferred_element_type=F32)

        _emit_interleaved([functools.partial(gating_backward, cn) for cn in range(SGU_PER_STEP)],
                          [functools.partial(product, *cols) for cols in DZ_COLS[1:]])
        product(*DZ_COLS[0])

        @pl.when(t == nt - 1)
        def _():
            dbs_ref[...] = jnp.broadcast_to(jnp.sum(dbs_ref[...], axis=-1, keepdims=True), dbs_ref.shape)
            shard = lambda j: slice(j * SHARD_IN, (j + 1) * SHARD_IN)
            for j in range(NCHIP):
                wire_i[j] = acc[half(1 - c, rhi), shard(j)].astype(BF16)
                pair_i[j].start()
            load_o.wait()
            pair_o.wait_recv()
            for j in range(NCHIP):
                wire_o[j] = (mine_o[j] + rcv_o[j]).astype(BF16)
            keep_o[...] = mine_o[k] + rcv_o[k]
            mine = half(c, rhi)
            for j in range(NCHIP):
                pair_i[j].wait_recv()
                pair_i[j].wait_send()
                pair_sum = acc[mine, shard(j)] + rcv_i[j].astype(F32)
                wire_i[j] = pair_sum.astype(BF16)

                @pl.when(k == j)
                def _():
                    keep_i[...] = pair_sum
            pair_o.wait_send()

    whole = lambda *shape: pl.BlockSpec(shape, lambda t: (0,) * len(shape))
    rows, sgu_specs = _sgu_specs()
    assert rows == tl
    a_spec, *attn_specs = _dz_specs(tl)
    return pl.pallas_call(
        kern, name="dw_bwd", grid=(nt,),
        in_specs=[pl.BlockSpec((tl, DM), lambda t: (t, 0))] + sgu_specs + [pl.BlockSpec((tl, 512), lambda t: (t, 0))]
        + attn_specs + [whole(CTX, DM), whole(CTX, 512), whole(CTX, 512), pl.BlockSpec(memory_space=pl.ANY)],
        out_specs=[whole(NCHIP, rhi, wi), whole(rhi, wi), whole(NCHIP, rho, wo), whole(rho, wo),
                   a_spec, _row(512), whole(4, 128, 128), whole(4, 128, 128)],
        out_shape=[jax.ShapeDtypeStruct((NCHIP, rhi, wi), BF16), jax.ShapeDtypeStruct((rhi, wi), F32),
                   jax.ShapeDtypeStruct((NCHIP, rho, wo), BF16), jax.ShapeDtypeStruct((rho, wo), F32),
                   jax.ShapeDtypeStruct((SEQ, 1536), BF16), jax.ShapeDtypeStruct((1, 512), F32),
                   jax.ShapeDtypeStruct((4, 128, 128), F32), jax.ShapeDtypeStruct((4, 128, 128), F32)],
        scratch_shapes=[pltpu.VMEM((DM, DIN), F32), pltpu.VMEM((NCHIP, rhi, wi), BF16),
                        pltpu.VMEM((NCHIP, rho, wo), F32), pltpu.VMEM((NCHIP, rho, wo), F32),
                        pltpu.SemaphoreType.DMA(()), pltpu.SemaphoreType.DMA((1 + NCHIP,)),
                        pltpu.SemaphoreType.DMA((1 + NCHIP,))],
        compiler_params=_cparams(("arbitrary",), 60 * 1024 * 1024),
    )(h, z, z, z, sg, ws, bsb, dcat, *dz_attn, hc, dck, dcv, g_out)


def ctx_bwd(dck, dcv, w_full, ctx, cshift, cscale, norm_g):
    def kern(dck_ref, dcv_ref, w_ref, c_ref, sh_ref, sc_ref, g_ref, dsh_ref, dsc_ref, dg_ref):
        csrc = dict(k=dck_ref, v=dcv_ref)
        dhc = None
        first = DZC_COLS[0][1]
        for name, c0, c1 in DZC_COLS:
            part = lax.dot_general(csrc[name][...].astype(BF16), w_ref[:, c0 - first:c1 - first], _NT,
                                   preferred_element_type=F32)
            dhc = part if dhc is None else dhc + part
        _, vjp = jax.vjp(lambda g, sc, sh: _modulated(c_ref[...], g, sc, sh), g_ref[...], sc_ref[...], sh_ref[...])
        dg_ref[...], dsc_ref[...], dsh_ref[...] = vjp(dhc)

    whole = lambda r, c: pl.BlockSpec((r, c), lambda i: (0, 0))
    return pl.pallas_call(
        kern, name="ctx_bwd", grid=(1,),
        in_specs=[whole(CTX, 512), whole(CTX, 512), pl.BlockSpec((DM, 1024), lambda i: (0, DZC_COLS[0][1] // 1024)),
                  whole(CTX, DM), _row(DM), _row(DM), _row(DM)],
        out_specs=[_row(DM), _row(DM), _row(DM)],
        out_shape=[jax.ShapeDtypeStruct((1, DM), F32)] * 3,
        compiler_params=_cparams(("arbitrary",), 40 * 1024 * 1024),
    )(dck, dcv, w_full, ctx, cshift, cscale, norm_g)


def _lane_pad_rpb(rpb):
    r = jnp.pad(rpb, ((0, 0), (0, 0), (0, GRID_W - rpb.shape[-1])))
    return jnp.concatenate([r, r], axis=-1)


def local_step(chip, dev, x, c_vec, c_ctx, w_ada, b_shard, ctx, target, norm_g, sgu_g, w_s, b_s, q_g, k_g, rpb,
               w_in_shard, w_out_shard):
    bsb = jnp.broadcast_to(b_s[:, :, None], (4, 128, 128))
    qg2, kg2 = jnp.tile(q_g, (1, 2)), jnp.tile(k_g, (1, 2))

    z, h, w_in_full, w_out_full, mod_all, cs = inproj_fwd(chip, x, c_vec, c_ctx, w_ada, b_shard, norm_g, w_in_shard,
                                                          w_out_shard)
    mods = mod_all.transpose(1, 0, 2).reshape(CS_ROWS, 3 * DM)
    mod = lax.dynamic_slice(mods, (8 * dev, 0), (1, 3 * DM))
    shift, scale, gate = mod[:, :DM], mod[:, DM:2 * DM], mod[:, 2 * DM:]
    cshift, cscale = mods[8 * NDEV:8 * NDEV + 1, :DM], mods[8 * NDEV:8 * NDEV + 1, DM:2 * DM]
    zc, hc = ctx_fwd(ctx, cshift, cscale, norm_g, w_in_full)
    out_a = sgu_fwd(z, sgu_g, w_s, bsb)
    out_b, *saved = attn_fwd(z, zc, _lane_pad_rpb(rpb), qg2, kg2)
    loss8, dy, dcat, dgate, dwo = outproj(out_a, out_b, x, target, gate, w_out_full.reshape(DM, DM))
    dq, dk, dv, dbg, dck, dcv, drpb, dqg2, dkg2 = attn_bwd(z, zc, qg2, kg2, dcat, saved)
    drpb = drpb[:, :, :rpb.shape[-1]]
    dcshift, dcscale, dng_c = ctx_bwd(dck, dcv, w_in_full, ctx, cshift, cscale, norm_g)
    wire_i, keep_i, wire_o, keep_o, dz_a, dsg, dws, dbsb = dw_bwd(
        h, z, sgu_g, w_s, bsb, dcat, (dq, dk, dv, dbg), hc, dck, dcv, dwo.reshape(NCHIP, SHARD_OUT, DM))
    dz_parts = (dz_a, dq, dk, dv, dbg)
    *in_flight, token = rs_start(wire_i, wire_o)
    grad_x, dshift, dscale, dng = dh_bwd(dz_parts, w_in_full, x, dy, shift, scale, norm_g, dng_c + token[0, 0])
    got_i, got_o = rs_wait(*in_flight, dshift)
    return dict(
        loss=loss8[0:1, 0:1], grad_x=grad_x, rs=(keep_i, got_i, keep_o, got_o), cs=cs,
        dmod=jnp.concatenate([dshift, dscale, dgate], axis=-1),
        dcmod=jnp.concatenate([dcshift, dcscale, jnp.zeros((1, DM), F32)], axis=-1),
        d_norm_g=dng, d_sgu_g=dsg, d_w_s=dws, d_b_s=dbsb[:, :, 0],
        d_q_g=dqg2[:, :HDIM], d_k_g=dkg2[:, :HDIM], d_rpb=drpb)


def _me():
    return lax.axis_index("x"), lax.axis_index("y"), lax.axis_index("c")


def _flip(q):
    x, y, c = _me()
    return ((1 - x) if q & 4 else x, (1 - y) if q & 2 else y, (1 - c) if q & 1 else c)


def _chip_of(dev):
    return 2 * dev[0] + dev[1]


def _rcopy(src, dst, send_sems, recv_sems, k, dev):
    return pltpu.make_async_remote_copy(src_ref=src, dst_ref=dst, send_sem=send_sems.at[k], recv_sem=recv_sems.at[k],
                                        device_id=dev, device_id_type=MESH_ID)


_VMEM_SPEC = pl.BlockSpec(memory_space=pltpu.VMEM)
SLAB_ROWS = 80


RS_SHAPES = ((DM // 2, SHARD_IN), (SHARD_OUT // 2, DM))
_HBM_SPEC = pl.BlockSpec(memory_space=pltpu.HBM)
_SEM_SPEC = pl.BlockSpec(memory_space=pltpu.SEMAPHORE)
_IN_FLIGHT = pltpu.SideEffectType.DATAFLOW_SIDE_EFFECTING


def _rs_copies(wires, lands, send_sems, recv_sems):
    return [pltpu.make_async_remote_copy(
        src_ref=wires[n].at[_chip_of(_flip(q))], dst_ref=lands[n].at[q // 2 - 1],
        send_sem=send_sems.at[3 * n + q // 2 - 1], recv_sem=recv_sems.at[3 * n + q // 2 - 1],
        device_id=_flip(q), device_id_type=MESH_ID) for n in (0, 1) for q in (2, 4, 6)]


def rs_start(wire_i, wire_o):
    lands = [lax.empty((NCHIP - 1, rh, w), BF16) for rh, w in RS_SHAPES]

    def body(wi_ref, wo_ref, li_ref, lo_ref, send_sems, recv_sems, wi_thru, wo_thru, li_thru, lo_thru, token):
        for cp in _rs_copies((wi_ref, wo_ref), (li_ref, lo_ref), send_sems, recv_sems):
            cp.start()
        token[...] = jnp.zeros_like(token)

    hbm = lambda a: pltpu.HBM(a.shape, a.dtype)
    return pl.pallas_call(
        body, name="rs_start",
        out_shape=(pltpu.SemaphoreType.DMA((6,)), pltpu.SemaphoreType.DMA((6,)), hbm(wire_i), hbm(wire_o),
                   hbm(lands[0]), hbm(lands[1]), jax.ShapeDtypeStruct((8, 128), F32)),
        in_specs=(_HBM_SPEC,) * 4, out_specs=(_SEM_SPEC, _SEM_SPEC) + (_HBM_SPEC,) * 4 + (_VMEM_SPEC,),
        input_output_aliases={0: 2, 1: 3, 2: 4, 3: 5},
        compiler_params=pltpu.CompilerParams(has_side_effects=_IN_FLIGHT),
    )(*[pltpu.with_memory_space_constraint(a, pltpu.HBM) for a in (wire_i, wire_o, *lands)])


def rs_wait(send_sems, recv_sems, wire_i, wire_o, land_i, land_o, after):
    def body(wi_ref, wo_ref, li_ref, lo_ref, send_sems, recv_sems, after_ref, wi_dead, wo_dead, gi_ref, go_ref):
        for cp in _rs_copies((wi_ref, wo_ref), (li_ref, lo_ref), send_sems, recv_sems):
            cp.wait_send()
            cp.wait_recv()

    hbm = lambda a: pltpu.HBM(a.shape, a.dtype)
    return pl.pallas_call(
        body, name="rs_wait", out_shape=(hbm(wire_i), hbm(wire_o), hbm(land_i), hbm(land_o)),
        in_specs=(_HBM_SPEC,) * 4 + (_SEM_SPEC, _SEM_SPEC, pl.BlockSpec(memory_space=pl.ANY)),
        out_specs=(_HBM_SPEC,) * 4, input_output_aliases={0: 0, 1: 1, 2: 2, 3: 3},
        compiler_params=pltpu.CompilerParams(has_side_effects=_IN_FLIGHT),
    )(wire_i, wire_o, land_i, land_o, send_sems, recv_sems, after)[2:]


def final_reduce(keep_i, got_i, keep_o, got_o, slab, cs, w_ada, c_ctx):
    (rhi, wi), (rho, wo) = RS_SHAPES

    def kern(ki_hbm, gi_hbm, ko_hbm, go_hbm, s_ref, cs_ref, w_hbm, cc_ref,
             gin_ref, gout_ref, tot_ref, dw_ref, db_ref, dcc_ref,
             ki, gi, ko, go, w_scr, all_ref, dms_scr, parts, load_sems, send_sems, recv_sems):
        x, y, c = _me()
        k = 2 * x + y
        sib = _flip(1)
        dev = lambda d: 4 * d[0] + 2 * d[1] + d[2]
        me = dev((x, y, c))

        def slab_copy(idx, owner, to):
            return _rcopy(all_ref.at[dev(owner)], all_ref.at[dev(owner)], send_sems, recv_sems, idx, to)

        all_ref[me] = s_ref[...]
        first = [slab_copy(0, (x, y, c), sib)] + [slab_copy(q // 2, (x, y, c), _flip(q)) for q in (2, 4, 6)]
        for cp in first:
            cp.start()
        loads = [pltpu.make_async_copy(src, dst, load_sems.at[n]) for n, (src, dst) in enumerate(
            ((ki_hbm, ki), (gi_hbm, gi), (ko_hbm, ko), (go_hbm, go), (w_hbm, w_scr)))]
        for cp in loads:
            cp.start()

        shares = []
        for n, (keep, got, out) in enumerate(((ki, gi, gin_ref), (ko, go, gout_ref))):
            rh = RS_SHAPES[n][0]
            half = lambda hh, rh=rh: pl.ds(pl.multiple_of(hh * rh, rh), rh)
            loads[2 * n].wait()
            loads[2 * n + 1].wait()
            out[half(c), :] = ((keep[...] + got[0].astype(F32)) + got[1].astype(F32)) + got[2].astype(F32)
            share = _rcopy(out.at[half(c), :], out.at[half(c), :], send_sems, recv_sems, 7 + n, sib)
            share.start()
            shares.append((share, _rcopy(out.at[half(1 - c), :], out.at[half(1 - c), :], send_sems, recv_sems, 7 + n,
                                         sib)))

        passed = []
        for q in (2, 4, 6):
            slab_copy(q // 2, _flip(q), (x, y, c)).wait_recv()
            cp = slab_copy(3 + q // 2, _flip(q), sib)
            cp.start()
            passed.append(cp)
        slab_copy(0, sib, (x, y, c)).wait_recv()
        for q in (2, 4, 6):
            slab_copy(3 + q // 2, _flip(q | 1), (x, y, c)).wait_recv()
        tot = all_ref[0]
        for d in range(1, NDEV):
            tot = tot + all_ref[d]
        tot_ref[...] = tot

        pad = jnp.zeros((7, DM), F32)
        dm = [jnp.concatenate([all_ref[d, 12 + j:13 + j, :] for d in range(NDEV)] + [tot[9 + j:10 + j, :], pad], axis=0)
              for j in range(3)]
        db_ref[...] = jnp.concatenate([jnp.sum(part, axis=0, keepdims=True) for part in dm], axis=0)
        dm = jnp.concatenate(dm, axis=-1)
        for j in range(NCHIP):
            @pl.when(k == j)
            def _():
                dms_scr[...] = dm[:, j * SHARD_ADA:(j + 1) * SHARD_ADA].astype(BF16)

        a_in = jnp.concatenate([cs_ref[8 * d:8 * d + 1, :] for d in range(NDEV)]
                               + [cs_ref[8 * NDEV:8 * NDEV + 1, :], pad], axis=0)
        act = jax.nn.silu(a_in).astype(BF16)
        dms = dms_scr[...]
        dw_ref[...] = lax.dot_general(act, dms, (((0,), (0,)), ((), ())), preferred_element_type=F32)
        loads[4].wait()
        parts[k] = lax.dot_general(dms, w_scr[...].astype(BF16), (((1,), (1,)), ((), ())), preferred_element_type=F32)
        sends = [_rcopy(parts.at[k], parts.at[k], send_sems, recv_sems, 8 + q // 2, _flip(q)) for q in (2, 4, 6)]
        for cp in sends:
            cp.start()
        for q in (2, 4, 6):
            kq = _chip_of(_flip(q))
            _rcopy(parts.at[kq], parts.at[kq], send_sems, recv_sems, 8 + q // 2, _flip(q)).wait_recv()
        dact = ((parts[0] + parts[1]) + parts[2]) + parts[3]
        _, vjp = jax.vjp(jax.nn.silu, cc_ref[...])
        dcc_ref[...] = vjp(dact[8:9, :])[0]

        for share, arrival in shares:
            arrival.wait_recv()
            share.wait_send()
        for cp in first + passed + sends:
            cp.wait_send()

    any_spec = pl.BlockSpec(memory_space=pl.ANY)
    return pl.pallas_call(
        kern, name="final_reduce",
        in_specs=[any_spec] * 4 + [_VMEM_SPEC, _VMEM_SPEC, any_spec, _VMEM_SPEC], out_specs=[_VMEM_SPEC] * 6,
        out_shape=[jax.ShapeDtypeStruct((2 * rhi, wi), F32), jax.ShapeDtypeStruct((2 * rho, wo), F32),
                   jax.ShapeDtypeStruct((SLAB_ROWS, DM), F32), jax.ShapeDtypeStruct((DM, SHARD_ADA), F32),
                   jax.ShapeDtypeStruct((3, DM), F32), jax.ShapeDtypeStruct((1, DM), F32)],
        scratch_shapes=[pltpu.VMEM((rhi, wi), F32), pltpu.VMEM((NCHIP - 1, rhi, wi), BF16),
                        pltpu.VMEM((rho, wo), F32), pltpu.VMEM((NCHIP - 1, rho, wo), BF16),
                        pltpu.VMEM((DM, SHARD_ADA), F32), pltpu.VMEM((NDEV, SLAB_ROWS, DM), F32),
                        pltpu.VMEM((16, SHARD_ADA), BF16), pltpu.VMEM((NCHIP, 16, DM), F32),
                        pltpu.SemaphoreType.DMA((5,)), pltpu.SemaphoreType.DMA((12,)), pltpu.SemaphoreType.DMA((12,))],
        compiler_params=pltpu.CompilerParams(vmem_limit_bytes=40 * 1024 * 1024),
    )(keep_i, got_i, keep_o, got_o, slab, cs, w_ada, c_ctx)


def _adamw_math(w, g, m, v):
    m = B1 * m + (1.0 - B1) * g
    v = B2 * v + (1.0 - B2) * (g * g)
    m_hat = m / (1.0 - B1 ** STEP)
    v_hat = v / (1.0 - B2 ** STEP)
    return -LR * (m_hat / (jnp.sqrt(v_hat) + ADAM_EPS) + WD * w), m, v


def adamw_big(w, g, m, v, name, block_rows=256):
    rows, width = w.shape

    def kern(w_ref, g_ref, m_ref, v_ref, d_ref, nm_ref, nv_ref):
        d_ref[...], nm_ref[...], nv_ref[...] = _adamw_math(w_ref[...], g_ref[...], m_ref[...], v_ref[...])

    spec = pl.BlockSpec((block_rows, width), lambda i: (i, 0))
    return pl.pallas_call(
        kern, name=name, grid=(rows // block_rows,), in_specs=[spec] * 4, out_specs=[spec] * 3,
        out_shape=[jax.ShapeDtypeStruct((rows, width), F32)] * 3,
        compiler_params=_cparams(("arbitrary",)),
    )(w, g, m, v)


def adamw_small(quads):
    n = len(quads)

    def kern(*refs):
        ins, outs = refs[:4 * n], refs[4 * n:]
        for i in range(n):
            w, g, m, v = (r[...] for r in ins[4 * i:4 * i + 4])
            outs[3 * i][...], outs[3 * i + 1][...], outs[3 * i + 2][...] = _adamw_math(w, g, m, v)

    flat = [a for quad in quads for a in quad]
    res = pl.pallas_call(
        kern, name="adamw_small", in_specs=[_VMEM_SPEC] * (4 * n), out_specs=[_VMEM_SPEC] * (3 * n),
        out_shape=[jax.ShapeDtypeStruct(q[0].shape, F32) for q in quads for _ in range(3)],
    )(*flat)
    return [tuple(res[3 * i:3 * i + 3]) for i in range(n)]


def _rows_of(a, rows):
    flat = a.reshape(-1)
    return jnp.pad(flat, (0, rows * DM - flat.shape[0])).reshape(rows, DM)


def kernel(x, c, ctx, c_ctx, w_ada, b_ada, norm_g, w_in, sgu_norm_g, w_spatial, b_spatial, q_norm_g, k_norm_g, rpb, w_out, loss_target, m_c_ctx, m_w_ada, m_b_ada, m_norm_g, m_w_in, m_sgu_norm_g, m_w_spatial, m_b_spatial, m_q_norm_g, m_k_norm_g, m_rpb, m_w_out, v_c_ctx, v_w_ada, v_b_ada, v_norm_g, v_w_in, v_sgu_norm_g, v_w_spatial, v_b_spatial, v_q_norm_g, v_k_norm_g, v_rpb, v_w_out):
    xi, yi, ci = lax.axis_index("x"), lax.axis_index("y"), lax.axis_index("c")
    chip, dev = 2 * xi + yi, 4 * xi + 2 * yi + ci
    c_ctx2 = c_ctx.reshape(1, DM)

    b_shard = lax.dynamic_slice(b_ada, (0, chip * SHARD_ADA), (1, SHARD_ADA))
    part = local_step(chip.reshape(1).astype(jnp.int32), dev, x[0], c, c_ctx2, w_ada[0], b_shard, ctx[0], loss_target[0],
                      norm_g, sgu_norm_g, w_spatial[0], b_spatial[0], q_norm_g, k_norm_g, rpb[0], w_in[0], w_out[0])
    cs = part["cs"]

    slab = jnp.concatenate([
        part["d_norm_g"], _rows_of(part["d_sgu_g"], 1), _rows_of(part["d_b_s"], 1),
        _rows_of(jnp.concatenate([part["d_q_g"], part["d_k_g"]], axis=-1), 1), _rows_of(part["d_rpb"], 4),
        _rows_of(part["loss"], 1), _rows_of(part["dcmod"], 3), _rows_of(part["dmod"], 3), jnp.zeros((1, DM), F32),
        _rows_of(part["d_w_s"], 64)], axis=0)
    g_w_in, g_w_out, tot, g_w_ada, g_b_ada, g_c_ctx = final_reduce(*part["rs"], slab, cs, w_ada[0], c_ctx2)
    g_b_ada = g_b_ada.reshape(1, 3 * DM)

    loss = tot[8, 0]
    g_small = dict(
        c_ctx=g_c_ctx, b_ada=g_b_ada, norm_g=tot[0:1], sgu_norm_g=tot[1:2, :512], w_spatial=tot[16:80].reshape(512, 128),
        b_spatial=tot[2:3, :512].reshape(4, 128), q_norm_g=tot[3:4, :HDIM], k_norm_g=tot[3:4, HDIM:2 * HDIM],
        rpb=tot[4:8].reshape(-1)[:HEADS * 15 * 31].reshape(HEADS * 15, 31))
    shapes = dict(c_ctx=(DM,), w_ada=(1, DM, SHARD_ADA), b_ada=(1, 3 * DM), norm_g=(1, DM), w_in=(1, DM, SHARD_IN),
                  sgu_norm_g=(1, 512), w_spatial=(1, 4, 128, 128), b_spatial=(1, 4, 128), q_norm_g=(1, HDIM),
                  k_norm_g=(1, HDIM), rpb=(1, HEADS, 15, 31), w_out=(1, SHARD_OUT, DM))
    names = list(shapes)
    weights = dict(c_ctx=c_ctx, w_ada=w_ada, b_ada=b_ada, norm_g=norm_g, w_in=w_in, sgu_norm_g=sgu_norm_g,
                   w_spatial=w_spatial, b_spatial=b_spatial, q_norm_g=q_norm_g, k_norm_g=k_norm_g, rpb=rpb, w_out=w_out)
    m_in = dict(zip(names, (m_c_ctx, m_w_ada, m_b_ada, m_norm_g, m_w_in, m_sgu_norm_g, m_w_spatial, m_b_spatial,
                            m_q_norm_g, m_k_norm_g, m_rpb, m_w_out)))
    v_in = dict(zip(names, (v_c_ctx, v_w_ada, v_b_ada, v_norm_g, v_w_in, v_sgu_norm_g, v_w_spatial, v_b_spatial,
                            v_q_norm_g, v_k_norm_g, v_rpb, v_w_out)))
    grads = dict(g_small, w_ada=g_w_ada, w_in=g_w_in, w_out=g_w_out)
    upd = {}
    for n in ("w_ada", "w_in", "w_out"):
        g = grads[n]
        upd[n] = adamw_big(weights[n].reshape(g.shape), g, m_in[n].reshape(g.shape), v_in[n].reshape(g.shape),
                           "adamw_" + n)
    small = [n for n in names if n not in upd]
    res = adamw_small([(weights[n].reshape(grads[n].shape), grads[n], m_in[n].reshape(grads[n].shape),
                        v_in[n].reshape(grads[n].shape)) for n in small])
    upd.update(zip(small, res))
    out = [loss, part["grad_x"].reshape(1, SEQ, DM)]
    out += [grads[n].reshape(shapes[n]) for n in names]
    for slot in range(3):
        out += [upd[n][slot].reshape(shapes[n]) for n in names]
    return tuple(out)
```

```python
import functools

import jax
import jax.numpy as jnp
from jax import lax
from jax.experimental import pallas as pl
from jax.experimental.pallas import tpu as pltpu

F32, BF16 = jnp.float32, jnp.bfloat16
SEQ, DM, CTX, DIN = 4096, 1024, 256, 3584
NCHIP, NDEV = 4, 8
SHARD_IN = DIN // NCHIP
SHARD_ADA = 3 * DM // NCHIP
SHARD_OUT = DM // NCHIP
GRID_W = 64
QROWS = 4
KROWS = 12
QBLK, KBLK = QROWS * GRID_W, KROWS * GRID_W
NQBLK = SEQ // QBLK
HEADS, HDIM, NPAIR = 8, 64, 4
EPS = 1e-6
NEG_INF = -1e30
ZQ, ZK, ZV, ZG = 12, 16, 20, 24
LR, B1, B2, ADAM_EPS, WD, STEP = 0.001, 0.9, 0.999, 1e-08, 0.01, 10
VMEM_BIG = 56 * 1024 * 1024
MESH_ID = pl.DeviceIdType.MESH


def _dot(a, b, lhs_c, rhs_c):
    return lax.dot_general(a.astype(BF16), b.astype(BF16), (((lhs_c,), (rhs_c,)), ((), ())),
                           preferred_element_type=F32)


@jax.custom_vjp
def mm(a, b):
    return _dot(a, b, 1, 0)


@jax.custom_vjp
def mm_nt(a, b):
    return _dot(a, b, 1, 1)


@jax.custom_vjp
def mm_tn(a, b):
    return _dot(a, b, 0, 0)


mm.defvjp(lambda a, b: (mm(a, b), (a, b)), lambda r, ct: (mm_nt(ct, r[1]), mm_tn(r[0], ct)))
mm_nt.defvjp(lambda a, b: (mm_nt(a, b), (a, b)), lambda r, ct: (mm(ct, r[1]), mm_tn(ct, r[0])))
mm_tn.defvjp(lambda a, b: (mm_tn(a, b), (a, b)), lambda r, ct: (mm_nt(r[1], ct), mm(r[0], ct)))


def _rms(x, g):
    return x * lax.rsqrt(jnp.mean(x * x, axis=-1, keepdims=True) + EPS) * g


def _modulated(x, g, scale, shift):
    return _rms(x, g) * (1.0 + scale) + shift


def _pair_rms(x, g2):
    lo = lax.broadcasted_iota(jnp.int32, (1, 2 * HDIM), 1) < HDIM
    sq = x * x
    s_lo = jnp.sum(jnp.where(lo, sq, 0.0), axis=-1, keepdims=True)
    s_hi = jnp.sum(jnp.where(lo, 0.0, sq), axis=-1, keepdims=True)
    rs = jnp.where(lo, lax.rsqrt(s_lo / HDIM + EPS), lax.rsqrt(s_hi / HDIM + EPS))
    return x * rs * g2


def _cparams(sem, vmem=None):
    return pltpu.CompilerParams(dimension_semantics=sem, vmem_limit_bytes=vmem)


def _row(n):
    return pl.BlockSpec((1, n), lambda *_: (0, 0))


CS_ROWS = 8 * NDEV + 8


def _mod_part(mod_ref, row, part):
    pieces = []
    for j in range(NCHIP):
        lo, hi = max(part * DM, j * SHARD_ADA), min((part + 1) * DM, (j + 1) * SHARD_ADA)
        if lo < hi:
            pieces.append(mod_ref[j, row, lo - j * SHARD_ADA:hi - j * SHARD_ADA])
    return jnp.concatenate(pieces, axis=-1)


def inproj_fwd(chip, x, c_vec, c_ctx, w_ada, b_shard, norm_g, w_shard, wo_shard):
    tl = 1024
    nt = SEQ // tl
    halves = (DM // 2, SHARD_OUT // 2)
    n_w, n_c = 12, NDEV - 1

    def kern(k_ref, x_ref, cv_ref, cc_ref, wa_ref, b_ref, g_ref, w_ref, wo_ref,
             z_ref, h_ref, wfull_ref, wofull_ref, modall_ref, csall_ref,
             w_scr, wo_scr, h_scr, mine, cs_scr, mod_scr, shsc_scr, send_sems, recv_sems, out_sems):
        s, t = pl.program_id(0), pl.program_id(1)
        xi, yi, c = _me()
        k, me = 2 * xi + yi, 4 * xi + 2 * yi + c
        sib = _flip(1)
        rows = pl.ds(pl.multiple_of(t * tl, tl), tl)
        gathered = (w_scr, wo_scr)
        slot = lambda d: pl.ds(pl.multiple_of(8 * d, 8), 8)

        def c_copy(q, owner):
            return _rcopy(mine, cs_scr.at[slot(owner), :], send_sems, recv_sems, n_w + q - 1, _flip(q))

        def m_copy(q, chip_of_block):
            return _rcopy(mod_scr.at[chip_of_block], mod_scr.at[chip_of_block], send_sems, recv_sems,
                          n_w + n_c + q // 2 - 1, _flip(q))

        def adaln():
            first = lax.broadcasted_iota(jnp.int32, (8, DM), 0) == 0
            mine[...] = jnp.where(first, jnp.broadcast_to(cv_ref[...], (8, DM)), 0.0)
            cs_scr[slot(me), :] = mine[...]
            cs_scr[slot(NDEV), :] = jnp.where(first, jnp.broadcast_to(cc_ref[...], (8, DM)), 0.0)
            for q in range(1, NDEV):
                c_copy(q, me).start()
            wa = wa_ref[...].astype(BF16)
            for q in range(1, NDEV):
                px, py, pc = _flip(q)
                c_copy(q, 4 * px + 2 * py + pc).wait_recv()
            act = jax.nn.silu(cs_scr[...]).astype(BF16)
            mod_scr[k] = jnp.dot(act, wa, preferred_element_type=F32) + b_ref[...]
            for q in (2, 4, 6):
                m_copy(q, k).start()
            for q in (2, 4, 6):
                m_copy(q, _chip_of(_flip(q))).wait_recv()
            row = pl.ds(8 * me, 1)
            shsc_scr[0:1, :] = _mod_part(mod_scr, row, 0)
            shsc_scr[1:2, :] = _mod_part(mod_scr, row, 1)
            pltpu.sync_copy(mod_scr, modall_ref)
            pltpu.sync_copy(cs_scr, csall_ref)

        def block(n, chip_of_block, hh):
            return gathered[n].at[chip_of_block, pl.ds(pl.multiple_of(hh * halves[n], halves[n]), halves[n]), :]

        def ici(n, q, chip_of_block):
            blk = block(n, chip_of_block, c)
            return _rcopy(blk, blk, send_sems, recv_sems, 6 * n + q // 2 - 1, _flip(q))

        def d2d(n, q, chip_of_block, hh):
            blk = block(n, chip_of_block, hh)
            return _rcopy(blk, blk, send_sems, recv_sems, 6 * n + 3 + q // 2 - 1, sib)

        @pl.when((s == 0) & (t == 0))
        def _():
            adaln()
            w_scr[k] = w_ref[...].astype(BF16)
            wo_scr[k] = wo_ref[...].astype(BF16)
            for q in (2, 4, 6):
                ici(0, q, k).start()
                ici(1, q, k).start()

        for sweep in (1, 2, 3):
            @pl.when((s == sweep) & (t == 0))
            def _():
                q = 2 * sweep
                src = _chip_of(_flip(q))
                for n in (0, 1):
                    ici(n, q, src).wait_recv()
                    d2d(n, q, src, c).start()
                for n in (0, 1):
                    d2d(n, q, src, 1 - c).wait_recv()

        @pl.when(s == 0)
        def _():
            hb = _modulated(x_ref[...], g_ref[...], shsc_scr[1:2, :], shsc_scr[0:1, :]).astype(BF16)
            h_scr[rows, :] = hb
            h_ref[...] = hb

        z_ref[...] = jnp.dot(h_scr[rows, :], w_scr[lax.bitwise_xor(k, s)], preferred_element_type=F32)

        @pl.when((s == NCHIP - 1) & (t == nt - 1))
        def _():
            for q in range(1, NDEV):
                c_copy(q, me).wait_send()
            for q in (2, 4, 6):
                m_copy(q, k).wait_send()
            for n in (0, 1):
                for q in (2, 4, 6):
                    ici(n, q, k).wait_send()
                    d2d(n, q, _chip_of(_flip(q)), c).wait_send()
            outs = [pltpu.make_async_copy(w_scr.at[j], wfull_ref.at[:, j * SHARD_IN:(j + 1) * SHARD_IN], out_sems.at[j])
                    for j in range(NCHIP)] + [pltpu.make_async_copy(wo_scr, wofull_ref, out_sems.at[NCHIP])]
            for cp in outs:
                cp.start()
            for cp in outs:
                cp.wait()

    once = lambda s, t, k: (jnp.where(s == 0, t, nt - 1), 0)
    hbm = pl.BlockSpec(memory_space=pl.ANY)
    n_sem = n_w + n_c + 3
    return pl.pallas_call(
        kern, name="inproj_fwd",
        grid_spec=pltpu.PrefetchScalarGridSpec(
            num_scalar_prefetch=1, grid=(NCHIP, nt),
            in_specs=[pl.BlockSpec((tl, DM), once)] + [_VMEM_SPEC] * 7,
            out_specs=[pl.BlockSpec((tl, SHARD_IN), lambda s, t, k: (t, lax.bitwise_xor(k[0], s))),
                       pl.BlockSpec((tl, DM), once), hbm, hbm, hbm, hbm],
            scratch_shapes=[pltpu.VMEM((NCHIP, DM, SHARD_IN), BF16), pltpu.VMEM((NCHIP, SHARD_OUT, DM), BF16),
                            pltpu.VMEM((SEQ, DM), BF16), pltpu.VMEM((8, DM), F32), pltpu.VMEM((CS_ROWS, DM), F32),
                            pltpu.VMEM((NCHIP, CS_ROWS, SHARD_ADA), F32), pltpu.VMEM((8, DM), F32),
                            pltpu.SemaphoreType.DMA((n_sem,)), pltpu.SemaphoreType.DMA((n_sem,)),
                            pltpu.SemaphoreType.DMA((NCHIP + 1,))]),
        out_shape=[jax.ShapeDtypeStruct((SEQ, DIN), F32), jax.ShapeDtypeStruct((SEQ, DM), BF16),
                   jax.ShapeDtypeStruct((DM, DIN), BF16), jax.ShapeDtypeStruct((NCHIP, SHARD_OUT, DM), BF16),
                   jax.ShapeDtypeStruct((NCHIP, CS_ROWS, SHARD_ADA), F32), jax.ShapeDtypeStruct((CS_ROWS, DM), F32)],
        compiler_params=_cparams(("arbitrary", "arbitrary"), VMEM_BIG),
    )(chip, x, c_vec, c_ctx, w_ada, b_shard, norm_g, w_shard, wo_shard)


def ctx_fwd(ctx, cshift, cscale, norm_g, w_full):
    def kern(c_ref, sh_ref, sc_ref, g_ref, w_ref, zc_ref, hc_ref):
        hc = _modulated(c_ref[...], g_ref[...], sc_ref[...], sh_ref[...]).astype(BF16)
        hc_ref[...] = hc
        zc_ref[...] = jnp.dot(hc, w_ref[...], preferred_element_type=F32)

    return pl.pallas_call(
        kern, name="ctx_fwd", grid=(1,),
        in_specs=[pl.BlockSpec((CTX, DM), lambda i: (0, 0)), _row(DM), _row(DM), _row(DM),
                  pl.BlockSpec((DM, 2 * SHARD_IN), lambda i: (0, 1))],
        out_specs=[pl.BlockSpec((CTX, 2 * SHARD_IN), lambda i: (0, 0)),
                   pl.BlockSpec((CTX, DM), lambda i: (0, 0))],
        out_shape=[jax.ShapeDtypeStruct((CTX, 2 * SHARD_IN), F32), jax.ShapeDtypeStruct((CTX, DM), BF16)],
        compiler_params=_cparams(("arbitrary",)),
    )(ctx, cshift, cscale, norm_g, w_full)


SGU_CHUNK, SGU_PER_STEP = 128, 4


def _gelu(x):
    return 0.5 * x * (1.0 + lax.erf(x * 0.7071067811865476))


def _sgu_chunk(au, av, ag, sg, ws, bsb):
    u, v = _gelu(au), _gelu(av)
    outs = []
    for g in range(4):
        sl = slice(128 * g, 128 * (g + 1))
        mixed = mm(ws[g], _rms(v[:, sl], sg[:, sl])) + bsb[g]
        outs.append(u[:, sl] * mixed * jax.nn.silu(ag[:, sl]))
    return jnp.concatenate(outs, axis=-1)


def _sgu_specs():
    rows = SGU_CHUNK * SGU_PER_STEP
    zspec = lambda c: pl.BlockSpec((rows, 512), lambda n: (n, c))
    wspec = pl.BlockSpec((4, 128, 128), lambda n: (0, 0, 0))
    return rows, [zspec(0), zspec(1), zspec(2), _row(512), wspec, wspec]


_DR_OFF = (7, 3, -1)


def _row_valid(v, rr, j):
    return (j < 8, rr <= j < rr + 8, 4 <= j < 12)[v]


def _col_window():
    q = lax.broadcasted_iota(jnp.int32, (GRID_W, 128), 0)
    kc = lax.broadcasted_iota(jnp.int32, (GRID_W, 128), 1) % GRID_W
    c0 = jnp.clip(q - 8, 0, GRID_W - 16)
    return (kc >= c0) & (kc < c0 + 16)


def _bias_tiles(base, store):
    lo = lax.broadcasted_iota(jnp.int32, (1, 128), 1) < GRID_W
    win = _col_window()
    tiles = {}
    for v in range(3):
        for rr in range(QROWS):
            for jp in range(KROWS // 2):
                j0, j1 = 2 * jp, 2 * jp + 1
                ok0, ok1 = _row_valid(v, rr, j0), _row_valid(v, rr, j1)
                key = (j0 - rr + _DR_OFF[v], ok0, ok1) if (ok0 or ok1) else None
                if key not in tiles:
                    if key is None:
                        tiles[key] = jnp.full((GRID_W, 128), NEG_INF, F32)
                    else:
                        d0 = key[0]
                        r0 = base[d0:d0 + 1, :] if ok0 else jnp.zeros((1, 128), F32)
                        r1 = base[d0 + 1:d0 + 2, :] if ok1 else jnp.zeros((1, 128), F32)
                        y = jnp.broadcast_to(jnp.where(lo, r0, r1), (GRID_W, 128))
                        y = pltpu.roll(pltpu.roll(y, 128 - 15, 1), 0, 1, stride=1, stride_axis=0)
                        tiles[key] = jnp.where(win & jnp.where(lo, ok0, ok1), y, NEG_INF)
                store(v, slice(rr * GRID_W, (rr + 1) * GRID_W), slice(jp * 128, (jp + 1) * 128), tiles[key])


def _rpb_grad(load):
    lo = lax.broadcasted_iota(jnp.int32, (1, 128), 1) < GRID_W
    ri = lax.broadcasted_iota(jnp.int32, (GRID_W, GRID_W), 0)
    ci = lax.broadcasted_iota(jnp.int32, (GRID_W, GRID_W), 1)
    flip = (ri + ci == GRID_W - 1).astype(F32)
    groups = {}
    for v in range(3):
        for rr in range(QROWS):
            for jp in range(KROWS // 2):
                j0, j1 = 2 * jp, 2 * jp + 1
                ok0, ok1 = _row_valid(v, rr, j0), _row_valid(v, rr, j1)
                if not (ok0 or ok1):
                    continue
                g = load(v, slice(rr * GRID_W, (rr + 1) * GRID_W), slice(jp * 128, (jp + 1) * 128))
                key = (j0 - rr + _DR_OFF[v], ok0, ok1)
                groups[key] = g if key not in groups else groups[key] + g
    acc = [jnp.zeros((1, 128), F32) for _ in range(15)]
    for (d0, ok0, ok1), g in groups.items():
        g = lax.dot_general(flip, g, (((1,), (0,)), ((), ())), precision=lax.Precision.HIGHEST,
                            preferred_element_type=F32)
        g = pltpu.roll(pltpu.roll(g, 128 - 48, 1), 0, 1, stride=1, stride_axis=0)
        s = jnp.sum(g, axis=0, keepdims=True)
        if ok0:
            acc[d0] = acc[d0] + jnp.where(lo, s, 0.0)
        if ok1:
            acc[d0 + 1] = acc[d0 + 1] + jnp.where(lo, 0.0, s)
    return [row + pltpu.roll(row, GRID_W, 1) for row in acc]


def _scaled_q(q_raw, qg):
    return _pair_rms(q_raw, qg) * (HDIM ** -0.5)


def _head_lanes():
    lo = lax.broadcasted_iota(jnp.int32, (1, 2 * HDIM), 1) < HDIM
    return lo, jnp.logical_not(lo)


SOFTMAX_ROWS = 32


def _emit_interleaved(vector_work, matmul_work):
    for j in range(max(len(vector_work), len(matmul_work))):
        for work in (vector_work, matmul_work):
            if j < len(work):
                work[j]()


def _kblock(i):
    return jnp.clip(i - 1, 0, (SEQ - KBLK) // QBLK)


def _kstart(i):
    return pl.multiple_of(_kblock(i) * QBLK, QBLK)


ATTN_BLOCKS = 4
TILE_BUFFERS = 4
ATTN_STEPS = NQBLK // ATTN_BLOCKS
ATTN_ROWS = ATTN_BLOCKS * QBLK


def _bias_variant(i, b):
    if b == 0:
        return jnp.where(i == 0, 0, 1)
    if b == ATTN_BLOCKS - 1:
        return jnp.where(i == ATTN_STEPS - 1, 2, 1)
    return 1
KCOLS = QBLK


def _attn_in_specs():
    return [
        pl.BlockSpec((ATTN_ROWS, 128), lambda p, i: (i, ZQ + p)),
        pl.BlockSpec((SEQ, 128), lambda p, i: (0, ZK + p)),
        pl.BlockSpec((SEQ, 128), lambda p, i: (0, ZV + p)),
        pl.BlockSpec((ATTN_ROWS, 128), lambda p, i: (i, ZG + p)),
        pl.BlockSpec((CTX, 128), lambda p, i: (0, 2 + p)),
        pl.BlockSpec((CTX, 128), lambda p, i: (0, 6 + p)),
    ]


def _rpb_spec():
    return pl.BlockSpec((2, 15, 128), lambda p, i: (p, 0, 0))


def _prob_specs():
    return [pl.BlockSpec((2, ATTN_ROWS, KBLK), lambda p, i: (p, i, 0)),
            pl.BlockSpec((2, ATTN_ROWS, CTX), lambda p, i: (p, i, 0))]


NORM_ROWS = 512


def _half_sums(x):
    lo = lax.broadcasted_iota(jnp.int32, (1, 2 * HDIM), 1) < HDIM
    return jnp.where(lo, jnp.sum(jnp.where(lo, x, 0.0), axis=-1, keepdims=True),
                     jnp.sum(jnp.where(lo, 0.0, x), axis=-1, keepdims=True))


def _pair_rms_bwd(x, g2, ct):
    rs = lax.rsqrt(_half_sums(x * x) / HDIM + EPS)
    y = x * rs
    dy = ct * g2
    return rs * (dy - y * (_half_sums(dy * y) / HDIM)), jnp.sum(ct * y, axis=0, keepdims=True)


def _norm_keys(k_ref, ck_ref, kg_ref, kn_scr, ckn_scr):
    def body(c, carry):
        sl = pl.ds(pl.multiple_of(c * NORM_ROWS, NORM_ROWS), NORM_ROWS)
        kn_scr[sl, :] = _pair_rms(k_ref[sl, :], kg_ref[...]).astype(BF16)
        return carry

    lax.fori_loop(0, SEQ // NORM_ROWS, body, 0)
    ckn_scr[...] = _pair_rms(ck_ref[...], kg_ref[...]).astype(BF16)


def _values_with_ones(v_ref, cv_ref, v1_scr, cv1_scr):
    for a, mine in enumerate(_head_lanes()):
        def body(c, carry):
            sl = pl.ds(pl.multiple_of(c * NORM_ROWS, NORM_ROWS), NORM_ROWS)
            v1_scr[a, sl, :] = jnp.where(mine, v_ref[sl, :], 1.0).astype(BF16)
            return carry

        lax.fori_loop(0, SEQ // NORM_ROWS, body, 0)
        cv1_scr[a] = jnp.where(mine, cv_ref[...], 1.0).astype(BF16)


def _pair_major_spec():
    return pl.BlockSpec((1, ATTN_ROWS, 128), lambda p, i: (p, i, 0))


def _normed_key_specs():
    return [pl.BlockSpec((None, SEQ, 128), lambda p, i: (p, 0, 0)), pl.BlockSpec((None, CTX, 128), lambda p, i: (p, 0, 0))]


def attn_fwd(z, zc, rpb2, qg2, kg2):
    def kern(q_ref, k_ref, v_ref, bg_ref, ck_ref, cv_ref, rpb_ref, qg_ref, kg_ref,
             ob_ref, o_ref, rden_ref, pl_ref, pc_ref, kn_ref, ckn_ref, kn_scr, ckn_scr, v1_scr, cv1_scr, s_scr,
             bias_ref):
        i = pl.program_id(1)

        @pl.when(i == 0)
        def _():
            for a in range(2):
                def store(v, tile_rows, tile_cols, tile, a=a):
                    bias_ref[v, a, tile_rows, tile_cols] = tile

                _bias_tiles(rpb_ref[a], store)
            _norm_keys(k_ref, ck_ref, kg_ref, kn_scr, ckn_scr)
            kn_ref[...] = kn_scr[...]
            ckn_ref[...] = ckn_scr[...]
            _values_with_ones(v_ref, cv_ref, v1_scr, cv1_scr)

        heads = _head_lanes()
        tiles = [(b, a) for b in range(ATTN_BLOCKS) for a in range(2)]
        rows = [slice(b * QBLK, (b + 1) * QBLK) for b in range(ATTN_BLOCKS)]
        variant = [_bias_variant(i, b) for b in range(ATTN_BLOCKS)]
        pv = [None] * len(tiles)
        qa, done = {}, {}
        latent = KBLK // KCOLS
        buf = lambda t: t % TILE_BUFFERS

        def keys(b, n):
            return pl.ds(pl.multiple_of(_kstart(ATTN_BLOCKS * i + b) + n * KCOLS, KCOLS), KCOLS)

        def score_piece(t, n):
            b, a = tiles[t]
            cols = slice(n * KCOLS, (n + 1) * KCOLS)
            if n == 0:
                if a == 0:
                    done["qn", b] = _scaled_q(q_ref[rows[b], :], qg_ref[...])
                qa[t] = jnp.where(heads[a], done["qn", b], 0.0).astype(BF16)
            if n < latent:
                s_scr[buf(t), :, cols] = mm_nt(qa[t], kn_scr[keys(b, n), :]) + bias_ref[variant[b], a, :, cols]
            else:
                s_scr[buf(t), :, cols] = mm_nt(qa[t], ckn_scr[...])

        def softmax_rows(t, r):
            b, a = tiles[t]
            rs = slice(r * SOFTMAX_ROWS, (r + 1) * SOFTMAX_ROWS)
            out_rows = slice(b * QBLK + rs.start, b * QBLK + rs.stop)
            s = s_scr[buf(t), rs, :]
            p = jnp.exp(s - jnp.max(s, axis=-1, keepdims=True)).astype(BF16)
            pl_ref[a, out_rows, :] = p[:, :KBLK]
            pc_ref[a, out_rows, :] = p[:, KBLK:]

        def value_piece(t, n):
            b, a = tiles[t]
            if n < latent:
                part = mm(pl_ref[a, rows[b], n * KCOLS:(n + 1) * KCOLS], v1_scr[a, keys(b, n), :])
            else:
                part = mm(pc_ref[a, rows[b], :], cv1_scr[a])
            pv[t] = part if pv[t] is None else pv[t] + part
            if n == latent:
                finish(t)

        def finish(t):
            b, a = tiles[t]
            r = jnp.where(heads[a], pltpu.roll(1.0 / pv[t], HDIM, 1), 0.0)
            done[t] = (pv[t] * r, r)
            if a == 1:
                o, rden = (lo + hi for lo, hi in zip(done[t - 1], done[t]))
                ob_ref[rows[b], :] = o * jax.nn.silu(bg_ref[rows[b], :])
                o_ref[0, rows[b], :] = o
                rden_ref[0, rows[b], :] = rden

        pieces = range(latent + 1)
        for n in pieces:
            score_piece(0, n)
        for t in range(len(tiles)):
            matmuls = []
            for n in pieces:
                if t + 1 < len(tiles):
                    matmuls.append(functools.partial(score_piece, t + 1, n))
                if t > 0:
                    matmuls.append(functools.partial(value_piece, t - 1, n))
            _emit_interleaved([functools.partial(softmax_rows, t, r) for r in range(QBLK // SOFTMAX_ROWS)], matmuls)
        for n in pieces:
            value_piece(len(tiles) - 1, n)

    qblk = pl.BlockSpec((ATTN_ROWS, 128), lambda p, i: (i, p))
    return pl.pallas_call(
        kern, name="attn_fwd", grid=(NPAIR, ATTN_STEPS),
        in_specs=_attn_in_specs() + [_rpb_spec(), _row(128), _row(128)],
        out_specs=[qblk, _pair_major_spec(), _pair_major_spec()] + _prob_specs() + _normed_key_specs(),
        out_shape=[jax.ShapeDtypeStruct((SEQ, 512), F32)] + [jax.ShapeDtypeStruct((NPAIR, SEQ, 128), F32)] * 2
        + [jax.ShapeDtypeStruct((HEADS, SEQ, KBLK), BF16), jax.ShapeDtypeStruct((HEADS, SEQ, CTX), BF16),
           jax.ShapeDtypeStruct((NPAIR, SEQ, 128), BF16), jax.ShapeDtypeStruct((NPAIR, CTX, 128), BF16)],
        scratch_shapes=[pltpu.VMEM((SEQ, 128), BF16), pltpu.VMEM((CTX, 128), BF16),
                        pltpu.VMEM((2, SEQ, 128), BF16), pltpu.VMEM((2, CTX, 128), BF16),
                        pltpu.VMEM((TILE_BUFFERS, QBLK, KBLK + CTX), F32),
                        pltpu.VMEM((3, 2, QBLK, KBLK), F32)],
        compiler_params=_cparams(("arbitrary", "arbitrary"), VMEM_BIG),
    )(z, z, z, z, zc, zc, rpb2, qg2, kg2)


def attn_bwd(z, zc, qg2, kg2, dcat, saved):
    def kern(q_ref, k_ref, v_ref, bg_ref, ck_ref, cv_ref, qg_ref, kg_ref, do_ref, o_ref, rden_ref, pl_ref, pc_ref,
             kn_scr, ckn_scr, dq_ref, dk_ref, dv_ref, dbg_ref, dck_ref, dcv_ref, drpb_ref, dqg_ref, dkg_ref,
             v_scr, cv_scr, dknt_scr, dvt_scr, dcknt_scr, dcvt_scr, dp_scr, ds_scr, db_ref):
        p, i = pl.program_id(0), pl.program_id(1)
        last = i == ATTN_STEPS - 1

        @pl.when(i == 0)
        def _():
            def body(c, carry):
                sl = pl.ds(pl.multiple_of(c * NORM_ROWS, NORM_ROWS), NORM_ROWS)
                v_scr[sl, :] = v_ref[sl, :].astype(BF16)
                return carry

            lax.fori_loop(0, SEQ // NORM_ROWS, body, 0)
            cv_scr[...] = cv_ref[...].astype(BF16)
            for acc in (dknt_scr, dvt_scr, dcknt_scr, dcvt_scr, db_ref):
                acc[...] = jnp.zeros_like(acc)

        @pl.when((i == 0) & (p == 0))
        def _():
            dqg_ref[...] = jnp.zeros_like(dqg_ref)
            dkg_ref[...] = jnp.zeros_like(dkg_ref)

        heads = _head_lanes()
        tiles = [(b, a) for b in range(ATTN_BLOCKS) for a in range(2)]
        rows = [slice(b * QBLK, (b + 1) * QBLK) for b in range(ATTN_BLOCKS)]
        kb = [_kblock(ATTN_BLOCKS * i + b) for b in range(ATTN_BLOCKS)]
        variant = [_bias_variant(i, b) for b in range(ATTN_BLOCKS)]
        latent = KBLK // KCOLS
        buf = lambda t: t % TILE_BUFFERS

        def keys(b, n):
            return pl.ds(pl.multiple_of((kb[b] + n) * KCOLS, KCOLS), KCOLS)

        gated = {}

        def gate_backward(b):
            bg, dout, o = bg_ref[rows[b], :], do_ref[rows[b], :], o_ref[0, rows[b], :]
            sig = jax.nn.sigmoid(bg)
            do = dout * (bg * sig)
            dbg_ref[rows[b], :] = (dout * o * (sig * (1.0 + bg * (1.0 - sig)))).astype(BF16)
            rden = rden_ref[0, rows[b], :]
            dr = do * rden
            qn = _scaled_q(q_ref[rows[b], :], qg_ref[...])
            gated[b] = (dr, dr.T.astype(BF16), qn.T.astype(BF16), do * o * rden)

        feats = [slice(a * HDIM, (a + 1) * HDIM) for a in range(2)]
        doa, doa_t, qa_t, delta = {}, {}, {}, {}
        dqn = [None] * len(tiles)

        def cols(n):
            return slice(n * KCOLS, (n + 1) * KCOLS)

        def stage_a(t, n):
            b, a = tiles[t]
            if n == 0:
                if a == 0:
                    gate_backward(b)
                dr, dr_t, qn_t, weighted = gated[b]
                doa[t] = jnp.where(heads[a], dr, 0.0).astype(BF16)
                doa_t[t] = dr_t[feats[a], :]
                qa_t[t] = qn_t[feats[a], :]
                delta[t] = jnp.sum(jnp.where(heads[a], weighted, 0.0), axis=-1, keepdims=True)
            if n < latent:
                dp_scr[buf(t), :, cols(n)] = mm_nt(doa[t], v_scr[keys(b, n), :])
                dvt_scr[kb[b] + n, feats[a], :] += mm(doa_t[t], pl_ref[a, rows[b], cols(n)])
            else:
                dp_scr[buf(t), :, cols(n)] = mm_nt(doa[t], cv_scr[...])
                dcvt_scr[feats[a], :] += mm(doa_t[t], pc_ref[a, rows[b], :])

        def stage_b(t, r):
            b, a = tiles[t]
            rs = slice(r * SOFTMAX_ROWS, (r + 1) * SOFTMAX_ROWS)
            in_rows = slice(b * QBLK + rs.start, b * QBLK + rs.stop)
            d = dp_scr[buf(t), rs, :] - delta[t][rs, :]
            ds_lat = pl_ref[a, in_rows, :].astype(F32) * d[:, :KBLK]
            ds_ctx = pc_ref[a, in_rows, :].astype(F32) * d[:, KBLK:]
            db_ref[variant[b], a, rs, :] += ds_lat
            ds_scr[buf(t), rs, :KBLK] = ds_lat.astype(BF16)
            ds_scr[buf(t), rs, KBLK:] = ds_ctx.astype(BF16)

        def stage_c(t, n):
            b, a = tiles[t]
            ds = ds_scr[buf(t), :, cols(n)]
            if n < latent:
                part = mm(ds, kn_scr[keys(b, n), :])
                dknt_scr[kb[b] + n, feats[a], :] += mm(qa_t[t], ds)
            else:
                part = mm(ds, ckn_scr[...])
                dcknt_scr[feats[a], :] += mm(qa_t[t], ds)
            dqn[t] = part if dqn[t] is None else dqn[t] + part
            if n == latent and a == 1:
                both = jnp.where(heads[0], dqn[t - 1], 0.0) + jnp.where(heads[1], dqn[t], 0.0)
                dq, dqg = jax.vjp(_scaled_q, q_ref[rows[b], :], qg_ref[...])[1](both)
                dq_ref[rows[b], :] = dq.astype(BF16)
                dqg_ref[...] += dqg

        pieces = range(latent + 1)
        for n in pieces:
            stage_a(0, n)
        for t in range(len(tiles)):
            matmuls = []
            for n in pieces:
                if t + 1 < len(tiles):
                    matmuls.append(functools.partial(stage_a, t + 1, n))
                if t > 0:
                    matmuls.append(functools.partial(stage_c, t - 1, n))
            _emit_interleaved([functools.partial(stage_b, t, r) for r in range(QBLK // SOFTMAX_ROWS)], matmuls)
        for n in pieces:
            stage_c(len(tiles) - 1, n)

        @pl.when(last)
        def _():
            eye = (lax.broadcasted_iota(jnp.int32, (KCOLS, KCOLS), 0)
                   == lax.broadcasted_iota(jnp.int32, (KCOLS, KCOLS), 1)).astype(BF16)

            def turned(x):
                hi = x.astype(BF16)
                return mm_nt(eye, hi) + mm_nt(eye, x - hi.astype(F32))

            def body(c, dkg):
                sl = pl.ds(pl.multiple_of(c * NORM_ROWS, NORM_ROWS), NORM_ROWS)
                blocks = range(NORM_ROWS // KCOLS)
                dkn = jnp.concatenate([turned(dknt_scr[c * len(blocks) + n]) for n in blocks], axis=0)
                dv = jnp.concatenate([mm_nt(eye, dvt_scr[c * len(blocks) + n]) for n in blocks], axis=0)
                dk, dg = _pair_rms_bwd(k_ref[sl, :], kg_ref[...], dkn)
                dk_ref[sl, :] = dk.astype(BF16)
                dv_ref[sl, :] = dv.astype(BF16)
                return dkg + dg

            dkg = lax.fori_loop(0, SEQ // NORM_ROWS, body, jnp.zeros((1, 128), F32))
            dck, dg = _pair_rms_bwd(ck_ref[...], kg_ref[...], dcknt_scr[...].T)
            dck_ref[...] = dck
            dcv_ref[...] = dcvt_scr[...].T
            dkg_ref[...] += dkg + dg
            for a in range(2):
                rows_of_rpb = _rpb_grad(lambda v, tile_rows, tile_cols, a=a: db_ref[v, a, tile_rows, tile_cols])
                for d, row in enumerate(rows_of_rpb):
                    drpb_ref[a, d:d + 1, :] = row

        @pl.when(last & (p == NPAIR - 1))
        def _():
            dqg_ref[...] = dqg_ref[...] + pltpu.roll(dqg_ref[...], HDIM, 1)
            dkg_ref[...] = dkg_ref[...] + pltpu.roll(dkg_ref[...], HDIM, 1)

    blk = lambda rows: pl.BlockSpec((rows, 128), lambda p, i: (0, p))
    qblk = pl.BlockSpec((ATTN_ROWS, 128), lambda p, i: (i, p))
    return pl.pallas_call(
        kern, name="attn_bwd", grid=(NPAIR, ATTN_STEPS),
        in_specs=_attn_in_specs() + [_row(128), _row(128), pl.BlockSpec((ATTN_ROWS, 128), lambda p, i: (i, 4 + p)),
                                     _pair_major_spec(), _pair_major_spec()] + _prob_specs() + _normed_key_specs(),
        out_specs=[qblk, blk(SEQ), blk(SEQ), qblk, blk(CTX), blk(CTX), _rpb_spec(), _row(128), _row(128)],
        out_shape=[jax.ShapeDtypeStruct((SEQ, 512), BF16)] * 4 + [jax.ShapeDtypeStruct((CTX, 512), F32)] * 2
        + [jax.ShapeDtypeStruct((HEADS, 15, 128), F32)]
        + [jax.ShapeDtypeStruct((1, 128), F32), jax.ShapeDtypeStruct((1, 128), F32)],
        scratch_shapes=[pltpu.VMEM((SEQ, 128), BF16), pltpu.VMEM((CTX, 128), BF16),
                        pltpu.VMEM((SEQ // KCOLS, 128, KCOLS), F32), pltpu.VMEM((SEQ // KCOLS, 128, KCOLS), F32),
                        pltpu.VMEM((128, CTX), F32), pltpu.VMEM((128, CTX), F32),
                        pltpu.VMEM((TILE_BUFFERS, QBLK, KBLK + CTX), F32),
                        pltpu.VMEM((TILE_BUFFERS, QBLK, KBLK + CTX), BF16),
                        pltpu.VMEM((3, 2, QBLK, KBLK), F32)],
        compiler_params=_cparams(("arbitrary", "arbitrary"), VMEM_BIG),
    )(z, z, z, z, zc, zc, qg2, kg2, dcat, *saved)


def outproj(z, sg, ws, bsb, out_b, x, target, gate, wo):
    tl = SGU_CHUNK * SGU_PER_STEP
    nt = SEQ // tl

    def kern(au0_ref, av0_ref, ag0_ref, au1_ref, av1_ref, ag1_ref, sg_ref, ws_ref, bs_ref, b_ref, x_ref, t_ref, g_ref,
             w_ref, loss_ref, dy_ref, dcat_ref, dg_ref, dw_ref, a_scr):
        t = pl.program_id(0)
        cur, nxt = lax.rem(t, 2), lax.rem(t + 1, 2)

        def gating(refs, slot, cn):
            sl = slice(cn * SGU_CHUNK, (cn + 1) * SGU_CHUNK)
            au_ref, av_ref, ag_ref = refs
            a_scr[slot, sl, :] = _sgu_chunk(au_ref[sl, :], av_ref[sl, :], ag_ref[sl, :], sg_ref[...], ws_ref[...],
                                            bs_ref[...]).astype(BF16)

        @pl.when(t == 0)
        def _():
            loss_ref[...] = jnp.zeros_like(loss_ref)
            dg_ref[...] = jnp.zeros_like(dg_ref)
            dw_ref[...] = jnp.zeros_like(dw_ref)
            for cn in range(SGU_PER_STEP):
                gating((au0_ref, av0_ref, ag0_ref), 0, cn)

        a, b = a_scr[cur], b_ref[...].astype(BF16)
        mix = (jnp.dot(a, w_ref[0:512, :], preferred_element_type=F32)
               + jnp.dot(b, w_ref[512:1024, :], preferred_element_type=F32))
        err = x_ref[...] + g_ref[...] * mix - t_ref[...]
        loss_ref[...] += 0.5 * jnp.sum(jnp.mean(err * err, axis=-1))
        dy = err * (1.0 / DM)
        dy_ref[...] = dy
        dg_ref[...] += jnp.sum(dy * mix, axis=0, keepdims=True)
        dmix = (g_ref[...] * dy).astype(BF16)

        def dcat_half(n):
            part = slice(512 * n, 512 * (n + 1))
            dcat_ref[:, part] = lax.dot_general(dmix, w_ref[part, :], _NT, preferred_element_type=F32)

        def dw_half(n, src):
            dw_ref[512 * n:512 * (n + 1), :] += lax.dot_general(src, dmix, (((0,), (0,)), ((), ())),
                                                                preferred_element_type=F32)

        _emit_interleaved([functools.partial(gating, (au1_ref, av1_ref, ag1_ref), nxt, cn) for cn in range(SGU_PER_STEP)],
                          [functools.partial(dcat_half, 0), functools.partial(dcat_half, 1),
                           functools.partial(dw_half, 0, a), functools.partial(dw_half, 1, b)])

    tile = lambda w: pl.BlockSpec((tl, w), lambda t: (t, 0))
    whole = pl.BlockSpec((DM, DM), lambda t: (0, 0))
    zfirst = [pl.BlockSpec((tl, 512), functools.partial(lambda c, t: (0, c), c)) for c in range(3)]
    znext = [pl.BlockSpec((tl, 512), functools.partial(lambda c, t: (jnp.minimum(t + 1, nt - 1), c), c))
             for c in range(3)]
    wspec = pl.BlockSpec((4, 128, 128), lambda t: (0, 0, 0))
    return pl.pallas_call(
        kern, name="outproj", grid=(nt,),
        in_specs=zfirst + znext + [_row(512), wspec, wspec, tile(512), tile(DM), tile(DM), _row(DM), whole],
        out_specs=[pl.BlockSpec((8, 128), lambda t: (0, 0)), tile(DM), tile(DM), _row(DM), whole],
        out_shape=[jax.ShapeDtypeStruct((8, 128), F32), jax.ShapeDtypeStruct((SEQ, DM), F32),
                   jax.ShapeDtypeStruct((SEQ, DM), F32), jax.ShapeDtypeStruct((1, DM), F32),
                   jax.ShapeDtypeStruct((DM, DM), F32)],
        scratch_shapes=[pltpu.VMEM((2, tl, 512), BF16)],
        compiler_params=_cparams(("arbitrary",), 48 * 1024 * 1024),
    )(z, z, z, z, z, z, sg, ws, bsb, out_b, x, target, gate, wo)


DZ_COLS = (("a", 0, 1536), ("q", 1536, 2048), ("k", 2048, 2560), ("v", 2560, 3072), ("g", 3072, DIN))
DZC_COLS = (("k", 2048, 2560), ("v", 2560, 3072))
_NT = (((1,), (1,)), ((), ()))


DH_SUBTILES = 2


def _dz_specs(tl):
    return [pl.BlockSpec((tl, 1536), lambda t: (t, 0))] + [pl.BlockSpec((tl, 512), lambda t: (t, 0))] * 4


def dh_bwd(dz_parts, w_full, x, dy, shift, scale, norm_g, dg_ctx):
    tl = 512
    nt = SEQ // tl

    def kern(a_ref, q_ref, k_ref, v_ref, g_ref, w_ref, x_ref, dy_ref, sh_ref, sc_ref, gn_ref, dgc_ref,
             gx_ref, dsh_ref, dsc_ref, dg_ref):
        @pl.when(pl.program_id(0) == 0)
        def _():
            dsh_ref[...] = jnp.zeros_like(dsh_ref)
            dsc_ref[...] = jnp.zeros_like(dsc_ref)
            dg_ref[...] = dgc_ref[...]

        src = dict(a=a_ref, q=q_ref, k=k_ref, v=v_ref, g=g_ref)
        for sub in range(DH_SUBTILES):
            rows = slice(sub * tl // DH_SUBTILES, (sub + 1) * tl // DH_SUBTILES)
            dh = None
            for name, c0, c1 in DZ_COLS:
                part = lax.dot_general(src[name][rows, :], w_ref[:, c0:c1], _NT, preferred_element_type=F32)
                dh = part if dh is None else dh + part
            _, vjp = jax.vjp(_modulated, x_ref[rows, :], gn_ref[...], sc_ref[...], sh_ref[...])
            dx, dg, dsc, dsh = vjp(dh)
            gx_ref[rows, :] = dy_ref[rows, :] + dx
            dg_ref[...] += dg
            dsc_ref[...] += dsc
            dsh_ref[...] += dsh

    tile = pl.BlockSpec((tl, DM), lambda t: (t, 0))
    return pl.pallas_call(
        kern, name="dh_bwd", grid=(nt,),
        in_specs=_dz_specs(tl) + [pl.BlockSpec((DM, DIN), lambda t: (0, 0)), tile, tile, _row(DM),
                                  _row(DM), _row(DM), _row(DM)],
        out_specs=[tile, _row(DM), _row(DM), _row(DM)],
        out_shape=[jax.ShapeDtypeStruct((SEQ, DM), F32)] + [jax.ShapeDtypeStruct((1, DM), F32)] * 3,
        compiler_params=_cparams(("arbitrary",), 48 * 1024 * 1024),
    )(*dz_parts, w_full, x, dy, shift, scale, norm_g, dg_ctx)


def dw_bwd(h, z, sg, ws, bsb, dcat, dz_attn, hc, dck, dcv, g_out):
    tl = SGU_CHUNK * SGU_PER_STEP
    nt = SEQ // tl
    (rhi, wi), (rho, wo) = RS_SHAPES

    def kern(h_ref, au_ref, av_ref, ag_ref, sg_ref, ws_ref, bs_ref, do_ref, q_ref, k_ref, v_ref, g_ref,
             hc_ref, dck_ref, dcv_ref, go_hbm,
             wire_i, keep_i, wire_o, keep_o, a_ref, dsg_ref, dws_ref, dbs_ref,
             acc, rcv_i, mine_o, rcv_o, load_sem, send_sems, recv_sems):
        t = pl.program_id(0)
        x, y, c = _me()
        k = 2 * x + y
        sib = _flip(1)
        half = lambda hh, rh: pl.ds(pl.multiple_of(hh * rh, rh), rh)
        load_o = pltpu.make_async_copy(go_hbm.at[:, half(c, rho), :], mine_o, load_sem)
        pair_o = _rcopy(go_hbm.at[:, half(1 - c, rho), :], rcv_o, send_sems, recv_sems, 0, sib)
        pair_i = [_rcopy(wire_i.at[j], rcv_i.at[j], send_sems, recv_sems, 1 + j, sib) for j in range(NCHIP)]

        @pl.when(t == 0)
        def _():
            load_o.start()
            pair_o.start()
            acc[...] = jnp.zeros_like(acc)
            dsg_ref[...] = jnp.zeros_like(dsg_ref)
            dws_ref[...] = jnp.zeros_like(dws_ref)
            dbs_ref[...] = jnp.zeros_like(dbs_ref)
            hct = hc_ref[...].T
            csrc = dict(k=dck_ref, v=dcv_ref)
            for name, c0, c1 in DZC_COLS:
                acc[:, c0:c1] += jnp.dot(hct, csrc[name][...].astype(BF16), preferred_element_type=F32)

        ht = h_ref[...].T
        src = dict(a=a_ref, q=q_ref, k=k_ref, v=v_ref, g=g_ref)

        def gating_backward(cn):
            sl = slice(cn * SGU_CHUNK, (cn + 1) * SGU_CHUNK)
            _, vjp = jax.vjp(_sgu_chunk, au_ref[sl, :], av_ref[sl, :], ag_ref[sl, :], sg_ref[...], ws_ref[...],
                             bs_ref[...])
            dau, dav, dag, dsg, dws, dbs = vjp(do_ref[sl, :])
            a_ref[sl, 0:512] = dau.astype(BF16)
            a_ref[sl, 512:1024] = dav.astype(BF16)
            a_ref[sl, 1024:1536] = dag.astype(BF16)
            dsg_ref[...] += dsg
            dws_ref[...] += dws
            dbs_ref[...] += dbs

        def product(name, c0, c1):
            acc[:, c0:c1] += jnp.dot(ht, src[name][...], preferred_element_type=F32)

        _emit_interleaved([functools.partial(gating_backward, cn) for cn in range(SGU_PER_STEP)],
                          [functools.partial(product, *cols) for cols in DZ_COLS[1:]])
        product(*DZ_COLS[0])

        @pl.when(t == nt - 1)
        def _():
            dbs_ref[...] = jnp.broadcast_to(jnp.sum(dbs_ref[...], axis=-1, keepdims=True), dbs_ref.shape)
            shard = lambda j: slice(j * SHARD_IN, (j + 1) * SHARD_IN)
            for j in range(NCHIP):
                wire_i[j] = acc[half(1 - c, rhi), shard(j)].astype(BF16)
                pair_i[j].start()
            load_o.wait()
            pair_o.wait_recv()
            for j in range(NCHIP):
                wire_o[j] = (mine_o[j] + rcv_o[j]).astype(BF16)
            keep_o[...] = mine_o[k] + rcv_o[k]
            mine = half(c, rhi)
            for j in range(NCHIP):
                pair_i[j].wait_recv()
                pair_i[j].wait_send()
                pair_sum = acc[mine, shard(j)] + rcv_i[j].astype(F32)
                wire_i[j] = pair_sum.astype(BF16)

                @pl.when(k == j)
                def _():
                    keep_i[...] = pair_sum
            pair_o.wait_send()

    whole = lambda *shape: pl.BlockSpec(shape, lambda t: (0,) * len(shape))
    rows, sgu_specs = _sgu_specs()
    assert rows == tl
    a_spec, *attn_specs = _dz_specs(tl)
    return pl.pallas_call(
        kern, name="dw_bwd", grid=(nt,),
        in_specs=[pl.BlockSpec((tl, DM), lambda t: (t, 0))] + sgu_specs + [pl.BlockSpec((tl, 512), lambda t: (t, 0))]
        + attn_specs + [whole(CTX, DM), whole(CTX, 512), whole(CTX, 512), pl.BlockSpec(memory_space=pl.ANY)],
        out_specs=[whole(NCHIP, rhi, wi), whole(rhi, wi), whole(NCHIP, rho, wo), whole(rho, wo),
                   a_spec, _row(512), whole(4, 128, 128), whole(4, 128, 128)],
        out_shape=[jax.ShapeDtypeStruct((NCHIP, rhi, wi), BF16), jax.ShapeDtypeStruct((rhi, wi), F32),
                   jax.ShapeDtypeStruct((NCHIP, rho, wo), BF16), jax.ShapeDtypeStruct((rho, wo), F32),
                   jax.ShapeDtypeStruct((SEQ, 1536), BF16), jax.ShapeDtypeStruct((1, 512), F32),
                   jax.ShapeDtypeStruct((4, 128, 128), F32), jax.ShapeDtypeStruct((4, 128, 128), F32)],
        scratch_shapes=[pltpu.VMEM((DM, DIN), F32), pltpu.VMEM((NCHIP, rhi, wi), BF16),
                        pltpu.VMEM((NCHIP, rho, wo), F32), pltpu.VMEM((NCHIP, rho, wo), F32),
                        pltpu.SemaphoreType.DMA(()), pltpu.SemaphoreType.DMA((1 + NCHIP,)),
                        pltpu.SemaphoreType.DMA((1 + NCHIP,))],
        compiler_params=_cparams(("arbitrary",), 60 * 1024 * 1024),
    )(h, z, z, z, sg, ws, bsb, dcat, *dz_attn, hc, dck, dcv, g_out)


def ctx_bwd(dck, dcv, w_full, ctx, cshift, cscale, norm_g):
    def kern(dck_ref, dcv_ref, w_ref, c_ref, sh_ref, sc_ref, g_ref, dsh_ref, dsc_ref, dg_ref):
        csrc = dict(k=dck_ref, v=dcv_ref)
        dhc = None
        first = DZC_COLS[0][1]
        for name, c0, c1 in DZC_COLS:
            part = lax.dot_general(csrc[name][...].astype(BF16), w_ref[:, c0 - first:c1 - first], _NT,
                                   preferred_element_type=F32)
            dhc = part if dhc is None else dhc + part
        _, vjp = jax.vjp(lambda g, sc, sh: _modulated(c_ref[...], g, sc, sh), g_ref[...], sc_ref[...], sh_ref[...])
        dg_ref[...], dsc_ref[...], dsh_ref[...] = vjp(dhc)

    whole = lambda r, c: pl.BlockSpec((r, c), lambda i: (0, 0))
    return pl.pallas_call(
        kern, name="ctx_bwd", grid=(1,),
        in_specs=[whole(CTX, 512), whole(CTX, 512), pl.BlockSpec((DM, 1024), lambda i: (0, DZC_COLS[0][1] // 1024)),
                  whole(CTX, DM), _row(DM), _row(DM), _row(DM)],
        out_specs=[_row(DM), _row(DM), _row(DM)],
        out_shape=[jax.ShapeDtypeStruct((1, DM), F32)] * 3,
        compiler_params=_cparams(("arbitrary",), 40 * 1024 * 1024),
    )(dck, dcv, w_full, ctx, cshift, cscale, norm_g)


def _lane_pad_rpb(rpb):
    r = jnp.pad(rpb, ((0, 0), (0, 0), (0, GRID_W - rpb.shape[-1])))
    return jnp.concatenate([r, r], axis=-1)


def local_step(chip, dev, x, c_vec, c_ctx, w_ada, b_shard, ctx, target, norm_g, sgu_g, w_s, b_s, q_g, k_g, rpb,
               w_in_shard, w_out_shard):
    bsb = jnp.broadcast_to(b_s[:, :, None], (4, 128, 128))
    qg2, kg2 = jnp.tile(q_g, (1, 2)), jnp.tile(k_g, (1, 2))

    z, h, w_in_full, w_out_full, mod_all, cs = inproj_fwd(chip, x, c_vec, c_ctx, w_ada, b_shard, norm_g, w_in_shard,
                                                          w_out_shard)
    mods = mod_all.transpose(1, 0, 2).reshape(CS_ROWS, 3 * DM)
    mod = lax.dynamic_slice(mods, (8 * dev, 0), (1, 3 * DM))
    shift, scale, gate = mod[:, :DM], mod[:, DM:2 * DM], mod[:, 2 * DM:]
    cshift, cscale = mods[8 * NDEV:8 * NDEV + 1, :DM], mods[8 * NDEV:8 * NDEV + 1, DM:2 * DM]
    zc, hc = ctx_fwd(ctx, cshift, cscale, norm_g, w_in_full)
    out_b, *saved = attn_fwd(z, zc, _lane_pad_rpb(rpb), qg2, kg2)
    loss8, dy, dcat, dgate, dwo = outproj(z, sgu_g, w_s, bsb, out_b, x, target, gate, w_out_full.reshape(DM, DM))
    dq, dk, dv, dbg, dck, dcv, drpb, dqg2, dkg2 = attn_bwd(z, zc, qg2, kg2, dcat, saved)
    drpb = drpb[:, :, :rpb.shape[-1]]
    dcshift, dcscale, dng_c = ctx_bwd(dck, dcv, w_in_full, ctx, cshift, cscale, norm_g)
    wire_i, keep_i, wire_o, keep_o, dz_a, dsg, dws, dbsb = dw_bwd(
        h, z, sgu_g, w_s, bsb, dcat, (dq, dk, dv, dbg), hc, dck, dcv, dwo.reshape(NCHIP, SHARD_OUT, DM))
    dz_parts = (dz_a, dq, dk, dv, dbg)
    *in_flight, token = rs_start(wire_i, wire_o)
    grad_x, dshift, dscale, dng = dh_bwd(dz_parts, w_in_full, x, dy, shift, scale, norm_g, dng_c + token[0, 0])
    got_i, got_o = rs_wait(*in_flight, dshift)
    return dict(
        loss=loss8[0:1, 0:1], grad_x=grad_x, rs=(keep_i, got_i, keep_o, got_o), cs=cs,
        dmod=jnp.concatenate([dshift, dscale, dgate], axis=-1),
        dcmod=jnp.concatenate([dcshift, dcscale, jnp.zeros((1, DM), F32)], axis=-1),
        d_norm_g=dng, d_sgu_g=dsg, d_w_s=dws, d_b_s=dbsb[:, :, 0],
        d_q_g=dqg2[:, :HDIM], d_k_g=dkg2[:, :HDIM], d_rpb=drpb)


def _me():
    return lax.axis_index("x"), lax.axis_index("y"), lax.axis_index("c")


def _flip(q):
    x, y, c = _me()
    return ((1 - x) if q & 4 else x, (1 - y) if q & 2 else y, (1 - c) if q & 1 else c)


def _chip_of(dev):
    return 2 * dev[0] + dev[1]


def _rcopy(src, dst, send_sems, recv_sems, k, dev):
    return pltpu.make_async_remote_copy(src_ref=src, dst_ref=dst, send_sem=send_sems.at[k], recv_sem=recv_sems.at[k],
                                        device_id=dev, device_id_type=MESH_ID)


_VMEM_SPEC = pl.BlockSpec(memory_space=pltpu.VMEM)
SLAB_ROWS = 80


RS_SHAPES = ((DM // 2, SHARD_IN), (SHARD_OUT // 2, DM))
_HBM_SPEC = pl.BlockSpec(memory_space=pltpu.HBM)
_SEM_SPEC = pl.BlockSpec(memory_space=pltpu.SEMAPHORE)
_IN_FLIGHT = pltpu.SideEffectType.DATAFLOW_SIDE_EFFECTING


def _rs_copies(wires, lands, send_sems, recv_sems):
    return [pltpu.make_async_remote_copy(
        src_ref=wires[n].at[_chip_of(_flip(q))], dst_ref=lands[n].at[q // 2 - 1],
        send_sem=send_sems.at[3 * n + q // 2 - 1], recv_sem=recv_sems.at[3 * n + q // 2 - 1],
        device_id=_flip(q), device_id_type=MESH_ID) for n in (0, 1) for q in (2, 4, 6)]


def rs_start(wire_i, wire_o):
    lands = [lax.empty((NCHIP - 1, rh, w), BF16) for rh, w in RS_SHAPES]

    def body(wi_ref, wo_ref, li_ref, lo_ref, send_sems, recv_sems, wi_thru, wo_thru, li_thru, lo_thru, token):
        for cp in _rs_copies((wi_ref, wo_ref), (li_ref, lo_ref), send_sems, recv_sems):
            cp.start()
        token[...] = jnp.zeros_like(token)

    hbm = lambda a: pltpu.HBM(a.shape, a.dtype)
    return pl.pallas_call(
        body, name="rs_start",
        out_shape=(pltpu.SemaphoreType.DMA((6,)), pltpu.SemaphoreType.DMA((6,)), hbm(wire_i), hbm(wire_o),
                   hbm(lands[0]), hbm(lands[1]), jax.ShapeDtypeStruct((8, 128), F32)),
        in_specs=(_HBM_SPEC,) * 4, out_specs=(_SEM_SPEC, _SEM_SPEC) + (_HBM_SPEC,) * 4 + (_VMEM_SPEC,),
        input_output_aliases={0: 2, 1: 3, 2: 4, 3: 5},
        compiler_params=pltpu.CompilerParams(has_side_effects=_IN_FLIGHT),
    )(*[pltpu.with_memory_space_constraint(a, pltpu.HBM) for a in (wire_i, wire_o, *lands)])


def rs_wait(send_sems, recv_sems, wire_i, wire_o, land_i, land_o, after):
    def body(wi_ref, wo_ref, li_ref, lo_ref, send_sems, recv_sems, after_ref, wi_dead, wo_dead, gi_ref, go_ref):
        for cp in _rs_copies((wi_ref, wo_ref), (li_ref, lo_ref), send_sems, recv_sems):
            cp.wait_send()
            cp.wait_recv()

    hbm = lambda a: pltpu.HBM(a.shape, a.dtype)
    return pl.pallas_call(
        body, name="rs_wait", out_shape=(hbm(wire_i), hbm(wire_o), hbm(land_i), hbm(land_o)),
        in_specs=(_HBM_SPEC,) * 4 + (_SEM_SPEC, _SEM_SPEC, pl.BlockSpec(memory_space=pl.ANY)),
        out_specs=(_HBM_SPEC,) * 4, input_output_aliases={0: 0, 1: 1, 2: 2, 3: 3},
        compiler_params=pltpu.CompilerParams(has_side_effects=_IN_FLIGHT),
    )(wire_i, wire_o, land_i, land_o, send_sems, recv_sems, after)[2:]


def final_reduce(keep_i, got_i, keep_o, got_o, slab, cs, w_ada, c_ctx):
    (rhi, wi), (rho, wo) = RS_SHAPES

    def kern(ki_hbm, gi_hbm, ko_hbm, go_hbm, s_ref, cs_ref, w_hbm, cc_ref,
             gin_ref, gout_ref, tot_ref, dw_ref, db_ref, dcc_ref,
             ki, gi, ko, go, w_scr, all_ref, dms_scr, parts, load_sems, send_sems, recv_sems):
        x, y, c = _me()
        k = 2 * x + y
        sib = _flip(1)
        dev = lambda d: 4 * d[0] + 2 * d[1] + d[2]
        me = dev((x, y, c))

        def slab_copy(idx, owner, to):
            return _rcopy(all_ref.at[dev(owner)], all_ref.at[dev(owner)], send_sems, recv_sems, idx, to)

        all_ref[me] = s_ref[...]
        first = [slab_copy(0, (x, y, c), sib)] + [slab_copy(q // 2, (x, y, c), _flip(q)) for q in (2, 4, 6)]
        for cp in first:
            cp.start()
        loads = [pltpu.make_async_copy(src, dst, load_sems.at[n]) for n, (src, dst) in enumerate(
            ((ki_hbm, ki), (gi_hbm, gi), (ko_hbm, ko), (go_hbm, go), (w_hbm, w_scr)))]
        for cp in loads:
            cp.start()

        shares = []
        for n, (keep, got, out) in enumerate(((ki, gi, gin_ref), (ko, go, gout_ref))):
            rh = RS_SHAPES[n][0]
            half = lambda hh, rh=rh: pl.ds(pl.multiple_of(hh * rh, rh), rh)
            loads[2 * n].wait()
            loads[2 * n + 1].wait()
            out[half(c), :] = ((keep[...] + got[0].astype(F32)) + got[1].astype(F32)) + got[2].astype(F32)
            share = _rcopy(out.at[half(c), :], out.at[half(c), :], send_sems, recv_sems, 7 + n, sib)
            share.start()
            shares.append((share, _rcopy(out.at[half(1 - c), :], out.at[half(1 - c), :], send_sems, recv_sems, 7 + n,
                                         sib)))

        passed = []
        for q in (2, 4, 6):
            slab_copy(q // 2, _flip(q), (x, y, c)).wait_recv()
            cp = slab_copy(3 + q // 2, _flip(q), sib)
            cp.start()
            passed.append(cp)
        slab_copy(0, sib, (x, y, c)).wait_recv()
        for q in (2, 4, 6):
            slab_copy(3 + q // 2, _flip(q | 1), (x, y, c)).wait_recv()
        tot = all_ref[0]
        for d in range(1, NDEV):
            tot = tot + all_ref[d]
        tot_ref[...] = tot

        pad = jnp.zeros((7, DM), F32)
        dm = [jnp.concatenate([all_ref[d, 12 + j:13 + j, :] for d in range(NDEV)] + [tot[9 + j:10 + j, :], pad], axis=0)
              for j in range(3)]
        db_ref[...] = jnp.concatenate([jnp.sum(part, axis=0, keepdims=True) for part in dm], axis=0)
        dm = jnp.concatenate(dm, axis=-1)
        for j in range(NCHIP):
            @pl.when(k == j)
            def _():
                dms_scr[...] = dm[:, j * SHARD_ADA:(j + 1) * SHARD_ADA].astype(BF16)

        a_in = jnp.concatenate([cs_ref[8 * d:8 * d + 1, :] for d in range(NDEV)]
                               + [cs_ref[8 * NDEV:8 * NDEV + 1, :], pad], axis=0)
        act = jax.nn.silu(a_in).astype(BF16)
        dms = dms_scr[...]
        dw_ref[...] = lax.dot_general(act, dms, (((0,), (0,)), ((), ())), preferred_element_type=F32)
        loads[4].wait()
        parts[k] = lax.dot_general(dms, w_scr[...].astype(BF16), (((1,), (1,)), ((), ())), preferred_element_type=F32)
        sends = [_rcopy(parts.at[k], parts.at[k], send_sems, recv_sems, 8 + q // 2, _flip(q)) for q in (2, 4, 6)]
        for cp in sends:
            cp.start()
        for q in (2, 4, 6):
            kq = _chip_of(_flip(q))
            _rcopy(parts.at[kq], parts.at[kq], send_sems, recv_sems, 8 + q // 2, _flip(q)).wait_recv()
        dact = ((parts[0] + parts[1]) + parts[2]) + parts[3]
        _, vjp = jax.vjp(jax.nn.silu, cc_ref[...])
        dcc_ref[...] = vjp(dact[8:9, :])[0]

        for share, arrival in shares:
            arrival.wait_recv()
            share.wait_send()
        for cp in first + passed + sends:
            cp.wait_send()

    any_spec = pl.BlockSpec(memory_space=pl.ANY)
    return pl.pallas_call(
        kern, name="final_reduce",
        in_specs=[any_spec] * 4 + [_VMEM_SPEC, _VMEM_SPEC, any_spec, _VMEM_SPEC], out_specs=[_VMEM_SPEC] * 6,
        out_shape=[jax.ShapeDtypeStruct((2 * rhi, wi), F32), jax.ShapeDtypeStruct((2 * rho, wo), F32),
                   jax.ShapeDtypeStruct((SLAB_ROWS, DM), F32), jax.ShapeDtypeStruct((DM, SHARD_ADA), F32),
                   jax.ShapeDtypeStruct((3, DM), F32), jax.ShapeDtypeStruct((1, DM), F32)],
        scratch_shapes=[pltpu.VMEM((rhi, wi), F32), pltpu.VMEM((NCHIP - 1, rhi, wi), BF16),
                        pltpu.VMEM((rho, wo), F32), pltpu.VMEM((NCHIP - 1, rho, wo), BF16),
                        pltpu.VMEM((DM, SHARD_ADA), F32), pltpu.VMEM((NDEV, SLAB_ROWS, DM), F32),
                        pltpu.VMEM((16, SHARD_ADA), BF16), pltpu.VMEM((NCHIP, 16, DM), F32),
                        pltpu.SemaphoreType.DMA((5,)), pltpu.SemaphoreType.DMA((12,)), pltpu.SemaphoreType.DMA((12,))],
        compiler_params=pltpu.CompilerParams(vmem_limit_bytes=40 * 1024 * 1024),
    )(keep_i, got_i, keep_o, got_o, slab, cs, w_ada, c_ctx)


def _adamw_math(w, g, m, v):
    m = B1 * m + (1.0 - B1) * g
    v = B2 * v + (1.0 - B2) * (g * g)
    m_hat = m / (1.0 - B1 ** STEP)
    v_hat = v / (1.0 - B2 ** STEP)
    return -LR * (m_hat / (jnp.sqrt(v_hat) + ADAM_EPS) + WD * w), m, v


def adamw_big(w, g, m, v, name, block_rows=256):
    rows, width = w.shape

    def kern(w_ref, g_ref, m_ref, v_ref, d_ref, nm_ref, nv_ref):
        d_ref[...], nm_ref[...], nv_ref[...] = _adamw_math(w_ref[...], g_ref[...], m_ref[...], v_ref[...])

    spec = pl.BlockSpec((block_rows, width), lambda i: (i, 0))
    return pl.pallas_call(
        kern, name=name, grid=(rows // block_rows,), in_specs=[spec] * 4, out_specs=[spec] * 3,
        out_shape=[jax.ShapeDtypeStruct((rows, width), F32)] * 3,
        compiler_params=_cparams(("arbitrary",)),
    )(w, g, m, v)


def adamw_small(quads):
    n = len(quads)

    def kern(*refs):
        ins, outs = refs[:4 * n], refs[4 * n:]
        for i in range(n):
            w, g, m, v = (r[...] for r in ins[4 * i:4 * i + 4])
            outs[3 * i][...], outs[3 * i + 1][...], outs[3 * i + 2][...] = _adamw_math(w, g, m, v)

    flat = [a for quad in quads for a in quad]
    res = pl.pallas_call(
        kern, name="adamw_small", in_specs=[_VMEM_SPEC] * (4 * n), out_specs=[_VMEM_SPEC] * (3 * n),
        out_shape=[jax.ShapeDtypeStruct(q[0].shape, F32) for q in quads for _ in range(3)],
    )(*flat)
    return [tuple(res[3 * i:3 * i + 3]) for i in range(n)]


def _rows_of(a, rows):
    flat = a.reshape(-1)
    return jnp.pad(flat, (0, rows * DM - flat.shape[0])).reshape(rows, DM)


def kernel(x, c, ctx, c_ctx, w_ada, b_ada, norm_g, w_in, sgu_norm_g, w_spatial, b_spatial, q_norm_g, k_norm_g, rpb, w_out, loss_target, m_c_ctx, m_w_ada, m_b_ada, m_norm_g, m_w_in, m_sgu_norm_g, m_w_spatial, m_b_spatial, m_q_norm_g, m_k_norm_g, m_rpb, m_w_out, v_c_ctx, v_w_ada, v_b_ada, v_norm_g, v_w_in, v_sgu_norm_g, v_w_spatial, v_b_spatial, v_q_norm_g, v_k_norm_g, v_rpb, v_w_out):
    xi, yi, ci = lax.axis_index("x"), lax.axis_index("y"), lax.axis_index("c")
    chip, dev = 2 * xi + yi, 4 * xi + 2 * yi + ci
    c_ctx2 = c_ctx.reshape(1, DM)

    b_shard = lax.dynamic_slice(b_ada, (0, chip * SHARD_ADA), (1, SHARD_ADA))
    part = local_step(chip.reshape(1).astype(jnp.int32), dev, x[0], c, c_ctx2, w_ada[0], b_shard, ctx[0], loss_target[0],
                      norm_g, sgu_norm_g, w_spatial[0], b_spatial[0], q_norm_g, k_norm_g, rpb[0], w_in[0], w_out[0])
    cs = part["cs"]

    slab = jnp.concatenate([
        part["d_norm_g"], _rows_of(part["d_sgu_g"], 1), _rows_of(part["d_b_s"], 1),
        _rows_of(jnp.concatenate([part["d_q_g"], part["d_k_g"]], axis=-1), 1), _rows_of(part["d_rpb"], 4),
        _rows_of(part["loss"], 1), _rows_of(part["dcmod"], 3), _rows_of(part["dmod"], 3), jnp.zeros((1, DM), F32),
        _rows_of(part["d_w_s"], 64)], axis=0)
    g_w_in, g_w_out, tot, g_w_ada, g_b_ada, g_c_ctx = final_reduce(*part["rs"], slab, cs, w_ada[0], c_ctx2)
    g_b_ada = g_b_ada.reshape(1, 3 * DM)

    loss = tot[8, 0]
    g_small = dict(
        c_ctx=g_c_ctx, b_ada=g_b_ada, norm_g=tot[0:1], sgu_norm_g=tot[1:2, :512], w_spatial=tot[16:80].reshape(512, 128),
        b_spatial=tot[2:3, :512].reshape(4, 128), q_norm_g=tot[3:4, :HDIM], k_norm_g=tot[3:4, HDIM:2 * HDIM],
        rpb=tot[4:8].reshape(-1)[:HEADS * 15 * 31].reshape(HEADS * 15, 31))
    shapes = dict(c_ctx=(DM,), w_ada=(1, DM, SHARD_ADA), b_ada=(1, 3 * DM), norm_g=(1, DM), w_in=(1, DM, SHARD_IN),
                  sgu_norm_g=(1, 512), w_spatial=(1, 4, 128, 128), b_spatial=(1, 4, 128), q_norm_g=(1, HDIM),
                  k_norm_g=(1, HDIM), rpb=(1, HEADS, 15, 31), w_out=(1, SHARD_OUT, DM))
    names = list(shapes)
    weights = dict(c_ctx=c_ctx, w_ada=w_ada, b_ada=b_ada, norm_g=norm_g, w_in=w_in, sgu_norm_g=sgu_norm_g,
                   w_spatial=w_spatial, b_spatial=b_spatial, q_norm_g=q_norm_g, k_norm_g=k_norm_g, rpb=rpb, w_out=w_out)
    m_in = dict(zip(names, (m_c_ctx, m_w_ada, m_b_ada, m_norm_g, m_w_in, m_sgu_norm_g, m_w_spatial, m_b_spatial,
                            m_q_norm_g, m_k_norm_g, m_rpb, m_w_out)))
    v_in = dict(zip(names, (v_c_ctx, v_w_ada, v_b_ada, v_norm_g, v_w_in, v_sgu_norm_g, v_w_spatial, v_b_spatial,
                            v_q_norm_g, v_k_norm_g, v_rpb, v_w_out)))
    grads = dict(g_small, w_ada=g_w_ada, w_in=g_w_in, w_out=g_w_out)
    upd = {}
    for n in ("w_ada", "w_in", "w_out"):
        g = grads[n]
        upd[n] = adamw_big(weights[n].reshape(g.shape), g, m_in[n].reshape(g.shape), v_in[n].reshape(g.shape),
                           "adamw_" + n)
    small = [n for n in names if n not in upd]
    res = adamw_small([(weights[n].reshape(grads[n].shape), grads[n], m_in[n].reshape(grads[n].shape),
                        v_in[n].reshape(grads[n].shape)) for n in small])
    upd.update(zip(small, res))
    out = [loss, part["grad_x"].reshape(1, SEQ, DM)]
    out += [grads[n].reshape(shapes[n]) for n in names]
    for slot in range(3):
        out += [upd[n][slot].reshape(shapes[n]) for n in names]
    return tuple(out)
```

```python
import functools

import jax
import jax.numpy as jnp
from jax import lax
from jax.experimental import pallas as pl
from jax.experimental.pallas import tpu as pltpu

F32, BF16 = jnp.float32, jnp.bfloat16
SEQ, DM, CTX, DIN = 4096, 1024, 256, 3584
NCHIP, NDEV = 4, 8
SHARD_IN = DIN // NCHIP
SHARD_ADA = 3 * DM // NCHIP
SHARD_OUT = DM // NCHIP
GRID_W = 64
QROWS = 4
KROWS = 12
QBLK, KBLK = QROWS * GRID_W, KROWS * GRID_W
NQBLK = SEQ // QBLK
HEADS, HDIM, NPAIR = 8, 64, 4
EPS = 1e-6
NEG_INF = -1e30
ZQ, ZK, ZV, ZG = 12, 16, 20, 24
LR, B1, B2, ADAM_EPS, WD, STEP = 0.001, 0.9, 0.999, 1e-08, 0.01, 10
VMEM_BIG = 56 * 1024 * 1024
MESH_ID = pl.DeviceIdType.MESH


def _dot(a, b, lhs_c, rhs_c):
    return lax.dot_general(a.astype(BF16), b.astype(BF16), (((lhs_c,), (rhs_c,)), ((), ())),
                           preferred_element_type=F32)


@jax.custom_vjp
def mm(a, b):
    return _dot(a, b, 1, 0)


@jax.custom_vjp
def mm_nt(a, b):
    return _dot(a, b, 1, 1)


@jax.custom_vjp
def mm_tn(a, b):
    return _dot(a, b, 0, 0)


mm.defvjp(lambda a, b: (mm(a, b), (a, b)), lambda r, ct: (mm_nt(ct, r[1]), mm_tn(r[0], ct)))
mm_nt.defvjp(lambda a, b: (mm_nt(a, b), (a, b)), lambda r, ct: (mm(ct, r[1]), mm_tn(ct, r[0])))
mm_tn.defvjp(lambda a, b: (mm_tn(a, b), (a, b)), lambda r, ct: (mm_nt(r[1], ct), mm(r[0], ct)))


def _rms(x, g):
    return x * lax.rsqrt(jnp.mean(x * x, axis=-1, keepdims=True) + EPS) * g


def _modulated(x, g, scale, shift):
    return _rms(x, g) * (1.0 + scale) + shift


def _pair_rms(x, g2):
    lo = lax.broadcasted_iota(jnp.int32, (1, 2 * HDIM), 1) < HDIM
    sq = x * x
    s_lo = jnp.sum(jnp.where(lo, sq, 0.0), axis=-1, keepdims=True)
    s_hi = jnp.sum(jnp.where(lo, 0.0, sq), axis=-1, keepdims=True)
    rs = jnp.where(lo, lax.rsqrt(s_lo / HDIM + EPS), lax.rsqrt(s_hi / HDIM + EPS))
    return x * rs * g2


def _cparams(sem, vmem=None):
    return pltpu.CompilerParams(dimension_semantics=sem, vmem_limit_bytes=vmem)


def _row(n):
    return pl.BlockSpec((1, n), lambda *_: (0, 0))


CS_ROWS = 8 * NDEV + 8


def _mod_part(mod_ref, row, part):
    pieces = []
    for j in range(NCHIP):
        lo, hi = max(part * DM, j * SHARD_ADA), min((part + 1) * DM, (j + 1) * SHARD_ADA)
        if lo < hi:
            pieces.append(mod_ref[j, row, lo - j * SHARD_ADA:hi - j * SHARD_ADA])
    return jnp.concatenate(pieces, axis=-1)


def inproj_fwd(chip, x, c_vec, c_ctx, w_ada, b_shard, norm_g, w_shard, wo_shard):
    tl = 1024
    nt = SEQ // tl
    halves = (DM // 2, SHARD_OUT // 2)
    n_w, n_c = 12, NDEV - 1

    def kern(k_ref, x_ref, cv_ref, cc_ref, wa_ref, b_ref, g_ref, w_ref, wo_ref,
             z_ref, h_ref, wfull_ref, wofull_ref, modall_ref, csall_ref,
             w_scr, wo_scr, h_scr, mine, cs_scr, mod_scr, shsc_scr, send_sems, recv_sems, out_sems):
        s, t = pl.program_id(0), pl.program_id(1)
        xi, yi, c = _me()
        k, me = 2 * xi + yi, 4 * xi + 2 * yi + c
        sib = _flip(1)
        rows = pl.ds(pl.multiple_of(t * tl, tl), tl)
        gathered = (w_scr, wo_scr)
        slot = lambda d: pl.ds(pl.multiple_of(8 * d, 8), 8)

        def c_copy(q, owner):
            return _rcopy(mine, cs_scr.at[slot(owner), :], send_sems, recv_sems, n_w + q - 1, _flip(q))

        def m_copy(q, chip_of_block):
            return _rcopy(mod_scr.at[chip_of_block], mod_scr.at[chip_of_block], send_sems, recv_sems,
                          n_w + n_c + q // 2 - 1, _flip(q))

        def adaln():
            first = lax.broadcasted_iota(jnp.int32, (8, DM), 0) == 0
            mine[...] = jnp.where(first, jnp.broadcast_to(cv_ref[...], (8, DM)), 0.0)
            cs_scr[slot(me), :] = mine[...]
            cs_scr[slot(NDEV), :] = jnp.where(first, jnp.broadcast_to(cc_ref[...], (8, DM)), 0.0)
            for q in range(1, NDEV):
                c_copy(q, me).start()
            wa = wa_ref[...].astype(BF16)
            for q in range(1, NDEV):
                px, py, pc = _flip(q)
                c_copy(q, 4 * px + 2 * py + pc).wait_recv()
            act = jax.nn.silu(cs_scr[...]).astype(BF16)
            mod_scr[k] = jnp.dot(act, wa, preferred_element_type=F32) + b_ref[...]
            for q in (2, 4, 6):
                m_copy(q, k).start()
            for q in (2, 4, 6):
                m_copy(q, _chip_of(_flip(q))).wait_recv()
            row = pl.ds(8 * me, 1)
            shsc_scr[0:1, :] = _mod_part(mod_scr, row, 0)
            shsc_scr[1:2, :] = _mod_part(mod_scr, row, 1)
            pltpu.sync_copy(mod_scr, modall_ref)
            pltpu.sync_copy(cs_scr, csall_ref)

        def block(n, chip_of_block, hh):
            return gathered[n].at[chip_of_block, pl.ds(pl.multiple_of(hh * halves[n], halves[n]), halves[n]), :]

        def ici(n, q, chip_of_block):
            blk = block(n, chip_of_block, c)
            return _rcopy(blk, blk, send_sems, recv_sems, 6 * n + q // 2 - 1, _flip(q))

        def d2d(n, q, chip_of_block, hh):
            blk = block(n, chip_of_block, hh)
            return _rcopy(blk, blk, send_sems, recv_sems, 6 * n + 3 + q // 2 - 1, sib)

        @pl.when((s == 0) & (t == 0))
        def _():
            adaln()
            w_scr[k] = w_ref[...].astype(BF16)
            wo_scr[k] = wo_ref[...].astype(BF16)
            for q in (2, 4, 6):
                ici(0, q, k).start()
                ici(1, q, k).start()

        for sweep in (1, 2, 3):
            @pl.when((s == sweep) & (t == 0))
            def _():
                q = 2 * sweep
                src = _chip_of(_flip(q))
                for n in (0, 1):
                    ici(n, q, src).wait_recv()
                    d2d(n, q, src, c).start()
                for n in (0, 1):
                    d2d(n, q, src, 1 - c).wait_recv()

        @pl.when(s == 0)
        def _():
            hb = _modulated(x_ref[...], g_ref[...], shsc_scr[1:2, :], shsc_scr[0:1, :]).astype(BF16)
            h_scr[rows, :] = hb
            h_ref[...] = hb

        z_ref[...] = jnp.dot(h_scr[rows, :], w_scr[lax.bitwise_xor(k, s)], preferred_element_type=F32)

        @pl.when((s == NCHIP - 1) & (t == nt - 1))
        def _():
            for q in range(1, NDEV):
                c_copy(q, me).wait_send()
            for q in (2, 4, 6):
                m_copy(q, k).wait_send()
            for n in (0, 1):
                for q in (2, 4, 6):
                    ici(n, q, k).wait_send()
                    d2d(n, q, _chip_of(_flip(q)), c).wait_send()
            outs = [pltpu.make_async_copy(w_scr.at[j], wfull_ref.at[:, j * SHARD_IN:(j + 1) * SHARD_IN], out_sems.at[j])
                    for j in range(NCHIP)] + [pltpu.make_async_copy(wo_scr, wofull_ref, out_sems.at[NCHIP])]
            for cp in outs:
                cp.start()
            for cp in outs:
                cp.wait()

    once = lambda s, t, k: (jnp.where(s == 0, t, nt - 1), 0)
    hbm = pl.BlockSpec(memory_space=pl.ANY)
    n_sem = n_w + n_c + 3
    return pl.pallas_call(
        kern, name="inproj_fwd",
        grid_spec=pltpu.PrefetchScalarGridSpec(
            num_scalar_prefetch=1, grid=(NCHIP, nt),
            in_specs=[pl.BlockSpec((tl, DM), once)] + [_VMEM_SPEC] * 7,
            out_specs=[pl.BlockSpec((tl, SHARD_IN), lambda s, t, k: (t, lax.bitwise_xor(k[0], s))),
                       pl.BlockSpec((tl, DM), once), hbm, hbm, hbm, hbm],
            scratch_shapes=[pltpu.VMEM((NCHIP, DM, SHARD_IN), BF16), pltpu.VMEM((NCHIP, SHARD_OUT, DM), BF16),
                            pltpu.VMEM((SEQ, DM), BF16), pltpu.VMEM((8, DM), F32), pltpu.VMEM((CS_ROWS, DM), F32),
                            pltpu.VMEM((NCHIP, CS_ROWS, SHARD_ADA), F32), pltpu.VMEM((8, DM), F32),
                            pltpu.SemaphoreType.DMA((n_sem,)), pltpu.SemaphoreType.DMA((n_sem,)),
                            pltpu.SemaphoreType.DMA((NCHIP + 1,))]),
        out_shape=[jax.ShapeDtypeStruct((SEQ, DIN), F32), jax.ShapeDtypeStruct((SEQ, DM), BF16),
                   jax.ShapeDtypeStruct((DM, DIN), BF16), jax.ShapeDtypeStruct((NCHIP, SHARD_OUT, DM), BF16),
                   jax.ShapeDtypeStruct((NCHIP, CS_ROWS, SHARD_ADA), F32), jax.ShapeDtypeStruct((CS_ROWS, DM), F32)],
        compiler_params=_cparams(("arbitrary", "arbitrary"), VMEM_BIG),
    )(chip, x, c_vec, c_ctx, w_ada, b_shard, norm_g, w_shard, wo_shard)


def ctx_fwd(ctx, cshift, cscale, norm_g, w_full):
    blocks = 2 * SHARD_IN // 256

    def kern(c_ref, sh_ref, sc_ref, g_ref, w_ref, zc_ref, hc_ref):
        @pl.when(pl.program_id(0) == 0)
        def _():
            hc_ref[...] = _modulated(c_ref[...], g_ref[...], sc_ref[...], sh_ref[...]).astype(BF16)

        zc_ref[...] = jnp.dot(hc_ref[...], w_ref[...], preferred_element_type=F32)

    return pl.pallas_call(
        kern, name="ctx_fwd", grid=(blocks,),
        in_specs=[pl.BlockSpec((CTX, DM), lambda i: (0, 0)), _row(DM), _row(DM), _row(DM),
                  pl.BlockSpec((DM, 256), lambda i: (0, blocks + i))],
        out_specs=[pl.BlockSpec((CTX, 256), lambda i: (0, i)),
                   pl.BlockSpec((CTX, DM), lambda i: (0, 0))],
        out_shape=[jax.ShapeDtypeStruct((CTX, 2 * SHARD_IN), F32), jax.ShapeDtypeStruct((CTX, DM), BF16)],
        compiler_params=_cparams(("arbitrary",)),
    )(ctx, cshift, cscale, norm_g, w_full)


SGU_CHUNK, SGU_PER_STEP = 128, 4


def _gelu(x):
    return 0.5 * x * (1.0 + lax.erf(x * 0.7071067811865476))


def _sgu_chunk(au, av, ag, sg, ws, bsb):
    u, v = _gelu(au), _gelu(av)
    outs = []
    for g in range(4):
        sl = slice(128 * g, 128 * (g + 1))
        mixed = mm(ws[g], _rms(v[:, sl], sg[:, sl])) + bsb[g]
        outs.append(u[:, sl] * mixed * jax.nn.silu(ag[:, sl]))
    return jnp.concatenate(outs, axis=-1)


def _sgu_specs():
    rows = SGU_CHUNK * SGU_PER_STEP
    zspec = lambda c: pl.BlockSpec((rows, 512), lambda n: (n, c))
    wspec = pl.BlockSpec((4, 128, 128), lambda n: (0, 0, 0))
    return rows, [zspec(0), zspec(1), zspec(2), _row(512), wspec, wspec]


_DR_OFF = (7, 3, -1)


def _row_valid(v, rr, j):
    return (j < 8, rr <= j < rr + 8, 4 <= j < 12)[v]


def _col_window():
    q = lax.broadcasted_iota(jnp.int32, (GRID_W, 128), 0)
    kc = lax.broadcasted_iota(jnp.int32, (GRID_W, 128), 1) % GRID_W
    c0 = jnp.clip(q - 8, 0, GRID_W - 16)
    return (kc >= c0) & (kc < c0 + 16)


def _bias_tiles(base, store):
    lo = lax.broadcasted_iota(jnp.int32, (1, 128), 1) < GRID_W
    win = _col_window()
    tiles = {}
    for v in range(3):
        for rr in range(QROWS):
            for jp in range(KROWS // 2):
                j0, j1 = 2 * jp, 2 * jp + 1
                ok0, ok1 = _row_valid(v, rr, j0), _row_valid(v, rr, j1)
                key = (j0 - rr + _DR_OFF[v], ok0, ok1) if (ok0 or ok1) else None
                if key not in tiles:
                    if key is None:
                        tiles[key] = jnp.full((GRID_W, 128), NEG_INF, F32)
                    else:
                        d0 = key[0]
                        r0 = base[d0:d0 + 1, :] if ok0 else jnp.zeros((1, 128), F32)
                        r1 = base[d0 + 1:d0 + 2, :] if ok1 else jnp.zeros((1, 128), F32)
                        y = jnp.broadcast_to(jnp.where(lo, r0, r1), (GRID_W, 128))
                        y = pltpu.roll(pltpu.roll(y, 128 - 15, 1), 0, 1, stride=1, stride_axis=0)
                        tiles[key] = jnp.where(win & jnp.where(lo, ok0, ok1), y, NEG_INF)
                store(v, slice(rr * GRID_W, (rr + 1) * GRID_W), slice(jp * 128, (jp + 1) * 128), tiles[key])


def _rpb_grad(load):
    lo = lax.broadcasted_iota(jnp.int32, (1, 128), 1) < GRID_W
    ri = lax.broadcasted_iota(jnp.int32, (GRID_W, GRID_W), 0)
    ci = lax.broadcasted_iota(jnp.int32, (GRID_W, GRID_W), 1)
    flip = (ri + ci == GRID_W - 1).astype(F32)
    groups = {}
    for v in range(3):
        for rr in range(QROWS):
            for jp in range(KROWS // 2):
                j0, j1 = 2 * jp, 2 * jp + 1
                ok0, ok1 = _row_valid(v, rr, j0), _row_valid(v, rr, j1)
                if not (ok0 or ok1):
                    continue
                g = load(v, slice(rr * GRID_W, (rr + 1) * GRID_W), slice(jp * 128, (jp + 1) * 128))
                key = (j0 - rr + _DR_OFF[v], ok0, ok1)
                groups[key] = g if key not in groups else groups[key] + g
    acc = [jnp.zeros((1, 128), F32) for _ in range(15)]
    for (d0, ok0, ok1), g in groups.items():
        g = lax.dot_general(flip, g, (((1,), (0,)), ((), ())), precision=lax.Precision.HIGHEST,
                            preferred_element_type=F32)
        g = pltpu.roll(pltpu.roll(g, 128 - 48, 1), 0, 1, stride=1, stride_axis=0)
        s = jnp.sum(g, axis=0, keepdims=True)
        if ok0:
            acc[d0] = acc[d0] + jnp.where(lo, s, 0.0)
        if ok1:
            acc[d0 + 1] = acc[d0 + 1] + jnp.where(lo, 0.0, s)
    return [row + pltpu.roll(row, GRID_W, 1) for row in acc]


def _scaled_q(q_raw, qg):
    return _pair_rms(q_raw, qg) * (HDIM ** -0.5)


def _head_lanes():
    lo = lax.broadcasted_iota(jnp.int32, (1, 2 * HDIM), 1) < HDIM
    return lo, jnp.logical_not(lo)


SOFTMAX_ROWS = 32


def _emit_interleaved(vector_work, matmul_work):
    for j in range(max(len(vector_work), len(matmul_work))):
        for work in (vector_work, matmul_work):
            if j < len(work):
                work[j]()


def _kblock(i):
    return jnp.clip(i - 1, 0, (SEQ - KBLK) // QBLK)


def _kstart(i):
    return pl.multiple_of(_kblock(i) * QBLK, QBLK)


ATTN_BLOCKS = 4
TILE_BUFFERS = 4
ATTN_STEPS = NQBLK // ATTN_BLOCKS
ATTN_ROWS = ATTN_BLOCKS * QBLK


def _bias_variant(i, b):
    if b == 0:
        return jnp.where(i == 0, 0, 1)
    if b == ATTN_BLOCKS - 1:
        return jnp.where(i == ATTN_STEPS - 1, 2, 1)
    return 1
KCOLS = QBLK


def _attn_in_specs():
    return [
        pl.BlockSpec((ATTN_ROWS, 128), lambda p, i: (i, ZQ + p)),
        pl.BlockSpec((SEQ, 128), lambda p, i: (0, ZK + p)),
        pl.BlockSpec((SEQ, 128), lambda p, i: (0, ZV + p)),
        pl.BlockSpec((ATTN_ROWS, 128), lambda p, i: (i, ZG + p)),
        pl.BlockSpec((CTX, 128), lambda p, i: (0, 2 + p)),
        pl.BlockSpec((CTX, 128), lambda p, i: (0, 6 + p)),
    ]


def _rpb_spec():
    return pl.BlockSpec((2, 15, 128), lambda p, i: (p, 0, 0))


def _prob_specs():
    return [pl.BlockSpec((2, ATTN_ROWS, KBLK), lambda p, i: (p, i, 0)),
            pl.BlockSpec((2, ATTN_ROWS, CTX), lambda p, i: (p, i, 0))]


NORM_ROWS = 512


def _half_sums(x):
    lo = lax.broadcasted_iota(jnp.int32, (1, 2 * HDIM), 1) < HDIM
    return jnp.where(lo, jnp.sum(jnp.where(lo, x, 0.0), axis=-1, keepdims=True),
                     jnp.sum(jnp.where(lo, 0.0, x), axis=-1, keepdims=True))


def _pair_rms_bwd(x, g2, ct):
    rs = lax.rsqrt(_half_sums(x * x) / HDIM + EPS)
    y = x * rs
    dy = ct * g2
    return rs * (dy - y * (_half_sums(dy * y) / HDIM)), jnp.sum(ct * y, axis=0, keepdims=True)


def _norm_keys(k_ref, ck_ref, kg_ref, kn_scr, ckn_scr):
    def body(c, carry):
        sl = pl.ds(pl.multiple_of(c * NORM_ROWS, NORM_ROWS), NORM_ROWS)
        kn_scr[sl, :] = _pair_rms(k_ref[sl, :], kg_ref[...]).astype(BF16)
        return carry

    lax.fori_loop(0, SEQ // NORM_ROWS, body, 0, unroll=2)
    ckn_scr[...] = _pair_rms(ck_ref[...], kg_ref[...]).astype(BF16)


def _values_with_ones(v_ref, cv_ref, v1_scr, cv1_scr):
    for a, mine in enumerate(_head_lanes()):
        def body(c, carry):
            sl = pl.ds(pl.multiple_of(c * NORM_ROWS, NORM_ROWS), NORM_ROWS)
            v1_scr[a, sl, :] = jnp.where(mine, v_ref[sl, :], 1.0).astype(BF16)
            return carry

        lax.fori_loop(0, SEQ // NORM_ROWS, body, 0)
        cv1_scr[a] = jnp.where(mine, cv_ref[...], 1.0).astype(BF16)


def _pair_major_spec():
    return pl.BlockSpec((1, ATTN_ROWS, 128), lambda p, i: (p, i, 0))


def _normed_key_specs():
    return [pl.BlockSpec((None, SEQ, 128), lambda p, i: (p, 0, 0)), pl.BlockSpec((None, CTX, 128), lambda p, i: (p, 0, 0))]


def attn_fwd(z, zc, rpb2, qg2, kg2):
    def kern(q_ref, k_ref, v_ref, bg_ref, ck_ref, cv_ref, rpb_ref, qg_ref, kg_ref,
             ob_ref, o_ref, rden_ref, pl_ref, pc_ref, kn_ref, ckn_ref, kn_scr, ckn_scr, v1_scr, cv1_scr, s_scr,
             bias_ref):
        i = pl.program_id(1)

        @pl.when(i == 0)
        def _():
            for a in range(2):
                def store(v, tile_rows, tile_cols, tile, a=a):
                    bias_ref[v, a, tile_rows, tile_cols] = tile

                _bias_tiles(rpb_ref[a], store)
            _norm_keys(k_ref, ck_ref, kg_ref, kn_scr, ckn_scr)
            kn_ref[...] = kn_scr[...]
            ckn_ref[...] = ckn_scr[...]
            _values_with_ones(v_ref, cv_ref, v1_scr, cv1_scr)

        heads = _head_lanes()
        tiles = [(b, a) for b in range(ATTN_BLOCKS) for a in range(2)]
        rows = [slice(b * QBLK, (b + 1) * QBLK) for b in range(ATTN_BLOCKS)]
        variant = [_bias_variant(i, b) for b in range(ATTN_BLOCKS)]
        pv = [None] * len(tiles)
        qa, done = {}, {}
        latent = KBLK // KCOLS
        buf = lambda t: t % TILE_BUFFERS

        def keys(b, n):
            return pl.ds(pl.multiple_of(_kstart(ATTN_BLOCKS * i + b) + n * KCOLS, KCOLS), KCOLS)

        def score_piece(t, n):
            b, a = tiles[t]
            cols = slice(n * KCOLS, (n + 1) * KCOLS)
            if n == 0:
                if a == 0:
                    done["qn", b] = _scaled_q(q_ref[rows[b], :], qg_ref[...])
                qa[t] = jnp.where(heads[a], done["qn", b], 0.0).astype(BF16)
            if n < latent:
                s_scr[buf(t), :, cols] = mm_nt(qa[t], kn_scr[keys(b, n), :]) + bias_ref[variant[b], a, :, cols]
            else:
                s_scr[buf(t), :, cols] = mm_nt(qa[t], ckn_scr[...])

        def softmax_rows(t, r):
            b, a = tiles[t]
            rs = slice(r * SOFTMAX_ROWS, (r + 1) * SOFTMAX_ROWS)
            out_rows = slice(b * QBLK + rs.start, b * QBLK + rs.stop)
            s = s_scr[buf(t), rs, :]
            p = jnp.exp(s - jnp.max(s, axis=-1, keepdims=True)).astype(BF16)
            pl_ref[a, out_rows, :] = p[:, :KBLK]
            pc_ref[a, out_rows, :] = p[:, KBLK:]

        def value_piece(t, n):
            b, a = tiles[t]
            if n < latent:
                part = mm(pl_ref[a, rows[b], n * KCOLS:(n + 1) * KCOLS], v1_scr[a, keys(b, n), :])
            else:
                part = mm(pc_ref[a, rows[b], :], cv1_scr[a])
            pv[t] = part if pv[t] is None else pv[t] + part
            if n == latent:
                finish(t)

        def finish(t):
            b, a = tiles[t]
            r = jnp.where(heads[a], pltpu.roll(1.0 / pv[t], HDIM, 1), 0.0)
            done[t] = (pv[t] * r, r)
            if a == 1:
                o, rden = (lo + hi for lo, hi in zip(done[t - 1], done[t]))
                ob_ref[rows[b], :] = o * jax.nn.silu(bg_ref[rows[b], :])
                o_ref[0, rows[b], :] = o
                rden_ref[0, rows[b], :] = rden

        pieces = range(latent + 1)
        for n in pieces:
            score_piece(0, n)
        for t in range(len(tiles)):
            matmuls = []
            for n in pieces:
                if t + 1 < len(tiles):
                    matmuls.append(functools.partial(score_piece, t + 1, n))
                if t > 0:
                    matmuls.append(functools.partial(value_piece, t - 1, n))
            _emit_interleaved([functools.partial(softmax_rows, t, r) for r in range(QBLK // SOFTMAX_ROWS)], matmuls)
        for n in pieces:
            value_piece(len(tiles) - 1, n)

    qblk = pl.BlockSpec((ATTN_ROWS, 128), lambda p, i: (i, p))
    return pl.pallas_call(
        kern, name="attn_fwd", grid=(NPAIR, ATTN_STEPS),
        in_specs=_attn_in_specs() + [_rpb_spec(), _row(128), _row(128)],
        out_specs=[qblk, _pair_major_spec(), _pair_major_spec()] + _prob_specs() + _normed_key_specs(),
        out_shape=[jax.ShapeDtypeStruct((SEQ, 512), F32)] + [jax.ShapeDtypeStruct((NPAIR, SEQ, 128), F32)] * 2
        + [jax.ShapeDtypeStruct((HEADS, SEQ, KBLK), BF16), jax.ShapeDtypeStruct((HEADS, SEQ, CTX), BF16),
           jax.ShapeDtypeStruct((NPAIR, SEQ, 128), BF16), jax.ShapeDtypeStruct((NPAIR, CTX, 128), BF16)],
        scratch_shapes=[pltpu.VMEM((SEQ, 128), BF16), pltpu.VMEM((CTX, 128), BF16),
                        pltpu.VMEM((2, SEQ, 128), BF16), pltpu.VMEM((2, CTX, 128), BF16),
                        pltpu.VMEM((TILE_BUFFERS, QBLK, KBLK + CTX), F32),
                        pltpu.VMEM((3, 2, QBLK, KBLK), F32)],
        compiler_params=_cparams(("arbitrary", "arbitrary"), VMEM_BIG),
    )(z, z, z, z, zc, zc, rpb2, qg2, kg2)


def attn_bwd(z, zc, qg2, kg2, dcat, saved):
    def kern(q_ref, k_ref, v_ref, bg_ref, ck_ref, cv_ref, qg_ref, kg_ref, do_ref, o_ref, rden_ref, pl_ref, pc_ref,
             kn_scr, ckn_scr, dq_ref, dk_ref, dv_ref, dbg_ref, dck_ref, dcv_ref, drpb_ref, dqg_ref, dkg_ref,
             v_scr, cv_scr, dknt_scr, dvt_scr, dcknt_scr, dcvt_scr, dp_scr, ds_scr, db_ref):
        p, i = pl.program_id(0), pl.program_id(1)
        last = i == ATTN_STEPS - 1

        @pl.when(i == 0)
        def _():
            def body(c, carry):
                sl = pl.ds(pl.multiple_of(c * NORM_ROWS, NORM_ROWS), NORM_ROWS)
                v_scr[sl, :] = v_ref[sl, :].astype(BF16)
                return carry

            lax.fori_loop(0, SEQ // NORM_ROWS, body, 0)
            cv_scr[...] = cv_ref[...].astype(BF16)
            for acc in (dknt_scr, dvt_scr, dcknt_scr, dcvt_scr, db_ref):
                acc[...] = jnp.zeros_like(acc)

        @pl.when((i == 0) & (p == 0))
        def _():
            dqg_ref[...] = jnp.zeros_like(dqg_ref)
            dkg_ref[...] = jnp.zeros_like(dkg_ref)

        heads = _head_lanes()
        tiles = [(b, a) for b in range(ATTN_BLOCKS) for a in range(2)]
        rows = [slice(b * QBLK, (b + 1) * QBLK) for b in range(ATTN_BLOCKS)]
        kb = [_kblock(ATTN_BLOCKS * i + b) for b in range(ATTN_BLOCKS)]
        variant = [_bias_variant(i, b) for b in range(ATTN_BLOCKS)]
        latent = KBLK // KCOLS
        buf = lambda t: t % TILE_BUFFERS

        def keys(b, n):
            return pl.ds(pl.multiple_of((kb[b] + n) * KCOLS, KCOLS), KCOLS)

        gated = {}

        def gate_backward(b):
            bg, dout, o = bg_ref[rows[b], :], do_ref[rows[b], :], o_ref[0, rows[b], :]
            sig = jax.nn.sigmoid(bg)
            do = dout * (bg * sig)
            dbg_ref[rows[b], :] = (dout * o * (sig * (1.0 + bg * (1.0 - sig)))).astype(BF16)
            rden = rden_ref[0, rows[b], :]
            dr = do * rden
            qn = _scaled_q(q_ref[rows[b], :], qg_ref[...])
            gated[b] = (dr, dr.T.astype(BF16), qn.T.astype(BF16), do * o * rden)

        feats = [slice(a * HDIM, (a + 1) * HDIM) for a in range(2)]
        doa, doa_t, qa_t, delta = {}, {}, {}, {}
        dqn = [None] * len(tiles)

        def cols(n):
            return slice(n * KCOLS, (n + 1) * KCOLS)

        def stage_a(t, n):
            b, a = tiles[t]
            if n == 0:
                if a == 0:
                    gate_backward(b)
                dr, dr_t, qn_t, weighted = gated[b]
                doa[t] = jnp.where(heads[a], dr, 0.0).astype(BF16)
                doa_t[t] = dr_t[feats[a], :]
                qa_t[t] = qn_t[feats[a], :]
                delta[t] = jnp.sum(jnp.where(heads[a], weighted, 0.0), axis=-1, keepdims=True)
            if n < latent:
                dp_scr[buf(t), :, cols(n)] = mm_nt(doa[t], v_scr[keys(b, n), :])
                dvt_scr[kb[b] + n, feats[a], :] += mm(doa_t[t], pl_ref[a, rows[b], cols(n)])
            else:
                dp_scr[buf(t), :, cols(n)] = mm_nt(doa[t], cv_scr[...])
                dcvt_scr[feats[a], :] += mm(doa_t[t], pc_ref[a, rows[b], :])

        def stage_b(t, r):
            b, a = tiles[t]
            rs = slice(r * SOFTMAX_ROWS, (r + 1) * SOFTMAX_ROWS)
            in_rows = slice(b * QBLK + rs.start, b * QBLK + rs.stop)
            d = dp_scr[buf(t), rs, :] - delta[t][rs, :]
            ds_lat = pl_ref[a, in_rows, :].astype(F32) * d[:, :KBLK]
            ds_ctx = pc_ref[a, in_rows, :].astype(F32) * d[:, KBLK:]
            db_ref[variant[b], a, rs, :] += ds_lat
            ds_scr[buf(t), rs, :KBLK] = ds_lat.astype(BF16)
            ds_scr[buf(t), rs, KBLK:] = ds_ctx.astype(BF16)

        def stage_c(t, n):
            b, a = tiles[t]
            ds = ds_scr[buf(t), :, cols(n)]
            if n < latent:
                part = mm(ds, kn_scr[keys(b, n), :])
                dknt_scr[kb[b] + n, feats[a], :] += mm(qa_t[t], ds)
            else:
                part = mm(ds, ckn_scr[...])
                dcknt_scr[feats[a], :] += mm(qa_t[t], ds)
            dqn[t] = part if dqn[t] is None else dqn[t] + part
            if n == latent and a == 1:
                both = jnp.where(heads[0], dqn[t - 1], 0.0) + jnp.where(heads[1], dqn[t], 0.0)
                dq, dqg = jax.vjp(_scaled_q, q_ref[rows[b], :], qg_ref[...])[1](both)
                dq_ref[rows[b], :] = dq.astype(BF16)
                dqg_ref[...] += dqg

        pieces = range(latent + 1)
        for n in pieces:
            stage_a(0, n)
        for t in range(len(tiles)):
            matmuls = []
            for n in pieces:
                if t + 1 < len(tiles):
                    matmuls.append(functools.partial(stage_a, t + 1, n))
                if t > 0:
                    matmuls.append(functools.partial(stage_c, t - 1, n))
            _emit_interleaved([functools.partial(stage_b, t, r) for r in range(QBLK // SOFTMAX_ROWS)], matmuls)
        for n in pieces:
            stage_c(len(tiles) - 1, n)

        @pl.when(last)
        def _():
            eye = (lax.broadcasted_iota(jnp.int32, (KCOLS, KCOLS), 0)
                   == lax.broadcasted_iota(jnp.int32, (KCOLS, KCOLS), 1)).astype(BF16)

            def turned(x):
                hi = x.astype(BF16)
                return mm_nt(eye, hi) + mm_nt(eye, x - hi.astype(F32))

            def body(c, dkg):
                sl = pl.ds(pl.multiple_of(c * NORM_ROWS, NORM_ROWS), NORM_ROWS)
                blocks = range(NORM_ROWS // KCOLS)
                dkn = jnp.concatenate([turned(dknt_scr[c * len(blocks) + n]) for n in blocks], axis=0)
                dv = jnp.concatenate([mm_nt(eye, dvt_scr[c * len(blocks) + n]) for n in blocks], axis=0)
                dk, dg = _pair_rms_bwd(k_ref[sl, :], kg_ref[...], dkn)
                dk_ref[sl, :] = dk.astype(BF16)
                dv_ref[sl, :] = dv.astype(BF16)
                return dkg + dg

            dkg = lax.fori_loop(0, SEQ // NORM_ROWS, body, jnp.zeros((1, 128), F32), unroll=2)
            dck, dg = _pair_rms_bwd(ck_ref[...], kg_ref[...], dcknt_scr[...].T)
            dck_ref[...] = dck
            dcv_ref[...] = dcvt_scr[...].T
            dkg_ref[...] += dkg + dg
            for a in range(2):
                rows_of_rpb = _rpb_grad(lambda v, tile_rows, tile_cols, a=a: db_ref[v, a, tile_rows, tile_cols])
                for d, row in enumerate(rows_of_rpb):
                    drpb_ref[a, d:d + 1, :] = row

        @pl.when(last & (p == NPAIR - 1))
        def _():
            dqg_ref[...] = dqg_ref[...] + pltpu.roll(dqg_ref[...], HDIM, 1)
            dkg_ref[...] = dkg_ref[...] + pltpu.roll(dkg_ref[...], HDIM, 1)

    blk = lambda rows: pl.BlockSpec((rows, 128), lambda p, i: (0, p))
    qblk = pl.BlockSpec((ATTN_ROWS, 128), lambda p, i: (i, p))
    return pl.pallas_call(
        kern, name="attn_bwd", grid=(NPAIR, ATTN_STEPS),
        in_specs=_attn_in_specs() + [_row(128), _row(128), pl.BlockSpec((ATTN_ROWS, 128), lambda p, i: (i, 4 + p)),
                                     _pair_major_spec(), _pair_major_spec()] + _prob_specs() + _normed_key_specs(),
        out_specs=[qblk, blk(SEQ), blk(SEQ), qblk, blk(CTX), blk(CTX), _rpb_spec(), _row(128), _row(128)],
        out_shape=[jax.ShapeDtypeStruct((SEQ, 512), BF16)] * 4 + [jax.ShapeDtypeStruct((CTX, 512), F32)] * 2
        + [jax.ShapeDtypeStruct((HEADS, 15, 128), F32)]
        + [jax.ShapeDtypeStruct((1, 128), F32), jax.ShapeDtypeStruct((1, 128), F32)],
        scratch_shapes=[pltpu.VMEM((SEQ, 128), BF16), pltpu.VMEM((CTX, 128), BF16),
                        pltpu.VMEM((SEQ // KCOLS, 128, KCOLS), F32), pltpu.VMEM((SEQ // KCOLS, 128, KCOLS), F32),
                        pltpu.VMEM((128, CTX), F32), pltpu.VMEM((128, CTX), F32),
                        pltpu.VMEM((TILE_BUFFERS, QBLK, KBLK + CTX), F32),
                        pltpu.VMEM((TILE_BUFFERS, QBLK, KBLK + CTX), BF16),
                        pltpu.VMEM((3, 2, QBLK, KBLK), F32)],
        compiler_params=_cparams(("arbitrary", "arbitrary"), VMEM_BIG),
    )(z, z, z, z, zc, zc, qg2, kg2, dcat, *saved)


def outproj(z, sg, ws, bsb, out_b, x, target, gate, wo):
    tl = SGU_CHUNK * SGU_PER_STEP
    nt = SEQ // tl

    def kern(au0_ref, av0_ref, ag0_ref, au1_ref, av1_ref, ag1_ref, sg_ref, ws_ref, bs_ref, b_ref, x_ref, t_ref, g_ref,
             w_ref, loss_ref, dy_ref, dcat_ref, dg_ref, dw_ref, a_scr):
        t = pl.program_id(0)
        cur, nxt = lax.rem(t, 2), lax.rem(t + 1, 2)

        def gating(refs, slot, cn):
            sl = slice(cn * SGU_CHUNK, (cn + 1) * SGU_CHUNK)
            au_ref, av_ref, ag_ref = refs
            a_scr[slot, sl, :] = _sgu_chunk(au_ref[sl, :], av_ref[sl, :], ag_ref[sl, :], sg_ref[...], ws_ref[...],
                                            bs_ref[...]).astype(BF16)

        @pl.when(t == 0)
        def _():
            loss_ref[...] = jnp.zeros_like(loss_ref)
            dg_ref[...] = jnp.zeros_like(dg_ref)
            dw_ref[...] = jnp.zeros_like(dw_ref)
            for cn in range(SGU_PER_STEP):
                gating((au0_ref, av0_ref, ag0_ref), 0, cn)

        a, b = a_scr[cur], b_ref[...].astype(BF16)
        mix = (jnp.dot(a, w_ref[0:512, :], preferred_element_type=F32)
               + jnp.dot(b, w_ref[512:1024, :], preferred_element_type=F32))
        err = x_ref[...] + g_ref[...] * mix - t_ref[...]
        loss_ref[...] += 0.5 * jnp.sum(jnp.mean(err * err, axis=-1))
        dy = err * (1.0 / DM)
        dy_ref[...] = dy
        dg_ref[...] += jnp.sum(dy * mix, axis=0, keepdims=True)
        dmix = (g_ref[...] * dy).astype(BF16)

        def dcat_half(n):
            part = slice(512 * n, 512 * (n + 1))
            dcat_ref[:, part] = lax.dot_general(dmix, w_ref[part, :], _NT, preferred_element_type=F32)

        def dw_half(n, src):
            dw_ref[512 * n:512 * (n + 1), :] += lax.dot_general(src, dmix, (((0,), (0,)), ((), ())),
                                                                preferred_element_type=F32)

        _emit_interleaved([functools.partial(gating, (au1_ref, av1_ref, ag1_ref), nxt, cn) for cn in range(SGU_PER_STEP)],
                          [functools.partial(dcat_half, 0), functools.partial(dcat_half, 1),
                           functools.partial(dw_half, 0, a), functools.partial(dw_half, 1, b)])

    tile = lambda w: pl.BlockSpec((tl, w), lambda t: (t, 0))
    whole = pl.BlockSpec((DM, DM), lambda t: (0, 0))
    zfirst = [pl.BlockSpec((tl, 512), functools.partial(lambda c, t: (0, c), c)) for c in range(3)]
    znext = [pl.BlockSpec((tl, 512), functools.partial(lambda c, t: (jnp.minimum(t + 1, nt - 1), c), c))
             for c in range(3)]
    wspec = pl.BlockSpec((4, 128, 128), lambda t: (0, 0, 0))
    return pl.pallas_call(
        kern, name="outproj", grid=(nt,),
        in_specs=zfirst + znext + [_row(512), wspec, wspec, tile(512), tile(DM), tile(DM), _row(DM), whole],
        out_specs=[pl.BlockSpec((8, 128), lambda t: (0, 0)), tile(DM), tile(DM), _row(DM), whole],
        out_shape=[jax.ShapeDtypeStruct((8, 128), F32), jax.ShapeDtypeStruct((SEQ, DM), F32),
                   jax.ShapeDtypeStruct((SEQ, DM), F32), jax.ShapeDtypeStruct((1, DM), F32),
                   jax.ShapeDtypeStruct((DM, DM), F32)],
        scratch_shapes=[pltpu.VMEM((2, tl, 512), BF16)],
        compiler_params=_cparams(("arbitrary",), 48 * 1024 * 1024),
    )(z, z, z, z, z, z, sg, ws, bsb, out_b, x, target, gate, wo)


DZ_COLS = (("a", 0, 1536), ("q", 1536, 2048), ("k", 2048, 2560), ("v", 2560, 3072), ("g", 3072, DIN))
DZC_COLS = (("k", 2048, 2560), ("v", 2560, 3072))
_NT = (((1,), (1,)), ((), ()))


DH_SUBTILES = 2


def _dz_specs(tl):
    return [pl.BlockSpec((tl, 1536), lambda t: (t, 0))] + [pl.BlockSpec((tl, 512), lambda t: (t, 0))] * 4


def dh_bwd(dz_parts, w_full, x, dy, shift, scale, norm_g, dg_ctx):
    tl = 512
    nt = SEQ // tl

    def kern(a_ref, q_ref, k_ref, v_ref, g_ref, w_ref, x_ref, dy_ref, sh_ref, sc_ref, gn_ref, dgc_ref,
             gx_ref, dsh_ref, dsc_ref, dg_ref):
        @pl.when(pl.program_id(0) == 0)
        def _():
            dsh_ref[...] = jnp.zeros_like(dsh_ref)
            dsc_ref[...] = jnp.zeros_like(dsc_ref)
            dg_ref[...] = dgc_ref[...]

        src = dict(a=a_ref, q=q_ref, k=k_ref, v=v_ref, g=g_ref)
        for sub in range(DH_SUBTILES):
            rows = slice(sub * tl // DH_SUBTILES, (sub + 1) * tl // DH_SUBTILES)
            dh = None
            for name, c0, c1 in DZ_COLS:
                part = lax.dot_general(src[name][rows, :], w_ref[:, c0:c1], _NT, preferred_element_type=F32)
                dh = part if dh is None else dh + part
            _, vjp = jax.vjp(_modulated, x_ref[rows, :], gn_ref[...], sc_ref[...], sh_ref[...])
            dx, dg, dsc, dsh = vjp(dh)
            gx_ref[rows, :] = dy_ref[rows, :] + dx
            dg_ref[...] += dg
            dsc_ref[...] += dsc
            dsh_ref[...] += dsh

    tile = pl.BlockSpec((tl, DM), lambda t: (t, 0))
    return pl.pallas_call(
        kern, name="dh_bwd", grid=(nt,),
        in_specs=_dz_specs(tl) + [pl.BlockSpec((DM, DIN), lambda t: (0, 0)), tile, tile, _row(DM),
                                  _row(DM), _row(DM), _row(DM)],
        out_specs=[tile, _row(DM), _row(DM), _row(DM)],
        out_shape=[jax.ShapeDtypeStruct((SEQ, DM), F32)] + [jax.ShapeDtypeStruct((1, DM), F32)] * 3,
        compiler_params=_cparams(("arbitrary",), 48 * 1024 * 1024),
    )(*dz_parts, w_full, x, dy, shift, scale, norm_g, dg_ctx)


def dw_bwd(h, z, sg, ws, bsb, dcat, dz_attn, hc, dck, dcv, g_out):
    tl = SGU_CHUNK * SGU_PER_STEP
    nt = SEQ // tl
    (rhi, wi), (rho, wo) = RS_SHAPES

    def kern(h_ref, au_ref, av_ref, ag_ref, sg_ref, ws_ref, bs_ref, do_ref, q_ref, k_ref, v_ref, g_ref,
             hc_ref, dck_ref, dcv_ref, go_hbm,
             wire_i, keep_i, wire_o, keep_o, a_ref, dsg_ref, dws_ref, dbs_ref,
             acc, rcv_i, mine_o, rcv_o, load_sem, send_sems, recv_sems):
        t = pl.program_id(0)
        x, y, c = _me()
        k = 2 * x + y
        sib = _flip(1)
        half = lambda hh, rh: pl.ds(pl.multiple_of(hh * rh, rh), rh)
        load_o = pltpu.make_async_copy(go_hbm.at[:, half(c, rho), :], mine_o, load_sem)
        pair_o = _rcopy(go_hbm.at[:, half(1 - c, rho), :], rcv_o, send_sems, recv_sems, 0, sib)
        pair_i = [_rcopy(wire_i.at[j], rcv_i.at[j], send_sems, recv_sems, 1 + j, sib) for j in range(NCHIP)]

        @pl.when(t == 0)
        def _():
            load_o.start()
            pair_o.start()
            acc[...] = jnp.zeros_like(acc)
            dsg_ref[...] = jnp.zeros_like(dsg_ref)
            dws_ref[...] = jnp.zeros_like(dws_ref)
            dbs_ref[...] = jnp.zeros_like(dbs_ref)
            hct = hc_ref[...].T
            csrc = dict(k=dck_ref, v=dcv_ref)
            for name, c0, c1 in DZC_COLS:
                acc[:, c0:c1] += jnp.dot(hct, csrc[name][...].astype(BF16), preferred_element_type=F32)

        ht = h_ref[...].T
        src = dict(a=a_ref, q=q_ref, k=k_ref, v=v_ref, g=g_ref)

        def gating_backward(cn):
            sl = slice(cn * SGU_CHUNK, (cn + 1) * SGU_CHUNK)
            _, vjp = jax.vjp(_sgu_chunk, au_ref[sl, :], av_ref[sl, :], ag_ref[sl, :], sg_ref[...], ws_ref[...],
                             bs_ref[...])
            dau, dav, dag, dsg, dws, dbs = vjp(do_ref[sl, :])
            a_ref[sl, 0:512] = dau.astype(BF16)
            a_ref[sl, 512:1024] = dav.astype(BF16)
            a_ref[sl, 1024:1536] = dag.astype(BF16)
            dsg_ref[...] += dsg
            dws_ref[...] += dws
            dbs_ref[...] += dbs

        def product(name, c0, c1):
            acc[:, c0:c1] += jnp.dot(ht, src[name][...], preferred_element_type=F32)

        _emit_interleaved([functools.partial(gating_backward, cn) for cn in range(SGU_PER_STEP)],
                          [functools.partial(product, *cols) for cols in DZ_COLS[1:]])
        product(*DZ_COLS[0])

        @pl.when(t == nt - 1)
        def _():
            dbs_ref[...] = jnp.broadcast_to(jnp.sum(dbs_ref[...], axis=-1, keepdims=True), dbs_ref.shape)
            shard = lambda j: slice(j * SHARD_IN, (j + 1) * SHARD_IN)
            for j in range(NCHIP):
                wire_i[j] = acc[half(1 - c, rhi), shard(j)].astype(BF16)
                pair_i[j].start()
            load_o.wait()
            pair_o.wait_recv()
            for j in range(NCHIP):
                wire_o[j] = (mine_o[j] + rcv_o[j]).astype(BF16)
            keep_o[...] = mine_o[k] + rcv_o[k]
            mine = half(c, rhi)
            for j in range(NCHIP):
                pair_i[j].wait_recv()
                pair_i[j].wait_send()
                pair_sum = acc[mine, shard(j)] + rcv_i[j].astype(F32)
                wire_i[j] = pair_sum.astype(BF16)

                @pl.when(k == j)
                def _():
                    keep_i[...] = pair_sum
            pair_o.wait_send()

    whole = lambda *shape: pl.BlockSpec(shape, lambda t: (0,) * len(shape))
    rows, sgu_specs = _sgu_specs()
    assert rows == tl
    a_spec, *attn_specs = _dz_specs(tl)
    return pl.pallas_call(
        kern, name="dw_bwd", grid=(nt,),
        in_specs=[pl.BlockSpec((tl, DM), lambda t: (t, 0))] + sgu_specs + [pl.BlockSpec((tl, 512), lambda t: (t, 0))]
        + attn_specs + [whole(CTX, DM), whole(CTX, 512), whole(CTX, 512), pl.BlockSpec(memory_space=pl.ANY)],
        out_specs=[whole(NCHIP, rhi, wi), whole(rhi, wi), whole(NCHIP, rho, wo), whole(rho, wo),
                   a_spec, _row(512), whole(4, 128, 128), whole(4, 128, 128)],
        out_shape=[jax.ShapeDtypeStruct((NCHIP, rhi, wi), BF16), jax.ShapeDtypeStruct((rhi, wi), F32),
                   jax.ShapeDtypeStruct((NCHIP, rho, wo), BF16), jax.ShapeDtypeStruct((rho, wo), F32),
                   jax.ShapeDtypeStruct((SEQ, 1536), BF16), jax.ShapeDtypeStruct((1, 512), F32),
                   jax.ShapeDtypeStruct((4, 128, 128), F32), jax.ShapeDtypeStruct((4, 128, 128), F32)],
        scratch_shapes=[pltpu.VMEM((DM, DIN), F32), pltpu.VMEM((NCHIP, rhi, wi), BF16),
                        pltpu.VMEM((NCHIP, rho, wo), F32), pltpu.VMEM((NCHIP, rho, wo), F32),
                        pltpu.SemaphoreType.DMA(()), pltpu.SemaphoreType.DMA((1 + NCHIP,)),
                        pltpu.SemaphoreType.DMA((1 + NCHIP,))],
        compiler_params=_cparams(("arbitrary",), 60 * 1024 * 1024),
    )(h, z, z, z, sg, ws, bsb, dcat, *dz_attn, hc, dck, dcv, g_out)


def ctx_bwd(dck, dcv, w_full, ctx, cshift, cscale, norm_g):
    def kern(dck_ref, dcv_ref, w_ref, c_ref, sh_ref, sc_ref, g_ref, dsh_ref, dsc_ref, dg_ref):
        csrc = dict(k=dck_ref, v=dcv_ref)
        dhc = None
        first = DZC_COLS[0][1]
        for name, c0, c1 in DZC_COLS:
            part = lax.dot_general(csrc[name][...].astype(BF16), w_ref[:, c0 - first:c1 - first], _NT,
                                   preferred_element_type=F32)
            dhc = part if dhc is None else dhc + part
        _, vjp = jax.vjp(lambda g, sc, sh: _modulated(c_ref[...], g, sc, sh), g_ref[...], sc_ref[...], sh_ref[...])
        dg_ref[...], dsc_ref[...], dsh_ref[...] = vjp(dhc)

    whole = lambda r, c: pl.BlockSpec((r, c), lambda i: (0, 0))
    return pl.pallas_call(
        kern, name="ctx_bwd", grid=(1,),
        in_specs=[whole(CTX, 512), whole(CTX, 512), pl.BlockSpec((DM, 1024), lambda i: (0, DZC_COLS[0][1] // 1024)),
                  whole(CTX, DM), _row(DM), _row(DM), _row(DM)],
        out_specs=[_row(DM), _row(DM), _row(DM)],
        out_shape=[jax.ShapeDtypeStruct((1, DM), F32)] * 3,
        compiler_params=_cparams(("arbitrary",), 40 * 1024 * 1024),
    )(dck, dcv, w_full, ctx, cshift, cscale, norm_g)


def _lane_pad_rpb(rpb):
    r = jnp.pad(rpb, ((0, 0), (0, 0), (0, GRID_W - rpb.shape[-1])))
    return jnp.concatenate([r, r], axis=-1)


def local_step(chip, dev, x, c_vec, c_ctx, w_ada, b_shard, ctx, target, norm_g, sgu_g, w_s, b_s, q_g, k_g, rpb,
               w_in_shard, w_out_shard):
    bsb = jnp.broadcast_to(b_s[:, :, None], (4, 128, 128))
    qg2, kg2 = jnp.tile(q_g, (1, 2)), jnp.tile(k_g, (1, 2))

    z, h, w_in_full, w_out_full, mod_all, cs = inproj_fwd(chip, x, c_vec, c_ctx, w_ada, b_shard, norm_g, w_in_shard,
                                                          w_out_shard)
    mods = mod_all.transpose(1, 0, 2).reshape(CS_ROWS, 3 * DM)
    mod = lax.dynamic_slice(mods, (8 * dev, 0), (1, 3 * DM))
    shift, scale, gate = mod[:, :DM], mod[:, DM:2 * DM], mod[:, 2 * DM:]
    cshift, cscale = mods[8 * NDEV:8 * NDEV + 1, :DM], mods[8 * NDEV:8 * NDEV + 1, DM:2 * DM]
    zc, hc = ctx_fwd(ctx, cshift, cscale, norm_g, w_in_full)
    out_b, *saved = attn_fwd(z, zc, _lane_pad_rpb(rpb), qg2, kg2)
    loss8, dy, dcat, dgate, dwo = outproj(z, sgu_g, w_s, bsb, out_b, x, target, gate, w_out_full.reshape(DM, DM))
    dq, dk, dv, dbg, dck, dcv, drpb, dqg2, dkg2 = attn_bwd(z, zc, qg2, kg2, dcat, saved)
    drpb = drpb[:, :, :rpb.shape[-1]]
    dcshift, dcscale, dng_c = ctx_bwd(dck, dcv, w_in_full, ctx, cshift, cscale, norm_g)
    wire_i, keep_i, wire_o, keep_o, dz_a, dsg, dws, dbsb = dw_bwd(
        h, z, sgu_g, w_s, bsb, dcat, (dq, dk, dv, dbg), hc, dck, dcv, dwo.reshape(NCHIP, SHARD_OUT, DM))
    dz_parts = (dz_a, dq, dk, dv, dbg)
    *in_flight, token = rs_start(wire_i, wire_o)
    grad_x, dshift, dscale, dng = dh_bwd(dz_parts, w_in_full, x, dy, shift, scale, norm_g, dng_c + token[0, 0])
    got_i, got_o = rs_wait(*in_flight, dshift)
    return dict(
        loss=loss8[0:1, 0:1], grad_x=grad_x, rs=(keep_i, got_i, keep_o, got_o), cs=cs,
        dmod=jnp.concatenate([dshift, dscale, dgate], axis=-1),
        dcmod=jnp.concatenate([dcshift, dcscale, jnp.zeros((1, DM), F32)], axis=-1),
        d_norm_g=dng, d_sgu_g=dsg, d_w_s=dws, d_b_s=dbsb[:, :, 0],
        d_q_g=dqg2[:, :HDIM], d_k_g=dkg2[:, :HDIM], d_rpb=drpb)


def _me():
    return lax.axis_index("x"), lax.axis_index("y"), lax.axis_index("c")


def _flip(q):
    x, y, c = _me()
    return ((1 - x) if q & 4 else x, (1 - y) if q & 2 else y, (1 - c) if q & 1 else c)


def _chip_of(dev):
    return 2 * dev[0] + dev[1]


def _rcopy(src, dst, send_sems, recv_sems, k, dev):
    return pltpu.make_async_remote_copy(src_ref=src, dst_ref=dst, send_sem=send_sems.at[k], recv_sem=recv_sems.at[k],
                                        device_id=dev, device_id_type=MESH_ID)


_VMEM_SPEC = pl.BlockSpec(memory_space=pltpu.VMEM)
SLAB_ROWS = 80


RS_SHAPES = ((DM // 2, SHARD_IN), (SHARD_OUT // 2, DM))
_HBM_SPEC = pl.BlockSpec(memory_space=pltpu.HBM)
_SEM_SPEC = pl.BlockSpec(memory_space=pltpu.SEMAPHORE)
_IN_FLIGHT = pltpu.SideEffectType.DATAFLOW_SIDE_EFFECTING


def _rs_copies(wires, lands, send_sems, recv_sems):
    return [pltpu.make_async_remote_copy(
        src_ref=wires[n].at[_chip_of(_flip(q))], dst_ref=lands[n].at[q // 2 - 1],
        send_sem=send_sems.at[3 * n + q // 2 - 1], recv_sem=recv_sems.at[3 * n + q // 2 - 1],
        device_id=_flip(q), device_id_type=MESH_ID) for n in (0, 1) for q in (2, 4, 6)]


def rs_start(wire_i, wire_o):
    lands = [lax.empty((NCHIP - 1, rh, w), BF16) for rh, w in RS_SHAPES]

    def body(wi_ref, wo_ref, li_ref, lo_ref, send_sems, recv_sems, wi_thru, wo_thru, li_thru, lo_thru, token):
        for cp in _rs_copies((wi_ref, wo_ref), (li_ref, lo_ref), send_sems, recv_sems):
            cp.start()
        token[...] = jnp.zeros_like(token)

    hbm = lambda a: pltpu.HBM(a.shape, a.dtype)
    return pl.pallas_call(
        body, name="rs_start",
        out_shape=(pltpu.SemaphoreType.DMA((6,)), pltpu.SemaphoreType.DMA((6,)), hbm(wire_i), hbm(wire_o),
                   hbm(lands[0]), hbm(lands[1]), jax.ShapeDtypeStruct((8, 128), F32)),
        in_specs=(_HBM_SPEC,) * 4, out_specs=(_SEM_SPEC, _SEM_SPEC) + (_HBM_SPEC,) * 4 + (_VMEM_SPEC,),
        input_output_aliases={0: 2, 1: 3, 2: 4, 3: 5},
        compiler_params=pltpu.CompilerParams(has_side_effects=_IN_FLIGHT),
    )(*[pltpu.with_memory_space_constraint(a, pltpu.HBM) for a in (wire_i, wire_o, *lands)])


def rs_wait(send_sems, recv_sems, wire_i, wire_o, land_i, land_o, after):
    def body(wi_ref, wo_ref, li_ref, lo_ref, send_sems, recv_sems, after_ref, wi_dead, wo_dead, gi_ref, go_ref):
        for cp in _rs_copies((wi_ref, wo_ref), (li_ref, lo_ref), send_sems, recv_sems):
            cp.wait_send()
            cp.wait_recv()

    hbm = lambda a: pltpu.HBM(a.shape, a.dtype)
    return pl.pallas_call(
        body, name="rs_wait", out_shape=(hbm(wire_i), hbm(wire_o), hbm(land_i), hbm(land_o)),
        in_specs=(_HBM_SPEC,) * 4 + (_SEM_SPEC, _SEM_SPEC, pl.BlockSpec(memory_space=pl.ANY)),
        out_specs=(_HBM_SPEC,) * 4, input_output_aliases={0: 0, 1: 1, 2: 2, 3: 3},
        compiler_params=pltpu.CompilerParams(has_side_effects=_IN_FLIGHT),
    )(wire_i, wire_o, land_i, land_o, send_sems, recv_sems, after)[2:]


def final_reduce(keep_i, got_i, keep_o, got_o, slab, cs, w_ada, c_ctx):
    (rhi, wi), (rho, wo) = RS_SHAPES

    def kern(ki_hbm, gi_hbm, ko_hbm, go_hbm, s_ref, cs_ref, w_hbm, cc_ref,
             gin_ref, gout_ref, tot_ref, dw_ref, db_ref, dcc_ref,
             ki, gi, ko, go, w_scr, all_ref, dms_scr, parts, load_sems, send_sems, recv_sems):
        x, y, c = _me()
        k = 2 * x + y
        sib = _flip(1)
        dev = lambda d: 4 * d[0] + 2 * d[1] + d[2]
        me = dev((x, y, c))

        def slab_copy(idx, owner, to):
            return _rcopy(all_ref.at[dev(owner)], all_ref.at[dev(owner)], send_sems, recv_sems, idx, to)

        all_ref[me] = s_ref[...]
        first = [slab_copy(0, (x, y, c), sib)] + [slab_copy(q // 2, (x, y, c), _flip(q)) for q in (2, 4, 6)]
        for cp in first:
            cp.start()
        loads = [pltpu.make_async_copy(src, dst, load_sems.at[n]) for n, (src, dst) in enumerate(
            ((ki_hbm, ki), (gi_hbm, gi), (ko_hbm, ko), (go_hbm, go), (w_hbm, w_scr)))]
        for cp in loads:
            cp.start()

        shares = []
        for n, (keep, got, out) in enumerate(((ki, gi, gin_ref), (ko, go, gout_ref))):
            rh = RS_SHAPES[n][0]
            half = lambda hh, rh=rh: pl.ds(pl.multiple_of(hh * rh, rh), rh)
            loads[2 * n].wait()
            loads[2 * n + 1].wait()
            out[half(c), :] = ((keep[...] + got[0].astype(F32)) + got[1].astype(F32)) + got[2].astype(F32)
            share = _rcopy(out.at[half(c), :], out.at[half(c), :], send_sems, recv_sems, 7 + n, sib)
            share.start()
            shares.append((share, _rcopy(out.at[half(1 - c), :], out.at[half(1 - c), :], send_sems, recv_sems, 7 + n,
                                         sib)))

        passed = []
        for q in (2, 4, 6):
            slab_copy(q // 2, _flip(q), (x, y, c)).wait_recv()
            cp = slab_copy(3 + q // 2, _flip(q), sib)
            cp.start()
            passed.append(cp)
        slab_copy(0, sib, (x, y, c)).wait_recv()
        for q in (2, 4, 6):
            slab_copy(3 + q // 2, _flip(q | 1), (x, y, c)).wait_recv()
        tot = all_ref[0]
        for d in range(1, NDEV):
            tot = tot + all_ref[d]
        tot_ref[...] = tot

        pad = jnp.zeros((7, DM), F32)
        dm = [jnp.concatenate([all_ref[d, 12 + j:13 + j, :] for d in range(NDEV)] + [tot[9 + j:10 + j, :], pad], axis=0)
              for j in range(3)]
        db_ref[...] = jnp.concatenate([jnp.sum(part, axis=0, keepdims=True) for part in dm], axis=0)
        dm = jnp.concatenate(dm, axis=-1)
        for j in range(NCHIP):
            @pl.when(k == j)
            def _():
                dms_scr[...] = dm[:, j * SHARD_ADA:(j + 1) * SHARD_ADA].astype(BF16)

        a_in = jnp.concatenate([cs_ref[8 * d:8 * d + 1, :] for d in range(NDEV)]
                               + [cs_ref[8 * NDEV:8 * NDEV + 1, :], pad], axis=0)
        act = jax.nn.silu(a_in).astype(BF16)
        dms = dms_scr[...]
        dw_ref[...] = lax.dot_general(act, dms, (((0,), (0,)), ((), ())), preferred_element_type=F32)
        loads[4].wait()
        parts[k] = lax.dot_general(dms, w_scr[...].astype(BF16), (((1,), (1,)), ((), ())), preferred_element_type=F32)
        sends = [_rcopy(parts.at[k], parts.at[k], send_sems, recv_sems, 8 + q // 2, _flip(q)) for q in (2, 4, 6)]
        for cp in sends:
            cp.start()
        for q in (2, 4, 6):
            kq = _chip_of(_flip(q))
            _rcopy(parts.at[kq], parts.at[kq], send_sems, recv_sems, 8 + q // 2, _flip(q)).wait_recv()
        dact = ((parts[0] + parts[1]) + parts[2]) + parts[3]
        _, vjp = jax.vjp(jax.nn.silu, cc_ref[...])
        dcc_ref[...] = vjp(dact[8:9, :])[0]

        for share, arrival in shares:
            arrival.wait_recv()
            share.wait_send()
        for cp in first + passed + sends:
            cp.wait_send()

    any_spec = pl.BlockSpec(memory_space=pl.ANY)
    return pl.pallas_call(
        kern, name="final_reduce",
        in_specs=[any_spec] * 4 + [_VMEM_SPEC, _VMEM_SPEC, any_spec, _VMEM_SPEC], out_specs=[_VMEM_SPEC] * 6,
        out_shape=[jax.ShapeDtypeStruct((2 * rhi, wi), F32), jax.ShapeDtypeStruct((2 * rho, wo), F32),
                   jax.ShapeDtypeStruct((SLAB_ROWS, DM), F32), jax.ShapeDtypeStruct((DM, SHARD_ADA), F32),
                   jax.ShapeDtypeStruct((3, DM), F32), jax.ShapeDtypeStruct((1, DM), F32)],
        scratch_shapes=[pltpu.VMEM((rhi, wi), F32), pltpu.VMEM((NCHIP - 1, rhi, wi), BF16),
                        pltpu.VMEM((rho, wo), F32), pltpu.VMEM((NCHIP - 1, rho, wo), BF16),
                        pltpu.VMEM((DM, SHARD_ADA), F32), pltpu.VMEM((NDEV, SLAB_ROWS, DM), F32),
                        pltpu.VMEM((16, SHARD_ADA), BF16), pltpu.VMEM((NCHIP, 16, DM), F32),
                        pltpu.SemaphoreType.DMA((5,)), pltpu.SemaphoreType.DMA((12,)), pltpu.SemaphoreType.DMA((12,))],
        compiler_params=pltpu.CompilerParams(vmem_limit_bytes=40 * 1024 * 1024),
    )(keep_i, got_i, keep_o, got_o, slab, cs, w_ada, c_ctx)


def _adamw_math(w, g, m, v):
    m = B1 * m + (1.0 - B1) * g
    v = B2 * v + (1.0 - B2) * (g * g)
    m_hat = m / (1.0 - B1 ** STEP)
    v_hat = v / (1.0 - B2 ** STEP)
    return -LR * (m_hat / (jnp.sqrt(v_hat) + ADAM_EPS) + WD * w), m, v


def adamw_big(w, g, m, v, name, block_rows=256):
    rows, width = w.shape

    def kern(w_ref, g_ref, m_ref, v_ref, d_ref, nm_ref, nv_ref):
        d_ref[...], nm_ref[...], nv_ref[...] = _adamw_math(w_ref[...], g_ref[...], m_ref[...], v_ref[...])

    spec = pl.BlockSpec((block_rows, width), lambda i: (i, 0))
    return pl.pallas_call(
        kern, name=name, grid=(rows // block_rows,), in_specs=[spec] * 4, out_specs=[spec] * 3,
        out_shape=[jax.ShapeDtypeStruct((rows, width), F32)] * 3,
        compiler_params=_cparams(("arbitrary",)),
    )(w, g, m, v)


def adamw_small(quads):
    n = len(quads)

    def kern(*refs):
        ins, outs = refs[:4 * n], refs[4 * n:]
        for i in range(n):
            w, g, m, v = (r[...] for r in ins[4 * i:4 * i + 4])
            outs[3 * i][...], outs[3 * i + 1][...], outs[3 * i + 2][...] = _adamw_math(w, g, m, v)

    flat = [a for quad in quads for a in quad]
    res = pl.pallas_call(
        kern, name="adamw_small", in_specs=[_VMEM_SPEC] * (4 * n), out_specs=[_VMEM_SPEC] * (3 * n),
        out_shape=[jax.ShapeDtypeStruct(q[0].shape, F32) for q in quads for _ in range(3)],
    )(*flat)
    return [tuple(res[3 * i:3 * i + 3]) for i in range(n)]


def _rows_of(a, rows):
    flat = a.reshape(-1)
    return jnp.pad(flat, (0, rows * DM - flat.shape[0])).reshape(rows, DM)


def kernel(x, c, ctx, c_ctx, w_ada, b_ada, norm_g, w_in, sgu_norm_g, w_spatial, b_spatial, q_norm_g, k_norm_g, rpb, w_out, loss_target, m_c_ctx, m_w_ada, m_b_ada, m_norm_g, m_w_in, m_sgu_norm_g, m_w_spatial, m_b_spatial, m_q_norm_g, m_k_norm_g, m_rpb, m_w_out, v_c_ctx, v_w_ada, v_b_ada, v_norm_g, v_w_in, v_sgu_norm_g, v_w_spatial, v_b_spatial, v_q_norm_g, v_k_norm_g, v_rpb, v_w_out):
    xi, yi, ci = lax.axis_index("x"), lax.axis_index("y"), lax.axis_index("c")
    chip, dev = 2 * xi + yi, 4 * xi + 2 * yi + ci
    c_ctx2 = c_ctx.reshape(1, DM)

    b_shard = lax.dynamic_slice(b_ada, (0, chip * SHARD_ADA), (1, SHARD_ADA))
    part = local_step(chip.reshape(1).astype(jnp.int32), dev, x[0], c, c_ctx2, w_ada[0], b_shard, ctx[0], loss_target[0],
                      norm_g, sgu_norm_g, w_spatial[0], b_spatial[0], q_norm_g, k_norm_g, rpb[0], w_in[0], w_out[0])
    cs = part["cs"]

    slab = jnp.concatenate([
        part["d_norm_g"], _rows_of(part["d_sgu_g"], 1), _rows_of(part["d_b_s"], 1),
        _rows_of(jnp.concatenate([part["d_q_g"], part["d_k_g"]], axis=-1), 1), _rows_of(part["d_rpb"], 4),
        _rows_of(part["loss"], 1), _rows_of(part["dcmod"], 3), _rows_of(part["dmod"], 3), jnp.zeros((1, DM), F32),
        _rows_of(part["d_w_s"], 64)], axis=0)
    g_w_in, g_w_out, tot, g_w_ada, g_b_ada, g_c_ctx = final_reduce(*part["rs"], slab, cs, w_ada[0], c_ctx2)
    g_b_ada = g_b_ada.reshape(1, 3 * DM)

    loss = tot[8, 0]
    g_small = dict(
        c_ctx=g_c_ctx, b_ada=g_b_ada, norm_g=tot[0:1], sgu_norm_g=tot[1:2, :512], w_spatial=tot[16:80].reshape(512, 128),
        b_spatial=tot[2:3, :512].reshape(4, 128), q_norm_g=tot[3:4, :HDIM], k_norm_g=tot[3:4, HDIM:2 * HDIM],
        rpb=tot[4:8].reshape(-1)[:HEADS * 15 * 31].reshape(HEADS * 15, 31))
    shapes = dict(c_ctx=(DM,), w_ada=(1, DM, SHARD_ADA), b_ada=(1, 3 * DM), norm_g=(1, DM), w_in=(1, DM, SHARD_IN),
                  sgu_norm_g=(1, 512), w_spatial=(1, 4, 128, 128), b_spatial=(1, 4, 128), q_norm_g=(1, HDIM),
                  k_norm_g=(1, HDIM), rpb=(1, HEADS, 15, 31), w_out=(1, SHARD_OUT, DM))
    names = list(shapes)
    weights = dict(c_ctx=c_ctx, w_ada=w_ada, b_ada=b_ada, norm_g=norm_g, w_in=w_in, sgu_norm_g=sgu_norm_g,
                   w_spatial=w_spatial, b_spatial=b_spatial, q_norm_g=q_norm_g, k_norm_g=k_norm_g, rpb=rpb, w_out=w_out)
    m_in = dict(zip(names, (m_c_ctx, m_w_ada, m_b_ada, m_norm_g, m_w_in, m_sgu_norm_g, m_w_spatial, m_b_spatial,
                            m_q_norm_g, m_k_norm_g, m_rpb, m_w_out)))
    v_in = dict(zip(names, (v_c_ctx, v_w_ada, v_b_ada, v_norm_g, v_w_in, v_sgu_norm_g, v_w_spatial, v_b_spatial,
                            v_q_norm_g, v_k_norm_g, v_rpb, v_w_out)))
    grads = dict(g_small, w_ada=g_w_ada, w_in=g_w_in, w_out=g_w_out)
    upd = {}
    for n in ("w_ada", "w_in"):
        g = grads[n]
        upd[n] = adamw_big(weights[n].reshape(g.shape), g, m_in[n].reshape(g.shape), v_in[n].reshape(g.shape),
                           "adamw_" + n)
    small = [n for n in names if n not in upd]
    res = adamw_small([(weights[n].reshape(grads[n].shape), grads[n], m_in[n].reshape(grads[n].shape),
                        v_in[n].reshape(grads[n].shape)) for n in small])
    upd.update(zip(small, res))
    out = [loss, part["grad_x"].reshape(1, SEQ, DM)]
    out += [grads[n].reshape(shapes[n]) for n in names]
    for slot in range(3):
        out += [upd[n][slot].reshape(shapes[n]) for n in names]
    return tuple(out)
```

```python
import functools

import jax
import jax.numpy as jnp
from jax import lax
from jax.experimental import pallas as pl
from jax.experimental.pallas import tpu as pltpu

F32, BF16 = jnp.float32, jnp.bfloat16
SEQ, DM, CTX, DIN = 4096, 1024, 256, 3584
NCHIP, NDEV = 4, 8
SHARD_IN = DIN // NCHIP
SHARD_ADA = 3 * DM // NCHIP
SHARD_OUT = DM // NCHIP
GRID_W = 64
QROWS = 4
KROWS = 12
QBLK, KBLK = QROWS * GRID_W, KROWS * GRID_W
NQBLK = SEQ // QBLK
HEADS, HDIM, NPAIR = 8, 64, 4
EPS = 1e-6
NEG_INF = -1e30
ZQ, ZK, ZV, ZG = 12, 16, 20, 24
LR, B1, B2, ADAM_EPS, WD, STEP = 0.001, 0.9, 0.999, 1e-08, 0.01, 10
VMEM_BIG = 56 * 1024 * 1024
MESH_ID = pl.DeviceIdType.MESH


def _dot(a, b, lhs_c, rhs_c):
    return lax.dot_general(a.astype(BF16), b.astype(BF16), (((lhs_c,), (rhs_c,)), ((), ())),
                           preferred_element_type=F32)


@jax.custom_vjp
def mm(a, b):
    return _dot(a, b, 1, 0)


@jax.custom_vjp
def mm_nt(a, b):
    return _dot(a, b, 1, 1)


@jax.custom_vjp
def mm_tn(a, b):
    return _dot(a, b, 0, 0)


mm.defvjp(lambda a, b: (mm(a, b), (a, b)), lambda r, ct: (mm_nt(ct, r[1]), mm_tn(r[0], ct)))
mm_nt.defvjp(lambda a, b: (mm_nt(a, b), (a, b)), lambda r, ct: (mm(ct, r[1]), mm_tn(ct, r[0])))
mm_tn.defvjp(lambda a, b: (mm_tn(a, b), (a, b)), lambda r, ct: (mm_nt(r[1], ct), mm(r[0], ct)))


def _rms(x, g):
    return x * lax.rsqrt(jnp.mean(x * x, axis=-1, keepdims=True) + EPS) * g


def _modulated(x, g, scale, shift):
    return _rms(x, g) * (1.0 + scale) + shift


def _pair_rms(x, g2):
    lo = lax.broadcasted_iota(jnp.int32, (1, 2 * HDIM), 1) < HDIM
    sq = x * x
    s_lo = jnp.sum(jnp.where(lo, sq, 0.0), axis=-1, keepdims=True)
    s_hi = jnp.sum(jnp.where(lo, 0.0, sq), axis=-1, keepdims=True)
    rs = jnp.where(lo, lax.rsqrt(s_lo / HDIM + EPS), lax.rsqrt(s_hi / HDIM + EPS))
    return x * rs * g2


def _cparams(sem, vmem=None):
    return pltpu.CompilerParams(dimension_semantics=sem, vmem_limit_bytes=vmem)


def _row(n):
    return pl.BlockSpec((1, n), lambda *_: (0, 0))


CS_ROWS = 8 * NDEV + 8


def _mod_part(mod_ref, row, part):
    pieces = []
    for j in range(NCHIP):
        lo, hi = max(part * DM, j * SHARD_ADA), min((part + 1) * DM, (j + 1) * SHARD_ADA)
        if lo < hi:
            pieces.append(mod_ref[j, row, lo - j * SHARD_ADA:hi - j * SHARD_ADA])
    return jnp.concatenate(pieces, axis=-1)


def inproj_fwd(chip, x, c_vec, c_ctx, w_ada, b_shard, norm_g, w_shard, wo_shard):
    tl = 1024
    nt = SEQ // tl
    halves = (DM // 2, SHARD_OUT // 2)
    n_w, n_c = 12, NDEV - 1

    def kern(k_ref, x_ref, cv_ref, cc_ref, wa_ref, b_ref, g_ref, w_ref, wo_ref,
             z_ref, h_ref, wfull_ref, wofull_ref, modall_ref, csall_ref,
             w_scr, wo_scr, h_scr, mine, cs_scr, mod_scr, shsc_scr, send_sems, recv_sems, out_sems):
        s, t = pl.program_id(0), pl.program_id(1)
        xi, yi, c = _me()
        k, me = 2 * xi + yi, 4 * xi + 2 * yi + c
        sib = _flip(1)
        rows = pl.ds(pl.multiple_of(t * tl, tl), tl)
        gathered = (w_scr, wo_scr)
        slot = lambda d: pl.ds(pl.multiple_of(8 * d, 8), 8)

        def c_copy(q, owner):
            return _rcopy(mine, cs_scr.at[slot(owner), :], send_sems, recv_sems, n_w + q - 1, _flip(q))

        def m_copy(q, chip_of_block):
            return _rcopy(mod_scr.at[chip_of_block], mod_scr.at[chip_of_block], send_sems, recv_sems,
                          n_w + n_c + q // 2 - 1, _flip(q))

        def adaln():
            first = lax.broadcasted_iota(jnp.int32, (8, DM), 0) == 0
            mine[...] = jnp.where(first, jnp.broadcast_to(cv_ref[...], (8, DM)), 0.0)
            cs_scr[slot(me), :] = mine[...]
            cs_scr[slot(NDEV), :] = jnp.where(first, jnp.broadcast_to(cc_ref[...], (8, DM)), 0.0)
            for q in range(1, NDEV):
                c_copy(q, me).start()
            wa = wa_ref[...].astype(BF16)
            for q in range(1, NDEV):
                px, py, pc = _flip(q)
                c_copy(q, 4 * px + 2 * py + pc).wait_recv()
            act = jax.nn.silu(cs_scr[...]).astype(BF16)
            mod_scr[k] = jnp.dot(act, wa, preferred_element_type=F32) + b_ref[...]
            for q in (2, 4, 6):
                m_copy(q, k).start()
            for q in (2, 4, 6):
                m_copy(q, _chip_of(_flip(q))).wait_recv()
            row = pl.ds(8 * me, 1)
            shsc_scr[0:1, :] = _mod_part(mod_scr, row, 0)
            shsc_scr[1:2, :] = _mod_part(mod_scr, row, 1)
            pltpu.sync_copy(mod_scr, modall_ref)
            pltpu.sync_copy(cs_scr, csall_ref)

        def block(n, chip_of_block, hh):
            return gathered[n].at[chip_of_block, pl.ds(pl.multiple_of(hh * halves[n], halves[n]), halves[n]), :]

        def ici(n, q, chip_of_block):
            blk = block(n, chip_of_block, c)
            return _rcopy(blk, blk, send_sems, recv_sems, 6 * n + q // 2 - 1, _flip(q))

        def d2d(n, q, chip_of_block, hh):
            blk = block(n, chip_of_block, hh)
            return _rcopy(blk, blk, send_sems, recv_sems, 6 * n + 3 + q // 2 - 1, sib)

        @pl.when((s == 0) & (t == 0))
        def _():
            adaln()
            w_scr[k] = w_ref[...].astype(BF16)
            wo_scr[k] = wo_ref[...].astype(BF16)
            for q in (2, 4, 6):
                ici(0, q, k).start()
                ici(1, q, k).start()

        for sweep in (1, 2, 3):
            @pl.when((s == sweep) & (t == 0))
            def _():
                q = 2 * sweep
                src = _chip_of(_flip(q))
                for n in (0, 1):
                    ici(n, q, src).wait_recv()
                    d2d(n, q, src, c).start()
                for n in (0, 1):
                    d2d(n, q, src, 1 - c).wait_recv()

        @pl.when(s == 0)
        def _():
            hb = _modulated(x_ref[...], g_ref[...], shsc_scr[1:2, :], shsc_scr[0:1, :]).astype(BF16)
            h_scr[rows, :] = hb
            h_ref[...] = hb

        z_ref[...] = jnp.dot(h_scr[rows, :], w_scr[lax.bitwise_xor(k, s)], preferred_element_type=F32)

        @pl.when((s == NCHIP - 1) & (t == nt - 1))
        def _():
            for q in range(1, NDEV):
                c_copy(q, me).wait_send()
            for q in (2, 4, 6):
                m_copy(q, k).wait_send()
            for n in (0, 1):
                for q in (2, 4, 6):
                    ici(n, q, k).wait_send()
                    d2d(n, q, _chip_of(_flip(q)), c).wait_send()
            outs = [pltpu.make_async_copy(w_scr.at[j], wfull_ref.at[:, j * SHARD_IN:(j + 1) * SHARD_IN], out_sems.at[j])
                    for j in range(NCHIP)] + [pltpu.make_async_copy(wo_scr, wofull_ref, out_sems.at[NCHIP])]
            for cp in outs:
                cp.start()
            for cp in outs:
                cp.wait()

    once = lambda s, t, k: (jnp.where(s == 0, t, nt - 1), 0)
    hbm = pl.BlockSpec(memory_space=pl.ANY)
    n_sem = n_w + n_c + 3
    return pl.pallas_call(
        kern, name="inproj_fwd",
        grid_spec=pltpu.PrefetchScalarGridSpec(
            num_scalar_prefetch=1, grid=(NCHIP, nt),
            in_specs=[pl.BlockSpec((tl, DM), once)] + [_VMEM_SPEC] * 7,
            out_specs=[pl.BlockSpec((tl, SHARD_IN), lambda s, t, k: (t, lax.bitwise_xor(k[0], s))),
                       pl.BlockSpec((tl, DM), once), hbm, hbm, hbm, hbm],
            scratch_shapes=[pltpu.VMEM((NCHIP, DM, SHARD_IN), BF16), pltpu.VMEM((NCHIP, SHARD_OUT, DM), BF16),
                            pltpu.VMEM((SEQ, DM), BF16), pltpu.VMEM((8, DM), F32), pltpu.VMEM((CS_ROWS, DM), F32),
                            pltpu.VMEM((NCHIP, CS_ROWS, SHARD_ADA), F32), pltpu.VMEM((8, DM), F32),
                            pltpu.SemaphoreType.DMA((n_sem,)), pltpu.SemaphoreType.DMA((n_sem,)),
                            pltpu.SemaphoreType.DMA((NCHIP + 1,))]),
        out_shape=[jax.ShapeDtypeStruct((SEQ, DIN), F32), jax.ShapeDtypeStruct((SEQ, DM), BF16),
                   jax.ShapeDtypeStruct((DM, DIN), BF16), jax.ShapeDtypeStruct((NCHIP, SHARD_OUT, DM), BF16),
                   jax.ShapeDtypeStruct((NCHIP, CS_ROWS, SHARD_ADA), F32), jax.ShapeDtypeStruct((CS_ROWS, DM), F32)],
        compiler_params=_cparams(("arbitrary", "arbitrary"), VMEM_BIG),
    )(chip, x, c_vec, c_ctx, w_ada, b_shard, norm_g, w_shard, wo_shard)


def ctx_fwd(ctx, cshift, cscale, norm_g, w_full):
    def kern(c_ref, sh_ref, sc_ref, g_ref, w_ref, zc_ref, hc_ref):
        hc = _modulated(c_ref[...], g_ref[...], sc_ref[...], sh_ref[...]).astype(BF16)
        hc_ref[...] = hc
        zc_ref[...] = jnp.dot(hc, w_ref[...], preferred_element_type=F32)

    return pl.pallas_call(
        kern, name="ctx_fwd", grid=(1,),
        in_specs=[pl.BlockSpec((CTX, DM), lambda i: (0, 0)), _row(DM), _row(DM), _row(DM),
                  pl.BlockSpec((DM, 2 * SHARD_IN), lambda i: (0, 1))],
        out_specs=[pl.BlockSpec((CTX, 2 * SHARD_IN), lambda i: (0, 0)),
                   pl.BlockSpec((CTX, DM), lambda i: (0, 0))],
        out_shape=[jax.ShapeDtypeStruct((CTX, 2 * SHARD_IN), F32), jax.ShapeDtypeStruct((CTX, DM), BF16)],
        compiler_params=_cparams(("arbitrary",)),
    )(ctx, cshift, cscale, norm_g, w_full)


SGU_CHUNK, SGU_PER_STEP = 128, 4


def _gelu(x):
    return 0.5 * x * (1.0 + lax.erf(x * 0.7071067811865476))


def _sgu_chunk(au, av, ag, sg, ws, bsb):
    u, v = _gelu(au), _gelu(av)
    outs = []
    for g in range(4):
        sl = slice(128 * g, 128 * (g + 1))
        mixed = mm(ws[g], _rms(v[:, sl], sg[:, sl])) + bsb[g]
        outs.append(u[:, sl] * mixed * jax.nn.silu(ag[:, sl]))
    return jnp.concatenate(outs, axis=-1)


def _sgu_specs():
    rows = SGU_CHUNK * SGU_PER_STEP
    zspec = lambda c: pl.BlockSpec((rows, 512), lambda n: (n, c))
    wspec = pl.BlockSpec((4, 128, 128), lambda n: (0, 0, 0))
    return rows, [zspec(0), zspec(1), zspec(2), _row(512), wspec, wspec]


_DR_OFF = (7, 3, -1)


def _row_valid(v, rr, j):
    return (j < 8, rr <= j < rr + 8, 4 <= j < 12)[v]


def _col_window():
    q = lax.broadcasted_iota(jnp.int32, (GRID_W, 128), 0)
    kc = lax.broadcasted_iota(jnp.int32, (GRID_W, 128), 1) % GRID_W
    c0 = jnp.clip(q - 8, 0, GRID_W - 16)
    return (kc >= c0) & (kc < c0 + 16)


def _bias_tiles(base, store):
    lo = lax.broadcasted_iota(jnp.int32, (1, 128), 1) < GRID_W
    win = _col_window()
    tiles = {}
    for v in range(3):
        for rr in range(QROWS):
            for jp in range(KROWS // 2):
                j0, j1 = 2 * jp, 2 * jp + 1
                ok0, ok1 = _row_valid(v, rr, j0), _row_valid(v, rr, j1)
                key = (j0 - rr + _DR_OFF[v], ok0, ok1) if (ok0 or ok1) else None
                if key not in tiles:
                    if key is None:
                        tiles[key] = jnp.full((GRID_W, 128), NEG_INF, F32)
                    else:
                        d0 = key[0]
                        r0 = base[d0:d0 + 1, :] if ok0 else jnp.zeros((1, 128), F32)
                        r1 = base[d0 + 1:d0 + 2, :] if ok1 else jnp.zeros((1, 128), F32)
                        y = jnp.broadcast_to(jnp.where(lo, r0, r1), (GRID_W, 128))
                        y = pltpu.roll(pltpu.roll(y, 128 - 15, 1), 0, 1, stride=1, stride_axis=0)
                        tiles[key] = jnp.where(win & jnp.where(lo, ok0, ok1), y, NEG_INF)
                store(v, slice(rr * GRID_W, (rr + 1) * GRID_W), slice(jp * 128, (jp + 1) * 128), tiles[key])


def _rpb_grad(load):
    lo = lax.broadcasted_iota(jnp.int32, (1, 128), 1) < GRID_W
    ri = lax.broadcasted_iota(jnp.int32, (GRID_W, GRID_W), 0)
    ci = lax.broadcasted_iota(jnp.int32, (GRID_W, GRID_W), 1)
    flip = (ri + ci == GRID_W - 1).astype(F32)
    groups = {}
    for v in range(3):
        for rr in range(QROWS):
            for jp in range(KROWS // 2):
                j0, j1 = 2 * jp, 2 * jp + 1
                ok0, ok1 = _row_valid(v, rr, j0), _row_valid(v, rr, j1)
                if not (ok0 or ok1):
                    continue
                g = load(v, slice(rr * GRID_W, (rr + 1) * GRID_W), slice(jp * 128, (jp + 1) * 128))
                key = (j0 - rr + _DR_OFF[v], ok0, ok1)
                groups[key] = g if key not in groups else groups[key] + g
    acc = [jnp.zeros((1, 128), F32) for _ in range(15)]
    for (d0, ok0, ok1), g in groups.items():
        g = lax.dot_general(flip, g, (((1,), (0,)), ((), ())), precision=lax.Precision.HIGHEST,
                            preferred_element_type=F32)
        g = pltpu.roll(pltpu.roll(g, 128 - 48, 1), 0, 1, stride=1, stride_axis=0)
        s = jnp.sum(g, axis=0, keepdims=True)
        if ok0:
            acc[d0] = acc[d0] + jnp.where(lo, s, 0.0)
        if ok1:
            acc[d0 + 1] = acc[d0 + 1] + jnp.where(lo, 0.0, s)
    return [row + pltpu.roll(row, GRID_W, 1) for row in acc]


def _scaled_q(q_raw, qg):
    return _pair_rms(q_raw, qg) * (HDIM ** -0.5)


def _head_lanes():
    lo = lax.broadcasted_iota(jnp.int32, (1, 2 * HDIM), 1) < HDIM
    return lo, jnp.logical_not(lo)


SOFTMAX_ROWS = 32


def _emit_interleaved(vector_work, matmul_work):
    for j in range(max(len(vector_work), len(matmul_work))):
        for work in (vector_work, matmul_work):
            if j < len(work):
                work[j]()


def _kblock(i):
    return jnp.clip(i - 1, 0, (SEQ - KBLK) // QBLK)


def _kstart(i):
    return pl.multiple_of(_kblock(i) * QBLK, QBLK)


ATTN_BLOCKS = 4
TILE_BUFFERS = 4
ATTN_STEPS = NQBLK // ATTN_BLOCKS
ATTN_ROWS = ATTN_BLOCKS * QBLK


def _bias_variant(i, b):
    if b == 0:
        return jnp.where(i == 0, 0, 1)
    if b == ATTN_BLOCKS - 1:
        return jnp.where(i == ATTN_STEPS - 1, 2, 1)
    return 1
KCOLS = QBLK


def _attn_in_specs():
    return [
        pl.BlockSpec((ATTN_ROWS, 128), lambda p, i: (i, ZQ + p)),
        pl.BlockSpec((SEQ, 128), lambda p, i: (0, ZK + p)),
        pl.BlockSpec((SEQ, 128), lambda p, i: (0, ZV + p)),
        pl.BlockSpec((ATTN_ROWS, 128), lambda p, i: (i, ZG + p)),
        pl.BlockSpec((CTX, 128), lambda p, i: (0, 2 + p)),
        pl.BlockSpec((CTX, 128), lambda p, i: (0, 6 + p)),
    ]


def _rpb_spec():
    return pl.BlockSpec((2, 15, 128), lambda p, i: (p, 0, 0))


def _prob_specs():
    return [pl.BlockSpec((2, ATTN_ROWS, KBLK), lambda p, i: (p, i, 0)),
            pl.BlockSpec((2, ATTN_ROWS, CTX), lambda p, i: (p, i, 0))]


NORM_ROWS = 512


def _half_sums(x):
    lo = lax.broadcasted_iota(jnp.int32, (1, 2 * HDIM), 1) < HDIM
    return jnp.where(lo, jnp.sum(jnp.where(lo, x, 0.0), axis=-1, keepdims=True),
                     jnp.sum(jnp.where(lo, 0.0, x), axis=-1, keepdims=True))


def _pair_rms_bwd(x, g2, ct):
    rs = lax.rsqrt(_half_sums(x * x) / HDIM + EPS)
    y = x * rs
    dy = ct * g2
    return rs * (dy - y * (_half_sums(dy * y) / HDIM)), jnp.sum(ct * y, axis=0, keepdims=True)


def _norm_keys(k_ref, ck_ref, kg_ref, kn_scr, ckn_scr):
    def body(c, carry):
        sl = pl.ds(pl.multiple_of(c * NORM_ROWS, NORM_ROWS), NORM_ROWS)
        kn_scr[sl, :] = _pair_rms(k_ref[sl, :], kg_ref[...]).astype(BF16)
        return carry

    lax.fori_loop(0, SEQ // NORM_ROWS, body, 0)
    ckn_scr[...] = _pair_rms(ck_ref[...], kg_ref[...]).astype(BF16)


def _values_with_ones(v_ref, cv_ref, v1_scr, cv1_scr):
    for a, mine in enumerate(_head_lanes()):
        def body(c, carry):
            sl = pl.ds(pl.multiple_of(c * NORM_ROWS, NORM_ROWS), NORM_ROWS)
            v1_scr[a, sl, :] = jnp.where(mine, v_ref[sl, :], 1.0).astype(BF16)
            return carry

        lax.fori_loop(0, SEQ // NORM_ROWS, body, 0)
        cv1_scr[a] = jnp.where(mine, cv_ref[...], 1.0).astype(BF16)


def _pair_major_spec():
    return pl.BlockSpec((1, ATTN_ROWS, 128), lambda p, i: (p, i, 0))


def _normed_key_specs():
    return [pl.BlockSpec((None, SEQ, 128), lambda p, i: (p, 0, 0)), pl.BlockSpec((None, CTX, 128), lambda p, i: (p, 0, 0))]


def attn_fwd(z, zc, rpb2, qg2, kg2):
    def kern(q_ref, k_ref, v_ref, bg_ref, ck_ref, cv_ref, rpb_ref, qg_ref, kg_ref,
             ob_ref, o_ref, rden_ref, pl_ref, pc_ref, kn_ref, ckn_ref, kn_scr, ckn_scr, v1_scr, cv1_scr, s_scr,
             bias_ref):
        i = pl.program_id(1)

        @pl.when(i == 0)
        def _():
            for a in range(2):
                def store(v, tile_rows, tile_cols, tile, a=a):
                    bias_ref[v, a, tile_rows, tile_cols] = tile

                _bias_tiles(rpb_ref[a], store)
            _norm_keys(k_ref, ck_ref, kg_ref, kn_scr, ckn_scr)
            kn_ref[...] = kn_scr[...]
            ckn_ref[...] = ckn_scr[...]
            _values_with_ones(v_ref, cv_ref, v1_scr, cv1_scr)

        heads = _head_lanes()
        tiles = [(b, a) for b in range(ATTN_BLOCKS) for a in range(2)]
        rows = [slice(b * QBLK, (b + 1) * QBLK) for b in range(ATTN_BLOCKS)]
        variant = [_bias_variant(i, b) for b in range(ATTN_BLOCKS)]
        pv = [None] * len(tiles)
        qa, done = {}, {}
        latent = KBLK // KCOLS
        buf = lambda t: t % TILE_BUFFERS

        def keys(b, n):
            return pl.ds(pl.multiple_of(_kstart(ATTN_BLOCKS * i + b) + n * KCOLS, KCOLS), KCOLS)

        def score_piece(t, n):
            b, a = tiles[t]
            cols = slice(n * KCOLS, (n + 1) * KCOLS)
            if n == 0:
                if a == 0:
                    done["qn", b] = _scaled_q(q_ref[rows[b], :], qg_ref[...])
                qa[t] = jnp.where(heads[a], done["qn", b], 0.0).astype(BF16)
            if n < latent:
                s_scr[buf(t), :, cols] = mm_nt(qa[t], kn_scr[keys(b, n), :]) + bias_ref[variant[b], a, :, cols]
            else:
                s_scr[buf(t), :, cols] = mm_nt(qa[t], ckn_scr[...])

        def softmax_rows(t, r):
            b, a = tiles[t]
            rs = slice(r * SOFTMAX_ROWS, (r + 1) * SOFTMAX_ROWS)
            out_rows = slice(b * QBLK + rs.start, b * QBLK + rs.stop)
            s = s_scr[buf(t), rs, :]
            p = jnp.exp(s - jnp.max(s, axis=-1, keepdims=True)).astype(BF16)
            pl_ref[a, out_rows, :] = p[:, :KBLK]
            pc_ref[a, out_rows, :] = p[:, KBLK:]

        def value_piece(t, n):
            b, a = tiles[t]
            if n < latent:
                part = mm(pl_ref[a, rows[b], n * KCOLS:(n + 1) * KCOLS], v1_scr[a, keys(b, n), :])
            else:
                part = mm(pc_ref[a, rows[b], :], cv1_scr[a])
            pv[t] = part if pv[t] is None else pv[t] + part
            if n == latent:
                finish(t)

        def finish(t):
            b, a = tiles[t]
            r = jnp.where(heads[a], pltpu.roll(1.0 / pv[t], HDIM, 1), 0.0)
            done[t] = (pv[t] * r, r)
            if a == 1:
                o, rden = (lo + hi for lo, hi in zip(done[t - 1], done[t]))
                ob_ref[rows[b], :] = o * jax.nn.silu(bg_ref[rows[b], :])
                o_ref[0, rows[b], :] = o
                rden_ref[0, rows[b], :] = rden

        pieces = range(latent + 1)
        for n in pieces:
            score_piece(0, n)
        for t in range(len(tiles)):
            matmuls = []
            for n in pieces:
                if t + 1 < len(tiles):
                    matmuls.append(functools.partial(score_piece, t + 1, n))
                if t > 0:
                    matmuls.append(functools.partial(value_piece, t - 1, n))
            _emit_interleaved([functools.partial(softmax_rows, t, r) for r in range(QBLK // SOFTMAX_ROWS)], matmuls)
        for n in pieces:
            value_piece(len(tiles) - 1, n)

    qblk = pl.BlockSpec((ATTN_ROWS, 128), lambda p, i: (i, p))
    return pl.pallas_call(
        kern, name="attn_fwd", grid=(NPAIR, ATTN_STEPS),
        in_specs=_attn_in_specs() + [_rpb_spec(), _row(128), _row(128)],
        out_specs=[qblk, _pair_major_spec(), _pair_major_spec()] + _prob_specs() + _normed_key_specs(),
        out_shape=[jax.ShapeDtypeStruct((SEQ, 512), F32)] + [jax.ShapeDtypeStruct((NPAIR, SEQ, 128), F32)] * 2
        + [jax.ShapeDtypeStruct((HEADS, SEQ, KBLK), BF16), jax.ShapeDtypeStruct((HEADS, SEQ, CTX), BF16),
           jax.ShapeDtypeStruct((NPAIR, SEQ, 128), BF16), jax.ShapeDtypeStruct((NPAIR, CTX, 128), BF16)],
        scratch_shapes=[pltpu.VMEM((SEQ, 128), BF16), pltpu.VMEM((CTX, 128), BF16),
                        pltpu.VMEM((2, SEQ, 128), BF16), pltpu.VMEM((2, CTX, 128), BF16),
                        pltpu.VMEM((TILE_BUFFERS, QBLK, KBLK + CTX), F32),
                        pltpu.VMEM((3, 2, QBLK, KBLK), F32)],
        compiler_params=_cparams(("arbitrary", "arbitrary"), VMEM_BIG),
    )(z, z, z, z, zc, zc, rpb2, qg2, kg2)


def attn_bwd(z, zc, qg2, kg2, dcat, saved):
    def kern(q_ref, k_ref, v_ref, bg_ref, ck_ref, cv_ref, qg_ref, kg_ref, do_ref, o_ref, rden_ref, pl_ref, pc_ref,
             kn_scr, ckn_scr, dq_ref, dk_ref, dv_ref, dbg_ref, dck_ref, dcv_ref, drpb_ref, dqg_ref, dkg_ref,
             v_scr, cv_scr, dknt_scr, dvt_scr, dcknt_scr, dcvt_scr, dp_scr, ds_scr, db_ref):
        p, i = pl.program_id(0), pl.program_id(1)
        last = i == ATTN_STEPS - 1

        @pl.when(i == 0)
        def _():
            def body(c, carry):
                sl = pl.ds(pl.multiple_of(c * NORM_ROWS, NORM_ROWS), NORM_ROWS)
                v_scr[sl, :] = v_ref[sl, :].astype(BF16)
                return carry

            lax.fori_loop(0, SEQ // NORM_ROWS, body, 0)
            cv_scr[...] = cv_ref[...].astype(BF16)
            for acc in (dknt_scr, dvt_scr, dcknt_scr, dcvt_scr, db_ref):
                acc[...] = jnp.zeros_like(acc)

        @pl.when((i == 0) & (p == 0))
        def _():
            dqg_ref[...] = jnp.zeros_like(dqg_ref)
            dkg_ref[...] = jnp.zeros_like(dkg_ref)

        heads = _head_lanes()
        tiles = [(b, a) for b in range(ATTN_BLOCKS) for a in range(2)]
        rows = [slice(b * QBLK, (b + 1) * QBLK) for b in range(ATTN_BLOCKS)]
        kb = [_kblock(ATTN_BLOCKS * i + b) for b in range(ATTN_BLOCKS)]
        variant = [_bias_variant(i, b) for b in range(ATTN_BLOCKS)]
        latent = KBLK // KCOLS
        buf = lambda t: t % TILE_BUFFERS

        def keys(b, n):
            return pl.ds(pl.multiple_of((kb[b] + n) * KCOLS, KCOLS), KCOLS)

        gated = {}

        def gate_backward(b):
            bg, dout, o = bg_ref[rows[b], :], do_ref[rows[b], :], o_ref[0, rows[b], :]
            sig = jax.nn.sigmoid(bg)
            do = dout * (bg * sig)
            dbg_ref[rows[b], :] = (dout * o * (sig * (1.0 + bg * (1.0 - sig)))).astype(BF16)
            rden = rden_ref[0, rows[b], :]
            dr = do * rden
            qn = _scaled_q(q_ref[rows[b], :], qg_ref[...])
            gated[b] = (dr, dr.T.astype(BF16), qn.T.astype(BF16), do * o * rden)

        feats = [slice(a * HDIM, (a + 1) * HDIM) for a in range(2)]
        doa, doa_t, qa_t, delta = {}, {}, {}, {}
        dqn = [None] * len(tiles)

        def cols(n):
            return slice(n * KCOLS, (n + 1) * KCOLS)

        def stage_a(t, n):
            b, a = tiles[t]
            if n == 0:
                if a == 0:
                    gate_backward(b)
                dr, dr_t, qn_t, weighted = gated[b]
                doa[t] = jnp.where(heads[a], dr, 0.0).astype(BF16)
                doa_t[t] = dr_t[feats[a], :]
                qa_t[t] = qn_t[feats[a], :]
                delta[t] = jnp.sum(jnp.where(heads[a], weighted, 0.0), axis=-1, keepdims=True)
            if n < latent:
                dp_scr[buf(t), :, cols(n)] = mm_nt(doa[t], v_scr[keys(b, n), :])
                dvt_scr[kb[b] + n, feats[a], :] += mm(doa_t[t], pl_ref[a, rows[b], cols(n)])
            else:
                dp_scr[buf(t), :, cols(n)] = mm_nt(doa[t], cv_scr[...])
                dcvt_scr[feats[a], :] += mm(doa_t[t], pc_ref[a, rows[b], :])

        def stage_b(t, r):
            b, a = tiles[t]
            rs = slice(r * SOFTMAX_ROWS, (r + 1) * SOFTMAX_ROWS)
            in_rows = slice(b * QBLK + rs.start, b * QBLK + rs.stop)
            d = dp_scr[buf(t), rs, :] - delta[t][rs, :]
            ds_lat = pl_ref[a, in_rows, :].astype(F32) * d[:, :KBLK]
            ds_ctx = pc_ref[a, in_rows, :].astype(F32) * d[:, KBLK:]
            db_ref[variant[b], a, rs, :] += ds_lat
            ds_scr[buf(t), rs, :KBLK] = ds_lat.astype(BF16)
            ds_scr[buf(t), rs, KBLK:] = ds_ctx.astype(BF16)

        def stage_c(t, n):
            b, a = tiles[t]
            ds = ds_scr[buf(t), :, cols(n)]
            if n < latent:
                part = mm(ds, kn_scr[keys(b, n), :])
                dknt_scr[kb[b] + n, feats[a], :] += mm(qa_t[t], ds)
            else:
                part = mm(ds, ckn_scr[...])
                dcknt_scr[feats[a], :] += mm(qa_t[t], ds)
            dqn[t] = part if dqn[t] is None else dqn[t] + part
            if n == latent and a == 1:
                both = jnp.where(heads[0], dqn[t - 1], 0.0) + jnp.where(heads[1], dqn[t], 0.0)
                dq, dqg = jax.vjp(_scaled_q, q_ref[rows[b], :], qg_ref[...])[1](both)
                dq_ref[rows[b], :] = dq.astype(BF16)
                dqg_ref[...] += dqg

        pieces = range(latent + 1)
        for n in pieces:
            stage_a(0, n)
        for t in range(len(tiles)):
            matmuls = []
            for n in pieces:
                if t + 1 < len(tiles):
                    matmuls.append(functools.partial(stage_a, t + 1, n))
                if t > 0:
                    matmuls.append(functools.partial(stage_c, t - 1, n))
            _emit_interleaved([functools.partial(stage_b, t, r) for r in range(QBLK // SOFTMAX_ROWS)], matmuls)
        for n in pieces:
            stage_c(len(tiles) - 1, n)

        @pl.when(last)
        def _():
            eye = (lax.broadcasted_iota(jnp.int32, (KCOLS, KCOLS), 0)
                   == lax.broadcasted_iota(jnp.int32, (KCOLS, KCOLS), 1)).astype(BF16)

            def turned(x):
                hi = x.astype(BF16)
                return mm_nt(eye, hi) + mm_nt(eye, x - hi.astype(F32))

            def body(c, dkg):
                sl = pl.ds(pl.multiple_of(c * NORM_ROWS, NORM_ROWS), NORM_ROWS)
                blocks = range(NORM_ROWS // KCOLS)
                dkn = jnp.concatenate([turned(dknt_scr[c * len(blocks) + n]) for n in blocks], axis=0)
                dv = jnp.concatenate([mm_nt(eye, dvt_scr[c * len(blocks) + n]) for n in blocks], axis=0)
                dk, dg = _pair_rms_bwd(k_ref[sl, :], kg_ref[...], dkn)
                dk_ref[sl, :] = dk.astype(BF16)
                dv_ref[sl, :] = dv.astype(BF16)
                return dkg + dg

            dkg = lax.fori_loop(0, SEQ // NORM_ROWS, body, jnp.zeros((1, 128), F32))
            dck, dg = _pair_rms_bwd(ck_ref[...], kg_ref[...], dcknt_scr[...].T)
            dck_ref[...] = dck
            dcv_ref[...] = dcvt_scr[...].T
            dkg_ref[...] += dkg + dg
            for a in range(2):
                rows_of_rpb = _rpb_grad(lambda v, tile_rows, tile_cols, a=a: db_ref[v, a, tile_rows, tile_cols])
                for d, row in enumerate(rows_of_rpb):
                    drpb_ref[a, d:d + 1, :] = row

        @pl.when(last & (p == NPAIR - 1))
        def _():
            dqg_ref[...] = dqg_ref[...] + pltpu.roll(dqg_ref[...], HDIM, 1)
            dkg_ref[...] = dkg_ref[...] + pltpu.roll(dkg_ref[...], HDIM, 1)

    blk = lambda rows: pl.BlockSpec((rows, 128), lambda p, i: (0, p))
    qblk = pl.BlockSpec((ATTN_ROWS, 128), lambda p, i: (i, p))
    return pl.pallas_call(
        kern, name="attn_bwd", grid=(NPAIR, ATTN_STEPS),
        in_specs=_attn_in_specs() + [_row(128), _row(128), pl.BlockSpec((ATTN_ROWS, 128), lambda p, i: (i, 4 + p)),
                                     _pair_major_spec(), _pair_major_spec()] + _prob_specs() + _normed_key_specs(),
        out_specs=[qblk, blk(SEQ), blk(SEQ), qblk, blk(CTX), blk(CTX), _rpb_spec(), _row(128), _row(128)],
        out_shape=[jax.ShapeDtypeStruct((SEQ, 512), BF16)] * 4 + [jax.ShapeDtypeStruct((CTX, 512), F32)] * 2
        + [jax.ShapeDtypeStruct((HEADS, 15, 128), F32)]
        + [jax.ShapeDtypeStruct((1, 128), F32), jax.ShapeDtypeStruct((1, 128), F32)],
        scratch_shapes=[pltpu.VMEM((SEQ, 128), BF16), pltpu.VMEM((CTX, 128), BF16),
                        pltpu.VMEM((SEQ // KCOLS, 128, KCOLS), F32), pltpu.VMEM((SEQ // KCOLS, 128, KCOLS), F32),
                        pltpu.VMEM((128, CTX), F32), pltpu.VMEM((128, CTX), F32),
                        pltpu.VMEM((TILE_BUFFERS, QBLK, KBLK + CTX), F32),
                        pltpu.VMEM((TILE_BUFFERS, QBLK, KBLK + CTX), BF16),
                        pltpu.VMEM((3, 2, QBLK, KBLK), F32)],
        compiler_params=_cparams(("arbitrary", "arbitrary"), VMEM_BIG),
    )(z, z, z, z, zc, zc, qg2, kg2, dcat, *saved)


def outproj(z, sg, ws, bsb, out_b, x, target, gate, wo):
    tl = SGU_CHUNK * SGU_PER_STEP
    nt = SEQ // tl

    def kern(au0_ref, av0_ref, ag0_ref, au1_ref, av1_ref, ag1_ref, sg_ref, ws_ref, bs_ref, b_ref, x_ref, t_ref, g_ref,
             w_ref, loss_ref, dy_ref, dcat_ref, dg_ref, dw_ref, a_scr):
        t = pl.program_id(0)
        cur, nxt = lax.rem(t, 2), lax.rem(t + 1, 2)

        def gating(refs, slot, cn):
            sl = slice(cn * SGU_CHUNK, (cn + 1) * SGU_CHUNK)
            au_ref, av_ref, ag_ref = refs
            a_scr[slot, sl, :] = _sgu_chunk(au_ref[sl, :], av_ref[sl, :], ag_ref[sl, :], sg_ref[...], ws_ref[...],
                                            bs_ref[...]).astype(BF16)

        @pl.when(t == 0)
        def _():
            loss_ref[...] = jnp.zeros_like(loss_ref)
            dg_ref[...] = jnp.zeros_like(dg_ref)
            dw_ref[...] = jnp.zeros_like(dw_ref)
            for cn in range(SGU_PER_STEP):
                gating((au0_ref, av0_ref, ag0_ref), 0, cn)

        a, b = a_scr[cur], b_ref[...].astype(BF16)
        mix = (jnp.dot(a, w_ref[0:512, :], preferred_element_type=F32)
               + jnp.dot(b, w_ref[512:1024, :], preferred_element_type=F32))
        err = x_ref[...] + g_ref[...] * mix - t_ref[...]
        loss_ref[...] += 0.5 * jnp.sum(jnp.mean(err * err, axis=-1))
        dy = err * (1.0 / DM)
        dy_ref[...] = dy
        dg_ref[...] += jnp.sum(dy * mix, axis=0, keepdims=True)
        dmix = (g_ref[...] * dy).astype(BF16)

        def dcat_half(n):
            part = slice(512 * n, 512 * (n + 1))
            dcat_ref[:, part] = lax.dot_general(dmix, w_ref[part, :], _NT, preferred_element_type=F32)

        def dw_half(n, src):
            dw_ref[512 * n:512 * (n + 1), :] += lax.dot_general(src, dmix, (((0,), (0,)), ((), ())),
                                                                preferred_element_type=F32)

        _emit_interleaved([functools.partial(gating, (au1_ref, av1_ref, ag1_ref), nxt, cn) for cn in range(SGU_PER_STEP)],
                          [functools.partial(dcat_half, 0), functools.partial(dcat_half, 1),
                           functools.partial(dw_half, 0, a), functools.partial(dw_half, 1, b)])

    tile = lambda w: pl.BlockSpec((tl, w), lambda t: (t, 0))
    whole = pl.BlockSpec((DM, DM), lambda t: (0, 0))
    zfirst = [pl.BlockSpec((tl, 512), functools.partial(lambda c, t: (0, c), c)) for c in range(3)]
    znext = [pl.BlockSpec((tl, 512), functools.partial(lambda c, t: (jnp.minimum(t + 1, nt - 1), c), c))
             for c in range(3)]
    wspec = pl.BlockSpec((4, 128, 128), lambda t: (0, 0, 0))
    return pl.pallas_call(
        kern, name="outproj", grid=(nt,),
        in_specs=zfirst + znext + [_row(512), wspec, wspec, tile(512), tile(DM), tile(DM), _row(DM), whole],
        out_specs=[pl.BlockSpec((8, 128), lambda t: (0, 0)), tile(DM), tile(DM), _row(DM), whole],
        out_shape=[jax.ShapeDtypeStruct((8, 128), F32), jax.ShapeDtypeStruct((SEQ, DM), F32),
                   jax.ShapeDtypeStruct((SEQ, DM), F32), jax.ShapeDtypeStruct((1, DM), F32),
                   jax.ShapeDtypeStruct((DM, DM), F32)],
        scratch_shapes=[pltpu.VMEM((2, tl, 512), BF16)],
        compiler_params=_cparams(("arbitrary",), 48 * 1024 * 1024),
    )(z, z, z, z, z, z, sg, ws, bsb, out_b, x, target, gate, wo)


DZ_COLS = (("a", 0, 1536), ("q", 1536, 2048), ("k", 2048, 2560), ("v", 2560, 3072), ("g", 3072, DIN))
DZC_COLS = (("k", 2048, 2560), ("v", 2560, 3072))
_NT = (((1,), (1,)), ((), ()))


DH_SUBTILES = 4


def _dz_specs(tl):
    return [pl.BlockSpec((tl, 1536), lambda t: (t, 0))] + [pl.BlockSpec((tl, 512), lambda t: (t, 0))] * 4


def dh_bwd(dz_parts, w_full, x, dy, shift, scale, norm_g, dg_ctx):
    tl = 1024
    nt = SEQ // tl

    def kern(a_ref, q_ref, k_ref, v_ref, g_ref, w_ref, x_ref, dy_ref, sh_ref, sc_ref, gn_ref, dgc_ref,
             gx_ref, dsh_ref, dsc_ref, dg_ref):
        @pl.when(pl.program_id(0) == 0)
        def _():
            dsh_ref[...] = jnp.zeros_like(dsh_ref)
            dsc_ref[...] = jnp.zeros_like(dsc_ref)
            dg_ref[...] = dgc_ref[...]

        src = dict(a=a_ref, q=q_ref, k=k_ref, v=v_ref, g=g_ref)
        for sub in range(DH_SUBTILES):
            rows = slice(sub * tl // DH_SUBTILES, (sub + 1) * tl // DH_SUBTILES)
            dh = None
            for name, c0, c1 in DZ_COLS:
                part = lax.dot_general(src[name][rows, :], w_ref[:, c0:c1], _NT, preferred_element_type=F32)
                dh = part if dh is None else dh + part
            _, vjp = jax.vjp(_modulated, x_ref[rows, :], gn_ref[...], sc_ref[...], sh_ref[...])
            dx, dg, dsc, dsh = vjp(dh)
            gx_ref[rows, :] = dy_ref[rows, :] + dx
            dg_ref[...] += dg
            dsc_ref[...] += dsc
            dsh_ref[...] += dsh

    tile = pl.BlockSpec((tl, DM), lambda t: (t, 0))
    return pl.pallas_call(
        kern, name="dh_bwd", grid=(nt,),
        in_specs=_dz_specs(tl) + [pl.BlockSpec((DM, DIN), lambda t: (0, 0)), tile, tile, _row(DM),
                                  _row(DM), _row(DM), _row(DM)],
        out_specs=[tile, _row(DM), _row(DM), _row(DM)],
        out_shape=[jax.ShapeDtypeStruct((SEQ, DM), F32)] + [jax.ShapeDtypeStruct((1, DM), F32)] * 3,
        compiler_params=_cparams(("arbitrary",), VMEM_BIG),
    )(*dz_parts, w_full, x, dy, shift, scale, norm_g, dg_ctx)


def dw_bwd(h, z, sg, ws, bsb, dcat, dz_attn, hc, dck, dcv, g_out):
    tl = SGU_CHUNK * SGU_PER_STEP
    nt = SEQ // tl
    (rhi, wi), (rho, wo) = RS_SHAPES

    def kern(h_ref, au_ref, av_ref, ag_ref, sg_ref, ws_ref, bs_ref, do_ref, q_ref, k_ref, v_ref, g_ref,
             hc_ref, dck_ref, dcv_ref, go_hbm,
             wire_i, keep_i, wire_o, keep_o, a_ref, dsg_ref, dws_ref, dbs_ref,
             acc, rcv_i, mine_o, rcv_o, load_sem, send_sems, recv_sems):
        t = pl.program_id(0)
        x, y, c = _me()
        k = 2 * x + y
        sib = _flip(1)
        half = lambda hh, rh: pl.ds(pl.multiple_of(hh * rh, rh), rh)
        load_o = pltpu.make_async_copy(go_hbm.at[:, half(c, rho), :], mine_o, load_sem)
        pair_o = _rcopy(go_hbm.at[:, half(1 - c, rho), :], rcv_o, send_sems, recv_sems, 0, sib)
        pair_i = [_rcopy(wire_i.at[j], rcv_i.at[j], send_sems, recv_sems, 1 + j, sib) for j in range(NCHIP)]

        @pl.when(t == 0)
        def _():
            load_o.start()
            pair_o.start()
            acc[...] = jnp.zeros_like(acc)
            dsg_ref[...] = jnp.zeros_like(dsg_ref)
            dws_ref[...] = jnp.zeros_like(dws_ref)
            dbs_ref[...] = jnp.zeros_like(dbs_ref)
            hct = hc_ref[...].T
            csrc = dict(k=dck_ref, v=dcv_ref)
            for name, c0, c1 in DZC_COLS:
                acc[:, c0:c1] += jnp.dot(hct, csrc[name][...].astype(BF16), preferred_element_type=F32)

        ht = h_ref[...].T
        src = dict(a=a_ref, q=q_ref, k=k_ref, v=v_ref, g=g_ref)

        def gating_backward(cn):
            sl = slice(cn * SGU_CHUNK, (cn + 1) * SGU_CHUNK)
            _, vjp = jax.vjp(_sgu_chunk, au_ref[sl, :], av_ref[sl, :], ag_ref[sl, :], sg_ref[...], ws_ref[...],
                             bs_ref[...])
            dau, dav, dag, dsg, dws, dbs = vjp(do_ref[sl, :])
            a_ref[sl, 0:512] = dau.astype(BF16)
            a_ref[sl, 512:1024] = dav.astype(BF16)
            a_ref[sl, 1024:1536] = dag.astype(BF16)
            dsg_ref[...] += dsg
            dws_ref[...] += dws
            dbs_ref[...] += dbs

        def product(name, c0, c1):
            acc[:, c0:c1] += jnp.dot(ht, src[name][...], preferred_element_type=F32)

        _emit_interleaved([functools.partial(gating_backward, cn) for cn in range(SGU_PER_STEP)],
                          [functools.partial(product, *cols) for cols in DZ_COLS[1:]])
        product(*DZ_COLS[0])

        @pl.when(t == nt - 1)
        def _():
            dbs_ref[...] = jnp.broadcast_to(jnp.sum(dbs_ref[...], axis=-1, keepdims=True), dbs_ref.shape)
            shard = lambda j: slice(j * SHARD_IN, (j + 1) * SHARD_IN)
            for j in range(NCHIP):
                wire_i[j] = acc[half(1 - c, rhi), shard(j)].astype(BF16)
                pair_i[j].start()
            load_o.wait()
            pair_o.wait_recv()
            for j in range(NCHIP):
                wire_o[j] = (mine_o[j] + rcv_o[j]).astype(BF16)
            keep_o[...] = mine_o[k] + rcv_o[k]
            mine = half(c, rhi)
            for j in range(NCHIP):
                pair_i[j].wait_recv()
                pair_i[j].wait_send()
                pair_sum = acc[mine, shard(j)] + rcv_i[j].astype(F32)
                wire_i[j] = pair_sum.astype(BF16)

                @pl.when(k == j)
                def _():
                    keep_i[...] = pair_sum
            pair_o.wait_send()

    whole = lambda *shape: pl.BlockSpec(shape, lambda t: (0,) * len(shape))
    rows, sgu_specs = _sgu_specs()
    assert rows == tl
    a_spec, *attn_specs = _dz_specs(tl)
    return pl.pallas_call(
        kern, name="dw_bwd", grid=(nt,),
        in_specs=[pl.BlockSpec((tl, DM), lambda t: (t, 0))] + sgu_specs + [pl.BlockSpec((tl, 512), lambda t: (t, 0))]
        + attn_specs + [whole(CTX, DM), whole(CTX, 512), whole(CTX, 512), pl.BlockSpec(memory_space=pl.ANY)],
        out_specs=[whole(NCHIP, rhi, wi), whole(rhi, wi), whole(NCHIP, rho, wo), whole(rho, wo),
                   a_spec, _row(512), whole(4, 128, 128), whole(4, 128, 128)],
        out_shape=[jax.ShapeDtypeStruct((NCHIP, rhi, wi), BF16), jax.ShapeDtypeStruct((rhi, wi), F32),
                   jax.ShapeDtypeStruct((NCHIP, rho, wo), BF16), jax.ShapeDtypeStruct((rho, wo), F32),
                   jax.ShapeDtypeStruct((SEQ, 1536), BF16), jax.ShapeDtypeStruct((1, 512), F32),
                   jax.ShapeDtypeStruct((4, 128, 128), F32), jax.ShapeDtypeStruct((4, 128, 128), F32)],
        scratch_shapes=[pltpu.VMEM((DM, DIN), F32), pltpu.VMEM((NCHIP, rhi, wi), BF16),
                        pltpu.VMEM((NCHIP, rho, wo), F32), pltpu.VMEM((NCHIP, rho, wo), F32),
                        pltpu.SemaphoreType.DMA(()), pltpu.SemaphoreType.DMA((1 + NCHIP,)),
                        pltpu.SemaphoreType.DMA((1 + NCHIP,))],
        compiler_params=_cparams(("arbitrary",), 60 * 1024 * 1024),
    )(h, z, z, z, sg, ws, bsb, dcat, *dz_attn, hc, dck, dcv, g_out)


def ctx_bwd(dck, dcv, w_full, ctx, cshift, cscale, norm_g):
    def kern(dck_ref, dcv_ref, w_ref, c_ref, sh_ref, sc_ref, g_ref, dsh_ref, dsc_ref, dg_ref):
        csrc = dict(k=dck_ref, v=dcv_ref)
        dhc = None
        first = DZC_COLS[0][1]
        for name, c0, c1 in DZC_COLS:
            part = lax.dot_general(csrc[name][...].astype(BF16), w_ref[:, c0 - first:c1 - first], _NT,
                                   preferred_element_type=F32)
            dhc = part if dhc is None else dhc + part
        _, vjp = jax.vjp(lambda g, sc, sh: _modulated(c_ref[...], g, sc, sh), g_ref[...], sc_ref[...], sh_ref[...])
        dg_ref[...], dsc_ref[...], dsh_ref[...] = vjp(dhc)

    whole = lambda r, c: pl.BlockSpec((r, c), lambda i: (0, 0))
    return pl.pallas_call(
        kern, name="ctx_bwd", grid=(1,),
        in_specs=[whole(CTX, 512), whole(CTX, 512), pl.BlockSpec((DM, 1024), lambda i: (0, DZC_COLS[0][1] // 1024)),
                  whole(CTX, DM), _row(DM), _row(DM), _row(DM)],
        out_specs=[_row(DM), _row(DM), _row(DM)],
        out_shape=[jax.ShapeDtypeStruct((1, DM), F32)] * 3,
        compiler_params=_cparams(("arbitrary",), 40 * 1024 * 1024),
    )(dck, dcv, w_full, ctx, cshift, cscale, norm_g)


def _lane_pad_rpb(rpb):
    r = jnp.pad(rpb, ((0, 0), (0, 0), (0, GRID_W - rpb.shape[-1])))
    return jnp.concatenate([r, r], axis=-1)


def local_step(chip, dev, x, c_vec, c_ctx, w_ada, b_shard, ctx, target, norm_g, sgu_g, w_s, b_s, q_g, k_g, rpb,
               w_in_shard, w_out_shard):
    bsb = jnp.broadcast_to(b_s[:, :, None], (4, 128, 128))
    qg2, kg2 = jnp.tile(q_g, (1, 2)), jnp.tile(k_g, (1, 2))

    z, h, w_in_full, w_out_full, mod_all, cs = inproj_fwd(chip, x, c_vec, c_ctx, w_ada, b_shard, norm_g, w_in_shard,
                                                          w_out_shard)
    mods = mod_all.transpose(1, 0, 2).reshape(CS_ROWS, 3 * DM)
    mod = lax.dynamic_slice(mods, (8 * dev, 0), (1, 3 * DM))
    shift, scale, gate = mod[:, :DM], mod[:, DM:2 * DM], mod[:, 2 * DM:]
    cshift, cscale = mods[8 * NDEV:8 * NDEV + 1, :DM], mods[8 * NDEV:8 * NDEV + 1, DM:2 * DM]
    zc, hc = ctx_fwd(ctx, cshift, cscale, norm_g, w_in_full)
    out_b, *saved = attn_fwd(z, zc, _lane_pad_rpb(rpb), qg2, kg2)
    loss8, dy, dcat, dgate, dwo = outproj(z, sgu_g, w_s, bsb, out_b, x, target, gate, w_out_full.reshape(DM, DM))
    dq, dk, dv, dbg, dck, dcv, drpb, dqg2, dkg2 = attn_bwd(z, zc, qg2, kg2, dcat, saved)
    drpb = drpb[:, :, :rpb.shape[-1]]
    dcshift, dcscale, dng_c = ctx_bwd(dck, dcv, w_in_full, ctx, cshift, cscale, norm_g)
    wire_i, keep_i, wire_o, keep_o, dz_a, dsg, dws, dbsb = dw_bwd(
        h, z, sgu_g, w_s, bsb, dcat, (dq, dk, dv, dbg), hc, dck, dcv, dwo.reshape(NCHIP, SHARD_OUT, DM))
    dz_parts = (dz_a, dq, dk, dv, dbg)
    *in_flight, token = rs_start(wire_i, wire_o)
    grad_x, dshift, dscale, dng = dh_bwd(dz_parts, w_in_full, x, dy, shift, scale, norm_g, dng_c + token[0, 0])
    got_i, got_o = rs_wait(*in_flight, dshift)
    return dict(
        loss=loss8[0:1, 0:1], grad_x=grad_x, rs=(keep_i, got_i, keep_o, got_o), cs=cs,
        dmod=jnp.concatenate([dshift, dscale, dgate], axis=-1),
        dcmod=jnp.concatenate([dcshift, dcscale, jnp.zeros((1, DM), F32)], axis=-1),
        d_norm_g=dng, d_sgu_g=dsg, d_w_s=dws, d_b_s=dbsb[:, :, 0],
        d_q_g=dqg2[:, :HDIM], d_k_g=dkg2[:, :HDIM], d_rpb=drpb)


def _me():
    return lax.axis_index("x"), lax.axis_index("y"), lax.axis_index("c")


def _flip(q):
    x, y, c = _me()
    return ((1 - x) if q & 4 else x, (1 - y) if q & 2 else y, (1 - c) if q & 1 else c)


def _chip_of(dev):
    return 2 * dev[0] + dev[1]


def _rcopy(src, dst, send_sems, recv_sems, k, dev):
    return pltpu.make_async_remote_copy(src_ref=src, dst_ref=dst, send_sem=send_sems.at[k], recv_sem=recv_sems.at[k],
                                        device_id=dev, device_id_type=MESH_ID)


_VMEM_SPEC = pl.BlockSpec(memory_space=pltpu.VMEM)
SLAB_ROWS = 80


RS_SHAPES = ((DM // 2, SHARD_IN), (SHARD_OUT // 2, DM))
_HBM_SPEC = pl.BlockSpec(memory_space=pltpu.HBM)
_SEM_SPEC = pl.BlockSpec(memory_space=pltpu.SEMAPHORE)
_IN_FLIGHT = pltpu.SideEffectType.DATAFLOW_SIDE_EFFECTING


def _rs_copies(wires, lands, send_sems, recv_sems):
    return [pltpu.make_async_remote_copy(
        src_ref=wires[n].at[_chip_of(_flip(q))], dst_ref=lands[n].at[q // 2 - 1],
        send_sem=send_sems.at[3 * n + q // 2 - 1], recv_sem=recv_sems.at[3 * n + q // 2 - 1],
        device_id=_flip(q), device_id_type=MESH_ID) for n in (0, 1) for q in (2, 4, 6)]


def rs_start(wire_i, wire_o):
    lands = [lax.empty((NCHIP - 1, rh, w), BF16) for rh, w in RS_SHAPES]

    def body(wi_ref, wo_ref, li_ref, lo_ref, send_sems, recv_sems, wi_thru, wo_thru, li_thru, lo_thru, token):
        for cp in _rs_copies((wi_ref, wo_ref), (li_ref, lo_ref), send_sems, recv_sems):
            cp.start()
        token[...] = jnp.zeros_like(token)

    hbm = lambda a: pltpu.HBM(a.shape, a.dtype)
    return pl.pallas_call(
        body, name="rs_start",
        out_shape=(pltpu.SemaphoreType.DMA((6,)), pltpu.SemaphoreType.DMA((6,)), hbm(wire_i), hbm(wire_o),
                   hbm(lands[0]), hbm(lands[1]), jax.ShapeDtypeStruct((8, 128), F32)),
        in_specs=(_HBM_SPEC,) * 4, out_specs=(_SEM_SPEC, _SEM_SPEC) + (_HBM_SPEC,) * 4 + (_VMEM_SPEC,),
        input_output_aliases={0: 2, 1: 3, 2: 4, 3: 5},
        compiler_params=pltpu.CompilerParams(has_side_effects=_IN_FLIGHT),
    )(*[pltpu.with_memory_space_constraint(a, pltpu.HBM) for a in (wire_i, wire_o, *lands)])


def rs_wait(send_sems, recv_sems, wire_i, wire_o, land_i, land_o, after):
    def body(wi_ref, wo_ref, li_ref, lo_ref, send_sems, recv_sems, after_ref, wi_dead, wo_dead, gi_ref, go_ref):
        for cp in _rs_copies((wi_ref, wo_ref), (li_ref, lo_ref), send_sems, recv_sems):
            cp.wait_send()
            cp.wait_recv()

    hbm = lambda a: pltpu.HBM(a.shape, a.dtype)
    return pl.pallas_call(
        body, name="rs_wait", out_shape=(hbm(wire_i), hbm(wire_o), hbm(land_i), hbm(land_o)),
        in_specs=(_HBM_SPEC,) * 4 + (_SEM_SPEC, _SEM_SPEC, pl.BlockSpec(memory_space=pl.ANY)),
        out_specs=(_HBM_SPEC,) * 4, input_output_aliases={0: 0, 1: 1, 2: 2, 3: 3},
        compiler_params=pltpu.CompilerParams(has_side_effects=_IN_FLIGHT),
    )(wire_i, wire_o, land_i, land_o, send_sems, recv_sems, after)[2:]


def final_reduce(keep_i, got_i, keep_o, got_o, slab, cs, w_ada, c_ctx):
    (rhi, wi), (rho, wo) = RS_SHAPES

    def kern(ki_hbm, gi_hbm, ko_hbm, go_hbm, s_ref, cs_ref, w_hbm, cc_ref,
             gin_ref, gout_ref, tot_ref, dw_ref, db_ref, dcc_ref,
             ki, gi, ko, go, w_scr, all_ref, dms_scr, parts, load_sems, send_sems, recv_sems):
        x, y, c = _me()
        k = 2 * x + y
        sib = _flip(1)
        dev = lambda d: 4 * d[0] + 2 * d[1] + d[2]
        me = dev((x, y, c))

        def slab_copy(idx, owner, to):
            return _rcopy(all_ref.at[dev(owner)], all_ref.at[dev(owner)], send_sems, recv_sems, idx, to)

        all_ref[me] = s_ref[...]
        first = [slab_copy(0, (x, y, c), sib)] + [slab_copy(q // 2, (x, y, c), _flip(q)) for q in (2, 4, 6)]
        for cp in first:
            cp.start()
        loads = [pltpu.make_async_copy(src, dst, load_sems.at[n]) for n, (src, dst) in enumerate(
            ((ki_hbm, ki), (gi_hbm, gi), (ko_hbm, ko), (go_hbm, go), (w_hbm, w_scr)))]
        for cp in loads:
            cp.start()

        shares = []
        for n, (keep, got, out) in enumerate(((ki, gi, gin_ref), (ko, go, gout_ref))):
            rh = RS_SHAPES[n][0]
            half = lambda hh, rh=rh: pl.ds(pl.multiple_of(hh * rh, rh), rh)
            loads[2 * n].wait()
            loads[2 * n + 1].wait()
            out[half(c), :] = ((keep[...] + got[0].astype(F32)) + got[1].astype(F32)) + got[2].astype(F32)
            share = _rcopy(out.at[half(c), :], out.at[half(c), :], send_sems, recv_sems, 7 + n, sib)
            share.start()
            shares.append((share, _rcopy(out.at[half(1 - c), :], out.at[half(1 - c), :], send_sems, recv_sems, 7 + n,
                                         sib)))

        passed = []
        for q in (2, 4, 6):
            slab_copy(q // 2, _flip(q), (x, y, c)).wait_recv()
            cp = slab_copy(3 + q // 2, _flip(q), sib)
            cp.start()
            passed.append(cp)
        slab_copy(0, sib, (x, y, c)).wait_recv()
        for q in (2, 4, 6):
            slab_copy(3 + q // 2, _flip(q | 1), (x, y, c)).wait_recv()
        tot = all_ref[0]
        for d in range(1, NDEV):
            tot = tot + all_ref[d]
        tot_ref[...] = tot

        pad = jnp.zeros((7, DM), F32)
        dm = [jnp.concatenate([all_ref[d, 12 + j:13 + j, :] for d in range(NDEV)] + [tot[9 + j:10 + j, :], pad], axis=0)
              for j in range(3)]
        db_ref[...] = jnp.concatenate([jnp.sum(part, axis=0, keepdims=True) for part in dm], axis=0)
        dm = jnp.concatenate(dm, axis=-1)
        for j in range(NCHIP):
            @pl.when(k == j)
            def _():
                dms_scr[...] = dm[:, j * SHARD_ADA:(j + 1) * SHARD_ADA].astype(BF16)

        a_in = jnp.concatenate([cs_ref[8 * d:8 * d + 1, :] for d in range(NDEV)]
                               + [cs_ref[8 * NDEV:8 * NDEV + 1, :], pad], axis=0)
        act = jax.nn.silu(a_in).astype(BF16)
        dms = dms_scr[...]
        dw_ref[...] = lax.dot_general(act, dms, (((0,), (0,)), ((), ())), preferred_element_type=F32)
        loads[4].wait()
        parts[k] = lax.dot_general(dms, w_scr[...].astype(BF16), (((1,), (1,)), ((), ())), preferred_element_type=F32)
        sends = [_rcopy(parts.at[k], parts.at[k], send_sems, recv_sems, 8 + q // 2, _flip(q)) for q in (2, 4, 6)]
        for cp in sends:
            cp.start()
        for q in (2, 4, 6):
            kq = _chip_of(_flip(q))
            _rcopy(parts.at[kq], parts.at[kq], send_sems, recv_sems, 8 + q // 2, _flip(q)).wait_recv()
        dact = ((parts[0] + parts[1]) + parts[2]) + parts[3]
        _, vjp = jax.vjp(jax.nn.silu, cc_ref[...])
        dcc_ref[...] = vjp(dact[8:9, :])[0]

        for share, arrival in shares:
            arrival.wait_recv()
            share.wait_send()
        for cp in first + passed + sends:
            cp.wait_send()

    any_spec = pl.BlockSpec(memory_space=pl.ANY)
    return pl.pallas_call(
        kern, name="final_reduce",
        in_specs=[any_spec] * 4 + [_VMEM_SPEC, _VMEM_SPEC, any_spec, _VMEM_SPEC], out_specs=[_VMEM_SPEC] * 6,
        out_shape=[jax.ShapeDtypeStruct((2 * rhi, wi), F32), jax.ShapeDtypeStruct((2 * rho, wo), F32),
                   jax.ShapeDtypeStruct((SLAB_ROWS, DM), F32), jax.ShapeDtypeStruct((DM, SHARD_ADA), F32),
                   jax.ShapeDtypeStruct((3, DM), F32), jax.ShapeDtypeStruct((1, DM), F32)],
        scratch_shapes=[pltpu.VMEM((rhi, wi), F32), pltpu.VMEM((NCHIP - 1, rhi, wi), BF16),
                        pltpu.VMEM((rho, wo), F32), pltpu.VMEM((NCHIP - 1, rho, wo), BF16),
                        pltpu.VMEM((DM, SHARD_ADA), F32), pltpu.VMEM((NDEV, SLAB_ROWS, DM), F32),
                        pltpu.VMEM((16, SHARD_ADA), BF16), pltpu.VMEM((NCHIP, 16, DM), F32),
                        pltpu.SemaphoreType.DMA((5,)), pltpu.SemaphoreType.DMA((12,)), pltpu.SemaphoreType.DMA((12,))],
        compiler_params=pltpu.CompilerParams(vmem_limit_bytes=40 * 1024 * 1024),
    )(keep_i, got_i, keep_o, got_o, slab, cs, w_ada, c_ctx)


def _adamw_math(w, g, m, v):
    m = B1 * m + (1.0 - B1) * g
    v = B2 * v + (1.0 - B2) * (g * g)
    m_hat = m / (1.0 - B1 ** STEP)
    v_hat = v / (1.0 - B2 ** STEP)
    return -LR * (m_hat / (jnp.sqrt(v_hat) + ADAM_EPS) + WD * w), m, v


def adamw_big(w, g, m, v, name, block_rows=256):
    rows, width = w.shape

    def kern(w_ref, g_ref, m_ref, v_ref, d_ref, nm_ref, nv_ref):
        d_ref[...], nm_ref[...], nv_ref[...] = _adamw_math(w_ref[...], g_ref[...], m_ref[...], v_ref[...])

    spec = pl.BlockSpec((block_rows, width), lambda i: (i, 0))
    return pl.pallas_call(
        kern, name=name, grid=(rows // block_rows,), in_specs=[spec] * 4, out_specs=[spec] * 3,
        out_shape=[jax.ShapeDtypeStruct((rows, width), F32)] * 3,
        compiler_params=_cparams(("arbitrary",)),
    )(w, g, m, v)


def adamw_small(quads):
    n = len(quads)

    def kern(*refs):
        ins, outs = refs[:4 * n], refs[4 * n:]
        for i in range(n):
            w, g, m, v = (r[...] for r in ins[4 * i:4 * i + 4])
            outs[3 * i][...], outs[3 * i + 1][...], outs[3 * i + 2][...] = _adamw_math(w, g, m, v)

    flat = [a for quad in quads for a in quad]
    res = pl.pallas_call(
        kern, name="adamw_small", in_specs=[_VMEM_SPEC] * (4 * n), out_specs=[_VMEM_SPEC] * (3 * n),
        out_shape=[jax.ShapeDtypeStruct(q[0].shape, F32) for q in quads for _ in range(3)],
    )(*flat)
    return [tuple(res[3 * i:3 * i + 3]) for i in range(n)]


def _rows_of(a, rows):
    flat = a.reshape(-1)
    return jnp.pad(flat, (0, rows * DM - flat.shape[0])).reshape(rows, DM)


def kernel(x, c, ctx, c_ctx, w_ada, b_ada, norm_g, w_in, sgu_norm_g, w_spatial, b_spatial, q_norm_g, k_norm_g, rpb, w_out, loss_target, m_c_ctx, m_w_ada, m_b_ada, m_norm_g, m_w_in, m_sgu_norm_g, m_w_spatial, m_b_spatial, m_q_norm_g, m_k_norm_g, m_rpb, m_w_out, v_c_ctx, v_w_ada, v_b_ada, v_norm_g, v_w_in, v_sgu_norm_g, v_w_spatial, v_b_spatial, v_q_norm_g, v_k_norm_g, v_rpb, v_w_out):
    xi, yi, ci = lax.axis_index("x"), lax.axis_index("y"), lax.axis_index("c")
    chip, dev = 2 * xi + yi, 4 * xi + 2 * yi + ci
    c_ctx2 = c_ctx.reshape(1, DM)

    b_shard = lax.dynamic_slice(b_ada, (0, chip * SHARD_ADA), (1, SHARD_ADA))
    part = local_step(chip.reshape(1).astype(jnp.int32), dev, x[0], c, c_ctx2, w_ada[0], b_shard, ctx[0], loss_target[0],
                      norm_g, sgu_norm_g, w_spatial[0], b_spatial[0], q_norm_g, k_norm_g, rpb[0], w_in[0], w_out[0])
    cs = part["cs"]

    slab = jnp.concatenate([
        part["d_norm_g"], _rows_of(part["d_sgu_g"], 1), _rows_of(part["d_b_s"], 1),
        _rows_of(jnp.concatenate([part["d_q_g"], part["d_k_g"]], axis=-1), 1), _rows_of(part["d_rpb"], 4),
        _rows_of(part["loss"], 1), _rows_of(part["dcmod"], 3), _rows_of(part["dmod"], 3), jnp.zeros((1, DM), F32),
        _rows_of(part["d_w_s"], 64)], axis=0)
    g_w_in, g_w_out, tot, g_w_ada, g_b_ada, g_c_ctx = final_reduce(*part["rs"], slab, cs, w_ada[0], c_ctx2)
    g_b_ada = g_b_ada.reshape(1, 3 * DM)

    loss = tot[8, 0]
    g_small = dict(
        c_ctx=g_c_ctx, b_ada=g_b_ada, norm_g=tot[0:1], sgu_norm_g=tot[1:2, :512], w_spatial=tot[16:80].reshape(512, 128),
        b_spatial=tot[2:3, :512].reshape(4, 128), q_norm_g=tot[3:4, :HDIM], k_norm_g=tot[3:4, HDIM:2 * HDIM],
        rpb=tot[4:8].reshape(-1)[:HEADS * 15 * 31].reshape(HEADS * 15, 31))
    shapes = dict(c_ctx=(DM,), w_ada=(1, DM, SHARD_ADA), b_ada=(1, 3 * DM), norm_g=(1, DM), w_in=(1, DM, SHARD_IN),
                  sgu_norm_g=(1, 512), w_spatial=(1, 4, 128, 128), b_spatial=(1, 4, 128), q_norm_g=(1, HDIM),
                  k_norm_g=(1, HDIM), rpb=(1, HEADS, 15, 31), w_out=(1, SHARD_OUT, DM))
    names = list(shapes)
    weights = dict(c_ctx=c_ctx, w_ada=w_ada, b_ada=b_ada, norm_g=norm_g, w_in=w_in, sgu_norm_g=sgu_norm_g,
                   w_spatial=w_spatial, b_spatial=b_spatial, q_norm_g=q_norm_g, k_norm_g=k_norm_g, rpb=rpb, w_out=w_out)
    m_in = dict(zip(names, (m_c_ctx, m_w_ada, m_b_ada, m_norm_g, m_w_in, m_sgu_norm_g, m_w_spatial, m_b_spatial,
                            m_q_norm_g, m_k_norm_g, m_rpb, m_w_out)))
    v_in = dict(zip(names, (v_c_ctx, v_w_ada, v_b_ada, v_norm_g, v_w_in, v_sgu_norm_g, v_w_spatial, v_b_spatial,
                            v_q_norm_g, v_k_norm_g, v_rpb, v_w_out)))
    grads = dict(g_small, w_ada=g_w_ada, w_in=g_w_in, w_out=g_w_out)
    upd = {}
    for n in ("w_ada", "w_in", "w_out"):
        g = grads[n]
        upd[n] = adamw_big(weights[n].reshape(g.shape), g, m_in[n].reshape(g.shape), v_in[n].reshape(g.shape),
                           "adamw_" + n)
    small = [n for n in names if n not in upd]
    res = adamw_small([(weights[n].reshape(grads[n].shape), grads[n], m_in[n].reshape(grads[n].shape),
                        v_in[n].reshape(grads[n].shape)) for n in small])
    upd.update(zip(small, res))
    out = [loss, part["grad_x"].reshape(1, SEQ, DM)]
    out += [grads[n].reshape(shapes[n]) for n in names]
    for slot in range(3):
        out += [upd[n][slot].reshape(shapes[n]) for n in names]
    return tuple(out)
```

```python
import functools

import jax
import jax.numpy as jnp
from jax import lax
from jax.experimental import pallas as pl
from jax.experimental.pallas import tpu as pltpu

F32, BF16 = jnp.float32, jnp.bfloat16
SEQ, DM, CTX, DIN = 4096, 1024, 256, 3584
NCHIP, NDEV = 4, 8
SHARD_IN = DIN // NCHIP
SHARD_ADA = 3 * DM // NCHIP
SHARD_OUT = DM // NCHIP
GRID_W = 64
QROWS = 4
KROWS = 12
QBLK, KBLK = QROWS * GRID_W, KROWS * GRID_W
NQBLK = SEQ // QBLK
HEADS, HDIM, NPAIR = 8, 64, 4
EPS = 1e-6
NEG_INF = -1e30
ZQ, ZK, ZV, ZG = 12, 16, 20, 24
LR, B1, B2, ADAM_EPS, WD, STEP = 0.001, 0.9, 0.999, 1e-08, 0.01, 10
VMEM_BIG = 56 * 1024 * 1024
MESH_ID = pl.DeviceIdType.MESH


def _dot(a, b, lhs_c, rhs_c):
    return lax.dot_general(a.astype(BF16), b.astype(BF16), (((lhs_c,), (rhs_c,)), ((), ())),
                           preferred_element_type=F32)


@jax.custom_vjp
def mm(a, b):
    return _dot(a, b, 1, 0)


@jax.custom_vjp
def mm_nt(a, b):
    return _dot(a, b, 1, 1)


@jax.custom_vjp
def mm_tn(a, b):
    return _dot(a, b, 0, 0)


mm.defvjp(lambda a, b: (mm(a, b), (a, b)), lambda r, ct: (mm_nt(ct, r[1]), mm_tn(r[0], ct)))
mm_nt.defvjp(lambda a, b: (mm_nt(a, b), (a, b)), lambda r, ct: (mm(ct, r[1]), mm_tn(ct, r[0])))
mm_tn.defvjp(lambda a, b: (mm_tn(a, b), (a, b)), lambda r, ct: (mm_nt(r[1], ct), mm(r[0], ct)))


def _rms(x, g):
    return x * lax.rsqrt(jnp.mean(x * x, axis=-1, keepdims=True) + EPS) * g


def _modulated(x, g, scale, shift):
    return _rms(x, g) * (1.0 + scale) + shift


def _pair_rms(x, g2):
    lo = lax.broadcasted_iota(jnp.int32, (1, 2 * HDIM), 1) < HDIM
    sq = x * x
    s_lo = jnp.sum(jnp.where(lo, sq, 0.0), axis=-1, keepdims=True)
    s_hi = jnp.sum(jnp.where(lo, 0.0, sq), axis=-1, keepdims=True)
    rs = jnp.where(lo, lax.rsqrt(s_lo / HDIM + EPS), lax.rsqrt(s_hi / HDIM + EPS))
    return x * rs * g2


def _cparams(sem, vmem=None):
    return pltpu.CompilerParams(dimension_semantics=sem, vmem_limit_bytes=vmem)


def _row(n):
    return pl.BlockSpec((1, n), lambda *_: (0, 0))


CS_ROWS = 8 * NDEV + 8


def _mod_part(mod_ref, row, part):
    pieces = []
    for j in range(NCHIP):
        lo, hi = max(part * DM, j * SHARD_ADA), min((part + 1) * DM, (j + 1) * SHARD_ADA)
        if lo < hi:
            pieces.append(mod_ref[j, row, lo - j * SHARD_ADA:hi - j * SHARD_ADA])
    return jnp.concatenate(pieces, axis=-1)


def inproj_fwd(chip, x, c_vec, c_ctx, w_ada, b_shard, norm_g, w_shard, wo_shard):
    tl = 1024
    nt = SEQ // tl
    halves = (DM // 2, SHARD_OUT // 2)
    n_w, n_c = 12, NDEV - 1

    def kern(k_ref, x_ref, cv_ref, cc_ref, wa_ref, b_ref, g_ref, w_ref, wo_ref,
             z_ref, h_ref, wfull_ref, wofull_ref, modall_ref, csall_ref,
             w_scr, wo_scr, h_scr, mine, cs_scr, mod_scr, shsc_scr, send_sems, recv_sems, out_sems):
        s, t = pl.program_id(0), pl.program_id(1)
        xi, yi, c = _me()
        k, me = 2 * xi + yi, 4 * xi + 2 * yi + c
        sib = _flip(1)
        rows = pl.ds(pl.multiple_of(t * tl, tl), tl)
        gathered = (w_scr, wo_scr)
        slot = lambda d: pl.ds(pl.multiple_of(8 * d, 8), 8)

        def c_copy(q, owner):
            return _rcopy(mine, cs_scr.at[slot(owner), :], send_sems, recv_sems, n_w + q - 1, _flip(q))

        def m_copy(q, chip_of_block):
            return _rcopy(mod_scr.at[chip_of_block], mod_scr.at[chip_of_block], send_sems, recv_sems,
                          n_w + n_c + q // 2 - 1, _flip(q))

        def adaln():
            first = lax.broadcasted_iota(jnp.int32, (8, DM), 0) == 0
            mine[...] = jnp.where(first, jnp.broadcast_to(cv_ref[...], (8, DM)), 0.0)
            cs_scr[slot(me), :] = mine[...]
            cs_scr[slot(NDEV), :] = jnp.where(first, jnp.broadcast_to(cc_ref[...], (8, DM)), 0.0)
            for q in range(1, NDEV):
                c_copy(q, me).start()
            wa = wa_ref[...].astype(BF16)
            for q in range(1, NDEV):
                px, py, pc = _flip(q)
                c_copy(q, 4 * px + 2 * py + pc).wait_recv()
            act = jax.nn.silu(cs_scr[...]).astype(BF16)
            mod_scr[k] = jnp.dot(act, wa, preferred_element_type=F32) + b_ref[...]
            for q in (2, 4, 6):
                m_copy(q, k).start()
            for q in (2, 4, 6):
                m_copy(q, _chip_of(_flip(q))).wait_recv()
            row = pl.ds(8 * me, 1)
            shsc_scr[0:1, :] = _mod_part(mod_scr, row, 0)
            shsc_scr[1:2, :] = _mod_part(mod_scr, row, 1)
            pltpu.sync_copy(mod_scr, modall_ref)
            pltpu.sync_copy(cs_scr, csall_ref)

        def block(n, chip_of_block, hh):
            return gathered[n].at[chip_of_block, pl.ds(pl.multiple_of(hh * halves[n], halves[n]), halves[n]), :]

        def ici(n, q, chip_of_block):
            blk = block(n, chip_of_block, c)
            return _rcopy(blk, blk, send_sems, recv_sems, 6 * n + q // 2 - 1, _flip(q))

        def d2d(n, q, chip_of_block, hh):
            blk = block(n, chip_of_block, hh)
            return _rcopy(blk, blk, send_sems, recv_sems, 6 * n + 3 + q // 2 - 1, sib)

        @pl.when((s == 0) & (t == 0))
        def _():
            adaln()
            w_scr[k] = w_ref[...].astype(BF16)
            wo_scr[k] = wo_ref[...].astype(BF16)
            for q in (2, 4, 6):
                ici(0, q, k).start()
                ici(1, q, k).start()

        for sweep in (1, 2, 3):
            @pl.when((s == sweep) & (t == 0))
            def _():
                q = 2 * sweep
                src = _chip_of(_flip(q))
                for n in (0, 1):
                    ici(n, q, src).wait_recv()
                    d2d(n, q, src, c).start()
                for n in (0, 1):
                    d2d(n, q, src, 1 - c).wait_recv()

        @pl.when(s == 0)
        def _():
            hb = _modulated(x_ref[...], g_ref[...], shsc_scr[1:2, :], shsc_scr[0:1, :]).astype(BF16)
            h_scr[rows, :] = hb
            h_ref[...] = hb

        z_ref[...] = jnp.dot(h_scr[rows, :], w_scr[lax.bitwise_xor(k, s)], preferred_element_type=F32)

        @pl.when((s == NCHIP - 1) & (t == nt - 1))
        def _():
            for q in range(1, NDEV):
                c_copy(q, me).wait_send()
            for q in (2, 4, 6):
                m_copy(q, k).wait_send()
            for n in (0, 1):
                for q in (2, 4, 6):
                    ici(n, q, k).wait_send()
                    d2d(n, q, _chip_of(_flip(q)), c).wait_send()
            outs = [pltpu.make_async_copy(w_scr.at[j], wfull_ref.at[:, j * SHARD_IN:(j + 1) * SHARD_IN], out_sems.at[j])
                    for j in range(NCHIP)] + [pltpu.make_async_copy(wo_scr, wofull_ref, out_sems.at[NCHIP])]
            for cp in outs:
                cp.start()
            for cp in outs:
                cp.wait()

    once = lambda s, t, k: (jnp.where(s == 0, t, nt - 1), 0)
    hbm = pl.BlockSpec(memory_space=pl.ANY)
    n_sem = n_w + n_c + 3
    return pl.pallas_call(
        kern, name="inproj_fwd",
        grid_spec=pltpu.PrefetchScalarGridSpec(
            num_scalar_prefetch=1, grid=(NCHIP, nt),
            in_specs=[pl.BlockSpec((tl, DM), once)] + [_VMEM_SPEC] * 7,
            out_specs=[pl.BlockSpec((tl, SHARD_IN), lambda s, t, k: (t, lax.bitwise_xor(k[0], s))),
                       pl.BlockSpec((tl, DM), once), hbm, hbm, hbm, hbm],
            scratch_shapes=[pltpu.VMEM((NCHIP, DM, SHARD_IN), BF16), pltpu.VMEM((NCHIP, SHARD_OUT, DM), BF16),
                            pltpu.VMEM((SEQ, DM), BF16), pltpu.VMEM((8, DM), F32), pltpu.VMEM((CS_ROWS, DM), F32),
                            pltpu.VMEM((NCHIP, CS_ROWS, SHARD_ADA), F32), pltpu.VMEM((8, DM), F32),
                            pltpu.SemaphoreType.DMA((n_sem,)), pltpu.SemaphoreType.DMA((n_sem,)),
                            pltpu.SemaphoreType.DMA((NCHIP + 1,))]),
        out_shape=[jax.ShapeDtypeStruct((SEQ, DIN), F32), jax.ShapeDtypeStruct((SEQ, DM), BF16),
                   jax.ShapeDtypeStruct((DM, DIN), BF16), jax.ShapeDtypeStruct((NCHIP, SHARD_OUT, DM), BF16),
                   jax.ShapeDtypeStruct((NCHIP, CS_ROWS, SHARD_ADA), F32), jax.ShapeDtypeStruct((CS_ROWS, DM), F32)],
        compiler_params=_cparams(("arbitrary", "arbitrary"), VMEM_BIG),
    )(chip, x, c_vec, c_ctx, w_ada, b_shard, norm_g, w_shard, wo_shard)


def ctx_fwd(ctx, cshift, cscale, norm_g, w_full):
    def kern(c_ref, sh_ref, sc_ref, g_ref, w_ref, zc_ref, hc_ref):
        hc = _modulated(c_ref[...], g_ref[...], sc_ref[...], sh_ref[...]).astype(BF16)
        hc_ref[...] = hc
        zc_ref[...] = jnp.dot(hc, w_ref[...], preferred_element_type=F32)

    return pl.pallas_call(
        kern, name="ctx_fwd", grid=(1,),
        in_specs=[pl.BlockSpec((CTX, DM), lambda i: (0, 0)), _row(DM), _row(DM), _row(DM),
                  pl.BlockSpec((DM, 2 * SHARD_IN), lambda i: (0, 1))],
        out_specs=[pl.BlockSpec((CTX, 2 * SHARD_IN), lambda i: (0, 0)),
                   pl.BlockSpec((CTX, DM), lambda i: (0, 0))],
        out_shape=[jax.ShapeDtypeStruct((CTX, 2 * SHARD_IN), F32), jax.ShapeDtypeStruct((CTX, DM), BF16)],
        compiler_params=_cparams(("arbitrary",)),
    )(ctx, cshift, cscale, norm_g, w_full)


SGU_CHUNK, SGU_PER_STEP = 128, 4


def _gelu(x):
    return 0.5 * x * (1.0 + lax.erf(x * 0.7071067811865476))


def _sgu_chunk(au, av, ag, sg, ws, bsb):
    u, v = _gelu(au), _gelu(av)
    outs = []
    for g in range(4):
        sl = slice(128 * g, 128 * (g + 1))
        mixed = mm(ws[g], _rms(v[:, sl], sg[:, sl])) + bsb[g]
        outs.append(u[:, sl] * mixed * jax.nn.silu(ag[:, sl]))
    return jnp.concatenate(outs, axis=-1)


def _sgu_specs():
    rows = SGU_CHUNK * SGU_PER_STEP
    zspec = lambda c: pl.BlockSpec((rows, 512), lambda n: (n, c))
    wspec = pl.BlockSpec((4, 128, 128), lambda n: (0, 0, 0))
    return rows, [zspec(0), zspec(1), zspec(2), _row(512), wspec, wspec]


_DR_OFF = (7, 3, -1)


def _row_valid(v, rr, j):
    return (j < 8, rr <= j < rr + 8, 4 <= j < 12)[v]


def _col_window():
    q = lax.broadcasted_iota(jnp.int32, (GRID_W, 128), 0)
    kc = lax.broadcasted_iota(jnp.int32, (GRID_W, 128), 1) % GRID_W
    c0 = jnp.clip(q - 8, 0, GRID_W - 16)
    return (kc >= c0) & (kc < c0 + 16)


def _bias_tiles(base, store):
    lo = lax.broadcasted_iota(jnp.int32, (1, 128), 1) < GRID_W
    win = _col_window()
    tiles = {}
    for v in range(3):
        for rr in range(QROWS):
            for jp in range(KROWS // 2):
                j0, j1 = 2 * jp, 2 * jp + 1
                ok0, ok1 = _row_valid(v, rr, j0), _row_valid(v, rr, j1)
                key = (j0 - rr + _DR_OFF[v], ok0, ok1) if (ok0 or ok1) else None
                if key not in tiles:
                    if key is None:
                        tiles[key] = jnp.full((GRID_W, 128), NEG_INF, F32)
                    else:
                        d0 = key[0]
                        r0 = base[d0:d0 + 1, :] if ok0 else jnp.zeros((1, 128), F32)
                        r1 = base[d0 + 1:d0 + 2, :] if ok1 else jnp.zeros((1, 128), F32)
                        y = jnp.broadcast_to(jnp.where(lo, r0, r1), (GRID_W, 128))
                        y = pltpu.roll(pltpu.roll(y, 128 - 15, 1), 0, 1, stride=1, stride_axis=0)
                        tiles[key] = jnp.where(win & jnp.where(lo, ok0, ok1), y, NEG_INF)
                store(v, slice(rr * GRID_W, (rr + 1) * GRID_W), slice(jp * 128, (jp + 1) * 128), tiles[key])


def _rpb_grad(load):
    lo = lax.broadcasted_iota(jnp.int32, (1, 128), 1) < GRID_W
    ri = lax.broadcasted_iota(jnp.int32, (GRID_W, GRID_W), 0)
    ci = lax.broadcasted_iota(jnp.int32, (GRID_W, GRID_W), 1)
    flip = (ri + ci == GRID_W - 1).astype(F32)
    groups = {}
    for v in range(3):
        for rr in range(QROWS):
            for jp in range(KROWS // 2):
                j0, j1 = 2 * jp, 2 * jp + 1
                ok0, ok1 = _row_valid(v, rr, j0), _row_valid(v, rr, j1)
                if not (ok0 or ok1):
                    continue
                g = load(v, slice(rr * GRID_W, (rr + 1) * GRID_W), slice(jp * 128, (jp + 1) * 128))
                key = (j0 - rr + _DR_OFF[v], ok0, ok1)
                groups[key] = g if key not in groups else groups[key] + g
    acc = [jnp.zeros((1, 128), F32) for _ in range(15)]
    for (d0, ok0, ok1), g in groups.items():
        g = lax.dot_general(flip, g, (((1,), (0,)), ((), ())), precision=lax.Precision.HIGHEST,
                            preferred_element_type=F32)
        g = pltpu.roll(pltpu.roll(g, 128 - 48, 1), 0, 1, stride=1, stride_axis=0)
        s = jnp.sum(g, axis=0, keepdims=True)
        if ok0:
            acc[d0] = acc[d0] + jnp.where(lo, s, 0.0)
        if ok1:
            acc[d0 + 1] = acc[d0 + 1] + jnp.where(lo, 0.0, s)
    return [row + pltpu.roll(row, GRID_W, 1) for row in acc]


def _scaled_q(q_raw, qg):
    return _pair_rms(q_raw, qg) * (HDIM ** -0.5)


def _head_lanes():
    lo = lax.broadcasted_iota(jnp.int32, (1, 2 * HDIM), 1) < HDIM
    return lo, jnp.logical_not(lo)


SOFTMAX_ROWS = 32


def _emit_interleaved(vector_work, matmul_work):
    for j in range(max(len(vector_work), len(matmul_work))):
        for work in (vector_work, matmul_work):
            if j < len(work):
                work[j]()


def _kblock(i):
    return jnp.clip(i - 1, 0, (SEQ - KBLK) // QBLK)


def _kstart(i):
    return pl.multiple_of(_kblock(i) * QBLK, QBLK)


ATTN_BLOCKS = 4
TILE_BUFFERS = 4
ATTN_STEPS = NQBLK // ATTN_BLOCKS
ATTN_ROWS = ATTN_BLOCKS * QBLK


def _bias_variant(i, b):
    if b == 0:
        return jnp.where(i == 0, 0, 1)
    if b == ATTN_BLOCKS - 1:
        return jnp.where(i == ATTN_STEPS - 1, 2, 1)
    return 1
KCOLS = QBLK


def _attn_in_specs():
    return [
        pl.BlockSpec((ATTN_ROWS, 128), lambda p, i: (i, ZQ + p)),
        pl.BlockSpec((SEQ, 128), lambda p, i: (0, ZK + p)),
        pl.BlockSpec((SEQ, 128), lambda p, i: (0, ZV + p)),
        pl.BlockSpec((ATTN_ROWS, 128), lambda p, i: (i, ZG + p)),
        pl.BlockSpec((CTX, 128), lambda p, i: (0, 2 + p)),
        pl.BlockSpec((CTX, 128), lambda p, i: (0, 6 + p)),
    ]


def _rpb_spec():
    return pl.BlockSpec((2, 15, 128), lambda p, i: (p, 0, 0))


def _prob_specs():
    return [pl.BlockSpec((2, ATTN_ROWS, KBLK), lambda p, i: (p, i, 0)),
            pl.BlockSpec((2, ATTN_ROWS, CTX), lambda p, i: (p, i, 0))]


NORM_ROWS = 1024


def _half_sums(x):
    lo = lax.broadcasted_iota(jnp.int32, (1, 2 * HDIM), 1) < HDIM
    return jnp.where(lo, jnp.sum(jnp.where(lo, x, 0.0), axis=-1, keepdims=True),
                     jnp.sum(jnp.where(lo, 0.0, x), axis=-1, keepdims=True))


def _pair_rms_bwd(x, g2, ct):
    rs = lax.rsqrt(_half_sums(x * x) / HDIM + EPS)
    y = x * rs
    dy = ct * g2
    return rs * (dy - y * (_half_sums(dy * y) / HDIM)), jnp.sum(ct * y, axis=0, keepdims=True)


def _norm_keys(k_ref, ck_ref, kg_ref, kn_scr, ckn_scr):
    def body(c, carry):
        sl = pl.ds(pl.multiple_of(c * NORM_ROWS, NORM_ROWS), NORM_ROWS)
        kn_scr[sl, :] = _pair_rms(k_ref[sl, :], kg_ref[...]).astype(BF16)
        return carry

    lax.fori_loop(0, SEQ // NORM_ROWS, body, 0)
    ckn_scr[...] = _pair_rms(ck_ref[...], kg_ref[...]).astype(BF16)


def _values_with_ones(v_ref, cv_ref, v1_scr, cv1_scr):
    for a, mine in enumerate(_head_lanes()):
        def body(c, carry):
            sl = pl.ds(pl.multiple_of(c * NORM_ROWS, NORM_ROWS), NORM_ROWS)
            v1_scr[a, sl, :] = jnp.where(mine, v_ref[sl, :], 1.0).astype(BF16)
            return carry

        lax.fori_loop(0, SEQ // NORM_ROWS, body, 0)
        cv1_scr[a] = jnp.where(mine, cv_ref[...], 1.0).astype(BF16)


def _pair_major_spec():
    return pl.BlockSpec((1, ATTN_ROWS, 128), lambda p, i: (p, i, 0))


def _normed_key_specs():
    return [pl.BlockSpec((None, SEQ, 128), lambda p, i: (p, 0, 0)), pl.BlockSpec((None, CTX, 128), lambda p, i: (p, 0, 0))]


def attn_fwd(z, zc, rpb2, qg2, kg2):
    def kern(q_ref, k_ref, v_ref, bg_ref, ck_ref, cv_ref, rpb_ref, qg_ref, kg_ref,
             ob_ref, o_ref, rden_ref, pl_ref, pc_ref, kn_ref, ckn_ref, kn_scr, ckn_scr, v1_scr, cv1_scr, s_scr,
             bias_ref):
        i = pl.program_id(1)

        @pl.when(i == 0)
        def _():
            for a in range(2):
                def store(v, tile_rows, tile_cols, tile, a=a):
                    bias_ref[v, a, tile_rows, tile_cols] = tile

                _bias_tiles(rpb_ref[a], store)
            _norm_keys(k_ref, ck_ref, kg_ref, kn_scr, ckn_scr)
            kn_ref[...] = kn_scr[...]
            ckn_ref[...] = ckn_scr[...]
            _values_with_ones(v_ref, cv_ref, v1_scr, cv1_scr)

        heads = _head_lanes()
        tiles = [(b, a) for b in range(ATTN_BLOCKS) for a in range(2)]
        rows = [slice(b * QBLK, (b + 1) * QBLK) for b in range(ATTN_BLOCKS)]
        variant = [_bias_variant(i, b) for b in range(ATTN_BLOCKS)]
        pv = [None] * len(tiles)
        qa, done = {}, {}
        latent = KBLK // KCOLS
        buf = lambda t: t % TILE_BUFFERS

        def keys(b, n):
            return pl.ds(pl.multiple_of(_kstart(ATTN_BLOCKS * i + b) + n * KCOLS, KCOLS), KCOLS)

        def score_piece(t, n):
            b, a = tiles[t]
            cols = slice(n * KCOLS, (n + 1) * KCOLS)
            if n == 0:
                if a == 0:
                    done["qn", b] = _scaled_q(q_ref[rows[b], :], qg_ref[...])
                qa[t] = jnp.where(heads[a], done["qn", b], 0.0).astype(BF16)
            if n < latent:
                s_scr[buf(t), :, cols] = mm_nt(qa[t], kn_scr[keys(b, n), :]) + bias_ref[variant[b], a, :, cols]
            else:
                s_scr[buf(t), :, cols] = mm_nt(qa[t], ckn_scr[...])

        def softmax_rows(t, r):
            b, a = tiles[t]
            rs = slice(r * SOFTMAX_ROWS, (r + 1) * SOFTMAX_ROWS)
            out_rows = slice(b * QBLK + rs.start, b * QBLK + rs.stop)
            s = s_scr[buf(t), rs, :]
            p = jnp.exp(s - jnp.max(s, axis=-1, keepdims=True)).astype(BF16)
            pl_ref[a, out_rows, :] = p[:, :KBLK]
            pc_ref[a, out_rows, :] = p[:, KBLK:]

        def value_piece(t, n):
            b, a = tiles[t]
            if n < latent:
                part = mm(pl_ref[a, rows[b], n * KCOLS:(n + 1) * KCOLS], v1_scr[a, keys(b, n), :])
            else:
                part = mm(pc_ref[a, rows[b], :], cv1_scr[a])
            pv[t] = part if pv[t] is None else pv[t] + part
            if n == latent:
                finish(t)

        def finish(t):
            b, a = tiles[t]
            r = jnp.where(heads[a], pltpu.roll(1.0 / pv[t], HDIM, 1), 0.0)
            done[t] = (pv[t] * r, r)
            if a == 1:
                o, rden = (lo + hi for lo, hi in zip(done[t - 1], done[t]))
                ob_ref[rows[b], :] = o * jax.nn.silu(bg_ref[rows[b], :])
                o_ref[0, rows[b], :] = o
                rden_ref[0, rows[b], :] = rden

        pieces = range(latent + 1)
        for n in pieces:
            score_piece(0, n)
        for t in range(len(tiles)):
            matmuls = []
            for n in pieces:
                if t + 1 < len(tiles):
                    matmuls.append(functools.partial(score_piece, t + 1, n))
                if t > 0:
                    matmuls.append(functools.partial(value_piece, t - 1, n))
            _emit_interleaved([functools.partial(softmax_rows, t, r) for r in range(QBLK // SOFTMAX_ROWS)], matmuls)
        for n in pieces:
            value_piece(len(tiles) - 1, n)

    qblk = pl.BlockSpec((ATTN_ROWS, 128), lambda p, i: (i, p))
    return pl.pallas_call(
        kern, name="attn_fwd", grid=(NPAIR, ATTN_STEPS),
        in_specs=_attn_in_specs() + [_rpb_spec(), _row(128), _row(128)],
        out_specs=[qblk, _pair_major_spec(), _pair_major_spec()] + _prob_specs() + _normed_key_specs(),
        out_shape=[jax.ShapeDtypeStruct((SEQ, 512), F32)] + [jax.ShapeDtypeStruct((NPAIR, SEQ, 128), F32)] * 2
        + [jax.ShapeDtypeStruct((HEADS, SEQ, KBLK), BF16), jax.ShapeDtypeStruct((HEADS, SEQ, CTX), BF16),
           jax.ShapeDtypeStruct((NPAIR, SEQ, 128), BF16), jax.ShapeDtypeStruct((NPAIR, CTX, 128), BF16)],
        scratch_shapes=[pltpu.VMEM((SEQ, 128), BF16), pltpu.VMEM((CTX, 128), BF16),
                        pltpu.VMEM((2, SEQ, 128), BF16), pltpu.VMEM((2, CTX, 128), BF16),
                        pltpu.VMEM((TILE_BUFFERS, QBLK, KBLK + CTX), F32),
                        pltpu.VMEM((3, 2, QBLK, KBLK), F32)],
        compiler_params=_cparams(("arbitrary", "arbitrary"), VMEM_BIG),
    )(z, z, z, z, zc, zc, rpb2, qg2, kg2)


def attn_bwd(z, zc, qg2, kg2, dcat, saved):
    def kern(q_ref, k_ref, v_ref, bg_ref, ck_ref, cv_ref, qg_ref, kg_ref, do_ref, o_ref, rden_ref, pl_ref, pc_ref,
             kn_scr, ckn_scr, dq_ref, dk_ref, dv_ref, dbg_ref, dck_ref, dcv_ref, drpb_ref, dqg_ref, dkg_ref,
             v_scr, cv_scr, dknt_scr, dvt_scr, dcknt_scr, dcvt_scr, dp_scr, ds_scr, db_ref):
        p, i = pl.program_id(0), pl.program_id(1)
        last = i == ATTN_STEPS - 1

        @pl.when(i == 0)
        def _():
            def body(c, carry):
                sl = pl.ds(pl.multiple_of(c * NORM_ROWS, NORM_ROWS), NORM_ROWS)
                v_scr[sl, :] = v_ref[sl, :].astype(BF16)
                return carry

            lax.fori_loop(0, SEQ // NORM_ROWS, body, 0)
            cv_scr[...] = cv_ref[...].astype(BF16)
            for acc in (dknt_scr, dvt_scr, dcknt_scr, dcvt_scr, db_ref):
                acc[...] = jnp.zeros_like(acc)

        @pl.when((i == 0) & (p == 0))
        def _():
            dqg_ref[...] = jnp.zeros_like(dqg_ref)
            dkg_ref[...] = jnp.zeros_like(dkg_ref)

        heads = _head_lanes()
        tiles = [(b, a) for b in range(ATTN_BLOCKS) for a in range(2)]
        rows = [slice(b * QBLK, (b + 1) * QBLK) for b in range(ATTN_BLOCKS)]
        kb = [_kblock(ATTN_BLOCKS * i + b) for b in range(ATTN_BLOCKS)]
        variant = [_bias_variant(i, b) for b in range(ATTN_BLOCKS)]
        latent = KBLK // KCOLS
        buf = lambda t: t % TILE_BUFFERS

        def keys(b, n):
            return pl.ds(pl.multiple_of((kb[b] + n) * KCOLS, KCOLS), KCOLS)

        gated = {}

        def gate_backward(b):
            bg, dout, o = bg_ref[rows[b], :], do_ref[rows[b], :], o_ref[0, rows[b], :]
            sig = jax.nn.sigmoid(bg)
            do = dout * (bg * sig)
            dbg_ref[rows[b], :] = (dout * o * (sig * (1.0 + bg * (1.0 - sig)))).astype(BF16)
            rden = rden_ref[0, rows[b], :]
            dr = do * rden
            qn = _scaled_q(q_ref[rows[b], :], qg_ref[...])
            gated[b] = (dr, dr.T.astype(BF16), qn.T.astype(BF16), do * o * rden)

        feats = [slice(a * HDIM, (a + 1) * HDIM) for a in range(2)]
        doa, doa_t, qa_t, delta = {}, {}, {}, {}
        dqn = [None] * len(tiles)

        def cols(n):
            return slice(n * KCOLS, (n + 1) * KCOLS)

        def stage_a(t, n):
            b, a = tiles[t]
            if n == 0:
                if a == 0:
                    gate_backward(b)
                dr, dr_t, qn_t, weighted = gated[b]
                doa[t] = jnp.where(heads[a], dr, 0.0).astype(BF16)
                doa_t[t] = dr_t[feats[a], :]
                qa_t[t] = qn_t[feats[a], :]
                delta[t] = jnp.sum(jnp.where(heads[a], weighted, 0.0), axis=-1, keepdims=True)
            if n < latent:
                dp_scr[buf(t), :, cols(n)] = mm_nt(doa[t], v_scr[keys(b, n), :])
                dvt_scr[kb[b] + n, feats[a], :] += mm(doa_t[t], pl_ref[a, rows[b], cols(n)])
            else:
                dp_scr[buf(t), :, cols(n)] = mm_nt(doa[t], cv_scr[...])
                dcvt_scr[feats[a], :] += mm(doa_t[t], pc_ref[a, rows[b], :])

        def stage_b(t, r):
            b, a = tiles[t]
            rs = slice(r * SOFTMAX_ROWS, (r + 1) * SOFTMAX_ROWS)
            in_rows = slice(b * QBLK + rs.start, b * QBLK + rs.stop)
            d = dp_scr[buf(t), rs, :] - delta[t][rs, :]
            ds_lat = pl_ref[a, in_rows, :].astype(F32) * d[:, :KBLK]
            ds_ctx = pc_ref[a, in_rows, :].astype(F32) * d[:, KBLK:]
            db_ref[variant[b], a, rs, :] += ds_lat
            ds_scr[buf(t), rs, :KBLK] = ds_lat.astype(BF16)
            ds_scr[buf(t), rs, KBLK:] = ds_ctx.astype(BF16)

        def stage_c(t, n):
            b, a = tiles[t]
            ds = ds_scr[buf(t), :, cols(n)]
            if n < latent:
                part = mm(ds, kn_scr[keys(b, n), :])
                dknt_scr[kb[b] + n, feats[a], :] += mm(qa_t[t], ds)
            else:
                part = mm(ds, ckn_scr[...])
                dcknt_scr[feats[a], :] += mm(qa_t[t], ds)
            dqn[t] = part if dqn[t] is None else dqn[t] + part
            if n == latent and a == 1:
                both = jnp.where(heads[0], dqn[t - 1], 0.0) + jnp.where(heads[1], dqn[t], 0.0)
                dq, dqg = jax.vjp(_scaled_q, q_ref[rows[b], :], qg_ref[...])[1](both)
                dq_ref[rows[b], :] = dq.astype(BF16)
                dqg_ref[...] += dqg

        pieces = range(latent + 1)
        for n in pieces:
            stage_a(0, n)
        for t in range(len(tiles)):
            matmuls = []
            for n in pieces:
                if t + 1 < len(tiles):
                    matmuls.append(functools.partial(stage_a, t + 1, n))
                if t > 0:
                    matmuls.append(functools.partial(stage_c, t - 1, n))
            _emit_interleaved([functools.partial(stage_b, t, r) for r in range(QBLK // SOFTMAX_ROWS)], matmuls)
        for n in pieces:
            stage_c(len(tiles) - 1, n)

        @pl.when(last)
        def _():
            eye = (lax.broadcasted_iota(jnp.int32, (KCOLS, KCOLS), 0)
                   == lax.broadcasted_iota(jnp.int32, (KCOLS, KCOLS), 1)).astype(BF16)

            def turned(x):
                hi = x.astype(BF16)
                return mm_nt(eye, hi) + mm_nt(eye, x - hi.astype(F32))

            def body(c, dkg):
                sl = pl.ds(pl.multiple_of(c * NORM_ROWS, NORM_ROWS), NORM_ROWS)
                blocks = range(NORM_ROWS // KCOLS)
                dkn = jnp.concatenate([turned(dknt_scr[c * len(blocks) + n]) for n in blocks], axis=0)
                dv = jnp.concatenate([mm_nt(eye, dvt_scr[c * len(blocks) + n]) for n in blocks], axis=0)
                dk, dg = _pair_rms_bwd(k_ref[sl, :], kg_ref[...], dkn)
                dk_ref[sl, :] = dk.astype(BF16)
                dv_ref[sl, :] = dv.astype(BF16)
                return dkg + dg

            dkg = lax.fori_loop(0, SEQ // NORM_ROWS, body, jnp.zeros((1, 128), F32))
            dck, dg = _pair_rms_bwd(ck_ref[...], kg_ref[...], dcknt_scr[...].T)
            dck_ref[...] = dck
            dcv_ref[...] = dcvt_scr[...].T
            dkg_ref[...] += dkg + dg
            for a in range(2):
                rows_of_rpb = _rpb_grad(lambda v, tile_rows, tile_cols, a=a: db_ref[v, a, tile_rows, tile_cols])
                for d, row in enumerate(rows_of_rpb):
                    drpb_ref[a, d:d + 1, :] = row

        @pl.when(last & (p == NPAIR - 1))
        def _():
            dqg_ref[...] = dqg_ref[...] + pltpu.roll(dqg_ref[...], HDIM, 1)
            dkg_ref[...] = dkg_ref[...] + pltpu.roll(dkg_ref[...], HDIM, 1)

    blk = lambda rows: pl.BlockSpec((rows, 128), lambda p, i: (0, p))
    qblk = pl.BlockSpec((ATTN_ROWS, 128), lambda p, i: (i, p))
    return pl.pallas_call(
        kern, name="attn_bwd", grid=(NPAIR, ATTN_STEPS),
        in_specs=_attn_in_specs() + [_row(128), _row(128), pl.BlockSpec((ATTN_ROWS, 128), lambda p, i: (i, 4 + p)),
                                     _pair_major_spec(), _pair_major_spec()] + _prob_specs() + _normed_key_specs(),
        out_specs=[qblk, blk(SEQ), blk(SEQ), qblk, blk(CTX), blk(CTX), _rpb_spec(), _row(128), _row(128)],
        out_shape=[jax.ShapeDtypeStruct((SEQ, 512), BF16)] * 4 + [jax.ShapeDtypeStruct((CTX, 512), F32)] * 2
        + [jax.ShapeDtypeStruct((HEADS, 15, 128), F32)]
        + [jax.ShapeDtypeStruct((1, 128), F32), jax.ShapeDtypeStruct((1, 128), F32)],
        scratch_shapes=[pltpu.VMEM((SEQ, 128), BF16), pltpu.VMEM((CTX, 128), BF16),
                        pltpu.VMEM((SEQ // KCOLS, 128, KCOLS), F32), pltpu.VMEM((SEQ // KCOLS, 128, KCOLS), F32),
                        pltpu.VMEM((128, CTX), F32), pltpu.VMEM((128, CTX), F32),
                        pltpu.VMEM((TILE_BUFFERS, QBLK, KBLK + CTX), F32),
                        pltpu.VMEM((TILE_BUFFERS, QBLK, KBLK + CTX), BF16),
                        pltpu.VMEM((3, 2, QBLK, KBLK), F32)],
        compiler_params=_cparams(("arbitrary", "arbitrary"), VMEM_BIG),
    )(z, z, z, z, zc, zc, qg2, kg2, dcat, *saved)


def outproj(z, sg, ws, bsb, out_b, x, target, gate, wo):
    tl = SGU_CHUNK * SGU_PER_STEP
    nt = SEQ // tl

    def kern(au0_ref, av0_ref, ag0_ref, au1_ref, av1_ref, ag1_ref, sg_ref, ws_ref, bs_ref, b_ref, x_ref, t_ref, g_ref,
             w_ref, loss_ref, dy_ref, dcat_ref, dg_ref, dw_ref, a_scr):
        t = pl.program_id(0)
        cur, nxt = lax.rem(t, 2), lax.rem(t + 1, 2)

        def gating(refs, slot, cn):
            sl = slice(cn * SGU_CHUNK, (cn + 1) * SGU_CHUNK)
            au_ref, av_ref, ag_ref = refs
            a_scr[slot, sl, :] = _sgu_chunk(au_ref[sl, :], av_ref[sl, :], ag_ref[sl, :], sg_ref[...], ws_ref[...],
                                            bs_ref[...]).astype(BF16)

        @pl.when(t == 0)
        def _():
            loss_ref[...] = jnp.zeros_like(loss_ref)
            dg_ref[...] = jnp.zeros_like(dg_ref)
            dw_ref[...] = jnp.zeros_like(dw_ref)
            for cn in range(SGU_PER_STEP):
                gating((au0_ref, av0_ref, ag0_ref), 0, cn)

        a, b = a_scr[cur], b_ref[...].astype(BF16)
        mix = (jnp.dot(a, w_ref[0:512, :], preferred_element_type=F32)
               + jnp.dot(b, w_ref[512:1024, :], preferred_element_type=F32))
        err = x_ref[...] + g_ref[...] * mix - t_ref[...]
        loss_ref[...] += 0.5 * jnp.sum(jnp.mean(err * err, axis=-1))
        dy = err * (1.0 / DM)
        dy_ref[...] = dy
        dg_ref[...] += jnp.sum(dy * mix, axis=0, keepdims=True)
        dmix = (g_ref[...] * dy).astype(BF16)

        def dcat_half(n):
            part = slice(512 * n, 512 * (n + 1))
            dcat_ref[:, part] = lax.dot_general(dmix, w_ref[part, :], _NT, preferred_element_type=F32)

        def dw_half(n, src):
            dw_ref[512 * n:512 * (n + 1), :] += lax.dot_general(src, dmix, (((0,), (0,)), ((), ())),
                                                                preferred_element_type=F32)

        _emit_interleaved([functools.partial(gating, (au1_ref, av1_ref, ag1_ref), nxt, cn) for cn in range(SGU_PER_STEP)],
                          [functools.partial(dcat_half, 0), functools.partial(dcat_half, 1),
                           functools.partial(dw_half, 0, a), functools.partial(dw_half, 1, b)])

    tile = lambda w: pl.BlockSpec((tl, w), lambda t: (t, 0))
    whole = pl.BlockSpec((DM, DM), lambda t: (0, 0))
    zfirst = [pl.BlockSpec((tl, 512), functools.partial(lambda c, t: (0, c), c)) for c in range(3)]
    znext = [pl.BlockSpec((tl, 512), functools.partial(lambda c, t: (jnp.minimum(t + 1, nt - 1), c), c))
             for c in range(3)]
    wspec = pl.BlockSpec((4, 128, 128), lambda t: (0, 0, 0))
    return pl.pallas_call(
        kern, name="outproj", grid=(nt,),
        in_specs=zfirst + znext + [_row(512), wspec, wspec, tile(512), tile(DM), tile(DM), _row(DM), whole],
        out_specs=[pl.BlockSpec((8, 128), lambda t: (0, 0)), tile(DM), tile(DM), _row(DM), whole],
        out_shape=[jax.ShapeDtypeStruct((8, 128), F32), jax.ShapeDtypeStruct((SEQ, DM), F32),
                   jax.ShapeDtypeStruct((SEQ, DM), F32), jax.ShapeDtypeStruct((1, DM), F32),
                   jax.ShapeDtypeStruct((DM, DM), F32)],
        scratch_shapes=[pltpu.VMEM((2, tl, 512), BF16)],
        compiler_params=_cparams(("arbitrary",), 48 * 1024 * 1024),
    )(z, z, z, z, z, z, sg, ws, bsb, out_b, x, target, gate, wo)


DZ_COLS = (("a", 0, 1536), ("q", 1536, 2048), ("k", 2048, 2560), ("v", 2560, 3072), ("g", 3072, DIN))
DZC_COLS = (("k", 2048, 2560), ("v", 2560, 3072))
_NT = (((1,), (1,)), ((), ()))


DH_SUBTILES = 2


def _dz_specs(tl):
    return [pl.BlockSpec((tl, 1536), lambda t: (t, 0))] + [pl.BlockSpec((tl, 512), lambda t: (t, 0))] * 4


def dh_bwd(dz_parts, w_full, x, dy, shift, scale, norm_g, dg_ctx):
    tl = 512
    nt = SEQ // tl

    def kern(a_ref, q_ref, k_ref, v_ref, g_ref, w_ref, x_ref, dy_ref, sh_ref, sc_ref, gn_ref, dgc_ref,
             gx_ref, dsh_ref, dsc_ref, dg_ref):
        @pl.when(pl.program_id(0) == 0)
        def _():
            dsh_ref[...] = jnp.zeros_like(dsh_ref)
            dsc_ref[...] = jnp.zeros_like(dsc_ref)
            dg_ref[...] = dgc_ref[...]

        src = dict(a=a_ref, q=q_ref, k=k_ref, v=v_ref, g=g_ref)
        for sub in range(DH_SUBTILES):
            rows = slice(sub * tl // DH_SUBTILES, (sub + 1) * tl // DH_SUBTILES)
            dh = None
            for name, c0, c1 in DZ_COLS:
                part = lax.dot_general(src[name][rows, :], w_ref[:, c0:c1], _NT, preferred_element_type=F32)
                dh = part if dh is None else dh + part
            _, vjp = jax.vjp(_modulated, x_ref[rows, :], gn_ref[...], sc_ref[...], sh_ref[...])
            dx, dg, dsc, dsh = vjp(dh)
            gx_ref[rows, :] = dy_ref[rows, :] + dx
            dg_ref[...] += dg
            dsc_ref[...] += dsc
            dsh_ref[...] += dsh

    tile = pl.BlockSpec((tl, DM), lambda t: (t, 0))
    return pl.pallas_call(
        kern, name="dh_bwd", grid=(nt,),
        in_specs=_dz_specs(tl) + [pl.BlockSpec((DM, DIN), lambda t: (0, 0)), tile, tile, _row(DM),
                                  _row(DM), _row(DM), _row(DM)],
        out_specs=[tile, _row(DM), _row(DM), _row(DM)],
        out_shape=[jax.ShapeDtypeStruct((SEQ, DM), F32)] + [jax.ShapeDtypeStruct((1, DM), F32)] * 3,
        compiler_params=_cparams(("arbitrary",), 48 * 1024 * 1024),
    )(*dz_parts, w_full, x, dy, shift, scale, norm_g, dg_ctx)


def dw_bwd(h, z, sg, ws, bsb, dcat, dz_attn, hc, dck, dcv, g_out):
    tl = SGU_CHUNK * SGU_PER_STEP
    nt = SEQ // tl
    (rhi, wi), (rho, wo) = RS_SHAPES

    def kern(h_ref, au_ref, av_ref, ag_ref, sg_ref, ws_ref, bs_ref, do_ref, q_ref, k_ref, v_ref, g_ref,
             hc_ref, dck_ref, dcv_ref, go_hbm,
             wire_i, keep_i, wire_o, keep_o, a_ref, dsg_ref, dws_ref, dbs_ref,
             acc, rcv_i, mine_o, rcv_o, load_sem, send_sems, recv_sems):
        t = pl.program_id(0)
        x, y, c = _me()
        k = 2 * x + y
        sib = _flip(1)
        half = lambda hh, rh: pl.ds(pl.multiple_of(hh * rh, rh), rh)
        load_o = pltpu.make_async_copy(go_hbm.at[:, half(c, rho), :], mine_o, load_sem)
        pair_o = _rcopy(go_hbm.at[:, half(1 - c, rho), :], rcv_o, send_sems, recv_sems, 0, sib)
        pair_i = [_rcopy(wire_i.at[j], rcv_i.at[j], send_sems, recv_sems, 1 + j, sib) for j in range(NCHIP)]

        @pl.when(t == 0)
        def _():
            load_o.start()
            pair_o.start()
            acc[...] = jnp.zeros_like(acc)
            dsg_ref[...] = jnp.zeros_like(dsg_ref)
            dws_ref[...] = jnp.zeros_like(dws_ref)
            dbs_ref[...] = jnp.zeros_like(dbs_ref)
            hct = hc_ref[...].T
            csrc = dict(k=dck_ref, v=dcv_ref)
            for name, c0, c1 in DZC_COLS:
                acc[:, c0:c1] += jnp.dot(hct, csrc[name][...].astype(BF16), preferred_element_type=F32)

        ht = h_ref[...].T
        src = dict(a=a_ref, q=q_ref, k=k_ref, v=v_ref, g=g_ref)

        def gating_backward(cn):
            sl = slice(cn * SGU_CHUNK, (cn + 1) * SGU_CHUNK)
            _, vjp = jax.vjp(_sgu_chunk, au_ref[sl, :], av_ref[sl, :], ag_ref[sl, :], sg_ref[...], ws_ref[...],
                             bs_ref[...])
            dau, dav, dag, dsg, dws, dbs = vjp(do_ref[sl, :])
            a_ref[sl, 0:512] = dau.astype(BF16)
            a_ref[sl, 512:1024] = dav.astype(BF16)
            a_ref[sl, 1024:1536] = dag.astype(BF16)
            dsg_ref[...] += dsg
            dws_ref[...] += dws
            dbs_ref[...] += dbs

        def product(name, c0, c1):
            acc[:, c0:c1] += jnp.dot(ht, src[name][...], preferred_element_type=F32)

        _emit_interleaved([functools.partial(gating_backward, cn) for cn in range(SGU_PER_STEP)],
                          [functools.partial(product, *cols) for cols in DZ_COLS[1:]])
        product(*DZ_COLS[0])

        @pl.when(t == nt - 1)
        def _():
            dbs_ref[...] = jnp.broadcast_to(jnp.sum(dbs_ref[...], axis=-1, keepdims=True), dbs_ref.shape)
            shard = lambda j: slice(j * SHARD_IN, (j + 1) * SHARD_IN)
            for j in range(NCHIP):
                wire_i[j] = acc[half(1 - c, rhi), shard(j)].astype(BF16)
                pair_i[j].start()
            load_o.wait()
            pair_o.wait_recv()
            for j in range(NCHIP):
                wire_o[j] = (mine_o[j] + rcv_o[j]).astype(BF16)
            keep_o[...] = mine_o[k] + rcv_o[k]
            mine = half(c, rhi)
            for j in range(NCHIP):
                pair_i[j].wait_recv()
                pair_i[j].wait_send()
                pair_sum = acc[mine, shard(j)] + rcv_i[j].astype(F32)
                wire_i[j] = pair_sum.astype(BF16)

                @pl.when(k == j)
                def _():
                    keep_i[...] = pair_sum
            pair_o.wait_send()

    whole = lambda *shape: pl.BlockSpec(shape, lambda t: (0,) * len(shape))
    rows, sgu_specs = _sgu_specs()
    assert rows == tl
    a_spec, *attn_specs = _dz_specs(tl)
    return pl.pallas_call(
        kern, name="dw_bwd", grid=(nt,),
        in_specs=[pl.BlockSpec((tl, DM), lambda t: (t, 0))] + sgu_specs + [pl.BlockSpec((tl, 512), lambda t: (t, 0))]
        + attn_specs + [whole(CTX, DM), whole(CTX, 512), whole(CTX, 512), pl.BlockSpec(memory_space=pl.ANY)],
        out_specs=[whole(NCHIP, rhi, wi), whole(rhi, wi), whole(NCHIP, rho, wo), whole(rho, wo),
                   a_spec, _row(512), whole(4, 128, 128), whole(4, 128, 128)],
        out_shape=[jax.ShapeDtypeStruct((NCHIP, rhi, wi), BF16), jax.ShapeDtypeStruct((rhi, wi), F32),
                   jax.ShapeDtypeStruct((NCHIP, rho, wo), BF16), jax.ShapeDtypeStruct((rho, wo), F32),
                   jax.ShapeDtypeStruct((SEQ, 1536), BF16), jax.ShapeDtypeStruct((1, 512), F32),
                   jax.ShapeDtypeStruct((4, 128, 128), F32), jax.ShapeDtypeStruct((4, 128, 128), F32)],
        scratch_shapes=[pltpu.VMEM((DM, DIN), F32), pltpu.VMEM((NCHIP, rhi, wi), BF16),
                        pltpu.VMEM((NCHIP, rho, wo), F32), pltpu.VMEM((NCHIP, rho, wo), F32),
                        pltpu.SemaphoreType.DMA(()), pltpu.SemaphoreType.DMA((1 + NCHIP,)),
                        pltpu.SemaphoreType.DMA((1 + NCHIP,))],
        compiler_params=_cparams(("arbitrary",), 60 * 1024 * 1024),
    )(h, z, z, z, sg, ws, bsb, dcat, *dz_attn, hc, dck, dcv, g_out)


def ctx_bwd(dck, dcv, w_full, ctx, cshift, cscale, norm_g):
    def kern(dck_ref, dcv_ref, w_ref, c_ref, sh_ref, sc_ref, g_ref, dsh_ref, dsc_ref, dg_ref):
        csrc = dict(k=dck_ref, v=dcv_ref)
        dhc = None
        first = DZC_COLS[0][1]
        for name, c0, c1 in DZC_COLS:
            part = lax.dot_general(csrc[name][...].astype(BF16), w_ref[:, c0 - first:c1 - first], _NT,
                                   preferred_element_type=F32)
            dhc = part if dhc is None else dhc + part
        _, vjp = jax.vjp(lambda g, sc, sh: _modulated(c_ref[...], g, sc, sh), g_ref[...], sc_ref[...], sh_ref[...])
        dg_ref[...], dsc_ref[...], dsh_ref[...] = vjp(dhc)

    whole = lambda r, c: pl.BlockSpec((r, c), lambda i: (0, 0))
    return pl.pallas_call(
        kern, name="ctx_bwd", grid=(1,),
        in_specs=[whole(CTX, 512), whole(CTX, 512), pl.BlockSpec((DM, 1024), lambda i: (0, DZC_COLS[0][1] // 1024)),
                  whole(CTX, DM), _row(DM), _row(DM), _row(DM)],
        out_specs=[_row(DM), _row(DM), _row(DM)],
        out_shape=[jax.ShapeDtypeStruct((1, DM), F32)] * 3,
        compiler_params=_cparams(("arbitrary",), 40 * 1024 * 1024),
    )(dck, dcv, w_full, ctx, cshift, cscale, norm_g)


def _lane_pad_rpb(rpb):
    r = jnp.pad(rpb, ((0, 0), (0, 0), (0, GRID_W - rpb.shape[-1])))
    return jnp.concatenate([r, r], axis=-1)


def local_step(chip, dev, x, c_vec, c_ctx, w_ada, b_shard, ctx, target, norm_g, sgu_g, w_s, b_s, q_g, k_g, rpb,
               w_in_shard, w_out_shard):
    bsb = jnp.broadcast_to(b_s[:, :, None], (4, 128, 128))
    qg2, kg2 = jnp.tile(q_g, (1, 2)), jnp.tile(k_g, (1, 2))

    z, h, w_in_full, w_out_full, mod_all, cs = inproj_fwd(chip, x, c_vec, c_ctx, w_ada, b_shard, norm_g, w_in_shard,
                                                          w_out_shard)
    mods = mod_all.transpose(1, 0, 2).reshape(CS_ROWS, 3 * DM)
    mod = lax.dynamic_slice(mods, (8 * dev, 0), (1, 3 * DM))
    shift, scale, gate = mod[:, :DM], mod[:, DM:2 * DM], mod[:, 2 * DM:]
    cshift, cscale = mods[8 * NDEV:8 * NDEV + 1, :DM], mods[8 * NDEV:8 * NDEV + 1, DM:2 * DM]
    zc, hc = ctx_fwd(ctx, cshift, cscale, norm_g, w_in_full)
    out_b, *saved = attn_fwd(z, zc, _lane_pad_rpb(rpb), qg2, kg2)
    loss8, dy, dcat, dgate, dwo = outproj(z, sgu_g, w_s, bsb, out_b, x, target, gate, w_out_full.reshape(DM, DM))
    dq, dk, dv, dbg, dck, dcv, drpb, dqg2, dkg2 = attn_bwd(z, zc, qg2, kg2, dcat, saved)
    drpb = drpb[:, :, :rpb.shape[-1]]
    dcshift, dcscale, dng_c = ctx_bwd(dck, dcv, w_in_full, ctx, cshift, cscale, norm_g)
    wire_i, keep_i, wire_o, keep_o, dz_a, dsg, dws, dbsb = dw_bwd(
        h, z, sgu_g, w_s, bsb, dcat, (dq, dk, dv, dbg), hc, dck, dcv, dwo.reshape(NCHIP, SHARD_OUT, DM))
    dz_parts = (dz_a, dq, dk, dv, dbg)
    *in_flight, token = rs_start(wire_i, wire_o)
    grad_x, dshift, dscale, dng = dh_bwd(dz_parts, w_in_full, x, dy, shift, scale, norm_g, dng_c + token[0, 0])
    got_i, got_o = rs_wait(*in_flight, dshift)
    return dict(
        loss=loss8[0:1, 0:1], grad_x=grad_x, rs=(keep_i, got_i, keep_o, got_o), cs=cs,
        dmod=jnp.concatenate([dshift, dscale, dgate], axis=-1),
        dcmod=jnp.concatenate([dcshift, dcscale, jnp.zeros((1, DM), F32)], axis=-1),
        d_norm_g=dng, d_sgu_g=dsg, d_w_s=dws, d_b_s=dbsb[:, :, 0],
        d_q_g=dqg2[:, :HDIM], d_k_g=dkg2[:, :HDIM], d_rpb=drpb)


def _me():
    return lax.axis_index("x"), lax.axis_index("y"), lax.axis_index("c")


def _flip(q):
    x, y, c = _me()
    return ((1 - x) if q & 4 else x, (1 - y) if q & 2 else y, (1 - c) if q & 1 else c)


def _chip_of(dev):
    return 2 * dev[0] + dev[1]


def _rcopy(src, dst, send_sems, recv_sems, k, dev):
    return pltpu.make_async_remote_copy(src_ref=src, dst_ref=dst, send_sem=send_sems.at[k], recv_sem=recv_sems.at[k],
                                        device_id=dev, device_id_type=MESH_ID)


_VMEM_SPEC = pl.BlockSpec(memory_space=pltpu.VMEM)
SLAB_ROWS = 80


RS_SHAPES = ((DM // 2, SHARD_IN), (SHARD_OUT // 2, DM))
_HBM_SPEC = pl.BlockSpec(memory_space=pltpu.HBM)
_SEM_SPEC = pl.BlockSpec(memory_space=pltpu.SEMAPHORE)
_IN_FLIGHT = pltpu.SideEffectType.DATAFLOW_SIDE_EFFECTING


def _rs_copies(wires, lands, send_sems, recv_sems):
    return [pltpu.make_async_remote_copy(
        src_ref=wires[n].at[_chip_of(_flip(q))], dst_ref=lands[n].at[q // 2 - 1],
        send_sem=send_sems.at[3 * n + q // 2 - 1], recv_sem=recv_sems.at[3 * n + q // 2 - 1],
        device_id=_flip(q), device_id_type=MESH_ID) for n in (0, 1) for q in (2, 4, 6)]


def rs_start(wire_i, wire_o):
    lands = [lax.empty((NCHIP - 1, rh, w), BF16) for rh, w in RS_SHAPES]

    def body(wi_ref, wo_ref, li_ref, lo_ref, send_sems, recv_sems, wi_thru, wo_thru, li_thru, lo_thru, token):
        for cp in _rs_copies((wi_ref, wo_ref), (li_ref, lo_ref), send_sems, recv_sems):
            cp.start()
        token[...] = jnp.zeros_like(token)

    hbm = lambda a: pltpu.HBM(a.shape, a.dtype)
    return pl.pallas_call(
        body, name="rs_start",
        out_shape=(pltpu.SemaphoreType.DMA((6,)), pltpu.SemaphoreType.DMA((6,)), hbm(wire_i), hbm(wire_o),
                   hbm(lands[0]), hbm(lands[1]), jax.ShapeDtypeStruct((8, 128), F32)),
        in_specs=(_HBM_SPEC,) * 4, out_specs=(_SEM_SPEC, _SEM_SPEC) + (_HBM_SPEC,) * 4 + (_VMEM_SPEC,),
        input_output_aliases={0: 2, 1: 3, 2: 4, 3: 5},
        compiler_params=pltpu.CompilerParams(has_side_effects=_IN_FLIGHT),
    )(*[pltpu.with_memory_space_constraint(a, pltpu.HBM) for a in (wire_i, wire_o, *lands)])


def rs_wait(send_sems, recv_sems, wire_i, wire_o, land_i, land_o, after):
    def body(wi_ref, wo_ref, li_ref, lo_ref, send_sems, recv_sems, after_ref, wi_dead, wo_dead, gi_ref, go_ref):
        for cp in _rs_copies((wi_ref, wo_ref), (li_ref, lo_ref), send_sems, recv_sems):
            cp.wait_send()
            cp.wait_recv()

    hbm = lambda a: pltpu.HBM(a.shape, a.dtype)
    return pl.pallas_call(
        body, name="rs_wait", out_shape=(hbm(wire_i), hbm(wire_o), hbm(land_i), hbm(land_o)),
        in_specs=(_HBM_SPEC,) * 4 + (_SEM_SPEC, _SEM_SPEC, pl.BlockSpec(memory_space=pl.ANY)),
        out_specs=(_HBM_SPEC,) * 4, input_output_aliases={0: 0, 1: 1, 2: 2, 3: 3},
        compiler_params=pltpu.CompilerParams(has_side_effects=_IN_FLIGHT),
    )(wire_i, wire_o, land_i, land_o, send_sems, recv_sems, after)[2:]


def final_reduce(keep_i, got_i, keep_o, got_o, slab, cs, w_ada, c_ctx):
    (rhi, wi), (rho, wo) = RS_SHAPES

    def kern(ki_hbm, gi_hbm, ko_hbm, go_hbm, s_ref, cs_ref, w_hbm, cc_ref,
             gin_ref, gout_ref, tot_ref, dw_ref, db_ref, dcc_ref,
             ki, gi, ko, go, w_scr, all_ref, dms_scr, parts, load_sems, send_sems, recv_sems):
        x, y, c = _me()
        k = 2 * x + y
        sib = _flip(1)
        dev = lambda d: 4 * d[0] + 2 * d[1] + d[2]
        me = dev((x, y, c))

        def slab_copy(idx, owner, to):
            return _rcopy(all_ref.at[dev(owner)], all_ref.at[dev(owner)], send_sems, recv_sems, idx, to)

        all_ref[me] = s_ref[...]
        first = [slab_copy(0, (x, y, c), sib)] + [slab_copy(q // 2, (x, y, c), _flip(q)) for q in (2, 4, 6)]
        for cp in first:
            cp.start()
        loads = [pltpu.make_async_copy(src, dst, load_sems.at[n]) for n, (src, dst) in enumerate(
            ((ki_hbm, ki), (gi_hbm, gi), (ko_hbm, ko), (go_hbm, go), (w_hbm, w_scr)))]
        for cp in loads:
            cp.start()

        shares = []
        for n, (keep, got, out) in enumerate(((ki, gi, gin_ref), (ko, go, gout_ref))):
            rh = RS_SHAPES[n][0]
            half = lambda hh, rh=rh: pl.ds(pl.multiple_of(hh * rh, rh), rh)
            loads[2 * n].wait()
            loads[2 * n + 1].wait()
            out[half(c), :] = ((keep[...] + got[0].astype(F32)) + got[1].astype(F32)) + got[2].astype(F32)
            share = _rcopy(out.at[half(c), :], out.at[half(c), :], send_sems, recv_sems, 7 + n, sib)
            share.start()
            shares.append((share, _rcopy(out.at[half(1 - c), :], out.at[half(1 - c), :], send_sems, recv_sems, 7 + n,
                                         sib)))

        passed = []
        for q in (2, 4, 6):
            slab_copy(q // 2, _flip(q), (x, y, c)).wait_recv()
            cp = slab_copy(3 + q // 2, _flip(q), sib)
            cp.start()
            passed.append(cp)
        slab_copy(0, sib, (x, y, c)).wait_recv()
        for q in (2, 4, 6):
            slab_copy(3 + q // 2, _flip(q | 1), (x, y, c)).wait_recv()
        tot = all_ref[0]
        for d in range(1, NDEV):
            tot = tot + all_ref[d]
        tot_ref[...] = tot

        pad = jnp.zeros((7, DM), F32)
        dm = [jnp.concatenate([all_ref[d, 12 + j:13 + j, :] for d in range(NDEV)] + [tot[9 + j:10 + j, :], pad], axis=0)
              for j in range(3)]
        db_ref[...] = jnp.concatenate([jnp.sum(part, axis=0, keepdims=True) for part in dm], axis=0)
        dm = jnp.concatenate(dm, axis=-1)
        for j in range(NCHIP):
            @pl.when(k == j)
            def _():
                dms_scr[...] = dm[:, j * SHARD_ADA:(j + 1) * SHARD_ADA].astype(BF16)

        a_in = jnp.concatenate([cs_ref[8 * d:8 * d + 1, :] for d in range(NDEV)]
                               + [cs_ref[8 * NDEV:8 * NDEV + 1, :], pad], axis=0)
        act = jax.nn.silu(a_in).astype(BF16)
        dms = dms_scr[...]
        dw_ref[...] = lax.dot_general(act, dms, (((0,), (0,)), ((), ())), preferred_element_type=F32)
        loads[4].wait()
        parts[k] = lax.dot_general(dms, w_scr[...].astype(BF16), (((1,), (1,)), ((), ())), preferred_element_type=F32)
        sends = [_rcopy(parts.at[k], parts.at[k], send_sems, recv_sems, 8 + q // 2, _flip(q)) for q in (2, 4, 6)]
        for cp in sends:
            cp.start()
        for q in (2, 4, 6):
            kq = _chip_of(_flip(q))
            _rcopy(parts.at[kq], parts.at[kq], send_sems, recv_sems, 8 + q // 2, _flip(q)).wait_recv()
        dact = ((parts[0] + parts[1]) + parts[2]) + parts[3]
        _, vjp = jax.vjp(jax.nn.silu, cc_ref[...])
        dcc_ref[...] = vjp(dact[8:9, :])[0]

        for share, arrival in shares:
            arrival.wait_recv()
            share.wait_send()
        for cp in first + passed + sends:
            cp.wait_send()

    any_spec = pl.BlockSpec(memory_space=pl.ANY)
    return pl.pallas_call(
        kern, name="final_reduce",
        in_specs=[any_spec] * 4 + [_VMEM_SPEC, _VMEM_SPEC, any_spec, _VMEM_SPEC], out_specs=[_VMEM_SPEC] * 6,
        out_shape=[jax.ShapeDtypeStruct((2 * rhi, wi), F32), jax.ShapeDtypeStruct((2 * rho, wo), F32),
                   jax.ShapeDtypeStruct((SLAB_ROWS, DM), F32), jax.ShapeDtypeStruct((DM, SHARD_ADA), F32),
                   jax.ShapeDtypeStruct((3, DM), F32), jax.ShapeDtypeStruct((1, DM), F32)],
        scratch_shapes=[pltpu.VMEM((rhi, wi), F32), pltpu.VMEM((NCHIP - 1, rhi, wi), BF16),
                        pltpu.VMEM((rho, wo), F32), pltpu.VMEM((NCHIP - 1, rho, wo), BF16),
                        pltpu.VMEM((DM, SHARD_ADA), F32), pltpu.VMEM((NDEV, SLAB_ROWS, DM), F32),
                        pltpu.VMEM((16, SHARD_ADA), BF16), pltpu.VMEM((NCHIP, 16, DM), F32),
                        pltpu.SemaphoreType.DMA((5,)), pltpu.SemaphoreType.DMA((12,)), pltpu.SemaphoreType.DMA((12,))],
        compiler_params=pltpu.CompilerParams(vmem_limit_bytes=40 * 1024 * 1024),
    )(keep_i, got_i, keep_o, got_o, slab, cs, w_ada, c_ctx)


def _adamw_math(w, g, m, v):
    m = B1 * m + (1.0 - B1) * g
    v = B2 * v + (1.0 - B2) * (g * g)
    m_hat = m / (1.0 - B1 ** STEP)
    v_hat = v / (1.0 - B2 ** STEP)
    return -LR * (m_hat / (jnp.sqrt(v_hat) + ADAM_EPS) + WD * w), m, v


def adamw_big(w, g, m, v, name, block_rows=256):
    rows, width = w.shape

    def kern(w_ref, g_ref, m_ref, v_ref, d_ref, nm_ref, nv_ref):
        d_ref[...], nm_ref[...], nv_ref[...] = _adamw_math(w_ref[...], g_ref[...], m_ref[...], v_ref[...])

    spec = pl.BlockSpec((block_rows, width), lambda i: (i, 0))
    return pl.pallas_call(
        kern, name=name, grid=(rows // block_rows,), in_specs=[spec] * 4, out_specs=[spec] * 3,
        out_shape=[jax.ShapeDtypeStruct((rows, width), F32)] * 3,
        compiler_params=_cparams(("arbitrary",)),
    )(w, g, m, v)


def adamw_small(quads):
    n = len(quads)

    def kern(*refs):
        ins, outs = refs[:4 * n], refs[4 * n:]
        for i in range(n):
            w, g, m, v = (r[...] for r in ins[4 * i:4 * i + 4])
            outs[3 * i][...], outs[3 * i + 1][...], outs[3 * i + 2][...] = _adamw_math(w, g, m, v)

    flat = [a for quad in quads for a in quad]
    res = pl.pallas_call(
        kern, name="adamw_small", in_specs=[_VMEM_SPEC] * (4 * n), out_specs=[_VMEM_SPEC] * (3 * n),
        out_shape=[jax.ShapeDtypeStruct(q[0].shape, F32) for q in quads for _ in range(3)],
    )(*flat)
    return [tuple(res[3 * i:3 * i + 3]) for i in range(n)]


def _rows_of(a, rows):
    flat = a.reshape(-1)
    return jnp.pad(flat, (0, rows * DM - flat.shape[0])).reshape(rows, DM)


def kernel(x, c, ctx, c_ctx, w_ada, b_ada, norm_g, w_in, sgu_norm_g, w_spatial, b_spatial, q_norm_g, k_norm_g, rpb, w_out, loss_target, m_c_ctx, m_w_ada, m_b_ada, m_norm_g, m_w_in, m_sgu_norm_g, m_w_spatial, m_b_spatial, m_q_norm_g, m_k_norm_g, m_rpb, m_w_out, v_c_ctx, v_w_ada, v_b_ada, v_norm_g, v_w_in, v_sgu_norm_g, v_w_spatial, v_b_spatial, v_q_norm_g, v_k_norm_g, v_rpb, v_w_out):
    xi, yi, ci = lax.axis_index("x"), lax.axis_index("y"), lax.axis_index("c")
    chip, dev = 2 * xi + yi, 4 * xi + 2 * yi + ci
    c_ctx2 = c_ctx.reshape(1, DM)

    b_shard = lax.dynamic_slice(b_ada, (0, chip * SHARD_ADA), (1, SHARD_ADA))
    part = local_step(chip.reshape(1).astype(jnp.int32), dev, x[0], c, c_ctx2, w_ada[0], b_shard, ctx[0], loss_target[0],
                      norm_g, sgu_norm_g, w_spatial[0], b_spatial[0], q_norm_g, k_norm_g, rpb[0], w_in[0], w_out[0])
    cs = part["cs"]

    slab = jnp.concatenate([
        part["d_norm_g"], _rows_of(part["d_sgu_g"], 1), _rows_of(part["d_b_s"], 1),
        _rows_of(jnp.concatenate([part["d_q_g"], part["d_k_g"]], axis=-1), 1), _rows_of(part["d_rpb"], 4),
        _rows_of(part["loss"], 1), _rows_of(part["dcmod"], 3), _rows_of(part["dmod"], 3), jnp.zeros((1, DM), F32),
        _rows_of(part["d_w_s"], 64)], axis=0)
    g_w_in, g_w_out, tot, g_w_ada, g_b_ada, g_c_ctx = final_reduce(*part["rs"], slab, cs, w_ada[0], c_ctx2)
    g_b_ada = g_b_ada.reshape(1, 3 * DM)

    loss = tot[8, 0]
    g_small = dict(
        c_ctx=g_c_ctx, b_ada=g_b_ada, norm_g=tot[0:1], sgu_norm_g=tot[1:2, :512], w_spatial=tot[16:80].reshape(512, 128),
        b_spatial=tot[2:3, :512].reshape(4, 128), q_norm_g=tot[3:4, :HDIM], k_norm_g=tot[3:4, HDIM:2 * HDIM],
        rpb=tot[4:8].reshape(-1)[:HEADS * 15 * 31].reshape(HEADS * 15, 31))
    shapes = dict(c_ctx=(DM,), w_ada=(1, DM, SHARD_ADA), b_ada=(1, 3 * DM), norm_g=(1, DM), w_in=(1, DM, SHARD_IN),
                  sgu_norm_g=(1, 512), w_spatial=(1, 4, 128, 128), b_spatial=(1, 4, 128), q_norm_g=(1, HDIM),
                  k_norm_g=(1, HDIM), rpb=(1, HEADS, 15, 31), w_out=(1, SHARD_OUT, DM))
    names = list(shapes)
    weights = dict(c_ctx=c_ctx, w_ada=w_ada, b_ada=b_ada, norm_g=norm_g, w_in=w_in, sgu_norm_g=sgu_norm_g,
                   w_spatial=w_spatial, b_spatial=b_spatial, q_norm_g=q_norm_g, k_norm_g=k_norm_g, rpb=rpb, w_out=w_out)
    m_in = dict(zip(names, (m_c_ctx, m_w_ada, m_b_ada, m_norm_g, m_w_in, m_sgu_norm_g, m_w_spatial, m_b_spatial,
                            m_q_norm_g, m_k_norm_g, m_rpb, m_w_out)))
    v_in = dict(zip(names, (v_c_ctx, v_w_ada, v_b_ada, v_norm_g, v_w_in, v_sgu_norm_g, v_w_spatial, v_b_spatial,
                            v_q_norm_g, v_k_norm_g, v_rpb, v_w_out)))
    grads = dict(g_small, w_ada=g_w_ada, w_in=g_w_in, w_out=g_w_out)
    upd = {}
    for n in ("w_ada", "w_in", "w_out"):
        g = grads[n]
        upd[n] = adamw_big(weights[n].reshape(g.shape), g, m_in[n].reshape(g.shape), v_in[n].reshape(g.shape),
                           "adamw_" + n)
    small = [n for n in names if n not in upd]
    res = adamw_small([(weights[n].reshape(grads[n].shape), grads[n], m_in[n].reshape(grads[n].shape),
                        v_in[n].reshape(grads[n].shape)) for n in small])
    upd.update(zip(small, res))
    out = [loss, part["grad_x"].reshape(1, SEQ, DM)]
    out += [grads[n].reshape(shapes[n]) for n in names]
    for slot in range(3):
        out += [upd[n][slot].reshape(shapes[n]) for n in names]
    return tuple(out)
```

```python
import functools

import jax
import jax.numpy as jnp
from jax import lax
from jax.experimental import pallas as pl
from jax.experimental.pallas import tpu as pltpu

F32, BF16 = jnp.float32, jnp.bfloat16
SEQ, DM, CTX, DIN = 4096, 1024, 256, 3584
NCHIP, NDEV = 4, 8
SHARD_IN = DIN // NCHIP
SHARD_ADA = 3 * DM // NCHIP
SHARD_OUT = DM // NCHIP
GRID_W = 64
QROWS = 4
KROWS = 12
QBLK, KBLK = QROWS * GRID_W, KROWS * GRID_W
NQBLK = SEQ // QBLK
HEADS, HDIM, NPAIR = 8, 64, 4
EPS = 1e-6
NEG_INF = -1e30
ZQ, ZK, ZV, ZG = 12, 16, 20, 24
LR, B1, B2, ADAM_EPS, WD, STEP = 0.001, 0.9, 0.999, 1e-08, 0.01, 10
VMEM_BIG = 56 * 1024 * 1024
MESH_ID = pl.DeviceIdType.MESH


def _dot(a, b, lhs_c, rhs_c):
    return lax.dot_general(a.astype(BF16), b.astype(BF16), (((lhs_c,), (rhs_c,)), ((), ())),
                           preferred_element_type=F32)


@jax.custom_vjp
def mm(a, b):
    return _dot(a, b, 1, 0)


@jax.custom_vjp
def mm_nt(a, b):
    return _dot(a, b, 1, 1)


@jax.custom_vjp
def mm_tn(a, b):
    return _dot(a, b, 0, 0)


mm.defvjp(lambda a, b: (mm(a, b), (a, b)), lambda r, ct: (mm_nt(ct, r[1]), mm_tn(r[0], ct)))
mm_nt.defvjp(lambda a, b: (mm_nt(a, b), (a, b)), lambda r, ct: (mm(ct, r[1]), mm_tn(ct, r[0])))
mm_tn.defvjp(lambda a, b: (mm_tn(a, b), (a, b)), lambda r, ct: (mm_nt(r[1], ct), mm(r[0], ct)))


def _rms(x, g):
    return x * lax.rsqrt(jnp.mean(x * x, axis=-1, keepdims=True) + EPS) * g


def _modulated(x, g, scale, shift):
    return _rms(x, g) * (1.0 + scale) + shift


def _pair_rms(x, g2):
    lo = lax.broadcasted_iota(jnp.int32, (1, 2 * HDIM), 1) < HDIM
    sq = x * x
    s_lo = jnp.sum(jnp.where(lo, sq, 0.0), axis=-1, keepdims=True)
    s_hi = jnp.sum(jnp.where(lo, 0.0, sq), axis=-1, keepdims=True)
    rs = jnp.where(lo, lax.rsqrt(s_lo / HDIM + EPS), lax.rsqrt(s_hi / HDIM + EPS))
    return x * rs * g2


def _cparams(sem, vmem=None):
    return pltpu.CompilerParams(dimension_semantics=sem, vmem_limit_bytes=vmem)


def _row(n):
    return pl.BlockSpec((1, n), lambda *_: (0, 0))


CS_ROWS = 8 * NDEV + 8


def _mod_part(mod_ref, row, part):
    pieces = []
    for j in range(NCHIP):
        lo, hi = max(part * DM, j * SHARD_ADA), min((part + 1) * DM, (j + 1) * SHARD_ADA)
        if lo < hi:
            pieces.append(mod_ref[j, row, lo - j * SHARD_ADA:hi - j * SHARD_ADA])
    return jnp.concatenate(pieces, axis=-1)


def inproj_fwd(chip, x, c_vec, c_ctx, w_ada, b_shard, norm_g, w_shard, wo_shard):
    tl = 1024
    nt = SEQ // tl
    halves = (DM // 2, SHARD_OUT // 2)
    n_w, n_c = 12, NDEV - 1

    def kern(k_ref, x_ref, cv_ref, cc_ref, wa_ref, b_ref, g_ref, w_ref, wo_ref,
             z_ref, h_ref, wfull_ref, wofull_ref, modall_ref, csall_ref,
             w_scr, wo_scr, h_scr, mine, cs_scr, mod_scr, shsc_scr, send_sems, recv_sems, out_sems):
        s, t = pl.program_id(0), pl.program_id(1)
        xi, yi, c = _me()
        k, me = 2 * xi + yi, 4 * xi + 2 * yi + c
        sib = _flip(1)
        rows = pl.ds(pl.multiple_of(t * tl, tl), tl)
        gathered = (w_scr, wo_scr)
        slot = lambda d: pl.ds(pl.multiple_of(8 * d, 8), 8)

        def c_copy(q, owner):
            return _rcopy(mine, cs_scr.at[slot(owner), :], send_sems, recv_sems, n_w + q - 1, _flip(q))

        def m_copy(q, chip_of_block):
            return _rcopy(mod_scr.at[chip_of_block], mod_scr.at[chip_of_block], send_sems, recv_sems,
                          n_w + n_c + q // 2 - 1, _flip(q))

        def adaln():
            first = lax.broadcasted_iota(jnp.int32, (8, DM), 0) == 0
            mine[...] = jnp.where(first, jnp.broadcast_to(cv_ref[...], (8, DM)), 0.0)
            cs_scr[slot(me), :] = mine[...]
            cs_scr[slot(NDEV), :] = jnp.where(first, jnp.broadcast_to(cc_ref[...], (8, DM)), 0.0)
            for q in range(1, NDEV):
                c_copy(q, me).start()
            wa = wa_ref[...].astype(BF16)
            for q in range(1, NDEV):
                px, py, pc = _flip(q)
                c_copy(q, 4 * px + 2 * py + pc).wait_recv()
            act = jax.nn.silu(cs_scr[...]).astype(BF16)
            mod_scr[k] = jnp.dot(act, wa, preferred_element_type=F32) + b_ref[...]
            for q in (2, 4, 6):
                m_copy(q, k).start()
            for q in (2, 4, 6):
                m_copy(q, _chip_of(_flip(q))).wait_recv()
            row = pl.ds(8 * me, 1)
            shsc_scr[0:1, :] = _mod_part(mod_scr, row, 0)
            shsc_scr[1:2, :] = _mod_part(mod_scr, row, 1)
            pltpu.sync_copy(mod_scr, modall_ref)
            pltpu.sync_copy(cs_scr, csall_ref)

        def block(n, chip_of_block, hh):
            return gathered[n].at[chip_of_block, pl.ds(pl.multiple_of(hh * halves[n], halves[n]), halves[n]), :]

        def ici(n, q, chip_of_block):
            blk = block(n, chip_of_block, c)
            return _rcopy(blk, blk, send_sems, recv_sems, 6 * n + q // 2 - 1, _flip(q))

        def d2d(n, q, chip_of_block, hh):
            blk = block(n, chip_of_block, hh)
            return _rcopy(blk, blk, send_sems, recv_sems, 6 * n + 3 + q // 2 - 1, sib)

        @pl.when((s == 0) & (t == 0))
        def _():
            adaln()
            w_scr[k] = w_ref[...].astype(BF16)
            wo_scr[k] = wo_ref[...].astype(BF16)
            for q in (2, 4, 6):
                ici(0, q, k).start()
                ici(1, q, k).start()

        for sweep in (1, 2, 3):
            @pl.when((s == sweep) & (t == 0))
            def _():
                q = 2 * sweep
                src = _chip_of(_flip(q))
                for n in (0, 1):
                    ici(n, q, src).wait_recv()
                    d2d(n, q, src, c).start()
                for n in (0, 1):
                    d2d(n, q, src, 1 - c).wait_recv()

        @pl.when(s == 0)
        def _():
            hb = _modulated(x_ref[...], g_ref[...], shsc_scr[1:2, :], shsc_scr[0:1, :]).astype(BF16)
            h_scr[rows, :] = hb
            h_ref[...] = hb

        z_ref[...] = jnp.dot(h_scr[rows, :], w_scr[lax.bitwise_xor(k, s)], preferred_element_type=F32)

        @pl.when((s == NCHIP - 1) & (t == nt - 1))
        def _():
            for q in range(1, NDEV):
                c_copy(q, me).wait_send()
            for q in (2, 4, 6):
                m_copy(q, k).wait_send()
            for n in (0, 1):
                for q in (2, 4, 6):
                    ici(n, q, k).wait_send()
                    d2d(n, q, _chip_of(_flip(q)), c).wait_send()
            outs = [pltpu.make_async_copy(w_scr.at[j], wfull_ref.at[:, j * SHARD_IN:(j + 1) * SHARD_IN], out_sems.at[j])
                    for j in range(NCHIP)] + [pltpu.make_async_copy(wo_scr, wofull_ref, out_sems.at[NCHIP])]
            for cp in outs:
                cp.start()
            for cp in outs:
                cp.wait()

    once = lambda s, t, k: (jnp.where(s == 0, t, nt - 1), 0)
    hbm = pl.BlockSpec(memory_space=pl.ANY)
    n_sem = n_w + n_c + 3
    return pl.pallas_call(
        kern, name="inproj_fwd",
        grid_spec=pltpu.PrefetchScalarGridSpec(
            num_scalar_prefetch=1, grid=(NCHIP, nt),
            in_specs=[pl.BlockSpec((tl, DM), once)] + [_VMEM_SPEC] * 7,
            out_specs=[pl.BlockSpec((tl, SHARD_IN), lambda s, t, k: (t, lax.bitwise_xor(k[0], s))),
                       pl.BlockSpec((tl, DM), once), hbm, hbm, hbm, hbm],
            scratch_shapes=[pltpu.VMEM((NCHIP, DM, SHARD_IN), BF16), pltpu.VMEM((NCHIP, SHARD_OUT, DM), BF16),
                            pltpu.VMEM((SEQ, DM), BF16), pltpu.VMEM((8, DM), F32), pltpu.VMEM((CS_ROWS, DM), F32),
                            pltpu.VMEM((NCHIP, CS_ROWS, SHARD_ADA), F32), pltpu.VMEM((8, DM), F32),
                            pltpu.SemaphoreType.DMA((n_sem,)), pltpu.SemaphoreType.DMA((n_sem,)),
                            pltpu.SemaphoreType.DMA((NCHIP + 1,))]),
        out_shape=[jax.ShapeDtypeStruct((SEQ, DIN), F32), jax.ShapeDtypeStruct((SEQ, DM), BF16),
                   jax.ShapeDtypeStruct((DM, DIN), BF16), jax.ShapeDtypeStruct((NCHIP, SHARD_OUT, DM), BF16),
                   jax.ShapeDtypeStruct((NCHIP, CS_ROWS, SHARD_ADA), F32), jax.ShapeDtypeStruct((CS_ROWS, DM), F32)],
        compiler_params=_cparams(("arbitrary", "arbitrary"), VMEM_BIG),
    )(chip, x, c_vec, c_ctx, w_ada, b_shard, norm_g, w_shard, wo_shard)


def ctx_fwd(ctx, cshift, cscale, norm_g, w_full):
    def kern(c_ref, sh_ref, sc_ref, g_ref, w_ref, zc_ref, hc_ref):
        hc = _modulated(c_ref[...], g_ref[...], sc_ref[...], sh_ref[...]).astype(BF16)
        hc_ref[...] = hc
        zc_ref[...] = jnp.dot(hc, w_ref[...], preferred_element_type=F32)

    return pl.pallas_call(
        kern, name="ctx_fwd", grid=(1,),
        in_specs=[pl.BlockSpec((CTX, DM), lambda i: (0, 0)), _row(DM), _row(DM), _row(DM),
                  pl.BlockSpec((DM, 2 * SHARD_IN), lambda i: (0, 1))],
        out_specs=[pl.BlockSpec((CTX, 2 * SHARD_IN), lambda i: (0, 0)),
                   pl.BlockSpec((CTX, DM), lambda i: (0, 0))],
        out_shape=[jax.ShapeDtypeStruct((CTX, 2 * SHARD_IN), F32), jax.ShapeDtypeStruct((CTX, DM), BF16)],
        compiler_params=_cparams(("arbitrary",)),
    )(ctx, cshift, cscale, norm_g, w_full)


SGU_CHUNK, SGU_PER_STEP = 128, 4


def _gelu(x):
    return 0.5 * x * (1.0 + lax.erf(x * 0.7071067811865476))


def _sgu_chunk(au, av, ag, sg, ws, bsb):
    u, v = _gelu(au), _gelu(av)
    outs = []
    for g in range(4):
        sl = slice(128 * g, 128 * (g + 1))
        mixed = mm(ws[g], _rms(v[:, sl], sg[:, sl])) + bsb[g]
        outs.append(u[:, sl] * mixed * jax.nn.silu(ag[:, sl]))
    return jnp.concatenate(outs, axis=-1)


def _sgu_specs():
    rows = SGU_CHUNK * SGU_PER_STEP
    zspec = lambda c: pl.BlockSpec((rows, 512), lambda n: (n, c))
    wspec = pl.BlockSpec((4, 128, 128), lambda n: (0, 0, 0))
    return rows, [zspec(0), zspec(1), zspec(2), _row(512), wspec, wspec]


_DR_OFF = (7, 3, -1)


def _row_valid(v, rr, j):
    return (j < 8, rr <= j < rr + 8, 4 <= j < 12)[v]


def _col_window():
    q = lax.broadcasted_iota(jnp.int32, (GRID_W, 128), 0)
    kc = lax.broadcasted_iota(jnp.int32, (GRID_W, 128), 1) % GRID_W
    c0 = jnp.clip(q - 8, 0, GRID_W - 16)
    return (kc >= c0) & (kc < c0 + 16)


def _bias_tiles(base, store):
    lo = lax.broadcasted_iota(jnp.int32, (1, 128), 1) < GRID_W
    win = _col_window()
    tiles = {}
    for v in range(3):
        for rr in range(QROWS):
            for jp in range(KROWS // 2):
                j0, j1 = 2 * jp, 2 * jp + 1
                ok0, ok1 = _row_valid(v, rr, j0), _row_valid(v, rr, j1)
                key = (j0 - rr + _DR_OFF[v], ok0, ok1) if (ok0 or ok1) else None
                if key not in tiles:
                    if key is None:
                        tiles[key] = jnp.full((GRID_W, 128), NEG_INF, F32)
                    else:
                        d0 = key[0]
                        r0 = base[d0:d0 + 1, :] if ok0 else jnp.zeros((1, 128), F32)
                        r1 = base[d0 + 1:d0 + 2, :] if ok1 else jnp.zeros((1, 128), F32)
                        y = jnp.broadcast_to(jnp.where(lo, r0, r1), (GRID_W, 128))
                        y = pltpu.roll(pltpu.roll(y, 128 - 15, 1), 0, 1, stride=1, stride_axis=0)
                        tiles[key] = jnp.where(win & jnp.where(lo, ok0, ok1), y, NEG_INF)
                store(v, slice(rr * GRID_W, (rr + 1) * GRID_W), slice(jp * 128, (jp + 1) * 128), tiles[key])


def _rpb_grad(load):
    lo = lax.broadcasted_iota(jnp.int32, (1, 128), 1) < GRID_W
    ri = lax.broadcasted_iota(jnp.int32, (GRID_W, GRID_W), 0)
    ci = lax.broadcasted_iota(jnp.int32, (GRID_W, GRID_W), 1)
    flip = (ri + ci == GRID_W - 1).astype(F32)
    groups = {}
    for v in range(3):
        for rr in range(QROWS):
            for jp in range(KROWS // 2):
                j0, j1 = 2 * jp, 2 * jp + 1
                ok0, ok1 = _row_valid(v, rr, j0), _row_valid(v, rr, j1)
                if not (ok0 or ok1):
                    continue
                g = load(v, slice(rr * GRID_W, (rr + 1) * GRID_W), slice(jp * 128, (jp + 1) * 128))
                key = (j0 - rr + _DR_OFF[v], ok0, ok1)
                groups[key] = g if key not in groups else groups[key] + g
    acc = [jnp.zeros((1, 128), F32) for _ in range(15)]
    for (d0, ok0, ok1), g in groups.items():
        g = lax.dot_general(flip, g, (((1,), (0,)), ((), ())), precision=lax.Precision.HIGHEST,
                            preferred_element_type=F32)
        g = pltpu.roll(pltpu.roll(g, 128 - 48, 1), 0, 1, stride=1, stride_axis=0)
        s = jnp.sum(g, axis=0, keepdims=True)
        if ok0:
            acc[d0] = acc[d0] + jnp.where(lo, s, 0.0)
        if ok1:
            acc[d0 + 1] = acc[d0 + 1] + jnp.where(lo, 0.0, s)
    return [row + pltpu.roll(row, GRID_W, 1) for row in acc]


def _scaled_q(q_raw, qg):
    return _pair_rms(q_raw, qg) * (HDIM ** -0.5)


def _head_lanes():
    lo = lax.broadcasted_iota(jnp.int32, (1, 2 * HDIM), 1) < HDIM
    return lo, jnp.logical_not(lo)


SOFTMAX_ROWS = 32


def _emit_interleaved(vector_work, matmul_work):
    for j in range(max(len(vector_work), len(matmul_work))):
        for work in (vector_work, matmul_work):
            if j < len(work):
                work[j]()


def _kblock(i):
    return jnp.clip(i - 1, 0, (SEQ - KBLK) // QBLK)


def _kstart(i):
    return pl.multiple_of(_kblock(i) * QBLK, QBLK)


ATTN_BLOCKS = 4
TILE_BUFFERS = 4
ATTN_STEPS = NQBLK // ATTN_BLOCKS
ATTN_ROWS = ATTN_BLOCKS * QBLK


def _bias_variant(i, b):
    if b == 0:
        return jnp.where(i == 0, 0, 1)
    if b == ATTN_BLOCKS - 1:
        return jnp.where(i == ATTN_STEPS - 1, 2, 1)
    return 1
KCOLS = QBLK


def _attn_in_specs():
    return [
        pl.BlockSpec((ATTN_ROWS, 128), lambda p, i: (i, ZQ + p)),
        pl.BlockSpec((SEQ, 128), lambda p, i: (0, ZK + p)),
        pl.BlockSpec((SEQ, 128), lambda p, i: (0, ZV + p)),
        pl.BlockSpec((ATTN_ROWS, 128), lambda p, i: (i, ZG + p)),
        pl.BlockSpec((CTX, 128), lambda p, i: (0, 2 + p)),
        pl.BlockSpec((CTX, 128), lambda p, i: (0, 6 + p)),
    ]


def _rpb_spec():
    return pl.BlockSpec((2, 15, 128), lambda p, i: (p, 0, 0))


def _prob_specs():
    return [pl.BlockSpec((2, ATTN_ROWS, KBLK), lambda p, i: (p, i, 0)),
            pl.BlockSpec((2, ATTN_ROWS, CTX), lambda p, i: (p, i, 0))]


NORM_ROWS = 2048


def _half_sums(x):
    lo = lax.broadcasted_iota(jnp.int32, (1, 2 * HDIM), 1) < HDIM
    return jnp.where(lo, jnp.sum(jnp.where(lo, x, 0.0), axis=-1, keepdims=True),
                     jnp.sum(jnp.where(lo, 0.0, x), axis=-1, keepdims=True))


def _pair_rms_bwd(x, g2, ct):
    rs = lax.rsqrt(_half_sums(x * x) / HDIM + EPS)
    y = x * rs
    dy = ct * g2
    return rs * (dy - y * (_half_sums(dy * y) / HDIM)), jnp.sum(ct * y, axis=0, keepdims=True)


def _norm_keys(k_ref, ck_ref, kg_ref, kn_scr, ckn_scr):
    def body(c, carry):
        sl = pl.ds(pl.multiple_of(c * NORM_ROWS, NORM_ROWS), NORM_ROWS)
        kn_scr[sl, :] = _pair_rms(k_ref[sl, :], kg_ref[...]).astype(BF16)
        return carry

    lax.fori_loop(0, SEQ // NORM_ROWS, body, 0)
    ckn_scr[...] = _pair_rms(ck_ref[...], kg_ref[...]).astype(BF16)


def _values_with_ones(v_ref, cv_ref, v1_scr, cv1_scr):
    for a, mine in enumerate(_head_lanes()):
        def body(c, carry):
            sl = pl.ds(pl.multiple_of(c * NORM_ROWS, NORM_ROWS), NORM_ROWS)
            v1_scr[a, sl, :] = jnp.where(mine, v_ref[sl, :], 1.0).astype(BF16)
            return carry

        lax.fori_loop(0, SEQ // NORM_ROWS, body, 0)
        cv1_scr[a] = jnp.where(mine, cv_ref[...], 1.0).astype(BF16)


def _pair_major_spec():
    return pl.BlockSpec((1, ATTN_ROWS, 128), lambda p, i: (p, i, 0))


def _normed_key_specs():
    return [pl.BlockSpec((None, SEQ, 128), lambda p, i: (p, 0, 0)), pl.BlockSpec((None, CTX, 128), lambda p, i: (p, 0, 0))]


def attn_fwd(z, zc, rpb2, qg2, kg2):
    def kern(q_ref, k_ref, v_ref, bg_ref, ck_ref, cv_ref, rpb_ref, qg_ref, kg_ref,
             ob_ref, o_ref, rden_ref, pl_ref, pc_ref, kn_ref, ckn_ref, kn_scr, ckn_scr, v1_scr, cv1_scr, s_scr,
             bias_ref):
        i = pl.program_id(1)

        @pl.when(i == 0)
        def _():
            for a in range(2):
                def store(v, tile_rows, tile_cols, tile, a=a):
                    bias_ref[v, a, tile_rows, tile_cols] = tile

                _bias_tiles(rpb_ref[a], store)
            _norm_keys(k_ref, ck_ref, kg_ref, kn_scr, ckn_scr)
            kn_ref[...] = kn_scr[...]
            ckn_ref[...] = ckn_scr[...]
            _values_with_ones(v_ref, cv_ref, v1_scr, cv1_scr)

        heads = _head_lanes()
        tiles = [(b, a) for b in range(ATTN_BLOCKS) for a in range(2)]
        rows = [slice(b * QBLK, (b + 1) * QBLK) for b in range(ATTN_BLOCKS)]
        variant = [_bias_variant(i, b) for b in range(ATTN_BLOCKS)]
        pv = [None] * len(tiles)
        qa, done = {}, {}
        latent = KBLK // KCOLS
        buf = lambda t: t % TILE_BUFFERS

        def keys(b, n):
            return pl.ds(pl.multiple_of(_kstart(ATTN_BLOCKS * i + b) + n * KCOLS, KCOLS), KCOLS)

        def score_piece(t, n):
            b, a = tiles[t]
            cols = slice(n * KCOLS, (n + 1) * KCOLS)
            if n == 0:
                if a == 0:
                    done["qn", b] = _scaled_q(q_ref[rows[b], :], qg_ref[...])
                qa[t] = jnp.where(heads[a], done["qn", b], 0.0).astype(BF16)
            if n < latent:
                s_scr[buf(t), :, cols] = mm_nt(qa[t], kn_scr[keys(b, n), :]) + bias_ref[variant[b], a, :, cols]
            else:
                s_scr[buf(t), :, cols] = mm_nt(qa[t], ckn_scr[...])

        def softmax_rows(t, r):
            b, a = tiles[t]
            rs = slice(r * SOFTMAX_ROWS, (r + 1) * SOFTMAX_ROWS)
            out_rows = slice(b * QBLK + rs.start, b * QBLK + rs.stop)
            s = s_scr[buf(t), rs, :]
            p = jnp.exp(s - jnp.max(s, axis=-1, keepdims=True)).astype(BF16)
            pl_ref[a, out_rows, :] = p[:, :KBLK]
            pc_ref[a, out_rows, :] = p[:, KBLK:]

        def value_piece(t, n):
            b, a = tiles[t]
            if n < latent:
                part = mm(pl_ref[a, rows[b], n * KCOLS:(n + 1) * KCOLS], v1_scr[a, keys(b, n), :])
            else:
                part = mm(pc_ref[a, rows[b], :], cv1_scr[a])
            pv[t] = part if pv[t] is None else pv[t] + part
            if n == latent:
                finish(t)

        def finish(t):
            b, a = tiles[t]
            r = jnp.where(heads[a], pltpu.roll(1.0 / pv[t], HDIM, 1), 0.0)
            done[t] = (pv[t] * r, r)
            if a == 1:
                o, rden = (lo + hi for lo, hi in zip(done[t - 1], done[t]))
                ob_ref[rows[b], :] = o * jax.nn.silu(bg_ref[rows[b], :])
                o_ref[0, rows[b], :] = o
                rden_ref[0, rows[b], :] = rden

        pieces = range(latent + 1)
        for n in pieces:
            score_piece(0, n)
        for t in range(len(tiles)):
            matmuls = []
            for n in pieces:
                if t + 1 < len(tiles):
                    matmuls.append(functools.partial(score_piece, t + 1, n))
                if t > 0:
                    matmuls.append(functools.partial(value_piece, t - 1, n))
            _emit_interleaved([functools.partial(softmax_rows, t, r) for r in range(QBLK // SOFTMAX_ROWS)], matmuls)
        for n in pieces:
            value_piece(len(tiles) - 1, n)

    qblk = pl.BlockSpec((ATTN_ROWS, 128), lambda p, i: (i, p))
    return pl.pallas_call(
        kern, name="attn_fwd", grid=(NPAIR, ATTN_STEPS),
        in_specs=_attn_in_specs() + [_rpb_spec(), _row(128), _row(128)],
        out_specs=[qblk, _pair_major_spec(), _pair_major_spec()] + _prob_specs() + _normed_key_specs(),
        out_shape=[jax.ShapeDtypeStruct((SEQ, 512), F32)] + [jax.ShapeDtypeStruct((NPAIR, SEQ, 128), F32)] * 2
        + [jax.ShapeDtypeStruct((HEADS, SEQ, KBLK), BF16), jax.ShapeDtypeStruct((HEADS, SEQ, CTX), BF16),
           jax.ShapeDtypeStruct((NPAIR, SEQ, 128), BF16), jax.ShapeDtypeStruct((NPAIR, CTX, 128), BF16)],
        scratch_shapes=[pltpu.VMEM((SEQ, 128), BF16), pltpu.VMEM((CTX, 128), BF16),
                        pltpu.VMEM((2, SEQ, 128), BF16), pltpu.VMEM((2, CTX, 128), BF16),
                        pltpu.VMEM((TILE_BUFFERS, QBLK, KBLK + CTX), F32),
                        pltpu.VMEM((3, 2, QBLK, KBLK), F32)],
        compiler_params=_cparams(("arbitrary", "arbitrary"), VMEM_BIG),
    )(z, z, z, z, zc, zc, rpb2, qg2, kg2)


def attn_bwd(z, zc, qg2, kg2, dcat, saved):
    def kern(q_ref, k_ref, v_ref, bg_ref, ck_ref, cv_ref, qg_ref, kg_ref, do_ref, o_ref, rden_ref, pl_ref, pc_ref,
             kn_scr, ckn_scr, dq_ref, dk_ref, dv_ref, dbg_ref, dck_ref, dcv_ref, drpb_ref, dqg_ref, dkg_ref,
             v_scr, cv_scr, dknt_scr, dvt_scr, dcknt_scr, dcvt_scr, dp_scr, ds_scr, db_ref):
        p, i = pl.program_id(0), pl.program_id(1)
        last = i == ATTN_STEPS - 1

        @pl.when(i == 0)
        def _():
            def body(c, carry):
                sl = pl.ds(pl.multiple_of(c * NORM_ROWS, NORM_ROWS), NORM_ROWS)
                v_scr[sl, :] = v_ref[sl, :].astype(BF16)
                return carry

            lax.fori_loop(0, SEQ // NORM_ROWS, body, 0)
            cv_scr[...] = cv_ref[...].astype(BF16)
            for acc in (dknt_scr, dvt_scr, dcknt_scr, dcvt_scr, db_ref):
                acc[...] = jnp.zeros_like(acc)

        @pl.when((i == 0) & (p == 0))
        def _():
            dqg_ref[...] = jnp.zeros_like(dqg_ref)
            dkg_ref[...] = jnp.zeros_like(dkg_ref)

        heads = _head_lanes()
        tiles = [(b, a) for b in range(ATTN_BLOCKS) for a in range(2)]
        rows = [slice(b * QBLK, (b + 1) * QBLK) for b in range(ATTN_BLOCKS)]
        kb = [_kblock(ATTN_BLOCKS * i + b) for b in range(ATTN_BLOCKS)]
        variant = [_bias_variant(i, b) for b in range(ATTN_BLOCKS)]
        latent = KBLK // KCOLS
        buf = lambda t: t % TILE_BUFFERS

        def keys(b, n):
            return pl.ds(pl.multiple_of((kb[b] + n) * KCOLS, KCOLS), KCOLS)

        gated = {}

        def gate_backward(b):
            bg, dout, o = bg_ref[rows[b], :], do_ref[rows[b], :], o_ref[0, rows[b], :]
            sig = jax.nn.sigmoid(bg)
            do = dout * (bg * sig)
            dbg_ref[rows[b], :] = (dout * o * (sig * (1.0 + bg * (1.0 - sig)))).astype(BF16)
            rden = rden_ref[0, rows[b], :]
            dr = do * rden
            qn = _scaled_q(q_ref[rows[b], :], qg_ref[...])
            gated[b] = (dr, dr.T.astype(BF16), qn.T.astype(BF16), do * o * rden)

        feats = [slice(a * HDIM, (a + 1) * HDIM) for a in range(2)]
        doa, doa_t, qa_t, delta = {}, {}, {}, {}
        dqn = [None] * len(tiles)

        def cols(n):
            return slice(n * KCOLS, (n + 1) * KCOLS)

        def stage_a(t, n):
            b, a = tiles[t]
            if n == 0:
                if a == 0:
                    gate_backward(b)
                dr, dr_t, qn_t, weighted = gated[b]
                doa[t] = jnp.where(heads[a], dr, 0.0).astype(BF16)
                doa_t[t] = dr_t[feats[a], :]
                qa_t[t] = qn_t[feats[a], :]
                delta[t] = jnp.sum(jnp.where(heads[a], weighted, 0.0), axis=-1, keepdims=True)
            if n < latent:
                dp_scr[buf(t), :, cols(n)] = mm_nt(doa[t], v_scr[keys(b, n), :])
                dvt_scr[kb[b] + n, feats[a], :] += mm(doa_t[t], pl_ref[a, rows[b], cols(n)])
            else:
                dp_scr[buf(t), :, cols(n)] = mm_nt(doa[t], cv_scr[...])
                dcvt_scr[feats[a], :] += mm(doa_t[t], pc_ref[a, rows[b], :])

        def stage_b(t, r):
            b, a = tiles[t]
            rs = slice(r * SOFTMAX_ROWS, (r + 1) * SOFTMAX_ROWS)
            in_rows = slice(b * QBLK + rs.start, b * QBLK + rs.stop)
            d = dp_scr[buf(t), rs, :] - delta[t][rs, :]
            ds_lat = pl_ref[a, in_rows, :].astype(F32) * d[:, :KBLK]
            ds_ctx = pc_ref[a, in_rows, :].astype(F32) * d[:, KBLK:]
            db_ref[variant[b], a, rs, :] += ds_lat
            ds_scr[buf(t), rs, :KBLK] = ds_lat.astype(BF16)
            ds_scr[buf(t), rs, KBLK:] = ds_ctx.astype(BF16)

        def stage_c(t, n):
            b, a = tiles[t]
            ds = ds_scr[buf(t), :, cols(n)]
            if n < latent:
                part = mm(ds, kn_scr[keys(b, n), :])
                dknt_scr[kb[b] + n, feats[a], :] += mm(qa_t[t], ds)
            else:
                part = mm(ds, ckn_scr[...])
                dcknt_scr[feats[a], :] += mm(qa_t[t], ds)
            dqn[t] = part if dqn[t] is None else dqn[t] + part
            if n == latent and a == 1:
                both = jnp.where(heads[0], dqn[t - 1], 0.0) + jnp.where(heads[1], dqn[t], 0.0)
                dq, dqg = jax.vjp(_scaled_q, q_ref[rows[b], :], qg_ref[...])[1](both)
                dq_ref[rows[b], :] = dq.astype(BF16)
                dqg_ref[...] += dqg

        pieces = range(latent + 1)
        for n in pieces:
            stage_a(0, n)
        for t in range(len(tiles)):
            matmuls = []
            for n in pieces:
                if t + 1 < len(tiles):
                    matmuls.append(functools.partial(stage_a, t + 1, n))
                if t > 0:
                    matmuls.append(functools.partial(stage_c, t - 1, n))
            _emit_interleaved([functools.partial(stage_b, t, r) for r in range(QBLK // SOFTMAX_ROWS)], matmuls)
        for n in pieces:
            stage_c(len(tiles) - 1, n)

        @pl.when(last)
        def _():
            eye = (lax.broadcasted_iota(jnp.int32, (KCOLS, KCOLS), 0)
                   == lax.broadcasted_iota(jnp.int32, (KCOLS, KCOLS), 1)).astype(BF16)

            def turned(x):
                hi = x.astype(BF16)
                return mm_nt(eye, hi) + mm_nt(eye, x - hi.astype(F32))

            def body(c, dkg):
                sl = pl.ds(pl.multiple_of(c * NORM_ROWS, NORM_ROWS), NORM_ROWS)
                blocks = range(NORM_ROWS // KCOLS)
                dkn = jnp.concatenate([turned(dknt_scr[c * len(blocks) + n]) for n in blocks], axis=0)
                dv = jnp.concatenate([mm_nt(eye, dvt_scr[c * len(blocks) + n]) for n in blocks], axis=0)
                dk, dg = _pair_rms_bwd(k_ref[sl, :], kg_ref[...], dkn)
                dk_ref[sl, :] = dk.astype(BF16)
                dv_ref[sl, :] = dv.astype(BF16)
                return dkg + dg

            dkg = lax.fori_loop(0, SEQ // NORM_ROWS, body, jnp.zeros((1, 128), F32))
            dck, dg = _pair_rms_bwd(ck_ref[...], kg_ref[...], dcknt_scr[...].T)
            dck_ref[...] = dck
            dcv_ref[...] = dcvt_scr[...].T
            dkg_ref[...] += dkg + dg
            for a in range(2):
                rows_of_rpb = _rpb_grad(lambda v, tile_rows, tile_cols, a=a: db_ref[v, a, tile_rows, tile_cols])
                for d, row in enumerate(rows_of_rpb):
                    drpb_ref[a, d:d + 1, :] = row

        @pl.when(last & (p == NPAIR - 1))
        def _():
            dqg_ref[...] = dqg_ref[...] + pltpu.roll(dqg_ref[...], HDIM, 1)
            dkg_ref[...] = dkg_ref[...] + pltpu.roll(dkg_ref[...], HDIM, 1)

    blk = lambda rows: pl.BlockSpec((rows, 128), lambda p, i: (0, p))
    qblk = pl.BlockSpec((ATTN_ROWS, 128), lambda p, i: (i, p))
    return pl.pallas_call(
        kern, name="attn_bwd", grid=(NPAIR, ATTN_STEPS),
        in_specs=_attn_in_specs() + [_row(128), _row(128), pl.BlockSpec((ATTN_ROWS, 128), lambda p, i: (i, 4 + p)),
                                     _pair_major_spec(), _pair_major_spec()] + _prob_specs() + _normed_key_specs(),
        out_specs=[qblk, blk(SEQ), blk(SEQ), qblk, blk(CTX), blk(CTX), _rpb_spec(), _row(128), _row(128)],
        out_shape=[jax.ShapeDtypeStruct((SEQ, 512), BF16)] * 4 + [jax.ShapeDtypeStruct((CTX, 512), F32)] * 2
        + [jax.ShapeDtypeStruct((HEADS, 15, 128), F32)]
        + [jax.ShapeDtypeStruct((1, 128), F32), jax.ShapeDtypeStruct((1, 128), F32)],
        scratch_shapes=[pltpu.VMEM((SEQ, 128), BF16), pltpu.VMEM((CTX, 128), BF16),
                        pltpu.VMEM((SEQ // KCOLS, 128, KCOLS), F32), pltpu.VMEM((SEQ // KCOLS, 128, KCOLS), F32),
                        pltpu.VMEM((128, CTX), F32), pltpu.VMEM((128, CTX), F32),
                        pltpu.VMEM((TILE_BUFFERS, QBLK, KBLK + CTX), F32),
                        pltpu.VMEM((TILE_BUFFERS, QBLK, KBLK + CTX), BF16),
                        pltpu.VMEM((3, 2, QBLK, KBLK), F32)],
        compiler_params=_cparams(("arbitrary", "arbitrary"), VMEM_BIG),
    )(z, z, z, z, zc, zc, qg2, kg2, dcat, *saved)


def outproj(z, sg, ws, bsb, out_b, x, target, gate, wo):
    tl = SGU_CHUNK * SGU_PER_STEP
    nt = SEQ // tl

    def kern(au0_ref, av0_ref, ag0_ref, au1_ref, av1_ref, ag1_ref, sg_ref, ws_ref, bs_ref, b_ref, x_ref, t_ref, g_ref,
             w_ref, loss_ref, dy_ref, dcat_ref, dg_ref, dw_ref, a_scr):
        t = pl.program_id(0)
        cur, nxt = lax.rem(t, 2), lax.rem(t + 1, 2)

        def gating(refs, slot, cn):
            sl = slice(cn * SGU_CHUNK, (cn + 1) * SGU_CHUNK)
            au_ref, av_ref, ag_ref = refs
            a_scr[slot, sl, :] = _sgu_chunk(au_ref[sl, :], av_ref[sl, :], ag_ref[sl, :], sg_ref[...], ws_ref[...],
                                            bs_ref[...]).astype(BF16)

        @pl.when(t == 0)
        def _():
            loss_ref[...] = jnp.zeros_like(loss_ref)
            dg_ref[...] = jnp.zeros_like(dg_ref)
            dw_ref[...] = jnp.zeros_like(dw_ref)
            for cn in range(SGU_PER_STEP):
                gating((au0_ref, av0_ref, ag0_ref), 0, cn)

        a, b = a_scr[cur], b_ref[...].astype(BF16)
        mix = (jnp.dot(a, w_ref[0:512, :], preferred_element_type=F32)
               + jnp.dot(b, w_ref[512:1024, :], preferred_element_type=F32))
        err = x_ref[...] + g_ref[...] * mix - t_ref[...]
        loss_ref[...] += 0.5 * jnp.sum(jnp.mean(err * err, axis=-1))
        dy = err * (1.0 / DM)
        dy_ref[...] = dy
        dg_ref[...] += jnp.sum(dy * mix, axis=0, keepdims=True)
        dmix = (g_ref[...] * dy).astype(BF16)

        def dcat_half(n):
            part = slice(512 * n, 512 * (n + 1))
            dcat_ref[:, part] = lax.dot_general(dmix, w_ref[part, :], _NT, preferred_element_type=F32)

        def dw_half(n, src):
            dw_ref[512 * n:512 * (n + 1), :] += lax.dot_general(src, dmix, (((0,), (0,)), ((), ())),
                                                                preferred_element_type=F32)

        _emit_interleaved([functools.partial(gating, (au1_ref, av1_ref, ag1_ref), nxt, cn) for cn in range(SGU_PER_STEP)],
                          [functools.partial(dcat_half, 0), functools.partial(dcat_half, 1),
                           functools.partial(dw_half, 0, a), functools.partial(dw_half, 1, b)])

    tile = lambda w: pl.BlockSpec((tl, w), lambda t: (t, 0))
    whole = pl.BlockSpec((DM, DM), lambda t: (0, 0))
    zfirst = [pl.BlockSpec((tl, 512), functools.partial(lambda c, t: (0, c), c)) for c in range(3)]
    znext = [pl.BlockSpec((tl, 512), functools.partial(lambda c, t: (jnp.minimum(t + 1, nt - 1), c), c))
             for c in range(3)]
    wspec = pl.BlockSpec((4, 128, 128), lambda t: (0, 0, 0))
    return pl.pallas_call(
        kern, name="outproj", grid=(nt,),
        in_specs=zfirst + znext + [_row(512), wspec, wspec, tile(512), tile(DM), tile(DM), _row(DM), whole],
        out_specs=[pl.BlockSpec((8, 128), lambda t: (0, 0)), tile(DM), tile(DM), _row(DM), whole],
        out_shape=[jax.ShapeDtypeStruct((8, 128), F32), jax.ShapeDtypeStruct((SEQ, DM), F32),
                   jax.ShapeDtypeStruct((SEQ, DM), F32), jax.ShapeDtypeStruct((1, DM), F32),
                   jax.ShapeDtypeStruct((DM, DM), F32)],
        scratch_shapes=[pltpu.VMEM((2, tl, 512), BF16)],
        compiler_params=_cparams(("arbitrary",), 48 * 1024 * 1024),
    )(z, z, z, z, z, z, sg, ws, bsb, out_b, x, target, gate, wo)


DZ_COLS = (("a", 0, 1536), ("q", 1536, 2048), ("k", 2048, 2560), ("v", 2560, 3072), ("g", 3072, DIN))
DZC_COLS = (("k", 2048, 2560), ("v", 2560, 3072))
_NT = (((1,), (1,)), ((), ()))


DH_SUBTILES = 2


def _dz_specs(tl):
    return [pl.BlockSpec((tl, 1536), lambda t: (t, 0))] + [pl.BlockSpec((tl, 512), lambda t: (t, 0))] * 4


def dh_bwd(dz_parts, w_full, x, dy, shift, scale, norm_g, dg_ctx):
    tl = 512
    nt = SEQ // tl

    def kern(a_ref, q_ref, k_ref, v_ref, g_ref, w_ref, x_ref, dy_ref, sh_ref, sc_ref, gn_ref, dgc_ref,
             gx_ref, dsh_ref, dsc_ref, dg_ref):
        @pl.when(pl.program_id(0) == 0)
        def _():
            dsh_ref[...] = jnp.zeros_like(dsh_ref)
            dsc_ref[...] = jnp.zeros_like(dsc_ref)
            dg_ref[...] = dgc_ref[...]

        src = dict(a=a_ref, q=q_ref, k=k_ref, v=v_ref, g=g_ref)
        for sub in range(DH_SUBTILES):
            rows = slice(sub * tl // DH_SUBTILES, (sub + 1) * tl // DH_SUBTILES)
            dh = None
            for name, c0, c1 in DZ_COLS:
                part = lax.dot_general(src[name][rows, :], w_ref[:, c0:c1], _NT, preferred_element_type=F32)
                dh = part if dh is None else dh + part
            _, vjp = jax.vjp(_modulated, x_ref[rows, :], gn_ref[...], sc_ref[...], sh_ref[...])
            dx, dg, dsc, dsh = vjp(dh)
            gx_ref[rows, :] = dy_ref[rows, :] + dx
            dg_ref[...] += dg
            dsc_ref[...] += dsc
            dsh_ref[...] += dsh

    tile = pl.BlockSpec((tl, DM), lambda t: (t, 0))
    return pl.pallas_call(
        kern, name="dh_bwd", grid=(nt,),
        in_specs=_dz_specs(tl) + [pl.BlockSpec((DM, DIN), lambda t: (0, 0)), tile, tile, _row(DM),
                                  _row(DM), _row(DM), _row(DM)],
        out_specs=[tile, _row(DM), _row(DM), _row(DM)],
        out_shape=[jax.ShapeDtypeStruct((SEQ, DM), F32)] + [jax.ShapeDtypeStruct((1, DM), F32)] * 3,
        compiler_params=_cparams(("arbitrary",), 48 * 1024 * 1024),
    )(*dz_parts, w_full, x, dy, shift, scale, norm_g, dg_ctx)


def dw_bwd(h, z, sg, ws, bsb, dcat, dz_attn, hc, dck, dcv, g_out):
    tl = SGU_CHUNK * SGU_PER_STEP
    nt = SEQ // tl
    (rhi, wi), (rho, wo) = RS_SHAPES

    def kern(h_ref, au_ref, av_ref, ag_ref, sg_ref, ws_ref, bs_ref, do_ref, q_ref, k_ref, v_ref, g_ref,
             hc_ref, dck_ref, dcv_ref, go_hbm,
             wire_i, keep_i, wire_o, keep_o, a_ref, dsg_ref, dws_ref, dbs_ref,
             acc, rcv_i, mine_o, rcv_o, load_sem, send_sems, recv_sems):
        t = pl.program_id(0)
        x, y, c = _me()
        k = 2 * x + y
        sib = _flip(1)
        half = lambda hh, rh: pl.ds(pl.multiple_of(hh * rh, rh), rh)
        load_o = pltpu.make_async_copy(go_hbm.at[:, half(c, rho), :], mine_o, load_sem)
        pair_o = _rcopy(go_hbm.at[:, half(1 - c, rho), :], rcv_o, send_sems, recv_sems, 0, sib)
        pair_i = [_rcopy(wire_i.at[j], rcv_i.at[j], send_sems, recv_sems, 1 + j, sib) for j in range(NCHIP)]

        @pl.when(t == 0)
        def _():
            load_o.start()
            pair_o.start()
            acc[...] = jnp.zeros_like(acc)
            dsg_ref[...] = jnp.zeros_like(dsg_ref)
            dws_ref[...] = jnp.zeros_like(dws_ref)
            dbs_ref[...] = jnp.zeros_like(dbs_ref)
            hct = hc_ref[...].T
            csrc = dict(k=dck_ref, v=dcv_ref)
            for name, c0, c1 in DZC_COLS:
                acc[:, c0:c1] += jnp.dot(hct, csrc[name][...].astype(BF16), preferred_element_type=F32)

        ht = h_ref[...].T
        src = dict(a=a_ref, q=q_ref, k=k_ref, v=v_ref, g=g_ref)

        def gating_backward(cn):
            sl = slice(cn * SGU_CHUNK, (cn + 1) * SGU_CHUNK)
            _, vjp = jax.vjp(_sgu_chunk, au_ref[sl, :], av_ref[sl, :], ag_ref[sl, :], sg_ref[...], ws_ref[...],
                             bs_ref[...])
            dau, dav, dag, dsg, dws, dbs = vjp(do_ref[sl, :])
            a_ref[sl, 0:512] = dau.astype(BF16)
            a_ref[sl, 512:1024] = dav.astype(BF16)
            a_ref[sl, 1024:1536] = dag.astype(BF16)
            dsg_ref[...] += dsg
            dws_ref[...] += dws
            dbs_ref[...] += dbs

        def product(name, c0, c1):
            acc[:, c0:c1] += jnp.dot(ht, src[name][...], preferred_element_type=F32)

        _emit_interleaved([functools.partial(gating_backward, cn) for cn in range(SGU_PER_STEP)],
                          [functools.partial(product, *cols) for cols in DZ_COLS[1:]])
        product(*DZ_COLS[0])

        @pl.when(t == nt - 1)
        def _():
            dbs_ref[...] = jnp.broadcast_to(jnp.sum(dbs_ref[...], axis=-1, keepdims=True), dbs_ref.shape)
            shard = lambda j: slice(j * SHARD_IN, (j + 1) * SHARD_IN)
            for j in range(NCHIP):
                wire_i[j] = acc[half(1 - c, rhi), shard(j)].astype(BF16)
                pair_i[j].start()
            load_o.wait()
            pair_o.wait_recv()
            for j in range(NCHIP):
                wire_o[j] = (mine_o[j] + rcv_o[j]).astype(BF16)
            keep_o[...] = mine_o[k] + rcv_o[k]
            mine = half(c, rhi)
            for j in range(NCHIP):
                pair_i[j].wait_recv()
                pair_i[j].wait_send()
                pair_sum = acc[mine, shard(j)] + rcv_i[j].astype(F32)
                wire_i[j] = pair_sum.astype(BF16)

                @pl.when(k == j)
                def _():
                    keep_i[...] = pair_sum
            pair_o.wait_send()

    whole = lambda *shape: pl.BlockSpec(shape, lambda t: (0,) * len(shape))
    rows, sgu_specs = _sgu_specs()
    assert rows == tl
    a_spec, *attn_specs = _dz_specs(tl)
    return pl.pallas_call(
        kern, name="dw_bwd", grid=(nt,),
        in_specs=[pl.BlockSpec((tl, DM), lambda t: (t, 0))] + sgu_specs + [pl.BlockSpec((tl, 512), lambda t: (t, 0))]
        + attn_specs + [whole(CTX, DM), whole(CTX, 512), whole(CTX, 512), pl.BlockSpec(memory_space=pl.ANY)],
        out_specs=[whole(NCHIP, rhi, wi), whole(rhi, wi), whole(NCHIP, rho, wo), whole(rho, wo),
                   a_spec, _row(512), whole(4, 128, 128), whole(4, 128, 128)],
        out_shape=[jax.ShapeDtypeStruct((NCHIP, rhi, wi), BF16), jax.ShapeDtypeStruct((rhi, wi), F32),
                   jax.ShapeDtypeStruct((NCHIP, rho, wo), BF16), jax.ShapeDtypeStruct((rho, wo), F32),
                   jax.ShapeDtypeStruct((SEQ, 1536), BF16), jax.ShapeDtypeStruct((1, 512), F32),
                   jax.ShapeDtypeStruct((4, 128, 128), F32), jax.ShapeDtypeStruct((4, 128, 128), F32)],
        scratch_shapes=[pltpu.VMEM((DM, DIN), F32), pltpu.VMEM((NCHIP, rhi, wi), BF16),
                        pltpu.VMEM((NCHIP, rho, wo), F32), pltpu.VMEM((NCHIP, rho, wo), F32),
                        pltpu.SemaphoreType.DMA(()), pltpu.SemaphoreType.DMA((1 + NCHIP,)),
                        pltpu.SemaphoreType.DMA((1 + NCHIP,))],
        compiler_params=_cparams(("arbitrary",), 60 * 1024 * 1024),
    )(h, z, z, z, sg, ws, bsb, dcat, *dz_attn, hc, dck, dcv, g_out)


def ctx_bwd(dck, dcv, w_full, ctx, cshift, cscale, norm_g):
    def kern(dck_ref, dcv_ref, w_ref, c_ref, sh_ref, sc_ref, g_ref, dsh_ref, dsc_ref, dg_ref):
        csrc = dict(k=dck_ref, v=dcv_ref)
        dhc = None
        first = DZC_COLS[0][1]
        for name, c0, c1 in DZC_COLS:
            part = lax.dot_general(csrc[name][...].astype(BF16), w_ref[:, c0 - first:c1 - first], _NT,
                                   preferred_element_type=F32)
            dhc = part if dhc is None else dhc + part
        _, vjp = jax.vjp(lambda g, sc, sh: _modulated(c_ref[...], g, sc, sh), g_ref[...], sc_ref[...], sh_ref[...])
        dg_ref[...], dsc_ref[...], dsh_ref[...] = vjp(dhc)

    whole = lambda r, c: pl.BlockSpec((r, c), lambda i: (0, 0))
    return pl.pallas_call(
        kern, name="ctx_bwd", grid=(1,),
        in_specs=[whole(CTX, 512), whole(CTX, 512), pl.BlockSpec((DM, 1024), lambda i: (0, DZC_COLS[0][1] // 1024)),
                  whole(CTX, DM), _row(DM), _row(DM), _row(DM)],
        out_specs=[_row(DM), _row(DM), _row(DM)],
        out_shape=[jax.ShapeDtypeStruct((1, DM), F32)] * 3,
        compiler_params=_cparams(("arbitrary",), 40 * 1024 * 1024),
    )(dck, dcv, w_full, ctx, cshift, cscale, norm_g)


def _lane_pad_rpb(rpb):
    r = jnp.pad(rpb, ((0, 0), (0, 0), (0, GRID_W - rpb.shape[-1])))
    return jnp.concatenate([r, r], axis=-1)


def local_step(chip, dev, x, c_vec, c_ctx, w_ada, b_shard, ctx, target, norm_g, sgu_g, w_s, b_s, q_g, k_g, rpb,
               w_in_shard, w_out_shard):
    bsb = jnp.broadcast_to(b_s[:, :, None], (4, 128, 128))
    qg2, kg2 = jnp.tile(q_g, (1, 2)), jnp.tile(k_g, (1, 2))

    z, h, w_in_full, w_out_full, mod_all, cs = inproj_fwd(chip, x, c_vec, c_ctx, w_ada, b_shard, norm_g, w_in_shard,
                                                          w_out_shard)
    mods = mod_all.transpose(1, 0, 2).reshape(CS_ROWS, 3 * DM)
    mod = lax.dynamic_slice(mods, (8 * dev, 0), (1, 3 * DM))
    shift, scale, gate = mod[:, :DM], mod[:, DM:2 * DM], mod[:, 2 * DM:]
    cshift, cscale = mods[8 * NDEV:8 * NDEV + 1, :DM], mods[8 * NDEV:8 * NDEV + 1, DM:2 * DM]
    zc, hc = ctx_fwd(ctx, cshift, cscale, norm_g, w_in_full)
    out_b, *saved = attn_fwd(z, zc, _lane_pad_rpb(rpb), qg2, kg2)
    loss8, dy, dcat, dgate, dwo = outproj(z, sgu_g, w_s, bsb, out_b, x, target, gate, w_out_full.reshape(DM, DM))
    dq, dk, dv, dbg, dck, dcv, drpb, dqg2, dkg2 = attn_bwd(z, zc, qg2, kg2, dcat, saved)
    drpb = drpb[:, :, :rpb.shape[-1]]
    dcshift, dcscale, dng_c = ctx_bwd(dck, dcv, w_in_full, ctx, cshift, cscale, norm_g)
    wire_i, keep_i, wire_o, keep_o, dz_a, dsg, dws, dbsb = dw_bwd(
        h, z, sgu_g, w_s, bsb, dcat, (dq, dk, dv, dbg), hc, dck, dcv, dwo.reshape(NCHIP, SHARD_OUT, DM))
    dz_parts = (dz_a, dq, dk, dv, dbg)
    *in_flight, token = rs_start(wire_i, wire_o)
    grad_x, dshift, dscale, dng = dh_bwd(dz_parts, w_in_full, x, dy, shift, scale, norm_g, dng_c + token[0, 0])
    got_i, got_o = rs_wait(*in_flight, dshift)
    return dict(
        loss=loss8[0:1, 0:1], grad_x=grad_x, rs=(keep_i, got_i, keep_o, got_o), cs=cs,
        dmod=jnp.concatenate([dshift, dscale, dgate], axis=-1),
        dcmod=jnp.concatenate([dcshift, dcscale, jnp.zeros((1, DM), F32)], axis=-1),
        d_norm_g=dng, d_sgu_g=dsg, d_w_s=dws, d_b_s=dbsb[:, :, 0],
        d_q_g=dqg2[:, :HDIM], d_k_g=dkg2[:, :HDIM], d_rpb=drpb)


def _me():
    return lax.axis_index("x"), lax.axis_index("y"), lax.axis_index("c")


def _flip(q):
    x, y, c = _me()
    return ((1 - x) if q & 4 else x, (1 - y) if q & 2 else y, (1 - c) if q & 1 else c)


def _chip_of(dev):
    return 2 * dev[0] + dev[1]


def _rcopy(src, dst, send_sems, recv_sems, k, dev):
    return pltpu.make_async_remote_copy(src_ref=src, dst_ref=dst, send_sem=send_sems.at[k], recv_sem=recv_sems.at[k],
                                        device_id=dev, device_id_type=MESH_ID)


_VMEM_SPEC = pl.BlockSpec(memory_space=pltpu.VMEM)
SLAB_ROWS = 80


RS_SHAPES = ((DM // 2, SHARD_IN), (SHARD_OUT // 2, DM))
_HBM_SPEC = pl.BlockSpec(memory_space=pltpu.HBM)
_SEM_SPEC = pl.BlockSpec(memory_space=pltpu.SEMAPHORE)
_IN_FLIGHT = pltpu.SideEffectType.DATAFLOW_SIDE_EFFECTING


def _rs_copies(wires, lands, send_sems, recv_sems):
    return [pltpu.make_async_remote_copy(
        src_ref=wires[n].at[_chip_of(_flip(q))], dst_ref=lands[n].at[q // 2 - 1],
        send_sem=send_sems.at[3 * n + q // 2 - 1], recv_sem=recv_sems.at[3 * n + q // 2 - 1],
        device_id=_flip(q), device_id_type=MESH_ID) for n in (0, 1) for q in (2, 4, 6)]


def rs_start(wire_i, wire_o):
    lands = [lax.empty((NCHIP - 1, rh, w), BF16) for rh, w in RS_SHAPES]

    def body(wi_ref, wo_ref, li_ref, lo_ref, send_sems, recv_sems, wi_thru, wo_thru, li_thru, lo_thru, token):
        for cp in _rs_copies((wi_ref, wo_ref), (li_ref, lo_ref), send_sems, recv_sems):
            cp.start()
        token[...] = jnp.zeros_like(token)

    hbm = lambda a: pltpu.HBM(a.shape, a.dtype)
    return pl.pallas_call(
        body, name="rs_start",
        out_shape=(pltpu.SemaphoreType.DMA((6,)), pltpu.SemaphoreType.DMA((6,)), hbm(wire_i), hbm(wire_o),
                   hbm(lands[0]), hbm(lands[1]), jax.ShapeDtypeStruct((8, 128), F32)),
        in_specs=(_HBM_SPEC,) * 4, out_specs=(_SEM_SPEC, _SEM_SPEC) + (_HBM_SPEC,) * 4 + (_VMEM_SPEC,),
        input_output_aliases={0: 2, 1: 3, 2: 4, 3: 5},
        compiler_params=pltpu.CompilerParams(has_side_effects=_IN_FLIGHT),
    )(*[pltpu.with_memory_space_constraint(a, pltpu.HBM) for a in (wire_i, wire_o, *lands)])


def rs_wait(send_sems, recv_sems, wire_i, wire_o, land_i, land_o, after):
    def body(wi_ref, wo_ref, li_ref, lo_ref, send_sems, recv_sems, after_ref, wi_dead, wo_dead, gi_ref, go_ref):
        for cp in _rs_copies((wi_ref, wo_ref), (li_ref, lo_ref), send_sems, recv_sems):
            cp.wait_send()
            cp.wait_recv()

    hbm = lambda a: pltpu.HBM(a.shape, a.dtype)
    return pl.pallas_call(
        body, name="rs_wait", out_shape=(hbm(wire_i), hbm(wire_o), hbm(land_i), hbm(land_o)),
        in_specs=(_HBM_SPEC,) * 4 + (_SEM_SPEC, _SEM_SPEC, pl.BlockSpec(memory_space=pl.ANY)),
        out_specs=(_HBM_SPEC,) * 4, input_output_aliases={0: 0, 1: 1, 2: 2, 3: 3},
        compiler_params=pltpu.CompilerParams(has_side_effects=_IN_FLIGHT),
    )(wire_i, wire_o, land_i, land_o, send_sems, recv_sems, after)[2:]


def final_reduce(keep_i, got_i, keep_o, got_o, slab, cs, w_ada, c_ctx):
    (rhi, wi), (rho, wo) = RS_SHAPES

    def kern(ki_hbm, gi_hbm, ko_hbm, go_hbm, s_ref, cs_ref, w_hbm, cc_ref,
             gin_ref, gout_ref, tot_ref, dw_ref, db_ref, dcc_ref,
             ki, gi, ko, go, w_scr, all_ref, dms_scr, parts, load_sems, send_sems, recv_sems):
        x, y, c = _me()
        k = 2 * x + y
        sib = _flip(1)
        dev = lambda d: 4 * d[0] + 2 * d[1] + d[2]
        me = dev((x, y, c))

        def slab_copy(idx, owner, to):
            return _rcopy(all_ref.at[dev(owner)], all_ref.at[dev(owner)], send_sems, recv_sems, idx, to)

        all_ref[me] = s_ref[...]
        first = [slab_copy(0, (x, y, c), sib)] + [slab_copy(q // 2, (x, y, c), _flip(q)) for q in (2, 4, 6)]
        for cp in first:
            cp.start()
        loads = [pltpu.make_async_copy(src, dst, load_sems.at[n]) for n, (src, dst) in enumerate(
            ((ki_hbm, ki), (gi_hbm, gi), (ko_hbm, ko), (go_hbm, go), (w_hbm, w_scr)))]
        for cp in loads:
            cp.start()

        shares = []
        for n, (keep, got, out) in enumerate(((ki, gi, gin_ref), (ko, go, gout_ref))):
            rh = RS_SHAPES[n][0]
            half = lambda hh, rh=rh: pl.ds(pl.multiple_of(hh * rh, rh), rh)
            loads[2 * n].wait()
            loads[2 * n + 1].wait()
            out[half(c), :] = ((keep[...] + got[0].astype(F32)) + got[1].astype(F32)) + got[2].astype(F32)
            share = _rcopy(out.at[half(c), :], out.at[half(c), :], send_sems, recv_sems, 7 + n, sib)
            share.start()
            shares.append((share, _rcopy(out.at[half(1 - c), :], out.at[half(1 - c), :], send_sems, recv_sems, 7 + n,
                                         sib)))

        passed = []
        for q in (2, 4, 6):
            slab_copy(q // 2, _flip(q), (x, y, c)).wait_recv()
            cp = slab_copy(3 + q // 2, _flip(q), sib)
            cp.start()
            passed.append(cp)
        slab_copy(0, sib, (x, y, c)).wait_recv()
        for q in (2, 4, 6):
            slab_copy(3 + q // 2, _flip(q | 1), (x, y, c)).wait_recv()
        tot = all_ref[0]
        for d in range(1, NDEV):
            tot = tot + all_ref[d]
        tot_ref[...] = tot

        pad = jnp.zeros((7, DM), F32)
        dm = [jnp.concatenate([all_ref[d, 12 + j:13 + j, :] for d in range(NDEV)] + [tot[9 + j:10 + j, :], pad], axis=0)
              for j in range(3)]
        db_ref[...] = jnp.concatenate([jnp.sum(part, axis=0, keepdims=True) for part in dm], axis=0)
        dm = jnp.concatenate(dm, axis=-1)
        for j in range(NCHIP):
            @pl.when(k == j)
            def _():
                dms_scr[...] = dm[:, j * SHARD_ADA:(j + 1) * SHARD_ADA].astype(BF16)

        a_in = jnp.concatenate([cs_ref[8 * d:8 * d + 1, :] for d in range(NDEV)]
                               + [cs_ref[8 * NDEV:8 * NDEV + 1, :], pad], axis=0)
        act = jax.nn.silu(a_in).astype(BF16)
        dms = dms_scr[...]
        dw_ref[...] = lax.dot_general(act, dms, (((0,), (0,)), ((), ())), preferred_element_type=F32)
        loads[4].wait()
        parts[k] = lax.dot_general(dms, w_scr[...].astype(BF16), (((1,), (1,)), ((), ())), preferred_element_type=F32)
        sends = [_rcopy(parts.at[k], parts.at[k], send_sems, recv_sems, 8 + q // 2, _flip(q)) for q in (2, 4, 6)]
        for cp in sends:
            cp.start()
        for q in (2, 4, 6):
            kq = _chip_of(_flip(q))
            _rcopy(parts.at[kq], parts.at[kq], send_sems, recv_sems, 8 + q // 2, _flip(q)).wait_recv()
        dact = ((parts[0] + parts[1]) + parts[2]) + parts[3]
        _, vjp = jax.vjp(jax.nn.silu, cc_ref[...])
        dcc_ref[...] = vjp(dact[8:9, :])[0]

        for share, arrival in shares:
            arrival.wait_recv()
            share.wait_send()
        for cp in first + passed + sends:
            cp.wait_send()

    any_spec = pl.BlockSpec(memory_space=pl.ANY)
    return pl.pallas_call(
        kern, name="final_reduce",
        in_specs=[any_spec] * 4 + [_VMEM_SPEC, _VMEM_SPEC, any_spec, _VMEM_SPEC], out_specs=[_VMEM_SPEC] * 6,
        out_shape=[jax.ShapeDtypeStruct((2 * rhi, wi), F32), jax.ShapeDtypeStruct((2 * rho, wo), F32),
                   jax.ShapeDtypeStruct((SLAB_ROWS, DM), F32), jax.ShapeDtypeStruct((DM, SHARD_ADA), F32),
                   jax.ShapeDtypeStruct((3, DM), F32), jax.ShapeDtypeStruct((1, DM), F32)],
        scratch_shapes=[pltpu.VMEM((rhi, wi), F32), pltpu.VMEM((NCHIP - 1, rhi, wi), BF16),
                        pltpu.VMEM((rho, wo), F32), pltpu.VMEM((NCHIP - 1, rho, wo), BF16),
                        pltpu.VMEM((DM, SHARD_ADA), F32), pltpu.VMEM((NDEV, SLAB_ROWS, DM), F32),
                        pltpu.VMEM((16, SHARD_ADA), BF16), pltpu.VMEM((NCHIP, 16, DM), F32),
                        pltpu.SemaphoreType.DMA((5,)), pltpu.SemaphoreType.DMA((12,)), pltpu.SemaphoreType.DMA((12,))],
        compiler_params=pltpu.CompilerParams(vmem_limit_bytes=40 * 1024 * 1024),
    )(keep_i, got_i, keep_o, got_o, slab, cs, w_ada, c_ctx)


def _adamw_math(w, g, m, v):
    m = B1 * m + (1.0 - B1) * g
    v = B2 * v + (1.0 - B2) * (g * g)
    m_hat = m / (1.0 - B1 ** STEP)
    v_hat = v / (1.0 - B2 ** STEP)
    return -LR * (m_hat / (jnp.sqrt(v_hat) + ADAM_EPS) + WD * w), m, v


def adamw_big(w, g, m, v, name, block_rows=256):
    rows, width = w.shape

    def kern(w_ref, g_ref, m_ref, v_ref, d_ref, nm_ref, nv_ref):
        d_ref[...], nm_ref[...], nv_ref[...] = _adamw_math(w_ref[...], g_ref[...], m_ref[...], v_ref[...])

    spec = pl.BlockSpec((block_rows, width), lambda i: (i, 0))
    return pl.pallas_call(
        kern, name=name, grid=(rows // block_rows,), in_specs=[spec] * 4, out_specs=[spec] * 3,
        out_shape=[jax.ShapeDtypeStruct((rows, width), F32)] * 3,
        compiler_params=_cparams(("arbitrary",)),
    )(w, g, m, v)


def adamw_small(quads):
    n = len(quads)

    def kern(*refs):
        ins, outs = refs[:4 * n], refs[4 * n:]
        for i in range(n):
            w, g, m, v = (r[...] for r in ins[4 * i:4 * i + 4])
            outs[3 * i][...], outs[3 * i + 1][...], outs[3 * i + 2][...] = _adamw_math(w, g, m, v)

    flat = [a for quad in quads for a in quad]
    res = pl.pallas_call(
        kern, name="adamw_small", in_specs=[_VMEM_SPEC] * (4 * n), out_specs=[_VMEM_SPEC] * (3 * n),
        out_shape=[jax.ShapeDtypeStruct(q[0].shape, F32) for q in quads for _ in range(3)],
    )(*flat)
    return [tuple(res[3 * i:3 * i + 3]) for i in range(n)]


def _rows_of(a, rows):
    flat = a.reshape(-1)
    return jnp.pad(flat, (0, rows * DM - flat.shape[0])).reshape(rows, DM)


def kernel(x, c, ctx, c_ctx, w_ada, b_ada, norm_g, w_in, sgu_norm_g, w_spatial, b_spatial, q_norm_g, k_norm_g, rpb, w_out, loss_target, m_c_ctx, m_w_ada, m_b_ada, m_norm_g, m_w_in, m_sgu_norm_g, m_w_spatial, m_b_spatial, m_q_norm_g, m_k_norm_g, m_rpb, m_w_out, v_c_ctx, v_w_ada, v_b_ada, v_norm_g, v_w_in, v_sgu_norm_g, v_w_spatial, v_b_spatial, v_q_norm_g, v_k_norm_g, v_rpb, v_w_out):
    xi, yi, ci = lax.axis_index("x"), lax.axis_index("y"), lax.axis_index("c")
    chip, dev = 2 * xi + yi, 4 * xi + 2 * yi + ci
    c_ctx2 = c_ctx.reshape(1, DM)

    b_shard = lax.dynamic_slice(b_ada, (0, chip * SHARD_ADA), (1, SHARD_ADA))
    part = local_step(chip.reshape(1).astype(jnp.int32), dev, x[0], c, c_ctx2, w_ada[0], b_shard, ctx[0], loss_target[0],
                      norm_g, sgu_norm_g, w_spatial[0], b_spatial[0], q_norm_g, k_norm_g, rpb[0], w_in[0], w_out[0])
    cs = part["cs"]

    slab = jnp.concatenate([
        part["d_norm_g"], _rows_of(part["d_sgu_g"], 1), _rows_of(part["d_b_s"], 1),
        _rows_of(jnp.concatenate([part["d_q_g"], part["d_k_g"]], axis=-1), 1), _rows_of(part["d_rpb"], 4),
        _rows_of(part["loss"], 1), _rows_of(part["dcmod"], 3), _rows_of(part["dmod"], 3), jnp.zeros((1, DM), F32),
        _rows_of(part["d_w_s"], 64)], axis=0)
    g_w_in, g_w_out, tot, g_w_ada, g_b_ada, g_c_ctx = final_reduce(*part["rs"], slab, cs, w_ada[0], c_ctx2)
    g_b_ada = g_b_ada.reshape(1, 3 * DM)

    loss = tot[8, 0]
    g_small = dict(
        c_ctx=g_c_ctx, b_ada=g_b_ada, norm_g=tot[0:1], sgu_norm_g=tot[1:2, :512], w_spatial=tot[16:80].reshape(512, 128),
        b_spatial=tot[2:3, :512].reshape(4, 128), q_norm_g=tot[3:4, :HDIM], k_norm_g=tot[3:4, HDIM:2 * HDIM],
        rpb=tot[4:8].reshape(-1)[:HEADS * 15 * 31].reshape(HEADS * 15, 31))
    shapes = dict(c_ctx=(DM,), w_ada=(1, DM, SHARD_ADA), b_ada=(1, 3 * DM), norm_g=(1, DM), w_in=(1, DM, SHARD_IN),
                  sgu_norm_g=(1, 512), w_spatial=(1, 4, 128, 128), b_spatial=(1, 4, 128), q_norm_g=(1, HDIM),
                  k_norm_g=(1, HDIM), rpb=(1, HEADS, 15, 31), w_out=(1, SHARD_OUT, DM))
    names = list(shapes)
    weights = dict(c_ctx=c_ctx, w_ada=w_ada, b_ada=b_ada, norm_g=norm_g, w_in=w_in, sgu_norm_g=sgu_norm_g,
                   w_spatial=w_spatial, b_spatial=b_spatial, q_norm_g=q_norm_g, k_norm_g=k_norm_g, rpb=rpb, w_out=w_out)
    m_in = dict(zip(names, (m_c_ctx, m_w_ada, m_b_ada, m_norm_g, m_w_in, m_sgu_norm_g, m_w_spatial, m_b_spatial,
                            m_q_norm_g, m_k_norm_g, m_rpb, m_w_out)))
    v_in = dict(zip(names, (v_c_ctx, v_w_ada, v_b_ada, v_norm_g, v_w_in, v_sgu_norm_g, v_w_spatial, v_b_spatial,
                            v_q_norm_g, v_k_norm_g, v_rpb, v_w_out)))
    grads = dict(g_small, w_ada=g_w_ada, w_in=g_w_in, w_out=g_w_out)
    upd = {}
    for n in ("w_ada", "w_in", "w_out"):
        g = grads[n]
        upd[n] = adamw_big(weights[n].reshape(g.shape), g, m_in[n].reshape(g.shape), v_in[n].reshape(g.shape),
                           "adamw_" + n)
    small = [n for n in names if n not in upd]
    res = adamw_small([(weights[n].reshape(grads[n].shape), grads[n], m_in[n].reshape(grads[n].shape),
                        v_in[n].reshape(grads[n].shape)) for n in small])
    upd.update(zip(small, res))
    out = [loss, part["grad_x"].reshape(1, SEQ, DM)]
    out += [grads[n].reshape(shapes[n]) for n in names]
    for slot in range(3):
        out += [upd[n][slot].reshape(shapes[n]) for n in names]
    return tuple(out)
```

```python
import functools

import jax
import jax.numpy as jnp
from jax import lax
from jax.experimental import pallas as pl
from jax.experimental.pallas import tpu as pltpu

F32, BF16 = jnp.float32, jnp.bfloat16
SEQ, DM, CTX, DIN = 4096, 1024, 256, 3584
NCHIP, NDEV = 4, 8
SHARD_IN = DIN // NCHIP
SHARD_ADA = 3 * DM // NCHIP
SHARD_OUT = DM // NCHIP
GRID_W = 64
QROWS = 4
KROWS = 12
QBLK, KBLK = QROWS * GRID_W, KROWS * GRID_W
NQBLK = SEQ // QBLK
HEADS, HDIM, NPAIR = 8, 64, 4
EPS = 1e-6
NEG_INF = -1e30
ZQ, ZK, ZV, ZG = 12, 16, 20, 24
LR, B1, B2, ADAM_EPS, WD, STEP = 0.001, 0.9, 0.999, 1e-08, 0.01, 10
VMEM_BIG = 56 * 1024 * 1024
MESH_ID = pl.DeviceIdType.MESH


def _dot(a, b, lhs_c, rhs_c):
    return lax.dot_general(a.astype(BF16), b.astype(BF16), (((lhs_c,), (rhs_c,)), ((), ())),
                           preferred_element_type=F32)


@jax.custom_vjp
def mm(a, b):
    return _dot(a, b, 1, 0)


@jax.custom_vjp
def mm_nt(a, b):
    return _dot(a, b, 1, 1)


@jax.custom_vjp
def mm_tn(a, b):
    return _dot(a, b, 0, 0)


mm.defvjp(lambda a, b: (mm(a, b), (a, b)), lambda r, ct: (mm_nt(ct, r[1]), mm_tn(r[0], ct)))
mm_nt.defvjp(lambda a, b: (mm_nt(a, b), (a, b)), lambda r, ct: (mm(ct, r[1]), mm_tn(ct, r[0])))
mm_tn.defvjp(lambda a, b: (mm_tn(a, b), (a, b)), lambda r, ct: (mm_nt(r[1], ct), mm(r[0], ct)))


def _rms(x, g):
    return x * lax.rsqrt(jnp.mean(x * x, axis=-1, keepdims=True) + EPS) * g


def _modulated(x, g, scale, shift):
    return _rms(x, g) * (1.0 + scale) + shift


def _pair_rms(x, g2):
    lo = lax.broadcasted_iota(jnp.int32, (1, 2 * HDIM), 1) < HDIM
    sq = x * x
    s_lo = jnp.sum(jnp.where(lo, sq, 0.0), axis=-1, keepdims=True)
    s_hi = jnp.sum(jnp.where(lo, 0.0, sq), axis=-1, keepdims=True)
    rs = jnp.where(lo, lax.rsqrt(s_lo / HDIM + EPS), lax.rsqrt(s_hi / HDIM + EPS))
    return x * rs * g2


def _cparams(sem, vmem=None):
    return pltpu.CompilerParams(dimension_semantics=sem, vmem_limit_bytes=vmem)


def _row(n):
    return pl.BlockSpec((1, n), lambda *_: (0, 0))


CS_ROWS = 8 * NDEV + 8


def _mod_part(mod_ref, row, part):
    pieces = []
    for j in range(NCHIP):
        lo, hi = max(part * DM, j * SHARD_ADA), min((part + 1) * DM, (j + 1) * SHARD_ADA)
        if lo < hi:
            pieces.append(mod_ref[j, row, lo - j * SHARD_ADA:hi - j * SHARD_ADA])
    return jnp.concatenate(pieces, axis=-1)


def inproj_fwd(chip, x, c_vec, c_ctx, w_ada, b_shard, norm_g, w_shard, wo_shard):
    tl = 1024
    nt = SEQ // tl
    halves = (DM // 2, SHARD_OUT // 2)
    n_w, n_c = 12, NDEV - 1

    def kern(k_ref, x_ref, cv_ref, cc_ref, wa_ref, b_ref, g_ref, w_ref, wo_ref,
             z_ref, h_ref, wfull_ref, wofull_ref, modall_ref, csall_ref,
             w_scr, wo_scr, h_scr, mine, cs_scr, mod_scr, shsc_scr, send_sems, recv_sems, out_sems):
        s, t = pl.program_id(0), pl.program_id(1)
        xi, yi, c = _me()
        k, me = 2 * xi + yi, 4 * xi + 2 * yi + c
        sib = _flip(1)
        rows = pl.ds(pl.multiple_of(t * tl, tl), tl)
        gathered = (w_scr, wo_scr)
        slot = lambda d: pl.ds(pl.multiple_of(8 * d, 8), 8)

        def c_copy(q, owner):
            return _rcopy(mine, cs_scr.at[slot(owner), :], send_sems, recv_sems, n_w + q - 1, _flip(q))

        def m_copy(q, chip_of_block):
            return _rcopy(mod_scr.at[chip_of_block], mod_scr.at[chip_of_block], send_sems, recv_sems,
                          n_w + n_c + q // 2 - 1, _flip(q))

        def adaln():
            first = lax.broadcasted_iota(jnp.int32, (8, DM), 0) == 0
            mine[...] = jnp.where(first, jnp.broadcast_to(cv_ref[...], (8, DM)), 0.0)
            cs_scr[slot(me), :] = mine[...]
            cs_scr[slot(NDEV), :] = jnp.where(first, jnp.broadcast_to(cc_ref[...], (8, DM)), 0.0)
            for q in range(1, NDEV):
                c_copy(q, me).start()
            wa = wa_ref[...].astype(BF16)
            for q in range(1, NDEV):
                px, py, pc = _flip(q)
                c_copy(q, 4 * px + 2 * py + pc).wait_recv()
            act = jax.nn.silu(cs_scr[...]).astype(BF16)
            mod_scr[k] = jnp.dot(act, wa, preferred_element_type=F32) + b_ref[...]
            for q in (2, 4, 6):
                m_copy(q, k).start()
            for q in (2, 4, 6):
                m_copy(q, _chip_of(_flip(q))).wait_recv()
            row = pl.ds(8 * me, 1)
            shsc_scr[0:1, :] = _mod_part(mod_scr, row, 0)
            shsc_scr[1:2, :] = _mod_part(mod_scr, row, 1)
            pltpu.sync_copy(mod_scr, modall_ref)
            pltpu.sync_copy(cs_scr, csall_ref)

        def block(n, chip_of_block, hh):
            return gathered[n].at[chip_of_block, pl.ds(pl.multiple_of(hh * halves[n], halves[n]), halves[n]), :]

        def ici(n, q, chip_of_block):
            blk = block(n, chip_of_block, c)
            return _rcopy(blk, blk, send_sems, recv_sems, 6 * n + q // 2 - 1, _flip(q))

        def d2d(n, q, chip_of_block, hh):
            blk = block(n, chip_of_block, hh)
            return _rcopy(blk, blk, send_sems, recv_sems, 6 * n + 3 + q // 2 - 1, sib)

        @pl.when((s == 0) & (t == 0))
        def _():
            adaln()
            w_scr[k] = w_ref[...].astype(BF16)
            wo_scr[k] = wo_ref[...].astype(BF16)
            for q in (2, 4, 6):
                ici(0, q, k).start()
                ici(1, q, k).start()

        for sweep in (1, 2, 3):
            @pl.when((s == sweep) & (t == 0))
            def _():
                q = 2 * sweep
                src = _chip_of(_flip(q))
                for n in (0, 1):
                    ici(n, q, src).wait_recv()
                    d2d(n, q, src, c).start()
                for n in (0, 1):
                    d2d(n, q, src, 1 - c).wait_recv()

        @pl.when(s == 0)
        def _():
            hb = _modulated(x_ref[...], g_ref[...], shsc_scr[1:2, :], shsc_scr[0:1, :]).astype(BF16)
            h_scr[rows, :] = hb
            h_ref[...] = hb

        z_ref[...] = jnp.dot(h_scr[rows, :], w_scr[lax.bitwise_xor(k, s)], preferred_element_type=F32)

        @pl.when((s == NCHIP - 1) & (t == nt - 1))
        def _():
            for q in range(1, NDEV):
                c_copy(q, me).wait_send()
            for q in (2, 4, 6):
                m_copy(q, k).wait_send()
            for n in (0, 1):
                for q in (2, 4, 6):
                    ici(n, q, k).wait_send()
                    d2d(n, q, _chip_of(_flip(q)), c).wait_send()
            outs = [pltpu.make_async_copy(w_scr.at[j], wfull_ref.at[:, j * SHARD_IN:(j + 1) * SHARD_IN], out_sems.at[j])
                    for j in range(NCHIP)] + [pltpu.make_async_copy(wo_scr, wofull_ref, out_sems.at[NCHIP])]
            for cp in outs:
                cp.start()
            for cp in outs:
                cp.wait()

    once = lambda s, t, k: (jnp.where(s == 0, t, nt - 1), 0)
    hbm = pl.BlockSpec(memory_space=pl.ANY)
    n_sem = n_w + n_c + 3
    return pl.pallas_call(
        kern, name="inproj_fwd",
        grid_spec=pltpu.PrefetchScalarGridSpec(
            num_scalar_prefetch=1, grid=(NCHIP, nt),
            in_specs=[pl.BlockSpec((tl, DM), once)] + [_VMEM_SPEC] * 7,
            out_specs=[pl.BlockSpec((tl, SHARD_IN), lambda s, t, k: (t, lax.bitwise_xor(k[0], s))),
                       pl.BlockSpec((tl, DM), once), hbm, hbm, hbm, hbm],
            scratch_shapes=[pltpu.VMEM((NCHIP, DM, SHARD_IN), BF16), pltpu.VMEM((NCHIP, SHARD_OUT, DM), BF16),
                            pltpu.VMEM((SEQ, DM), BF16), pltpu.VMEM((8, DM), F32), pltpu.VMEM((CS_ROWS, DM), F32),
                            pltpu.VMEM((NCHIP, CS_ROWS, SHARD_ADA), F32), pltpu.VMEM((8, DM), F32),
                            pltpu.SemaphoreType.DMA((n_sem,)), pltpu.SemaphoreType.DMA((n_sem,)),
                            pltpu.SemaphoreType.DMA((NCHIP + 1,))]),
        out_shape=[jax.ShapeDtypeStruct((SEQ, DIN), F32), jax.ShapeDtypeStruct((SEQ, DM), BF16),
                   jax.ShapeDtypeStruct((DM, DIN), BF16), jax.ShapeDtypeStruct((NCHIP, SHARD_OUT, DM), BF16),
                   jax.ShapeDtypeStruct((NCHIP, CS_ROWS, SHARD_ADA), F32), jax.ShapeDtypeStruct((CS_ROWS, DM), F32)],
        compiler_params=_cparams(("arbitrary", "arbitrary"), VMEM_BIG),
    )(chip, x, c_vec, c_ctx, w_ada, b_shard, norm_g, w_shard, wo_shard)


def ctx_fwd(ctx, cshift, cscale, norm_g, w_full):
    def kern(c_ref, sh_ref, sc_ref, g_ref, w_ref, zc_ref, hc_ref):
        hc = _modulated(c_ref[...], g_ref[...], sc_ref[...], sh_ref[...]).astype(BF16)
        hc_ref[...] = hc
        zc_ref[...] = jnp.dot(hc, w_ref[...], preferred_element_type=F32)

    return pl.pallas_call(
        kern, name="ctx_fwd", grid=(1,),
        in_specs=[pl.BlockSpec((CTX, DM), lambda i: (0, 0)), _row(DM), _row(DM), _row(DM),
                  pl.BlockSpec((DM, 2 * SHARD_IN), lambda i: (0, 1))],
        out_specs=[pl.BlockSpec((CTX, 2 * SHARD_IN), lambda i: (0, 0)),
                   pl.BlockSpec((CTX, DM), lambda i: (0, 0))],
        out_shape=[jax.ShapeDtypeStruct((CTX, 2 * SHARD_IN), F32), jax.ShapeDtypeStruct((CTX, DM), BF16)],
        compiler_params=_cparams(("arbitrary",)),
    )(ctx, cshift, cscale, norm_g, w_full)


SGU_CHUNK, SGU_PER_STEP = 128, 4


def _gelu(x):
    return 0.5 * x * (1.0 + lax.erf(x * 0.7071067811865476))


def _sgu_chunk(au, av, ag, sg, ws, bsb):
    u, v = _gelu(au), _gelu(av)
    outs = []
    for g in range(4):
        sl = slice(128 * g, 128 * (g + 1))
        mixed = mm(ws[g], _rms(v[:, sl], sg[:, sl])) + bsb[g]
        outs.append(u[:, sl] * mixed * jax.nn.silu(ag[:, sl]))
    return jnp.concatenate(outs, axis=-1)


def _sgu_specs():
    rows = SGU_CHUNK * SGU_PER_STEP
    zspec = lambda c: pl.BlockSpec((rows, 512), lambda n: (n, c))
    wspec = pl.BlockSpec((4, 128, 128), lambda n: (0, 0, 0))
    return rows, [zspec(0), zspec(1), zspec(2), _row(512), wspec, wspec]


_DR_OFF = (7, 3, -1)


def _row_valid(v, rr, j):
    return (j < 8, rr <= j < rr + 8, 4 <= j < 12)[v]


def _col_window():
    q = lax.broadcasted_iota(jnp.int32, (GRID_W, 128), 0)
    kc = lax.broadcasted_iota(jnp.int32, (GRID_W, 128), 1) % GRID_W
    c0 = jnp.clip(q - 8, 0, GRID_W - 16)
    return (kc >= c0) & (kc < c0 + 16)


def _bias_tiles(base, store):
    lo = lax.broadcasted_iota(jnp.int32, (1, 128), 1) < GRID_W
    win = _col_window()
    tiles = {}
    for v in range(3):
        for rr in range(QROWS):
            for jp in range(KROWS // 2):
                j0, j1 = 2 * jp, 2 * jp + 1
                ok0, ok1 = _row_valid(v, rr, j0), _row_valid(v, rr, j1)
                key = (j0 - rr + _DR_OFF[v], ok0, ok1) if (ok0 or ok1) else None
                if key not in tiles:
                    if key is None:
                        tiles[key] = jnp.full((GRID_W, 128), NEG_INF, F32)
                    else:
                        d0 = key[0]
                        r0 = base[d0:d0 + 1, :] if ok0 else jnp.zeros((1, 128), F32)
                        r1 = base[d0 + 1:d0 + 2, :] if ok1 else jnp.zeros((1, 128), F32)
                        y = jnp.broadcast_to(jnp.where(lo, r0, r1), (GRID_W, 128))
                        y = pltpu.roll(pltpu.roll(y, 128 - 15, 1), 0, 1, stride=1, stride_axis=0)
                        tiles[key] = jnp.where(win & jnp.where(lo, ok0, ok1), y, NEG_INF)
                store(v, slice(rr * GRID_W, (rr + 1) * GRID_W), slice(jp * 128, (jp + 1) * 128), tiles[key])


def _rpb_grad(load):
    lo = lax.broadcasted_iota(jnp.int32, (1, 128), 1) < GRID_W
    ri = lax.broadcasted_iota(jnp.int32, (GRID_W, GRID_W), 0)
    ci = lax.broadcasted_iota(jnp.int32, (GRID_W, GRID_W), 1)
    flip = (ri + ci == GRID_W - 1).astype(F32)
    groups = {}
    for v in range(3):
        for rr in range(QROWS):
            for jp in range(KROWS // 2):
                j0, j1 = 2 * jp, 2 * jp + 1
                ok0, ok1 = _row_valid(v, rr, j0), _row_valid(v, rr, j1)
                if not (ok0 or ok1):
                    continue
                g = load(v, slice(rr * GRID_W, (rr + 1) * GRID_W), slice(jp * 128, (jp + 1) * 128))
                key = (j0 - rr + _DR_OFF[v], ok0, ok1)
                groups[key] = g if key not in groups else groups[key] + g
    acc = [jnp.zeros((1, 128), F32) for _ in range(15)]
    for (d0, ok0, ok1), g in groups.items():
        g = lax.dot_general(flip, g, (((1,), (0,)), ((), ())), precision=lax.Precision.HIGHEST,
                            preferred_element_type=F32)
        g = pltpu.roll(pltpu.roll(g, 128 - 48, 1), 0, 1, stride=1, stride_axis=0)
        s = jnp.sum(g, axis=0, keepdims=True)
        if ok0:
            acc[d0] = acc[d0] + jnp.where(lo, s, 0.0)
        if ok1:
            acc[d0 + 1] = acc[d0 + 1] + jnp.where(lo, 0.0, s)
    return [row + pltpu.roll(row, GRID_W, 1) for row in acc]


def _scaled_q(q_raw, qg):
    return _pair_rms(q_raw, qg) * (HDIM ** -0.5)


def _head_lanes():
    lo = lax.broadcasted_iota(jnp.int32, (1, 2 * HDIM), 1) < HDIM
    return lo, jnp.logical_not(lo)


SOFTMAX_ROWS = 32


def _emit_interleaved(vector_work, matmul_work):
    for j in range(max(len(vector_work), len(matmul_work))):
        for work in (vector_work, matmul_work):
            if j < len(work):
                work[j]()


def _kblock(i):
    return jnp.clip(i - 1, 0, (SEQ - KBLK) // QBLK)


def _kstart(i):
    return pl.multiple_of(_kblock(i) * QBLK, QBLK)


ATTN_BLOCKS = 4
TILE_BUFFERS = 4
ATTN_STEPS = NQBLK // ATTN_BLOCKS
ATTN_ROWS = ATTN_BLOCKS * QBLK


def _bias_variant(i, b):
    if b == 0:
        return jnp.where(i == 0, 0, 1)
    if b == ATTN_BLOCKS - 1:
        return jnp.where(i == ATTN_STEPS - 1, 2, 1)
    return 1
KCOLS = QBLK


def _attn_in_specs():
    return [
        pl.BlockSpec((ATTN_ROWS, 128), lambda p, i: (i, ZQ + p)),
        pl.BlockSpec((SEQ, 128), lambda p, i: (0, ZK + p)),
        pl.BlockSpec((SEQ, 128), lambda p, i: (0, ZV + p)),
        pl.BlockSpec((ATTN_ROWS, 128), lambda p, i: (i, ZG + p)),
        pl.BlockSpec((CTX, 128), lambda p, i: (0, 2 + p)),
        pl.BlockSpec((CTX, 128), lambda p, i: (0, 6 + p)),
    ]


def _rpb_spec():
    return pl.BlockSpec((2, 15, 128), lambda p, i: (p, 0, 0))


def _prob_specs():
    return [pl.BlockSpec((2, ATTN_ROWS, KBLK), lambda p, i: (p, i, 0)),
            pl.BlockSpec((2, ATTN_ROWS, CTX), lambda p, i: (p, i, 0))]


NORM_ROWS = 2048


def _half_sums(x):
    lo = lax.broadcasted_iota(jnp.int32, (1, 2 * HDIM), 1) < HDIM
    return jnp.where(lo, jnp.sum(jnp.where(lo, x, 0.0), axis=-1, keepdims=True),
                     jnp.sum(jnp.where(lo, 0.0, x), axis=-1, keepdims=True))


def _pair_rms_bwd(x, g2, ct):
    rs = lax.rsqrt(_half_sums(x * x) / HDIM + EPS)
    y = x * rs
    dy = ct * g2
    return rs * (dy - y * (_half_sums(dy * y) / HDIM)), jnp.sum(ct * y, axis=0, keepdims=True)


def _norm_keys(k_ref, ck_ref, kg_ref, kn_scr, ckn_scr):
    def body(c, carry):
        sl = pl.ds(pl.multiple_of(c * NORM_ROWS, NORM_ROWS), NORM_ROWS)
        kn_scr[sl, :] = _pair_rms(k_ref[sl, :], kg_ref[...]).astype(BF16)
        return carry

    lax.fori_loop(0, SEQ // NORM_ROWS, body, 0)
    ckn_scr[...] = _pair_rms(ck_ref[...], kg_ref[...]).astype(BF16)


def _values_with_ones(v_ref, cv_ref, v1_scr, cv1_scr):
    for a, mine in enumerate(_head_lanes()):
        def body(c, carry):
            sl = pl.ds(pl.multiple_of(c * NORM_ROWS, NORM_ROWS), NORM_ROWS)
            v1_scr[a, sl, :] = jnp.where(mine, v_ref[sl, :], 1.0).astype(BF16)
            return carry

        lax.fori_loop(0, SEQ // NORM_ROWS, body, 0)
        cv1_scr[a] = jnp.where(mine, cv_ref[...], 1.0).astype(BF16)


def _pair_major_spec():
    return pl.BlockSpec((1, ATTN_ROWS, 128), lambda p, i: (p, i, 0))


def _normed_key_specs():
    return [pl.BlockSpec((None, SEQ, 128), lambda p, i: (p, 0, 0)), pl.BlockSpec((None, CTX, 128), lambda p, i: (p, 0, 0))]


def attn_fwd(z, zc, rpb2, qg2, kg2):
    def kern(q_ref, k_ref, v_ref, bg_ref, ck_ref, cv_ref, rpb_ref, qg_ref, kg_ref,
             ob_ref, o_ref, rden_ref, pl_ref, pc_ref, kn_ref, ckn_ref, kn_scr, ckn_scr, v1_scr, cv1_scr, s_scr,
             bias_ref):
        i = pl.program_id(1)

        @pl.when(i == 0)
        def _():
            for a in range(2):
                def store(v, tile_rows, tile_cols, tile, a=a):
                    bias_ref[v, a, tile_rows, tile_cols] = tile

                _bias_tiles(rpb_ref[a], store)
            _norm_keys(k_ref, ck_ref, kg_ref, kn_scr, ckn_scr)
            kn_ref[...] = kn_scr[...]
            ckn_ref[...] = ckn_scr[...]
            _values_with_ones(v_ref, cv_ref, v1_scr, cv1_scr)

        heads = _head_lanes()
        tiles = [(b, a) for b in range(ATTN_BLOCKS) for a in range(2)]
        rows = [slice(b * QBLK, (b + 1) * QBLK) for b in range(ATTN_BLOCKS)]
        variant = [_bias_variant(i, b) for b in range(ATTN_BLOCKS)]
        pv = [None] * len(tiles)
        qa, done = {}, {}
        latent = KBLK // KCOLS
        buf = lambda t: t % TILE_BUFFERS

        def keys(b, n):
            return pl.ds(pl.multiple_of(_kstart(ATTN_BLOCKS * i + b) + n * KCOLS, KCOLS), KCOLS)

        def score_piece(t, n):
            b, a = tiles[t]
            cols = slice(n * KCOLS, (n + 1) * KCOLS)
            if n == 0:
                if a == 0:
                    done["qn", b] = _scaled_q(q_ref[rows[b], :], qg_ref[...])
                qa[t] = jnp.where(heads[a], done["qn", b], 0.0).astype(BF16)
            if n < latent:
                s_scr[buf(t), :, cols] = mm_nt(qa[t], kn_scr[keys(b, n), :]) + bias_ref[variant[b], a, :, cols]
            else:
                s_scr[buf(t), :, cols] = mm_nt(qa[t], ckn_scr[...])

        def softmax_rows(t, r):
            b, a = tiles[t]
            rs = slice(r * SOFTMAX_ROWS, (r + 1) * SOFTMAX_ROWS)
            out_rows = slice(b * QBLK + rs.start, b * QBLK + rs.stop)
            s = s_scr[buf(t), rs, :]
            p = jnp.exp(s - jnp.max(s, axis=-1, keepdims=True)).astype(BF16)
            pl_ref[a, out_rows, :] = p[:, :KBLK]
            pc_ref[a, out_rows, :] = p[:, KBLK:]

        def value_piece(t, n):
            b, a = tiles[t]
            if n < latent:
                part = mm(pl_ref[a, rows[b], n * KCOLS:(n + 1) * KCOLS], v1_scr[a, keys(b, n), :])
            else:
                part = mm(pc_ref[a, rows[b], :], cv1_scr[a])
            pv[t] = part if pv[t] is None else pv[t] + part
            if n == latent:
                finish(t)

        def finish(t):
            b, a = tiles[t]
            r = jnp.where(heads[a], pltpu.roll(1.0 / pv[t], HDIM, 1), 0.0)
            done[t] = (pv[t] * r, r)
            if a == 1:
                o, rden = (lo + hi for lo, hi in zip(done[t - 1], done[t]))
                ob_ref[rows[b], :] = o * jax.nn.silu(bg_ref[rows[b], :])
                o_ref[0, rows[b], :] = o
                rden_ref[0, rows[b], :] = rden

        pieces = range(latent + 1)
        for n in pieces:
            score_piece(0, n)
        for t in range(len(tiles)):
            matmuls = []
            for n in pieces:
                if t + 1 < len(tiles):
                    matmuls.append(functools.partial(score_piece, t + 1, n))
                if t > 0:
                    matmuls.append(functools.partial(value_piece, t - 1, n))
            _emit_interleaved([functools.partial(softmax_rows, t, r) for r in range(QBLK // SOFTMAX_ROWS)], matmuls)
        for n in pieces:
            value_piece(len(tiles) - 1, n)

    qblk = pl.BlockSpec((ATTN_ROWS, 128), lambda p, i: (i, p))
    return pl.pallas_call(
        kern, name="attn_fwd", grid=(NPAIR, ATTN_STEPS),
        in_specs=_attn_in_specs() + [_rpb_spec(), _row(128), _row(128)],
        out_specs=[qblk, _pair_major_spec(), _pair_major_spec()] + _prob_specs() + _normed_key_specs(),
        out_shape=[jax.ShapeDtypeStruct((SEQ, 512), F32)] + [jax.ShapeDtypeStruct((NPAIR, SEQ, 128), F32)] * 2
        + [jax.ShapeDtypeStruct((HEADS, SEQ, KBLK), BF16), jax.ShapeDtypeStruct((HEADS, SEQ, CTX), BF16),
           jax.ShapeDtypeStruct((NPAIR, SEQ, 128), BF16), jax.ShapeDtypeStruct((NPAIR, CTX, 128), BF16)],
        scratch_shapes=[pltpu.VMEM((SEQ, 128), BF16), pltpu.VMEM((CTX, 128), BF16),
                        pltpu.VMEM((2, SEQ, 128), BF16), pltpu.VMEM((2, CTX, 128), BF16),
                        pltpu.VMEM((TILE_BUFFERS, QBLK, KBLK + CTX), F32),
                        pltpu.VMEM((3, 2, QBLK, KBLK), F32)],
        compiler_params=_cparams(("arbitrary", "arbitrary"), VMEM_BIG),
    )(z, z, z, z, zc, zc, rpb2, qg2, kg2)


def attn_bwd(z, zc, qg2, kg2, dcat, saved):
    def kern(q_ref, k_ref, v_ref, bg_ref, ck_ref, cv_ref, qg_ref, kg_ref, do_ref, o_ref, rden_ref, pl_ref, pc_ref,
             kn_scr, ckn_scr, dq_ref, dk_ref, dv_ref, dbg_ref, dck_ref, dcv_ref, drpb_ref, dqg_ref, dkg_ref,
             v_scr, cv_scr, dknt_scr, dvt_scr, dcknt_scr, dcvt_scr, dp_scr, ds_scr, db_ref):
        p, i = pl.program_id(0), pl.program_id(1)
        last = i == ATTN_STEPS - 1

        @pl.when(i == 0)
        def _():
            def body(c, carry):
                sl = pl.ds(pl.multiple_of(c * NORM_ROWS, NORM_ROWS), NORM_ROWS)
                v_scr[sl, :] = v_ref[sl, :].astype(BF16)
                return carry

            lax.fori_loop(0, SEQ // NORM_ROWS, body, 0)
            cv_scr[...] = cv_ref[...].astype(BF16)
            for acc in (dknt_scr, dvt_scr, dcknt_scr, dcvt_scr, db_ref):
                acc[...] = jnp.zeros_like(acc)

        @pl.when((i == 0) & (p == 0))
        def _():
            dqg_ref[...] = jnp.zeros_like(dqg_ref)
            dkg_ref[...] = jnp.zeros_like(dkg_ref)

        heads = _head_lanes()
        tiles = [(b, a) for b in range(ATTN_BLOCKS) for a in range(2)]
        rows = [slice(b * QBLK, (b + 1) * QBLK) for b in range(ATTN_BLOCKS)]
        kb = [_kblock(ATTN_BLOCKS * i + b) for b in range(ATTN_BLOCKS)]
        variant = [_bias_variant(i, b) for b in range(ATTN_BLOCKS)]
        latent = KBLK // KCOLS
        buf = lambda t: t % TILE_BUFFERS

        def keys(b, n):
            return pl.ds(pl.multiple_of((kb[b] + n) * KCOLS, KCOLS), KCOLS)

        gated = {}

        def gate_backward(b):
            bg, dout, o = bg_ref[rows[b], :], do_ref[rows[b], :], o_ref[0, rows[b], :]
            sig = jax.nn.sigmoid(bg)
            do = dout * (bg * sig)
            dbg_ref[rows[b], :] = (dout * o * (sig * (1.0 + bg * (1.0 - sig)))).astype(BF16)
            rden = rden_ref[0, rows[b], :]
            dr = do * rden
            qn = _scaled_q(q_ref[rows[b], :], qg_ref[...])
            gated[b] = (dr, dr.T.astype(BF16), qn.T.astype(BF16), do * o * rden)

        feats = [slice(a * HDIM, (a + 1) * HDIM) for a in range(2)]
        doa, doa_t, qa_t, delta = {}, {}, {}, {}
        dqn = [None] * len(tiles)

        def cols(n):
            return slice(n * KCOLS, (n + 1) * KCOLS)

        def stage_a(t, n):
            b, a = tiles[t]
            if n == 0:
                if a == 0:
                    gate_backward(b)
                dr, dr_t, qn_t, weighted = gated[b]
                doa[t] = jnp.where(heads[a], dr, 0.0).astype(BF16)
                doa_t[t] = dr_t[feats[a], :]
                qa_t[t] = qn_t[feats[a], :]
                delta[t] = jnp.sum(jnp.where(heads[a], weighted, 0.0), axis=-1, keepdims=True)
            if n < latent:
                dp_scr[buf(t), :, cols(n)] = mm_nt(doa[t], v_scr[keys(b, n), :])
                dvt_scr[kb[b] + n, feats[a], :] += mm(doa_t[t], pl_ref[a, rows[b], cols(n)])
            else:
                dp_scr[buf(t), :, cols(n)] = mm_nt(doa[t], cv_scr[...])
                dcvt_scr[feats[a], :] += mm(doa_t[t], pc_ref[a, rows[b], :])

        def stage_b(t, r):
            b, a = tiles[t]
            rs = slice(r * SOFTMAX_ROWS, (r + 1) * SOFTMAX_ROWS)
            in_rows = slice(b * QBLK + rs.start, b * QBLK + rs.stop)
            d = dp_scr[buf(t), rs, :] - delta[t][rs, :]
            ds_lat = pl_ref[a, in_rows, :].astype(F32) * d[:, :KBLK]
            ds_ctx = pc_ref[a, in_rows, :].astype(F32) * d[:, KBLK:]
            db_ref[variant[b], a, rs, :] += ds_lat
            ds_scr[buf(t), rs, :KBLK] = ds_lat.astype(BF16)
            ds_scr[buf(t), rs, KBLK:] = ds_ctx.astype(BF16)

        def stage_c(t, n):
            b, a = tiles[t]
            ds = ds_scr[buf(t), :, cols(n)]
            if n < latent:
                part = mm(ds, kn_scr[keys(b, n), :])
                dknt_scr[kb[b] + n, feats[a], :] += mm(qa_t[t], ds)
            else:
                part = mm(ds, ckn_scr[...])
                dcknt_scr[feats[a], :] += mm(qa_t[t], ds)
            dqn[t] = part if dqn[t] is None else dqn[t] + part
            if n == latent and a == 1:
                both = jnp.where(heads[0], dqn[t - 1], 0.0) + jnp.where(heads[1], dqn[t], 0.0)
                dq, dqg = jax.vjp(_scaled_q, q_ref[rows[b], :], qg_ref[...])[1](both)
                dq_ref[rows[b], :] = dq.astype(BF16)
                dqg_ref[...] += dqg

        pieces = range(latent + 1)
        for n in pieces:
            stage_a(0, n)
        for t in range(len(tiles)):
            matmuls = []
            for n in pieces:
                if t + 1 < len(tiles):
                    matmuls.append(functools.partial(stage_a, t + 1, n))
                if t > 0:
                    matmuls.append(functools.partial(stage_c, t - 1, n))
            _emit_interleaved([functools.partial(stage_b, t, r) for r in range(QBLK // SOFTMAX_ROWS)], matmuls)
        for n in pieces:
            stage_c(len(tiles) - 1, n)

        @pl.when(last)
        def _():
            eye = (lax.broadcasted_iota(jnp.int32, (KCOLS, KCOLS), 0)
                   == lax.broadcasted_iota(jnp.int32, (KCOLS, KCOLS), 1)).astype(BF16)

            def turned(x):
                hi = x.astype(BF16)
                return mm_nt(eye, hi) + mm_nt(eye, x - hi.astype(F32))

            def body(c, dkg):
                sl = pl.ds(pl.multiple_of(c * NORM_ROWS, NORM_ROWS), NORM_ROWS)
                blocks = range(NORM_ROWS // KCOLS)
                dkn = jnp.concatenate([turned(dknt_scr[c * len(blocks) + n]) for n in blocks], axis=0)
                dv = jnp.concatenate([mm_nt(eye, dvt_scr[c * len(blocks) + n]) for n in blocks], axis=0)
                dk, dg = _pair_rms_bwd(k_ref[sl, :], kg_ref[...], dkn)
                dk_ref[sl, :] = dk.astype(BF16)
                dv_ref[sl, :] = dv.astype(BF16)
                return dkg + dg

            dkg = lax.fori_loop(0, SEQ // NORM_ROWS, body, jnp.zeros((1, 128), F32))
            dck, dg = _pair_rms_bwd(ck_ref[...], kg_ref[...], dcknt_scr[...].T)
            dck_ref[...] = dck
            dcv_ref[...] = dcvt_scr[...].T
            dkg_ref[...] += dkg + dg
            for a in range(2):
                rows_of_rpb = _rpb_grad(lambda v, tile_rows, tile_cols, a=a: db_ref[v, a, tile_rows, tile_cols])
                for d, row in enumerate(rows_of_rpb):
                    drpb_ref[a, d:d + 1, :] = row

        @pl.when(last & (p == NPAIR - 1))
        def _():
            dqg_ref[...] = dqg_ref[...] + pltpu.roll(dqg_ref[...], HDIM, 1)
            dkg_ref[...] = dkg_ref[...] + pltpu.roll(dkg_ref[...], HDIM, 1)

    blk = lambda rows: pl.BlockSpec((rows, 128), lambda p, i: (0, p))
    qblk = pl.BlockSpec((ATTN_ROWS, 128), lambda p, i: (i, p))
    return pl.pallas_call(
        kern, name="attn_bwd", grid=(NPAIR, ATTN_STEPS),
        in_specs=_attn_in_specs() + [_row(128), _row(128), pl.BlockSpec((ATTN_ROWS, 128), lambda p, i: (i, 4 + p)),
                                     _pair_major_spec(), _pair_major_spec()] + _prob_specs() + _normed_key_specs(),
        out_specs=[qblk, blk(SEQ), blk(SEQ), qblk, blk(CTX), blk(CTX), _rpb_spec(), _row(128), _row(128)],
        out_shape=[jax.ShapeDtypeStruct((SEQ, 512), BF16)] * 4 + [jax.ShapeDtypeStruct((CTX, 512), F32)] * 2
        + [jax.ShapeDtypeStruct((HEADS, 15, 128), F32)]
        + [jax.ShapeDtypeStruct((1, 128), F32), jax.ShapeDtypeStruct((1, 128), F32)],
        scratch_shapes=[pltpu.VMEM((SEQ, 128), BF16), pltpu.VMEM((CTX, 128), BF16),
                        pltpu.VMEM((SEQ // KCOLS, 128, KCOLS), F32), pltpu.VMEM((SEQ // KCOLS, 128, KCOLS), F32),
                        pltpu.VMEM((128, CTX), F32), pltpu.VMEM((128, CTX), F32),
                        pltpu.VMEM((TILE_BUFFERS, QBLK, KBLK + CTX), F32),
                        pltpu.VMEM((TILE_BUFFERS, QBLK, KBLK + CTX), BF16),
                        pltpu.VMEM((3, 2, QBLK, KBLK), F32)],
        compiler_params=_cparams(("arbitrary", "arbitrary"), VMEM_BIG),
    )(z, z, z, z, zc, zc, qg2, kg2, dcat, *saved)


def outproj(z, sg, ws, bsb, out_b, x, target, gate, wo):
    tl = SGU_CHUNK * SGU_PER_STEP
    nt = SEQ // tl

    def kern(au0_ref, av0_ref, ag0_ref, au1_ref, av1_ref, ag1_ref, sg_ref, ws_ref, bs_ref, b_ref, x_ref, t_ref, g_ref,
             w_ref, loss_ref, dy_ref, dcat_ref, dg_ref, dw_ref, a_scr):
        t = pl.program_id(0)
        cur, nxt = lax.rem(t, 2), lax.rem(t + 1, 2)

        def gating(refs, slot, cn):
            sl = slice(cn * SGU_CHUNK, (cn + 1) * SGU_CHUNK)
            au_ref, av_ref, ag_ref = refs
            a_scr[slot, sl, :] = _sgu_chunk(au_ref[sl, :], av_ref[sl, :], ag_ref[sl, :], sg_ref[...], ws_ref[...],
                                            bs_ref[...]).astype(BF16)

        @pl.when(t == 0)
        def _():
            loss_ref[...] = jnp.zeros_like(loss_ref)
            dg_ref[...] = jnp.zeros_like(dg_ref)
            dw_ref[...] = jnp.zeros_like(dw_ref)
            for cn in range(SGU_PER_STEP):
                gating((au0_ref, av0_ref, ag0_ref), 0, cn)

        a, b = a_scr[cur], b_ref[...].astype(BF16)
        mix = (jnp.dot(a, w_ref[0:512, :], preferred_element_type=F32)
               + jnp.dot(b, w_ref[512:1024, :], preferred_element_type=F32))
        err = x_ref[...] + g_ref[...] * mix - t_ref[...]
        loss_ref[...] += 0.5 * jnp.sum(jnp.mean(err * err, axis=-1))
        dy = err * (1.0 / DM)
        dy_ref[...] = dy
        dg_ref[...] += jnp.sum(dy * mix, axis=0, keepdims=True)
        dmix = (g_ref[...] * dy).astype(BF16)

        def dcat_half(n):
            part = slice(512 * n, 512 * (n + 1))
            dcat_ref[:, part] = lax.dot_general(dmix, w_ref[part, :], _NT, preferred_element_type=F32)

        def dw_half(n, src):
            dw_ref[512 * n:512 * (n + 1), :] += lax.dot_general(src, dmix, (((0,), (0,)), ((), ())),
                                                                preferred_element_type=F32)

        _emit_interleaved([functools.partial(gating, (au1_ref, av1_ref, ag1_ref), nxt, cn) for cn in range(SGU_PER_STEP)],
                          [functools.partial(dcat_half, 0), functools.partial(dcat_half, 1),
                           functools.partial(dw_half, 0, a), functools.partial(dw_half, 1, b)])

    tile = lambda w: pl.BlockSpec((tl, w), lambda t: (t, 0))
    whole = pl.BlockSpec((DM, DM), lambda t: (0, 0))
    zfirst = [pl.BlockSpec((tl, 512), functools.partial(lambda c, t: (0, c), c)) for c in range(3)]
    znext = [pl.BlockSpec((tl, 512), functools.partial(lambda c, t: (jnp.minimum(t + 1, nt - 1), c), c))
             for c in range(3)]
    wspec = pl.BlockSpec((4, 128, 128), lambda t: (0, 0, 0))
    return pl.pallas_call(
        kern, name="outproj", grid=(nt,),
        in_specs=zfirst + znext + [_row(512), wspec, wspec, tile(512), tile(DM), tile(DM), _row(DM), whole],
        out_specs=[pl.BlockSpec((8, 128), lambda t: (0, 0)), tile(DM), tile(DM), _row(DM), whole],
        out_shape=[jax.ShapeDtypeStruct((8, 128), F32), jax.ShapeDtypeStruct((SEQ, DM), F32),
                   jax.ShapeDtypeStruct((SEQ, DM), F32), jax.ShapeDtypeStruct((1, DM), F32),
                   jax.ShapeDtypeStruct((DM, DM), F32)],
        scratch_shapes=[pltpu.VMEM((2, tl, 512), BF16)],
        compiler_params=_cparams(("arbitrary",), 48 * 1024 * 1024),
    )(z, z, z, z, z, z, sg, ws, bsb, out_b, x, target, gate, wo)


DZ_COLS = (("a", 0, 1536), ("q", 1536, 2048), ("k", 2048, 2560), ("v", 2560, 3072), ("g", 3072, DIN))
DZC_COLS = (("k", 2048, 2560), ("v", 2560, 3072))
_NT = (((1,), (1,)), ((), ()))


DH_SUBTILES = 2


def _dz_specs(tl):
    return [pl.BlockSpec((tl, 1536), lambda t: (t, 0))] + [pl.BlockSpec((tl, 512), lambda t: (t, 0))] * 4


def dh_bwd(dz_parts, w_full, x, dy, shift, scale, norm_g, dg_ctx):
    tl = 512
    nt = SEQ // tl

    def kern(a_ref, q_ref, k_ref, v_ref, g_ref, w_ref, x_ref, dy_ref, sh_ref, sc_ref, gn_ref, dgc_ref,
             gx_ref, dsh_ref, dsc_ref, dg_ref):
        @pl.when(pl.program_id(0) == 0)
        def _():
            dsh_ref[...] = jnp.zeros_like(dsh_ref)
            dsc_ref[...] = jnp.zeros_like(dsc_ref)
            dg_ref[...] = dgc_ref[...]

        src = dict(a=a_ref, q=q_ref, k=k_ref, v=v_ref, g=g_ref)
        for sub in range(DH_SUBTILES):
            rows = slice(sub * tl // DH_SUBTILES, (sub + 1) * tl // DH_SUBTILES)
            dh = None
            for name, c0, c1 in DZ_COLS:
                part = lax.dot_general(src[name][rows, :], w_ref[:, c0:c1], _NT, preferred_element_type=F32)
                dh = part if dh is None else dh + part
            _, vjp = jax.vjp(_modulated, x_ref[rows, :], gn_ref[...], sc_ref[...], sh_ref[...])
            dx, dg, dsc, dsh = vjp(dh)
            gx_ref[rows, :] = dy_ref[rows, :] + dx
            dg_ref[...] += dg
            dsc_ref[...] += dsc
            dsh_ref[...] += dsh

    tile = pl.BlockSpec((tl, DM), lambda t: (t, 0))
    return pl.pallas_call(
        kern, name="dh_bwd", grid=(nt,),
        in_specs=_dz_specs(tl) + [pl.BlockSpec((DM, DIN), lambda t: (0, 0)), tile, tile, _row(DM),
                                  _row(DM), _row(DM), _row(DM)],
        out_specs=[tile, _row(DM), _row(DM), _row(DM)],
        out_shape=[jax.ShapeDtypeStruct((SEQ, DM), F32)] + [jax.ShapeDtypeStruct((1, DM), F32)] * 3,
        compiler_params=_cparams(("arbitrary",), 48 * 1024 * 1024),
    )(*dz_parts, w_full, x, dy, shift, scale, norm_g, dg_ctx)


def dw_bwd(h, z, sg, ws, bsb, dcat, dz_attn, hc, dck, dcv, g_out):
    tl = SGU_CHUNK * SGU_PER_STEP
    nt = SEQ // tl
    (rhi, wi), (rho, wo) = RS_SHAPES

    def kern(h_ref, au_ref, av_ref, ag_ref, sg_ref, ws_ref, bs_ref, do_ref, q_ref, k_ref, v_ref, g_ref,
             hc_ref, dck_ref, dcv_ref, go_hbm,
             wire_i, keep_i, wire_o, keep_o, a_ref, dsg_ref, dws_ref, dbs_ref,
             acc, rcv_i, mine_o, rcv_o, load_sem, send_sems, recv_sems):
        t = pl.program_id(0)
        x, y, c = _me()
        k = 2 * x + y
        sib = _flip(1)
        half = lambda hh, rh: pl.ds(pl.multiple_of(hh * rh, rh), rh)
        load_o = pltpu.make_async_copy(go_hbm.at[:, half(c, rho), :], mine_o, load_sem)
        pair_o = _rcopy(go_hbm.at[:, half(1 - c, rho), :], rcv_o, send_sems, recv_sems, 0, sib)
        pair_i = [_rcopy(wire_i.at[j], rcv_i.at[j], send_sems, recv_sems, 1 + j, sib) for j in range(NCHIP)]

        @pl.when(t == 0)
        def _():
            load_o.start()
            pair_o.start()
            acc[...] = jnp.zeros_like(acc)
            dsg_ref[...] = jnp.zeros_like(dsg_ref)
            dws_ref[...] = jnp.zeros_like(dws_ref)
            dbs_ref[...] = jnp.zeros_like(dbs_ref)
            hct = hc_ref[...].T
            csrc = dict(k=dck_ref, v=dcv_ref)
            for name, c0, c1 in DZC_COLS:
                acc[:, c0:c1] += jnp.dot(hct, csrc[name][...].astype(BF16), preferred_element_type=F32)

        ht = h_ref[...].T
        src = dict(a=a_ref, q=q_ref, k=k_ref, v=v_ref, g=g_ref)

        def gating_backward(cn):
            sl = slice(cn * SGU_CHUNK, (cn + 1) * SGU_CHUNK)
            _, vjp = jax.vjp(_sgu_chunk, au_ref[sl, :], av_ref[sl, :], ag_ref[sl, :], sg_ref[...], ws_ref[...],
                             bs_ref[...])
            dau, dav, dag, dsg, dws, dbs = vjp(do_ref[sl, :])
            a_ref[sl, 0:512] = dau.astype(BF16)
            a_ref[sl, 512:1024] = dav.astype(BF16)
            a_ref[sl, 1024:1536] = dag.astype(BF16)
            dsg_ref[...] += dsg
            dws_ref[...] += dws
            dbs_ref[...] += dbs

        def product(name, c0, c1):
            acc[:, c0:c1] += jnp.dot(ht, src[name][...], preferred_element_type=F32)

        _emit_interleaved([functools.partial(product, *cols) for cols in DZ_COLS[1:]],
                          [functools.partial(gating_backward, cn) for cn in range(SGU_PER_STEP)])
        product(*DZ_COLS[0])

        @pl.when(t == nt - 1)
        def _():
            dbs_ref[...] = jnp.broadcast_to(jnp.sum(dbs_ref[...], axis=-1, keepdims=True), dbs_ref.shape)
            shard = lambda j: slice(j * SHARD_IN, (j + 1) * SHARD_IN)
            for j in range(NCHIP):
                wire_i[j] = acc[half(1 - c, rhi), shard(j)].astype(BF16)
                pair_i[j].start()
            load_o.wait()
            pair_o.wait_recv()
            for j in range(NCHIP):
                wire_o[j] = (mine_o[j] + rcv_o[j]).astype(BF16)
            keep_o[...] = mine_o[k] + rcv_o[k]
            mine = half(c, rhi)
            for j in range(NCHIP):
                pair_i[j].wait_recv()
                pair_i[j].wait_send()
                pair_sum = acc[mine, shard(j)] + rcv_i[j].astype(F32)
                wire_i[j] = pair_sum.astype(BF16)

                @pl.when(k == j)
                def _():
                    keep_i[...] = pair_sum
            pair_o.wait_send()

    whole = lambda *shape: pl.BlockSpec(shape, lambda t: (0,) * len(shape))
    rows, sgu_specs = _sgu_specs()
    assert rows == tl
    a_spec, *attn_specs = _dz_specs(tl)
    return pl.pallas_call(
        kern, name="dw_bwd", grid=(nt,),
        in_specs=[pl.BlockSpec((tl, DM), lambda t: (t, 0))] + sgu_specs + [pl.BlockSpec((tl, 512), lambda t: (t, 0))]
        + attn_specs + [whole(CTX, DM), whole(CTX, 512), whole(CTX, 512), pl.BlockSpec(memory_space=pl.ANY)],
        out_specs=[whole(NCHIP, rhi, wi), whole(rhi, wi), whole(NCHIP, rho, wo), whole(rho, wo),
                   a_spec, _row(512), whole(4, 128, 128), whole(4, 128, 128)],
        out_shape=[jax.ShapeDtypeStruct((NCHIP, rhi, wi), BF16), jax.ShapeDtypeStruct((rhi, wi), F32),
                   jax.ShapeDtypeStruct((NCHIP, rho, wo), BF16), jax.ShapeDtypeStruct((rho, wo), F32),
                   jax.ShapeDtypeStruct((SEQ, 1536), BF16), jax.ShapeDtypeStruct((1, 512), F32),
                   jax.ShapeDtypeStruct((4, 128, 128), F32), jax.ShapeDtypeStruct((4, 128, 128), F32)],
        scratch_shapes=[pltpu.VMEM((DM, DIN), F32), pltpu.VMEM((NCHIP, rhi, wi), BF16),
                        pltpu.VMEM((NCHIP, rho, wo), F32), pltpu.VMEM((NCHIP, rho, wo), F32),
                        pltpu.SemaphoreType.DMA(()), pltpu.SemaphoreType.DMA((1 + NCHIP,)),
                        pltpu.SemaphoreType.DMA((1 + NCHIP,))],
        compiler_params=_cparams(("arbitrary",), 60 * 1024 * 1024),
    )(h, z, z, z, sg, ws, bsb, dcat, *dz_attn, hc, dck, dcv, g_out)


def ctx_bwd(dck, dcv, w_full, ctx, cshift, cscale, norm_g):
    def kern(dck_ref, dcv_ref, w_ref, c_ref, sh_ref, sc_ref, g_ref, dsh_ref, dsc_ref, dg_ref):
        csrc = dict(k=dck_ref, v=dcv_ref)
        dhc = None
        first = DZC_COLS[0][1]
        for name, c0, c1 in DZC_COLS:
            part = lax.dot_general(csrc[name][...].astype(BF16), w_ref[:, c0 - first:c1 - first], _NT,
                                   preferred_element_type=F32)
            dhc = part if dhc is None else dhc + part
        _, vjp = jax.vjp(lambda g, sc, sh: _modulated(c_ref[...], g, sc, sh), g_ref[...], sc_ref[...], sh_ref[...])
        dg_ref[...], dsc_ref[...], dsh_ref[...] = vjp(dhc)

    whole = lambda r, c: pl.BlockSpec((r, c), lambda i: (0, 0))
    return pl.pallas_call(
        kern, name="ctx_bwd", grid=(1,),
        in_specs=[whole(CTX, 512), whole(CTX, 512), pl.BlockSpec((DM, 1024), lambda i: (0, DZC_COLS[0][1] // 1024)),
                  whole(CTX, DM), _row(DM), _row(DM), _row(DM)],
        out_specs=[_row(DM), _row(DM), _row(DM)],
        out_shape=[jax.ShapeDtypeStruct((1, DM), F32)] * 3,
        compiler_params=_cparams(("arbitrary",), 40 * 1024 * 1024),
    )(dck, dcv, w_full, ctx, cshift, cscale, norm_g)


def _lane_pad_rpb(rpb):
    r = jnp.pad(rpb, ((0, 0), (0, 0), (0, GRID_W - rpb.shape[-1])))
    return jnp.concatenate([r, r], axis=-1)


def local_step(chip, dev, x, c_vec, c_ctx, w_ada, b_shard, ctx, target, norm_g, sgu_g, w_s, b_s, q_g, k_g, rpb,
               w_in_shard, w_out_shard):
    bsb = jnp.broadcast_to(b_s[:, :, None], (4, 128, 128))
    qg2, kg2 = jnp.tile(q_g, (1, 2)), jnp.tile(k_g, (1, 2))

    z, h, w_in_full, w_out_full, mod_all, cs = inproj_fwd(chip, x, c_vec, c_ctx, w_ada, b_shard, norm_g, w_in_shard,
                                                          w_out_shard)
    mods = mod_all.transpose(1, 0, 2).reshape(CS_ROWS, 3 * DM)
    mod = lax.dynamic_slice(mods, (8 * dev, 0), (1, 3 * DM))
    shift, scale, gate = mod[:, :DM], mod[:, DM:2 * DM], mod[:, 2 * DM:]
    cshift, cscale = mods[8 * NDEV:8 * NDEV + 1, :DM], mods[8 * NDEV:8 * NDEV + 1, DM:2 * DM]
    zc, hc = ctx_fwd(ctx, cshift, cscale, norm_g, w_in_full)
    out_b, *saved = attn_fwd(z, zc, _lane_pad_rpb(rpb), qg2, kg2)
    loss8, dy, dcat, dgate, dwo = outproj(z, sgu_g, w_s, bsb, out_b, x, target, gate, w_out_full.reshape(DM, DM))
    dq, dk, dv, dbg, dck, dcv, drpb, dqg2, dkg2 = attn_bwd(z, zc, qg2, kg2, dcat, saved)
    drpb = drpb[:, :, :rpb.shape[-1]]
    dcshift, dcscale, dng_c = ctx_bwd(dck, dcv, w_in_full, ctx, cshift, cscale, norm_g)
    wire_i, keep_i, wire_o, keep_o, dz_a, dsg, dws, dbsb = dw_bwd(
        h, z, sgu_g, w_s, bsb, dcat, (dq, dk, dv, dbg), hc, dck, dcv, dwo.reshape(NCHIP, SHARD_OUT, DM))
    dz_parts = (dz_a, dq, dk, dv, dbg)
    *in_flight, token = rs_start(wire_i, wire_o)
    grad_x, dshift, dscale, dng = dh_bwd(dz_parts, w_in_full, x, dy, shift, scale, norm_g, dng_c + token[0, 0])
    got_i, got_o = rs_wait(*in_flight, dshift)
    return dict(
        loss=loss8[0:1, 0:1], grad_x=grad_x, rs=(keep_i, got_i, keep_o, got_o), cs=cs,
        dmod=jnp.concatenate([dshift, dscale, dgate], axis=-1),
        dcmod=jnp.concatenate([dcshift, dcscale, jnp.zeros((1, DM), F32)], axis=-1),
        d_norm_g=dng, d_sgu_g=dsg, d_w_s=dws, d_b_s=dbsb[:, :, 0],
        d_q_g=dqg2[:, :HDIM], d_k_g=dkg2[:, :HDIM], d_rpb=drpb)


def _me():
    return lax.axis_index("x"), lax.axis_index("y"), lax.axis_index("c")


def _flip(q):
    x, y, c = _me()
    return ((1 - x) if q & 4 else x, (1 - y) if q & 2 else y, (1 - c) if q & 1 else c)


def _chip_of(dev):
    return 2 * dev[0] + dev[1]


def _rcopy(src, dst, send_sems, recv_sems, k, dev):
    return pltpu.make_async_remote_copy(src_ref=src, dst_ref=dst, send_sem=send_sems.at[k], recv_sem=recv_sems.at[k],
                                        device_id=dev, device_id_type=MESH_ID)


_VMEM_SPEC = pl.BlockSpec(memory_space=pltpu.VMEM)
SLAB_ROWS = 80


RS_SHAPES = ((DM // 2, SHARD_IN), (SHARD_OUT // 2, DM))
_HBM_SPEC = pl.BlockSpec(memory_space=pltpu.HBM)
_SEM_SPEC = pl.BlockSpec(memory_space=pltpu.SEMAPHORE)
_IN_FLIGHT = pltpu.SideEffectType.DATAFLOW_SIDE_EFFECTING


def _rs_copies(wires, lands, send_sems, recv_sems):
    return [pltpu.make_async_remote_copy(
        src_ref=wires[n].at[_chip_of(_flip(q))], dst_ref=lands[n].at[q // 2 - 1],
        send_sem=send_sems.at[3 * n + q // 2 - 1], recv_sem=recv_sems.at[3 * n + q // 2 - 1],
        device_id=_flip(q), device_id_type=MESH_ID) for n in (0, 1) for q in (2, 4, 6)]


def rs_start(wire_i, wire_o):
    lands = [lax.empty((NCHIP - 1, rh, w), BF16) for rh, w in RS_SHAPES]

    def body(wi_ref, wo_ref, li_ref, lo_ref, send_sems, recv_sems, wi_thru, wo_thru, li_thru, lo_thru, token):
        for cp in _rs_copies((wi_ref, wo_ref), (li_ref, lo_ref), send_sems, recv_sems):
            cp.start()
        token[...] = jnp.zeros_like(token)

    hbm = lambda a: pltpu.HBM(a.shape, a.dtype)
    return pl.pallas_call(
        body, name="rs_start",
        out_shape=(pltpu.SemaphoreType.DMA((6,)), pltpu.SemaphoreType.DMA((6,)), hbm(wire_i), hbm(wire_o),
                   hbm(lands[0]), hbm(lands[1]), jax.ShapeDtypeStruct((8, 128), F32)),
        in_specs=(_HBM_SPEC,) * 4, out_specs=(_SEM_SPEC, _SEM_SPEC) + (_HBM_SPEC,) * 4 + (_VMEM_SPEC,),
        input_output_aliases={0: 2, 1: 3, 2: 4, 3: 5},
        compiler_params=pltpu.CompilerParams(has_side_effects=_IN_FLIGHT),
    )(*[pltpu.with_memory_space_constraint(a, pltpu.HBM) for a in (wire_i, wire_o, *lands)])


def rs_wait(send_sems, recv_sems, wire_i, wire_o, land_i, land_o, after):
    def body(wi_ref, wo_ref, li_ref, lo_ref, send_sems, recv_sems, after_ref, wi_dead, wo_dead, gi_ref, go_ref):
        for cp in _rs_copies((wi_ref, wo_ref), (li_ref, lo_ref), send_sems, recv_sems):
            cp.wait_send()
            cp.wait_recv()

    hbm = lambda a: pltpu.HBM(a.shape, a.dtype)
    return pl.pallas_call(
        body, name="rs_wait", out_shape=(hbm(wire_i), hbm(wire_o), hbm(land_i), hbm(land_o)),
        in_specs=(_HBM_SPEC,) * 4 + (_SEM_SPEC, _SEM_SPEC, pl.BlockSpec(memory_space=pl.ANY)),
        out_specs=(_HBM_SPEC,) * 4, input_output_aliases={0: 0, 1: 1, 2: 2, 3: 3},
        compiler_params=pltpu.CompilerParams(has_side_effects=_IN_FLIGHT),
    )(wire_i, wire_o, land_i, land_o, send_sems, recv_sems, after)[2:]


def final_reduce(keep_i, got_i, keep_o, got_o, slab, cs, w_ada, c_ctx):
    (rhi, wi), (rho, wo) = RS_SHAPES

    def kern(ki_hbm, gi_hbm, ko_hbm, go_hbm, s_ref, cs_ref, w_hbm, cc_ref,
             gin_ref, gout_ref, tot_ref, dw_ref, db_ref, dcc_ref,
             ki, gi, ko, go, w_scr, all_ref, dms_scr, parts, load_sems, send_sems, recv_sems):
        x, y, c = _me()
        k = 2 * x + y
        sib = _flip(1)
        dev = lambda d: 4 * d[0] + 2 * d[1] + d[2]
        me = dev((x, y, c))

        def slab_copy(idx, owner, to):
            return _rcopy(all_ref.at[dev(owner)], all_ref.at[dev(owner)], send_sems, recv_sems, idx, to)

        all_ref[me] = s_ref[...]
        first = [slab_copy(0, (x, y, c), sib)] + [slab_copy(q // 2, (x, y, c), _flip(q)) for q in (2, 4, 6)]
        for cp in first:
            cp.start()
        loads = [pltpu.make_async_copy(src, dst, load_sems.at[n]) for n, (src, dst) in enumerate(
            ((ki_hbm, ki), (gi_hbm, gi), (ko_hbm, ko), (go_hbm, go), (w_hbm, w_scr)))]
        for cp in loads:
            cp.start()

        shares = []
        for n, (keep, got, out) in enumerate(((ki, gi, gin_ref), (ko, go, gout_ref))):
            rh = RS_SHAPES[n][0]
            half = lambda hh, rh=rh: pl.ds(pl.multiple_of(hh * rh, rh), rh)
            loads[2 * n].wait()
            loads[2 * n + 1].wait()
            out[half(c), :] = ((keep[...] + got[0].astype(F32)) + got[1].astype(F32)) + got[2].astype(F32)
            share = _rcopy(out.at[half(c), :], out.at[half(c), :], send_sems, recv_sems, 7 + n, sib)
            share.start()
            shares.append((share, _rcopy(out.at[half(1 - c), :], out.at[half(1 - c), :], send_sems, recv_sems, 7 + n,
                                         sib)))

        passed = []
        for q in (2, 4, 6):
            slab_copy(q // 2, _flip(q), (x, y, c)).wait_recv()
            cp = slab_copy(3 + q // 2, _flip(q), sib)
            cp.start()
            passed.append(cp)
        slab_copy(0, sib, (x, y, c)).wait_recv()
        for q in (2, 4, 6):
            slab_copy(3 + q // 2, _flip(q | 1), (x, y, c)).wait_recv()
        tot = all_ref[0]
        for d in range(1, NDEV):
            tot = tot + all_ref[d]
        tot_ref[...] = tot

        pad = jnp.zeros((7, DM), F32)
        dm = [jnp.concatenate([all_ref[d, 12 + j:13 + j, :] for d in range(NDEV)] + [tot[9 + j:10 + j, :], pad], axis=0)
              for j in range(3)]
        db_ref[...] = jnp.concatenate([jnp.sum(part, axis=0, keepdims=True) for part in dm], axis=0)
        dm = jnp.concatenate(dm, axis=-1)
        for j in range(NCHIP):
            @pl.when(k == j)
            def _():
                dms_scr[...] = dm[:, j * SHARD_ADA:(j + 1) * SHARD_ADA].astype(BF16)

        a_in = jnp.concatenate([cs_ref[8 * d:8 * d + 1, :] for d in range(NDEV)]
                               + [cs_ref[8 * NDEV:8 * NDEV + 1, :], pad], axis=0)
        act = jax.nn.silu(a_in).astype(BF16)
        dms = dms_scr[...]
        dw_ref[...] = lax.dot_general(act, dms, (((0,), (0,)), ((), ())), preferred_element_type=F32)
        loads[4].wait()
        parts[k] = lax.dot_general(dms, w_scr[...].astype(BF16), (((1,), (1,)), ((), ())), preferred_element_type=F32)
        sends = [_rcopy(parts.at[k], parts.at[k], send_sems, recv_sems, 8 + q // 2, _flip(q)) for q in (2, 4, 6)]
        for cp in sends:
            cp.start()
        for q in (2, 4, 6):
            kq = _chip_of(_flip(q))
            _rcopy(parts.at[kq], parts.at[kq], send_sems, recv_sems, 8 + q // 2, _flip(q)).wait_recv()
        dact = ((parts[0] + parts[1]) + parts[2]) + parts[3]
        _, vjp = jax.vjp(jax.nn.silu, cc_ref[...])
        dcc_ref[...] = vjp(dact[8:9, :])[0]

        for share, arrival in shares:
            arrival.wait_recv()
            share.wait_send()
        for cp in first + passed + sends:
            cp.wait_send()

    any_spec = pl.BlockSpec(memory_space=pl.ANY)
    return pl.pallas_call(
        kern, name="final_reduce",
        in_specs=[any_spec] * 4 + [_VMEM_SPEC, _VMEM_SPEC, any_spec, _VMEM_SPEC], out_specs=[_VMEM_SPEC] * 6,
        out_shape=[jax.ShapeDtypeStruct((2 * rhi, wi), F32), jax.ShapeDtypeStruct((2 * rho, wo), F32),
                   jax.ShapeDtypeStruct((SLAB_ROWS, DM), F32), jax.ShapeDtypeStruct((DM, SHARD_ADA), F32),
                   jax.ShapeDtypeStruct((3, DM), F32), jax.ShapeDtypeStruct((1, DM), F32)],
        scratch_shapes=[pltpu.VMEM((rhi, wi), F32), pltpu.VMEM((NCHIP - 1, rhi, wi), BF16),
                        pltpu.VMEM((rho, wo), F32), pltpu.VMEM((NCHIP - 1, rho, wo), BF16),
                        pltpu.VMEM((DM, SHARD_ADA), F32), pltpu.VMEM((NDEV, SLAB_ROWS, DM), F32),
                        pltpu.VMEM((16, SHARD_ADA), BF16), pltpu.VMEM((NCHIP, 16, DM), F32),
                        pltpu.SemaphoreType.DMA((5,)), pltpu.SemaphoreType.DMA((12,)), pltpu.SemaphoreType.DMA((12,))],
        compiler_params=pltpu.CompilerParams(vmem_limit_bytes=40 * 1024 * 1024),
    )(keep_i, got_i, keep_o, got_o, slab, cs, w_ada, c_ctx)


def _adamw_math(w, g, m, v):
    m = B1 * m + (1.0 - B1) * g
    v = B2 * v + (1.0 - B2) * (g * g)
    m_hat = m / (1.0 - B1 ** STEP)
    v_hat = v / (1.0 - B2 ** STEP)
    return -LR * (m_hat / (jnp.sqrt(v_hat) + ADAM_EPS) + WD * w), m, v


def adamw_big(w, g, m, v, name, block_rows=256):
    rows, width = w.shape

    def kern(w_ref, g_ref, m_ref, v_ref, d_ref, nm_ref, nv_ref):
        d_ref[...], nm_ref[...], nv_ref[...] = _adamw_math(w_ref[...], g_ref[...], m_ref[...], v_ref[...])

    spec = pl.BlockSpec((block_rows, width), lambda i: (i, 0))
    return pl.pallas_call(
        kern, name=name, grid=(rows // block_rows,), in_specs=[spec] * 4, out_specs=[spec] * 3,
        out_shape=[jax.ShapeDtypeStruct((rows, width), F32)] * 3,
        compiler_params=_cparams(("arbitrary",)),
    )(w, g, m, v)


def adamw_small(quads):
    n = len(quads)

    def kern(*refs):
        ins, outs = refs[:4 * n], refs[4 * n:]
        for i in range(n):
            w, g, m, v = (r[...] for r in ins[4 * i:4 * i + 4])
            outs[3 * i][...], outs[3 * i + 1][...], outs[3 * i + 2][...] = _adamw_math(w, g, m, v)

    flat = [a for quad in quads for a in quad]
    res = pl.pallas_call(
        kern, name="adamw_small", in_specs=[_VMEM_SPEC] * (4 * n), out_specs=[_VMEM_SPEC] * (3 * n),
        out_shape=[jax.ShapeDtypeStruct(q[0].shape, F32) for q in quads for _ in range(3)],
    )(*flat)
    return [tuple(res[3 * i:3 * i + 3]) for i in range(n)]


def _rows_of(a, rows):
    flat = a.reshape(-1)
    return jnp.pad(flat, (0, rows * DM - flat.shape[0])).reshape(rows, DM)


def kernel(x, c, ctx, c_ctx, w_ada, b_ada, norm_g, w_in, sgu_norm_g, w_spatial, b_spatial, q_norm_g, k_norm_g, rpb, w_out, loss_target, m_c_ctx, m_w_ada, m_b_ada, m_norm_g, m_w_in, m_sgu_norm_g, m_w_spatial, m_b_spatial, m_q_norm_g, m_k_norm_g, m_rpb, m_w_out, v_c_ctx, v_w_ada, v_b_ada, v_norm_g, v_w_in, v_sgu_norm_g, v_w_spatial, v_b_spatial, v_q_norm_g, v_k_norm_g, v_rpb, v_w_out):
    xi, yi, ci = lax.axis_index("x"), lax.axis_index("y"), lax.axis_index("c")
    chip, dev = 2 * xi + yi, 4 * xi + 2 * yi + ci
    c_ctx2 = c_ctx.reshape(1, DM)

    b_shard = lax.dynamic_slice(b_ada, (0, chip * SHARD_ADA), (1, SHARD_ADA))
    part = local_step(chip.reshape(1).astype(jnp.int32), dev, x[0], c, c_ctx2, w_ada[0], b_shard, ctx[0], loss_target[0],
                      norm_g, sgu_norm_g, w_spatial[0], b_spatial[0], q_norm_g, k_norm_g, rpb[0], w_in[0], w_out[0])
    cs = part["cs"]

    slab = jnp.concatenate([
        part["d_norm_g"], _rows_of(part["d_sgu_g"], 1), _rows_of(part["d_b_s"], 1),
        _rows_of(jnp.concatenate([part["d_q_g"], part["d_k_g"]], axis=-1), 1), _rows_of(part["d_rpb"], 4),
        _rows_of(part["loss"], 1), _rows_of(part["dcmod"], 3), _rows_of(part["dmod"], 3), jnp.zeros((1, DM), F32),
        _rows_of(part["d_w_s"], 64)], axis=0)
    g_w_in, g_w_out, tot, g_w_ada, g_b_ada, g_c_ctx = final_reduce(*part["rs"], slab, cs, w_ada[0], c_ctx2)
    g_b_ada = g_b_ada.reshape(1, 3 * DM)

    loss = tot[8, 0]
    g_small = dict(
        c_ctx=g_c_ctx, b_ada=g_b_ada, norm_g=tot[0:1], sgu_norm_g=tot[1:2, :512], w_spatial=tot[16:80].reshape(512, 128),
        b_spatial=tot[2:3, :512].reshape(4, 128), q_norm_g=tot[3:4, :HDIM], k_norm_g=tot[3:4, HDIM:2 * HDIM],
        rpb=tot[4:8].reshape(-1)[:HEADS * 15 * 31].reshape(HEADS * 15, 31))
    shapes = dict(c_ctx=(DM,), w_ada=(1, DM, SHARD_ADA), b_ada=(1, 3 * DM), norm_g=(1, DM), w_in=(1, DM, SHARD_IN),
                  sgu_norm_g=(1, 512), w_spatial=(1, 4, 128, 128), b_spatial=(1, 4, 128), q_norm_g=(1, HDIM),
                  k_norm_g=(1, HDIM), rpb=(1, HEADS, 15, 31), w_out=(1, SHARD_OUT, DM))
    names = list(shapes)
    weights = dict(c_ctx=c_ctx, w_ada=w_ada, b_ada=b_ada, norm_g=norm_g, w_in=w_in, sgu_norm_g=sgu_norm_g,
                   w_spatial=w_spatial, b_spatial=b_spatial, q_norm_g=q_norm_g, k_norm_g=k_norm_g, rpb=rpb, w_out=w_out)
    m_in = dict(zip(names, (m_c_ctx, m_w_ada, m_b_ada, m_norm_g, m_w_in, m_sgu_norm_g, m_w_spatial, m_b_spatial,
                            m_q_norm_g, m_k_norm_g, m_rpb, m_w_out)))
    v_in = dict(zip(names, (v_c_ctx, v_w_ada, v_b_ada, v_norm_g, v_w_in, v_sgu_norm_g, v_w_spatial, v_b_spatial,
                            v_q_norm_g, v_k_norm_g, v_rpb, v_w_out)))
    grads = dict(g_small, w_ada=g_w_ada, w_in=g_w_in, w_out=g_w_out)
    upd = {}
    for n in ("w_ada", "w_in", "w_out"):
        g = grads[n]
        upd[n] = adamw_big(weights[n].reshape(g.shape), g, m_in[n].reshape(g.shape), v_in[n].reshape(g.shape),
                           "adamw_" + n)
    small = [n for n in names if n not in upd]
    res = adamw_small([(weights[n].reshape(grads[n].shape), grads[n], m_in[n].reshape(grads[n].shape),
                        v_in[n].reshape(grads[n].shape)) for n in small])
    upd.update(zip(small, res))
    out = [loss, part["grad_x"].reshape(1, SEQ, DM)]
    out += [grads[n].reshape(shapes[n]) for n in names]
    for slot in range(3):
        out += [upd[n][slot].reshape(shapes[n]) for n in names]
    return tuple(out)
```

```python
import functools

import jax
import jax.numpy as jnp
from jax import lax
from jax.experimental import pallas as pl
from jax.experimental.pallas import tpu as pltpu

F32, BF16 = jnp.float32, jnp.bfloat16
SEQ, DM, CTX, DIN = 4096, 1024, 256, 3584
NCHIP, NDEV = 4, 8
SHARD_IN = DIN // NCHIP
SHARD_ADA = 3 * DM // NCHIP
SHARD_OUT = DM // NCHIP
GRID_W = 64
QROWS = 4
KROWS = 12
QBLK, KBLK = QROWS * GRID_W, KROWS * GRID_W
NQBLK = SEQ // QBLK
HEADS, HDIM, NPAIR = 8, 64, 4
EPS = 1e-6
NEG_INF = -1e30
ZQ, ZK, ZV, ZG = 12, 16, 20, 24
LR, B1, B2, ADAM_EPS, WD, STEP = 0.001, 0.9, 0.999, 1e-08, 0.01, 10
VMEM_BIG = 56 * 1024 * 1024
MESH_ID = pl.DeviceIdType.MESH


def _dot(a, b, lhs_c, rhs_c):
    return lax.dot_general(a.astype(BF16), b.astype(BF16), (((lhs_c,), (rhs_c,)), ((), ())),
                           preferred_element_type=F32)


@jax.custom_vjp
def mm(a, b):
    return _dot(a, b, 1, 0)


@jax.custom_vjp
def mm_nt(a, b):
    return _dot(a, b, 1, 1)


@jax.custom_vjp
def mm_tn(a, b):
    return _dot(a, b, 0, 0)


mm.defvjp(lambda a, b: (mm(a, b), (a, b)), lambda r, ct: (mm_nt(ct, r[1]), mm_tn(r[0], ct)))
mm_nt.defvjp(lambda a, b: (mm_nt(a, b), (a, b)), lambda r, ct: (mm(ct, r[1]), mm_tn(ct, r[0])))
mm_tn.defvjp(lambda a, b: (mm_tn(a, b), (a, b)), lambda r, ct: (mm_nt(r[1], ct), mm(r[0], ct)))


def _rms(x, g):
    return x * lax.rsqrt(jnp.mean(x * x, axis=-1, keepdims=True) + EPS) * g


def _modulated(x, g, scale, shift):
    return _rms(x, g) * (1.0 + scale) + shift


def _pair_rms(x, g2):
    lo = lax.broadcasted_iota(jnp.int32, (1, 2 * HDIM), 1) < HDIM
    sq = x * x
    s_lo = jnp.sum(jnp.where(lo, sq, 0.0), axis=-1, keepdims=True)
    s_hi = jnp.sum(jnp.where(lo, 0.0, sq), axis=-1, keepdims=True)
    rs = jnp.where(lo, lax.rsqrt(s_lo / HDIM + EPS), lax.rsqrt(s_hi / HDIM + EPS))
    return x * rs * g2


def _cparams(sem, vmem=None):
    return pltpu.CompilerParams(dimension_semantics=sem, vmem_limit_bytes=vmem)


def _row(n):
    return pl.BlockSpec((1, n), lambda *_: (0, 0))


CS_ROWS = 8 * NDEV + 8


def _mod_part(mod_ref, row, part):
    pieces = []
    for j in range(NCHIP):
        lo, hi = max(part * DM, j * SHARD_ADA), min((part + 1) * DM, (j + 1) * SHARD_ADA)
        if lo < hi:
            pieces.append(mod_ref[j, row, lo - j * SHARD_ADA:hi - j * SHARD_ADA])
    return jnp.concatenate(pieces, axis=-1)


def inproj_fwd(chip, x, c_vec, c_ctx, w_ada, b_shard, norm_g, w_shard, wo_shard):
    tl = 1024
    nt = SEQ // tl
    halves = (DM // 2, SHARD_OUT // 2)
    n_w, n_c = 12, NDEV - 1

    def kern(k_ref, x_ref, cv_ref, cc_ref, wa_ref, b_ref, g_ref, w_ref, wo_ref,
             z_ref, h_ref, wfull_ref, wofull_ref, modall_ref, csall_ref,
             w_scr, wo_scr, h_scr, mine, cs_scr, mod_scr, shsc_scr, send_sems, recv_sems, out_sems):
        s, t = pl.program_id(0), pl.program_id(1)
        xi, yi, c = _me()
        k, me = 2 * xi + yi, 4 * xi + 2 * yi + c
        sib = _flip(1)
        rows = pl.ds(pl.multiple_of(t * tl, tl), tl)
        gathered = (w_scr, wo_scr)
        slot = lambda d: pl.ds(pl.multiple_of(8 * d, 8), 8)

        def c_copy(q, owner):
            return _rcopy(mine, cs_scr.at[slot(owner), :], send_sems, recv_sems, n_w + q - 1, _flip(q))

        def m_copy(q, chip_of_block):
            return _rcopy(mod_scr.at[chip_of_block], mod_scr.at[chip_of_block], send_sems, recv_sems,
                          n_w + n_c + q // 2 - 1, _flip(q))

        def adaln():
            first = lax.broadcasted_iota(jnp.int32, (8, DM), 0) == 0
            mine[...] = jnp.where(first, jnp.broadcast_to(cv_ref[...], (8, DM)), 0.0)
            cs_scr[slot(me), :] = mine[...]
            cs_scr[slot(NDEV), :] = jnp.where(first, jnp.broadcast_to(cc_ref[...], (8, DM)), 0.0)
            for q in range(1, NDEV):
                c_copy(q, me).start()
            wa = wa_ref[...].astype(BF16)
            for q in range(1, NDEV):
                px, py, pc = _flip(q)
                c_copy(q, 4 * px + 2 * py + pc).wait_recv()
            act = jax.nn.silu(cs_scr[...]).astype(BF16)
            mod_scr[k] = jnp.dot(act, wa, preferred_element_type=F32) + b_ref[...]
            for q in (2, 4, 6):
                m_copy(q, k).start()
            for q in (2, 4, 6):
                m_copy(q, _chip_of(_flip(q))).wait_recv()
            row = pl.ds(8 * me, 1)
            shsc_scr[0:1, :] = _mod_part(mod_scr, row, 0)
            shsc_scr[1:2, :] = _mod_part(mod_scr, row, 1)
            pltpu.sync_copy(mod_scr, modall_ref)
            pltpu.sync_copy(cs_scr, csall_ref)

        def block(n, chip_of_block, hh):
            return gathered[n].at[chip_of_block, pl.ds(pl.multiple_of(hh * halves[n], halves[n]), halves[n]), :]

        def ici(n, q, chip_of_block):
            blk = block(n, chip_of_block, c)
            return _rcopy(blk, blk, send_sems, recv_sems, 6 * n + q // 2 - 1, _flip(q))

        def d2d(n, q, chip_of_block, hh):
            blk = block(n, chip_of_block, hh)
            return _rcopy(blk, blk, send_sems, recv_sems, 6 * n + 3 + q // 2 - 1, sib)

        @pl.when((s == 0) & (t == 0))
        def _():
            adaln()
            w_scr[k] = w_ref[...].astype(BF16)
            wo_scr[k] = wo_ref[...].astype(BF16)
            for q in (2, 4, 6):
                ici(0, q, k).start()
                ici(1, q, k).start()

        for sweep in (1, 2, 3):
            @pl.when((s == sweep) & (t == 0))
            def _():
                q = 2 * sweep
                src = _chip_of(_flip(q))
                for n in (0, 1):
                    ici(n, q, src).wait_recv()
                    d2d(n, q, src, c).start()
                for n in (0, 1):
                    d2d(n, q, src, 1 - c).wait_recv()

        @pl.when(s == 0)
        def _():
            hb = _modulated(x_ref[...], g_ref[...], shsc_scr[1:2, :], shsc_scr[0:1, :]).astype(BF16)
            h_scr[rows, :] = hb
            h_ref[...] = hb

        z_ref[...] = jnp.dot(h_scr[rows, :], w_scr[lax.bitwise_xor(k, s)], preferred_element_type=F32)

        @pl.when((s == NCHIP - 1) & (t == nt - 1))
        def _():
            for q in range(1, NDEV):
                c_copy(q, me).wait_send()
            for q in (2, 4, 6):
                m_copy(q, k).wait_send()
            for n in (0, 1):
                for q in (2, 4, 6):
                    ici(n, q, k).wait_send()
                    d2d(n, q, _chip_of(_flip(q)), c).wait_send()
            outs = [pltpu.make_async_copy(w_scr.at[j], wfull_ref.at[:, j * SHARD_IN:(j + 1) * SHARD_IN], out_sems.at[j])
                    for j in range(NCHIP)] + [pltpu.make_async_copy(wo_scr, wofull_ref, out_sems.at[NCHIP])]
            for cp in outs:
                cp.start()
            for cp in outs:
                cp.wait()

    once = lambda s, t, k: (jnp.where(s == 0, t, nt - 1), 0)
    hbm = pl.BlockSpec(memory_space=pl.ANY)
    n_sem = n_w + n_c + 3
    return pl.pallas_call(
        kern, name="inproj_fwd",
        grid_spec=pltpu.PrefetchScalarGridSpec(
            num_scalar_prefetch=1, grid=(NCHIP, nt),
            in_specs=[pl.BlockSpec((tl, DM), once)] + [_VMEM_SPEC] * 7,
            out_specs=[pl.BlockSpec((tl, SHARD_IN), lambda s, t, k: (t, lax.bitwise_xor(k[0], s))),
                       pl.BlockSpec((tl, DM), once), hbm, hbm, hbm, hbm],
            scratch_shapes=[pltpu.VMEM((NCHIP, DM, SHARD_IN), BF16), pltpu.VMEM((NCHIP, SHARD_OUT, DM), BF16),
                            pltpu.VMEM((SEQ, DM), BF16), pltpu.VMEM((8, DM), F32), pltpu.VMEM((CS_ROWS, DM), F32),
                            pltpu.VMEM((NCHIP, CS_ROWS, SHARD_ADA), F32), pltpu.VMEM((8, DM), F32),
                            pltpu.SemaphoreType.DMA((n_sem,)), pltpu.SemaphoreType.DMA((n_sem,)),
                            pltpu.SemaphoreType.DMA((NCHIP + 1,))]),
        out_shape=[jax.ShapeDtypeStruct((SEQ, DIN), F32), jax.ShapeDtypeStruct((SEQ, DM), BF16),
                   jax.ShapeDtypeStruct((DM, DIN), BF16), jax.ShapeDtypeStruct((NCHIP, SHARD_OUT, DM), BF16),
                   jax.ShapeDtypeStruct((NCHIP, CS_ROWS, SHARD_ADA), F32), jax.ShapeDtypeStruct((CS_ROWS, DM), F32)],
        compiler_params=_cparams(("arbitrary", "arbitrary"), VMEM_BIG),
    )(chip, x, c_vec, c_ctx, w_ada, b_shard, norm_g, w_shard, wo_shard)


def ctx_fwd(ctx, cshift, cscale, norm_g, w_full):
    def kern(c_ref, sh_ref, sc_ref, g_ref, w_ref, zc_ref, hc_ref):
        hc = _modulated(c_ref[...], g_ref[...], sc_ref[...], sh_ref[...]).astype(BF16)
        hc_ref[...] = hc
        zc_ref[...] = jnp.dot(hc, w_ref[...], preferred_element_type=F32)

    return pl.pallas_call(
        kern, name="ctx_fwd", grid=(1,),
        in_specs=[pl.BlockSpec((CTX, DM), lambda i: (0, 0)), _row(DM), _row(DM), _row(DM),
                  pl.BlockSpec((DM, 2 * SHARD_IN), lambda i: (0, 1))],
        out_specs=[pl.BlockSpec((CTX, 2 * SHARD_IN), lambda i: (0, 0)),
                   pl.BlockSpec((CTX, DM), lambda i: (0, 0))],
        out_shape=[jax.ShapeDtypeStruct((CTX, 2 * SHARD_IN), F32), jax.ShapeDtypeStruct((CTX, DM), BF16)],
        compiler_params=_cparams(("arbitrary",)),
    )(ctx, cshift, cscale, norm_g, w_full)


SGU_CHUNK, SGU_PER_STEP = 128, 4


def _gelu(x):
    return 0.5 * x * (1.0 + lax.erf(x * 0.7071067811865476))


def _sgu_chunk(au, av, ag, sg, ws, bsb):
    u, v = _gelu(au), _gelu(av)
    outs = []
    for g in range(4):
        sl = slice(128 * g, 128 * (g + 1))
        mixed = mm(ws[g], _rms(v[:, sl], sg[:, sl])) + bsb[g]
        outs.append(u[:, sl] * mixed * jax.nn.silu(ag[:, sl]))
    return jnp.concatenate(outs, axis=-1)


def _sgu_specs():
    rows = SGU_CHUNK * SGU_PER_STEP
    zspec = lambda c: pl.BlockSpec((rows, 512), lambda n: (n, c))
    wspec = pl.BlockSpec((4, 128, 128), lambda n: (0, 0, 0))
    return rows, [zspec(0), zspec(1), zspec(2), _row(512), wspec, wspec]


_DR_OFF = (7, 3, -1)


def _row_valid(v, rr, j):
    return (j < 8, rr <= j < rr + 8, 4 <= j < 12)[v]


def _col_window():
    q = lax.broadcasted_iota(jnp.int32, (GRID_W, 128), 0)
    kc = lax.broadcasted_iota(jnp.int32, (GRID_W, 128), 1) % GRID_W
    c0 = jnp.clip(q - 8, 0, GRID_W - 16)
    return (kc >= c0) & (kc < c0 + 16)


def _bias_tiles(base, store):
    lo = lax.broadcasted_iota(jnp.int32, (1, 128), 1) < GRID_W
    win = _col_window()
    tiles = {}
    for v in range(3):
        for rr in range(QROWS):
            for jp in range(KROWS // 2):
                j0, j1 = 2 * jp, 2 * jp + 1
                ok0, ok1 = _row_valid(v, rr, j0), _row_valid(v, rr, j1)
                key = (j0 - rr + _DR_OFF[v], ok0, ok1) if (ok0 or ok1) else None
                if key not in tiles:
                    if key is None:
                        tiles[key] = jnp.full((GRID_W, 128), NEG_INF, F32)
                    else:
                        d0 = key[0]
                        r0 = base[d0:d0 + 1, :] if ok0 else jnp.zeros((1, 128), F32)
                        r1 = base[d0 + 1:d0 + 2, :] if ok1 else jnp.zeros((1, 128), F32)
                        y = jnp.broadcast_to(jnp.where(lo, r0, r1), (GRID_W, 128))
                        y = pltpu.roll(pltpu.roll(y, 128 - 15, 1), 0, 1, stride=1, stride_axis=0)
                        tiles[key] = jnp.where(win & jnp.where(lo, ok0, ok1), y, NEG_INF)
                store(v, slice(rr * GRID_W, (rr + 1) * GRID_W), slice(jp * 128, (jp + 1) * 128), tiles[key])


def _rpb_grad(load):
    lo = lax.broadcasted_iota(jnp.int32, (1, 128), 1) < GRID_W
    ri = lax.broadcasted_iota(jnp.int32, (GRID_W, GRID_W), 0)
    ci = lax.broadcasted_iota(jnp.int32, (GRID_W, GRID_W), 1)
    flip = (ri + ci == GRID_W - 1).astype(F32)
    groups = {}
    for v in range(3):
        for rr in range(QROWS):
            for jp in range(KROWS // 2):
                j0, j1 = 2 * jp, 2 * jp + 1
                ok0, ok1 = _row_valid(v, rr, j0), _row_valid(v, rr, j1)
                if not (ok0 or ok1):
                    continue
                g = load(v, slice(rr * GRID_W, (rr + 1) * GRID_W), slice(jp * 128, (jp + 1) * 128))
                key = (j0 - rr + _DR_OFF[v], ok0, ok1)
                groups[key] = g if key not in groups else groups[key] + g
    acc = [jnp.zeros((1, 128), F32) for _ in range(15)]
    for (d0, ok0, ok1), g in groups.items():
        g = lax.dot_general(flip, g, (((1,), (0,)), ((), ())), precision=lax.Precision.HIGHEST,
                            preferred_element_type=F32)
        g = pltpu.roll(pltpu.roll(g, 128 - 48, 1), 0, 1, stride=1, stride_axis=0)
        s = jnp.sum(g, axis=0, keepdims=True)
        if ok0:
            acc[d0] = acc[d0] + jnp.where(lo, s, 0.0)
        if ok1:
            acc[d0 + 1] = acc[d0 + 1] + jnp.where(lo, 0.0, s)
    return [row + pltpu.roll(row, GRID_W, 1) for row in acc]


def _scaled_q(q_raw, qg):
    return _pair_rms(q_raw, qg) * (HDIM ** -0.5)


def _head_lanes():
    lo = lax.broadcasted_iota(jnp.int32, (1, 2 * HDIM), 1) < HDIM
    return lo, jnp.logical_not(lo)


SOFTMAX_ROWS = 32


def _emit_interleaved(vector_work, matmul_work):
    for j in range(max(len(vector_work), len(matmul_work))):
        for work in (vector_work, matmul_work):
            if j < len(work):
                work[j]()


def _kblock(i):
    return jnp.clip(i - 1, 0, (SEQ - KBLK) // QBLK)


def _kstart(i):
    return pl.multiple_of(_kblock(i) * QBLK, QBLK)


ATTN_BLOCKS = 4
TILE_BUFFERS = 4
ATTN_STEPS = NQBLK // ATTN_BLOCKS
ATTN_ROWS = ATTN_BLOCKS * QBLK


def _bias_variant(i, b):
    if b == 0:
        return jnp.where(i == 0, 0, 1)
    if b == ATTN_BLOCKS - 1:
        return jnp.where(i == ATTN_STEPS - 1, 2, 1)
    return 1
KCOLS = QBLK


def _attn_in_specs():
    return [
        pl.BlockSpec((ATTN_ROWS, 128), lambda p, i: (i, ZQ + p)),
        pl.BlockSpec((SEQ, 128), lambda p, i: (0, ZK + p)),
        pl.BlockSpec((SEQ, 128), lambda p, i: (0, ZV + p)),
        pl.BlockSpec((ATTN_ROWS, 128), lambda p, i: (i, ZG + p)),
        pl.BlockSpec((CTX, 128), lambda p, i: (0, 2 + p)),
        pl.BlockSpec((CTX, 128), lambda p, i: (0, 6 + p)),
    ]


def _rpb_spec():
    return pl.BlockSpec((2, 15, 128), lambda p, i: (p, 0, 0))


def _prob_specs():
    return [pl.BlockSpec((2, ATTN_ROWS, KBLK), lambda p, i: (p, i, 0)),
            pl.BlockSpec((2, ATTN_ROWS, CTX), lambda p, i: (p, i, 0))]


NORM_ROWS = 2048


def _half_sums(x):
    lo = lax.broadcasted_iota(jnp.int32, (1, 2 * HDIM), 1) < HDIM
    return jnp.where(lo, jnp.sum(jnp.where(lo, x, 0.0), axis=-1, keepdims=True),
                     jnp.sum(jnp.where(lo, 0.0, x), axis=-1, keepdims=True))


def _pair_rms_bwd(x, g2, ct):
    rs = lax.rsqrt(_half_sums(x * x) / HDIM + EPS)
    y = x * rs
    dy = ct * g2
    return rs * (dy - y * (_half_sums(dy * y) / HDIM)), jnp.sum(ct * y, axis=0, keepdims=True)


def _norm_keys(k_ref, ck_ref, kg_ref, kn_scr, ckn_scr):
    def body(c, carry):
        sl = pl.ds(pl.multiple_of(c * NORM_ROWS, NORM_ROWS), NORM_ROWS)
        kn_scr[sl, :] = _pair_rms(k_ref[sl, :], kg_ref[...]).astype(BF16)
        return carry

    lax.fori_loop(0, SEQ // NORM_ROWS, body, 0)
    ckn_scr[...] = _pair_rms(ck_ref[...], kg_ref[...]).astype(BF16)


def _values_with_ones(v_ref, cv_ref, v1_scr, cv1_scr):
    for a, mine in enumerate(_head_lanes()):
        def body(c, carry):
            sl = pl.ds(pl.multiple_of(c * NORM_ROWS, NORM_ROWS), NORM_ROWS)
            v1_scr[a, sl, :] = jnp.where(mine, v_ref[sl, :], 1.0).astype(BF16)
            return carry

        lax.fori_loop(0, SEQ // NORM_ROWS, body, 0)
        cv1_scr[a] = jnp.where(mine, cv_ref[...], 1.0).astype(BF16)


def _pair_major_spec():
    return pl.BlockSpec((1, ATTN_ROWS, 128), lambda p, i: (p, i, 0))


def _normed_key_specs():
    return [pl.BlockSpec((None, SEQ, 128), lambda p, i: (p, 0, 0)), pl.BlockSpec((None, CTX, 128), lambda p, i: (p, 0, 0))]


def attn_fwd(z, zc, rpb2, qg2, kg2):
    def kern(q_ref, k_ref, v_ref, bg_ref, ck_ref, cv_ref, rpb_ref, qg_ref, kg_ref,
             ob_ref, o_ref, rden_ref, pl_ref, pc_ref, kn_ref, ckn_ref, kn_scr, ckn_scr, v1_scr, cv1_scr, s_scr,
             bias_ref):
        i = pl.program_id(1)

        @pl.when(i == 0)
        def _():
            for a in range(2):
                def store(v, tile_rows, tile_cols, tile, a=a):
                    bias_ref[v, a, tile_rows, tile_cols] = tile

                _bias_tiles(rpb_ref[a], store)
            _norm_keys(k_ref, ck_ref, kg_ref, kn_scr, ckn_scr)
            kn_ref[...] = kn_scr[...]
            ckn_ref[...] = ckn_scr[...]
            _values_with_ones(v_ref, cv_ref, v1_scr, cv1_scr)

        heads = _head_lanes()
        tiles = [(b, a) for b in range(ATTN_BLOCKS) for a in range(2)]
        rows = [slice(b * QBLK, (b + 1) * QBLK) for b in range(ATTN_BLOCKS)]
        variant = [_bias_variant(i, b) for b in range(ATTN_BLOCKS)]
        pv = [None] * len(tiles)
        qa, done = {}, {}
        latent = KBLK // KCOLS
        buf = lambda t: t % TILE_BUFFERS

        def keys(b, n):
            return pl.ds(pl.multiple_of(_kstart(ATTN_BLOCKS * i + b) + n * KCOLS, KCOLS), KCOLS)

        def score_piece(t, n):
            b, a = tiles[t]
            cols = slice(n * KCOLS, (n + 1) * KCOLS)
            if n == 0:
                if a == 0:
                    done["qn", b] = _scaled_q(q_ref[rows[b], :], qg_ref[...])
                qa[t] = jnp.where(heads[a], done["qn", b], 0.0).astype(BF16)
            if n < latent:
                s_scr[buf(t), :, cols] = mm_nt(qa[t], kn_scr[keys(b, n), :]) + bias_ref[variant[b], a, :, cols]
            else:
                s_scr[buf(t), :, cols] = mm_nt(qa[t], ckn_scr[...])

        def softmax_rows(t, r):
            b, a = tiles[t]
            rs = slice(r * SOFTMAX_ROWS, (r + 1) * SOFTMAX_ROWS)
            out_rows = slice(b * QBLK + rs.start, b * QBLK + rs.stop)
            s = s_scr[buf(t), rs, :]
            p = jnp.exp(s - jnp.max(s, axis=-1, keepdims=True)).astype(BF16)
            pl_ref[a, out_rows, :] = p[:, :KBLK]
            pc_ref[a, out_rows, :] = p[:, KBLK:]

        def value_piece(t, n):
            b, a = tiles[t]
            if n < latent:
                part = mm(pl_ref[a, rows[b], n * KCOLS:(n + 1) * KCOLS], v1_scr[a, keys(b, n), :])
            else:
                part = mm(pc_ref[a, rows[b], :], cv1_scr[a])
            pv[t] = part if pv[t] is None else pv[t] + part
            if n == latent:
                finish(t)

        def finish(t):
            b, a = tiles[t]
            r = jnp.where(heads[a], pltpu.roll(1.0 / pv[t], HDIM, 1), 0.0)
            done[t] = (pv[t] * r, r)
            if a == 1:
                o, rden = (lo + hi for lo, hi in zip(done[t - 1], done[t]))
                ob_ref[rows[b], :] = o * jax.nn.silu(bg_ref[rows[b], :])
                o_ref[0, rows[b], :] = o
                rden_ref[0, rows[b], :] = rden

        pieces = range(latent + 1)
        for n in pieces:
            score_piece(0, n)
        for t in range(len(tiles)):
            matmuls = []
            for n in pieces:
                if t + 1 < len(tiles):
                    matmuls.append(functools.partial(score_piece, t + 1, n))
                if t > 0:
                    matmuls.append(functools.partial(value_piece, t - 1, n))
            _emit_interleaved([functools.partial(softmax_rows, t, r) for r in range(QBLK // SOFTMAX_ROWS)], matmuls)
        for n in pieces:
            value_piece(len(tiles) - 1, n)

    qblk = pl.BlockSpec((ATTN_ROWS, 128), lambda p, i: (i, p))
    return pl.pallas_call(
        kern, name="attn_fwd", grid=(NPAIR, ATTN_STEPS),
        in_specs=_attn_in_specs() + [_rpb_spec(), _row(128), _row(128)],
        out_specs=[qblk, _pair_major_spec(), _pair_major_spec()] + _prob_specs() + _normed_key_specs(),
        out_shape=[jax.ShapeDtypeStruct((SEQ, 512), F32)] + [jax.ShapeDtypeStruct((NPAIR, SEQ, 128), F32)] * 2
        + [jax.ShapeDtypeStruct((HEADS, SEQ, KBLK), BF16), jax.ShapeDtypeStruct((HEADS, SEQ, CTX), BF16),
           jax.ShapeDtypeStruct((NPAIR, SEQ, 128), BF16), jax.ShapeDtypeStruct((NPAIR, CTX, 128), BF16)],
        scratch_shapes=[pltpu.VMEM((SEQ, 128), BF16), pltpu.VMEM((CTX, 128), BF16),
                        pltpu.VMEM((2, SEQ, 128), BF16), pltpu.VMEM((2, CTX, 128), BF16),
                        pltpu.VMEM((TILE_BUFFERS, QBLK, KBLK + CTX), F32),
                        pltpu.VMEM((3, 2, QBLK, KBLK), F32)],
        compiler_params=_cparams(("arbitrary", "arbitrary"), VMEM_BIG),
    )(z, z, z, z, zc, zc, rpb2, qg2, kg2)


def attn_bwd(z, zc, qg2, kg2, dcat, saved):
    def kern(q_ref, k_ref, v_ref, bg_ref, ck_ref, cv_ref, qg_ref, kg_ref, do_ref, o_ref, rden_ref, pl_ref, pc_ref,
             kn_scr, ckn_scr, dq_ref, dk_ref, dv_ref, dbg_ref, dck_ref, dcv_ref, drpb_ref, dqg_ref, dkg_ref,
             v_scr, cv_scr, dknt_scr, dvt_scr, dcknt_scr, dcvt_scr, dp_scr, ds_scr, db_ref):
        p, i = pl.program_id(0), pl.program_id(1)
        last = i == ATTN_STEPS - 1

        @pl.when(i == 0)
        def _():
            def body(c, carry):
                sl = pl.ds(pl.multiple_of(c * NORM_ROWS, NORM_ROWS), NORM_ROWS)
                v_scr[sl, :] = v_ref[sl, :].astype(BF16)
                return carry

            lax.fori_loop(0, SEQ // NORM_ROWS, body, 0)
            cv_scr[...] = cv_ref[...].astype(BF16)
            for acc in (dknt_scr, dvt_scr, dcknt_scr, dcvt_scr, db_ref):
                acc[...] = jnp.zeros_like(acc)

        @pl.when((i == 0) & (p == 0))
        def _():
            dqg_ref[...] = jnp.zeros_like(dqg_ref)
            dkg_ref[...] = jnp.zeros_like(dkg_ref)

        heads = _head_lanes()
        tiles = [(b, a) for b in range(ATTN_BLOCKS) for a in range(2)]
        rows = [slice(b * QBLK, (b + 1) * QBLK) for b in range(ATTN_BLOCKS)]
        kb = [_kblock(ATTN_BLOCKS * i + b) for b in range(ATTN_BLOCKS)]
        variant = [_bias_variant(i, b) for b in range(ATTN_BLOCKS)]
        latent = KBLK // KCOLS
        buf = lambda t: t % TILE_BUFFERS

        def keys(b, n):
            return pl.ds(pl.multiple_of((kb[b] + n) * KCOLS, KCOLS), KCOLS)

        gated = {}

        def gate_backward(b):
            bg, dout, o = bg_ref[rows[b], :], do_ref[rows[b], :], o_ref[0, rows[b], :]
            sig = jax.nn.sigmoid(bg)
            do = dout * (bg * sig)
            dbg_ref[rows[b], :] = (dout * o * (sig * (1.0 + bg * (1.0 - sig)))).astype(BF16)
            rden = rden_ref[0, rows[b], :]
            dr = do * rden
            qn = _scaled_q(q_ref[rows[b], :], qg_ref[...])
            gated[b] = (dr, dr.T.astype(BF16), qn.T.astype(BF16), do * o * rden)

        feats = [slice(a * HDIM, (a + 1) * HDIM) for a in range(2)]
        doa, doa_t, qa_t, delta = {}, {}, {}, {}
        dqn = [None] * len(tiles)

        def cols(n):
            return slice(n * KCOLS, (n + 1) * KCOLS)

        def stage_a(t, n):
            b, a = tiles[t]
            if n == 0:
                if a == 0:
                    gate_backward(b)
                dr, dr_t, qn_t, weighted = gated[b]
                doa[t] = jnp.where(heads[a], dr, 0.0).astype(BF16)
                doa_t[t] = dr_t[feats[a], :]
                qa_t[t] = qn_t[feats[a], :]
                delta[t] = jnp.sum(jnp.where(heads[a], weighted, 0.0), axis=-1, keepdims=True)
            if n < latent:
                dp_scr[buf(t), :, cols(n)] = mm_nt(doa[t], v_scr[keys(b, n), :])
                dvt_scr[kb[b] + n, feats[a], :] += mm(doa_t[t], pl_ref[a, rows[b], cols(n)])
            else:
                dp_scr[buf(t), :, cols(n)] = mm_nt(doa[t], cv_scr[...])
                dcvt_scr[feats[a], :] += mm(doa_t[t], pc_ref[a, rows[b], :])

        def stage_b(t, r):
            b, a = tiles[t]
            rs = slice(r * SOFTMAX_ROWS, (r + 1) * SOFTMAX_ROWS)
            in_rows = slice(b * QBLK + rs.start, b * QBLK + rs.stop)
            d = dp_scr[buf(t), rs, :] - delta[t][rs, :]
            ds_lat = pl_ref[a, in_rows, :].astype(F32) * d[:, :KBLK]
            ds_ctx = pc_ref[a, in_rows, :].astype(F32) * d[:, KBLK:]
            db_ref[variant[b], a, rs, :] += ds_lat
            ds_scr[buf(t), rs, :KBLK] = ds_lat.astype(BF16)
            ds_scr[buf(t), rs, KBLK:] = ds_ctx.astype(BF16)

        def stage_c(t, n):
            b, a = tiles[t]
            ds = ds_scr[buf(t), :, cols(n)]
            if n < latent:
                part = mm(ds, kn_scr[keys(b, n), :])
                dknt_scr[kb[b] + n, feats[a], :] += mm(qa_t[t], ds)
            else:
                part = mm(ds, ckn_scr[...])
                dcknt_scr[feats[a], :] += mm(qa_t[t], ds)
            dqn[t] = part if dqn[t] is None else dqn[t] + part
            if n == latent and a == 1:
                both = jnp.where(heads[0], dqn[t - 1], 0.0) + jnp.where(heads[1], dqn[t], 0.0)
                dq, dqg = jax.vjp(_scaled_q, q_ref[rows[b], :], qg_ref[...])[1](both)
                dq_ref[rows[b], :] = dq.astype(BF16)
                dqg_ref[...] += dqg

        pieces = range(latent + 1)
        for n in pieces:
            stage_a(0, n)
        for t in range(len(tiles)):
            matmuls = []
            for n in pieces:
                if t + 1 < len(tiles):
                    matmuls.append(functools.partial(stage_a, t + 1, n))
                if t > 0:
                    matmuls.append(functools.partial(stage_c, t - 1, n))
            _emit_interleaved([functools.partial(stage_b, t, r) for r in range(QBLK // SOFTMAX_ROWS)], matmuls)
        for n in pieces:
            stage_c(len(tiles) - 1, n)

        @pl.when(last)
        def _():
            eye = (lax.broadcasted_iota(jnp.int32, (KCOLS, KCOLS), 0)
                   == lax.broadcasted_iota(jnp.int32, (KCOLS, KCOLS), 1)).astype(BF16)

            def turned(x):
                hi = x.astype(BF16)
                return mm_nt(eye, hi) + mm_nt(eye, x - hi.astype(F32))

            def body(c, dkg):
                sl = pl.ds(pl.multiple_of(c * NORM_ROWS, NORM_ROWS), NORM_ROWS)
                blocks = range(NORM_ROWS // KCOLS)
                dkn = jnp.concatenate([turned(dknt_scr[c * len(blocks) + n]) for n in blocks], axis=0)
                dv = jnp.concatenate([mm_nt(eye, dvt_scr[c * len(blocks) + n]) for n in blocks], axis=0)
                dk, dg = _pair_rms_bwd(k_ref[sl, :], kg_ref[...], dkn)
                dk_ref[sl, :] = dk.astype(BF16)
                dv_ref[sl, :] = dv.astype(BF16)
                return dkg + dg

            dkg = lax.fori_loop(0, SEQ // NORM_ROWS, body, jnp.zeros((1, 128), F32))
            dck, dg = _pair_rms_bwd(ck_ref[...], kg_ref[...], dcknt_scr[...].T)
            dck_ref[...] = dck
            dcv_ref[...] = dcvt_scr[...].T
            dkg_ref[...] += dkg + dg
            for a in range(2):
                rows_of_rpb = _rpb_grad(lambda v, tile_rows, tile_cols, a=a: db_ref[v, a, tile_rows, tile_cols])
                for d, row in enumerate(rows_of_rpb):
                    drpb_ref[a, d:d + 1, :] = row

        @pl.when(last & (p == NPAIR - 1))
        def _():
            dqg_ref[...] = dqg_ref[...] + pltpu.roll(dqg_ref[...], HDIM, 1)
            dkg_ref[...] = dkg_ref[...] + pltpu.roll(dkg_ref[...], HDIM, 1)

    blk = lambda rows: pl.BlockSpec((rows, 128), lambda p, i: (0, p))
    qblk = pl.BlockSpec((ATTN_ROWS, 128), lambda p, i: (i, p))
    return pl.pallas_call(
        kern, name="attn_bwd", grid=(NPAIR, ATTN_STEPS),
        in_specs=_attn_in_specs() + [_row(128), _row(128), pl.BlockSpec((ATTN_ROWS, 128), lambda p, i: (i, 4 + p)),
                                     _pair_major_spec(), _pair_major_spec()] + _prob_specs() + _normed_key_specs(),
        out_specs=[qblk, blk(SEQ), blk(SEQ), qblk, blk(CTX), blk(CTX), _rpb_spec(), _row(128), _row(128)],
        out_shape=[jax.ShapeDtypeStruct((SEQ, 512), BF16)] * 4 + [jax.ShapeDtypeStruct((CTX, 512), F32)] * 2
        + [jax.ShapeDtypeStruct((HEADS, 15, 128), F32)]
        + [jax.ShapeDtypeStruct((1, 128), F32), jax.ShapeDtypeStruct((1, 128), F32)],
        scratch_shapes=[pltpu.VMEM((SEQ, 128), BF16), pltpu.VMEM((CTX, 128), BF16),
                        pltpu.VMEM((SEQ // KCOLS, 128, KCOLS), F32), pltpu.VMEM((SEQ // KCOLS, 128, KCOLS), F32),
                        pltpu.VMEM((128, CTX), F32), pltpu.VMEM((128, CTX), F32),
                        pltpu.VMEM((TILE_BUFFERS, QBLK, KBLK + CTX), F32),
                        pltpu.VMEM((TILE_BUFFERS, QBLK, KBLK + CTX), BF16),
                        pltpu.VMEM((3, 2, QBLK, KBLK), F32)],
        compiler_params=_cparams(("arbitrary", "arbitrary"), VMEM_BIG),
    )(z, z, z, z, zc, zc, qg2, kg2, dcat, *saved)


def outproj(z, sg, ws, bsb, out_b, x, target, gate, wo):
    tl = SGU_CHUNK * SGU_PER_STEP
    nt = SEQ // tl

    def kern(au0_ref, av0_ref, ag0_ref, au1_ref, av1_ref, ag1_ref, sg_ref, ws_ref, bs_ref, b_ref, x_ref, t_ref, g_ref,
             w_ref, loss_ref, dy_ref, dcat_ref, dg_ref, dw_ref, a_scr):
        t = pl.program_id(0)
        cur, nxt = lax.rem(t, 2), lax.rem(t + 1, 2)

        def gating(refs, slot, cn):
            sl = slice(cn * SGU_CHUNK, (cn + 1) * SGU_CHUNK)
            au_ref, av_ref, ag_ref = refs
            a_scr[slot, sl, :] = _sgu_chunk(au_ref[sl, :], av_ref[sl, :], ag_ref[sl, :], sg_ref[...], ws_ref[...],
                                            bs_ref[...]).astype(BF16)

        @pl.when(t == 0)
        def _():
            loss_ref[...] = jnp.zeros_like(loss_ref)
            dg_ref[...] = jnp.zeros_like(dg_ref)
            dw_ref[...] = jnp.zeros_like(dw_ref)
            for cn in range(SGU_PER_STEP):
                gating((au0_ref, av0_ref, ag0_ref), 0, cn)

        a, b = a_scr[cur], b_ref[...].astype(BF16)
        mix = (jnp.dot(a, w_ref[0:512, :], preferred_element_type=F32)
               + jnp.dot(b, w_ref[512:1024, :], preferred_element_type=F32))
        err = x_ref[...] + g_ref[...] * mix - t_ref[...]
        loss_ref[...] += 0.5 * jnp.sum(jnp.mean(err * err, axis=-1))
        dy = err * (1.0 / DM)
        dy_ref[...] = dy
        dg_ref[...] += jnp.sum(dy * mix, axis=0, keepdims=True)
        dmix = (g_ref[...] * dy).astype(BF16)

        def dcat_half(n):
            part = slice(512 * n, 512 * (n + 1))
            dcat_ref[:, part] = lax.dot_general(dmix, w_ref[part, :], _NT, preferred_element_type=F32)

        def dw_half(n, src):
            dw_ref[512 * n:512 * (n + 1), :] += lax.dot_general(src, dmix, (((0,), (0,)), ((), ())),
                                                                preferred_element_type=F32)

        _emit_interleaved([functools.partial(gating, (au1_ref, av1_ref, ag1_ref), nxt, cn) for cn in range(SGU_PER_STEP)],
                          [functools.partial(dcat_half, 0), functools.partial(dcat_half, 1),
                           functools.partial(dw_half, 0, a), functools.partial(dw_half, 1, b)])

    tile = lambda w: pl.BlockSpec((tl, w), lambda t: (t, 0))
    whole = pl.BlockSpec((DM, DM), lambda t: (0, 0))
    zfirst = [pl.BlockSpec((tl, 512), functools.partial(lambda c, t: (0, c), c)) for c in range(3)]
    znext = [pl.BlockSpec((tl, 512), functools.partial(lambda c, t: (jnp.minimum(t + 1, nt - 1), c), c))
             for c in range(3)]
    wspec = pl.BlockSpec((4, 128, 128), lambda t: (0, 0, 0))
    return pl.pallas_call(
        kern, name="outproj", grid=(nt,),
        in_specs=zfirst + znext + [_row(512), wspec, wspec, tile(512), tile(DM), tile(DM), _row(DM), whole],
        out_specs=[pl.BlockSpec((8, 128), lambda t: (0, 0)), tile(DM), tile(DM), _row(DM), whole],
        out_shape=[jax.ShapeDtypeStruct((8, 128), F32), jax.ShapeDtypeStruct((SEQ, DM), F32),
                   jax.ShapeDtypeStruct((SEQ, DM), F32), jax.ShapeDtypeStruct((1, DM), F32),
                   jax.ShapeDtypeStruct((DM, DM), F32)],
        scratch_shapes=[pltpu.VMEM((2, tl, 512), BF16)],
        compiler_params=_cparams(("arbitrary",), 48 * 1024 * 1024),
    )(z, z, z, z, z, z, sg, ws, bsb, out_b, x, target, gate, wo)


DZ_COLS = (("a", 0, 1536), ("q", 1536, 2048), ("k", 2048, 2560), ("v", 2560, 3072), ("g", 3072, DIN))
DZC_COLS = (("k", 2048, 2560), ("v", 2560, 3072))
_NT = (((1,), (1,)), ((), ()))


DH_SUBTILES = 2


def _dz_specs(tl):
    return [pl.BlockSpec((tl, 1536), lambda t: (t, 0))] + [pl.BlockSpec((tl, 512), lambda t: (t, 0))] * 4


def dh_bwd(dz_parts, w_full, x, dy, shift, scale, norm_g, dg_ctx):
    tl = 512
    nt = SEQ // tl

    depth = 3

    def kern(a_ref, q_ref, k_ref, v_ref, g_ref, w_ref, x_hbm, dy_hbm, sh_ref, sc_ref, gn_ref, dgc_ref,
             gx_ref, dsh_ref, dsc_ref, dg_ref, x_buf, dy_buf, sems):
        t = pl.program_id(0)

        def fetches(step):
            slot = lax.rem(step, depth)
            tile_rows = pl.ds(pl.multiple_of(step * tl, tl), tl)
            return [pltpu.make_async_copy(hbm.at[tile_rows, :], buf.at[slot], sems.at[n, slot])
                    for n, (hbm, buf) in enumerate(((x_hbm, x_buf), (dy_hbm, dy_buf)))]

        @pl.when(t == 0)
        def _():
            dsh_ref[...] = jnp.zeros_like(dsh_ref)
            dsc_ref[...] = jnp.zeros_like(dsc_ref)
            dg_ref[...] = dgc_ref[...]
            for step in range(depth - 1):
                for cp in fetches(step):
                    cp.start()

        @pl.when(t + depth - 1 < nt)
        def _():
            for cp in fetches(t + depth - 1):
                cp.start()

        for cp in fetches(t):
            cp.wait()
        x_ref, dy_ref = x_buf.at[lax.rem(t, depth)], dy_buf.at[lax.rem(t, depth)]

        src = dict(a=a_ref, q=q_ref, k=k_ref, v=v_ref, g=g_ref)
        for sub in range(DH_SUBTILES):
            rows = slice(sub * tl // DH_SUBTILES, (sub + 1) * tl // DH_SUBTILES)
            dh = None
            for name, c0, c1 in DZ_COLS:
                part = lax.dot_general(src[name][rows, :], w_ref[:, c0:c1], _NT, preferred_element_type=F32)
                dh = part if dh is None else dh + part
            _, vjp = jax.vjp(_modulated, x_ref[rows, :], gn_ref[...], sc_ref[...], sh_ref[...])
            dx, dg, dsc, dsh = vjp(dh)
            gx_ref[rows, :] = dy_ref[rows, :] + dx
            dg_ref[...] += dg
            dsc_ref[...] += dsc
            dsh_ref[...] += dsh

    tile = pl.BlockSpec((tl, DM), lambda t: (t, 0))
    hbm = pl.BlockSpec(memory_space=pl.ANY)
    return pl.pallas_call(
        kern, name="dh_bwd", grid=(nt,),
        in_specs=_dz_specs(tl) + [pl.BlockSpec((DM, DIN), lambda t: (0, 0), pipeline_mode=pl.Buffered(1)), hbm, hbm,
                                  _row(DM), _row(DM), _row(DM), _row(DM)],
        out_specs=[tile, _row(DM), _row(DM), _row(DM)],
        out_shape=[jax.ShapeDtypeStruct((SEQ, DM), F32)] + [jax.ShapeDtypeStruct((1, DM), F32)] * 3,
        scratch_shapes=[pltpu.VMEM((depth, tl, DM), F32), pltpu.VMEM((depth, tl, DM), F32),
                        pltpu.SemaphoreType.DMA((2, depth))],
        compiler_params=_cparams(("arbitrary",), 48 * 1024 * 1024),
    )(*dz_parts, w_full, x, dy, shift, scale, norm_g, dg_ctx)


def dw_bwd(h, z, sg, ws, bsb, dcat, dz_attn, hc, dck, dcv, g_out):
    tl = SGU_CHUNK * SGU_PER_STEP
    nt = SEQ // tl
    (rhi, wi), (rho, wo) = RS_SHAPES

    def kern(h_ref, au_ref, av_ref, ag_ref, sg_ref, ws_ref, bs_ref, do_ref, q_ref, k_ref, v_ref, g_ref,
             hc_ref, dck_ref, dcv_ref, go_hbm,
             wire_i, keep_i, wire_o, keep_o, a_ref, dsg_ref, dws_ref, dbs_ref,
             acc, rcv_i, mine_o, rcv_o, load_sem, send_sems, recv_sems):
        t = pl.program_id(0)
        x, y, c = _me()
        k = 2 * x + y
        sib = _flip(1)
        half = lambda hh, rh: pl.ds(pl.multiple_of(hh * rh, rh), rh)
        load_o = pltpu.make_async_copy(go_hbm.at[:, half(c, rho), :], mine_o, load_sem)
        pair_o = _rcopy(go_hbm.at[:, half(1 - c, rho), :], rcv_o, send_sems, recv_sems, 0, sib)
        pair_i = [_rcopy(wire_i.at[j], rcv_i.at[j], send_sems, recv_sems, 1 + j, sib) for j in range(NCHIP)]

        @pl.when(t == 0)
        def _():
            load_o.start()
            pair_o.start()
            acc[...] = jnp.zeros_like(acc)
            dsg_ref[...] = jnp.zeros_like(dsg_ref)
            dws_ref[...] = jnp.zeros_like(dws_ref)
            dbs_ref[...] = jnp.zeros_like(dbs_ref)
            hct = hc_ref[...].T
            csrc = dict(k=dck_ref, v=dcv_ref)
            for name, c0, c1 in DZC_COLS:
                acc[:, c0:c1] += jnp.dot(hct, csrc[name][...].astype(BF16), preferred_element_type=F32)

        ht = h_ref[...].T
        src = dict(a=a_ref, q=q_ref, k=k_ref, v=v_ref, g=g_ref)

        def gating_backward(cn):
            sl = slice(cn * SGU_CHUNK, (cn + 1) * SGU_CHUNK)
            _, vjp = jax.vjp(_sgu_chunk, au_ref[sl, :], av_ref[sl, :], ag_ref[sl, :], sg_ref[...], ws_ref[...],
                             bs_ref[...])
            dau, dav, dag, dsg, dws, dbs = vjp(do_ref[sl, :])
            a_ref[sl, 0:512] = dau.astype(BF16)
            a_ref[sl, 512:1024] = dav.astype(BF16)
            a_ref[sl, 1024:1536] = dag.astype(BF16)
            dsg_ref[...] += dsg
            dws_ref[...] += dws
            dbs_ref[...] += dbs

        def product(name, c0, c1):
            acc[:, c0:c1] += jnp.dot(ht, src[name][...], preferred_element_type=F32)

        _emit_interleaved([functools.partial(product, *cols) for cols in DZ_COLS[1:]],
                          [functools.partial(gating_backward, cn) for cn in range(SGU_PER_STEP)])
        product(*DZ_COLS[0])

        @pl.when(t == nt - 1)
        def _():
            dbs_ref[...] = jnp.broadcast_to(jnp.sum(dbs_ref[...], axis=-1, keepdims=True), dbs_ref.shape)
            shard = lambda j: slice(j * SHARD_IN, (j + 1) * SHARD_IN)
            for j in range(NCHIP):
                wire_i[j] = acc[half(1 - c, rhi), shard(j)].astype(BF16)
                pair_i[j].start()
            load_o.wait()
            pair_o.wait_recv()
            for j in range(NCHIP):
                wire_o[j] = (mine_o[j] + rcv_o[j]).astype(BF16)
            keep_o[...] = mine_o[k] + rcv_o[k]
            mine = half(c, rhi)
            for j in range(NCHIP):
                pair_i[j].wait_recv()
                pair_i[j].wait_send()
                pair_sum = acc[mine, shard(j)] + rcv_i[j].astype(F32)
                wire_i[j] = pair_sum.astype(BF16)

                @pl.when(k == j)
                def _():
                    keep_i[...] = pair_sum
            pair_o.wait_send()

    whole = lambda *shape: pl.BlockSpec(shape, lambda t: (0,) * len(shape))
    rows, sgu_specs = _sgu_specs()
    assert rows == tl
    a_spec, *attn_specs = _dz_specs(tl)
    return pl.pallas_call(
        kern, name="dw_bwd", grid=(nt,),
        in_specs=[pl.BlockSpec((tl, DM), lambda t: (t, 0))] + sgu_specs + [pl.BlockSpec((tl, 512), lambda t: (t, 0))]
        + attn_specs + [whole(CTX, DM), whole(CTX, 512), whole(CTX, 512), pl.BlockSpec(memory_space=pl.ANY)],
        out_specs=[whole(NCHIP, rhi, wi), whole(rhi, wi), whole(NCHIP, rho, wo), whole(rho, wo),
                   a_spec, _row(512), whole(4, 128, 128), whole(4, 128, 128)],
        out_shape=[jax.ShapeDtypeStruct((NCHIP, rhi, wi), BF16), jax.ShapeDtypeStruct((rhi, wi), F32),
                   jax.ShapeDtypeStruct((NCHIP, rho, wo), BF16), jax.ShapeDtypeStruct((rho, wo), F32),
                   jax.ShapeDtypeStruct((SEQ, 1536), BF16), jax.ShapeDtypeStruct((1, 512), F32),
                   jax.ShapeDtypeStruct((4, 128, 128), F32), jax.ShapeDtypeStruct((4, 128, 128), F32)],
        scratch_shapes=[pltpu.VMEM((DM, DIN), F32), pltpu.VMEM((NCHIP, rhi, wi), BF16),
                        pltpu.VMEM((NCHIP, rho, wo), F32), pltpu.VMEM((NCHIP, rho, wo), F32),
                        pltpu.SemaphoreType.DMA(()), pltpu.SemaphoreType.DMA((1 + NCHIP,)),
                        pltpu.SemaphoreType.DMA((1 + NCHIP,))],
        compiler_params=_cparams(("arbitrary",), 60 * 1024 * 1024),
    )(h, z, z, z, sg, ws, bsb, dcat, *dz_attn, hc, dck, dcv, g_out)


def ctx_bwd(dck, dcv, w_full, ctx, cshift, cscale, norm_g):
    def kern(dck_ref, dcv_ref, w_ref, c_ref, sh_ref, sc_ref, g_ref, dsh_ref, dsc_ref, dg_ref):
        csrc = dict(k=dck_ref, v=dcv_ref)
        dhc = None
        first = DZC_COLS[0][1]
        for name, c0, c1 in DZC_COLS:
            part = lax.dot_general(csrc[name][...].astype(BF16), w_ref[:, c0 - first:c1 - first], _NT,
                                   preferred_element_type=F32)
            dhc = part if dhc is None else dhc + part
        _, vjp = jax.vjp(lambda g, sc, sh: _modulated(c_ref[...], g, sc, sh), g_ref[...], sc_ref[...], sh_ref[...])
        dg_ref[...], dsc_ref[...], dsh_ref[...] = vjp(dhc)

    whole = lambda r, c: pl.BlockSpec((r, c), lambda i: (0, 0))
    return pl.pallas_call(
        kern, name="ctx_bwd", grid=(1,),
        in_specs=[whole(CTX, 512), whole(CTX, 512), pl.BlockSpec((DM, 1024), lambda i: (0, DZC_COLS[0][1] // 1024)),
                  whole(CTX, DM), _row(DM), _row(DM), _row(DM)],
        out_specs=[_row(DM), _row(DM), _row(DM)],
        out_shape=[jax.ShapeDtypeStruct((1, DM), F32)] * 3,
        compiler_params=_cparams(("arbitrary",), 40 * 1024 * 1024),
    )(dck, dcv, w_full, ctx, cshift, cscale, norm_g)


def _lane_pad_rpb(rpb):
    r = jnp.pad(rpb, ((0, 0), (0, 0), (0, GRID_W - rpb.shape[-1])))
    return jnp.concatenate([r, r], axis=-1)


def local_step(chip, dev, x, c_vec, c_ctx, w_ada, b_shard, ctx, target, norm_g, sgu_g, w_s, b_s, q_g, k_g, rpb,
               w_in_shard, w_out_shard):
    bsb = jnp.broadcast_to(b_s[:, :, None], (4, 128, 128))
    qg2, kg2 = jnp.tile(q_g, (1, 2)), jnp.tile(k_g, (1, 2))

    z, h, w_in_full, w_out_full, mod_all, cs = inproj_fwd(chip, x, c_vec, c_ctx, w_ada, b_shard, norm_g, w_in_shard,
                                                          w_out_shard)
    mods = mod_all.transpose(1, 0, 2).reshape(CS_ROWS, 3 * DM)
    mod = lax.dynamic_slice(mods, (8 * dev, 0), (1, 3 * DM))
    shift, scale, gate = mod[:, :DM], mod[:, DM:2 * DM], mod[:, 2 * DM:]
    cshift, cscale = mods[8 * NDEV:8 * NDEV + 1, :DM], mods[8 * NDEV:8 * NDEV + 1, DM:2 * DM]
    zc, hc = ctx_fwd(ctx, cshift, cscale, norm_g, w_in_full)
    out_b, *saved = attn_fwd(z, zc, _lane_pad_rpb(rpb), qg2, kg2)
    loss8, dy, dcat, dgate, dwo = outproj(z, sgu_g, w_s, bsb, out_b, x, target, gate, w_out_full.reshape(DM, DM))
    dq, dk, dv, dbg, dck, dcv, drpb, dqg2, dkg2 = attn_bwd(z, zc, qg2, kg2, dcat, saved)
    drpb = drpb[:, :, :rpb.shape[-1]]
    dcshift, dcscale, dng_c = ctx_bwd(dck, dcv, w_in_full, ctx, cshift, cscale, norm_g)
    wire_i, keep_i, wire_o, keep_o, dz_a, dsg, dws, dbsb = dw_bwd(
        h, z, sgu_g, w_s, bsb, dcat, (dq, dk, dv, dbg), hc, dck, dcv, dwo.reshape(NCHIP, SHARD_OUT, DM))
    dz_parts = (dz_a, dq, dk, dv, dbg)
    *in_flight, token = rs_start(wire_i, wire_o)
    grad_x, dshift, dscale, dng = dh_bwd(dz_parts, w_in_full, x, dy, shift, scale, norm_g, dng_c + token[0, 0])
    got_i, got_o = rs_wait(*in_flight, dshift)
    return dict(
        loss=loss8[0:1, 0:1], grad_x=grad_x, rs=(keep_i, got_i, keep_o, got_o), cs=cs,
        dmod=jnp.concatenate([dshift, dscale, dgate], axis=-1),
        dcmod=jnp.concatenate([dcshift, dcscale, jnp.zeros((1, DM), F32)], axis=-1),
        d_norm_g=dng, d_sgu_g=dsg, d_w_s=dws, d_b_s=dbsb[:, :, 0],
        d_q_g=dqg2[:, :HDIM], d_k_g=dkg2[:, :HDIM], d_rpb=drpb)


def _me():
    return lax.axis_index("x"), lax.axis_index("y"), lax.axis_index("c")


def _flip(q):
    x, y, c = _me()
    return ((1 - x) if q & 4 else x, (1 - y) if q & 2 else y, (1 - c) if q & 1 else c)


def _chip_of(dev):
    return 2 * dev[0] + dev[1]


def _rcopy(src, dst, send_sems, recv_sems, k, dev):
    return pltpu.make_async_remote_copy(src_ref=src, dst_ref=dst, send_sem=send_sems.at[k], recv_sem=recv_sems.at[k],
                                        device_id=dev, device_id_type=MESH_ID)


_VMEM_SPEC = pl.BlockSpec(memory_space=pltpu.VMEM)
SLAB_ROWS = 80


RS_SHAPES = ((DM // 2, SHARD_IN), (SHARD_OUT // 2, DM))
_HBM_SPEC = pl.BlockSpec(memory_space=pltpu.HBM)
_SEM_SPEC = pl.BlockSpec(memory_space=pltpu.SEMAPHORE)
_IN_FLIGHT = pltpu.SideEffectType.DATAFLOW_SIDE_EFFECTING


def _rs_copies(wires, lands, send_sems, recv_sems):
    return [pltpu.make_async_remote_copy(
        src_ref=wires[n].at[_chip_of(_flip(q))], dst_ref=lands[n].at[q // 2 - 1],
        send_sem=send_sems.at[3 * n + q // 2 - 1], recv_sem=recv_sems.at[3 * n + q // 2 - 1],
        device_id=_flip(q), device_id_type=MESH_ID) for n in (0, 1) for q in (2, 4, 6)]


def rs_start(wire_i, wire_o):
    lands = [lax.empty((NCHIP - 1, rh, w), BF16) for rh, w in RS_SHAPES]

    def body(wi_ref, wo_ref, li_ref, lo_ref, send_sems, recv_sems, wi_thru, wo_thru, li_thru, lo_thru, token):
        for cp in _rs_copies((wi_ref, wo_ref), (li_ref, lo_ref), send_sems, recv_sems):
            cp.start()
        token[...] = jnp.zeros_like(token)

    hbm = lambda a: pltpu.HBM(a.shape, a.dtype)
    return pl.pallas_call(
        body, name="rs_start",
        out_shape=(pltpu.SemaphoreType.DMA((6,)), pltpu.SemaphoreType.DMA((6,)), hbm(wire_i), hbm(wire_o),
                   hbm(lands[0]), hbm(lands[1]), jax.ShapeDtypeStruct((8, 128), F32)),
        in_specs=(_HBM_SPEC,) * 4, out_specs=(_SEM_SPEC, _SEM_SPEC) + (_HBM_SPEC,) * 4 + (_VMEM_SPEC,),
        input_output_aliases={0: 2, 1: 3, 2: 4, 3: 5},
        compiler_params=pltpu.CompilerParams(has_side_effects=_IN_FLIGHT),
    )(*[pltpu.with_memory_space_constraint(a, pltpu.HBM) for a in (wire_i, wire_o, *lands)])


def rs_wait(send_sems, recv_sems, wire_i, wire_o, land_i, land_o, after):
    def body(wi_ref, wo_ref, li_ref, lo_ref, send_sems, recv_sems, after_ref, wi_dead, wo_dead, gi_ref, go_ref):
        for cp in _rs_copies((wi_ref, wo_ref), (li_ref, lo_ref), send_sems, recv_sems):
            cp.wait_send()
            cp.wait_recv()

    hbm = lambda a: pltpu.HBM(a.shape, a.dtype)
    return pl.pallas_call(
        body, name="rs_wait", out_shape=(hbm(wire_i), hbm(wire_o), hbm(land_i), hbm(land_o)),
        in_specs=(_HBM_SPEC,) * 4 + (_SEM_SPEC, _SEM_SPEC, pl.BlockSpec(memory_space=pl.ANY)),
        out_specs=(_HBM_SPEC,) * 4, input_output_aliases={0: 0, 1: 1, 2: 2, 3: 3},
        compiler_params=pltpu.CompilerParams(has_side_effects=_IN_FLIGHT),
    )(wire_i, wire_o, land_i, land_o, send_sems, recv_sems, after)[2:]


def final_reduce(keep_i, got_i, keep_o, got_o, slab, cs, w_ada, c_ctx):
    (rhi, wi), (rho, wo) = RS_SHAPES

    def kern(ki_hbm, gi_hbm, ko_hbm, go_hbm, s_ref, cs_ref, w_hbm, cc_ref,
             gin_ref, gout_ref, tot_ref, dw_ref, db_ref, dcc_ref,
             ki, gi, ko, go, w_scr, all_ref, dms_scr, parts, load_sems, send_sems, recv_sems):
        x, y, c = _me()
        k = 2 * x + y
        sib = _flip(1)
        dev = lambda d: 4 * d[0] + 2 * d[1] + d[2]
        me = dev((x, y, c))

        def slab_copy(idx, owner, to):
            return _rcopy(all_ref.at[dev(owner)], all_ref.at[dev(owner)], send_sems, recv_sems, idx, to)

        all_ref[me] = s_ref[...]
        first = [slab_copy(0, (x, y, c), sib)] + [slab_copy(q // 2, (x, y, c), _flip(q)) for q in (2, 4, 6)]
        for cp in first:
            cp.start()
        loads = [pltpu.make_async_copy(src, dst, load_sems.at[n]) for n, (src, dst) in enumerate(
            ((ki_hbm, ki), (gi_hbm, gi), (ko_hbm, ko), (go_hbm, go), (w_hbm, w_scr)))]
        for cp in loads:
            cp.start()

        shares = []
        for n, (keep, got, out) in enumerate(((ki, gi, gin_ref), (ko, go, gout_ref))):
            rh = RS_SHAPES[n][0]
            half = lambda hh, rh=rh: pl.ds(pl.multiple_of(hh * rh, rh), rh)
            loads[2 * n].wait()
            loads[2 * n + 1].wait()
            out[half(c), :] = ((keep[...] + got[0].astype(F32)) + got[1].astype(F32)) + got[2].astype(F32)
            share = _rcopy(out.at[half(c), :], out.at[half(c), :], send_sems, recv_sems, 7 + n, sib)
            share.start()
            shares.append((share, _rcopy(out.at[half(1 - c), :], out.at[half(1 - c), :], send_sems, recv_sems, 7 + n,
                                         sib)))

        passed = []
        for q in (2, 4, 6):
            slab_copy(q // 2, _flip(q), (x, y, c)).wait_recv()
            cp = slab_copy(3 + q // 2, _flip(q), sib)
            cp.start()
            passed.append(cp)
        slab_copy(0, sib, (x, y, c)).wait_recv()
        for q in (2, 4, 6):
            slab_copy(3 + q // 2, _flip(q | 1), (x, y, c)).wait_recv()
        tot = all_ref[0]
        for d in range(1, NDEV):
            tot = tot + all_ref[d]
        tot_ref[...] = tot

        pad = jnp.zeros((7, DM), F32)
        dm = [jnp.concatenate([all_ref[d, 12 + j:13 + j, :] for d in range(NDEV)] + [tot[9 + j:10 + j, :], pad], axis=0)
              for j in range(3)]
        db_ref[...] = jnp.concatenate([jnp.sum(part, axis=0, keepdims=True) for part in dm], axis=0)
        dm = jnp.concatenate(dm, axis=-1)
        for j in range(NCHIP):
            @pl.when(k == j)
            def _():
                dms_scr[...] = dm[:, j * SHARD_ADA:(j + 1) * SHARD_ADA].astype(BF16)

        a_in = jnp.concatenate([cs_ref[8 * d:8 * d + 1, :] for d in range(NDEV)]
                               + [cs_ref[8 * NDEV:8 * NDEV + 1, :], pad], axis=0)
        act = jax.nn.silu(a_in).astype(BF16)
        dms = dms_scr[...]
        dw_ref[...] = lax.dot_general(act, dms, (((0,), (0,)), ((), ())), preferred_element_type=F32)
        loads[4].wait()
        parts[k] = lax.dot_general(dms, w_scr[...].astype(BF16), (((1,), (1,)), ((), ())), preferred_element_type=F32)
        sends = [_rcopy(parts.at[k], parts.at[k], send_sems, recv_sems, 8 + q // 2, _flip(q)) for q in (2, 4, 6)]
        for cp in sends:
            cp.start()
        for q in (2, 4, 6):
            kq = _chip_of(_flip(q))
            _rcopy(parts.at[kq], parts.at[kq], send_sems, recv_sems, 8 + q // 2, _flip(q)).wait_recv()
        dact = ((parts[0] + parts[1]) + parts[2]) + parts[3]
        _, vjp = jax.vjp(jax.nn.silu, cc_ref[...])
        dcc_ref[...] = vjp(dact[8:9, :])[0]

        for share, arrival in shares:
            arrival.wait_recv()
            share.wait_send()
        for cp in first + passed + sends:
            cp.wait_send()

    any_spec = pl.BlockSpec(memory_space=pl.ANY)
    return pl.pallas_call(
        kern, name="final_reduce",
        in_specs=[any_spec] * 4 + [_VMEM_SPEC, _VMEM_SPEC, any_spec, _VMEM_SPEC], out_specs=[_VMEM_SPEC] * 6,
        out_shape=[jax.ShapeDtypeStruct((2 * rhi, wi), F32), jax.ShapeDtypeStruct((2 * rho, wo), F32),
                   jax.ShapeDtypeStruct((SLAB_ROWS, DM), F32), jax.ShapeDtypeStruct((DM, SHARD_ADA), F32),
                   jax.ShapeDtypeStruct((3, DM), F32), jax.ShapeDtypeStruct((1, DM), F32)],
        scratch_shapes=[pltpu.VMEM((rhi, wi), F32), pltpu.VMEM((NCHIP - 1, rhi, wi), BF16),
                        pltpu.VMEM((rho, wo), F32), pltpu.VMEM((NCHIP - 1, rho, wo), BF16),
                        pltpu.VMEM((DM, SHARD_ADA), F32), pltpu.VMEM((NDEV, SLAB_ROWS, DM), F32),
                        pltpu.VMEM((16, SHARD_ADA), BF16), pltpu.VMEM((NCHIP, 16, DM), F32),
                        pltpu.SemaphoreType.DMA((5,)), pltpu.SemaphoreType.DMA((12,)), pltpu.SemaphoreType.DMA((12,))],
        compiler_params=pltpu.CompilerParams(vmem_limit_bytes=40 * 1024 * 1024),
    )(keep_i, got_i, keep_o, got_o, slab, cs, w_ada, c_ctx)


def _adamw_math(w, g, m, v):
    m = B1 * m + (1.0 - B1) * g
    v = B2 * v + (1.0 - B2) * (g * g)
    m_hat = m / (1.0 - B1 ** STEP)
    v_hat = v / (1.0 - B2 ** STEP)
    return -LR * (m_hat / (jnp.sqrt(v_hat) + ADAM_EPS) + WD * w), m, v


def adamw_big(w, g, m, v, name, block_rows=256):
    rows, width = w.shape

    def kern(w_ref, g_ref, m_ref, v_ref, d_ref, nm_ref, nv_ref):
        d_ref[...], nm_ref[...], nv_ref[...] = _adamw_math(w_ref[...], g_ref[...], m_ref[...], v_ref[...])

    spec = pl.BlockSpec((block_rows, width), lambda i: (i, 0))
    return pl.pallas_call(
        kern, name=name, grid=(rows // block_rows,), in_specs=[spec] * 4, out_specs=[spec] * 3,
        out_shape=[jax.ShapeDtypeStruct((rows, width), F32)] * 3,
        compiler_params=_cparams(("arbitrary",)),
    )(w, g, m, v)


def adamw_small(quads):
    n = len(quads)

    def kern(*refs):
        ins, outs = refs[:4 * n], refs[4 * n:]
        for i in range(n):
            w, g, m, v = (r[...] for r in ins[4 * i:4 * i + 4])
            outs[3 * i][...], outs[3 * i + 1][...], outs[3 * i + 2][...] = _adamw_math(w, g, m, v)

    flat = [a for quad in quads for a in quad]
    res = pl.pallas_call(
        kern, name="adamw_small", in_specs=[_VMEM_SPEC] * (4 * n), out_specs=[_VMEM_SPEC] * (3 * n),
        out_shape=[jax.ShapeDtypeStruct(q[0].shape, F32) for q in quads for _ in range(3)],
    )(*flat)
    return [tuple(res[3 * i:3 * i + 3]) for i in range(n)]


def _rows_of(a, rows):
    flat = a.reshape(-1)
    return jnp.pad(flat, (0, rows * DM - flat.shape[0])).reshape(rows, DM)


def kernel(x, c, ctx, c_ctx, w_ada, b_ada, norm_g, w_in, sgu_norm_g, w_spatial, b_spatial, q_norm_g, k_norm_g, rpb, w_out, loss_target, m_c_ctx, m_w_ada, m_b_ada, m_norm_g, m_w_in, m_sgu_norm_g, m_w_spatial, m_b_spatial, m_q_norm_g, m_k_norm_g, m_rpb, m_w_out, v_c_ctx, v_w_ada, v_b_ada, v_norm_g, v_w_in, v_sgu_norm_g, v_w_spatial, v_b_spatial, v_q_norm_g, v_k_norm_g, v_rpb, v_w_out):
    xi, yi, ci = lax.axis_index("x"), lax.axis_index("y"), lax.axis_index("c")
    chip, dev = 2 * xi + yi, 4 * xi + 2 * yi + ci
    c_ctx2 = c_ctx.reshape(1, DM)

    b_shard = lax.dynamic_slice(b_ada, (0, chip * SHARD_ADA), (1, SHARD_ADA))
    part = local_step(chip.reshape(1).astype(jnp.int32), dev, x[0], c, c_ctx2, w_ada[0], b_shard, ctx[0], loss_target[0],
                      norm_g, sgu_norm_g, w_spatial[0], b_spatial[0], q_norm_g, k_norm_g, rpb[0], w_in[0], w_out[0])
    cs = part["cs"]

    slab = jnp.concatenate([
        part["d_norm_g"], _rows_of(part["d_sgu_g"], 1), _rows_of(part["d_b_s"], 1),
        _rows_of(jnp.concatenate([part["d_q_g"], part["d_k_g"]], axis=-1), 1), _rows_of(part["d_rpb"], 4),
        _rows_of(part["loss"], 1), _rows_of(part["dcmod"], 3), _rows_of(part["dmod"], 3), jnp.zeros((1, DM), F32),
        _rows_of(part["d_w_s"], 64)], axis=0)
    g_w_in, g_w_out, tot, g_w_ada, g_b_ada, g_c_ctx = final_reduce(*part["rs"], slab, cs, w_ada[0], c_ctx2)
    g_b_ada = g_b_ada.reshape(1, 3 * DM)

    loss = tot[8, 0]
    g_small = dict(
        c_ctx=g_c_ctx, b_ada=g_b_ada, norm_g=tot[0:1], sgu_norm_g=tot[1:2, :512], w_spatial=tot[16:80].reshape(512, 128),
        b_spatial=tot[2:3, :512].reshape(4, 128), q_norm_g=tot[3:4, :HDIM], k_norm_g=tot[3:4, HDIM:2 * HDIM],
        rpb=tot[4:8].reshape(-1)[:HEADS * 15 * 31].reshape(HEADS * 15, 31))
    shapes = dict(c_ctx=(DM,), w_ada=(1, DM, SHARD_ADA), b_ada=(1, 3 * DM), norm_g=(1, DM), w_in=(1, DM, SHARD_IN),
                  sgu_norm_g=(1, 512), w_spatial=(1, 4, 128, 128), b_spatial=(1, 4, 128), q_norm_g=(1, HDIM),
                  k_norm_g=(1, HDIM), rpb=(1, HEADS, 15, 31), w_out=(1, SHARD_OUT, DM))
    names = list(shapes)
    weights = dict(c_ctx=c_ctx, w_ada=w_ada, b_ada=b_ada, norm_g=norm_g, w_in=w_in, sgu_norm_g=sgu_norm_g,
                   w_spatial=w_spatial, b_spatial=b_spatial, q_norm_g=q_norm_g, k_norm_g=k_norm_g, rpb=rpb, w_out=w_out)
    m_in = dict(zip(names, (m_c_ctx, m_w_ada, m_b_ada, m_norm_g, m_w_in, m_sgu_norm_g, m_w_spatial, m_b_spatial,
                            m_q_norm_g, m_k_norm_g, m_rpb, m_w_out)))
    v_in = dict(zip(names, (v_c_ctx, v_w_ada, v_b_ada, v_norm_g, v_w_in, v_sgu_norm_g, v_w_spatial, v_b_spatial,
                            v_q_norm_g, v_k_norm_g, v_rpb, v_w_out)))
    grads = dict(g_small, w_ada=g_w_ada, w_in=g_w_in, w_out=g_w_out)
    upd = {}
    for n in ("w_ada", "w_in", "w_out"):
        g = grads[n]
        upd[n] = adamw_big(weights[n].reshape(g.shape), g, m_in[n].reshape(g.shape), v_in[n].reshape(g.shape),
                           "adamw_" + n)
    small = [n for n in names if n not in upd]
    res = adamw_small([(weights[n].reshape(grads[n].shape), grads[n], m_in[n].reshape(grads[n].shape),
                        v_in[n].reshape(grads[n].shape)) for n in small])
    upd.update(zip(small, res))
    out = [loss, part["grad_x"].reshape(1, SEQ, DM)]
    out += [grads[n].reshape(shapes[n]) for n in names]
    for slot in range(3):
        out += [upd[n][slot].reshape(shapes[n]) for n in names]
    return tuple(out)
```

```python
import functools

import jax
import jax.numpy as jnp
from jax import lax
from jax.experimental import pallas as pl
from jax.experimental.pallas import tpu as pltpu

F32, BF16 = jnp.float32, jnp.bfloat16
SEQ, DM, CTX, DIN = 4096, 1024, 256, 3584
NCHIP, NDEV = 4, 8
SHARD_IN = DIN // NCHIP
SHARD_ADA = 3 * DM // NCHIP
SHARD_OUT = DM // NCHIP
GRID_W = 64
QROWS = 4
KROWS = 12
QBLK, KBLK = QROWS * GRID_W, KROWS * GRID_W
NQBLK = SEQ // QBLK
HEADS, HDIM, NPAIR = 8, 64, 4
EPS = 1e-6
NEG_INF = -1e30
ZQ, ZK, ZV, ZG = 12, 16, 20, 24
LR, B1, B2, ADAM_EPS, WD, STEP = 0.001, 0.9, 0.999, 1e-08, 0.01, 10
VMEM_BIG = 56 * 1024 * 1024
MESH_ID = pl.DeviceIdType.MESH


def _dot(a, b, lhs_c, rhs_c):
    return lax.dot_general(a.astype(BF16), b.astype(BF16), (((lhs_c,), (rhs_c,)), ((), ())),
                           preferred_element_type=F32)


@jax.custom_vjp
def mm(a, b):
    return _dot(a, b, 1, 0)


@jax.custom_vjp
def mm_nt(a, b):
    return _dot(a, b, 1, 1)


@jax.custom_vjp
def mm_tn(a, b):
    return _dot(a, b, 0, 0)


mm.defvjp(lambda a, b: (mm(a, b), (a, b)), lambda r, ct: (mm_nt(ct, r[1]), mm_tn(r[0], ct)))
mm_nt.defvjp(lambda a, b: (mm_nt(a, b), (a, b)), lambda r, ct: (mm(ct, r[1]), mm_tn(ct, r[0])))
mm_tn.defvjp(lambda a, b: (mm_tn(a, b), (a, b)), lambda r, ct: (mm_nt(r[1], ct), mm(r[0], ct)))


def _rms(x, g):
    return x * lax.rsqrt(jnp.mean(x * x, axis=-1, keepdims=True) + EPS) * g


def _modulated(x, g, scale, shift):
    return _rms(x, g) * (1.0 + scale) + shift


def _pair_rms(x, g2):
    lo = lax.broadcasted_iota(jnp.int32, (1, 2 * HDIM), 1) < HDIM
    sq = x * x
    s_lo = jnp.sum(jnp.where(lo, sq, 0.0), axis=-1, keepdims=True)
    s_hi = jnp.sum(jnp.where(lo, 0.0, sq), axis=-1, keepdims=True)
    rs = jnp.where(lo, lax.rsqrt(s_lo / HDIM + EPS), lax.rsqrt(s_hi / HDIM + EPS))
    return x * rs * g2


def _cparams(sem, vmem=None):
    return pltpu.CompilerParams(dimension_semantics=sem, vmem_limit_bytes=vmem)


def _row(n):
    return pl.BlockSpec((1, n), lambda *_: (0, 0))


CS_ROWS = 8 * NDEV + 8


def _mod_part(mod_ref, row, part):
    pieces = []
    for j in range(NCHIP):
        lo, hi = max(part * DM, j * SHARD_ADA), min((part + 1) * DM, (j + 1) * SHARD_ADA)
        if lo < hi:
            pieces.append(mod_ref[j, row, lo - j * SHARD_ADA:hi - j * SHARD_ADA])
    return jnp.concatenate(pieces, axis=-1)


def inproj_fwd(chip, x, c_vec, c_ctx, w_ada, b_shard, norm_g, w_shard, wo_shard):
    tl = 1024
    nt = SEQ // tl
    halves = (DM // 2, SHARD_OUT // 2)
    n_w, n_c = 12, NDEV - 1

    def kern(k_ref, x_ref, cv_ref, cc_ref, wa_ref, b_ref, g_ref, w_ref, wo_ref,
             z_ref, h_ref, wfull_ref, wofull_ref, modall_ref, csall_ref,
             w_scr, wo_scr, h_scr, mine, cs_scr, mod_scr, shsc_scr, send_sems, recv_sems, out_sems):
        s, t = pl.program_id(0), pl.program_id(1)
        xi, yi, c = _me()
        k, me = 2 * xi + yi, 4 * xi + 2 * yi + c
        sib = _flip(1)
        rows = pl.ds(pl.multiple_of(t * tl, tl), tl)
        gathered = (w_scr, wo_scr)
        slot = lambda d: pl.ds(pl.multiple_of(8 * d, 8), 8)

        def c_copy(q, owner):
            return _rcopy(mine, cs_scr.at[slot(owner), :], send_sems, recv_sems, n_w + q - 1, _flip(q))

        def m_copy(q, chip_of_block):
            return _rcopy(mod_scr.at[chip_of_block], mod_scr.at[chip_of_block], send_sems, recv_sems,
                          n_w + n_c + q // 2 - 1, _flip(q))

        def adaln():
            first = lax.broadcasted_iota(jnp.int32, (8, DM), 0) == 0
            mine[...] = jnp.where(first, jnp.broadcast_to(cv_ref[...], (8, DM)), 0.0)
            cs_scr[slot(me), :] = mine[...]
            cs_scr[slot(NDEV), :] = jnp.where(first, jnp.broadcast_to(cc_ref[...], (8, DM)), 0.0)
            for q in range(1, NDEV):
                c_copy(q, me).start()
            wa = wa_ref[...].astype(BF16)
            for q in range(1, NDEV):
                px, py, pc = _flip(q)
                c_copy(q, 4 * px + 2 * py + pc).wait_recv()
            act = jax.nn.silu(cs_scr[...]).astype(BF16)
            mod_scr[k] = jnp.dot(act, wa, preferred_element_type=F32) + b_ref[...]
            for q in (2, 4, 6):
                m_copy(q, k).start()
            for q in (2, 4, 6):
                m_copy(q, _chip_of(_flip(q))).wait_recv()
            row = pl.ds(8 * me, 1)
            shsc_scr[0:1, :] = _mod_part(mod_scr, row, 0)
            shsc_scr[1:2, :] = _mod_part(mod_scr, row, 1)
            pltpu.sync_copy(mod_scr, modall_ref)
            pltpu.sync_copy(cs_scr, csall_ref)

        def block(n, chip_of_block, hh):
            return gathered[n].at[chip_of_block, pl.ds(pl.multiple_of(hh * halves[n], halves[n]), halves[n]), :]

        def ici(n, q, chip_of_block):
            blk = block(n, chip_of_block, c)
            return _rcopy(blk, blk, send_sems, recv_sems, 6 * n + q // 2 - 1, _flip(q))

        def d2d(n, q, chip_of_block, hh):
            blk = block(n, chip_of_block, hh)
            return _rcopy(blk, blk, send_sems, recv_sems, 6 * n + 3 + q // 2 - 1, sib)

        @pl.when((s == 0) & (t == 0))
        def _():
            adaln()
            w_scr[k] = w_ref[...].astype(BF16)
            wo_scr[k] = wo_ref[...].astype(BF16)
            for q in (2, 4, 6):
                ici(0, q, k).start()
                ici(1, q, k).start()

        for sweep in (1, 2, 3):
            @pl.when((s == sweep) & (t == 0))
            def _():
                q = 2 * sweep
                src = _chip_of(_flip(q))
                for n in (0, 1):
                    ici(n, q, src).wait_recv()
                    d2d(n, q, src, c).start()
                for n in (0, 1):
                    d2d(n, q, src, 1 - c).wait_recv()

        @pl.when(s == 0)
        def _():
            hb = _modulated(x_ref[...], g_ref[...], shsc_scr[1:2, :], shsc_scr[0:1, :]).astype(BF16)
            h_scr[rows, :] = hb
            h_ref[...] = hb

        z_ref[...] = jnp.dot(h_scr[rows, :], w_scr[lax.bitwise_xor(k, s)], preferred_element_type=F32)

        @pl.when((s == NCHIP - 1) & (t == nt - 1))
        def _():
            for q in range(1, NDEV):
                c_copy(q, me).wait_send()
            for q in (2, 4, 6):
                m_copy(q, k).wait_send()
            for n in (0, 1):
                for q in (2, 4, 6):
                    ici(n, q, k).wait_send()
                    d2d(n, q, _chip_of(_flip(q)), c).wait_send()
            outs = [pltpu.make_async_copy(w_scr.at[j], wfull_ref.at[:, j * SHARD_IN:(j + 1) * SHARD_IN], out_sems.at[j])
                    for j in range(NCHIP)] + [pltpu.make_async_copy(wo_scr, wofull_ref, out_sems.at[NCHIP])]
            for cp in outs:
                cp.start()
            for cp in outs:
                cp.wait()

    once = lambda s, t, k: (jnp.where(s == 0, t, nt - 1), 0)
    hbm = pl.BlockSpec(memory_space=pl.ANY)
    n_sem = n_w + n_c + 3
    return pl.pallas_call(
        kern, name="inproj_fwd",
        grid_spec=pltpu.PrefetchScalarGridSpec(
            num_scalar_prefetch=1, grid=(NCHIP, nt),
            in_specs=[pl.BlockSpec((tl, DM), once)] + [_VMEM_SPEC] * 7,
            out_specs=[pl.BlockSpec((tl, SHARD_IN), lambda s, t, k: (t, lax.bitwise_xor(k[0], s))),
                       pl.BlockSpec((tl, DM), once), hbm, hbm, hbm, hbm],
            scratch_shapes=[pltpu.VMEM((NCHIP, DM, SHARD_IN), BF16), pltpu.VMEM((NCHIP, SHARD_OUT, DM), BF16),
                            pltpu.VMEM((SEQ, DM), BF16), pltpu.VMEM((8, DM), F32), pltpu.VMEM((CS_ROWS, DM), F32),
                            pltpu.VMEM((NCHIP, CS_ROWS, SHARD_ADA), F32), pltpu.VMEM((8, DM), F32),
                            pltpu.SemaphoreType.DMA((n_sem,)), pltpu.SemaphoreType.DMA((n_sem,)),
                            pltpu.SemaphoreType.DMA((NCHIP + 1,))]),
        out_shape=[jax.ShapeDtypeStruct((SEQ, DIN), F32), jax.ShapeDtypeStruct((SEQ, DM), BF16),
                   jax.ShapeDtypeStruct((DM, DIN), BF16), jax.ShapeDtypeStruct((NCHIP, SHARD_OUT, DM), BF16),
                   jax.ShapeDtypeStruct((NCHIP, CS_ROWS, SHARD_ADA), F32), jax.ShapeDtypeStruct((CS_ROWS, DM), F32)],
        compiler_params=_cparams(("arbitrary", "arbitrary"), VMEM_BIG),
    )(chip, x, c_vec, c_ctx, w_ada, b_shard, norm_g, w_shard, wo_shard)


def ctx_fwd(ctx, cshift, cscale, norm_g, w_full):
    def kern(c_ref, sh_ref, sc_ref, g_ref, w_ref, zc_ref, hc_ref):
        hc = _modulated(c_ref[...], g_ref[...], sc_ref[...], sh_ref[...]).astype(BF16)
        hc_ref[...] = hc
        zc_ref[...] = jnp.dot(hc, w_ref[...], preferred_element_type=F32)

    return pl.pallas_call(
        kern, name="ctx_fwd", grid=(1,),
        in_specs=[pl.BlockSpec((CTX, DM), lambda i: (0, 0)), _row(DM), _row(DM), _row(DM),
                  pl.BlockSpec((DM, 2 * SHARD_IN), lambda i: (0, 1))],
        out_specs=[pl.BlockSpec((CTX, 2 * SHARD_IN), lambda i: (0, 0)),
                   pl.BlockSpec((CTX, DM), lambda i: (0, 0))],
        out_shape=[jax.ShapeDtypeStruct((CTX, 2 * SHARD_IN), F32), jax.ShapeDtypeStruct((CTX, DM), BF16)],
        compiler_params=_cparams(("arbitrary",)),
    )(ctx, cshift, cscale, norm_g, w_full)


SGU_CHUNK, SGU_PER_STEP = 128, 4


def _gelu(x):
    return 0.5 * x * (1.0 + lax.erf(x * 0.7071067811865476))


def _sgu_group(au, av, ag, sg, ws, bs):
    return _gelu(au) * (mm(ws, _rms(_gelu(av), sg)) + bs) * jax.nn.silu(ag)


def _sgu_chunk(au, av, ag, sg, ws, bsb):
    u, v = _gelu(au), _gelu(av)
    outs = []
    for g in range(4):
        sl = slice(128 * g, 128 * (g + 1))
        mixed = mm(ws[g], _rms(v[:, sl], sg[:, sl])) + bsb[g]
        outs.append(u[:, sl] * mixed * jax.nn.silu(ag[:, sl]))
    return jnp.concatenate(outs, axis=-1)


def _sgu_specs():
    rows = SGU_CHUNK * SGU_PER_STEP
    zspec = lambda c: pl.BlockSpec((rows, 512), lambda n: (n, c))
    wspec = pl.BlockSpec((4, 128, 128), lambda n: (0, 0, 0))
    return rows, [zspec(0), zspec(1), zspec(2), _row(512), wspec, wspec]


_DR_OFF = (7, 3, -1)


def _row_valid(v, rr, j):
    return (j < 8, rr <= j < rr + 8, 4 <= j < 12)[v]


def _col_window():
    q = lax.broadcasted_iota(jnp.int32, (GRID_W, 128), 0)
    kc = lax.broadcasted_iota(jnp.int32, (GRID_W, 128), 1) % GRID_W
    c0 = jnp.clip(q - 8, 0, GRID_W - 16)
    return (kc >= c0) & (kc < c0 + 16)


def _bias_tiles(base, store):
    lo = lax.broadcasted_iota(jnp.int32, (1, 128), 1) < GRID_W
    win = _col_window()
    tiles = {}
    for v in range(3):
        for rr in range(QROWS):
            for jp in range(KROWS // 2):
                j0, j1 = 2 * jp, 2 * jp + 1
                ok0, ok1 = _row_valid(v, rr, j0), _row_valid(v, rr, j1)
                key = (j0 - rr + _DR_OFF[v], ok0, ok1) if (ok0 or ok1) else None
                if key not in tiles:
                    if key is None:
                        tiles[key] = jnp.full((GRID_W, 128), NEG_INF, F32)
                    else:
                        d0 = key[0]
                        r0 = base[d0:d0 + 1, :] if ok0 else jnp.zeros((1, 128), F32)
                        r1 = base[d0 + 1:d0 + 2, :] if ok1 else jnp.zeros((1, 128), F32)
                        y = jnp.broadcast_to(jnp.where(lo, r0, r1), (GRID_W, 128))
                        y = pltpu.roll(pltpu.roll(y, 128 - 15, 1), 0, 1, stride=1, stride_axis=0)
                        tiles[key] = jnp.where(win & jnp.where(lo, ok0, ok1), y, NEG_INF)
                store(v, slice(rr * GRID_W, (rr + 1) * GRID_W), slice(jp * 128, (jp + 1) * 128), tiles[key])


def _rpb_grad(load):
    lo = lax.broadcasted_iota(jnp.int32, (1, 128), 1) < GRID_W
    ri = lax.broadcasted_iota(jnp.int32, (GRID_W, GRID_W), 0)
    ci = lax.broadcasted_iota(jnp.int32, (GRID_W, GRID_W), 1)
    flip = (ri + ci == GRID_W - 1).astype(F32)
    groups = {}
    for v in range(3):
        for rr in range(QROWS):
            for jp in range(KROWS // 2):
                j0, j1 = 2 * jp, 2 * jp + 1
                ok0, ok1 = _row_valid(v, rr, j0), _row_valid(v, rr, j1)
                if not (ok0 or ok1):
                    continue
                g = load(v, slice(rr * GRID_W, (rr + 1) * GRID_W), slice(jp * 128, (jp + 1) * 128))
                key = (j0 - rr + _DR_OFF[v], ok0, ok1)
                groups[key] = g if key not in groups else groups[key] + g
    acc = [jnp.zeros((1, 128), F32) for _ in range(15)]
    for (d0, ok0, ok1), g in groups.items():
        g = lax.dot_general(flip, g, (((1,), (0,)), ((), ())), precision=lax.Precision.HIGHEST,
                            preferred_element_type=F32)
        g = pltpu.roll(pltpu.roll(g, 128 - 48, 1), 0, 1, stride=1, stride_axis=0)
        s = jnp.sum(g, axis=0, keepdims=True)
        if ok0:
            acc[d0] = acc[d0] + jnp.where(lo, s, 0.0)
        if ok1:
            acc[d0 + 1] = acc[d0 + 1] + jnp.where(lo, 0.0, s)
    return [row + pltpu.roll(row, GRID_W, 1) for row in acc]


def _scaled_q(q_raw, qg):
    return _pair_rms(q_raw, qg) * (HDIM ** -0.5)


def _head_lanes():
    lo = lax.broadcasted_iota(jnp.int32, (1, 2 * HDIM), 1) < HDIM
    return lo, jnp.logical_not(lo)


SOFTMAX_ROWS = 32


def _emit_interleaved(vector_work, matmul_work):
    for j in range(max(len(vector_work), len(matmul_work))):
        for work in (vector_work, matmul_work):
            if j < len(work):
                work[j]()


def _kblock(i):
    return jnp.clip(i - 1, 0, (SEQ - KBLK) // QBLK)


def _kstart(i):
    return pl.multiple_of(_kblock(i) * QBLK, QBLK)


ATTN_BLOCKS = 4
TILE_BUFFERS = 4
ATTN_STEPS = NQBLK // ATTN_BLOCKS
ATTN_ROWS = ATTN_BLOCKS * QBLK


def _bias_variant(i, b):
    if b == 0:
        return jnp.where(i == 0, 0, 1)
    if b == ATTN_BLOCKS - 1:
        return jnp.where(i == ATTN_STEPS - 1, 2, 1)
    return 1
KCOLS = QBLK


def _attn_in_specs():
    return [
        pl.BlockSpec((ATTN_ROWS, 128), lambda p, i: (i, ZQ + p)),
        pl.BlockSpec((SEQ, 128), lambda p, i: (0, ZK + p)),
        pl.BlockSpec((SEQ, 128), lambda p, i: (0, ZV + p)),
        pl.BlockSpec((ATTN_ROWS, 128), lambda p, i: (i, ZG + p)),
        pl.BlockSpec((CTX, 128), lambda p, i: (0, 2 + p)),
        pl.BlockSpec((CTX, 128), lambda p, i: (0, 6 + p)),
    ]


def _rpb_spec():
    return pl.BlockSpec((2, 15, 128), lambda p, i: (p, 0, 0))


def _prob_specs():
    return [pl.BlockSpec((2, ATTN_ROWS, KBLK), lambda p, i: (p, i, 0)),
            pl.BlockSpec((2, ATTN_ROWS, CTX), lambda p, i: (p, i, 0))]


NORM_ROWS = 2048


def _half_sums(x):
    lo = lax.broadcasted_iota(jnp.int32, (1, 2 * HDIM), 1) < HDIM
    return jnp.where(lo, jnp.sum(jnp.where(lo, x, 0.0), axis=-1, keepdims=True),
                     jnp.sum(jnp.where(lo, 0.0, x), axis=-1, keepdims=True))


def _pair_rms_bwd(x, g2, ct):
    rs = lax.rsqrt(_half_sums(x * x) / HDIM + EPS)
    y = x * rs
    dy = ct * g2
    return rs * (dy - y * (_half_sums(dy * y) / HDIM)), jnp.sum(ct * y, axis=0, keepdims=True)


def _norm_keys(k_ref, ck_ref, kg_ref, kn_scr, ckn_scr):
    def body(c, carry):
        sl = pl.ds(pl.multiple_of(c * NORM_ROWS, NORM_ROWS), NORM_ROWS)
        kn_scr[sl, :] = _pair_rms(k_ref[sl, :], kg_ref[...]).astype(BF16)
        return carry

    lax.fori_loop(0, SEQ // NORM_ROWS, body, 0)
    ckn_scr[...] = _pair_rms(ck_ref[...], kg_ref[...]).astype(BF16)


def _values_with_ones(v_ref, cv_ref, v1_scr, cv1_scr):
    for a, mine in enumerate(_head_lanes()):
        def body(c, carry):
            sl = pl.ds(pl.multiple_of(c * NORM_ROWS, NORM_ROWS), NORM_ROWS)
            v1_scr[a, sl, :] = jnp.where(mine, v_ref[sl, :], 1.0).astype(BF16)
            return carry

        lax.fori_loop(0, SEQ // NORM_ROWS, body, 0)
        cv1_scr[a] = jnp.where(mine, cv_ref[...], 1.0).astype(BF16)


def _pair_major_spec():
    return pl.BlockSpec((1, ATTN_ROWS, 128), lambda p, i: (p, i, 0))


def _normed_key_specs():
    return [pl.BlockSpec((None, SEQ, 128), lambda p, i: (p, 0, 0)), pl.BlockSpec((None, CTX, 128), lambda p, i: (p, 0, 0))]


def attn_fwd(z, zc, rpb2, qg2, kg2):
    def kern(q_ref, k_ref, v_ref, bg_ref, ck_ref, cv_ref, rpb_ref, qg_ref, kg_ref,
             ob_ref, o_ref, rden_ref, pl_ref, pc_ref, kn_ref, ckn_ref, kn_scr, ckn_scr, v1_scr, cv1_scr, s_scr,
             bias_ref):
        i = pl.program_id(1)

        @pl.when(i == 0)
        def _():
            for a in range(2):
                def store(v, tile_rows, tile_cols, tile, a=a):
                    bias_ref[v, a, tile_rows, tile_cols] = tile

                _bias_tiles(rpb_ref[a], store)
            _norm_keys(k_ref, ck_ref, kg_ref, kn_scr, ckn_scr)
            kn_ref[...] = kn_scr[...]
            ckn_ref[...] = ckn_scr[...]
            _values_with_ones(v_ref, cv_ref, v1_scr, cv1_scr)

        heads = _head_lanes()
        tiles = [(b, a) for b in range(ATTN_BLOCKS) for a in range(2)]
        rows = [slice(b * QBLK, (b + 1) * QBLK) for b in range(ATTN_BLOCKS)]
        variant = [_bias_variant(i, b) for b in range(ATTN_BLOCKS)]
        pv = [None] * len(tiles)
        qa, done = {}, {}
        latent = KBLK // KCOLS
        buf = lambda t: t % TILE_BUFFERS

        def keys(b, n):
            return pl.ds(pl.multiple_of(_kstart(ATTN_BLOCKS * i + b) + n * KCOLS, KCOLS), KCOLS)

        def score_piece(t, n):
            b, a = tiles[t]
            cols = slice(n * KCOLS, (n + 1) * KCOLS)
            if n == 0:
                if a == 0:
                    done["qn", b] = _scaled_q(q_ref[rows[b], :], qg_ref[...])
                qa[t] = jnp.where(heads[a], done["qn", b], 0.0).astype(BF16)
            if n < latent:
                s_scr[buf(t), :, cols] = mm_nt(qa[t], kn_scr[keys(b, n), :]) + bias_ref[variant[b], a, :, cols]
            else:
                s_scr[buf(t), :, cols] = mm_nt(qa[t], ckn_scr[...])

        def softmax_rows(t, r):
            b, a = tiles[t]
            rs = slice(r * SOFTMAX_ROWS, (r + 1) * SOFTMAX_ROWS)
            out_rows = slice(b * QBLK + rs.start, b * QBLK + rs.stop)
            s = s_scr[buf(t), rs, :]
            p = jnp.exp(s - jnp.max(s, axis=-1, keepdims=True)).astype(BF16)
            pl_ref[a, out_rows, :] = p[:, :KBLK]
            pc_ref[a, out_rows, :] = p[:, KBLK:]

        def value_piece(t, n):
            b, a = tiles[t]
            if n < latent:
                part = mm(pl_ref[a, rows[b], n * KCOLS:(n + 1) * KCOLS], v1_scr[a, keys(b, n), :])
            else:
                part = mm(pc_ref[a, rows[b], :], cv1_scr[a])
            pv[t] = part if pv[t] is None else pv[t] + part
            if n == latent:
                finish(t)

        def finish(t):
            b, a = tiles[t]
            r = jnp.where(heads[a], pltpu.roll(1.0 / pv[t], HDIM, 1), 0.0)
            done[t] = (pv[t] * r, r)
            if a == 1:
                o, rden = (lo + hi for lo, hi in zip(done[t - 1], done[t]))
                ob_ref[rows[b], :] = o * jax.nn.silu(bg_ref[rows[b], :])
                o_ref[0, rows[b], :] = o
                rden_ref[0, rows[b], :] = rden

        pieces = range(latent + 1)
        for n in pieces:
            score_piece(0, n)
        for t in range(len(tiles)):
            matmuls = []
            for n in pieces:
                if t + 1 < len(tiles):
                    matmuls.append(functools.partial(score_piece, t + 1, n))
                if t > 0:
                    matmuls.append(functools.partial(value_piece, t - 1, n))
            _emit_interleaved([functools.partial(softmax_rows, t, r) for r in range(QBLK // SOFTMAX_ROWS)], matmuls)
        for n in pieces:
            value_piece(len(tiles) - 1, n)

    qblk = pl.BlockSpec((ATTN_ROWS, 128), lambda p, i: (i, p))
    return pl.pallas_call(
        kern, name="attn_fwd", grid=(NPAIR, ATTN_STEPS),
        in_specs=_attn_in_specs() + [_rpb_spec(), _row(128), _row(128)],
        out_specs=[qblk, _pair_major_spec(), _pair_major_spec()] + _prob_specs() + _normed_key_specs(),
        out_shape=[jax.ShapeDtypeStruct((SEQ, 512), F32)] + [jax.ShapeDtypeStruct((NPAIR, SEQ, 128), F32)] * 2
        + [jax.ShapeDtypeStruct((HEADS, SEQ, KBLK), BF16), jax.ShapeDtypeStruct((HEADS, SEQ, CTX), BF16),
           jax.ShapeDtypeStruct((NPAIR, SEQ, 128), BF16), jax.ShapeDtypeStruct((NPAIR, CTX, 128), BF16)],
        scratch_shapes=[pltpu.VMEM((SEQ, 128), BF16), pltpu.VMEM((CTX, 128), BF16),
                        pltpu.VMEM((2, SEQ, 128), BF16), pltpu.VMEM((2, CTX, 128), BF16),
                        pltpu.VMEM((TILE_BUFFERS, QBLK, KBLK + CTX), F32),
                        pltpu.VMEM((3, 2, QBLK, KBLK), F32)],
        compiler_params=_cparams(("arbitrary", "arbitrary"), VMEM_BIG),
    )(z, z, z, z, zc, zc, rpb2, qg2, kg2)


def attn_bwd(z, zc, qg2, kg2, dcat, saved):
    def kern(q_ref, k_ref, v_ref, bg_ref, ck_ref, cv_ref, qg_ref, kg_ref, do_ref, o_ref, rden_ref, pl_ref, pc_ref,
             kn_scr, ckn_scr, dq_ref, dk_ref, dv_ref, dbg_ref, dck_ref, dcv_ref, drpb_ref, dqg_ref, dkg_ref,
             v_scr, cv_scr, dknt_scr, dvt_scr, dcknt_scr, dcvt_scr, dp_scr, ds_scr, db_ref):
        p, i = pl.program_id(0), pl.program_id(1)
        last = i == ATTN_STEPS - 1

        @pl.when(i == 0)
        def _():
            def body(c, carry):
                sl = pl.ds(pl.multiple_of(c * NORM_ROWS, NORM_ROWS), NORM_ROWS)
                v_scr[sl, :] = v_ref[sl, :].astype(BF16)
                return carry

            lax.fori_loop(0, SEQ // NORM_ROWS, body, 0)
            cv_scr[...] = cv_ref[...].astype(BF16)
            for acc in (dknt_scr, dvt_scr, dcknt_scr, dcvt_scr, db_ref):
                acc[...] = jnp.zeros_like(acc)

        @pl.when((i == 0) & (p == 0))
        def _():
            dqg_ref[...] = jnp.zeros_like(dqg_ref)
            dkg_ref[...] = jnp.zeros_like(dkg_ref)

        heads = _head_lanes()
        tiles = [(b, a) for b in range(ATTN_BLOCKS) for a in range(2)]
        rows = [slice(b * QBLK, (b + 1) * QBLK) for b in range(ATTN_BLOCKS)]
        kb = [_kblock(ATTN_BLOCKS * i + b) for b in range(ATTN_BLOCKS)]
        variant = [_bias_variant(i, b) for b in range(ATTN_BLOCKS)]
        latent = KBLK // KCOLS
        buf = lambda t: t % TILE_BUFFERS

        def keys(b, n):
            return pl.ds(pl.multiple_of((kb[b] + n) * KCOLS, KCOLS), KCOLS)

        gated = {}

        def gate_backward(b):
            bg, dout, o = bg_ref[rows[b], :], do_ref[rows[b], :], o_ref[0, rows[b], :]
            sig = jax.nn.sigmoid(bg)
            do = dout * (bg * sig)
            dbg_ref[rows[b], :] = (dout * o * (sig * (1.0 + bg * (1.0 - sig)))).astype(BF16)
            rden = rden_ref[0, rows[b], :]
            dr = do * rden
            qn = _scaled_q(q_ref[rows[b], :], qg_ref[...])
            gated[b] = (dr, dr.T.astype(BF16), qn.T.astype(BF16), do * o * rden)

        feats = [slice(a * HDIM, (a + 1) * HDIM) for a in range(2)]
        doa, doa_t, qa_t, delta = {}, {}, {}, {}
        dqn = [None] * len(tiles)

        def cols(n):
            return slice(n * KCOLS, (n + 1) * KCOLS)

        def stage_a(t, n):
            b, a = tiles[t]
            if n == 0:
                if a == 0:
                    gate_backward(b)
                dr, dr_t, qn_t, weighted = gated[b]
                doa[t] = jnp.where(heads[a], dr, 0.0).astype(BF16)
                doa_t[t] = dr_t[feats[a], :]
                qa_t[t] = qn_t[feats[a], :]
                delta[t] = jnp.sum(jnp.where(heads[a], weighted, 0.0), axis=-1, keepdims=True)
            if n < latent:
                dp_scr[buf(t), :, cols(n)] = mm_nt(doa[t], v_scr[keys(b, n), :])
                dvt_scr[kb[b] + n, feats[a], :] += mm(doa_t[t], pl_ref[a, rows[b], cols(n)])
            else:
                dp_scr[buf(t), :, cols(n)] = mm_nt(doa[t], cv_scr[...])
                dcvt_scr[feats[a], :] += mm(doa_t[t], pc_ref[a, rows[b], :])

        def stage_b(t, r):
            b, a = tiles[t]
            rs = slice(r * SOFTMAX_ROWS, (r + 1) * SOFTMAX_ROWS)
            in_rows = slice(b * QBLK + rs.start, b * QBLK + rs.stop)
            d = dp_scr[buf(t), rs, :] - delta[t][rs, :]
            ds_lat = pl_ref[a, in_rows, :].astype(F32) * d[:, :KBLK]
            ds_ctx = pc_ref[a, in_rows, :].astype(F32) * d[:, KBLK:]
            db_ref[variant[b], a, rs, :] += ds_lat
            ds_scr[buf(t), rs, :KBLK] = ds_lat.astype(BF16)
            ds_scr[buf(t), rs, KBLK:] = ds_ctx.astype(BF16)

        def stage_c(t, n):
            b, a = tiles[t]
            ds = ds_scr[buf(t), :, cols(n)]
            if n < latent:
                part = mm(ds, kn_scr[keys(b, n), :])
                dknt_scr[kb[b] + n, feats[a], :] += mm(qa_t[t], ds)
            else:
                part = mm(ds, ckn_scr[...])
                dcknt_scr[feats[a], :] += mm(qa_t[t], ds)
            dqn[t] = part if dqn[t] is None else dqn[t] + part
            if n == latent and a == 1:
                both = jnp.where(heads[0], dqn[t - 1], 0.0) + jnp.where(heads[1], dqn[t], 0.0)
                dq, dqg = jax.vjp(_scaled_q, q_ref[rows[b], :], qg_ref[...])[1](both)
                dq_ref[rows[b], :] = dq.astype(BF16)
                dqg_ref[...] += dqg

        pieces = range(latent + 1)
        for n in pieces:
            stage_a(0, n)
        for t in range(len(tiles)):
            matmuls = []
            for n in pieces:
                if t + 1 < len(tiles):
                    matmuls.append(functools.partial(stage_a, t + 1, n))
                if t > 0:
                    matmuls.append(functools.partial(stage_c, t - 1, n))
            _emit_interleaved([functools.partial(stage_b, t, r) for r in range(QBLK // SOFTMAX_ROWS)], matmuls)
        for n in pieces:
            stage_c(len(tiles) - 1, n)

        @pl.when(last)
        def _():
            eye = (lax.broadcasted_iota(jnp.int32, (KCOLS, KCOLS), 0)
                   == lax.broadcasted_iota(jnp.int32, (KCOLS, KCOLS), 1)).astype(BF16)

            def turned(x):
                hi = x.astype(BF16)
                return mm_nt(eye, hi) + mm_nt(eye, x - hi.astype(F32))

            def body(c, dkg):
                sl = pl.ds(pl.multiple_of(c * NORM_ROWS, NORM_ROWS), NORM_ROWS)
                blocks = range(NORM_ROWS // KCOLS)
                dkn = jnp.concatenate([turned(dknt_scr[c * len(blocks) + n]) for n in blocks], axis=0)
                dv = jnp.concatenate([mm_nt(eye, dvt_scr[c * len(blocks) + n]) for n in blocks], axis=0)
                dk, dg = _pair_rms_bwd(k_ref[sl, :], kg_ref[...], dkn)
                dk_ref[sl, :] = dk.astype(BF16)
                dv_ref[sl, :] = dv.astype(BF16)
                return dkg + dg

            dkg = lax.fori_loop(0, SEQ // NORM_ROWS, body, jnp.zeros((1, 128), F32))
            dck, dg = _pair_rms_bwd(ck_ref[...], kg_ref[...], dcknt_scr[...].T)
            dck_ref[...] = dck
            dcv_ref[...] = dcvt_scr[...].T
            dkg_ref[...] += dkg + dg
            for a in range(2):
                rows_of_rpb = _rpb_grad(lambda v, tile_rows, tile_cols, a=a: db_ref[v, a, tile_rows, tile_cols])
                for d, row in enumerate(rows_of_rpb):
                    drpb_ref[a, d:d + 1, :] = row

        @pl.when(last & (p == NPAIR - 1))
        def _():
            dqg_ref[...] = dqg_ref[...] + pltpu.roll(dqg_ref[...], HDIM, 1)
            dkg_ref[...] = dkg_ref[...] + pltpu.roll(dkg_ref[...], HDIM, 1)

    blk = lambda rows: pl.BlockSpec((rows, 128), lambda p, i: (0, p))
    qblk = pl.BlockSpec((ATTN_ROWS, 128), lambda p, i: (i, p))
    return pl.pallas_call(
        kern, name="attn_bwd", grid=(NPAIR, ATTN_STEPS),
        in_specs=_attn_in_specs() + [_row(128), _row(128), pl.BlockSpec((ATTN_ROWS, 128), lambda p, i: (i, 4 + p)),
                                     _pair_major_spec(), _pair_major_spec()] + _prob_specs() + _normed_key_specs(),
        out_specs=[qblk, blk(SEQ), blk(SEQ), qblk, blk(CTX), blk(CTX), _rpb_spec(), _row(128), _row(128)],
        out_shape=[jax.ShapeDtypeStruct((SEQ, 512), BF16)] * 4 + [jax.ShapeDtypeStruct((CTX, 512), F32)] * 2
        + [jax.ShapeDtypeStruct((HEADS, 15, 128), F32)]
        + [jax.ShapeDtypeStruct((1, 128), F32), jax.ShapeDtypeStruct((1, 128), F32)],
        scratch_shapes=[pltpu.VMEM((SEQ, 128), BF16), pltpu.VMEM((CTX, 128), BF16),
                        pltpu.VMEM((SEQ // KCOLS, 128, KCOLS), F32), pltpu.VMEM((SEQ // KCOLS, 128, KCOLS), F32),
                        pltpu.VMEM((128, CTX), F32), pltpu.VMEM((128, CTX), F32),
                        pltpu.VMEM((TILE_BUFFERS, QBLK, KBLK + CTX), F32),
                        pltpu.VMEM((TILE_BUFFERS, QBLK, KBLK + CTX), BF16),
                        pltpu.VMEM((3, 2, QBLK, KBLK), F32)],
        compiler_params=_cparams(("arbitrary", "arbitrary"), VMEM_BIG),
    )(z, z, z, z, zc, zc, qg2, kg2, dcat, *saved)


def outproj(z, sg, ws, bsb, out_b, x, target, gate, wo):
    tl = SGU_CHUNK * SGU_PER_STEP
    nt = SEQ // tl

    def kern(au0_ref, av0_ref, ag0_ref, au1_ref, av1_ref, ag1_ref, sg_ref, ws_ref, bs_ref, b_ref, x_ref, t_ref, g_ref,
             w_ref, loss_ref, dy_ref, dcat_ref, dg_ref, dw_ref, a_scr):
        t = pl.program_id(0)
        cur, nxt = lax.rem(t, 2), lax.rem(t + 1, 2)

        def gating(refs, slot, cn):
            sl = slice(cn * SGU_CHUNK, (cn + 1) * SGU_CHUNK)
            au_ref, av_ref, ag_ref = refs
            a_scr[slot, sl, :] = _sgu_chunk(au_ref[sl, :], av_ref[sl, :], ag_ref[sl, :], sg_ref[...], ws_ref[...],
                                            bs_ref[...]).astype(BF16)

        @pl.when(t == 0)
        def _():
            loss_ref[...] = jnp.zeros_like(loss_ref)
            dg_ref[...] = jnp.zeros_like(dg_ref)
            dw_ref[...] = jnp.zeros_like(dw_ref)
            for cn in range(SGU_PER_STEP):
                gating((au0_ref, av0_ref, ag0_ref), 0, cn)

        a, b = a_scr[cur], b_ref[...].astype(BF16)
        mix = (jnp.dot(a, w_ref[0:512, :], preferred_element_type=F32)
               + jnp.dot(b, w_ref[512:1024, :], preferred_element_type=F32))
        err = x_ref[...] + g_ref[...] * mix - t_ref[...]
        loss_ref[...] += 0.5 * jnp.sum(jnp.mean(err * err, axis=-1))
        dy = err * (1.0 / DM)
        dy_ref[...] = dy
        dg_ref[...] += jnp.sum(dy * mix, axis=0, keepdims=True)
        dmix = (g_ref[...] * dy).astype(BF16)

        def dcat_half(n):
            part = slice(512 * n, 512 * (n + 1))
            dcat_ref[:, part] = lax.dot_general(dmix, w_ref[part, :], _NT, preferred_element_type=F32)

        def dw_half(n, src):
            dw_ref[512 * n:512 * (n + 1), :] += lax.dot_general(src, dmix, (((0,), (0,)), ((), ())),
                                                                preferred_element_type=F32)

        _emit_interleaved([functools.partial(gating, (au1_ref, av1_ref, ag1_ref), nxt, cn) for cn in range(SGU_PER_STEP)],
                          [functools.partial(dcat_half, 0), functools.partial(dcat_half, 1),
                           functools.partial(dw_half, 0, a), functools.partial(dw_half, 1, b)])

    tile = lambda w: pl.BlockSpec((tl, w), lambda t: (t, 0))
    whole = pl.BlockSpec((DM, DM), lambda t: (0, 0))
    zfirst = [pl.BlockSpec((tl, 512), functools.partial(lambda c, t: (0, c), c)) for c in range(3)]
    znext = [pl.BlockSpec((tl, 512), functools.partial(lambda c, t: (jnp.minimum(t + 1, nt - 1), c), c))
             for c in range(3)]
    wspec = pl.BlockSpec((4, 128, 128), lambda t: (0, 0, 0))
    return pl.pallas_call(
        kern, name="outproj", grid=(nt,),
        in_specs=zfirst + znext + [_row(512), wspec, wspec, tile(512), tile(DM), tile(DM), _row(DM), whole],
        out_specs=[pl.BlockSpec((8, 128), lambda t: (0, 0)), tile(DM), tile(DM), _row(DM), whole],
        out_shape=[jax.ShapeDtypeStruct((8, 128), F32), jax.ShapeDtypeStruct((SEQ, DM), F32),
                   jax.ShapeDtypeStruct((SEQ, DM), F32), jax.ShapeDtypeStruct((1, DM), F32),
                   jax.ShapeDtypeStruct((DM, DM), F32)],
        scratch_shapes=[pltpu.VMEM((2, tl, 512), BF16)],
        compiler_params=_cparams(("arbitrary",), 48 * 1024 * 1024),
    )(z, z, z, z, z, z, sg, ws, bsb, out_b, x, target, gate, wo)


DZ_COLS = (("a", 0, 1536), ("q", 1536, 2048), ("k", 2048, 2560), ("v", 2560, 3072), ("g", 3072, DIN))
DZC_COLS = (("k", 2048, 2560), ("v", 2560, 3072))
_NT = (((1,), (1,)), ((), ()))


DH_SUBTILES = 2


def _dz_specs(tl):
    return [pl.BlockSpec((tl, 1536), lambda t: (t, 0))] + [pl.BlockSpec((tl, 512), lambda t: (t, 0))] * 4


def dh_bwd(dz_parts, w_full, x, dy, shift, scale, norm_g, dg_ctx):
    tl = 512
    nt = SEQ // tl

    def kern(a_ref, q_ref, k_ref, v_ref, g_ref, w_ref, x_ref, dy_ref, sh_ref, sc_ref, gn_ref, dgc_ref,
             gx_ref, dsh_ref, dsc_ref, dg_ref):
        @pl.when(pl.program_id(0) == 0)
        def _():
            dsh_ref[...] = jnp.zeros_like(dsh_ref)
            dsc_ref[...] = jnp.zeros_like(dsc_ref)
            dg_ref[...] = dgc_ref[...]

        src = dict(a=a_ref, q=q_ref, k=k_ref, v=v_ref, g=g_ref)
        for sub in range(DH_SUBTILES):
            rows = slice(sub * tl // DH_SUBTILES, (sub + 1) * tl // DH_SUBTILES)
            dh = None
            for name, c0, c1 in DZ_COLS:
                part = lax.dot_general(src[name][rows, :], w_ref[:, c0:c1], _NT, preferred_element_type=F32)
                dh = part if dh is None else dh + part
            _, vjp = jax.vjp(_modulated, x_ref[rows, :], gn_ref[...], sc_ref[...], sh_ref[...])
            dx, dg, dsc, dsh = vjp(dh)
            gx_ref[rows, :] = dy_ref[rows, :] + dx
            dg_ref[...] += dg
            dsc_ref[...] += dsc
            dsh_ref[...] += dsh

    tile = pl.BlockSpec((tl, DM), lambda t: (t, 0))
    return pl.pallas_call(
        kern, name="dh_bwd", grid=(nt,),
        in_specs=_dz_specs(tl) + [pl.BlockSpec((DM, DIN), lambda t: (0, 0)), tile, tile, _row(DM),
                                  _row(DM), _row(DM), _row(DM)],
        out_specs=[tile, _row(DM), _row(DM), _row(DM)],
        out_shape=[jax.ShapeDtypeStruct((SEQ, DM), F32)] + [jax.ShapeDtypeStruct((1, DM), F32)] * 3,
        compiler_params=_cparams(("arbitrary",), 48 * 1024 * 1024),
    )(*dz_parts, w_full, x, dy, shift, scale, norm_g, dg_ctx)


def dw_bwd(h, z, sg, ws, bsb, dcat, dz_attn, hc, dck, dcv, g_out):
    tl = SGU_CHUNK * SGU_PER_STEP
    nt = SEQ // tl
    (rhi, wi), (rho, wo) = RS_SHAPES

    def kern(h_ref, au_ref, av_ref, ag_ref, sg_ref, ws_ref, bs_ref, do_ref, q_ref, k_ref, v_ref, g_ref,
             hc_ref, dck_ref, dcv_ref, go_hbm,
             wire_i, keep_i, wire_o, keep_o, a_ref, dsg_ref, dws_ref, dbs_ref,
             acc, rcv_i, mine_o, rcv_o, load_sem, send_sems, recv_sems):
        t = pl.program_id(0)
        x, y, c = _me()
        k = 2 * x + y
        sib = _flip(1)
        half = lambda hh, rh: pl.ds(pl.multiple_of(hh * rh, rh), rh)
        load_o = pltpu.make_async_copy(go_hbm.at[:, half(c, rho), :], mine_o, load_sem)
        pair_o = _rcopy(go_hbm.at[:, half(1 - c, rho), :], rcv_o, send_sems, recv_sems, 0, sib)
        pair_i = [_rcopy(wire_i.at[j], rcv_i.at[j], send_sems, recv_sems, 1 + j, sib) for j in range(NCHIP)]

        @pl.when(t == 0)
        def _():
            load_o.start()
            pair_o.start()
            acc[...] = jnp.zeros_like(acc)
            dsg_ref[...] = jnp.zeros_like(dsg_ref)
            dws_ref[...] = jnp.zeros_like(dws_ref)
            dbs_ref[...] = jnp.zeros_like(dbs_ref)
            hct = hc_ref[...].T
            csrc = dict(k=dck_ref, v=dcv_ref)
            for name, c0, c1 in DZC_COLS:
                acc[:, c0:c1] += jnp.dot(hct, csrc[name][...].astype(BF16), preferred_element_type=F32)

        ht = h_ref[...].T
        src = dict(a=a_ref, q=q_ref, k=k_ref, v=v_ref, g=g_ref)

        def gating_backward(cn):
            sl = slice(cn * SGU_CHUNK, (cn + 1) * SGU_CHUNK)
            for g in range(4):
                lanes = slice(128 * g, 128 * (g + 1))
                _, vjp = jax.vjp(_sgu_group, au_ref[sl, lanes], av_ref[sl, lanes], ag_ref[sl, lanes], sg_ref[:, lanes],
                                 ws_ref[g], bs_ref[g])
                dau, dav, dag, dsg, dws, dbs = vjp(do_ref[sl, lanes])
                a_ref[sl, 128 * g:128 * (g + 1)] = dau.astype(BF16)
                a_ref[sl, 512 + 128 * g:512 + 128 * (g + 1)] = dav.astype(BF16)
                a_ref[sl, 1024 + 128 * g:1024 + 128 * (g + 1)] = dag.astype(BF16)
                dsg_ref[:, lanes] += dsg
                dws_ref[g] += dws
                dbs_ref[g] += dbs

        def product(name, c0, c1):
            acc[:, c0:c1] += jnp.dot(ht, src[name][...], preferred_element_type=F32)

        _emit_interleaved([functools.partial(product, *cols) for cols in DZ_COLS[1:]],
                          [functools.partial(gating_backward, cn) for cn in range(SGU_PER_STEP)])
        product(*DZ_COLS[0])

        @pl.when(t == nt - 1)
        def _():
            dbs_ref[...] = jnp.broadcast_to(jnp.sum(dbs_ref[...], axis=-1, keepdims=True), dbs_ref.shape)
            shard = lambda j: slice(j * SHARD_IN, (j + 1) * SHARD_IN)
            for j in range(NCHIP):
                wire_i[j] = acc[half(1 - c, rhi), shard(j)].astype(BF16)
                pair_i[j].start()
            load_o.wait()
            pair_o.wait_recv()
            for j in range(NCHIP):
                wire_o[j] = (mine_o[j] + rcv_o[j]).astype(BF16)
            keep_o[...] = mine_o[k] + rcv_o[k]
            mine = half(c, rhi)
            for j in range(NCHIP):
                pair_i[j].wait_recv()
                pair_i[j].wait_send()
                pair_sum = acc[mine, shard(j)] + rcv_i[j].astype(F32)
                wire_i[j] = pair_sum.astype(BF16)

                @pl.when(k == j)
                def _():
                    keep_i[...] = pair_sum
            pair_o.wait_send()

    whole = lambda *shape: pl.BlockSpec(shape, lambda t: (0,) * len(shape))
    rows, sgu_specs = _sgu_specs()
    assert rows == tl
    a_spec, *attn_specs = _dz_specs(tl)
    return pl.pallas_call(
        kern, name="dw_bwd", grid=(nt,),
        in_specs=[pl.BlockSpec((tl, DM), lambda t: (t, 0))] + sgu_specs + [pl.BlockSpec((tl, 512), lambda t: (t, 0))]
        + attn_specs + [whole(CTX, DM), whole(CTX, 512), whole(CTX, 512), pl.BlockSpec(memory_space=pl.ANY)],
        out_specs=[whole(NCHIP, rhi, wi), whole(rhi, wi), whole(NCHIP, rho, wo), whole(rho, wo),
                   a_spec, _row(512), whole(4, 128, 128), whole(4, 128, 128)],
        out_shape=[jax.ShapeDtypeStruct((NCHIP, rhi, wi), BF16), jax.ShapeDtypeStruct((rhi, wi), F32),
                   jax.ShapeDtypeStruct((NCHIP, rho, wo), BF16), jax.ShapeDtypeStruct((rho, wo), F32),
                   jax.ShapeDtypeStruct((SEQ, 1536), BF16), jax.ShapeDtypeStruct((1, 512), F32),
                   jax.ShapeDtypeStruct((4, 128, 128), F32), jax.ShapeDtypeStruct((4, 128, 128), F32)],
        scratch_shapes=[pltpu.VMEM((DM, DIN), F32), pltpu.VMEM((NCHIP, rhi, wi), BF16),
                        pltpu.VMEM((NCHIP, rho, wo), F32), pltpu.VMEM((NCHIP, rho, wo), F32),
                        pltpu.SemaphoreType.DMA(()), pltpu.SemaphoreType.DMA((1 + NCHIP,)),
                        pltpu.SemaphoreType.DMA((1 + NCHIP,))],
        compiler_params=_cparams(("arbitrary",), 60 * 1024 * 1024),
    )(h, z, z, z, sg, ws, bsb, dcat, *dz_attn, hc, dck, dcv, g_out)


def ctx_bwd(dck, dcv, w_full, ctx, cshift, cscale, norm_g):
    def kern(dck_ref, dcv_ref, w_ref, c_ref, sh_ref, sc_ref, g_ref, dsh_ref, dsc_ref, dg_ref):
        csrc = dict(k=dck_ref, v=dcv_ref)
        dhc = None
        first = DZC_COLS[0][1]
        for name, c0, c1 in DZC_COLS:
            part = lax.dot_general(csrc[name][...].astype(BF16), w_ref[:, c0 - first:c1 - first], _NT,
                                   preferred_element_type=F32)
            dhc = part if dhc is None else dhc + part
        _, vjp = jax.vjp(lambda g, sc, sh: _modulated(c_ref[...], g, sc, sh), g_ref[...], sc_ref[...], sh_ref[...])
        dg_ref[...], dsc_ref[...], dsh_ref[...] = vjp(dhc)

    whole = lambda r, c: pl.BlockSpec((r, c), lambda i: (0, 0))
    return pl.pallas_call(
        kern, name="ctx_bwd", grid=(1,),
        in_specs=[whole(CTX, 512), whole(CTX, 512), pl.BlockSpec((DM, 1024), lambda i: (0, DZC_COLS[0][1] // 1024)),
                  whole(CTX, DM), _row(DM), _row(DM), _row(DM)],
        out_specs=[_row(DM), _row(DM), _row(DM)],
        out_shape=[jax.ShapeDtypeStruct((1, DM), F32)] * 3,
        compiler_params=_cparams(("arbitrary",), 40 * 1024 * 1024),
    )(dck, dcv, w_full, ctx, cshift, cscale, norm_g)


def _lane_pad_rpb(rpb):
    r = jnp.pad(rpb, ((0, 0), (0, 0), (0, GRID_W - rpb.shape[-1])))
    return jnp.concatenate([r, r], axis=-1)


def local_step(chip, dev, x, c_vec, c_ctx, w_ada, b_shard, ctx, target, norm_g, sgu_g, w_s, b_s, q_g, k_g, rpb,
               w_in_shard, w_out_shard):
    bsb = jnp.broadcast_to(b_s[:, :, None], (4, 128, 128))
    qg2, kg2 = jnp.tile(q_g, (1, 2)), jnp.tile(k_g, (1, 2))

    z, h, w_in_full, w_out_full, mod_all, cs = inproj_fwd(chip, x, c_vec, c_ctx, w_ada, b_shard, norm_g, w_in_shard,
                                                          w_out_shard)
    mods = mod_all.transpose(1, 0, 2).reshape(CS_ROWS, 3 * DM)
    mod = lax.dynamic_slice(mods, (8 * dev, 0), (1, 3 * DM))
    shift, scale, gate = mod[:, :DM], mod[:, DM:2 * DM], mod[:, 2 * DM:]
    cshift, cscale = mods[8 * NDEV:8 * NDEV + 1, :DM], mods[8 * NDEV:8 * NDEV + 1, DM:2 * DM]
    zc, hc = ctx_fwd(ctx, cshift, cscale, norm_g, w_in_full)
    out_b, *saved = attn_fwd(z, zc, _lane_pad_rpb(rpb), qg2, kg2)
    loss8, dy, dcat, dgate, dwo = outproj(z, sgu_g, w_s, bsb, out_b, x, target, gate, w_out_full.reshape(DM, DM))
    dq, dk, dv, dbg, dck, dcv, drpb, dqg2, dkg2 = attn_bwd(z, zc, qg2, kg2, dcat, saved)
    drpb = drpb[:, :, :rpb.shape[-1]]
    dcshift, dcscale, dng_c = ctx_bwd(dck, dcv, w_in_full, ctx, cshift, cscale, norm_g)
    wire_i, keep_i, wire_o, keep_o, dz_a, dsg, dws, dbsb = dw_bwd(
        h, z, sgu_g, w_s, bsb, dcat, (dq, dk, dv, dbg), hc, dck, dcv, dwo.reshape(NCHIP, SHARD_OUT, DM))
    dz_parts = (dz_a, dq, dk, dv, dbg)
    *in_flight, token = rs_start(wire_i, wire_o)
    grad_x, dshift, dscale, dng = dh_bwd(dz_parts, w_in_full, x, dy, shift, scale, norm_g, dng_c + token[0, 0])
    got_i, got_o = rs_wait(*in_flight, dshift)
    return dict(
        loss=loss8[0:1, 0:1], grad_x=grad_x, rs=(keep_i, got_i, keep_o, got_o), cs=cs,
        dmod=jnp.concatenate([dshift, dscale, dgate], axis=-1),
        dcmod=jnp.concatenate([dcshift, dcscale, jnp.zeros((1, DM), F32)], axis=-1),
        d_norm_g=dng, d_sgu_g=dsg, d_w_s=dws, d_b_s=dbsb[:, :, 0],
        d_q_g=dqg2[:, :HDIM], d_k_g=dkg2[:, :HDIM], d_rpb=drpb)


def _me():
    return lax.axis_index("x"), lax.axis_index("y"), lax.axis_index("c")


def _flip(q):
    x, y, c = _me()
    return ((1 - x) if q & 4 else x, (1 - y) if q & 2 else y, (1 - c) if q & 1 else c)


def _chip_of(dev):
    return 2 * dev[0] + dev[1]


def _rcopy(src, dst, send_sems, recv_sems, k, dev):
    return pltpu.make_async_remote_copy(src_ref=src, dst_ref=dst, send_sem=send_sems.at[k], recv_sem=recv_sems.at[k],
                                        device_id=dev, device_id_type=MESH_ID)


_VMEM_SPEC = pl.BlockSpec(memory_space=pltpu.VMEM)
SLAB_ROWS = 80


RS_SHAPES = ((DM // 2, SHARD_IN), (SHARD_OUT // 2, DM))
_HBM_SPEC = pl.BlockSpec(memory_space=pltpu.HBM)
_SEM_SPEC = pl.BlockSpec(memory_space=pltpu.SEMAPHORE)
_IN_FLIGHT = pltpu.SideEffectType.DATAFLOW_SIDE_EFFECTING


def _rs_copies(wires, lands, send_sems, recv_sems):
    return [pltpu.make_async_remote_copy(
        src_ref=wires[n].at[_chip_of(_flip(q))], dst_ref=lands[n].at[q // 2 - 1],
        send_sem=send_sems.at[3 * n + q // 2 - 1], recv_sem=recv_sems.at[3 * n + q // 2 - 1],
        device_id=_flip(q), device_id_type=MESH_ID) for n in (0, 1) for q in (2, 4, 6)]


def rs_start(wire_i, wire_o):
    lands = [lax.empty((NCHIP - 1, rh, w), BF16) for rh, w in RS_SHAPES]

    def body(wi_ref, wo_ref, li_ref, lo_ref, send_sems, recv_sems, wi_thru, wo_thru, li_thru, lo_thru, token):
        for cp in _rs_copies((wi_ref, wo_ref), (li_ref, lo_ref), send_sems, recv_sems):
            cp.start()
        token[...] = jnp.zeros_like(token)

    hbm = lambda a: pltpu.HBM(a.shape, a.dtype)
    return pl.pallas_call(
        body, name="rs_start",
        out_shape=(pltpu.SemaphoreType.DMA((6,)), pltpu.SemaphoreType.DMA((6,)), hbm(wire_i), hbm(wire_o),
                   hbm(lands[0]), hbm(lands[1]), jax.ShapeDtypeStruct((8, 128), F32)),
        in_specs=(_HBM_SPEC,) * 4, out_specs=(_SEM_SPEC, _SEM_SPEC) + (_HBM_SPEC,) * 4 + (_VMEM_SPEC,),
        input_output_aliases={0: 2, 1: 3, 2: 4, 3: 5},
        compiler_params=pltpu.CompilerParams(has_side_effects=_IN_FLIGHT),
    )(*[pltpu.with_memory_space_constraint(a, pltpu.HBM) for a in (wire_i, wire_o, *lands)])


def rs_wait(send_sems, recv_sems, wire_i, wire_o, land_i, land_o, after):
    def body(wi_ref, wo_ref, li_ref, lo_ref, send_sems, recv_sems, after_ref, wi_dead, wo_dead, gi_ref, go_ref):
        for cp in _rs_copies((wi_ref, wo_ref), (li_ref, lo_ref), send_sems, recv_sems):
            cp.wait_send()
            cp.wait_recv()

    hbm = lambda a: pltpu.HBM(a.shape, a.dtype)
    return pl.pallas_call(
        body, name="rs_wait", out_shape=(hbm(wire_i), hbm(wire_o), hbm(land_i), hbm(land_o)),
        in_specs=(_HBM_SPEC,) * 4 + (_SEM_SPEC, _SEM_SPEC, pl.BlockSpec(memory_space=pl.ANY)),
        out_specs=(_HBM_SPEC,) * 4, input_output_aliases={0: 0, 1: 1, 2: 2, 3: 3},
        compiler_params=pltpu.CompilerParams(has_side_effects=_IN_FLIGHT),
    )(wire_i, wire_o, land_i, land_o, send_sems, recv_sems, after)[2:]


def final_reduce(keep_i, got_i, keep_o, got_o, slab, cs, w_ada, c_ctx):
    (rhi, wi), (rho, wo) = RS_SHAPES

    def kern(ki_hbm, gi_hbm, ko_hbm, go_hbm, s_ref, cs_ref, w_hbm, cc_ref,
             gin_ref, gout_ref, tot_ref, dw_ref, db_ref, dcc_ref,
             ki, gi, ko, go, w_scr, all_ref, dms_scr, parts, load_sems, send_sems, recv_sems):
        x, y, c = _me()
        k = 2 * x + y
        sib = _flip(1)
        dev = lambda d: 4 * d[0] + 2 * d[1] + d[2]
        me = dev((x, y, c))

        def slab_copy(idx, owner, to):
            return _rcopy(all_ref.at[dev(owner)], all_ref.at[dev(owner)], send_sems, recv_sems, idx, to)

        all_ref[me] = s_ref[...]
        first = [slab_copy(0, (x, y, c), sib)] + [slab_copy(q // 2, (x, y, c), _flip(q)) for q in (2, 4, 6)]
        for cp in first:
            cp.start()
        loads = [pltpu.make_async_copy(src, dst, load_sems.at[n]) for n, (src, dst) in enumerate(
            ((ki_hbm, ki), (gi_hbm, gi), (ko_hbm, ko), (go_hbm, go), (w_hbm, w_scr)))]
        for cp in loads:
            cp.start()

        shares = []
        for n, (keep, got, out) in enumerate(((ki, gi, gin_ref), (ko, go, gout_ref))):
            rh = RS_SHAPES[n][0]
            half = lambda hh, rh=rh: pl.ds(pl.multiple_of(hh * rh, rh), rh)
            loads[2 * n].wait()
            loads[2 * n + 1].wait()
            out[half(c), :] = ((keep[...] + got[0].astype(F32)) + got[1].astype(F32)) + got[2].astype(F32)
            share = _rcopy(out.at[half(c), :], out.at[half(c), :], send_sems, recv_sems, 7 + n, sib)
            share.start()
            shares.append((share, _rcopy(out.at[half(1 - c), :], out.at[half(1 - c), :], send_sems, recv_sems, 7 + n,
                                         sib)))

        passed = []
        for q in (2, 4, 6):
            slab_copy(q // 2, _flip(q), (x, y, c)).wait_recv()
            cp = slab_copy(3 + q // 2, _flip(q), sib)
            cp.start()
            passed.append(cp)
        slab_copy(0, sib, (x, y, c)).wait_recv()
        for q in (2, 4, 6):
            slab_copy(3 + q // 2, _flip(q | 1), (x, y, c)).wait_recv()
        tot = all_ref[0]
        for d in range(1, NDEV):
            tot = tot + all_ref[d]
        tot_ref[...] = tot

        pad = jnp.zeros((7, DM), F32)
        dm = [jnp.concatenate([all_ref[d, 12 + j:13 + j, :] for d in range(NDEV)] + [tot[9 + j:10 + j, :], pad], axis=0)
              for j in range(3)]
        db_ref[...] = jnp.concatenate([jnp.sum(part, axis=0, keepdims=True) for part in dm], axis=0)
        dm = jnp.concatenate(dm, axis=-1)
        for j in range(NCHIP):
            @pl.when(k == j)
            def _():
                dms_scr[...] = dm[:, j * SHARD_ADA:(j + 1) * SHARD_ADA].astype(BF16)

        a_in = jnp.concatenate([cs_ref[8 * d:8 * d + 1, :] for d in range(NDEV)]
                               + [cs_ref[8 * NDEV:8 * NDEV + 1, :], pad], axis=0)
        act = jax.nn.silu(a_in).astype(BF16)
        dms = dms_scr[...]
        dw_ref[...] = lax.dot_general(act, dms, (((0,), (0,)), ((), ())), preferred_element_type=F32)
        loads[4].wait()
        parts[k] = lax.dot_general(dms, w_scr[...].astype(BF16), (((1,), (1,)), ((), ())), preferred_element_type=F32)
        sends = [_rcopy(parts.at[k], parts.at[k], send_sems, recv_sems, 8 + q // 2, _flip(q)) for q in (2, 4, 6)]
        for cp in sends:
            cp.start()
        for q in (2, 4, 6):
            kq = _chip_of(_flip(q))
            _rcopy(parts.at[kq], parts.at[kq], send_sems, recv_sems, 8 + q // 2, _flip(q)).wait_recv()
        dact = ((parts[0] + parts[1]) + parts[2]) + parts[3]
        _, vjp = jax.vjp(jax.nn.silu, cc_ref[...])
        dcc_ref[...] = vjp(dact[8:9, :])[0]

        for share, arrival in shares:
            arrival.wait_recv()
            share.wait_send()
        for cp in first + passed + sends:
            cp.wait_send()

    any_spec = pl.BlockSpec(memory_space=pl.ANY)
    return pl.pallas_call(
        kern, name="final_reduce",
        in_specs=[any_spec] * 4 + [_VMEM_SPEC, _VMEM_SPEC, any_spec, _VMEM_SPEC], out_specs=[_VMEM_SPEC] * 6,
        out_shape=[jax.ShapeDtypeStruct((2 * rhi, wi), F32), jax.ShapeDtypeStruct((2 * rho, wo), F32),
                   jax.ShapeDtypeStruct((SLAB_ROWS, DM), F32), jax.ShapeDtypeStruct((DM, SHARD_ADA), F32),
                   jax.ShapeDtypeStruct((3, DM), F32), jax.ShapeDtypeStruct((1, DM), F32)],
        scratch_shapes=[pltpu.VMEM((rhi, wi), F32), pltpu.VMEM((NCHIP - 1, rhi, wi), BF16),
                        pltpu.VMEM((rho, wo), F32), pltpu.VMEM((NCHIP - 1, rho, wo), BF16),
                        pltpu.VMEM((DM, SHARD_ADA), F32), pltpu.VMEM((NDEV, SLAB_ROWS, DM), F32),
                        pltpu.VMEM((16, SHARD_ADA), BF16), pltpu.VMEM((NCHIP, 16, DM), F32),
                        pltpu.SemaphoreType.DMA((5,)), pltpu.SemaphoreType.DMA((12,)), pltpu.SemaphoreType.DMA((12,))],
        compiler_params=pltpu.CompilerParams(vmem_limit_bytes=40 * 1024 * 1024),
    )(keep_i, got_i, keep_o, got_o, slab, cs, w_ada, c_ctx)


def _adamw_math(w, g, m, v):
    m = B1 * m + (1.0 - B1) * g
    v = B2 * v + (1.0 - B2) * (g * g)
    m_hat = m / (1.0 - B1 ** STEP)
    v_hat = v / (1.0 - B2 ** STEP)
    return -LR * (m_hat / (jnp.sqrt(v_hat) + ADAM_EPS) + WD * w), m, v


def adamw_big(w, g, m, v, name, block_rows=256):
    rows, width = w.shape

    def kern(w_ref, g_ref, m_ref, v_ref, d_ref, nm_ref, nv_ref):
        d_ref[...], nm_ref[...], nv_ref[...] = _adamw_math(w_ref[...], g_ref[...], m_ref[...], v_ref[...])

    spec = pl.BlockSpec((block_rows, width), lambda i: (i, 0))
    return pl.pallas_call(
        kern, name=name, grid=(rows // block_rows,), in_specs=[spec] * 4, out_specs=[spec] * 3,
        out_shape=[jax.ShapeDtypeStruct((rows, width), F32)] * 3,
        compiler_params=_cparams(("arbitrary",)),
    )(w, g, m, v)


def adamw_small(quads):
    n = len(quads)

    def kern(*refs):
        ins, outs = refs[:4 * n], refs[4 * n:]
        for i in range(n):
            w, g, m, v = (r[...] for r in ins[4 * i:4 * i + 4])
            outs[3 * i][...], outs[3 * i + 1][...], outs[3 * i + 2][...] = _adamw_math(w, g, m, v)

    flat = [a for quad in quads for a in quad]
    res = pl.pallas_call(
        kern, name="adamw_small", in_specs=[_VMEM_SPEC] * (4 * n), out_specs=[_VMEM_SPEC] * (3 * n),
        out_shape=[jax.ShapeDtypeStruct(q[0].shape, F32) for q in quads for _ in range(3)],
    )(*flat)
    return [tuple(res[3 * i:3 * i + 3]) for i in range(n)]


def _rows_of(a, rows):
    flat = a.reshape(-1)
    return jnp.pad(flat, (0, rows * DM - flat.shape[0])).reshape(rows, DM)


def kernel(x, c, ctx, c_ctx, w_ada, b_ada, norm_g, w_in, sgu_norm_g, w_spatial, b_spatial, q_norm_g, k_norm_g, rpb, w_out, loss_target, m_c_ctx, m_w_ada, m_b_ada, m_norm_g, m_w_in, m_sgu_norm_g, m_w_spatial, m_b_spatial, m_q_norm_g, m_k_norm_g, m_rpb, m_w_out, v_c_ctx, v_w_ada, v_b_ada, v_norm_g, v_w_in, v_sgu_norm_g, v_w_spatial, v_b_spatial, v_q_norm_g, v_k_norm_g, v_rpb, v_w_out):
    xi, yi, ci = lax.axis_index("x"), lax.axis_index("y"), lax.axis_index("c")
    chip, dev = 2 * xi + yi, 4 * xi + 2 * yi + ci
    c_ctx2 = c_ctx.reshape(1, DM)

    b_shard = lax.dynamic_slice(b_ada, (0, chip * SHARD_ADA), (1, SHARD_ADA))
    part = local_step(chip.reshape(1).astype(jnp.int32), dev, x[0], c, c_ctx2, w_ada[0], b_shard, ctx[0], loss_target[0],
                      norm_g, sgu_norm_g, w_spatial[0], b_spatial[0], q_norm_g, k_norm_g, rpb[0], w_in[0], w_out[0])
    cs = part["cs"]

    slab = jnp.concatenate([
        part["d_norm_g"], _rows_of(part["d_sgu_g"], 1), _rows_of(part["d_b_s"], 1),
        _rows_of(jnp.concatenate([part["d_q_g"], part["d_k_g"]], axis=-1), 1), _rows_of(part["d_rpb"], 4),
        _rows_of(part["loss"], 1), _rows_of(part["dcmod"], 3), _rows_of(part["dmod"], 3), jnp.zeros((1, DM), F32),
        _rows_of(part["d_w_s"], 64)], axis=0)
    g_w_in, g_w_out, tot, g_w_ada, g_b_ada, g_c_ctx = final_reduce(*part["rs"], slab, cs, w_ada[0], c_ctx2)
    g_b_ada = g_b_ada.reshape(1, 3 * DM)

    loss = tot[8, 0]
    g_small = dict(
        c_ctx=g_c_ctx, b_ada=g_b_ada, norm_g=tot[0:1], sgu_norm_g=tot[1:2, :512], w_spatial=tot[16:80].reshape(512, 128),
        b_spatial=tot[2:3, :512].reshape(4, 128), q_norm_g=tot[3:4, :HDIM], k_norm_g=tot[3:4, HDIM:2 * HDIM],
        rpb=tot[4:8].reshape(-1)[:HEADS * 15 * 31].reshape(HEADS * 15, 31))
    shapes = dict(c_ctx=(DM,), w_ada=(1, DM, SHARD_ADA), b_ada=(1, 3 * DM), norm_g=(1, DM), w_in=(1, DM, SHARD_IN),
                  sgu_norm_g=(1, 512), w_spatial=(1, 4, 128, 128), b_spatial=(1, 4, 128), q_norm_g=(1, HDIM),
                  k_norm_g=(1, HDIM), rpb=(1, HEADS, 15, 31), w_out=(1, SHARD_OUT, DM))
    names = list(shapes)
    weights = dict(c_ctx=c_ctx, w_ada=w_ada, b_ada=b_ada, norm_g=norm_g, w_in=w_in, sgu_norm_g=sgu_norm_g,
                   w_spatial=w_spatial, b_spatial=b_spatial, q_norm_g=q_norm_g, k_norm_g=k_norm_g, rpb=rpb, w_out=w_out)
    m_in = dict(zip(names, (m_c_ctx, m_w_ada, m_b_ada, m_norm_g, m_w_in, m_sgu_norm_g, m_w_spatial, m_b_spatial,
                            m_q_norm_g, m_k_norm_g, m_rpb, m_w_out)))
    v_in = dict(zip(names, (v_c_ctx, v_w_ada, v_b_ada, v_norm_g, v_w_in, v_sgu_norm_g, v_w_spatial, v_b_spatial,
                            v_q_norm_g, v_k_norm_g, v_rpb, v_w_out)))
    grads = dict(g_small, w_ada=g_w_ada, w_in=g_w_in, w_out=g_w_out)
    upd = {}
    for n in ("w_ada", "w_in", "w_out"):
        g = grads[n]
        upd[n] = adamw_big(weights[n].reshape(g.shape), g, m_in[n].reshape(g.shape), v_in[n].reshape(g.shape),
                           "adamw_" + n)
    small = [n for n in names if n not in upd]
    res = adamw_small([(weights[n].reshape(grads[n].shape), grads[n], m_in[n].reshape(grads[n].shape),
                        v_in[n].reshape(grads[n].shape)) for n in small])
    upd.update(zip(small, res))
    out = [loss, part["grad_x"].reshape(1, SEQ, DM)]
    out += [grads[n].reshape(shapes[n]) for n in names]
    for slot in range(3):
        out += [upd[n][slot].reshape(shapes[n]) for n in names]
    return tuple(out)
```
